```python
import jax, jax.numpy as jnp
from jax import lax
import numpy as np

D_MODEL = 1024
BATCH = 8
SEQ = 4096
DEPTH = 2

CHUNK = 64
N_META = 16
D_FF = 4 * D_MODEL
EPS = 1e-5

CONV_DIM = D_MODEL // 2
CONV_WIDTH = 31
POOL_DIM = D_MODEL // 2
POOL_WINDOWS = (2, 4, 8, 16)
POOL_GROUP = POOL_DIM // len(POOL_WINDOWS)
EVEN_IN = 2 * CONV_DIM + POOL_DIM
EVEN_MIX = CONV_DIM + POOL_DIM

GLA_HEADS = 4
GLA_DK = D_MODEL // 2
GLA_DV = D_MODEL
GLA_HK = GLA_DK // GLA_HEADS
GLA_HV = GLA_DV // GLA_HEADS
GLA_GATE_RANK = 16
GLA_GATE_NORM = 16.0
ODD_IN = 2 * GLA_DK + 2 * GLA_DV + GLA_GATE_RANK

N_EVEN = (DEPTH + 1) // 2
N_ODD = DEPTH // 2

kernel_name = "hybrid_conv_pool_gla_trunk"


def rms_norm(x, g):
    xf = x.astype(jnp.float32)
    y = xf * lax.rsqrt(jnp.mean(xf * xf, axis=-1, keepdims=True) + EPS)
    return (y * g.astype(jnp.float32)).astype(x.dtype)


def layer_norm(x, g, b):
    xf = x.astype(jnp.float32)
    mu = jnp.mean(xf, axis=-1, keepdims=True)
    xc = xf - mu
    y = xc * lax.rsqrt(jnp.mean(xc * xc, axis=-1, keepdims=True) + EPS)
    return (y * g.astype(jnp.float32) + b.astype(jnp.float32)).astype(x.dtype)


def causal_depthwise_conv(u, w, b):
    c = u.shape[-1]
    lhs = jnp.pad(u, ((0, 0), (CONV_WIDTH - 1, 0), (0, 0)))
    out = lax.conv_general_dilated(
        lhs, w[:, None, :].astype(u.dtype), window_strides=(1,), padding="VALID",
        dimension_numbers=("NWC", "WIO", "NWC"), feature_group_count=c)
    return out + b.astype(u.dtype)


def trailing_mean_minus_self(u, window):
    L = u.shape[1]
    uf = u.astype(jnp.float32)
    csum = jnp.cumsum(uf, axis=1)
    cpad = jnp.pad(csum, ((0, 0), (1, 0), (0, 0)))
    lower = jnp.pad(cpad[:, :L - window + 1], ((0, 0), (window - 1, 0), (0, 0)))
    count = jnp.minimum(jnp.arange(L) + 1, window).astype(jnp.float32)[None, :, None]
    return ((csum - lower) / count - uf).astype(u.dtype)


def conv_pool_mixer(h, w_in, conv_w, conv_b, ln_g, ln_b, pool_w, pool_scale, w_out):
    bsz, L, _ = h.shape
    z = h @ w_in
    a_in, p_in = z[..., :2 * CONV_DIM], z[..., 2 * CONV_DIM:]
    a = a_in[..., :CONV_DIM] * jax.nn.sigmoid(a_in[..., CONV_DIM:])
    a = causal_depthwise_conv(a, conv_w, conv_b)
    a = jax.nn.silu(layer_norm(a, ln_g, ln_b))
    groups = [trailing_mean_minus_self(p_in[..., i * POOL_GROUP:(i + 1) * POOL_GROUP], w)
              for i, w in enumerate(POOL_WINDOWS)]
    p = jnp.stack(groups, axis=2)
    p = jnp.einsum("blgc,gcd->blgd", p, pool_w).reshape(bsz, L, POOL_DIM) * pool_scale
    return jnp.concatenate([a, p], axis=-1) @ w_out


def gla_mixer(h, w_in, gate_w2, gate_b, head_g, w_out):
    bsz, L, _ = h.shape
    z = h @ w_in
    split_at = np.cumsum([GLA_DK, GLA_DK, GLA_DV, GLA_DV]).tolist()
    q, k, v, g, r = jnp.split(z, split_at, axis=-1)
    log_a = jax.nn.log_sigmoid((r @ gate_w2 + gate_b).astype(jnp.float32)) / GLA_GATE_NORM
    pad = (-L) % CHUNK
    n_chunks = (L + pad) // CHUNK

    def to_chunks(t, hd):
        t = jnp.pad(t.astype(jnp.float32), ((0, 0), (pad, 0), (0, 0)))
        return t.reshape(bsz, n_chunks, CHUNK, GLA_HEADS, hd).transpose(1, 0, 3, 2, 4)

    qc = to_chunks(q * (GLA_HK ** -0.5), GLA_HK)
    kc = to_chunks(k, GLA_HK)
    vc = to_chunks(v, GLA_HV)
    lac = to_chunks(log_a, GLA_HK)
    cum = jnp.cumsum(lac, axis=3)
    total = cum[:, :, :, -1]
    k_dec = kc * jnp.exp(total[:, :, :, None, :] - cum)

    def step(state, inp):
        q_i, k_i, v_i, tot_i = inp
        state = jnp.exp(tot_i)[..., None] * state + jnp.einsum("bhck,bhcv->bhkv", k_i, v_i)
        o_i = jnp.einsum("bhck,bhkv->bhcv", q_i, state)
        return state, o_i

    s0 = jnp.zeros((bsz, GLA_HEADS, GLA_HK, GLA_HV), jnp.float32)
    _, o = lax.scan(step, s0, (qc, k_dec, vc, total))
    o = o.transpose(1, 0, 3, 2, 4).reshape(bsz, n_chunks * CHUNK, GLA_HEADS, GLA_HV)[:, pad:]
    o = rms_norm(o, head_g).reshape(bsz, L, GLA_DV)
    o = o * jax.nn.silu(g.astype(jnp.float32))
    return o.astype(h.dtype) @ w_out


def squared_relu_mlp(u, w1, w2):
    a = jax.nn.relu(u @ w1)
    return (a * a) @ w2


def _fwd_setup_inputs(seed: int = 0) -> dict:
    key = jax.random.key(seed)
    ks = jax.random.split(key, 21)
    f32 = jnp.float32
    nrm = lambda k, shape, scale: jax.random.normal(k, shape, f32) * scale
    return {
        "x": nrm(ks[0], (BATCH, SEQ, D_MODEL), 1.0),
        "meta_tokens": nrm(ks[1], (N_META, D_MODEL), 1.0),
        "mix_norm_g": 1.0 + nrm(ks[2], (DEPTH, D_MODEL), 0.02),
        "ffn_norm_g": 1.0 + nrm(ks[3], (DEPTH, D_MODEL), 0.02),
        "ffn_w1": nrm(ks[4], (DEPTH, D_MODEL, D_FF), D_MODEL ** -0.5),
        "ffn_w2": nrm(ks[5], (DEPTH, D_FF, D_MODEL), D_FF ** -0.5),
        "cp_w_in": nrm(ks[6], (N_EVEN, D_MODEL, EVEN_IN), D_MODEL ** -0.5),
        "cp_conv_w": nrm(ks[7], (N_EVEN, CONV_WIDTH, CONV_DIM), CONV_WIDTH ** -0.5),
        "cp_conv_b": nrm(ks[8], (N_EVEN, CONV_DIM), 0.02),
        "cp_ln_g": 1.0 + nrm(ks[9], (N_EVEN, CONV_DIM), 0.02),
        "cp_ln_b": nrm(ks[10], (N_EVEN, CONV_DIM), 0.02),
        "cp_pool_w": nrm(ks[11], (N_EVEN, len(POOL_WINDOWS), POOL_GROUP, POOL_GROUP), POOL_GROUP ** -0.5),
        "cp_pool_scale": 1.0 + nrm(ks[12], (N_EVEN, POOL_DIM), 0.02),
        "cp_w_out": nrm(ks[13], (N_EVEN, EVEN_MIX, D_MODEL), EVEN_MIX ** -0.5),
        "gla_w_in": nrm(ks[14], (N_ODD, D_MODEL, ODD_IN), D_MODEL ** -0.5),
        "gla_gate_w2": nrm(ks[15], (N_ODD, GLA_GATE_RANK, GLA_DK), GLA_GATE_RANK ** -0.5),
        "gla_gate_b": nrm(ks[16], (N_ODD, GLA_DK), 0.02),
        "gla_head_g": 1.0 + nrm(ks[17], (N_ODD, GLA_HV), 0.02),
        "gla_w_out": nrm(ks[18], (N_ODD, GLA_DV, D_MODEL), GLA_DV ** -0.5),
        "final_norm_g": 1.0 + nrm(ks[19], (D_MODEL,), 0.02),
    }


def _fwd_reference(x, meta_tokens, mix_norm_g, ffn_norm_g, ffn_w1, ffn_w2, cp_w_in, cp_conv_w,
              cp_conv_b, cp_ln_g, cp_ln_b, cp_pool_w, cp_pool_scale, cp_w_out, gla_w_in,
              gla_gate_w2, gla_gate_b, gla_head_g, gla_w_out, final_norm_g):
    bsz = x.shape[0]
    meta = jnp.broadcast_to(meta_tokens[None].astype(x.dtype), (bsz, N_META, D_MODEL))
    h = jnp.concatenate([meta, x], axis=1)
    for i in range(DEPTH):
        j = i // 2
        u = rms_norm(h, mix_norm_g[i])
        if i % 2 == 0:
            h = h + conv_pool_mixer(u, cp_w_in[j], cp_conv_w[j], cp_conv_b[j], cp_ln_g[j],
                                    cp_ln_b[j], cp_pool_w[j], cp_pool_scale[j], cp_w_out[j])
        else:
            h = h + gla_mixer(u, gla_w_in[j], gla_gate_w2[j], gla_gate_b[j], gla_head_g[j],
                              gla_w_out[j])
        u = rms_norm(h, ffn_norm_g[i])
        h = h + squared_relu_mlp(u, ffn_w1[i], ffn_w2[i])
    return rms_norm(h[:, N_META:], final_norm_g)


import jax as _jax
import jax.numpy as _jnp

TWIN_FORMAT = 'train_step'
FWD_PARAMS = ['x', 'meta_tokens', 'mix_norm_g', 'ffn_norm_g', 'ffn_w1', 'ffn_w2', 'cp_w_in', 'cp_conv_w', 'cp_conv_b', 'cp_ln_g', 'cp_ln_b', 'cp_pool_w', 'cp_pool_scale', 'cp_w_out', 'gla_w_in', 'gla_gate_w2', 'gla_gate_b', 'gla_head_g', 'gla_w_out', 'final_norm_g']
TWIN_WEIGHTS = ['meta_tokens', 'mix_norm_g', 'ffn_norm_g', 'ffn_w1', 'ffn_w2', 'cp_w_in', 'cp_conv_w', 'cp_conv_b', 'cp_ln_g', 'cp_ln_b', 'cp_pool_w', 'cp_pool_scale', 'cp_w_out', 'gla_w_in', 'gla_gate_w2', 'gla_gate_b', 'gla_head_g', 'gla_w_out', 'final_norm_g']
TWIN_DIFF_INPUT = 'x'
TWIN_INPUTS = ['x', 'meta_tokens', 'mix_norm_g', 'ffn_norm_g', 'ffn_w1', 'ffn_w2', 'cp_w_in', 'cp_conv_w', 'cp_conv_b', 'cp_ln_g', 'cp_ln_b', 'cp_pool_w', 'cp_pool_scale', 'cp_w_out', 'gla_w_in', 'gla_gate_w2', 'gla_gate_b', 'gla_head_g', 'gla_w_out', 'final_norm_g', 'loss_target', 'm_meta_tokens', 'm_mix_norm_g', 'm_ffn_norm_g', 'm_ffn_w1', 'm_ffn_w2', 'm_cp_w_in', 'm_cp_conv_w', 'm_cp_conv_b', 'm_cp_ln_g', 'm_cp_ln_b', 'm_cp_pool_w', 'm_cp_pool_scale', 'm_cp_w_out', 'm_gla_w_in', 'm_gla_gate_w2', 'm_gla_gate_b', 'm_gla_head_g', 'm_gla_w_out', 'm_final_norm_g', 'v_meta_tokens', 'v_mix_norm_g', 'v_ffn_norm_g', 'v_ffn_w1', 'v_ffn_w2', 'v_cp_w_in', 'v_cp_conv_w', 'v_cp_conv_b', 'v_cp_ln_g', 'v_cp_ln_b', 'v_cp_pool_w', 'v_cp_pool_scale', 'v_cp_w_out', 'v_gla_w_in', 'v_gla_gate_w2', 'v_gla_gate_b', 'v_gla_head_g', 'v_gla_w_out', 'v_final_norm_g']
TWIN_OUTPUTS = ['loss', 'grad_x', 'grad_meta_tokens', 'grad_mix_norm_g', 'grad_ffn_norm_g', 'grad_ffn_w1', 'grad_ffn_w2', 'grad_cp_w_in', 'grad_cp_conv_w', 'grad_cp_conv_b', 'grad_cp_ln_g', 'grad_cp_ln_b', 'grad_cp_pool_w', 'grad_cp_pool_scale', 'grad_cp_w_out', 'grad_gla_w_in', 'grad_gla_gate_w2', 'grad_gla_gate_b', 'grad_gla_head_g', 'grad_gla_w_out', 'grad_final_norm_g', 'delta_meta_tokens', 'delta_mix_norm_g', 'delta_ffn_norm_g', 'delta_ffn_w1', 'delta_ffn_w2', 'delta_cp_w_in', 'delta_cp_conv_w', 'delta_cp_conv_b', 'delta_cp_ln_g', 'delta_cp_ln_b', 'delta_cp_pool_w', 'delta_cp_pool_scale', 'delta_cp_w_out', 'delta_gla_w_in', 'delta_gla_gate_w2', 'delta_gla_gate_b', 'delta_gla_head_g', 'delta_gla_w_out', 'delta_final_norm_g', 'new_m_meta_tokens', 'new_m_mix_norm_g', 'new_m_ffn_norm_g', 'new_m_ffn_w1', 'new_m_ffn_w2', 'new_m_cp_w_in', 'new_m_cp_conv_w', 'new_m_cp_conv_b', 'new_m_cp_ln_g', 'new_m_cp_ln_b', 'new_m_cp_pool_w', 'new_m_cp_pool_scale', 'new_m_cp_w_out', 'new_m_gla_w_in', 'new_m_gla_gate_w2', 'new_m_gla_gate_b', 'new_m_gla_head_g', 'new_m_gla_w_out', 'new_m_final_norm_g', 'new_v_meta_tokens', 'new_v_mix_norm_g', 'new_v_ffn_norm_g', 'new_v_ffn_w1', 'new_v_ffn_w2', 'new_v_cp_w_in', 'new_v_cp_conv_w', 'new_v_cp_conv_b', 'new_v_cp_ln_g', 'new_v_cp_ln_b', 'new_v_cp_pool_w', 'new_v_cp_pool_scale', 'new_v_cp_w_out', 'new_v_gla_w_in', 'new_v_gla_gate_w2', 'new_v_gla_gate_b', 'new_v_gla_head_g', 'new_v_gla_w_out', 'new_v_final_norm_g']
TWIN_LEAF_KINDS = {'loss': 'loss', 'grad_x': 'grad_x', 'grad_meta_tokens': 'grad_w', 'grad_mix_norm_g': 'grad_w', 'grad_ffn_norm_g': 'grad_w', 'grad_ffn_w1': 'grad_w', 'grad_ffn_w2': 'grad_w', 'grad_cp_w_in': 'grad_w', 'grad_cp_conv_w': 'grad_w', 'grad_cp_conv_b': 'grad_w', 'grad_cp_ln_g': 'grad_w', 'grad_cp_ln_b': 'grad_w', 'grad_cp_pool_w': 'grad_w', 'grad_cp_pool_scale': 'grad_w', 'grad_cp_w_out': 'grad_w', 'grad_gla_w_in': 'grad_w', 'grad_gla_gate_w2': 'grad_w', 'grad_gla_gate_b': 'grad_w', 'grad_gla_head_g': 'grad_w', 'grad_gla_w_out': 'grad_w', 'grad_final_norm_g': 'grad_w', 'delta_meta_tokens': 'delta_w', 'delta_mix_norm_g': 'delta_w', 'delta_ffn_norm_g': 'delta_w', 'delta_ffn_w1': 'delta_w', 'delta_ffn_w2': 'delta_w', 'delta_cp_w_in': 'delta_w', 'delta_cp_conv_w': 'delta_w', 'delta_cp_conv_b': 'delta_w', 'delta_cp_ln_g': 'delta_w', 'delta_cp_ln_b': 'delta_w', 'delta_cp_pool_w': 'delta_w', 'delta_cp_pool_scale': 'delta_w', 'delta_cp_w_out': 'delta_w', 'delta_gla_w_in': 'delta_w', 'delta_gla_gate_w2': 'delta_w', 'delta_gla_gate_b': 'delta_w', 'delta_gla_head_g': 'delta_w', 'delta_gla_w_out': 'delta_w', 'delta_final_norm_g': 'delta_w', 'new_m_meta_tokens': 'new_m', 'new_m_mix_norm_g': 'new_m', 'new_m_ffn_norm_g': 'new_m', 'new_m_ffn_w1': 'new_m', 'new_m_ffn_w2': 'new_m', 'new_m_cp_w_in': 'new_m', 'new_m_cp_conv_w': 'new_m', 'new_m_cp_conv_b': 'new_m', 'new_m_cp_ln_g': 'new_m', 'new_m_cp_ln_b': 'new_m', 'new_m_cp_pool_w': 'new_m', 'new_m_cp_pool_scale': 'new_m', 'new_m_cp_w_out': 'new_m', 'new_m_gla_w_in': 'new_m', 'new_m_gla_gate_w2': 'new_m', 'new_m_gla_gate_b': 'new_m', 'new_m_gla_head_g': 'new_m', 'new_m_gla_w_out': 'new_m', 'new_m_final_norm_g': 'new_m', 'new_v_meta_tokens': 'new_v', 'new_v_mix_norm_g': 'new_v', 'new_v_ffn_norm_g': 'new_v', 'new_v_ffn_w1': 'new_v', 'new_v_ffn_w2': 'new_v', 'new_v_cp_w_in': 'new_v', 'new_v_cp_conv_w': 'new_v', 'new_v_cp_conv_b': 'new_v', 'new_v_cp_ln_g': 'new_v', 'new_v_cp_ln_b': 'new_v', 'new_v_cp_pool_w': 'new_v', 'new_v_cp_pool_scale': 'new_v', 'new_v_cp_w_out': 'new_v', 'new_v_gla_w_in': 'new_v', 'new_v_gla_gate_w2': 'new_v', 'new_v_gla_gate_b': 'new_v', 'new_v_gla_head_g': 'new_v', 'new_v_gla_w_out': 'new_v', 'new_v_final_norm_g': 'new_v'}


def _forward(args):
    return _fwd_reference(*[args[k] for k in FWD_PARAMS])


def _output_shape():
    def fwd():
        inp = _fwd_setup_inputs(0)
        return _fwd_reference(*[inp[k] for k in FWD_PARAMS])
    out = _jax.eval_shape(fwd)
    return out.shape, out.dtype

N_MICROBATCH = 1
ADAM_LR = 0.001
ADAM_B1 = 0.9
ADAM_B2 = 0.999
ADAM_EPS = 1e-08
ADAM_WD = 0.01
ADAM_STEP = 10
PER_EXAMPLE_BATCH_AXIS = {'x': 0, 'loss_target': 0}
SHARED_INPUTS = []
_WEIGHT_DTYPES = {'meta_tokens': _jnp.float32, 'mix_norm_g': _jnp.float32, 'ffn_norm_g': _jnp.float32, 'ffn_w1': _jnp.float32, 'ffn_w2': _jnp.float32, 'cp_w_in': _jnp.float32, 'cp_conv_w': _jnp.float32, 'cp_conv_b': _jnp.float32, 'cp_ln_g': _jnp.float32, 'cp_ln_b': _jnp.float32, 'cp_pool_w': _jnp.float32, 'cp_pool_scale': _jnp.float32, 'cp_w_out': _jnp.float32, 'gla_w_in': _jnp.float32, 'gla_gate_w2': _jnp.float32, 'gla_gate_b': _jnp.float32, 'gla_head_g': _jnp.float32, 'gla_w_out': _jnp.float32, 'final_norm_g': _jnp.float32}
MOMENT_SCALE = {'meta_tokens': 4.889398e-03, 'mix_norm_g': 1.404423e-01, 'ffn_norm_g': 1.460706e-01, 'ffn_w1': 7.398822e-02, 'ffn_w2': 1.467015e-01, 'cp_w_in': 1.174493e-01, 'cp_conv_w': 1.158082e-01, 'cp_conv_b': 2.615543e-01, 'cp_ln_g': 1.354739e-01, 'cp_ln_b': 1.395996e-01, 'cp_pool_w': 1.625387e-01, 'cp_pool_scale': 1.631637e-01, 'cp_w_out': 1.409871e-01, 'gla_w_in': 7.565513e-02, 'gla_gate_w2': 1.897686e-02, 'gla_gate_b': 5.600200e-02, 'gla_head_g': 1.334626e-01, 'gla_w_out': 6.137399e-02, 'final_norm_g': 3.240250e+01}


def _to_microbatches(a, axis):
    t = _jnp.moveaxis(a, axis, 0)
    t = t.reshape((N_MICROBATCH, t.shape[0] // N_MICROBATCH) + t.shape[1:])
    return _jnp.moveaxis(t, 1, axis + 1)


def setup_inputs(seed: int = 0) -> dict:
    inp = _fwd_setup_inputs(seed)
    key = _jax.random.fold_in(_jax.random.key(seed), 7919)
    shape, _ = _output_shape()
    out = dict(inp)
    out["loss_target"] = _jax.random.normal(_jax.random.fold_in(key, 0), shape, _jnp.float32)
    for i, name in enumerate(TWIN_WEIGHTS):
        w = inp[name].astype(_jnp.float32)
        if MOMENT_SCALE is None:
            s = _jnp.sqrt(_jnp.mean(_jnp.square(w)) + 1e-30)
        else:
            s = MOMENT_SCALE[name]
        km, kv = _jax.random.split(_jax.random.fold_in(key, i + 1))
        out[name] = w
        out["m_" + name] = s * _jax.random.normal(km, w.shape, _jnp.float32)
        out["v_" + name] = (s * s) * _jax.random.uniform(kv, w.shape, _jnp.float32, 0.5, 1.5)
    if N_MICROBATCH > 1:
        for name, axis in PER_EXAMPLE_BATCH_AXIS.items():
            out[name] = _to_microbatches(out[name], axis)
    return {'x': out['x'], 'meta_tokens': out['meta_tokens'], 'mix_norm_g': out['mix_norm_g'], 'ffn_norm_g': out['ffn_norm_g'], 'ffn_w1': out['ffn_w1'], 'ffn_w2': out['ffn_w2'], 'cp_w_in': out['cp_w_in'], 'cp_conv_w': out['cp_conv_w'], 'cp_conv_b': out['cp_conv_b'], 'cp_ln_g': out['cp_ln_g'], 'cp_ln_b': out['cp_ln_b'], 'cp_pool_w': out['cp_pool_w'], 'cp_pool_scale': out['cp_pool_scale'], 'cp_w_out': out['cp_w_out'], 'gla_w_in': out['gla_w_in'], 'gla_gate_w2': out['gla_gate_w2'], 'gla_gate_b': out['gla_gate_b'], 'gla_head_g': out['gla_head_g'], 'gla_w_out': out['gla_w_out'], 'final_norm_g': out['final_norm_g'], 'loss_target': out['loss_target'], 'm_meta_tokens': out['m_meta_tokens'], 'm_mix_norm_g': out['m_mix_norm_g'], 'm_ffn_norm_g': out['m_ffn_norm_g'], 'm_ffn_w1': out['m_ffn_w1'], 'm_ffn_w2': out['m_ffn_w2'], 'm_cp_w_in': out['m_cp_w_in'], 'm_cp_conv_w': out['m_cp_conv_w'], 'm_cp_conv_b': out['m_cp_conv_b'], 'm_cp_ln_g': out['m_cp_ln_g'], 'm_cp_ln_b': out['m_cp_ln_b'], 'm_cp_pool_w': out['m_cp_pool_w'], 'm_cp_pool_scale': out['m_cp_pool_scale'], 'm_cp_w_out': out['m_cp_w_out'], 'm_gla_w_in': out['m_gla_w_in'], 'm_gla_gate_w2': out['m_gla_gate_w2'], 'm_gla_gate_b': out['m_gla_gate_b'], 'm_gla_head_g': out['m_gla_head_g'], 'm_gla_w_out': out['m_gla_w_out'], 'm_final_norm_g': out['m_final_norm_g'], 'v_meta_tokens': out['v_meta_tokens'], 'v_mix_norm_g': out['v_mix_norm_g'], 'v_ffn_norm_g': out['v_ffn_norm_g'], 'v_ffn_w1': out['v_ffn_w1'], 'v_ffn_w2': out['v_ffn_w2'], 'v_cp_w_in': out['v_cp_w_in'], 'v_cp_conv_w': out['v_cp_conv_w'], 'v_cp_conv_b': out['v_cp_conv_b'], 'v_cp_ln_g': out['v_cp_ln_g'], 'v_cp_ln_b': out['v_cp_ln_b'], 'v_cp_pool_w': out['v_cp_pool_w'], 'v_cp_pool_scale': out['v_cp_pool_scale'], 'v_cp_w_out': out['v_cp_w_out'], 'v_gla_w_in': out['v_gla_w_in'], 'v_gla_gate_w2': out['v_gla_gate_w2'], 'v_gla_gate_b': out['v_gla_gate_b'], 'v_gla_head_g': out['v_gla_head_g'], 'v_gla_w_out': out['v_gla_w_out'], 'v_final_norm_g': out['v_final_norm_g']}


def _loss(weights, diff, rest, loss_target):
    with _jax.named_scope("forward"):
        args = {**rest, TWIN_DIFF_INPUT: diff, **{k: w.astype(_WEIGHT_DTYPES[k]) for k, w in weights.items()}}
        y = _forward(args)
    with _jax.named_scope("loss_head"):
        err = _jnp.square(y.astype(_jnp.float32) - loss_target)
        return 0.5 * _jnp.sum(_jnp.mean(err, axis=-1)) if err.ndim else 0.5 * err


def _adamw(w, g, m, v):
    m = ADAM_B1 * m + (1.0 - ADAM_B1) * g
    v = ADAM_B2 * v + (1.0 - ADAM_B2) * _jnp.square(g)
    m_hat = m / (1.0 - ADAM_B1 ** ADAM_STEP)
    v_hat = v / (1.0 - ADAM_B2 ** ADAM_STEP)
    delta = -ADAM_LR * (m_hat / (_jnp.sqrt(v_hat) + ADAM_EPS) + ADAM_WD * w)
    return delta, m, v


def reference(x, meta_tokens, mix_norm_g, ffn_norm_g, ffn_w1, ffn_w2, cp_w_in, cp_conv_w, cp_conv_b, cp_ln_g, cp_ln_b, cp_pool_w, cp_pool_scale, cp_w_out, gla_w_in, gla_gate_w2, gla_gate_b, gla_head_g, gla_w_out, final_norm_g, loss_target, m_meta_tokens, m_mix_norm_g, m_ffn_norm_g, m_ffn_w1, m_ffn_w2, m_cp_w_in, m_cp_conv_w, m_cp_conv_b, m_cp_ln_g, m_cp_ln_b, m_cp_pool_w, m_cp_pool_scale, m_cp_w_out, m_gla_w_in, m_gla_gate_w2, m_gla_gate_b, m_gla_head_g, m_gla_w_out, m_final_norm_g, v_meta_tokens, v_mix_norm_g, v_ffn_norm_g, v_ffn_w1, v_ffn_w2, v_cp_w_in, v_cp_conv_w, v_cp_conv_b, v_cp_ln_g, v_cp_ln_b, v_cp_pool_w, v_cp_pool_scale, v_cp_w_out, v_gla_w_in, v_gla_gate_w2, v_gla_gate_b, v_gla_head_g, v_gla_w_out, v_final_norm_g):
    given = dict(x=x, meta_tokens=meta_tokens, mix_norm_g=mix_norm_g, ffn_norm_g=ffn_norm_g, ffn_w1=ffn_w1, ffn_w2=ffn_w2, cp_w_in=cp_w_in, cp_conv_w=cp_conv_w, cp_conv_b=cp_conv_b, cp_ln_g=cp_ln_g, cp_ln_b=cp_ln_b, cp_pool_w=cp_pool_w, cp_pool_scale=cp_pool_scale, cp_w_out=cp_w_out, gla_w_in=gla_w_in, gla_gate_w2=gla_gate_w2, gla_gate_b=gla_gate_b, gla_head_g=gla_head_g, gla_w_out=gla_w_out, final_norm_g=final_norm_g, loss_target=loss_target, m_meta_tokens=m_meta_tokens, m_mix_norm_g=m_mix_norm_g, m_ffn_norm_g=m_ffn_norm_g, m_ffn_w1=m_ffn_w1, m_ffn_w2=m_ffn_w2, m_cp_w_in=m_cp_w_in, m_cp_conv_w=m_cp_conv_w, m_cp_conv_b=m_cp_conv_b, m_cp_ln_g=m_cp_ln_g, m_cp_ln_b=m_cp_ln_b, m_cp_pool_w=m_cp_pool_w, m_cp_pool_scale=m_cp_pool_scale, m_cp_w_out=m_cp_w_out, m_gla_w_in=m_gla_w_in, m_gla_gate_w2=m_gla_gate_w2, m_gla_gate_b=m_gla_gate_b, m_gla_head_g=m_gla_head_g, m_gla_w_out=m_gla_w_out, m_final_norm_g=m_final_norm_g, v_meta_tokens=v_meta_tokens, v_mix_norm_g=v_mix_norm_g, v_ffn_norm_g=v_ffn_norm_g, v_ffn_w1=v_ffn_w1, v_ffn_w2=v_ffn_w2, v_cp_w_in=v_cp_w_in, v_cp_conv_w=v_cp_conv_w, v_cp_conv_b=v_cp_conv_b, v_cp_ln_g=v_cp_ln_g, v_cp_ln_b=v_cp_ln_b, v_cp_pool_w=v_cp_pool_w, v_cp_pool_scale=v_cp_pool_scale, v_cp_w_out=v_cp_w_out, v_gla_w_in=v_gla_w_in, v_gla_gate_w2=v_gla_gate_w2, v_gla_gate_b=v_gla_gate_b, v_gla_head_g=v_gla_head_g, v_gla_w_out=v_gla_w_out, v_final_norm_g=v_final_norm_g)
    weights = {n: given[n] for n in TWIN_WEIGHTS}
    shared = {n: given[n] for n in SHARED_INPUTS}
    per_example = {n: given[n] for n in ['x']}
    grad_fn = _jax.value_and_grad(_loss, argnums=(0, 1))

    def one_microbatch(ex, loss_target):
        ex = dict(ex)
        diff = ex.pop(TWIN_DIFF_INPUT)
        return grad_fn(weights, diff, {**shared, **ex}, loss_target)

    if N_MICROBATCH == 1:
        loss, (grad_w, grad_x) = one_microbatch(per_example, given["loss_target"])
    else:
        def body(carry, xs):
            loss_sum, grad_sum = carry
            l_k, (gw_k, gx_k) = one_microbatch(xs[0], xs[1])
            with _jax.named_scope("update"):
                return (loss_sum + l_k, _jax.tree.map(_jnp.add, grad_sum, gw_k)), gx_k

        init = (_jnp.zeros((), _jnp.float32), _jax.tree.map(_jnp.zeros_like, weights))
        (loss, grad_w), grad_x = _jax.lax.scan(body, init, (per_example, given["loss_target"]))
    with _jax.named_scope("update"):
        delta_w, new_m, new_v = {}, {}, {}
        for n in TWIN_WEIGHTS:
            delta_w[n], new_m[n], new_v[n] = _adamw(weights[n], grad_w[n], given["m_" + n], given["v_" + n])
    return (loss, grad_x, *[grad_w[n] for n in TWIN_WEIGHTS], *[delta_w[n] for n in TWIN_WEIGHTS],
            *[new_m[n] for n in TWIN_WEIGHTS], *[new_v[n] for n in TWIN_WEIGHTS])
```

```python
import functools

import jax
import jax.numpy as jnp
from jax import lax
from jax.experimental import pallas as pl
from jax.experimental.pallas import tpu as pltpu

F32, BF16 = jnp.float32, jnp.bfloat16
N_DEV = 8
CHUNK = 64
N_META = 16
PAD_ROWS = CHUNK - N_META
HALO = 32
EPS = 1e-5
CONV_WIDTH = 31
POOL_WINDOWS = (2, 4, 8, 16)
HEADS = 4
GATE_RANK = 16
GATE_NORM = 16.0
GATE_PAD = 128
ADAM_LR, ADAM_B1, ADAM_B2, ADAM_EPS, ADAM_WD, ADAM_STEP = 0.001, 0.9, 0.999, 1e-08, 0.01, 10
V7X_VMEM_LIMIT = 56 * 2 ** 20
LANE = 128
HI = lax.Precision.HIGHEST


def _cparams(*sem):
    return pltpu.CompilerParams(dimension_semantics=sem, vmem_limit_bytes=V7X_VMEM_LIMIT)


def _row_tile(t, cap):
    best = CHUNK
    for r in range(CHUNK, min(t, cap) + 1, CHUNK):
        if t % r == 0:
            best = r
    return best


def _resident(shape):
    return pl.BlockSpec(shape, lambda *_: (0,) * len(shape), pipeline_mode=pl.Buffered(1))


def _dot(a, b):
    return jnp.dot(a, b, preferred_element_type=F32)


def _dot_nt(a, b):
    return lax.dot_general(a, b, (((1,), (1,)), ((), ())), preferred_element_type=F32)


def _dot_tn(a, b):
    return lax.dot_general(a, b, (((0,), (0,)), ((), ())), preferred_element_type=F32)


def _rowsum(a):
    return jnp.sum(a, axis=0, keepdims=True)


def _sigmoid(a):
    return 1.0 / (1.0 + jnp.exp(-a))


def _row_ids(tile, rt):
    return tile * rt + lax.broadcasted_iota(jnp.int32, (rt, 1), 0)


def _sds(shape, dtype):
    return jax.ShapeDtypeStruct(shape, dtype)


def _linear_fwd(x, w, *, gain=None, res=None, out_dtype=F32, name):
    t, k = x.shape
    n = w.shape[1]
    rt = _row_tile(t, 320)

    def body(*refs):
        refs = list(refs)
        x_ref, w_ref = refs[:2]
        pos = 2
        g_ref = r_ref = u_ref = None
        if gain is not None:
            g_ref = refs[pos]
            pos += 1
        if res is not None:
            r_ref = refs[pos]
            pos += 1
        y_ref = refs[pos]
        if gain is not None:
            u_ref = refs[pos + 1]
            xv = x_ref[...]
            u = (xv * lax.rsqrt(jnp.mean(xv * xv, axis=-1, keepdims=True) + EPS) * g_ref[...]).astype(BF16)
            u_ref[...] = u
        else:
            u = x_ref[...]
        y = _dot(u, w_ref[...])
        if res is not None:
            y = y + r_ref[...]
        y_ref[...] = y.astype(y_ref.dtype)

    rows = lambda i: (i, 0)
    in_specs = [pl.BlockSpec((rt, k), rows), _resident((k, n))]
    args = [x, w]
    if gain is not None:
        in_specs.append(_resident((1, k)))
        args.append(gain)
    if res is not None:
        in_specs.append(pl.BlockSpec((rt, n), rows))
        args.append(res)
    out_shape = [_sds((t, n), out_dtype)]
    out_specs = [pl.BlockSpec((rt, n), rows)]
    if gain is not None:
        out_shape.append(_sds((t, k), BF16))
        out_specs.append(pl.BlockSpec((rt, k), rows))
    out = pl.pallas_call(body, grid=(t // rt,), in_specs=in_specs, out_specs=out_specs, out_shape=out_shape,
                         compiler_params=_cparams("parallel"), name=name)(*args)
    return out if gain is not None else out[0]


def _linear_bwd_x(dys, ws, *, norm=None, name):
    t = dys[0].shape[0]
    k = ws[0].shape[0]
    rt = _row_tile(t, 320)
    nd = len(dys)

    def body(*refs):
        refs = list(refs)
        dy_refs, w_refs = refs[:nd], refs[nd:2 * nd]
        dx = None
        for dy_ref, w_ref in zip(dy_refs, w_refs):
            part = _dot_nt(dy_ref[...].astype(BF16), w_ref[...])
            dx = part if dx is None else dx + part
        if norm is None:
            refs[2 * nd][...] = dx
            return
        h_ref, g_ref, dres_ref, dh_ref, dg_ref = refs[2 * nd:]
        hv = h_ref[...]
        rstd = lax.rsqrt(jnp.mean(hv * hv, axis=-1, keepdims=True) + EPS)
        xh = hv * rstd

        @pl.when(pl.program_id(0) == 0)
        def _():
            dg_ref[...] = jnp.zeros_like(dg_ref)

        dg_ref[...] += _rowsum(dx * xh)
        dxh = dx * g_ref[...]
        dh_ref[...] = dres_ref[...] + rstd * (dxh - xh * jnp.mean(dxh * xh, axis=-1, keepdims=True))

    rows = lambda i: (i, 0)
    in_specs = [pl.BlockSpec((rt, dy.shape[1]), rows) for dy in dys] + [_resident(w.shape) for w in ws]
    args = list(dys) + list(ws)
    out_shape = [_sds((t, k), F32)]
    out_specs = [pl.BlockSpec((rt, k), rows)]
    if norm is not None:
        h, gain, dres = norm
        in_specs += [pl.BlockSpec((rt, k), rows), _resident((1, k)), pl.BlockSpec((rt, k), rows)]
        args += [h, gain, dres]
        out_shape.append(_sds((1, k), F32))
        out_specs.append(pl.BlockSpec((1, k), lambda i: (0, 0)))
    out = pl.pallas_call(body, grid=(t // rt,), in_specs=in_specs, out_specs=out_specs, out_shape=out_shape,
                         compiler_params=_cparams("arbitrary"), name=name)(*args)
    return out if norm is not None else out[0]


def _linear_bwd_w(x, dy, *, name):
    t, k = x.shape
    n = dy.shape[1]
    rt = _row_tile(t, 832)
    nt = max(c for c in (512, 384, 256, LANE) if n % c == 0)

    def body(x_ref, dy_ref, o_ref):
        @pl.when(pl.program_id(1) == 0)
        def _():
            o_ref[...] = jnp.zeros_like(o_ref)

        o_ref[...] += _dot_tn(x_ref[...], dy_ref[...].astype(BF16))

    return pl.pallas_call(
        body, grid=(n // nt, t // rt),
        in_specs=[pl.BlockSpec((rt, k), lambda j, i: (i, 0)), pl.BlockSpec((rt, nt), lambda j, i: (i, j))],
        out_specs=pl.BlockSpec((k, nt), lambda j, i: (0, j)), out_shape=_sds((k, n), F32),
        compiler_params=_cparams("parallel", "arbitrary"), name=name)(x, dy)


def _ffn_fwd(h, gain, w1g, w2g, layer, *, name):
    t, d = h.shape
    f8 = w1g.shape[-1]
    rt = _row_tile(t, 832)

    def body(h_ref, g_ref, w1_ref, w2_ref, o_ref, u_ref, acc_ref):
        j = pl.program_id(1)

        @pl.when(j == 0)
        def _():
            hv = h_ref[...]
            u_ref[...] = (hv * lax.rsqrt(jnp.mean(hv * hv, axis=-1, keepdims=True) + EPS) * g_ref[...]).astype(BF16)
            acc_ref[...] = jnp.zeros_like(acc_ref)

        a = jnp.maximum(_dot(u_ref[...], w1_ref[...]), 0.0)
        acc_ref[...] += _dot((a * a).astype(BF16), w2_ref[...])

        @pl.when(j == N_DEV - 1)
        def _():
            o_ref[...] = h_ref[...] + acc_ref[...]

    return pl.pallas_call(
        body, grid=(t // rt, N_DEV),
        in_specs=[pl.BlockSpec((rt, d), lambda i, j: (i, 0)), _resident((1, d)),
                  pl.BlockSpec((None, None, d, f8), lambda i, j: (j, layer, 0, 0)),
                  pl.BlockSpec((None, None, f8, d), lambda i, j: (j, layer, 0, 0))],
        out_specs=pl.BlockSpec((rt, d), lambda i, j: (i, 0)), out_shape=_sds((t, d), F32),
        scratch_shapes=[pltpu.VMEM((rt, d), BF16), pltpu.VMEM((rt, d), F32)],
        compiler_params=_cparams("parallel", "arbitrary"), name=name)(h, gain, w1g, w2g)


def _ffn_bwd(h, dout, gain, w1g, w2g, layer, *, name):
    t, d = h.shape
    f8 = w1g.shape[-1]
    rt = _row_tile(t, 320)
    nrow = t // rt
    last = N_DEV - 1

    def body(h_ref, do_ref, g_ref, w1_ref, w2_ref, dh_ref, dw1_ref, dw2_ref, dg_ref, du_ref):
        j, i = pl.program_id(0), pl.program_id(1)
        hv = h_ref[...]
        rstd = lax.rsqrt(jnp.mean(hv * hv, axis=-1, keepdims=True) + EPS)
        xh = hv * rstd
        u = (xh * g_ref[...]).astype(BF16)
        db = do_ref[...].astype(BF16)
        r = jnp.maximum(_dot(u, w1_ref[...]), 0.0)
        act = (r * r).astype(BF16)
        dhh = (_dot_nt(db, w2_ref[...]) * (2.0 * r)).astype(BF16)

        @pl.when(i == 0)
        def _():
            dw1_ref[...] = jnp.zeros_like(dw1_ref)
            dw2_ref[...] = jnp.zeros_like(dw2_ref)

        dw1_ref[...] += _dot_tn(u, dhh)
        dw2_ref[...] += _dot_tn(act, db)
        du_part = _dot_nt(dhh, w1_ref[...])
        rows = pl.ds(pl.multiple_of(i * rt, CHUNK), rt)

        @pl.when(j == 0)
        def _():
            du_ref[rows, :] = du_part

        @pl.when(j > 0)
        def _():
            du_ref[rows, :] += du_part

        @pl.when(j == last)
        def _():
            @pl.when(i == 0)
            def _():
                dg_ref[...] = jnp.zeros_like(dg_ref)

            du = du_ref[rows, :]
            dg_ref[...] += _rowsum(du * xh)
            dxh = du * g_ref[...]
            dh_ref[...] = do_ref[...] + rstd * (dxh - xh * jnp.mean(dxh * xh, axis=-1, keepdims=True))

    return pl.pallas_call(
        body, grid=(N_DEV, nrow),
        in_specs=[pl.BlockSpec((rt, d), lambda j, i: (i, 0)), pl.BlockSpec((rt, d), lambda j, i: (i, 0)),
                  _resident((1, d)),
                  pl.BlockSpec((None, None, d, f8), lambda j, i: (j, layer, 0, 0)),
                  pl.BlockSpec((None, None, f8, d), lambda j, i: (j, layer, 0, 0))],
        out_specs=[pl.BlockSpec((rt, d), lambda j, i: (jnp.where(j == last, i, 0), 0)),
                   pl.BlockSpec((None, d, f8), lambda j, i: (j, 0, 0)),
                   pl.BlockSpec((None, f8, d), lambda j, i: (j, 0, 0)),
                   pl.BlockSpec((1, d), lambda j, i: (0, 0))],
        out_shape=[_sds((t, d), F32), _sds((N_DEV, d, f8), F32), _sds((N_DEV, f8, d), F32), _sds((1, d), F32)],
        scratch_shapes=[pltpu.VMEM((t, d), F32)],
        compiler_params=_cparams("arbitrary", "arbitrary"), name=name)(h, dout, gain, w1g, w2g)


def _lane_blocks(width):
    lb = min(LANE, width)
    return [slice(s, s + lb) for s in range(0, width, lb)]


def _conv_rows(src_ref, w_ref, offset, dst_ref, nblk, width, bias_ref=None):
    def blk(rb, carry):
        base = pl.multiple_of(rb * CHUNK, CHUNK)
        for l, ls in enumerate(_lane_blocks(width)):
            acc = jnp.zeros((CHUNK, ls.stop - ls.start), F32)
            if bias_ref is not None:
                acc = acc + bias_ref[:, ls]
            for k in range(CONV_WIDTH):
                acc = acc + w_ref[k:k + 1, ls] * src_ref[l, pl.ds(base + offset(k), CHUNK), :]
            dst_ref[l, pl.ds(base, CHUNK), :] = acc
        return carry

    lax.fori_loop(0, nblk, blk, 0)


def _to_lane_blocks(ref, row0, value):
    for l, ls in enumerate(_lane_blocks(value.shape[1])):
        ref[l, row0:row0 + value.shape[0], :] = value[:, ls]


def _from_lane_blocks(ref):
    return jnp.concatenate([ref[l] for l in range(ref.shape[0])], axis=1)


def _pool_counts(rows, window):
    return jnp.clip(rows - PAD_ROWS + 1, 1, window).astype(F32)


def _trailing_sum(v, window):
    s, sh = v, 1
    while sh < window:
        s = s + pltpu.roll(s, sh, 0)
        sh *= 2
    return s


def _leading_sum(v, window):
    s, sh, n = v, 1, v.shape[0]
    while sh < window:
        s = s + pltpu.roll(s, n - sh, 0)
        sh *= 2
    return s


def _cp_mid_fwd(z, conv_w, conv_b, ln_g, ln_b, pool_w, pool_scale, *, name):
    t, ein = z.shape
    cd = conv_b.shape[1]
    pd = pool_scale.shape[1]
    pg = pd // len(POOL_WINDOWS)
    rt = _row_tile(t, 320)

    def body(z_ref, cw_ref, cb_ref, lg_ref, lb_ref, pw_ref, ps_ref, o_ref, gext, pext, conv_s):
        i = pl.program_id(0)

        @pl.when(i == 0)
        def _():
            _to_lane_blocks(gext, 0, jnp.zeros((HALO, cd), F32))
            pext[0:HALO, :] = jnp.zeros((HALO, pd), F32)

        _to_lane_blocks(gext, HALO, z_ref[:, 0:cd] * _sigmoid(z_ref[:, cd:2 * cd]))
        pext[HALO:HALO + rt, :] = z_ref[:, 2 * cd:]
        _conv_rows(gext, cw_ref, lambda k: k + HALO - (CONV_WIDTH - 1), conv_s, rt // CHUNK, cd, cb_ref)
        cv = _from_lane_blocks(conv_s)
        xc = cv - jnp.mean(cv, axis=-1, keepdims=True)
        y = xc * lax.rsqrt(jnp.mean(xc * xc, axis=-1, keepdims=True) + EPS) * lg_ref[...] + lb_ref[...]
        rows = _row_ids(i, rt)
        a = jnp.where(rows >= PAD_ROWS, y * _sigmoid(y), 0.0)
        o_ref[:, 0:cd] = a.astype(BF16)
        for gi, window in enumerate(POOL_WINDOWS):
            ls = slice(gi * pg, (gi + 1) * pg)
            v = pext[:, ls]
            tm = _trailing_sum(v, window)[HALO:] / _pool_counts(rows, window) - v[HALO:]
            p = _dot(tm.astype(BF16), pw_ref[gi]) * ps_ref[:, ls]
            o_ref[:, cd + gi * pg:cd + (gi + 1) * pg] = p.astype(BF16)
        gext[:, 0:HALO, :] = gext[:, rt:rt + HALO, :]
        pext[0:HALO, :] = pext[rt:rt + HALO, :]

    nl, lb = len(_lane_blocks(cd)), min(LANE, cd)
    return pl.pallas_call(
        body, grid=(t // rt,),
        in_specs=[pl.BlockSpec((rt, ein), lambda i: (i, 0)), _resident(conv_w.shape), _resident((1, cd)),
                  _resident((1, cd)), _resident((1, cd)), _resident(pool_w.shape), _resident((1, pd))],
        out_specs=pl.BlockSpec((rt, cd + pd), lambda i: (i, 0)), out_shape=_sds((t, cd + pd), BF16),
        scratch_shapes=[pltpu.VMEM((nl, rt + HALO, lb), F32), pltpu.VMEM((rt + HALO, pd), F32),
                        pltpu.VMEM((nl, rt, lb), F32)],
        compiler_params=_cparams("arbitrary"), name=name)(z, conv_w, conv_b, ln_g, ln_b, pool_w, pool_scale)


def _cp_mid_bwd(z, dcat, conv_w, conv_b, ln_g, ln_b, pool_w, pool_scale, *, name):
    t, ein = z.shape
    cd = conv_b.shape[1]
    pd = pool_scale.shape[1]
    pg = pd // len(POOL_WINDOWS)
    rt = _row_tile(t, 320)
    ntile = t // rt
    per = rt // CHUNK

    def body(z_ref, zh_ref, dc_ref, cw_ref, cb_ref, lg_ref, lb_ref, pw_ref, ps_ref,
             dz_ref, dcw_ref, dcb_ref, dlg_ref, dlb_ref, dpw_ref, dps_ref, gext, pext, conv_s, dcv, dsp):
        step = pl.program_id(0)
        tile = ntile - 1 - step

        @pl.when(step == 0)
        def _():
            for ref in (dcw_ref, dcb_ref, dlg_ref, dlb_ref, dpw_ref, dps_ref):
                ref[...] = jnp.zeros_like(ref)
            _to_lane_blocks(dcv, rt, jnp.zeros((HALO, cd), F32))
            dsp[rt:rt + HALO, :] = jnp.zeros((HALO, pd), F32)

        keep = jnp.where(tile > 0, 1.0, 0.0)
        zh = zh_ref[CHUNK - HALO:CHUNK, :]
        _to_lane_blocks(gext, 0, keep * zh[:, 0:cd] * _sigmoid(zh[:, cd:2 * cd]))
        pext[0:HALO, :] = keep * zh[:, 2 * cd:]
        za = z_ref[:, 0:cd]
        sg = _sigmoid(z_ref[:, cd:2 * cd])
        _to_lane_blocks(gext, HALO, za * sg)
        pext[HALO:HALO + rt, :] = z_ref[:, 2 * cd:]
        _conv_rows(gext, cw_ref, lambda k: k + HALO - (CONV_WIDTH - 1), conv_s, per, cd, cb_ref)
        cv = _from_lane_blocks(conv_s)
        xc = cv - jnp.mean(cv, axis=-1, keepdims=True)
        rstd = lax.rsqrt(jnp.mean(xc * xc, axis=-1, keepdims=True) + EPS)
        xh = xc * rstd
        y = xh * lg_ref[...] + lb_ref[...]
        sy = _sigmoid(y)
        rows = _row_ids(tile, rt)
        da = jnp.where(rows >= PAD_ROWS, dc_ref[:, 0:cd], 0.0)
        dy = da * (sy * (1.0 + y * (1.0 - sy)))
        dlg_ref[...] += _rowsum(dy * xh)
        dlb_ref[...] += _rowsum(dy)
        dxh = dy * lg_ref[...]
        dconv = rstd * (dxh - jnp.mean(dxh, axis=-1, keepdims=True) - xh * jnp.mean(dxh * xh, axis=-1, keepdims=True))
        dcb_ref[...] += _rowsum(dconv)
        _to_lane_blocks(dcv, 0, dconv)
        for l, ls in enumerate(_lane_blocks(cd)):
            for k in range(CONV_WIDTH):
                def acc_rows(rb, acc, l=l, k=k):
                    base = pl.multiple_of(rb * CHUNK, CHUNK)
                    return acc + dcv[l, pl.ds(base, CHUNK), :] * gext[l, pl.ds(base + k + HALO - (CONV_WIDTH - 1), CHUNK), :]

                acc = lax.fori_loop(0, per, acc_rows, jnp.zeros((CHUNK, ls.stop - ls.start), F32))
                dcw_ref[k:k + 1, ls] += _rowsum(acc)
        _conv_rows(dcv, cw_ref, lambda k: CONV_WIDTH - 1 - k, conv_s, per, cd)
        dglu = _from_lane_blocks(conv_s)
        dz_ref[:, 0:cd] = (dglu * sg).astype(BF16)
        dz_ref[:, cd:2 * cd] = (dglu * za * sg * (1.0 - sg)).astype(BF16)
        dcv[:, rt:rt + HALO, :] = dcv[:, 0:HALO, :]
        for gi, window in enumerate(POOL_WINDOWS):
            ls = slice(gi * pg, (gi + 1) * pg)
            v = pext[:, ls]
            cnt = _pool_counts(rows, window)
            tm = (_trailing_sum(v, window)[HALO:] / cnt - v[HALO:]).astype(BF16)
            dp = dc_ref[:, cd + gi * pg:cd + (gi + 1) * pg]
            dps_ref[:, ls] += _rowsum(dp * _dot(tm, pw_ref[gi]))
            dpl = (dp * ps_ref[:, ls]).astype(BF16)
            dpw_ref[gi] += _dot_tn(tm, dpl)
            dtm = _dot_nt(dpl, pw_ref[gi])
            dsp[0:rt, ls] = dtm / cnt
            dpin = _leading_sum(dsp[:, ls], window)[0:rt] - dtm
            dz_ref[:, 2 * cd + gi * pg:2 * cd + (gi + 1) * pg] = dpin.astype(BF16)
        dsp[rt:rt + HALO, :] = dsp[0:HALO, :]

    back = lambda i: (ntile - 1 - i, 0)
    halo_idx = lambda i: (jnp.maximum((ntile - 1 - i) * per - 1, 0), 0)
    const2 = lambda i: (0, 0)
    nl, lb = len(_lane_blocks(cd)), min(LANE, cd)
    return pl.pallas_call(
        body, grid=(ntile,),
        in_specs=[pl.BlockSpec((rt, ein), back), pl.BlockSpec((CHUNK, ein), halo_idx), pl.BlockSpec((rt, cd + pd), back),
                  _resident(conv_w.shape), _resident((1, cd)), _resident((1, cd)), _resident((1, cd)),
                  _resident(pool_w.shape), _resident((1, pd))],
        out_specs=[pl.BlockSpec((rt, ein), back), pl.BlockSpec(conv_w.shape, const2), pl.BlockSpec((1, cd), const2),
                   pl.BlockSpec((1, cd), const2), pl.BlockSpec((1, cd), const2),
                   pl.BlockSpec(pool_w.shape, lambda i: (0, 0, 0)), pl.BlockSpec((1, pd), const2)],
        out_shape=[_sds((t, ein), BF16), _sds(conv_w.shape, F32), _sds((1, cd), F32), _sds((1, cd), F32),
                   _sds((1, cd), F32), _sds(pool_w.shape, F32), _sds((1, pd), F32)],
        scratch_shapes=[pltpu.VMEM((nl, rt + HALO, lb), F32), pltpu.VMEM((rt + HALO, pd), F32), pltpu.VMEM((nl, rt, lb), F32),
                        pltpu.VMEM((nl, rt + HALO, lb), F32), pltpu.VMEM((rt + HALO, pd), F32)],
        compiler_params=_cparams("arbitrary"), name=name)(z, z, dcat, conv_w, conv_b, ln_g, ln_b, pool_w, pool_scale)


def _log_decay(r_ref, gw_ref, gb_ref, rows):
    gp = _dot(r_ref[...].astype(BF16), gw_ref[...]) + gb_ref[...]
    log_sig = jnp.minimum(gp, 0.0) - jnp.log(1.0 + jnp.exp(-jnp.abs(gp)))
    return gp, jnp.where(rows >= PAD_ROWS, log_sig / GATE_NORM, 0.0)


def _tri(strict):
    r = lax.broadcasted_iota(jnp.int32, (CHUNK, CHUNK), 0)
    c = lax.broadcasted_iota(jnp.int32, (CHUNK, CHUNK), 1)
    return jnp.where(c < r if strict else c <= r, 1.0, 0.0).astype(F32)


def _gla_mid_fwd(z, r, gate_w, gate_b, head_g, *, name):
    t = z.shape[0]
    dk = gate_b.shape[1]
    hv = head_g.shape[1]
    hk = dk // HEADS
    dv = hv * HEADS
    rt = _row_tile(t, 320)
    per = rt // CHUNK
    scale = hk ** -0.5

    def body(z_ref, r_ref, gw_ref, gb_ref, hg_ref, o_ref, st_ref, s_ref, la_ref):
        i = pl.program_id(0)

        @pl.when(i == 0)
        def _():
            s_ref[...] = jnp.zeros_like(s_ref)

        _, la = _log_decay(r_ref, gw_ref, gb_ref, _row_ids(i, rt))
        la_ref[...] = la
        tri = _tri(False)

        def chunk(c, carry):
            rows = pl.ds(pl.multiple_of(c * CHUNK, CHUNK), CHUNK)
            la_c = la_ref[rows, :]
            cum = jnp.dot(tri, la_c, precision=HI, preferred_element_type=F32)
            tot = _rowsum(la_c)
            dec = jnp.exp(tot - cum)
            etot = jnp.exp(tot)
            for hd in range(HEADS):
                ks = slice(hd * hk, (hd + 1) * hk)
                q = z_ref[rows, hd * hk:(hd + 1) * hk] * scale
                kd = z_ref[rows, dk + hd * hk:dk + (hd + 1) * hk] * dec[:, ks]
                v = z_ref[rows, 2 * dk + hd * hv:2 * dk + (hd + 1) * hv]
                g = z_ref[rows, 2 * dk + dv + hd * hv:2 * dk + dv + (hd + 1) * hv]
                s_new = s_ref[hd] * etot[:, ks] + _dot_tn(v.astype(BF16), kd.astype(BF16))
                s_ref[hd] = s_new
                st_ref[c, hd] = s_new
                o = _dot_nt(q.astype(BF16), s_new.astype(BF16))
                on = o * lax.rsqrt(jnp.mean(o * o, axis=-1, keepdims=True) + EPS) * hg_ref[...]
                o_ref[rows, hd * hv:(hd + 1) * hv] = (on * (g * _sigmoid(g))).astype(BF16)
            return carry

        lax.fori_loop(0, per, chunk, 0)

    return pl.pallas_call(
        body, grid=(t // rt,),
        in_specs=[pl.BlockSpec((rt, z.shape[1]), lambda i: (i, 0)), pl.BlockSpec((rt, GATE_PAD), lambda i: (i, 0)),
                  _resident(gate_w.shape), _resident((1, dk)), _resident((1, hv))],
        out_specs=[pl.BlockSpec((rt, dv), lambda i: (i, 0)), pl.BlockSpec((per, HEADS, hv, hk), lambda i: (i, 0, 0, 0))],
        out_shape=[_sds((t, dv), BF16), _sds((t // CHUNK, HEADS, hv, hk), F32)],
        scratch_shapes=[pltpu.VMEM((HEADS, hv, hk), F32), pltpu.VMEM((rt, dk), F32)],
        compiler_params=_cparams("arbitrary"), name=name)(z, r, gate_w, gate_b, head_g)


def _gla_mid_bwd(z, r, dog, states, gate_w, gate_b, head_g, *, name):
    t = z.shape[0]
    dk = gate_b.shape[1]
    hv = head_g.shape[1]
    hk = dk // HEADS
    dv = hv * HEADS
    rt = _row_tile(t, 320)
    ntile = t // rt
    per = rt // CHUNK
    scale = hk ** -0.5

    def body(z_ref, r_ref, do_ref, st_ref, stp_ref, gw_ref, gb_ref, hg_ref,
             dz_ref, dr_ref, dgw_ref, dgb_ref, dhg_ref, ds_ref, la_ref, dla_ref):
        step = pl.program_id(0)
        tile = ntile - 1 - step

        @pl.when(step == 0)
        def _():
            ds_ref[...] = jnp.zeros_like(ds_ref)
            dgw_ref[...] = jnp.zeros_like(dgw_ref)
            dgb_ref[...] = jnp.zeros_like(dgb_ref)
            dhg_ref[...] = jnp.zeros_like(dhg_ref)

        rows_id = _row_ids(tile, rt)
        gp, la = _log_decay(r_ref, gw_ref, gb_ref, rows_id)
        la_ref[...] = la
        tri, tri_strict = _tri(False), _tri(True)
        keep = jnp.where(tile > 0, 1.0, 0.0)

        def chunk(cc, carry):
            c = per - 1 - cc
            rows = pl.ds(pl.multiple_of(c * CHUNK, CHUNK), CHUNK)
            la_c = la_ref[rows, :]
            cum = jnp.dot(tri, la_c, precision=HI, preferred_element_type=F32)
            tot = _rowsum(la_c)
            dec = jnp.exp(tot - cum)
            etot = jnp.exp(tot)
            inside = jnp.where(c > 0, 1.0, 0.0)
            for hd in range(HEADS):
                ks = slice(hd * hk, (hd + 1) * hk)
                q = (z_ref[rows, hd * hk:(hd + 1) * hk] * scale).astype(BF16)
                k = z_ref[rows, dk + hd * hk:dk + (hd + 1) * hk]
                kd = k * dec[:, ks]
                v = z_ref[rows, 2 * dk + hd * hv:2 * dk + (hd + 1) * hv].astype(BF16)
                g = z_ref[rows, 2 * dk + dv + hd * hv:2 * dk + dv + (hd + 1) * hv]
                s_now = st_ref[c, hd]
                s_prev = inside * st_ref[jnp.maximum(c - 1, 0), hd] + (1.0 - inside) * keep * stp_ref[0, hd]
                s_b = s_now.astype(BF16)
                o = _dot_nt(q, s_b)
                rstd = lax.rsqrt(jnp.mean(o * o, axis=-1, keepdims=True) + EPS)
                oh = o * rstd
                sg = _sigmoid(g)
                d_og = do_ref[rows, hd * hv:(hd + 1) * hv]
                dz_ref[rows, 2 * dk + dv + hd * hv:2 * dk + dv + (hd + 1) * hv] = (
                    d_og * oh * hg_ref[...] * (sg * (1.0 + g * (1.0 - sg)))).astype(BF16)
                don = d_og * (g * sg)
                dhg_ref[...] += _rowsum(don * oh)
                doh = don * hg_ref[...]
                d_o = (rstd * (doh - oh * jnp.mean(doh * oh, axis=-1, keepdims=True))).astype(BF16)
                dz_ref[rows, hd * hk:(hd + 1) * hk] = (_dot(d_o, s_b) * scale).astype(BF16)
                ds_t = ds_ref[hd] + _dot_tn(d_o, q)
                ds_b = ds_t.astype(BF16)
                dkd = _dot(v, ds_b)
                dz_ref[rows, 2 * dk + hd * hv:2 * dk + (hd + 1) * hv] = _dot_nt(kd.astype(BF16), ds_b).astype(BF16)
                dtot = etot[:, ks] * _rowsum(ds_t * s_prev)
                ds_ref[hd] = ds_t * etot[:, ks]
                dz_ref[rows, dk + hd * hk:dk + (hd + 1) * hk] = (dkd * dec[:, ks]).astype(BF16)
                e = dkd * kd
                dla_ref[rows, ks] = dtot + jnp.dot(tri_strict, e, precision=HI, preferred_element_type=F32)
            return carry

        lax.fori_loop(0, per, chunk, 0)
        dla = jnp.where(rows_id >= PAD_ROWS, dla_ref[...], 0.0)
        dgp = dla * (1.0 / GATE_NORM) * (1.0 - _sigmoid(gp))
        dgb_ref[...] += _rowsum(dgp)
        dgp_b = dgp.astype(BF16)
        dgw_ref[...] += _dot_tn(r_ref[...].astype(BF16), dgp_b)
        dr_ref[...] = _dot_nt(dgp_b, gw_ref[...]).astype(BF16)

    back = lambda i: (ntile - 1 - i, 0)
    const2 = lambda i: (0, 0)
    return pl.pallas_call(
        body, grid=(ntile,),
        in_specs=[pl.BlockSpec((rt, z.shape[1]), back), pl.BlockSpec((rt, GATE_PAD), back), pl.BlockSpec((rt, dv), back),
                  pl.BlockSpec((per, HEADS, hv, hk), lambda i: (ntile - 1 - i, 0, 0, 0)),
                  pl.BlockSpec((1, HEADS, hv, hk), lambda i: (jnp.maximum((ntile - 1 - i) * per - 1, 0), 0, 0, 0)),
                  _resident(gate_w.shape), _resident((1, dk)), _resident((1, hv))],
        out_specs=[pl.BlockSpec((rt, z.shape[1]), back), pl.BlockSpec((rt, GATE_PAD), back),
                   pl.BlockSpec(gate_w.shape, const2), pl.BlockSpec((1, dk), const2), pl.BlockSpec((1, hv), const2)],
        out_shape=[_sds(z.shape, BF16), _sds((t, GATE_PAD), BF16), _sds(gate_w.shape, F32), _sds((1, dk), F32),
                   _sds((1, hv), F32)],
        scratch_shapes=[pltpu.VMEM((HEADS, hv, hk), F32), pltpu.VMEM((rt, dk), F32), pltpu.VMEM((rt, dk), F32)],
        compiler_params=_cparams("arbitrary"), name=name)(z, r, dog, states, states, gate_w, gate_b, head_g)


def _head(h, gain, target, *, name):
    t, d = h.shape
    rt = _row_tile(t, 832)

    def body(h_ref, g_ref, t_ref, dh_ref, loss_ref, dg_ref):
        i = pl.program_id(0)

        @pl.when(i == 0)
        def _():
            loss_ref[...] = jnp.zeros_like(loss_ref)
            dg_ref[...] = jnp.zeros_like(dg_ref)

        hv = h_ref[...]
        rstd = lax.rsqrt(jnp.mean(hv * hv, axis=-1, keepdims=True) + EPS)
        xh = hv * rstd
        err = jnp.where(_row_ids(i, rt) >= CHUNK, xh * g_ref[...] - t_ref[...], 0.0)
        loss_ref[...] += (0.5 / d) * jnp.sum(err * err)
        dy = err * (1.0 / d)
        dg_ref[...] += _rowsum(dy * xh)
        dxh = dy * g_ref[...]
        dh_ref[...] = rstd * (dxh - xh * jnp.mean(dxh * xh, axis=-1, keepdims=True))

    return pl.pallas_call(
        body, grid=(t // rt,),
        in_specs=[pl.BlockSpec((rt, d), lambda i: (i, 0)), _resident((1, d)), pl.BlockSpec((rt, d), lambda i: (i, 0))],
        out_specs=[pl.BlockSpec((rt, d), lambda i: (i, 0)), pl.BlockSpec((8, LANE), lambda i: (0, 0)),
                   pl.BlockSpec((1, d), lambda i: (0, 0))],
        out_shape=[_sds((t, d), F32), _sds((8, LANE), F32), _sds((1, d), F32)],
        compiler_params=_cparams("arbitrary"), name=name)(h, gain, target)


def _adamw_math(w, g, m, v):
    m = ADAM_B1 * m + (1.0 - ADAM_B1) * g
    v = ADAM_B2 * v + (1.0 - ADAM_B2) * (g * g)
    m_hat = m / (1.0 - ADAM_B1 ** ADAM_STEP)
    v_hat = v / (1.0 - ADAM_B2 ** ADAM_STEP)
    return -ADAM_LR * (m_hat / (jnp.sqrt(v_hat) + ADAM_EPS) + ADAM_WD * w), m, v


def _reduce_adam(parts, w, m, v, *, name):
    nl, r, c = w.shape
    rb = 256 if r % 256 == 0 else r

    def body(*refs):
        p_refs = refs[:nl]
        w_ref, m_ref, v_ref, g_out, d_out, m_out, v_out = refs[nl:]
        layer = pl.program_id(0)
        for li in range(nl):
            @pl.when(layer == li)
            def _(li=li):
                g = p_refs[li][0]
                for dev in range(1, N_DEV):
                    g = g + p_refs[li][dev]
                g_out[...] = g
                d_out[...], m_out[...], v_out[...] = _adamw_math(w_ref[...], g, m_ref[...], v_ref[...])

    blk = pl.BlockSpec((None, rb, c), lambda l, i: (l, i, 0))
    p_specs = [pl.BlockSpec((N_DEV, rb, c), lambda l, i, li=li: (0, jnp.where(l == li, i, 0), 0)) for li in range(nl)]
    return pl.pallas_call(
        body, grid=(nl, r // rb), in_specs=p_specs + [blk, blk, blk], out_specs=[blk] * 4,
        out_shape=[_sds(w.shape, F32)] * 4, compiler_params=_cparams("arbitrary", "arbitrary"), name=name)(*parts, w, m, v)


def _reduce8(parts, *, name):
    _, r, c = parts.shape

    def body(p_ref, o_ref):
        g = p_ref[0]
        for dev in range(1, N_DEV):
            g = g + p_ref[dev]
        o_ref[...] = g

    return pl.pallas_call(body, out_shape=_sds((r, c), F32), name=name)(parts)


def _adamw_small(w, g, m, v, *, name):
    def body(w_ref, g_ref, m_ref, v_ref, d_out, m_out, v_out):
        d_out[...], m_out[...], v_out[...] = _adamw_math(w_ref[...], g_ref[...], m_ref[...], v_ref[...])

    return pl.pallas_call(body, out_shape=[_sds(w.shape, F32)] * 3, name=name)(w, g, m, v)


def _exchange(gathers, scatters, *, name):
    arrs = list(gathers) + list(scatters)
    ng, n = len(gathers), len(arrs)
    npeer = N_DEV - 1
    out_shape = [_sds((N_DEV,) + a.shape, a.dtype) for a in gathers] + [_sds(a.shape, a.dtype) for a in scatters]

    def body(*refs):
        ins, outs = refs[:n], refs[n:2 * n]
        send, recv, local = refs[2 * n:]
        x, y, c = lax.axis_index("x"), lax.axis_index("y"), lax.axis_index("c")
        me = 4 * x + 2 * y + c

        def peer(d):
            px, py, pc = x ^ ((d >> 2) & 1), y ^ ((d >> 1) & 1), c ^ (d & 1)
            return (px, py, pc), 4 * px + 2 * py + pc

        def src(a, slot):
            return ins[a] if a < ng else ins[a].at[slot]

        def remote(a, d, landing):
            dev, slot = peer(d)
            sem = a * npeer + d - 1
            return pltpu.make_async_remote_copy(
                src_ref=src(a, slot), dst_ref=outs[a].at[slot if landing else me], send_sem=send.at[sem],
                recv_sem=recv.at[sem], device_id=dev, device_id_type=pl.DeviceIdType.MESH)

        own = [pltpu.make_async_copy(src(a, me), outs[a].at[me], local.at[a]) for a in range(n)]
        for a in range(n):
            own[a].start()
            for d in range(1, N_DEV):
                remote(a, d, False).start()
        for a in range(n):
            for d in range(1, N_DEV):
                remote(a, d, True).wait_recv()
                remote(a, d, False).wait_send()
            own[a].wait()

    hbm = pl.BlockSpec(memory_space=pl.ANY)
    return pl.pallas_call(
        body, in_specs=[hbm] * n, out_specs=[hbm] * n, out_shape=out_shape,
        scratch_shapes=[pltpu.SemaphoreType.DMA((n * npeer,)), pltpu.SemaphoreType.DMA((n * npeer,)),
                        pltpu.SemaphoreType.DMA((n,))],
        name=name)(*arrs)


def _pack(arrays):
    flat = jnp.concatenate([a.reshape(-1) for a in arrays])
    size = flat.shape[0]
    padded = -(-size // (8 * LANE)) * (8 * LANE)
    return jnp.pad(flat, (0, padded - size)).reshape(padded // LANE, LANE)


def _unpack(packed, shapes):
    flat = packed.reshape(-1)
    out, pos = [], 0
    for shp in shapes:
        size = 1
        for s in shp:
            size *= s
        out.append(flat[pos:pos + size].reshape(shp))
        pos += size
    return out


def _undo_column_split(g):
    return jnp.transpose(g, (1, 0, 2)).reshape(g.shape[1], N_DEV * g.shape[2])


def _column_split(a):
    r, c = a.shape
    return jnp.transpose(a.reshape(r, N_DEV, c // N_DEV), (1, 0, 2))


def _local_step(x, target, wts):
    d = x.shape[1]
    h0 = jnp.concatenate([jnp.zeros((PAD_ROWS, d), F32), wts["meta"], x], axis=0)
    tgt = jnp.concatenate([jnp.zeros((CHUNK, d), F32), target], axis=0)
    mix_g, ffn_g = wts["mix_g"], wts["ffn_g"]
    cp = (wts["conv_w"], wts["conv_b"], wts["ln_g"], wts["ln_b"], wts["pool_w"], wts["pool_scale"])
    gla = (wts["gate_w"], wts["gate_b"], wts["head_g"])

    z0, u0 = _linear_fwd(h0, wts["cp_w_in"], gain=mix_g[0:1], name="cp_in")
    cat = _cp_mid_fwd(z0, *cp, name="cp_mid")
    h1 = _linear_fwd(cat, wts["cp_w_out"], res=h0, name="cp_out")
    h2 = _ffn_fwd(h1, ffn_g[0:1], wts["w1"], wts["w2"], 0, name="ffn0")
    z1, u1 = _linear_fwd(h2, wts["gla_w_qkvg"], gain=mix_g[1:2], name="gla_in")
    r1 = _linear_fwd(u1, wts["gla_w_r"], name="gla_in_r")
    og, states = _gla_mid_fwd(z1, r1, *gla, name="gla_mid")
    h3 = _linear_fwd(og, wts["gla_w_out"], res=h2, name="gla_out")
    h4 = _ffn_fwd(h3, ffn_g[1:2], wts["w1"], wts["w2"], 1, name="ffn1")
    dh4, loss, d_final_g = _head(h4, wts["final_g"], tgt, name="head")

    dh3, dw1_1, dw2_1, dffn_g1 = _ffn_bwd(h3, dh4, ffn_g[1:2], wts["w1"], wts["w2"], 1, name="ffn1_bwd")
    dog = _linear_bwd_x([dh3], [wts["gla_w_out"]], name="gla_out_dx")
    d_gla_w_out = _linear_bwd_w(og, dh3, name="gla_out_dw")
    dz1, dr1, d_gate_w, d_gate_b, d_head_g = _gla_mid_bwd(z1, r1, dog, states, *gla, name="gla_mid_bwd")
    dh2, dmix_g1 = _linear_bwd_x([dz1, dr1], [wts["gla_w_qkvg"], wts["gla_w_r"]],
                                 norm=(h2, mix_g[1:2], dh3), name="gla_in_dx")
    d_gla_w_qkvg = _linear_bwd_w(u1, dz1, name="gla_in_dw")
    d_gla_w_r = _linear_bwd_w(u1, dr1, name="gla_in_r_dw")
    dh1, dw1_0, dw2_0, dffn_g0 = _ffn_bwd(h1, dh2, ffn_g[0:1], wts["w1"], wts["w2"], 0, name="ffn0_bwd")
    dcat = _linear_bwd_x([dh1], [wts["cp_w_out"]], name="cp_out_dx")
    d_cp_w_out = _linear_bwd_w(cat, dh1, name="cp_out_dw")
    dz0, d_conv_w, d_conv_b, d_ln_g, d_ln_b, d_pool_w, d_pool_scale = _cp_mid_bwd(z0, dcat, *cp, name="cp_mid_bwd")
    dh0, dmix_g0 = _linear_bwd_x([dz0], [wts["cp_w_in"]], norm=(h0, mix_g[0:1], dh1), name="cp_in_dx")
    d_cp_w_in = _linear_bwd_w(u0, dz0, name="cp_in_dw")

    grads = dict(
        meta=dh0[PAD_ROWS:CHUNK], mix_g=jnp.concatenate([dmix_g0, dmix_g1]), ffn_g=jnp.concatenate([dffn_g0, dffn_g1]),
        w1=(dw1_0, dw1_1), w2=(dw2_0, dw2_1), cp_w_in=d_cp_w_in, conv_w=d_conv_w, conv_b=d_conv_b, ln_g=d_ln_g,
        ln_b=d_ln_b, pool_w=d_pool_w, pool_scale=d_pool_scale, cp_w_out=d_cp_w_out, gla_w_qkvg=d_gla_w_qkvg,
        gla_w_r=d_gla_w_r, gate_w=d_gate_w, gate_b=d_gate_b, head_g=d_head_g, gla_w_out=d_gla_w_out, final_g=d_final_g)
    return loss, dh0[CHUNK:], grads


_REPLICATED = ("mix_norm_g", "ffn_norm_g", "cp_conv_b", "cp_ln_g", "cp_ln_b", "cp_pool_w", "cp_pool_scale", "final_norm_g")
_SMALL_SHARDED = ("meta_tokens", "cp_conv_w", "gla_gate_w2", "gla_gate_b", "gla_head_g")
_LARGE = ("ffn_w1", "ffn_w2", "cp_w_in", "cp_w_out", "gla_w_in", "gla_w_out")
_NAMES = ("meta_tokens", "mix_norm_g", "ffn_norm_g", "ffn_w1", "ffn_w2", "cp_w_in", "cp_conv_w", "cp_conv_b", "cp_ln_g",
          "cp_ln_b", "cp_pool_w", "cp_pool_scale", "cp_w_out", "gla_w_in", "gla_gate_w2", "gla_gate_b", "gla_head_g",
          "gla_w_out", "final_norm_g")


def kernel(x, meta_tokens, mix_norm_g, ffn_norm_g, ffn_w1, ffn_w2, cp_w_in, cp_conv_w, cp_conv_b, cp_ln_g, cp_ln_b, cp_pool_w, cp_pool_scale, cp_w_out, gla_w_in, gla_gate_w2, gla_gate_b, gla_head_g, gla_w_out, final_norm_g, loss_target, m_meta_tokens, m_mix_norm_g, m_ffn_norm_g, m_ffn_w1, m_ffn_w2, m_cp_w_in, m_cp_conv_w, m_cp_conv_b, m_cp_ln_g, m_cp_ln_b, m_cp_pool_w, m_cp_pool_scale, m_cp_w_out, m_gla_w_in, m_gla_gate_w2, m_gla_gate_b, m_gla_head_g, m_gla_w_out, m_final_norm_g, v_meta_tokens, v_mix_norm_g, v_ffn_norm_g, v_ffn_w1, v_ffn_w2, v_cp_w_in, v_cp_conv_w, v_cp_conv_b, v_cp_ln_g, v_cp_ln_b, v_cp_pool_w, v_cp_pool_scale, v_cp_w_out, v_gla_w_in, v_gla_gate_w2, v_gla_gate_b, v_gla_head_g, v_gla_w_out, v_final_norm_g):
    w = dict(meta_tokens=meta_tokens, mix_norm_g=mix_norm_g, ffn_norm_g=ffn_norm_g, ffn_w1=ffn_w1, ffn_w2=ffn_w2,
             cp_w_in=cp_w_in, cp_conv_w=cp_conv_w, cp_conv_b=cp_conv_b, cp_ln_g=cp_ln_g, cp_ln_b=cp_ln_b,
             cp_pool_w=cp_pool_w, cp_pool_scale=cp_pool_scale, cp_w_out=cp_w_out, gla_w_in=gla_w_in,
             gla_gate_w2=gla_gate_w2, gla_gate_b=gla_gate_b, gla_head_g=gla_head_g, gla_w_out=gla_w_out,
             final_norm_g=final_norm_g.reshape(1, -1))
    mom = dict(meta_tokens=m_meta_tokens, mix_norm_g=m_mix_norm_g, ffn_norm_g=m_ffn_norm_g, ffn_w1=m_ffn_w1, ffn_w2=m_ffn_w2,
               cp_w_in=m_cp_w_in, cp_conv_w=m_cp_conv_w, cp_conv_b=m_cp_conv_b, cp_ln_g=m_cp_ln_g, cp_ln_b=m_cp_ln_b,
               cp_pool_w=m_cp_pool_w, cp_pool_scale=m_cp_pool_scale, cp_w_out=m_cp_w_out, gla_w_in=m_gla_w_in,
               gla_gate_w2=m_gla_gate_w2, gla_gate_b=m_gla_gate_b, gla_head_g=m_gla_head_g, gla_w_out=m_gla_w_out,
               final_norm_g=m_final_norm_g.reshape(1, -1))
    var = dict(meta_tokens=v_meta_tokens, mix_norm_g=v_mix_norm_g, ffn_norm_g=v_ffn_norm_g, ffn_w1=v_ffn_w1, ffn_w2=v_ffn_w2,
               cp_w_in=v_cp_w_in, cp_conv_w=v_cp_conv_w, cp_conv_b=v_cp_conv_b, cp_ln_g=v_cp_ln_g, cp_ln_b=v_cp_ln_b,
               cp_pool_w=v_cp_pool_w, cp_pool_scale=v_cp_pool_scale, cp_w_out=v_cp_w_out, gla_w_in=v_gla_w_in,
               gla_gate_w2=v_gla_gate_w2, gla_gate_b=v_gla_gate_b, gla_head_g=v_gla_head_g, gla_w_out=v_gla_w_out,
               final_norm_g=v_final_norm_g.reshape(1, -1))
    d = x.shape[-1]
    cd = d // 2
    dk = d // 2
    hv = d // HEADS
    qkvg = 2 * dk + 2 * d
    me = 4 * lax.axis_index("x") + 2 * lax.axis_index("y") + lax.axis_index("c")

    small_shards = [w[n] for n in _SMALL_SHARDED]
    gathered = _exchange(
        [w["ffn_w1"].astype(BF16), w["ffn_w2"].astype(BF16), w["cp_w_in"][0].astype(BF16), w["cp_w_out"][0].astype(BF16),
         w["gla_w_in"][0].astype(BF16), w["gla_w_out"][0].astype(BF16), _pack(small_shards)], [], name="gather_weights")
    w1_all, w2_all, cp_in_g, cp_out_g, gla_in_g, gla_out_g, small_g = gathered
    small_parts = [_unpack(small_g[dev], [s.shape for s in small_shards]) for dev in range(N_DEV)]
    meta_full, conv_w_full, gate_w_full, gate_b_full, head_g_full = [
        jnp.concatenate([small_parts[dev][k] for dev in range(N_DEV)], axis=-1) for k in range(len(small_shards))]
    gla_in_full = _undo_column_split(gla_in_g)
    wts = dict(
        meta=meta_full, mix_g=w["mix_norm_g"], ffn_g=w["ffn_norm_g"], w1=w1_all, w2=w2_all,
        cp_w_in=_undo_column_split(cp_in_g), cp_w_out=cp_out_g.reshape(d, d),
        conv_w=jnp.pad(conv_w_full[0], ((0, 1), (0, 0))), conv_b=w["cp_conv_b"], ln_g=w["cp_ln_g"], ln_b=w["cp_ln_b"],
        pool_w=w["cp_pool_w"][0].astype(BF16), pool_scale=w["cp_pool_scale"],
        gla_w_qkvg=gla_in_full[:, :qkvg], gla_w_r=jnp.pad(gla_in_full[:, qkvg:], ((0, 0), (0, GATE_PAD - GATE_RANK))),
        gate_w=jnp.pad(gate_w_full[0], ((0, GATE_PAD - GATE_RANK), (0, 0))).astype(BF16), gate_b=gate_b_full,
        head_g=head_g_full, gla_w_out=gla_out_g.reshape(d, d), final_g=w["final_norm_g"])

    loss_blk, grad_x, g = _local_step(x[0], loss_target[0], wts)
    loss = lax.psum(loss_blk[0, 0], ("x", "y", "c"))

    d_gla_in = jnp.concatenate([g["gla_w_qkvg"], g["gla_w_r"][:, :GATE_RANK]], axis=1)
    small_grads = [g["mix_g"], g["ffn_g"], g["conv_b"], g["ln_g"], g["ln_b"], g["pool_w"][None], g["pool_scale"],
                   g["final_g"], g["meta"], g["conv_w"][None, :CONV_WIDTH], g["gate_w"][None, :GATE_RANK], g["gate_b"],
                   g["head_g"]]
    small_names = _REPLICATED + _SMALL_SHARDED
    recv = _exchange(
        [_pack(small_grads)],
        [g["w1"][0], g["w1"][1], g["w2"][0], g["w2"][1], _column_split(g["cp_w_in"]),
         g["cp_w_out"].reshape(N_DEV, d // N_DEV, d), _column_split(d_gla_in), g["gla_w_out"].reshape(N_DEV, d // N_DEV, d)],
        name="exchange_grads")
    small_all, p_w1_0, p_w1_1, p_w2_0, p_w2_1, p_cp_in, p_cp_out, p_gla_in, p_gla_out = recv

    out = {}
    out["ffn_w1"] = _reduce_adam([p_w1_0, p_w1_1], w["ffn_w1"], mom["ffn_w1"], var["ffn_w1"], name="adam_ffn_w1")
    out["ffn_w2"] = _reduce_adam([p_w2_0, p_w2_1], w["ffn_w2"], mom["ffn_w2"], var["ffn_w2"], name="adam_ffn_w2")
    out["cp_w_in"] = _reduce_adam([p_cp_in], w["cp_w_in"], mom["cp_w_in"], var["cp_w_in"], name="adam_cp_w_in")
    out["cp_w_out"] = _reduce_adam([p_cp_out], w["cp_w_out"], mom["cp_w_out"], var["cp_w_out"], name="adam_cp_w_out")
    out["gla_w_in"] = _reduce_adam([p_gla_in], w["gla_w_in"], mom["gla_w_in"], var["gla_w_in"], name="adam_gla_w_in")
    out["gla_w_out"] = _reduce_adam([p_gla_out], w["gla_w_out"], mom["gla_w_out"], var["gla_w_out"], name="adam_gla_w_out")

    small_sum = _unpack(_reduce8(small_all, name="sum_small_grads"), [a.shape for a in small_grads])
    small_grad = {}
    for n, full in zip(small_names, small_sum):
        full = full.reshape(w[n].shape[:-1] + (full.shape[-1],))
        if n in _SMALL_SHARDED:
            width = w[n].shape[-1]
            full = lax.dynamic_slice_in_dim(full, me * width, width, axis=full.ndim - 1)
        small_grad[n] = full
    packed = [_pack([t[n] for n in small_names]) for t in (w, small_grad, mom, var)]
    small_out = [_unpack(p, [w[n].shape for n in small_names]) for p in _adamw_small(*packed, name="adam_small")]
    for k, n in enumerate(small_names):
        out[n] = (small_grad[n], small_out[0][k], small_out[1][k], small_out[2][k])

    def leaf(n, k):
        a = out[n][k]
        return a.reshape(-1) if n == "final_norm_g" else a

    return (loss, grad_x[None], *[leaf(n, 0) for n in _NAMES], *[leaf(n, 1) for n in _NAMES],
            *[leaf(n, 2) for n in _NAMES], *[leaf(n, 3) for n in _NAMES])
```

```python
import functools

import jax
import jax.numpy as jnp
from jax import lax
from jax.experimental import pallas as pl
from jax.experimental.pallas import tpu as pltpu

F32, BF16 = jnp.float32, jnp.bfloat16
N_DEV = 8
CHUNK = 64
N_META = 16
PAD_ROWS = CHUNK - N_META
HALO = 32
EPS = 1e-5
CONV_WIDTH = 31
POOL_WINDOWS = (2, 4, 8, 16)
HEADS = 4
GATE_RANK = 16
GATE_NORM = 16.0
GATE_PAD = 128
ADAM_LR, ADAM_B1, ADAM_B2, ADAM_EPS, ADAM_WD, ADAM_STEP = 0.001, 0.9, 0.999, 1e-08, 0.01, 10
V7X_VMEM_LIMIT = 56 * 2 ** 20
LANE = 128
HI = lax.Precision.HIGHEST


def _cparams(*sem):
    return pltpu.CompilerParams(dimension_semantics=sem, vmem_limit_bytes=V7X_VMEM_LIMIT)


def _row_tile(t, cap):
    best = CHUNK
    for r in range(CHUNK, min(t, cap) + 1, CHUNK):
        if t % r == 0:
            best = r
    return best


def _resident(shape):
    return pl.BlockSpec(shape, lambda *_: (0,) * len(shape), pipeline_mode=pl.Buffered(1))


def _dot(a, b):
    return jnp.dot(a, b, preferred_element_type=F32)


def _dot_nt(a, b):
    return lax.dot_general(a, b, (((1,), (1,)), ((), ())), preferred_element_type=F32)


def _dot_tn(a, b):
    return lax.dot_general(a, b, (((0,), (0,)), ((), ())), preferred_element_type=F32)


def _rowsum(a):
    return jnp.sum(a, axis=0, keepdims=True)


def _sigmoid(a):
    return 1.0 / (1.0 + jnp.exp(-a))


def _row_ids(tile, rt):
    return tile * rt + lax.broadcasted_iota(jnp.int32, (rt, 1), 0)


def _sds(shape, dtype):
    return jax.ShapeDtypeStruct(shape, dtype)


def _linear_fwd(x, w, *, gain=None, res=None, out_dtype=F32, name):
    t, k = x.shape
    n = w.shape[1]
    rt = _row_tile(t, 320)

    def body(*refs):
        refs = list(refs)
        x_ref, w_ref = refs[:2]
        pos = 2
        g_ref = r_ref = u_ref = None
        if gain is not None:
            g_ref = refs[pos]
            pos += 1
        if res is not None:
            r_ref = refs[pos]
            pos += 1
        y_ref = refs[pos]
        if gain is not None:
            u_ref = refs[pos + 1]
            xv = x_ref[...]
            u = (xv * lax.rsqrt(jnp.mean(xv * xv, axis=-1, keepdims=True) + EPS) * g_ref[...]).astype(BF16)
            u_ref[...] = u
        else:
            u = x_ref[...]
        y = _dot(u, w_ref[...])
        if res is not None:
            y = y + r_ref[...]
        y_ref[...] = y.astype(y_ref.dtype)

    rows = lambda i: (i, 0)
    in_specs = [pl.BlockSpec((rt, k), rows), _resident((k, n))]
    args = [x, w]
    if gain is not None:
        in_specs.append(_resident((1, k)))
        args.append(gain)
    if res is not None:
        in_specs.append(pl.BlockSpec((rt, n), rows))
        args.append(res)
    out_shape = [_sds((t, n), out_dtype)]
    out_specs = [pl.BlockSpec((rt, n), rows)]
    if gain is not None:
        out_shape.append(_sds((t, k), BF16))
        out_specs.append(pl.BlockSpec((rt, k), rows))
    out = pl.pallas_call(body, grid=(t // rt,), in_specs=in_specs, out_specs=out_specs, out_shape=out_shape,
                         compiler_params=_cparams("parallel"), name=name)(*args)
    return out if gain is not None else out[0]


def _linear_bwd_x(dys, ws, *, norm=None, name):
    t = dys[0].shape[0]
    k = ws[0].shape[0]
    rt = _row_tile(t, 320)
    nd = len(dys)

    def body(*refs):
        refs = list(refs)
        dy_refs, w_refs = refs[:nd], refs[nd:2 * nd]
        dx = None
        for dy_ref, w_ref in zip(dy_refs, w_refs):
            part = _dot_nt(dy_ref[...].astype(BF16), w_ref[...])
            dx = part if dx is None else dx + part
        if norm is None:
            refs[2 * nd][...] = dx
            return
        h_ref, g_ref, dres_ref, dh_ref, dg_ref = refs[2 * nd:]
        hv = h_ref[...]
        rstd = lax.rsqrt(jnp.mean(hv * hv, axis=-1, keepdims=True) + EPS)
        xh = hv * rstd

        @pl.when(pl.program_id(0) == 0)
        def _():
            dg_ref[...] = jnp.zeros_like(dg_ref)

        dg_ref[...] += _rowsum(dx * xh)
        dxh = dx * g_ref[...]
        dh_ref[...] = dres_ref[...] + rstd * (dxh - xh * jnp.mean(dxh * xh, axis=-1, keepdims=True))

    rows = lambda i: (i, 0)
    in_specs = [pl.BlockSpec((rt, dy.shape[1]), rows) for dy in dys] + [_resident(w.shape) for w in ws]
    args = list(dys) + list(ws)
    out_shape = [_sds((t, k), F32)]
    out_specs = [pl.BlockSpec((rt, k), rows)]
    if norm is not None:
        h, gain, dres = norm
        in_specs += [pl.BlockSpec((rt, k), rows), _resident((1, k)), pl.BlockSpec((rt, k), rows)]
        args += [h, gain, dres]
        out_shape.append(_sds((1, k), F32))
        out_specs.append(pl.BlockSpec((1, k), lambda i: (0, 0)))
    out = pl.pallas_call(body, grid=(t // rt,), in_specs=in_specs, out_specs=out_specs, out_shape=out_shape,
                         compiler_params=_cparams("arbitrary"), name=name)(*args)
    return out if norm is not None else out[0]


def _linear_bwd_w(x, dy, *, name):
    t, k = x.shape
    n = dy.shape[1]
    rt = _row_tile(t, 832)
    nt = max(c for c in (512, 384, 256, LANE) if n % c == 0)

    def body(x_ref, dy_ref, o_ref):
        @pl.when(pl.program_id(1) == 0)
        def _():
            o_ref[...] = jnp.zeros_like(o_ref)

        o_ref[...] += _dot_tn(x_ref[...], dy_ref[...].astype(BF16))

    return pl.pallas_call(
        body, grid=(n // nt, t // rt),
        in_specs=[pl.BlockSpec((rt, k), lambda j, i: (i, 0)), pl.BlockSpec((rt, nt), lambda j, i: (i, j))],
        out_specs=pl.BlockSpec((k, nt), lambda j, i: (0, j)), out_shape=_sds((k, n), F32),
        compiler_params=_cparams("parallel", "arbitrary"), name=name)(x, dy)


def _ffn_fwd(h, gain, w1g, w2g, *, name):
    t, d = h.shape
    f8 = w1g.shape[-1]
    rt = _row_tile(t, 832)

    def body(h_ref, g_ref, w1_ref, w2_ref, o_ref, u_ref, acc_ref):
        j = pl.program_id(1)

        @pl.when(j == 0)
        def _():
            hv = h_ref[...]
            u_ref[...] = (hv * lax.rsqrt(jnp.mean(hv * hv, axis=-1, keepdims=True) + EPS) * g_ref[...]).astype(BF16)
            acc_ref[...] = jnp.zeros_like(acc_ref)

        a = jnp.maximum(_dot(u_ref[...], w1_ref[...]), 0.0)
        acc_ref[...] += _dot((a * a).astype(BF16), w2_ref[...])

        @pl.when(j == N_DEV - 1)
        def _():
            o_ref[...] = h_ref[...] + acc_ref[...]

    return pl.pallas_call(
        body, grid=(t // rt, N_DEV),
        in_specs=[pl.BlockSpec((rt, d), lambda i, j: (i, 0)), _resident((1, d)),
                  pl.BlockSpec((None, d, f8), lambda i, j: (j, 0, 0)),
                  pl.BlockSpec((None, f8, d), lambda i, j: (j, 0, 0))],
        out_specs=pl.BlockSpec((rt, d), lambda i, j: (i, 0)), out_shape=_sds((t, d), F32),
        scratch_shapes=[pltpu.VMEM((rt, d), BF16), pltpu.VMEM((rt, d), F32)],
        compiler_params=_cparams("parallel", "arbitrary"), name=name)(h, gain, w1g, w2g)


def _ffn_bwd(h, dout, gain, w1g, w2g, *, name):
    t, d = h.shape
    f8 = w1g.shape[-1]
    rt = _row_tile(t, 320)
    nrow = t // rt
    last = N_DEV - 1

    def body(h_ref, do_ref, g_ref, w1_ref, w2_ref, dh_ref, dw1_ref, dw2_ref, dg_ref, du_ref):
        j, i = pl.program_id(0), pl.program_id(1)
        hv = h_ref[...]
        rstd = lax.rsqrt(jnp.mean(hv * hv, axis=-1, keepdims=True) + EPS)
        xh = hv * rstd
        u = (xh * g_ref[...]).astype(BF16)
        db = do_ref[...].astype(BF16)
        r = jnp.maximum(_dot(u, w1_ref[...]), 0.0)
        act = (r * r).astype(BF16)
        dhh = (_dot_nt(db, w2_ref[...]) * (2.0 * r)).astype(BF16)

        @pl.when(i == 0)
        def _():
            dw1_ref[...] = jnp.zeros_like(dw1_ref)
            dw2_ref[...] = jnp.zeros_like(dw2_ref)

        dw1_ref[...] += _dot_tn(u, dhh)
        dw2_ref[...] += _dot_tn(act, db)
        du_part = _dot_nt(dhh, w1_ref[...])
        rows = pl.ds(pl.multiple_of(i * rt, CHUNK), rt)

        @pl.when(j == 0)
        def _():
            du_ref[rows, :] = du_part

        @pl.when(j > 0)
        def _():
            du_ref[rows, :] += du_part

        @pl.when(j == last)
        def _():
            @pl.when(i == 0)
            def _():
                dg_ref[...] = jnp.zeros_like(dg_ref)

            du = du_ref[rows, :]
            dg_ref[...] += _rowsum(du * xh)
            dxh = du * g_ref[...]
            dh_ref[...] = do_ref[...] + rstd * (dxh - xh * jnp.mean(dxh * xh, axis=-1, keepdims=True))

    return pl.pallas_call(
        body, grid=(N_DEV, nrow),
        in_specs=[pl.BlockSpec((rt, d), lambda j, i: (i, 0)), pl.BlockSpec((rt, d), lambda j, i: (i, 0)),
                  _resident((1, d)),
                  pl.BlockSpec((None, d, f8), lambda j, i: (j, 0, 0)),
                  pl.BlockSpec((None, f8, d), lambda j, i: (j, 0, 0))],
        out_specs=[pl.BlockSpec((rt, d), lambda j, i: (jnp.where(j == last, i, 0), 0)),
                   pl.BlockSpec((None, d, f8), lambda j, i: (j, 0, 0)),
                   pl.BlockSpec((None, f8, d), lambda j, i: (j, 0, 0)),
                   pl.BlockSpec((1, d), lambda j, i: (0, 0))],
        out_shape=[_sds((t, d), F32), _sds((N_DEV, d, f8), F32), _sds((N_DEV, f8, d), F32), _sds((1, d), F32)],
        scratch_shapes=[pltpu.VMEM((t, d), F32)],
        compiler_params=_cparams("arbitrary", "arbitrary"), name=name)(h, dout, gain, w1g, w2g)


def _lane_blocks(width):
    lb = min(LANE, width)
    return [slice(s, s + lb) for s in range(0, width, lb)]


def _conv_rows(src_ref, w_ref, offset, dst_ref, nblk, width, bias_ref=None):
    def blk(rb, carry):
        base = pl.multiple_of(rb * CHUNK, CHUNK)
        for l, ls in enumerate(_lane_blocks(width)):
            acc = jnp.zeros((CHUNK, ls.stop - ls.start), F32)
            if bias_ref is not None:
                acc = acc + bias_ref[:, ls]
            for k in range(CONV_WIDTH):
                acc = acc + w_ref[k:k + 1, ls] * src_ref[l, pl.ds(base + offset(k), CHUNK), :]
            dst_ref[l, pl.ds(base, CHUNK), :] = acc
        return carry

    lax.fori_loop(0, nblk, blk, 0)


def _to_lane_blocks(ref, row0, value):
    for l, ls in enumerate(_lane_blocks(value.shape[1])):
        ref[l, row0:row0 + value.shape[0], :] = value[:, ls]


def _from_lane_blocks(ref):
    return jnp.concatenate([ref[l] for l in range(ref.shape[0])], axis=1)


def _pool_counts(rows, window):
    return jnp.clip(rows - PAD_ROWS + 1, 1, window).astype(F32)


def _trailing_sum(v, window):
    s, sh = v, 1
    while sh < window:
        s = s + pltpu.roll(s, sh, 0)
        sh *= 2
    return s


def _leading_sum(v, window):
    s, sh, n = v, 1, v.shape[0]
    while sh < window:
        s = s + pltpu.roll(s, n - sh, 0)
        sh *= 2
    return s


def _cp_mid_fwd(z, conv_w, conv_b, ln_g, ln_b, pool_w, pool_scale, *, name):
    t, ein = z.shape
    cd = conv_b.shape[1]
    pd = pool_scale.shape[1]
    pg = pd // len(POOL_WINDOWS)
    rt = _row_tile(t, 320)

    def body(z_ref, cw_ref, cb_ref, lg_ref, lb_ref, pw_ref, ps_ref, o_ref, gext, pext, conv_s):
        i = pl.program_id(0)

        @pl.when(i == 0)
        def _():
            _to_lane_blocks(gext, 0, jnp.zeros((HALO, cd), F32))
            pext[0:HALO, :] = jnp.zeros((HALO, pd), F32)

        _to_lane_blocks(gext, HALO, z_ref[:, 0:cd] * _sigmoid(z_ref[:, cd:2 * cd]))
        pext[HALO:HALO + rt, :] = z_ref[:, 2 * cd:]
        _conv_rows(gext, cw_ref, lambda k: k + HALO - (CONV_WIDTH - 1), conv_s, rt // CHUNK, cd, cb_ref)
        cv = _from_lane_blocks(conv_s)
        xc = cv - jnp.mean(cv, axis=-1, keepdims=True)
        y = xc * lax.rsqrt(jnp.mean(xc * xc, axis=-1, keepdims=True) + EPS) * lg_ref[...] + lb_ref[...]
        rows = _row_ids(i, rt)
        a = jnp.where(rows >= PAD_ROWS, y * _sigmoid(y), 0.0)
        o_ref[:, 0:cd] = a.astype(BF16)
        for gi, window in enumerate(POOL_WINDOWS):
            ls = slice(gi * pg, (gi + 1) * pg)
            v = pext[:, ls]
            tm = _trailing_sum(v, window)[HALO:] / _pool_counts(rows, window) - v[HALO:]
            p = _dot(tm.astype(BF16), pw_ref[gi]) * ps_ref[:, ls]
            o_ref[:, cd + gi * pg:cd + (gi + 1) * pg] = p.astype(BF16)
        gext[:, 0:HALO, :] = gext[:, rt:rt + HALO, :]
        pext[0:HALO, :] = pext[rt:rt + HALO, :]

    nl, lb = len(_lane_blocks(cd)), min(LANE, cd)
    return pl.pallas_call(
        body, grid=(t // rt,),
        in_specs=[pl.BlockSpec((rt, ein), lambda i: (i, 0)), _resident(conv_w.shape), _resident((1, cd)),
                  _resident((1, cd)), _resident((1, cd)), _resident(pool_w.shape), _resident((1, pd))],
        out_specs=pl.BlockSpec((rt, cd + pd), lambda i: (i, 0)), out_shape=_sds((t, cd + pd), BF16),
        scratch_shapes=[pltpu.VMEM((nl, rt + HALO, lb), F32), pltpu.VMEM((rt + HALO, pd), F32),
                        pltpu.VMEM((nl, rt, lb), F32)],
        compiler_params=_cparams("arbitrary"), name=name)(z, conv_w, conv_b, ln_g, ln_b, pool_w, pool_scale)


def _cp_mid_bwd(z, dcat, conv_w, conv_b, ln_g, ln_b, pool_w, pool_scale, *, name):
    t, ein = z.shape
    cd = conv_b.shape[1]
    pd = pool_scale.shape[1]
    pg = pd // len(POOL_WINDOWS)
    rt = _row_tile(t, 320)
    ntile = t // rt
    per = rt // CHUNK

    def body(z_ref, zh_ref, dc_ref, cw_ref, cb_ref, lg_ref, lb_ref, pw_ref, ps_ref,
             dz_ref, dcw_ref, dcb_ref, dlg_ref, dlb_ref, dpw_ref, dps_ref, gext, pext, conv_s, dcv, dsp):
        step = pl.program_id(0)
        tile = ntile - 1 - step

        @pl.when(step == 0)
        def _():
            for ref in (dcw_ref, dcb_ref, dlg_ref, dlb_ref, dpw_ref, dps_ref):
                ref[...] = jnp.zeros_like(ref)
            _to_lane_blocks(dcv, rt, jnp.zeros((HALO, cd), F32))
            dsp[rt:rt + HALO, :] = jnp.zeros((HALO, pd), F32)

        keep = jnp.where(tile > 0, 1.0, 0.0)
        zh = zh_ref[CHUNK - HALO:CHUNK, :]
        _to_lane_blocks(gext, 0, keep * zh[:, 0:cd] * _sigmoid(zh[:, cd:2 * cd]))
        pext[0:HALO, :] = keep * zh[:, 2 * cd:]
        za = z_ref[:, 0:cd]
        sg = _sigmoid(z_ref[:, cd:2 * cd])
        _to_lane_blocks(gext, HALO, za * sg)
        pext[HALO:HALO + rt, :] = z_ref[:, 2 * cd:]
        _conv_rows(gext, cw_ref, lambda k: k + HALO - (CONV_WIDTH - 1), conv_s, per, cd, cb_ref)
        cv = _from_lane_blocks(conv_s)
        xc = cv - jnp.mean(cv, axis=-1, keepdims=True)
        rstd = lax.rsqrt(jnp.mean(xc * xc, axis=-1, keepdims=True) + EPS)
        xh = xc * rstd
        y = xh * lg_ref[...] + lb_ref[...]
        sy = _sigmoid(y)
        rows = _row_ids(tile, rt)
        da = jnp.where(rows >= PAD_ROWS, dc_ref[:, 0:cd], 0.0)
        dy = da * (sy * (1.0 + y * (1.0 - sy)))
        dlg_ref[...] += _rowsum(dy * xh)
        dlb_ref[...] += _rowsum(dy)
        dxh = dy * lg_ref[...]
        dconv = rstd * (dxh - jnp.mean(dxh, axis=-1, keepdims=True) - xh * jnp.mean(dxh * xh, axis=-1, keepdims=True))
        dcb_ref[...] += _rowsum(dconv)
        _to_lane_blocks(dcv, 0, dconv)
        for l, ls in enumerate(_lane_blocks(cd)):
            for k in range(CONV_WIDTH):
                def acc_rows(rb, acc, l=l, k=k):
                    base = pl.multiple_of(rb * CHUNK, CHUNK)
                    return acc + dcv[l, pl.ds(base, CHUNK), :] * gext[l, pl.ds(base + k + HALO - (CONV_WIDTH - 1), CHUNK), :]

                acc = lax.fori_loop(0, per, acc_rows, jnp.zeros((CHUNK, ls.stop - ls.start), F32))
                dcw_ref[k:k + 1, ls] += _rowsum(acc)
        _conv_rows(dcv, cw_ref, lambda k: CONV_WIDTH - 1 - k, conv_s, per, cd)
        dglu = _from_lane_blocks(conv_s)
        dz_ref[:, 0:cd] = (dglu * sg).astype(BF16)
        dz_ref[:, cd:2 * cd] = (dglu * za * sg * (1.0 - sg)).astype(BF16)
        dcv[:, rt:rt + HALO, :] = dcv[:, 0:HALO, :]
        for gi, window in enumerate(POOL_WINDOWS):
            ls = slice(gi * pg, (gi + 1) * pg)
            v = pext[:, ls]
            cnt = _pool_counts(rows, window)
            tm = (_trailing_sum(v, window)[HALO:] / cnt - v[HALO:]).astype(BF16)
            dp = dc_ref[:, cd + gi * pg:cd + (gi + 1) * pg]
            dps_ref[:, ls] += _rowsum(dp * _dot(tm, pw_ref[gi]))
            dpl = (dp * ps_ref[:, ls]).astype(BF16)
            dpw_ref[gi] += _dot_tn(tm, dpl)
            dtm = _dot_nt(dpl, pw_ref[gi])
            dsp[0:rt, ls] = dtm / cnt
            dpin = _leading_sum(dsp[:, ls], window)[0:rt] - dtm
            dz_ref[:, 2 * cd + gi * pg:2 * cd + (gi + 1) * pg] = dpin.astype(BF16)
        dsp[rt:rt + HALO, :] = dsp[0:HALO, :]

    back = lambda i: (ntile - 1 - i, 0)
    halo_idx = lambda i: (jnp.maximum((ntile - 1 - i) * per - 1, 0), 0)
    const2 = lambda i: (0, 0)
    nl, lb = len(_lane_blocks(cd)), min(LANE, cd)
    return pl.pallas_call(
        body, grid=(ntile,),
        in_specs=[pl.BlockSpec((rt, ein), back), pl.BlockSpec((CHUNK, ein), halo_idx), pl.BlockSpec((rt, cd + pd), back),
                  _resident(conv_w.shape), _resident((1, cd)), _resident((1, cd)), _resident((1, cd)),
                  _resident(pool_w.shape), _resident((1, pd))],
        out_specs=[pl.BlockSpec((rt, ein), back), pl.BlockSpec(conv_w.shape, const2), pl.BlockSpec((1, cd), const2),
                   pl.BlockSpec((1, cd), const2), pl.BlockSpec((1, cd), const2),
                   pl.BlockSpec(pool_w.shape, lambda i: (0, 0, 0)), pl.BlockSpec((1, pd), const2)],
        out_shape=[_sds((t, ein), BF16), _sds(conv_w.shape, F32), _sds((1, cd), F32), _sds((1, cd), F32),
                   _sds((1, cd), F32), _sds(pool_w.shape, F32), _sds((1, pd), F32)],
        scratch_shapes=[pltpu.VMEM((nl, rt + HALO, lb), F32), pltpu.VMEM((rt + HALO, pd), F32), pltpu.VMEM((nl, rt, lb), F32),
                        pltpu.VMEM((nl, rt + HALO, lb), F32), pltpu.VMEM((rt + HALO, pd), F32)],
        compiler_params=_cparams("arbitrary"), name=name)(z, z, dcat, conv_w, conv_b, ln_g, ln_b, pool_w, pool_scale)


def _log_decay(r_ref, gw_ref, gb_ref, rows):
    gp = _dot(r_ref[...].astype(BF16), gw_ref[...]) + gb_ref[...]
    log_sig = jnp.minimum(gp, 0.0) - jnp.log(1.0 + jnp.exp(-jnp.abs(gp)))
    return gp, jnp.where(rows >= PAD_ROWS, log_sig / GATE_NORM, 0.0)


def _tri(strict):
    r = lax.broadcasted_iota(jnp.int32, (CHUNK, CHUNK), 0)
    c = lax.broadcasted_iota(jnp.int32, (CHUNK, CHUNK), 1)
    return jnp.where(c < r if strict else c <= r, 1.0, 0.0).astype(F32)


def _gla_mid_fwd(z, r, gate_w, gate_b, head_g, *, name):
    t = z.shape[0]
    dk = gate_b.shape[1]
    hv = head_g.shape[1]
    hk = dk // HEADS
    dv = hv * HEADS
    rt = _row_tile(t, 320)
    per = rt // CHUNK
    scale = hk ** -0.5

    def body(z_ref, r_ref, gw_ref, gb_ref, hg_ref, o_ref, st_ref, s_ref, la_ref):
        i = pl.program_id(0)

        @pl.when(i == 0)
        def _():
            s_ref[...] = jnp.zeros_like(s_ref)

        _, la = _log_decay(r_ref, gw_ref, gb_ref, _row_ids(i, rt))
        la_ref[...] = la
        tri = _tri(False)

        def chunk(c, carry):
            rows = pl.ds(pl.multiple_of(c * CHUNK, CHUNK), CHUNK)
            la_c = la_ref[rows, :]
            cum = jnp.dot(tri, la_c, precision=HI, preferred_element_type=F32)
            tot = _rowsum(la_c)
            dec = jnp.exp(tot - cum)
            etot = jnp.exp(tot)
            for hd in range(HEADS):
                ks = slice(hd * hk, (hd + 1) * hk)
                q = z_ref[rows, hd * hk:(hd + 1) * hk] * scale
                kd = z_ref[rows, dk + hd * hk:dk + (hd + 1) * hk] * dec[:, ks]
                v = z_ref[rows, 2 * dk + hd * hv:2 * dk + (hd + 1) * hv]
                g = z_ref[rows, 2 * dk + dv + hd * hv:2 * dk + dv + (hd + 1) * hv]
                s_new = s_ref[hd] * etot[:, ks] + _dot_tn(v.astype(BF16), kd.astype(BF16))
                s_ref[hd] = s_new
                st_ref[c, hd] = s_new
                o = _dot_nt(q.astype(BF16), s_new.astype(BF16))
                on = o * lax.rsqrt(jnp.mean(o * o, axis=-1, keepdims=True) + EPS) * hg_ref[...]
                o_ref[rows, hd * hv:(hd + 1) * hv] = (on * (g * _sigmoid(g))).astype(BF16)
            return carry

        lax.fori_loop(0, per, chunk, 0)

    return pl.pallas_call(
        body, grid=(t // rt,),
        in_specs=[pl.BlockSpec((rt, z.shape[1]), lambda i: (i, 0)), pl.BlockSpec((rt, GATE_PAD), lambda i: (i, 0)),
                  _resident(gate_w.shape), _resident((1, dk)), _resident((1, hv))],
        out_specs=[pl.BlockSpec((rt, dv), lambda i: (i, 0)), pl.BlockSpec((per, HEADS, hv, hk), lambda i: (i, 0, 0, 0))],
        out_shape=[_sds((t, dv), BF16), _sds((t // CHUNK, HEADS, hv, hk), F32)],
        scratch_shapes=[pltpu.VMEM((HEADS, hv, hk), F32), pltpu.VMEM((rt, dk), F32)],
        compiler_params=_cparams("arbitrary"), name=name)(z, r, gate_w, gate_b, head_g)


def _gla_mid_bwd(z, r, dog, states, gate_w, gate_b, head_g, *, name):
    t = z.shape[0]
    dk = gate_b.shape[1]
    hv = head_g.shape[1]
    hk = dk // HEADS
    dv = hv * HEADS
    rt = _row_tile(t, 320)
    ntile = t // rt
    per = rt // CHUNK
    scale = hk ** -0.5

    def body(z_ref, r_ref, do_ref, st_ref, stp_ref, gw_ref, gb_ref, hg_ref,
             dz_ref, dr_ref, dgw_ref, dgb_ref, dhg_ref, ds_ref, la_ref, dla_ref):
        step = pl.program_id(0)
        tile = ntile - 1 - step

        @pl.when(step == 0)
        def _():
            ds_ref[...] = jnp.zeros_like(ds_ref)
            dgw_ref[...] = jnp.zeros_like(dgw_ref)
            dgb_ref[...] = jnp.zeros_like(dgb_ref)
            dhg_ref[...] = jnp.zeros_like(dhg_ref)

        rows_id = _row_ids(tile, rt)
        gp, la = _log_decay(r_ref, gw_ref, gb_ref, rows_id)
        la_ref[...] = la
        tri, tri_strict = _tri(False), _tri(True)
        keep = jnp.where(tile > 0, 1.0, 0.0)

        def chunk(cc, carry):
            c = per - 1 - cc
            rows = pl.ds(pl.multiple_of(c * CHUNK, CHUNK), CHUNK)
            la_c = la_ref[rows, :]
            cum = jnp.dot(tri, la_c, precision=HI, preferred_element_type=F32)
            tot = _rowsum(la_c)
            dec = jnp.exp(tot - cum)
            etot = jnp.exp(tot)
            inside = jnp.where(c > 0, 1.0, 0.0)
            for hd in range(HEADS):
                ks = slice(hd * hk, (hd + 1) * hk)
                q = (z_ref[rows, hd * hk:(hd + 1) * hk] * scale).astype(BF16)
                k = z_ref[rows, dk + hd * hk:dk + (hd + 1) * hk]
                kd = k * dec[:, ks]
                v = z_ref[rows, 2 * dk + hd * hv:2 * dk + (hd + 1) * hv].astype(BF16)
                g = z_ref[rows, 2 * dk + dv + hd * hv:2 * dk + dv + (hd + 1) * hv]
                s_now = st_ref[c, hd]
                s_prev = inside * st_ref[jnp.maximum(c - 1, 0), hd] + (1.0 - inside) * keep * stp_ref[0, hd]
                s_b = s_now.astype(BF16)
                o = _dot_nt(q, s_b)
                rstd = lax.rsqrt(jnp.mean(o * o, axis=-1, keepdims=True) + EPS)
                oh = o * rstd
                sg = _sigmoid(g)
                d_og = do_ref[rows, hd * hv:(hd + 1) * hv]
                dz_ref[rows, 2 * dk + dv + hd * hv:2 * dk + dv + (hd + 1) * hv] = (
                    d_og * oh * hg_ref[...] * (sg * (1.0 + g * (1.0 - sg)))).astype(BF16)
                don = d_og * (g * sg)
                dhg_ref[...] += _rowsum(don * oh)
                doh = don * hg_ref[...]
                d_o = (rstd * (doh - oh * jnp.mean(doh * oh, axis=-1, keepdims=True))).astype(BF16)
                dz_ref[rows, hd * hk:(hd + 1) * hk] = (_dot(d_o, s_b) * scale).astype(BF16)
                ds_t = ds_ref[hd] + _dot_tn(d_o, q)
                ds_b = ds_t.astype(BF16)
                dkd = _dot(v, ds_b)
                dz_ref[rows, 2 * dk + hd * hv:2 * dk + (hd + 1) * hv] = _dot_nt(kd.astype(BF16), ds_b).astype(BF16)
                dtot = etot[:, ks] * _rowsum(ds_t * s_prev)
                ds_ref[hd] = ds_t * etot[:, ks]
                dz_ref[rows, dk + hd * hk:dk + (hd + 1) * hk] = (dkd * dec[:, ks]).astype(BF16)
                e = dkd * kd
                dla_ref[rows, ks] = dtot + jnp.dot(tri_strict, e, precision=HI, preferred_element_type=F32)
            return carry

        lax.fori_loop(0, per, chunk, 0)
        dla = jnp.where(rows_id >= PAD_ROWS, dla_ref[...], 0.0)
        dgp = dla * (1.0 / GATE_NORM) * (1.0 - _sigmoid(gp))
        dgb_ref[...] += _rowsum(dgp)
        dgp_b = dgp.astype(BF16)
        dgw_ref[...] += _dot_tn(r_ref[...].astype(BF16), dgp_b)
        dr_ref[...] = _dot_nt(dgp_b, gw_ref[...]).astype(BF16)

    back = lambda i: (ntile - 1 - i, 0)
    const2 = lambda i: (0, 0)
    return pl.pallas_call(
        body, grid=(ntile,),
        in_specs=[pl.BlockSpec((rt, z.shape[1]), back), pl.BlockSpec((rt, GATE_PAD), back), pl.BlockSpec((rt, dv), back),
                  pl.BlockSpec((per, HEADS, hv, hk), lambda i: (ntile - 1 - i, 0, 0, 0)),
                  pl.BlockSpec((1, HEADS, hv, hk), lambda i: (jnp.maximum((ntile - 1 - i) * per - 1, 0), 0, 0, 0)),
                  _resident(gate_w.shape), _resident((1, dk)), _resident((1, hv))],
        out_specs=[pl.BlockSpec((rt, z.shape[1]), back), pl.BlockSpec((rt, GATE_PAD), back),
                   pl.BlockSpec(gate_w.shape, const2), pl.BlockSpec((1, dk), const2), pl.BlockSpec((1, hv), const2)],
        out_shape=[_sds(z.shape, BF16), _sds((t, GATE_PAD), BF16), _sds(gate_w.shape, F32), _sds((1, dk), F32),
                   _sds((1, hv), F32)],
        scratch_shapes=[pltpu.VMEM((HEADS, hv, hk), F32), pltpu.VMEM((rt, dk), F32), pltpu.VMEM((rt, dk), F32)],
        compiler_params=_cparams("arbitrary"), name=name)(z, r, dog, states, states, gate_w, gate_b, head_g)


def _head(h, gain, target, *, name):
    t, d = h.shape
    rt = _row_tile(t, 832)

    def body(h_ref, g_ref, t_ref, dh_ref, loss_ref, dg_ref):
        i = pl.program_id(0)

        @pl.when(i == 0)
        def _():
            loss_ref[...] = jnp.zeros_like(loss_ref)
            dg_ref[...] = jnp.zeros_like(dg_ref)

        hv = h_ref[...]
        rstd = lax.rsqrt(jnp.mean(hv * hv, axis=-1, keepdims=True) + EPS)
        xh = hv * rstd
        err = jnp.where(_row_ids(i, rt) >= CHUNK, xh * g_ref[...] - t_ref[...], 0.0)
        loss_ref[...] += (0.5 / d) * jnp.sum(err * err)
        dy = err * (1.0 / d)
        dg_ref[...] += _rowsum(dy * xh)
        dxh = dy * g_ref[...]
        dh_ref[...] = rstd * (dxh - xh * jnp.mean(dxh * xh, axis=-1, keepdims=True))

    return pl.pallas_call(
        body, grid=(t // rt,),
        in_specs=[pl.BlockSpec((rt, d), lambda i: (i, 0)), _resident((1, d)), pl.BlockSpec((rt, d), lambda i: (i, 0))],
        out_specs=[pl.BlockSpec((rt, d), lambda i: (i, 0)), pl.BlockSpec((8, LANE), lambda i: (0, 0)),
                   pl.BlockSpec((1, d), lambda i: (0, 0))],
        out_shape=[_sds((t, d), F32), _sds((8, LANE), F32), _sds((1, d), F32)],
        compiler_params=_cparams("arbitrary"), name=name)(h, gain, target)


def _adamw_math(w, g, m, v):
    m = ADAM_B1 * m + (1.0 - ADAM_B1) * g
    v = ADAM_B2 * v + (1.0 - ADAM_B2) * (g * g)
    m_hat = m / (1.0 - ADAM_B1 ** ADAM_STEP)
    v_hat = v / (1.0 - ADAM_B2 ** ADAM_STEP)
    return -ADAM_LR * (m_hat / (jnp.sqrt(v_hat) + ADAM_EPS) + ADAM_WD * w), m, v


def _reduce_adam(parts, w, m, v, *, name):
    nl, r, c = w.shape
    rb = 256 if r % 256 == 0 else r

    def body(*refs):
        p_refs = refs[:nl]
        w_ref, m_ref, v_ref, g_out, d_out, m_out, v_out = refs[nl:]
        layer = pl.program_id(0)
        for li in range(nl):
            @pl.when(layer == li)
            def _(li=li):
                g = p_refs[li][0]
                for dev in range(1, N_DEV):
                    g = g + p_refs[li][dev]
                g_out[...] = g
                d_out[...], m_out[...], v_out[...] = _adamw_math(w_ref[...], g, m_ref[...], v_ref[...])

    blk = pl.BlockSpec((None, rb, c), lambda l, i: (l, i, 0))
    p_specs = [pl.BlockSpec((N_DEV, rb, c), lambda l, i, li=li: (0, jnp.where(l == li, i, 0), 0)) for li in range(nl)]
    return pl.pallas_call(
        body, grid=(nl, r // rb), in_specs=p_specs + [blk, blk, blk], out_specs=[blk] * 4,
        out_shape=[_sds(w.shape, F32)] * 4, compiler_params=_cparams("arbitrary", "arbitrary"), name=name)(*parts, w, m, v)


def _reduce8(parts, *, name):
    _, r, c = parts.shape

    def body(p_ref, o_ref):
        g = p_ref[0]
        for dev in range(1, N_DEV):
            g = g + p_ref[dev]
        o_ref[...] = g

    return pl.pallas_call(body, out_shape=_sds((r, c), F32), name=name)(parts)


def _adamw_small(w, g, m, v, *, name):
    def body(w_ref, g_ref, m_ref, v_ref, d_out, m_out, v_out):
        d_out[...], m_out[...], v_out[...] = _adamw_math(w_ref[...], g_ref[...], m_ref[...], v_ref[...])

    return pl.pallas_call(body, out_shape=[_sds(w.shape, F32)] * 3, name=name)(w, g, m, v)


_NPEER = N_DEV - 1
_HBM = pl.BlockSpec(memory_space=pltpu.HBM)
_SEM = pl.BlockSpec(memory_space=pltpu.SEMAPHORE)
_DATAFLOW = pltpu.SideEffectType.DATAFLOW_SIDE_EFFECTING


def _my_slot():
    return 4 * lax.axis_index("x") + 2 * lax.axis_index("y") + lax.axis_index("c")


def _exchange_copy(ins, lands, send, recv, ng, a, d, landing):
    x, y, c = lax.axis_index("x"), lax.axis_index("y"), lax.axis_index("c")
    px, py, pc = x ^ ((d >> 2) & 1), y ^ ((d >> 1) & 1), c ^ (d & 1)
    slot = 4 * px + 2 * py + pc
    sem = a * _NPEER + d - 1
    return pltpu.make_async_remote_copy(
        src_ref=ins[a] if a < ng else ins[a].at[slot], dst_ref=lands[a].at[slot if landing else _my_slot()],
        send_sem=send.at[sem], recv_sem=recv.at[sem], device_id=(px, py, pc), device_id_type=pl.DeviceIdType.MESH)


def _exchange_place(gathers, scatters, *, name):
    arrs = list(gathers) + list(scatters)
    ng, n = len(gathers), len(arrs)

    def body(*refs):
        ins, outs, sems = refs[:n], refs[n:2 * n], refs[2 * n]
        me = _my_slot()
        own = [pltpu.make_async_copy(ins[a] if a < ng else ins[a].at[me], outs[a].at[me], sems.at[a]) for a in range(n)]
        for cp in own:
            cp.start()
        for cp in own:
            cp.wait()

    hbm = pl.BlockSpec(memory_space=pl.ANY)
    return pl.pallas_call(
        body, in_specs=[hbm] * n, out_specs=[hbm] * n,
        out_shape=[_sds((N_DEV,) + a.shape, a.dtype) for a in gathers] + [_sds(a.shape, a.dtype) for a in scatters],
        scratch_shapes=[pltpu.SemaphoreType.DMA((n,))], name=name)(*arrs)


def _exchange_start(gathers, scatters, lands, *, name):
    arrs = list(gathers) + list(scatters)
    ng, n = len(gathers), len(arrs)

    def body(*refs):
        ins, land_refs, send, recv = refs[:n], refs[n:2 * n], refs[2 * n], refs[2 * n + 1]
        token = refs[-1]
        for a in range(n):
            for d in range(1, N_DEV):
                _exchange_copy(ins, land_refs, send, recv, ng, a, d, False).start()
        token[...] = jnp.zeros_like(token)

    thru = [pltpu.HBM(a.shape, a.dtype) for a in arrs] + [pltpu.HBM(a.shape, a.dtype) for a in lands]
    out = pl.pallas_call(
        body, name=name,
        out_shape=(pltpu.SemaphoreType.DMA((n * _NPEER,)), pltpu.SemaphoreType.DMA((n * _NPEER,)), *thru, _sds((8, LANE), F32)),
        in_specs=[_HBM] * (2 * n), out_specs=(_SEM, _SEM, *([_HBM] * (2 * n)), pl.BlockSpec(memory_space=pltpu.VMEM)),
        input_output_aliases={i: 2 + i for i in range(2 * n)},
        compiler_params=pltpu.CompilerParams(has_side_effects=_DATAFLOW),
    )(*[pltpu.with_memory_space_constraint(a, pltpu.HBM) for a in arrs],
      *[pltpu.with_memory_space_constraint(a, pltpu.HBM) for a in lands])
    return (out[0], out[1], list(out[2:2 + n]), list(out[2 + n:2 + 2 * n]), ng), out[-1]


def _exchange_wait(state, after, *, name):
    send_sem, recv_sem, srcs, lands, ng = state
    n = len(srcs)

    def body(*refs):
        ins, land_refs, send, recv = refs[:n], refs[n:2 * n], refs[2 * n], refs[2 * n + 1]
        for a in range(n):
            for d in range(1, N_DEV):
                _exchange_copy(ins, land_refs, send, recv, ng, a, d, True).wait_recv()
                _exchange_copy(ins, land_refs, send, recv, ng, a, d, False).wait_send()

    out = pl.pallas_call(
        body, name=name, out_shape=[pltpu.HBM(a.shape, a.dtype) for a in srcs + lands],
        in_specs=[_HBM] * (2 * n) + [_SEM, _SEM, pl.BlockSpec(memory_space=pl.ANY)], out_specs=[_HBM] * (2 * n),
        input_output_aliases={i: i for i in range(2 * n)},
        compiler_params=pltpu.CompilerParams(has_side_effects=_DATAFLOW),
    )(*srcs, *lands, send_sem, recv_sem, after)
    return list(out[n:])


def _after(value, token):
    return value if token is None else lax.optimization_barrier((value, token))[0]


def _pack(arrays):
    flat = jnp.concatenate([a.reshape(-1) for a in arrays])
    size = flat.shape[0]
    padded = -(-size // (8 * LANE)) * (8 * LANE)
    return jnp.pad(flat, (0, padded - size)).reshape(padded // LANE, LANE)


def _unpack(packed, shapes):
    flat = packed.reshape(-1)
    out, pos = [], 0
    for shp in shapes:
        size = 1
        for s in shp:
            size *= s
        out.append(flat[pos:pos + size].reshape(shp))
        pos += size
    return out


def _undo_column_split(g):
    return jnp.transpose(g, (1, 0, 2)).reshape(g.shape[1], N_DEV * g.shape[2])


def _column_split(a):
    r, c = a.shape
    return jnp.transpose(a.reshape(r, N_DEV, c // N_DEV), (1, 0, 2))


class _WholeWeights:
    def __init__(self, groups):
        self.groups = groups
        self.grads = {}

    def fetch(self, group, after):
        return self.groups[group]

    def emit(self, group, grads):
        self.grads.update(grads)
        return None


def _local_step(x, target, replicated, src):
    d = x.shape[1]
    mix_g, ffn_g = replicated["mix_g"], replicated["ffn_g"]
    cp = src.fetch("cp", x)
    h0 = jnp.concatenate([jnp.zeros((PAD_ROWS, d), F32), cp["meta"], x], axis=0)
    tgt = jnp.concatenate([jnp.zeros((CHUNK, d), F32), target], axis=0)
    cp_mid = (cp["conv_w"], replicated["conv_b"], replicated["ln_g"], replicated["ln_b"], replicated["pool_w"],
              replicated["pool_scale"])

    z0, u0 = _linear_fwd(h0, cp["cp_w_in"], gain=mix_g[0:1], name="cp_in")
    cat = _cp_mid_fwd(z0, *cp_mid, name="cp_mid")
    h1 = _linear_fwd(cat, cp["cp_w_out"], res=h0, name="cp_out")
    ffn0 = src.fetch("ffn0", h1)
    h2 = _ffn_fwd(h1, ffn_g[0:1], ffn0["w1"], ffn0["w2"], name="ffn0")
    gla = src.fetch("gla", h2)
    gla_mid = (gla["gate_w"], gla["gate_b"], gla["head_g"])
    z1, u1 = _linear_fwd(h2, gla["gla_w_qkvg"], gain=mix_g[1:2], name="gla_in")
    r1 = _linear_fwd(u1, gla["gla_w_r"], name="gla_in_r")
    og, states = _gla_mid_fwd(z1, r1, *gla_mid, name="gla_mid")
    h3 = _linear_fwd(og, gla["gla_w_out"], res=h2, name="gla_out")
    ffn1 = src.fetch("ffn1", h3)
    h4 = _ffn_fwd(h3, ffn_g[1:2], ffn1["w1"], ffn1["w2"], name="ffn1")
    dh4, loss, d_final_g = _head(h4, replicated["final_g"], tgt, name="head")

    dh3, dw1_1, dw2_1, dffn_g1 = _ffn_bwd(h3, dh4, ffn_g[1:2], ffn1["w1"], ffn1["w2"], name="ffn1_bwd")
    dh3 = _after(dh3, src.emit("ffn1", dict(w1=dw1_1, w2=dw2_1)))
    dog = _linear_bwd_x([dh3], [gla["gla_w_out"]], name="gla_out_dx")
    d_gla_w_out = _linear_bwd_w(og, dh3, name="gla_out_dw")
    dz1, dr1, d_gate_w, d_gate_b, d_head_g = _gla_mid_bwd(z1, r1, dog, states, *gla_mid, name="gla_mid_bwd")
    dh2, dmix_g1 = _linear_bwd_x([dz1, dr1], [gla["gla_w_qkvg"], gla["gla_w_r"]],
                                 norm=(h2, mix_g[1:2], dh3), name="gla_in_dx")
    d_gla_w_qkvg = _linear_bwd_w(u1, dz1, name="gla_in_dw")
    d_gla_w_r = _linear_bwd_w(u1, dr1, name="gla_in_r_dw")
    dh2 = _after(dh2, src.emit("gla", dict(gla_w_qkvg=d_gla_w_qkvg, gla_w_r=d_gla_w_r, gla_w_out=d_gla_w_out)))
    dh1, dw1_0, dw2_0, dffn_g0 = _ffn_bwd(h1, dh2, ffn_g[0:1], ffn0["w1"], ffn0["w2"], name="ffn0_bwd")
    dh1 = _after(dh1, src.emit("ffn0", dict(w1=dw1_0, w2=dw2_0)))
    dcat = _linear_bwd_x([dh1], [cp["cp_w_out"]], name="cp_out_dx")
    d_cp_w_out = _linear_bwd_w(cat, dh1, name="cp_out_dw")
    dz0, d_conv_w, d_conv_b, d_ln_g, d_ln_b, d_pool_w, d_pool_scale = _cp_mid_bwd(z0, dcat, *cp_mid, name="cp_mid_bwd")
    dh0, dmix_g0 = _linear_bwd_x([dz0], [cp["cp_w_in"]], norm=(h0, mix_g[0:1], dh1), name="cp_in_dx")
    d_cp_w_in = _linear_bwd_w(u0, dz0, name="cp_in_dw")

    small = dict(
        mix_g=jnp.concatenate([dmix_g0, dmix_g1]), ffn_g=jnp.concatenate([dffn_g0, dffn_g1]), conv_b=d_conv_b, ln_g=d_ln_g,
        ln_b=d_ln_b, pool_w=d_pool_w, pool_scale=d_pool_scale, final_g=d_final_g, meta=dh0[PAD_ROWS:CHUNK], conv_w=d_conv_w,
        gate_w=d_gate_w, gate_b=d_gate_b, head_g=d_head_g)
    src.emit("cp", dict(cp_w_in=d_cp_w_in, cp_w_out=d_cp_w_out, small=small))
    return loss, dh0[CHUNK:], small


_REPLICATED = ("mix_norm_g", "ffn_norm_g", "cp_conv_b", "cp_ln_g", "cp_ln_b", "cp_pool_w", "cp_pool_scale", "final_norm_g")
_SMALL_SHARDED = ("meta_tokens", "cp_conv_w", "gla_gate_w2", "gla_gate_b", "gla_head_g")
_LARGE = ("ffn_w1", "ffn_w2", "cp_w_in", "cp_w_out", "gla_w_in", "gla_w_out")
_NAMES = ("meta_tokens", "mix_norm_g", "ffn_norm_g", "ffn_w1", "ffn_w2", "cp_w_in", "cp_conv_w", "cp_conv_b", "cp_ln_g",
          "cp_ln_b", "cp_pool_w", "cp_pool_scale", "cp_w_out", "gla_w_in", "gla_gate_w2", "gla_gate_b", "gla_head_g",
          "gla_w_out", "final_norm_g")
_SMALL_GRADS = ("mix_g", "ffn_g", "conv_b", "ln_g", "ln_b", "pool_w", "pool_scale", "final_g", "meta", "conv_w", "gate_w",
                "gate_b", "head_g")
_GROUPS = ("cp", "ffn0", "gla", "ffn1")


class _Exchanges:
    def __init__(self, w, d):
        self.d = d
        self.small_shards = [w[n] for n in _SMALL_SHARDED]
        shards = dict(
            cp=[w["cp_w_in"][0].astype(BF16), w["cp_w_out"][0].astype(BF16), _pack(self.small_shards)],
            ffn0=[w["ffn_w1"][0].astype(BF16), w["ffn_w2"][0].astype(BF16)],
            gla=[w["gla_w_in"][0].astype(BF16), w["gla_w_out"][0].astype(BF16)],
            ffn1=[w["ffn_w1"][1].astype(BF16), w["ffn_w2"][1].astype(BF16)])
        self.gathers, self.sent = {}, {}
        token = None
        for group in _GROUPS:
            arrs = [_after(a, token) for a in shards[group]]
            lands = _exchange_place(arrs, [], name=f"place_w_{group}")
            self.gathers[group], token = _exchange_start(arrs, [], lands, name=f"start_w_{group}")
        self.token = token

    def fetch(self, group, after):
        d = self.d
        got = _exchange_wait(self.gathers[group], _after(after, self.token), name=f"wait_w_{group}")
        if group in ("ffn0", "ffn1"):
            return dict(w1=got[0], w2=got[1])
        if group == "gla":
            qkvg = 3 * d
            w_in = _undo_column_split(got[0])
            return dict(gla_w_qkvg=w_in[:, :qkvg], gla_w_r=jnp.pad(w_in[:, qkvg:], ((0, 0), (0, GATE_PAD - GATE_RANK))),
                        gla_w_out=got[1].reshape(d, d), gate_w=self.gate_w, gate_b=self.gate_b, head_g=self.head_g)
        shapes = [s.shape for s in self.small_shards]
        parts = [_unpack(got[2][dev], shapes) for dev in range(N_DEV)]
        meta, conv_w, gate_w, self.gate_b, self.head_g = [
            jnp.concatenate([parts[dev][k] for dev in range(N_DEV)], axis=-1) for k in range(len(shapes))]
        self.gate_w = jnp.pad(gate_w[0], ((0, GATE_PAD - GATE_RANK), (0, 0))).astype(BF16)
        return dict(cp_w_in=_undo_column_split(got[0]), cp_w_out=got[1].reshape(d, d), meta=meta,
                    conv_w=jnp.pad(conv_w[0], ((0, 1), (0, 0))))

    def emit(self, group, g):
        d = self.d
        gathers = []
        if group in ("ffn0", "ffn1"):
            scatters = [g["w1"], g["w2"]]
        elif group == "gla":
            w_in = jnp.concatenate([g["gla_w_qkvg"], g["gla_w_r"][:, :GATE_RANK]], axis=1)
            scatters = [_column_split(w_in), g["gla_w_out"].reshape(N_DEV, d // N_DEV, d)]
        else:
            s = dict(g["small"])
            s.update(pool_w=s["pool_w"][None], conv_w=s["conv_w"][None, :CONV_WIDTH], gate_w=s["gate_w"][None, :GATE_RANK])
            self.small_grads = [s[n] for n in _SMALL_GRADS]
            gathers = [_pack(self.small_grads)]
            scatters = [_column_split(g["cp_w_in"]), g["cp_w_out"].reshape(N_DEV, d // N_DEV, d)]
        lands = _exchange_place(gathers, scatters, name=f"place_g_{group}")
        self.sent[group], self.token = _exchange_start(gathers, scatters, lands, name=f"start_g_{group}")
        return self.token

    def finish(self, w, mom, var):
        out = {}

        def adam(n, parts):
            out[n] = _reduce_adam(parts, w[n], mom[n], var[n], name=f"adam_{n}")

        ffn1 = _exchange_wait(self.sent["ffn1"], self.token, name="wait_g_ffn1")
        gla = _exchange_wait(self.sent["gla"], ffn1[0], name="wait_g_gla")
        ffn0 = _exchange_wait(self.sent["ffn0"], gla[0], name="wait_g_ffn0")
        adam("ffn_w1", [ffn0[0], ffn1[0]])
        adam("ffn_w2", [ffn0[1], ffn1[1]])
        adam("gla_w_in", [gla[0]])
        adam("gla_w_out", [gla[1]])
        small_all, cp_in, cp_out = _exchange_wait(self.sent["cp"], out["gla_w_out"][0], name="wait_g_cp")
        adam("cp_w_in", [cp_in])
        adam("cp_w_out", [cp_out])

        names = _REPLICATED + _SMALL_SHARDED
        small_sum = _unpack(_reduce8(small_all, name="sum_small_grads"), [a.shape for a in self.small_grads])
        me = _my_slot()
        grad = {}
        for n, full in zip(names, small_sum):
            if n in _SMALL_SHARDED:
                width = w[n].shape[-1]
                full = lax.dynamic_slice_in_dim(full, me * width, width, axis=full.ndim - 1)
            grad[n] = full
        packed = [_pack([t[n] for n in names]) for t in (w, grad, mom, var)]
        small_out = [_unpack(p, [w[n].shape for n in names]) for p in _adamw_small(*packed, name="adam_small")]
        for k, n in enumerate(names):
            out[n] = (grad[n], small_out[0][k], small_out[1][k], small_out[2][k])
        return out


def kernel(x, meta_tokens, mix_norm_g, ffn_norm_g, ffn_w1, ffn_w2, cp_w_in, cp_conv_w, cp_conv_b, cp_ln_g, cp_ln_b, cp_pool_w, cp_pool_scale, cp_w_out, gla_w_in, gla_gate_w2, gla_gate_b, gla_head_g, gla_w_out, final_norm_g, loss_target, m_meta_tokens, m_mix_norm_g, m_ffn_norm_g, m_ffn_w1, m_ffn_w2, m_cp_w_in, m_cp_conv_w, m_cp_conv_b, m_cp_ln_g, m_cp_ln_b, m_cp_pool_w, m_cp_pool_scale, m_cp_w_out, m_gla_w_in, m_gla_gate_w2, m_gla_gate_b, m_gla_head_g, m_gla_w_out, m_final_norm_g, v_meta_tokens, v_mix_norm_g, v_ffn_norm_g, v_ffn_w1, v_ffn_w2, v_cp_w_in, v_cp_conv_w, v_cp_conv_b, v_cp_ln_g, v_cp_ln_b, v_cp_pool_w, v_cp_pool_scale, v_cp_w_out, v_gla_w_in, v_gla_gate_w2, v_gla_gate_b, v_gla_head_g, v_gla_w_out, v_final_norm_g):
    w = dict(meta_tokens=meta_tokens, mix_norm_g=mix_norm_g, ffn_norm_g=ffn_norm_g, ffn_w1=ffn_w1, ffn_w2=ffn_w2,
             cp_w_in=cp_w_in, cp_conv_w=cp_conv_w, cp_conv_b=cp_conv_b, cp_ln_g=cp_ln_g, cp_ln_b=cp_ln_b,
             cp_pool_w=cp_pool_w, cp_pool_scale=cp_pool_scale, cp_w_out=cp_w_out, gla_w_in=gla_w_in,
             gla_gate_w2=gla_gate_w2, gla_gate_b=gla_gate_b, gla_head_g=gla_head_g, gla_w_out=gla_w_out,
             final_norm_g=final_norm_g.reshape(1, -1))
    mom = dict(meta_tokens=m_meta_tokens, mix_norm_g=m_mix_norm_g, ffn_norm_g=m_ffn_norm_g, ffn_w1=m_ffn_w1, ffn_w2=m_ffn_w2,
               cp_w_in=m_cp_w_in, cp_conv_w=m_cp_conv_w, cp_conv_b=m_cp_conv_b, cp_ln_g=m_cp_ln_g, cp_ln_b=m_cp_ln_b,
               cp_pool_w=m_cp_pool_w, cp_pool_scale=m_cp_pool_scale, cp_w_out=m_cp_w_out, gla_w_in=m_gla_w_in,
               gla_gate_w2=m_gla_gate_w2, gla_gate_b=m_gla_gate_b, gla_head_g=m_gla_head_g, gla_w_out=m_gla_w_out,
               final_norm_g=m_final_norm_g.reshape(1, -1))
    var = dict(meta_tokens=v_meta_tokens, mix_norm_g=v_mix_norm_g, ffn_norm_g=v_ffn_norm_g, ffn_w1=v_ffn_w1, ffn_w2=v_ffn_w2,
               cp_w_in=v_cp_w_in, cp_conv_w=v_cp_conv_w, cp_conv_b=v_cp_conv_b, cp_ln_g=v_cp_ln_g, cp_ln_b=v_cp_ln_b,
               cp_pool_w=v_cp_pool_w, cp_pool_scale=v_cp_pool_scale, cp_w_out=v_cp_w_out, gla_w_in=v_gla_w_in,
               gla_gate_w2=v_gla_gate_w2, gla_gate_b=v_gla_gate_b, gla_head_g=v_gla_head_g, gla_w_out=v_gla_w_out,
               final_norm_g=v_final_norm_g.reshape(1, -1))
    d = x.shape[-1]
    replicated = dict(mix_g=w["mix_norm_g"], ffn_g=w["ffn_norm_g"], conv_b=w["cp_conv_b"], ln_g=w["cp_ln_g"],
                      ln_b=w["cp_ln_b"], pool_w=w["cp_pool_w"][0].astype(BF16), pool_scale=w["cp_pool_scale"],
                      final_g=w["final_norm_g"])
    exchanges = _Exchanges(w, d)
    loss_blk, grad_x, _ = _local_step(x[0], loss_target[0], replicated, exchanges)
    loss = lax.psum(loss_blk[0, 0], ("x", "y", "c"))
    out = exchanges.finish(w, mom, var)

    def leaf(n, k):
        a = out[n][k]
        return a.reshape(-1) if n == "final_norm_g" else a

    return (loss, grad_x[None], *[leaf(n, 0) for n in _NAMES], *[leaf(n, 1) for n in _NAMES],
            *[leaf(n, 2) for n in _NAMES], *[leaf(n, 3) for n in _NAMES])
```

```python
import functools

import jax
import jax.numpy as jnp
from jax import lax
from jax.experimental import pallas as pl
from jax.experimental.pallas import tpu as pltpu

F32, BF16 = jnp.float32, jnp.bfloat16
N_DEV = 8
CHUNK = 64
N_META = 16
PAD_ROWS = CHUNK - N_META
HALO = 32
EPS = 1e-5
CONV_WIDTH = 31
POOL_WINDOWS = (2, 4, 8, 16)
HEADS = 4
GATE_RANK = 16
GATE_NORM = 16.0
GATE_PAD = 128
ADAM_LR, ADAM_B1, ADAM_B2, ADAM_EPS, ADAM_WD, ADAM_STEP = 0.001, 0.9, 0.999, 1e-08, 0.01, 10
V7X_VMEM_LIMIT = 56 * 2 ** 20
LANE = 128
HI = lax.Precision.HIGHEST


def _cparams(*sem):
    return pltpu.CompilerParams(dimension_semantics=sem, vmem_limit_bytes=V7X_VMEM_LIMIT)


def _row_tile(t, cap):
    best = CHUNK
    for r in range(CHUNK, min(t, cap) + 1, CHUNK):
        if t % r == 0:
            best = r
    return best


def _resident(shape):
    return pl.BlockSpec(shape, lambda *_: (0,) * len(shape), pipeline_mode=pl.Buffered(1))


def _dot(a, b):
    return jnp.dot(a, b, preferred_element_type=F32)


def _dot_nt(a, b):
    return lax.dot_general(a, b, (((1,), (1,)), ((), ())), preferred_element_type=F32)


def _dot_tn(a, b):
    return lax.dot_general(a, b, (((0,), (0,)), ((), ())), preferred_element_type=F32)


def _rowsum(a):
    return jnp.sum(a, axis=0, keepdims=True)


def _sigmoid(a):
    return 1.0 / (1.0 + jnp.exp(-a))


def _row_ids(tile, rt):
    return tile * rt + lax.broadcasted_iota(jnp.int32, (rt, 1), 0)


def _sds(shape, dtype):
    return jax.ShapeDtypeStruct(shape, dtype)


def _linear_fwd(x, w, *, gain=None, res=None, out_dtype=F32, name):
    t, k = x.shape
    n = w.shape[1]
    rt = _row_tile(t, 320)

    def body(*refs):
        refs = list(refs)
        x_ref, w_ref = refs[:2]
        pos = 2
        g_ref = r_ref = u_ref = None
        if gain is not None:
            g_ref = refs[pos]
            pos += 1
        if res is not None:
            r_ref = refs[pos]
            pos += 1
        y_ref = refs[pos]
        if gain is not None:
            u_ref = refs[pos + 1]
            xv = x_ref[...]
            u = (xv * lax.rsqrt(jnp.mean(xv * xv, axis=-1, keepdims=True) + EPS) * g_ref[...]).astype(BF16)
            u_ref[...] = u
        else:
            u = x_ref[...]
        y = _dot(u, w_ref[...])
        if res is not None:
            y = y + r_ref[...]
        y_ref[...] = y.astype(y_ref.dtype)

    rows = lambda i: (i, 0)
    in_specs = [pl.BlockSpec((rt, k), rows), _resident((k, n))]
    args = [x, w]
    if gain is not None:
        in_specs.append(_resident((1, k)))
        args.append(gain)
    if res is not None:
        in_specs.append(pl.BlockSpec((rt, n), rows))
        args.append(res)
    out_shape = [_sds((t, n), out_dtype)]
    out_specs = [pl.BlockSpec((rt, n), rows)]
    if gain is not None:
        out_shape.append(_sds((t, k), BF16))
        out_specs.append(pl.BlockSpec((rt, k), rows))
    out = pl.pallas_call(body, grid=(t // rt,), in_specs=in_specs, out_specs=out_specs, out_shape=out_shape,
                         compiler_params=_cparams("parallel"), name=name)(*args)
    return out if gain is not None else out[0]


def _linear_bwd_x(dys, ws, *, norm=None, after=None, name):
    t = dys[0].shape[0]
    k = ws[0].shape[0]
    rt = _row_tile(t, 320)
    nd = len(dys)
    dep_specs, deps = _dep_specs(after)

    def body(*refs):
        refs = list(refs)[len(deps):]
        dy_refs, w_refs = refs[:nd], refs[nd:2 * nd]
        dx = None
        for dy_ref, w_ref in zip(dy_refs, w_refs):
            part = _dot_nt(dy_ref[...].astype(BF16), w_ref[...])
            dx = part if dx is None else dx + part
        if norm is None:
            refs[2 * nd][...] = dx
            return
        h_ref, g_ref, dres_ref, dh_ref, dg_ref = refs[2 * nd:]
        hv = h_ref[...]
        rstd = lax.rsqrt(jnp.mean(hv * hv, axis=-1, keepdims=True) + EPS)
        xh = hv * rstd

        @pl.when(pl.program_id(0) == 0)
        def _():
            dg_ref[...] = jnp.zeros_like(dg_ref)

        dg_ref[...] += _rowsum(dx * xh)
        dxh = dx * g_ref[...]
        dh_ref[...] = dres_ref[...] + rstd * (dxh - xh * jnp.mean(dxh * xh, axis=-1, keepdims=True))

    rows = lambda i: (i, 0)
    in_specs = [pl.BlockSpec((rt, dy.shape[1]), rows) for dy in dys] + [_resident(w.shape) for w in ws]
    args = list(dys) + list(ws)
    out_shape = [_sds((t, k), F32)]
    out_specs = [pl.BlockSpec((rt, k), rows)]
    if norm is not None:
        h, gain, dres = norm
        in_specs += [pl.BlockSpec((rt, k), rows), _resident((1, k)), pl.BlockSpec((rt, k), rows)]
        args += [h, gain, dres]
        out_shape.append(_sds((1, k), F32))
        out_specs.append(pl.BlockSpec((1, k), lambda i: (0, 0)))
    out = pl.pallas_call(body, grid=(t // rt,), in_specs=dep_specs + in_specs, out_specs=out_specs, out_shape=out_shape,
                         compiler_params=_cparams("arbitrary"), name=name)(*deps, *args)
    return out if norm is not None else out[0]


def _linear_bwd_w(x, dy, *, name):
    t, k = x.shape
    n = dy.shape[1]
    rt = _row_tile(t, 832)
    nt = max(c for c in (512, 384, 256, LANE) if n % c == 0)

    def body(x_ref, dy_ref, o_ref):
        @pl.when(pl.program_id(1) == 0)
        def _():
            o_ref[...] = jnp.zeros_like(o_ref)

        o_ref[...] += _dot_tn(x_ref[...], dy_ref[...].astype(BF16))

    return pl.pallas_call(
        body, grid=(n // nt, t // rt),
        in_specs=[pl.BlockSpec((rt, k), lambda j, i: (i, 0)), pl.BlockSpec((rt, nt), lambda j, i: (i, j))],
        out_specs=pl.BlockSpec((k, nt), lambda j, i: (0, j)), out_shape=_sds((k, n), F32),
        compiler_params=_cparams("parallel", "arbitrary"), name=name)(x, dy)


def _ffn_fwd(h, gain, w1g, w2g, *, name):
    t, d = h.shape
    f8 = w1g.shape[-1]
    rt = _row_tile(t, 832)

    def body(h_ref, g_ref, w1_ref, w2_ref, o_ref, u_ref, acc_ref):
        j = pl.program_id(1)

        @pl.when(j == 0)
        def _():
            hv = h_ref[...]
            u_ref[...] = (hv * lax.rsqrt(jnp.mean(hv * hv, axis=-1, keepdims=True) + EPS) * g_ref[...]).astype(BF16)
            acc_ref[...] = jnp.zeros_like(acc_ref)

        a = jnp.maximum(_dot(u_ref[...], w1_ref[...]), 0.0)
        acc_ref[...] += _dot((a * a).astype(BF16), w2_ref[...])

        @pl.when(j == N_DEV - 1)
        def _():
            o_ref[...] = h_ref[...] + acc_ref[...]

    return pl.pallas_call(
        body, grid=(t // rt, N_DEV),
        in_specs=[pl.BlockSpec((rt, d), lambda i, j: (i, 0)), _resident((1, d)),
                  pl.BlockSpec((None, d, f8), lambda i, j: (j, 0, 0)),
                  pl.BlockSpec((None, f8, d), lambda i, j: (j, 0, 0))],
        out_specs=pl.BlockSpec((rt, d), lambda i, j: (i, 0)), out_shape=_sds((t, d), F32),
        scratch_shapes=[pltpu.VMEM((rt, d), BF16), pltpu.VMEM((rt, d), F32)],
        compiler_params=_cparams("parallel", "arbitrary"), name=name)(h, gain, w1g, w2g)


def _ffn_bwd(h, dout, gain, w1g, w2g, *, after=None, name):
    t, d = h.shape
    f8 = w1g.shape[-1]
    rt = _row_tile(t, 320)
    nrow = t // rt
    last = N_DEV - 1
    dep_specs, deps = _dep_specs(after)

    def body(*refs):
        h_ref, do_ref, g_ref, w1_ref, w2_ref, dh_ref, dw1_ref, dw2_ref, dg_ref, du_ref = refs[len(deps):]
        j, i = pl.program_id(0), pl.program_id(1)
        hv = h_ref[...]
        rstd = lax.rsqrt(jnp.mean(hv * hv, axis=-1, keepdims=True) + EPS)
        xh = hv * rstd
        u = (xh * g_ref[...]).astype(BF16)
        db = do_ref[...].astype(BF16)
        r = jnp.maximum(_dot(u, w1_ref[...]), 0.0)
        act = (r * r).astype(BF16)
        dhh = (_dot_nt(db, w2_ref[...]) * (2.0 * r)).astype(BF16)

        @pl.when(i == 0)
        def _():
            dw1_ref[...] = jnp.zeros_like(dw1_ref)
            dw2_ref[...] = jnp.zeros_like(dw2_ref)

        dw1_ref[...] += _dot_tn(u, dhh)
        dw2_ref[...] += _dot_tn(act, db)
        du_part = _dot_nt(dhh, w1_ref[...])
        rows = pl.ds(pl.multiple_of(i * rt, CHUNK), rt)

        @pl.when(j == 0)
        def _():
            du_ref[rows, :] = du_part

        @pl.when(j > 0)
        def _():
            du_ref[rows, :] += du_part

        @pl.when(j == last)
        def _():
            @pl.when(i == 0)
            def _():
                dg_ref[...] = jnp.zeros_like(dg_ref)

            du = du_ref[rows, :]
            dg_ref[...] += _rowsum(du * xh)
            dxh = du * g_ref[...]
            dh_ref[...] = do_ref[...] + rstd * (dxh - xh * jnp.mean(dxh * xh, axis=-1, keepdims=True))

    return pl.pallas_call(
        body, grid=(N_DEV, nrow),
        in_specs=dep_specs + [
                  pl.BlockSpec((rt, d), lambda j, i: (i, 0)), pl.BlockSpec((rt, d), lambda j, i: (i, 0)),
                  _resident((1, d)),
                  pl.BlockSpec((None, d, f8), lambda j, i: (j, 0, 0)),
                  pl.BlockSpec((None, f8, d), lambda j, i: (j, 0, 0))],
        out_specs=[pl.BlockSpec((rt, d), lambda j, i: (jnp.where(j == last, i, 0), 0)),
                   pl.BlockSpec((None, d, f8), lambda j, i: (j, 0, 0)),
                   pl.BlockSpec((None, f8, d), lambda j, i: (j, 0, 0)),
                   pl.BlockSpec((1, d), lambda j, i: (0, 0))],
        out_shape=[_sds((t, d), F32), _sds((N_DEV, d, f8), F32), _sds((N_DEV, f8, d), F32), _sds((1, d), F32)],
        scratch_shapes=[pltpu.VMEM((t, d), F32)],
        compiler_params=_cparams("arbitrary", "arbitrary"), name=name)(*deps, h, dout, gain, w1g, w2g)


def _lane_blocks(width):
    lb = min(LANE, width)
    return [slice(s, s + lb) for s in range(0, width, lb)]


def _conv_rows(src_ref, w_ref, offset, dst_ref, nblk, width, bias_ref=None):
    def blk(rb, carry):
        base = pl.multiple_of(rb * CHUNK, CHUNK)
        for l, ls in enumerate(_lane_blocks(width)):
            acc = jnp.zeros((CHUNK, ls.stop - ls.start), F32)
            if bias_ref is not None:
                acc = acc + bias_ref[:, ls]
            for k in range(CONV_WIDTH):
                acc = acc + w_ref[k:k + 1, ls] * src_ref[l, pl.ds(base + offset(k), CHUNK), :]
            dst_ref[l, pl.ds(base, CHUNK), :] = acc
        return carry

    lax.fori_loop(0, nblk, blk, 0)


def _to_lane_blocks(ref, row0, value):
    for l, ls in enumerate(_lane_blocks(value.shape[1])):
        ref[l, row0:row0 + value.shape[0], :] = value[:, ls]


def _from_lane_blocks(ref):
    return jnp.concatenate([ref[l] for l in range(ref.shape[0])], axis=1)


def _pool_counts(rows, window):
    return jnp.clip(rows - PAD_ROWS + 1, 1, window).astype(F32)


def _trailing_sum(v, window):
    s, sh = v, 1
    while sh < window:
        s = s + pltpu.roll(s, sh, 0)
        sh *= 2
    return s


def _leading_sum(v, window):
    s, sh, n = v, 1, v.shape[0]
    while sh < window:
        s = s + pltpu.roll(s, n - sh, 0)
        sh *= 2
    return s


def _cp_mid_fwd(z, conv_w, conv_b, ln_g, ln_b, pool_w, pool_scale, *, name):
    t, ein = z.shape
    cd = conv_b.shape[1]
    pd = pool_scale.shape[1]
    pg = pd // len(POOL_WINDOWS)
    rt = _row_tile(t, 320)

    def body(z_ref, cw_ref, cb_ref, lg_ref, lb_ref, pw_ref, ps_ref, o_ref, gext, pext, conv_s):
        i = pl.program_id(0)

        @pl.when(i == 0)
        def _():
            _to_lane_blocks(gext, 0, jnp.zeros((HALO, cd), F32))
            pext[0:HALO, :] = jnp.zeros((HALO, pd), F32)

        _to_lane_blocks(gext, HALO, z_ref[:, 0:cd] * _sigmoid(z_ref[:, cd:2 * cd]))
        pext[HALO:HALO + rt, :] = z_ref[:, 2 * cd:]
        _conv_rows(gext, cw_ref, lambda k: k + HALO - (CONV_WIDTH - 1), conv_s, rt // CHUNK, cd, cb_ref)
        cv = _from_lane_blocks(conv_s)
        xc = cv - jnp.mean(cv, axis=-1, keepdims=True)
        y = xc * lax.rsqrt(jnp.mean(xc * xc, axis=-1, keepdims=True) + EPS) * lg_ref[...] + lb_ref[...]
        rows = _row_ids(i, rt)
        a = jnp.where(rows >= PAD_ROWS, y * _sigmoid(y), 0.0)
        o_ref[:, 0:cd] = a.astype(BF16)
        for gi, window in enumerate(POOL_WINDOWS):
            ls = slice(gi * pg, (gi + 1) * pg)
            v = pext[:, ls]
            tm = _trailing_sum(v, window)[HALO:] / _pool_counts(rows, window) - v[HALO:]
            p = _dot(tm.astype(BF16), pw_ref[gi]) * ps_ref[:, ls]
            o_ref[:, cd + gi * pg:cd + (gi + 1) * pg] = p.astype(BF16)
        gext[:, 0:HALO, :] = gext[:, rt:rt + HALO, :]
        pext[0:HALO, :] = pext[rt:rt + HALO, :]

    nl, lb = len(_lane_blocks(cd)), min(LANE, cd)
    return pl.pallas_call(
        body, grid=(t // rt,),
        in_specs=[pl.BlockSpec((rt, ein), lambda i: (i, 0)), _resident(conv_w.shape), _resident((1, cd)),
                  _resident((1, cd)), _resident((1, cd)), _resident(pool_w.shape), _resident((1, pd))],
        out_specs=pl.BlockSpec((rt, cd + pd), lambda i: (i, 0)), out_shape=_sds((t, cd + pd), BF16),
        scratch_shapes=[pltpu.VMEM((nl, rt + HALO, lb), F32), pltpu.VMEM((rt + HALO, pd), F32),
                        pltpu.VMEM((nl, rt, lb), F32)],
        compiler_params=_cparams("arbitrary"), name=name)(z, conv_w, conv_b, ln_g, ln_b, pool_w, pool_scale)


def _cp_mid_bwd(z, dcat, conv_w, conv_b, ln_g, ln_b, pool_w, pool_scale, *, after=None, name):
    t, ein = z.shape
    cd = conv_b.shape[1]
    pd = pool_scale.shape[1]
    pg = pd // len(POOL_WINDOWS)
    rt = _row_tile(t, 320)
    ntile = t // rt
    per = rt // CHUNK
    dep_specs, deps = _dep_specs(after)

    def body(*refs):
        (z_ref, zh_ref, dc_ref, cw_ref, cb_ref, lg_ref, lb_ref, pw_ref, ps_ref,
         dz_ref, dcw_ref, dcb_ref, dlg_ref, dlb_ref, dpw_ref, dps_ref, gext, pext, conv_s, dcv, dsp) = refs[len(deps):]
        step = pl.program_id(0)
        tile = ntile - 1 - step

        @pl.when(step == 0)
        def _():
            for ref in (dcw_ref, dcb_ref, dlg_ref, dlb_ref, dpw_ref, dps_ref):
                ref[...] = jnp.zeros_like(ref)
            _to_lane_blocks(dcv, rt, jnp.zeros((HALO, cd), F32))
            dsp[rt:rt + HALO, :] = jnp.zeros((HALO, pd), F32)

        keep = jnp.where(tile > 0, 1.0, 0.0)
        zh = zh_ref[CHUNK - HALO:CHUNK, :]
        _to_lane_blocks(gext, 0, keep * zh[:, 0:cd] * _sigmoid(zh[:, cd:2 * cd]))
        pext[0:HALO, :] = keep * zh[:, 2 * cd:]
        za = z_ref[:, 0:cd]
        sg = _sigmoid(z_ref[:, cd:2 * cd])
        _to_lane_blocks(gext, HALO, za * sg)
        pext[HALO:HALO + rt, :] = z_ref[:, 2 * cd:]
        _conv_rows(gext, cw_ref, lambda k: k + HALO - (CONV_WIDTH - 1), conv_s, per, cd, cb_ref)
        cv = _from_lane_blocks(conv_s)
        xc = cv - jnp.mean(cv, axis=-1, keepdims=True)
        rstd = lax.rsqrt(jnp.mean(xc * xc, axis=-1, keepdims=True) + EPS)
        xh = xc * rstd
        y = xh * lg_ref[...] + lb_ref[...]
        sy = _sigmoid(y)
        rows = _row_ids(tile, rt)
        da = jnp.where(rows >= PAD_ROWS, dc_ref[:, 0:cd], 0.0)
        dy = da * (sy * (1.0 + y * (1.0 - sy)))
        dlg_ref[...] += _rowsum(dy * xh)
        dlb_ref[...] += _rowsum(dy)
        dxh = dy * lg_ref[...]
        dconv = rstd * (dxh - jnp.mean(dxh, axis=-1, keepdims=True) - xh * jnp.mean(dxh * xh, axis=-1, keepdims=True))
        dcb_ref[...] += _rowsum(dconv)
        _to_lane_blocks(dcv, 0, dconv)
        for l, ls in enumerate(_lane_blocks(cd)):
            for k in range(CONV_WIDTH):
                def acc_rows(rb, acc, l=l, k=k):
                    base = pl.multiple_of(rb * CHUNK, CHUNK)
                    return acc + dcv[l, pl.ds(base, CHUNK), :] * gext[l, pl.ds(base + k + HALO - (CONV_WIDTH - 1), CHUNK), :]

                acc = lax.fori_loop(0, per, acc_rows, jnp.zeros((CHUNK, ls.stop - ls.start), F32))
                dcw_ref[k:k + 1, ls] += _rowsum(acc)
        _conv_rows(dcv, cw_ref, lambda k: CONV_WIDTH - 1 - k, conv_s, per, cd)
        dglu = _from_lane_blocks(conv_s)
        dz_ref[:, 0:cd] = (dglu * sg).astype(BF16)
        dz_ref[:, cd:2 * cd] = (dglu * za * sg * (1.0 - sg)).astype(BF16)
        dcv[:, rt:rt + HALO, :] = dcv[:, 0:HALO, :]
        for gi, window in enumerate(POOL_WINDOWS):
            ls = slice(gi * pg, (gi + 1) * pg)
            v = pext[:, ls]
            cnt = _pool_counts(rows, window)
            tm = (_trailing_sum(v, window)[HALO:] / cnt - v[HALO:]).astype(BF16)
            dp = dc_ref[:, cd + gi * pg:cd + (gi + 1) * pg]
            dps_ref[:, ls] += _rowsum(dp * _dot(tm, pw_ref[gi]))
            dpl = (dp * ps_ref[:, ls]).astype(BF16)
            dpw_ref[gi] += _dot_tn(tm, dpl)
            dtm = _dot_nt(dpl, pw_ref[gi])
            dsp[0:rt, ls] = dtm / cnt
            dpin = _leading_sum(dsp[:, ls], window)[0:rt] - dtm
            dz_ref[:, 2 * cd + gi * pg:2 * cd + (gi + 1) * pg] = dpin.astype(BF16)
        dsp[rt:rt + HALO, :] = dsp[0:HALO, :]

    back = lambda i: (ntile - 1 - i, 0)
    halo_idx = lambda i: (jnp.maximum((ntile - 1 - i) * per - 1, 0), 0)
    const2 = lambda i: (0, 0)
    nl, lb = len(_lane_blocks(cd)), min(LANE, cd)
    return pl.pallas_call(
        body, grid=(ntile,),
        in_specs=dep_specs + [
                  pl.BlockSpec((rt, ein), back), pl.BlockSpec((CHUNK, ein), halo_idx), pl.BlockSpec((rt, cd + pd), back),
                  _resident(conv_w.shape), _resident((1, cd)), _resident((1, cd)), _resident((1, cd)),
                  _resident(pool_w.shape), _resident((1, pd))],
        out_specs=[pl.BlockSpec((rt, ein), back), pl.BlockSpec(conv_w.shape, const2), pl.BlockSpec((1, cd), const2),
                   pl.BlockSpec((1, cd), const2), pl.BlockSpec((1, cd), const2),
                   pl.BlockSpec(pool_w.shape, lambda i: (0, 0, 0)), pl.BlockSpec((1, pd), const2)],
        out_shape=[_sds((t, ein), BF16), _sds(conv_w.shape, F32), _sds((1, cd), F32), _sds((1, cd), F32),
                   _sds((1, cd), F32), _sds(pool_w.shape, F32), _sds((1, pd), F32)],
        scratch_shapes=[pltpu.VMEM((nl, rt + HALO, lb), F32), pltpu.VMEM((rt + HALO, pd), F32), pltpu.VMEM((nl, rt, lb), F32),
                        pltpu.VMEM((nl, rt + HALO, lb), F32), pltpu.VMEM((rt + HALO, pd), F32)],
        compiler_params=_cparams("arbitrary"), name=name)(*deps, z, z, dcat, conv_w, conv_b, ln_g, ln_b, pool_w, pool_scale)


def _log_decay(r_ref, gw_ref, gb_ref, rows):
    gp = _dot(r_ref[...].astype(BF16), gw_ref[...]) + gb_ref[...]
    log_sig = jnp.minimum(gp, 0.0) - jnp.log(1.0 + jnp.exp(-jnp.abs(gp)))
    return gp, jnp.where(rows >= PAD_ROWS, log_sig / GATE_NORM, 0.0)


def _tri(strict):
    r = lax.broadcasted_iota(jnp.int32, (CHUNK, CHUNK), 0)
    c = lax.broadcasted_iota(jnp.int32, (CHUNK, CHUNK), 1)
    return jnp.where(c < r if strict else c <= r, 1.0, 0.0).astype(F32)


def _gla_mid_fwd(z, r, gate_w, gate_b, head_g, *, name):
    t = z.shape[0]
    dk = gate_b.shape[1]
    hv = head_g.shape[1]
    hk = dk // HEADS
    dv = hv * HEADS
    rt = _row_tile(t, 320)
    per = rt // CHUNK
    scale = hk ** -0.5

    def body(z_ref, r_ref, gw_ref, gb_ref, hg_ref, o_ref, st_ref, s_ref, la_ref):
        i = pl.program_id(0)

        @pl.when(i == 0)
        def _():
            s_ref[...] = jnp.zeros_like(s_ref)

        _, la = _log_decay(r_ref, gw_ref, gb_ref, _row_ids(i, rt))
        la_ref[...] = la
        tri = _tri(False)

        def chunk(c, carry):
            rows = pl.ds(pl.multiple_of(c * CHUNK, CHUNK), CHUNK)
            la_c = la_ref[rows, :]
            cum = jnp.dot(tri, la_c, precision=HI, preferred_element_type=F32)
            tot = _rowsum(la_c)
            dec = jnp.exp(tot - cum)
            etot = jnp.exp(tot)
            for hd in range(HEADS):
                ks = slice(hd * hk, (hd + 1) * hk)
                q = z_ref[rows, hd * hk:(hd + 1) * hk] * scale
                kd = z_ref[rows, dk + hd * hk:dk + (hd + 1) * hk] * dec[:, ks]
                v = z_ref[rows, 2 * dk + hd * hv:2 * dk + (hd + 1) * hv]
                g = z_ref[rows, 2 * dk + dv + hd * hv:2 * dk + dv + (hd + 1) * hv]
                s_new = s_ref[hd] * etot[:, ks] + _dot_tn(v.astype(BF16), kd.astype(BF16))
                s_ref[hd] = s_new
                st_ref[c, hd] = s_new
                o = _dot_nt(q.astype(BF16), s_new.astype(BF16))
                on = o * lax.rsqrt(jnp.mean(o * o, axis=-1, keepdims=True) + EPS) * hg_ref[...]
                o_ref[rows, hd * hv:(hd + 1) * hv] = (on * (g * _sigmoid(g))).astype(BF16)
            return carry

        lax.fori_loop(0, per, chunk, 0)

    return pl.pallas_call(
        body, grid=(t // rt,),
        in_specs=[pl.BlockSpec((rt, z.shape[1]), lambda i: (i, 0)), pl.BlockSpec((rt, GATE_PAD), lambda i: (i, 0)),
                  _resident(gate_w.shape), _resident((1, dk)), _resident((1, hv))],
        out_specs=[pl.BlockSpec((rt, dv), lambda i: (i, 0)), pl.BlockSpec((per, HEADS, hv, hk), lambda i: (i, 0, 0, 0))],
        out_shape=[_sds((t, dv), BF16), _sds((t // CHUNK, HEADS, hv, hk), F32)],
        scratch_shapes=[pltpu.VMEM((HEADS, hv, hk), F32), pltpu.VMEM((rt, dk), F32)],
        compiler_params=_cparams("arbitrary"), name=name)(z, r, gate_w, gate_b, head_g)


def _gla_mid_bwd(z, r, dog, states, gate_w, gate_b, head_g, *, after=None, name):
    t = z.shape[0]
    dk = gate_b.shape[1]
    hv = head_g.shape[1]
    hk = dk // HEADS
    dv = hv * HEADS
    rt = _row_tile(t, 320)
    ntile = t // rt
    per = rt // CHUNK
    scale = hk ** -0.5
    dep_specs, deps = _dep_specs(after)

    def body(*refs):
        (z_ref, r_ref, do_ref, st_ref, stp_ref, gw_ref, gb_ref, hg_ref,
         dz_ref, dr_ref, dgw_ref, dgb_ref, dhg_ref, ds_ref, la_ref, dla_ref) = refs[len(deps):]
        step = pl.program_id(0)
        tile = ntile - 1 - step

        @pl.when(step == 0)
        def _():
            ds_ref[...] = jnp.zeros_like(ds_ref)
            dgw_ref[...] = jnp.zeros_like(dgw_ref)
            dgb_ref[...] = jnp.zeros_like(dgb_ref)
            dhg_ref[...] = jnp.zeros_like(dhg_ref)

        rows_id = _row_ids(tile, rt)
        gp, la = _log_decay(r_ref, gw_ref, gb_ref, rows_id)
        la_ref[...] = la
        tri, tri_strict = _tri(False), _tri(True)
        keep = jnp.where(tile > 0, 1.0, 0.0)

        def chunk(cc, carry):
            c = per - 1 - cc
            rows = pl.ds(pl.multiple_of(c * CHUNK, CHUNK), CHUNK)
            la_c = la_ref[rows, :]
            cum = jnp.dot(tri, la_c, precision=HI, preferred_element_type=F32)
            tot = _rowsum(la_c)
            dec = jnp.exp(tot - cum)
            etot = jnp.exp(tot)
            inside = jnp.where(c > 0, 1.0, 0.0)
            for hd in range(HEADS):
                ks = slice(hd * hk, (hd + 1) * hk)
                q = (z_ref[rows, hd * hk:(hd + 1) * hk] * scale).astype(BF16)
                k = z_ref[rows, dk + hd * hk:dk + (hd + 1) * hk]
                kd = k * dec[:, ks]
                v = z_ref[rows, 2 * dk + hd * hv:2 * dk + (hd + 1) * hv].astype(BF16)
                g = z_ref[rows, 2 * dk + dv + hd * hv:2 * dk + dv + (hd + 1) * hv]
                s_now = st_ref[c, hd]
                s_prev = inside * st_ref[jnp.maximum(c - 1, 0), hd] + (1.0 - inside) * keep * stp_ref[0, hd]
                s_b = s_now.astype(BF16)
                o = _dot_nt(q, s_b)
                rstd = lax.rsqrt(jnp.mean(o * o, axis=-1, keepdims=True) + EPS)
                oh = o * rstd
                sg = _sigmoid(g)
                d_og = do_ref[rows, hd * hv:(hd + 1) * hv]
                dz_ref[rows, 2 * dk + dv + hd * hv:2 * dk + dv + (hd + 1) * hv] = (
                    d_og * oh * hg_ref[...] * (sg * (1.0 + g * (1.0 - sg)))).astype(BF16)
                don = d_og * (g * sg)
                dhg_ref[...] += _rowsum(don * oh)
                doh = don * hg_ref[...]
                d_o = (rstd * (doh - oh * jnp.mean(doh * oh, axis=-1, keepdims=True))).astype(BF16)
                dz_ref[rows, hd * hk:(hd + 1) * hk] = (_dot(d_o, s_b) * scale).astype(BF16)
                ds_t = ds_ref[hd] + _dot_tn(d_o, q)
                ds_b = ds_t.astype(BF16)
                dkd = _dot(v, ds_b)
                dz_ref[rows, 2 * dk + hd * hv:2 * dk + (hd + 1) * hv] = _dot_nt(kd.astype(BF16), ds_b).astype(BF16)
                dtot = etot[:, ks] * _rowsum(ds_t * s_prev)
                ds_ref[hd] = ds_t * etot[:, ks]
                dz_ref[rows, dk + hd * hk:dk + (hd + 1) * hk] = (dkd * dec[:, ks]).astype(BF16)
                e = dkd * kd
                dla_ref[rows, ks] = dtot + jnp.dot(tri_strict, e, precision=HI, preferred_element_type=F32)
            return carry

        lax.fori_loop(0, per, chunk, 0)
        dla = jnp.where(rows_id >= PAD_ROWS, dla_ref[...], 0.0)
        dgp = dla * (1.0 / GATE_NORM) * (1.0 - _sigmoid(gp))
        dgb_ref[...] += _rowsum(dgp)
        dgp_b = dgp.astype(BF16)
        dgw_ref[...] += _dot_tn(r_ref[...].astype(BF16), dgp_b)
        dr_ref[...] = _dot_nt(dgp_b, gw_ref[...]).astype(BF16)

    back = lambda i: (ntile - 1 - i, 0)
    const2 = lambda i: (0, 0)
    return pl.pallas_call(
        body, grid=(ntile,),
        in_specs=dep_specs + [
                  pl.BlockSpec((rt, z.shape[1]), back), pl.BlockSpec((rt, GATE_PAD), back), pl.BlockSpec((rt, dv), back),
                  pl.BlockSpec((per, HEADS, hv, hk), lambda i: (ntile - 1 - i, 0, 0, 0)),
                  pl.BlockSpec((1, HEADS, hv, hk), lambda i: (jnp.maximum((ntile - 1 - i) * per - 1, 0), 0, 0, 0)),
                  _resident(gate_w.shape), _resident((1, dk)), _resident((1, hv))],
        out_specs=[pl.BlockSpec((rt, z.shape[1]), back), pl.BlockSpec((rt, GATE_PAD), back),
                   pl.BlockSpec(gate_w.shape, const2), pl.BlockSpec((1, dk), const2), pl.BlockSpec((1, hv), const2)],
        out_shape=[_sds(z.shape, BF16), _sds((t, GATE_PAD), BF16), _sds(gate_w.shape, F32), _sds((1, dk), F32),
                   _sds((1, hv), F32)],
        scratch_shapes=[pltpu.VMEM((HEADS, hv, hk), F32), pltpu.VMEM((rt, dk), F32), pltpu.VMEM((rt, dk), F32)],
        compiler_params=_cparams("arbitrary"), name=name)(*deps, z, r, dog, states, states, gate_w, gate_b, head_g)


def _head(h, gain, target, *, name):
    t, d = h.shape
    rt = _row_tile(t, 832)

    def body(h_ref, g_ref, t_ref, dh_ref, loss_ref, dg_ref):
        i = pl.program_id(0)

        @pl.when(i == 0)
        def _():
            loss_ref[...] = jnp.zeros_like(loss_ref)
            dg_ref[...] = jnp.zeros_like(dg_ref)

        hv = h_ref[...]
        rstd = lax.rsqrt(jnp.mean(hv * hv, axis=-1, keepdims=True) + EPS)
        xh = hv * rstd
        err = jnp.where(_row_ids(i, rt) >= CHUNK, xh * g_ref[...] - t_ref[...], 0.0)
        loss_ref[...] += (0.5 / d) * jnp.sum(err * err)
        dy = err * (1.0 / d)
        dg_ref[...] += _rowsum(dy * xh)
        dxh = dy * g_ref[...]
        dh_ref[...] = rstd * (dxh - xh * jnp.mean(dxh * xh, axis=-1, keepdims=True))

    return pl.pallas_call(
        body, grid=(t // rt,),
        in_specs=[pl.BlockSpec((rt, d), lambda i: (i, 0)), _resident((1, d)), pl.BlockSpec((rt, d), lambda i: (i, 0))],
        out_specs=[pl.BlockSpec((rt, d), lambda i: (i, 0)), pl.BlockSpec((8, LANE), lambda i: (0, 0)),
                   pl.BlockSpec((1, d), lambda i: (0, 0))],
        out_shape=[_sds((t, d), F32), _sds((8, LANE), F32), _sds((1, d), F32)],
        compiler_params=_cparams("arbitrary"), name=name)(h, gain, target)


def _adamw_math(w, g, m, v):
    m = ADAM_B1 * m + (1.0 - ADAM_B1) * g
    v = ADAM_B2 * v + (1.0 - ADAM_B2) * (g * g)
    m_hat = m / (1.0 - ADAM_B1 ** ADAM_STEP)
    v_hat = v / (1.0 - ADAM_B2 ** ADAM_STEP)
    return -ADAM_LR * (m_hat / (jnp.sqrt(v_hat) + ADAM_EPS) + ADAM_WD * w), m, v


N_CHIP = N_DEV // 2
BLOCK_ELEMS = 128 * 1024


def _my_slot():
    return 4 * lax.axis_index("x") + 2 * lax.axis_index("y") + lax.axis_index("c")


def _my_chip():
    return 2 * lax.axis_index("x") + lax.axis_index("y")


def _row_block(r, c):
    cap = max(8, BLOCK_ELEMS // (-(-c // LANE) * LANE))
    return max(b for b in range(8, r + 1, 8) if r % b == 0 and b <= max(cap, 8))


def _pair_add(a, landed, *, name):
    _, r, c = a.shape
    rb = _row_block(r, c)

    def body(a_ref, l_ref, o_ref):
        mine = jnp.where(lax.axis_index("c") == 0, a_ref[0], a_ref[1])
        o_ref[...] = mine + l_ref[...]

    one = pl.BlockSpec((None, rb, c), lambda q, i: (q, i, 0))
    return pl.pallas_call(
        body, grid=(N_CHIP, r // rb), in_specs=[pl.BlockSpec((2, rb, c), lambda q, i: (q, i, 0)), one], out_specs=one,
        out_shape=_sds((N_CHIP, r, c), F32), compiler_params=_cparams("parallel", "parallel"), name=name)(a, landed)


def _reduce_adam(parts, w, m, v, *, name):
    nl, r, c = w.shape
    rb = _row_block(r, c)

    def body(*refs):
        p_refs = refs[:2 * nl]
        w_ref, m_ref, v_ref, g_out, d_out, m_out, v_out = refs[2 * nl:]
        layer = pl.program_id(0)
        chip = _my_chip()
        for li in range(nl):
            @pl.when(layer == li)
            def _(li=li):
                mine_ref, land_ref = p_refs[2 * li], p_refs[2 * li + 1]
                g = None
                for q in range(N_CHIP):
                    term = jnp.where(chip == q, mine_ref[q], land_ref[q])
                    g = term if g is None else g + term
                g_out[...] = g
                d_out[...], m_out[...], v_out[...] = _adamw_math(w_ref[...], g, m_ref[...], v_ref[...])

    blk = pl.BlockSpec((None, rb, c), lambda l, i: (l, i, 0))
    p_specs = [pl.BlockSpec((N_CHIP, rb, c), lambda l, i, li=li: (0, jnp.where(l == li, i, 0), 0))
               for li in range(nl) for _ in range(2)]
    flat = [p for pair in parts for p in pair]
    return pl.pallas_call(
        body, grid=(nl, r // rb), in_specs=p_specs + [blk, blk, blk], out_specs=[blk] * 4,
        out_shape=[_sds(w.shape, F32)] * 4, compiler_params=_cparams("arbitrary", "arbitrary"), name=name)(*flat, w, m, v)


def _reduce8(own, landed, *, name):
    r, c = own.shape

    def body(own_ref, p_ref, o_ref):
        me = _my_slot()
        g = None
        for dev in range(N_DEV):
            term = jnp.where(me == dev, own_ref[...], p_ref[dev])
            g = term if g is None else g + term
        o_ref[...] = g

    return pl.pallas_call(body, out_shape=_sds((r, c), F32), name=name)(own, landed)


def _adamw_small(w, g, m, v, *, name):
    def body(w_ref, g_ref, m_ref, v_ref, d_out, m_out, v_out):
        d_out[...], m_out[...], v_out[...] = _adamw_math(w_ref[...], g_ref[...], m_ref[...], v_ref[...])

    return pl.pallas_call(body, out_shape=[_sds(w.shape, F32)] * 3, name=name)(w, g, m, v)


_HBM = pl.BlockSpec(memory_space=pltpu.HBM)
_SEM = pl.BlockSpec(memory_space=pltpu.SEMAPHORE)
_DATAFLOW = pltpu.SideEffectType.DATAFLOW_SIDE_EFFECTING


def _plan_to_all(src, land):
    x, y, c = lax.axis_index("x"), lax.axis_index("y"), lax.axis_index("c")
    return [(src, land.at[_my_slot()], (x ^ ((d >> 2) & 1), y ^ ((d >> 1) & 1), c ^ (d & 1))) for d in range(1, N_DEV)]


def _plan_to_sibling(src, land):
    x, y, c = lax.axis_index("x"), lax.axis_index("y"), lax.axis_index("c")
    return [(src.at[2 * q + 1 - c], land.at[q], (x, y, 1 - c)) for q in range(N_CHIP)]


def _plan_to_chips(src, land):
    x, y, c = lax.axis_index("x"), lax.axis_index("y"), lax.axis_index("c")
    peers = [(x ^ (d >> 1), y ^ (d & 1)) for d in range(1, N_CHIP)]
    return [(src.at[2 * px + py], land.at[_my_chip()], (px, py, c)) for px, py in peers]


_PLAN_COPIES = {_plan_to_all: N_DEV - 1, _plan_to_sibling: N_CHIP, _plan_to_chips: N_CHIP - 1}


def _exchange_copies(plan, ins, lands, send, recv):
    per = _PLAN_COPIES[plan]
    copies = []
    for a, (src, land) in enumerate(zip(ins, lands)):
        for i, (s, dst, dev) in enumerate(plan(src, land)):
            copies.append(pltpu.make_async_remote_copy(
                src_ref=s, dst_ref=dst, send_sem=send.at[a * per + i], recv_sem=recv.at[a * per + i],
                device_id=dev, device_id_type=pl.DeviceIdType.MESH))
    return copies


def _place_own(arrs, *, name):
    n = len(arrs)

    def body(*refs):
        ins, outs, sems = refs[:n], refs[n:2 * n], refs[2 * n]
        own = [pltpu.make_async_copy(ins[a], outs[a].at[_my_slot()], sems.at[a]) for a in range(n)]
        for cp in own:
            cp.start()
        for cp in own:
            cp.wait()

    hbm = pl.BlockSpec(memory_space=pl.ANY)
    return pl.pallas_call(
        body, in_specs=[hbm] * n, out_specs=[hbm] * n, out_shape=[_sds((N_DEV,) + a.shape, a.dtype) for a in arrs],
        scratch_shapes=[pltpu.SemaphoreType.DMA((n,))], name=name)(*arrs)


def _exchange_start(plan, arrs, lands, *, after=None, name):
    n = len(arrs)
    if lands is None:
        lands = [lax.empty((N_CHIP,) + a.shape[1:], a.dtype) for a in arrs]
    nsem = n * _PLAN_COPIES[plan]
    dep_specs, deps = _dep_specs(after)

    def body(*refs):
        ins, land_refs = refs[:n], refs[n:2 * n]
        send, recv = refs[2 * n + len(deps)], refs[2 * n + len(deps) + 1]
        for cp in _exchange_copies(plan, ins, land_refs, send, recv):
            cp.start()
        refs[-1][...] = jnp.zeros_like(refs[-1])

    thru = [pltpu.HBM(a.shape, a.dtype) for a in list(arrs) + list(lands)]
    out = pl.pallas_call(
        body, name=name,
        out_shape=(pltpu.SemaphoreType.DMA((nsem,)), pltpu.SemaphoreType.DMA((nsem,)), *thru, _sds((8, LANE), F32)),
        in_specs=[_HBM] * (2 * n) + dep_specs,
        out_specs=(_SEM, _SEM, *([_HBM] * (2 * n)), pl.BlockSpec(memory_space=pltpu.VMEM)),
        input_output_aliases={i: 2 + i for i in range(2 * n)},
        compiler_params=pltpu.CompilerParams(has_side_effects=_DATAFLOW),
    )(*[pltpu.with_memory_space_constraint(a, pltpu.HBM) for a in arrs],
      *[pltpu.with_memory_space_constraint(a, pltpu.HBM) for a in lands], *deps)
    return (plan, out[0], out[1], list(out[2:2 + n]), list(out[2 + n:2 + 2 * n])), out[-1]


def _exchange_wait(state, after, *, name):
    plan, send_sem, recv_sem, srcs, lands = state
    n = len(srcs)

    def body(*refs):
        ins, land_refs, send, recv = refs[:n], refs[n:2 * n], refs[2 * n], refs[2 * n + 1]
        for cp in _exchange_copies(plan, ins, land_refs, send, recv):
            cp.wait_send()
            cp.wait_recv()

    out = pl.pallas_call(
        body, name=name, out_shape=[pltpu.HBM(a.shape, a.dtype) for a in srcs + lands],
        in_specs=[_HBM] * (2 * n) + [_SEM, _SEM, pl.BlockSpec(memory_space=pl.ANY)], out_specs=[_HBM] * (2 * n),
        input_output_aliases={i: i for i in range(2 * n)},
        compiler_params=pltpu.CompilerParams(has_side_effects=_DATAFLOW),
    )(*srcs, *lands, send_sem, recv_sem, after)
    return list(out[n:])


def _dep_specs(after):
    return ([], []) if after is None else ([pl.BlockSpec(memory_space=pl.ANY)], [after])


def _pack(arrays):
    flat = jnp.concatenate([a.reshape(-1) for a in arrays])
    size = flat.shape[0]
    padded = -(-size // (8 * LANE)) * (8 * LANE)
    return jnp.pad(flat, (0, padded - size)).reshape(padded // LANE, LANE)


def _unpack(packed, shapes):
    flat = packed.reshape(-1)
    out, pos = [], 0
    for shp in shapes:
        size = 1
        for s in shp:
            size *= s
        out.append(flat[pos:pos + size].reshape(shp))
        pos += size
    return out


def _undo_column_split(g):
    return jnp.transpose(g, (1, 0, 2)).reshape(g.shape[1], N_DEV * g.shape[2])


def _column_split(a):
    r, c = a.shape
    return jnp.transpose(a.reshape(r, N_DEV, c // N_DEV), (1, 0, 2))


class _WholeWeights:
    def __init__(self, groups):
        self.groups = groups
        self.grads = {}

    def fetch(self, group, after):
        return self.groups[group]

    def emit(self, group, grads):
        self.grads.update(grads)
        return None

    def poll(self, after):
        return None


def _local_step(x, target, replicated, src):
    d = x.shape[1]
    mix_g, ffn_g = replicated["mix_g"], replicated["ffn_g"]
    cp = src.fetch("cp", x)
    h0 = jnp.concatenate([jnp.zeros((PAD_ROWS, d), F32), cp["meta"], x], axis=0)
    tgt = jnp.concatenate([jnp.zeros((CHUNK, d), F32), target], axis=0)
    cp_mid = (cp["conv_w"], replicated["conv_b"], replicated["ln_g"], replicated["ln_b"], replicated["pool_w"],
              replicated["pool_scale"])

    z0, u0 = _linear_fwd(h0, cp["cp_w_in"], gain=mix_g[0:1], name="cp_in")
    cat = _cp_mid_fwd(z0, *cp_mid, name="cp_mid")
    h1 = _linear_fwd(cat, cp["cp_w_out"], res=h0, name="cp_out")
    ffn0 = src.fetch("ffn0", h1)
    h2 = _ffn_fwd(h1, ffn_g[0:1], ffn0["w1"], ffn0["w2"], name="ffn0")
    gla = src.fetch("gla", h2)
    gla_mid = (gla["gate_w"], gla["gate_b"], gla["head_g"])
    z1, u1 = _linear_fwd(h2, gla["gla_w_qkvg"], gain=mix_g[1:2], name="gla_in")
    r1 = _linear_fwd(u1, gla["gla_w_r"], name="gla_in_r")
    og, states = _gla_mid_fwd(z1, r1, *gla_mid, name="gla_mid")
    h3 = _linear_fwd(og, gla["gla_w_out"], res=h2, name="gla_out")
    ffn1 = src.fetch("ffn1", h3)
    h4 = _ffn_fwd(h3, ffn_g[1:2], ffn1["w1"], ffn1["w2"], name="ffn1")
    dh4, loss, d_final_g = _head(h4, replicated["final_g"], tgt, name="head")

    dh3, dw1_1, dw2_1, dffn_g1 = _ffn_bwd(h3, dh4, ffn_g[1:2], ffn1["w1"], ffn1["w2"], name="ffn1_bwd")
    sent = src.emit("ffn1", dict(w1=dw1_1, w2=dw2_1))
    dog = _linear_bwd_x([dh3], [gla["gla_w_out"]], after=sent, name="gla_out_dx")
    d_gla_w_out = _linear_bwd_w(og, dh3, name="gla_out_dw")
    sent = src.poll(d_gla_w_out)
    dz1, dr1, d_gate_w, d_gate_b, d_head_g = _gla_mid_bwd(z1, r1, dog, states, *gla_mid, after=sent, name="gla_mid_bwd")
    dh2, dmix_g1 = _linear_bwd_x([dz1, dr1], [gla["gla_w_qkvg"], gla["gla_w_r"]],
                                 norm=(h2, mix_g[1:2], dh3), name="gla_in_dx")
    d_gla_w_qkvg = _linear_bwd_w(u1, dz1, name="gla_in_dw")
    d_gla_w_r = _linear_bwd_w(u1, dr1, name="gla_in_r_dw")
    sent = src.emit("gla", dict(gla_w_qkvg=d_gla_w_qkvg, gla_w_r=d_gla_w_r, gla_w_out=d_gla_w_out))
    dh1, dw1_0, dw2_0, dffn_g0 = _ffn_bwd(h1, dh2, ffn_g[0:1], ffn0["w1"], ffn0["w2"], after=sent, name="ffn0_bwd")
    src.poll(dw1_0)
    sent = src.emit("ffn0", dict(w1=dw1_0, w2=dw2_0))
    dcat = _linear_bwd_x([dh1], [cp["cp_w_out"]], after=sent, name="cp_out_dx")
    d_cp_w_out = _linear_bwd_w(cat, dh1, name="cp_out_dw")
    sent = src.poll(d_cp_w_out)
    dz0, d_conv_w, d_conv_b, d_ln_g, d_ln_b, d_pool_w, d_pool_scale = _cp_mid_bwd(z0, dcat, *cp_mid, after=sent,
                                                                                 name="cp_mid_bwd")
    dh0, dmix_g0 = _linear_bwd_x([dz0], [cp["cp_w_in"]], norm=(h0, mix_g[0:1], dh1), name="cp_in_dx")
    d_cp_w_in = _linear_bwd_w(u0, dz0, name="cp_in_dw")

    small = dict(
        mix_g=jnp.concatenate([dmix_g0, dmix_g1]), ffn_g=jnp.concatenate([dffn_g0, dffn_g1]), conv_b=d_conv_b, ln_g=d_ln_g,
        ln_b=d_ln_b, pool_w=d_pool_w, pool_scale=d_pool_scale, final_g=d_final_g, meta=dh0[PAD_ROWS:CHUNK], conv_w=d_conv_w,
        gate_w=d_gate_w, gate_b=d_gate_b, head_g=d_head_g)
    src.emit("cp", dict(cp_w_in=d_cp_w_in, cp_w_out=d_cp_w_out, small=small))
    return loss, dh0[CHUNK:], small


_REPLICATED = ("mix_norm_g", "ffn_norm_g", "cp_conv_b", "cp_ln_g", "cp_ln_b", "cp_pool_w", "cp_pool_scale", "final_norm_g")
_SMALL_SHARDED = ("meta_tokens", "cp_conv_w", "gla_gate_w2", "gla_gate_b", "gla_head_g")
_LARGE = ("ffn_w1", "ffn_w2", "cp_w_in", "cp_w_out", "gla_w_in", "gla_w_out")
_NAMES = ("meta_tokens", "mix_norm_g", "ffn_norm_g", "ffn_w1", "ffn_w2", "cp_w_in", "cp_conv_w", "cp_conv_b", "cp_ln_g",
          "cp_ln_b", "cp_pool_w", "cp_pool_scale", "cp_w_out", "gla_w_in", "gla_gate_w2", "gla_gate_b", "gla_head_g",
          "gla_w_out", "final_norm_g")
_SMALL_GRADS = ("mix_g", "ffn_g", "conv_b", "ln_g", "ln_b", "pool_w", "pool_scale", "final_g", "meta", "conv_w", "gate_w",
                "gate_b", "head_g")
_GROUPS = ("cp", "ffn0", "gla", "ffn1")


class _Exchanges:
    def __init__(self, w, d):
        self.d = d
        self.small_shards = [w[n] for n in _SMALL_SHARDED]
        shards = dict(
            cp=[w["cp_w_in"][0].astype(BF16), w["cp_w_out"][0].astype(BF16), _pack(self.small_shards)],
            ffn0=[w["ffn_w1"][0].astype(BF16), w["ffn_w2"][0].astype(BF16)],
            gla=[w["gla_w_in"][0].astype(BF16), w["gla_w_out"][0].astype(BF16)],
            ffn1=[w["ffn_w1"][1].astype(BF16), w["ffn_w2"][1].astype(BF16)])
        self.gathers = {}
        self.to_sibling, self.to_chips = [], {}
        lands = _place_own([a for group in _GROUPS for a in shards[group]], name="place_weights")
        token = None
        for group in _GROUPS:
            n = len(shards[group])
            self.gathers[group], token = _exchange_start(_plan_to_all, shards[group], lands[:n], after=token,
                                                         name=f"start_w_{group}")
            lands = lands[n:]
        self.token = token

    def fetch(self, group, after):
        d = self.d
        got = _exchange_wait(self.gathers[group], self.token if group == _GROUPS[0] else after, name=f"wait_w_{group}")
        if group in ("ffn0", "ffn1"):
            return dict(w1=got[0], w2=got[1])
        if group == "gla":
            qkvg = 3 * d
            w_in = _undo_column_split(got[0])
            return dict(gla_w_qkvg=w_in[:, :qkvg], gla_w_r=jnp.pad(w_in[:, qkvg:], ((0, 0), (0, GATE_PAD - GATE_RANK))),
                        gla_w_out=got[1].reshape(d, d), gate_w=self.gate_w, gate_b=self.gate_b, head_g=self.head_g)
        shapes = [s.shape for s in self.small_shards]
        parts = [_unpack(got[2][dev], shapes) for dev in range(N_DEV)]
        meta, conv_w, gate_w, self.gate_b, self.head_g = [
            jnp.concatenate([parts[dev][k] for dev in range(N_DEV)], axis=-1) for k in range(len(shapes))]
        self.gate_w = jnp.pad(gate_w[0], ((0, GATE_PAD - GATE_RANK), (0, 0))).astype(BF16)
        return dict(cp_w_in=_undo_column_split(got[0]), cp_w_out=got[1].reshape(d, d), meta=meta,
                    conv_w=jnp.pad(conv_w[0], ((0, 1), (0, 0))))

    def emit(self, group, g):
        d = self.d
        if group in ("ffn0", "ffn1"):
            arrs = [g["w1"], g["w2"]]
        elif group == "gla":
            w_in = jnp.concatenate([g["gla_w_qkvg"], g["gla_w_r"][:, :GATE_RANK]], axis=1)
            arrs = [_column_split(w_in), g["gla_w_out"].reshape(N_DEV, d // N_DEV, d)]
        else:
            s = dict(g["small"])
            s.update(pool_w=s["pool_w"][None], conv_w=s["conv_w"][None, :CONV_WIDTH], gate_w=s["gate_w"][None, :GATE_RANK])
            self.small_grads = [s[n] for n in _SMALL_GRADS]
            self.small_own = _pack(self.small_grads)
            self.small_sent, self.token = _exchange_start(
                _plan_to_all, [self.small_own], [lax.empty((N_DEV,) + self.small_own.shape, F32)], after=self.token,
                name="start_g_small")
            arrs = [_column_split(g["cp_w_in"]), g["cp_w_out"].reshape(N_DEV, d // N_DEV, d)]
        state, self.token = _exchange_start(_plan_to_sibling, arrs, None, after=self.token, name=f"start_g1_{group}")
        self.to_sibling.append((group, state, arrs))
        return self.token

    def poll(self, after):
        for group, state, arrs in self.to_sibling:
            landed = _exchange_wait(state, after, name=f"wait_g1_{group}")
            sums = [_pair_add(a, l, name=f"chip_sum_{group}_{k}") for k, (a, l) in enumerate(zip(arrs, landed))]
            state2, self.token = _exchange_start(_plan_to_chips, sums, None, after=self.token, name=f"start_g2_{group}")
            self.to_chips[group] = (state2, sums)
            after = self.token
        self.to_sibling = []
        return self.token

    def finish(self, w, mom, var):
        out = {}
        self.poll(self.token)
        after = self.token

        def landed(group):
            state, sums = self.to_chips[group]
            got = _exchange_wait(state, after, name=f"wait_g2_{group}")
            return list(zip(sums, got))

        def adam(n, parts):
            out[n] = _reduce_adam(parts, w[n], mom[n], var[n], name=f"adam_{n}")
            return out[n][0]

        ffn1 = landed("ffn1")
        after = ffn1[0][1]
        gla = landed("gla")
        after = gla[0][1]
        ffn0 = landed("ffn0")
        adam("ffn_w1", [ffn0[0], ffn1[0]])
        adam("ffn_w2", [ffn0[1], ffn1[1]])
        adam("gla_w_in", [gla[0]])
        after = adam("gla_w_out", [gla[1]])
        cp = landed("cp")
        adam("cp_w_in", [cp[0]])
        after = adam("cp_w_out", [cp[1]])
        small_all, = _exchange_wait(self.small_sent, after, name="wait_g_small")

        names = _REPLICATED + _SMALL_SHARDED
        small_sum = _unpack(_reduce8(self.small_own, small_all, name="sum_small_grads"), [a.shape for a in self.small_grads])
        me = _my_slot()
        grad = {}
        for n, full in zip(names, small_sum):
            if n in _SMALL_SHARDED:
                width = w[n].shape[-1]
                full = lax.dynamic_slice_in_dim(full, me * width, width, axis=full.ndim - 1)
            grad[n] = full
        packed = [_pack([t[n] for n in names]) for t in (w, grad, mom, var)]
        small_out = [_unpack(p, [w[n].shape for n in names]) for p in _adamw_small(*packed, name="adam_small")]
        for k, n in enumerate(names):
            out[n] = (grad[n], small_out[0][k], small_out[1][k], small_out[2][k])
        return out


def kernel(x, meta_tokens, mix_norm_g, ffn_norm_g, ffn_w1, ffn_w2, cp_w_in, cp_conv_w, cp_conv_b, cp_ln_g, cp_ln_b, cp_pool_w, cp_pool_scale, cp_w_out, gla_w_in, gla_gate_w2, gla_gate_b, gla_head_g, gla_w_out, final_norm_g, loss_target, m_meta_tokens, m_mix_norm_g, m_ffn_norm_g, m_ffn_w1, m_ffn_w2, m_cp_w_in, m_cp_conv_w, m_cp_conv_b, m_cp_ln_g, m_cp_ln_b, m_cp_pool_w, m_cp_pool_scale, m_cp_w_out, m_gla_w_in, m_gla_gate_w2, m_gla_gate_b, m_gla_head_g, m_gla_w_out, m_final_norm_g, v_meta_tokens, v_mix_norm_g, v_ffn_norm_g, v_ffn_w1, v_ffn_w2, v_cp_w_in, v_cp_conv_w, v_cp_conv_b, v_cp_ln_g, v_cp_ln_b, v_cp_pool_w, v_cp_pool_scale, v_cp_w_out, v_gla_w_in, v_gla_gate_w2, v_gla_gate_b, v_gla_head_g, v_gla_w_out, v_final_norm_g):
    w = dict(meta_tokens=meta_tokens, mix_norm_g=mix_norm_g, ffn_norm_g=ffn_norm_g, ffn_w1=ffn_w1, ffn_w2=ffn_w2,
             cp_w_in=cp_w_in, cp_conv_w=cp_conv_w, cp_conv_b=cp_conv_b, cp_ln_g=cp_ln_g, cp_ln_b=cp_ln_b,
             cp_pool_w=cp_pool_w, cp_pool_scale=cp_pool_scale, cp_w_out=cp_w_out, gla_w_in=gla_w_in,
             gla_gate_w2=gla_gate_w2, gla_gate_b=gla_gate_b, gla_head_g=gla_head_g, gla_w_out=gla_w_out,
             final_norm_g=final_norm_g.reshape(1, -1))
    mom = dict(meta_tokens=m_meta_tokens, mix_norm_g=m_mix_norm_g, ffn_norm_g=m_ffn_norm_g, ffn_w1=m_ffn_w1, ffn_w2=m_ffn_w2,
               cp_w_in=m_cp_w_in, cp_conv_w=m_cp_conv_w, cp_conv_b=m_cp_conv_b, cp_ln_g=m_cp_ln_g, cp_ln_b=m_cp_ln_b,
               cp_pool_w=m_cp_pool_w, cp_pool_scale=m_cp_pool_scale, cp_w_out=m_cp_w_out, gla_w_in=m_gla_w_in,
               gla_gate_w2=m_gla_gate_w2, gla_gate_b=m_gla_gate_b, gla_head_g=m_gla_head_g, gla_w_out=m_gla_w_out,
               final_norm_g=m_final_norm_g.reshape(1, -1))
    var = dict(meta_tokens=v_meta_tokens, mix_norm_g=v_mix_norm_g, ffn_norm_g=v_ffn_norm_g, ffn_w1=v_ffn_w1, ffn_w2=v_ffn_w2,
               cp_w_in=v_cp_w_in, cp_conv_w=v_cp_conv_w, cp_conv_b=v_cp_conv_b, cp_ln_g=v_cp_ln_g, cp_ln_b=v_cp_ln_b,
               cp_pool_w=v_cp_pool_w, cp_pool_scale=v_cp_pool_scale, cp_w_out=v_cp_w_out, gla_w_in=v_gla_w_in,
               gla_gate_w2=v_gla_gate_w2, gla_gate_b=v_gla_gate_b, gla_head_g=v_gla_head_g, gla_w_out=v_gla_w_out,
               final_norm_g=v_final_norm_g.reshape(1, -1))
    d = x.shape[-1]
    replicated = dict(mix_g=w["mix_norm_g"], ffn_g=w["ffn_norm_g"], conv_b=w["cp_conv_b"], ln_g=w["cp_ln_g"],
                      ln_b=w["cp_ln_b"], pool_w=w["cp_pool_w"][0].astype(BF16), pool_scale=w["cp_pool_scale"],
                      final_g=w["final_norm_g"])
    exchanges = _Exchanges(w, d)
    loss_blk, grad_x, _ = _local_step(x[0], loss_target[0], replicated, exchanges)
    loss = lax.psum(loss_blk[0, 0], ("x", "y", "c"))
    out = exchanges.finish(w, mom, var)

    def leaf(n, k):
        a = out[n][k]
        return a.reshape(-1) if n == "final_norm_g" else a

    return (loss, grad_x[None], *[leaf(n, 0) for n in _NAMES], *[leaf(n, 1) for n in _NAMES],
            *[leaf(n, 2) for n in _NAMES], *[leaf(n, 3) for n in _NAMES])
```

```python
import functools

import jax
import jax.numpy as jnp
from jax import lax
from jax.experimental import pallas as pl
from jax.experimental.pallas import tpu as pltpu

F32, BF16 = jnp.float32, jnp.bfloat16
N_DEV = 8
CHUNK = 64
N_META = 16
PAD_ROWS = CHUNK - N_META
HALO = 32
EPS = 1e-5
CONV_WIDTH = 31
POOL_WINDOWS = (2, 4, 8, 16)
HEADS = 4
GATE_RANK = 16
GATE_NORM = 16.0
GATE_PAD = 128
ADAM_LR, ADAM_B1, ADAM_B2, ADAM_EPS, ADAM_WD, ADAM_STEP = 0.001, 0.9, 0.999, 1e-08, 0.01, 10
V7X_VMEM_LIMIT = 56 * 2 ** 20
LANE = 128
HI = lax.Precision.HIGHEST


def _cparams(*sem):
    return pltpu.CompilerParams(dimension_semantics=sem, vmem_limit_bytes=V7X_VMEM_LIMIT)


def _row_tile(t, cap):
    best = CHUNK
    for r in range(CHUNK, min(t, cap) + 1, CHUNK):
        if t % r == 0:
            best = r
    return best


def _resident(shape):
    return pl.BlockSpec(shape, lambda *_: (0,) * len(shape), pipeline_mode=pl.Buffered(1))


def _dot(a, b):
    return jnp.dot(a, b, preferred_element_type=F32)


def _dot_nt(a, b):
    return lax.dot_general(a, b, (((1,), (1,)), ((), ())), preferred_element_type=F32)


def _dot_tn(a, b):
    return lax.dot_general(a, b, (((0,), (0,)), ((), ())), preferred_element_type=F32)


def _rowsum(a):
    return jnp.sum(a, axis=0, keepdims=True)


def _sigmoid(a):
    return 1.0 / (1.0 + jnp.exp(-a))


def _row_ids(tile, rt):
    return tile * rt + lax.broadcasted_iota(jnp.int32, (rt, 1), 0)


def _sds(shape, dtype):
    return jax.ShapeDtypeStruct(shape, dtype)


def _linear_fwd(x, w, *, gain=None, res=None, out_dtype=F32, name):
    t, k = x.shape
    n = w.shape[1]
    rt = _row_tile(t, 320)

    def body(*refs):
        refs = list(refs)
        x_ref, w_ref = refs[:2]
        pos = 2
        g_ref = r_ref = u_ref = None
        if gain is not None:
            g_ref = refs[pos]
            pos += 1
        if res is not None:
            r_ref = refs[pos]
            pos += 1
        y_ref = refs[pos]
        if gain is not None:
            u_ref = refs[pos + 1]
            xv = x_ref[...]
            u = (xv * lax.rsqrt(jnp.mean(xv * xv, axis=-1, keepdims=True) + EPS) * g_ref[...]).astype(BF16)
            u_ref[...] = u
        else:
            u = x_ref[...]
        y = _dot(u, w_ref[...])
        if res is not None:
            y = y + r_ref[...]
        y_ref[...] = y.astype(y_ref.dtype)

    rows = lambda i: (i, 0)
    in_specs = [pl.BlockSpec((rt, k), rows), _resident((k, n))]
    args = [x, w]
    if gain is not None:
        in_specs.append(_resident((1, k)))
        args.append(gain)
    if res is not None:
        in_specs.append(pl.BlockSpec((rt, n), rows))
        args.append(res)
    out_shape = [_sds((t, n), out_dtype)]
    out_specs = [pl.BlockSpec((rt, n), rows)]
    if gain is not None:
        out_shape.append(_sds((t, k), BF16))
        out_specs.append(pl.BlockSpec((rt, k), rows))
    out = pl.pallas_call(body, grid=(t // rt,), in_specs=in_specs, out_specs=out_specs, out_shape=out_shape,
                         compiler_params=_cparams("parallel"), name=name)(*args)
    return out if gain is not None else out[0]


def _linear_bwd_x(dys, ws, *, norm=None, after=None, name):
    t = dys[0].shape[0]
    k = ws[0].shape[0]
    rt = _row_tile(t, 320)
    nd = len(dys)
    dep_specs, deps = _dep_specs(after)

    def body(*refs):
        refs = list(refs)[len(deps):]
        dy_refs, w_refs = refs[:nd], refs[nd:2 * nd]
        dx = None
        for dy_ref, w_ref in zip(dy_refs, w_refs):
            part = _dot_nt(dy_ref[...].astype(BF16), w_ref[...])
            dx = part if dx is None else dx + part
        if norm is None:
            refs[2 * nd][...] = dx
            return
        h_ref, g_ref, dres_ref, dh_ref, dg_ref = refs[2 * nd:]
        hv = h_ref[...]
        rstd = lax.rsqrt(jnp.mean(hv * hv, axis=-1, keepdims=True) + EPS)
        xh = hv * rstd

        @pl.when(pl.program_id(0) == 0)
        def _():
            dg_ref[...] = jnp.zeros_like(dg_ref)

        dg_ref[...] += _rowsum(dx * xh)
        dxh = dx * g_ref[...]
        dh_ref[...] = dres_ref[...] + rstd * (dxh - xh * jnp.mean(dxh * xh, axis=-1, keepdims=True))

    rows = lambda i: (i, 0)
    in_specs = [pl.BlockSpec((rt, dy.shape[1]), rows) for dy in dys] + [_resident(w.shape) for w in ws]
    args = list(dys) + list(ws)
    out_shape = [_sds((t, k), F32)]
    out_specs = [pl.BlockSpec((rt, k), rows)]
    if norm is not None:
        h, gain, dres = norm
        in_specs += [pl.BlockSpec((rt, k), rows), _resident((1, k)), pl.BlockSpec((rt, k), rows)]
        args += [h, gain, dres]
        out_shape.append(_sds((1, k), F32))
        out_specs.append(pl.BlockSpec((1, k), lambda i: (0, 0)))
    out = pl.pallas_call(body, grid=(t // rt,), in_specs=dep_specs + in_specs, out_specs=out_specs, out_shape=out_shape,
                         compiler_params=_cparams("arbitrary"), name=name)(*deps, *args)
    return out if norm is not None else out[0]


def _linear_bwd_w(x, dy, *, name):
    t, k = x.shape
    n = dy.shape[1]
    rt = _row_tile(t, 832)
    nt = max(c for c in (512, 384, 256, LANE) if n % c == 0)

    def body(x_ref, dy_ref, o_ref):
        @pl.when(pl.program_id(1) == 0)
        def _():
            o_ref[...] = jnp.zeros_like(o_ref)

        o_ref[...] += _dot_tn(x_ref[...], dy_ref[...].astype(BF16))

    return pl.pallas_call(
        body, grid=(n // nt, t // rt),
        in_specs=[pl.BlockSpec((rt, k), lambda j, i: (i, 0)), pl.BlockSpec((rt, nt), lambda j, i: (i, j))],
        out_specs=pl.BlockSpec((k, nt), lambda j, i: (0, j)), out_shape=_sds((k, n), F32),
        compiler_params=_cparams("parallel", "arbitrary"), name=name)(x, dy)


def _ffn_fwd(h, gain, w1g, w2g, *, name):
    t, d = h.shape
    f8 = w1g.shape[-1]
    rt = _row_tile(t, 832)

    def body(h_ref, g_ref, w1_ref, w2_ref, o_ref, u_ref, acc_ref):
        j = pl.program_id(1)

        @pl.when(j == 0)
        def _():
            hv = h_ref[...]
            u_ref[...] = (hv * lax.rsqrt(jnp.mean(hv * hv, axis=-1, keepdims=True) + EPS) * g_ref[...]).astype(BF16)
            acc_ref[...] = jnp.zeros_like(acc_ref)

        a = jnp.maximum(_dot(u_ref[...], w1_ref[...]), 0.0)
        acc_ref[...] += _dot((a * a).astype(BF16), w2_ref[...])

        @pl.when(j == N_DEV - 1)
        def _():
            o_ref[...] = h_ref[...] + acc_ref[...]

    return pl.pallas_call(
        body, grid=(t // rt, N_DEV),
        in_specs=[pl.BlockSpec((rt, d), lambda i, j: (i, 0)), _resident((1, d)),
                  pl.BlockSpec((None, d, f8), lambda i, j: (j, 0, 0)),
                  pl.BlockSpec((None, f8, d), lambda i, j: (j, 0, 0))],
        out_specs=pl.BlockSpec((rt, d), lambda i, j: (i, 0)), out_shape=_sds((t, d), F32),
        scratch_shapes=[pltpu.VMEM((rt, d), BF16), pltpu.VMEM((rt, d), F32)],
        compiler_params=_cparams("parallel", "arbitrary"), name=name)(h, gain, w1g, w2g)


def _ffn_bwd(h, dout, gain, w1g, w2g, *, after=None, name):
    t, d = h.shape
    f8 = w1g.shape[-1]
    rt = _row_tile(t, 320)
    nrow = t // rt
    last = N_DEV - 1
    dep_specs, deps = _dep_specs(after)

    def body(*refs):
        h_ref, do_ref, g_ref, w1_ref, w2_ref, dh_ref, dw1_ref, dw2_ref, dg_ref, du_ref = refs[len(deps):]
        j, i = pl.program_id(0), pl.program_id(1)
        hv = h_ref[...]
        rstd = lax.rsqrt(jnp.mean(hv * hv, axis=-1, keepdims=True) + EPS)
        xh = hv * rstd
        u = (xh * g_ref[...]).astype(BF16)
        db = do_ref[...].astype(BF16)
        r = jnp.maximum(_dot(u, w1_ref[...]), 0.0)
        act = (r * r).astype(BF16)
        dhh = (_dot_nt(db, w2_ref[...]) * (2.0 * r)).astype(BF16)

        @pl.when(i == 0)
        def _():
            dw1_ref[...] = jnp.zeros_like(dw1_ref)
            dw2_ref[...] = jnp.zeros_like(dw2_ref)

        dw1_ref[...] += _dot_tn(u, dhh)
        dw2_ref[...] += _dot_tn(act, db)
        du_part = _dot_nt(dhh, w1_ref[...])
        rows = pl.ds(pl.multiple_of(i * rt, CHUNK), rt)

        @pl.when(j == 0)
        def _():
            du_ref[rows, :] = du_part

        @pl.when(j > 0)
        def _():
            du_ref[rows, :] += du_part

        @pl.when(j == last)
        def _():
            @pl.when(i == 0)
            def _():
                dg_ref[...] = jnp.zeros_like(dg_ref)

            du = du_ref[rows, :]
            dg_ref[...] += _rowsum(du * xh)
            dxh = du * g_ref[...]
            dh_ref[...] = do_ref[...] + rstd * (dxh - xh * jnp.mean(dxh * xh, axis=-1, keepdims=True))

    return pl.pallas_call(
        body, grid=(N_DEV, nrow),
        in_specs=dep_specs + [
                  pl.BlockSpec((rt, d), lambda j, i: (i, 0)), pl.BlockSpec((rt, d), lambda j, i: (i, 0)),
                  _resident((1, d)),
                  pl.BlockSpec((None, d, f8), lambda j, i: (j, 0, 0)),
                  pl.BlockSpec((None, f8, d), lambda j, i: (j, 0, 0))],
        out_specs=[pl.BlockSpec((rt, d), lambda j, i: (jnp.where(j == last, i, 0), 0)),
                   pl.BlockSpec((None, d, f8), lambda j, i: (j, 0, 0)),
                   pl.BlockSpec((None, f8, d), lambda j, i: (j, 0, 0)),
                   pl.BlockSpec((1, d), lambda j, i: (0, 0))],
        out_shape=[_sds((t, d), F32), _sds((N_DEV, d, f8), F32), _sds((N_DEV, f8, d), F32), _sds((1, d), F32)],
        scratch_shapes=[pltpu.VMEM((t, d), F32)],
        compiler_params=_cparams("arbitrary", "arbitrary"), name=name)(*deps, h, dout, gain, w1g, w2g)


def _lane_blocks(width):
    lb = min(LANE, width)
    return [slice(s, s + lb) for s in range(0, width, lb)]


def _conv_rows(src_ref, w_ref, offset, dst_ref, nblk, width, bias_ref=None):
    def blk(rb, carry):
        base = pl.multiple_of(rb * CHUNK, CHUNK)
        for l, ls in enumerate(_lane_blocks(width)):
            acc = jnp.zeros((CHUNK, ls.stop - ls.start), F32)
            if bias_ref is not None:
                acc = acc + bias_ref[:, ls]
            for k in range(CONV_WIDTH):
                acc = acc + w_ref[k:k + 1, ls] * src_ref[l, pl.ds(base + offset(k), CHUNK), :]
            dst_ref[l, pl.ds(base, CHUNK), :] = acc
        return carry

    lax.fori_loop(0, nblk, blk, 0)


def _to_lane_blocks(ref, row0, value):
    for l, ls in enumerate(_lane_blocks(value.shape[1])):
        ref[l, row0:row0 + value.shape[0], :] = value[:, ls]


def _from_lane_blocks(ref):
    return jnp.concatenate([ref[l] for l in range(ref.shape[0])], axis=1)


def _pool_counts(rows, window):
    return jnp.clip(rows - PAD_ROWS + 1, 1, window).astype(F32)


def _trailing_sum(v, window):
    s, sh = v, 1
    while sh < window:
        s = s + pltpu.roll(s, sh, 0)
        sh *= 2
    return s


def _leading_sum(v, window):
    s, sh, n = v, 1, v.shape[0]
    while sh < window:
        s = s + pltpu.roll(s, n - sh, 0)
        sh *= 2
    return s


def _cp_mid_fwd(z, conv_w, conv_b, ln_g, ln_b, pool_w, pool_scale, *, name):
    t, ein = z.shape
    cd = conv_b.shape[1]
    pd = pool_scale.shape[1]
    pg = pd // len(POOL_WINDOWS)
    rt = _row_tile(t, 320)

    def body(z_ref, cw_ref, cb_ref, lg_ref, lb_ref, pw_ref, ps_ref, o_ref, gext, pext, conv_s):
        i = pl.program_id(0)

        @pl.when(i == 0)
        def _():
            _to_lane_blocks(gext, 0, jnp.zeros((HALO, cd), F32))
            pext[0:HALO, :] = jnp.zeros((HALO, pd), F32)

        _to_lane_blocks(gext, HALO, z_ref[:, 0:cd] * _sigmoid(z_ref[:, cd:2 * cd]))
        pext[HALO:HALO + rt, :] = z_ref[:, 2 * cd:]
        _conv_rows(gext, cw_ref, lambda k: k + HALO - (CONV_WIDTH - 1), conv_s, rt // CHUNK, cd, cb_ref)
        cv = _from_lane_blocks(conv_s)
        xc = cv - jnp.mean(cv, axis=-1, keepdims=True)
        y = xc * lax.rsqrt(jnp.mean(xc * xc, axis=-1, keepdims=True) + EPS) * lg_ref[...] + lb_ref[...]
        rows = _row_ids(i, rt)
        a = jnp.where(rows >= PAD_ROWS, y * _sigmoid(y), 0.0)
        o_ref[:, 0:cd] = a.astype(BF16)
        for gi, window in enumerate(POOL_WINDOWS):
            ls = slice(gi * pg, (gi + 1) * pg)
            v = pext[:, ls]
            tm = _trailing_sum(v, window)[HALO:] / _pool_counts(rows, window) - v[HALO:]
            p = _dot(tm.astype(BF16), pw_ref[gi]) * ps_ref[:, ls]
            o_ref[:, cd + gi * pg:cd + (gi + 1) * pg] = p.astype(BF16)
        gext[:, 0:HALO, :] = gext[:, rt:rt + HALO, :]
        pext[0:HALO, :] = pext[rt:rt + HALO, :]

    nl, lb = len(_lane_blocks(cd)), min(LANE, cd)
    return pl.pallas_call(
        body, grid=(t // rt,),
        in_specs=[pl.BlockSpec((rt, ein), lambda i: (i, 0)), _resident(conv_w.shape), _resident((1, cd)),
                  _resident((1, cd)), _resident((1, cd)), _resident(pool_w.shape), _resident((1, pd))],
        out_specs=pl.BlockSpec((rt, cd + pd), lambda i: (i, 0)), out_shape=_sds((t, cd + pd), BF16),
        scratch_shapes=[pltpu.VMEM((nl, rt + HALO, lb), F32), pltpu.VMEM((rt + HALO, pd), F32),
                        pltpu.VMEM((nl, rt, lb), F32)],
        compiler_params=_cparams("arbitrary"), name=name)(z, conv_w, conv_b, ln_g, ln_b, pool_w, pool_scale)


def _cp_mid_bwd(z, dcat, conv_w, conv_b, ln_g, ln_b, pool_w, pool_scale, *, after=None, name):
    t, ein = z.shape
    cd = conv_b.shape[1]
    pd = pool_scale.shape[1]
    pg = pd // len(POOL_WINDOWS)
    rt = _row_tile(t, 320)
    ntile = t // rt
    per = rt // CHUNK
    dep_specs, deps = _dep_specs(after)

    def body(*refs):
        (z_ref, zh_ref, dc_ref, cw_ref, cb_ref, lg_ref, lb_ref, pw_ref, ps_ref,
         dz_ref, dcw_ref, dcb_ref, dlg_ref, dlb_ref, dpw_ref, dps_ref, gext, pext, conv_s, dcv, dsp) = refs[len(deps):]
        step = pl.program_id(0)
        tile = ntile - 1 - step

        @pl.when(step == 0)
        def _():
            for ref in (dcw_ref, dcb_ref, dlg_ref, dlb_ref, dpw_ref, dps_ref):
                ref[...] = jnp.zeros_like(ref)
            _to_lane_blocks(dcv, rt, jnp.zeros((HALO, cd), F32))
            dsp[rt:rt + HALO, :] = jnp.zeros((HALO, pd), F32)

        keep = jnp.where(tile > 0, 1.0, 0.0)
        zh = zh_ref[CHUNK - HALO:CHUNK, :]
        _to_lane_blocks(gext, 0, keep * zh[:, 0:cd] * _sigmoid(zh[:, cd:2 * cd]))
        pext[0:HALO, :] = keep * zh[:, 2 * cd:]
        za = z_ref[:, 0:cd]
        sg = _sigmoid(z_ref[:, cd:2 * cd])
        _to_lane_blocks(gext, HALO, za * sg)
        pext[HALO:HALO + rt, :] = z_ref[:, 2 * cd:]
        _conv_rows(gext, cw_ref, lambda k: k + HALO - (CONV_WIDTH - 1), conv_s, per, cd, cb_ref)
        cv = _from_lane_blocks(conv_s)
        xc = cv - jnp.mean(cv, axis=-1, keepdims=True)
        rstd = lax.rsqrt(jnp.mean(xc * xc, axis=-1, keepdims=True) + EPS)
        xh = xc * rstd
        y = xh * lg_ref[...] + lb_ref[...]
        sy = _sigmoid(y)
        rows = _row_ids(tile, rt)
        da = jnp.where(rows >= PAD_ROWS, dc_ref[:, 0:cd], 0.0)
        dy = da * (sy * (1.0 + y * (1.0 - sy)))
        dlg_ref[...] += _rowsum(dy * xh)
        dlb_ref[...] += _rowsum(dy)
        dxh = dy * lg_ref[...]
        dconv = rstd * (dxh - jnp.mean(dxh, axis=-1, keepdims=True) - xh * jnp.mean(dxh * xh, axis=-1, keepdims=True))
        dcb_ref[...] += _rowsum(dconv)
        _to_lane_blocks(dcv, 0, dconv)
        for l, ls in enumerate(_lane_blocks(cd)):
            for k in range(CONV_WIDTH):
                def acc_rows(rb, acc, l=l, k=k):
                    base = pl.multiple_of(rb * CHUNK, CHUNK)
                    return acc + dcv[l, pl.ds(base, CHUNK), :] * gext[l, pl.ds(base + k + HALO - (CONV_WIDTH - 1), CHUNK), :]

                acc = lax.fori_loop(0, per, acc_rows, jnp.zeros((CHUNK, ls.stop - ls.start), F32))
                dcw_ref[k:k + 1, ls] += _rowsum(acc)
        _conv_rows(dcv, cw_ref, lambda k: CONV_WIDTH - 1 - k, conv_s, per, cd)
        dglu = _from_lane_blocks(conv_s)
        dz_ref[:, 0:cd] = (dglu * sg).astype(BF16)
        dz_ref[:, cd:2 * cd] = (dglu * za * sg * (1.0 - sg)).astype(BF16)
        dcv[:, rt:rt + HALO, :] = dcv[:, 0:HALO, :]
        for gi, window in enumerate(POOL_WINDOWS):
            ls = slice(gi * pg, (gi + 1) * pg)
            v = pext[:, ls]
            cnt = _pool_counts(rows, window)
            tm = (_trailing_sum(v, window)[HALO:] / cnt - v[HALO:]).astype(BF16)
            dp = dc_ref[:, cd + gi * pg:cd + (gi + 1) * pg]
            dps_ref[:, ls] += _rowsum(dp * _dot(tm, pw_ref[gi]))
            dpl = (dp * ps_ref[:, ls]).astype(BF16)
            dpw_ref[gi] += _dot_tn(tm, dpl)
            dtm = _dot_nt(dpl, pw_ref[gi])
            dsp[0:rt, ls] = dtm / cnt
            dpin = _leading_sum(dsp[:, ls], window)[0:rt] - dtm
            dz_ref[:, 2 * cd + gi * pg:2 * cd + (gi + 1) * pg] = dpin.astype(BF16)
        dsp[rt:rt + HALO, :] = dsp[0:HALO, :]

    back = lambda i: (ntile - 1 - i, 0)
    halo_idx = lambda i: (jnp.maximum((ntile - 1 - i) * per - 1, 0), 0)
    const2 = lambda i: (0, 0)
    nl, lb = len(_lane_blocks(cd)), min(LANE, cd)
    return pl.pallas_call(
        body, grid=(ntile,),
        in_specs=dep_specs + [
                  pl.BlockSpec((rt, ein), back), pl.BlockSpec((CHUNK, ein), halo_idx), pl.BlockSpec((rt, cd + pd), back),
                  _resident(conv_w.shape), _resident((1, cd)), _resident((1, cd)), _resident((1, cd)),
                  _resident(pool_w.shape), _resident((1, pd))],
        out_specs=[pl.BlockSpec((rt, ein), back), pl.BlockSpec(conv_w.shape, const2), pl.BlockSpec((1, cd), const2),
                   pl.BlockSpec((1, cd), const2), pl.BlockSpec((1, cd), const2),
                   pl.BlockSpec(pool_w.shape, lambda i: (0, 0, 0)), pl.BlockSpec((1, pd), const2)],
        out_shape=[_sds((t, ein), BF16), _sds(conv_w.shape, F32), _sds((1, cd), F32), _sds((1, cd), F32),
                   _sds((1, cd), F32), _sds(pool_w.shape, F32), _sds((1, pd), F32)],
        scratch_shapes=[pltpu.VMEM((nl, rt + HALO, lb), F32), pltpu.VMEM((rt + HALO, pd), F32), pltpu.VMEM((nl, rt, lb), F32),
                        pltpu.VMEM((nl, rt + HALO, lb), F32), pltpu.VMEM((rt + HALO, pd), F32)],
        compiler_params=_cparams("arbitrary"), name=name)(*deps, z, z, dcat, conv_w, conv_b, ln_g, ln_b, pool_w, pool_scale)


def _log_decay(r_ref, gw_ref, gb_ref, rows):
    gp = _dot(r_ref[...].astype(BF16), gw_ref[...]) + gb_ref[...]
    log_sig = jnp.minimum(gp, 0.0) - jnp.log(1.0 + jnp.exp(-jnp.abs(gp)))
    return gp, jnp.where(rows >= PAD_ROWS, log_sig / GATE_NORM, 0.0)


def _tri(strict):
    r = lax.broadcasted_iota(jnp.int32, (CHUNK, CHUNK), 0)
    c = lax.broadcasted_iota(jnp.int32, (CHUNK, CHUNK), 1)
    return jnp.where(c < r if strict else c <= r, 1.0, 0.0).astype(F32)


def _gla_mid_fwd(z, r, gate_w, gate_b, head_g, *, name):
    t = z.shape[0]
    dk = gate_b.shape[1]
    hv = head_g.shape[1]
    hk = dk // HEADS
    dv = hv * HEADS
    rt = _row_tile(t, 320)
    per = rt // CHUNK
    scale = hk ** -0.5

    def body(z_ref, r_ref, gw_ref, gb_ref, hg_ref, o_ref, st_ref, s_ref, la_ref):
        i = pl.program_id(0)

        @pl.when(i == 0)
        def _():
            s_ref[...] = jnp.zeros_like(s_ref)

        _, la = _log_decay(r_ref, gw_ref, gb_ref, _row_ids(i, rt))
        la_ref[...] = la
        tri = _tri(False)

        def chunk(c, carry):
            rows = pl.ds(pl.multiple_of(c * CHUNK, CHUNK), CHUNK)
            la_c = la_ref[rows, :]
            cum = jnp.dot(tri, la_c, precision=HI, preferred_element_type=F32)
            tot = _rowsum(la_c)
            dec = jnp.exp(tot - cum)
            etot = jnp.exp(tot)
            for hd in range(HEADS):
                ks = slice(hd * hk, (hd + 1) * hk)
                q = z_ref[rows, hd * hk:(hd + 1) * hk] * scale
                kd = z_ref[rows, dk + hd * hk:dk + (hd + 1) * hk] * dec[:, ks]
                v = z_ref[rows, 2 * dk + hd * hv:2 * dk + (hd + 1) * hv]
                g = z_ref[rows, 2 * dk + dv + hd * hv:2 * dk + dv + (hd + 1) * hv]
                s_new = s_ref[hd] * etot[:, ks] + _dot_tn(v.astype(BF16), kd.astype(BF16))
                s_ref[hd] = s_new
                st_ref[c, hd] = s_new
                o = _dot_nt(q.astype(BF16), s_new.astype(BF16))
                on = o * lax.rsqrt(jnp.mean(o * o, axis=-1, keepdims=True) + EPS) * hg_ref[...]
                o_ref[rows, hd * hv:(hd + 1) * hv] = (on * (g * _sigmoid(g))).astype(BF16)
            return carry

        lax.fori_loop(0, per, chunk, 0)

    return pl.pallas_call(
        body, grid=(t // rt,),
        in_specs=[pl.BlockSpec((rt, z.shape[1]), lambda i: (i, 0)), pl.BlockSpec((rt, GATE_PAD), lambda i: (i, 0)),
                  _resident(gate_w.shape), _resident((1, dk)), _resident((1, hv))],
        out_specs=[pl.BlockSpec((rt, dv), lambda i: (i, 0)), pl.BlockSpec((per, HEADS, hv, hk), lambda i: (i, 0, 0, 0))],
        out_shape=[_sds((t, dv), BF16), _sds((t // CHUNK, HEADS, hv, hk), F32)],
        scratch_shapes=[pltpu.VMEM((HEADS, hv, hk), F32), pltpu.VMEM((rt, dk), F32)],
        compiler_params=_cparams("arbitrary"), name=name)(z, r, gate_w, gate_b, head_g)


def _gla_mid_bwd(z, r, dog, states, gate_w, gate_b, head_g, *, after=None, name):
    t = z.shape[0]
    dk = gate_b.shape[1]
    hv = head_g.shape[1]
    hk = dk // HEADS
    dv = hv * HEADS
    rt = _row_tile(t, 320)
    ntile = t // rt
    per = rt // CHUNK
    scale = hk ** -0.5
    dep_specs, deps = _dep_specs(after)

    def body(*refs):
        (z_ref, r_ref, do_ref, st_ref, stp_ref, gw_ref, gb_ref, hg_ref,
         dz_ref, dr_ref, dgw_ref, dgb_ref, dhg_ref, ds_ref, la_ref, dla_ref) = refs[len(deps):]
        step = pl.program_id(0)
        tile = ntile - 1 - step

        @pl.when(step == 0)
        def _():
            ds_ref[...] = jnp.zeros_like(ds_ref)
            dgw_ref[...] = jnp.zeros_like(dgw_ref)
            dgb_ref[...] = jnp.zeros_like(dgb_ref)
            dhg_ref[...] = jnp.zeros_like(dhg_ref)

        rows_id = _row_ids(tile, rt)
        gp, la = _log_decay(r_ref, gw_ref, gb_ref, rows_id)
        la_ref[...] = la
        tri, tri_strict = _tri(False), _tri(True)
        keep = jnp.where(tile > 0, 1.0, 0.0)

        def chunk(cc, carry):
            c = per - 1 - cc
            rows = pl.ds(pl.multiple_of(c * CHUNK, CHUNK), CHUNK)
            la_c = la_ref[rows, :]
            cum = jnp.dot(tri, la_c, precision=HI, preferred_element_type=F32)
            tot = _rowsum(la_c)
            dec = jnp.exp(tot - cum)
            etot = jnp.exp(tot)
            inside = jnp.where(c > 0, 1.0, 0.0)
            for hd in range(HEADS):
                ks = slice(hd * hk, (hd + 1) * hk)
                q = (z_ref[rows, hd * hk:(hd + 1) * hk] * scale).astype(BF16)
                k = z_ref[rows, dk + hd * hk:dk + (hd + 1) * hk]
                kd = k * dec[:, ks]
                v = z_ref[rows, 2 * dk + hd * hv:2 * dk + (hd + 1) * hv].astype(BF16)
                g = z_ref[rows, 2 * dk + dv + hd * hv:2 * dk + dv + (hd + 1) * hv]
                s_now = st_ref[c, hd]
                s_prev = inside * st_ref[jnp.maximum(c - 1, 0), hd] + (1.0 - inside) * keep * stp_ref[0, hd]
                s_b = s_now.astype(BF16)
                o = _dot_nt(q, s_b)
                rstd = lax.rsqrt(jnp.mean(o * o, axis=-1, keepdims=True) + EPS)
                oh = o * rstd
                sg = _sigmoid(g)
                d_og = do_ref[rows, hd * hv:(hd + 1) * hv]
                dz_ref[rows, 2 * dk + dv + hd * hv:2 * dk + dv + (hd + 1) * hv] = (
                    d_og * oh * hg_ref[...] * (sg * (1.0 + g * (1.0 - sg)))).astype(BF16)
                don = d_og * (g * sg)
                dhg_ref[...] += _rowsum(don * oh)
                doh = don * hg_ref[...]
                d_o = (rstd * (doh - oh * jnp.mean(doh * oh, axis=-1, keepdims=True))).astype(BF16)
                dz_ref[rows, hd * hk:(hd + 1) * hk] = (_dot(d_o, s_b) * scale).astype(BF16)
                ds_t = ds_ref[hd] + _dot_tn(d_o, q)
                ds_b = ds_t.astype(BF16)
                dkd = _dot(v, ds_b)
                dz_ref[rows, 2 * dk + hd * hv:2 * dk + (hd + 1) * hv] = _dot_nt(kd.astype(BF16), ds_b).astype(BF16)
                dtot = etot[:, ks] * _rowsum(ds_t * s_prev)
                ds_ref[hd] = ds_t * etot[:, ks]
                dz_ref[rows, dk + hd * hk:dk + (hd + 1) * hk] = (dkd * dec[:, ks]).astype(BF16)
                e = dkd * kd
                dla_ref[rows, ks] = dtot + jnp.dot(tri_strict, e, precision=HI, preferred_element_type=F32)
            return carry

        lax.fori_loop(0, per, chunk, 0)
        dla = jnp.where(rows_id >= PAD_ROWS, dla_ref[...], 0.0)
        dgp = dla * (1.0 / GATE_NORM) * (1.0 - _sigmoid(gp))
        dgb_ref[...] += _rowsum(dgp)
        dgp_b = dgp.astype(BF16)
        dgw_ref[...] += _dot_tn(r_ref[...].astype(BF16), dgp_b)
        dr_ref[...] = _dot_nt(dgp_b, gw_ref[...]).astype(BF16)

    back = lambda i: (ntile - 1 - i, 0)
    const2 = lambda i: (0, 0)
    return pl.pallas_call(
        body, grid=(ntile,),
        in_specs=dep_specs + [
                  pl.BlockSpec((rt, z.shape[1]), back), pl.BlockSpec((rt, GATE_PAD), back), pl.BlockSpec((rt, dv), back),
                  pl.BlockSpec((per, HEADS, hv, hk), lambda i: (ntile - 1 - i, 0, 0, 0)),
                  pl.BlockSpec((1, HEADS, hv, hk), lambda i: (jnp.maximum((ntile - 1 - i) * per - 1, 0), 0, 0, 0)),
                  _resident(gate_w.shape), _resident((1, dk)), _resident((1, hv))],
        out_specs=[pl.BlockSpec((rt, z.shape[1]), back), pl.BlockSpec((rt, GATE_PAD), back),
                   pl.BlockSpec(gate_w.shape, const2), pl.BlockSpec((1, dk), const2), pl.BlockSpec((1, hv), const2)],
        out_shape=[_sds(z.shape, BF16), _sds((t, GATE_PAD), BF16), _sds(gate_w.shape, F32), _sds((1, dk), F32),
                   _sds((1, hv), F32)],
        scratch_shapes=[pltpu.VMEM((HEADS, hv, hk), F32), pltpu.VMEM((rt, dk), F32), pltpu.VMEM((rt, dk), F32)],
        compiler_params=_cparams("arbitrary"), name=name)(*deps, z, r, dog, states, states, gate_w, gate_b, head_g)


def _head(h, gain, target, *, name):
    t, d = h.shape
    rt = _row_tile(t, 832)

    def body(h_ref, g_ref, t_ref, dh_ref, loss_ref, dg_ref):
        i = pl.program_id(0)

        @pl.when(i == 0)
        def _():
            loss_ref[...] = jnp.zeros_like(loss_ref)
            dg_ref[...] = jnp.zeros_like(dg_ref)

        hv = h_ref[...]
        rstd = lax.rsqrt(jnp.mean(hv * hv, axis=-1, keepdims=True) + EPS)
        xh = hv * rstd
        err = jnp.where(_row_ids(i, rt) >= CHUNK, xh * g_ref[...] - t_ref[...], 0.0)
        loss_ref[...] += (0.5 / d) * jnp.sum(err * err)
        dy = err * (1.0 / d)
        dg_ref[...] += _rowsum(dy * xh)
        dxh = dy * g_ref[...]
        dh_ref[...] = rstd * (dxh - xh * jnp.mean(dxh * xh, axis=-1, keepdims=True))

    return pl.pallas_call(
        body, grid=(t // rt,),
        in_specs=[pl.BlockSpec((rt, d), lambda i: (i, 0)), _resident((1, d)), pl.BlockSpec((rt, d), lambda i: (i, 0))],
        out_specs=[pl.BlockSpec((rt, d), lambda i: (i, 0)), pl.BlockSpec((8, LANE), lambda i: (0, 0)),
                   pl.BlockSpec((1, d), lambda i: (0, 0))],
        out_shape=[_sds((t, d), F32), _sds((8, LANE), F32), _sds((1, d), F32)],
        compiler_params=_cparams("arbitrary"), name=name)(h, gain, target)


def _adamw_math(w, g, m, v):
    m = ADAM_B1 * m + (1.0 - ADAM_B1) * g
    v = ADAM_B2 * v + (1.0 - ADAM_B2) * (g * g)
    m_hat = m / (1.0 - ADAM_B1 ** ADAM_STEP)
    v_hat = v / (1.0 - ADAM_B2 ** ADAM_STEP)
    return -ADAM_LR * (m_hat / (jnp.sqrt(v_hat) + ADAM_EPS) + ADAM_WD * w), m, v


N_CHIP = N_DEV // 2
BLOCK_ELEMS = 128 * 1024


def _my_slot():
    return 4 * lax.axis_index("x") + 2 * lax.axis_index("y") + lax.axis_index("c")


def _my_chip():
    return 2 * lax.axis_index("x") + lax.axis_index("y")


def _row_block(r, c):
    cap = max(8, BLOCK_ELEMS // (-(-c // LANE) * LANE))
    return max(b for b in range(8, r + 1, 8) if r % b == 0 and b <= max(cap, 8))


def _pair_add(a, landed, *, name):
    _, r, c = a.shape
    rb = _row_block(r, c)

    def body(a_ref, l_ref, o_ref):
        mine = jnp.where(lax.axis_index("c") == 0, a_ref[0], a_ref[1])
        o_ref[...] = mine + l_ref[...]

    one = pl.BlockSpec((None, rb, c), lambda q, i: (q, i, 0))
    return pl.pallas_call(
        body, grid=(N_CHIP, r // rb), in_specs=[pl.BlockSpec((2, rb, c), lambda q, i: (q, i, 0)), one], out_specs=one,
        out_shape=_sds((N_CHIP, r, c), F32), compiler_params=_cparams("parallel", "parallel"), name=name)(a, landed)


def _reduce_adam(parts, w, m, v, *, after=None, name):
    nl, r, c = w.shape
    rb = _row_block(r, c)
    dep_specs, deps = _dep_specs(after)

    def body(*refs):
        refs = refs[len(deps):]
        p_refs = refs[:2 * nl]
        w_ref, m_ref, v_ref, g_out, d_out, m_out, v_out = refs[2 * nl:]
        layer = pl.program_id(0)
        chip = _my_chip()
        for li in range(nl):
            @pl.when(layer == li)
            def _(li=li):
                mine_ref, land_ref = p_refs[2 * li], p_refs[2 * li + 1]
                g = None
                for q in range(N_CHIP):
                    term = jnp.where(chip == q, mine_ref[q], land_ref[q])
                    g = term if g is None else g + term
                g_out[...] = g
                d_out[...], m_out[...], v_out[...] = _adamw_math(w_ref[...], g, m_ref[...], v_ref[...])

    blk = pl.BlockSpec((None, rb, c), lambda l, i: (l, i, 0))
    p_specs = [pl.BlockSpec((N_CHIP, rb, c), lambda l, i, li=li: (0, jnp.where(l == li, i, 0), 0))
               for li in range(nl) for _ in range(2)]
    flat = [p for pair in parts for p in pair]
    return pl.pallas_call(
        body, grid=(nl, r // rb), in_specs=dep_specs + p_specs + [blk, blk, blk], out_specs=[blk] * 4,
        out_shape=[_sds(w.shape, F32)] * 4, compiler_params=_cparams("arbitrary", "arbitrary"),
        name=name)(*deps, *flat, w, m, v)


def _reduce8(own, landed, *, name):
    r, c = own.shape

    def body(own_ref, p_ref, o_ref):
        me = _my_slot()
        g = None
        for dev in range(N_DEV):
            term = jnp.where(me == dev, own_ref[...], p_ref[dev])
            g = term if g is None else g + term
        o_ref[...] = g

    return pl.pallas_call(body, out_shape=_sds((r, c), F32), name=name)(own, landed)


def _adamw_small(w, g, m, v, *, name):
    def body(w_ref, g_ref, m_ref, v_ref, d_out, m_out, v_out):
        d_out[...], m_out[...], v_out[...] = _adamw_math(w_ref[...], g_ref[...], m_ref[...], v_ref[...])

    return pl.pallas_call(body, out_shape=[_sds(w.shape, F32)] * 3, name=name)(w, g, m, v)


_HBM = pl.BlockSpec(memory_space=pltpu.HBM)
_SEM = pl.BlockSpec(memory_space=pltpu.SEMAPHORE)
_DATAFLOW = pltpu.SideEffectType.DATAFLOW_SIDE_EFFECTING


def _plan_to_all(src, land):
    x, y, c = lax.axis_index("x"), lax.axis_index("y"), lax.axis_index("c")
    return [(src, land.at[_my_slot()], (x ^ ((d >> 2) & 1), y ^ ((d >> 1) & 1), c ^ (d & 1))) for d in range(1, N_DEV)]


def _plan_to_sibling(src, land):
    x, y, c = lax.axis_index("x"), lax.axis_index("y"), lax.axis_index("c")
    return [(src.at[2 * q + 1 - c], land.at[q], (x, y, 1 - c)) for q in range(N_CHIP)]


def _plan_to_chips(src, land):
    x, y, c = lax.axis_index("x"), lax.axis_index("y"), lax.axis_index("c")
    peers = [(x ^ (d >> 1), y ^ (d & 1)) for d in range(1, N_CHIP)]
    return [(src.at[2 * px + py], land.at[_my_chip()], (px, py, c)) for px, py in peers]


_PLAN_COPIES = {_plan_to_all: N_DEV - 1, _plan_to_sibling: N_CHIP, _plan_to_chips: N_CHIP - 1}


def _exchange_copies(plan, ins, lands, send, recv):
    per = _PLAN_COPIES[plan]
    copies = []
    for a, (src, land) in enumerate(zip(ins, lands)):
        for i, (s, dst, dev) in enumerate(plan(src, land)):
            copies.append(pltpu.make_async_remote_copy(
                src_ref=s, dst_ref=dst, send_sem=send.at[a * per + i], recv_sem=recv.at[a * per + i],
                device_id=dev, device_id_type=pl.DeviceIdType.MESH))
    return copies


def _place_own(a, dtype, *, name):
    r, c = a.shape
    rb = _row_block(r, c)

    def body(me_ref, a_ref, o_ref):
        o_ref[...] = a_ref[...].astype(dtype)

    grid_spec = pltpu.PrefetchScalarGridSpec(
        num_scalar_prefetch=1, grid=(r // rb,), in_specs=[pl.BlockSpec((rb, c), lambda i, me: (i, 0))],
        out_specs=pl.BlockSpec((None, rb, c), lambda i, me: (me[0], i, 0)))
    return pl.pallas_call(body, grid_spec=grid_spec, out_shape=_sds((N_DEV, r, c), dtype),
                          compiler_params=_cparams("arbitrary"), name=name)(_my_slot().reshape(1), a)


def _plan_to_all_in_place(land, _):
    return _plan_to_all(land.at[_my_slot()], land)


_PLAN_COPIES[_plan_to_all_in_place] = N_DEV - 1


def _exchange_start(plan, arrs, lands, *, after=None, name):
    if lands is None:
        lands = [lax.empty((N_CHIP,) + a.shape[1:], a.dtype) for a in arrs]
    bufs = list(lands) if arrs is None else list(arrs) + list(lands)
    n, nb = len(lands), len(bufs)
    nsem = n * _PLAN_COPIES[plan]
    dep_specs, deps = _dep_specs(after)

    def body(*refs):
        ins, land_refs = refs[:n], refs[nb - n:nb]
        send, recv = refs[nb + len(deps)], refs[nb + len(deps) + 1]
        for cp in _exchange_copies(plan, ins, land_refs, send, recv):
            cp.start()
        refs[-1][...] = jnp.zeros_like(refs[-1])

    out = pl.pallas_call(
        body, name=name,
        out_shape=(pltpu.SemaphoreType.DMA((nsem,)), pltpu.SemaphoreType.DMA((nsem,)),
                   *[pltpu.HBM(a.shape, a.dtype) for a in bufs], _sds((8, LANE), F32)),
        in_specs=[_HBM] * nb + dep_specs,
        out_specs=(_SEM, _SEM, *([_HBM] * nb), pl.BlockSpec(memory_space=pltpu.VMEM)),
        input_output_aliases={i: 2 + i for i in range(nb)},
        compiler_params=pltpu.CompilerParams(has_side_effects=_DATAFLOW),
    )(*[pltpu.with_memory_space_constraint(a, pltpu.HBM) for a in bufs], *deps)
    return (plan, n, out[0], out[1], list(out[2:2 + nb])), out[-1]


def _exchange_wait(state, after, *, name):
    plan, n, send_sem, recv_sem, bufs = state
    nb = len(bufs)

    def body(*refs):
        ins, land_refs, send, recv = refs[:n], refs[nb - n:nb], refs[nb], refs[nb + 1]
        for cp in _exchange_copies(plan, ins, land_refs, send, recv):
            cp.wait_send()
            cp.wait_recv()

    out = pl.pallas_call(
        body, name=name, out_shape=[pltpu.HBM(a.shape, a.dtype) for a in bufs],
        in_specs=[_HBM] * nb + [_SEM, _SEM, pl.BlockSpec(memory_space=pl.ANY)], out_specs=[_HBM] * nb,
        input_output_aliases={i: i for i in range(nb)},
        compiler_params=pltpu.CompilerParams(has_side_effects=_DATAFLOW),
    )(*bufs, send_sem, recv_sem, after)
    return list(out[:n]), list(out[nb - n:])


def _dep_specs(after):
    return ([], []) if after is None else ([pl.BlockSpec(memory_space=pl.ANY)], [after])


def _pack(arrays):
    flat = jnp.concatenate([a.reshape(-1) for a in arrays])
    size = flat.shape[0]
    padded = -(-size // (8 * LANE)) * (8 * LANE)
    return jnp.pad(flat, (0, padded - size)).reshape(padded // LANE, LANE)


def _unpack(packed, shapes):
    flat = packed.reshape(-1)
    out, pos = [], 0
    for shp in shapes:
        size = 1
        for s in shp:
            size *= s
        out.append(flat[pos:pos + size].reshape(shp))
        pos += size
    return out


def _undo_column_split(g):
    return jnp.transpose(g, (1, 0, 2)).reshape(g.shape[1], N_DEV * g.shape[2])


def _column_split(a):
    r, c = a.shape
    return jnp.transpose(a.reshape(r, N_DEV, c // N_DEV), (1, 0, 2))


class _WholeWeights:
    def __init__(self, groups):
        self.groups = groups
        self.grads = {}

    def fetch(self, group, after):
        return self.groups[group]

    def emit(self, group, grads):
        self.grads.update(grads)
        return None

    def poll(self, after):
        return None


def _local_step(x, target, replicated, src):
    d = x.shape[1]
    mix_g, ffn_g = replicated["mix_g"], replicated["ffn_g"]
    cp = src.fetch("cp", x)
    h0 = jnp.concatenate([jnp.zeros((PAD_ROWS, d), F32), cp["meta"], x], axis=0)
    tgt = jnp.concatenate([jnp.zeros((CHUNK, d), F32), target], axis=0)
    cp_mid = (cp["conv_w"], replicated["conv_b"], replicated["ln_g"], replicated["ln_b"], replicated["pool_w"],
              replicated["pool_scale"])

    z0, u0 = _linear_fwd(h0, cp["cp_w_in"], gain=mix_g[0:1], name="cp_in")
    cat = _cp_mid_fwd(z0, *cp_mid, name="cp_mid")
    h1 = _linear_fwd(cat, cp["cp_w_out"], res=h0, name="cp_out")
    ffn0 = src.fetch("ffn0", h1)
    h2 = _ffn_fwd(h1, ffn_g[0:1], ffn0["w1"], ffn0["w2"], name="ffn0")
    gla = src.fetch("gla", h2)
    gla_mid = (gla["gate_w"], gla["gate_b"], gla["head_g"])
    z1, u1 = _linear_fwd(h2, gla["gla_w_qkvg"], gain=mix_g[1:2], name="gla_in")
    r1 = _linear_fwd(u1, gla["gla_w_r"], name="gla_in_r")
    og, states = _gla_mid_fwd(z1, r1, *gla_mid, name="gla_mid")
    h3 = _linear_fwd(og, gla["gla_w_out"], res=h2, name="gla_out")
    ffn1 = src.fetch("ffn1", h3)
    h4 = _ffn_fwd(h3, ffn_g[1:2], ffn1["w1"], ffn1["w2"], name="ffn1")
    dh4, loss, d_final_g = _head(h4, replicated["final_g"], tgt, name="head")

    dh3, dw1_1, dw2_1, dffn_g1 = _ffn_bwd(h3, dh4, ffn_g[1:2], ffn1["w1"], ffn1["w2"], name="ffn1_bwd")
    sent = src.emit("ffn1", dict(w1=dw1_1, w2=dw2_1))
    dog = _linear_bwd_x([dh3], [gla["gla_w_out"]], after=sent, name="gla_out_dx")
    d_gla_w_out = _linear_bwd_w(og, dh3, name="gla_out_dw")
    sent = src.poll(d_gla_w_out)
    dz1, dr1, d_gate_w, d_gate_b, d_head_g = _gla_mid_bwd(z1, r1, dog, states, *gla_mid, after=sent, name="gla_mid_bwd")
    dh2, dmix_g1 = _linear_bwd_x([dz1, dr1], [gla["gla_w_qkvg"], gla["gla_w_r"]],
                                 norm=(h2, mix_g[1:2], dh3), name="gla_in_dx")
    d_gla_w_qkvg = _linear_bwd_w(u1, dz1, name="gla_in_dw")
    d_gla_w_r = _linear_bwd_w(u1, dr1, name="gla_in_r_dw")
    sent = src.emit("gla", dict(gla_w_qkvg=d_gla_w_qkvg, gla_w_r=d_gla_w_r, gla_w_out=d_gla_w_out))
    dh1, dw1_0, dw2_0, dffn_g0 = _ffn_bwd(h1, dh2, ffn_g[0:1], ffn0["w1"], ffn0["w2"], after=sent, name="ffn0_bwd")
    src.poll(dw1_0)
    sent = src.emit("ffn0", dict(w1=dw1_0, w2=dw2_0))
    dcat = _linear_bwd_x([dh1], [cp["cp_w_out"]], after=sent, name="cp_out_dx")
    d_cp_w_out = _linear_bwd_w(cat, dh1, name="cp_out_dw")
    sent = src.poll(d_cp_w_out)
    dz0, d_conv_w, d_conv_b, d_ln_g, d_ln_b, d_pool_w, d_pool_scale = _cp_mid_bwd(z0, dcat, *cp_mid, after=sent,
                                                                                 name="cp_mid_bwd")
    dh0, dmix_g0 = _linear_bwd_x([dz0], [cp["cp_w_in"]], norm=(h0, mix_g[0:1], dh1), name="cp_in_dx")
    d_cp_w_in = _linear_bwd_w(u0, dz0, name="cp_in_dw")

    small = dict(
        mix_g=jnp.concatenate([dmix_g0, dmix_g1]), ffn_g=jnp.concatenate([dffn_g0, dffn_g1]), conv_b=d_conv_b, ln_g=d_ln_g,
        ln_b=d_ln_b, pool_w=d_pool_w, pool_scale=d_pool_scale, final_g=d_final_g, meta=dh0[PAD_ROWS:CHUNK], conv_w=d_conv_w,
        gate_w=d_gate_w, gate_b=d_gate_b, head_g=d_head_g)
    src.emit("cp", dict(cp_w_in=d_cp_w_in, cp_w_out=d_cp_w_out, small=small))
    return loss, dh0[CHUNK:], small


_REPLICATED = ("mix_norm_g", "ffn_norm_g", "cp_conv_b", "cp_ln_g", "cp_ln_b", "cp_pool_w", "cp_pool_scale", "final_norm_g")
_SMALL_SHARDED = ("meta_tokens", "cp_conv_w", "gla_gate_w2", "gla_gate_b", "gla_head_g")
_LARGE = ("ffn_w1", "ffn_w2", "cp_w_in", "cp_w_out", "gla_w_in", "gla_w_out")
_NAMES = ("meta_tokens", "mix_norm_g", "ffn_norm_g", "ffn_w1", "ffn_w2", "cp_w_in", "cp_conv_w", "cp_conv_b", "cp_ln_g",
          "cp_ln_b", "cp_pool_w", "cp_pool_scale", "cp_w_out", "gla_w_in", "gla_gate_w2", "gla_gate_b", "gla_head_g",
          "gla_w_out", "final_norm_g")
_SMALL_GRADS = ("mix_g", "ffn_g", "conv_b", "ln_g", "ln_b", "pool_w", "pool_scale", "final_g", "meta", "conv_w", "gate_w",
                "gate_b", "head_g")
_GROUPS = ("cp", "ffn0", "gla", "ffn1")


class _Exchanges:
    def __init__(self, w, d):
        self.d = d
        self.small_shards = [w[n] for n in _SMALL_SHARDED]
        shards = dict(
            cp=[(w["cp_w_in"][0], BF16), (w["cp_w_out"][0], BF16), (_pack(self.small_shards), F32)],
            ffn0=[(w["ffn_w1"][0], BF16), (w["ffn_w2"][0], BF16)],
            gla=[(w["gla_w_in"][0], BF16), (w["gla_w_out"][0], BF16)],
            ffn1=[(w["ffn_w1"][1], BF16), (w["ffn_w2"][1], BF16)])
        self.gathers = {}
        self.to_sibling, self.to_chips = [], {}
        token = None
        for group in _GROUPS:
            lands = [_place_own(a, dtype, name=f"place_w_{group}_{k}") for k, (a, dtype) in enumerate(shards[group])]
            self.gathers[group], token = _exchange_start(_plan_to_all_in_place, None, lands, after=token,
                                                         name=f"start_w_{group}")
        self.token = token

    def fetch(self, group, after):
        d = self.d
        _, got = _exchange_wait(self.gathers[group], self.token if group == _GROUPS[0] else after, name=f"wait_w_{group}")
        if group in ("ffn0", "ffn1"):
            return dict(w1=got[0], w2=got[1])
        if group == "gla":
            qkvg = 3 * d
            w_in = _undo_column_split(got[0])
            return dict(gla_w_qkvg=w_in[:, :qkvg], gla_w_r=jnp.pad(w_in[:, qkvg:], ((0, 0), (0, GATE_PAD - GATE_RANK))),
                        gla_w_out=got[1].reshape(d, d), gate_w=self.gate_w, gate_b=self.gate_b, head_g=self.head_g)
        shapes = [s.shape for s in self.small_shards]
        parts = [_unpack(got[2][dev], shapes) for dev in range(N_DEV)]
        meta, conv_w, gate_w, self.gate_b, self.head_g = [
            jnp.concatenate([parts[dev][k] for dev in range(N_DEV)], axis=-1) for k in range(len(shapes))]
        self.gate_w = jnp.pad(gate_w[0], ((0, GATE_PAD - GATE_RANK), (0, 0))).astype(BF16)
        return dict(cp_w_in=_undo_column_split(got[0]), cp_w_out=got[1].reshape(d, d), meta=meta,
                    conv_w=jnp.pad(conv_w[0], ((0, 1), (0, 0))))

    def emit(self, group, g):
        d = self.d
        if group in ("ffn0", "ffn1"):
            arrs = [g["w1"], g["w2"]]
        elif group == "gla":
            w_in = jnp.concatenate([g["gla_w_qkvg"], g["gla_w_r"][:, :GATE_RANK]], axis=1)
            arrs = [_column_split(w_in), g["gla_w_out"].reshape(N_DEV, d // N_DEV, d)]
        else:
            s = dict(g["small"])
            s.update(pool_w=s["pool_w"][None], conv_w=s["conv_w"][None, :CONV_WIDTH], gate_w=s["gate_w"][None, :GATE_RANK])
            self.small_grads = [s[n] for n in _SMALL_GRADS]
            self.small_own = _pack(self.small_grads)
            self.small_sent, self.token = _exchange_start(
                _plan_to_all, [self.small_own], [lax.empty((N_DEV,) + self.small_own.shape, F32)], after=self.token,
                name="start_g_small")
            arrs = [_column_split(g["cp_w_in"]), g["cp_w_out"].reshape(N_DEV, d // N_DEV, d)]
        state, self.token = _exchange_start(_plan_to_sibling, arrs, None, after=self.token, name=f"start_g1_{group}")
        self.to_sibling.append((group, state))
        return self.token

    def poll(self, after):
        for group, state in self.to_sibling:
            arrs, landed = _exchange_wait(state, after, name=f"wait_g1_{group}")
            sums = [_pair_add(a, l, name=f"chip_sum_{group}_{k}") for k, (a, l) in enumerate(zip(arrs, landed))]
            self.to_chips[group], self.token = _exchange_start(_plan_to_chips, sums, None, after=self.token,
                                                               name=f"start_g2_{group}")
            after = self.token
        self.to_sibling = []
        return self.token

    def finish(self, w, mom, var):
        out = {}
        self.poll(self.token)
        after = self.token

        def landed(group):
            sums, got = _exchange_wait(self.to_chips[group], after, name=f"wait_g2_{group}")
            return list(zip(sums, got))

        def adam(n, parts, behind=None):
            out[n] = _reduce_adam(parts, w[n], mom[n], var[n], after=behind, name=f"adam_{n}")
            return out[n][0]

        ffn1 = landed("ffn1")
        after = ffn1[0][1]
        gla = landed("gla")
        after = gla[0][1]
        ffn0 = landed("ffn0")
        after = adam("ffn_w1", [ffn0[0], ffn1[0]])
        after = adam("ffn_w2", [ffn0[1], ffn1[1]], after)
        after = adam("gla_w_in", [gla[0]], after)
        after = adam("gla_w_out", [gla[1]], after)
        cp = landed("cp")
        adam("cp_w_in", [cp[0]])
        after = adam("cp_w_out", [cp[1]])
        (self.small_own,), (small_all,) = _exchange_wait(self.small_sent, after, name="wait_g_small")

        names = _REPLICATED + _SMALL_SHARDED
        small_sum = _unpack(_reduce8(self.small_own, small_all, name="sum_small_grads"), [a.shape for a in self.small_grads])
        me = _my_slot()
        grad = {}
        for n, full in zip(names, small_sum):
            if n in _SMALL_SHARDED:
                width = w[n].shape[-1]
                full = lax.dynamic_slice_in_dim(full, me * width, width, axis=full.ndim - 1)
            grad[n] = full
        packed = [_pack([t[n] for n in names]) for t in (w, grad, mom, var)]
        small_out = [_unpack(p, [w[n].shape for n in names]) for p in _adamw_small(*packed, name="adam_small")]
        for k, n in enumerate(names):
            out[n] = (grad[n], small_out[0][k], small_out[1][k], small_out[2][k])
        return out


def kernel(x, meta_tokens, mix_norm_g, ffn_norm_g, ffn_w1, ffn_w2, cp_w_in, cp_conv_w, cp_conv_b, cp_ln_g, cp_ln_b, cp_pool_w, cp_pool_scale, cp_w_out, gla_w_in, gla_gate_w2, gla_gate_b, gla_head_g, gla_w_out, final_norm_g, loss_target, m_meta_tokens, m_mix_norm_g, m_ffn_norm_g, m_ffn_w1, m_ffn_w2, m_cp_w_in, m_cp_conv_w, m_cp_conv_b, m_cp_ln_g, m_cp_ln_b, m_cp_pool_w, m_cp_pool_scale, m_cp_w_out, m_gla_w_in, m_gla_gate_w2, m_gla_gate_b, m_gla_head_g, m_gla_w_out, m_final_norm_g, v_meta_tokens, v_mix_norm_g, v_ffn_norm_g, v_ffn_w1, v_ffn_w2, v_cp_w_in, v_cp_conv_w, v_cp_conv_b, v_cp_ln_g, v_cp_ln_b, v_cp_pool_w, v_cp_pool_scale, v_cp_w_out, v_gla_w_in, v_gla_gate_w2, v_gla_gate_b, v_gla_head_g, v_gla_w_out, v_final_norm_g):
    w = dict(meta_tokens=meta_tokens, mix_norm_g=mix_norm_g, ffn_norm_g=ffn_norm_g, ffn_w1=ffn_w1, ffn_w2=ffn_w2,
             cp_w_in=cp_w_in, cp_conv_w=cp_conv_w, cp_conv_b=cp_conv_b, cp_ln_g=cp_ln_g, cp_ln_b=cp_ln_b,
             cp_pool_w=cp_pool_w, cp_pool_scale=cp_pool_scale, cp_w_out=cp_w_out, gla_w_in=gla_w_in,
             gla_gate_w2=gla_gate_w2, gla_gate_b=gla_gate_b, gla_head_g=gla_head_g, gla_w_out=gla_w_out,
             final_norm_g=final_norm_g.reshape(1, -1))
    mom = dict(meta_tokens=m_meta_tokens, mix_norm_g=m_mix_norm_g, ffn_norm_g=m_ffn_norm_g, ffn_w1=m_ffn_w1, ffn_w2=m_ffn_w2,
               cp_w_in=m_cp_w_in, cp_conv_w=m_cp_conv_w, cp_conv_b=m_cp_conv_b, cp_ln_g=m_cp_ln_g, cp_ln_b=m_cp_ln_b,
               cp_pool_w=m_cp_pool_w, cp_pool_scale=m_cp_pool_scale, cp_w_out=m_cp_w_out, gla_w_in=m_gla_w_in,
               gla_gate_w2=m_gla_gate_w2, gla_gate_b=m_gla_gate_b, gla_head_g=m_gla_head_g, gla_w_out=m_gla_w_out,
               final_norm_g=m_final_norm_g.reshape(1, -1))
    var = dict(meta_tokens=v_meta_tokens, mix_norm_g=v_mix_norm_g, ffn_norm_g=v_ffn_norm_g, ffn_w1=v_ffn_w1, ffn_w2=v_ffn_w2,
               cp_w_in=v_cp_w_in, cp_conv_w=v_cp_conv_w, cp_conv_b=v_cp_conv_b, cp_ln_g=v_cp_ln_g, cp_ln_b=v_cp_ln_b,
               cp_pool_w=v_cp_pool_w, cp_pool_scale=v_cp_pool_scale, cp_w_out=v_cp_w_out, gla_w_in=v_gla_w_in,
               gla_gate_w2=v_gla_gate_w2, gla_gate_b=v_gla_gate_b, gla_head_g=v_gla_head_g, gla_w_out=v_gla_w_out,
               final_norm_g=v_final_norm_g.reshape(1, -1))
    d = x.shape[-1]
    replicated = dict(mix_g=w["mix_norm_g"], ffn_g=w["ffn_norm_g"], conv_b=w["cp_conv_b"], ln_g=w["cp_ln_g"],
                      ln_b=w["cp_ln_b"], pool_w=w["cp_pool_w"][0].astype(BF16), pool_scale=w["cp_pool_scale"],
                      final_g=w["final_norm_g"])
    exchanges = _Exchanges(w, d)
    loss_blk, grad_x, _ = _local_step(x[0], loss_target[0], replicated, exchanges)
    loss = lax.psum(loss_blk[0, 0], ("x", "y", "c"))
    out = exchanges.finish(w, mom, var)

    def leaf(n, k):
        a = out[n][k]
        return a.reshape(-1) if n == "final_norm_g" else a

    return (loss, grad_x[None], *[leaf(n, 0) for n in _NAMES], *[leaf(n, 1) for n in _NAMES],
            *[leaf(n, 2) for n in _NAMES], *[leaf(n, 3) for n in _NAMES])
```

```python
import functools

import jax
import jax.numpy as jnp
from jax import lax
from jax.experimental import pallas as pl
from jax.experimental.pallas import tpu as pltpu

F32, BF16 = jnp.float32, jnp.bfloat16
N_DEV = 8
CHUNK = 64
N_META = 16
PAD_ROWS = CHUNK - N_META
HALO = 32
EPS = 1e-5
CONV_WIDTH = 31
POOL_WINDOWS = (2, 4, 8, 16)
HEADS = 4
GATE_RANK = 16
GATE_NORM = 16.0
GATE_PAD = 128
ADAM_LR, ADAM_B1, ADAM_B2, ADAM_EPS, ADAM_WD, ADAM_STEP = 0.001, 0.9, 0.999, 1e-08, 0.01, 10
V7X_VMEM_LIMIT = 56 * 2 ** 20
LANE = 128
HI = lax.Precision.HIGHEST


def _cparams(*sem):
    return pltpu.CompilerParams(dimension_semantics=sem, vmem_limit_bytes=V7X_VMEM_LIMIT)


def _row_tile(t, cap):
    best = CHUNK
    for r in range(CHUNK, min(t, cap) + 1, CHUNK):
        if t % r == 0:
            best = r
    return best


def _resident(shape):
    return pl.BlockSpec(shape, lambda *_: (0,) * len(shape), pipeline_mode=pl.Buffered(1))


def _dot(a, b):
    return jnp.dot(a, b, preferred_element_type=F32)


def _dot_nt(a, b):
    return lax.dot_general(a, b, (((1,), (1,)), ((), ())), preferred_element_type=F32)


def _dot_tn(a, b):
    return lax.dot_general(a, b, (((0,), (0,)), ((), ())), preferred_element_type=F32)


def _rowsum(a):
    return jnp.sum(a, axis=0, keepdims=True)


def _sigmoid(a):
    return 1.0 / (1.0 + jnp.exp(-a))


def _row_ids(tile, rt):
    return tile * rt + lax.broadcasted_iota(jnp.int32, (rt, 1), 0)


def _sds(shape, dtype):
    return jax.ShapeDtypeStruct(shape, dtype)


def _linear_fwd(x, w, *, gain=None, res=None, out_dtype=F32, name):
    t, k = x.shape
    n = w.shape[1]
    rt = _row_tile(t, 320)

    def body(*refs):
        refs = list(refs)
        x_ref, w_ref = refs[:2]
        pos = 2
        g_ref = r_ref = u_ref = None
        if gain is not None:
            g_ref = refs[pos]
            pos += 1
        if res is not None:
            r_ref = refs[pos]
            pos += 1
        y_ref = refs[pos]
        if gain is not None:
            u_ref = refs[pos + 1]
            xv = x_ref[...]
            u = (xv * lax.rsqrt(jnp.mean(xv * xv, axis=-1, keepdims=True) + EPS) * g_ref[...]).astype(BF16)
            u_ref[...] = u
        else:
            u = x_ref[...]
        y = _dot(u, w_ref[...])
        if res is not None:
            y = y + r_ref[...]
        y_ref[...] = y.astype(y_ref.dtype)

    rows = lambda i: (i, 0)
    in_specs = [pl.BlockSpec((rt, k), rows), _resident((k, n))]
    args = [x, w]
    if gain is not None:
        in_specs.append(_resident((1, k)))
        args.append(gain)
    if res is not None:
        in_specs.append(pl.BlockSpec((rt, n), rows))
        args.append(res)
    out_shape = [_sds((t, n), out_dtype)]
    out_specs = [pl.BlockSpec((rt, n), rows)]
    if gain is not None:
        out_shape.append(_sds((t, k), BF16))
        out_specs.append(pl.BlockSpec((rt, k), rows))
    out = pl.pallas_call(body, grid=(t // rt,), in_specs=in_specs, out_specs=out_specs, out_shape=out_shape,
                         compiler_params=_cparams("parallel"), name=name)(*args)
    return out if gain is not None else out[0]


def _linear_bwd_x(dys, ws, *, norm=None, after=None, name):
    t = dys[0].shape[0]
    k = ws[0].shape[0]
    rt = _row_tile(t, 320)
    nd = len(dys)
    dep_specs, deps = _dep_specs(after)

    def body(*refs):
        refs = list(refs)[len(deps):]
        dy_refs, w_refs = refs[:nd], refs[nd:2 * nd]
        dx = None
        for dy_ref, w_ref in zip(dy_refs, w_refs):
            part = _dot_nt(dy_ref[...].astype(BF16), w_ref[...])
            dx = part if dx is None else dx + part
        if norm is None:
            refs[2 * nd][...] = dx
            return
        h_ref, g_ref, dres_ref, dh_ref, dg_ref = refs[2 * nd:]
        hv = h_ref[...]
        rstd = lax.rsqrt(jnp.mean(hv * hv, axis=-1, keepdims=True) + EPS)
        xh = hv * rstd

        @pl.when(pl.program_id(0) == 0)
        def _():
            dg_ref[...] = jnp.zeros_like(dg_ref)

        dg_ref[...] += _rowsum(dx * xh)
        dxh = dx * g_ref[...]
        dh_ref[...] = dres_ref[...] + rstd * (dxh - xh * jnp.mean(dxh * xh, axis=-1, keepdims=True))

    rows = lambda i: (i, 0)
    in_specs = [pl.BlockSpec((rt, dy.shape[1]), rows) for dy in dys] + [_resident(w.shape) for w in ws]
    args = list(dys) + list(ws)
    out_shape = [_sds((t, k), F32)]
    out_specs = [pl.BlockSpec((rt, k), rows)]
    if norm is not None:
        h, gain, dres = norm
        in_specs += [pl.BlockSpec((rt, k), rows), _resident((1, k)), pl.BlockSpec((rt, k), rows)]
        args += [h, gain, dres]
        out_shape.append(_sds((1, k), F32))
        out_specs.append(pl.BlockSpec((1, k), lambda i: (0, 0)))
    out = pl.pallas_call(body, grid=(t // rt,), in_specs=dep_specs + in_specs, out_specs=out_specs, out_shape=out_shape,
                         compiler_params=_cparams("arbitrary"), name=name)(*deps, *args)
    return out if norm is not None else out[0]


def _linear_bwd_w(x, dy, *, name):
    t, k = x.shape
    n = dy.shape[1]
    rt = _row_tile(t, 832)
    nt = max(c for c in (512, 384, 256, LANE) if n % c == 0)

    def body(x_ref, dy_ref, o_ref):
        @pl.when(pl.program_id(1) == 0)
        def _():
            o_ref[...] = jnp.zeros_like(o_ref)

        o_ref[...] += _dot_tn(x_ref[...], dy_ref[...].astype(BF16))

    return pl.pallas_call(
        body, grid=(n // nt, t // rt),
        in_specs=[pl.BlockSpec((rt, k), lambda j, i: (i, 0)), pl.BlockSpec((rt, nt), lambda j, i: (i, j))],
        out_specs=pl.BlockSpec((k, nt), lambda j, i: (0, j)), out_shape=_sds((k, n), F32),
        compiler_params=_cparams("parallel", "arbitrary"), name=name)(x, dy)


def _ffn_fwd(h, gain, w1g, w2g, *, name):
    t, d = h.shape
    f8 = w1g.shape[-1]
    rt = _row_tile(t, 832)

    def body(h_ref, g_ref, w1_ref, w2_ref, o_ref, u_ref, r_ref, acc_ref):
        j = pl.program_id(1)

        @pl.when(j == 0)
        def _():
            hv = h_ref[...]
            u_ref[...] = (hv * lax.rsqrt(jnp.mean(hv * hv, axis=-1, keepdims=True) + EPS) * g_ref[...]).astype(BF16)
            acc_ref[...] = jnp.zeros_like(acc_ref)

        a = jnp.maximum(_dot(u_ref[...], w1_ref[...]), 0.0)
        r_ref[...] = a.astype(BF16)
        acc_ref[...] += _dot((a * a).astype(BF16), w2_ref[...])

        @pl.when(j == N_DEV - 1)
        def _():
            o_ref[...] = h_ref[...] + acc_ref[...]

    return pl.pallas_call(
        body, grid=(t // rt, N_DEV),
        in_specs=[pl.BlockSpec((rt, d), lambda i, j: (i, 0)), _resident((1, d)),
                  pl.BlockSpec((None, d, f8), lambda i, j: (j, 0, 0)),
                  pl.BlockSpec((None, f8, d), lambda i, j: (j, 0, 0))],
        out_specs=[pl.BlockSpec((rt, d), lambda i, j: (i, 0)), pl.BlockSpec((rt, d), lambda i, j: (i, 0)),
                   pl.BlockSpec((rt, f8), lambda i, j: (i, j))],
        out_shape=[_sds((t, d), F32), _sds((t, d), BF16), _sds((t, N_DEV * f8), BF16)],
        scratch_shapes=[pltpu.VMEM((rt, d), F32)],
        compiler_params=_cparams("parallel", "arbitrary"), name=name)(h, gain, w1g, w2g)


def _ffn_bwd_x(h, dout, gain, r, w1g, w2g, *, after=None, name):
    t, d = h.shape
    f8 = w1g.shape[-1]
    rt = _row_tile(t, 832)
    last = N_DEV - 1
    dep_specs, deps = _dep_specs(after)

    def body(*refs):
        h_ref, do_ref, g_ref, r_ref, w1_ref, w2_ref, dh_ref, dhh_ref, dob_ref, dg_ref, du_ref = refs[len(deps):]
        i, j = pl.program_id(0), pl.program_id(1)

        @pl.when(j == 0)
        def _():
            dob_ref[...] = do_ref[...].astype(BF16)
            du_ref[...] = jnp.zeros_like(du_ref)

        dhh = (_dot_nt(dob_ref[...], w2_ref[...]) * (2.0 * r_ref[...].astype(F32))).astype(BF16)
        dhh_ref[...] = dhh
        du_ref[...] += _dot_nt(dhh, w1_ref[...])

        @pl.when(j == last)
        def _():
            @pl.when(i == 0)
            def _():
                dg_ref[...] = jnp.zeros_like(dg_ref)

            hv = h_ref[...]
            rstd = lax.rsqrt(jnp.mean(hv * hv, axis=-1, keepdims=True) + EPS)
            xh = hv * rstd
            du = du_ref[...]
            dg_ref[...] += _rowsum(du * xh)
            dxh = du * g_ref[...]
            dh_ref[...] = do_ref[...] + rstd * (dxh - xh * jnp.mean(dxh * xh, axis=-1, keepdims=True))

    rows = lambda i, j: (i, 0)
    return pl.pallas_call(
        body, grid=(t // rt, N_DEV),
        in_specs=dep_specs + [
                  pl.BlockSpec((rt, d), rows), pl.BlockSpec((rt, d), rows), _resident((1, d)),
                  pl.BlockSpec((rt, f8), lambda i, j: (i, j)),
                  pl.BlockSpec((None, d, f8), lambda i, j: (j, 0, 0)),
                  pl.BlockSpec((None, f8, d), lambda i, j: (j, 0, 0))],
        out_specs=[pl.BlockSpec((rt, d), rows), pl.BlockSpec((rt, f8), lambda i, j: (i, j)), pl.BlockSpec((rt, d), rows),
                   pl.BlockSpec((1, d), lambda i, j: (0, 0))],
        out_shape=[_sds((t, d), F32), _sds((t, N_DEV * f8), BF16), _sds((t, d), BF16), _sds((1, d), F32)],
        scratch_shapes=[pltpu.VMEM((rt, d), F32)],
        compiler_params=_cparams("arbitrary", "arbitrary"), name=name)(*deps, h, dout, gain, r, w1g, w2g)


FFN_DW_ROWS = 1024


def _ffn_bwd_w(u, dhh, r, dout_b, *, name):
    t, d = u.shape
    f8 = dhh.shape[1] // N_DEV

    def body(u_ref, dhh_ref, r_ref, dob_ref, dw1_ref, dw2_ref):
        for c0 in range(0, t, FFN_DW_ROWS):
            rows = slice(c0, min(c0 + FFN_DW_ROWS, t))
            rr = r_ref[rows, :].astype(F32)
            part1 = _dot_tn(u_ref[rows, :], dhh_ref[rows, :])
            part2 = _dot_tn((rr * rr).astype(BF16), dob_ref[rows, :])
            if c0 == 0:
                dw1_ref[...] = part1
                dw2_ref[...] = part2
            else:
                dw1_ref[...] += part1
                dw2_ref[...] += part2

    return pl.pallas_call(
        body, grid=(N_DEV,),
        in_specs=[_resident((t, d)), pl.BlockSpec((t, f8), lambda j: (0, j)), pl.BlockSpec((t, f8), lambda j: (0, j)),
                  _resident((t, d))],
        out_specs=[pl.BlockSpec((None, d, f8), lambda j: (j, 0, 0)), pl.BlockSpec((None, f8, d), lambda j: (j, 0, 0))],
        out_shape=[_sds((N_DEV, d, f8), F32), _sds((N_DEV, f8, d), F32)],
        compiler_params=_cparams("parallel"), name=name)(u, dhh, r, dout_b)


def _lane_blocks(width):
    lb = min(LANE, width)
    return [slice(s, s + lb) for s in range(0, width, lb)]


def _conv_rows(src_ref, w_ref, offset, dst_ref, nblk, width, bias_ref=None):
    def blk(rb, carry):
        base = pl.multiple_of(rb * CHUNK, CHUNK)
        for l, ls in enumerate(_lane_blocks(width)):
            acc = jnp.zeros((CHUNK, ls.stop - ls.start), F32)
            if bias_ref is not None:
                acc = acc + bias_ref[:, ls]
            for k in range(CONV_WIDTH):
                acc = acc + w_ref[k:k + 1, ls] * src_ref[l, pl.ds(base + offset(k), CHUNK), :]
            dst_ref[l, pl.ds(base, CHUNK), :] = acc
        return carry

    lax.fori_loop(0, nblk, blk, 0)


def _to_lane_blocks(ref, row0, value):
    for l, ls in enumerate(_lane_blocks(value.shape[1])):
        ref[l, row0:row0 + value.shape[0], :] = value[:, ls]


def _from_lane_blocks(ref):
    return jnp.concatenate([ref[l] for l in range(ref.shape[0])], axis=1)


def _pool_counts(rows, window):
    return jnp.clip(rows - PAD_ROWS + 1, 1, window).astype(F32)


def _trailing_sum(v, window):
    s, sh = v, 1
    while sh < window:
        s = s + pltpu.roll(s, sh, 0)
        sh *= 2
    return s


def _leading_sum(v, window):
    s, sh, n = v, 1, v.shape[0]
    while sh < window:
        s = s + pltpu.roll(s, n - sh, 0)
        sh *= 2
    return s


def _cp_mid_fwd(z, conv_w, conv_b, ln_g, ln_b, pool_w, pool_scale, *, name):
    t, ein = z.shape
    cd = conv_b.shape[1]
    pd = pool_scale.shape[1]
    pg = pd // len(POOL_WINDOWS)
    rt = _row_tile(t, 320)

    def body(z_ref, cw_ref, cb_ref, lg_ref, lb_ref, pw_ref, ps_ref, o_ref, gext, pext, conv_s):
        i = pl.program_id(0)

        @pl.when(i == 0)
        def _():
            _to_lane_blocks(gext, 0, jnp.zeros((HALO, cd), F32))
            pext[0:HALO, :] = jnp.zeros((HALO, pd), F32)

        _to_lane_blocks(gext, HALO, z_ref[:, 0:cd] * _sigmoid(z_ref[:, cd:2 * cd]))
        pext[HALO:HALO + rt, :] = z_ref[:, 2 * cd:]
        _conv_rows(gext, cw_ref, lambda k: k + HALO - (CONV_WIDTH - 1), conv_s, rt // CHUNK, cd, cb_ref)
        cv = _from_lane_blocks(conv_s)
        xc = cv - jnp.mean(cv, axis=-1, keepdims=True)
        y = xc * lax.rsqrt(jnp.mean(xc * xc, axis=-1, keepdims=True) + EPS) * lg_ref[...] + lb_ref[...]
        rows = _row_ids(i, rt)
        a = jnp.where(rows >= PAD_ROWS, y * _sigmoid(y), 0.0)
        o_ref[:, 0:cd] = a.astype(BF16)
        for gi, window in enumerate(POOL_WINDOWS):
            ls = slice(gi * pg, (gi + 1) * pg)
            v = pext[:, ls]
            tm = _trailing_sum(v, window)[HALO:] / _pool_counts(rows, window) - v[HALO:]
            p = _dot(tm.astype(BF16), pw_ref[gi]) * ps_ref[:, ls]
            o_ref[:, cd + gi * pg:cd + (gi + 1) * pg] = p.astype(BF16)
        gext[:, 0:HALO, :] = gext[:, rt:rt + HALO, :]
        pext[0:HALO, :] = pext[rt:rt + HALO, :]

    nl, lb = len(_lane_blocks(cd)), min(LANE, cd)
    return pl.pallas_call(
        body, grid=(t // rt,),
        in_specs=[pl.BlockSpec((rt, ein), lambda i: (i, 0)), _resident(conv_w.shape), _resident((1, cd)),
                  _resident((1, cd)), _resident((1, cd)), _resident(pool_w.shape), _resident((1, pd))],
        out_specs=pl.BlockSpec((rt, cd + pd), lambda i: (i, 0)), out_shape=_sds((t, cd + pd), BF16),
        scratch_shapes=[pltpu.VMEM((nl, rt + HALO, lb), F32), pltpu.VMEM((rt + HALO, pd), F32),
                        pltpu.VMEM((nl, rt, lb), F32)],
        compiler_params=_cparams("arbitrary"), name=name)(z, conv_w, conv_b, ln_g, ln_b, pool_w, pool_scale)


def _cp_mid_bwd(z, dcat, conv_w, conv_b, ln_g, ln_b, pool_w, pool_scale, *, after=None, name):
    t, ein = z.shape
    cd = conv_b.shape[1]
    pd = pool_scale.shape[1]
    pg = pd // len(POOL_WINDOWS)
    rt = _row_tile(t, 320)
    ntile = t // rt
    per = rt // CHUNK
    dep_specs, deps = _dep_specs(after)

    def body(*refs):
        (z_ref, zh_ref, dc_ref, cw_ref, cb_ref, lg_ref, lb_ref, pw_ref, ps_ref,
         dz_ref, dcw_ref, dcb_ref, dlg_ref, dlb_ref, dpw_ref, dps_ref, gext, pext, conv_s, dcv, dsp) = refs[len(deps):]
        step = pl.program_id(0)
        tile = ntile - 1 - step

        @pl.when(step == 0)
        def _():
            for ref in (dcw_ref, dcb_ref, dlg_ref, dlb_ref, dpw_ref, dps_ref):
                ref[...] = jnp.zeros_like(ref)
            _to_lane_blocks(dcv, rt, jnp.zeros((HALO, cd), F32))
            dsp[rt:rt + HALO, :] = jnp.zeros((HALO, pd), F32)

        keep = jnp.where(tile > 0, 1.0, 0.0)
        zh = zh_ref[CHUNK - HALO:CHUNK, :]
        _to_lane_blocks(gext, 0, keep * zh[:, 0:cd] * _sigmoid(zh[:, cd:2 * cd]))
        pext[0:HALO, :] = keep * zh[:, 2 * cd:]
        za = z_ref[:, 0:cd]
        sg = _sigmoid(z_ref[:, cd:2 * cd])
        _to_lane_blocks(gext, HALO, za * sg)
        pext[HALO:HALO + rt, :] = z_ref[:, 2 * cd:]
        _conv_rows(gext, cw_ref, lambda k: k + HALO - (CONV_WIDTH - 1), conv_s, per, cd, cb_ref)
        cv = _from_lane_blocks(conv_s)
        xc = cv - jnp.mean(cv, axis=-1, keepdims=True)
        rstd = lax.rsqrt(jnp.mean(xc * xc, axis=-1, keepdims=True) + EPS)
        xh = xc * rstd
        y = xh * lg_ref[...] + lb_ref[...]
        sy = _sigmoid(y)
        rows = _row_ids(tile, rt)
        da = jnp.where(rows >= PAD_ROWS, dc_ref[:, 0:cd], 0.0)
        dy = da * (sy * (1.0 + y * (1.0 - sy)))
        dlg_ref[...] += _rowsum(dy * xh)
        dlb_ref[...] += _rowsum(dy)
        dxh = dy * lg_ref[...]
        dconv = rstd * (dxh - jnp.mean(dxh, axis=-1, keepdims=True) - xh * jnp.mean(dxh * xh, axis=-1, keepdims=True))
        dcb_ref[...] += _rowsum(dconv)
        _to_lane_blocks(dcv, 0, dconv)
        for l, ls in enumerate(_lane_blocks(cd)):
            for k in range(CONV_WIDTH):
                def acc_rows(rb, acc, l=l, k=k):
                    base = pl.multiple_of(rb * CHUNK, CHUNK)
                    return acc + dcv[l, pl.ds(base, CHUNK), :] * gext[l, pl.ds(base + k + HALO - (CONV_WIDTH - 1), CHUNK), :]

                acc = lax.fori_loop(0, per, acc_rows, jnp.zeros((CHUNK, ls.stop - ls.start), F32))
                dcw_ref[k:k + 1, ls] += _rowsum(acc)
        _conv_rows(dcv, cw_ref, lambda k: CONV_WIDTH - 1 - k, conv_s, per, cd)
        dglu = _from_lane_blocks(conv_s)
        dz_ref[:, 0:cd] = (dglu * sg).astype(BF16)
        dz_ref[:, cd:2 * cd] = (dglu * za * sg * (1.0 - sg)).astype(BF16)
        dcv[:, rt:rt + HALO, :] = dcv[:, 0:HALO, :]
        for gi, window in enumerate(POOL_WINDOWS):
            ls = slice(gi * pg, (gi + 1) * pg)
            v = pext[:, ls]
            cnt = _pool_counts(rows, window)
            tm = (_trailing_sum(v, window)[HALO:] / cnt - v[HALO:]).astype(BF16)
            dp = dc_ref[:, cd + gi * pg:cd + (gi + 1) * pg]
            dps_ref[:, ls] += _rowsum(dp * _dot(tm, pw_ref[gi]))
            dpl = (dp * ps_ref[:, ls]).astype(BF16)
            dpw_ref[gi] += _dot_tn(tm, dpl)
            dtm = _dot_nt(dpl, pw_ref[gi])
            dsp[0:rt, ls] = dtm / cnt
            dpin = _leading_sum(dsp[:, ls], window)[0:rt] - dtm
            dz_ref[:, 2 * cd + gi * pg:2 * cd + (gi + 1) * pg] = dpin.astype(BF16)
        dsp[rt:rt + HALO, :] = dsp[0:HALO, :]

    back = lambda i: (ntile - 1 - i, 0)
    halo_idx = lambda i: (jnp.maximum((ntile - 1 - i) * per - 1, 0), 0)
    const2 = lambda i: (0, 0)
    nl, lb = len(_lane_blocks(cd)), min(LANE, cd)
    return pl.pallas_call(
        body, grid=(ntile,),
        in_specs=dep_specs + [
                  pl.BlockSpec((rt, ein), back), pl.BlockSpec((CHUNK, ein), halo_idx), pl.BlockSpec((rt, cd + pd), back),
                  _resident(conv_w.shape), _resident((1, cd)), _resident((1, cd)), _resident((1, cd)),
                  _resident(pool_w.shape), _resident((1, pd))],
        out_specs=[pl.BlockSpec((rt, ein), back), pl.BlockSpec(conv_w.shape, const2), pl.BlockSpec((1, cd), const2),
                   pl.BlockSpec((1, cd), const2), pl.BlockSpec((1, cd), const2),
                   pl.BlockSpec(pool_w.shape, lambda i: (0, 0, 0)), pl.BlockSpec((1, pd), const2)],
        out_shape=[_sds((t, ein), BF16), _sds(conv_w.shape, F32), _sds((1, cd), F32), _sds((1, cd), F32),
                   _sds((1, cd), F32), _sds(pool_w.shape, F32), _sds((1, pd), F32)],
        scratch_shapes=[pltpu.VMEM((nl, rt + HALO, lb), F32), pltpu.VMEM((rt + HALO, pd), F32), pltpu.VMEM((nl, rt, lb), F32),
                        pltpu.VMEM((nl, rt + HALO, lb), F32), pltpu.VMEM((rt + HALO, pd), F32)],
        compiler_params=_cparams("arbitrary"), name=name)(*deps, z, z, dcat, conv_w, conv_b, ln_g, ln_b, pool_w, pool_scale)


def _log_decay(r_ref, gw_ref, gb_ref, rows):
    gp = _dot(r_ref[...].astype(BF16), gw_ref[...]) + gb_ref[...]
    log_sig = jnp.minimum(gp, 0.0) - jnp.log(1.0 + jnp.exp(-jnp.abs(gp)))
    return gp, jnp.where(rows >= PAD_ROWS, log_sig / GATE_NORM, 0.0)


def _tri(strict):
    r = lax.broadcasted_iota(jnp.int32, (CHUNK, CHUNK), 0)
    c = lax.broadcasted_iota(jnp.int32, (CHUNK, CHUNK), 1)
    return jnp.where(c < r if strict else c <= r, 1.0, 0.0).astype(F32)


def _gla_mid_fwd(z, r, gate_w, gate_b, head_g, *, name):
    t = z.shape[0]
    dk = gate_b.shape[1]
    hv = head_g.shape[1]
    hk = dk // HEADS
    dv = hv * HEADS
    rt = _row_tile(t, 320)
    per = rt // CHUNK
    scale = hk ** -0.5

    def body(z_ref, r_ref, gw_ref, gb_ref, hg_ref, o_ref, st_ref, s_ref, la_ref):
        i = pl.program_id(0)

        @pl.when(i == 0)
        def _():
            s_ref[...] = jnp.zeros_like(s_ref)

        _, la = _log_decay(r_ref, gw_ref, gb_ref, _row_ids(i, rt))
        la_ref[...] = la
        tri = _tri(False)

        def chunk(c, carry):
            rows = pl.ds(pl.multiple_of(c * CHUNK, CHUNK), CHUNK)
            la_c = la_ref[rows, :]
            cum = jnp.dot(tri, la_c, precision=HI, preferred_element_type=F32)
            tot = _rowsum(la_c)
            dec = jnp.exp(tot - cum)
            etot = jnp.exp(tot)
            for hd in range(HEADS):
                ks = slice(hd * hk, (hd + 1) * hk)
                q = z_ref[rows, hd * hk:(hd + 1) * hk] * scale
                kd = z_ref[rows, dk + hd * hk:dk + (hd + 1) * hk] * dec[:, ks]
                v = z_ref[rows, 2 * dk + hd * hv:2 * dk + (hd + 1) * hv]
                g = z_ref[rows, 2 * dk + dv + hd * hv:2 * dk + dv + (hd + 1) * hv]
                s_new = s_ref[hd] * etot[:, ks] + _dot_tn(v.astype(BF16), kd.astype(BF16))
                s_ref[hd] = s_new
                st_ref[c, hd] = s_new
                o = _dot_nt(q.astype(BF16), s_new.astype(BF16))
                on = o * lax.rsqrt(jnp.mean(o * o, axis=-1, keepdims=True) + EPS) * hg_ref[...]
                o_ref[rows, hd * hv:(hd + 1) * hv] = (on * (g * _sigmoid(g))).astype(BF16)
            return carry

        lax.fori_loop(0, per, chunk, 0)

    return pl.pallas_call(
        body, grid=(t // rt,),
        in_specs=[pl.BlockSpec((rt, z.shape[1]), lambda i: (i, 0)), pl.BlockSpec((rt, GATE_PAD), lambda i: (i, 0)),
                  _resident(gate_w.shape), _resident((1, dk)), _resident((1, hv))],
        out_specs=[pl.BlockSpec((rt, dv), lambda i: (i, 0)), pl.BlockSpec((per, HEADS, hv, hk), lambda i: (i, 0, 0, 0))],
        out_shape=[_sds((t, dv), BF16), _sds((t // CHUNK, HEADS, hv, hk), F32)],
        scratch_shapes=[pltpu.VMEM((HEADS, hv, hk), F32), pltpu.VMEM((rt, dk), F32)],
        compiler_params=_cparams("arbitrary"), name=name)(z, r, gate_w, gate_b, head_g)


def _gla_mid_bwd(z, r, dog, states, gate_w, gate_b, head_g, *, after=None, name):
    t = z.shape[0]
    dk = gate_b.shape[1]
    hv = head_g.shape[1]
    hk = dk // HEADS
    dv = hv * HEADS
    rt = _row_tile(t, 320)
    ntile = t // rt
    per = rt // CHUNK
    scale = hk ** -0.5
    dep_specs, deps = _dep_specs(after)

    def body(*refs):
        (z_ref, r_ref, do_ref, st_ref, stp_ref, gw_ref, gb_ref, hg_ref,
         dz_ref, dr_ref, dgw_ref, dgb_ref, dhg_ref, ds_ref, la_ref, dla_ref) = refs[len(deps):]
        step = pl.program_id(0)
        tile = ntile - 1 - step

        @pl.when(step == 0)
        def _():
            ds_ref[...] = jnp.zeros_like(ds_ref)
            dgw_ref[...] = jnp.zeros_like(dgw_ref)
            dgb_ref[...] = jnp.zeros_like(dgb_ref)
            dhg_ref[...] = jnp.zeros_like(dhg_ref)

        rows_id = _row_ids(tile, rt)
        gp, la = _log_decay(r_ref, gw_ref, gb_ref, rows_id)
        la_ref[...] = la
        tri, tri_strict = _tri(False), _tri(True)
        keep = jnp.where(tile > 0, 1.0, 0.0)

        def chunk(cc, carry):
            c = per - 1 - cc
            rows = pl.ds(pl.multiple_of(c * CHUNK, CHUNK), CHUNK)
            la_c = la_ref[rows, :]
            cum = jnp.dot(tri, la_c, precision=HI, preferred_element_type=F32)
            tot = _rowsum(la_c)
            dec = jnp.exp(tot - cum)
            etot = jnp.exp(tot)
            inside = jnp.where(c > 0, 1.0, 0.0)
            for hd in range(HEADS):
                ks = slice(hd * hk, (hd + 1) * hk)
                q = (z_ref[rows, hd * hk:(hd + 1) * hk] * scale).astype(BF16)
                k = z_ref[rows, dk + hd * hk:dk + (hd + 1) * hk]
                kd = k * dec[:, ks]
                v = z_ref[rows, 2 * dk + hd * hv:2 * dk + (hd + 1) * hv].astype(BF16)
                g = z_ref[rows, 2 * dk + dv + hd * hv:2 * dk + dv + (hd + 1) * hv]
                s_now = st_ref[c, hd]
                s_prev = inside * st_ref[jnp.maximum(c - 1, 0), hd] + (1.0 - inside) * keep * stp_ref[0, hd]
                s_b = s_now.astype(BF16)
                o = _dot_nt(q, s_b)
                rstd = lax.rsqrt(jnp.mean(o * o, axis=-1, keepdims=True) + EPS)
                oh = o * rstd
                sg = _sigmoid(g)
                d_og = do_ref[rows, hd * hv:(hd + 1) * hv]
                dz_ref[rows, 2 * dk + dv + hd * hv:2 * dk + dv + (hd + 1) * hv] = (
                    d_og * oh * hg_ref[...] * (sg * (1.0 + g * (1.0 - sg)))).astype(BF16)
                don = d_og * (g * sg)
                dhg_ref[...] += _rowsum(don * oh)
                doh = don * hg_ref[...]
                d_o = (rstd * (doh - oh * jnp.mean(doh * oh, axis=-1, keepdims=True))).astype(BF16)
                dz_ref[rows, hd * hk:(hd + 1) * hk] = (_dot(d_o, s_b) * scale).astype(BF16)
                ds_t = ds_ref[hd] + _dot_tn(d_o, q)
                ds_b = ds_t.astype(BF16)
                dkd = _dot(v, ds_b)
                dz_ref[rows, 2 * dk + hd * hv:2 * dk + (hd + 1) * hv] = _dot_nt(kd.astype(BF16), ds_b).astype(BF16)
                dtot = etot[:, ks] * _rowsum(ds_t * s_prev)
                ds_ref[hd] = ds_t * etot[:, ks]
                dz_ref[rows, dk + hd * hk:dk + (hd + 1) * hk] = (dkd * dec[:, ks]).astype(BF16)
                e = dkd * kd
                dla_ref[rows, ks] = dtot + jnp.dot(tri_strict, e, precision=HI, preferred_element_type=F32)
            return carry

        lax.fori_loop(0, per, chunk, 0)
        dla = jnp.where(rows_id >= PAD_ROWS, dla_ref[...], 0.0)
        dgp = dla * (1.0 / GATE_NORM) * (1.0 - _sigmoid(gp))
        dgb_ref[...] += _rowsum(dgp)
        dgp_b = dgp.astype(BF16)
        dgw_ref[...] += _dot_tn(r_ref[...].astype(BF16), dgp_b)
        dr_ref[...] = _dot_nt(dgp_b, gw_ref[...]).astype(BF16)

    back = lambda i: (ntile - 1 - i, 0)
    const2 = lambda i: (0, 0)
    return pl.pallas_call(
        body, grid=(ntile,),
        in_specs=dep_specs + [
                  pl.BlockSpec((rt, z.shape[1]), back), pl.BlockSpec((rt, GATE_PAD), back), pl.BlockSpec((rt, dv), back),
                  pl.BlockSpec((per, HEADS, hv, hk), lambda i: (ntile - 1 - i, 0, 0, 0)),
                  pl.BlockSpec((1, HEADS, hv, hk), lambda i: (jnp.maximum((ntile - 1 - i) * per - 1, 0), 0, 0, 0)),
                  _resident(gate_w.shape), _resident((1, dk)), _resident((1, hv))],
        out_specs=[pl.BlockSpec((rt, z.shape[1]), back), pl.BlockSpec((rt, GATE_PAD), back),
                   pl.BlockSpec(gate_w.shape, const2), pl.BlockSpec((1, dk), const2), pl.BlockSpec((1, hv), const2)],
        out_shape=[_sds(z.shape, BF16), _sds((t, GATE_PAD), BF16), _sds(gate_w.shape, F32), _sds((1, dk), F32),
                   _sds((1, hv), F32)],
        scratch_shapes=[pltpu.VMEM((HEADS, hv, hk), F32), pltpu.VMEM((rt, dk), F32), pltpu.VMEM((rt, dk), F32)],
        compiler_params=_cparams("arbitrary"), name=name)(*deps, z, r, dog, states, states, gate_w, gate_b, head_g)


def _head(h, gain, target, *, name):
    t, d = h.shape
    rt = _row_tile(t, 832)

    def body(h_ref, g_ref, t_ref, dh_ref, loss_ref, dg_ref):
        i = pl.program_id(0)

        @pl.when(i == 0)
        def _():
            loss_ref[...] = jnp.zeros_like(loss_ref)
            dg_ref[...] = jnp.zeros_like(dg_ref)

        hv = h_ref[...]
        rstd = lax.rsqrt(jnp.mean(hv * hv, axis=-1, keepdims=True) + EPS)
        xh = hv * rstd
        err = jnp.where(_row_ids(i, rt) >= CHUNK, xh * g_ref[...] - t_ref[...], 0.0)
        loss_ref[...] += (0.5 / d) * jnp.sum(err * err)
        dy = err * (1.0 / d)
        dg_ref[...] += _rowsum(dy * xh)
        dxh = dy * g_ref[...]
        dh_ref[...] = rstd * (dxh - xh * jnp.mean(dxh * xh, axis=-1, keepdims=True))

    return pl.pallas_call(
        body, grid=(t // rt,),
        in_specs=[pl.BlockSpec((rt, d), lambda i: (i, 0)), _resident((1, d)), pl.BlockSpec((rt, d), lambda i: (i, 0))],
        out_specs=[pl.BlockSpec((rt, d), lambda i: (i, 0)), pl.BlockSpec((8, LANE), lambda i: (0, 0)),
                   pl.BlockSpec((1, d), lambda i: (0, 0))],
        out_shape=[_sds((t, d), F32), _sds((8, LANE), F32), _sds((1, d), F32)],
        compiler_params=_cparams("arbitrary"), name=name)(h, gain, target)


def _adamw_math(w, g, m, v):
    m = ADAM_B1 * m + (1.0 - ADAM_B1) * g
    v = ADAM_B2 * v + (1.0 - ADAM_B2) * (g * g)
    m_hat = m / (1.0 - ADAM_B1 ** ADAM_STEP)
    v_hat = v / (1.0 - ADAM_B2 ** ADAM_STEP)
    return -ADAM_LR * (m_hat / (jnp.sqrt(v_hat) + ADAM_EPS) + ADAM_WD * w), m, v


N_CHIP = N_DEV // 2
BLOCK_ELEMS = 128 * 1024


def _my_slot():
    return 4 * lax.axis_index("x") + 2 * lax.axis_index("y") + lax.axis_index("c")


def _my_chip():
    return 2 * lax.axis_index("x") + lax.axis_index("y")


def _row_block(r, c):
    cap = max(8, BLOCK_ELEMS // (-(-c // LANE) * LANE))
    return max(b for b in range(8, r + 1, 8) if r % b == 0 and b <= max(cap, 8))


def _pair_add(a, landed, *, name):
    _, r, c = a.shape
    rb = _row_block(r, c)

    def body(a_ref, l_ref, o_ref):
        mine = jnp.where(lax.axis_index("c") == 0, a_ref[0], a_ref[1])
        o_ref[...] = mine + l_ref[...]

    one = pl.BlockSpec((None, rb, c), lambda q, i: (q, i, 0))
    return pl.pallas_call(
        body, grid=(N_CHIP, r // rb), in_specs=[pl.BlockSpec((2, rb, c), lambda q, i: (q, i, 0)), one], out_specs=one,
        out_shape=_sds((N_CHIP, r, c), F32), compiler_params=_cparams("parallel", "parallel"), name=name)(a, landed)


def _reduce_adam(parts, w, m, v, *, after=None, name):
    nl, r, c = w.shape
    rb = _row_block(r, c)
    dep_specs, deps = _dep_specs(after)

    def body(*refs):
        refs = refs[len(deps):]
        p_refs = refs[:2 * nl]
        w_ref, m_ref, v_ref, g_out, d_out, m_out, v_out = refs[2 * nl:]
        layer = pl.program_id(0)
        chip = _my_chip()
        for li in range(nl):
            @pl.when(layer == li)
            def _(li=li):
                mine_ref, land_ref = p_refs[2 * li], p_refs[2 * li + 1]
                g = None
                for q in range(N_CHIP):
                    term = jnp.where(chip == q, mine_ref[q], land_ref[q])
                    g = term if g is None else g + term
                g_out[...] = g
                d_out[...], m_out[...], v_out[...] = _adamw_math(w_ref[...], g, m_ref[...], v_ref[...])

    blk = pl.BlockSpec((None, rb, c), lambda l, i: (l, i, 0))
    p_specs = [pl.BlockSpec((N_CHIP, rb, c), lambda l, i, li=li: (0, jnp.where(l == li, i, 0), 0))
               for li in range(nl) for _ in range(2)]
    flat = [p for pair in parts for p in pair]
    return pl.pallas_call(
        body, grid=(nl, r // rb), in_specs=dep_specs + p_specs + [blk, blk, blk], out_specs=[blk] * 4,
        out_shape=[_sds(w.shape, F32)] * 4, compiler_params=_cparams("arbitrary", "arbitrary"),
        name=name)(*deps, *flat, w, m, v)


def _reduce8(own, landed, *, name):
    r, c = own.shape

    def body(own_ref, p_ref, o_ref):
        me = _my_slot()
        g = None
        for dev in range(N_DEV):
            term = jnp.where(me == dev, own_ref[...], p_ref[dev])
            g = term if g is None else g + term
        o_ref[...] = g

    return pl.pallas_call(body, out_shape=_sds((r, c), F32), name=name)(own, landed)


def _adamw_small(w, g, m, v, *, name):
    def body(w_ref, g_ref, m_ref, v_ref, d_out, m_out, v_out):
        d_out[...], m_out[...], v_out[...] = _adamw_math(w_ref[...], g_ref[...], m_ref[...], v_ref[...])

    return pl.pallas_call(body, out_shape=[_sds(w.shape, F32)] * 3, name=name)(w, g, m, v)


_HBM = pl.BlockSpec(memory_space=pltpu.HBM)
_SEM = pl.BlockSpec(memory_space=pltpu.SEMAPHORE)
_DATAFLOW = pltpu.SideEffectType.DATAFLOW_SIDE_EFFECTING


def _plan_to_all(src, land):
    x, y, c = lax.axis_index("x"), lax.axis_index("y"), lax.axis_index("c")
    return [(src, land.at[_my_slot()], (x ^ ((d >> 2) & 1), y ^ ((d >> 1) & 1), c ^ (d & 1))) for d in range(1, N_DEV)]


def _plan_to_sibling(src, land):
    x, y, c = lax.axis_index("x"), lax.axis_index("y"), lax.axis_index("c")
    return [(src.at[2 * q + 1 - c], land.at[q], (x, y, 1 - c)) for q in range(N_CHIP)]


def _plan_to_chips(src, land):
    x, y, c = lax.axis_index("x"), lax.axis_index("y"), lax.axis_index("c")
    peers = [(x ^ (d >> 1), y ^ (d & 1)) for d in range(1, N_CHIP)]
    return [(src.at[2 * px + py], land.at[_my_chip()], (px, py, c)) for px, py in peers]


_PLAN_COPIES = {_plan_to_all: N_DEV - 1, _plan_to_sibling: N_CHIP, _plan_to_chips: N_CHIP - 1}


def _exchange_copies(plan, ins, lands, send, recv):
    per = _PLAN_COPIES[plan]
    copies = []
    for a, (src, land) in enumerate(zip(ins, lands)):
        for i, (s, dst, dev) in enumerate(plan(src, land)):
            copies.append(pltpu.make_async_remote_copy(
                src_ref=s, dst_ref=dst, send_sem=send.at[a * per + i], recv_sem=recv.at[a * per + i],
                device_id=dev, device_id_type=pl.DeviceIdType.MESH))
    return copies


def _place_own(a, dtype, *, name):
    r, c = a.shape
    rb = _row_block(r, c)

    def body(me_ref, a_ref, o_ref):
        o_ref[...] = a_ref[...].astype(dtype)

    grid_spec = pltpu.PrefetchScalarGridSpec(
        num_scalar_prefetch=1, grid=(r // rb,), in_specs=[pl.BlockSpec((rb, c), lambda i, me: (i, 0))],
        out_specs=pl.BlockSpec((None, rb, c), lambda i, me: (me[0], i, 0)))
    return pl.pallas_call(body, grid_spec=grid_spec, out_shape=_sds((N_DEV, r, c), dtype),
                          compiler_params=_cparams("arbitrary"), name=name)(_my_slot().reshape(1), a)


def _plan_to_all_in_place(land, _):
    return _plan_to_all(land.at[_my_slot()], land)


_PLAN_COPIES[_plan_to_all_in_place] = N_DEV - 1


def _exchange_start(plan, arrs, lands, *, after=None, name):
    if lands is None:
        lands = [lax.empty((N_CHIP,) + a.shape[1:], a.dtype) for a in arrs]
    bufs = list(lands) if arrs is None else list(arrs) + list(lands)
    n, nb = len(lands), len(bufs)
    nsem = n * _PLAN_COPIES[plan]
    dep_specs, deps = _dep_specs(after)

    def body(*refs):
        ins, land_refs = refs[:n], refs[nb - n:nb]
        send, recv = refs[nb + len(deps)], refs[nb + len(deps) + 1]
        for cp in _exchange_copies(plan, ins, land_refs, send, recv):
            cp.start()
        refs[-1][...] = jnp.zeros_like(refs[-1])

    out = pl.pallas_call(
        body, name=name,
        out_shape=(pltpu.SemaphoreType.DMA((nsem,)), pltpu.SemaphoreType.DMA((nsem,)),
                   *[pltpu.HBM(a.shape, a.dtype) for a in bufs], _sds((8, LANE), F32)),
        in_specs=[_HBM] * nb + dep_specs,
        out_specs=(_SEM, _SEM, *([_HBM] * nb), pl.BlockSpec(memory_space=pltpu.VMEM)),
        input_output_aliases={i: 2 + i for i in range(nb)},
        compiler_params=pltpu.CompilerParams(has_side_effects=_DATAFLOW),
    )(*[pltpu.with_memory_space_constraint(a, pltpu.HBM) for a in bufs], *deps)
    return (plan, n, out[0], out[1], list(out[2:2 + nb])), out[-1]


def _exchange_wait(state, after, *, name):
    plan, n, send_sem, recv_sem, bufs = state
    nb = len(bufs)

    def body(*refs):
        ins, land_refs, send, recv = refs[:n], refs[nb - n:nb], refs[nb], refs[nb + 1]
        for cp in _exchange_copies(plan, ins, land_refs, send, recv):
            cp.wait_send()
            cp.wait_recv()

    out = pl.pallas_call(
        body, name=name, out_shape=[pltpu.HBM(a.shape, a.dtype) for a in bufs],
        in_specs=[_HBM] * nb + [_SEM, _SEM, pl.BlockSpec(memory_space=pl.ANY)], out_specs=[_HBM] * nb,
        input_output_aliases={i: i for i in range(nb)},
        compiler_params=pltpu.CompilerParams(has_side_effects=_DATAFLOW),
    )(*bufs, send_sem, recv_sem, after)
    return list(out[:n]), list(out[nb - n:])


def _dep_specs(after):
    return ([], []) if after is None else ([pl.BlockSpec(memory_space=pl.ANY)], [after])


def _pack(arrays):
    flat = jnp.concatenate([a.reshape(-1) for a in arrays])
    size = flat.shape[0]
    padded = -(-size // (8 * LANE)) * (8 * LANE)
    return jnp.pad(flat, (0, padded - size)).reshape(padded // LANE, LANE)


def _unpack(packed, shapes):
    flat = packed.reshape(-1)
    out, pos = [], 0
    for shp in shapes:
        size = 1
        for s in shp:
            size *= s
        out.append(flat[pos:pos + size].reshape(shp))
        pos += size
    return out


def _undo_column_split(g):
    return jnp.transpose(g, (1, 0, 2)).reshape(g.shape[1], N_DEV * g.shape[2])


def _column_split(a):
    r, c = a.shape
    return jnp.transpose(a.reshape(r, N_DEV, c // N_DEV), (1, 0, 2))


class _WholeWeights:
    def __init__(self, groups):
        self.groups = groups
        self.grads = {}

    def fetch(self, group, after):
        return self.groups[group]

    def emit(self, group, grads):
        self.grads.update(grads)
        return None

    def poll(self, after):
        return None


def _local_step(x, target, replicated, src):
    d = x.shape[1]
    mix_g, ffn_g = replicated["mix_g"], replicated["ffn_g"]
    cp = src.fetch("cp", x)
    h0 = jnp.concatenate([jnp.zeros((PAD_ROWS, d), F32), cp["meta"], x], axis=0)
    tgt = jnp.concatenate([jnp.zeros((CHUNK, d), F32), target], axis=0)
    cp_mid = (cp["conv_w"], replicated["conv_b"], replicated["ln_g"], replicated["ln_b"], replicated["pool_w"],
              replicated["pool_scale"])

    z0, u0 = _linear_fwd(h0, cp["cp_w_in"], gain=mix_g[0:1], name="cp_in")
    cat = _cp_mid_fwd(z0, *cp_mid, name="cp_mid")
    h1 = _linear_fwd(cat, cp["cp_w_out"], res=h0, name="cp_out")
    ffn0 = src.fetch("ffn0", h1)
    h2, uf0, rf0 = _ffn_fwd(h1, ffn_g[0:1], ffn0["w1"], ffn0["w2"], name="ffn0")
    gla = src.fetch("gla", h2)
    gla_mid = (gla["gate_w"], gla["gate_b"], gla["head_g"])
    z1, u1 = _linear_fwd(h2, gla["gla_w_qkvg"], gain=mix_g[1:2], name="gla_in")
    r1 = _linear_fwd(u1, gla["gla_w_r"], name="gla_in_r")
    og, states = _gla_mid_fwd(z1, r1, *gla_mid, name="gla_mid")
    h3 = _linear_fwd(og, gla["gla_w_out"], res=h2, name="gla_out")
    ffn1 = src.fetch("ffn1", h3)
    h4, uf1, rf1 = _ffn_fwd(h3, ffn_g[1:2], ffn1["w1"], ffn1["w2"], name="ffn1")
    dh4, loss, d_final_g = _head(h4, replicated["final_g"], tgt, name="head")

    dh3, dhh1, dob1, dffn_g1 = _ffn_bwd_x(h3, dh4, ffn_g[1:2], rf1, ffn1["w1"], ffn1["w2"], name="ffn1_bwd_x")
    dw1_1, dw2_1 = _ffn_bwd_w(uf1, dhh1, rf1, dob1, name="ffn1_bwd_w")
    sent = src.emit("ffn1", dict(w1=dw1_1, w2=dw2_1))
    dog = _linear_bwd_x([dh3], [gla["gla_w_out"]], after=sent, name="gla_out_dx")
    d_gla_w_out = _linear_bwd_w(og, dh3, name="gla_out_dw")
    sent = src.poll(d_gla_w_out)
    dz1, dr1, d_gate_w, d_gate_b, d_head_g = _gla_mid_bwd(z1, r1, dog, states, *gla_mid, after=sent, name="gla_mid_bwd")
    dh2, dmix_g1 = _linear_bwd_x([dz1, dr1], [gla["gla_w_qkvg"], gla["gla_w_r"]],
                                 norm=(h2, mix_g[1:2], dh3), name="gla_in_dx")
    d_gla_w_qkvg = _linear_bwd_w(u1, dz1, name="gla_in_dw")
    d_gla_w_r = _linear_bwd_w(u1, dr1, name="gla_in_r_dw")
    sent = src.emit("gla", dict(gla_w_qkvg=d_gla_w_qkvg, gla_w_r=d_gla_w_r, gla_w_out=d_gla_w_out))
    dh1, dhh0, dob0, dffn_g0 = _ffn_bwd_x(h1, dh2, ffn_g[0:1], rf0, ffn0["w1"], ffn0["w2"], after=sent, name="ffn0_bwd_x")
    dw1_0, dw2_0 = _ffn_bwd_w(uf0, dhh0, rf0, dob0, name="ffn0_bwd_w")
    src.poll(dw1_0)
    sent = src.emit("ffn0", dict(w1=dw1_0, w2=dw2_0))
    dcat = _linear_bwd_x([dh1], [cp["cp_w_out"]], after=sent, name="cp_out_dx")
    d_cp_w_out = _linear_bwd_w(cat, dh1, name="cp_out_dw")
    sent = src.poll(d_cp_w_out)
    dz0, d_conv_w, d_conv_b, d_ln_g, d_ln_b, d_pool_w, d_pool_scale = _cp_mid_bwd(z0, dcat, *cp_mid, after=sent,
                                                                                 name="cp_mid_bwd")
    dh0, dmix_g0 = _linear_bwd_x([dz0], [cp["cp_w_in"]], norm=(h0, mix_g[0:1], dh1), name="cp_in_dx")
    d_cp_w_in = _linear_bwd_w(u0, dz0, name="cp_in_dw")

    small = dict(
        mix_g=jnp.concatenate([dmix_g0, dmix_g1]), ffn_g=jnp.concatenate([dffn_g0, dffn_g1]), conv_b=d_conv_b, ln_g=d_ln_g,
        ln_b=d_ln_b, pool_w=d_pool_w, pool_scale=d_pool_scale, final_g=d_final_g, meta=dh0[PAD_ROWS:CHUNK], conv_w=d_conv_w,
        gate_w=d_gate_w, gate_b=d_gate_b, head_g=d_head_g)
    src.emit("cp", dict(cp_w_in=d_cp_w_in, cp_w_out=d_cp_w_out, small=small))
    return loss, dh0[CHUNK:], small


_REPLICATED = ("mix_norm_g", "ffn_norm_g", "cp_conv_b", "cp_ln_g", "cp_ln_b", "cp_pool_w", "cp_pool_scale", "final_norm_g")
_SMALL_SHARDED = ("meta_tokens", "cp_conv_w", "gla_gate_w2", "gla_gate_b", "gla_head_g")
_LARGE = ("ffn_w1", "ffn_w2", "cp_w_in", "cp_w_out", "gla_w_in", "gla_w_out")
_NAMES = ("meta_tokens", "mix_norm_g", "ffn_norm_g", "ffn_w1", "ffn_w2", "cp_w_in", "cp_conv_w", "cp_conv_b", "cp_ln_g",
          "cp_ln_b", "cp_pool_w", "cp_pool_scale", "cp_w_out", "gla_w_in", "gla_gate_w2", "gla_gate_b", "gla_head_g",
          "gla_w_out", "final_norm_g")
_SMALL_GRADS = ("mix_g", "ffn_g", "conv_b", "ln_g", "ln_b", "pool_w", "pool_scale", "final_g", "meta", "conv_w", "gate_w",
                "gate_b", "head_g")
_GROUPS = ("cp", "ffn0", "gla", "ffn1")


class _Exchanges:
    def __init__(self, w, d):
        self.d = d
        self.small_shards = [w[n] for n in _SMALL_SHARDED]
        shards = dict(
            cp=[(w["cp_w_in"][0], BF16), (w["cp_w_out"][0], BF16), (_pack(self.small_shards), F32)],
            ffn0=[(w["ffn_w1"][0], BF16), (w["ffn_w2"][0], BF16)],
            gla=[(w["gla_w_in"][0], BF16), (w["gla_w_out"][0], BF16)],
            ffn1=[(w["ffn_w1"][1], BF16), (w["ffn_w2"][1], BF16)])
        self.gathers = {}
        self.to_sibling, self.to_chips = [], {}
        token = None
        for group in _GROUPS:
            lands = [_place_own(a, dtype, name=f"place_w_{group}_{k}") for k, (a, dtype) in enumerate(shards[group])]
            self.gathers[group], token = _exchange_start(_plan_to_all_in_place, None, lands, after=token,
                                                         name=f"start_w_{group}")
        self.token = token

    def fetch(self, group, after):
        d = self.d
        _, got = _exchange_wait(self.gathers[group], self.token if group == _GROUPS[0] else after, name=f"wait_w_{group}")
        if group in ("ffn0", "ffn1"):
            return dict(w1=got[0], w2=got[1])
        if group == "gla":
            qkvg = 3 * d
            w_in = _undo_column_split(got[0])
            return dict(gla_w_qkvg=w_in[:, :qkvg], gla_w_r=jnp.pad(w_in[:, qkvg:], ((0, 0), (0, GATE_PAD - GATE_RANK))),
                        gla_w_out=got[1].reshape(d, d), gate_w=self.gate_w, gate_b=self.gate_b, head_g=self.head_g)
        shapes = [s.shape for s in self.small_shards]
        parts = [_unpack(got[2][dev], shapes) for dev in range(N_DEV)]
        meta, conv_w, gate_w, self.gate_b, self.head_g = [
            jnp.concatenate([parts[dev][k] for dev in range(N_DEV)], axis=-1) for k in range(len(shapes))]
        self.gate_w = jnp.pad(gate_w[0], ((0, GATE_PAD - GATE_RANK), (0, 0))).astype(BF16)
        return dict(cp_w_in=_undo_column_split(got[0]), cp_w_out=got[1].reshape(d, d), meta=meta,
                    conv_w=jnp.pad(conv_w[0], ((0, 1), (0, 0))))

    def emit(self, group, g):
        d = self.d
        if group in ("ffn0", "ffn1"):
            arrs = [g["w1"], g["w2"]]
        elif group == "gla":
            w_in = jnp.concatenate([g["gla_w_qkvg"], g["gla_w_r"][:, :GATE_RANK]], axis=1)
            arrs = [_column_split(w_in), g["gla_w_out"].reshape(N_DEV, d // N_DEV, d)]
        else:
            s = dict(g["small"])
            s.update(pool_w=s["pool_w"][None], conv_w=s["conv_w"][None, :CONV_WIDTH], gate_w=s["gate_w"][None, :GATE_RANK])
            self.small_grads = [s[n] for n in _SMALL_GRADS]
            self.small_own = _pack(self.small_grads)
            self.small_sent, self.token = _exchange_start(
                _plan_to_all, [self.small_own], [lax.empty((N_DEV,) + self.small_own.shape, F32)], after=self.token,
                name="start_g_small")
            arrs = [_column_split(g["cp_w_in"]), g["cp_w_out"].reshape(N_DEV, d // N_DEV, d)]
        state, self.token = _exchange_start(_plan_to_sibling, arrs, None, after=self.token, name=f"start_g1_{group}")
        self.to_sibling.append((group, state))
        return self.token

    def poll(self, after):
        for group, state in self.to_sibling:
            arrs, landed = _exchange_wait(state, after, name=f"wait_g1_{group}")
            sums = [_pair_add(a, l, name=f"chip_sum_{group}_{k}") for k, (a, l) in enumerate(zip(arrs, landed))]
            self.to_chips[group], self.token = _exchange_start(_plan_to_chips, sums, None, after=self.token,
                                                               name=f"start_g2_{group}")
            after = self.token
        self.to_sibling = []
        return self.token

    def finish(self, w, mom, var):
        out = {}
        self.poll(self.token)
        after = self.token

        def landed(group):
            sums, got = _exchange_wait(self.to_chips[group], after, name=f"wait_g2_{group}")
            return list(zip(sums, got))

        def adam(n, parts, behind=None):
            out[n] = _reduce_adam(parts, w[n], mom[n], var[n], after=behind, name=f"adam_{n}")
            return out[n][0]

        ffn1 = landed("ffn1")
        after = ffn1[0][1]
        gla = landed("gla")
        after = gla[0][1]
        ffn0 = landed("ffn0")
        after = adam("ffn_w1", [ffn0[0], ffn1[0]])
        after = adam("ffn_w2", [ffn0[1], ffn1[1]], after)
        after = adam("gla_w_in", [gla[0]], after)
        after = adam("gla_w_out", [gla[1]], after)
        cp = landed("cp")
        adam("cp_w_in", [cp[0]])
        after = adam("cp_w_out", [cp[1]])
        (self.small_own,), (small_all,) = _exchange_wait(self.small_sent, after, name="wait_g_small")

        names = _REPLICATED + _SMALL_SHARDED
        small_sum = _unpack(_reduce8(self.small_own, small_all, name="sum_small_grads"), [a.shape for a in self.small_grads])
        me = _my_slot()
        grad = {}
        for n, full in zip(names, small_sum):
            if n in _SMALL_SHARDED:
                width = w[n].shape[-1]
                full = lax.dynamic_slice_in_dim(full, me * width, width, axis=full.ndim - 1)
            grad[n] = full
        packed = [_pack([t[n] for n in names]) for t in (w, grad, mom, var)]
        small_out = [_unpack(p, [w[n].shape for n in names]) for p in _adamw_small(*packed, name="adam_small")]
        for k, n in enumerate(names):
            out[n] = (grad[n], small_out[0][k], small_out[1][k], small_out[2][k])
        return out


def kernel(x, meta_tokens, mix_norm_g, ffn_norm_g, ffn_w1, ffn_w2, cp_w_in, cp_conv_w, cp_conv_b, cp_ln_g, cp_ln_b, cp_pool_w, cp_pool_scale, cp_w_out, gla_w_in, gla_gate_w2, gla_gate_b, gla_head_g, gla_w_out, final_norm_g, loss_target, m_meta_tokens, m_mix_norm_g, m_ffn_norm_g, m_ffn_w1, m_ffn_w2, m_cp_w_in, m_cp_conv_w, m_cp_conv_b, m_cp_ln_g, m_cp_ln_b, m_cp_pool_w, m_cp_pool_scale, m_cp_w_out, m_gla_w_in, m_gla_gate_w2, m_gla_gate_b, m_gla_head_g, m_gla_w_out, m_final_norm_g, v_meta_tokens, v_mix_norm_g, v_ffn_norm_g, v_ffn_w1, v_ffn_w2, v_cp_w_in, v_cp_conv_w, v_cp_conv_b, v_cp_ln_g, v_cp_ln_b, v_cp_pool_w, v_cp_pool_scale, v_cp_w_out, v_gla_w_in, v_gla_gate_w2, v_gla_gate_b, v_gla_head_g, v_gla_w_out, v_final_norm_g):
    w = dict(meta_tokens=meta_tokens, mix_norm_g=mix_norm_g, ffn_norm_g=ffn_norm_g, ffn_w1=ffn_w1, ffn_w2=ffn_w2,
             cp_w_in=cp_w_in, cp_conv_w=cp_conv_w, cp_conv_b=cp_conv_b, cp_ln_g=cp_ln_g, cp_ln_b=cp_ln_b,
             cp_pool_w=cp_pool_w, cp_pool_scale=cp_pool_scale, cp_w_out=cp_w_out, gla_w_in=gla_w_in,
             gla_gate_w2=gla_gate_w2, gla_gate_b=gla_gate_b, gla_head_g=gla_head_g, gla_w_out=gla_w_out,
             final_norm_g=final_norm_g.reshape(1, -1))
    mom = dict(meta_tokens=m_meta_tokens, mix_norm_g=m_mix_norm_g, ffn_norm_g=m_ffn_norm_g, ffn_w1=m_ffn_w1, ffn_w2=m_ffn_w2,
               cp_w_in=m_cp_w_in, cp_conv_w=m_cp_conv_w, cp_conv_b=m_cp_conv_b, cp_ln_g=m_cp_ln_g, cp_ln_b=m_cp_ln_b,
               cp_pool_w=m_cp_pool_w, cp_pool_scale=m_cp_pool_scale, cp_w_out=m_cp_w_out, gla_w_in=m_gla_w_in,
               gla_gate_w2=m_gla_gate_w2, gla_gate_b=m_gla_gate_b, gla_head_g=m_gla_head_g, gla_w_out=m_gla_w_out,
               final_norm_g=m_final_norm_g.reshape(1, -1))
    var = dict(meta_tokens=v_meta_tokens, mix_norm_g=v_mix_norm_g, ffn_norm_g=v_ffn_norm_g, ffn_w1=v_ffn_w1, ffn_w2=v_ffn_w2,
               cp_w_in=v_cp_w_in, cp_conv_w=v_cp_conv_w, cp_conv_b=v_cp_conv_b, cp_ln_g=v_cp_ln_g, cp_ln_b=v_cp_ln_b,
               cp_pool_w=v_cp_pool_w, cp_pool_scale=v_cp_pool_scale, cp_w_out=v_cp_w_out, gla_w_in=v_gla_w_in,
               gla_gate_w2=v_gla_gate_w2, gla_gate_b=v_gla_gate_b, gla_head_g=v_gla_head_g, gla_w_out=v_gla_w_out,
               final_norm_g=v_final_norm_g.reshape(1, -1))
    d = x.shape[-1]
    replicated = dict(mix_g=w["mix_norm_g"], ffn_g=w["ffn_norm_g"], conv_b=w["cp_conv_b"], ln_g=w["cp_ln_g"],
                      ln_b=w["cp_ln_b"], pool_w=w["cp_pool_w"][0].astype(BF16), pool_scale=w["cp_pool_scale"],
                      final_g=w["final_norm_g"])
    exchanges = _Exchanges(w, d)
    loss_blk, grad_x, _ = _local_step(x[0], loss_target[0], replicated, exchanges)
    loss = lax.psum(loss_blk[0, 0], ("x", "y", "c"))
    out = exchanges.finish(w, mom, var)

    def leaf(n, k):
        a = out[n][k]
        return a.reshape(-1) if n == "final_norm_g" else a

    return (loss, grad_x[None], *[leaf(n, 0) for n in _NAMES], *[leaf(n, 1) for n in _NAMES],
            *[leaf(n, 2) for n in _NAMES], *[leaf(n, 3) for n in _NAMES])
```

```python
import functools

import jax
import jax.numpy as jnp
from jax import lax
from jax.experimental import pallas as pl
from jax.experimental.pallas import tpu as pltpu

F32, BF16 = jnp.float32, jnp.bfloat16
N_DEV = 8
CHUNK = 64
N_META = 16
PAD_ROWS = CHUNK - N_META
HALO = 32
EPS = 1e-5
CONV_WIDTH = 31
POOL_WINDOWS = (2, 4, 8, 16)
HEADS = 4
GATE_RANK = 16
GATE_NORM = 16.0
GATE_PAD = 128
ADAM_LR, ADAM_B1, ADAM_B2, ADAM_EPS, ADAM_WD, ADAM_STEP = 0.001, 0.9, 0.999, 1e-08, 0.01, 10
V7X_VMEM_LIMIT = 56 * 2 ** 20
LANE = 128
HI = lax.Precision.HIGHEST


def _cparams(*sem):
    return pltpu.CompilerParams(dimension_semantics=sem, vmem_limit_bytes=V7X_VMEM_LIMIT)


def _row_tile(t, cap):
    best = CHUNK
    for r in range(CHUNK, min(t, cap) + 1, CHUNK):
        if t % r == 0:
            best = r
    return best


def _resident(shape):
    return pl.BlockSpec(shape, lambda *_: (0,) * len(shape), pipeline_mode=pl.Buffered(1))


def _dot(a, b):
    return jnp.dot(a, b, preferred_element_type=F32)


def _dot_nt(a, b):
    return lax.dot_general(a, b, (((1,), (1,)), ((), ())), preferred_element_type=F32)


def _dot_tn(a, b):
    return lax.dot_general(a, b, (((0,), (0,)), ((), ())), preferred_element_type=F32)


def _rowsum(a):
    return jnp.sum(a, axis=0, keepdims=True)


def _sigmoid(a):
    return 1.0 / (1.0 + jnp.exp(-a))


def _row_ids(tile, rt):
    return tile * rt + lax.broadcasted_iota(jnp.int32, (rt, 1), 0)


def _sds(shape, dtype):
    return jax.ShapeDtypeStruct(shape, dtype)


def _linear_fwd(x, w, *, gain=None, res=None, out_dtype=F32, name):
    t, k = x.shape
    n = w.shape[1]
    rt = _row_tile(t, 320)

    def body(*refs):
        refs = list(refs)
        x_ref, w_ref = refs[:2]
        pos = 2
        g_ref = r_ref = u_ref = None
        if gain is not None:
            g_ref = refs[pos]
            pos += 1
        if res is not None:
            r_ref = refs[pos]
            pos += 1
        y_ref = refs[pos]
        if gain is not None:
            u_ref = refs[pos + 1]
            xv = x_ref[...]
            u = (xv * lax.rsqrt(jnp.mean(xv * xv, axis=-1, keepdims=True) + EPS) * g_ref[...]).astype(BF16)
            u_ref[...] = u
        else:
            u = x_ref[...]
        y = _dot(u, w_ref[...])
        if res is not None:
            y = y + r_ref[...]
        y_ref[...] = y.astype(y_ref.dtype)

    rows = lambda i: (i, 0)
    in_specs = [pl.BlockSpec((rt, k), rows), _resident((k, n))]
    args = [x, w]
    if gain is not None:
        in_specs.append(_resident((1, k)))
        args.append(gain)
    if res is not None:
        in_specs.append(pl.BlockSpec((rt, n), rows))
        args.append(res)
    out_shape = [_sds((t, n), out_dtype)]
    out_specs = [pl.BlockSpec((rt, n), rows)]
    if gain is not None:
        out_shape.append(_sds((t, k), BF16))
        out_specs.append(pl.BlockSpec((rt, k), rows))
    out = pl.pallas_call(body, grid=(t // rt,), in_specs=in_specs, out_specs=out_specs, out_shape=out_shape,
                         compiler_params=_cparams("parallel"), name=name)(*args)
    return out if gain is not None else out[0]


def _linear_bwd_x(dys, ws, *, norm=None, after=None, name):
    t = dys[0].shape[0]
    k = ws[0].shape[0]
    rt = _row_tile(t, 320)
    nd = len(dys)
    dep_specs, deps = _dep_specs(after)

    def body(*refs):
        refs = list(refs)[len(deps):]
        dy_refs, w_refs = refs[:nd], refs[nd:2 * nd]
        dx = None
        for dy_ref, w_ref in zip(dy_refs, w_refs):
            part = _dot_nt(dy_ref[...].astype(BF16), w_ref[...])
            dx = part if dx is None else dx + part
        if norm is None:
            refs[2 * nd][...] = dx
            return
        h_ref, g_ref, dres_ref, dh_ref, dg_ref = refs[2 * nd:]
        hv = h_ref[...]
        rstd = lax.rsqrt(jnp.mean(hv * hv, axis=-1, keepdims=True) + EPS)
        xh = hv * rstd

        @pl.when(pl.program_id(0) == 0)
        def _():
            dg_ref[...] = jnp.zeros_like(dg_ref)

        dg_ref[...] += _rowsum(dx * xh)
        dxh = dx * g_ref[...]
        dh_ref[...] = dres_ref[...] + rstd * (dxh - xh * jnp.mean(dxh * xh, axis=-1, keepdims=True))

    rows = lambda i: (i, 0)
    in_specs = [pl.BlockSpec((rt, dy.shape[1]), rows) for dy in dys] + [_resident(w.shape) for w in ws]
    args = list(dys) + list(ws)
    out_shape = [_sds((t, k), F32)]
    out_specs = [pl.BlockSpec((rt, k), rows)]
    if norm is not None:
        h, gain, dres = norm
        in_specs += [pl.BlockSpec((rt, k), rows), _resident((1, k)), pl.BlockSpec((rt, k), rows)]
        args += [h, gain, dres]
        out_shape.append(_sds((1, k), F32))
        out_specs.append(pl.BlockSpec((1, k), lambda i: (0, 0)))
    out = pl.pallas_call(body, grid=(t // rt,), in_specs=dep_specs + in_specs, out_specs=out_specs, out_shape=out_shape,
                         compiler_params=_cparams("arbitrary"), name=name)(*deps, *args)
    return out if norm is not None else out[0]


def _linear_bwd_w(x, dy, *, name):
    t, k = x.shape
    n = dy.shape[1]
    rt = _row_tile(t, 832)
    nt = max(c for c in (512, 384, 256, LANE) if n % c == 0)

    def body(x_ref, dy_ref, o_ref):
        @pl.when(pl.program_id(1) == 0)
        def _():
            o_ref[...] = jnp.zeros_like(o_ref)

        o_ref[...] += _dot_tn(x_ref[...], dy_ref[...].astype(BF16))

    return pl.pallas_call(
        body, grid=(n // nt, t // rt),
        in_specs=[pl.BlockSpec((rt, k), lambda j, i: (i, 0)), pl.BlockSpec((rt, nt), lambda j, i: (i, j))],
        out_specs=pl.BlockSpec((k, nt), lambda j, i: (0, j)), out_shape=_sds((k, n), F32),
        compiler_params=_cparams("parallel", "arbitrary"), name=name)(x, dy)


def _ffn_fwd(h, gain, w1g, w2g, *, name):
    t, d = h.shape
    f8 = w1g.shape[-1]
    rt = _row_tile(t, 832)

    def body(h_ref, g_ref, w1_ref, w2_ref, o_ref, u_ref, r_ref, acc_ref):
        j = pl.program_id(1)

        @pl.when(j == 0)
        def _():
            hv = h_ref[...]
            u_ref[...] = (hv * lax.rsqrt(jnp.mean(hv * hv, axis=-1, keepdims=True) + EPS) * g_ref[...]).astype(BF16)
            acc_ref[...] = jnp.zeros_like(acc_ref)

        a = jnp.maximum(_dot(u_ref[...], w1_ref[...]), 0.0)
        r_ref[...] = a.astype(BF16)
        acc_ref[...] += _dot((a * a).astype(BF16), w2_ref[...])

        @pl.when(j == N_DEV - 1)
        def _():
            o_ref[...] = h_ref[...] + acc_ref[...]

    return pl.pallas_call(
        body, grid=(t // rt, N_DEV),
        in_specs=[pl.BlockSpec((rt, d), lambda i, j: (i, 0)), _resident((1, d)),
                  pl.BlockSpec((None, d, f8), lambda i, j: (j, 0, 0)),
                  pl.BlockSpec((None, f8, d), lambda i, j: (j, 0, 0))],
        out_specs=[pl.BlockSpec((rt, d), lambda i, j: (i, 0)), pl.BlockSpec((rt, d), lambda i, j: (i, 0)),
                   pl.BlockSpec((rt, f8), lambda i, j: (i, j))],
        out_shape=[_sds((t, d), F32), _sds((t, d), BF16), _sds((t, N_DEV * f8), BF16)],
        scratch_shapes=[pltpu.VMEM((rt, d), F32)],
        compiler_params=_cparams("parallel", "arbitrary"), name=name)(h, gain, w1g, w2g)


def _ffn_bwd_x(h, dout, gain, r, w1g, w2g, *, after=None, name):
    t, d = h.shape
    f8 = w1g.shape[-1]
    rt = _row_tile(t, 832)
    last = N_DEV - 1
    dep_specs, deps = _dep_specs(after)

    def body(*refs):
        h_ref, do_ref, g_ref, r_ref, w1_ref, w2_ref, dh_ref, dhh_ref, dob_ref, dg_ref, du_ref = refs[len(deps):]
        i, j = pl.program_id(0), pl.program_id(1)

        @pl.when(j == 0)
        def _():
            dob_ref[...] = do_ref[...].astype(BF16)
            du_ref[...] = jnp.zeros_like(du_ref)

        dhh = (_dot_nt(dob_ref[...], w2_ref[...]) * (2.0 * r_ref[...].astype(F32))).astype(BF16)
        dhh_ref[...] = dhh
        du_ref[...] += _dot_nt(dhh, w1_ref[...])

        @pl.when(j == last)
        def _():
            @pl.when(i == 0)
            def _():
                dg_ref[...] = jnp.zeros_like(dg_ref)

            hv = h_ref[...]
            rstd = lax.rsqrt(jnp.mean(hv * hv, axis=-1, keepdims=True) + EPS)
            xh = hv * rstd
            du = du_ref[...]
            dg_ref[...] += _rowsum(du * xh)
            dxh = du * g_ref[...]
            dh_ref[...] = do_ref[...] + rstd * (dxh - xh * jnp.mean(dxh * xh, axis=-1, keepdims=True))

    rows = lambda i, j: (i, 0)
    return pl.pallas_call(
        body, grid=(t // rt, N_DEV),
        in_specs=dep_specs + [
                  pl.BlockSpec((rt, d), rows), pl.BlockSpec((rt, d), rows), _resident((1, d)),
                  pl.BlockSpec((rt, f8), lambda i, j: (i, j)),
                  pl.BlockSpec((None, d, f8), lambda i, j: (j, 0, 0)),
                  pl.BlockSpec((None, f8, d), lambda i, j: (j, 0, 0))],
        out_specs=[pl.BlockSpec((rt, d), rows), pl.BlockSpec((rt, f8), lambda i, j: (i, j)), pl.BlockSpec((rt, d), rows),
                   pl.BlockSpec((1, d), lambda i, j: (0, 0))],
        out_shape=[_sds((t, d), F32), _sds((t, N_DEV * f8), BF16), _sds((t, d), BF16), _sds((1, d), F32)],
        scratch_shapes=[pltpu.VMEM((rt, d), F32)],
        compiler_params=_cparams("arbitrary", "arbitrary"), name=name)(*deps, h, dout, gain, r, w1g, w2g)


FFN_DW_ROWS = 1024


def _ffn_bwd_w(u, dhh, r, dout_b, *, name):
    t, d = u.shape
    f8 = dhh.shape[1] // N_DEV

    def body(u_ref, dhh_ref, r_ref, dob_ref, dw1_ref, dw2_ref):
        for c0 in range(0, t, FFN_DW_ROWS):
            rows = slice(c0, min(c0 + FFN_DW_ROWS, t))
            rr = r_ref[rows, :].astype(F32)
            part1 = _dot_tn(u_ref[rows, :], dhh_ref[rows, :])
            part2 = _dot_tn((rr * rr).astype(BF16), dob_ref[rows, :])
            if c0 == 0:
                dw1_ref[...] = part1
                dw2_ref[...] = part2
            else:
                dw1_ref[...] += part1
                dw2_ref[...] += part2

    return pl.pallas_call(
        body, grid=(N_DEV,),
        in_specs=[_resident((t, d)), pl.BlockSpec((t, f8), lambda j: (0, j)), pl.BlockSpec((t, f8), lambda j: (0, j)),
                  _resident((t, d))],
        out_specs=[pl.BlockSpec((None, d, f8), lambda j: (j, 0, 0)), pl.BlockSpec((None, f8, d), lambda j: (j, 0, 0))],
        out_shape=[_sds((N_DEV, d, f8), F32), _sds((N_DEV, f8, d), F32)],
        compiler_params=_cparams("parallel"), name=name)(u, dhh, r, dout_b)


def _lane_blocks(width):
    lb = min(LANE, width)
    return [slice(s, s + lb) for s in range(0, width, lb)]


def _conv_rows(src_ref, w_ref, offset, dst_ref, nblk, width, bias_ref=None):
    def blk(rb, carry):
        base = pl.multiple_of(rb * CHUNK, CHUNK)
        for l, ls in enumerate(_lane_blocks(width)):
            acc = jnp.zeros((CHUNK, ls.stop - ls.start), F32)
            if bias_ref is not None:
                acc = acc + bias_ref[:, ls]
            for k in range(CONV_WIDTH):
                acc = acc + w_ref[k:k + 1, ls] * src_ref[l, pl.ds(base + offset(k), CHUNK), :]
            dst_ref[l, pl.ds(base, CHUNK), :] = acc
        return carry

    lax.fori_loop(0, nblk, blk, 0)


def _to_lane_blocks(ref, row0, value):
    for l, ls in enumerate(_lane_blocks(value.shape[1])):
        ref[l, row0:row0 + value.shape[0], :] = value[:, ls]


def _from_lane_blocks(ref):
    return jnp.concatenate([ref[l] for l in range(ref.shape[0])], axis=1)


def _pool_counts(rows, window):
    return jnp.clip(rows - PAD_ROWS + 1, 1, window).astype(F32)


def _trailing_sum(v, window):
    s, sh = v, 1
    while sh < window:
        s = s + pltpu.roll(s, sh, 0)
        sh *= 2
    return s


def _leading_sum(v, window):
    s, sh, n = v, 1, v.shape[0]
    while sh < window:
        s = s + pltpu.roll(s, n - sh, 0)
        sh *= 2
    return s


def _cp_mid_fwd(z, conv_w, conv_b, ln_g, ln_b, pool_w, pool_scale, *, name):
    t, ein = z.shape
    cd = conv_b.shape[1]
    pd = pool_scale.shape[1]
    pg = pd // len(POOL_WINDOWS)
    rt = _row_tile(t, 320)

    def body(z_ref, cw_ref, cb_ref, lg_ref, lb_ref, pw_ref, ps_ref, o_ref, gext, pext, conv_s):
        i = pl.program_id(0)

        @pl.when(i == 0)
        def _():
            _to_lane_blocks(gext, 0, jnp.zeros((HALO, cd), F32))
            pext[0:HALO, :] = jnp.zeros((HALO, pd), F32)

        _to_lane_blocks(gext, HALO, z_ref[:, 0:cd] * _sigmoid(z_ref[:, cd:2 * cd]))
        pext[HALO:HALO + rt, :] = z_ref[:, 2 * cd:]
        _conv_rows(gext, cw_ref, lambda k: k + HALO - (CONV_WIDTH - 1), conv_s, rt // CHUNK, cd, cb_ref)
        cv = _from_lane_blocks(conv_s)
        xc = cv - jnp.mean(cv, axis=-1, keepdims=True)
        y = xc * lax.rsqrt(jnp.mean(xc * xc, axis=-1, keepdims=True) + EPS) * lg_ref[...] + lb_ref[...]
        rows = _row_ids(i, rt)
        a = jnp.where(rows >= PAD_ROWS, y * _sigmoid(y), 0.0)
        o_ref[:, 0:cd] = a.astype(BF16)
        for gi, window in enumerate(POOL_WINDOWS):
            ls = slice(gi * pg, (gi + 1) * pg)
            v = pext[:, ls]
            tm = _trailing_sum(v, window)[HALO:] / _pool_counts(rows, window) - v[HALO:]
            p = _dot(tm.astype(BF16), pw_ref[gi]) * ps_ref[:, ls]
            o_ref[:, cd + gi * pg:cd + (gi + 1) * pg] = p.astype(BF16)
        gext[:, 0:HALO, :] = gext[:, rt:rt + HALO, :]
        pext[0:HALO, :] = pext[rt:rt + HALO, :]

    nl, lb = len(_lane_blocks(cd)), min(LANE, cd)
    return pl.pallas_call(
        body, grid=(t // rt,),
        in_specs=[pl.BlockSpec((rt, ein), lambda i: (i, 0)), _resident(conv_w.shape), _resident((1, cd)),
                  _resident((1, cd)), _resident((1, cd)), _resident(pool_w.shape), _resident((1, pd))],
        out_specs=pl.BlockSpec((rt, cd + pd), lambda i: (i, 0)), out_shape=_sds((t, cd + pd), BF16),
        scratch_shapes=[pltpu.VMEM((nl, rt + HALO, lb), F32), pltpu.VMEM((rt + HALO, pd), F32),
                        pltpu.VMEM((nl, rt, lb), F32)],
        compiler_params=_cparams("arbitrary"), name=name)(z, conv_w, conv_b, ln_g, ln_b, pool_w, pool_scale)


def _cp_mid_bwd(z, dcat, conv_w, conv_b, ln_g, ln_b, pool_w, pool_scale, *, after=None, name):
    t, ein = z.shape
    cd = conv_b.shape[1]
    pd = pool_scale.shape[1]
    pg = pd // len(POOL_WINDOWS)
    rt = _row_tile(t, 320)
    ntile = t // rt
    per = rt // CHUNK
    dep_specs, deps = _dep_specs(after)

    def body(*refs):
        (z_ref, zh_ref, dc_ref, cw_ref, cb_ref, lg_ref, lb_ref, pw_ref, ps_ref,
         dz_ref, dcw_ref, dcb_ref, dlg_ref, dlb_ref, dpw_ref, dps_ref, gext, pext, conv_s, dcv, dsp) = refs[len(deps):]
        step = pl.program_id(0)
        tile = ntile - 1 - step

        @pl.when(step == 0)
        def _():
            for ref in (dcw_ref, dcb_ref, dlg_ref, dlb_ref, dpw_ref, dps_ref):
                ref[...] = jnp.zeros_like(ref)
            _to_lane_blocks(dcv, rt, jnp.zeros((HALO, cd), F32))
            dsp[rt:rt + HALO, :] = jnp.zeros((HALO, pd), F32)

        keep = jnp.where(tile > 0, 1.0, 0.0)
        zh = zh_ref[CHUNK - HALO:CHUNK, :]
        _to_lane_blocks(gext, 0, keep * zh[:, 0:cd] * _sigmoid(zh[:, cd:2 * cd]))
        pext[0:HALO, :] = keep * zh[:, 2 * cd:]
        za = z_ref[:, 0:cd]
        sg = _sigmoid(z_ref[:, cd:2 * cd])
        _to_lane_blocks(gext, HALO, za * sg)
        pext[HALO:HALO + rt, :] = z_ref[:, 2 * cd:]
        _conv_rows(gext, cw_ref, lambda k: k + HALO - (CONV_WIDTH - 1), conv_s, per, cd, cb_ref)
        cv = _from_lane_blocks(conv_s)
        xc = cv - jnp.mean(cv, axis=-1, keepdims=True)
        rstd = lax.rsqrt(jnp.mean(xc * xc, axis=-1, keepdims=True) + EPS)
        xh = xc * rstd
        y = xh * lg_ref[...] + lb_ref[...]
        sy = _sigmoid(y)
        rows = _row_ids(tile, rt)
        da = jnp.where(rows >= PAD_ROWS, dc_ref[:, 0:cd], 0.0)
        dy = da * (sy * (1.0 + y * (1.0 - sy)))
        dlg_ref[...] += _rowsum(dy * xh)
        dlb_ref[...] += _rowsum(dy)
        dxh = dy * lg_ref[...]
        dconv = rstd * (dxh - jnp.mean(dxh, axis=-1, keepdims=True) - xh * jnp.mean(dxh * xh, axis=-1, keepdims=True))
        dcb_ref[...] += _rowsum(dconv)
        _to_lane_blocks(dcv, 0, dconv)
        for l, ls in enumerate(_lane_blocks(cd)):
            def acc_rows(rb, accs, l=l):
                base = pl.multiple_of(rb * CHUNK, CHUNK)
                d_blk = dcv[l, pl.ds(base, CHUNK), :]
                out = []
                for k in range(CONV_WIDTH):
                    prod = d_blk * gext[l, pl.ds(base + k + HALO - (CONV_WIDTH - 1), CHUNK), :]
                    part = prod[0:8]
                    for s in range(8, CHUNK, 8):
                        part = part + prod[s:s + 8]
                    out.append(accs[k] + part)
                return tuple(out)

            zero = jnp.zeros((8, ls.stop - ls.start), F32)
            accs = lax.fori_loop(0, per, acc_rows, (zero,) * CONV_WIDTH)
            for k in range(CONV_WIDTH):
                dcw_ref[k:k + 1, ls] += _rowsum(accs[k])
        _conv_rows(dcv, cw_ref, lambda k: CONV_WIDTH - 1 - k, conv_s, per, cd)
        dglu = _from_lane_blocks(conv_s)
        dz_ref[:, 0:cd] = (dglu * sg).astype(BF16)
        dz_ref[:, cd:2 * cd] = (dglu * za * sg * (1.0 - sg)).astype(BF16)
        dcv[:, rt:rt + HALO, :] = dcv[:, 0:HALO, :]
        for gi, window in enumerate(POOL_WINDOWS):
            ls = slice(gi * pg, (gi + 1) * pg)
            v = pext[:, ls]
            cnt = _pool_counts(rows, window)
            tm = (_trailing_sum(v, window)[HALO:] / cnt - v[HALO:]).astype(BF16)
            dp = dc_ref[:, cd + gi * pg:cd + (gi + 1) * pg]
            dps_ref[:, ls] += _rowsum(dp * _dot(tm, pw_ref[gi]))
            dpl = (dp * ps_ref[:, ls]).astype(BF16)
            dpw_ref[gi] += _dot_tn(tm, dpl)
            dtm = _dot_nt(dpl, pw_ref[gi])
            dsp[0:rt, ls] = dtm / cnt
            dpin = _leading_sum(dsp[:, ls], window)[0:rt] - dtm
            dz_ref[:, 2 * cd + gi * pg:2 * cd + (gi + 1) * pg] = dpin.astype(BF16)
        dsp[rt:rt + HALO, :] = dsp[0:HALO, :]

    back = lambda i: (ntile - 1 - i, 0)
    halo_idx = lambda i: (jnp.maximum((ntile - 1 - i) * per - 1, 0), 0)
    const2 = lambda i: (0, 0)
    nl, lb = len(_lane_blocks(cd)), min(LANE, cd)
    return pl.pallas_call(
        body, grid=(ntile,),
        in_specs=dep_specs + [
                  pl.BlockSpec((rt, ein), back), pl.BlockSpec((CHUNK, ein), halo_idx), pl.BlockSpec((rt, cd + pd), back),
                  _resident(conv_w.shape), _resident((1, cd)), _resident((1, cd)), _resident((1, cd)),
                  _resident(pool_w.shape), _resident((1, pd))],
        out_specs=[pl.BlockSpec((rt, ein), back), pl.BlockSpec(conv_w.shape, const2), pl.BlockSpec((1, cd), const2),
                   pl.BlockSpec((1, cd), const2), pl.BlockSpec((1, cd), const2),
                   pl.BlockSpec(pool_w.shape, lambda i: (0, 0, 0)), pl.BlockSpec((1, pd), const2)],
        out_shape=[_sds((t, ein), BF16), _sds(conv_w.shape, F32), _sds((1, cd), F32), _sds((1, cd), F32),
                   _sds((1, cd), F32), _sds(pool_w.shape, F32), _sds((1, pd), F32)],
        scratch_shapes=[pltpu.VMEM((nl, rt + HALO, lb), F32), pltpu.VMEM((rt + HALO, pd), F32), pltpu.VMEM((nl, rt, lb), F32),
                        pltpu.VMEM((nl, rt + HALO, lb), F32), pltpu.VMEM((rt + HALO, pd), F32)],
        compiler_params=_cparams("arbitrary"), name=name)(*deps, z, z, dcat, conv_w, conv_b, ln_g, ln_b, pool_w, pool_scale)


def _log_decay(r_ref, gw_ref, gb_ref, rows):
    gp = _dot(r_ref[...].astype(BF16), gw_ref[...]) + gb_ref[...]
    log_sig = jnp.minimum(gp, 0.0) - jnp.log(1.0 + jnp.exp(-jnp.abs(gp)))
    return gp, jnp.where(rows >= PAD_ROWS, log_sig / GATE_NORM, 0.0)


def _tri(strict):
    r = lax.broadcasted_iota(jnp.int32, (CHUNK, CHUNK), 0)
    c = lax.broadcasted_iota(jnp.int32, (CHUNK, CHUNK), 1)
    return jnp.where(c < r if strict else c <= r, 1.0, 0.0).astype(F32)


def _gla_mid_fwd(z, r, gate_w, gate_b, head_g, *, name):
    t = z.shape[0]
    dk = gate_b.shape[1]
    hv = head_g.shape[1]
    hk = dk // HEADS
    dv = hv * HEADS
    rt = _row_tile(t, 320)
    per = rt // CHUNK
    scale = hk ** -0.5

    def body(z_ref, r_ref, gw_ref, gb_ref, hg_ref, o_ref, st_ref, s_ref, la_ref):
        i = pl.program_id(0)

        @pl.when(i == 0)
        def _():
            s_ref[...] = jnp.zeros_like(s_ref)

        _, la = _log_decay(r_ref, gw_ref, gb_ref, _row_ids(i, rt))
        la_ref[...] = la
        tri = _tri(False)

        def chunk(c, carry):
            rows = pl.ds(pl.multiple_of(c * CHUNK, CHUNK), CHUNK)
            la_c = la_ref[rows, :]
            cum = jnp.dot(tri, la_c, precision=HI, preferred_element_type=F32)
            tot = _rowsum(la_c)
            dec = jnp.exp(tot - cum)
            etot = jnp.exp(tot)
            for hd in range(HEADS):
                ks = slice(hd * hk, (hd + 1) * hk)
                q = z_ref[rows, hd * hk:(hd + 1) * hk] * scale
                kd = z_ref[rows, dk + hd * hk:dk + (hd + 1) * hk] * dec[:, ks]
                v = z_ref[rows, 2 * dk + hd * hv:2 * dk + (hd + 1) * hv]
                g = z_ref[rows, 2 * dk + dv + hd * hv:2 * dk + dv + (hd + 1) * hv]
                s_new = s_ref[hd] * etot[:, ks] + _dot_tn(v.astype(BF16), kd.astype(BF16))
                s_ref[hd] = s_new
                st_ref[c, hd] = s_new
                o = _dot_nt(q.astype(BF16), s_new.astype(BF16))
                on = o * lax.rsqrt(jnp.mean(o * o, axis=-1, keepdims=True) + EPS) * hg_ref[...]
                o_ref[rows, hd * hv:(hd + 1) * hv] = (on * (g * _sigmoid(g))).astype(BF16)
            return carry

        lax.fori_loop(0, per, chunk, 0)

    return pl.pallas_call(
        body, grid=(t // rt,),
        in_specs=[pl.BlockSpec((rt, z.shape[1]), lambda i: (i, 0)), pl.BlockSpec((rt, GATE_PAD), lambda i: (i, 0)),
                  _resident(gate_w.shape), _resident((1, dk)), _resident((1, hv))],
        out_specs=[pl.BlockSpec((rt, dv), lambda i: (i, 0)), pl.BlockSpec((per, HEADS, hv, hk), lambda i: (i, 0, 0, 0))],
        out_shape=[_sds((t, dv), BF16), _sds((t // CHUNK, HEADS, hv, hk), F32)],
        scratch_shapes=[pltpu.VMEM((HEADS, hv, hk), F32), pltpu.VMEM((rt, dk), F32)],
        compiler_params=_cparams("arbitrary"), name=name)(z, r, gate_w, gate_b, head_g)


def _gla_mid_bwd(z, r, dog, states, gate_w, gate_b, head_g, *, after=None, name):
    t = z.shape[0]
    dk = gate_b.shape[1]
    hv = head_g.shape[1]
    hk = dk // HEADS
    dv = hv * HEADS
    rt = _row_tile(t, 320)
    ntile = t // rt
    per = rt // CHUNK
    scale = hk ** -0.5
    dep_specs, deps = _dep_specs(after)

    def body(*refs):
        (z_ref, r_ref, do_ref, st_ref, stp_ref, gw_ref, gb_ref, hg_ref,
         dz_ref, dr_ref, dgw_ref, dgb_ref, dhg_ref, ds_ref, la_ref, dla_ref) = refs[len(deps):]
        step = pl.program_id(0)
        tile = ntile - 1 - step

        @pl.when(step == 0)
        def _():
            ds_ref[...] = jnp.zeros_like(ds_ref)
            dgw_ref[...] = jnp.zeros_like(dgw_ref)
            dgb_ref[...] = jnp.zeros_like(dgb_ref)
            dhg_ref[...] = jnp.zeros_like(dhg_ref)

        rows_id = _row_ids(tile, rt)
        gp, la = _log_decay(r_ref, gw_ref, gb_ref, rows_id)
        la_ref[...] = la
        tri, tri_strict = _tri(False), _tri(True)
        keep = jnp.where(tile > 0, 1.0, 0.0)

        def chunk(cc, carry):
            c = per - 1 - cc
            rows = pl.ds(pl.multiple_of(c * CHUNK, CHUNK), CHUNK)
            la_c = la_ref[rows, :]
            cum = jnp.dot(tri, la_c, precision=HI, preferred_element_type=F32)
            tot = _rowsum(la_c)
            dec = jnp.exp(tot - cum)
            etot = jnp.exp(tot)
            inside = jnp.where(c > 0, 1.0, 0.0)
            for hd in range(HEADS):
                ks = slice(hd * hk, (hd + 1) * hk)
                q = (z_ref[rows, hd * hk:(hd + 1) * hk] * scale).astype(BF16)
                k = z_ref[rows, dk + hd * hk:dk + (hd + 1) * hk]
                kd = k * dec[:, ks]
                v = z_ref[rows, 2 * dk + hd * hv:2 * dk + (hd + 1) * hv].astype(BF16)
                g = z_ref[rows, 2 * dk + dv + hd * hv:2 * dk + dv + (hd + 1) * hv]
                s_now = st_ref[c, hd]
                s_prev = inside * st_ref[jnp.maximum(c - 1, 0), hd] + (1.0 - inside) * keep * stp_ref[0, hd]
                s_b = s_now.astype(BF16)
                o = _dot_nt(q, s_b)
                rstd = lax.rsqrt(jnp.mean(o * o, axis=-1, keepdims=True) + EPS)
                oh = o * rstd
                sg = _sigmoid(g)
                d_og = do_ref[rows, hd * hv:(hd + 1) * hv]
                dz_ref[rows, 2 * dk + dv + hd * hv:2 * dk + dv + (hd + 1) * hv] = (
                    d_og * oh * hg_ref[...] * (sg * (1.0 + g * (1.0 - sg)))).astype(BF16)
                don = d_og * (g * sg)
                dhg_ref[...] += _rowsum(don * oh)
                doh = don * hg_ref[...]
                d_o = (rstd * (doh - oh * jnp.mean(doh * oh, axis=-1, keepdims=True))).astype(BF16)
                dz_ref[rows, hd * hk:(hd + 1) * hk] = (_dot(d_o, s_b) * scale).astype(BF16)
                ds_t = ds_ref[hd] + _dot_tn(d_o, q)
                ds_b = ds_t.astype(BF16)
                dkd = _dot(v, ds_b)
                dz_ref[rows, 2 * dk + hd * hv:2 * dk + (hd + 1) * hv] = _dot_nt(kd.astype(BF16), ds_b).astype(BF16)
                dtot = etot[:, ks] * _rowsum(ds_t * s_prev)
                ds_ref[hd] = ds_t * etot[:, ks]
                dz_ref[rows, dk + hd * hk:dk + (hd + 1) * hk] = (dkd * dec[:, ks]).astype(BF16)
                e = dkd * kd
                dla_ref[rows, ks] = dtot + jnp.dot(tri_strict, e, precision=HI, preferred_element_type=F32)
            return carry

        lax.fori_loop(0, per, chunk, 0, unroll=True)
        dla = jnp.where(rows_id >= PAD_ROWS, dla_ref[...], 0.0)
        dgp = dla * (1.0 / GATE_NORM) * (1.0 - _sigmoid(gp))
        dgb_ref[...] += _rowsum(dgp)
        dgp_b = dgp.astype(BF16)
        dgw_ref[...] += _dot_tn(r_ref[...].astype(BF16), dgp_b)
        dr_ref[...] = _dot_nt(dgp_b, gw_ref[...]).astype(BF16)

    back = lambda i: (ntile - 1 - i, 0)
    const2 = lambda i: (0, 0)
    return pl.pallas_call(
        body, grid=(ntile,),
        in_specs=dep_specs + [
                  pl.BlockSpec((rt, z.shape[1]), back), pl.BlockSpec((rt, GATE_PAD), back), pl.BlockSpec((rt, dv), back),
                  pl.BlockSpec((per, HEADS, hv, hk), lambda i: (ntile - 1 - i, 0, 0, 0)),
                  pl.BlockSpec((1, HEADS, hv, hk), lambda i: (jnp.maximum((ntile - 1 - i) * per - 1, 0), 0, 0, 0)),
                  _resident(gate_w.shape), _resident((1, dk)), _resident((1, hv))],
        out_specs=[pl.BlockSpec((rt, z.shape[1]), back), pl.BlockSpec((rt, GATE_PAD), back),
                   pl.BlockSpec(gate_w.shape, const2), pl.BlockSpec((1, dk), const2), pl.BlockSpec((1, hv), const2)],
        out_shape=[_sds(z.shape, BF16), _sds((t, GATE_PAD), BF16), _sds(gate_w.shape, F32), _sds((1, dk), F32),
                   _sds((1, hv), F32)],
        scratch_shapes=[pltpu.VMEM((HEADS, hv, hk), F32), pltpu.VMEM((rt, dk), F32), pltpu.VMEM((rt, dk), F32)],
        compiler_params=_cparams("arbitrary"), name=name)(*deps, z, r, dog, states, states, gate_w, gate_b, head_g)


def _head(h, gain, target, *, name):
    t, d = h.shape
    rt = _row_tile(t, 832)

    def body(h_ref, g_ref, t_ref, dh_ref, loss_ref, dg_ref):
        i = pl.program_id(0)

        @pl.when(i == 0)
        def _():
            loss_ref[...] = jnp.zeros_like(loss_ref)
            dg_ref[...] = jnp.zeros_like(dg_ref)

        hv = h_ref[...]
        rstd = lax.rsqrt(jnp.mean(hv * hv, axis=-1, keepdims=True) + EPS)
        xh = hv * rstd
        err = jnp.where(_row_ids(i, rt) >= CHUNK, xh * g_ref[...] - t_ref[...], 0.0)
        loss_ref[...] += (0.5 / d) * jnp.sum(err * err)
        dy = err * (1.0 / d)
        dg_ref[...] += _rowsum(dy * xh)
        dxh = dy * g_ref[...]
        dh_ref[...] = rstd * (dxh - xh * jnp.mean(dxh * xh, axis=-1, keepdims=True))

    return pl.pallas_call(
        body, grid=(t // rt,),
        in_specs=[pl.BlockSpec((rt, d), lambda i: (i, 0)), _resident((1, d)), pl.BlockSpec((rt, d), lambda i: (i, 0))],
        out_specs=[pl.BlockSpec((rt, d), lambda i: (i, 0)), pl.BlockSpec((8, LANE), lambda i: (0, 0)),
                   pl.BlockSpec((1, d), lambda i: (0, 0))],
        out_shape=[_sds((t, d), F32), _sds((8, LANE), F32), _sds((1, d), F32)],
        compiler_params=_cparams("arbitrary"), name=name)(h, gain, target)


def _adamw_math(w, g, m, v):
    m = ADAM_B1 * m + (1.0 - ADAM_B1) * g
    v = ADAM_B2 * v + (1.0 - ADAM_B2) * (g * g)
    m_hat = m / (1.0 - ADAM_B1 ** ADAM_STEP)
    v_hat = v / (1.0 - ADAM_B2 ** ADAM_STEP)
    return -ADAM_LR * (m_hat / (jnp.sqrt(v_hat) + ADAM_EPS) + ADAM_WD * w), m, v


N_CHIP = N_DEV // 2
BLOCK_ELEMS = 128 * 1024


def _my_slot():
    return 4 * lax.axis_index("x") + 2 * lax.axis_index("y") + lax.axis_index("c")


def _my_chip():
    return 2 * lax.axis_index("x") + lax.axis_index("y")


def _row_block(r, c):
    cap = max(8, BLOCK_ELEMS // (-(-c // LANE) * LANE))
    return max(b for b in range(8, r + 1, 8) if r % b == 0 and b <= max(cap, 8))


def _pair_add(a, landed, *, name):
    _, r, c = a.shape
    rb = _row_block(r, c)

    def body(core_ref, a_ref, l_ref, o_ref):
        o_ref[...] = a_ref[...] + l_ref[...]

    one = pl.BlockSpec((None, rb, c), lambda q, i, core: (q, i, 0))
    grid_spec = pltpu.PrefetchScalarGridSpec(
        num_scalar_prefetch=1, grid=(N_CHIP, r // rb),
        in_specs=[pl.BlockSpec((None, rb, c), lambda q, i, core: (2 * q + core[0], i, 0)), one], out_specs=one)
    return pl.pallas_call(
        body, grid_spec=grid_spec, out_shape=_sds((N_CHIP, r, c), F32),
        compiler_params=_cparams("parallel", "parallel"), name=name)(lax.axis_index("c").reshape(1), a, landed)


def _reduce_adam(parts, w, m, v, *, after=None, name):
    nl, r, c = w.shape
    rb = _row_block(r, c)
    dep_specs, deps = _dep_specs(after)

    def body(*refs):
        refs = refs[len(deps):]
        p_refs = refs[:2 * nl]
        w_ref, m_ref, v_ref, g_out, d_out, m_out, v_out = refs[2 * nl:]
        layer = pl.program_id(0)
        chip = _my_chip()
        for li in range(nl):
            @pl.when(layer == li)
            def _(li=li):
                mine_ref, land_ref = p_refs[2 * li], p_refs[2 * li + 1]
                g = None
                for q in range(N_CHIP):
                    term = jnp.where(chip == q, mine_ref[q], land_ref[q])
                    g = term if g is None else g + term
                g_out[...] = g
                d_out[...], m_out[...], v_out[...] = _adamw_math(w_ref[...], g, m_ref[...], v_ref[...])

    blk = pl.BlockSpec((None, rb, c), lambda l, i: (l, i, 0))
    p_specs = [pl.BlockSpec((N_CHIP, rb, c), lambda l, i, li=li: (0, jnp.where(l == li, i, 0), 0))
               for li in range(nl) for _ in range(2)]
    flat = [p for pair in parts for p in pair]
    return pl.pallas_call(
        body, grid=(nl, r // rb), in_specs=dep_specs + p_specs + [blk, blk, blk], out_specs=[blk] * 4,
        out_shape=[_sds(w.shape, F32)] * 4, compiler_params=_cparams("arbitrary", "arbitrary"),
        name=name)(*deps, *flat, w, m, v)


def _reduce8(own, landed, *, name):
    r, c = own.shape

    def body(own_ref, p_ref, o_ref):
        me = _my_slot()
        g = None
        for dev in range(N_DEV):
            term = jnp.where(me == dev, own_ref[...], p_ref[dev])
            g = term if g is None else g + term
        o_ref[...] = g

    return pl.pallas_call(body, out_shape=_sds((r, c), F32), name=name)(own, landed)


def _adamw_small(w, g, m, v, *, name):
    def body(w_ref, g_ref, m_ref, v_ref, d_out, m_out, v_out):
        d_out[...], m_out[...], v_out[...] = _adamw_math(w_ref[...], g_ref[...], m_ref[...], v_ref[...])

    return pl.pallas_call(body, out_shape=[_sds(w.shape, F32)] * 3, name=name)(w, g, m, v)


_HBM = pl.BlockSpec(memory_space=pltpu.HBM)
_SEM = pl.BlockSpec(memory_space=pltpu.SEMAPHORE)
_DATAFLOW = pltpu.SideEffectType.DATAFLOW_SIDE_EFFECTING


def _plan_to_all(src, land):
    x, y, c = lax.axis_index("x"), lax.axis_index("y"), lax.axis_index("c")
    return [(src, land.at[_my_slot()], (x ^ ((d >> 2) & 1), y ^ ((d >> 1) & 1), c ^ (d & 1))) for d in range(1, N_DEV)]


def _plan_to_sibling(src, land):
    x, y, c = lax.axis_index("x"), lax.axis_index("y"), lax.axis_index("c")
    return [(src.at[2 * q + 1 - c], land.at[q], (x, y, 1 - c)) for q in range(N_CHIP)]


def _plan_to_chips(src, land):
    x, y, c = lax.axis_index("x"), lax.axis_index("y"), lax.axis_index("c")
    peers = [(x ^ (d >> 1), y ^ (d & 1)) for d in range(1, N_CHIP)]
    return [(src.at[2 * px + py], land.at[_my_chip()], (px, py, c)) for px, py in peers]


_PLAN_COPIES = {_plan_to_all: N_DEV - 1, _plan_to_sibling: N_CHIP, _plan_to_chips: N_CHIP - 1}


def _exchange_copies(plan, ins, lands, send, recv):
    per = _PLAN_COPIES[plan]
    copies = []
    for a, (src, land) in enumerate(zip(ins, lands)):
        for i, (s, dst, dev) in enumerate(plan(src, land)):
            copies.append(pltpu.make_async_remote_copy(
                src_ref=s, dst_ref=dst, send_sem=send.at[a * per + i], recv_sem=recv.at[a * per + i],
                device_id=dev, device_id_type=pl.DeviceIdType.MESH))
    return copies


def _place_own(a, dtype, *, name):
    r, c = a.shape
    rb = _row_block(r, c)

    def body(me_ref, a_ref, o_ref):
        o_ref[...] = a_ref[...].astype(dtype)

    grid_spec = pltpu.PrefetchScalarGridSpec(
        num_scalar_prefetch=1, grid=(r // rb,), in_specs=[pl.BlockSpec((rb, c), lambda i, me: (i, 0))],
        out_specs=pl.BlockSpec((None, rb, c), lambda i, me: (me[0], i, 0)))
    return pl.pallas_call(body, grid_spec=grid_spec, out_shape=_sds((N_DEV, r, c), dtype),
                          compiler_params=_cparams("arbitrary"), name=name)(_my_slot().reshape(1), a)


def _plan_to_all_in_place(land, _):
    return _plan_to_all(land.at[_my_slot()], land)


_PLAN_COPIES[_plan_to_all_in_place] = N_DEV - 1


def _exchange_start(plan, arrs, lands, *, after=None, name):
    if lands is None:
        lands = [lax.empty((N_CHIP,) + a.shape[1:], a.dtype) for a in arrs]
    bufs = list(lands) if arrs is None else list(arrs) + list(lands)
    n, nb = len(lands), len(bufs)
    nsem = n * _PLAN_COPIES[plan]
    dep_specs, deps = _dep_specs(after)

    def body(*refs):
        ins, land_refs = refs[:n], refs[nb - n:nb]
        send, recv = refs[nb + len(deps)], refs[nb + len(deps) + 1]
        for cp in _exchange_copies(plan, ins, land_refs, send, recv):
            cp.start()
        refs[-1][...] = jnp.zeros_like(refs[-1])

    out = pl.pallas_call(
        body, name=name,
        out_shape=(pltpu.SemaphoreType.DMA((nsem,)), pltpu.SemaphoreType.DMA((nsem,)),
                   *[pltpu.HBM(a.shape, a.dtype) for a in bufs], _sds((8, LANE), F32)),
        in_specs=[_HBM] * nb + dep_specs,
        out_specs=(_SEM, _SEM, *([_HBM] * nb), pl.BlockSpec(memory_space=pltpu.VMEM)),
        input_output_aliases={i: 2 + i for i in range(nb)},
        compiler_params=pltpu.CompilerParams(has_side_effects=_DATAFLOW),
    )(*[pltpu.with_memory_space_constraint(a, pltpu.HBM) for a in bufs], *deps)
    return (plan, n, out[0], out[1], list(out[2:2 + nb])), out[-1]


def _exchange_wait(state, after, *, name):
    plan, n, send_sem, recv_sem, bufs = state
    nb = len(bufs)

    def body(*refs):
        ins, land_refs, send, recv = refs[:n], refs[nb - n:nb], refs[nb], refs[nb + 1]
        for cp in _exchange_copies(plan, ins, land_refs, send, recv):
            cp.wait_send()
            cp.wait_recv()

    out = pl.pallas_call(
        body, name=name, out_shape=[pltpu.HBM(a.shape, a.dtype) for a in bufs],
        in_specs=[_HBM] * nb + [_SEM, _SEM, pl.BlockSpec(memory_space=pl.ANY)], out_specs=[_HBM] * nb,
        input_output_aliases={i: i for i in range(nb)},
        compiler_params=pltpu.CompilerParams(has_side_effects=_DATAFLOW),
    )(*bufs, send_sem, recv_sem, after)
    return list(out[:n]), list(out[nb - n:])


def _dep_specs(after):
    return ([], []) if after is None else ([pl.BlockSpec(memory_space=pl.ANY)], [after])


def _pack(arrays):
    flat = jnp.concatenate([a.reshape(-1) for a in arrays])
    size = flat.shape[0]
    padded = -(-size // (8 * LANE)) * (8 * LANE)
    return jnp.pad(flat, (0, padded - size)).reshape(padded // LANE, LANE)


def _unpack(packed, shapes):
    flat = packed.reshape(-1)
    out, pos = [], 0
    for shp in shapes:
        size = 1
        for s in shp:
            size *= s
        out.append(flat[pos:pos + size].reshape(shp))
        pos += size
    return out


def _undo_column_split(g):
    return jnp.transpose(g, (1, 0, 2)).reshape(g.shape[1], N_DEV * g.shape[2])


def _column_split(a):
    r, c = a.shape
    return jnp.transpose(a.reshape(r, N_DEV, c // N_DEV), (1, 0, 2))


class _WholeWeights:
    def __init__(self, groups):
        self.groups = groups
        self.grads = {}

    def fetch(self, group, after):
        return self.groups[group]

    def emit(self, group, grads):
        self.grads.update(grads)
        return None

    def poll(self, after):
        return None


def _local_step(x, target, replicated, src):
    d = x.shape[1]
    mix_g, ffn_g = replicated["mix_g"], replicated["ffn_g"]
    cp = src.fetch("cp", x)
    h0 = jnp.concatenate([jnp.zeros((PAD_ROWS, d), F32), cp["meta"], x], axis=0)
    tgt = jnp.concatenate([jnp.zeros((CHUNK, d), F32), target], axis=0)
    cp_mid = (cp["conv_w"], replicated["conv_b"], replicated["ln_g"], replicated["ln_b"], replicated["pool_w"],
              replicated["pool_scale"])

    z0, u0 = _linear_fwd(h0, cp["cp_w_in"], gain=mix_g[0:1], name="cp_in")
    cat = _cp_mid_fwd(z0, *cp_mid, name="cp_mid")
    h1 = _linear_fwd(cat, cp["cp_w_out"], res=h0, name="cp_out")
    ffn0 = src.fetch("ffn0", h1)
    h2, uf0, rf0 = _ffn_fwd(h1, ffn_g[0:1], ffn0["w1"], ffn0["w2"], name="ffn0")
    gla = src.fetch("gla", h2)
    gla_mid = (gla["gate_w"], gla["gate_b"], gla["head_g"])
    z1, u1 = _linear_fwd(h2, gla["gla_w_qkvg"], gain=mix_g[1:2], name="gla_in")
    r1 = _linear_fwd(u1, gla["gla_w_r"], name="gla_in_r")
    og, states = _gla_mid_fwd(z1, r1, *gla_mid, name="gla_mid")
    h3 = _linear_fwd(og, gla["gla_w_out"], res=h2, name="gla_out")
    ffn1 = src.fetch("ffn1", h3)
    h4, uf1, rf1 = _ffn_fwd(h3, ffn_g[1:2], ffn1["w1"], ffn1["w2"], name="ffn1")
    dh4, loss, d_final_g = _head(h4, replicated["final_g"], tgt, name="head")

    dh3, dhh1, dob1, dffn_g1 = _ffn_bwd_x(h3, dh4, ffn_g[1:2], rf1, ffn1["w1"], ffn1["w2"], name="ffn1_bwd_x")
    dw1_1, dw2_1 = _ffn_bwd_w(uf1, dhh1, rf1, dob1, name="ffn1_bwd_w")
    sent = src.emit("ffn1", dict(w1=dw1_1, w2=dw2_1))
    dog = _linear_bwd_x([dh3], [gla["gla_w_out"]], after=sent, name="gla_out_dx")
    d_gla_w_out = _linear_bwd_w(og, dh3, name="gla_out_dw")
    sent = src.poll(d_gla_w_out)
    dz1, dr1, d_gate_w, d_gate_b, d_head_g = _gla_mid_bwd(z1, r1, dog, states, *gla_mid, after=sent, name="gla_mid_bwd")
    dh2, dmix_g1 = _linear_bwd_x([dz1, dr1], [gla["gla_w_qkvg"], gla["gla_w_r"]],
                                 norm=(h2, mix_g[1:2], dh3), name="gla_in_dx")
    d_gla_w_qkvg = _linear_bwd_w(u1, dz1, name="gla_in_dw")
    d_gla_w_r = _linear_bwd_w(u1, dr1, name="gla_in_r_dw")
    sent = src.emit("gla", dict(gla_w_qkvg=d_gla_w_qkvg, gla_w_r=d_gla_w_r, gla_w_out=d_gla_w_out))
    dh1, dhh0, dob0, dffn_g0 = _ffn_bwd_x(h1, dh2, ffn_g[0:1], rf0, ffn0["w1"], ffn0["w2"], after=sent, name="ffn0_bwd_x")
    dw1_0, dw2_0 = _ffn_bwd_w(uf0, dhh0, rf0, dob0, name="ffn0_bwd_w")
    src.poll(dw1_0)
    sent = src.emit("ffn0", dict(w1=dw1_0, w2=dw2_0))
    dcat = _linear_bwd_x([dh1], [cp["cp_w_out"]], after=sent, name="cp_out_dx")
    d_cp_w_out = _linear_bwd_w(cat, dh1, name="cp_out_dw")
    sent = src.poll(d_cp_w_out)
    dz0, d_conv_w, d_conv_b, d_ln_g, d_ln_b, d_pool_w, d_pool_scale = _cp_mid_bwd(z0, dcat, *cp_mid, after=sent,
                                                                                 name="cp_mid_bwd")
    dh0, dmix_g0 = _linear_bwd_x([dz0], [cp["cp_w_in"]], norm=(h0, mix_g[0:1], dh1), name="cp_in_dx")
    d_cp_w_in = _linear_bwd_w(u0, dz0, name="cp_in_dw")

    small = dict(
        mix_g=jnp.concatenate([dmix_g0, dmix_g1]), ffn_g=jnp.concatenate([dffn_g0, dffn_g1]), conv_b=d_conv_b, ln_g=d_ln_g,
        ln_b=d_ln_b, pool_w=d_pool_w, pool_scale=d_pool_scale, final_g=d_final_g, meta=dh0[PAD_ROWS:CHUNK], conv_w=d_conv_w,
        gate_w=d_gate_w, gate_b=d_gate_b, head_g=d_head_g)
    src.emit("cp", dict(cp_w_in=d_cp_w_in, cp_w_out=d_cp_w_out, small=small))
    return loss, dh0[CHUNK:], small


_REPLICATED = ("mix_norm_g", "ffn_norm_g", "cp_conv_b", "cp_ln_g", "cp_ln_b", "cp_pool_w", "cp_pool_scale", "final_norm_g")
_SMALL_SHARDED = ("meta_tokens", "cp_conv_w", "gla_gate_w2", "gla_gate_b", "gla_head_g")
_LARGE = ("ffn_w1", "ffn_w2", "cp_w_in", "cp_w_out", "gla_w_in", "gla_w_out")
_NAMES = ("meta_tokens", "mix_norm_g", "ffn_norm_g", "ffn_w1", "ffn_w2", "cp_w_in", "cp_conv_w", "cp_conv_b", "cp_ln_g",
          "cp_ln_b", "cp_pool_w", "cp_pool_scale", "cp_w_out", "gla_w_in", "gla_gate_w2", "gla_gate_b", "gla_head_g",
          "gla_w_out", "final_norm_g")
_SMALL_GRADS = ("mix_g", "ffn_g", "conv_b", "ln_g", "ln_b", "pool_w", "pool_scale", "final_g", "meta", "conv_w", "gate_w",
                "gate_b", "head_g")
_GROUPS = ("cp", "ffn0", "gla", "ffn1")


class _Exchanges:
    def __init__(self, w, d):
        self.d = d
        self.small_shards = [w[n] for n in _SMALL_SHARDED]
        shards = dict(
            cp=[(w["cp_w_in"][0], BF16), (w["cp_w_out"][0], BF16), (_pack(self.small_shards), F32)],
            ffn0=[(w["ffn_w1"][0], BF16), (w["ffn_w2"][0], BF16)],
            gla=[(w["gla_w_in"][0], BF16), (w["gla_w_out"][0], BF16)],
            ffn1=[(w["ffn_w1"][1], BF16), (w["ffn_w2"][1], BF16)])
        self.gathers = {}
        self.to_sibling, self.to_chips = [], {}
        token = None
        for group in _GROUPS:
            lands = [_place_own(a, dtype, name=f"place_w_{group}_{k}") for k, (a, dtype) in enumerate(shards[group])]
            self.gathers[group], token = _exchange_start(_plan_to_all_in_place, None, lands, after=token,
                                                         name=f"start_w_{group}")
        self.token = token

    def fetch(self, group, after):
        d = self.d
        _, got = _exchange_wait(self.gathers[group], self.token if group == _GROUPS[0] else after, name=f"wait_w_{group}")
        if group in ("ffn0", "ffn1"):
            return dict(w1=got[0], w2=got[1])
        if group == "gla":
            qkvg = 3 * d
            w_in = _undo_column_split(got[0])
            return dict(gla_w_qkvg=w_in[:, :qkvg], gla_w_r=jnp.pad(w_in[:, qkvg:], ((0, 0), (0, GATE_PAD - GATE_RANK))),
                        gla_w_out=got[1].reshape(d, d), gate_w=self.gate_w, gate_b=self.gate_b, head_g=self.head_g)
        shapes = [s.shape for s in self.small_shards]
        parts = [_unpack(got[2][dev], shapes) for dev in range(N_DEV)]
        meta, conv_w, gate_w, self.gate_b, self.head_g = [
            jnp.concatenate([parts[dev][k] for dev in range(N_DEV)], axis=-1) for k in range(len(shapes))]
        self.gate_w = jnp.pad(gate_w[0], ((0, GATE_PAD - GATE_RANK), (0, 0))).astype(BF16)
        return dict(cp_w_in=_undo_column_split(got[0]), cp_w_out=got[1].reshape(d, d), meta=meta,
                    conv_w=jnp.pad(conv_w[0], ((0, 1), (0, 0))))

    def emit(self, group, g):
        d = self.d
        if group in ("ffn0", "ffn1"):
            arrs = [g["w1"], g["w2"]]
        elif group == "gla":
            w_in = jnp.concatenate([g["gla_w_qkvg"], g["gla_w_r"][:, :GATE_RANK]], axis=1)
            arrs = [_column_split(w_in), g["gla_w_out"].reshape(N_DEV, d // N_DEV, d)]
        else:
            s = dict(g["small"])
            s.update(pool_w=s["pool_w"][None], conv_w=s["conv_w"][None, :CONV_WIDTH], gate_w=s["gate_w"][None, :GATE_RANK])
            self.small_grads = [s[n] for n in _SMALL_GRADS]
            self.small_own = _pack(self.small_grads)
            self.small_sent, self.token = _exchange_start(
                _plan_to_all, [self.small_own], [lax.empty((N_DEV,) + self.small_own.shape, F32)], after=self.token,
                name="start_g_small")
            arrs = [_column_split(g["cp_w_in"]), g["cp_w_out"].reshape(N_DEV, d // N_DEV, d)]
        state, self.token = _exchange_start(_plan_to_sibling, arrs, None, after=self.token, name=f"start_g1_{group}")
        self.to_sibling.append((group, state))
        return self.token

    def poll(self, after):
        for group, state in self.to_sibling:
            arrs, landed = _exchange_wait(state, after, name=f"wait_g1_{group}")
            sums = [_pair_add(a, l, name=f"chip_sum_{group}_{k}") for k, (a, l) in enumerate(zip(arrs, landed))]
            self.to_chips[group], self.token = _exchange_start(_plan_to_chips, sums, None, after=self.token,
                                                               name=f"start_g2_{group}")
            after = self.token
        self.to_sibling = []
        return self.token

    def finish(self, w, mom, var):
        out = {}
        self.poll(self.token)
        after = self.token

        def landed(group):
            sums, got = _exchange_wait(self.to_chips[group], after, name=f"wait_g2_{group}")
            return list(zip(sums, got))

        def adam(n, parts, behind=None):
            out[n] = _reduce_adam(parts, w[n], mom[n], var[n], after=behind, name=f"adam_{n}")
            return out[n][0]

        ffn1 = landed("ffn1")
        after = ffn1[0][1]
        gla = landed("gla")
        after = gla[0][1]
        ffn0 = landed("ffn0")
        after = adam("ffn_w1", [ffn0[0], ffn1[0]])
        after = adam("ffn_w2", [ffn0[1], ffn1[1]], after)
        after = adam("gla_w_in", [gla[0]], after)
        after = adam("gla_w_out", [gla[1]], after)
        cp = landed("cp")
        adam("cp_w_in", [cp[0]])
        after = adam("cp_w_out", [cp[1]])
        (self.small_own,), (small_all,) = _exchange_wait(self.small_sent, after, name="wait_g_small")

        names = _REPLICATED + _SMALL_SHARDED
        small_sum = _unpack(_reduce8(self.small_own, small_all, name="sum_small_grads"), [a.shape for a in self.small_grads])
        me = _my_slot()
        grad = {}
        for n, full in zip(names, small_sum):
            if n in _SMALL_SHARDED:
                width = w[n].shape[-1]
                full = lax.dynamic_slice_in_dim(full, me * width, width, axis=full.ndim - 1)
            grad[n] = full
        packed = [_pack([t[n] for n in names]) for t in (w, grad, mom, var)]
        small_out = [_unpack(p, [w[n].shape for n in names]) for p in _adamw_small(*packed, name="adam_small")]
        for k, n in enumerate(names):
            out[n] = (grad[n], small_out[0][k], small_out[1][k], small_out[2][k])
        return out


def kernel(x, meta_tokens, mix_norm_g, ffn_norm_g, ffn_w1, ffn_w2, cp_w_in, cp_conv_w, cp_conv_b, cp_ln_g, cp_ln_b, cp_pool_w, cp_pool_scale, cp_w_out, gla_w_in, gla_gate_w2, gla_gate_b, gla_head_g, gla_w_out, final_norm_g, loss_target, m_meta_tokens, m_mix_norm_g, m_ffn_norm_g, m_ffn_w1, m_ffn_w2, m_cp_w_in, m_cp_conv_w, m_cp_conv_b, m_cp_ln_g, m_cp_ln_b, m_cp_pool_w, m_cp_pool_scale, m_cp_w_out, m_gla_w_in, m_gla_gate_w2, m_gla_gate_b, m_gla_head_g, m_gla_w_out, m_final_norm_g, v_meta_tokens, v_mix_norm_g, v_ffn_norm_g, v_ffn_w1, v_ffn_w2, v_cp_w_in, v_cp_conv_w, v_cp_conv_b, v_cp_ln_g, v_cp_ln_b, v_cp_pool_w, v_cp_pool_scale, v_cp_w_out, v_gla_w_in, v_gla_gate_w2, v_gla_gate_b, v_gla_head_g, v_gla_w_out, v_final_norm_g):
    w = dict(meta_tokens=meta_tokens, mix_norm_g=mix_norm_g, ffn_norm_g=ffn_norm_g, ffn_w1=ffn_w1, ffn_w2=ffn_w2,
             cp_w_in=cp_w_in, cp_conv_w=cp_conv_w, cp_conv_b=cp_conv_b, cp_ln_g=cp_ln_g, cp_ln_b=cp_ln_b,
             cp_pool_w=cp_pool_w, cp_pool_scale=cp_pool_scale, cp_w_out=cp_w_out, gla_w_in=gla_w_in,
             gla_gate_w2=gla_gate_w2, gla_gate_b=gla_gate_b, gla_head_g=gla_head_g, gla_w_out=gla_w_out,
             final_norm_g=final_norm_g.reshape(1, -1))
    mom = dict(meta_tokens=m_meta_tokens, mix_norm_g=m_mix_norm_g, ffn_norm_g=m_ffn_norm_g, ffn_w1=m_ffn_w1, ffn_w2=m_ffn_w2,
               cp_w_in=m_cp_w_in, cp_conv_w=m_cp_conv_w, cp_conv_b=m_cp_conv_b, cp_ln_g=m_cp_ln_g, cp_ln_b=m_cp_ln_b,
               cp_pool_w=m_cp_pool_w, cp_pool_scale=m_cp_pool_scale, cp_w_out=m_cp_w_out, gla_w_in=m_gla_w_in,
               gla_gate_w2=m_gla_gate_w2, gla_gate_b=m_gla_gate_b, gla_head_g=m_gla_head_g, gla_w_out=m_gla_w_out,
               final_norm_g=m_final_norm_g.reshape(1, -1))
    var = dict(meta_tokens=v_meta_tokens, mix_norm_g=v_mix_norm_g, ffn_norm_g=v_ffn_norm_g, ffn_w1=v_ffn_w1, ffn_w2=v_ffn_w2,
               cp_w_in=v_cp_w_in, cp_conv_w=v_cp_conv_w, cp_conv_b=v_cp_conv_b, cp_ln_g=v_cp_ln_g, cp_ln_b=v_cp_ln_b,
               cp_pool_w=v_cp_pool_w, cp_pool_scale=v_cp_pool_scale, cp_w_out=v_cp_w_out, gla_w_in=v_gla_w_in,
               gla_gate_w2=v_gla_gate_w2, gla_gate_b=v_gla_gate_b, gla_head_g=v_gla_head_g, gla_w_out=v_gla_w_out,
               final_norm_g=v_final_norm_g.reshape(1, -1))
    d = x.shape[-1]
    replicated = dict(mix_g=w["mix_norm_g"], ffn_g=w["ffn_norm_g"], conv_b=w["cp_conv_b"], ln_g=w["cp_ln_g"],
                      ln_b=w["cp_ln_b"], pool_w=w["cp_pool_w"][0].astype(BF16), pool_scale=w["cp_pool_scale"],
                      final_g=w["final_norm_g"])
    exchanges = _Exchanges(w, d)
    loss_blk, grad_x, _ = _local_step(x[0], loss_target[0], replicated, exchanges)
    loss = lax.psum(loss_blk[0, 0], ("x", "y", "c"))
    out = exchanges.finish(w, mom, var)

    def leaf(n, k):
        a = out[n][k]
        return a.reshape(-1) if n == "final_norm_g" else a

    return (loss, grad_x[None], *[leaf(n, 0) for n in _NAMES], *[leaf(n, 1) for n in _NAMES],
            *[leaf(n, 2) for n in _NAMES], *[leaf(n, 3) for n in _NAMES])
```

```python
import functools

import jax
import jax.numpy as jnp
from jax import lax
from jax.experimental import pallas as pl
from jax.experimental.pallas import tpu as pltpu

F32, BF16 = jnp.float32, jnp.bfloat16
N_DEV = 8
CHUNK = 64
N_META = 16
PAD_ROWS = CHUNK - N_META
HALO = 32
EPS = 1e-5
CONV_WIDTH = 31
POOL_WINDOWS = (2, 4, 8, 16)
HEADS = 4
GATE_RANK = 16
GATE_NORM = 16.0
GATE_PAD = 128
ADAM_LR, ADAM_B1, ADAM_B2, ADAM_EPS, ADAM_WD, ADAM_STEP = 0.001, 0.9, 0.999, 1e-08, 0.01, 10
V7X_VMEM_LIMIT = 56 * 2 ** 20
LANE = 128
HI = lax.Precision.HIGHEST


def _cparams(*sem):
    return pltpu.CompilerParams(dimension_semantics=sem, vmem_limit_bytes=V7X_VMEM_LIMIT)


def _row_tile(t, cap):
    best = CHUNK
    for r in range(CHUNK, min(t, cap) + 1, CHUNK):
        if t % r == 0:
            best = r
    return best


def _resident(shape):
    return pl.BlockSpec(shape, lambda *_: (0,) * len(shape), pipeline_mode=pl.Buffered(1))


def _dot(a, b):
    return jnp.dot(a, b, preferred_element_type=F32)


def _dot_nt(a, b):
    return lax.dot_general(a, b, (((1,), (1,)), ((), ())), preferred_element_type=F32)


def _dot_tn(a, b):
    return lax.dot_general(a, b, (((0,), (0,)), ((), ())), preferred_element_type=F32)


def _rowsum(a):
    return jnp.sum(a, axis=0, keepdims=True)


def _sigmoid(a):
    return 1.0 / (1.0 + jnp.exp(-a))


def _row_ids(tile, rt):
    return tile * rt + lax.broadcasted_iota(jnp.int32, (rt, 1), 0)


def _sds(shape, dtype):
    return jax.ShapeDtypeStruct(shape, dtype)


def _linear_fwd(x, w, *, gain=None, res=None, out_dtype=F32, name):
    t, k = x.shape
    n = w.shape[1]
    rt = _row_tile(t, 320)

    def body(*refs):
        refs = list(refs)
        x_ref, w_ref = refs[:2]
        pos = 2
        g_ref = r_ref = u_ref = None
        if gain is not None:
            g_ref = refs[pos]
            pos += 1
        if res is not None:
            r_ref = refs[pos]
            pos += 1
        y_ref = refs[pos]
        if gain is not None:
            u_ref = refs[pos + 1]
            xv = x_ref[...]
            u = (xv * lax.rsqrt(jnp.mean(xv * xv, axis=-1, keepdims=True) + EPS) * g_ref[...]).astype(BF16)
            u_ref[...] = u
        else:
            u = x_ref[...]
        y = _dot(u, w_ref[...])
        if res is not None:
            y = y + r_ref[...]
        y_ref[...] = y.astype(y_ref.dtype)

    rows = lambda i: (i, 0)
    in_specs = [pl.BlockSpec((rt, k), rows), _resident((k, n))]
    args = [x, w]
    if gain is not None:
        in_specs.append(_resident((1, k)))
        args.append(gain)
    if res is not None:
        in_specs.append(pl.BlockSpec((rt, n), rows))
        args.append(res)
    out_shape = [_sds((t, n), out_dtype)]
    out_specs = [pl.BlockSpec((rt, n), rows)]
    if gain is not None:
        out_shape.append(_sds((t, k), BF16))
        out_specs.append(pl.BlockSpec((rt, k), rows))
    out = pl.pallas_call(body, grid=(t // rt,), in_specs=in_specs, out_specs=out_specs, out_shape=out_shape,
                         compiler_params=_cparams("parallel"), name=name)(*args)
    return out if gain is not None else out[0]


def _linear_bwd_x(dys, ws, *, norm=None, after=None, name):
    t = dys[0].shape[0]
    k = ws[0].shape[0]
    rt = _row_tile(t, 320)
    nd = len(dys)
    dep_specs, deps = _dep_specs(after)

    def body(*refs):
        refs = list(refs)[len(deps):]
        dy_refs, w_refs = refs[:nd], refs[nd:2 * nd]
        dx = None
        for dy_ref, w_ref in zip(dy_refs, w_refs):
            part = _dot_nt(dy_ref[...].astype(BF16), w_ref[...])
            dx = part if dx is None else dx + part
        if norm is None:
            refs[2 * nd][...] = dx
            return
        h_ref, g_ref, dres_ref, dh_ref, dg_ref = refs[2 * nd:]
        hv = h_ref[...]
        rstd = lax.rsqrt(jnp.mean(hv * hv, axis=-1, keepdims=True) + EPS)
        xh = hv * rstd

        @pl.when(pl.program_id(0) == 0)
        def _():
            dg_ref[...] = jnp.zeros_like(dg_ref)

        dg_ref[...] += _rowsum(dx * xh)
        dxh = dx * g_ref[...]
        dh_ref[...] = dres_ref[...] + rstd * (dxh - xh * jnp.mean(dxh * xh, axis=-1, keepdims=True))

    rows = lambda i: (i, 0)
    in_specs = [pl.BlockSpec((rt, dy.shape[1]), rows) for dy in dys] + [_resident(w.shape) for w in ws]
    args = list(dys) + list(ws)
    out_shape = [_sds((t, k), F32)]
    out_specs = [pl.BlockSpec((rt, k), rows)]
    if norm is not None:
        h, gain, dres = norm
        in_specs += [pl.BlockSpec((rt, k), rows), _resident((1, k)), pl.BlockSpec((rt, k), rows)]
        args += [h, gain, dres]
        out_shape.append(_sds((1, k), F32))
        out_specs.append(pl.BlockSpec((1, k), lambda i: (0, 0)))
    out = pl.pallas_call(body, grid=(t // rt,), in_specs=dep_specs + in_specs, out_specs=out_specs, out_shape=out_shape,
                         compiler_params=_cparams("arbitrary"), name=name)(*deps, *args)
    return out if norm is not None else out[0]


def _linear_bwd_w(x, dy, *, name):
    t, k = x.shape
    n = dy.shape[1]
    rt = _row_tile(t, 832)
    nt = max(c for c in (512, 384, 256, LANE) if n % c == 0)

    def body(x_ref, dy_ref, o_ref):
        @pl.when(pl.program_id(1) == 0)
        def _():
            o_ref[...] = jnp.zeros_like(o_ref)

        o_ref[...] += _dot_tn(x_ref[...], dy_ref[...].astype(BF16))

    return pl.pallas_call(
        body, grid=(n // nt, t // rt),
        in_specs=[pl.BlockSpec((rt, k), lambda j, i: (i, 0)), pl.BlockSpec((rt, nt), lambda j, i: (i, j))],
        out_specs=pl.BlockSpec((k, nt), lambda j, i: (0, j)), out_shape=_sds((k, n), F32),
        compiler_params=_cparams("parallel", "arbitrary"), name=name)(x, dy)


def _ffn_fwd(h, gain, w1g, w2g, *, name):
    t, d = h.shape
    f8 = w1g.shape[-1]
    rt = _row_tile(t, 832)

    def body(h_ref, g_ref, w1_ref, w2_ref, o_ref, u_ref, r_ref, acc_ref):
        j = pl.program_id(1)

        @pl.when(j == 0)
        def _():
            hv = h_ref[...]
            u_ref[...] = (hv * lax.rsqrt(jnp.mean(hv * hv, axis=-1, keepdims=True) + EPS) * g_ref[...]).astype(BF16)
            acc_ref[...] = jnp.zeros_like(acc_ref)

        a = jnp.maximum(_dot(u_ref[...], w1_ref[...]), 0.0)
        r_ref[...] = a.astype(BF16)
        acc_ref[...] += _dot((a * a).astype(BF16), w2_ref[...])

        @pl.when(j == N_DEV - 1)
        def _():
            o_ref[...] = h_ref[...] + acc_ref[...]

    return pl.pallas_call(
        body, grid=(t // rt, N_DEV),
        in_specs=[pl.BlockSpec((rt, d), lambda i, j: (i, 0)), _resident((1, d)),
                  pl.BlockSpec((None, d, f8), lambda i, j: (j, 0, 0)),
                  pl.BlockSpec((None, f8, d), lambda i, j: (j, 0, 0))],
        out_specs=[pl.BlockSpec((rt, d), lambda i, j: (i, 0)), pl.BlockSpec((rt, d), lambda i, j: (i, 0)),
                   pl.BlockSpec((rt, f8), lambda i, j: (i, j))],
        out_shape=[_sds((t, d), F32), _sds((t, d), BF16), _sds((t, N_DEV * f8), BF16)],
        scratch_shapes=[pltpu.VMEM((rt, d), F32)],
        compiler_params=_cparams("parallel", "arbitrary"), name=name)(h, gain, w1g, w2g)


def _ffn_bwd_x(h, dout, gain, r, w1g, w2g, *, after=None, name):
    t, d = h.shape
    f8 = w1g.shape[-1]
    rt = _row_tile(t, 832)
    last = N_DEV - 1
    dep_specs, deps = _dep_specs(after)

    def body(*refs):
        h_ref, do_ref, g_ref, r_ref, w1_ref, w2_ref, dh_ref, dhh_ref, dob_ref, dg_ref, du_ref = refs[len(deps):]
        i, j = pl.program_id(0), pl.program_id(1)

        @pl.when(j == 0)
        def _():
            dob_ref[...] = do_ref[...].astype(BF16)
            du_ref[...] = jnp.zeros_like(du_ref)

        dhh = (_dot_nt(dob_ref[...], w2_ref[...]) * (2.0 * r_ref[...].astype(F32))).astype(BF16)
        dhh_ref[...] = dhh
        du_ref[...] += _dot_nt(dhh, w1_ref[...])

        @pl.when(j == last)
        def _():
            @pl.when(i == 0)
            def _():
                dg_ref[...] = jnp.zeros_like(dg_ref)

            hv = h_ref[...]
            rstd = lax.rsqrt(jnp.mean(hv * hv, axis=-1, keepdims=True) + EPS)
            xh = hv * rstd
            du = du_ref[...]
            dg_ref[...] += _rowsum(du * xh)
            dxh = du * g_ref[...]
            dh_ref[...] = do_ref[...] + rstd * (dxh - xh * jnp.mean(dxh * xh, axis=-1, keepdims=True))

    rows = lambda i, j: (i, 0)
    return pl.pallas_call(
        body, grid=(t // rt, N_DEV),
        in_specs=dep_specs + [
                  pl.BlockSpec((rt, d), rows), pl.BlockSpec((rt, d), rows), _resident((1, d)),
                  pl.BlockSpec((rt, f8), lambda i, j: (i, j)),
                  pl.BlockSpec((None, d, f8), lambda i, j: (j, 0, 0)),
                  pl.BlockSpec((None, f8, d), lambda i, j: (j, 0, 0))],
        out_specs=[pl.BlockSpec((rt, d), rows), pl.BlockSpec((rt, f8), lambda i, j: (i, j)), pl.BlockSpec((rt, d), rows),
                   pl.BlockSpec((1, d), lambda i, j: (0, 0))],
        out_shape=[_sds((t, d), F32), _sds((t, N_DEV * f8), BF16), _sds((t, d), BF16), _sds((1, d), F32)],
        scratch_shapes=[pltpu.VMEM((rt, d), F32)],
        compiler_params=_cparams("arbitrary", "arbitrary"), name=name)(*deps, h, dout, gain, r, w1g, w2g)


FFN_DW_ROWS = 1024


def _ffn_bwd_w(u, dhh, r, dout_b, *, name):
    t, d = u.shape
    f8 = dhh.shape[1] // N_DEV

    def body(u_ref, dhh_ref, r_ref, dob_ref, dw1_ref, dw2_ref):
        for c0 in range(0, t, FFN_DW_ROWS):
            rows = slice(c0, min(c0 + FFN_DW_ROWS, t))
            rr = r_ref[rows, :].astype(F32)
            part1 = _dot_tn(u_ref[rows, :], dhh_ref[rows, :])
            part2 = _dot_tn((rr * rr).astype(BF16), dob_ref[rows, :])
            if c0 == 0:
                dw1_ref[...] = part1
                dw2_ref[...] = part2
            else:
                dw1_ref[...] += part1
                dw2_ref[...] += part2

    return pl.pallas_call(
        body, grid=(N_DEV,),
        in_specs=[_resident((t, d)), pl.BlockSpec((t, f8), lambda j: (0, j)), pl.BlockSpec((t, f8), lambda j: (0, j)),
                  _resident((t, d))],
        out_specs=[pl.BlockSpec((None, d, f8), lambda j: (j, 0, 0)), pl.BlockSpec((None, f8, d), lambda j: (j, 0, 0))],
        out_shape=[_sds((N_DEV, d, f8), F32), _sds((N_DEV, f8, d), F32)],
        compiler_params=_cparams("parallel"), name=name)(u, dhh, r, dout_b)


def _lane_blocks(width):
    lb = min(LANE, width)
    return [slice(s, s + lb) for s in range(0, width, lb)]


def _conv_rows(src_ref, w_ref, offset, dst_ref, nblk, width, bias_ref=None):
    def blk(rb, carry):
        base = pl.multiple_of(rb * CHUNK, CHUNK)
        for l, ls in enumerate(_lane_blocks(width)):
            acc = jnp.zeros((CHUNK, ls.stop - ls.start), F32)
            if bias_ref is not None:
                acc = acc + bias_ref[:, ls]
            for k in range(CONV_WIDTH):
                acc = acc + w_ref[k:k + 1, ls] * src_ref[l, pl.ds(base + offset(k), CHUNK), :]
            dst_ref[l, pl.ds(base, CHUNK), :] = acc
        return carry

    lax.fori_loop(0, nblk, blk, 0)


def _to_lane_blocks(ref, row0, value):
    for l, ls in enumerate(_lane_blocks(value.shape[1])):
        ref[l, row0:row0 + value.shape[0], :] = value[:, ls]


def _from_lane_blocks(ref):
    return jnp.concatenate([ref[l] for l in range(ref.shape[0])], axis=1)


def _pool_counts(rows, window):
    return jnp.clip(rows - PAD_ROWS + 1, 1, window).astype(F32)


def _trailing_sum(v, window):
    s, sh = v, 1
    while sh < window:
        s = s + pltpu.roll(s, sh, 0)
        sh *= 2
    return s


def _leading_sum(v, window):
    s, sh, n = v, 1, v.shape[0]
    while sh < window:
        s = s + pltpu.roll(s, n - sh, 0)
        sh *= 2
    return s


def _cp_mid_fwd(z, conv_w, conv_b, ln_g, ln_b, pool_w, pool_scale, *, name):
    t, ein = z.shape
    cd = conv_b.shape[1]
    pd = pool_scale.shape[1]
    pg = pd // len(POOL_WINDOWS)
    rt = _row_tile(t, 320)

    def body(z_ref, cw_ref, cb_ref, lg_ref, lb_ref, pw_ref, ps_ref, o_ref, gext, pext, conv_s):
        i = pl.program_id(0)

        @pl.when(i == 0)
        def _():
            _to_lane_blocks(gext, 0, jnp.zeros((HALO, cd), F32))
            pext[0:HALO, :] = jnp.zeros((HALO, pd), F32)

        _to_lane_blocks(gext, HALO, z_ref[:, 0:cd] * _sigmoid(z_ref[:, cd:2 * cd]))
        pext[HALO:HALO + rt, :] = z_ref[:, 2 * cd:]
        _conv_rows(gext, cw_ref, lambda k: k + HALO - (CONV_WIDTH - 1), conv_s, rt // CHUNK, cd, cb_ref)
        cv = _from_lane_blocks(conv_s)
        xc = cv - jnp.mean(cv, axis=-1, keepdims=True)
        y = xc * lax.rsqrt(jnp.mean(xc * xc, axis=-1, keepdims=True) + EPS) * lg_ref[...] + lb_ref[...]
        rows = _row_ids(i, rt)
        a = jnp.where(rows >= PAD_ROWS, y * _sigmoid(y), 0.0)
        o_ref[:, 0:cd] = a.astype(BF16)
        for gi, window in enumerate(POOL_WINDOWS):
            ls = slice(gi * pg, (gi + 1) * pg)
            v = pext[:, ls]
            tm = _trailing_sum(v, window)[HALO:] / _pool_counts(rows, window) - v[HALO:]
            p = _dot(tm.astype(BF16), pw_ref[gi]) * ps_ref[:, ls]
            o_ref[:, cd + gi * pg:cd + (gi + 1) * pg] = p.astype(BF16)
        gext[:, 0:HALO, :] = gext[:, rt:rt + HALO, :]
        pext[0:HALO, :] = pext[rt:rt + HALO, :]

    nl, lb = len(_lane_blocks(cd)), min(LANE, cd)
    return pl.pallas_call(
        body, grid=(t // rt,),
        in_specs=[pl.BlockSpec((rt, ein), lambda i: (i, 0)), _resident(conv_w.shape), _resident((1, cd)),
                  _resident((1, cd)), _resident((1, cd)), _resident(pool_w.shape), _resident((1, pd))],
        out_specs=pl.BlockSpec((rt, cd + pd), lambda i: (i, 0)), out_shape=_sds((t, cd + pd), BF16),
        scratch_shapes=[pltpu.VMEM((nl, rt + HALO, lb), F32), pltpu.VMEM((rt + HALO, pd), F32),
                        pltpu.VMEM((nl, rt, lb), F32)],
        compiler_params=_cparams("arbitrary"), name=name)(z, conv_w, conv_b, ln_g, ln_b, pool_w, pool_scale)


def _cp_mid_bwd(z, dcat, conv_w, conv_b, ln_g, ln_b, pool_w, pool_scale, *, after=None, name):
    t, ein = z.shape
    cd = conv_b.shape[1]
    pd = pool_scale.shape[1]
    pg = pd // len(POOL_WINDOWS)
    rt = _row_tile(t, 320)
    ntile = t // rt
    per = rt // CHUNK
    dep_specs, deps = _dep_specs(after)

    def body(*refs):
        (z_ref, zh_ref, dc_ref, cw_ref, cb_ref, lg_ref, lb_ref, pw_ref, ps_ref,
         dz_ref, dcw_ref, dcb_ref, dlg_ref, dlb_ref, dpw_ref, dps_ref, gext, pext, conv_s, dcv, dsp) = refs[len(deps):]
        step = pl.program_id(0)
        tile = ntile - 1 - step

        @pl.when(step == 0)
        def _():
            for ref in (dcw_ref, dcb_ref, dlg_ref, dlb_ref, dpw_ref, dps_ref):
                ref[...] = jnp.zeros_like(ref)
            _to_lane_blocks(dcv, rt, jnp.zeros((HALO, cd), F32))
            dsp[rt:rt + HALO, :] = jnp.zeros((HALO, pd), F32)

        keep = jnp.where(tile > 0, 1.0, 0.0)
        zh = zh_ref[CHUNK - HALO:CHUNK, :]
        _to_lane_blocks(gext, 0, keep * zh[:, 0:cd] * _sigmoid(zh[:, cd:2 * cd]))
        pext[0:HALO, :] = keep * zh[:, 2 * cd:]
        za = z_ref[:, 0:cd]
        sg = _sigmoid(z_ref[:, cd:2 * cd])
        _to_lane_blocks(gext, HALO, za * sg)
        pext[HALO:HALO + rt, :] = z_ref[:, 2 * cd:]
        _conv_rows(gext, cw_ref, lambda k: k + HALO - (CONV_WIDTH - 1), conv_s, per, cd, cb_ref)
        cv = _from_lane_blocks(conv_s)
        xc = cv - jnp.mean(cv, axis=-1, keepdims=True)
        rstd = lax.rsqrt(jnp.mean(xc * xc, axis=-1, keepdims=True) + EPS)
        xh = xc * rstd
        y = xh * lg_ref[...] + lb_ref[...]
        sy = _sigmoid(y)
        rows = _row_ids(tile, rt)
        da = jnp.where(rows >= PAD_ROWS, dc_ref[:, 0:cd], 0.0)
        dy = da * (sy * (1.0 + y * (1.0 - sy)))
        dlg_ref[...] += _rowsum(dy * xh)
        dlb_ref[...] += _rowsum(dy)
        dxh = dy * lg_ref[...]
        dconv = rstd * (dxh - jnp.mean(dxh, axis=-1, keepdims=True) - xh * jnp.mean(dxh * xh, axis=-1, keepdims=True))
        dcb_ref[...] += _rowsum(dconv)
        _to_lane_blocks(dcv, 0, dconv)
        for l, ls in enumerate(_lane_blocks(cd)):
            def acc_rows(rb, accs, l=l):
                base = pl.multiple_of(rb * CHUNK, CHUNK)
                d_blk = dcv[l, pl.ds(base, CHUNK), :]
                out = []
                for k in range(CONV_WIDTH):
                    prod = d_blk * gext[l, pl.ds(base + k + HALO - (CONV_WIDTH - 1), CHUNK), :]
                    part = prod[0:8]
                    for s in range(8, CHUNK, 8):
                        part = part + prod[s:s + 8]
                    out.append(accs[k] + part)
                return tuple(out)

            zero = jnp.zeros((8, ls.stop - ls.start), F32)
            accs = lax.fori_loop(0, per, acc_rows, (zero,) * CONV_WIDTH)
            for k in range(CONV_WIDTH):
                dcw_ref[k:k + 1, ls] += _rowsum(accs[k])
        _conv_rows(dcv, cw_ref, lambda k: CONV_WIDTH - 1 - k, conv_s, per, cd)
        dglu = _from_lane_blocks(conv_s)
        dz_ref[:, 0:cd] = (dglu * sg).astype(BF16)
        dz_ref[:, cd:2 * cd] = (dglu * za * sg * (1.0 - sg)).astype(BF16)
        dcv[:, rt:rt + HALO, :] = dcv[:, 0:HALO, :]
        for gi, window in enumerate(POOL_WINDOWS):
            ls = slice(gi * pg, (gi + 1) * pg)
            v = pext[:, ls]
            cnt = _pool_counts(rows, window)
            tm = (_trailing_sum(v, window)[HALO:] / cnt - v[HALO:]).astype(BF16)
            dp = dc_ref[:, cd + gi * pg:cd + (gi + 1) * pg]
            dps_ref[:, ls] += _rowsum(dp * _dot(tm, pw_ref[gi]))
            dpl = (dp * ps_ref[:, ls]).astype(BF16)
            dpw_ref[gi] += _dot_tn(tm, dpl)
            dtm = _dot_nt(dpl, pw_ref[gi])
            dsp[0:rt, ls] = dtm / cnt
            dpin = _leading_sum(dsp[:, ls], window)[0:rt] - dtm
            dz_ref[:, 2 * cd + gi * pg:2 * cd + (gi + 1) * pg] = dpin.astype(BF16)
        dsp[rt:rt + HALO, :] = dsp[0:HALO, :]

    back = lambda i: (ntile - 1 - i, 0)
    halo_idx = lambda i: (jnp.maximum((ntile - 1 - i) * per - 1, 0), 0)
    const2 = lambda i: (0, 0)
    nl, lb = len(_lane_blocks(cd)), min(LANE, cd)
    return pl.pallas_call(
        body, grid=(ntile,),
        in_specs=dep_specs + [
                  pl.BlockSpec((rt, ein), back), pl.BlockSpec((CHUNK, ein), halo_idx), pl.BlockSpec((rt, cd + pd), back),
                  _resident(conv_w.shape), _resident((1, cd)), _resident((1, cd)), _resident((1, cd)),
                  _resident(pool_w.shape), _resident((1, pd))],
        out_specs=[pl.BlockSpec((rt, ein), back), pl.BlockSpec(conv_w.shape, const2), pl.BlockSpec((1, cd), const2),
                   pl.BlockSpec((1, cd), const2), pl.BlockSpec((1, cd), const2),
                   pl.BlockSpec(pool_w.shape, lambda i: (0, 0, 0)), pl.BlockSpec((1, pd), const2)],
        out_shape=[_sds((t, ein), BF16), _sds(conv_w.shape, F32), _sds((1, cd), F32), _sds((1, cd), F32),
                   _sds((1, cd), F32), _sds(pool_w.shape, F32), _sds((1, pd), F32)],
        scratch_shapes=[pltpu.VMEM((nl, rt + HALO, lb), F32), pltpu.VMEM((rt + HALO, pd), F32), pltpu.VMEM((nl, rt, lb), F32),
                        pltpu.VMEM((nl, rt + HALO, lb), F32), pltpu.VMEM((rt + HALO, pd), F32)],
        compiler_params=_cparams("arbitrary"), name=name)(*deps, z, z, dcat, conv_w, conv_b, ln_g, ln_b, pool_w, pool_scale)


def _log_decay(r_ref, gw_ref, gb_ref, rows):
    gp = _dot(r_ref[...].astype(BF16), gw_ref[...]) + gb_ref[...]
    log_sig = jnp.minimum(gp, 0.0) - jnp.log(1.0 + jnp.exp(-jnp.abs(gp)))
    return gp, jnp.where(rows >= PAD_ROWS, log_sig / GATE_NORM, 0.0)


def _tri(strict):
    r = lax.broadcasted_iota(jnp.int32, (CHUNK, CHUNK), 0)
    c = lax.broadcasted_iota(jnp.int32, (CHUNK, CHUNK), 1)
    return jnp.where(c < r if strict else c <= r, 1.0, 0.0).astype(F32)


def _gla_mid_fwd(z, r, gate_w, gate_b, head_g, *, name):
    t = z.shape[0]
    dk = gate_b.shape[1]
    hv = head_g.shape[1]
    hk = dk // HEADS
    dv = hv * HEADS
    rt = _row_tile(t, 320)
    per = rt // CHUNK
    scale = hk ** -0.5

    def body(z_ref, r_ref, gw_ref, gb_ref, hg_ref, o_ref, st_ref, s_ref, la_ref):
        i = pl.program_id(0)

        @pl.when(i == 0)
        def _():
            s_ref[...] = jnp.zeros_like(s_ref)

        _, la = _log_decay(r_ref, gw_ref, gb_ref, _row_ids(i, rt))
        la_ref[...] = la
        tri = _tri(False)

        def chunk(c, carry):
            rows = pl.ds(pl.multiple_of(c * CHUNK, CHUNK), CHUNK)
            la_c = la_ref[rows, :]
            cum = jnp.dot(tri, la_c, precision=HI, preferred_element_type=F32)
            tot = _rowsum(la_c)
            dec = jnp.exp(tot - cum)
            etot = jnp.exp(tot)
            for hd in range(HEADS):
                ks = slice(hd * hk, (hd + 1) * hk)
                q = z_ref[rows, hd * hk:(hd + 1) * hk] * scale
                kd = z_ref[rows, dk + hd * hk:dk + (hd + 1) * hk] * dec[:, ks]
                v = z_ref[rows, 2 * dk + hd * hv:2 * dk + (hd + 1) * hv]
                g = z_ref[rows, 2 * dk + dv + hd * hv:2 * dk + dv + (hd + 1) * hv]
                s_new = s_ref[hd] * etot[:, ks] + _dot_tn(v.astype(BF16), kd.astype(BF16))
                s_ref[hd] = s_new
                st_ref[c, hd] = s_new
                o = _dot_nt(q.astype(BF16), s_new.astype(BF16))
                on = o * lax.rsqrt(jnp.mean(o * o, axis=-1, keepdims=True) + EPS) * hg_ref[...]
                o_ref[rows, hd * hv:(hd + 1) * hv] = (on * (g * _sigmoid(g))).astype(BF16)
            return carry

        lax.fori_loop(0, per, chunk, 0)

    return pl.pallas_call(
        body, grid=(t // rt,),
        in_specs=[pl.BlockSpec((rt, z.shape[1]), lambda i: (i, 0)), pl.BlockSpec((rt, GATE_PAD), lambda i: (i, 0)),
                  _resident(gate_w.shape), _resident((1, dk)), _resident((1, hv))],
        out_specs=[pl.BlockSpec((rt, dv), lambda i: (i, 0)), pl.BlockSpec((per, HEADS, hv, hk), lambda i: (i, 0, 0, 0))],
        out_shape=[_sds((t, dv), BF16), _sds((t // CHUNK, HEADS, hv, hk), F32)],
        scratch_shapes=[pltpu.VMEM((HEADS, hv, hk), F32), pltpu.VMEM((rt, dk), F32)],
        compiler_params=_cparams("arbitrary"), name=name)(z, r, gate_w, gate_b, head_g)


def _gla_mid_bwd(z, r, dog, states, gate_w, gate_b, head_g, *, after=None, name):
    t = z.shape[0]
    dk = gate_b.shape[1]
    hv = head_g.shape[1]
    hk = dk // HEADS
    dv = hv * HEADS
    rt = _row_tile(t, 320)
    ntile = t // rt
    per = rt // CHUNK
    scale = hk ** -0.5
    dep_specs, deps = _dep_specs(after)

    def body(*refs):
        (z_ref, r_ref, do_ref, st_ref, stp_ref, gw_ref, gb_ref, hg_ref,
         dz_ref, dr_ref, dgw_ref, dgb_ref, dhg_ref, ds_ref, la_ref, dla_ref) = refs[len(deps):]
        step = pl.program_id(0)
        tile = ntile - 1 - step

        @pl.when(step == 0)
        def _():
            ds_ref[...] = jnp.zeros_like(ds_ref)
            dgw_ref[...] = jnp.zeros_like(dgw_ref)
            dgb_ref[...] = jnp.zeros_like(dgb_ref)
            dhg_ref[...] = jnp.zeros_like(dhg_ref)

        rows_id = _row_ids(tile, rt)
        gp, la = _log_decay(r_ref, gw_ref, gb_ref, rows_id)
        la_ref[...] = la
        tri, tri_strict = _tri(False), _tri(True)
        keep = jnp.where(tile > 0, 1.0, 0.0)

        def chunk(cc, carry):
            c = per - 1 - cc
            rows = pl.ds(pl.multiple_of(c * CHUNK, CHUNK), CHUNK)
            la_c = la_ref[rows, :]
            cum = jnp.dot(tri, la_c, precision=HI, preferred_element_type=F32)
            tot = _rowsum(la_c)
            dec = jnp.exp(tot - cum)
            etot = jnp.exp(tot)
            inside = jnp.where(c > 0, 1.0, 0.0)
            for hd in range(HEADS):
                ks = slice(hd * hk, (hd + 1) * hk)
                q = (z_ref[rows, hd * hk:(hd + 1) * hk] * scale).astype(BF16)
                k = z_ref[rows, dk + hd * hk:dk + (hd + 1) * hk]
                kd = k * dec[:, ks]
                v = z_ref[rows, 2 * dk + hd * hv:2 * dk + (hd + 1) * hv].astype(BF16)
                g = z_ref[rows, 2 * dk + dv + hd * hv:2 * dk + dv + (hd + 1) * hv]
                s_now = st_ref[c, hd]
                s_prev = inside * st_ref[jnp.maximum(c - 1, 0), hd] + (1.0 - inside) * keep * stp_ref[0, hd]
                s_b = s_now.astype(BF16)
                o = _dot_nt(q, s_b)
                rstd = lax.rsqrt(jnp.mean(o * o, axis=-1, keepdims=True) + EPS)
                oh = o * rstd
                sg = _sigmoid(g)
                d_og = do_ref[rows, hd * hv:(hd + 1) * hv]
                dz_ref[rows, 2 * dk + dv + hd * hv:2 * dk + dv + (hd + 1) * hv] = (
                    d_og * oh * hg_ref[...] * (sg * (1.0 + g * (1.0 - sg)))).astype(BF16)
                don = d_og * (g * sg)
                dhg_ref[...] += _rowsum(don * oh)
                doh = don * hg_ref[...]
                d_o = (rstd * (doh - oh * jnp.mean(doh * oh, axis=-1, keepdims=True))).astype(BF16)
                dz_ref[rows, hd * hk:(hd + 1) * hk] = (_dot(d_o, s_b) * scale).astype(BF16)
                ds_t = ds_ref[hd] + _dot_tn(d_o, q)
                ds_b = ds_t.astype(BF16)
                dkd = _dot(v, ds_b)
                dz_ref[rows, 2 * dk + hd * hv:2 * dk + (hd + 1) * hv] = _dot_nt(kd.astype(BF16), ds_b).astype(BF16)
                dtot = etot[:, ks] * _rowsum(ds_t * s_prev)
                ds_ref[hd] = ds_t * etot[:, ks]
                dz_ref[rows, dk + hd * hk:dk + (hd + 1) * hk] = (dkd * dec[:, ks]).astype(BF16)
                e = dkd * kd
                dla_ref[rows, ks] = dtot + jnp.dot(tri_strict, e, precision=HI, preferred_element_type=F32)
            return carry

        lax.fori_loop(0, per, chunk, 0, unroll=True)
        dla = jnp.where(rows_id >= PAD_ROWS, dla_ref[...], 0.0)
        dgp = dla * (1.0 / GATE_NORM) * (1.0 - _sigmoid(gp))
        dgb_ref[...] += _rowsum(dgp)
        dgp_b = dgp.astype(BF16)
        dgw_ref[...] += _dot_tn(r_ref[...].astype(BF16), dgp_b)
        dr_ref[...] = _dot_nt(dgp_b, gw_ref[...]).astype(BF16)

    back = lambda i: (ntile - 1 - i, 0)
    const2 = lambda i: (0, 0)
    return pl.pallas_call(
        body, grid=(ntile,),
        in_specs=dep_specs + [
                  pl.BlockSpec((rt, z.shape[1]), back), pl.BlockSpec((rt, GATE_PAD), back), pl.BlockSpec((rt, dv), back),
                  pl.BlockSpec((per, HEADS, hv, hk), lambda i: (ntile - 1 - i, 0, 0, 0)),
                  pl.BlockSpec((1, HEADS, hv, hk), lambda i: (jnp.maximum((ntile - 1 - i) * per - 1, 0), 0, 0, 0)),
                  _resident(gate_w.shape), _resident((1, dk)), _resident((1, hv))],
        out_specs=[pl.BlockSpec((rt, z.shape[1]), back), pl.BlockSpec((rt, GATE_PAD), back),
                   pl.BlockSpec(gate_w.shape, const2), pl.BlockSpec((1, dk), const2), pl.BlockSpec((1, hv), const2)],
        out_shape=[_sds(z.shape, BF16), _sds((t, GATE_PAD), BF16), _sds(gate_w.shape, F32), _sds((1, dk), F32),
                   _sds((1, hv), F32)],
        scratch_shapes=[pltpu.VMEM((HEADS, hv, hk), F32), pltpu.VMEM((rt, dk), F32), pltpu.VMEM((rt, dk), F32)],
        compiler_params=_cparams("arbitrary"), name=name)(*deps, z, r, dog, states, states, gate_w, gate_b, head_g)


def _head(h, gain, target, *, name):
    t, d = h.shape
    rt = _row_tile(t, 832)

    def body(h_ref, g_ref, t_ref, dh_ref, loss_ref, dg_ref):
        i = pl.program_id(0)

        @pl.when(i == 0)
        def _():
            loss_ref[...] = jnp.zeros_like(loss_ref)
            dg_ref[...] = jnp.zeros_like(dg_ref)

        hv = h_ref[...]
        rstd = lax.rsqrt(jnp.mean(hv * hv, axis=-1, keepdims=True) + EPS)
        xh = hv * rstd
        err = jnp.where(_row_ids(i, rt) >= CHUNK, xh * g_ref[...] - t_ref[...], 0.0)
        loss_ref[...] += (0.5 / d) * jnp.sum(err * err)
        dy = err * (1.0 / d)
        dg_ref[...] += _rowsum(dy * xh)
        dxh = dy * g_ref[...]
        dh_ref[...] = rstd * (dxh - xh * jnp.mean(dxh * xh, axis=-1, keepdims=True))

    return pl.pallas_call(
        body, grid=(t // rt,),
        in_specs=[pl.BlockSpec((rt, d), lambda i: (i, 0)), _resident((1, d)), pl.BlockSpec((rt, d), lambda i: (i, 0))],
        out_specs=[pl.BlockSpec((rt, d), lambda i: (i, 0)), pl.BlockSpec((8, LANE), lambda i: (0, 0)),
                   pl.BlockSpec((1, d), lambda i: (0, 0))],
        out_shape=[_sds((t, d), F32), _sds((8, LANE), F32), _sds((1, d), F32)],
        compiler_params=_cparams("arbitrary"), name=name)(h, gain, target)


def _adamw_math(w, g, m, v):
    m = ADAM_B1 * m + (1.0 - ADAM_B1) * g
    v = ADAM_B2 * v + (1.0 - ADAM_B2) * (g * g)
    m_hat = m / (1.0 - ADAM_B1 ** ADAM_STEP)
    v_hat = v / (1.0 - ADAM_B2 ** ADAM_STEP)
    return -ADAM_LR * (m_hat / (jnp.sqrt(v_hat) + ADAM_EPS) + ADAM_WD * w), m, v


N_CHIP = N_DEV // 2
BLOCK_ELEMS = 128 * 1024


def _my_slot():
    return 4 * lax.axis_index("x") + 2 * lax.axis_index("y") + lax.axis_index("c")


def _my_chip():
    return 2 * lax.axis_index("x") + lax.axis_index("y")


def _row_block(r, c):
    cap = max(8, BLOCK_ELEMS // (-(-c // LANE) * LANE))
    return max(b for b in range(8, r + 1, 8) if r % b == 0 and b <= max(cap, 8))


def _pair_add(a, landed, *, name):
    _, r, c = a.shape
    rb = _row_block(r, c)

    def body(core_ref, a_ref, l_ref, o_ref):
        o_ref[...] = a_ref[...] + l_ref[...]

    one = pl.BlockSpec((None, rb, c), lambda q, i, core: (q, i, 0))
    grid_spec = pltpu.PrefetchScalarGridSpec(
        num_scalar_prefetch=1, grid=(N_CHIP, r // rb),
        in_specs=[pl.BlockSpec((None, rb, c), lambda q, i, core: (2 * q + core[0], i, 0)), one], out_specs=one)
    return pl.pallas_call(
        body, grid_spec=grid_spec, out_shape=_sds((N_CHIP, r, c), F32),
        compiler_params=_cparams("parallel", "parallel"), name=name)(lax.axis_index("c").reshape(1), a, landed)


def _reduce_adam(parts, w, m, v, *, after=None, name):
    nl, r, c = w.shape
    rb = _row_block(r, c)
    dep_specs, deps = _dep_specs(after)

    def body(*refs):
        refs = refs[len(deps):]
        p_refs = refs[:2 * nl]
        w_ref, m_ref, v_ref, g_out, d_out, m_out, v_out = refs[2 * nl:]
        layer = pl.program_id(0)
        chip = _my_chip()
        for li in range(nl):
            @pl.when(layer == li)
            def _(li=li):
                mine_ref, land_ref = p_refs[2 * li], p_refs[2 * li + 1]
                g = None
                for q in range(N_CHIP):
                    term = jnp.where(chip == q, mine_ref[q], land_ref[q])
                    g = term if g is None else g + term
                g_out[...] = g
                d_out[...], m_out[...], v_out[...] = _adamw_math(w_ref[...], g, m_ref[...], v_ref[...])

    blk = pl.BlockSpec((None, rb, c), lambda l, i: (l, i, 0))
    p_specs = [pl.BlockSpec((N_CHIP, rb, c), lambda l, i, li=li: (0, jnp.where(l == li, i, 0), 0))
               for li in range(nl) for _ in range(2)]
    flat = [p for pair in parts for p in pair]
    return pl.pallas_call(
        body, grid=(nl, r // rb), in_specs=dep_specs + p_specs + [blk, blk, blk], out_specs=[blk] * 4,
        out_shape=[_sds(w.shape, F32)] * 4, compiler_params=_cparams("arbitrary", "arbitrary"),
        name=name)(*deps, *flat, w, m, v)


def _reduce8(own, landed, *, name):
    r, c = own.shape

    def body(own_ref, p_ref, o_ref):
        me = _my_slot()
        g = None
        for dev in range(N_DEV):
            term = jnp.where(me == dev, own_ref[...], p_ref[dev])
            g = term if g is None else g + term
        o_ref[...] = g

    return pl.pallas_call(body, out_shape=_sds((r, c), F32), name=name)(own, landed)


def _adamw_small(w, g, m, v, *, name):
    def body(w_ref, g_ref, m_ref, v_ref, d_out, m_out, v_out):
        d_out[...], m_out[...], v_out[...] = _adamw_math(w_ref[...], g_ref[...], m_ref[...], v_ref[...])

    return pl.pallas_call(body, out_shape=[_sds(w.shape, F32)] * 3, name=name)(w, g, m, v)


_HBM = pl.BlockSpec(memory_space=pltpu.HBM)
_SEM = pl.BlockSpec(memory_space=pltpu.SEMAPHORE)
_DATAFLOW = pltpu.SideEffectType.DATAFLOW_SIDE_EFFECTING


def _plan_to_all(src, land):
    x, y, c = lax.axis_index("x"), lax.axis_index("y"), lax.axis_index("c")
    return [(src, land.at[_my_slot()], (x ^ ((d >> 2) & 1), y ^ ((d >> 1) & 1), c ^ (d & 1))) for d in range(1, N_DEV)]


def _plan_to_sibling(src, land):
    x, y, c = lax.axis_index("x"), lax.axis_index("y"), lax.axis_index("c")
    return [(src.at[2 * q + 1 - c], land.at[q], (x, y, 1 - c)) for q in range(N_CHIP)]


def _plan_to_chips(src, land):
    x, y, c = lax.axis_index("x"), lax.axis_index("y"), lax.axis_index("c")
    peers = [(x ^ (d >> 1), y ^ (d & 1)) for d in range(1, N_CHIP)]
    return [(src.at[2 * px + py], land.at[_my_chip()], (px, py, c)) for px, py in peers]


_PLAN_COPIES = {_plan_to_all: N_DEV - 1, _plan_to_sibling: N_CHIP, _plan_to_chips: N_CHIP - 1}


def _exchange_copies(plan, ins, lands, send, recv):
    per = _PLAN_COPIES[plan]
    copies = []
    for a, (src, land) in enumerate(zip(ins, lands)):
        for i, (s, dst, dev) in enumerate(plan(src, land)):
            copies.append(pltpu.make_async_remote_copy(
                src_ref=s, dst_ref=dst, send_sem=send.at[a * per + i], recv_sem=recv.at[a * per + i],
                device_id=dev, device_id_type=pl.DeviceIdType.MESH))
    return copies


def _place_own(a, dtype, *, after=None, name):
    r, c = a.shape
    rb = _row_block(r, c)
    dep_specs, deps = _dep_specs(after)

    def body(*refs):
        a_ref, o_ref = refs[1 + len(deps):]
        o_ref[...] = a_ref[...].astype(dtype)

    grid_spec = pltpu.PrefetchScalarGridSpec(
        num_scalar_prefetch=1, grid=(r // rb,), in_specs=dep_specs + [pl.BlockSpec((rb, c), lambda i, me: (i, 0))],
        out_specs=pl.BlockSpec((None, rb, c), lambda i, me: (me[0], i, 0)))
    return pl.pallas_call(body, grid_spec=grid_spec, out_shape=_sds((N_DEV, r, c), dtype),
                          compiler_params=_cparams("arbitrary"), name=name)(_my_slot().reshape(1), *deps, a)


def _plan_gather_first(land, _):
    x, y, c = lax.axis_index("x"), lax.axis_index("y"), lax.axis_index("c")
    mine = land.at[_my_slot()]
    return [(mine, mine, (x, y, 1 - c))] + [(mine, mine, (x ^ (d >> 1), y ^ (d & 1), c)) for d in range(1, N_CHIP)]


def _plan_gather_relay(land, _):
    x, y, c = lax.axis_index("x"), lax.axis_index("y"), lax.axis_index("c")
    slots = [land.at[4 * (x ^ (d >> 1)) + 2 * (y ^ (d & 1)) + c] for d in range(1, N_CHIP)]
    return [(s, s, (x, y, 1 - c)) for s in slots]


_PLAN_COPIES[_plan_gather_first] = N_CHIP
_PLAN_COPIES[_plan_gather_relay] = N_CHIP - 1


def _exchange_start(plan, arrs, lands, *, after=None, name):
    if lands is None:
        lands = [lax.empty((N_CHIP,) + a.shape[1:], a.dtype) for a in arrs]
    bufs = list(lands) if arrs is None else list(arrs) + list(lands)
    n, nb = len(lands), len(bufs)
    nsem = n * _PLAN_COPIES[plan]
    dep_specs, deps = _dep_specs(after)

    def body(*refs):
        ins, land_refs = refs[:n], refs[nb - n:nb]
        send, recv = refs[nb + len(deps)], refs[nb + len(deps) + 1]
        for cp in _exchange_copies(plan, ins, land_refs, send, recv):
            cp.start()
        refs[-1][...] = jnp.zeros_like(refs[-1])

    out = pl.pallas_call(
        body, name=name,
        out_shape=(pltpu.SemaphoreType.DMA((nsem,)), pltpu.SemaphoreType.DMA((nsem,)),
                   *[pltpu.HBM(a.shape, a.dtype) for a in bufs], _sds((8, LANE), F32)),
        in_specs=[_HBM] * nb + dep_specs,
        out_specs=(_SEM, _SEM, *([_HBM] * nb), pl.BlockSpec(memory_space=pltpu.VMEM)),
        input_output_aliases={i: 2 + i for i in range(nb)},
        compiler_params=pltpu.CompilerParams(has_side_effects=_DATAFLOW),
    )(*[pltpu.with_memory_space_constraint(a, pltpu.HBM) for a in bufs], *deps)
    return (plan, n, out[0], out[1], list(out[2:2 + nb])), out[-1]


def _exchange_wait(state, after, *, name):
    plan, n, send_sem, recv_sem, bufs = state
    nb = len(bufs)
    after = list(after) if isinstance(after, (list, tuple)) else [after]

    def body(*refs):
        ins, land_refs, send, recv = refs[:n], refs[nb - n:nb], refs[nb], refs[nb + 1]
        for cp in _exchange_copies(plan, ins, land_refs, send, recv):
            cp.wait_send()
            cp.wait_recv()

    out = pl.pallas_call(
        body, name=name, out_shape=[pltpu.HBM(a.shape, a.dtype) for a in bufs],
        in_specs=[_HBM] * nb + [_SEM, _SEM] + [pl.BlockSpec(memory_space=pl.ANY)] * len(after), out_specs=[_HBM] * nb,
        input_output_aliases={i: i for i in range(nb)},
        compiler_params=pltpu.CompilerParams(has_side_effects=_DATAFLOW),
    )(*bufs, send_sem, recv_sem, *after)
    return list(out[:n]), list(out[nb - n:])


def _dep_specs(after):
    return ([], []) if after is None else ([pl.BlockSpec(memory_space=pl.ANY)], [after])


def _pack(arrays):
    flat = jnp.concatenate([a.reshape(-1) for a in arrays])
    size = flat.shape[0]
    padded = -(-size // (8 * LANE)) * (8 * LANE)
    return jnp.pad(flat, (0, padded - size)).reshape(padded // LANE, LANE)


def _unpack(packed, shapes):
    flat = packed.reshape(-1)
    out, pos = [], 0
    for shp in shapes:
        size = 1
        for s in shp:
            size *= s
        out.append(flat[pos:pos + size].reshape(shp))
        pos += size
    return out


def _undo_column_split(g):
    return jnp.transpose(g, (1, 0, 2)).reshape(g.shape[1], N_DEV * g.shape[2])


def _column_split(a):
    r, c = a.shape
    return jnp.transpose(a.reshape(r, N_DEV, c // N_DEV), (1, 0, 2))


class _WholeWeights:
    def __init__(self, groups):
        self.groups = groups
        self.grads = {}

    def fetch(self, group, after):
        return self.groups[group]

    def emit(self, group, grads):
        self.grads.update(grads)
        return None

    def poll(self, after):
        return None


def _local_step(x, target, replicated, src):
    d = x.shape[1]
    mix_g, ffn_g = replicated["mix_g"], replicated["ffn_g"]
    h0 = jnp.concatenate([jnp.zeros((CHUNK, d), F32), x], axis=0)
    tgt = jnp.concatenate([jnp.zeros((CHUNK, d), F32), target], axis=0)
    cp = src.fetch("cp", [h0, tgt])
    h0 = lax.dynamic_update_slice(h0, cp["meta"], (PAD_ROWS, 0))
    cp_mid = (cp["conv_w"], replicated["conv_b"], replicated["ln_g"], replicated["ln_b"], replicated["pool_w"],
              replicated["pool_scale"])

    z0, u0 = _linear_fwd(h0, cp["cp_w_in"], gain=mix_g[0:1], name="cp_in")
    cat = _cp_mid_fwd(z0, *cp_mid, name="cp_mid")
    h1 = _linear_fwd(cat, cp["cp_w_out"], res=h0, name="cp_out")
    ffn0 = src.fetch("ffn0", h1)
    h2, uf0, rf0 = _ffn_fwd(h1, ffn_g[0:1], ffn0["w1"], ffn0["w2"], name="ffn0")
    gla = src.fetch("gla", h2)
    gla_mid = (gla["gate_w"], gla["gate_b"], gla["head_g"])
    z1, u1 = _linear_fwd(h2, gla["gla_w_qkvg"], gain=mix_g[1:2], name="gla_in")
    r1 = _linear_fwd(u1, gla["gla_w_r"], name="gla_in_r")
    og, states = _gla_mid_fwd(z1, r1, *gla_mid, name="gla_mid")
    h3 = _linear_fwd(og, gla["gla_w_out"], res=h2, name="gla_out")
    ffn1 = src.fetch("ffn1", h3)
    h4, uf1, rf1 = _ffn_fwd(h3, ffn_g[1:2], ffn1["w1"], ffn1["w2"], name="ffn1")
    dh4, loss, d_final_g = _head(h4, replicated["final_g"], tgt, name="head")

    dh3, dhh1, dob1, dffn_g1 = _ffn_bwd_x(h3, dh4, ffn_g[1:2], rf1, ffn1["w1"], ffn1["w2"], name="ffn1_bwd_x")
    dw1_1, dw2_1 = _ffn_bwd_w(uf1, dhh1, rf1, dob1, name="ffn1_bwd_w")
    sent = src.emit("ffn1", dict(w1=dw1_1, w2=dw2_1))
    dog = _linear_bwd_x([dh3], [gla["gla_w_out"]], after=sent, name="gla_out_dx")
    d_gla_w_out = _linear_bwd_w(og, dh3, name="gla_out_dw")
    sent = src.poll(d_gla_w_out)
    dz1, dr1, d_gate_w, d_gate_b, d_head_g = _gla_mid_bwd(z1, r1, dog, states, *gla_mid, after=sent, name="gla_mid_bwd")
    dh2, dmix_g1 = _linear_bwd_x([dz1, dr1], [gla["gla_w_qkvg"], gla["gla_w_r"]],
                                 norm=(h2, mix_g[1:2], dh3), name="gla_in_dx")
    d_gla_w_qkvg = _linear_bwd_w(u1, dz1, name="gla_in_dw")
    d_gla_w_r = _linear_bwd_w(u1, dr1, name="gla_in_r_dw")
    sent = src.emit("gla", dict(gla_w_qkvg=d_gla_w_qkvg, gla_w_r=d_gla_w_r, gla_w_out=d_gla_w_out))
    dh1, dhh0, dob0, dffn_g0 = _ffn_bwd_x(h1, dh2, ffn_g[0:1], rf0, ffn0["w1"], ffn0["w2"], after=sent, name="ffn0_bwd_x")
    dw1_0, dw2_0 = _ffn_bwd_w(uf0, dhh0, rf0, dob0, name="ffn0_bwd_w")
    src.poll(dw1_0)
    sent = src.emit("ffn0", dict(w1=dw1_0, w2=dw2_0))
    dcat = _linear_bwd_x([dh1], [cp["cp_w_out"]], after=sent, name="cp_out_dx")
    d_cp_w_out = _linear_bwd_w(cat, dh1, name="cp_out_dw")
    sent = src.poll(d_cp_w_out)
    dz0, d_conv_w, d_conv_b, d_ln_g, d_ln_b, d_pool_w, d_pool_scale = _cp_mid_bwd(z0, dcat, *cp_mid, after=sent,
                                                                                 name="cp_mid_bwd")
    dh0, dmix_g0 = _linear_bwd_x([dz0], [cp["cp_w_in"]], norm=(h0, mix_g[0:1], dh1), name="cp_in_dx")
    d_cp_w_in = _linear_bwd_w(u0, dz0, name="cp_in_dw")

    small = dict(
        mix_g=jnp.concatenate([dmix_g0, dmix_g1]), ffn_g=jnp.concatenate([dffn_g0, dffn_g1]), conv_b=d_conv_b, ln_g=d_ln_g,
        ln_b=d_ln_b, pool_w=d_pool_w, pool_scale=d_pool_scale, final_g=d_final_g, meta=dh0[PAD_ROWS:CHUNK], conv_w=d_conv_w,
        gate_w=d_gate_w, gate_b=d_gate_b, head_g=d_head_g)
    src.emit("cp", dict(cp_w_in=d_cp_w_in, cp_w_out=d_cp_w_out, small=small))
    return loss, dh0[CHUNK:], small


_REPLICATED = ("mix_norm_g", "ffn_norm_g", "cp_conv_b", "cp_ln_g", "cp_ln_b", "cp_pool_w", "cp_pool_scale", "final_norm_g")
_SMALL_SHARDED = ("meta_tokens", "cp_conv_w", "gla_gate_w2", "gla_gate_b", "gla_head_g")
_LARGE = ("ffn_w1", "ffn_w2", "cp_w_in", "cp_w_out", "gla_w_in", "gla_w_out")
_NAMES = ("meta_tokens", "mix_norm_g", "ffn_norm_g", "ffn_w1", "ffn_w2", "cp_w_in", "cp_conv_w", "cp_conv_b", "cp_ln_g",
          "cp_ln_b", "cp_pool_w", "cp_pool_scale", "cp_w_out", "gla_w_in", "gla_gate_w2", "gla_gate_b", "gla_head_g",
          "gla_w_out", "final_norm_g")
_SMALL_GRADS = ("mix_g", "ffn_g", "conv_b", "ln_g", "ln_b", "pool_w", "pool_scale", "final_g", "meta", "conv_w", "gate_w",
                "gate_b", "head_g")
_GROUPS = ("cp", "ffn0", "gla", "ffn1")


class _Exchanges:
    def __init__(self, w, d):
        self.d = d
        self.small_shards = [w[n] for n in _SMALL_SHARDED]
        shards = dict(
            cp=[(w["cp_w_in"][0], BF16), (w["cp_w_out"][0], BF16), (_pack(self.small_shards), F32)],
            ffn0=[(w["ffn_w1"][0], BF16), (w["ffn_w2"][0], BF16)],
            gla=[(w["gla_w_in"][0], BF16), (w["gla_w_out"][0], BF16)],
            ffn1=[(w["ffn_w1"][1], BF16), (w["ffn_w2"][1], BF16)])
        self.gathers = {}
        self.to_sibling, self.to_chips = [], {}
        token = None
        for group in _GROUPS:
            lands = [_place_own(a, dtype, after=token, name=f"place_w_{group}_{k}")
                     for k, (a, dtype) in enumerate(shards[group])]
            self.gathers[group], token = _exchange_start(_plan_gather_first, None, lands, after=token,
                                                         name=f"start_w_{group}")
        self.token = token

    def fetch(self, group, after):
        d = self.d
        after = (list(after) if isinstance(after, (list, tuple)) else [after]) + [self.token]
        _, lands = _exchange_wait(self.gathers[group], after, name=f"wait_w_{group}")
        relay, token = _exchange_start(_plan_gather_relay, None, lands, name=f"relay_w_{group}")
        _, got = _exchange_wait(relay, token, name=f"wait_relay_w_{group}")
        if group in ("ffn0", "ffn1"):
            return dict(w1=got[0], w2=got[1])
        if group == "gla":
            qkvg = 3 * d
            w_in = _undo_column_split(got[0])
            return dict(gla_w_qkvg=w_in[:, :qkvg], gla_w_r=jnp.pad(w_in[:, qkvg:], ((0, 0), (0, GATE_PAD - GATE_RANK))),
                        gla_w_out=got[1].reshape(d, d), gate_w=self.gate_w, gate_b=self.gate_b, head_g=self.head_g)
        shapes = [s.shape for s in self.small_shards]
        parts = [_unpack(got[2][dev], shapes) for dev in range(N_DEV)]
        meta, conv_w, gate_w, self.gate_b, self.head_g = [
            jnp.concatenate([parts[dev][k] for dev in range(N_DEV)], axis=-1) for k in range(len(shapes))]
        self.gate_w = jnp.pad(gate_w[0], ((0, GATE_PAD - GATE_RANK), (0, 0))).astype(BF16)
        return dict(cp_w_in=_undo_column_split(got[0]), cp_w_out=got[1].reshape(d, d), meta=meta,
                    conv_w=jnp.pad(conv_w[0], ((0, 1), (0, 0))))

    def emit(self, group, g):
        d = self.d
        if group in ("ffn0", "ffn1"):
            arrs = [g["w1"], g["w2"]]
        elif group == "gla":
            w_in = jnp.concatenate([g["gla_w_qkvg"], g["gla_w_r"][:, :GATE_RANK]], axis=1)
            arrs = [_column_split(w_in), g["gla_w_out"].reshape(N_DEV, d // N_DEV, d)]
        else:
            s = dict(g["small"])
            s.update(pool_w=s["pool_w"][None], conv_w=s["conv_w"][None, :CONV_WIDTH], gate_w=s["gate_w"][None, :GATE_RANK])
            self.small_grads = [s[n] for n in _SMALL_GRADS]
            self.small_own = _pack(self.small_grads)
            self.small_sent, self.token = _exchange_start(
                _plan_to_all, [self.small_own], [lax.empty((N_DEV,) + self.small_own.shape, F32)], after=self.token,
                name="start_g_small")
            arrs = [_column_split(g["cp_w_in"]), g["cp_w_out"].reshape(N_DEV, d // N_DEV, d)]
        state, self.token = _exchange_start(_plan_to_sibling, arrs, None, after=self.token, name=f"start_g1_{group}")
        self.to_sibling.append((group, state))
        return self.token

    def poll(self, after):
        for group, state in self.to_sibling:
            arrs, landed = _exchange_wait(state, after, name=f"wait_g1_{group}")
            sums = [_pair_add(a, l, name=f"chip_sum_{group}_{k}") for k, (a, l) in enumerate(zip(arrs, landed))]
            self.to_chips[group], self.token = _exchange_start(_plan_to_chips, sums, None, after=self.token,
                                                               name=f"start_g2_{group}")
            after = self.token
        self.to_sibling = []
        return self.token

    def finish(self, w, mom, var):
        out = {}
        self.poll(self.token)
        after = self.token

        def landed(group):
            sums, got = _exchange_wait(self.to_chips[group], after, name=f"wait_g2_{group}")
            return list(zip(sums, got))

        def adam(n, parts, behind=None):
            out[n] = _reduce_adam(parts, w[n], mom[n], var[n], after=behind, name=f"adam_{n}")
            return out[n][0]

        ffn1 = landed("ffn1")
        after = ffn1[0][1]
        gla = landed("gla")
        after = gla[0][1]
        ffn0 = landed("ffn0")
        after = adam("ffn_w1", [ffn0[0], ffn1[0]])
        after = adam("ffn_w2", [ffn0[1], ffn1[1]], after)
        after = adam("gla_w_in", [gla[0]], after)
        after = adam("gla_w_out", [gla[1]], after)
        cp = landed("cp")
        adam("cp_w_in", [cp[0]])
        after = adam("cp_w_out", [cp[1]])
        (self.small_own,), (small_all,) = _exchange_wait(self.small_sent, after, name="wait_g_small")

        names = _REPLICATED + _SMALL_SHARDED
        small_sum = _unpack(_reduce8(self.small_own, small_all, name="sum_small_grads"), [a.shape for a in self.small_grads])
        me = _my_slot()
        grad = {}
        for n, full in zip(names, small_sum):
            if n in _SMALL_SHARDED:
                width = w[n].shape[-1]
                full = lax.dynamic_slice_in_dim(full, me * width, width, axis=full.ndim - 1)
            grad[n] = full
        packed = [_pack([t[n] for n in names]) for t in (w, grad, mom, var)]
        small_out = [_unpack(p, [w[n].shape for n in names]) for p in _adamw_small(*packed, name="adam_small")]
        for k, n in enumerate(names):
            out[n] = (grad[n], small_out[0][k], small_out[1][k], small_out[2][k])
        return out


def kernel(x, meta_tokens, mix_norm_g, ffn_norm_g, ffn_w1, ffn_w2, cp_w_in, cp_conv_w, cp_conv_b, cp_ln_g, cp_ln_b, cp_pool_w, cp_pool_scale, cp_w_out, gla_w_in, gla_gate_w2, gla_gate_b, gla_head_g, gla_w_out, final_norm_g, loss_target, m_meta_tokens, m_mix_norm_g, m_ffn_norm_g, m_ffn_w1, m_ffn_w2, m_cp_w_in, m_cp_conv_w, m_cp_conv_b, m_cp_ln_g, m_cp_ln_b, m_cp_pool_w, m_cp_pool_scale, m_cp_w_out, m_gla_w_in, m_gla_gate_w2, m_gla_gate_b, m_gla_head_g, m_gla_w_out, m_final_norm_g, v_meta_tokens, v_mix_norm_g, v_ffn_norm_g, v_ffn_w1, v_ffn_w2, v_cp_w_in, v_cp_conv_w, v_cp_conv_b, v_cp_ln_g, v_cp_ln_b, v_cp_pool_w, v_cp_pool_scale, v_cp_w_out, v_gla_w_in, v_gla_gate_w2, v_gla_gate_b, v_gla_head_g, v_gla_w_out, v_final_norm_g):
    w = dict(meta_tokens=meta_tokens, mix_norm_g=mix_norm_g, ffn_norm_g=ffn_norm_g, ffn_w1=ffn_w1, ffn_w2=ffn_w2,
             cp_w_in=cp_w_in, cp_conv_w=cp_conv_w, cp_conv_b=cp_conv_b, cp_ln_g=cp_ln_g, cp_ln_b=cp_ln_b,
             cp_pool_w=cp_pool_w, cp_pool_scale=cp_pool_scale, cp_w_out=cp_w_out, gla_w_in=gla_w_in,
             gla_gate_w2=gla_gate_w2, gla_gate_b=gla_gate_b, gla_head_g=gla_head_g, gla_w_out=gla_w_out,
             final_norm_g=final_norm_g.reshape(1, -1))
    mom = dict(meta_tokens=m_meta_tokens, mix_norm_g=m_mix_norm_g, ffn_norm_g=m_ffn_norm_g, ffn_w1=m_ffn_w1, ffn_w2=m_ffn_w2,
               cp_w_in=m_cp_w_in, cp_conv_w=m_cp_conv_w, cp_conv_b=m_cp_conv_b, cp_ln_g=m_cp_ln_g, cp_ln_b=m_cp_ln_b,
               cp_pool_w=m_cp_pool_w, cp_pool_scale=m_cp_pool_scale, cp_w_out=m_cp_w_out, gla_w_in=m_gla_w_in,
               gla_gate_w2=m_gla_gate_w2, gla_gate_b=m_gla_gate_b, gla_head_g=m_gla_head_g, gla_w_out=m_gla_w_out,
               final_norm_g=m_final_norm_g.reshape(1, -1))
    var = dict(meta_tokens=v_meta_tokens, mix_norm_g=v_mix_norm_g, ffn_norm_g=v_ffn_norm_g, ffn_w1=v_ffn_w1, ffn_w2=v_ffn_w2,
               cp_w_in=v_cp_w_in, cp_conv_w=v_cp_conv_w, cp_conv_b=v_cp_conv_b, cp_ln_g=v_cp_ln_g, cp_ln_b=v_cp_ln_b,
               cp_pool_w=v_cp_pool_w, cp_pool_scale=v_cp_pool_scale, cp_w_out=v_cp_w_out, gla_w_in=v_gla_w_in,
               gla_gate_w2=v_gla_gate_w2, gla_gate_b=v_gla_gate_b, gla_head_g=v_gla_head_g, gla_w_out=v_gla_w_out,
               final_norm_g=v_final_norm_g.reshape(1, -1))
    d = x.shape[-1]
    replicated = dict(mix_g=w["mix_norm_g"], ffn_g=w["ffn_norm_g"], conv_b=w["cp_conv_b"], ln_g=w["cp_ln_g"],
                      ln_b=w["cp_ln_b"], pool_w=w["cp_pool_w"][0].astype(BF16), pool_scale=w["cp_pool_scale"],
                      final_g=w["final_norm_g"])
    exchanges = _Exchanges(w, d)
    loss_blk, grad_x, _ = _local_step(x[0], loss_target[0], replicated, exchanges)
    loss = lax.psum(loss_blk[0, 0], ("x", "y", "c"))
    out = exchanges.finish(w, mom, var)

    def leaf(n, k):
        a = out[n][k]
        return a.reshape(-1) if n == "final_norm_g" else a

    return (loss, grad_x[None], *[leaf(n, 0) for n in _NAMES], *[leaf(n, 1) for n in _NAMES],
            *[leaf(n, 2) for n in _NAMES], *[leaf(n, 3) for n in _NAMES])
```

```python
import functools

import jax
import jax.numpy as jnp
from jax import lax
from jax.experimental import pallas as pl
from jax.experimental.pallas import tpu as pltpu

F32, BF16 = jnp.float32, jnp.bfloat16
N_DEV = 8
CHUNK = 64
N_META = 16
PAD_ROWS = CHUNK - N_META
HALO = 32
EPS = 1e-5
CONV_WIDTH = 31
POOL_WINDOWS = (2, 4, 8, 16)
HEADS = 4
GATE_RANK = 16
GATE_NORM = 16.0
GATE_PAD = 128
ADAM_LR, ADAM_B1, ADAM_B2, ADAM_EPS, ADAM_WD, ADAM_STEP = 0.001, 0.9, 0.999, 1e-08, 0.01, 10
V7X_VMEM_LIMIT = 56 * 2 ** 20
LANE = 128
HI = lax.Precision.HIGHEST


def _cparams(*sem):
    return pltpu.CompilerParams(dimension_semantics=sem, vmem_limit_bytes=V7X_VMEM_LIMIT)


def _row_tile(t, cap):
    best = CHUNK
    for r in range(CHUNK, min(t, cap) + 1, CHUNK):
        if t % r == 0:
            best = r
    return best


def _resident(shape):
    return pl.BlockSpec(shape, lambda *_: (0,) * len(shape), pipeline_mode=pl.Buffered(1))


def _dot(a, b):
    return jnp.dot(a, b, preferred_element_type=F32)


def _dot_nt(a, b):
    return lax.dot_general(a, b, (((1,), (1,)), ((), ())), preferred_element_type=F32)


def _dot_tn(a, b):
    return lax.dot_general(a, b, (((0,), (0,)), ((), ())), preferred_element_type=F32)


def _rowsum(a):
    return jnp.sum(a, axis=0, keepdims=True)


def _sigmoid(a):
    return 1.0 / (1.0 + jnp.exp(-a))


def _row_ids(tile, rt):
    return tile * rt + lax.broadcasted_iota(jnp.int32, (rt, 1), 0)


def _sds(shape, dtype):
    return jax.ShapeDtypeStruct(shape, dtype)


def _linear_fwd(x, w, *, gain=None, res=None, out_dtype=F32, name):
    t, k = x.shape
    n = w.shape[1]
    rt = _row_tile(t, 320)

    def body(*refs):
        refs = list(refs)
        x_ref, w_ref = refs[:2]
        pos = 2
        g_ref = r_ref = u_ref = None
        if gain is not None:
            g_ref = refs[pos]
            pos += 1
        if res is not None:
            r_ref = refs[pos]
            pos += 1
        y_ref = refs[pos]
        if gain is not None:
            u_ref = refs[pos + 1]
            xv = x_ref[...]
            u = (xv * lax.rsqrt(jnp.mean(xv * xv, axis=-1, keepdims=True) + EPS) * g_ref[...]).astype(BF16)
            u_ref[...] = u
        else:
            u = x_ref[...]
        y = _dot(u, w_ref[...])
        if res is not None:
            y = y + r_ref[...]
        y_ref[...] = y.astype(y_ref.dtype)

    rows = lambda i: (i, 0)
    in_specs = [pl.BlockSpec((rt, k), rows), _resident((k, n))]
    args = [x, w]
    if gain is not None:
        in_specs.append(_resident((1, k)))
        args.append(gain)
    if res is not None:
        in_specs.append(pl.BlockSpec((rt, n), rows))
        args.append(res)
    out_shape = [_sds((t, n), out_dtype)]
    out_specs = [pl.BlockSpec((rt, n), rows)]
    if gain is not None:
        out_shape.append(_sds((t, k), BF16))
        out_specs.append(pl.BlockSpec((rt, k), rows))
    out = pl.pallas_call(body, grid=(t // rt,), in_specs=in_specs, out_specs=out_specs, out_shape=out_shape,
                         compiler_params=_cparams("parallel"), name=name)(*args)
    return out if gain is not None else out[0]


def _linear_bwd_x(dys, ws, *, norm=None, after=None, name):
    t = dys[0].shape[0]
    k = ws[0].shape[0]
    rt = _row_tile(t, 320)
    nd = len(dys)
    dep_specs, deps = _dep_specs(after)

    def body(*refs):
        refs = list(refs)[len(deps):]
        dy_refs, w_refs = refs[:nd], refs[nd:2 * nd]
        dx = None
        for dy_ref, w_ref in zip(dy_refs, w_refs):
            part = _dot_nt(dy_ref[...].astype(BF16), w_ref[...])
            dx = part if dx is None else dx + part
        if norm is None:
            refs[2 * nd][...] = dx
            return
        h_ref, g_ref, dres_ref, dh_ref, dg_ref = refs[2 * nd:]
        hv = h_ref[...]
        rstd = lax.rsqrt(jnp.mean(hv * hv, axis=-1, keepdims=True) + EPS)
        xh = hv * rstd

        @pl.when(pl.program_id(0) == 0)
        def _():
            dg_ref[...] = jnp.zeros_like(dg_ref)

        dg_ref[...] += _rowsum(dx * xh)
        dxh = dx * g_ref[...]
        dh_ref[...] = dres_ref[...] + rstd * (dxh - xh * jnp.mean(dxh * xh, axis=-1, keepdims=True))

    rows = lambda i: (i, 0)
    in_specs = [pl.BlockSpec((rt, dy.shape[1]), rows) for dy in dys] + [_resident(w.shape) for w in ws]
    args = list(dys) + list(ws)
    out_shape = [_sds((t, k), F32)]
    out_specs = [pl.BlockSpec((rt, k), rows)]
    if norm is not None:
        h, gain, dres = norm
        in_specs += [pl.BlockSpec((rt, k), rows), _resident((1, k)), pl.BlockSpec((rt, k), rows)]
        args += [h, gain, dres]
        out_shape.append(_sds((1, k), F32))
        out_specs.append(pl.BlockSpec((1, k), lambda i: (0, 0)))
    out = pl.pallas_call(body, grid=(t // rt,), in_specs=dep_specs + in_specs, out_specs=out_specs, out_shape=out_shape,
                         compiler_params=_cparams("arbitrary"), name=name)(*deps, *args)
    return out if norm is not None else out[0]


DW_ROWS = 1024


def _linear_bwd_w(x, dy, *, name):
    t, k = x.shape
    n = dy.shape[1]
    nt = max(c for c in (512, 384, 256, LANE) if n % c == 0)

    def body(x_ref, dy_ref, o_ref):
        for c0 in range(0, t, DW_ROWS):
            rows = slice(c0, min(c0 + DW_ROWS, t))
            part = _dot_tn(x_ref[rows, :], dy_ref[rows, :].astype(BF16))
            if c0 == 0:
                o_ref[...] = part
            else:
                o_ref[...] += part

    return pl.pallas_call(
        body, grid=(n // nt,),
        in_specs=[_resident((t, k)), pl.BlockSpec((t, nt), lambda j: (0, j))],
        out_specs=pl.BlockSpec((k, nt), lambda j: (0, j)), out_shape=_sds((k, n), F32),
        compiler_params=_cparams("parallel"), name=name)(x, dy)


def _ffn_fwd(h, gain, w1g, w2g, *, name):
    t, d = h.shape
    f8 = w1g.shape[-1]
    rt = _row_tile(t, 832)

    def body(h_ref, g_ref, w1_ref, w2_ref, o_ref, u_ref, r_ref, acc_ref):
        j = pl.program_id(1)

        @pl.when(j == 0)
        def _():
            hv = h_ref[...]
            u_ref[...] = (hv * lax.rsqrt(jnp.mean(hv * hv, axis=-1, keepdims=True) + EPS) * g_ref[...]).astype(BF16)
            acc_ref[...] = jnp.zeros_like(acc_ref)

        a = jnp.maximum(_dot(u_ref[...], w1_ref[...]), 0.0)
        r_ref[...] = a.astype(BF16)
        acc_ref[...] += _dot((a * a).astype(BF16), w2_ref[...])

        @pl.when(j == N_DEV - 1)
        def _():
            o_ref[...] = h_ref[...] + acc_ref[...]

    return pl.pallas_call(
        body, grid=(t // rt, N_DEV),
        in_specs=[pl.BlockSpec((rt, d), lambda i, j: (i, 0)), _resident((1, d)),
                  pl.BlockSpec((None, d, f8), lambda i, j: (j, 0, 0)),
                  pl.BlockSpec((None, f8, d), lambda i, j: (j, 0, 0))],
        out_specs=[pl.BlockSpec((rt, d), lambda i, j: (i, 0)), pl.BlockSpec((rt, d), lambda i, j: (i, 0)),
                   pl.BlockSpec((rt, f8), lambda i, j: (i, j))],
        out_shape=[_sds((t, d), F32), _sds((t, d), BF16), _sds((t, N_DEV * f8), BF16)],
        scratch_shapes=[pltpu.VMEM((rt, d), F32)],
        compiler_params=_cparams("parallel", "arbitrary"), name=name)(h, gain, w1g, w2g)


def _ffn_bwd_x(h, dout, gain, r, w1g, w2g, *, after=None, name):
    t, d = h.shape
    f8 = w1g.shape[-1]
    rt = _row_tile(t, 832)
    last = N_DEV - 1
    dep_specs, deps = _dep_specs(after)

    def body(*refs):
        h_ref, do_ref, g_ref, r_ref, w1_ref, w2_ref, dh_ref, dhh_ref, dob_ref, dg_ref, du_ref = refs[len(deps):]
        i, j = pl.program_id(0), pl.program_id(1)

        @pl.when(j == 0)
        def _():
            dob_ref[...] = do_ref[...].astype(BF16)
            du_ref[...] = jnp.zeros_like(du_ref)

        dhh = (_dot_nt(dob_ref[...], w2_ref[...]) * (2.0 * r_ref[...].astype(F32))).astype(BF16)
        dhh_ref[...] = dhh
        du_ref[...] += _dot_nt(dhh, w1_ref[...])

        @pl.when(j == last)
        def _():
            @pl.when(i == 0)
            def _():
                dg_ref[...] = jnp.zeros_like(dg_ref)

            hv = h_ref[...]
            rstd = lax.rsqrt(jnp.mean(hv * hv, axis=-1, keepdims=True) + EPS)
            xh = hv * rstd
            du = du_ref[...]
            dg_ref[...] += _rowsum(du * xh)
            dxh = du * g_ref[...]
            dh_ref[...] = do_ref[...] + rstd * (dxh - xh * jnp.mean(dxh * xh, axis=-1, keepdims=True))

    rows = lambda i, j: (i, 0)
    return pl.pallas_call(
        body, grid=(t // rt, N_DEV),
        in_specs=dep_specs + [
                  pl.BlockSpec((rt, d), rows), pl.BlockSpec((rt, d), rows), _resident((1, d)),
                  pl.BlockSpec((rt, f8), lambda i, j: (i, j)),
                  pl.BlockSpec((None, d, f8), lambda i, j: (j, 0, 0)),
                  pl.BlockSpec((None, f8, d), lambda i, j: (j, 0, 0))],
        out_specs=[pl.BlockSpec((rt, d), rows), pl.BlockSpec((rt, f8), lambda i, j: (i, j)), pl.BlockSpec((rt, d), rows),
                   pl.BlockSpec((1, d), lambda i, j: (0, 0))],
        out_shape=[_sds((t, d), F32), _sds((t, N_DEV * f8), BF16), _sds((t, d), BF16), _sds((1, d), F32)],
        scratch_shapes=[pltpu.VMEM((rt, d), F32)],
        compiler_params=_cparams("arbitrary", "arbitrary"), name=name)(*deps, h, dout, gain, r, w1g, w2g)


def _ffn_bwd_w(u, dhh, r, dout_b, *, name):
    t, d = u.shape
    f8 = dhh.shape[1] // N_DEV

    def body(u_ref, dhh_ref, r_ref, dob_ref, dw1_ref, dw2_ref):
        for c0 in range(0, t, DW_ROWS):
            rows = slice(c0, min(c0 + DW_ROWS, t))
            rr = r_ref[rows, :].astype(F32)
            part1 = _dot_tn(u_ref[rows, :], dhh_ref[rows, :])
            part2 = _dot_tn((rr * rr).astype(BF16), dob_ref[rows, :])
            if c0 == 0:
                dw1_ref[...] = part1
                dw2_ref[...] = part2
            else:
                dw1_ref[...] += part1
                dw2_ref[...] += part2

    return pl.pallas_call(
        body, grid=(N_DEV,),
        in_specs=[_resident((t, d)), pl.BlockSpec((t, f8), lambda j: (0, j)), pl.BlockSpec((t, f8), lambda j: (0, j)),
                  _resident((t, d))],
        out_specs=[pl.BlockSpec((None, d, f8), lambda j: (j, 0, 0)), pl.BlockSpec((None, f8, d), lambda j: (j, 0, 0))],
        out_shape=[_sds((N_DEV, d, f8), F32), _sds((N_DEV, f8, d), F32)],
        compiler_params=_cparams("parallel"), name=name)(u, dhh, r, dout_b)


def _lane_blocks(width):
    lb = min(LANE, width)
    return [slice(s, s + lb) for s in range(0, width, lb)]


def _conv_rows(src_ref, w_ref, offset, dst_ref, nblk, width, bias_ref=None):
    def blk(rb, carry):
        base = pl.multiple_of(rb * CHUNK, CHUNK)
        for l, ls in enumerate(_lane_blocks(width)):
            acc = jnp.zeros((CHUNK, ls.stop - ls.start), F32)
            if bias_ref is not None:
                acc = acc + bias_ref[:, ls]
            for k in range(CONV_WIDTH):
                acc = acc + w_ref[k:k + 1, ls] * src_ref[l, pl.ds(base + offset(k), CHUNK), :]
            dst_ref[l, pl.ds(base, CHUNK), :] = acc
        return carry

    lax.fori_loop(0, nblk, blk, 0)


def _to_lane_blocks(ref, row0, value):
    for l, ls in enumerate(_lane_blocks(value.shape[1])):
        ref[l, row0:row0 + value.shape[0], :] = value[:, ls]


def _from_lane_blocks(ref):
    return jnp.concatenate([ref[l] for l in range(ref.shape[0])], axis=1)


def _pool_counts(rows, window):
    return jnp.clip(rows - PAD_ROWS + 1, 1, window).astype(F32)


def _trailing_sum(v, window):
    s, sh = v, 1
    while sh < window:
        s = s + pltpu.roll(s, sh, 0)
        sh *= 2
    return s


def _leading_sum(v, window):
    s, sh, n = v, 1, v.shape[0]
    while sh < window:
        s = s + pltpu.roll(s, n - sh, 0)
        sh *= 2
    return s


def _cp_mid_fwd(z, conv_w, conv_b, ln_g, ln_b, pool_w, pool_scale, *, name):
    t, ein = z.shape
    cd = conv_b.shape[1]
    pd = pool_scale.shape[1]
    pg = pd // len(POOL_WINDOWS)
    rt = _row_tile(t, 320)

    def body(z_ref, cw_ref, cb_ref, lg_ref, lb_ref, pw_ref, ps_ref, o_ref, gext, pext, conv_s):
        i = pl.program_id(0)

        @pl.when(i == 0)
        def _():
            _to_lane_blocks(gext, 0, jnp.zeros((HALO, cd), F32))
            pext[0:HALO, :] = jnp.zeros((HALO, pd), F32)

        _to_lane_blocks(gext, HALO, z_ref[:, 0:cd] * _sigmoid(z_ref[:, cd:2 * cd]))
        pext[HALO:HALO + rt, :] = z_ref[:, 2 * cd:]
        _conv_rows(gext, cw_ref, lambda k: k + HALO - (CONV_WIDTH - 1), conv_s, rt // CHUNK, cd, cb_ref)
        cv = _from_lane_blocks(conv_s)
        xc = cv - jnp.mean(cv, axis=-1, keepdims=True)
        y = xc * lax.rsqrt(jnp.mean(xc * xc, axis=-1, keepdims=True) + EPS) * lg_ref[...] + lb_ref[...]
        rows = _row_ids(i, rt)
        a = jnp.where(rows >= PAD_ROWS, y * _sigmoid(y), 0.0)
        o_ref[:, 0:cd] = a.astype(BF16)
        for gi, window in enumerate(POOL_WINDOWS):
            ls = slice(gi * pg, (gi + 1) * pg)
            v = pext[:, ls]
            tm = _trailing_sum(v, window)[HALO:] / _pool_counts(rows, window) - v[HALO:]
            p = _dot(tm.astype(BF16), pw_ref[gi]) * ps_ref[:, ls]
            o_ref[:, cd + gi * pg:cd + (gi + 1) * pg] = p.astype(BF16)
        gext[:, 0:HALO, :] = gext[:, rt:rt + HALO, :]
        pext[0:HALO, :] = pext[rt:rt + HALO, :]

    nl, lb = len(_lane_blocks(cd)), min(LANE, cd)
    return pl.pallas_call(
        body, grid=(t // rt,),
        in_specs=[pl.BlockSpec((rt, ein), lambda i: (i, 0)), _resident(conv_w.shape), _resident((1, cd)),
                  _resident((1, cd)), _resident((1, cd)), _resident(pool_w.shape), _resident((1, pd))],
        out_specs=pl.BlockSpec((rt, cd + pd), lambda i: (i, 0)), out_shape=_sds((t, cd + pd), BF16),
        scratch_shapes=[pltpu.VMEM((nl, rt + HALO, lb), F32), pltpu.VMEM((rt + HALO, pd), F32),
                        pltpu.VMEM((nl, rt, lb), F32)],
        compiler_params=_cparams("arbitrary"), name=name)(z, conv_w, conv_b, ln_g, ln_b, pool_w, pool_scale)


def _cp_mid_bwd(z, dcat, conv_w, conv_b, ln_g, ln_b, pool_w, pool_scale, *, after=None, name):
    t, ein = z.shape
    cd = conv_b.shape[1]
    pd = pool_scale.shape[1]
    pg = pd // len(POOL_WINDOWS)
    rt = _row_tile(t, 320)
    ntile = t // rt
    per = rt // CHUNK
    dep_specs, deps = _dep_specs(after)

    def body(*refs):
        (z_ref, zh_ref, dc_ref, cw_ref, cb_ref, lg_ref, lb_ref, pw_ref, ps_ref,
         dz_ref, dcw_ref, dcb_ref, dlg_ref, dlb_ref, dpw_ref, dps_ref, gext, pext, conv_s, dcv, dsp) = refs[len(deps):]
        step = pl.program_id(0)
        tile = ntile - 1 - step

        @pl.when(step == 0)
        def _():
            for ref in (dcw_ref, dcb_ref, dlg_ref, dlb_ref, dpw_ref, dps_ref):
                ref[...] = jnp.zeros_like(ref)
            _to_lane_blocks(dcv, rt, jnp.zeros((HALO, cd), F32))
            dsp[rt:rt + HALO, :] = jnp.zeros((HALO, pd), F32)

        keep = jnp.where(tile > 0, 1.0, 0.0)
        zh = zh_ref[CHUNK - HALO:CHUNK, :]
        _to_lane_blocks(gext, 0, keep * zh[:, 0:cd] * _sigmoid(zh[:, cd:2 * cd]))
        pext[0:HALO, :] = keep * zh[:, 2 * cd:]
        za = z_ref[:, 0:cd]
        sg = _sigmoid(z_ref[:, cd:2 * cd])
        _to_lane_blocks(gext, HALO, za * sg)
        pext[HALO:HALO + rt, :] = z_ref[:, 2 * cd:]
        _conv_rows(gext, cw_ref, lambda k: k + HALO - (CONV_WIDTH - 1), conv_s, per, cd, cb_ref)
        cv = _from_lane_blocks(conv_s)
        xc = cv - jnp.mean(cv, axis=-1, keepdims=True)
        rstd = lax.rsqrt(jnp.mean(xc * xc, axis=-1, keepdims=True) + EPS)
        xh = xc * rstd
        y = xh * lg_ref[...] + lb_ref[...]
        sy = _sigmoid(y)
        rows = _row_ids(tile, rt)
        da = jnp.where(rows >= PAD_ROWS, dc_ref[:, 0:cd], 0.0)
        dy = da * (sy * (1.0 + y * (1.0 - sy)))
        dlg_ref[...] += _rowsum(dy * xh)
        dlb_ref[...] += _rowsum(dy)
        dxh = dy * lg_ref[...]
        dconv = rstd * (dxh - jnp.mean(dxh, axis=-1, keepdims=True) - xh * jnp.mean(dxh * xh, axis=-1, keepdims=True))
        dcb_ref[...] += _rowsum(dconv)
        _to_lane_blocks(dcv, 0, dconv)
        for l, ls in enumerate(_lane_blocks(cd)):
            def acc_rows(rb, accs, l=l):
                base = pl.multiple_of(rb * CHUNK, CHUNK)
                d_blk = dcv[l, pl.ds(base, CHUNK), :]
                out = []
                for k in range(CONV_WIDTH):
                    prod = d_blk * gext[l, pl.ds(base + k + HALO - (CONV_WIDTH - 1), CHUNK), :]
                    part = prod[0:8]
                    for s in range(8, CHUNK, 8):
                        part = part + prod[s:s + 8]
                    out.append(accs[k] + part)
                return tuple(out)

            zero = jnp.zeros((8, ls.stop - ls.start), F32)
            accs = lax.fori_loop(0, per, acc_rows, (zero,) * CONV_WIDTH)
            for k in range(CONV_WIDTH):
                dcw_ref[k:k + 1, ls] += _rowsum(accs[k])
        _conv_rows(dcv, cw_ref, lambda k: CONV_WIDTH - 1 - k, conv_s, per, cd)
        dglu = _from_lane_blocks(conv_s)
        dz_ref[:, 0:cd] = (dglu * sg).astype(BF16)
        dz_ref[:, cd:2 * cd] = (dglu * za * sg * (1.0 - sg)).astype(BF16)
        dcv[:, rt:rt + HALO, :] = dcv[:, 0:HALO, :]
        for gi, window in enumerate(POOL_WINDOWS):
            ls = slice(gi * pg, (gi + 1) * pg)
            v = pext[:, ls]
            cnt = _pool_counts(rows, window)
            tm = (_trailing_sum(v, window)[HALO:] / cnt - v[HALO:]).astype(BF16)
            dp = dc_ref[:, cd + gi * pg:cd + (gi + 1) * pg]
            dps_ref[:, ls] += _rowsum(dp * _dot(tm, pw_ref[gi]))
            dpl = (dp * ps_ref[:, ls]).astype(BF16)
            dpw_ref[gi] += _dot_tn(tm, dpl)
            dtm = _dot_nt(dpl, pw_ref[gi])
            dsp[0:rt, ls] = dtm / cnt
            dpin = _leading_sum(dsp[:, ls], window)[0:rt] - dtm
            dz_ref[:, 2 * cd + gi * pg:2 * cd + (gi + 1) * pg] = dpin.astype(BF16)
        dsp[rt:rt + HALO, :] = dsp[0:HALO, :]

    back = lambda i: (ntile - 1 - i, 0)
    halo_idx = lambda i: (jnp.maximum((ntile - 1 - i) * per - 1, 0), 0)
    const2 = lambda i: (0, 0)
    nl, lb = len(_lane_blocks(cd)), min(LANE, cd)
    return pl.pallas_call(
        body, grid=(ntile,),
        in_specs=dep_specs + [
                  pl.BlockSpec((rt, ein), back), pl.BlockSpec((CHUNK, ein), halo_idx), pl.BlockSpec((rt, cd + pd), back),
                  _resident(conv_w.shape), _resident((1, cd)), _resident((1, cd)), _resident((1, cd)),
                  _resident(pool_w.shape), _resident((1, pd))],
        out_specs=[pl.BlockSpec((rt, ein), back), pl.BlockSpec(conv_w.shape, const2), pl.BlockSpec((1, cd), const2),
                   pl.BlockSpec((1, cd), const2), pl.BlockSpec((1, cd), const2),
                   pl.BlockSpec(pool_w.shape, lambda i: (0, 0, 0)), pl.BlockSpec((1, pd), const2)],
        out_shape=[_sds((t, ein), BF16), _sds(conv_w.shape, F32), _sds((1, cd), F32), _sds((1, cd), F32),
                   _sds((1, cd), F32), _sds(pool_w.shape, F32), _sds((1, pd), F32)],
        scratch_shapes=[pltpu.VMEM((nl, rt + HALO, lb), F32), pltpu.VMEM((rt + HALO, pd), F32), pltpu.VMEM((nl, rt, lb), F32),
                        pltpu.VMEM((nl, rt + HALO, lb), F32), pltpu.VMEM((rt + HALO, pd), F32)],
        compiler_params=_cparams("arbitrary"), name=name)(*deps, z, z, dcat, conv_w, conv_b, ln_g, ln_b, pool_w, pool_scale)


def _log_decay(r_ref, gw_ref, gb_ref, rows):
    gp = _dot(r_ref[...].astype(BF16), gw_ref[...]) + gb_ref[...]
    log_sig = jnp.minimum(gp, 0.0) - jnp.log(1.0 + jnp.exp(-jnp.abs(gp)))
    return gp, jnp.where(rows >= PAD_ROWS, log_sig / GATE_NORM, 0.0)


def _tri(strict):
    r = lax.broadcasted_iota(jnp.int32, (CHUNK, CHUNK), 0)
    c = lax.broadcasted_iota(jnp.int32, (CHUNK, CHUNK), 1)
    return jnp.where(c < r if strict else c <= r, 1.0, 0.0).astype(F32)


def _gla_mid_fwd(z, r, gate_w, gate_b, head_g, *, name):
    t = z.shape[0]
    dk = gate_b.shape[1]
    hv = head_g.shape[1]
    hk = dk // HEADS
    dv = hv * HEADS
    rt = _row_tile(t, 320)
    per = rt // CHUNK
    scale = hk ** -0.5

    def body(z_ref, r_ref, gw_ref, gb_ref, hg_ref, o_ref, st_ref, s_ref, la_ref):
        i = pl.program_id(0)

        @pl.when(i == 0)
        def _():
            s_ref[...] = jnp.zeros_like(s_ref)

        _, la = _log_decay(r_ref, gw_ref, gb_ref, _row_ids(i, rt))
        la_ref[...] = la
        tri = _tri(False)

        def chunk(c, carry):
            rows = pl.ds(pl.multiple_of(c * CHUNK, CHUNK), CHUNK)
            la_c = la_ref[rows, :]
            cum = jnp.dot(tri, la_c, precision=HI, preferred_element_type=F32)
            tot = _rowsum(la_c)
            dec = jnp.exp(tot - cum)
            etot = jnp.exp(tot)
            for hd in range(HEADS):
                ks = slice(hd * hk, (hd + 1) * hk)
                q = z_ref[rows, hd * hk:(hd + 1) * hk] * scale
                kd = z_ref[rows, dk + hd * hk:dk + (hd + 1) * hk] * dec[:, ks]
                v = z_ref[rows, 2 * dk + hd * hv:2 * dk + (hd + 1) * hv]
                g = z_ref[rows, 2 * dk + dv + hd * hv:2 * dk + dv + (hd + 1) * hv]
                s_new = s_ref[hd] * etot[:, ks] + _dot_tn(v.astype(BF16), kd.astype(BF16))
                s_ref[hd] = s_new
                st_ref[c, hd] = s_new
                o = _dot_nt(q.astype(BF16), s_new.astype(BF16))
                on = o * lax.rsqrt(jnp.mean(o * o, axis=-1, keepdims=True) + EPS) * hg_ref[...]
                o_ref[rows, hd * hv:(hd + 1) * hv] = (on * (g * _sigmoid(g))).astype(BF16)
            return carry

        lax.fori_loop(0, per, chunk, 0)

    return pl.pallas_call(
        body, grid=(t // rt,),
        in_specs=[pl.BlockSpec((rt, z.shape[1]), lambda i: (i, 0)), pl.BlockSpec((rt, GATE_PAD), lambda i: (i, 0)),
                  _resident(gate_w.shape), _resident((1, dk)), _resident((1, hv))],
        out_specs=[pl.BlockSpec((rt, dv), lambda i: (i, 0)), pl.BlockSpec((per, HEADS, hv, hk), lambda i: (i, 0, 0, 0))],
        out_shape=[_sds((t, dv), BF16), _sds((t // CHUNK, HEADS, hv, hk), F32)],
        scratch_shapes=[pltpu.VMEM((HEADS, hv, hk), F32), pltpu.VMEM((rt, dk), F32)],
        compiler_params=_cparams("arbitrary"), name=name)(z, r, gate_w, gate_b, head_g)


def _gla_mid_bwd(z, r, dog, states, gate_w, gate_b, head_g, *, after=None, name):
    t = z.shape[0]
    dk = gate_b.shape[1]
    hv = head_g.shape[1]
    hk = dk // HEADS
    dv = hv * HEADS
    rt = _row_tile(t, 320)
    ntile = t // rt
    per = rt // CHUNK
    scale = hk ** -0.5
    dep_specs, deps = _dep_specs(after)

    def body(*refs):
        (z_ref, r_ref, do_ref, st_ref, stp_ref, gw_ref, gb_ref, hg_ref,
         dz_ref, dr_ref, dgw_ref, dgb_ref, dhg_ref, ds_ref, la_ref, dla_ref) = refs[len(deps):]
        step = pl.program_id(0)
        tile = ntile - 1 - step

        @pl.when(step == 0)
        def _():
            ds_ref[...] = jnp.zeros_like(ds_ref)
            dgw_ref[...] = jnp.zeros_like(dgw_ref)
            dgb_ref[...] = jnp.zeros_like(dgb_ref)
            dhg_ref[...] = jnp.zeros_like(dhg_ref)

        rows_id = _row_ids(tile, rt)
        gp, la = _log_decay(r_ref, gw_ref, gb_ref, rows_id)
        la_ref[...] = la
        tri, tri_strict = _tri(False), _tri(True)
        keep = jnp.where(tile > 0, 1.0, 0.0)

        def chunk(cc, carry):
            c = per - 1 - cc
            rows = pl.ds(pl.multiple_of(c * CHUNK, CHUNK), CHUNK)
            la_c = la_ref[rows, :]
            cum = jnp.dot(tri, la_c, precision=HI, preferred_element_type=F32)
            tot = _rowsum(la_c)
            dec = jnp.exp(tot - cum)
            etot = jnp.exp(tot)
            inside = jnp.where(c > 0, 1.0, 0.0)
            for hd in range(HEADS):
                ks = slice(hd * hk, (hd + 1) * hk)
                q = (z_ref[rows, hd * hk:(hd + 1) * hk] * scale).astype(BF16)
                k = z_ref[rows, dk + hd * hk:dk + (hd + 1) * hk]
                kd = k * dec[:, ks]
                v = z_ref[rows, 2 * dk + hd * hv:2 * dk + (hd + 1) * hv].astype(BF16)
                g = z_ref[rows, 2 * dk + dv + hd * hv:2 * dk + dv + (hd + 1) * hv]
                s_now = st_ref[c, hd]
                s_prev = inside * st_ref[jnp.maximum(c - 1, 0), hd] + (1.0 - inside) * keep * stp_ref[0, hd]
                s_b = s_now.astype(BF16)
                o = _dot_nt(q, s_b)
                rstd = lax.rsqrt(jnp.mean(o * o, axis=-1, keepdims=True) + EPS)
                oh = o * rstd
                sg = _sigmoid(g)
                d_og = do_ref[rows, hd * hv:(hd + 1) * hv]
                dz_ref[rows, 2 * dk + dv + hd * hv:2 * dk + dv + (hd + 1) * hv] = (
                    d_og * oh * hg_ref[...] * (sg * (1.0 + g * (1.0 - sg)))).astype(BF16)
                don = d_og * (g * sg)
                dhg_ref[...] += _rowsum(don * oh)
                doh = don * hg_ref[...]
                d_o = (rstd * (doh - oh * jnp.mean(doh * oh, axis=-1, keepdims=True))).astype(BF16)
                dz_ref[rows, hd * hk:(hd + 1) * hk] = (_dot(d_o, s_b) * scale).astype(BF16)
                ds_t = ds_ref[hd] + _dot_tn(d_o, q)
                ds_b = ds_t.astype(BF16)
                dkd = _dot(v, ds_b)
                dz_ref[rows, 2 * dk + hd * hv:2 * dk + (hd + 1) * hv] = _dot_nt(kd.astype(BF16), ds_b).astype(BF16)
                dtot = etot[:, ks] * _rowsum(ds_t * s_prev)
                ds_ref[hd] = ds_t * etot[:, ks]
                dz_ref[rows, dk + hd * hk:dk + (hd + 1) * hk] = (dkd * dec[:, ks]).astype(BF16)
                e = dkd * kd
                dla_ref[rows, ks] = dtot + jnp.dot(tri_strict, e, precision=HI, preferred_element_type=F32)
            return carry

        lax.fori_loop(0, per, chunk, 0, unroll=True)
        dla = jnp.where(rows_id >= PAD_ROWS, dla_ref[...], 0.0)
        dgp = dla * (1.0 / GATE_NORM) * (1.0 - _sigmoid(gp))
        dgb_ref[...] += _rowsum(dgp)
        dgp_b = dgp.astype(BF16)
        dgw_ref[...] += _dot_tn(r_ref[...].astype(BF16), dgp_b)
        dr_ref[...] = _dot_nt(dgp_b, gw_ref[...]).astype(BF16)

    back = lambda i: (ntile - 1 - i, 0)
    const2 = lambda i: (0, 0)
    return pl.pallas_call(
        body, grid=(ntile,),
        in_specs=dep_specs + [
                  pl.BlockSpec((rt, z.shape[1]), back), pl.BlockSpec((rt, GATE_PAD), back), pl.BlockSpec((rt, dv), back),
                  pl.BlockSpec((per, HEADS, hv, hk), lambda i: (ntile - 1 - i, 0, 0, 0)),
                  pl.BlockSpec((1, HEADS, hv, hk), lambda i: (jnp.maximum((ntile - 1 - i) * per - 1, 0), 0, 0, 0)),
                  _resident(gate_w.shape), _resident((1, dk)), _resident((1, hv))],
        out_specs=[pl.BlockSpec((rt, z.shape[1]), back), pl.BlockSpec((rt, GATE_PAD), back),
                   pl.BlockSpec(gate_w.shape, const2), pl.BlockSpec((1, dk), const2), pl.BlockSpec((1, hv), const2)],
        out_shape=[_sds(z.shape, BF16), _sds((t, GATE_PAD), BF16), _sds(gate_w.shape, F32), _sds((1, dk), F32),
                   _sds((1, hv), F32)],
        scratch_shapes=[pltpu.VMEM((HEADS, hv, hk), F32), pltpu.VMEM((rt, dk), F32), pltpu.VMEM((rt, dk), F32)],
        compiler_params=_cparams("arbitrary"), name=name)(*deps, z, r, dog, states, states, gate_w, gate_b, head_g)


def _head(h, gain, target, *, name):
    t, d = h.shape
    rt = _row_tile(t, 832)

    def body(h_ref, g_ref, t_ref, dh_ref, loss_ref, dg_ref):
        i = pl.program_id(0)

        @pl.when(i == 0)
        def _():
            loss_ref[...] = jnp.zeros_like(loss_ref)
            dg_ref[...] = jnp.zeros_like(dg_ref)

        hv = h_ref[...]
        rstd = lax.rsqrt(jnp.mean(hv * hv, axis=-1, keepdims=True) + EPS)
        xh = hv * rstd
        err = jnp.where(_row_ids(i, rt) >= CHUNK, xh * g_ref[...] - t_ref[...], 0.0)
        loss_ref[...] += (0.5 / d) * jnp.sum(err * err)
        dy = err * (1.0 / d)
        dg_ref[...] += _rowsum(dy * xh)
        dxh = dy * g_ref[...]
        dh_ref[...] = rstd * (dxh - xh * jnp.mean(dxh * xh, axis=-1, keepdims=True))

    return pl.pallas_call(
        body, grid=(t // rt,),
        in_specs=[pl.BlockSpec((rt, d), lambda i: (i, 0)), _resident((1, d)), pl.BlockSpec((rt, d), lambda i: (i, 0))],
        out_specs=[pl.BlockSpec((rt, d), lambda i: (i, 0)), pl.BlockSpec((8, LANE), lambda i: (0, 0)),
                   pl.BlockSpec((1, d), lambda i: (0, 0))],
        out_shape=[_sds((t, d), F32), _sds((8, LANE), F32), _sds((1, d), F32)],
        compiler_params=_cparams("arbitrary"), name=name)(h, gain, target)


def _adamw_math(w, g, m, v):
    m = ADAM_B1 * m + (1.0 - ADAM_B1) * g
    v = ADAM_B2 * v + (1.0 - ADAM_B2) * (g * g)
    m_hat = m / (1.0 - ADAM_B1 ** ADAM_STEP)
    v_hat = v / (1.0 - ADAM_B2 ** ADAM_STEP)
    return -ADAM_LR * (m_hat / (jnp.sqrt(v_hat) + ADAM_EPS) + ADAM_WD * w), m, v


N_CHIP = N_DEV // 2
BLOCK_ELEMS = 128 * 1024


def _my_slot():
    return 4 * lax.axis_index("x") + 2 * lax.axis_index("y") + lax.axis_index("c")


def _my_chip():
    return 2 * lax.axis_index("x") + lax.axis_index("y")


def _row_block(r, c):
    cap = max(8, BLOCK_ELEMS // (-(-c // LANE) * LANE))
    return max(b for b in range(8, r + 1, 8) if r % b == 0 and b <= max(cap, 8))


def _pair_add(a, landed, *, name):
    _, r, c = a.shape
    rb = _row_block(r, c)

    def body(core_ref, a_ref, l_ref, o_ref):
        o_ref[...] = a_ref[...] + l_ref[...]

    one = pl.BlockSpec((None, rb, c), lambda q, i, core: (q, i, 0))
    grid_spec = pltpu.PrefetchScalarGridSpec(
        num_scalar_prefetch=1, grid=(N_CHIP, r // rb),
        in_specs=[pl.BlockSpec((None, rb, c), lambda q, i, core: (2 * q + core[0], i, 0)), one], out_specs=one)
    return pl.pallas_call(
        body, grid_spec=grid_spec, out_shape=_sds((N_CHIP, r, c), F32),
        compiler_params=_cparams("parallel", "parallel"), name=name)(lax.axis_index("c").reshape(1), a, landed)


def _reduce_adam(parts, w, m, v, *, after=None, name):
    nl, r, c = w.shape
    rb = _row_block(r, c)
    dep_specs, deps = _dep_specs(after)

    def body(*refs):
        refs = refs[len(deps):]
        p_refs = refs[:2 * nl]
        w_ref, m_ref, v_ref, g_out, d_out, m_out, v_out = refs[2 * nl:]
        layer = pl.program_id(0)
        chip = _my_chip()
        for li in range(nl):
            @pl.when(layer == li)
            def _(li=li):
                mine_ref, land_ref = p_refs[2 * li], p_refs[2 * li + 1]
                g = None
                for q in range(N_CHIP):
                    term = jnp.where(chip == q, mine_ref[q], land_ref[q])
                    g = term if g is None else g + term
                g_out[...] = g
                d_out[...], m_out[...], v_out[...] = _adamw_math(w_ref[...], g, m_ref[...], v_ref[...])

    blk = pl.BlockSpec((None, rb, c), lambda l, i: (l, i, 0))
    p_specs = [pl.BlockSpec((N_CHIP, rb, c), lambda l, i, li=li: (0, jnp.where(l == li, i, 0), 0))
               for li in range(nl) for _ in range(2)]
    flat = [p for pair in parts for p in pair]
    return pl.pallas_call(
        body, grid=(nl, r // rb), in_specs=dep_specs + p_specs + [blk, blk, blk], out_specs=[blk] * 4,
        out_shape=[_sds(w.shape, F32)] * 4, compiler_params=_cparams("arbitrary", "arbitrary"),
        name=name)(*deps, *flat, w, m, v)


def _reduce8(own, landed, *, name):
    r, c = own.shape

    def body(own_ref, p_ref, o_ref):
        me = _my_slot()
        g = None
        for dev in range(N_DEV):
            term = jnp.where(me == dev, own_ref[...], p_ref[dev])
            g = term if g is None else g + term
        o_ref[...] = g

    return pl.pallas_call(body, out_shape=_sds((r, c), F32), name=name)(own, landed)


def _adamw_small(w, g, m, v, *, name):
    def body(w_ref, g_ref, m_ref, v_ref, d_out, m_out, v_out):
        d_out[...], m_out[...], v_out[...] = _adamw_math(w_ref[...], g_ref[...], m_ref[...], v_ref[...])

    return pl.pallas_call(body, out_shape=[_sds(w.shape, F32)] * 3, name=name)(w, g, m, v)


_HBM = pl.BlockSpec(memory_space=pltpu.HBM)
_SEM = pl.BlockSpec(memory_space=pltpu.SEMAPHORE)
_DATAFLOW = pltpu.SideEffectType.DATAFLOW_SIDE_EFFECTING


def _plan_to_all(src, land):
    x, y, c = lax.axis_index("x"), lax.axis_index("y"), lax.axis_index("c")
    return [(src, land.at[_my_slot()], (x ^ ((d >> 2) & 1), y ^ ((d >> 1) & 1), c ^ (d & 1))) for d in range(1, N_DEV)]


def _plan_to_sibling(src, land):
    x, y, c = lax.axis_index("x"), lax.axis_index("y"), lax.axis_index("c")
    return [(src.at[2 * q + 1 - c], land.at[q], (x, y, 1 - c)) for q in range(N_CHIP)]


def _plan_to_chips(src, land):
    x, y, c = lax.axis_index("x"), lax.axis_index("y"), lax.axis_index("c")
    peers = [(x ^ (d >> 1), y ^ (d & 1)) for d in range(1, N_CHIP)]
    return [(src.at[2 * px + py], land.at[_my_chip()], (px, py, c)) for px, py in peers]


_PLAN_COPIES = {_plan_to_all: N_DEV - 1, _plan_to_sibling: N_CHIP, _plan_to_chips: N_CHIP - 1}


def _exchange_copies(plan, ins, lands, send, recv):
    per = _PLAN_COPIES[plan]
    copies = []
    for a, (src, land) in enumerate(zip(ins, lands)):
        for i, (s, dst, dev) in enumerate(plan(src, land)):
            copies.append(pltpu.make_async_remote_copy(
                src_ref=s, dst_ref=dst, send_sem=send.at[a * per + i], recv_sem=recv.at[a * per + i],
                device_id=dev, device_id_type=pl.DeviceIdType.MESH))
    return copies


def _place_own(a, dtype, *, after=None, name):
    r, c = a.shape
    rb = _row_block(r, c)
    dep_specs, deps = _dep_specs(after)

    def body(*refs):
        a_ref, o_ref = refs[1 + len(deps):]
        o_ref[...] = a_ref[...].astype(dtype)

    grid_spec = pltpu.PrefetchScalarGridSpec(
        num_scalar_prefetch=1, grid=(r // rb,), in_specs=dep_specs + [pl.BlockSpec((rb, c), lambda i, me: (i, 0))],
        out_specs=pl.BlockSpec((None, rb, c), lambda i, me: (me[0], i, 0)))
    return pl.pallas_call(body, grid_spec=grid_spec, out_shape=_sds((N_DEV, r, c), dtype),
                          compiler_params=_cparams("arbitrary"), name=name)(_my_slot().reshape(1), *deps, a)


def _plan_gather_first(land, _):
    x, y, c = lax.axis_index("x"), lax.axis_index("y"), lax.axis_index("c")
    mine = land.at[_my_slot()]
    return [(mine, mine, (x, y, 1 - c))] + [(mine, mine, (x ^ (d >> 1), y ^ (d & 1), c)) for d in range(1, N_CHIP)]


def _plan_gather_relay(land, _):
    x, y, c = lax.axis_index("x"), lax.axis_index("y"), lax.axis_index("c")
    slots = [land.at[4 * (x ^ (d >> 1)) + 2 * (y ^ (d & 1)) + c] for d in range(1, N_CHIP)]
    return [(s, s, (x, y, 1 - c)) for s in slots]


_PLAN_COPIES[_plan_gather_first] = N_CHIP
_PLAN_COPIES[_plan_gather_relay] = N_CHIP - 1


def _exchange_start(plan, arrs, lands, *, after=None, name):
    if lands is None:
        lands = [lax.empty((N_CHIP,) + a.shape[1:], a.dtype) for a in arrs]
    bufs = list(lands) if arrs is None else list(arrs) + list(lands)
    n, nb = len(lands), len(bufs)
    nsem = n * _PLAN_COPIES[plan]
    dep_specs, deps = _dep_specs(after)

    def body(*refs):
        ins, land_refs = refs[:n], refs[nb - n:nb]
        send, recv = refs[nb + len(deps)], refs[nb + len(deps) + 1]
        for cp in _exchange_copies(plan, ins, land_refs, send, recv):
            cp.start()
        refs[-1][...] = jnp.zeros_like(refs[-1])

    out = pl.pallas_call(
        body, name=name,
        out_shape=(pltpu.SemaphoreType.DMA((nsem,)), pltpu.SemaphoreType.DMA((nsem,)),
                   *[pltpu.HBM(a.shape, a.dtype) for a in bufs], _sds((8, LANE), F32)),
        in_specs=[_HBM] * nb + dep_specs,
        out_specs=(_SEM, _SEM, *([_HBM] * nb), pl.BlockSpec(memory_space=pltpu.VMEM)),
        input_output_aliases={i: 2 + i for i in range(nb)},
        compiler_params=pltpu.CompilerParams(has_side_effects=_DATAFLOW),
    )(*[pltpu.with_memory_space_constraint(a, pltpu.HBM) for a in bufs], *deps)
    return (plan, n, out[0], out[1], list(out[2:2 + nb])), out[-1]


def _exchange_wait(state, after, *, name):
    plan, n, send_sem, recv_sem, bufs = state
    nb = len(bufs)
    after = list(after) if isinstance(after, (list, tuple)) else [after]

    def body(*refs):
        ins, land_refs, send, recv = refs[:n], refs[nb - n:nb], refs[nb], refs[nb + 1]
        for cp in _exchange_copies(plan, ins, land_refs, send, recv):
            cp.wait_send()
            cp.wait_recv()

    out = pl.pallas_call(
        body, name=name, out_shape=[pltpu.HBM(a.shape, a.dtype) for a in bufs],
        in_specs=[_HBM] * nb + [_SEM, _SEM] + [pl.BlockSpec(memory_space=pl.ANY)] * len(after), out_specs=[_HBM] * nb,
        input_output_aliases={i: i for i in range(nb)},
        compiler_params=pltpu.CompilerParams(has_side_effects=_DATAFLOW),
    )(*bufs, send_sem, recv_sem, *after)
    return list(out[:n]), list(out[nb - n:])


def _dep_specs(after):
    return ([], []) if after is None else ([pl.BlockSpec(memory_space=pl.ANY)], [after])


def _pack(arrays):
    flat = jnp.concatenate([a.reshape(-1) for a in arrays])
    size = flat.shape[0]
    padded = -(-size // (8 * LANE)) * (8 * LANE)
    return jnp.pad(flat, (0, padded - size)).reshape(padded // LANE, LANE)


def _unpack(packed, shapes):
    flat = packed.reshape(-1)
    out, pos = [], 0
    for shp in shapes:
        size = 1
        for s in shp:
            size *= s
        out.append(flat[pos:pos + size].reshape(shp))
        pos += size
    return out


def _undo_column_split(g):
    return jnp.transpose(g, (1, 0, 2)).reshape(g.shape[1], N_DEV * g.shape[2])


def _column_split(a):
    r, c = a.shape
    return jnp.transpose(a.reshape(r, N_DEV, c // N_DEV), (1, 0, 2))


class _WholeWeights:
    def __init__(self, groups):
        self.groups = groups
        self.grads = {}

    def fetch(self, group, after):
        return self.groups[group]

    def emit(self, group, grads):
        self.grads.update(grads)
        return None

    def poll(self, after):
        return None


def _local_step(x, target, replicated, src):
    d = x.shape[1]
    mix_g, ffn_g = replicated["mix_g"], replicated["ffn_g"]
    h0 = jnp.concatenate([jnp.zeros((CHUNK, d), F32), x], axis=0)
    tgt = jnp.concatenate([jnp.zeros((CHUNK, d), F32), target], axis=0)
    cp = src.fetch("cp", [h0, tgt])
    h0 = lax.dynamic_update_slice(h0, cp["meta"], (PAD_ROWS, 0))
    cp_mid = (cp["conv_w"], replicated["conv_b"], replicated["ln_g"], replicated["ln_b"], replicated["pool_w"],
              replicated["pool_scale"])

    z0, u0 = _linear_fwd(h0, cp["cp_w_in"], gain=mix_g[0:1], name="cp_in")
    cat = _cp_mid_fwd(z0, *cp_mid, name="cp_mid")
    h1 = _linear_fwd(cat, cp["cp_w_out"], res=h0, name="cp_out")
    ffn0 = src.fetch("ffn0", h1)
    h2, uf0, rf0 = _ffn_fwd(h1, ffn_g[0:1], ffn0["w1"], ffn0["w2"], name="ffn0")
    gla = src.fetch("gla", h2)
    gla_mid = (gla["gate_w"], gla["gate_b"], gla["head_g"])
    z1, u1 = _linear_fwd(h2, gla["gla_w_qkvg"], gain=mix_g[1:2], name="gla_in")
    r1 = _linear_fwd(u1, gla["gla_w_r"], name="gla_in_r")
    og, states = _gla_mid_fwd(z1, r1, *gla_mid, name="gla_mid")
    h3 = _linear_fwd(og, gla["gla_w_out"], res=h2, name="gla_out")
    ffn1 = src.fetch("ffn1", h3)
    h4, uf1, rf1 = _ffn_fwd(h3, ffn_g[1:2], ffn1["w1"], ffn1["w2"], name="ffn1")
    dh4, loss, d_final_g = _head(h4, replicated["final_g"], tgt, name="head")

    dh3, dhh1, dob1, dffn_g1 = _ffn_bwd_x(h3, dh4, ffn_g[1:2], rf1, ffn1["w1"], ffn1["w2"], name="ffn1_bwd_x")
    dw1_1, dw2_1 = _ffn_bwd_w(uf1, dhh1, rf1, dob1, name="ffn1_bwd_w")
    sent = src.emit("ffn1", dict(w1=dw1_1, w2=dw2_1))
    dog = _linear_bwd_x([dh3], [gla["gla_w_out"]], after=sent, name="gla_out_dx")
    d_gla_w_out = _linear_bwd_w(og, dh3, name="gla_out_dw")
    sent = src.poll(d_gla_w_out)
    dz1, dr1, d_gate_w, d_gate_b, d_head_g = _gla_mid_bwd(z1, r1, dog, states, *gla_mid, after=sent, name="gla_mid_bwd")
    dh2, dmix_g1 = _linear_bwd_x([dz1, dr1], [gla["gla_w_qkvg"], gla["gla_w_r"]],
                                 norm=(h2, mix_g[1:2], dh3), name="gla_in_dx")
    d_gla_w_qkvg = _linear_bwd_w(u1, dz1, name="gla_in_dw")
    d_gla_w_r = _linear_bwd_w(u1, dr1, name="gla_in_r_dw")
    sent = src.emit("gla", dict(gla_w_qkvg=d_gla_w_qkvg, gla_w_r=d_gla_w_r, gla_w_out=d_gla_w_out))
    dh1, dhh0, dob0, dffn_g0 = _ffn_bwd_x(h1, dh2, ffn_g[0:1], rf0, ffn0["w1"], ffn0["w2"], after=sent, name="ffn0_bwd_x")
    dw1_0, dw2_0 = _ffn_bwd_w(uf0, dhh0, rf0, dob0, name="ffn0_bwd_w")
    src.poll(dw1_0)
    sent = src.emit("ffn0", dict(w1=dw1_0, w2=dw2_0))
    dcat = _linear_bwd_x([dh1], [cp["cp_w_out"]], after=sent, name="cp_out_dx")
    d_cp_w_out = _linear_bwd_w(cat, dh1, name="cp_out_dw")
    src.emit("cp_out", dict(cp_w_out=d_cp_w_out))
    sent = src.poll(d_cp_w_out)
    dz0, d_conv_w, d_conv_b, d_ln_g, d_ln_b, d_pool_w, d_pool_scale = _cp_mid_bwd(z0, dcat, *cp_mid, after=sent,
                                                                                 name="cp_mid_bwd")
    dh0, dmix_g0 = _linear_bwd_x([dz0], [cp["cp_w_in"]], norm=(h0, mix_g[0:1], dh1), name="cp_in_dx")
    d_cp_w_in = _linear_bwd_w(u0, dz0, name="cp_in_dw")

    small = dict(
        mix_g=jnp.concatenate([dmix_g0, dmix_g1]), ffn_g=jnp.concatenate([dffn_g0, dffn_g1]), conv_b=d_conv_b, ln_g=d_ln_g,
        ln_b=d_ln_b, pool_w=d_pool_w, pool_scale=d_pool_scale, final_g=d_final_g, meta=dh0[PAD_ROWS:CHUNK], conv_w=d_conv_w,
        gate_w=d_gate_w, gate_b=d_gate_b, head_g=d_head_g)
    src.emit("cp", dict(cp_w_in=d_cp_w_in, small=small))
    return loss, dh0[CHUNK:], small


_REPLICATED = ("mix_norm_g", "ffn_norm_g", "cp_conv_b", "cp_ln_g", "cp_ln_b", "cp_pool_w", "cp_pool_scale", "final_norm_g")
_SMALL_SHARDED = ("meta_tokens", "cp_conv_w", "gla_gate_w2", "gla_gate_b", "gla_head_g")
_LARGE = ("ffn_w1", "ffn_w2", "cp_w_in", "cp_w_out", "gla_w_in", "gla_w_out")
_NAMES = ("meta_tokens", "mix_norm_g", "ffn_norm_g", "ffn_w1", "ffn_w2", "cp_w_in", "cp_conv_w", "cp_conv_b", "cp_ln_g",
          "cp_ln_b", "cp_pool_w", "cp_pool_scale", "cp_w_out", "gla_w_in", "gla_gate_w2", "gla_gate_b", "gla_head_g",
          "gla_w_out", "final_norm_g")
_SMALL_GRADS = ("mix_g", "ffn_g", "conv_b", "ln_g", "ln_b", "pool_w", "pool_scale", "final_g", "meta", "conv_w", "gate_w",
                "gate_b", "head_g")
_GROUPS = ("cp", "ffn0", "gla", "ffn1")


class _Exchanges:
    def __init__(self, w, d):
        self.d = d
        self.small_shards = [w[n] for n in _SMALL_SHARDED]
        shards = dict(
            cp=[(w["cp_w_in"][0], BF16), (w["cp_w_out"][0], BF16), (_pack(self.small_shards), F32)],
            ffn0=[(w["ffn_w1"][0], BF16), (w["ffn_w2"][0], BF16)],
            gla=[(w["gla_w_in"][0], BF16), (w["gla_w_out"][0], BF16)],
            ffn1=[(w["ffn_w1"][1], BF16), (w["ffn_w2"][1], BF16)])
        self.gathers = {}
        self.to_sibling, self.to_chips = [], {}
        token = None
        for group in _GROUPS:
            lands = [_place_own(a, dtype, after=token, name=f"place_w_{group}_{k}")
                     for k, (a, dtype) in enumerate(shards[group])]
            self.gathers[group], token = _exchange_start(_plan_gather_first, None, lands, after=token,
                                                         name=f"start_w_{group}")
        self.token = token

    def fetch(self, group, after):
        d = self.d
        after = (list(after) if isinstance(after, (list, tuple)) else [after]) + [self.token]
        _, lands = _exchange_wait(self.gathers[group], after, name=f"wait_w_{group}")
        relay, token = _exchange_start(_plan_gather_relay, None, lands, name=f"relay_w_{group}")
        _, got = _exchange_wait(relay, token, name=f"wait_relay_w_{group}")
        if group in ("ffn0", "ffn1"):
            return dict(w1=got[0], w2=got[1])
        if group == "gla":
            qkvg = 3 * d
            w_in = _undo_column_split(got[0])
            return dict(gla_w_qkvg=w_in[:, :qkvg], gla_w_r=jnp.pad(w_in[:, qkvg:], ((0, 0), (0, GATE_PAD - GATE_RANK))),
                        gla_w_out=got[1].reshape(d, d), gate_w=self.gate_w, gate_b=self.gate_b, head_g=self.head_g)
        shapes = [s.shape for s in self.small_shards]
        parts = [_unpack(got[2][dev], shapes) for dev in range(N_DEV)]
        meta, conv_w, gate_w, self.gate_b, self.head_g = [
            jnp.concatenate([parts[dev][k] for dev in range(N_DEV)], axis=-1) for k in range(len(shapes))]
        self.gate_w = jnp.pad(gate_w[0], ((0, GATE_PAD - GATE_RANK), (0, 0))).astype(BF16)
        return dict(cp_w_in=_undo_column_split(got[0]), cp_w_out=got[1].reshape(d, d), meta=meta,
                    conv_w=jnp.pad(conv_w[0], ((0, 1), (0, 0))))

    def emit(self, group, g):
        d = self.d
        if group in ("ffn0", "ffn1"):
            arrs = [g["w1"], g["w2"]]
        elif group == "gla":
            w_in = jnp.concatenate([g["gla_w_qkvg"], g["gla_w_r"][:, :GATE_RANK]], axis=1)
            arrs = [_column_split(w_in), g["gla_w_out"].reshape(N_DEV, d // N_DEV, d)]
        elif group == "cp_out":
            arrs = [g["cp_w_out"].reshape(N_DEV, d // N_DEV, d)]
        else:
            s = dict(g["small"])
            s.update(pool_w=s["pool_w"][None], conv_w=s["conv_w"][None, :CONV_WIDTH], gate_w=s["gate_w"][None, :GATE_RANK])
            self.small_grads = [s[n] for n in _SMALL_GRADS]
            self.small_own = _pack(self.small_grads)
            self.small_sent, self.token = _exchange_start(
                _plan_to_all, [self.small_own], [lax.empty((N_DEV,) + self.small_own.shape, F32)], after=self.token,
                name="start_g_small")
            arrs = [_column_split(g["cp_w_in"])]
        state, self.token = _exchange_start(_plan_to_sibling, arrs, None, after=self.token, name=f"start_g1_{group}")
        self.to_sibling.append((group, state))
        return self.token

    def poll(self, after):
        for group, state in self.to_sibling:
            arrs, landed = _exchange_wait(state, after, name=f"wait_g1_{group}")
            sums = [_pair_add(a, l, name=f"chip_sum_{group}_{k}") for k, (a, l) in enumerate(zip(arrs, landed))]
            self.to_chips[group], self.token = _exchange_start(_plan_to_chips, sums, None, after=self.token,
                                                               name=f"start_g2_{group}")
            after = self.token
        self.to_sibling = []
        return self.token

    def finish(self, w, mom, var):
        out = {}
        self.poll(self.token)
        after = self.token

        def landed(group):
            sums, got = _exchange_wait(self.to_chips[group], after, name=f"wait_g2_{group}")
            return list(zip(sums, got))

        def adam(n, parts, behind=None):
            out[n] = _reduce_adam(parts, w[n], mom[n], var[n], after=behind, name=f"adam_{n}")
            return out[n][0]

        ffn1 = landed("ffn1")
        after = ffn1[0][1]
        gla = landed("gla")
        after = gla[0][1]
        ffn0 = landed("ffn0")
        after = adam("ffn_w1", [ffn0[0], ffn1[0]])
        after = adam("ffn_w2", [ffn0[1], ffn1[1]], after)
        after = adam("gla_w_in", [gla[0]], after)
        after = adam("gla_w_out", [gla[1]], after)
        (self.small_own,), (small_all,) = _exchange_wait(self.small_sent, after, name="wait_g_small")

        names = _REPLICATED + _SMALL_SHARDED
        small_sum = _unpack(_reduce8(self.small_own, small_all, name="sum_small_grads"), [a.shape for a in self.small_grads])
        me = _my_slot()
        grad = {}
        for n, full in zip(names, small_sum):
            if n in _SMALL_SHARDED:
                width = w[n].shape[-1]
                full = lax.dynamic_slice_in_dim(full, me * width, width, axis=full.ndim - 1)
            grad[n] = full
        packed = [_pack([t[n] for n in names]) for t in (w, grad, mom, var)]
        small_new = _adamw_small(*packed, name="adam_small")
        small_out = [_unpack(p, [w[n].shape for n in names]) for p in small_new]
        for k, n in enumerate(names):
            out[n] = (grad[n], small_out[0][k], small_out[1][k], small_out[2][k])

        after = small_new[0]
        cp_out = landed("cp_out")
        after = adam("cp_w_out", [cp_out[0]])
        cp = landed("cp")
        adam("cp_w_in", [cp[0]])
        return out


def kernel(x, meta_tokens, mix_norm_g, ffn_norm_g, ffn_w1, ffn_w2, cp_w_in, cp_conv_w, cp_conv_b, cp_ln_g, cp_ln_b, cp_pool_w, cp_pool_scale, cp_w_out, gla_w_in, gla_gate_w2, gla_gate_b, gla_head_g, gla_w_out, final_norm_g, loss_target, m_meta_tokens, m_mix_norm_g, m_ffn_norm_g, m_ffn_w1, m_ffn_w2, m_cp_w_in, m_cp_conv_w, m_cp_conv_b, m_cp_ln_g, m_cp_ln_b, m_cp_pool_w, m_cp_pool_scale, m_cp_w_out, m_gla_w_in, m_gla_gate_w2, m_gla_gate_b, m_gla_head_g, m_gla_w_out, m_final_norm_g, v_meta_tokens, v_mix_norm_g, v_ffn_norm_g, v_ffn_w1, v_ffn_w2, v_cp_w_in, v_cp_conv_w, v_cp_conv_b, v_cp_ln_g, v_cp_ln_b, v_cp_pool_w, v_cp_pool_scale, v_cp_w_out, v_gla_w_in, v_gla_gate_w2, v_gla_gate_b, v_gla_head_g, v_gla_w_out, v_final_norm_g):
    w = dict(meta_tokens=meta_tokens, mix_norm_g=mix_norm_g, ffn_norm_g=ffn_norm_g, ffn_w1=ffn_w1, ffn_w2=ffn_w2,
             cp_w_in=cp_w_in, cp_conv_w=cp_conv_w, cp_conv_b=cp_conv_b, cp_ln_g=cp_ln_g, cp_ln_b=cp_ln_b,
             cp_pool_w=cp_pool_w, cp_pool_scale=cp_pool_scale, cp_w_out=cp_w_out, gla_w_in=gla_w_in,
             gla_gate_w2=gla_gate_w2, gla_gate_b=gla_gate_b, gla_head_g=gla_head_g, gla_w_out=gla_w_out,
             final_norm_g=final_norm_g.reshape(1, -1))
    mom = dict(meta_tokens=m_meta_tokens, mix_norm_g=m_mix_norm_g, ffn_norm_g=m_ffn_norm_g, ffn_w1=m_ffn_w1, ffn_w2=m_ffn_w2,
               cp_w_in=m_cp_w_in, cp_conv_w=m_cp_conv_w, cp_conv_b=m_cp_conv_b, cp_ln_g=m_cp_ln_g, cp_ln_b=m_cp_ln_b,
               cp_pool_w=m_cp_pool_w, cp_pool_scale=m_cp_pool_scale, cp_w_out=m_cp_w_out, gla_w_in=m_gla_w_in,
               gla_gate_w2=m_gla_gate_w2, gla_gate_b=m_gla_gate_b, gla_head_g=m_gla_head_g, gla_w_out=m_gla_w_out,
               final_norm_g=m_final_norm_g.reshape(1, -1))
    var = dict(meta_tokens=v_meta_tokens, mix_norm_g=v_mix_norm_g, ffn_norm_g=v_ffn_norm_g, ffn_w1=v_ffn_w1, ffn_w2=v_ffn_w2,
               cp_w_in=v_cp_w_in, cp_conv_w=v_cp_conv_w, cp_conv_b=v_cp_conv_b, cp_ln_g=v_cp_ln_g, cp_ln_b=v_cp_ln_b,
               cp_pool_w=v_cp_pool_w, cp_pool_scale=v_cp_pool_scale, cp_w_out=v_cp_w_out, gla_w_in=v_gla_w_in,
               gla_gate_w2=v_gla_gate_w2, gla_gate_b=v_gla_gate_b, gla_head_g=v_gla_head_g, gla_w_out=v_gla_w_out,
               final_norm_g=v_final_norm_g.reshape(1, -1))
    d = x.shape[-1]
    replicated = dict(mix_g=w["mix_norm_g"], ffn_g=w["ffn_norm_g"], conv_b=w["cp_conv_b"], ln_g=w["cp_ln_g"],
                      ln_b=w["cp_ln_b"], pool_w=w["cp_pool_w"][0].astype(BF16), pool_scale=w["cp_pool_scale"],
                      final_g=w["final_norm_g"])
    exchanges = _Exchanges(w, d)
    loss_blk, grad_x, _ = _local_step(x[0], loss_target[0], replicated, exchanges)
    loss = lax.psum(loss_blk[0, 0], ("x", "y", "c"))
    out = exchanges.finish(w, mom, var)

    def leaf(n, k):
        a = out[n][k]
        return a.reshape(-1) if n == "final_norm_g" else a

    return (loss, grad_x[None], *[leaf(n, 0) for n in _NAMES], *[leaf(n, 1) for n in _NAMES],
            *[leaf(n, 2) for n in _NAMES], *[leaf(n, 3) for n in _NAMES])
```

```python
import functools

import jax
import jax.numpy as jnp
from jax import lax
from jax.experimental import pallas as pl
from jax.experimental.pallas import tpu as pltpu

F32, BF16 = jnp.float32, jnp.bfloat16
N_DEV = 8
CHUNK = 64
N_META = 16
PAD_ROWS = CHUNK - N_META
HALO = 32
EPS = 1e-5
CONV_WIDTH = 31
POOL_WINDOWS = (2, 4, 8, 16)
HEADS = 4
GATE_RANK = 16
GATE_NORM = 16.0
GATE_PAD = 128
ADAM_LR, ADAM_B1, ADAM_B2, ADAM_EPS, ADAM_WD, ADAM_STEP = 0.001, 0.9, 0.999, 1e-08, 0.01, 10
V7X_VMEM_LIMIT = 56 * 2 ** 20
LANE = 128
HI = lax.Precision.HIGHEST


def _cparams(*sem):
    return pltpu.CompilerParams(dimension_semantics=sem, vmem_limit_bytes=V7X_VMEM_LIMIT)


def _row_tile(t, cap):
    best = CHUNK
    for r in range(CHUNK, min(t, cap) + 1, CHUNK):
        if t % r == 0:
            best = r
    return best


def _resident(shape):
    return pl.BlockSpec(shape, lambda *_: (0,) * len(shape), pipeline_mode=pl.Buffered(1))


def _dot(a, b):
    return jnp.dot(a, b, preferred_element_type=F32)


def _dot_nt(a, b):
    return lax.dot_general(a, b, (((1,), (1,)), ((), ())), preferred_element_type=F32)


def _dot_tn(a, b):
    return lax.dot_general(a, b, (((0,), (0,)), ((), ())), preferred_element_type=F32)


def _rowsum(a):
    return jnp.sum(a, axis=0, keepdims=True)


def _sigmoid(a):
    return 1.0 / (1.0 + jnp.exp(-a))


def _row_ids(tile, rt):
    return tile * rt + lax.broadcasted_iota(jnp.int32, (rt, 1), 0)


def _sds(shape, dtype):
    return jax.ShapeDtypeStruct(shape, dtype)


def _linear_fwd(x, w, *, gain=None, res=None, out_dtype=F32, name):
    t, k = x.shape
    n = w.shape[1]
    rt = _row_tile(t, 320)

    def body(*refs):
        refs = list(refs)
        x_ref, w_ref = refs[:2]
        pos = 2
        g_ref = r_ref = u_ref = None
        if gain is not None:
            g_ref = refs[pos]
            pos += 1
        if res is not None:
            r_ref = refs[pos]
            pos += 1
        y_ref = refs[pos]
        if gain is not None:
            u_ref = refs[pos + 1]
            xv = x_ref[...]
            u = (xv * lax.rsqrt(jnp.mean(xv * xv, axis=-1, keepdims=True) + EPS) * g_ref[...]).astype(BF16)
            u_ref[...] = u
        else:
            u = x_ref[...]
        y = _dot(u, w_ref[...])
        if res is not None:
            y = y + r_ref[...]
        y_ref[...] = y.astype(y_ref.dtype)

    rows = lambda i: (i, 0)
    in_specs = [pl.BlockSpec((rt, k), rows), _resident((k, n))]
    args = [x, w]
    if gain is not None:
        in_specs.append(_resident((1, k)))
        args.append(gain)
    if res is not None:
        in_specs.append(pl.BlockSpec((rt, n), rows))
        args.append(res)
    out_shape = [_sds((t, n), out_dtype)]
    out_specs = [pl.BlockSpec((rt, n), rows)]
    if gain is not None:
        out_shape.append(_sds((t, k), BF16))
        out_specs.append(pl.BlockSpec((rt, k), rows))
    out = pl.pallas_call(body, grid=(t // rt,), in_specs=in_specs, out_specs=out_specs, out_shape=out_shape,
                         compiler_params=_cparams("parallel"), name=name)(*args)
    return out if gain is not None else out[0]


def _linear_bwd_x(dys, ws, *, norm=None, after=None, name):
    t = dys[0].shape[0]
    k = ws[0].shape[0]
    rt = _row_tile(t, 320)
    nd = len(dys)
    dep_specs, deps = _dep_specs(after)

    def body(*refs):
        refs = list(refs)[len(deps):]
        dy_refs, w_refs = refs[:nd], refs[nd:2 * nd]
        dx = None
        for dy_ref, w_ref in zip(dy_refs, w_refs):
            part = _dot_nt(dy_ref[...].astype(BF16), w_ref[...])
            dx = part if dx is None else dx + part
        if norm is None:
            refs[2 * nd][...] = dx
            return
        h_ref, g_ref, dres_ref, dh_ref, dg_ref = refs[2 * nd:]
        hv = h_ref[...]
        rstd = lax.rsqrt(jnp.mean(hv * hv, axis=-1, keepdims=True) + EPS)
        xh = hv * rstd

        @pl.when(pl.program_id(0) == 0)
        def _():
            dg_ref[...] = jnp.zeros_like(dg_ref)

        dg_ref[...] += _rowsum(dx * xh)
        dxh = dx * g_ref[...]
        dh_ref[...] = dres_ref[...] + rstd * (dxh - xh * jnp.mean(dxh * xh, axis=-1, keepdims=True))

    rows = lambda i: (i, 0)
    in_specs = [pl.BlockSpec((rt, dy.shape[1]), rows) for dy in dys] + [_resident(w.shape) for w in ws]
    args = list(dys) + list(ws)
    out_shape = [_sds((t, k), F32)]
    out_specs = [pl.BlockSpec((rt, k), rows)]
    if norm is not None:
        h, gain, dres = norm
        in_specs += [pl.BlockSpec((rt, k), rows), _resident((1, k)), pl.BlockSpec((rt, k), rows)]
        args += [h, gain, dres]
        out_shape.append(_sds((1, k), F32))
        out_specs.append(pl.BlockSpec((1, k), lambda i: (0, 0)))
    out = pl.pallas_call(body, grid=(t // rt,), in_specs=dep_specs + in_specs, out_specs=out_specs, out_shape=out_shape,
                         compiler_params=_cparams("arbitrary"), name=name)(*deps, *args)
    return out if norm is not None else out[0]


DW_ROWS = 1024


def _linear_bwd_w(x, dy, *, name):
    t, k = x.shape
    n = dy.shape[1]
    nt = max(c for c in (512, 384, 256, LANE) if n % c == 0)

    def body(x_ref, dy_ref, o_ref):
        for c0 in range(0, t, DW_ROWS):
            rows = slice(c0, min(c0 + DW_ROWS, t))
            part = _dot_tn(x_ref[rows, :], dy_ref[rows, :].astype(BF16))
            if c0 == 0:
                o_ref[...] = part
            else:
                o_ref[...] += part

    return pl.pallas_call(
        body, grid=(n // nt,),
        in_specs=[_resident((t, k)), pl.BlockSpec((t, nt), lambda j: (0, j))],
        out_specs=pl.BlockSpec((k, nt), lambda j: (0, j)), out_shape=_sds((k, n), F32),
        compiler_params=_cparams("parallel"), name=name)(x, dy)


def _ffn_fwd(h, gain, w1g, w2g, *, name):
    t, d = h.shape
    f8 = w1g.shape[-1]
    rt = _row_tile(t, 832)

    def body(h_ref, g_ref, w1_ref, w2_ref, o_ref, u_ref, r_ref, acc_ref):
        j = pl.program_id(1)

        @pl.when(j == 0)
        def _():
            hv = h_ref[...]
            u_ref[...] = (hv * lax.rsqrt(jnp.mean(hv * hv, axis=-1, keepdims=True) + EPS) * g_ref[...]).astype(BF16)
            acc_ref[...] = jnp.zeros_like(acc_ref)

        a = jnp.maximum(_dot(u_ref[...], w1_ref[...]), 0.0)
        r_ref[...] = a.astype(BF16)
        acc_ref[...] += _dot((a * a).astype(BF16), w2_ref[...])

        @pl.when(j == N_DEV - 1)
        def _():
            o_ref[...] = h_ref[...] + acc_ref[...]

    return pl.pallas_call(
        body, grid=(t // rt, N_DEV),
        in_specs=[pl.BlockSpec((rt, d), lambda i, j: (i, 0)), _resident((1, d)),
                  pl.BlockSpec((None, d, f8), lambda i, j: (j, 0, 0)),
                  pl.BlockSpec((None, f8, d), lambda i, j: (j, 0, 0))],
        out_specs=[pl.BlockSpec((rt, d), lambda i, j: (i, 0)), pl.BlockSpec((rt, d), lambda i, j: (i, 0)),
                   pl.BlockSpec((rt, f8), lambda i, j: (i, j))],
        out_shape=[_sds((t, d), F32), _sds((t, d), BF16), _sds((t, N_DEV * f8), BF16)],
        scratch_shapes=[pltpu.VMEM((rt, d), F32)],
        compiler_params=_cparams("parallel", "arbitrary"), name=name)(h, gain, w1g, w2g)


def _ffn_bwd_x(h, dout, gain, r, w1g, w2g, *, after=None, name):
    t, d = h.shape
    f8 = w1g.shape[-1]
    rt = _row_tile(t, 832)
    last = N_DEV - 1
    dep_specs, deps = _dep_specs(after)

    def body(*refs):
        h_ref, do_ref, g_ref, r_ref, w1_ref, w2_ref, dh_ref, dhh_ref, dob_ref, dg_ref, du_ref = refs[len(deps):]
        i, j = pl.program_id(0), pl.program_id(1)

        @pl.when(j == 0)
        def _():
            dob_ref[...] = do_ref[...].astype(BF16)
            du_ref[...] = jnp.zeros_like(du_ref)

        dhh = (_dot_nt(dob_ref[...], w2_ref[...]) * (2.0 * r_ref[...].astype(F32))).astype(BF16)
        dhh_ref[...] = dhh
        du_ref[...] += _dot_nt(dhh, w1_ref[...])

        @pl.when(j == last)
        def _():
            @pl.when(i == 0)
            def _():
                dg_ref[...] = jnp.zeros_like(dg_ref)

            hv = h_ref[...]
            rstd = lax.rsqrt(jnp.mean(hv * hv, axis=-1, keepdims=True) + EPS)
            xh = hv * rstd
            du = du_ref[...]
            dg_ref[...] += _rowsum(du * xh)
            dxh = du * g_ref[...]
            dh_ref[...] = do_ref[...] + rstd * (dxh - xh * jnp.mean(dxh * xh, axis=-1, keepdims=True))

    rows = lambda i, j: (i, 0)
    return pl.pallas_call(
        body, grid=(t // rt, N_DEV),
        in_specs=dep_specs + [
                  pl.BlockSpec((rt, d), rows), pl.BlockSpec((rt, d), rows), _resident((1, d)),
                  pl.BlockSpec((rt, f8), lambda i, j: (i, j)),
                  pl.BlockSpec((None, d, f8), lambda i, j: (j, 0, 0)),
                  pl.BlockSpec((None, f8, d), lambda i, j: (j, 0, 0))],
        out_specs=[pl.BlockSpec((rt, d), rows), pl.BlockSpec((rt, f8), lambda i, j: (i, j)), pl.BlockSpec((rt, d), rows),
                   pl.BlockSpec((1, d), lambda i, j: (0, 0))],
        out_shape=[_sds((t, d), F32), _sds((t, N_DEV * f8), BF16), _sds((t, d), BF16), _sds((1, d), F32)],
        scratch_shapes=[pltpu.VMEM((rt, d), F32)],
        compiler_params=_cparams("arbitrary", "arbitrary"), name=name)(*deps, h, dout, gain, r, w1g, w2g)


def _ffn_bwd_w(u, dhh, r, dout_b, *, name):
    t, d = u.shape
    f8 = dhh.shape[1] // N_DEV

    def body(u_ref, dhh_ref, r_ref, dob_ref, dw1_ref, dw2_ref):
        for c0 in range(0, t, DW_ROWS):
            rows = slice(c0, min(c0 + DW_ROWS, t))
            rr = r_ref[rows, :].astype(F32)
            part1 = _dot_tn(u_ref[rows, :], dhh_ref[rows, :])
            part2 = _dot_tn((rr * rr).astype(BF16), dob_ref[rows, :])
            if c0 == 0:
                dw1_ref[...] = part1
                dw2_ref[...] = part2
            else:
                dw1_ref[...] += part1
                dw2_ref[...] += part2

    return pl.pallas_call(
        body, grid=(N_DEV,),
        in_specs=[_resident((t, d)), pl.BlockSpec((t, f8), lambda j: (0, j)), pl.BlockSpec((t, f8), lambda j: (0, j)),
                  _resident((t, d))],
        out_specs=[pl.BlockSpec((None, d, f8), lambda j: (j, 0, 0)), pl.BlockSpec((None, f8, d), lambda j: (j, 0, 0))],
        out_shape=[_sds((N_DEV, d, f8), F32), _sds((N_DEV, f8, d), F32)],
        compiler_params=_cparams("parallel"), name=name)(u, dhh, r, dout_b)


def _lane_blocks(width):
    lb = min(LANE, width)
    return [slice(s, s + lb) for s in range(0, width, lb)]


def _conv_rows(src_ref, w_ref, offset, dst_ref, nblk, width, bias_ref=None):
    def blk(rb, carry):
        base = pl.multiple_of(rb * CHUNK, CHUNK)
        for l, ls in enumerate(_lane_blocks(width)):
            acc = jnp.zeros((CHUNK, ls.stop - ls.start), F32)
            if bias_ref is not None:
                acc = acc + bias_ref[:, ls]
            for k in range(CONV_WIDTH):
                acc = acc + w_ref[k:k + 1, ls] * src_ref[l, pl.ds(base + offset(k), CHUNK), :]
            dst_ref[l, pl.ds(base, CHUNK), :] = acc
        return carry

    lax.fori_loop(0, nblk, blk, 0)


def _to_lane_blocks(ref, row0, value):
    for l, ls in enumerate(_lane_blocks(value.shape[1])):
        ref[l, row0:row0 + value.shape[0], :] = value[:, ls]


def _from_lane_blocks(ref):
    return jnp.concatenate([ref[l] for l in range(ref.shape[0])], axis=1)


def _pool_counts(rows, window):
    return jnp.clip(rows - PAD_ROWS + 1, 1, window).astype(F32)


def _trailing_sum(v, window):
    s, sh = v, 1
    while sh < window:
        s = s + pltpu.roll(s, sh, 0)
        sh *= 2
    return s


def _leading_sum(v, window):
    s, sh, n = v, 1, v.shape[0]
    while sh < window:
        s = s + pltpu.roll(s, n - sh, 0)
        sh *= 2
    return s


def _cp_mid_fwd(z, conv_w, conv_b, ln_g, ln_b, pool_w, pool_scale, *, name):
    t, ein = z.shape
    cd = conv_b.shape[1]
    pd = pool_scale.shape[1]
    pg = pd // len(POOL_WINDOWS)
    rt = _row_tile(t, 320)

    def body(z_ref, cw_ref, cb_ref, lg_ref, lb_ref, pw_ref, ps_ref, o_ref, gext, pext, conv_s):
        i = pl.program_id(0)

        @pl.when(i == 0)
        def _():
            _to_lane_blocks(gext, 0, jnp.zeros((HALO, cd), F32))
            pext[0:HALO, :] = jnp.zeros((HALO, pd), F32)

        _to_lane_blocks(gext, HALO, z_ref[:, 0:cd] * _sigmoid(z_ref[:, cd:2 * cd]))
        pext[HALO:HALO + rt, :] = z_ref[:, 2 * cd:]
        _conv_rows(gext, cw_ref, lambda k: k + HALO - (CONV_WIDTH - 1), conv_s, rt // CHUNK, cd, cb_ref)
        cv = _from_lane_blocks(conv_s)
        xc = cv - jnp.mean(cv, axis=-1, keepdims=True)
        y = xc * lax.rsqrt(jnp.mean(xc * xc, axis=-1, keepdims=True) + EPS) * lg_ref[...] + lb_ref[...]
        rows = _row_ids(i, rt)
        a = jnp.where(rows >= PAD_ROWS, y * _sigmoid(y), 0.0)
        o_ref[:, 0:cd] = a.astype(BF16)
        for gi, window in enumerate(POOL_WINDOWS):
            ls = slice(gi * pg, (gi + 1) * pg)
            v = pext[:, ls]
            tm = _trailing_sum(v, window)[HALO:] / _pool_counts(rows, window) - v[HALO:]
            p = _dot(tm.astype(BF16), pw_ref[gi]) * ps_ref[:, ls]
            o_ref[:, cd + gi * pg:cd + (gi + 1) * pg] = p.astype(BF16)
        gext[:, 0:HALO, :] = gext[:, rt:rt + HALO, :]
        pext[0:HALO, :] = pext[rt:rt + HALO, :]

    nl, lb = len(_lane_blocks(cd)), min(LANE, cd)
    return pl.pallas_call(
        body, grid=(t // rt,),
        in_specs=[pl.BlockSpec((rt, ein), lambda i: (i, 0)), _resident(conv_w.shape), _resident((1, cd)),
                  _resident((1, cd)), _resident((1, cd)), _resident(pool_w.shape), _resident((1, pd))],
        out_specs=pl.BlockSpec((rt, cd + pd), lambda i: (i, 0)), out_shape=_sds((t, cd + pd), BF16),
        scratch_shapes=[pltpu.VMEM((nl, rt + HALO, lb), F32), pltpu.VMEM((rt + HALO, pd), F32),
                        pltpu.VMEM((nl, rt, lb), F32)],
        compiler_params=_cparams("arbitrary"), name=name)(z, conv_w, conv_b, ln_g, ln_b, pool_w, pool_scale)


def _cp_mid_bwd(z, dcat, conv_w, conv_b, ln_g, ln_b, pool_w, pool_scale, *, after=None, name):
    t, ein = z.shape
    cd = conv_b.shape[1]
    pd = pool_scale.shape[1]
    pg = pd // len(POOL_WINDOWS)
    rt = _row_tile(t, 320)
    ntile = t // rt
    per = rt // CHUNK
    dep_specs, deps = _dep_specs(after)

    def body(*refs):
        (z_ref, zh_ref, dc_ref, cw_ref, cb_ref, lg_ref, lb_ref, pw_ref, ps_ref,
         dz_ref, dcw_ref, dcb_ref, dlg_ref, dlb_ref, dpw_ref, dps_ref, gext, pext, conv_s, dcv, dsp) = refs[len(deps):]
        step = pl.program_id(0)
        tile = ntile - 1 - step

        @pl.when(step == 0)
        def _():
            for ref in (dcw_ref, dcb_ref, dlg_ref, dlb_ref, dpw_ref, dps_ref):
                ref[...] = jnp.zeros_like(ref)
            _to_lane_blocks(dcv, rt, jnp.zeros((HALO, cd), F32))
            dsp[rt:rt + HALO, :] = jnp.zeros((HALO, pd), F32)

        keep = jnp.where(tile > 0, 1.0, 0.0)
        zh = zh_ref[CHUNK - HALO:CHUNK, :]
        _to_lane_blocks(gext, 0, keep * zh[:, 0:cd] * _sigmoid(zh[:, cd:2 * cd]))
        pext[0:HALO, :] = keep * zh[:, 2 * cd:]
        za = z_ref[:, 0:cd]
        sg = _sigmoid(z_ref[:, cd:2 * cd])
        _to_lane_blocks(gext, HALO, za * sg)
        pext[HALO:HALO + rt, :] = z_ref[:, 2 * cd:]
        _conv_rows(gext, cw_ref, lambda k: k + HALO - (CONV_WIDTH - 1), conv_s, per, cd, cb_ref)
        cv = _from_lane_blocks(conv_s)
        xc = cv - jnp.mean(cv, axis=-1, keepdims=True)
        rstd = lax.rsqrt(jnp.mean(xc * xc, axis=-1, keepdims=True) + EPS)
        xh = xc * rstd
        y = xh * lg_ref[...] + lb_ref[...]
        sy = _sigmoid(y)
        rows = _row_ids(tile, rt)
        da = jnp.where(rows >= PAD_ROWS, dc_ref[:, 0:cd], 0.0)
        dy = da * (sy * (1.0 + y * (1.0 - sy)))
        dlg_ref[...] += _rowsum(dy * xh)
        dlb_ref[...] += _rowsum(dy)
        dxh = dy * lg_ref[...]
        dconv = rstd * (dxh - jnp.mean(dxh, axis=-1, keepdims=True) - xh * jnp.mean(dxh * xh, axis=-1, keepdims=True))
        dcb_ref[...] += _rowsum(dconv)
        _to_lane_blocks(dcv, 0, dconv)
        for l, ls in enumerate(_lane_blocks(cd)):
            def acc_rows(rb, accs, l=l):
                base = pl.multiple_of(rb * CHUNK, CHUNK)
                d_blk = dcv[l, pl.ds(base, CHUNK), :]
                out = []
                for k in range(CONV_WIDTH):
                    prod = d_blk * gext[l, pl.ds(base + k + HALO - (CONV_WIDTH - 1), CHUNK), :]
                    part = prod[0:8]
                    for s in range(8, CHUNK, 8):
                        part = part + prod[s:s + 8]
                    out.append(accs[k] + part)
                return tuple(out)

            zero = jnp.zeros((8, ls.stop - ls.start), F32)
            accs = lax.fori_loop(0, per, acc_rows, (zero,) * CONV_WIDTH)
            for k in range(CONV_WIDTH):
                dcw_ref[k:k + 1, ls] += _rowsum(accs[k])
        _conv_rows(dcv, cw_ref, lambda k: CONV_WIDTH - 1 - k, conv_s, per, cd)
        dglu = _from_lane_blocks(conv_s)
        dz_ref[:, 0:cd] = (dglu * sg).astype(BF16)
        dz_ref[:, cd:2 * cd] = (dglu * za * sg * (1.0 - sg)).astype(BF16)
        dcv[:, rt:rt + HALO, :] = dcv[:, 0:HALO, :]
        for gi, window in enumerate(POOL_WINDOWS):
            ls = slice(gi * pg, (gi + 1) * pg)
            v = pext[:, ls]
            cnt = _pool_counts(rows, window)
            tm = (_trailing_sum(v, window)[HALO:] / cnt - v[HALO:]).astype(BF16)
            dp = dc_ref[:, cd + gi * pg:cd + (gi + 1) * pg]
            dps_ref[:, ls] += _rowsum(dp * _dot(tm, pw_ref[gi]))
            dpl = (dp * ps_ref[:, ls]).astype(BF16)
            dpw_ref[gi] += _dot_tn(tm, dpl)
            dtm = _dot_nt(dpl, pw_ref[gi])
            dsp[0:rt, ls] = dtm / cnt
            dpin = _leading_sum(dsp[:, ls], window)[0:rt] - dtm
            dz_ref[:, 2 * cd + gi * pg:2 * cd + (gi + 1) * pg] = dpin.astype(BF16)
        dsp[rt:rt + HALO, :] = dsp[0:HALO, :]

    back = lambda i: (ntile - 1 - i, 0)
    halo_idx = lambda i: (jnp.maximum((ntile - 1 - i) * per - 1, 0), 0)
    const2 = lambda i: (0, 0)
    nl, lb = len(_lane_blocks(cd)), min(LANE, cd)
    return pl.pallas_call(
        body, grid=(ntile,),
        in_specs=dep_specs + [
                  pl.BlockSpec((rt, ein), back), pl.BlockSpec((CHUNK, ein), halo_idx), pl.BlockSpec((rt, cd + pd), back),
                  _resident(conv_w.shape), _resident((1, cd)), _resident((1, cd)), _resident((1, cd)),
                  _resident(pool_w.shape), _resident((1, pd))],
        out_specs=[pl.BlockSpec((rt, ein), back), pl.BlockSpec(conv_w.shape, const2), pl.BlockSpec((1, cd), const2),
                   pl.BlockSpec((1, cd), const2), pl.BlockSpec((1, cd), const2),
                   pl.BlockSpec(pool_w.shape, lambda i: (0, 0, 0)), pl.BlockSpec((1, pd), const2)],
        out_shape=[_sds((t, ein), BF16), _sds(conv_w.shape, F32), _sds((1, cd), F32), _sds((1, cd), F32),
                   _sds((1, cd), F32), _sds(pool_w.shape, F32), _sds((1, pd), F32)],
        scratch_shapes=[pltpu.VMEM((nl, rt + HALO, lb), F32), pltpu.VMEM((rt + HALO, pd), F32), pltpu.VMEM((nl, rt, lb), F32),
                        pltpu.VMEM((nl, rt + HALO, lb), F32), pltpu.VMEM((rt + HALO, pd), F32)],
        compiler_params=_cparams("arbitrary"), name=name)(*deps, z, z, dcat, conv_w, conv_b, ln_g, ln_b, pool_w, pool_scale)


def _log_decay(r_ref, gw_ref, gb_ref, rows):
    gp = _dot(r_ref[...].astype(BF16), gw_ref[...]) + gb_ref[...]
    log_sig = jnp.minimum(gp, 0.0) - jnp.log(1.0 + jnp.exp(-jnp.abs(gp)))
    return gp, jnp.where(rows >= PAD_ROWS, log_sig / GATE_NORM, 0.0)


def _tri(strict):
    r = lax.broadcasted_iota(jnp.int32, (CHUNK, CHUNK), 0)
    c = lax.broadcasted_iota(jnp.int32, (CHUNK, CHUNK), 1)
    return jnp.where(c < r if strict else c <= r, 1.0, 0.0).astype(F32)


def _gla_mid_fwd(z, r, gate_w, gate_b, head_g, *, name):
    t = z.shape[0]
    dk = gate_b.shape[1]
    hv = head_g.shape[1]
    hk = dk // HEADS
    dv = hv * HEADS
    rt = _row_tile(t, 320)
    per = rt // CHUNK
    scale = hk ** -0.5

    def body(z_ref, r_ref, gw_ref, gb_ref, hg_ref, o_ref, st_ref, s_ref, la_ref):
        i = pl.program_id(0)

        @pl.when(i == 0)
        def _():
            s_ref[...] = jnp.zeros_like(s_ref)

        _, la = _log_decay(r_ref, gw_ref, gb_ref, _row_ids(i, rt))
        la_ref[...] = la
        tri = _tri(False)

        def chunk(c, carry):
            rows = pl.ds(pl.multiple_of(c * CHUNK, CHUNK), CHUNK)
            la_c = la_ref[rows, :]
            cum = jnp.dot(tri, la_c, precision=HI, preferred_element_type=F32)
            tot = _rowsum(la_c)
            dec = jnp.exp(tot - cum)
            etot = jnp.exp(tot)
            for hd in range(HEADS):
                ks = slice(hd * hk, (hd + 1) * hk)
                q = z_ref[rows, hd * hk:(hd + 1) * hk] * scale
                kd = z_ref[rows, dk + hd * hk:dk + (hd + 1) * hk] * dec[:, ks]
                v = z_ref[rows, 2 * dk + hd * hv:2 * dk + (hd + 1) * hv]
                g = z_ref[rows, 2 * dk + dv + hd * hv:2 * dk + dv + (hd + 1) * hv]
                s_new = s_ref[hd] * etot[:, ks] + _dot_tn(v.astype(BF16), kd.astype(BF16))
                s_ref[hd] = s_new
                st_ref[c, hd] = s_new
                o = _dot_nt(q.astype(BF16), s_new.astype(BF16))
                on = o * lax.rsqrt(jnp.mean(o * o, axis=-1, keepdims=True) + EPS) * hg_ref[...]
                o_ref[rows, hd * hv:(hd + 1) * hv] = (on * (g * _sigmoid(g))).astype(BF16)
            return carry

        lax.fori_loop(0, per, chunk, 0)

    return pl.pallas_call(
        body, grid=(t // rt,),
        in_specs=[pl.BlockSpec((rt, z.shape[1]), lambda i: (i, 0)), pl.BlockSpec((rt, GATE_PAD), lambda i: (i, 0)),
                  _resident(gate_w.shape), _resident((1, dk)), _resident((1, hv))],
        out_specs=[pl.BlockSpec((rt, dv), lambda i: (i, 0)), pl.BlockSpec((per, HEADS, hv, hk), lambda i: (i, 0, 0, 0))],
        out_shape=[_sds((t, dv), BF16), _sds((t // CHUNK, HEADS, hv, hk), F32)],
        scratch_shapes=[pltpu.VMEM((HEADS, hv, hk), F32), pltpu.VMEM((rt, dk), F32)],
        compiler_params=_cparams("arbitrary"), name=name)(z, r, gate_w, gate_b, head_g)


def _gla_mid_bwd(z, r, dog, states, gate_w, gate_b, head_g, *, after=None, name):
    t = z.shape[0]
    dk = gate_b.shape[1]
    hv = head_g.shape[1]
    hk = dk // HEADS
    dv = hv * HEADS
    rt = _row_tile(t, 320)
    ntile = t // rt
    per = rt // CHUNK
    scale = hk ** -0.5
    dep_specs, deps = _dep_specs(after)

    def body(*refs):
        (z_ref, r_ref, do_ref, st_ref, stp_ref, gw_ref, gb_ref, hg_ref,
         dz_ref, dr_ref, dgw_ref, dgb_ref, dhg_ref, ds_ref, la_ref, dla_ref) = refs[len(deps):]
        step = pl.program_id(0)
        tile = ntile - 1 - step

        @pl.when(step == 0)
        def _():
            ds_ref[...] = jnp.zeros_like(ds_ref)
            dgw_ref[...] = jnp.zeros_like(dgw_ref)
            dgb_ref[...] = jnp.zeros_like(dgb_ref)
            dhg_ref[...] = jnp.zeros_like(dhg_ref)

        rows_id = _row_ids(tile, rt)
        gp, la = _log_decay(r_ref, gw_ref, gb_ref, rows_id)
        la_ref[...] = la
        tri, tri_strict = _tri(False), _tri(True)
        keep = jnp.where(tile > 0, 1.0, 0.0)

        def chunk(cc, carry):
            c = per - 1 - cc
            rows = pl.ds(pl.multiple_of(c * CHUNK, CHUNK), CHUNK)
            la_c = la_ref[rows, :]
            cum = jnp.dot(tri, la_c, precision=HI, preferred_element_type=F32)
            tot = _rowsum(la_c)
            dec = jnp.exp(tot - cum)
            etot = jnp.exp(tot)
            inside = jnp.where(c > 0, 1.0, 0.0)
            for hd in range(HEADS):
                ks = slice(hd * hk, (hd + 1) * hk)
                q = (z_ref[rows, hd * hk:(hd + 1) * hk] * scale).astype(BF16)
                k = z_ref[rows, dk + hd * hk:dk + (hd + 1) * hk]
                kd = k * dec[:, ks]
                v = z_ref[rows, 2 * dk + hd * hv:2 * dk + (hd + 1) * hv].astype(BF16)
                g = z_ref[rows, 2 * dk + dv + hd * hv:2 * dk + dv + (hd + 1) * hv]
                s_now = st_ref[c, hd]
                s_prev = inside * st_ref[jnp.maximum(c - 1, 0), hd] + (1.0 - inside) * keep * stp_ref[0, hd]
                s_b = s_now.astype(BF16)
                o = _dot_nt(q, s_b)
                rstd = lax.rsqrt(jnp.mean(o * o, axis=-1, keepdims=True) + EPS)
                oh = o * rstd
                sg = _sigmoid(g)
                d_og = do_ref[rows, hd * hv:(hd + 1) * hv]
                dz_ref[rows, 2 * dk + dv + hd * hv:2 * dk + dv + (hd + 1) * hv] = (
                    d_og * oh * hg_ref[...] * (sg * (1.0 + g * (1.0 - sg)))).astype(BF16)
                don = d_og * (g * sg)
                dhg_ref[...] += _rowsum(don * oh)
                doh = don * hg_ref[...]
                d_o = (rstd * (doh - oh * jnp.mean(doh * oh, axis=-1, keepdims=True))).astype(BF16)
                dz_ref[rows, hd * hk:(hd + 1) * hk] = (_dot(d_o, s_b) * scale).astype(BF16)
                ds_t = ds_ref[hd] + _dot_tn(d_o, q)
                ds_b = ds_t.astype(BF16)
                dkd = _dot(v, ds_b)
                dz_ref[rows, 2 * dk + hd * hv:2 * dk + (hd + 1) * hv] = _dot_nt(kd.astype(BF16), ds_b).astype(BF16)
                dtot = etot[:, ks] * _rowsum(ds_t * s_prev)
                ds_ref[hd] = ds_t * etot[:, ks]
                dz_ref[rows, dk + hd * hk:dk + (hd + 1) * hk] = (dkd * dec[:, ks]).astype(BF16)
                e = dkd * kd
                dla_ref[rows, ks] = dtot + jnp.dot(tri_strict, e, precision=HI, preferred_element_type=F32)
            return carry

        lax.fori_loop(0, per, chunk, 0, unroll=True)
        dla = jnp.where(rows_id >= PAD_ROWS, dla_ref[...], 0.0)
        dgp = dla * (1.0 / GATE_NORM) * (1.0 - _sigmoid(gp))
        dgb_ref[...] += _rowsum(dgp)
        dgp_b = dgp.astype(BF16)
        dgw_ref[...] += _dot_tn(r_ref[...].astype(BF16), dgp_b)
        dr_ref[...] = _dot_nt(dgp_b, gw_ref[...]).astype(BF16)

    back = lambda i: (ntile - 1 - i, 0)
    const2 = lambda i: (0, 0)
    return pl.pallas_call(
        body, grid=(ntile,),
        in_specs=dep_specs + [
                  pl.BlockSpec((rt, z.shape[1]), back), pl.BlockSpec((rt, GATE_PAD), back), pl.BlockSpec((rt, dv), back),
                  pl.BlockSpec((per, HEADS, hv, hk), lambda i: (ntile - 1 - i, 0, 0, 0)),
                  pl.BlockSpec((1, HEADS, hv, hk), lambda i: (jnp.maximum((ntile - 1 - i) * per - 1, 0), 0, 0, 0)),
                  _resident(gate_w.shape), _resident((1, dk)), _resident((1, hv))],
        out_specs=[pl.BlockSpec((rt, z.shape[1]), back), pl.BlockSpec((rt, GATE_PAD), back),
                   pl.BlockSpec(gate_w.shape, const2), pl.BlockSpec((1, dk), const2), pl.BlockSpec((1, hv), const2)],
        out_shape=[_sds(z.shape, BF16), _sds((t, GATE_PAD), BF16), _sds(gate_w.shape, F32), _sds((1, dk), F32),
                   _sds((1, hv), F32)],
        scratch_shapes=[pltpu.VMEM((HEADS, hv, hk), F32), pltpu.VMEM((rt, dk), F32), pltpu.VMEM((rt, dk), F32)],
        compiler_params=_cparams("arbitrary"), name=name)(*deps, z, r, dog, states, states, gate_w, gate_b, head_g)


def _head(h, gain, target, *, name):
    t, d = h.shape
    rt = _row_tile(t, 832)

    def body(h_ref, g_ref, t_ref, dh_ref, loss_ref, dg_ref):
        i = pl.program_id(0)

        @pl.when(i == 0)
        def _():
            loss_ref[...] = jnp.zeros_like(loss_ref)
            dg_ref[...] = jnp.zeros_like(dg_ref)

        hv = h_ref[...]
        rstd = lax.rsqrt(jnp.mean(hv * hv, axis=-1, keepdims=True) + EPS)
        xh = hv * rstd
        err = jnp.where(_row_ids(i, rt) >= CHUNK, xh * g_ref[...] - t_ref[...], 0.0)
        loss_ref[...] += (0.5 / d) * jnp.sum(err * err)
        dy = err * (1.0 / d)
        dg_ref[...] += _rowsum(dy * xh)
        dxh = dy * g_ref[...]
        dh_ref[...] = rstd * (dxh - xh * jnp.mean(dxh * xh, axis=-1, keepdims=True))

    return pl.pallas_call(
        body, grid=(t // rt,),
        in_specs=[pl.BlockSpec((rt, d), lambda i: (i, 0)), _resident((1, d)), pl.BlockSpec((rt, d), lambda i: (i, 0))],
        out_specs=[pl.BlockSpec((rt, d), lambda i: (i, 0)), pl.BlockSpec((8, LANE), lambda i: (0, 0)),
                   pl.BlockSpec((1, d), lambda i: (0, 0))],
        out_shape=[_sds((t, d), F32), _sds((8, LANE), F32), _sds((1, d), F32)],
        compiler_params=_cparams("arbitrary"), name=name)(h, gain, target)


def _adamw_math(w, g, m, v):
    m = ADAM_B1 * m + (1.0 - ADAM_B1) * g
    v = ADAM_B2 * v + (1.0 - ADAM_B2) * (g * g)
    m_hat = m / (1.0 - ADAM_B1 ** ADAM_STEP)
    v_hat = v / (1.0 - ADAM_B2 ** ADAM_STEP)
    return -ADAM_LR * (m_hat / (jnp.sqrt(v_hat) + ADAM_EPS) + ADAM_WD * w), m, v


N_CHIP = N_DEV // 2
BLOCK_ELEMS = 128 * 1024


def _my_slot():
    return 4 * lax.axis_index("x") + 2 * lax.axis_index("y") + lax.axis_index("c")


def _my_chip():
    return 2 * lax.axis_index("x") + lax.axis_index("y")


def _row_block(r, c):
    cap = max(8, BLOCK_ELEMS // (-(-c // LANE) * LANE))
    return max([b for b in range(8, r + 1, 8) if r % b == 0 and b <= cap] or [r])


def _pair_add(a, landed, *, name):
    _, r, c = a.shape
    rb = _row_block(r, c)

    def body(core_ref, a_ref, l_ref, o_ref):
        o_ref[...] = a_ref[...] + l_ref[...]

    one = pl.BlockSpec((None, rb, c), lambda q, i, core: (q, i, 0))
    grid_spec = pltpu.PrefetchScalarGridSpec(
        num_scalar_prefetch=1, grid=(N_CHIP, r // rb),
        in_specs=[pl.BlockSpec((None, rb, c), lambda q, i, core: (2 * q + core[0], i, 0)), one], out_specs=one)
    return pl.pallas_call(
        body, grid_spec=grid_spec, out_shape=_sds((N_CHIP, r, c), F32),
        compiler_params=_cparams("parallel", "parallel"), name=name)(lax.axis_index("c").reshape(1), a, landed)


def _reduce_adam(parts, w, m, v, *, after=None, name):
    nl, r, c = w.shape
    rb = _row_block(r, c)
    dep_specs, deps = _dep_specs(after)

    def body(*refs):
        refs = refs[len(deps):]
        p_refs = refs[:2 * nl]
        w_ref, m_ref, v_ref, g_out, d_out, m_out, v_out = refs[2 * nl:]
        layer = pl.program_id(0)
        chip = _my_chip()
        for li in range(nl):
            @pl.when(layer == li)
            def _(li=li):
                mine_ref, land_ref = p_refs[2 * li], p_refs[2 * li + 1]
                g = None
                for q in range(N_CHIP):
                    term = jnp.where(chip == q, mine_ref[q], land_ref[q])
                    g = term if g is None else g + term
                g_out[...] = g
                d_out[...], m_out[...], v_out[...] = _adamw_math(w_ref[...], g, m_ref[...], v_ref[...])

    blk = pl.BlockSpec((None, rb, c), lambda l, i: (l, i, 0))
    p_specs = [pl.BlockSpec((N_CHIP, rb, c), lambda l, i, li=li: (0, jnp.where(l == li, i, 0), 0))
               for li in range(nl) for _ in range(2)]
    flat = [p for pair in parts for p in pair]
    return pl.pallas_call(
        body, grid=(nl, r // rb), in_specs=dep_specs + p_specs + [blk, blk, blk], out_specs=[blk] * 4,
        out_shape=[_sds(w.shape, F32)] * 4, compiler_params=_cparams("arbitrary", "arbitrary"),
        name=name)(*deps, *flat, w, m, v)


def _adam_small(own, landed, split, w, m, v, *, name):
    n = len(w)

    def body(*refs):
        own_refs, land_refs, w_refs, m_refs, v_refs = (refs[k * n:(k + 1) * n] for k in range(5))
        outs = refs[5 * n:]
        me = _my_slot()
        for k in range(n):
            mine = own_refs[k][me] if split[k] else own_refs[k][...]
            g = None
            for dev in range(N_DEV):
                term = jnp.where(me == dev, mine, land_refs[k][dev])
                g = term if g is None else g + term
            outs[4 * k][...] = g
            outs[4 * k + 1][...], outs[4 * k + 2][...], outs[4 * k + 3][...] = _adamw_math(
                w_refs[k][...], g, m_refs[k][...], v_refs[k][...])

    out = pl.pallas_call(body, out_shape=[_sds(a.shape, F32) for a in w for _ in range(4)],
                         compiler_params=pltpu.CompilerParams(vmem_limit_bytes=V7X_VMEM_LIMIT), name=name)(
        *own, *landed, *w, *m, *v)
    return [tuple(out[4 * k:4 * k + 4]) for k in range(n)]


_HBM = pl.BlockSpec(memory_space=pltpu.HBM)
_SEM = pl.BlockSpec(memory_space=pltpu.SEMAPHORE)
_DATAFLOW = pltpu.SideEffectType.DATAFLOW_SIDE_EFFECTING


def _plan_to_all(src, land):
    x, y, c = lax.axis_index("x"), lax.axis_index("y"), lax.axis_index("c")
    return [(src, land.at[_my_slot()], (x ^ ((d >> 2) & 1), y ^ ((d >> 1) & 1), c ^ (d & 1))) for d in range(1, N_DEV)]


def _plan_to_sibling(src, land):
    x, y, c = lax.axis_index("x"), lax.axis_index("y"), lax.axis_index("c")
    return [(src.at[2 * q + 1 - c], land.at[q], (x, y, 1 - c)) for q in range(N_CHIP)]


def _plan_to_chips(src, land):
    x, y, c = lax.axis_index("x"), lax.axis_index("y"), lax.axis_index("c")
    peers = [(x ^ (d >> 1), y ^ (d & 1)) for d in range(1, N_CHIP)]
    return [(src.at[2 * px + py], land.at[_my_chip()], (px, py, c)) for px, py in peers]


def _plan_split_to_all(src, land):
    x, y, c = lax.axis_index("x"), lax.axis_index("y"), lax.axis_index("c")
    peers = [(x ^ ((d >> 2) & 1), y ^ ((d >> 1) & 1), c ^ (d & 1)) for d in range(1, N_DEV)]
    return [(src.at[4 * px + 2 * py + pc], land.at[_my_slot()], (px, py, pc)) for px, py, pc in peers]


_PLAN_COPIES = {_plan_to_all: N_DEV - 1, _plan_split_to_all: N_DEV - 1, _plan_to_sibling: N_CHIP,
                _plan_to_chips: N_CHIP - 1}


def _plans(plan, n):
    return list(plan) if isinstance(plan, (list, tuple)) else [plan] * n


def _exchange_copies(plan, ins, lands, send, recv):
    copies, sem = [], 0
    for p, src, land in zip(_plans(plan, len(lands)), ins, lands):
        for s, dst, dev in p(src, land):
            copies.append(pltpu.make_async_remote_copy(
                src_ref=s, dst_ref=dst, send_sem=send.at[sem], recv_sem=recv.at[sem],
                device_id=dev, device_id_type=pl.DeviceIdType.MESH))
            sem += 1
    return copies


def _place_own(a, dtype, *, after=None, name):
    r, c = a.shape
    rb = _row_block(r, c)
    dep_specs, deps = _dep_specs(after)

    def body(*refs):
        a_ref, o_ref = refs[1 + len(deps):]
        o_ref[...] = a_ref[...].astype(dtype)

    grid_spec = pltpu.PrefetchScalarGridSpec(
        num_scalar_prefetch=1, grid=(r // rb,), in_specs=dep_specs + [pl.BlockSpec((rb, c), lambda i, me: (i, 0))],
        out_specs=pl.BlockSpec((None, rb, c), lambda i, me: (me[0], i, 0)))
    return pl.pallas_call(body, grid_spec=grid_spec, out_shape=_sds((N_DEV, r, c), dtype),
                          compiler_params=_cparams("arbitrary"), name=name)(_my_slot().reshape(1), *deps, a)


def _plan_gather_first(land, _):
    x, y, c = lax.axis_index("x"), lax.axis_index("y"), lax.axis_index("c")
    mine = land.at[_my_slot()]
    return [(mine, mine, (x, y, 1 - c))] + [(mine, mine, (x ^ (d >> 1), y ^ (d & 1), c)) for d in range(1, N_CHIP)]


def _plan_gather_relay(land, _):
    x, y, c = lax.axis_index("x"), lax.axis_index("y"), lax.axis_index("c")
    slots = [land.at[4 * (x ^ (d >> 1)) + 2 * (y ^ (d & 1)) + c] for d in range(1, N_CHIP)]
    return [(s, s, (x, y, 1 - c)) for s in slots]


_PLAN_COPIES[_plan_gather_first] = N_CHIP
_PLAN_COPIES[_plan_gather_relay] = N_CHIP - 1


def _exchange_start(plan, arrs, lands, *, after=None, name):
    if lands is None:
        lands = [lax.empty((N_CHIP,) + a.shape[1:], a.dtype) for a in arrs]
    bufs = list(lands) if arrs is None else list(arrs) + list(lands)
    n, nb = len(lands), len(bufs)
    nsem = sum(_PLAN_COPIES[p] for p in _plans(plan, n))
    dep_specs, deps = _dep_specs(after)

    def body(*refs):
        ins, land_refs = refs[:n], refs[nb - n:nb]
        send, recv = refs[nb + len(deps)], refs[nb + len(deps) + 1]
        for cp in _exchange_copies(plan, ins, land_refs, send, recv):
            cp.start()
        refs[-1][...] = jnp.zeros_like(refs[-1])

    out = pl.pallas_call(
        body, name=name,
        out_shape=(pltpu.SemaphoreType.DMA((nsem,)), pltpu.SemaphoreType.DMA((nsem,)),
                   *[pltpu.HBM(a.shape, a.dtype) for a in bufs], _sds((8, LANE), F32)),
        in_specs=[_HBM] * nb + dep_specs,
        out_specs=(_SEM, _SEM, *([_HBM] * nb), pl.BlockSpec(memory_space=pltpu.VMEM)),
        input_output_aliases={i: 2 + i for i in range(nb)},
        compiler_params=pltpu.CompilerParams(has_side_effects=_DATAFLOW),
    )(*[pltpu.with_memory_space_constraint(a, pltpu.HBM) for a in bufs], *deps)
    return (plan, n, out[0], out[1], list(out[2:2 + nb])), out[-1]


def _exchange_wait(state, after, *, name):
    plan, n, send_sem, recv_sem, bufs = state
    nb = len(bufs)
    after = list(after) if isinstance(after, (list, tuple)) else [after]

    def body(*refs):
        ins, land_refs, send, recv = refs[:n], refs[nb - n:nb], refs[nb], refs[nb + 1]
        for cp in _exchange_copies(plan, ins, land_refs, send, recv):
            cp.wait_send()
            cp.wait_recv()

    out = pl.pallas_call(
        body, name=name, out_shape=[pltpu.HBM(a.shape, a.dtype) for a in bufs],
        in_specs=[_HBM] * nb + [_SEM, _SEM] + [pl.BlockSpec(memory_space=pl.ANY)] * len(after), out_specs=[_HBM] * nb,
        input_output_aliases={i: i for i in range(nb)},
        compiler_params=pltpu.CompilerParams(has_side_effects=_DATAFLOW),
    )(*bufs, send_sem, recv_sem, *after)
    return list(out[:n]), list(out[nb - n:])


def _dep_specs(after):
    return ([], []) if after is None else ([pl.BlockSpec(memory_space=pl.ANY)], [after])


def _undo_column_split(g):
    return jnp.transpose(g, (1, 0, 2)).reshape(g.shape[1], N_DEV * g.shape[2])


def _column_split(a):
    r, c = a.shape
    return jnp.transpose(a.reshape(r, N_DEV, c // N_DEV), (1, 0, 2))


class _WholeWeights:
    def __init__(self, groups):
        self.groups = groups
        self.grads = {}

    def fetch(self, group, after):
        return self.groups[group]

    def emit(self, group, grads):
        self.grads.update(grads)
        return None

    def poll(self, after):
        return None


def _local_step(x, target, replicated, src):
    d = x.shape[1]
    mix_g, ffn_g = replicated["mix_g"], replicated["ffn_g"]
    h0 = jnp.concatenate([jnp.zeros((CHUNK, d), F32), x], axis=0)
    tgt = jnp.concatenate([jnp.zeros((CHUNK, d), F32), target], axis=0)
    cp = src.fetch("cp", [h0, tgt])
    h0 = lax.dynamic_update_slice(h0, cp["meta"], (PAD_ROWS, 0))
    cp_mid = (cp["conv_w"], replicated["conv_b"], replicated["ln_g"], replicated["ln_b"], replicated["pool_w"],
              replicated["pool_scale"])

    z0, u0 = _linear_fwd(h0, cp["cp_w_in"], gain=mix_g[0:1], name="cp_in")
    cat = _cp_mid_fwd(z0, *cp_mid, name="cp_mid")
    h1 = _linear_fwd(cat, cp["cp_w_out"], res=h0, name="cp_out")
    ffn0 = src.fetch("ffn0", h1)
    h2, uf0, rf0 = _ffn_fwd(h1, ffn_g[0:1], ffn0["w1"], ffn0["w2"], name="ffn0")
    gla = src.fetch("gla", h2)
    gla_mid = (gla["gate_w"], gla["gate_b"], gla["head_g"])
    z1, u1 = _linear_fwd(h2, gla["gla_w_qkvg"], gain=mix_g[1:2], name="gla_in")
    r1 = _linear_fwd(u1, gla["gla_w_r"], name="gla_in_r")
    og, states = _gla_mid_fwd(z1, r1, *gla_mid, name="gla_mid")
    h3 = _linear_fwd(og, gla["gla_w_out"], res=h2, name="gla_out")
    ffn1 = src.fetch("ffn1", h3)
    h4, uf1, rf1 = _ffn_fwd(h3, ffn_g[1:2], ffn1["w1"], ffn1["w2"], name="ffn1")
    dh4, loss, d_final_g = _head(h4, replicated["final_g"], tgt, name="head")

    dh3, dhh1, dob1, dffn_g1 = _ffn_bwd_x(h3, dh4, ffn_g[1:2], rf1, ffn1["w1"], ffn1["w2"], name="ffn1_bwd_x")
    dw1_1, dw2_1 = _ffn_bwd_w(uf1, dhh1, rf1, dob1, name="ffn1_bwd_w")
    sent = src.emit("ffn1", dict(w1=dw1_1, w2=dw2_1))
    dog = _linear_bwd_x([dh3], [gla["gla_w_out"]], after=sent, name="gla_out_dx")
    d_gla_w_out = _linear_bwd_w(og, dh3, name="gla_out_dw")
    sent = src.poll(d_gla_w_out)
    dz1, dr1, d_gate_w, d_gate_b, d_head_g = _gla_mid_bwd(z1, r1, dog, states, *gla_mid, after=sent, name="gla_mid_bwd")
    dh2, dmix_g1 = _linear_bwd_x([dz1, dr1], [gla["gla_w_qkvg"], gla["gla_w_r"]],
                                 norm=(h2, mix_g[1:2], dh3), name="gla_in_dx")
    d_gla_w_qkvg = _linear_bwd_w(u1, dz1, name="gla_in_dw")
    d_gla_w_r = _linear_bwd_w(u1, dr1, name="gla_in_r_dw")
    sent = src.emit("gla", dict(gla_w_qkvg=d_gla_w_qkvg, gla_w_r=d_gla_w_r, gla_w_out=d_gla_w_out))
    dh1, dhh0, dob0, dffn_g0 = _ffn_bwd_x(h1, dh2, ffn_g[0:1], rf0, ffn0["w1"], ffn0["w2"], after=sent, name="ffn0_bwd_x")
    dw1_0, dw2_0 = _ffn_bwd_w(uf0, dhh0, rf0, dob0, name="ffn0_bwd_w")
    src.poll(dw1_0)
    sent = src.emit("ffn0", dict(w1=dw1_0, w2=dw2_0))
    dcat = _linear_bwd_x([dh1], [cp["cp_w_out"]], after=sent, name="cp_out_dx")
    d_cp_w_out = _linear_bwd_w(cat, dh1, name="cp_out_dw")
    src.emit("cp_out", dict(cp_w_out=d_cp_w_out))
    sent = src.poll(d_cp_w_out)
    dz0, d_conv_w, d_conv_b, d_ln_g, d_ln_b, d_pool_w, d_pool_scale = _cp_mid_bwd(z0, dcat, *cp_mid, after=sent,
                                                                                 name="cp_mid_bwd")
    dh0, dmix_g0 = _linear_bwd_x([dz0], [cp["cp_w_in"]], norm=(h0, mix_g[0:1], dh1), name="cp_in_dx")
    d_cp_w_in = _linear_bwd_w(u0, dz0, name="cp_in_dw")

    small = dict(
        mix_g=jnp.concatenate([dmix_g0, dmix_g1]), ffn_g=jnp.concatenate([dffn_g0, dffn_g1]), conv_b=d_conv_b, ln_g=d_ln_g,
        ln_b=d_ln_b, pool_w=d_pool_w, pool_scale=d_pool_scale, final_g=d_final_g, meta=dh0[PAD_ROWS:CHUNK], conv_w=d_conv_w,
        gate_w=d_gate_w, gate_b=d_gate_b, head_g=d_head_g)
    src.emit("cp", dict(cp_w_in=d_cp_w_in, small=small))
    return loss, dh0[CHUNK:], small


_REPLICATED = ("mix_norm_g", "ffn_norm_g", "cp_conv_b", "cp_ln_g", "cp_ln_b", "cp_pool_w", "cp_pool_scale", "final_norm_g")
_SMALL_SHARDED = ("meta_tokens", "cp_conv_w", "gla_gate_w2", "gla_gate_b", "gla_head_g")
_NAMES = ("meta_tokens", "mix_norm_g", "ffn_norm_g", "ffn_w1", "ffn_w2", "cp_w_in", "cp_conv_w", "cp_conv_b", "cp_ln_g",
          "cp_ln_b", "cp_pool_w", "cp_pool_scale", "cp_w_out", "gla_w_in", "gla_gate_w2", "gla_gate_b", "gla_head_g",
          "gla_w_out", "final_norm_g")
_SMALL_GRADS = ("mix_g", "ffn_g", "conv_b", "ln_g", "ln_b", "pool_w", "pool_scale", "final_g", "meta", "conv_w", "gate_w",
                "gate_b", "head_g")
_GROUPS = ("cp", "ffn0", "gla", "ffn1")


class _Exchanges:
    def __init__(self, w, d):
        self.d = d
        small = [w[n].reshape(w[n].shape[-2:]) for n in _SMALL_SHARDED]
        self.small_shard_shapes = [w[n].shape for n in _SMALL_SHARDED]
        shards = dict(
            cp=[(w["cp_w_in"][0], BF16), (w["cp_w_out"][0], BF16)] + [(a, F32) for a in small],
            ffn0=[(w["ffn_w1"][0], BF16), (w["ffn_w2"][0], BF16)],
            gla=[(w["gla_w_in"][0], BF16), (w["gla_w_out"][0], BF16)],
            ffn1=[(w["ffn_w1"][1], BF16), (w["ffn_w2"][1], BF16)])
        self.gathers = {}
        self.to_sibling, self.to_chips = [], {}
        token = None
        for group in _GROUPS:
            lands = [_place_own(a, dtype, after=token, name=f"place_w_{group}_{k}")
                     for k, (a, dtype) in enumerate(shards[group])]
            self.gathers[group], token = _exchange_start(_plan_gather_first, None, lands, after=token,
                                                         name=f"start_w_{group}")
        self.token = token

    def fetch(self, group, after):
        d = self.d
        after = (list(after) if isinstance(after, (list, tuple)) else [after]) + [self.token]
        _, lands = _exchange_wait(self.gathers[group], after, name=f"wait_w_{group}")
        relay, token = _exchange_start(_plan_gather_relay, None, lands, name=f"relay_w_{group}")
        _, got = _exchange_wait(relay, token, name=f"wait_relay_w_{group}")
        if group in ("ffn0", "ffn1"):
            return dict(w1=got[0], w2=got[1])
        if group == "gla":
            qkvg = 3 * d
            w_in = _undo_column_split(got[0])
            return dict(gla_w_qkvg=w_in[:, :qkvg], gla_w_r=jnp.pad(w_in[:, qkvg:], ((0, 0), (0, GATE_PAD - GATE_RANK))),
                        gla_w_out=got[1].reshape(d, d), gate_w=self.gate_w, gate_b=self.gate_b, head_g=self.head_g)
        meta, conv_w, gate_w, self.gate_b, self.head_g = [_undo_column_split(a) for a in got[2:]]
        self.gate_w = jnp.pad(gate_w, ((0, GATE_PAD - GATE_RANK), (0, 0))).astype(BF16)
        return dict(cp_w_in=_undo_column_split(got[0]), cp_w_out=got[1].reshape(d, d), meta=meta,
                    conv_w=jnp.pad(conv_w, ((0, 1), (0, 0))))

    def emit(self, group, g):
        d = self.d
        if group in ("ffn0", "ffn1"):
            arrs = [g["w1"], g["w2"]]
        elif group == "gla":
            w_in = jnp.concatenate([g["gla_w_qkvg"], g["gla_w_r"][:, :GATE_RANK]], axis=1)
            arrs = [_column_split(w_in), g["gla_w_out"].reshape(N_DEV, d // N_DEV, d)]
        elif group == "cp_out":
            arrs = [g["cp_w_out"].reshape(N_DEV, d // N_DEV, d)]
        else:
            s = dict(g["small"])
            s.update(pool_w=s["pool_w"][None], conv_w=s["conv_w"][:CONV_WIDTH], gate_w=s["gate_w"][:GATE_RANK])
            own = [s[n] for n in _SMALL_GRADS[:len(_REPLICATED)]]
            own += [_column_split(s[n]).reshape((N_DEV,) + shape)
                    for n, shape in zip(_SMALL_GRADS[len(_REPLICATED):], self.small_shard_shapes)]
            plans = [_plan_to_all] * len(_REPLICATED) + [_plan_split_to_all] * len(_SMALL_SHARDED)
            lands = [lax.empty((N_DEV,) + a.shape, F32) for a in own[:len(_REPLICATED)]]
            lands += [lax.empty(a.shape, F32) for a in own[len(_REPLICATED):]]
            self.small_sent, self.token = _exchange_start(plans, own, lands, after=self.token, name="start_g_small")
            arrs = [_column_split(g["cp_w_in"])]
        state, self.token = _exchange_start(_plan_to_sibling, arrs, None, after=self.token, name=f"start_g1_{group}")
        self.to_sibling.append((group, state))
        return self.token

    def poll(self, after):
        for group, state in self.to_sibling:
            arrs, landed = _exchange_wait(state, after, name=f"wait_g1_{group}")
            sums = [_pair_add(a, l, name=f"chip_sum_{group}_{k}") for k, (a, l) in enumerate(zip(arrs, landed))]
            self.to_chips[group], self.token = _exchange_start(_plan_to_chips, sums, None, after=self.token,
                                                               name=f"start_g2_{group}")
            after = self.token
        self.to_sibling = []
        return self.token

    def finish(self, w, mom, var):
        out = {}
        self.poll(self.token)
        after = self.token

        def landed(group):
            sums, got = _exchange_wait(self.to_chips[group], after, name=f"wait_g2_{group}")
            return list(zip(sums, got))

        def adam(n, parts, behind=None):
            out[n] = _reduce_adam(parts, w[n], mom[n], var[n], after=behind, name=f"adam_{n}")
            return out[n][0]

        ffn1 = landed("ffn1")
        after = ffn1[0][1]
        gla = landed("gla")
        after = gla[0][1]
        ffn0 = landed("ffn0")
        after = adam("ffn_w1", [ffn0[0], ffn1[0]])
        after = adam("ffn_w2", [ffn0[1], ffn1[1]], after)
        after = adam("gla_w_in", [gla[0]], after)
        after = adam("gla_w_out", [gla[1]], after)
        small_own, small_landed = _exchange_wait(self.small_sent, after, name="wait_g_small")
        names = _REPLICATED + _SMALL_SHARDED
        split = [False] * len(_REPLICATED) + [True] * len(_SMALL_SHARDED)
        small_new = _adam_small(small_own, small_landed, split, [w[n] for n in names], [mom[n] for n in names],
                                [var[n] for n in names], name="adam_small")
        out.update(zip(names, small_new))

        after = small_new[0][0]
        cp_out = landed("cp_out")
        after = adam("cp_w_out", [cp_out[0]])
        cp = landed("cp")
        adam("cp_w_in", [cp[0]])
        return out


def kernel(x, meta_tokens, mix_norm_g, ffn_norm_g, ffn_w1, ffn_w2, cp_w_in, cp_conv_w, cp_conv_b, cp_ln_g, cp_ln_b, cp_pool_w, cp_pool_scale, cp_w_out, gla_w_in, gla_gate_w2, gla_gate_b, gla_head_g, gla_w_out, final_norm_g, loss_target, m_meta_tokens, m_mix_norm_g, m_ffn_norm_g, m_ffn_w1, m_ffn_w2, m_cp_w_in, m_cp_conv_w, m_cp_conv_b, m_cp_ln_g, m_cp_ln_b, m_cp_pool_w, m_cp_pool_scale, m_cp_w_out, m_gla_w_in, m_gla_gate_w2, m_gla_gate_b, m_gla_head_g, m_gla_w_out, m_final_norm_g, v_meta_tokens, v_mix_norm_g, v_ffn_norm_g, v_ffn_w1, v_ffn_w2, v_cp_w_in, v_cp_conv_w, v_cp_conv_b, v_cp_ln_g, v_cp_ln_b, v_cp_pool_w, v_cp_pool_scale, v_cp_w_out, v_gla_w_in, v_gla_gate_w2, v_gla_gate_b, v_gla_head_g, v_gla_w_out, v_final_norm_g):
    w = dict(meta_tokens=meta_tokens, mix_norm_g=mix_norm_g, ffn_norm_g=ffn_norm_g, ffn_w1=ffn_w1, ffn_w2=ffn_w2,
             cp_w_in=cp_w_in, cp_conv_w=cp_conv_w, cp_conv_b=cp_conv_b, cp_ln_g=cp_ln_g, cp_ln_b=cp_ln_b,
             cp_pool_w=cp_pool_w, cp_pool_scale=cp_pool_scale, cp_w_out=cp_w_out, gla_w_in=gla_w_in,
             gla_gate_w2=gla_gate_w2, gla_gate_b=gla_gate_b, gla_head_g=gla_head_g, gla_w_out=gla_w_out,
             final_norm_g=final_norm_g.reshape(1, -1))
    mom = dict(meta_tokens=m_meta_tokens, mix_norm_g=m_mix_norm_g, ffn_norm_g=m_ffn_norm_g, ffn_w1=m_ffn_w1, ffn_w2=m_ffn_w2,
               cp_w_in=m_cp_w_in, cp_conv_w=m_cp_conv_w, cp_conv_b=m_cp_conv_b, cp_ln_g=m_cp_ln_g, cp_ln_b=m_cp_ln_b,
               cp_pool_w=m_cp_pool_w, cp_pool_scale=m_cp_pool_scale, cp_w_out=m_cp_w_out, gla_w_in=m_gla_w_in,
               gla_gate_w2=m_gla_gate_w2, gla_gate_b=m_gla_gate_b, gla_head_g=m_gla_head_g, gla_w_out=m_gla_w_out,
               final_norm_g=m_final_norm_g.reshape(1, -1))
    var = dict(meta_tokens=v_meta_tokens, mix_norm_g=v_mix_norm_g, ffn_norm_g=v_ffn_norm_g, ffn_w1=v_ffn_w1, ffn_w2=v_ffn_w2,
               cp_w_in=v_cp_w_in, cp_conv_w=v_cp_conv_w, cp_conv_b=v_cp_conv_b, cp_ln_g=v_cp_ln_g, cp_ln_b=v_cp_ln_b,
               cp_pool_w=v_cp_pool_w, cp_pool_scale=v_cp_pool_scale, cp_w_out=v_cp_w_out, gla_w_in=v_gla_w_in,
               gla_gate_w2=v_gla_gate_w2, gla_gate_b=v_gla_gate_b, gla_head_g=v_gla_head_g, gla_w_out=v_gla_w_out,
               final_norm_g=v_final_norm_g.reshape(1, -1))
    d = x.shape[-1]
    replicated = dict(mix_g=w["mix_norm_g"], ffn_g=w["ffn_norm_g"], conv_b=w["cp_conv_b"], ln_g=w["cp_ln_g"],
                      ln_b=w["cp_ln_b"], pool_w=w["cp_pool_w"][0].astype(BF16), pool_scale=w["cp_pool_scale"],
                      final_g=w["final_norm_g"])
    exchanges = _Exchanges(w, d)
    loss_blk, grad_x, _ = _local_step(x[0], loss_target[0], replicated, exchanges)
    loss = lax.psum(loss_blk[0, 0], ("x", "y", "c"))
    out = exchanges.finish(w, mom, var)

    def leaf(n, k):
        a = out[n][k]
        return a.reshape(-1) if n == "final_norm_g" else a

    return (loss, grad_x[None], *[leaf(n, 0) for n in _NAMES], *[leaf(n, 1) for n in _NAMES],
            *[leaf(n, 2) for n in _NAMES], *[leaf(n, 3) for n in _NAMES])
```

```python
import functools

import jax
import jax.numpy as jnp
from jax import lax
from jax.experimental import pallas as pl
from jax.experimental.pallas import tpu as pltpu

F32, BF16 = jnp.float32, jnp.bfloat16
N_DEV = 8
CHUNK = 64
N_META = 16
PAD_ROWS = CHUNK - N_META
HALO = 32
EPS = 1e-5
CONV_WIDTH = 31
POOL_WINDOWS = (2, 4, 8, 16)
HEADS = 4
GATE_RANK = 16
GATE_NORM = 16.0
GATE_PAD = 128
ADAM_LR, ADAM_B1, ADAM_B2, ADAM_EPS, ADAM_WD, ADAM_STEP = 0.001, 0.9, 0.999, 1e-08, 0.01, 10
V7X_VMEM_LIMIT = 56 * 2 ** 20
LANE = 128
HI = lax.Precision.HIGHEST


def _cparams(*sem):
    return pltpu.CompilerParams(dimension_semantics=sem, vmem_limit_bytes=V7X_VMEM_LIMIT)


def _row_tile(t, cap):
    best = CHUNK
    for r in range(CHUNK, min(t, cap) + 1, CHUNK):
        if t % r == 0:
            best = r
    return best


def _resident(shape):
    return pl.BlockSpec(shape, lambda *_: (0,) * len(shape), pipeline_mode=pl.Buffered(1))


def _dot(a, b):
    return jnp.dot(a, b, preferred_element_type=F32)


def _dot_nt(a, b):
    return lax.dot_general(a, b, (((1,), (1,)), ((), ())), preferred_element_type=F32)


def _dot_tn(a, b):
    return lax.dot_general(a, b, (((0,), (0,)), ((), ())), preferred_element_type=F32)


def _rowsum(a):
    return jnp.sum(a, axis=0, keepdims=True)


def _sigmoid(a):
    return 1.0 / (1.0 + jnp.exp(-a))


def _row_ids(tile, rt):
    return tile * rt + lax.broadcasted_iota(jnp.int32, (rt, 1), 0)


def _sds(shape, dtype):
    return jax.ShapeDtypeStruct(shape, dtype)


def _linear_fwd(x, w, *, gain=None, res=None, w_t=False, out_dtype=F32, name):
    t, k = x.shape
    n = w.shape[0] if w_t else w.shape[1]
    rt = _row_tile(t, 320)

    def body(*refs):
        refs = list(refs)
        x_ref, w_ref = refs[:2]
        pos = 2
        g_ref = r_ref = u_ref = None
        if gain is not None:
            g_ref = refs[pos]
            pos += 1
        if res is not None:
            r_ref = refs[pos]
            pos += 1
        y_ref = refs[pos]
        if gain is not None:
            u_ref = refs[pos + 1]
            xv = x_ref[...]
            u = (xv * lax.rsqrt(jnp.mean(xv * xv, axis=-1, keepdims=True) + EPS) * g_ref[...]).astype(BF16)
            u_ref[...] = u
        else:
            u = x_ref[...]
        y = _dot_nt(u, w_ref[...]) if w_t else _dot(u, w_ref[...])
        if res is not None:
            y = y + r_ref[...]
        y_ref[...] = y.astype(y_ref.dtype)

    rows = lambda i: (i, 0)
    in_specs = [pl.BlockSpec((rt, k), rows), _resident(w.shape)]
    args = [x, w]
    if gain is not None:
        in_specs.append(_resident((1, k)))
        args.append(gain)
    if res is not None:
        in_specs.append(pl.BlockSpec((rt, n), rows))
        args.append(res)
    out_shape = [_sds((t, n), out_dtype)]
    out_specs = [pl.BlockSpec((rt, n), rows)]
    if gain is not None:
        out_shape.append(_sds((t, k), BF16))
        out_specs.append(pl.BlockSpec((rt, k), rows))
    out = pl.pallas_call(body, grid=(t // rt,), in_specs=in_specs, out_specs=out_specs, out_shape=out_shape,
                         compiler_params=_cparams("parallel"), name=name)(*args)
    return out if gain is not None else out[0]


def _linear_bwd_x(dys, ws, *, norm=None, w_t=False, after=None, name):
    t = dys[0].shape[0]
    k = ws[0].shape[1] if w_t else ws[0].shape[0]
    rt = _row_tile(t, 320)
    nd = len(dys)
    dep_specs, deps = _dep_specs(after)

    def body(*refs):
        refs = list(refs)[len(deps):]
        dy_refs, w_refs = refs[:nd], refs[nd:2 * nd]
        dx = None
        for dy_ref, w_ref in zip(dy_refs, w_refs):
            dy = dy_ref[...].astype(BF16)
            part = _dot(dy, w_ref[...]) if w_t else _dot_nt(dy, w_ref[...])
            dx = part if dx is None else dx + part
        if norm is None:
            refs[2 * nd][...] = dx
            return
        h_ref, g_ref, dres_ref, dh_ref, dg_ref = refs[2 * nd:]
        hv = h_ref[...]
        rstd = lax.rsqrt(jnp.mean(hv * hv, axis=-1, keepdims=True) + EPS)
        xh = hv * rstd

        @pl.when(pl.program_id(0) == 0)
        def _():
            dg_ref[...] = jnp.zeros_like(dg_ref)

        dg_ref[...] += _rowsum(dx * xh)
        dxh = dx * g_ref[...]
        dh_ref[...] = dres_ref[...] + rstd * (dxh - xh * jnp.mean(dxh * xh, axis=-1, keepdims=True))

    rows = lambda i: (i, 0)
    in_specs = [pl.BlockSpec((rt, dy.shape[1]), rows) for dy in dys] + [_resident(w.shape) for w in ws]
    args = list(dys) + list(ws)
    out_shape = [_sds((t, k), F32)]
    out_specs = [pl.BlockSpec((rt, k), rows)]
    if norm is not None:
        h, gain, dres = norm
        in_specs += [pl.BlockSpec((rt, k), rows), _resident((1, k)), pl.BlockSpec((rt, k), rows)]
        args += [h, gain, dres]
        out_shape.append(_sds((1, k), F32))
        out_specs.append(pl.BlockSpec((1, k), lambda i: (0, 0)))
    out = pl.pallas_call(body, grid=(t // rt,), in_specs=dep_specs + in_specs, out_specs=out_specs, out_shape=out_shape,
                         compiler_params=_cparams("arbitrary"), name=name)(*deps, *args)
    return out if norm is not None else out[0]


DW_ROWS = 1024


def _linear_bwd_w(x, dy, *, name):
    t, k = x.shape
    n = dy.shape[1]
    cut_k = k > n
    width = k if cut_k else n
    blk = max(c for c in (512, 384, 256, LANE) if width % c == 0)

    def body(x_ref, dy_ref, o_ref):
        for c0 in range(0, t, DW_ROWS):
            rows = slice(c0, min(c0 + DW_ROWS, t))
            part = _dot_tn(x_ref[rows, :].astype(BF16), dy_ref[rows, :].astype(BF16))
            if c0 == 0:
                o_ref[...] = part
            else:
                o_ref[...] += part

    if cut_k:
        in_specs = [pl.BlockSpec((t, blk), lambda j: (0, j)), _resident((t, n))]
        out_specs = pl.BlockSpec((blk, n), lambda j: (j, 0))
    else:
        in_specs = [_resident((t, k)), pl.BlockSpec((t, blk), lambda j: (0, j))]
        out_specs = pl.BlockSpec((k, blk), lambda j: (0, j))
    return pl.pallas_call(
        body, grid=(width // blk,), in_specs=in_specs, out_specs=out_specs, out_shape=_sds((k, n), F32),
        compiler_params=_cparams("parallel"), name=name)(x, dy)


def _ffn_fwd(h, gain, w1g, w2g, *, name):
    t, d = h.shape
    f8 = w1g.shape[-1]
    rt = _row_tile(t, 832)

    def body(h_ref, g_ref, w1_ref, w2_ref, o_ref, u_ref, r_ref, acc_ref):
        j = pl.program_id(1)

        @pl.when(j == 0)
        def _():
            hv = h_ref[...]
            u_ref[...] = (hv * lax.rsqrt(jnp.mean(hv * hv, axis=-1, keepdims=True) + EPS) * g_ref[...]).astype(BF16)
            acc_ref[...] = jnp.zeros_like(acc_ref)

        a = jnp.maximum(_dot(u_ref[...], w1_ref[...]), 0.0)
        r_ref[...] = a.astype(BF16)
        acc_ref[...] += _dot((a * a).astype(BF16), w2_ref[...])

        @pl.when(j == N_DEV - 1)
        def _():
            o_ref[...] = h_ref[...] + acc_ref[...]

    return pl.pallas_call(
        body, grid=(t // rt, N_DEV),
        in_specs=[pl.BlockSpec((rt, d), lambda i, j: (i, 0)), _resident((1, d)),
                  pl.BlockSpec((None, d, f8), lambda i, j: (j, 0, 0)),
                  pl.BlockSpec((None, f8, d), lambda i, j: (j, 0, 0))],
        out_specs=[pl.BlockSpec((rt, d), lambda i, j: (i, 0)), pl.BlockSpec((rt, d), lambda i, j: (i, 0)),
                   pl.BlockSpec((rt, f8), lambda i, j: (i, j))],
        out_shape=[_sds((t, d), F32), _sds((t, d), BF16), _sds((t, N_DEV * f8), BF16)],
        scratch_shapes=[pltpu.VMEM((rt, d), F32)],
        compiler_params=_cparams("parallel", "arbitrary"), name=name)(h, gain, w1g, w2g)


def _ffn_bwd_x(h, dout, gain, r, w1g, w2g, *, after=None, name):
    t, d = h.shape
    f8 = w1g.shape[-1]
    rt = _row_tile(t, 832)
    last = N_DEV - 1
    dep_specs, deps = _dep_specs(after)

    def body(*refs):
        h_ref, do_ref, g_ref, r_ref, w1_ref, w2_ref, dh_ref, dhh_ref, dob_ref, dg_ref, du_ref = refs[len(deps):]
        i, j = pl.program_id(0), pl.program_id(1)

        @pl.when(j == 0)
        def _():
            dob_ref[...] = do_ref[...].astype(BF16)
            du_ref[...] = jnp.zeros_like(du_ref)

        dhh = (_dot_nt(dob_ref[...], w2_ref[...]) * (2.0 * r_ref[...].astype(F32))).astype(BF16)
        dhh_ref[...] = dhh
        du_ref[...] += _dot_nt(dhh, w1_ref[...])

        @pl.when(j == last)
        def _():
            @pl.when(i == 0)
            def _():
                dg_ref[...] = jnp.zeros_like(dg_ref)

            hv = h_ref[...]
            rstd = lax.rsqrt(jnp.mean(hv * hv, axis=-1, keepdims=True) + EPS)
            xh = hv * rstd
            du = du_ref[...]
            dg_ref[...] += _rowsum(du * xh)
            dxh = du * g_ref[...]
            dh_ref[...] = do_ref[...] + rstd * (dxh - xh * jnp.mean(dxh * xh, axis=-1, keepdims=True))

    rows = lambda i, j: (i, 0)
    return pl.pallas_call(
        body, grid=(t // rt, N_DEV),
        in_specs=dep_specs + [
                  pl.BlockSpec((rt, d), rows), pl.BlockSpec((rt, d), rows), _resident((1, d)),
                  pl.BlockSpec((rt, f8), lambda i, j: (i, j)),
                  pl.BlockSpec((None, d, f8), lambda i, j: (j, 0, 0)),
                  pl.BlockSpec((None, f8, d), lambda i, j: (j, 0, 0))],
        out_specs=[pl.BlockSpec((rt, d), rows), pl.BlockSpec((rt, f8), lambda i, j: (i, j)), pl.BlockSpec((rt, d), rows),
                   pl.BlockSpec((1, d), lambda i, j: (0, 0))],
        out_shape=[_sds((t, d), F32), _sds((t, N_DEV * f8), BF16), _sds((t, d), BF16), _sds((1, d), F32)],
        scratch_shapes=[pltpu.VMEM((rt, d), F32)],
        compiler_params=_cparams("arbitrary", "arbitrary"), name=name)(*deps, h, dout, gain, r, w1g, w2g)


def _ffn_bwd_w(u, dhh, r, dout_b, *, name):
    t, d = u.shape
    f8 = dhh.shape[1] // N_DEV

    def body(u_ref, dhh_ref, r_ref, dob_ref, dw1_ref, dw2_ref):
        for c0 in range(0, t, DW_ROWS):
            rows = slice(c0, min(c0 + DW_ROWS, t))
            rr = r_ref[rows, :].astype(F32)
            part1 = _dot_tn(u_ref[rows, :], dhh_ref[rows, :])
            part2 = _dot_tn((rr * rr).astype(BF16), dob_ref[rows, :])
            if c0 == 0:
                dw1_ref[...] = part1
                dw2_ref[...] = part2
            else:
                dw1_ref[...] += part1
                dw2_ref[...] += part2

    return pl.pallas_call(
        body, grid=(N_DEV,),
        in_specs=[_resident((t, d)), pl.BlockSpec((t, f8), lambda j: (0, j)), pl.BlockSpec((t, f8), lambda j: (0, j)),
                  _resident((t, d))],
        out_specs=[pl.BlockSpec((None, d, f8), lambda j: (j, 0, 0)), pl.BlockSpec((None, f8, d), lambda j: (j, 0, 0))],
        out_shape=[_sds((N_DEV, d, f8), F32), _sds((N_DEV, f8, d), F32)],
        compiler_params=_cparams("parallel"), name=name)(u, dhh, r, dout_b)


def _lane_blocks(width):
    lb = min(LANE, width)
    return [slice(s, s + lb) for s in range(0, width, lb)]


def _conv_rows(src_ref, w_ref, offset, dst_ref, nblk, width, bias_ref=None):
    def blk(rb, carry):
        base = pl.multiple_of(rb * CHUNK, CHUNK)
        for l, ls in enumerate(_lane_blocks(width)):
            acc = jnp.zeros((CHUNK, ls.stop - ls.start), F32)
            if bias_ref is not None:
                acc = acc + bias_ref[:, ls]
            for k in range(CONV_WIDTH):
                acc = acc + w_ref[k:k + 1, ls] * src_ref[l, pl.ds(base + offset(k), CHUNK), :]
            dst_ref[l, pl.ds(base, CHUNK), :] = acc
        return carry

    lax.fori_loop(0, nblk, blk, 0)


def _to_lane_blocks(ref, row0, value):
    for l, ls in enumerate(_lane_blocks(value.shape[1])):
        ref[l, row0:row0 + value.shape[0], :] = value[:, ls]


def _from_lane_blocks(ref):
    return jnp.concatenate([ref[l] for l in range(ref.shape[0])], axis=1)


def _pool_counts(rows, window):
    return jnp.clip(rows - PAD_ROWS + 1, 1, window).astype(F32)


def _trailing_sum(v, window):
    s, sh = v, 1
    while sh < window:
        s = s + pltpu.roll(s, sh, 0)
        sh *= 2
    return s


def _leading_sum(v, window):
    s, sh, n = v, 1, v.shape[0]
    while sh < window:
        s = s + pltpu.roll(s, n - sh, 0)
        sh *= 2
    return s


def _cp_mid_fwd(z, conv_w, conv_b, ln_g, ln_b, pool_w, pool_scale, *, name):
    t, ein = z.shape
    cd = conv_b.shape[1]
    pd = pool_scale.shape[1]
    pg = pd // len(POOL_WINDOWS)
    rt = _row_tile(t, 320)

    def body(z_ref, cw_ref, cb_ref, lg_ref, lb_ref, pw_ref, ps_ref, o_ref, gext, pext, conv_s):
        i = pl.program_id(0)

        @pl.when(i == 0)
        def _():
            _to_lane_blocks(gext, 0, jnp.zeros((HALO, cd), F32))
            pext[0:HALO, :] = jnp.zeros((HALO, pd), F32)

        _to_lane_blocks(gext, HALO, z_ref[:, 0:cd] * _sigmoid(z_ref[:, cd:2 * cd]))
        pext[HALO:HALO + rt, :] = z_ref[:, 2 * cd:]
        _conv_rows(gext, cw_ref, lambda k: k + HALO - (CONV_WIDTH - 1), conv_s, rt // CHUNK, cd, cb_ref)
        cv = _from_lane_blocks(conv_s)
        xc = cv - jnp.mean(cv, axis=-1, keepdims=True)
        y = xc * lax.rsqrt(jnp.mean(xc * xc, axis=-1, keepdims=True) + EPS) * lg_ref[...] + lb_ref[...]
        rows = _row_ids(i, rt)
        a = jnp.where(rows >= PAD_ROWS, y * _sigmoid(y), 0.0)
        o_ref[:, 0:cd] = a.astype(BF16)
        for gi, window in enumerate(POOL_WINDOWS):
            ls = slice(gi * pg, (gi + 1) * pg)
            v = pext[:, ls]
            tm = _trailing_sum(v, window)[HALO:] / _pool_counts(rows, window) - v[HALO:]
            p = _dot(tm.astype(BF16), pw_ref[gi]) * ps_ref[:, ls]
            o_ref[:, cd + gi * pg:cd + (gi + 1) * pg] = p.astype(BF16)
        gext[:, 0:HALO, :] = gext[:, rt:rt + HALO, :]
        pext[0:HALO, :] = pext[rt:rt + HALO, :]

    nl, lb = len(_lane_blocks(cd)), min(LANE, cd)
    return pl.pallas_call(
        body, grid=(t // rt,),
        in_specs=[pl.BlockSpec((rt, ein), lambda i: (i, 0)), _resident(conv_w.shape), _resident((1, cd)),
                  _resident((1, cd)), _resident((1, cd)), _resident(pool_w.shape), _resident((1, pd))],
        out_specs=pl.BlockSpec((rt, cd + pd), lambda i: (i, 0)), out_shape=_sds((t, cd + pd), BF16),
        scratch_shapes=[pltpu.VMEM((nl, rt + HALO, lb), F32), pltpu.VMEM((rt + HALO, pd), F32),
                        pltpu.VMEM((nl, rt, lb), F32)],
        compiler_params=_cparams("arbitrary"), name=name)(z, conv_w, conv_b, ln_g, ln_b, pool_w, pool_scale)


def _cp_mid_bwd(z, dcat, conv_w, conv_b, ln_g, ln_b, pool_w, pool_scale, *, after=None, name):
    t, ein = z.shape
    cd = conv_b.shape[1]
    pd = pool_scale.shape[1]
    pg = pd // len(POOL_WINDOWS)
    rt = _row_tile(t, 320)
    ntile = t // rt
    per = rt // CHUNK
    dep_specs, deps = _dep_specs(after)

    def body(*refs):
        (z_ref, zh_ref, dc_ref, cw_ref, cb_ref, lg_ref, lb_ref, pw_ref, ps_ref,
         dz_ref, dcw_ref, dcb_ref, dlg_ref, dlb_ref, dpw_ref, dps_ref, gext, pext, conv_s, dcv, dsp) = refs[len(deps):]
        step = pl.program_id(0)
        tile = ntile - 1 - step

        @pl.when(step == 0)
        def _():
            for ref in (dcw_ref, dcb_ref, dlg_ref, dlb_ref, dpw_ref, dps_ref):
                ref[...] = jnp.zeros_like(ref)
            _to_lane_blocks(dcv, rt, jnp.zeros((HALO, cd), F32))
            dsp[rt:rt + HALO, :] = jnp.zeros((HALO, pd), F32)

        keep = jnp.where(tile > 0, 1.0, 0.0)
        zh = zh_ref[CHUNK - HALO:CHUNK, :]
        _to_lane_blocks(gext, 0, keep * zh[:, 0:cd] * _sigmoid(zh[:, cd:2 * cd]))
        pext[0:HALO, :] = keep * zh[:, 2 * cd:]
        za = z_ref[:, 0:cd]
        sg = _sigmoid(z_ref[:, cd:2 * cd])
        _to_lane_blocks(gext, HALO, za * sg)
        pext[HALO:HALO + rt, :] = z_ref[:, 2 * cd:]
        _conv_rows(gext, cw_ref, lambda k: k + HALO - (CONV_WIDTH - 1), conv_s, per, cd, cb_ref)
        cv = _from_lane_blocks(conv_s)
        xc = cv - jnp.mean(cv, axis=-1, keepdims=True)
        rstd = lax.rsqrt(jnp.mean(xc * xc, axis=-1, keepdims=True) + EPS)
        xh = xc * rstd
        y = xh * lg_ref[...] + lb_ref[...]
        sy = _sigmoid(y)
        rows = _row_ids(tile, rt)
        da = jnp.where(rows >= PAD_ROWS, dc_ref[:, 0:cd], 0.0)
        dy = da * (sy * (1.0 + y * (1.0 - sy)))
        dlg_ref[...] += _rowsum(dy * xh)
        dlb_ref[...] += _rowsum(dy)
        dxh = dy * lg_ref[...]
        dconv = rstd * (dxh - jnp.mean(dxh, axis=-1, keepdims=True) - xh * jnp.mean(dxh * xh, axis=-1, keepdims=True))
        dcb_ref[...] += _rowsum(dconv)
        _to_lane_blocks(dcv, 0, dconv)
        for l, ls in enumerate(_lane_blocks(cd)):
            def acc_rows(rb, accs, l=l):
                base = pl.multiple_of(rb * CHUNK, CHUNK)
                d_blk = dcv[l, pl.ds(base, CHUNK), :]
                out = []
                for k in range(CONV_WIDTH):
                    prod = d_blk * gext[l, pl.ds(base + k + HALO - (CONV_WIDTH - 1), CHUNK), :]
                    part = prod[0:8]
                    for s in range(8, CHUNK, 8):
                        part = part + prod[s:s + 8]
                    out.append(accs[k] + part)
                return tuple(out)

            zero = jnp.zeros((8, ls.stop - ls.start), F32)
            accs = lax.fori_loop(0, per, acc_rows, (zero,) * CONV_WIDTH)
            for k in range(CONV_WIDTH):
                dcw_ref[k:k + 1, ls] += _rowsum(accs[k])
        _conv_rows(dcv, cw_ref, lambda k: CONV_WIDTH - 1 - k, conv_s, per, cd)
        dglu = _from_lane_blocks(conv_s)
        dz_ref[:, 0:cd] = (dglu * sg).astype(BF16)
        dz_ref[:, cd:2 * cd] = (dglu * za * sg * (1.0 - sg)).astype(BF16)
        dcv[:, rt:rt + HALO, :] = dcv[:, 0:HALO, :]
        for gi, window in enumerate(POOL_WINDOWS):
            ls = slice(gi * pg, (gi + 1) * pg)
            v = pext[:, ls]
            cnt = _pool_counts(rows, window)
            tm = (_trailing_sum(v, window)[HALO:] / cnt - v[HALO:]).astype(BF16)
            dp = dc_ref[:, cd + gi * pg:cd + (gi + 1) * pg]
            dps_ref[:, ls] += _rowsum(dp * _dot(tm, pw_ref[gi]))
            dpl = (dp * ps_ref[:, ls]).astype(BF16)
            dpw_ref[gi] += _dot_tn(tm, dpl)
            dtm = _dot_nt(dpl, pw_ref[gi])
            dsp[0:rt, ls] = dtm / cnt
            dpin = _leading_sum(dsp[:, ls], window)[0:rt] - dtm
            dz_ref[:, 2 * cd + gi * pg:2 * cd + (gi + 1) * pg] = dpin.astype(BF16)
        dsp[rt:rt + HALO, :] = dsp[0:HALO, :]

    back = lambda i: (ntile - 1 - i, 0)
    halo_idx = lambda i: (jnp.maximum((ntile - 1 - i) * per - 1, 0), 0)
    const2 = lambda i: (0, 0)
    nl, lb = len(_lane_blocks(cd)), min(LANE, cd)
    return pl.pallas_call(
        body, grid=(ntile,),
        in_specs=dep_specs + [
                  pl.BlockSpec((rt, ein), back), pl.BlockSpec((CHUNK, ein), halo_idx), pl.BlockSpec((rt, cd + pd), back),
                  _resident(conv_w.shape), _resident((1, cd)), _resident((1, cd)), _resident((1, cd)),
                  _resident(pool_w.shape), _resident((1, pd))],
        out_specs=[pl.BlockSpec((rt, ein), back), pl.BlockSpec(conv_w.shape, const2), pl.BlockSpec((1, cd), const2),
                   pl.BlockSpec((1, cd), const2), pl.BlockSpec((1, cd), const2),
                   pl.BlockSpec(pool_w.shape, lambda i: (0, 0, 0)), pl.BlockSpec((1, pd), const2)],
        out_shape=[_sds((t, ein), BF16), _sds(conv_w.shape, F32), _sds((1, cd), F32), _sds((1, cd), F32),
                   _sds((1, cd), F32), _sds(pool_w.shape, F32), _sds((1, pd), F32)],
        scratch_shapes=[pltpu.VMEM((nl, rt + HALO, lb), F32), pltpu.VMEM((rt + HALO, pd), F32), pltpu.VMEM((nl, rt, lb), F32),
                        pltpu.VMEM((nl, rt + HALO, lb), F32), pltpu.VMEM((rt + HALO, pd), F32)],
        compiler_params=_cparams("arbitrary"), name=name)(*deps, z, z, dcat, conv_w, conv_b, ln_g, ln_b, pool_w, pool_scale)


def _log_decay(r_ref, gw_ref, gb_ref, rows):
    gp = _dot(r_ref[...].astype(BF16), gw_ref[...]) + gb_ref[...]
    log_sig = jnp.minimum(gp, 0.0) - jnp.log(1.0 + jnp.exp(-jnp.abs(gp)))
    return gp, jnp.where(rows >= PAD_ROWS, log_sig / GATE_NORM, 0.0)


def _tri(strict):
    r = lax.broadcasted_iota(jnp.int32, (CHUNK, CHUNK), 0)
    c = lax.broadcasted_iota(jnp.int32, (CHUNK, CHUNK), 1)
    return jnp.where(c < r if strict else c <= r, 1.0, 0.0).astype(F32)


def _gla_mid_fwd(z, r, gate_w, gate_b, head_g, *, name):
    t = z.shape[0]
    dk = gate_b.shape[1]
    hv = head_g.shape[1]
    hk = dk // HEADS
    dv = hv * HEADS
    rt = _row_tile(t, 320)
    per = rt // CHUNK
    scale = hk ** -0.5

    def body(z_ref, r_ref, gw_ref, gb_ref, hg_ref, o_ref, st_ref, s_ref, la_ref):
        i = pl.program_id(0)

        @pl.when(i == 0)
        def _():
            s_ref[...] = jnp.zeros_like(s_ref)

        _, la = _log_decay(r_ref, gw_ref, gb_ref, _row_ids(i, rt))
        la_ref[...] = la
        tri = _tri(False)

        def chunk(c, carry):
            rows = pl.ds(pl.multiple_of(c * CHUNK, CHUNK), CHUNK)
            la_c = la_ref[rows, :]
            cum = jnp.dot(tri, la_c, precision=HI, preferred_element_type=F32)
            tot = _rowsum(la_c)
            dec = jnp.exp(tot - cum)
            etot = jnp.exp(tot)
            for hd in range(HEADS):
                ks = slice(hd * hk, (hd + 1) * hk)
                q = z_ref[rows, hd * hk:(hd + 1) * hk] * scale
                kd = z_ref[rows, dk + hd * hk:dk + (hd + 1) * hk] * dec[:, ks]
                v = z_ref[rows, 2 * dk + hd * hv:2 * dk + (hd + 1) * hv]
                g = z_ref[rows, 2 * dk + dv + hd * hv:2 * dk + dv + (hd + 1) * hv]
                s_new = s_ref[hd] * etot[:, ks] + _dot_tn(v.astype(BF16), kd.astype(BF16))
                s_ref[hd] = s_new
                st_ref[c, hd] = s_new
                o = _dot_nt(q.astype(BF16), s_new.astype(BF16))
                on = o * lax.rsqrt(jnp.mean(o * o, axis=-1, keepdims=True) + EPS) * hg_ref[...]
                o_ref[rows, hd * hv:(hd + 1) * hv] = (on * (g * _sigmoid(g))).astype(BF16)
            return carry

        lax.fori_loop(0, per, chunk, 0)

    return pl.pallas_call(
        body, grid=(t // rt,),
        in_specs=[pl.BlockSpec((rt, z.shape[1]), lambda i: (i, 0)), pl.BlockSpec((rt, GATE_PAD), lambda i: (i, 0)),
                  _resident(gate_w.shape), _resident((1, dk)), _resident((1, hv))],
        out_specs=[pl.BlockSpec((rt, dv), lambda i: (i, 0)), pl.BlockSpec((per, HEADS, hv, hk), lambda i: (i, 0, 0, 0))],
        out_shape=[_sds((t, dv), BF16), _sds((t // CHUNK, HEADS, hv, hk), F32)],
        scratch_shapes=[pltpu.VMEM((HEADS, hv, hk), F32), pltpu.VMEM((rt, dk), F32)],
        compiler_params=_cparams("arbitrary"), name=name)(z, r, gate_w, gate_b, head_g)


def _gla_mid_bwd(z, r, dog, states, gate_w, gate_b, head_g, *, after=None, name):
    t = z.shape[0]
    dk = gate_b.shape[1]
    hv = head_g.shape[1]
    hk = dk // HEADS
    dv = hv * HEADS
    rt = _row_tile(t, 320)
    ntile = t // rt
    per = rt // CHUNK
    scale = hk ** -0.5
    dep_specs, deps = _dep_specs(after)

    def body(*refs):
        (z_ref, r_ref, do_ref, st_ref, stp_ref, gw_ref, gb_ref, hg_ref,
         dz_ref, dr_ref, dgw_ref, dgb_ref, dhg_ref, ds_ref, la_ref, dla_ref) = refs[len(deps):]
        step = pl.program_id(0)
        tile = ntile - 1 - step

        @pl.when(step == 0)
        def _():
            ds_ref[...] = jnp.zeros_like(ds_ref)
            dgw_ref[...] = jnp.zeros_like(dgw_ref)
            dgb_ref[...] = jnp.zeros_like(dgb_ref)
            dhg_ref[...] = jnp.zeros_like(dhg_ref)

        rows_id = _row_ids(tile, rt)
        gp, la = _log_decay(r_ref, gw_ref, gb_ref, rows_id)
        la_ref[...] = la
        tri, tri_strict = _tri(False), _tri(True)
        keep = jnp.where(tile > 0, 1.0, 0.0)

        def chunk(cc, carry):
            c = per - 1 - cc
            rows = pl.ds(pl.multiple_of(c * CHUNK, CHUNK), CHUNK)
            la_c = la_ref[rows, :]
            cum = jnp.dot(tri, la_c, precision=HI, preferred_element_type=F32)
            tot = _rowsum(la_c)
            dec = jnp.exp(tot - cum)
            etot = jnp.exp(tot)
            inside = jnp.where(c > 0, 1.0, 0.0)
            for hd in range(HEADS):
                ks = slice(hd * hk, (hd + 1) * hk)
                q = (z_ref[rows, hd * hk:(hd + 1) * hk] * scale).astype(BF16)
                k = z_ref[rows, dk + hd * hk:dk + (hd + 1) * hk]
                kd = k * dec[:, ks]
                v = z_ref[rows, 2 * dk + hd * hv:2 * dk + (hd + 1) * hv].astype(BF16)
                g = z_ref[rows, 2 * dk + dv + hd * hv:2 * dk + dv + (hd + 1) * hv]
                s_now = st_ref[c, hd]
                s_prev = inside * st_ref[jnp.maximum(c - 1, 0), hd] + (1.0 - inside) * keep * stp_ref[0, hd]
                s_b = s_now.astype(BF16)
                o = _dot_nt(q, s_b)
                rstd = lax.rsqrt(jnp.mean(o * o, axis=-1, keepdims=True) + EPS)
                oh = o * rstd
                sg = _sigmoid(g)
                d_og = do_ref[rows, hd * hv:(hd + 1) * hv]
                dz_ref[rows, 2 * dk + dv + hd * hv:2 * dk + dv + (hd + 1) * hv] = (
                    d_og * oh * hg_ref[...] * (sg * (1.0 + g * (1.0 - sg)))).astype(BF16)
                don = d_og * (g * sg)
                dhg_ref[...] += _rowsum(don * oh)
                doh = don * hg_ref[...]
                d_o = (rstd * (doh - oh * jnp.mean(doh * oh, axis=-1, keepdims=True))).astype(BF16)
                dz_ref[rows, hd * hk:(hd + 1) * hk] = (_dot(d_o, s_b) * scale).astype(BF16)
                ds_t = ds_ref[hd] + _dot_tn(d_o, q)
                ds_b = ds_t.astype(BF16)
                dkd = _dot(v, ds_b)
                dz_ref[rows, 2 * dk + hd * hv:2 * dk + (hd + 1) * hv] = _dot_nt(kd.astype(BF16), ds_b).astype(BF16)
                dtot = etot[:, ks] * _rowsum(ds_t * s_prev)
                ds_ref[hd] = ds_t * etot[:, ks]
                dz_ref[rows, dk + hd * hk:dk + (hd + 1) * hk] = (dkd * dec[:, ks]).astype(BF16)
                e = dkd * kd
                dla_ref[rows, ks] = dtot + jnp.dot(tri_strict, e, precision=HI, preferred_element_type=F32)
            return carry

        lax.fori_loop(0, per, chunk, 0, unroll=True)
        dla = jnp.where(rows_id >= PAD_ROWS, dla_ref[...], 0.0)
        dgp = dla * (1.0 / GATE_NORM) * (1.0 - _sigmoid(gp))
        dgb_ref[...] += _rowsum(dgp)
        dgp_b = dgp.astype(BF16)
        dgw_ref[...] += _dot_tn(r_ref[...].astype(BF16), dgp_b)
        dr_ref[...] = _dot_nt(dgp_b, gw_ref[...]).astype(BF16)

    back = lambda i: (ntile - 1 - i, 0)
    const2 = lambda i: (0, 0)
    return pl.pallas_call(
        body, grid=(ntile,),
        in_specs=dep_specs + [
                  pl.BlockSpec((rt, z.shape[1]), back), pl.BlockSpec((rt, GATE_PAD), back), pl.BlockSpec((rt, dv), back),
                  pl.BlockSpec((per, HEADS, hv, hk), lambda i: (ntile - 1 - i, 0, 0, 0)),
                  pl.BlockSpec((1, HEADS, hv, hk), lambda i: (jnp.maximum((ntile - 1 - i) * per - 1, 0), 0, 0, 0)),
                  _resident(gate_w.shape), _resident((1, dk)), _resident((1, hv))],
        out_specs=[pl.BlockSpec((rt, z.shape[1]), back), pl.BlockSpec((rt, GATE_PAD), back),
                   pl.BlockSpec(gate_w.shape, const2), pl.BlockSpec((1, dk), const2), pl.BlockSpec((1, hv), const2)],
        out_shape=[_sds(z.shape, BF16), _sds((t, GATE_PAD), BF16), _sds(gate_w.shape, F32), _sds((1, dk), F32),
                   _sds((1, hv), F32)],
        scratch_shapes=[pltpu.VMEM((HEADS, hv, hk), F32), pltpu.VMEM((rt, dk), F32), pltpu.VMEM((rt, dk), F32)],
        compiler_params=_cparams("arbitrary"), name=name)(*deps, z, r, dog, states, states, gate_w, gate_b, head_g)


def _head(h, gain, target, *, name):
    t, d = h.shape
    rt = _row_tile(t, 832)

    def body(h_ref, g_ref, t_ref, dh_ref, loss_ref, dg_ref):
        i = pl.program_id(0)

        @pl.when(i == 0)
        def _():
            loss_ref[...] = jnp.zeros_like(loss_ref)
            dg_ref[...] = jnp.zeros_like(dg_ref)

        hv = h_ref[...]
        rstd = lax.rsqrt(jnp.mean(hv * hv, axis=-1, keepdims=True) + EPS)
        xh = hv * rstd
        err = jnp.where(_row_ids(i, rt) >= CHUNK, xh * g_ref[...] - t_ref[...], 0.0)
        loss_ref[...] += (0.5 / d) * jnp.sum(err * err)
        dy = err * (1.0 / d)
        dg_ref[...] += _rowsum(dy * xh)
        dxh = dy * g_ref[...]
        dh_ref[...] = rstd * (dxh - xh * jnp.mean(dxh * xh, axis=-1, keepdims=True))

    return pl.pallas_call(
        body, grid=(t // rt,),
        in_specs=[pl.BlockSpec((rt, d), lambda i: (i, 0)), _resident((1, d)), pl.BlockSpec((rt, d), lambda i: (i, 0))],
        out_specs=[pl.BlockSpec((rt, d), lambda i: (i, 0)), pl.BlockSpec((8, LANE), lambda i: (0, 0)),
                   pl.BlockSpec((1, d), lambda i: (0, 0))],
        out_shape=[_sds((t, d), F32), _sds((8, LANE), F32), _sds((1, d), F32)],
        compiler_params=_cparams("arbitrary"), name=name)(h, gain, target)


def _adamw_math(w, g, m, v):
    m = ADAM_B1 * m + (1.0 - ADAM_B1) * g
    v = ADAM_B2 * v + (1.0 - ADAM_B2) * (g * g)
    m_hat = m / (1.0 - ADAM_B1 ** ADAM_STEP)
    v_hat = v / (1.0 - ADAM_B2 ** ADAM_STEP)
    return -ADAM_LR * (m_hat / (jnp.sqrt(v_hat) + ADAM_EPS) + ADAM_WD * w), m, v


N_CHIP = N_DEV // 2
BLOCK_ELEMS = 128 * 1024


def _my_slot():
    return 4 * lax.axis_index("x") + 2 * lax.axis_index("y") + lax.axis_index("c")


def _my_chip():
    return 2 * lax.axis_index("x") + lax.axis_index("y")


def _row_block(r, c):
    cap = max(8, BLOCK_ELEMS // (-(-c // LANE) * LANE))
    return max([b for b in range(8, r + 1, 8) if r % b == 0 and b <= cap] or [r])


def _blocks(r, c):
    rb = _row_block(r, c)
    if rb < r or r * c <= BLOCK_ELEMS:
        return rb, c
    return r, max([b for b in (512, 256, LANE) if c % b == 0 and r * b <= BLOCK_ELEMS] or [c])


def _pair_add(a, landed, *, name):
    _, r, c = a.shape
    rb, cb = _blocks(r, c)

    def body(core_ref, a_ref, l_ref, o_ref):
        o_ref[...] = a_ref[...] + l_ref[...]

    one = pl.BlockSpec((None, rb, cb), lambda q, i, j, core: (q, i, j))
    grid_spec = pltpu.PrefetchScalarGridSpec(
        num_scalar_prefetch=1, grid=(N_CHIP, r // rb, c // cb),
        in_specs=[pl.BlockSpec((None, rb, cb), lambda q, i, j, core: (2 * q + core[0], i, j)), one], out_specs=one)
    return pl.pallas_call(
        body, grid_spec=grid_spec, out_shape=_sds((N_CHIP, r, c), F32),
        compiler_params=_cparams("parallel", "parallel", "parallel"), name=name)(lax.axis_index("c").reshape(1), a, landed)


def _reduce_adam(parts, w, m, v, *, after=None, name):
    nl, r, c = w.shape
    rb, cb = _blocks(r, c)
    dep_specs, deps = _dep_specs(after)

    def body(*refs):
        refs = refs[len(deps):]
        p_refs = refs[:2 * nl]
        w_ref, m_ref, v_ref, g_out, d_out, m_out, v_out = refs[2 * nl:]
        layer = pl.program_id(0)
        chip = _my_chip()
        for li in range(nl):
            @pl.when(layer == li)
            def _(li=li):
                mine_ref, land_ref = p_refs[2 * li], p_refs[2 * li + 1]
                g = None
                for q in range(N_CHIP):
                    term = jnp.where(chip == q, mine_ref[q], land_ref[q])
                    g = term if g is None else g + term
                g_out[...] = g
                d_out[...], m_out[...], v_out[...] = _adamw_math(w_ref[...], g, m_ref[...], v_ref[...])

    blk = pl.BlockSpec((None, rb, cb), lambda l, i, j: (l, i, j))
    p_specs = [pl.BlockSpec((N_CHIP, rb, cb),
                            lambda l, i, j, li=li: (0, jnp.where(l == li, i, 0), jnp.where(l == li, j, 0)))
               for li in range(nl) for _ in range(2)]
    flat = [p for pair in parts for p in pair]
    return pl.pallas_call(
        body, grid=(nl, r // rb, c // cb), in_specs=dep_specs + p_specs + [blk, blk, blk], out_specs=[blk] * 4,
        out_shape=[_sds(w.shape, F32)] * 4, compiler_params=_cparams("arbitrary", "arbitrary", "arbitrary"),
        name=name)(*deps, *flat, w, m, v)


def _adam_small(own, landed, split, w, m, v, *, name):
    n = len(w)

    def body(*refs):
        own_refs, land_refs, w_refs, m_refs, v_refs = (refs[k * n:(k + 1) * n] for k in range(5))
        outs = refs[5 * n:]
        me = _my_slot()
        for k in range(n):
            mine = own_refs[k][me] if split[k] else own_refs[k][...]
            g = None
            for dev in range(N_DEV):
                term = jnp.where(me == dev, mine, land_refs[k][dev])
                g = term if g is None else g + term
            outs[4 * k][...] = g
            outs[4 * k + 1][...], outs[4 * k + 2][...], outs[4 * k + 3][...] = _adamw_math(
                w_refs[k][...], g, m_refs[k][...], v_refs[k][...])

    out = pl.pallas_call(body, out_shape=[_sds(a.shape, F32) for a in w for _ in range(4)],
                         compiler_params=pltpu.CompilerParams(vmem_limit_bytes=V7X_VMEM_LIMIT), name=name)(
        *own, *landed, *w, *m, *v)
    return [tuple(out[4 * k:4 * k + 4]) for k in range(n)]


_HBM = pl.BlockSpec(memory_space=pltpu.HBM)
_SEM = pl.BlockSpec(memory_space=pltpu.SEMAPHORE)
_DATAFLOW = pltpu.SideEffectType.DATAFLOW_SIDE_EFFECTING


def _plan_to_all(src, land):
    x, y, c = lax.axis_index("x"), lax.axis_index("y"), lax.axis_index("c")
    return [(src, land.at[_my_slot()], (x ^ ((d >> 2) & 1), y ^ ((d >> 1) & 1), c ^ (d & 1))) for d in range(1, N_DEV)]


def _plan_to_sibling(src, land):
    x, y, c = lax.axis_index("x"), lax.axis_index("y"), lax.axis_index("c")
    return [(src.at[2 * q + 1 - c], land.at[q], (x, y, 1 - c)) for q in range(N_CHIP)]


def _plan_to_chips(src, land):
    x, y, c = lax.axis_index("x"), lax.axis_index("y"), lax.axis_index("c")
    peers = [(x ^ (d >> 1), y ^ (d & 1)) for d in range(1, N_CHIP)]
    return [(src.at[2 * px + py], land.at[_my_chip()], (px, py, c)) for px, py in peers]


def _plan_split_to_all(src, land):
    x, y, c = lax.axis_index("x"), lax.axis_index("y"), lax.axis_index("c")
    peers = [(x ^ ((d >> 2) & 1), y ^ ((d >> 1) & 1), c ^ (d & 1)) for d in range(1, N_DEV)]
    return [(src.at[4 * px + 2 * py + pc], land.at[_my_slot()], (px, py, pc)) for px, py, pc in peers]


_PLAN_COPIES = {_plan_to_all: N_DEV - 1, _plan_split_to_all: N_DEV - 1, _plan_to_sibling: N_CHIP,
                _plan_to_chips: N_CHIP - 1}


def _plans(plan, n):
    return list(plan) if isinstance(plan, (list, tuple)) else [plan] * n


def _exchange_copies(plan, ins, lands, send, recv):
    copies, sem = [], 0
    for p, src, land in zip(_plans(plan, len(lands)), ins, lands):
        for s, dst, dev in p(src, land):
            copies.append(pltpu.make_async_remote_copy(
                src_ref=s, dst_ref=dst, send_sem=send.at[sem], recv_sem=recv.at[sem],
                device_id=dev, device_id_type=pl.DeviceIdType.MESH))
            sem += 1
    return copies


def _place_own(a, dtype, *, after=None, name):
    r, c = a.shape
    rb = _row_block(r, c)
    dep_specs, deps = _dep_specs(after)

    def body(*refs):
        a_ref, o_ref = refs[1 + len(deps):]
        o_ref[...] = a_ref[...].astype(dtype)

    grid_spec = pltpu.PrefetchScalarGridSpec(
        num_scalar_prefetch=1, grid=(r // rb,), in_specs=dep_specs + [pl.BlockSpec((rb, c), lambda i, me: (i, 0))],
        out_specs=pl.BlockSpec((None, rb, c), lambda i, me: (me[0], i, 0)))
    return pl.pallas_call(body, grid_spec=grid_spec, out_shape=_sds((N_DEV, r, c), dtype),
                          compiler_params=_cparams("arbitrary"), name=name)(_my_slot().reshape(1), *deps, a)


def _plan_gather_first(land, _):
    x, y, c = lax.axis_index("x"), lax.axis_index("y"), lax.axis_index("c")
    mine = land.at[_my_slot()]
    return [(mine, mine, (x, y, 1 - c))] + [(mine, mine, (x ^ (d >> 1), y ^ (d & 1), c)) for d in range(1, N_CHIP)]


def _plan_gather_relay(land, _):
    x, y, c = lax.axis_index("x"), lax.axis_index("y"), lax.axis_index("c")
    slots = [land.at[4 * (x ^ (d >> 1)) + 2 * (y ^ (d & 1)) + c] for d in range(1, N_CHIP)]
    return [(s, s, (x, y, 1 - c)) for s in slots]


_PLAN_COPIES[_plan_gather_first] = N_CHIP
_PLAN_COPIES[_plan_gather_relay] = N_CHIP - 1


def _exchange_start(plan, arrs, lands, *, after=None, name):
    if lands is None:
        lands = [lax.empty((N_CHIP,) + a.shape[1:], a.dtype) for a in arrs]
    bufs = list(lands) if arrs is None else list(arrs) + list(lands)
    n, nb = len(lands), len(bufs)
    nsem = sum(_PLAN_COPIES[p] for p in _plans(plan, n))
    dep_specs, deps = _dep_specs(after)

    def body(*refs):
        ins, land_refs = refs[:n], refs[nb - n:nb]
        send, recv = refs[nb + len(deps)], refs[nb + len(deps) + 1]
        for cp in _exchange_copies(plan, ins, land_refs, send, recv):
            cp.start()
        refs[-1][...] = jnp.zeros_like(refs[-1])

    out = pl.pallas_call(
        body, name=name,
        out_shape=(pltpu.SemaphoreType.DMA((nsem,)), pltpu.SemaphoreType.DMA((nsem,)),
                   *[pltpu.HBM(a.shape, a.dtype) for a in bufs], _sds((8, LANE), F32)),
        in_specs=[_HBM] * nb + dep_specs,
        out_specs=(_SEM, _SEM, *([_HBM] * nb), pl.BlockSpec(memory_space=pltpu.VMEM)),
        input_output_aliases={i: 2 + i for i in range(nb)},
        compiler_params=pltpu.CompilerParams(has_side_effects=_DATAFLOW),
    )(*[pltpu.with_memory_space_constraint(a, pltpu.HBM) for a in bufs], *deps)
    return (plan, n, out[0], out[1], list(out[2:2 + nb])), out[-1]


def _exchange_wait(state, after, *, name):
    plan, n, send_sem, recv_sem, bufs = state
    nb = len(bufs)
    after = list(after) if isinstance(after, (list, tuple)) else [after]

    def body(*refs):
        ins, land_refs, send, recv = refs[:n], refs[nb - n:nb], refs[nb], refs[nb + 1]
        for cp in _exchange_copies(plan, ins, land_refs, send, recv):
            cp.wait_send()
            cp.wait_recv()

    out = pl.pallas_call(
        body, name=name, out_shape=[pltpu.HBM(a.shape, a.dtype) for a in bufs],
        in_specs=[_HBM] * nb + [_SEM, _SEM] + [pl.BlockSpec(memory_space=pl.ANY)] * len(after), out_specs=[_HBM] * nb,
        input_output_aliases={i: i for i in range(nb)},
        compiler_params=pltpu.CompilerParams(has_side_effects=_DATAFLOW),
    )(*bufs, send_sem, recv_sem, *after)
    return list(out[:n]), list(out[nb - n:])


def _dep_specs(after):
    return ([], []) if after is None else ([pl.BlockSpec(memory_space=pl.ANY)], [after])


def _undo_column_split(g):
    return jnp.transpose(g, (1, 0, 2)).reshape(g.shape[1], N_DEV * g.shape[2])


def _column_split(a):
    r, c = a.shape
    return jnp.transpose(a.reshape(r, N_DEV, c // N_DEV), (1, 0, 2))


class _WholeWeights:
    def __init__(self, groups):
        self.groups = groups
        self.grads = {}

    def fetch(self, group, after):
        return self.groups[group]

    def emit(self, group, grads):
        self.grads.update(grads)
        return None

    def poll(self, after):
        return None


def _local_step(x, target, replicated, src):
    d = x.shape[1]
    mix_g, ffn_g = replicated["mix_g"], replicated["ffn_g"]
    h0 = jnp.concatenate([jnp.zeros((CHUNK, d), F32), x], axis=0)
    tgt = jnp.concatenate([jnp.zeros((CHUNK, d), F32), target], axis=0)
    cp = src.fetch("cp", [h0, tgt])
    h0 = lax.dynamic_update_slice(h0, cp["meta"], (PAD_ROWS, 0))
    cp_mid = (cp["conv_w"], replicated["conv_b"], replicated["ln_g"], replicated["ln_b"], replicated["pool_w"],
              replicated["pool_scale"])

    z0, u0 = _linear_fwd(h0, cp["cp_w_in_t"], gain=mix_g[0:1], w_t=True, name="cp_in")
    cat = _cp_mid_fwd(z0, *cp_mid, name="cp_mid")
    h1 = _linear_fwd(cat, cp["cp_w_out"], res=h0, name="cp_out")
    ffn0 = src.fetch("ffn0", h1)
    h2, uf0, rf0 = _ffn_fwd(h1, ffn_g[0:1], ffn0["w1"], ffn0["w2"], name="ffn0")
    gla = src.fetch("gla", h2)
    gla_mid = (gla["gate_w"], gla["gate_b"], gla["head_g"])
    z1, u1 = _linear_fwd(h2, gla["gla_w_qkvg_t"], gain=mix_g[1:2], w_t=True, name="gla_in")
    r1 = _linear_fwd(u1, gla["gla_w_r_t"], w_t=True, name="gla_in_r")
    og, states = _gla_mid_fwd(z1, r1, *gla_mid, name="gla_mid")
    h3 = _linear_fwd(og, gla["gla_w_out"], res=h2, name="gla_out")
    ffn1 = src.fetch("ffn1", h3)
    h4, uf1, rf1 = _ffn_fwd(h3, ffn_g[1:2], ffn1["w1"], ffn1["w2"], name="ffn1")
    dh4, loss, d_final_g = _head(h4, replicated["final_g"], tgt, name="head")

    dh3, dhh1, dob1, dffn_g1 = _ffn_bwd_x(h3, dh4, ffn_g[1:2], rf1, ffn1["w1"], ffn1["w2"], name="ffn1_bwd_x")
    dw1_1, dw2_1 = _ffn_bwd_w(uf1, dhh1, rf1, dob1, name="ffn1_bwd_w")
    sent = src.emit("ffn1", dict(w1=dw1_1, w2=dw2_1))
    dog = _linear_bwd_x([dh3], [gla["gla_w_out"]], after=sent, name="gla_out_dx")
    d_gla_w_out = _linear_bwd_w(og, dh3, name="gla_out_dw")
    sent = src.poll(d_gla_w_out)
    dz1, dr1, d_gate_w, d_gate_b, d_head_g = _gla_mid_bwd(z1, r1, dog, states, *gla_mid, after=sent, name="gla_mid_bwd")
    dh2, dmix_g1 = _linear_bwd_x([dz1, dr1], [gla["gla_w_qkvg_t"], gla["gla_w_r_t"]], w_t=True,
                                 norm=(h2, mix_g[1:2], dh3), name="gla_in_dx")
    d_gla_w_qkvg_t = _linear_bwd_w(dz1, u1, name="gla_in_dw")
    d_gla_w_r_t = _linear_bwd_w(dr1, u1, name="gla_in_r_dw")
    sent = src.emit("gla", dict(gla_w_qkvg_t=d_gla_w_qkvg_t, gla_w_r_t=d_gla_w_r_t, gla_w_out=d_gla_w_out))
    dh1, dhh0, dob0, dffn_g0 = _ffn_bwd_x(h1, dh2, ffn_g[0:1], rf0, ffn0["w1"], ffn0["w2"], after=sent, name="ffn0_bwd_x")
    dw1_0, dw2_0 = _ffn_bwd_w(uf0, dhh0, rf0, dob0, name="ffn0_bwd_w")
    src.poll(dw1_0)
    sent = src.emit("ffn0", dict(w1=dw1_0, w2=dw2_0))
    dcat = _linear_bwd_x([dh1], [cp["cp_w_out"]], after=sent, name="cp_out_dx")
    d_cp_w_out = _linear_bwd_w(cat, dh1, name="cp_out_dw")
    src.emit("cp_out", dict(cp_w_out=d_cp_w_out))
    sent = src.poll(d_cp_w_out)
    dz0, d_conv_w, d_conv_b, d_ln_g, d_ln_b, d_pool_w, d_pool_scale = _cp_mid_bwd(z0, dcat, *cp_mid, after=sent,
                                                                                 name="cp_mid_bwd")
    dh0, dmix_g0 = _linear_bwd_x([dz0], [cp["cp_w_in_t"]], w_t=True, norm=(h0, mix_g[0:1], dh1), name="cp_in_dx")
    d_cp_w_in_t = _linear_bwd_w(dz0, u0, name="cp_in_dw")

    small = dict(
        mix_g=jnp.concatenate([dmix_g0, dmix_g1]), ffn_g=jnp.concatenate([dffn_g0, dffn_g1]), conv_b=d_conv_b, ln_g=d_ln_g,
        ln_b=d_ln_b, pool_w=d_pool_w, pool_scale=d_pool_scale, final_g=d_final_g, meta=dh0[PAD_ROWS:CHUNK], conv_w=d_conv_w,
        gate_w=d_gate_w, gate_b=d_gate_b, head_g=d_head_g)
    src.emit("cp", dict(cp_w_in_t=d_cp_w_in_t, small=small))
    return loss, dh0[CHUNK:], small


_REPLICATED = ("mix_norm_g", "ffn_norm_g", "cp_conv_b", "cp_ln_g", "cp_ln_b", "cp_pool_w", "cp_pool_scale", "final_norm_g")
_SMALL_SHARDED = ("meta_tokens", "cp_conv_w", "gla_gate_w2", "gla_gate_b", "gla_head_g")
_NAMES = ("meta_tokens", "mix_norm_g", "ffn_norm_g", "ffn_w1", "ffn_w2", "cp_w_in", "cp_conv_w", "cp_conv_b", "cp_ln_g",
          "cp_ln_b", "cp_pool_w", "cp_pool_scale", "cp_w_out", "gla_w_in", "gla_gate_w2", "gla_gate_b", "gla_head_g",
          "gla_w_out", "final_norm_g")
_SMALL_GRADS = ("mix_g", "ffn_g", "conv_b", "ln_g", "ln_b", "pool_w", "pool_scale", "final_g", "meta", "conv_w", "gate_w",
                "gate_b", "head_g")
_GROUPS = ("cp", "ffn0", "gla", "ffn1")


class _Exchanges:
    def __init__(self, w, d):
        self.d = d
        small = [w[n].reshape(w[n].shape[-2:]) for n in _SMALL_SHARDED]
        self.small_shard_shapes = [w[n].shape for n in _SMALL_SHARDED]
        shards = dict(
            cp=[(w["cp_w_in"][0].T, BF16), (w["cp_w_out"][0], BF16)] + [(a, F32) for a in small],
            ffn0=[(w["ffn_w1"][0], BF16), (w["ffn_w2"][0], BF16)],
            gla=[(w["gla_w_in"][0].T, BF16), (w["gla_w_out"][0], BF16)],
            ffn1=[(w["ffn_w1"][1], BF16), (w["ffn_w2"][1], BF16)])
        self.gathers = {}
        self.to_sibling, self.to_chips = [], {}
        token = None
        for group in _GROUPS:
            lands = [_place_own(a, dtype, after=token, name=f"place_w_{group}_{k}")
                     for k, (a, dtype) in enumerate(shards[group])]
            self.gathers[group], token = _exchange_start(_plan_gather_first, None, lands, after=token,
                                                         name=f"start_w_{group}")
        self.token = token

    def fetch(self, group, after):
        d = self.d
        after = (list(after) if isinstance(after, (list, tuple)) else [after]) + [self.token]
        _, lands = _exchange_wait(self.gathers[group], after, name=f"wait_w_{group}")
        relay, token = _exchange_start(_plan_gather_relay, None, lands, name=f"relay_w_{group}")
        _, got = _exchange_wait(relay, token, name=f"wait_relay_w_{group}")
        if group in ("ffn0", "ffn1"):
            return dict(w1=got[0], w2=got[1])
        if group == "gla":
            qkvg = 3 * d
            w_in_t = got[0].reshape(-1, d)
            return dict(gla_w_qkvg_t=w_in_t[:qkvg], gla_w_r_t=jnp.pad(w_in_t[qkvg:], ((0, GATE_PAD - GATE_RANK), (0, 0))),
                        gla_w_out=got[1].reshape(d, d), gate_w=self.gate_w, gate_b=self.gate_b, head_g=self.head_g)
        meta, conv_w, gate_w, self.gate_b, self.head_g = [_undo_column_split(a) for a in got[2:]]
        self.gate_w = jnp.pad(gate_w, ((0, GATE_PAD - GATE_RANK), (0, 0))).astype(BF16)
        return dict(cp_w_in_t=got[0].reshape(-1, d), cp_w_out=got[1].reshape(d, d), meta=meta,
                    conv_w=jnp.pad(conv_w, ((0, 1), (0, 0))))

    def emit(self, group, g):
        d = self.d
        if group in ("ffn0", "ffn1"):
            arrs = [g["w1"], g["w2"]]
        elif group == "gla":
            w_in_t = jnp.concatenate([g["gla_w_qkvg_t"], g["gla_w_r_t"][:GATE_RANK]], axis=0)
            arrs = [w_in_t.reshape(N_DEV, -1, d), g["gla_w_out"].reshape(N_DEV, d // N_DEV, d)]
        elif group == "cp_out":
            arrs = [g["cp_w_out"].reshape(N_DEV, d // N_DEV, d)]
        else:
            s = dict(g["small"])
            s.update(pool_w=s["pool_w"][None], conv_w=s["conv_w"][:CONV_WIDTH], gate_w=s["gate_w"][:GATE_RANK])
            own = [s[n] for n in _SMALL_GRADS[:len(_REPLICATED)]]
            own += [_column_split(s[n]).reshape((N_DEV,) + shape)
                    for n, shape in zip(_SMALL_GRADS[len(_REPLICATED):], self.small_shard_shapes)]
            plans = [_plan_to_all] * len(_REPLICATED) + [_plan_split_to_all] * len(_SMALL_SHARDED)
            lands = [lax.empty((N_DEV,) + a.shape, F32) for a in own[:len(_REPLICATED)]]
            lands += [lax.empty(a.shape, F32) for a in own[len(_REPLICATED):]]
            self.small_sent, self.token = _exchange_start(plans, own, lands, after=self.token, name="start_g_small")
            arrs = [g["cp_w_in_t"].reshape(N_DEV, -1, d)]
        state, self.token = _exchange_start(_plan_to_sibling, arrs, None, after=self.token, name=f"start_g1_{group}")
        self.to_sibling.append((group, state))
        return self.token

    def poll(self, after):
        for group, state in self.to_sibling:
            arrs, landed = _exchange_wait(state, after, name=f"wait_g1_{group}")
            sums = [_pair_add(a, l, name=f"chip_sum_{group}_{k}") for k, (a, l) in enumerate(zip(arrs, landed))]
            self.to_chips[group], self.token = _exchange_start(_plan_to_chips, sums, None, after=self.token,
                                                               name=f"start_g2_{group}")
            after = self.token
        self.to_sibling = []
        return self.token

    def finish(self, w, mom, var):
        out = {}
        self.poll(self.token)
        after = self.token

        def landed(group):
            sums, got = _exchange_wait(self.to_chips[group], after, name=f"wait_g2_{group}")
            return list(zip(sums, got))

        def adam(n, parts, behind=None, transposed=False):
            flip = (lambda a: jnp.transpose(a, (0, 2, 1))) if transposed else (lambda a: a)
            res = _reduce_adam(parts, flip(w[n]), flip(mom[n]), flip(var[n]), after=behind, name=f"adam_{n}")
            out[n] = tuple(flip(a) for a in res)
            return res[0]

        ffn1 = landed("ffn1")
        after = ffn1[0][1]
        gla = landed("gla")
        after = gla[0][1]
        ffn0 = landed("ffn0")
        after = adam("ffn_w1", [ffn0[0], ffn1[0]])
        after = adam("ffn_w2", [ffn0[1], ffn1[1]], after)
        after = adam("gla_w_in", [gla[0]], after, transposed=True)
        after = adam("gla_w_out", [gla[1]], after)
        small_own, small_landed = _exchange_wait(self.small_sent, after, name="wait_g_small")
        names = _REPLICATED + _SMALL_SHARDED
        split = [False] * len(_REPLICATED) + [True] * len(_SMALL_SHARDED)
        small_new = _adam_small(small_own, small_landed, split, [w[n] for n in names], [mom[n] for n in names],
                                [var[n] for n in names], name="adam_small")
        out.update(zip(names, small_new))

        after = small_new[0][0]
        cp_out = landed("cp_out")
        after = adam("cp_w_out", [cp_out[0]])
        cp = landed("cp")
        adam("cp_w_in", [cp[0]], transposed=True)
        return out


def kernel(x, meta_tokens, mix_norm_g, ffn_norm_g, ffn_w1, ffn_w2, cp_w_in, cp_conv_w, cp_conv_b, cp_ln_g, cp_ln_b, cp_pool_w, cp_pool_scale, cp_w_out, gla_w_in, gla_gate_w2, gla_gate_b, gla_head_g, gla_w_out, final_norm_g, loss_target, m_meta_tokens, m_mix_norm_g, m_ffn_norm_g, m_ffn_w1, m_ffn_w2, m_cp_w_in, m_cp_conv_w, m_cp_conv_b, m_cp_ln_g, m_cp_ln_b, m_cp_pool_w, m_cp_pool_scale, m_cp_w_out, m_gla_w_in, m_gla_gate_w2, m_gla_gate_b, m_gla_head_g, m_gla_w_out, m_final_norm_g, v_meta_tokens, v_mix_norm_g, v_ffn_norm_g, v_ffn_w1, v_ffn_w2, v_cp_w_in, v_cp_conv_w, v_cp_conv_b, v_cp_ln_g, v_cp_ln_b, v_cp_pool_w, v_cp_pool_scale, v_cp_w_out, v_gla_w_in, v_gla_gate_w2, v_gla_gate_b, v_gla_head_g, v_gla_w_out, v_final_norm_g):
    w = dict(meta_tokens=meta_tokens, mix_norm_g=mix_norm_g, ffn_norm_g=ffn_norm_g, ffn_w1=ffn_w1, ffn_w2=ffn_w2,
             cp_w_in=cp_w_in, cp_conv_w=cp_conv_w, cp_conv_b=cp_conv_b, cp_ln_g=cp_ln_g, cp_ln_b=cp_ln_b,
             cp_pool_w=cp_pool_w, cp_pool_scale=cp_pool_scale, cp_w_out=cp_w_out, gla_w_in=gla_w_in,
             gla_gate_w2=gla_gate_w2, gla_gate_b=gla_gate_b, gla_head_g=gla_head_g, gla_w_out=gla_w_out,
             final_norm_g=final_norm_g.reshape(1, -1))
    mom = dict(meta_tokens=m_meta_tokens, mix_norm_g=m_mix_norm_g, ffn_norm_g=m_ffn_norm_g, ffn_w1=m_ffn_w1, ffn_w2=m_ffn_w2,
               cp_w_in=m_cp_w_in, cp_conv_w=m_cp_conv_w, cp_conv_b=m_cp_conv_b, cp_ln_g=m_cp_ln_g, cp_ln_b=m_cp_ln_b,
               cp_pool_w=m_cp_pool_w, cp_pool_scale=m_cp_pool_scale, cp_w_out=m_cp_w_out, gla_w_in=m_gla_w_in,
               gla_gate_w2=m_gla_gate_w2, gla_gate_b=m_gla_gate_b, gla_head_g=m_gla_head_g, gla_w_out=m_gla_w_out,
               final_norm_g=m_final_norm_g.reshape(1, -1))
    var = dict(meta_tokens=v_meta_tokens, mix_norm_g=v_mix_norm_g, ffn_norm_g=v_ffn_norm_g, ffn_w1=v_ffn_w1, ffn_w2=v_ffn_w2,
               cp_w_in=v_cp_w_in, cp_conv_w=v_cp_conv_w, cp_conv_b=v_cp_conv_b, cp_ln_g=v_cp_ln_g, cp_ln_b=v_cp_ln_b,
               cp_pool_w=v_cp_pool_w, cp_pool_scale=v_cp_pool_scale, cp_w_out=v_cp_w_out, gla_w_in=v_gla_w_in,
               gla_gate_w2=v_gla_gate_w2, gla_gate_b=v_gla_gate_b, gla_head_g=v_gla_head_g, gla_w_out=v_gla_w_out,
               final_norm_g=v_final_norm_g.reshape(1, -1))
    d = x.shape[-1]
    replicated = dict(mix_g=w["mix_norm_g"], ffn_g=w["ffn_norm_g"], conv_b=w["cp_conv_b"], ln_g=w["cp_ln_g"],
                      ln_b=w["cp_ln_b"], pool_w=w["cp_pool_w"][0].astype(BF16), pool_scale=w["cp_pool_scale"],
                      final_g=w["final_norm_g"])
    exchanges = _Exchanges(w, d)
    loss_blk, grad_x, _ = _local_step(x[0], loss_target[0], replicated, exchanges)
    loss = lax.psum(loss_blk[0, 0], ("x", "y", "c"))
    out = exchanges.finish(w, mom, var)

    def leaf(n, k):
        a = out[n][k]
        return a.reshape(-1) if n == "final_norm_g" else a

    return (loss, grad_x[None], *[leaf(n, 0) for n in _NAMES], *[leaf(n, 1) for n in _NAMES],
            *[leaf(n, 2) for n in _NAMES], *[leaf(n, 3) for n in _NAMES])
```

```python
import functools

import jax
import jax.numpy as jnp
from jax import lax
from jax.experimental import pallas as pl
from jax.experimental.pallas import tpu as pltpu

F32, BF16 = jnp.float32, jnp.bfloat16
N_DEV = 8
CHUNK = 64
N_META = 16
PAD_ROWS = CHUNK - N_META
HALO = 32
EPS = 1e-5
CONV_WIDTH = 31
POOL_WINDOWS = (2, 4, 8, 16)
HEADS = 4
GATE_RANK = 16
GATE_NORM = 16.0
GATE_PAD = 128
ADAM_LR, ADAM_B1, ADAM_B2, ADAM_EPS, ADAM_WD, ADAM_STEP = 0.001, 0.9, 0.999, 1e-08, 0.01, 10
V7X_VMEM_LIMIT = 56 * 2 ** 20
LANE = 128
HI = lax.Precision.HIGHEST


def _cparams(*sem):
    return pltpu.CompilerParams(dimension_semantics=sem, vmem_limit_bytes=V7X_VMEM_LIMIT)


def _row_tile(t, cap):
    best = CHUNK
    for r in range(CHUNK, min(t, cap) + 1, CHUNK):
        if t % r == 0:
            best = r
    return best


def _resident(shape):
    return pl.BlockSpec(shape, lambda *_: (0,) * len(shape), pipeline_mode=pl.Buffered(1))


def _dot(a, b):
    return jnp.dot(a, b, preferred_element_type=F32)


def _dot_nt(a, b):
    return lax.dot_general(a, b, (((1,), (1,)), ((), ())), preferred_element_type=F32)


def _dot_tn(a, b):
    return lax.dot_general(a, b, (((0,), (0,)), ((), ())), preferred_element_type=F32)


def _rowsum(a):
    return jnp.sum(a, axis=0, keepdims=True)


def _sigmoid(a):
    return 1.0 / (1.0 + jnp.exp(-a))


def _row_ids(tile, rt):
    return tile * rt + lax.broadcasted_iota(jnp.int32, (rt, 1), 0)


def _sds(shape, dtype):
    return jax.ShapeDtypeStruct(shape, dtype)


def _linear_fwd(x, w, *, gain=None, res=None, w_t=False, out_dtype=F32, name):
    t, k = x.shape
    n = w.shape[0] if w_t else w.shape[1]
    rt = _row_tile(t, 320)

    def body(*refs):
        refs = list(refs)
        x_ref, w_ref = refs[:2]
        pos = 2
        g_ref = r_ref = u_ref = None
        if gain is not None:
            g_ref = refs[pos]
            pos += 1
        if res is not None:
            r_ref = refs[pos]
            pos += 1
        y_ref = refs[pos]
        if gain is not None:
            u_ref = refs[pos + 1]
            xv = x_ref[...]
            u = (xv * lax.rsqrt(jnp.mean(xv * xv, axis=-1, keepdims=True) + EPS) * g_ref[...]).astype(BF16)
            u_ref[...] = u
        else:
            u = x_ref[...]
        y = _dot_nt(u, w_ref[...]) if w_t else _dot(u, w_ref[...])
        if res is not None:
            y = y + r_ref[...]
        y_ref[...] = y.astype(y_ref.dtype)

    rows = lambda i: (i, 0)
    in_specs = [pl.BlockSpec((rt, k), rows), _resident(w.shape)]
    args = [x, w]
    if gain is not None:
        in_specs.append(_resident((1, k)))
        args.append(gain)
    if res is not None:
        in_specs.append(pl.BlockSpec((rt, n), rows))
        args.append(res)
    out_shape = [_sds((t, n), out_dtype)]
    out_specs = [pl.BlockSpec((rt, n), rows)]
    if gain is not None:
        out_shape.append(_sds((t, k), BF16))
        out_specs.append(pl.BlockSpec((rt, k), rows))
    out = pl.pallas_call(body, grid=(t // rt,), in_specs=in_specs, out_specs=out_specs, out_shape=out_shape,
                         compiler_params=_cparams("parallel"), name=name)(*args)
    return out if gain is not None else out[0]


def _linear_bwd_x(dys, ws, *, norm=None, w_t=False, after=None, name):
    t = dys[0].shape[0]
    k = ws[0].shape[1] if w_t else ws[0].shape[0]
    rt = _row_tile(t, 320)
    nd = len(dys)
    dep_specs, deps = _dep_specs(after)

    def body(*refs):
        refs = list(refs)[len(deps):]
        dy_refs, w_refs = refs[:nd], refs[nd:2 * nd]
        dx = None
        for dy_ref, w_ref in zip(dy_refs, w_refs):
            dy = dy_ref[...].astype(BF16)
            part = _dot(dy, w_ref[...]) if w_t else _dot_nt(dy, w_ref[...])
            dx = part if dx is None else dx + part
        if norm is None:
            refs[2 * nd][...] = dx
            return
        h_ref, g_ref, dres_ref, dh_ref, dg_ref = refs[2 * nd:]
        hv = h_ref[...]
        rstd = lax.rsqrt(jnp.mean(hv * hv, axis=-1, keepdims=True) + EPS)
        xh = hv * rstd

        @pl.when(pl.program_id(0) == 0)
        def _():
            dg_ref[...] = jnp.zeros_like(dg_ref)

        dg_ref[...] += _rowsum(dx * xh)
        dxh = dx * g_ref[...]
        dh_ref[...] = dres_ref[...] + rstd * (dxh - xh * jnp.mean(dxh * xh, axis=-1, keepdims=True))

    rows = lambda i: (i, 0)
    in_specs = [pl.BlockSpec((rt, dy.shape[1]), rows) for dy in dys] + [_resident(w.shape) for w in ws]
    args = list(dys) + list(ws)
    out_shape = [_sds((t, k), F32)]
    out_specs = [pl.BlockSpec((rt, k), rows)]
    if norm is not None:
        h, gain, dres = norm
        in_specs += [pl.BlockSpec((rt, k), rows), _resident((1, k)), pl.BlockSpec((rt, k), rows)]
        args += [h, gain, dres]
        out_shape.append(_sds((1, k), F32))
        out_specs.append(pl.BlockSpec((1, k), lambda i: (0, 0)))
    out = pl.pallas_call(body, grid=(t // rt,), in_specs=dep_specs + in_specs, out_specs=out_specs, out_shape=out_shape,
                         compiler_params=_cparams("arbitrary"), name=name)(*deps, *args)
    return out if norm is not None else out[0]


DW_ROWS = 1024


def _linear_bwd_w(x, dy, *, name):
    t, k = x.shape
    n = dy.shape[1]
    cut_k = k > n
    width = k if cut_k else n
    blk = max(c for c in (512, 384, 256, LANE) if width % c == 0)

    def body(x_ref, dy_ref, o_ref, acc):
        for c0 in range(0, t, DW_ROWS):
            rows = slice(c0, min(c0 + DW_ROWS, t))
            part = _dot_tn(x_ref[rows, :].astype(BF16), dy_ref[rows, :].astype(BF16))
            if c0 == 0:
                acc[...] = part
            else:
                acc[...] += part
        o_ref[...] = acc[...].astype(BF16)

    if cut_k:
        in_specs = [pl.BlockSpec((t, blk), lambda j: (0, j)), _resident((t, n))]
        out_specs = pl.BlockSpec((blk, n), lambda j: (j, 0))
        acc_shape = (blk, n)
    else:
        in_specs = [_resident((t, k)), pl.BlockSpec((t, blk), lambda j: (0, j))]
        out_specs = pl.BlockSpec((k, blk), lambda j: (0, j))
        acc_shape = (k, blk)
    return pl.pallas_call(
        body, grid=(width // blk,), in_specs=in_specs, out_specs=out_specs, out_shape=_sds((k, n), BF16),
        scratch_shapes=[pltpu.VMEM(acc_shape, F32)], compiler_params=_cparams("parallel"), name=name)(x, dy)


def _ffn_fwd(h, gain, w1g, w2g, *, name):
    t, d = h.shape
    f8 = w1g.shape[-1]
    rt = _row_tile(t, 832)

    def body(h_ref, g_ref, w1_ref, w2_ref, o_ref, u_ref, r_ref, acc_ref):
        j = pl.program_id(1)

        @pl.when(j == 0)
        def _():
            hv = h_ref[...]
            u_ref[...] = (hv * lax.rsqrt(jnp.mean(hv * hv, axis=-1, keepdims=True) + EPS) * g_ref[...]).astype(BF16)
            acc_ref[...] = jnp.zeros_like(acc_ref)

        a = jnp.maximum(_dot(u_ref[...], w1_ref[...]), 0.0)
        r_ref[...] = a.astype(BF16)
        acc_ref[...] += _dot((a * a).astype(BF16), w2_ref[...])

        @pl.when(j == N_DEV - 1)
        def _():
            o_ref[...] = h_ref[...] + acc_ref[...]

    return pl.pallas_call(
        body, grid=(t // rt, N_DEV),
        in_specs=[pl.BlockSpec((rt, d), lambda i, j: (i, 0)), _resident((1, d)),
                  pl.BlockSpec((None, d, f8), lambda i, j: (j, 0, 0)),
                  pl.BlockSpec((None, f8, d), lambda i, j: (j, 0, 0))],
        out_specs=[pl.BlockSpec((rt, d), lambda i, j: (i, 0)), pl.BlockSpec((rt, d), lambda i, j: (i, 0)),
                   pl.BlockSpec((rt, f8), lambda i, j: (i, j))],
        out_shape=[_sds((t, d), F32), _sds((t, d), BF16), _sds((t, N_DEV * f8), BF16)],
        scratch_shapes=[pltpu.VMEM((rt, d), F32)],
        compiler_params=_cparams("parallel", "arbitrary"), name=name)(h, gain, w1g, w2g)


def _ffn_bwd_x(h, dout, gain, r, w1g, w2g, *, after=None, name):
    t, d = h.shape
    f8 = w1g.shape[-1]
    rt = _row_tile(t, 832)
    last = N_DEV - 1
    dep_specs, deps = _dep_specs(after)

    def body(*refs):
        h_ref, do_ref, g_ref, r_ref, w1_ref, w2_ref, dh_ref, dhh_ref, dob_ref, dg_ref, du_ref = refs[len(deps):]
        i, j = pl.program_id(0), pl.program_id(1)

        @pl.when(j == 0)
        def _():
            dob_ref[...] = do_ref[...].astype(BF16)
            du_ref[...] = jnp.zeros_like(du_ref)

        dhh = (_dot_nt(dob_ref[...], w2_ref[...]) * (2.0 * r_ref[...].astype(F32))).astype(BF16)
        dhh_ref[...] = dhh
        du_ref[...] += _dot_nt(dhh, w1_ref[...])

        @pl.when(j == last)
        def _():
            @pl.when(i == 0)
            def _():
                dg_ref[...] = jnp.zeros_like(dg_ref)

            hv = h_ref[...]
            rstd = lax.rsqrt(jnp.mean(hv * hv, axis=-1, keepdims=True) + EPS)
            xh = hv * rstd
            du = du_ref[...]
            dg_ref[...] += _rowsum(du * xh)
            dxh = du * g_ref[...]
            dh_ref[...] = do_ref[...] + rstd * (dxh - xh * jnp.mean(dxh * xh, axis=-1, keepdims=True))

    rows = lambda i, j: (i, 0)
    return pl.pallas_call(
        body, grid=(t // rt, N_DEV),
        in_specs=dep_specs + [
                  pl.BlockSpec((rt, d), rows), pl.BlockSpec((rt, d), rows), _resident((1, d)),
                  pl.BlockSpec((rt, f8), lambda i, j: (i, j)),
                  pl.BlockSpec((None, d, f8), lambda i, j: (j, 0, 0)),
                  pl.BlockSpec((None, f8, d), lambda i, j: (j, 0, 0))],
        out_specs=[pl.BlockSpec((rt, d), rows), pl.BlockSpec((rt, f8), lambda i, j: (i, j)), pl.BlockSpec((rt, d), rows),
                   pl.BlockSpec((1, d), lambda i, j: (0, 0))],
        out_shape=[_sds((t, d), F32), _sds((t, N_DEV * f8), BF16), _sds((t, d), BF16), _sds((1, d), F32)],
        scratch_shapes=[pltpu.VMEM((rt, d), F32)],
        compiler_params=_cparams("arbitrary", "arbitrary"), name=name)(*deps, h, dout, gain, r, w1g, w2g)


def _ffn_bwd_w(u, dhh, r, dout_b, *, name):
    t, d = u.shape
    f8 = dhh.shape[1] // N_DEV

    def body(u_ref, dhh_ref, r_ref, dob_ref, dw1_ref, dw2_ref, acc1, acc2):
        for c0 in range(0, t, DW_ROWS):
            rows = slice(c0, min(c0 + DW_ROWS, t))
            rr = r_ref[rows, :].astype(F32)
            part1 = _dot_tn(u_ref[rows, :], dhh_ref[rows, :])
            part2 = _dot_tn((rr * rr).astype(BF16), dob_ref[rows, :])
            if c0 == 0:
                acc1[...] = part1
                acc2[...] = part2
            else:
                acc1[...] += part1
                acc2[...] += part2
        dw1_ref[...] = acc1[...].astype(BF16)
        dw2_ref[...] = acc2[...].astype(BF16)

    return pl.pallas_call(
        body, grid=(N_DEV,),
        in_specs=[_resident((t, d)), pl.BlockSpec((t, f8), lambda j: (0, j)), pl.BlockSpec((t, f8), lambda j: (0, j)),
                  _resident((t, d))],
        out_specs=[pl.BlockSpec((None, d, f8), lambda j: (j, 0, 0)), pl.BlockSpec((None, f8, d), lambda j: (j, 0, 0))],
        out_shape=[_sds((N_DEV, d, f8), BF16), _sds((N_DEV, f8, d), BF16)],
        scratch_shapes=[pltpu.VMEM((d, f8), F32), pltpu.VMEM((f8, d), F32)],
        compiler_params=_cparams("parallel"), name=name)(u, dhh, r, dout_b)


def _lane_blocks(width):
    lb = min(LANE, width)
    return [slice(s, s + lb) for s in range(0, width, lb)]


def _conv_rows(src_ref, w_ref, offset, dst_ref, nblk, width, bias_ref=None):
    def blk(rb, carry):
        base = pl.multiple_of(rb * CHUNK, CHUNK)
        for l, ls in enumerate(_lane_blocks(width)):
            acc = jnp.zeros((CHUNK, ls.stop - ls.start), F32)
            if bias_ref is not None:
                acc = acc + bias_ref[:, ls]
            for k in range(CONV_WIDTH):
                acc = acc + w_ref[k:k + 1, ls] * src_ref[l, pl.ds(base + offset(k), CHUNK), :]
            dst_ref[l, pl.ds(base, CHUNK), :] = acc
        return carry

    lax.fori_loop(0, nblk, blk, 0)


def _to_lane_blocks(ref, row0, value):
    for l, ls in enumerate(_lane_blocks(value.shape[1])):
        ref[l, row0:row0 + value.shape[0], :] = value[:, ls]


def _from_lane_blocks(ref):
    return jnp.concatenate([ref[l] for l in range(ref.shape[0])], axis=1)


def _pool_counts(rows, window):
    return jnp.clip(rows - PAD_ROWS + 1, 1, window).astype(F32)


def _trailing_sum(v, window):
    s, sh = v, 1
    while sh < window:
        s = s + pltpu.roll(s, sh, 0)
        sh *= 2
    return s


def _leading_sum(v, window):
    s, sh, n = v, 1, v.shape[0]
    while sh < window:
        s = s + pltpu.roll(s, n - sh, 0)
        sh *= 2
    return s


def _cp_mid_fwd(z, conv_w, conv_b, ln_g, ln_b, pool_w, pool_scale, *, name):
    t, ein = z.shape
    cd = conv_b.shape[1]
    pd = pool_scale.shape[1]
    pg = pd // len(POOL_WINDOWS)
    rt = _row_tile(t, 320)

    def body(z_ref, cw_ref, cb_ref, lg_ref, lb_ref, pw_ref, ps_ref, o_ref, gext, pext, conv_s):
        i = pl.program_id(0)

        @pl.when(i == 0)
        def _():
            _to_lane_blocks(gext, 0, jnp.zeros((HALO, cd), F32))
            pext[0:HALO, :] = jnp.zeros((HALO, pd), F32)

        _to_lane_blocks(gext, HALO, z_ref[:, 0:cd] * _sigmoid(z_ref[:, cd:2 * cd]))
        pext[HALO:HALO + rt, :] = z_ref[:, 2 * cd:]
        _conv_rows(gext, cw_ref, lambda k: k + HALO - (CONV_WIDTH - 1), conv_s, rt // CHUNK, cd, cb_ref)
        cv = _from_lane_blocks(conv_s)
        xc = cv - jnp.mean(cv, axis=-1, keepdims=True)
        y = xc * lax.rsqrt(jnp.mean(xc * xc, axis=-1, keepdims=True) + EPS) * lg_ref[...] + lb_ref[...]
        rows = _row_ids(i, rt)
        a = jnp.where(rows >= PAD_ROWS, y * _sigmoid(y), 0.0)
        o_ref[:, 0:cd] = a.astype(BF16)
        for gi, window in enumerate(POOL_WINDOWS):
            ls = slice(gi * pg, (gi + 1) * pg)
            v = pext[:, ls]
            tm = _trailing_sum(v, window)[HALO:] / _pool_counts(rows, window) - v[HALO:]
            p = _dot(tm.astype(BF16), pw_ref[gi]) * ps_ref[:, ls]
            o_ref[:, cd + gi * pg:cd + (gi + 1) * pg] = p.astype(BF16)
        gext[:, 0:HALO, :] = gext[:, rt:rt + HALO, :]
        pext[0:HALO, :] = pext[rt:rt + HALO, :]

    nl, lb = len(_lane_blocks(cd)), min(LANE, cd)
    return pl.pallas_call(
        body, grid=(t // rt,),
        in_specs=[pl.BlockSpec((rt, ein), lambda i: (i, 0)), _resident(conv_w.shape), _resident((1, cd)),
                  _resident((1, cd)), _resident((1, cd)), _resident(pool_w.shape), _resident((1, pd))],
        out_specs=pl.BlockSpec((rt, cd + pd), lambda i: (i, 0)), out_shape=_sds((t, cd + pd), BF16),
        scratch_shapes=[pltpu.VMEM((nl, rt + HALO, lb), F32), pltpu.VMEM((rt + HALO, pd), F32),
                        pltpu.VMEM((nl, rt, lb), F32)],
        compiler_params=_cparams("arbitrary"), name=name)(z, conv_w, conv_b, ln_g, ln_b, pool_w, pool_scale)


def _cp_mid_bwd(z, dcat, conv_w, conv_b, ln_g, ln_b, pool_w, pool_scale, *, after=None, name):
    t, ein = z.shape
    cd = conv_b.shape[1]
    pd = pool_scale.shape[1]
    pg = pd // len(POOL_WINDOWS)
    rt = _row_tile(t, 320)
    ntile = t // rt
    per = rt // CHUNK
    dep_specs, deps = _dep_specs(after)

    def body(*refs):
        (z_ref, zh_ref, dc_ref, cw_ref, cb_ref, lg_ref, lb_ref, pw_ref, ps_ref,
         dz_ref, dcw_ref, dcb_ref, dlg_ref, dlb_ref, dpw_ref, dps_ref, gext, pext, conv_s, dcv, dsp) = refs[len(deps):]
        step = pl.program_id(0)
        tile = ntile - 1 - step

        @pl.when(step == 0)
        def _():
            for ref in (dcw_ref, dcb_ref, dlg_ref, dlb_ref, dpw_ref, dps_ref):
                ref[...] = jnp.zeros_like(ref)
            _to_lane_blocks(dcv, rt, jnp.zeros((HALO, cd), F32))
            dsp[rt:rt + HALO, :] = jnp.zeros((HALO, pd), F32)

        keep = jnp.where(tile > 0, 1.0, 0.0)
        zh = zh_ref[CHUNK - HALO:CHUNK, :]
        _to_lane_blocks(gext, 0, keep * zh[:, 0:cd] * _sigmoid(zh[:, cd:2 * cd]))
        pext[0:HALO, :] = keep * zh[:, 2 * cd:]
        za = z_ref[:, 0:cd]
        sg = _sigmoid(z_ref[:, cd:2 * cd])
        _to_lane_blocks(gext, HALO, za * sg)
        pext[HALO:HALO + rt, :] = z_ref[:, 2 * cd:]
        _conv_rows(gext, cw_ref, lambda k: k + HALO - (CONV_WIDTH - 1), conv_s, per, cd, cb_ref)
        cv = _from_lane_blocks(conv_s)
        xc = cv - jnp.mean(cv, axis=-1, keepdims=True)
        rstd = lax.rsqrt(jnp.mean(xc * xc, axis=-1, keepdims=True) + EPS)
        xh = xc * rstd
        y = xh * lg_ref[...] + lb_ref[...]
        sy = _sigmoid(y)
        rows = _row_ids(tile, rt)
        da = jnp.where(rows >= PAD_ROWS, dc_ref[:, 0:cd], 0.0)
        dy = da * (sy * (1.0 + y * (1.0 - sy)))
        dlg_ref[...] += _rowsum(dy * xh)
        dlb_ref[...] += _rowsum(dy)
        dxh = dy * lg_ref[...]
        dconv = rstd * (dxh - jnp.mean(dxh, axis=-1, keepdims=True) - xh * jnp.mean(dxh * xh, axis=-1, keepdims=True))
        dcb_ref[...] += _rowsum(dconv)
        _to_lane_blocks(dcv, 0, dconv)
        for l, ls in enumerate(_lane_blocks(cd)):
            def acc_rows(rb, accs, l=l):
                base = pl.multiple_of(rb * CHUNK, CHUNK)
                d_blk = dcv[l, pl.ds(base, CHUNK), :]
                out = []
                for k in range(CONV_WIDTH):
                    prod = d_blk * gext[l, pl.ds(base + k + HALO - (CONV_WIDTH - 1), CHUNK), :]
                    part = prod[0:8]
                    for s in range(8, CHUNK, 8):
                        part = part + prod[s:s + 8]
                    out.append(accs[k] + part)
                return tuple(out)

            zero = jnp.zeros((8, ls.stop - ls.start), F32)
            accs = lax.fori_loop(0, per, acc_rows, (zero,) * CONV_WIDTH)
            for k in range(CONV_WIDTH):
                dcw_ref[k:k + 1, ls] += _rowsum(accs[k])
        _conv_rows(dcv, cw_ref, lambda k: CONV_WIDTH - 1 - k, conv_s, per, cd)
        dglu = _from_lane_blocks(conv_s)
        dz_ref[:, 0:cd] = (dglu * sg).astype(BF16)
        dz_ref[:, cd:2 * cd] = (dglu * za * sg * (1.0 - sg)).astype(BF16)
        dcv[:, rt:rt + HALO, :] = dcv[:, 0:HALO, :]
        for gi, window in enumerate(POOL_WINDOWS):
            ls = slice(gi * pg, (gi + 1) * pg)
            v = pext[:, ls]
            cnt = _pool_counts(rows, window)
            tm = (_trailing_sum(v, window)[HALO:] / cnt - v[HALO:]).astype(BF16)
            dp = dc_ref[:, cd + gi * pg:cd + (gi + 1) * pg]
            dps_ref[:, ls] += _rowsum(dp * _dot(tm, pw_ref[gi]))
            dpl = (dp * ps_ref[:, ls]).astype(BF16)
            dpw_ref[gi] += _dot_tn(tm, dpl)
            dtm = _dot_nt(dpl, pw_ref[gi])
            dsp[0:rt, ls] = dtm / cnt
            dpin = _leading_sum(dsp[:, ls], window)[0:rt] - dtm
            dz_ref[:, 2 * cd + gi * pg:2 * cd + (gi + 1) * pg] = dpin.astype(BF16)
        dsp[rt:rt + HALO, :] = dsp[0:HALO, :]

    back = lambda i: (ntile - 1 - i, 0)
    halo_idx = lambda i: (jnp.maximum((ntile - 1 - i) * per - 1, 0), 0)
    const2 = lambda i: (0, 0)
    nl, lb = len(_lane_blocks(cd)), min(LANE, cd)
    return pl.pallas_call(
        body, grid=(ntile,),
        in_specs=dep_specs + [
                  pl.BlockSpec((rt, ein), back), pl.BlockSpec((CHUNK, ein), halo_idx), pl.BlockSpec((rt, cd + pd), back),
                  _resident(conv_w.shape), _resident((1, cd)), _resident((1, cd)), _resident((1, cd)),
                  _resident(pool_w.shape), _resident((1, pd))],
        out_specs=[pl.BlockSpec((rt, ein), back), pl.BlockSpec(conv_w.shape, const2), pl.BlockSpec((1, cd), const2),
                   pl.BlockSpec((1, cd), const2), pl.BlockSpec((1, cd), const2),
                   pl.BlockSpec(pool_w.shape, lambda i: (0, 0, 0)), pl.BlockSpec((1, pd), const2)],
        out_shape=[_sds((t, ein), BF16), _sds(conv_w.shape, F32), _sds((1, cd), F32), _sds((1, cd), F32),
                   _sds((1, cd), F32), _sds(pool_w.shape, F32), _sds((1, pd), F32)],
        scratch_shapes=[pltpu.VMEM((nl, rt + HALO, lb), F32), pltpu.VMEM((rt + HALO, pd), F32), pltpu.VMEM((nl, rt, lb), F32),
                        pltpu.VMEM((nl, rt + HALO, lb), F32), pltpu.VMEM((rt + HALO, pd), F32)],
        compiler_params=_cparams("arbitrary"), name=name)(*deps, z, z, dcat, conv_w, conv_b, ln_g, ln_b, pool_w, pool_scale)


def _log_decay(r_ref, gw_ref, gb_ref, rows):
    gp = _dot(r_ref[...].astype(BF16), gw_ref[...]) + gb_ref[...]
    log_sig = jnp.minimum(gp, 0.0) - jnp.log(1.0 + jnp.exp(-jnp.abs(gp)))
    return gp, jnp.where(rows >= PAD_ROWS, log_sig / GATE_NORM, 0.0)


def _tri(strict):
    r = lax.broadcasted_iota(jnp.int32, (CHUNK, CHUNK), 0)
    c = lax.broadcasted_iota(jnp.int32, (CHUNK, CHUNK), 1)
    return jnp.where(c < r if strict else c <= r, 1.0, 0.0).astype(F32)


def _gla_mid_fwd(z, r, gate_w, gate_b, head_g, *, name):
    t = z.shape[0]
    dk = gate_b.shape[1]
    hv = head_g.shape[1]
    hk = dk // HEADS
    dv = hv * HEADS
    rt = _row_tile(t, 320)
    per = rt // CHUNK
    scale = hk ** -0.5

    def body(z_ref, r_ref, gw_ref, gb_ref, hg_ref, o_ref, st_ref, s_ref, la_ref):
        i = pl.program_id(0)

        @pl.when(i == 0)
        def _():
            s_ref[...] = jnp.zeros_like(s_ref)

        _, la = _log_decay(r_ref, gw_ref, gb_ref, _row_ids(i, rt))
        la_ref[...] = la
        tri = _tri(False)

        def chunk(c, carry):
            rows = pl.ds(pl.multiple_of(c * CHUNK, CHUNK), CHUNK)
            la_c = la_ref[rows, :]
            cum = jnp.dot(tri, la_c, precision=HI, preferred_element_type=F32)
            tot = _rowsum(la_c)
            dec = jnp.exp(tot - cum)
            etot = jnp.exp(tot)
            for hd in range(HEADS):
                ks = slice(hd * hk, (hd + 1) * hk)
                q = z_ref[rows, hd * hk:(hd + 1) * hk] * scale
                kd = z_ref[rows, dk + hd * hk:dk + (hd + 1) * hk] * dec[:, ks]
                v = z_ref[rows, 2 * dk + hd * hv:2 * dk + (hd + 1) * hv]
                g = z_ref[rows, 2 * dk + dv + hd * hv:2 * dk + dv + (hd + 1) * hv]
                s_new = s_ref[hd] * etot[:, ks] + _dot_tn(v.astype(BF16), kd.astype(BF16))
                s_ref[hd] = s_new
                st_ref[c, hd] = s_new
                o = _dot_nt(q.astype(BF16), s_new.astype(BF16))
                on = o * lax.rsqrt(jnp.mean(o * o, axis=-1, keepdims=True) + EPS) * hg_ref[...]
                o_ref[rows, hd * hv:(hd + 1) * hv] = (on * (g * _sigmoid(g))).astype(BF16)
            return carry

        lax.fori_loop(0, per, chunk, 0)

    return pl.pallas_call(
        body, grid=(t // rt,),
        in_specs=[pl.BlockSpec((rt, z.shape[1]), lambda i: (i, 0)), pl.BlockSpec((rt, GATE_PAD), lambda i: (i, 0)),
                  _resident(gate_w.shape), _resident((1, dk)), _resident((1, hv))],
        out_specs=[pl.BlockSpec((rt, dv), lambda i: (i, 0)), pl.BlockSpec((per, HEADS, hv, hk), lambda i: (i, 0, 0, 0))],
        out_shape=[_sds((t, dv), BF16), _sds((t // CHUNK, HEADS, hv, hk), F32)],
        scratch_shapes=[pltpu.VMEM((HEADS, hv, hk), F32), pltpu.VMEM((rt, dk), F32)],
        compiler_params=_cparams("arbitrary"), name=name)(z, r, gate_w, gate_b, head_g)


def _gla_mid_bwd(z, r, dog, states, gate_w, gate_b, head_g, *, after=None, name):
    t = z.shape[0]
    dk = gate_b.shape[1]
    hv = head_g.shape[1]
    hk = dk // HEADS
    dv = hv * HEADS
    rt = _row_tile(t, 320)
    ntile = t // rt
    per = rt // CHUNK
    scale = hk ** -0.5
    dep_specs, deps = _dep_specs(after)

    def body(*refs):
        (z_ref, r_ref, do_ref, st_ref, stp_ref, gw_ref, gb_ref, hg_ref,
         dz_ref, dr_ref, dgw_ref, dgb_ref, dhg_ref, ds_ref, la_ref, dla_ref) = refs[len(deps):]
        step = pl.program_id(0)
        tile = ntile - 1 - step

        @pl.when(step == 0)
        def _():
            ds_ref[...] = jnp.zeros_like(ds_ref)
            dgw_ref[...] = jnp.zeros_like(dgw_ref)
            dgb_ref[...] = jnp.zeros_like(dgb_ref)
            dhg_ref[...] = jnp.zeros_like(dhg_ref)

        rows_id = _row_ids(tile, rt)
        gp, la = _log_decay(r_ref, gw_ref, gb_ref, rows_id)
        la_ref[...] = la
        tri, tri_strict = _tri(False), _tri(True)
        keep = jnp.where(tile > 0, 1.0, 0.0)

        def chunk(cc, carry):
            c = per - 1 - cc
            rows = pl.ds(pl.multiple_of(c * CHUNK, CHUNK), CHUNK)
            la_c = la_ref[rows, :]
            cum = jnp.dot(tri, la_c, precision=HI, preferred_element_type=F32)
            tot = _rowsum(la_c)
            dec = jnp.exp(tot - cum)
            etot = jnp.exp(tot)
            inside = jnp.where(c > 0, 1.0, 0.0)
            for hd in range(HEADS):
                ks = slice(hd * hk, (hd + 1) * hk)
                q = (z_ref[rows, hd * hk:(hd + 1) * hk] * scale).astype(BF16)
                k = z_ref[rows, dk + hd * hk:dk + (hd + 1) * hk]
                kd = k * dec[:, ks]
                v = z_ref[rows, 2 * dk + hd * hv:2 * dk + (hd + 1) * hv].astype(BF16)
                g = z_ref[rows, 2 * dk + dv + hd * hv:2 * dk + dv + (hd + 1) * hv]
                s_now = st_ref[c, hd]
                s_prev = inside * st_ref[jnp.maximum(c - 1, 0), hd] + (1.0 - inside) * keep * stp_ref[0, hd]
                s_b = s_now.astype(BF16)
                o = _dot_nt(q, s_b)
                rstd = lax.rsqrt(jnp.mean(o * o, axis=-1, keepdims=True) + EPS)
                oh = o * rstd
                sg = _sigmoid(g)
                d_og = do_ref[rows, hd * hv:(hd + 1) * hv]
                dz_ref[rows, 2 * dk + dv + hd * hv:2 * dk + dv + (hd + 1) * hv] = (
                    d_og * oh * hg_ref[...] * (sg * (1.0 + g * (1.0 - sg)))).astype(BF16)
                don = d_og * (g * sg)
                dhg_ref[...] += _rowsum(don * oh)
                doh = don * hg_ref[...]
                d_o = (rstd * (doh - oh * jnp.mean(doh * oh, axis=-1, keepdims=True))).astype(BF16)
                dz_ref[rows, hd * hk:(hd + 1) * hk] = (_dot(d_o, s_b) * scale).astype(BF16)
                ds_t = ds_ref[hd] + _dot_tn(d_o, q)
                ds_b = ds_t.astype(BF16)
                dkd = _dot(v, ds_b)
                dz_ref[rows, 2 * dk + hd * hv:2 * dk + (hd + 1) * hv] = _dot_nt(kd.astype(BF16), ds_b).astype(BF16)
                dtot = etot[:, ks] * _rowsum(ds_t * s_prev)
                ds_ref[hd] = ds_t * etot[:, ks]
                dz_ref[rows, dk + hd * hk:dk + (hd + 1) * hk] = (dkd * dec[:, ks]).astype(BF16)
                e = dkd * kd
                dla_ref[rows, ks] = dtot + jnp.dot(tri_strict, e, precision=HI, preferred_element_type=F32)
            return carry

        lax.fori_loop(0, per, chunk, 0, unroll=True)
        dla = jnp.where(rows_id >= PAD_ROWS, dla_ref[...], 0.0)
        dgp = dla * (1.0 / GATE_NORM) * (1.0 - _sigmoid(gp))
        dgb_ref[...] += _rowsum(dgp)
        dgp_b = dgp.astype(BF16)
        dgw_ref[...] += _dot_tn(r_ref[...].astype(BF16), dgp_b)
        dr_ref[...] = _dot_nt(dgp_b, gw_ref[...]).astype(BF16)

    back = lambda i: (ntile - 1 - i, 0)
    const2 = lambda i: (0, 0)
    return pl.pallas_call(
        body, grid=(ntile,),
        in_specs=dep_specs + [
                  pl.BlockSpec((rt, z.shape[1]), back), pl.BlockSpec((rt, GATE_PAD), back), pl.BlockSpec((rt, dv), back),
                  pl.BlockSpec((per, HEADS, hv, hk), lambda i: (ntile - 1 - i, 0, 0, 0)),
                  pl.BlockSpec((1, HEADS, hv, hk), lambda i: (jnp.maximum((ntile - 1 - i) * per - 1, 0), 0, 0, 0)),
                  _resident(gate_w.shape), _resident((1, dk)), _resident((1, hv))],
        out_specs=[pl.BlockSpec((rt, z.shape[1]), back), pl.BlockSpec((rt, GATE_PAD), back),
                   pl.BlockSpec(gate_w.shape, const2), pl.BlockSpec((1, dk), const2), pl.BlockSpec((1, hv), const2)],
        out_shape=[_sds(z.shape, BF16), _sds((t, GATE_PAD), BF16), _sds(gate_w.shape, F32), _sds((1, dk), F32),
                   _sds((1, hv), F32)],
        scratch_shapes=[pltpu.VMEM((HEADS, hv, hk), F32), pltpu.VMEM((rt, dk), F32), pltpu.VMEM((rt, dk), F32)],
        compiler_params=_cparams("arbitrary"), name=name)(*deps, z, r, dog, states, states, gate_w, gate_b, head_g)


def _head(h, gain, target, *, name):
    t, d = h.shape
    rt = _row_tile(t, 832)

    def body(h_ref, g_ref, t_ref, dh_ref, loss_ref, dg_ref):
        i = pl.program_id(0)

        @pl.when(i == 0)
        def _():
            loss_ref[...] = jnp.zeros_like(loss_ref)
            dg_ref[...] = jnp.zeros_like(dg_ref)

        hv = h_ref[...]
        rstd = lax.rsqrt(jnp.mean(hv * hv, axis=-1, keepdims=True) + EPS)
        xh = hv * rstd
        err = jnp.where(_row_ids(i, rt) >= CHUNK, xh * g_ref[...] - t_ref[...], 0.0)
        loss_ref[...] += (0.5 / d) * jnp.sum(err * err)
        dy = err * (1.0 / d)
        dg_ref[...] += _rowsum(dy * xh)
        dxh = dy * g_ref[...]
        dh_ref[...] = rstd * (dxh - xh * jnp.mean(dxh * xh, axis=-1, keepdims=True))

    return pl.pallas_call(
        body, grid=(t // rt,),
        in_specs=[pl.BlockSpec((rt, d), lambda i: (i, 0)), _resident((1, d)), pl.BlockSpec((rt, d), lambda i: (i, 0))],
        out_specs=[pl.BlockSpec((rt, d), lambda i: (i, 0)), pl.BlockSpec((8, LANE), lambda i: (0, 0)),
                   pl.BlockSpec((1, d), lambda i: (0, 0))],
        out_shape=[_sds((t, d), F32), _sds((8, LANE), F32), _sds((1, d), F32)],
        compiler_params=_cparams("arbitrary"), name=name)(h, gain, target)


def _adamw_math(w, g, m, v):
    m = ADAM_B1 * m + (1.0 - ADAM_B1) * g
    v = ADAM_B2 * v + (1.0 - ADAM_B2) * (g * g)
    m_hat = m / (1.0 - ADAM_B1 ** ADAM_STEP)
    v_hat = v / (1.0 - ADAM_B2 ** ADAM_STEP)
    return -ADAM_LR * (m_hat / (jnp.sqrt(v_hat) + ADAM_EPS) + ADAM_WD * w), m, v


N_CHIP = N_DEV // 2
BLOCK_ELEMS = 128 * 1024


def _my_slot():
    return 4 * lax.axis_index("x") + 2 * lax.axis_index("y") + lax.axis_index("c")


def _row_block(r, c):
    cap = max(8, BLOCK_ELEMS // (-(-c // LANE) * LANE))
    return max([b for b in range(8, r + 1, 8) if r % b == 0 and b <= cap] or [r])


def _blocks(r, c):
    rb = _row_block(r, c)
    if rb < r or r * c <= BLOCK_ELEMS:
        return rb, c
    return r, max([b for b in (512, 256, LANE) if c % b == 0 and r * b <= BLOCK_ELEMS] or [c])


def _reduce_adam(parts, w, m, v, *, after=None, name):
    nl, r, c = w.shape
    rb, cb = _blocks(r, c)
    dep_specs, deps = _dep_specs(after)

    def body(*refs):
        me = refs[0][0]
        refs = refs[1 + len(deps):]
        p_refs = refs[:2 * nl]
        w_ref, m_ref, v_ref, g_out, d_out, m_out, v_out = refs[2 * nl:]
        layer = pl.program_id(0)
        for li in range(nl):
            @pl.when(layer == li)
            def _(li=li):
                own_ref, land_ref = p_refs[2 * li], p_refs[2 * li + 1]
                mine = own_ref[...].astype(F32)
                g = None
                for dev in range(N_DEV):
                    term = jnp.where(me == dev, mine, land_ref[dev].astype(F32))
                    g = term if g is None else g + term
                g_out[...] = g
                d_out[...], m_out[...], v_out[...] = _adamw_math(w_ref[...], g, m_ref[...], v_ref[...])

    blk = pl.BlockSpec((None, rb, cb), lambda l, i, j, me: (l, i, j))
    p_specs = []
    for li in range(nl):
        p_specs += [
            pl.BlockSpec((None, rb, cb), lambda l, i, j, me, li=li: (me[0], jnp.where(l == li, i, 0), jnp.where(l == li, j, 0))),
            pl.BlockSpec((N_DEV, rb, cb), lambda l, i, j, me, li=li: (0, jnp.where(l == li, i, 0), jnp.where(l == li, j, 0)))]
    flat = [p for pair in parts for p in pair]
    grid_spec = pltpu.PrefetchScalarGridSpec(
        num_scalar_prefetch=1, grid=(nl, r // rb, c // cb), in_specs=dep_specs + p_specs + [blk, blk, blk],
        out_specs=[blk] * 4)
    return pl.pallas_call(
        body, grid_spec=grid_spec, out_shape=[_sds(w.shape, F32)] * 4,
        compiler_params=_cparams("arbitrary", "arbitrary", "arbitrary"), name=name)(
        _my_slot().reshape(1), *deps, *flat, w, m, v)


def _adam_small(own, landed, split, w, m, v, *, name):
    n = len(w)

    def body(*refs):
        own_refs, land_refs, w_refs, m_refs, v_refs = (refs[k * n:(k + 1) * n] for k in range(5))
        outs = refs[5 * n:]
        me = _my_slot()
        for k in range(n):
            mine = own_refs[k][me] if split[k] else own_refs[k][...]
            g = None
            for dev in range(N_DEV):
                term = jnp.where(me == dev, mine, land_refs[k][dev])
                g = term if g is None else g + term
            outs[4 * k][...] = g
            outs[4 * k + 1][...], outs[4 * k + 2][...], outs[4 * k + 3][...] = _adamw_math(
                w_refs[k][...], g, m_refs[k][...], v_refs[k][...])

    out = pl.pallas_call(body, out_shape=[_sds(a.shape, F32) for a in w for _ in range(4)],
                         compiler_params=pltpu.CompilerParams(vmem_limit_bytes=V7X_VMEM_LIMIT), name=name)(
        *own, *landed, *w, *m, *v)
    return [tuple(out[4 * k:4 * k + 4]) for k in range(n)]


_HBM = pl.BlockSpec(memory_space=pltpu.HBM)
_SEM = pl.BlockSpec(memory_space=pltpu.SEMAPHORE)
_DATAFLOW = pltpu.SideEffectType.DATAFLOW_SIDE_EFFECTING


def _plan_to_all(src, land):
    x, y, c = lax.axis_index("x"), lax.axis_index("y"), lax.axis_index("c")
    return [(src, land.at[_my_slot()], (x ^ ((d >> 2) & 1), y ^ ((d >> 1) & 1), c ^ (d & 1))) for d in range(1, N_DEV)]


def _plan_split_to_all(src, land):
    x, y, c = lax.axis_index("x"), lax.axis_index("y"), lax.axis_index("c")
    peers = [(x ^ ((d >> 2) & 1), y ^ ((d >> 1) & 1), c ^ (d & 1)) for d in range(1, N_DEV)]
    return [(src.at[4 * px + 2 * py + pc], land.at[_my_slot()], (px, py, pc)) for px, py, pc in peers]


_PLAN_COPIES = {_plan_to_all: N_DEV - 1, _plan_split_to_all: N_DEV - 1}


def _plans(plan, n):
    return list(plan) if isinstance(plan, (list, tuple)) else [plan] * n


def _exchange_copies(plan, ins, lands, send, recv):
    copies, sem = [], 0
    for p, src, land in zip(_plans(plan, len(lands)), ins, lands):
        for s, dst, dev in p(src, land):
            copies.append(pltpu.make_async_remote_copy(
                src_ref=s, dst_ref=dst, send_sem=send.at[sem], recv_sem=recv.at[sem],
                device_id=dev, device_id_type=pl.DeviceIdType.MESH))
            sem += 1
    return copies


def _place_own(a, dtype, *, after=None, name):
    r, c = a.shape
    rb = _row_block(r, c)
    dep_specs, deps = _dep_specs(after)

    def body(*refs):
        a_ref, o_ref = refs[1 + len(deps):]
        o_ref[...] = a_ref[...].astype(dtype)

    grid_spec = pltpu.PrefetchScalarGridSpec(
        num_scalar_prefetch=1, grid=(r // rb,), in_specs=dep_specs + [pl.BlockSpec((rb, c), lambda i, me: (i, 0))],
        out_specs=pl.BlockSpec((None, rb, c), lambda i, me: (me[0], i, 0)))
    return pl.pallas_call(body, grid_spec=grid_spec, out_shape=_sds((N_DEV, r, c), dtype),
                          compiler_params=_cparams("arbitrary"), name=name)(_my_slot().reshape(1), *deps, a)


def _plan_gather_first(land, _):
    x, y, c = lax.axis_index("x"), lax.axis_index("y"), lax.axis_index("c")
    mine = land.at[_my_slot()]
    return [(mine, mine, (x, y, 1 - c))] + [(mine, mine, (x ^ (d >> 1), y ^ (d & 1), c)) for d in range(1, N_CHIP)]


def _plan_gather_relay(land, _):
    x, y, c = lax.axis_index("x"), lax.axis_index("y"), lax.axis_index("c")
    slots = [land.at[4 * (x ^ (d >> 1)) + 2 * (y ^ (d & 1)) + c] for d in range(1, N_CHIP)]
    return [(s, s, (x, y, 1 - c)) for s in slots]


_PLAN_COPIES[_plan_gather_first] = N_CHIP
_PLAN_COPIES[_plan_gather_relay] = N_CHIP - 1


def _exchange_start(plan, arrs, lands, *, after=None, name):
    bufs = list(lands) if arrs is None else list(arrs) + list(lands)
    n, nb = len(lands), len(bufs)
    nsem = sum(_PLAN_COPIES[p] for p in _plans(plan, n))
    dep_specs, deps = _dep_specs(after)

    def body(*refs):
        ins, land_refs = refs[:n], refs[nb - n:nb]
        send, recv = refs[nb + len(deps)], refs[nb + len(deps) + 1]
        for cp in _exchange_copies(plan, ins, land_refs, send, recv):
            cp.start()
        refs[-1][...] = jnp.zeros_like(refs[-1])

    out = pl.pallas_call(
        body, name=name,
        out_shape=(pltpu.SemaphoreType.DMA((nsem,)), pltpu.SemaphoreType.DMA((nsem,)),
                   *[pltpu.HBM(a.shape, a.dtype) for a in bufs], _sds((8, LANE), F32)),
        in_specs=[_HBM] * nb + dep_specs,
        out_specs=(_SEM, _SEM, *([_HBM] * nb), pl.BlockSpec(memory_space=pltpu.VMEM)),
        input_output_aliases={i: 2 + i for i in range(nb)},
        compiler_params=pltpu.CompilerParams(has_side_effects=_DATAFLOW),
    )(*[pltpu.with_memory_space_constraint(a, pltpu.HBM) for a in bufs], *deps)
    return (plan, n, out[0], out[1], list(out[2:2 + nb])), out[-1]


def _exchange_wait(state, after, *, name):
    plan, n, send_sem, recv_sem, bufs = state
    nb = len(bufs)
    after = list(after) if isinstance(after, (list, tuple)) else [after]

    def body(*refs):
        ins, land_refs, send, recv = refs[:n], refs[nb - n:nb], refs[nb], refs[nb + 1]
        for cp in _exchange_copies(plan, ins, land_refs, send, recv):
            cp.wait_send()
            cp.wait_recv()

    out = pl.pallas_call(
        body, name=name, out_shape=[pltpu.HBM(a.shape, a.dtype) for a in bufs],
        in_specs=[_HBM] * nb + [_SEM, _SEM] + [pl.BlockSpec(memory_space=pl.ANY)] * len(after), out_specs=[_HBM] * nb,
        input_output_aliases={i: i for i in range(nb)},
        compiler_params=pltpu.CompilerParams(has_side_effects=_DATAFLOW),
    )(*bufs, send_sem, recv_sem, *after)
    return list(out[:n]), list(out[nb - n:])


def _dep_specs(after):
    return ([], []) if after is None else ([pl.BlockSpec(memory_space=pl.ANY)], [after])


def _undo_column_split(g):
    return jnp.transpose(g, (1, 0, 2)).reshape(g.shape[1], N_DEV * g.shape[2])


def _column_split(a):
    r, c = a.shape
    return jnp.transpose(a.reshape(r, N_DEV, c // N_DEV), (1, 0, 2))


class _WholeWeights:
    def __init__(self, groups):
        self.groups = groups
        self.grads = {}

    def fetch(self, group, after):
        return self.groups[group]

    def emit(self, group, grads):
        self.grads.update(grads)
        return None

    def poll(self, after):
        return None


def _local_step(x, target, replicated, src):
    d = x.shape[1]
    mix_g, ffn_g = replicated["mix_g"], replicated["ffn_g"]
    h0 = jnp.concatenate([jnp.zeros((CHUNK, d), F32), x], axis=0)
    tgt = jnp.concatenate([jnp.zeros((CHUNK, d), F32), target], axis=0)
    cp = src.fetch("cp", [h0, tgt])
    h0 = lax.dynamic_update_slice(h0, cp["meta"], (PAD_ROWS, 0))
    cp_mid = (cp["conv_w"], replicated["conv_b"], replicated["ln_g"], replicated["ln_b"], replicated["pool_w"],
              replicated["pool_scale"])

    z0, u0 = _linear_fwd(h0, cp["cp_w_in_t"], gain=mix_g[0:1], w_t=True, name="cp_in")
    cat = _cp_mid_fwd(z0, *cp_mid, name="cp_mid")
    h1 = _linear_fwd(cat, cp["cp_w_out"], res=h0, name="cp_out")
    ffn0 = src.fetch("ffn0", h1)
    h2, uf0, rf0 = _ffn_fwd(h1, ffn_g[0:1], ffn0["w1"], ffn0["w2"], name="ffn0")
    gla = src.fetch("gla", h2)
    gla_mid = (gla["gate_w"], gla["gate_b"], gla["head_g"])
    z1, u1 = _linear_fwd(h2, gla["gla_w_qkvg_t"], gain=mix_g[1:2], w_t=True, name="gla_in")
    r1 = _linear_fwd(u1, gla["gla_w_r_t"], w_t=True, name="gla_in_r")
    og, states = _gla_mid_fwd(z1, r1, *gla_mid, name="gla_mid")
    h3 = _linear_fwd(og, gla["gla_w_out"], res=h2, name="gla_out")
    ffn1 = src.fetch("ffn1", h3)
    h4, uf1, rf1 = _ffn_fwd(h3, ffn_g[1:2], ffn1["w1"], ffn1["w2"], name="ffn1")
    dh4, loss, d_final_g = _head(h4, replicated["final_g"], tgt, name="head")

    dh3, dhh1, dob1, dffn_g1 = _ffn_bwd_x(h3, dh4, ffn_g[1:2], rf1, ffn1["w1"], ffn1["w2"], name="ffn1_bwd_x")
    dw1_1, dw2_1 = _ffn_bwd_w(uf1, dhh1, rf1, dob1, name="ffn1_bwd_w")
    sent = src.emit("ffn1", dict(w1=dw1_1, w2=dw2_1))
    dog = _linear_bwd_x([dh3], [gla["gla_w_out"]], after=sent, name="gla_out_dx")
    d_gla_w_out = _linear_bwd_w(og, dh3, name="gla_out_dw")
    sent = src.poll(d_gla_w_out)
    dz1, dr1, d_gate_w, d_gate_b, d_head_g = _gla_mid_bwd(z1, r1, dog, states, *gla_mid, after=sent, name="gla_mid_bwd")
    dh2, dmix_g1 = _linear_bwd_x([dz1, dr1], [gla["gla_w_qkvg_t"], gla["gla_w_r_t"]], w_t=True,
                                 norm=(h2, mix_g[1:2], dh3), name="gla_in_dx")
    d_gla_w_qkvg_t = _linear_bwd_w(dz1, u1, name="gla_in_dw")
    d_gla_w_r_t = _linear_bwd_w(dr1, u1, name="gla_in_r_dw")
    sent = src.emit("gla", dict(gla_w_qkvg_t=d_gla_w_qkvg_t, gla_w_r_t=d_gla_w_r_t, gla_w_out=d_gla_w_out))
    dh1, dhh0, dob0, dffn_g0 = _ffn_bwd_x(h1, dh2, ffn_g[0:1], rf0, ffn0["w1"], ffn0["w2"], after=sent, name="ffn0_bwd_x")
    dw1_0, dw2_0 = _ffn_bwd_w(uf0, dhh0, rf0, dob0, name="ffn0_bwd_w")
    src.poll(dw1_0)
    sent = src.emit("ffn0", dict(w1=dw1_0, w2=dw2_0))
    dcat = _linear_bwd_x([dh1], [cp["cp_w_out"]], after=sent, name="cp_out_dx")
    d_cp_w_out = _linear_bwd_w(cat, dh1, name="cp_out_dw")
    src.emit("cp_out", dict(cp_w_out=d_cp_w_out))
    sent = src.poll(d_cp_w_out)
    dz0, d_conv_w, d_conv_b, d_ln_g, d_ln_b, d_pool_w, d_pool_scale = _cp_mid_bwd(z0, dcat, *cp_mid, after=sent,
                                                                                 name="cp_mid_bwd")
    dh0, dmix_g0 = _linear_bwd_x([dz0], [cp["cp_w_in_t"]], w_t=True, norm=(h0, mix_g[0:1], dh1), name="cp_in_dx")
    d_cp_w_in_t = _linear_bwd_w(dz0, u0, name="cp_in_dw")

    small = dict(
        mix_g=jnp.concatenate([dmix_g0, dmix_g1]), ffn_g=jnp.concatenate([dffn_g0, dffn_g1]), conv_b=d_conv_b, ln_g=d_ln_g,
        ln_b=d_ln_b, pool_w=d_pool_w, pool_scale=d_pool_scale, final_g=d_final_g, meta=dh0[PAD_ROWS:CHUNK], conv_w=d_conv_w,
        gate_w=d_gate_w, gate_b=d_gate_b, head_g=d_head_g)
    src.emit("cp", dict(cp_w_in_t=d_cp_w_in_t, small=small))
    return loss, dh0[CHUNK:], small


_REPLICATED = ("mix_norm_g", "ffn_norm_g", "cp_conv_b", "cp_ln_g", "cp_ln_b", "cp_pool_w", "cp_pool_scale", "final_norm_g")
_SMALL_SHARDED = ("meta_tokens", "cp_conv_w", "gla_gate_w2", "gla_gate_b", "gla_head_g")
_NAMES = ("meta_tokens", "mix_norm_g", "ffn_norm_g", "ffn_w1", "ffn_w2", "cp_w_in", "cp_conv_w", "cp_conv_b", "cp_ln_g",
          "cp_ln_b", "cp_pool_w", "cp_pool_scale", "cp_w_out", "gla_w_in", "gla_gate_w2", "gla_gate_b", "gla_head_g",
          "gla_w_out", "final_norm_g")
_SMALL_GRADS = ("mix_g", "ffn_g", "conv_b", "ln_g", "ln_b", "pool_w", "pool_scale", "final_g", "meta", "conv_w", "gate_w",
                "gate_b", "head_g")
_GROUPS = ("cp", "ffn0", "gla", "ffn1")


class _Exchanges:
    def __init__(self, w, d):
        self.d = d
        small = [w[n].reshape(w[n].shape[-2:]) for n in _SMALL_SHARDED]
        self.small_shard_shapes = [w[n].shape for n in _SMALL_SHARDED]
        shards = dict(
            cp=[(w["cp_w_in"][0].T, BF16), (w["cp_w_out"][0], BF16)] + [(a, F32) for a in small],
            ffn0=[(w["ffn_w1"][0], BF16), (w["ffn_w2"][0], BF16)],
            gla=[(w["gla_w_in"][0].T, BF16), (w["gla_w_out"][0], BF16)],
            ffn1=[(w["ffn_w1"][1], BF16), (w["ffn_w2"][1], BF16)])
        self.gathers = {}
        self.sent = {}
        token = None
        for group in _GROUPS:
            lands = [_place_own(a, dtype, after=token, name=f"place_w_{group}_{k}")
                     for k, (a, dtype) in enumerate(shards[group])]
            self.gathers[group], token = _exchange_start(_plan_gather_first, None, lands, after=token,
                                                         name=f"start_w_{group}")
        self.token = token

    def fetch(self, group, after):
        d = self.d
        after = (list(after) if isinstance(after, (list, tuple)) else [after]) + [self.token]
        _, lands = _exchange_wait(self.gathers[group], after, name=f"wait_w_{group}")
        relay, token = _exchange_start(_plan_gather_relay, None, lands, name=f"relay_w_{group}")
        _, got = _exchange_wait(relay, token, name=f"wait_relay_w_{group}")
        if group in ("ffn0", "ffn1"):
            return dict(w1=got[0], w2=got[1])
        if group == "gla":
            qkvg = 3 * d
            w_in_t = got[0].reshape(-1, d)
            return dict(gla_w_qkvg_t=w_in_t[:qkvg], gla_w_r_t=jnp.pad(w_in_t[qkvg:], ((0, GATE_PAD - GATE_RANK), (0, 0))),
                        gla_w_out=got[1].reshape(d, d), gate_w=self.gate_w, gate_b=self.gate_b, head_g=self.head_g)
        meta, conv_w, gate_w, self.gate_b, self.head_g = [_undo_column_split(a) for a in got[2:]]
        self.gate_w = jnp.pad(gate_w, ((0, GATE_PAD - GATE_RANK), (0, 0))).astype(BF16)
        return dict(cp_w_in_t=got[0].reshape(-1, d), cp_w_out=got[1].reshape(d, d), meta=meta,
                    conv_w=jnp.pad(conv_w, ((0, 1), (0, 0))))

    def emit(self, group, g):
        d = self.d
        if group in ("ffn0", "ffn1"):
            arrs = [g["w1"], g["w2"]]
        elif group == "gla":
            w_in_t = jnp.concatenate([g["gla_w_qkvg_t"], g["gla_w_r_t"][:GATE_RANK]], axis=0)
            arrs = [w_in_t.reshape(N_DEV, -1, d), g["gla_w_out"].reshape(N_DEV, d // N_DEV, d)]
        elif group == "cp_out":
            arrs = [g["cp_w_out"].reshape(N_DEV, d // N_DEV, d)]
        else:
            s = dict(g["small"])
            s.update(pool_w=s["pool_w"][None], conv_w=s["conv_w"][:CONV_WIDTH], gate_w=s["gate_w"][:GATE_RANK])
            own = [s[n] for n in _SMALL_GRADS[:len(_REPLICATED)]]
            own += [_column_split(s[n]).reshape((N_DEV,) + shape)
                    for n, shape in zip(_SMALL_GRADS[len(_REPLICATED):], self.small_shard_shapes)]
            plans = [_plan_to_all] * len(_REPLICATED) + [_plan_split_to_all] * len(_SMALL_SHARDED)
            lands = [lax.empty((N_DEV,) + a.shape, F32) for a in own[:len(_REPLICATED)]]
            lands += [lax.empty(a.shape, F32) for a in own[len(_REPLICATED):]]
            self.small_sent, self.token = _exchange_start(plans, own, lands, after=self.token, name="start_g_small")
            arrs = [g["cp_w_in_t"].reshape(N_DEV, -1, d)]
        self.sent[group], self.token = _exchange_start(_plan_split_to_all, arrs, [lax.empty(a.shape, a.dtype) for a in arrs],
                                                       after=self.token, name=f"start_g_{group}")
        return self.token

    def poll(self, after):
        return self.token

    def finish(self, w, mom, var):
        out = {}
        after = self.token

        def landed(group):
            own, got = _exchange_wait(self.sent[group], after, name=f"wait_g_{group}")
            return list(zip(own, got))

        def adam(n, parts, behind=None, transposed=False):
            flip = (lambda a: jnp.transpose(a, (0, 2, 1))) if transposed else (lambda a: a)
            res = _reduce_adam(parts, flip(w[n]), flip(mom[n]), flip(var[n]), after=behind, name=f"adam_{n}")
            out[n] = tuple(flip(a) for a in res)
            return res[0]

        ffn1 = landed("ffn1")
        after = ffn1[0][1]
        gla = landed("gla")
        after = gla[0][1]
        ffn0 = landed("ffn0")
        after = adam("ffn_w1", [ffn0[0], ffn1[0]])
        after = adam("ffn_w2", [ffn0[1], ffn1[1]], after)
        after = adam("gla_w_in", [gla[0]], after, transposed=True)
        after = adam("gla_w_out", [gla[1]], after)
        small_own, small_landed = _exchange_wait(self.small_sent, after, name="wait_g_small")
        names = _REPLICATED + _SMALL_SHARDED
        split = [False] * len(_REPLICATED) + [True] * len(_SMALL_SHARDED)
        small_new = _adam_small(small_own, small_landed, split, [w[n] for n in names], [mom[n] for n in names],
                                [var[n] for n in names], name="adam_small")
        out.update(zip(names, small_new))

        after = small_new[0][0]
        cp_out = landed("cp_out")
        after = adam("cp_w_out", [cp_out[0]])
        cp = landed("cp")
        adam("cp_w_in", [cp[0]], transposed=True)
        return out


def kernel(x, meta_tokens, mix_norm_g, ffn_norm_g, ffn_w1, ffn_w2, cp_w_in, cp_conv_w, cp_conv_b, cp_ln_g, cp_ln_b, cp_pool_w, cp_pool_scale, cp_w_out, gla_w_in, gla_gate_w2, gla_gate_b, gla_head_g, gla_w_out, final_norm_g, loss_target, m_meta_tokens, m_mix_norm_g, m_ffn_norm_g, m_ffn_w1, m_ffn_w2, m_cp_w_in, m_cp_conv_w, m_cp_conv_b, m_cp_ln_g, m_cp_ln_b, m_cp_pool_w, m_cp_pool_scale, m_cp_w_out, m_gla_w_in, m_gla_gate_w2, m_gla_gate_b, m_gla_head_g, m_gla_w_out, m_final_norm_g, v_meta_tokens, v_mix_norm_g, v_ffn_norm_g, v_ffn_w1, v_ffn_w2, v_cp_w_in, v_cp_conv_w, v_cp_conv_b, v_cp_ln_g, v_cp_ln_b, v_cp_pool_w, v_cp_pool_scale, v_cp_w_out, v_gla_w_in, v_gla_gate_w2, v_gla_gate_b, v_gla_head_g, v_gla_w_out, v_final_norm_g):
    w = dict(meta_tokens=meta_tokens, mix_norm_g=mix_norm_g, ffn_norm_g=ffn_norm_g, ffn_w1=ffn_w1, ffn_w2=ffn_w2,
             cp_w_in=cp_w_in, cp_conv_w=cp_conv_w, cp_conv_b=cp_conv_b, cp_ln_g=cp_ln_g, cp_ln_b=cp_ln_b,
             cp_pool_w=cp_pool_w, cp_pool_scale=cp_pool_scale, cp_w_out=cp_w_out, gla_w_in=gla_w_in,
             gla_gate_w2=gla_gate_w2, gla_gate_b=gla_gate_b, gla_head_g=gla_head_g, gla_w_out=gla_w_out,
             final_norm_g=final_norm_g.reshape(1, -1))
    mom = dict(meta_tokens=m_meta_tokens, mix_norm_g=m_mix_norm_g, ffn_norm_g=m_ffn_norm_g, ffn_w1=m_ffn_w1, ffn_w2=m_ffn_w2,
               cp_w_in=m_cp_w_in, cp_conv_w=m_cp_conv_w, cp_conv_b=m_cp_conv_b, cp_ln_g=m_cp_ln_g, cp_ln_b=m_cp_ln_b,
               cp_pool_w=m_cp_pool_w, cp_pool_scale=m_cp_pool_scale, cp_w_out=m_cp_w_out, gla_w_in=m_gla_w_in,
               gla_gate_w2=m_gla_gate_w2, gla_gate_b=m_gla_gate_b, gla_head_g=m_gla_head_g, gla_w_out=m_gla_w_out,
               final_norm_g=m_final_norm_g.reshape(1, -1))
    var = dict(meta_tokens=v_meta_tokens, mix_norm_g=v_mix_norm_g, ffn_norm_g=v_ffn_norm_g, ffn_w1=v_ffn_w1, ffn_w2=v_ffn_w2,
               cp_w_in=v_cp_w_in, cp_conv_w=v_cp_conv_w, cp_conv_b=v_cp_conv_b, cp_ln_g=v_cp_ln_g, cp_ln_b=v_cp_ln_b,
               cp_pool_w=v_cp_pool_w, cp_pool_scale=v_cp_pool_scale, cp_w_out=v_cp_w_out, gla_w_in=v_gla_w_in,
               gla_gate_w2=v_gla_gate_w2, gla_gate_b=v_gla_gate_b, gla_head_g=v_gla_head_g, gla_w_out=v_gla_w_out,
               final_norm_g=v_final_norm_g.reshape(1, -1))
    d = x.shape[-1]
    replicated = dict(mix_g=w["mix_norm_g"], ffn_g=w["ffn_norm_g"], conv_b=w["cp_conv_b"], ln_g=w["cp_ln_g"],
                      ln_b=w["cp_ln_b"], pool_w=w["cp_pool_w"][0].astype(BF16), pool_scale=w["cp_pool_scale"],
                      final_g=w["final_norm_g"])
    exchanges = _Exchanges(w, d)
    loss_blk, grad_x, _ = _local_step(x[0], loss_target[0], replicated, exchanges)
    loss = lax.psum(loss_blk[0, 0], ("x", "y", "c"))
    out = exchanges.finish(w, mom, var)

    def leaf(n, k):
        a = out[n][k]
        return a.reshape(-1) if n == "final_norm_g" else a

    return (loss, grad_x[None], *[leaf(n, 0) for n in _NAMES], *[leaf(n, 1) for n in _NAMES],
            *[leaf(n, 2) for n in _NAMES], *[leaf(n, 3) for n in _NAMES])
```

```python
import functools

import jax
import jax.numpy as jnp
from jax import lax
from jax.experimental import pallas as pl
from jax.experimental.pallas import tpu as pltpu

F32, BF16 = jnp.float32, jnp.bfloat16
N_DEV = 8
CHUNK = 64
N_META = 16
PAD_ROWS = CHUNK - N_META
HALO = 32
EPS = 1e-5
CONV_WIDTH = 31
POOL_WINDOWS = (2, 4, 8, 16)
HEADS = 4
GATE_RANK = 16
GATE_NORM = 16.0
GATE_PAD = 128
ADAM_LR, ADAM_B1, ADAM_B2, ADAM_EPS, ADAM_WD, ADAM_STEP = 0.001, 0.9, 0.999, 1e-08, 0.01, 10
V7X_VMEM_LIMIT = 56 * 2 ** 20
LANE = 128
HI = lax.Precision.HIGHEST


def _cparams(*sem):
    return pltpu.CompilerParams(dimension_semantics=sem, vmem_limit_bytes=V7X_VMEM_LIMIT)


def _row_tile(t, cap):
    best = CHUNK
    for r in range(CHUNK, min(t, cap) + 1, CHUNK):
        if t % r == 0:
            best = r
    return best


def _resident(shape):
    return pl.BlockSpec(shape, lambda *_: (0,) * len(shape), pipeline_mode=pl.Buffered(1))


def _dot(a, b):
    return jnp.dot(a, b, preferred_element_type=F32)


def _dot_nt(a, b):
    return lax.dot_general(a, b, (((1,), (1,)), ((), ())), preferred_element_type=F32)


def _dot_tn(a, b):
    return lax.dot_general(a, b, (((0,), (0,)), ((), ())), preferred_element_type=F32)


def _rowsum(a):
    return jnp.sum(a, axis=0, keepdims=True)


def _sigmoid(a):
    return 1.0 / (1.0 + jnp.exp(-a))


def _row_ids(tile, rt):
    return tile * rt + lax.broadcasted_iota(jnp.int32, (rt, 1), 0)


def _sds(shape, dtype):
    return jax.ShapeDtypeStruct(shape, dtype)


def _linear_fwd(x, w, *, gain=None, res=None, w_t=False, out_dtype=F32, name):
    t, k = x.shape
    n = w.shape[0] if w_t else w.shape[1]
    rt = _row_tile(t, 320)

    def body(*refs):
        refs = list(refs)
        x_ref, w_ref = refs[:2]
        pos = 2
        g_ref = r_ref = u_ref = None
        if gain is not None:
            g_ref = refs[pos]
            pos += 1
        if res is not None:
            r_ref = refs[pos]
            pos += 1
        y_ref = refs[pos]
        if gain is not None:
            u_ref = refs[pos + 1]
            xv = x_ref[...]
            u = (xv * lax.rsqrt(jnp.mean(xv * xv, axis=-1, keepdims=True) + EPS) * g_ref[...]).astype(BF16)
            u_ref[...] = u
        else:
            u = x_ref[...]
        y = _dot_nt(u, w_ref[...]) if w_t else _dot(u, w_ref[...])
        if res is not None:
            y = y + r_ref[...]
        y_ref[...] = y.astype(y_ref.dtype)

    rows = lambda i: (i, 0)
    in_specs = [pl.BlockSpec((rt, k), rows), _resident(w.shape)]
    args = [x, w]
    if gain is not None:
        in_specs.append(_resident((1, k)))
        args.append(gain)
    if res is not None:
        in_specs.append(pl.BlockSpec((rt, n), rows))
        args.append(res)
    out_shape = [_sds((t, n), out_dtype)]
    out_specs = [pl.BlockSpec((rt, n), rows)]
    if gain is not None:
        out_shape.append(_sds((t, k), BF16))
        out_specs.append(pl.BlockSpec((rt, k), rows))
    out = pl.pallas_call(body, grid=(t // rt,), in_specs=in_specs, out_specs=out_specs, out_shape=out_shape,
                         compiler_params=_cparams("parallel"), name=name)(*args)
    return out if gain is not None else out[0]


def _linear_bwd_x(dys, ws, *, norm=None, w_t=False, after=None, name):
    t = dys[0].shape[0]
    k = ws[0].shape[1] if w_t else ws[0].shape[0]
    rt = _row_tile(t, 320)
    nd = len(dys)
    dep_specs, deps = _dep_specs(after)

    def body(*refs):
        refs = list(refs)[len(deps):]
        dy_refs, w_refs = refs[:nd], refs[nd:2 * nd]
        dx = None
        for dy_ref, w_ref in zip(dy_refs, w_refs):
            dy = dy_ref[...].astype(BF16)
            part = _dot(dy, w_ref[...]) if w_t else _dot_nt(dy, w_ref[...])
            dx = part if dx is None else dx + part
        if norm is None:
            refs[2 * nd][...] = dx
            return
        h_ref, g_ref, dres_ref, dh_ref, dg_ref = refs[2 * nd:]
        hv = h_ref[...]
        rstd = lax.rsqrt(jnp.mean(hv * hv, axis=-1, keepdims=True) + EPS)
        xh = hv * rstd

        @pl.when(pl.program_id(0) == 0)
        def _():
            dg_ref[...] = jnp.zeros_like(dg_ref)

        dg_ref[...] += _rowsum(dx * xh)
        dxh = dx * g_ref[...]
        dh_ref[...] = dres_ref[...] + rstd * (dxh - xh * jnp.mean(dxh * xh, axis=-1, keepdims=True))

    rows = lambda i: (i, 0)
    in_specs = [pl.BlockSpec((rt, dy.shape[1]), rows) for dy in dys] + [_resident(w.shape) for w in ws]
    args = list(dys) + list(ws)
    out_shape = [_sds((t, k), F32)]
    out_specs = [pl.BlockSpec((rt, k), rows)]
    if norm is not None:
        h, gain, dres = norm
        in_specs += [pl.BlockSpec((rt, k), rows), _resident((1, k)), pl.BlockSpec((rt, k), rows)]
        args += [h, gain, dres]
        out_shape.append(_sds((1, k), F32))
        out_specs.append(pl.BlockSpec((1, k), lambda i: (0, 0)))
    out = pl.pallas_call(body, grid=(t // rt,), in_specs=dep_specs + in_specs, out_specs=out_specs, out_shape=out_shape,
                         compiler_params=_cparams("arbitrary"), name=name)(*deps, *args)
    return out if norm is not None else out[0]


DW_ROWS = 1024


def _linear_bwd_w(x, dy, *, name):
    t, k = x.shape
    n = dy.shape[1]
    cut_k = k > n
    width = k if cut_k else n
    blk = max(c for c in (512, 384, 256, LANE) if width % c == 0)

    def body(x_ref, dy_ref, o_ref, acc):
        for c0 in range(0, t, DW_ROWS):
            rows = slice(c0, min(c0 + DW_ROWS, t))
            part = _dot_tn(x_ref[rows, :].astype(BF16), dy_ref[rows, :].astype(BF16))
            if c0 == 0:
                acc[...] = part
            else:
                acc[...] += part
        o_ref[...] = acc[...].astype(BF16)

    if cut_k:
        in_specs = [pl.BlockSpec((t, blk), lambda j: (0, j)), _resident((t, n))]
        out_specs = pl.BlockSpec((blk, n), lambda j: (j, 0))
        acc_shape = (blk, n)
    else:
        in_specs = [_resident((t, k)), pl.BlockSpec((t, blk), lambda j: (0, j))]
        out_specs = pl.BlockSpec((k, blk), lambda j: (0, j))
        acc_shape = (k, blk)
    return pl.pallas_call(
        body, grid=(width // blk,), in_specs=in_specs, out_specs=out_specs, out_shape=_sds((k, n), BF16),
        scratch_shapes=[pltpu.VMEM(acc_shape, F32)], compiler_params=_cparams("parallel"), name=name)(x, dy)


def _ffn_fwd(h, gain, w1g, w2g, *, name):
    t, d = h.shape
    f8 = w1g.shape[-1]
    rt = _row_tile(t, 832)

    def body(h_ref, g_ref, w1_ref, w2_ref, o_ref, u_ref, r_ref, acc_ref):
        j = pl.program_id(1)

        @pl.when(j == 0)
        def _():
            hv = h_ref[...]
            u_ref[...] = (hv * lax.rsqrt(jnp.mean(hv * hv, axis=-1, keepdims=True) + EPS) * g_ref[...]).astype(BF16)
            acc_ref[...] = jnp.zeros_like(acc_ref)

        a = jnp.maximum(_dot(u_ref[...], w1_ref[...]), 0.0)
        r_ref[...] = a.astype(BF16)
        acc_ref[...] += _dot((a * a).astype(BF16), w2_ref[...])

        @pl.when(j == N_DEV - 1)
        def _():
            o_ref[...] = h_ref[...] + acc_ref[...]

    return pl.pallas_call(
        body, grid=(t // rt, N_DEV),
        in_specs=[pl.BlockSpec((rt, d), lambda i, j: (i, 0)), _resident((1, d)),
                  pl.BlockSpec((None, d, f8), lambda i, j: (j, 0, 0)),
                  pl.BlockSpec((None, f8, d), lambda i, j: (j, 0, 0))],
        out_specs=[pl.BlockSpec((rt, d), lambda i, j: (i, 0)), pl.BlockSpec((rt, d), lambda i, j: (i, 0)),
                   pl.BlockSpec((rt, f8), lambda i, j: (i, j))],
        out_shape=[_sds((t, d), F32), _sds((t, d), BF16), _sds((t, N_DEV * f8), BF16)],
        scratch_shapes=[pltpu.VMEM((rt, d), F32)],
        compiler_params=_cparams("parallel", "arbitrary"), name=name)(h, gain, w1g, w2g)


def _ffn_bwd_x(h, dout, gain, r, w1g, w2g, *, after=None, name):
    t, d = h.shape
    f8 = w1g.shape[-1]
    rt = _row_tile(t, 832)
    last = N_DEV - 1
    dep_specs, deps = _dep_specs(after)

    def body(*refs):
        h_ref, do_ref, g_ref, r_ref, w1_ref, w2_ref, dh_ref, dhh_ref, dob_ref, dg_ref, du_ref = refs[len(deps):]
        i, j = pl.program_id(0), pl.program_id(1)

        @pl.when(j == 0)
        def _():
            dob_ref[...] = do_ref[...].astype(BF16)
            du_ref[...] = jnp.zeros_like(du_ref)

        dhh = (_dot_nt(dob_ref[...], w2_ref[...]) * (2.0 * r_ref[...].astype(F32))).astype(BF16)
        dhh_ref[...] = dhh
        du_ref[...] += _dot_nt(dhh, w1_ref[...])

        @pl.when(j == last)
        def _():
            @pl.when(i == 0)
            def _():
                dg_ref[...] = jnp.zeros_like(dg_ref)

            hv = h_ref[...]
            rstd = lax.rsqrt(jnp.mean(hv * hv, axis=-1, keepdims=True) + EPS)
            xh = hv * rstd
            du = du_ref[...]
            dg_ref[...] += _rowsum(du * xh)
            dxh = du * g_ref[...]
            dh_ref[...] = do_ref[...] + rstd * (dxh - xh * jnp.mean(dxh * xh, axis=-1, keepdims=True))

    rows = lambda i, j: (i, 0)
    return pl.pallas_call(
        body, grid=(t // rt, N_DEV),
        in_specs=dep_specs + [
                  pl.BlockSpec((rt, d), rows), pl.BlockSpec((rt, d), rows), _resident((1, d)),
                  pl.BlockSpec((rt, f8), lambda i, j: (i, j)),
                  pl.BlockSpec((None, d, f8), lambda i, j: (j, 0, 0)),
                  pl.BlockSpec((None, f8, d), lambda i, j: (j, 0, 0))],
        out_specs=[pl.BlockSpec((rt, d), rows), pl.BlockSpec((rt, f8), lambda i, j: (i, j)), pl.BlockSpec((rt, d), rows),
                   pl.BlockSpec((1, d), lambda i, j: (0, 0))],
        out_shape=[_sds((t, d), F32), _sds((t, N_DEV * f8), BF16), _sds((t, d), BF16), _sds((1, d), F32)],
        scratch_shapes=[pltpu.VMEM((rt, d), F32)],
        compiler_params=_cparams("arbitrary", "arbitrary"), name=name)(*deps, h, dout, gain, r, w1g, w2g)


def _ffn_bwd_w(u, dhh, r, dout_b, *, name):
    t, d = u.shape
    f8 = dhh.shape[1] // N_DEV

    def body(u_ref, dhh_ref, r_ref, dob_ref, dw1_ref, dw2_ref, acc1, acc2):
        for c0 in range(0, t, DW_ROWS):
            rows = slice(c0, min(c0 + DW_ROWS, t))
            rr = r_ref[rows, :].astype(F32)
            part1 = _dot_tn(u_ref[rows, :], dhh_ref[rows, :])
            part2 = _dot_tn((rr * rr).astype(BF16), dob_ref[rows, :])
            if c0 == 0:
                acc1[...] = part1
                acc2[...] = part2
            else:
                acc1[...] += part1
                acc2[...] += part2
        dw1_ref[...] = acc1[...].astype(BF16)
        dw2_ref[...] = acc2[...].astype(BF16)

    return pl.pallas_call(
        body, grid=(N_DEV,),
        in_specs=[_resident((t, d)), pl.BlockSpec((t, f8), lambda j: (0, j)), pl.BlockSpec((t, f8), lambda j: (0, j)),
                  _resident((t, d))],
        out_specs=[pl.BlockSpec((None, d, f8), lambda j: (j, 0, 0)), pl.BlockSpec((None, f8, d), lambda j: (j, 0, 0))],
        out_shape=[_sds((N_DEV, d, f8), BF16), _sds((N_DEV, f8, d), BF16)],
        scratch_shapes=[pltpu.VMEM((d, f8), F32), pltpu.VMEM((f8, d), F32)],
        compiler_params=_cparams("parallel"), name=name)(u, dhh, r, dout_b)


def _lane_blocks(width):
    lb = min(LANE, width)
    return [slice(s, s + lb) for s in range(0, width, lb)]


def _conv_rows(src_ref, w_ref, offset, dst_ref, nblk, width, bias_ref=None):
    def blk(rb, carry):
        base = pl.multiple_of(rb * CHUNK, CHUNK)
        for l, ls in enumerate(_lane_blocks(width)):
            acc = jnp.zeros((CHUNK, ls.stop - ls.start), F32)
            if bias_ref is not None:
                acc = acc + bias_ref[:, ls]
            for k in range(CONV_WIDTH):
                acc = acc + w_ref[k:k + 1, ls] * src_ref[l, pl.ds(base + offset(k), CHUNK), :]
            dst_ref[l, pl.ds(base, CHUNK), :] = acc
        return carry

    lax.fori_loop(0, nblk, blk, 0)


def _to_lane_blocks(ref, row0, value):
    for l, ls in enumerate(_lane_blocks(value.shape[1])):
        ref[l, row0:row0 + value.shape[0], :] = value[:, ls]


def _from_lane_blocks(ref):
    return jnp.concatenate([ref[l] for l in range(ref.shape[0])], axis=1)


def _pool_counts(rows, window):
    return jnp.clip(rows - PAD_ROWS + 1, 1, window).astype(F32)


def _trailing_sum(v, window):
    s, sh = v, 1
    while sh < window:
        s = s + pltpu.roll(s, sh, 0)
        sh *= 2
    return s


def _leading_sum(v, window):
    s, sh, n = v, 1, v.shape[0]
    while sh < window:
        s = s + pltpu.roll(s, n - sh, 0)
        sh *= 2
    return s


def _cp_mid_fwd(z, conv_w, conv_b, ln_g, ln_b, pool_w, pool_scale, *, name):
    t, ein = z.shape
    cd = conv_b.shape[1]
    pd = pool_scale.shape[1]
    pg = pd // len(POOL_WINDOWS)
    rt = _row_tile(t, 320)

    def body(z_ref, cw_ref, cb_ref, lg_ref, lb_ref, pw_ref, ps_ref, o_ref, gext, pext, conv_s):
        i = pl.program_id(0)

        @pl.when(i == 0)
        def _():
            _to_lane_blocks(gext, 0, jnp.zeros((HALO, cd), F32))
            pext[0:HALO, :] = jnp.zeros((HALO, pd), F32)

        _to_lane_blocks(gext, HALO, z_ref[:, 0:cd] * _sigmoid(z_ref[:, cd:2 * cd]))
        pext[HALO:HALO + rt, :] = z_ref[:, 2 * cd:]
        _conv_rows(gext, cw_ref, lambda k: k + HALO - (CONV_WIDTH - 1), conv_s, rt // CHUNK, cd, cb_ref)
        cv = _from_lane_blocks(conv_s)
        xc = cv - jnp.mean(cv, axis=-1, keepdims=True)
        y = xc * lax.rsqrt(jnp.mean(xc * xc, axis=-1, keepdims=True) + EPS) * lg_ref[...] + lb_ref[...]
        rows = _row_ids(i, rt)
        a = jnp.where(rows >= PAD_ROWS, y * _sigmoid(y), 0.0)
        o_ref[:, 0:cd] = a.astype(BF16)
        for gi, window in enumerate(POOL_WINDOWS):
            ls = slice(gi * pg, (gi + 1) * pg)
            v = pext[:, ls]
            tm = _trailing_sum(v, window)[HALO:] / _pool_counts(rows, window) - v[HALO:]
            p = _dot(tm.astype(BF16), pw_ref[gi]) * ps_ref[:, ls]
            o_ref[:, cd + gi * pg:cd + (gi + 1) * pg] = p.astype(BF16)
        gext[:, 0:HALO, :] = gext[:, rt:rt + HALO, :]
        pext[0:HALO, :] = pext[rt:rt + HALO, :]

    nl, lb = len(_lane_blocks(cd)), min(LANE, cd)
    return pl.pallas_call(
        body, grid=(t // rt,),
        in_specs=[pl.BlockSpec((rt, ein), lambda i: (i, 0)), _resident(conv_w.shape), _resident((1, cd)),
                  _resident((1, cd)), _resident((1, cd)), _resident(pool_w.shape), _resident((1, pd))],
        out_specs=pl.BlockSpec((rt, cd + pd), lambda i: (i, 0)), out_shape=_sds((t, cd + pd), BF16),
        scratch_shapes=[pltpu.VMEM((nl, rt + HALO, lb), F32), pltpu.VMEM((rt + HALO, pd), F32),
                        pltpu.VMEM((nl, rt, lb), F32)],
        compiler_params=_cparams("arbitrary"), name=name)(z, conv_w, conv_b, ln_g, ln_b, pool_w, pool_scale)


def _cp_mid_bwd(z, dcat, conv_w, conv_b, ln_g, ln_b, pool_w, pool_scale, *, after=None, name):
    t, ein = z.shape
    cd = conv_b.shape[1]
    pd = pool_scale.shape[1]
    pg = pd // len(POOL_WINDOWS)
    rt = _row_tile(t, 320)
    ntile = t // rt
    per = rt // CHUNK
    dep_specs, deps = _dep_specs(after)

    def body(*refs):
        (z_ref, zh_ref, dc_ref, cw_ref, cb_ref, lg_ref, lb_ref, pw_ref, ps_ref,
         dz_ref, dcw_ref, dcb_ref, dlg_ref, dlb_ref, dpw_ref, dps_ref, gext, pext, conv_s, dcv, dsp) = refs[len(deps):]
        step = pl.program_id(0)
        tile = ntile - 1 - step

        @pl.when(step == 0)
        def _():
            for ref in (dcw_ref, dcb_ref, dlg_ref, dlb_ref, dpw_ref, dps_ref):
                ref[...] = jnp.zeros_like(ref)
            _to_lane_blocks(dcv, rt, jnp.zeros((HALO, cd), F32))
            dsp[rt:rt + HALO, :] = jnp.zeros((HALO, pd), F32)

        keep = jnp.where(tile > 0, 1.0, 0.0)
        zh = zh_ref[CHUNK - HALO:CHUNK, :]
        _to_lane_blocks(gext, 0, keep * zh[:, 0:cd] * _sigmoid(zh[:, cd:2 * cd]))
        pext[0:HALO, :] = keep * zh[:, 2 * cd:]
        za = z_ref[:, 0:cd]
        sg = _sigmoid(z_ref[:, cd:2 * cd])
        _to_lane_blocks(gext, HALO, za * sg)
        pext[HALO:HALO + rt, :] = z_ref[:, 2 * cd:]
        _conv_rows(gext, cw_ref, lambda k: k + HALO - (CONV_WIDTH - 1), conv_s, per, cd, cb_ref)
        cv = _from_lane_blocks(conv_s)
        xc = cv - jnp.mean(cv, axis=-1, keepdims=True)
        rstd = lax.rsqrt(jnp.mean(xc * xc, axis=-1, keepdims=True) + EPS)
        xh = xc * rstd
        y = xh * lg_ref[...] + lb_ref[...]
        sy = _sigmoid(y)
        rows = _row_ids(tile, rt)
        da = jnp.where(rows >= PAD_ROWS, dc_ref[:, 0:cd], 0.0)
        dy = da * (sy * (1.0 + y * (1.0 - sy)))
        dlg_ref[...] += _rowsum(dy * xh)
        dlb_ref[...] += _rowsum(dy)
        dxh = dy * lg_ref[...]
        dconv = rstd * (dxh - jnp.mean(dxh, axis=-1, keepdims=True) - xh * jnp.mean(dxh * xh, axis=-1, keepdims=True))
        dcb_ref[...] += _rowsum(dconv)
        _to_lane_blocks(dcv, 0, dconv)
        for l, ls in enumerate(_lane_blocks(cd)):
            def acc_rows(rb, accs, l=l):
                base = pl.multiple_of(rb * CHUNK, CHUNK)
                d_blk = dcv[l, pl.ds(base, CHUNK), :]
                out = []
                for k in range(CONV_WIDTH):
                    prod = d_blk * gext[l, pl.ds(base + k + HALO - (CONV_WIDTH - 1), CHUNK), :]
                    part = prod[0:8]
                    for s in range(8, CHUNK, 8):
                        part = part + prod[s:s + 8]
                    out.append(accs[k] + part)
                return tuple(out)

            zero = jnp.zeros((8, ls.stop - ls.start), F32)
            accs = lax.fori_loop(0, per, acc_rows, (zero,) * CONV_WIDTH)
            for k in range(CONV_WIDTH):
                dcw_ref[k:k + 1, ls] += _rowsum(accs[k])
        _conv_rows(dcv, cw_ref, lambda k: CONV_WIDTH - 1 - k, conv_s, per, cd)
        dglu = _from_lane_blocks(conv_s)
        dz_ref[:, 0:cd] = (dglu * sg).astype(BF16)
        dz_ref[:, cd:2 * cd] = (dglu * za * sg * (1.0 - sg)).astype(BF16)
        dcv[:, rt:rt + HALO, :] = dcv[:, 0:HALO, :]
        for gi, window in enumerate(POOL_WINDOWS):
            ls = slice(gi * pg, (gi + 1) * pg)
            v = pext[:, ls]
            cnt = _pool_counts(rows, window)
            tm = (_trailing_sum(v, window)[HALO:] / cnt - v[HALO:]).astype(BF16)
            dp = dc_ref[:, cd + gi * pg:cd + (gi + 1) * pg]
            dps_ref[:, ls] += _rowsum(dp * _dot(tm, pw_ref[gi]))
            dpl = (dp * ps_ref[:, ls]).astype(BF16)
            dpw_ref[gi] += _dot_tn(tm, dpl)
            dtm = _dot_nt(dpl, pw_ref[gi])
            dsp[0:rt, ls] = dtm / cnt
            dpin = _leading_sum(dsp[:, ls], window)[0:rt] - dtm
            dz_ref[:, 2 * cd + gi * pg:2 * cd + (gi + 1) * pg] = dpin.astype(BF16)
        dsp[rt:rt + HALO, :] = dsp[0:HALO, :]

    back = lambda i: (ntile - 1 - i, 0)
    halo_idx = lambda i: (jnp.maximum((ntile - 1 - i) * per - 1, 0), 0)
    const2 = lambda i: (0, 0)
    nl, lb = len(_lane_blocks(cd)), min(LANE, cd)
    return pl.pallas_call(
        body, grid=(ntile,),
        in_specs=dep_specs + [
                  pl.BlockSpec((rt, ein), back), pl.BlockSpec((CHUNK, ein), halo_idx), pl.BlockSpec((rt, cd + pd), back),
                  _resident(conv_w.shape), _resident((1, cd)), _resident((1, cd)), _resident((1, cd)),
                  _resident(pool_w.shape), _resident((1, pd))],
        out_specs=[pl.BlockSpec((rt, ein), back), pl.BlockSpec(conv_w.shape, const2), pl.BlockSpec((1, cd), const2),
                   pl.BlockSpec((1, cd), const2), pl.BlockSpec((1, cd), const2),
                   pl.BlockSpec(pool_w.shape, lambda i: (0, 0, 0)), pl.BlockSpec((1, pd), const2)],
        out_shape=[_sds((t, ein), BF16), _sds(conv_w.shape, F32), _sds((1, cd), F32), _sds((1, cd), F32),
                   _sds((1, cd), F32), _sds(pool_w.shape, F32), _sds((1, pd), F32)],
        scratch_shapes=[pltpu.VMEM((nl, rt + HALO, lb), F32), pltpu.VMEM((rt + HALO, pd), F32), pltpu.VMEM((nl, rt, lb), F32),
                        pltpu.VMEM((nl, rt + HALO, lb), F32), pltpu.VMEM((rt + HALO, pd), F32)],
        compiler_params=_cparams("arbitrary"), name=name)(*deps, z, z, dcat, conv_w, conv_b, ln_g, ln_b, pool_w, pool_scale)


def _log_decay(r_ref, gw_ref, gb_ref, rows):
    gp = _dot(r_ref[...].astype(BF16), gw_ref[...]) + gb_ref[...]
    log_sig = jnp.minimum(gp, 0.0) - jnp.log(1.0 + jnp.exp(-jnp.abs(gp)))
    return gp, jnp.where(rows >= PAD_ROWS, log_sig / GATE_NORM, 0.0)


def _tri(strict):
    r = lax.broadcasted_iota(jnp.int32, (CHUNK, CHUNK), 0)
    c = lax.broadcasted_iota(jnp.int32, (CHUNK, CHUNK), 1)
    return jnp.where(c < r if strict else c <= r, 1.0, 0.0).astype(F32)


def _gla_mid_fwd(z, r, gate_w, gate_b, head_g, *, name):
    t = z.shape[0]
    dk = gate_b.shape[1]
    hv = head_g.shape[1]
    hk = dk // HEADS
    dv = hv * HEADS
    rt = _row_tile(t, 320)
    per = rt // CHUNK
    scale = hk ** -0.5

    def body(z_ref, r_ref, gw_ref, gb_ref, hg_ref, o_ref, st_ref, s_ref, la_ref, dec_ref):
        i = pl.program_id(0)

        @pl.when(i == 0)
        def _():
            s_ref[...] = jnp.zeros_like(s_ref)

        _, la = _log_decay(r_ref, gw_ref, gb_ref, _row_ids(i, rt))
        la_ref[...] = la
        tri = _tri(False)

        def chunk_rows(c):
            return pl.ds(pl.multiple_of(c * CHUNK, CHUNK), CHUNK)

        def decays(c, carry):
            rows = chunk_rows(c)
            la_c = la_ref[rows, :]
            cum = jnp.dot(tri, la_c, precision=HI, preferred_element_type=F32)
            dec_ref[rows, :] = jnp.exp(_rowsum(la_c) - cum)
            return carry

        def states(c, carry):
            rows = chunk_rows(c)
            etot = jnp.exp(_rowsum(la_ref[rows, :]))
            for hd in range(HEADS):
                ks = slice(hd * hk, (hd + 1) * hk)
                kd = z_ref[rows, dk + hd * hk:dk + (hd + 1) * hk] * dec_ref[rows, ks]
                v = z_ref[rows, 2 * dk + hd * hv:2 * dk + (hd + 1) * hv]
                s_new = s_ref[hd] * etot[:, ks] + _dot_tn(v.astype(BF16), kd.astype(BF16))
                s_ref[hd] = s_new
                st_ref[c, hd] = s_new
            return carry

        def outputs(c, carry):
            rows = chunk_rows(c)
            for hd in range(HEADS):
                q = z_ref[rows, hd * hk:(hd + 1) * hk] * scale
                g = z_ref[rows, 2 * dk + dv + hd * hv:2 * dk + dv + (hd + 1) * hv]
                o = _dot_nt(q.astype(BF16), st_ref[c, hd].astype(BF16))
                on = o * lax.rsqrt(jnp.mean(o * o, axis=-1, keepdims=True) + EPS) * hg_ref[...]
                o_ref[rows, hd * hv:(hd + 1) * hv] = (on * (g * _sigmoid(g))).astype(BF16)
            return carry

        lax.fori_loop(0, per, decays, 0, unroll=True)
        lax.fori_loop(0, per, states, 0, unroll=True)
        lax.fori_loop(0, per, outputs, 0, unroll=True)

    return pl.pallas_call(
        body, grid=(t // rt,),
        in_specs=[pl.BlockSpec((rt, z.shape[1]), lambda i: (i, 0)), pl.BlockSpec((rt, GATE_PAD), lambda i: (i, 0)),
                  _resident(gate_w.shape), _resident((1, dk)), _resident((1, hv))],
        out_specs=[pl.BlockSpec((rt, dv), lambda i: (i, 0)), pl.BlockSpec((per, HEADS, hv, hk), lambda i: (i, 0, 0, 0))],
        out_shape=[_sds((t, dv), BF16), _sds((t // CHUNK, HEADS, hv, hk), F32)],
        scratch_shapes=[pltpu.VMEM((HEADS, hv, hk), F32), pltpu.VMEM((rt, dk), F32), pltpu.VMEM((rt, dk), F32)],
        compiler_params=_cparams("arbitrary"), name=name)(z, r, gate_w, gate_b, head_g)


def _gla_mid_bwd(z, r, dog, states, gate_w, gate_b, head_g, *, after=None, name):
    t = z.shape[0]
    dk = gate_b.shape[1]
    hv = head_g.shape[1]
    hk = dk // HEADS
    dv = hv * HEADS
    rt = _row_tile(t, 320)
    ntile = t // rt
    per = rt // CHUNK
    scale = hk ** -0.5
    dep_specs, deps = _dep_specs(after)

    def body(*refs):
        (z_ref, r_ref, do_ref, st_ref, stp_ref, gw_ref, gb_ref, hg_ref,
         dz_ref, dr_ref, dgw_ref, dgb_ref, dhg_ref, ds_ref, la_ref, dla_ref, dec_ref, dos_ref, e_ref) = refs[len(deps):]
        step = pl.program_id(0)
        tile = ntile - 1 - step

        @pl.when(step == 0)
        def _():
            ds_ref[...] = jnp.zeros_like(ds_ref)
            dgw_ref[...] = jnp.zeros_like(dgw_ref)
            dgb_ref[...] = jnp.zeros_like(dgb_ref)
            dhg_ref[...] = jnp.zeros_like(dhg_ref)

        rows_id = _row_ids(tile, rt)
        gp, la = _log_decay(r_ref, gw_ref, gb_ref, rows_id)
        la_ref[...] = la
        tri, tri_strict = _tri(False), _tri(True)
        keep = jnp.where(tile > 0, 1.0, 0.0)

        def chunk_rows(c):
            return pl.ds(pl.multiple_of(c * CHUNK, CHUNK), CHUNK)

        def recompute(c, dhg):
            rows = chunk_rows(c)
            la_c = la_ref[rows, :]
            cum = jnp.dot(tri, la_c, precision=HI, preferred_element_type=F32)
            dec_ref[rows, :] = jnp.exp(_rowsum(la_c) - cum)
            for hd in range(HEADS):
                q = (z_ref[rows, hd * hk:(hd + 1) * hk] * scale).astype(BF16)
                g = z_ref[rows, 2 * dk + dv + hd * hv:2 * dk + dv + (hd + 1) * hv]
                s_b = st_ref[c, hd].astype(BF16)
                o = _dot_nt(q, s_b)
                rstd = lax.rsqrt(jnp.mean(o * o, axis=-1, keepdims=True) + EPS)
                oh = o * rstd
                sg = _sigmoid(g)
                d_og = do_ref[rows, hd * hv:(hd + 1) * hv]
                dz_ref[rows, 2 * dk + dv + hd * hv:2 * dk + dv + (hd + 1) * hv] = (
                    d_og * oh * hg_ref[...] * (sg * (1.0 + g * (1.0 - sg)))).astype(BF16)
                don = d_og * (g * sg)
                dhg = dhg + _rowsum(don * oh)
                doh = don * hg_ref[...]
                d_o = (rstd * (doh - oh * jnp.mean(doh * oh, axis=-1, keepdims=True))).astype(BF16)
                dos_ref[rows, hd * hv:(hd + 1) * hv] = d_o
                dz_ref[rows, hd * hk:(hd + 1) * hk] = (_dot(d_o, s_b) * scale).astype(BF16)
            return dhg

        def recurrence(cc, carry):
            c = per - 1 - cc
            rows = chunk_rows(c)
            etot = jnp.exp(_rowsum(la_ref[rows, :]))
            inside = jnp.where(c > 0, 1.0, 0.0)
            for hd in range(HEADS):
                ks = slice(hd * hk, (hd + 1) * hk)
                q = (z_ref[rows, hd * hk:(hd + 1) * hk] * scale).astype(BF16)
                dec = dec_ref[rows, ks]
                kd = z_ref[rows, dk + hd * hk:dk + (hd + 1) * hk] * dec
                v = z_ref[rows, 2 * dk + hd * hv:2 * dk + (hd + 1) * hv].astype(BF16)
                s_prev = inside * st_ref[jnp.maximum(c - 1, 0), hd] + (1.0 - inside) * keep * stp_ref[0, hd]
                ds_t = ds_ref[hd] + _dot_tn(dos_ref[rows, hd * hv:(hd + 1) * hv], q)
                ds_b = ds_t.astype(BF16)
                dkd = _dot(v, ds_b)
                dz_ref[rows, 2 * dk + hd * hv:2 * dk + (hd + 1) * hv] = _dot_nt(kd.astype(BF16), ds_b).astype(BF16)
                dtot = etot[:, ks] * _rowsum(ds_t * s_prev)
                ds_ref[hd] = ds_t * etot[:, ks]
                dz_ref[rows, dk + hd * hk:dk + (hd + 1) * hk] = (dkd * dec).astype(BF16)
                e_ref[rows, ks] = dkd * kd
                dla_ref[rows, ks] = jnp.broadcast_to(dtot, (CHUNK, hk))
            return carry

        def decay_cotangent(c, carry):
            rows = chunk_rows(c)
            dla_ref[rows, :] += jnp.dot(tri_strict, e_ref[rows, :], precision=HI, preferred_element_type=F32)
            return carry

        dhg_ref[...] += lax.fori_loop(0, per, recompute, jnp.zeros((1, hv), F32), unroll=True)
        lax.fori_loop(0, per, recurrence, 0, unroll=True)
        lax.fori_loop(0, per, decay_cotangent, 0, unroll=True)
        dla = jnp.where(rows_id >= PAD_ROWS, dla_ref[...], 0.0)
        dgp = dla * (1.0 / GATE_NORM) * (1.0 - _sigmoid(gp))
        dgb_ref[...] += _rowsum(dgp)
        dgp_b = dgp.astype(BF16)
        dgw_ref[...] += _dot_tn(r_ref[...].astype(BF16), dgp_b)
        dr_ref[...] = _dot_nt(dgp_b, gw_ref[...]).astype(BF16)

    back = lambda i: (ntile - 1 - i, 0)
    const2 = lambda i: (0, 0)
    return pl.pallas_call(
        body, grid=(ntile,),
        in_specs=dep_specs + [
                  pl.BlockSpec((rt, z.shape[1]), back), pl.BlockSpec((rt, GATE_PAD), back), pl.BlockSpec((rt, dv), back),
                  pl.BlockSpec((per, HEADS, hv, hk), lambda i: (ntile - 1 - i, 0, 0, 0)),
                  pl.BlockSpec((1, HEADS, hv, hk), lambda i: (jnp.maximum((ntile - 1 - i) * per - 1, 0), 0, 0, 0)),
                  _resident(gate_w.shape), _resident((1, dk)), _resident((1, hv))],
        out_specs=[pl.BlockSpec((rt, z.shape[1]), back), pl.BlockSpec((rt, GATE_PAD), back),
                   pl.BlockSpec(gate_w.shape, const2), pl.BlockSpec((1, dk), const2), pl.BlockSpec((1, hv), const2)],
        out_shape=[_sds(z.shape, BF16), _sds((t, GATE_PAD), BF16), _sds(gate_w.shape, F32), _sds((1, dk), F32),
                   _sds((1, hv), F32)],
        scratch_shapes=[pltpu.VMEM((HEADS, hv, hk), F32), pltpu.VMEM((rt, dk), F32), pltpu.VMEM((rt, dk), F32),
                        pltpu.VMEM((rt, dk), F32), pltpu.VMEM((rt, dv), BF16), pltpu.VMEM((rt, dk), F32)],
        compiler_params=_cparams("arbitrary"), name=name)(*deps, z, r, dog, states, states, gate_w, gate_b, head_g)


def _head(h, gain, target, *, name):
    t, d = h.shape
    rt = _row_tile(t, 832)

    def body(h_ref, g_ref, t_ref, dh_ref, loss_ref, dg_ref):
        i = pl.program_id(0)

        @pl.when(i == 0)
        def _():
            loss_ref[...] = jnp.zeros_like(loss_ref)
            dg_ref[...] = jnp.zeros_like(dg_ref)

        hv = h_ref[...]
        rstd = lax.rsqrt(jnp.mean(hv * hv, axis=-1, keepdims=True) + EPS)
        xh = hv * rstd
        err = jnp.where(_row_ids(i, rt) >= CHUNK, xh * g_ref[...] - t_ref[...], 0.0)
        loss_ref[...] += (0.5 / d) * jnp.sum(err * err)
        dy = err * (1.0 / d)
        dg_ref[...] += _rowsum(dy * xh)
        dxh = dy * g_ref[...]
        dh_ref[...] = rstd * (dxh - xh * jnp.mean(dxh * xh, axis=-1, keepdims=True))

    return pl.pallas_call(
        body, grid=(t // rt,),
        in_specs=[pl.BlockSpec((rt, d), lambda i: (i, 0)), _resident((1, d)), pl.BlockSpec((rt, d), lambda i: (i, 0))],
        out_specs=[pl.BlockSpec((rt, d), lambda i: (i, 0)), pl.BlockSpec((8, LANE), lambda i: (0, 0)),
                   pl.BlockSpec((1, d), lambda i: (0, 0))],
        out_shape=[_sds((t, d), F32), _sds((8, LANE), F32), _sds((1, d), F32)],
        compiler_params=_cparams("arbitrary"), name=name)(h, gain, target)


def _adamw_math(w, g, m, v):
    m = ADAM_B1 * m + (1.0 - ADAM_B1) * g
    v = ADAM_B2 * v + (1.0 - ADAM_B2) * (g * g)
    m_hat = m / (1.0 - ADAM_B1 ** ADAM_STEP)
    v_hat = v / (1.0 - ADAM_B2 ** ADAM_STEP)
    return -ADAM_LR * (m_hat / (jnp.sqrt(v_hat) + ADAM_EPS) + ADAM_WD * w), m, v


N_CHIP = N_DEV // 2
BLOCK_ELEMS = 128 * 1024


def _my_slot():
    return 4 * lax.axis_index("x") + 2 * lax.axis_index("y") + lax.axis_index("c")


def _row_block(r, c):
    cap = max(8, BLOCK_ELEMS // (-(-c // LANE) * LANE))
    return max([b for b in range(8, r + 1, 8) if r % b == 0 and b <= cap] or [r])


def _blocks(r, c):
    rb = _row_block(r, c)
    if rb < r or r * c <= BLOCK_ELEMS:
        return rb, c
    return r, max([b for b in (512, 256, LANE) if c % b == 0 and r * b <= BLOCK_ELEMS] or [c])


def _reduce_adam(parts, w, m, v, *, after=None, name):
    nl, r, c = w.shape
    rb, cb = _blocks(r, c)
    dep_specs, deps = _dep_specs(after)

    def body(*refs):
        me = refs[0][0]
        refs = refs[1 + len(deps):]
        p_refs = refs[:2 * nl]
        w_ref, m_ref, v_ref, g_out, d_out, m_out, v_out = refs[2 * nl:]
        layer = pl.program_id(0)
        for li in range(nl):
            @pl.when(layer == li)
            def _(li=li):
                own_ref, land_ref = p_refs[2 * li], p_refs[2 * li + 1]
                mine = own_ref[...].astype(F32)
                g = None
                for dev in range(N_DEV):
                    term = jnp.where(me == dev, mine, land_ref[dev].astype(F32))
                    g = term if g is None else g + term
                g_out[...] = g
                d_out[...], m_out[...], v_out[...] = _adamw_math(w_ref[...], g, m_ref[...], v_ref[...])

    blk = pl.BlockSpec((None, rb, cb), lambda l, i, j, me: (l, i, j))
    p_specs = []
    for li in range(nl):
        p_specs += [
            pl.BlockSpec((None, rb, cb), lambda l, i, j, me, li=li: (me[0], jnp.where(l == li, i, 0), jnp.where(l == li, j, 0))),
            pl.BlockSpec((N_DEV, rb, cb), lambda l, i, j, me, li=li: (0, jnp.where(l == li, i, 0), jnp.where(l == li, j, 0)))]
    flat = [p for pair in parts for p in pair]
    grid_spec = pltpu.PrefetchScalarGridSpec(
        num_scalar_prefetch=1, grid=(nl, r // rb, c // cb), in_specs=dep_specs + p_specs + [blk, blk, blk],
        out_specs=[blk] * 4)
    return pl.pallas_call(
        body, grid_spec=grid_spec, out_shape=[_sds(w.shape, F32)] * 4,
        compiler_params=_cparams("arbitrary", "arbitrary", "arbitrary"), name=name)(
        _my_slot().reshape(1), *deps, *flat, w, m, v)


def _adam_small(own, landed, split, w, m, v, *, name):
    n = len(w)

    def body(*refs):
        own_refs, land_refs, w_refs, m_refs, v_refs = (refs[k * n:(k + 1) * n] for k in range(5))
        outs = refs[5 * n:]
        me = _my_slot()
        for k in range(n):
            mine = own_refs[k][me] if split[k] else own_refs[k][...]
            g = None
            for dev in range(N_DEV):
                term = jnp.where(me == dev, mine, land_refs[k][dev])
                g = term if g is None else g + term
            outs[4 * k][...] = g
            outs[4 * k + 1][...], outs[4 * k + 2][...], outs[4 * k + 3][...] = _adamw_math(
                w_refs[k][...], g, m_refs[k][...], v_refs[k][...])

    out = pl.pallas_call(body, out_shape=[_sds(a.shape, F32) for a in w for _ in range(4)],
                         compiler_params=pltpu.CompilerParams(vmem_limit_bytes=V7X_VMEM_LIMIT), name=name)(
        *own, *landed, *w, *m, *v)
    return [tuple(out[4 * k:4 * k + 4]) for k in range(n)]


_HBM = pl.BlockSpec(memory_space=pltpu.HBM)
_SEM = pl.BlockSpec(memory_space=pltpu.SEMAPHORE)
_DATAFLOW = pltpu.SideEffectType.DATAFLOW_SIDE_EFFECTING


def _plan_to_all(src, land):
    x, y, c = lax.axis_index("x"), lax.axis_index("y"), lax.axis_index("c")
    return [(src, land.at[_my_slot()], (x ^ ((d >> 2) & 1), y ^ ((d >> 1) & 1), c ^ (d & 1))) for d in range(1, N_DEV)]


def _plan_split_to_all(src, land):
    x, y, c = lax.axis_index("x"), lax.axis_index("y"), lax.axis_index("c")
    peers = [(x ^ ((d >> 2) & 1), y ^ ((d >> 1) & 1), c ^ (d & 1)) for d in range(1, N_DEV)]
    return [(src.at[4 * px + 2 * py + pc], land.at[_my_slot()], (px, py, pc)) for px, py, pc in peers]


_PLAN_COPIES = {_plan_to_all: N_DEV - 1, _plan_split_to_all: N_DEV - 1}


def _plans(plan, n):
    return list(plan) if isinstance(plan, (list, tuple)) else [plan] * n


def _exchange_copies(plan, ins, lands, send, recv):
    copies, sem = [], 0
    for p, src, land in zip(_plans(plan, len(lands)), ins, lands):
        for s, dst, dev in p(src, land):
            copies.append(pltpu.make_async_remote_copy(
                src_ref=s, dst_ref=dst, send_sem=send.at[sem], recv_sem=recv.at[sem],
                device_id=dev, device_id_type=pl.DeviceIdType.MESH))
            sem += 1
    return copies


def _place_own(a, dtype, *, after=None, name):
    r, c = a.shape
    rb = _row_block(r, c)
    dep_specs, deps = _dep_specs(after)

    def body(*refs):
        a_ref, o_ref = refs[1 + len(deps):]
        o_ref[...] = a_ref[...].astype(dtype)

    grid_spec = pltpu.PrefetchScalarGridSpec(
        num_scalar_prefetch=1, grid=(r // rb,), in_specs=dep_specs + [pl.BlockSpec((rb, c), lambda i, me: (i, 0))],
        out_specs=pl.BlockSpec((None, rb, c), lambda i, me: (me[0], i, 0)))
    return pl.pallas_call(body, grid_spec=grid_spec, out_shape=_sds((N_DEV, r, c), dtype),
                          compiler_params=_cparams("arbitrary"), name=name)(_my_slot().reshape(1), *deps, a)


def _plan_gather_first(land, _):
    x, y, c = lax.axis_index("x"), lax.axis_index("y"), lax.axis_index("c")
    mine = land.at[_my_slot()]
    return [(mine, mine, (x, y, 1 - c))] + [(mine, mine, (x ^ (d >> 1), y ^ (d & 1), c)) for d in range(1, N_CHIP)]


def _plan_gather_relay(land, _):
    x, y, c = lax.axis_index("x"), lax.axis_index("y"), lax.axis_index("c")
    slots = [land.at[4 * (x ^ (d >> 1)) + 2 * (y ^ (d & 1)) + c] for d in range(1, N_CHIP)]
    return [(s, s, (x, y, 1 - c)) for s in slots]


_PLAN_COPIES[_plan_gather_first] = N_CHIP
_PLAN_COPIES[_plan_gather_relay] = N_CHIP - 1


def _exchange_start(plan, arrs, lands, *, after=None, name):
    bufs = list(lands) if arrs is None else list(arrs) + list(lands)
    n, nb = len(lands), len(bufs)
    nsem = sum(_PLAN_COPIES[p] for p in _plans(plan, n))
    dep_specs, deps = _dep_specs(after)

    def body(*refs):
        ins, land_refs = refs[:n], refs[nb - n:nb]
        send, recv = refs[nb + len(deps)], refs[nb + len(deps) + 1]
        for cp in _exchange_copies(plan, ins, land_refs, send, recv):
            cp.start()
        refs[-1][...] = jnp.zeros_like(refs[-1])

    out = pl.pallas_call(
        body, name=name,
        out_shape=(pltpu.SemaphoreType.DMA((nsem,)), pltpu.SemaphoreType.DMA((nsem,)),
                   *[pltpu.HBM(a.shape, a.dtype) for a in bufs], _sds((8, LANE), F32)),
        in_specs=[_HBM] * nb + dep_specs,
        out_specs=(_SEM, _SEM, *([_HBM] * nb), pl.BlockSpec(memory_space=pltpu.VMEM)),
        input_output_aliases={i: 2 + i for i in range(nb)},
        compiler_params=pltpu.CompilerParams(has_side_effects=_DATAFLOW),
    )(*[pltpu.with_memory_space_constraint(a, pltpu.HBM) for a in bufs], *deps)
    return (plan, n, out[0], out[1], list(out[2:2 + nb])), out[-1]


def _exchange_wait(state, after, *, name):
    plan, n, send_sem, recv_sem, bufs = state
    nb = len(bufs)
    after = list(after) if isinstance(after, (list, tuple)) else [after]

    def body(*refs):
        ins, land_refs, send, recv = refs[:n], refs[nb - n:nb], refs[nb], refs[nb + 1]
        for cp in _exchange_copies(plan, ins, land_refs, send, recv):
            cp.wait_send()
            cp.wait_recv()

    out = pl.pallas_call(
        body, name=name, out_shape=[pltpu.HBM(a.shape, a.dtype) for a in bufs],
        in_specs=[_HBM] * nb + [_SEM, _SEM] + [pl.BlockSpec(memory_space=pl.ANY)] * len(after), out_specs=[_HBM] * nb,
        input_output_aliases={i: i for i in range(nb)},
        compiler_params=pltpu.CompilerParams(has_side_effects=_DATAFLOW),
    )(*bufs, send_sem, recv_sem, *after)
    return list(out[:n]), list(out[nb - n:])


def _dep_specs(after):
    return ([], []) if after is None else ([pl.BlockSpec(memory_space=pl.ANY)], [after])


def _undo_column_split(g):
    return jnp.transpose(g, (1, 0, 2)).reshape(g.shape[1], N_DEV * g.shape[2])


def _column_split(a):
    r, c = a.shape
    return jnp.transpose(a.reshape(r, N_DEV, c // N_DEV), (1, 0, 2))


class _WholeWeights:
    def __init__(self, groups):
        self.groups = groups
        self.grads = {}

    def fetch(self, group, after):
        return self.groups[group]

    def emit(self, group, grads):
        self.grads.update(grads)
        return None

    def poll(self, after):
        return None


def _local_step(x, target, replicated, src):
    d = x.shape[1]
    mix_g, ffn_g = replicated["mix_g"], replicated["ffn_g"]
    h0 = jnp.concatenate([jnp.zeros((CHUNK, d), F32), x], axis=0)
    tgt = jnp.concatenate([jnp.zeros((CHUNK, d), F32), target], axis=0)
    cp = src.fetch("cp", [h0, tgt])
    h0 = lax.dynamic_update_slice(h0, cp["meta"], (PAD_ROWS, 0))
    cp_mid = (cp["conv_w"], replicated["conv_b"], replicated["ln_g"], replicated["ln_b"], replicated["pool_w"],
              replicated["pool_scale"])

    z0, u0 = _linear_fwd(h0, cp["cp_w_in_t"], gain=mix_g[0:1], w_t=True, name="cp_in")
    cat = _cp_mid_fwd(z0, *cp_mid, name="cp_mid")
    h1 = _linear_fwd(cat, cp["cp_w_out"], res=h0, name="cp_out")
    ffn0 = src.fetch("ffn0", h1)
    h2, uf0, rf0 = _ffn_fwd(h1, ffn_g[0:1], ffn0["w1"], ffn0["w2"], name="ffn0")
    gla = src.fetch("gla", h2)
    gla_mid = (gla["gate_w"], gla["gate_b"], gla["head_g"])
    z1, u1 = _linear_fwd(h2, gla["gla_w_qkvg_t"], gain=mix_g[1:2], w_t=True, name="gla_in")
    r1 = _linear_fwd(u1, gla["gla_w_r_t"], w_t=True, name="gla_in_r")
    og, states = _gla_mid_fwd(z1, r1, *gla_mid, name="gla_mid")
    h3 = _linear_fwd(og, gla["gla_w_out"], res=h2, name="gla_out")
    ffn1 = src.fetch("ffn1", h3)
    h4, uf1, rf1 = _ffn_fwd(h3, ffn_g[1:2], ffn1["w1"], ffn1["w2"], name="ffn1")
    dh4, loss, d_final_g = _head(h4, replicated["final_g"], tgt, name="head")

    dh3, dhh1, dob1, dffn_g1 = _ffn_bwd_x(h3, dh4, ffn_g[1:2], rf1, ffn1["w1"], ffn1["w2"], name="ffn1_bwd_x")
    dw1_1, dw2_1 = _ffn_bwd_w(uf1, dhh1, rf1, dob1, name="ffn1_bwd_w")
    sent = src.emit("ffn1", dict(w1=dw1_1, w2=dw2_1))
    dog = _linear_bwd_x([dh3], [gla["gla_w_out"]], after=sent, name="gla_out_dx")
    d_gla_w_out = _linear_bwd_w(og, dh3, name="gla_out_dw")
    sent = src.poll(d_gla_w_out)
    dz1, dr1, d_gate_w, d_gate_b, d_head_g = _gla_mid_bwd(z1, r1, dog, states, *gla_mid, after=sent, name="gla_mid_bwd")
    dh2, dmix_g1 = _linear_bwd_x([dz1, dr1], [gla["gla_w_qkvg_t"], gla["gla_w_r_t"]], w_t=True,
                                 norm=(h2, mix_g[1:2], dh3), name="gla_in_dx")
    d_gla_w_qkvg_t = _linear_bwd_w(dz1, u1, name="gla_in_dw")
    d_gla_w_r_t = _linear_bwd_w(dr1, u1, name="gla_in_r_dw")
    sent = src.emit("gla", dict(gla_w_qkvg_t=d_gla_w_qkvg_t, gla_w_r_t=d_gla_w_r_t, gla_w_out=d_gla_w_out))
    dh1, dhh0, dob0, dffn_g0 = _ffn_bwd_x(h1, dh2, ffn_g[0:1], rf0, ffn0["w1"], ffn0["w2"], after=sent, name="ffn0_bwd_x")
    dw1_0, dw2_0 = _ffn_bwd_w(uf0, dhh0, rf0, dob0, name="ffn0_bwd_w")
    src.poll(dw1_0)
    sent = src.emit("ffn0", dict(w1=dw1_0, w2=dw2_0))
    dcat = _linear_bwd_x([dh1], [cp["cp_w_out"]], after=sent, name="cp_out_dx")
    d_cp_w_out = _linear_bwd_w(cat, dh1, name="cp_out_dw")
    src.emit("cp_out", dict(cp_w_out=d_cp_w_out))
    sent = src.poll(d_cp_w_out)
    dz0, d_conv_w, d_conv_b, d_ln_g, d_ln_b, d_pool_w, d_pool_scale = _cp_mid_bwd(z0, dcat, *cp_mid, after=sent,
                                                                                 name="cp_mid_bwd")
    dh0, dmix_g0 = _linear_bwd_x([dz0], [cp["cp_w_in_t"]], w_t=True, norm=(h0, mix_g[0:1], dh1), name="cp_in_dx")
    d_cp_w_in_t = _linear_bwd_w(dz0, u0, name="cp_in_dw")

    small = dict(
        mix_g=jnp.concatenate([dmix_g0, dmix_g1]), ffn_g=jnp.concatenate([dffn_g0, dffn_g1]), conv_b=d_conv_b, ln_g=d_ln_g,
        ln_b=d_ln_b, pool_w=d_pool_w, pool_scale=d_pool_scale, final_g=d_final_g, meta=dh0[PAD_ROWS:CHUNK], conv_w=d_conv_w,
        gate_w=d_gate_w, gate_b=d_gate_b, head_g=d_head_g)
    src.emit("cp", dict(cp_w_in_t=d_cp_w_in_t, small=small))
    return loss, dh0[CHUNK:], small


_REPLICATED = ("mix_norm_g", "ffn_norm_g", "cp_conv_b", "cp_ln_g", "cp_ln_b", "cp_pool_w", "cp_pool_scale", "final_norm_g")
_SMALL_SHARDED = ("meta_tokens", "cp_conv_w", "gla_gate_w2", "gla_gate_b", "gla_head_g")
_NAMES = ("meta_tokens", "mix_norm_g", "ffn_norm_g", "ffn_w1", "ffn_w2", "cp_w_in", "cp_conv_w", "cp_conv_b", "cp_ln_g",
          "cp_ln_b", "cp_pool_w", "cp_pool_scale", "cp_w_out", "gla_w_in", "gla_gate_w2", "gla_gate_b", "gla_head_g",
          "gla_w_out", "final_norm_g")
_SMALL_GRADS = ("mix_g", "ffn_g", "conv_b", "ln_g", "ln_b", "pool_w", "pool_scale", "final_g", "meta", "conv_w", "gate_w",
                "gate_b", "head_g")
_GROUPS = ("cp", "ffn0", "gla", "ffn1")


class _Exchanges:
    def __init__(self, w, d):
        self.d = d
        small = [w[n].reshape(w[n].shape[-2:]) for n in _SMALL_SHARDED]
        self.small_shard_shapes = [w[n].shape for n in _SMALL_SHARDED]
        shards = dict(
            cp=[(w["cp_w_in"][0].T, BF16), (w["cp_w_out"][0], BF16)] + [(a, F32) for a in small],
            ffn0=[(w["ffn_w1"][0], BF16), (w["ffn_w2"][0], BF16)],
            gla=[(w["gla_w_in"][0].T, BF16), (w["gla_w_out"][0], BF16)],
            ffn1=[(w["ffn_w1"][1], BF16), (w["ffn_w2"][1], BF16)])
        self.gathers = {}
        self.sent = {}
        token = None
        for group in _GROUPS:
            lands = [_place_own(a, dtype, after=token, name=f"place_w_{group}_{k}")
                     for k, (a, dtype) in enumerate(shards[group])]
            self.gathers[group], token = _exchange_start(_plan_gather_first, None, lands, after=token,
                                                         name=f"start_w_{group}")
        self.token = token

    def fetch(self, group, after):
        d = self.d
        after = (list(after) if isinstance(after, (list, tuple)) else [after]) + [self.token]
        _, lands = _exchange_wait(self.gathers[group], after, name=f"wait_w_{group}")
        relay, token = _exchange_start(_plan_gather_relay, None, lands, name=f"relay_w_{group}")
        _, got = _exchange_wait(relay, token, name=f"wait_relay_w_{group}")
        if group in ("ffn0", "ffn1"):
            return dict(w1=got[0], w2=got[1])
        if group == "gla":
            qkvg = 3 * d
            w_in_t = got[0].reshape(-1, d)
            return dict(gla_w_qkvg_t=w_in_t[:qkvg], gla_w_r_t=jnp.pad(w_in_t[qkvg:], ((0, GATE_PAD - GATE_RANK), (0, 0))),
                        gla_w_out=got[1].reshape(d, d), gate_w=self.gate_w, gate_b=self.gate_b, head_g=self.head_g)
        meta, conv_w, gate_w, self.gate_b, self.head_g = [_undo_column_split(a) for a in got[2:]]
        self.gate_w = jnp.pad(gate_w, ((0, GATE_PAD - GATE_RANK), (0, 0))).astype(BF16)
        return dict(cp_w_in_t=got[0].reshape(-1, d), cp_w_out=got[1].reshape(d, d), meta=meta,
                    conv_w=jnp.pad(conv_w, ((0, 1), (0, 0))))

    def emit(self, group, g):
        d = self.d
        if group in ("ffn0", "ffn1"):
            arrs = [g["w1"], g["w2"]]
        elif group == "gla":
            w_in_t = jnp.concatenate([g["gla_w_qkvg_t"], g["gla_w_r_t"][:GATE_RANK]], axis=0)
            arrs = [w_in_t.reshape(N_DEV, -1, d), g["gla_w_out"].reshape(N_DEV, d // N_DEV, d)]
        elif group == "cp_out":
            arrs = [g["cp_w_out"].reshape(N_DEV, d // N_DEV, d)]
        else:
            s = dict(g["small"])
            s.update(pool_w=s["pool_w"][None], conv_w=s["conv_w"][:CONV_WIDTH], gate_w=s["gate_w"][:GATE_RANK])
            own = [s[n] for n in _SMALL_GRADS[:len(_REPLICATED)]]
            own += [_column_split(s[n]).reshape((N_DEV,) + shape)
                    for n, shape in zip(_SMALL_GRADS[len(_REPLICATED):], self.small_shard_shapes)]
            plans = [_plan_to_all] * len(_REPLICATED) + [_plan_split_to_all] * len(_SMALL_SHARDED)
            lands = [lax.empty((N_DEV,) + a.shape, F32) for a in own[:len(_REPLICATED)]]
            lands += [lax.empty(a.shape, F32) for a in own[len(_REPLICATED):]]
            self.small_sent, self.token = _exchange_start(plans, own, lands, after=self.token, name="start_g_small")
            arrs = [g["cp_w_in_t"].reshape(N_DEV, -1, d)]
        self.sent[group], self.token = _exchange_start(_plan_split_to_all, arrs, [lax.empty(a.shape, a.dtype) for a in arrs],
                                                       after=self.token, name=f"start_g_{group}")
        return self.token

    def poll(self, after):
        return self.token

    def finish(self, w, mom, var):
        out = {}
        after = self.token

        def landed(group):
            own, got = _exchange_wait(self.sent[group], after, name=f"wait_g_{group}")
            return list(zip(own, got))

        def adam(n, parts, behind=None, transposed=False):
            flip = (lambda a: jnp.transpose(a, (0, 2, 1))) if transposed else (lambda a: a)
            res = _reduce_adam(parts, flip(w[n]), flip(mom[n]), flip(var[n]), after=behind, name=f"adam_{n}")
            out[n] = tuple(flip(a) for a in res)
            return res[0]

        ffn1 = landed("ffn1")
        after = ffn1[0][1]
        gla = landed("gla")
        after = gla[0][1]
        ffn0 = landed("ffn0")
        after = adam("ffn_w1", [ffn0[0], ffn1[0]])
        after = adam("ffn_w2", [ffn0[1], ffn1[1]], after)
        after = adam("gla_w_in", [gla[0]], after, transposed=True)
        after = adam("gla_w_out", [gla[1]], after)
        small_own, small_landed = _exchange_wait(self.small_sent, after, name="wait_g_small")
        names = _REPLICATED + _SMALL_SHARDED
        split = [False] * len(_REPLICATED) + [True] * len(_SMALL_SHARDED)
        small_new = _adam_small(small_own, small_landed, split, [w[n] for n in names], [mom[n] for n in names],
                                [var[n] for n in names], name="adam_small")
        out.update(zip(names, small_new))

        after = small_new[0][0]
        cp_out = landed("cp_out")
        after = adam("cp_w_out", [cp_out[0]])
        cp = landed("cp")
        adam("cp_w_in", [cp[0]], transposed=True)
        return out


def kernel(x, meta_tokens, mix_norm_g, ffn_norm_g, ffn_w1, ffn_w2, cp_w_in, cp_conv_w, cp_conv_b, cp_ln_g, cp_ln_b, cp_pool_w, cp_pool_scale, cp_w_out, gla_w_in, gla_gate_w2, gla_gate_b, gla_head_g, gla_w_out, final_norm_g, loss_target, m_meta_tokens, m_mix_norm_g, m_ffn_norm_g, m_ffn_w1, m_ffn_w2, m_cp_w_in, m_cp_conv_w, m_cp_conv_b, m_cp_ln_g, m_cp_ln_b, m_cp_pool_w, m_cp_pool_scale, m_cp_w_out, m_gla_w_in, m_gla_gate_w2, m_gla_gate_b, m_gla_head_g, m_gla_w_out, m_final_norm_g, v_meta_tokens, v_mix_norm_g, v_ffn_norm_g, v_ffn_w1, v_ffn_w2, v_cp_w_in, v_cp_conv_w, v_cp_conv_b, v_cp_ln_g, v_cp_ln_b, v_cp_pool_w, v_cp_pool_scale, v_cp_w_out, v_gla_w_in, v_gla_gate_w2, v_gla_gate_b, v_gla_head_g, v_gla_w_out, v_final_norm_g):
    w = dict(meta_tokens=meta_tokens, mix_norm_g=mix_norm_g, ffn_norm_g=ffn_norm_g, ffn_w1=ffn_w1, ffn_w2=ffn_w2,
             cp_w_in=cp_w_in, cp_conv_w=cp_conv_w, cp_conv_b=cp_conv_b, cp_ln_g=cp_ln_g, cp_ln_b=cp_ln_b,
             cp_pool_w=cp_pool_w, cp_pool_scale=cp_pool_scale, cp_w_out=cp_w_out, gla_w_in=gla_w_in,
             gla_gate_w2=gla_gate_w2, gla_gate_b=gla_gate_b, gla_head_g=gla_head_g, gla_w_out=gla_w_out,
             final_norm_g=final_norm_g.reshape(1, -1))
    mom = dict(meta_tokens=m_meta_tokens, mix_norm_g=m_mix_norm_g, ffn_norm_g=m_ffn_norm_g, ffn_w1=m_ffn_w1, ffn_w2=m_ffn_w2,
               cp_w_in=m_cp_w_in, cp_conv_w=m_cp_conv_w, cp_conv_b=m_cp_conv_b, cp_ln_g=m_cp_ln_g, cp_ln_b=m_cp_ln_b,
               cp_pool_w=m_cp_pool_w, cp_pool_scale=m_cp_pool_scale, cp_w_out=m_cp_w_out, gla_w_in=m_gla_w_in,
               gla_gate_w2=m_gla_gate_w2, gla_gate_b=m_gla_gate_b, gla_head_g=m_gla_head_g, gla_w_out=m_gla_w_out,
               final_norm_g=m_final_norm_g.reshape(1, -1))
    var = dict(meta_tokens=v_meta_tokens, mix_norm_g=v_mix_norm_g, ffn_norm_g=v_ffn_norm_g, ffn_w1=v_ffn_w1, ffn_w2=v_ffn_w2,
               cp_w_in=v_cp_w_in, cp_conv_w=v_cp_conv_w, cp_conv_b=v_cp_conv_b, cp_ln_g=v_cp_ln_g, cp_ln_b=v_cp_ln_b,
               cp_pool_w=v_cp_pool_w, cp_pool_scale=v_cp_pool_scale, cp_w_out=v_cp_w_out, gla_w_in=v_gla_w_in,
               gla_gate_w2=v_gla_gate_w2, gla_gate_b=v_gla_gate_b, gla_head_g=v_gla_head_g, gla_w_out=v_gla_w_out,
               final_norm_g=v_final_norm_g.reshape(1, -1))
    d = x.shape[-1]
    replicated = dict(mix_g=w["mix_norm_g"], ffn_g=w["ffn_norm_g"], conv_b=w["cp_conv_b"], ln_g=w["cp_ln_g"],
                      ln_b=w["cp_ln_b"], pool_w=w["cp_pool_w"][0].astype(BF16), pool_scale=w["cp_pool_scale"],
                      final_g=w["final_norm_g"])
    exchanges = _Exchanges(w, d)
    loss_blk, grad_x, _ = _local_step(x[0], loss_target[0], replicated, exchanges)
    loss = lax.psum(loss_blk[0, 0], ("x", "y", "c"))
    out = exchanges.finish(w, mom, var)

    def leaf(n, k):
        a = out[n][k]
        return a.reshape(-1) if n == "final_norm_g" else a

    return (loss, grad_x[None], *[leaf(n, 0) for n in _NAMES], *[leaf(n, 1) for n in _NAMES],
            *[leaf(n, 2) for n in _NAMES], *[leaf(n, 3) for n in _NAMES])
```

```python
import functools

import jax
import jax.numpy as jnp
from jax import lax
from jax.experimental import pallas as pl
from jax.experimental.pallas import tpu as pltpu

F32, BF16 = jnp.float32, jnp.bfloat16
N_DEV = 8
CHUNK = 64
N_META = 16
PAD_ROWS = CHUNK - N_META
HALO = 32
EPS = 1e-5
CONV_WIDTH = 31
POOL_WINDOWS = (2, 4, 8, 16)
HEADS = 4
GATE_RANK = 16
GATE_NORM = 16.0
GATE_PAD = 128
ADAM_LR, ADAM_B1, ADAM_B2, ADAM_EPS, ADAM_WD, ADAM_STEP = 0.001, 0.9, 0.999, 1e-08, 0.01, 10
V7X_VMEM_LIMIT = 56 * 2 ** 20
LANE = 128
HI = lax.Precision.HIGHEST


def _cparams(*sem):
    return pltpu.CompilerParams(dimension_semantics=sem, vmem_limit_bytes=V7X_VMEM_LIMIT)


def _row_tile(t, cap):
    best = CHUNK
    for r in range(CHUNK, min(t, cap) + 1, CHUNK):
        if t % r == 0:
            best = r
    return best


def _resident(shape):
    return pl.BlockSpec(shape, lambda *_: (0,) * len(shape), pipeline_mode=pl.Buffered(1))


def _dot(a, b):
    return jnp.dot(a, b, preferred_element_type=F32)


def _dot_nt(a, b):
    return lax.dot_general(a, b, (((1,), (1,)), ((), ())), preferred_element_type=F32)


def _dot_tn(a, b):
    return lax.dot_general(a, b, (((0,), (0,)), ((), ())), preferred_element_type=F32)


def _rowsum(a):
    return jnp.sum(a, axis=0, keepdims=True)


def _sigmoid(a):
    return 1.0 / (1.0 + jnp.exp(-a))


def _row_ids(tile, rt):
    return tile * rt + lax.broadcasted_iota(jnp.int32, (rt, 1), 0)


def _sds(shape, dtype):
    return jax.ShapeDtypeStruct(shape, dtype)


def _linear_fwd(x, w, *, gain=None, res=None, w_t=False, out_dtype=F32, name):
    t, k = x.shape
    n = w.shape[0] if w_t else w.shape[1]
    rt = _row_tile(t, 320)

    def body(*refs):
        refs = list(refs)
        x_ref, w_ref = refs[:2]
        pos = 2
        g_ref = r_ref = u_ref = None
        if gain is not None:
            g_ref = refs[pos]
            pos += 1
        if res is not None:
            r_ref = refs[pos]
            pos += 1
        y_ref = refs[pos]
        if gain is not None:
            u_ref = refs[pos + 1]
            xv = x_ref[...]
            u = (xv * lax.rsqrt(jnp.mean(xv * xv, axis=-1, keepdims=True) + EPS) * g_ref[...]).astype(BF16)
            u_ref[...] = u
        else:
            u = x_ref[...]
        y = _dot_nt(u, w_ref[...]) if w_t else _dot(u, w_ref[...])
        if res is not None:
            y = y + r_ref[...]
        y_ref[...] = y.astype(y_ref.dtype)

    rows = lambda i: (i, 0)
    in_specs = [pl.BlockSpec((rt, k), rows), _resident(w.shape)]
    args = [x, w]
    if gain is not None:
        in_specs.append(_resident((1, k)))
        args.append(gain)
    if res is not None:
        in_specs.append(pl.BlockSpec((rt, n), rows))
        args.append(res)
    out_shape = [_sds((t, n), out_dtype)]
    out_specs = [pl.BlockSpec((rt, n), rows)]
    if gain is not None:
        out_shape.append(_sds((t, k), BF16))
        out_specs.append(pl.BlockSpec((rt, k), rows))
    out = pl.pallas_call(body, grid=(t // rt,), in_specs=in_specs, out_specs=out_specs, out_shape=out_shape,
                         compiler_params=_cparams("parallel"), name=name)(*args)
    return out if gain is not None else out[0]


def _linear_bwd_x(dys, ws, *, norm=None, w_t=False, after=None, name):
    t = dys[0].shape[0]
    k = ws[0].shape[1] if w_t else ws[0].shape[0]
    rt = _row_tile(t, 320)
    nd = len(dys)
    dep_specs, deps = _dep_specs(after)

    def body(*refs):
        refs = list(refs)[len(deps):]
        dy_refs, w_refs = refs[:nd], refs[nd:2 * nd]
        dx = None
        for dy_ref, w_ref in zip(dy_refs, w_refs):
            dy = dy_ref[...].astype(BF16)
            part = _dot(dy, w_ref[...]) if w_t else _dot_nt(dy, w_ref[...])
            dx = part if dx is None else dx + part
        if norm is None:
            refs[2 * nd][...] = dx
            return
        h_ref, g_ref, dres_ref, dh_ref, dg_ref = refs[2 * nd:]
        hv = h_ref[...]
        rstd = lax.rsqrt(jnp.mean(hv * hv, axis=-1, keepdims=True) + EPS)
        xh = hv * rstd

        @pl.when(pl.program_id(0) == 0)
        def _():
            dg_ref[...] = jnp.zeros_like(dg_ref)

        dg_ref[...] += _rowsum(dx * xh)
        dxh = dx * g_ref[...]
        dh_ref[...] = dres_ref[...] + rstd * (dxh - xh * jnp.mean(dxh * xh, axis=-1, keepdims=True))

    rows = lambda i: (i, 0)
    in_specs = [pl.BlockSpec((rt, dy.shape[1]), rows) for dy in dys] + [_resident(w.shape) for w in ws]
    args = list(dys) + list(ws)
    out_shape = [_sds((t, k), F32)]
    out_specs = [pl.BlockSpec((rt, k), rows)]
    if norm is not None:
        h, gain, dres = norm
        in_specs += [pl.BlockSpec((rt, k), rows), _resident((1, k)), pl.BlockSpec((rt, k), rows)]
        args += [h, gain, dres]
        out_shape.append(_sds((1, k), F32))
        out_specs.append(pl.BlockSpec((1, k), lambda i: (0, 0)))
    out = pl.pallas_call(body, grid=(t // rt,), in_specs=dep_specs + in_specs, out_specs=out_specs, out_shape=out_shape,
                         compiler_params=_cparams("arbitrary"), name=name)(*deps, *args)
    return out if norm is not None else out[0]


DW_ROWS = 1024


def _linear_bwd_w(x, dy, *, name):
    t, k = x.shape
    n = dy.shape[1]
    cut_k = k > n
    width = k if cut_k else n
    blk = max(c for c in (640, 512, 384, 256, LANE) if width % c == 0)

    def body(x_ref, dy_ref, o_ref, acc):
        for c0 in range(0, t, DW_ROWS):
            rows = slice(c0, min(c0 + DW_ROWS, t))
            part = _dot_tn(x_ref[rows, :].astype(BF16), dy_ref[rows, :].astype(BF16))
            if c0 == 0:
                acc[...] = part
            else:
                acc[...] += part
        o_ref[...] = acc[...].astype(BF16)

    if cut_k:
        in_specs = [pl.BlockSpec((t, blk), lambda j: (0, j)), _resident((t, n))]
        out_specs = pl.BlockSpec((blk, n), lambda j: (j, 0))
        acc_shape = (blk, n)
    else:
        in_specs = [_resident((t, k)), pl.BlockSpec((t, blk), lambda j: (0, j))]
        out_specs = pl.BlockSpec((k, blk), lambda j: (0, j))
        acc_shape = (k, blk)
    return pl.pallas_call(
        body, grid=(width // blk,), in_specs=in_specs, out_specs=out_specs, out_shape=_sds((k, n), BF16),
        scratch_shapes=[pltpu.VMEM(acc_shape, F32)], compiler_params=_cparams("parallel"), name=name)(x, dy)


def _ffn_fwd(h, gain, w1g, w2g, *, name):
    t, d = h.shape
    f8 = w1g.shape[-1]
    rt = _row_tile(t, 832)

    def body(h_ref, g_ref, w1_ref, w2_ref, o_ref, u_ref, r_ref, acc_ref):
        j = pl.program_id(1)

        @pl.when(j == 0)
        def _():
            hv = h_ref[...]
            u_ref[...] = (hv * lax.rsqrt(jnp.mean(hv * hv, axis=-1, keepdims=True) + EPS) * g_ref[...]).astype(BF16)
            acc_ref[...] = jnp.zeros_like(acc_ref)

        a = jnp.maximum(_dot(u_ref[...], w1_ref[...]), 0.0)
        r_ref[...] = a.astype(BF16)
        acc_ref[...] += _dot((a * a).astype(BF16), w2_ref[...])

        @pl.when(j == N_DEV - 1)
        def _():
            o_ref[...] = h_ref[...] + acc_ref[...]

    return pl.pallas_call(
        body, grid=(t // rt, N_DEV),
        in_specs=[pl.BlockSpec((rt, d), lambda i, j: (i, 0)), _resident((1, d)),
                  pl.BlockSpec((None, d, f8), lambda i, j: (j, 0, 0)),
                  pl.BlockSpec((None, f8, d), lambda i, j: (j, 0, 0))],
        out_specs=[pl.BlockSpec((rt, d), lambda i, j: (i, 0)), pl.BlockSpec((rt, d), lambda i, j: (i, 0)),
                   pl.BlockSpec((rt, f8), lambda i, j: (i, j))],
        out_shape=[_sds((t, d), F32), _sds((t, d), BF16), _sds((t, N_DEV * f8), BF16)],
        scratch_shapes=[pltpu.VMEM((rt, d), F32)],
        compiler_params=_cparams("parallel", "arbitrary"), name=name)(h, gain, w1g, w2g)


def _ffn_bwd_x(h, dout, gain, r, w1g, w2g, *, after=None, name):
    t, d = h.shape
    f8 = w1g.shape[-1]
    rt = _row_tile(t, 832)
    last = N_DEV - 1
    dep_specs, deps = _dep_specs(after)

    def body(*refs):
        h_ref, do_ref, g_ref, r_ref, w1_ref, w2_ref, dh_ref, dhh_ref, dob_ref, dg_ref, du_ref = refs[len(deps):]
        i, j = pl.program_id(0), pl.program_id(1)

        @pl.when(j == 0)
        def _():
            dob_ref[...] = do_ref[...].astype(BF16)
            du_ref[...] = jnp.zeros_like(du_ref)

        dhh = (_dot_nt(dob_ref[...], w2_ref[...]) * (2.0 * r_ref[...].astype(F32))).astype(BF16)
        dhh_ref[...] = dhh
        du_ref[...] += _dot_nt(dhh, w1_ref[...])

        @pl.when(j == last)
        def _():
            @pl.when(i == 0)
            def _():
                dg_ref[...] = jnp.zeros_like(dg_ref)

            hv = h_ref[...]
            rstd = lax.rsqrt(jnp.mean(hv * hv, axis=-1, keepdims=True) + EPS)
            xh = hv * rstd
            du = du_ref[...]
            dg_ref[...] += _rowsum(du * xh)
            dxh = du * g_ref[...]
            dh_ref[...] = do_ref[...] + rstd * (dxh - xh * jnp.mean(dxh * xh, axis=-1, keepdims=True))

    rows = lambda i, j: (i, 0)
    return pl.pallas_call(
        body, grid=(t // rt, N_DEV),
        in_specs=dep_specs + [
                  pl.BlockSpec((rt, d), rows), pl.BlockSpec((rt, d), rows), _resident((1, d)),
                  pl.BlockSpec((rt, f8), lambda i, j: (i, j)),
                  pl.BlockSpec((None, d, f8), lambda i, j: (j, 0, 0)),
                  pl.BlockSpec((None, f8, d), lambda i, j: (j, 0, 0))],
        out_specs=[pl.BlockSpec((rt, d), rows), pl.BlockSpec((rt, f8), lambda i, j: (i, j)), pl.BlockSpec((rt, d), rows),
                   pl.BlockSpec((1, d), lambda i, j: (0, 0))],
        out_shape=[_sds((t, d), F32), _sds((t, N_DEV * f8), BF16), _sds((t, d), BF16), _sds((1, d), F32)],
        scratch_shapes=[pltpu.VMEM((rt, d), F32)],
        compiler_params=_cparams("arbitrary", "arbitrary"), name=name)(*deps, h, dout, gain, r, w1g, w2g)


def _ffn_bwd_w(u, dhh, r, dout_b, *, name):
    t, d = u.shape
    f8 = dhh.shape[1] // N_DEV

    def body(u_ref, dhh_ref, r_ref, dob_ref, dw1_ref, dw2_ref, acc1, acc2):
        for c0 in range(0, t, DW_ROWS):
            rows = slice(c0, min(c0 + DW_ROWS, t))
            rr = r_ref[rows, :].astype(F32)
            part1 = _dot_tn(u_ref[rows, :], dhh_ref[rows, :])
            part2 = _dot_tn((rr * rr).astype(BF16), dob_ref[rows, :])
            if c0 == 0:
                acc1[...] = part1
                acc2[...] = part2
            else:
                acc1[...] += part1
                acc2[...] += part2
        dw1_ref[...] = acc1[...].astype(BF16)
        dw2_ref[...] = acc2[...].astype(BF16)

    return pl.pallas_call(
        body, grid=(N_DEV,),
        in_specs=[_resident((t, d)), pl.BlockSpec((t, f8), lambda j: (0, j)), pl.BlockSpec((t, f8), lambda j: (0, j)),
                  _resident((t, d))],
        out_specs=[pl.BlockSpec((None, d, f8), lambda j: (j, 0, 0)), pl.BlockSpec((None, f8, d), lambda j: (j, 0, 0))],
        out_shape=[_sds((N_DEV, d, f8), BF16), _sds((N_DEV, f8, d), BF16)],
        scratch_shapes=[pltpu.VMEM((d, f8), F32), pltpu.VMEM((f8, d), F32)],
        compiler_params=_cparams("parallel"), name=name)(u, dhh, r, dout_b)


def _lane_blocks(width):
    lb = min(LANE, width)
    return [slice(s, s + lb) for s in range(0, width, lb)]


def _conv_rows(src_ref, w_ref, offset, dst_ref, nblk, width, bias_ref=None):
    def blk(rb, carry):
        base = pl.multiple_of(rb * CHUNK, CHUNK)
        for l, ls in enumerate(_lane_blocks(width)):
            acc = jnp.zeros((CHUNK, ls.stop - ls.start), F32)
            if bias_ref is not None:
                acc = acc + bias_ref[:, ls]
            for k in range(CONV_WIDTH):
                acc = acc + w_ref[k:k + 1, ls] * src_ref[l, pl.ds(base + offset(k), CHUNK), :]
            dst_ref[l, pl.ds(base, CHUNK), :] = acc
        return carry

    lax.fori_loop(0, nblk, blk, 0)


def _to_lane_blocks(ref, row0, value):
    for l, ls in enumerate(_lane_blocks(value.shape[1])):
        ref[l, row0:row0 + value.shape[0], :] = value[:, ls]


def _from_lane_blocks(ref):
    return jnp.concatenate([ref[l] for l in range(ref.shape[0])], axis=1)


def _pool_counts(rows, window):
    return jnp.clip(rows - PAD_ROWS + 1, 1, window).astype(F32)


def _trailing_sum(v, window):
    s, sh = v, 1
    while sh < window:
        s = s + pltpu.roll(s, sh, 0)
        sh *= 2
    return s


def _leading_sum(v, window):
    s, sh, n = v, 1, v.shape[0]
    while sh < window:
        s = s + pltpu.roll(s, n - sh, 0)
        sh *= 2
    return s


def _cp_mid_fwd(z, conv_w, conv_b, ln_g, ln_b, pool_w, pool_scale, *, name):
    t, ein = z.shape
    cd = conv_b.shape[1]
    pd = pool_scale.shape[1]
    pg = pd // len(POOL_WINDOWS)
    rt = _row_tile(t, 320)

    def body(z_ref, cw_ref, cb_ref, lg_ref, lb_ref, pw_ref, ps_ref, o_ref, gext, pext, conv_s):
        i = pl.program_id(0)

        @pl.when(i == 0)
        def _():
            _to_lane_blocks(gext, 0, jnp.zeros((HALO, cd), F32))
            pext[0:HALO, :] = jnp.zeros((HALO, pd), F32)

        _to_lane_blocks(gext, HALO, z_ref[:, 0:cd] * _sigmoid(z_ref[:, cd:2 * cd]))
        pext[HALO:HALO + rt, :] = z_ref[:, 2 * cd:]
        _conv_rows(gext, cw_ref, lambda k: k + HALO - (CONV_WIDTH - 1), conv_s, rt // CHUNK, cd, cb_ref)
        cv = _from_lane_blocks(conv_s)
        xc = cv - jnp.mean(cv, axis=-1, keepdims=True)
        y = xc * lax.rsqrt(jnp.mean(xc * xc, axis=-1, keepdims=True) + EPS) * lg_ref[...] + lb_ref[...]
        rows = _row_ids(i, rt)
        a = jnp.where(rows >= PAD_ROWS, y * _sigmoid(y), 0.0)
        o_ref[:, 0:cd] = a.astype(BF16)
        for gi, window in enumerate(POOL_WINDOWS):
            ls = slice(gi * pg, (gi + 1) * pg)
            v = pext[:, ls]
            tm = _trailing_sum(v, window)[HALO:] / _pool_counts(rows, window) - v[HALO:]
            p = _dot(tm.astype(BF16), pw_ref[gi]) * ps_ref[:, ls]
            o_ref[:, cd + gi * pg:cd + (gi + 1) * pg] = p.astype(BF16)
        gext[:, 0:HALO, :] = gext[:, rt:rt + HALO, :]
        pext[0:HALO, :] = pext[rt:rt + HALO, :]

    nl, lb = len(_lane_blocks(cd)), min(LANE, cd)
    return pl.pallas_call(
        body, grid=(t // rt,),
        in_specs=[pl.BlockSpec((rt, ein), lambda i: (i, 0)), _resident(conv_w.shape), _resident((1, cd)),
                  _resident((1, cd)), _resident((1, cd)), _resident(pool_w.shape), _resident((1, pd))],
        out_specs=pl.BlockSpec((rt, cd + pd), lambda i: (i, 0)), out_shape=_sds((t, cd + pd), BF16),
        scratch_shapes=[pltpu.VMEM((nl, rt + HALO, lb), F32), pltpu.VMEM((rt + HALO, pd), F32),
                        pltpu.VMEM((nl, rt, lb), F32)],
        compiler_params=_cparams("arbitrary"), name=name)(z, conv_w, conv_b, ln_g, ln_b, pool_w, pool_scale)


def _cp_mid_bwd(z, dcat, conv_w, conv_b, ln_g, ln_b, pool_w, pool_scale, *, after=None, name):
    t, ein = z.shape
    cd = conv_b.shape[1]
    pd = pool_scale.shape[1]
    pg = pd // len(POOL_WINDOWS)
    rt = _row_tile(t, 320)
    ntile = t // rt
    per = rt // CHUNK
    dep_specs, deps = _dep_specs(after)

    def body(*refs):
        (z_ref, zh_ref, dc_ref, cw_ref, cb_ref, lg_ref, lb_ref, pw_ref, ps_ref,
         dz_ref, dcw_ref, dcb_ref, dlg_ref, dlb_ref, dpw_ref, dps_ref, gext, pext, conv_s, dcv, dsp) = refs[len(deps):]
        step = pl.program_id(0)
        tile = ntile - 1 - step

        @pl.when(step == 0)
        def _():
            for ref in (dcw_ref, dcb_ref, dlg_ref, dlb_ref, dpw_ref, dps_ref):
                ref[...] = jnp.zeros_like(ref)
            _to_lane_blocks(dcv, rt, jnp.zeros((HALO, cd), F32))
            dsp[rt:rt + HALO, :] = jnp.zeros((HALO, pd), F32)

        keep = jnp.where(tile > 0, 1.0, 0.0)
        zh = zh_ref[CHUNK - HALO:CHUNK, :]
        _to_lane_blocks(gext, 0, keep * zh[:, 0:cd] * _sigmoid(zh[:, cd:2 * cd]))
        pext[0:HALO, :] = keep * zh[:, 2 * cd:]
        za = z_ref[:, 0:cd]
        sg = _sigmoid(z_ref[:, cd:2 * cd])
        _to_lane_blocks(gext, HALO, za * sg)
        pext[HALO:HALO + rt, :] = z_ref[:, 2 * cd:]
        _conv_rows(gext, cw_ref, lambda k: k + HALO - (CONV_WIDTH - 1), conv_s, per, cd, cb_ref)
        cv = _from_lane_blocks(conv_s)
        xc = cv - jnp.mean(cv, axis=-1, keepdims=True)
        rstd = lax.rsqrt(jnp.mean(xc * xc, axis=-1, keepdims=True) + EPS)
        xh = xc * rstd
        y = xh * lg_ref[...] + lb_ref[...]
        sy = _sigmoid(y)
        rows = _row_ids(tile, rt)
        da = jnp.where(rows >= PAD_ROWS, dc_ref[:, 0:cd], 0.0)
        dy = da * (sy * (1.0 + y * (1.0 - sy)))
        dlg_ref[...] += _rowsum(dy * xh)
        dlb_ref[...] += _rowsum(dy)
        dxh = dy * lg_ref[...]
        dconv = rstd * (dxh - jnp.mean(dxh, axis=-1, keepdims=True) - xh * jnp.mean(dxh * xh, axis=-1, keepdims=True))
        dcb_ref[...] += _rowsum(dconv)
        _to_lane_blocks(dcv, 0, dconv)
        for l, ls in enumerate(_lane_blocks(cd)):
            def acc_rows(rb, accs, l=l):
                base = pl.multiple_of(rb * CHUNK, CHUNK)
                d_blk = dcv[l, pl.ds(base, CHUNK), :]
                out = []
                for k in range(CONV_WIDTH):
                    prod = d_blk * gext[l, pl.ds(base + k + HALO - (CONV_WIDTH - 1), CHUNK), :]
                    part = prod[0:8]
                    for s in range(8, CHUNK, 8):
                        part = part + prod[s:s + 8]
                    out.append(accs[k] + part)
                return tuple(out)

            zero = jnp.zeros((8, ls.stop - ls.start), F32)
            accs = lax.fori_loop(0, per, acc_rows, (zero,) * CONV_WIDTH)
            for k in range(CONV_WIDTH):
                dcw_ref[k:k + 1, ls] += _rowsum(accs[k])
        _conv_rows(dcv, cw_ref, lambda k: CONV_WIDTH - 1 - k, conv_s, per, cd)
        dglu = _from_lane_blocks(conv_s)
        dz_ref[:, 0:cd] = (dglu * sg).astype(BF16)
        dz_ref[:, cd:2 * cd] = (dglu * za * sg * (1.0 - sg)).astype(BF16)
        dcv[:, rt:rt + HALO, :] = dcv[:, 0:HALO, :]
        for gi, window in enumerate(POOL_WINDOWS):
            ls = slice(gi * pg, (gi + 1) * pg)
            v = pext[:, ls]
            cnt = _pool_counts(rows, window)
            tm = (_trailing_sum(v, window)[HALO:] / cnt - v[HALO:]).astype(BF16)
            dp = dc_ref[:, cd + gi * pg:cd + (gi + 1) * pg]
            dps_ref[:, ls] += _rowsum(dp * _dot(tm, pw_ref[gi]))
            dpl = (dp * ps_ref[:, ls]).astype(BF16)
            dpw_ref[gi] += _dot_tn(tm, dpl)
            dtm = _dot_nt(dpl, pw_ref[gi])
            dsp[0:rt, ls] = dtm / cnt
            dpin = _leading_sum(dsp[:, ls], window)[0:rt] - dtm
            dz_ref[:, 2 * cd + gi * pg:2 * cd + (gi + 1) * pg] = dpin.astype(BF16)
        dsp[rt:rt + HALO, :] = dsp[0:HALO, :]

    back = lambda i: (ntile - 1 - i, 0)
    halo_idx = lambda i: (jnp.maximum((ntile - 1 - i) * per - 1, 0), 0)
    const2 = lambda i: (0, 0)
    nl, lb = len(_lane_blocks(cd)), min(LANE, cd)
    return pl.pallas_call(
        body, grid=(ntile,),
        in_specs=dep_specs + [
                  pl.BlockSpec((rt, ein), back), pl.BlockSpec((CHUNK, ein), halo_idx), pl.BlockSpec((rt, cd + pd), back),
                  _resident(conv_w.shape), _resident((1, cd)), _resident((1, cd)), _resident((1, cd)),
                  _resident(pool_w.shape), _resident((1, pd))],
        out_specs=[pl.BlockSpec((rt, ein), back), pl.BlockSpec(conv_w.shape, const2), pl.BlockSpec((1, cd), const2),
                   pl.BlockSpec((1, cd), const2), pl.BlockSpec((1, cd), const2),
                   pl.BlockSpec(pool_w.shape, lambda i: (0, 0, 0)), pl.BlockSpec((1, pd), const2)],
        out_shape=[_sds((t, ein), BF16), _sds(conv_w.shape, F32), _sds((1, cd), F32), _sds((1, cd), F32),
                   _sds((1, cd), F32), _sds(pool_w.shape, F32), _sds((1, pd), F32)],
        scratch_shapes=[pltpu.VMEM((nl, rt + HALO, lb), F32), pltpu.VMEM((rt + HALO, pd), F32), pltpu.VMEM((nl, rt, lb), F32),
                        pltpu.VMEM((nl, rt + HALO, lb), F32), pltpu.VMEM((rt + HALO, pd), F32)],
        compiler_params=_cparams("arbitrary"), name=name)(*deps, z, z, dcat, conv_w, conv_b, ln_g, ln_b, pool_w, pool_scale)


def _log_decay(r, gw_ref, gb_ref, rows):
    gp = _dot(r.astype(BF16), gw_ref[...]) + gb_ref[...]
    log_sig = jnp.minimum(gp, 0.0) - jnp.log(1.0 + jnp.exp(-jnp.abs(gp)))
    return gp, jnp.where(rows >= PAD_ROWS, log_sig / GATE_NORM, 0.0)


def _tri(strict):
    r = lax.broadcasted_iota(jnp.int32, (CHUNK, CHUNK), 0)
    c = lax.broadcasted_iota(jnp.int32, (CHUNK, CHUNK), 1)
    return jnp.where(c < r if strict else c <= r, 1.0, 0.0).astype(F32)


def _gla_mid_fwd(z, gate_w, gate_b, head_g, *, name):
    t = z.shape[0]
    dk = gate_b.shape[1]
    hv = head_g.shape[1]
    hk = dk // HEADS
    dv = hv * HEADS
    r_at = 2 * dk + 2 * dv
    rt = _row_tile(t, 320)
    per = rt // CHUNK
    scale = hk ** -0.5

    def body(z_ref, gw_ref, gb_ref, hg_ref, o_ref, st_ref, s_ref, la_ref, dec_ref):
        i = pl.program_id(0)

        @pl.when(i == 0)
        def _():
            s_ref[...] = jnp.zeros_like(s_ref)

        _, la = _log_decay(z_ref[:, r_at:r_at + GATE_PAD], gw_ref, gb_ref, _row_ids(i, rt))
        la_ref[...] = la
        tri = _tri(False)

        def chunk_rows(c):
            return slice(c * CHUNK, (c + 1) * CHUNK)

        def decays(c, carry):
            rows = chunk_rows(c)
            la_c = la_ref[rows, :]
            cum = jnp.dot(tri, la_c, precision=HI, preferred_element_type=F32)
            dec_ref[rows, :] = jnp.exp(_rowsum(la_c) - cum)
            return carry

        def states(c, carry):
            rows = chunk_rows(c)
            etot = jnp.exp(_rowsum(la_ref[rows, :]))
            for hd in range(HEADS):
                ks = slice(hd * hk, (hd + 1) * hk)
                kd = z_ref[rows, dk + hd * hk:dk + (hd + 1) * hk] * dec_ref[rows, ks]
                v = z_ref[rows, 2 * dk + hd * hv:2 * dk + (hd + 1) * hv]
                s_new = s_ref[hd] * etot[:, ks] + _dot_tn(v.astype(BF16), kd.astype(BF16))
                s_ref[hd] = s_new
                st_ref[c, hd] = s_new
            return carry

        def outputs(c, carry):
            rows = chunk_rows(c)
            for hd in range(HEADS):
                q = z_ref[rows, hd * hk:(hd + 1) * hk] * scale
                g = z_ref[rows, 2 * dk + dv + hd * hv:2 * dk + dv + (hd + 1) * hv]
                o = _dot_nt(q.astype(BF16), st_ref[c, hd].astype(BF16))
                on = o * lax.rsqrt(jnp.mean(o * o, axis=-1, keepdims=True) + EPS) * hg_ref[...]
                o_ref[rows, hd * hv:(hd + 1) * hv] = (on * (g * _sigmoid(g))).astype(BF16)
            return carry

        for phase in (decays, states, outputs):
            for c in range(per):
                phase(c, 0)

    return pl.pallas_call(
        body, grid=(t // rt,),
        in_specs=[pl.BlockSpec((rt, z.shape[1]), lambda i: (i, 0)),
                  _resident(gate_w.shape), _resident((1, dk)), _resident((1, hv))],
        out_specs=[pl.BlockSpec((rt, dv), lambda i: (i, 0)), pl.BlockSpec((per, HEADS, hv, hk), lambda i: (i, 0, 0, 0))],
        out_shape=[_sds((t, dv), BF16), _sds((t // CHUNK, HEADS, hv, hk), F32)],
        scratch_shapes=[pltpu.VMEM((HEADS, hv, hk), F32), pltpu.VMEM((rt, dk), F32), pltpu.VMEM((rt, dk), F32)],
        compiler_params=_cparams("arbitrary"), name=name)(z, gate_w, gate_b, head_g)


def _gla_mid_bwd(z, dog, states, gate_w, gate_b, head_g, *, after=None, name):
    t = z.shape[0]
    dk = gate_b.shape[1]
    hv = head_g.shape[1]
    hk = dk // HEADS
    dv = hv * HEADS
    r_at = 2 * dk + 2 * dv
    rt = _row_tile(t, 320)
    ntile = t // rt
    per = rt // CHUNK
    scale = hk ** -0.5
    dep_specs, deps = _dep_specs(after)

    def body(*refs):
        (z_ref, do_ref, st_ref, stp_ref, gw_ref, gb_ref, hg_ref,
         dz_ref, dgw_ref, dgb_ref, dhg_ref, ds_ref, la_ref, dla_ref, dec_ref, dos_ref, e_ref) = refs[len(deps):]
        step = pl.program_id(0)
        tile = ntile - 1 - step

        @pl.when(step == 0)
        def _():
            ds_ref[...] = jnp.zeros_like(ds_ref)
            dgw_ref[...] = jnp.zeros_like(dgw_ref)
            dgb_ref[...] = jnp.zeros_like(dgb_ref)
            dhg_ref[...] = jnp.zeros_like(dhg_ref)

        rows_id = _row_ids(tile, rt)
        r = z_ref[:, r_at:r_at + GATE_PAD]
        gp, la = _log_decay(r, gw_ref, gb_ref, rows_id)
        la_ref[...] = la
        tri, tri_strict = _tri(False), _tri(True)
        keep = jnp.where(tile > 0, 1.0, 0.0)

        def chunk_rows(c):
            return slice(c * CHUNK, (c + 1) * CHUNK)

        def recompute(c, dhg):
            rows = chunk_rows(c)
            la_c = la_ref[rows, :]
            cum = jnp.dot(tri, la_c, precision=HI, preferred_element_type=F32)
            dec_ref[rows, :] = jnp.exp(_rowsum(la_c) - cum)
            for hd in range(HEADS):
                q = (z_ref[rows, hd * hk:(hd + 1) * hk] * scale).astype(BF16)
                g = z_ref[rows, 2 * dk + dv + hd * hv:2 * dk + dv + (hd + 1) * hv]
                s_b = st_ref[c, hd].astype(BF16)
                o = _dot_nt(q, s_b)
                rstd = lax.rsqrt(jnp.mean(o * o, axis=-1, keepdims=True) + EPS)
                oh = o * rstd
                sg = _sigmoid(g)
                d_og = do_ref[rows, hd * hv:(hd + 1) * hv]
                dz_ref[rows, 2 * dk + dv + hd * hv:2 * dk + dv + (hd + 1) * hv] = (
                    d_og * oh * hg_ref[...] * (sg * (1.0 + g * (1.0 - sg)))).astype(BF16)
                don = d_og * (g * sg)
                dhg = dhg + _rowsum(don * oh)
                doh = don * hg_ref[...]
                d_o = (rstd * (doh - oh * jnp.mean(doh * oh, axis=-1, keepdims=True))).astype(BF16)
                dos_ref[rows, hd * hv:(hd + 1) * hv] = d_o
                dz_ref[rows, hd * hk:(hd + 1) * hk] = (_dot(d_o, s_b) * scale).astype(BF16)
            return dhg

        def recurrence(cc, carry):
            c = per - 1 - cc
            rows = chunk_rows(c)
            etot = jnp.exp(_rowsum(la_ref[rows, :]))
            for hd in range(HEADS):
                ks = slice(hd * hk, (hd + 1) * hk)
                q = (z_ref[rows, hd * hk:(hd + 1) * hk] * scale).astype(BF16)
                dec = dec_ref[rows, ks]
                kd = z_ref[rows, dk + hd * hk:dk + (hd + 1) * hk] * dec
                v = z_ref[rows, 2 * dk + hd * hv:2 * dk + (hd + 1) * hv].astype(BF16)
                s_prev = st_ref[c - 1, hd] if c > 0 else keep * stp_ref[0, hd]
                ds_t = ds_ref[hd] + _dot_tn(dos_ref[rows, hd * hv:(hd + 1) * hv], q)
                ds_b = ds_t.astype(BF16)
                dkd = _dot(v, ds_b)
                dz_ref[rows, 2 * dk + hd * hv:2 * dk + (hd + 1) * hv] = _dot_nt(kd.astype(BF16), ds_b).astype(BF16)
                dtot = etot[:, ks] * _rowsum(ds_t * s_prev)
                ds_ref[hd] = ds_t * etot[:, ks]
                dz_ref[rows, dk + hd * hk:dk + (hd + 1) * hk] = (dkd * dec).astype(BF16)
                e_ref[rows, ks] = dkd * kd
                dla_ref[rows, ks] = jnp.broadcast_to(dtot, (CHUNK, hk))
            return carry

        def decay_cotangent(c, carry):
            rows = chunk_rows(c)
            dla_ref[rows, :] += jnp.dot(tri_strict, e_ref[rows, :], precision=HI, preferred_element_type=F32)
            return carry

        dhg = jnp.zeros((1, hv), F32)
        for c in range(per):
            dhg = recompute(c, dhg)
        dhg_ref[...] += dhg
        for phase in (recurrence, decay_cotangent):
            for c in range(per):
                phase(c, 0)
        dla = jnp.where(rows_id >= PAD_ROWS, dla_ref[...], 0.0)
        dgp = dla * (1.0 / GATE_NORM) * (1.0 - _sigmoid(gp))
        dgb_ref[...] += _rowsum(dgp)
        dgp_b = dgp.astype(BF16)
        dgw_ref[...] += _dot_tn(r.astype(BF16), dgp_b)
        dz_ref[:, r_at:r_at + GATE_PAD] = _dot_nt(dgp_b, gw_ref[...]).astype(BF16)

    back = lambda i: (ntile - 1 - i, 0)
    const2 = lambda i: (0, 0)
    return pl.pallas_call(
        body, grid=(ntile,),
        in_specs=dep_specs + [
                  pl.BlockSpec((rt, z.shape[1]), back), pl.BlockSpec((rt, dv), back),
                  pl.BlockSpec((per, HEADS, hv, hk), lambda i: (ntile - 1 - i, 0, 0, 0)),
                  pl.BlockSpec((1, HEADS, hv, hk), lambda i: (jnp.maximum((ntile - 1 - i) * per - 1, 0), 0, 0, 0)),
                  _resident(gate_w.shape), _resident((1, dk)), _resident((1, hv))],
        out_specs=[pl.BlockSpec((rt, z.shape[1]), back),
                   pl.BlockSpec(gate_w.shape, const2), pl.BlockSpec((1, dk), const2), pl.BlockSpec((1, hv), const2)],
        out_shape=[_sds(z.shape, BF16), _sds(gate_w.shape, F32), _sds((1, dk), F32), _sds((1, hv), F32)],
        scratch_shapes=[pltpu.VMEM((HEADS, hv, hk), F32), pltpu.VMEM((rt, dk), F32), pltpu.VMEM((rt, dk), F32),
                        pltpu.VMEM((rt, dk), F32), pltpu.VMEM((rt, dv), BF16), pltpu.VMEM((rt, dk), F32)],
        compiler_params=_cparams("arbitrary"), name=name)(*deps, z, dog, states, states, gate_w, gate_b, head_g)


def _head(h, gain, target, *, name):
    t, d = h.shape
    rt = _row_tile(t, 832)

    def body(h_ref, g_ref, t_ref, dh_ref, loss_ref, dg_ref):
        i = pl.program_id(0)

        @pl.when(i == 0)
        def _():
            loss_ref[...] = jnp.zeros_like(loss_ref)
            dg_ref[...] = jnp.zeros_like(dg_ref)

        hv = h_ref[...]
        rstd = lax.rsqrt(jnp.mean(hv * hv, axis=-1, keepdims=True) + EPS)
        xh = hv * rstd
        err = jnp.where(_row_ids(i, rt) >= CHUNK, xh * g_ref[...] - t_ref[...], 0.0)
        loss_ref[...] += (0.5 / d) * jnp.sum(err * err)
        dy = err * (1.0 / d)
        dg_ref[...] += _rowsum(dy * xh)
        dxh = dy * g_ref[...]
        dh_ref[...] = rstd * (dxh - xh * jnp.mean(dxh * xh, axis=-1, keepdims=True))

    return pl.pallas_call(
        body, grid=(t // rt,),
        in_specs=[pl.BlockSpec((rt, d), lambda i: (i, 0)), _resident((1, d)), pl.BlockSpec((rt, d), lambda i: (i, 0))],
        out_specs=[pl.BlockSpec((rt, d), lambda i: (i, 0)), pl.BlockSpec((8, LANE), lambda i: (0, 0)),
                   pl.BlockSpec((1, d), lambda i: (0, 0))],
        out_shape=[_sds((t, d), F32), _sds((8, LANE), F32), _sds((1, d), F32)],
        compiler_params=_cparams("arbitrary"), name=name)(h, gain, target)


def _adamw_math(w, g, m, v):
    m = ADAM_B1 * m + (1.0 - ADAM_B1) * g
    v = ADAM_B2 * v + (1.0 - ADAM_B2) * (g * g)
    m_hat = m / (1.0 - ADAM_B1 ** ADAM_STEP)
    v_hat = v / (1.0 - ADAM_B2 ** ADAM_STEP)
    return -ADAM_LR * (m_hat / (jnp.sqrt(v_hat) + ADAM_EPS) + ADAM_WD * w), m, v


N_CHIP = N_DEV // 2
BLOCK_ELEMS = 128 * 1024


def _my_slot():
    return 4 * lax.axis_index("x") + 2 * lax.axis_index("y") + lax.axis_index("c")


def _row_block(r, c):
    cap = max(8, BLOCK_ELEMS // (-(-c // LANE) * LANE))
    return max([b for b in range(8, r + 1, 8) if r % b == 0 and b <= cap] or [r])


def _blocks(r, c):
    rb = _row_block(r, c)
    if rb < r or r * c <= BLOCK_ELEMS:
        return rb, c
    return r, max([b for b in (512, 256, LANE) if c % b == 0 and r * b <= BLOCK_ELEMS] or [c])


def _reduce_adam(parts, w, m, v, *, after=None, name):
    nl, r, c = w.shape
    rb, cb = _blocks(r, c)
    dep_specs, deps = _dep_specs(after)

    def body(*refs):
        me = refs[0][0]
        refs = refs[1 + len(deps):]
        p_refs = refs[:2 * nl]
        w_ref, m_ref, v_ref, g_out, d_out, m_out, v_out = refs[2 * nl:]
        layer = pl.program_id(0)
        for li in range(nl):
            @pl.when(layer == li)
            def _(li=li):
                own_ref, land_ref = p_refs[2 * li], p_refs[2 * li + 1]
                mine = own_ref[...].astype(F32)
                g = None
                for dev in range(N_DEV):
                    term = jnp.where(me == dev, mine, land_ref[dev].astype(F32))
                    g = term if g is None else g + term
                g_out[...] = g
                d_out[...], m_out[...], v_out[...] = _adamw_math(w_ref[...], g, m_ref[...], v_ref[...])

    blk = pl.BlockSpec((None, rb, cb), lambda l, i, j, me: (l, i, j))
    p_specs = []
    for li in range(nl):
        p_specs += [
            pl.BlockSpec((None, rb, cb), lambda l, i, j, me, li=li: (me[0], jnp.where(l == li, i, 0), jnp.where(l == li, j, 0))),
            pl.BlockSpec((N_DEV, rb, cb), lambda l, i, j, me, li=li: (0, jnp.where(l == li, i, 0), jnp.where(l == li, j, 0)))]
    flat = [p for pair in parts for p in pair]
    grid_spec = pltpu.PrefetchScalarGridSpec(
        num_scalar_prefetch=1, grid=(nl, r // rb, c // cb), in_specs=dep_specs + p_specs + [blk, blk, blk],
        out_specs=[blk] * 4)
    return pl.pallas_call(
        body, grid_spec=grid_spec, out_shape=[_sds(w.shape, F32)] * 4,
        compiler_params=_cparams("arbitrary", "arbitrary", "arbitrary"), name=name)(
        _my_slot().reshape(1), *deps, *flat, w, m, v)


def _adam_small(own, landed, split, w, m, v, *, name):
    n = len(w)

    def body(*refs):
        own_refs, land_refs, w_refs, m_refs, v_refs = (refs[k * n:(k + 1) * n] for k in range(5))
        outs = refs[5 * n:]
        me = _my_slot()
        for k in range(n):
            mine = own_refs[k][me] if split[k] else own_refs[k][...]
            g = None
            for dev in range(N_DEV):
                term = jnp.where(me == dev, mine, land_refs[k][dev])
                g = term if g is None else g + term
            outs[4 * k][...] = g
            outs[4 * k + 1][...], outs[4 * k + 2][...], outs[4 * k + 3][...] = _adamw_math(
                w_refs[k][...], g, m_refs[k][...], v_refs[k][...])

    out = pl.pallas_call(body, out_shape=[_sds(a.shape, F32) for a in w for _ in range(4)],
                         compiler_params=pltpu.CompilerParams(vmem_limit_bytes=V7X_VMEM_LIMIT), name=name)(
        *own, *landed, *w, *m, *v)
    return [tuple(out[4 * k:4 * k + 4]) for k in range(n)]


_HBM = pl.BlockSpec(memory_space=pltpu.HBM)
_SEM = pl.BlockSpec(memory_space=pltpu.SEMAPHORE)
_DATAFLOW = pltpu.SideEffectType.DATAFLOW_SIDE_EFFECTING


def _plan_to_all(src, land):
    x, y, c = lax.axis_index("x"), lax.axis_index("y"), lax.axis_index("c")
    return [(src, land.at[_my_slot()], (x ^ ((d >> 2) & 1), y ^ ((d >> 1) & 1), c ^ (d & 1))) for d in range(1, N_DEV)]


def _plan_split_to_all(src, land):
    x, y, c = lax.axis_index("x"), lax.axis_index("y"), lax.axis_index("c")
    peers = [(x ^ ((d >> 2) & 1), y ^ ((d >> 1) & 1), c ^ (d & 1)) for d in range(1, N_DEV)]
    return [(src.at[4 * px + 2 * py + pc], land.at[_my_slot()], (px, py, pc)) for px, py, pc in peers]


_PLAN_COPIES = {_plan_to_all: N_DEV - 1, _plan_split_to_all: N_DEV - 1}


def _plans(plan, n):
    return list(plan) if isinstance(plan, (list, tuple)) else [plan] * n


def _exchange_copies(plan, ins, lands, send, recv):
    copies, sem = [], 0
    for p, src, land in zip(_plans(plan, len(lands)), ins, lands):
        for s, dst, dev in p(src, land):
            copies.append(pltpu.make_async_remote_copy(
                src_ref=s, dst_ref=dst, send_sem=send.at[sem], recv_sem=recv.at[sem],
                device_id=dev, device_id_type=pl.DeviceIdType.MESH))
            sem += 1
    return copies


def _place_own(a, dtype, *, after=None, name):
    r, c = a.shape
    rb = _row_block(r, c)
    dep_specs, deps = _dep_specs(after)

    def body(*refs):
        a_ref, o_ref = refs[1 + len(deps):]
        o_ref[...] = a_ref[...].astype(dtype)

    grid_spec = pltpu.PrefetchScalarGridSpec(
        num_scalar_prefetch=1, grid=(r // rb,), in_specs=dep_specs + [pl.BlockSpec((rb, c), lambda i, me: (i, 0))],
        out_specs=pl.BlockSpec((None, rb, c), lambda i, me: (me[0], i, 0)))
    return pl.pallas_call(body, grid_spec=grid_spec, out_shape=_sds((N_DEV, r, c), dtype),
                          compiler_params=_cparams("arbitrary"), name=name)(_my_slot().reshape(1), *deps, a)


def _plan_gather_first(land, _):
    x, y, c = lax.axis_index("x"), lax.axis_index("y"), lax.axis_index("c")
    mine = land.at[_my_slot()]
    return [(mine, mine, (x, y, 1 - c))] + [(mine, mine, (x ^ (d >> 1), y ^ (d & 1), c)) for d in range(1, N_CHIP)]


def _plan_gather_relay(land, _):
    x, y, c = lax.axis_index("x"), lax.axis_index("y"), lax.axis_index("c")
    slots = [land.at[4 * (x ^ (d >> 1)) + 2 * (y ^ (d & 1)) + c] for d in range(1, N_CHIP)]
    return [(s, s, (x, y, 1 - c)) for s in slots]


_PLAN_COPIES[_plan_gather_first] = N_CHIP
_PLAN_COPIES[_plan_gather_relay] = N_CHIP - 1


def _exchange_start(plan, arrs, lands, *, after=None, name):
    bufs = list(lands) if arrs is None else list(arrs) + list(lands)
    n, nb = len(lands), len(bufs)
    nsem = sum(_PLAN_COPIES[p] for p in _plans(plan, n))
    dep_specs, deps = _dep_specs(after)

    def body(*refs):
        ins, land_refs = refs[:n], refs[nb - n:nb]
        send, recv = refs[nb + len(deps)], refs[nb + len(deps) + 1]
        for cp in _exchange_copies(plan, ins, land_refs, send, recv):
            cp.start()
        refs[-1][...] = jnp.zeros_like(refs[-1])

    out = pl.pallas_call(
        body, name=name,
        out_shape=(pltpu.SemaphoreType.DMA((nsem,)), pltpu.SemaphoreType.DMA((nsem,)),
                   *[pltpu.HBM(a.shape, a.dtype) for a in bufs], _sds((8, LANE), F32)),
        in_specs=[_HBM] * nb + dep_specs,
        out_specs=(_SEM, _SEM, *([_HBM] * nb), pl.BlockSpec(memory_space=pltpu.VMEM)),
        input_output_aliases={i: 2 + i for i in range(nb)},
        compiler_params=pltpu.CompilerParams(has_side_effects=_DATAFLOW),
    )(*[pltpu.with_memory_space_constraint(a, pltpu.HBM) for a in bufs], *deps)
    return (plan, n, out[0], out[1], list(out[2:2 + nb])), out[-1]


def _exchange_now(plan, lands, *, name):
    n = len(lands)
    nsem = sum(_PLAN_COPIES[p] for p in _plans(plan, n))

    def body(*refs):
        land_refs, send, recv = refs[n:2 * n], refs[2 * n], refs[2 * n + 1]
        copies = _exchange_copies(plan, land_refs, land_refs, send, recv)
        for cp in copies:
            cp.start()
        for cp in copies:
            cp.wait_send()
            cp.wait_recv()

    hbm = pl.BlockSpec(memory_space=pl.ANY)
    return pl.pallas_call(
        body, in_specs=[hbm] * n, out_specs=[hbm] * n, out_shape=[_sds(a.shape, a.dtype) for a in lands],
        input_output_aliases={i: i for i in range(n)},
        scratch_shapes=[pltpu.SemaphoreType.DMA((nsem,)), pltpu.SemaphoreType.DMA((nsem,))], name=name)(*lands)


def _exchange_wait(state, after, *, name):
    plan, n, send_sem, recv_sem, bufs = state
    nb = len(bufs)
    after = list(after) if isinstance(after, (list, tuple)) else [after]

    def body(*refs):
        ins, land_refs, send, recv = refs[:n], refs[nb - n:nb], refs[nb], refs[nb + 1]
        for cp in _exchange_copies(plan, ins, land_refs, send, recv):
            cp.wait_send()
            cp.wait_recv()

    out = pl.pallas_call(
        body, name=name, out_shape=[pltpu.HBM(a.shape, a.dtype) for a in bufs],
        in_specs=[_HBM] * nb + [_SEM, _SEM] + [pl.BlockSpec(memory_space=pl.ANY)] * len(after), out_specs=[_HBM] * nb,
        input_output_aliases={i: i for i in range(nb)},
        compiler_params=pltpu.CompilerParams(has_side_effects=_DATAFLOW),
    )(*bufs, send_sem, recv_sem, *after)
    return list(out[:n]), list(out[nb - n:])


def _dep_specs(after):
    return ([], []) if after is None else ([pl.BlockSpec(memory_space=pl.ANY)], [after])


def _undo_column_split(g):
    return jnp.transpose(g, (1, 0, 2)).reshape(g.shape[1], N_DEV * g.shape[2])


def _column_split(a):
    r, c = a.shape
    return jnp.transpose(a.reshape(r, N_DEV, c // N_DEV), (1, 0, 2))


class _WholeWeights:
    def __init__(self, groups):
        self.groups = groups
        self.grads = {}

    def fetch(self, group, after):
        return self.groups[group]

    def emit(self, group, grads):
        self.grads.update(grads)
        return None

    def poll(self, after):
        return None


def _local_step(x, target, replicated, src):
    d = x.shape[1]
    mix_g, ffn_g = replicated["mix_g"], replicated["ffn_g"]
    h0 = jnp.concatenate([jnp.zeros((CHUNK, d), F32), x], axis=0)
    tgt = jnp.concatenate([jnp.zeros((CHUNK, d), F32), target], axis=0)
    cp = src.fetch("cp", [h0, tgt])
    h0 = lax.dynamic_update_slice(h0, cp["meta"], (PAD_ROWS, 0))
    cp_mid = (cp["conv_w"], replicated["conv_b"], replicated["ln_g"], replicated["ln_b"], replicated["pool_w"],
              replicated["pool_scale"])

    z0, u0 = _linear_fwd(h0, cp["cp_w_in_t"], gain=mix_g[0:1], w_t=True, name="cp_in")
    cat = _cp_mid_fwd(z0, *cp_mid, name="cp_mid")
    h1 = _linear_fwd(cat, cp["cp_w_out"], res=h0, name="cp_out")
    ffn0 = src.fetch("ffn0", h1)
    h2, uf0, rf0 = _ffn_fwd(h1, ffn_g[0:1], ffn0["w1"], ffn0["w2"], name="ffn0")
    gla = src.fetch("gla", h2)
    gla_mid = (gla["gate_w"], gla["gate_b"], gla["head_g"])
    z1, u1 = _linear_fwd(h2, gla["gla_w_in_t"], gain=mix_g[1:2], w_t=True, name="gla_in")
    og, states = _gla_mid_fwd(z1, *gla_mid, name="gla_mid")
    h3 = _linear_fwd(og, gla["gla_w_out"], res=h2, name="gla_out")
    ffn1 = src.fetch("ffn1", h3)
    h4, uf1, rf1 = _ffn_fwd(h3, ffn_g[1:2], ffn1["w1"], ffn1["w2"], name="ffn1")
    dh4, loss, d_final_g = _head(h4, replicated["final_g"], tgt, name="head")

    dh3, dhh1, dob1, dffn_g1 = _ffn_bwd_x(h3, dh4, ffn_g[1:2], rf1, ffn1["w1"], ffn1["w2"], name="ffn1_bwd_x")
    dw1_1, dw2_1 = _ffn_bwd_w(uf1, dhh1, rf1, dob1, name="ffn1_bwd_w")
    sent = src.emit("ffn1", dict(w1=dw1_1, w2=dw2_1))
    dog = _linear_bwd_x([dh3], [gla["gla_w_out"]], after=sent, name="gla_out_dx")
    d_gla_w_out = _linear_bwd_w(og, dh3, name="gla_out_dw")
    sent = src.poll(d_gla_w_out)
    dz1, d_gate_w, d_gate_b, d_head_g = _gla_mid_bwd(z1, dog, states, *gla_mid, after=sent, name="gla_mid_bwd")
    dh2, dmix_g1 = _linear_bwd_x([dz1], [gla["gla_w_in_t"]], w_t=True, norm=(h2, mix_g[1:2], dh3), name="gla_in_dx")
    d_gla_w_in_t = _linear_bwd_w(dz1, u1, name="gla_in_dw")
    sent = src.emit("gla", dict(gla_w_in_t=d_gla_w_in_t, gla_w_out=d_gla_w_out))
    dh1, dhh0, dob0, dffn_g0 = _ffn_bwd_x(h1, dh2, ffn_g[0:1], rf0, ffn0["w1"], ffn0["w2"], after=sent, name="ffn0_bwd_x")
    dw1_0, dw2_0 = _ffn_bwd_w(uf0, dhh0, rf0, dob0, name="ffn0_bwd_w")
    src.poll(dw1_0)
    sent = src.emit("ffn0", dict(w1=dw1_0, w2=dw2_0))
    dcat = _linear_bwd_x([dh1], [cp["cp_w_out"]], after=sent, name="cp_out_dx")
    d_cp_w_out = _linear_bwd_w(cat, dh1, name="cp_out_dw")
    src.emit("cp_out", dict(cp_w_out=d_cp_w_out))
    sent = src.poll(d_cp_w_out)
    dz0, d_conv_w, d_conv_b, d_ln_g, d_ln_b, d_pool_w, d_pool_scale = _cp_mid_bwd(z0, dcat, *cp_mid, after=sent,
                                                                                 name="cp_mid_bwd")
    dh0, dmix_g0 = _linear_bwd_x([dz0], [cp["cp_w_in_t"]], w_t=True, norm=(h0, mix_g[0:1], dh1), name="cp_in_dx")
    d_cp_w_in_t = _linear_bwd_w(dz0, u0, name="cp_in_dw")

    small = dict(
        mix_g=jnp.concatenate([dmix_g0, dmix_g1]), ffn_g=jnp.concatenate([dffn_g0, dffn_g1]), conv_b=d_conv_b, ln_g=d_ln_g,
        ln_b=d_ln_b, pool_w=d_pool_w, pool_scale=d_pool_scale, final_g=d_final_g, meta=dh0[PAD_ROWS:CHUNK], conv_w=d_conv_w,
        gate_w=d_gate_w, gate_b=d_gate_b, head_g=d_head_g)
    src.emit("cp", dict(cp_w_in_t=d_cp_w_in_t, small=small))
    return loss, dh0[CHUNK:], small


_REPLICATED = ("mix_norm_g", "ffn_norm_g", "cp_conv_b", "cp_ln_g", "cp_ln_b", "cp_pool_w", "cp_pool_scale", "final_norm_g")
_SMALL_SHARDED = ("meta_tokens", "cp_conv_w", "gla_gate_w2", "gla_gate_b", "gla_head_g")
_NAMES = ("meta_tokens", "mix_norm_g", "ffn_norm_g", "ffn_w1", "ffn_w2", "cp_w_in", "cp_conv_w", "cp_conv_b", "cp_ln_g",
          "cp_ln_b", "cp_pool_w", "cp_pool_scale", "cp_w_out", "gla_w_in", "gla_gate_w2", "gla_gate_b", "gla_head_g",
          "gla_w_out", "final_norm_g")
_SMALL_GRADS = ("mix_g", "ffn_g", "conv_b", "ln_g", "ln_b", "pool_w", "pool_scale", "final_g", "meta", "conv_w", "gate_w",
                "gate_b", "head_g")
_GROUPS = ("cp", "ffn0", "gla", "ffn1")


class _Exchanges:
    def __init__(self, w, d):
        self.d = d
        small = [w[n].reshape(w[n].shape[-2:]) for n in _SMALL_SHARDED]
        self.small_shard_shapes = [w[n].shape for n in _SMALL_SHARDED]
        shards = dict(
            cp=[(w["cp_w_in"][0].T, BF16), (w["cp_w_out"][0], BF16)] + [(a, F32) for a in small],
            ffn0=[(w["ffn_w1"][0], BF16), (w["ffn_w2"][0], BF16)],
            gla=[(w["gla_w_in"][0].T, BF16), (w["gla_w_out"][0], BF16)],
            ffn1=[(w["ffn_w1"][1], BF16), (w["ffn_w2"][1], BF16)])
        self.gathers = {}
        self.sent = {}
        token = None
        for group in _GROUPS:
            lands = [_place_own(a, dtype, after=token, name=f"place_w_{group}_{k}")
                     for k, (a, dtype) in enumerate(shards[group])]
            self.gathers[group], token = _exchange_start(_plan_gather_first, None, lands, after=token,
                                                         name=f"start_w_{group}")
        self.token = token

    def fetch(self, group, after):
        d = self.d
        after = (list(after) if isinstance(after, (list, tuple)) else [after]) + [self.token]
        _, lands = _exchange_wait(self.gathers[group], after, name=f"wait_w_{group}")
        got = _exchange_now(_plan_gather_relay, lands, name=f"relay_w_{group}")
        if group in ("ffn0", "ffn1"):
            return dict(w1=got[0], w2=got[1])
        if group == "gla":
            w_in_t = jnp.pad(got[0].reshape(-1, d), ((0, GATE_PAD - GATE_RANK), (0, 0)))
            return dict(gla_w_in_t=w_in_t, gla_w_out=got[1].reshape(d, d), gate_w=self.gate_w, gate_b=self.gate_b,
                        head_g=self.head_g)
        meta, conv_w, gate_w, self.gate_b, self.head_g = [_undo_column_split(a) for a in got[2:]]
        self.gate_w = jnp.pad(gate_w, ((0, GATE_PAD - GATE_RANK), (0, 0))).astype(BF16)
        return dict(cp_w_in_t=got[0].reshape(-1, d), cp_w_out=got[1].reshape(d, d), meta=meta,
                    conv_w=jnp.pad(conv_w, ((0, 1), (0, 0))))

    def emit(self, group, g):
        d = self.d
        if group in ("ffn0", "ffn1"):
            arrs = [g["w1"], g["w2"]]
        elif group == "gla":
            w_in_t = g["gla_w_in_t"][:3 * d + GATE_RANK]
            arrs = [w_in_t.reshape(N_DEV, -1, d), g["gla_w_out"].reshape(N_DEV, d // N_DEV, d)]
        elif group == "cp_out":
            arrs = [g["cp_w_out"].reshape(N_DEV, d // N_DEV, d)]
        else:
            s = dict(g["small"])
            s.update(pool_w=s["pool_w"][None], conv_w=s["conv_w"][:CONV_WIDTH], gate_w=s["gate_w"][:GATE_RANK])
            own = [s[n] for n in _SMALL_GRADS[:len(_REPLICATED)]]
            own += [_column_split(s[n]).reshape((N_DEV,) + shape)
                    for n, shape in zip(_SMALL_GRADS[len(_REPLICATED):], self.small_shard_shapes)]
            plans = [_plan_to_all] * len(_REPLICATED) + [_plan_split_to_all] * len(_SMALL_SHARDED)
            lands = [lax.empty((N_DEV,) + a.shape, F32) for a in own[:len(_REPLICATED)]]
            lands += [lax.empty(a.shape, F32) for a in own[len(_REPLICATED):]]
            self.small_sent, self.token = _exchange_start(plans, own, lands, after=self.token, name="start_g_small")
            arrs = [g["cp_w_in_t"].reshape(N_DEV, -1, d)]
        self.sent[group], self.token = _exchange_start(_plan_split_to_all, arrs, [lax.empty(a.shape, a.dtype) for a in arrs],
                                                       after=self.token, name=f"start_g_{group}")
        return self.token

    def poll(self, after):
        return self.token

    def finish(self, w, mom, var):
        out = {}
        after = self.token

        def landed(group):
            own, got = _exchange_wait(self.sent[group], after, name=f"wait_g_{group}")
            return list(zip(own, got))

        def adam(n, parts, behind=None, transposed=False):
            flip = (lambda a: jnp.transpose(a, (0, 2, 1))) if transposed else (lambda a: a)
            res = _reduce_adam(parts, flip(w[n]), flip(mom[n]), flip(var[n]), after=behind, name=f"adam_{n}")
            out[n] = tuple(flip(a) for a in res)
            return res[0]

        ffn1 = landed("ffn1")
        after = ffn1[0][1]
        gla = landed("gla")
        after = gla[0][1]
        ffn0 = landed("ffn0")
        after = adam("ffn_w1", [ffn0[0], ffn1[0]])
        after = adam("ffn_w2", [ffn0[1], ffn1[1]], after)
        after = adam("gla_w_in", [gla[0]], after, transposed=True)
        after = adam("gla_w_out", [gla[1]], after)
        small_own, small_landed = _exchange_wait(self.small_sent, after, name="wait_g_small")
        names = _REPLICATED + _SMALL_SHARDED
        split = [False] * len(_REPLICATED) + [True] * len(_SMALL_SHARDED)
        small_new = _adam_small(small_own, small_landed, split, [w[n] for n in names], [mom[n] for n in names],
                                [var[n] for n in names], name="adam_small")
        out.update(zip(names, small_new))

        after = small_new[0][0]
        cp_out = landed("cp_out")
        after = adam("cp_w_out", [cp_out[0]])
        cp = landed("cp")
        adam("cp_w_in", [cp[0]], transposed=True)
        return out


def kernel(x, meta_tokens, mix_norm_g, ffn_norm_g, ffn_w1, ffn_w2, cp_w_in, cp_conv_w, cp_conv_b, cp_ln_g, cp_ln_b, cp_pool_w, cp_pool_scale, cp_w_out, gla_w_in, gla_gate_w2, gla_gate_b, gla_head_g, gla_w_out, final_norm_g, loss_target, m_meta_tokens, m_mix_norm_g, m_ffn_norm_g, m_ffn_w1, m_ffn_w2, m_cp_w_in, m_cp_conv_w, m_cp_conv_b, m_cp_ln_g, m_cp_ln_b, m_cp_pool_w, m_cp_pool_scale, m_cp_w_out, m_gla_w_in, m_gla_gate_w2, m_gla_gate_b, m_gla_head_g, m_gla_w_out, m_final_norm_g, v_meta_tokens, v_mix_norm_g, v_ffn_norm_g, v_ffn_w1, v_ffn_w2, v_cp_w_in, v_cp_conv_w, v_cp_conv_b, v_cp_ln_g, v_cp_ln_b, v_cp_pool_w, v_cp_pool_scale, v_cp_w_out, v_gla_w_in, v_gla_gate_w2, v_gla_gate_b, v_gla_head_g, v_gla_w_out, v_final_norm_g):
    w = dict(meta_tokens=meta_tokens, mix_norm_g=mix_norm_g, ffn_norm_g=ffn_norm_g, ffn_w1=ffn_w1, ffn_w2=ffn_w2,
             cp_w_in=cp_w_in, cp_conv_w=cp_conv_w, cp_conv_b=cp_conv_b, cp_ln_g=cp_ln_g, cp_ln_b=cp_ln_b,
             cp_pool_w=cp_pool_w, cp_pool_scale=cp_pool_scale, cp_w_out=cp_w_out, gla_w_in=gla_w_in,
             gla_gate_w2=gla_gate_w2, gla_gate_b=gla_gate_b, gla_head_g=gla_head_g, gla_w_out=gla_w_out,
             final_norm_g=final_norm_g.reshape(1, -1))
    mom = dict(meta_tokens=m_meta_tokens, mix_norm_g=m_mix_norm_g, ffn_norm_g=m_ffn_norm_g, ffn_w1=m_ffn_w1, ffn_w2=m_ffn_w2,
               cp_w_in=m_cp_w_in, cp_conv_w=m_cp_conv_w, cp_conv_b=m_cp_conv_b, cp_ln_g=m_cp_ln_g, cp_ln_b=m_cp_ln_b,
               cp_pool_w=m_cp_pool_w, cp_pool_scale=m_cp_pool_scale, cp_w_out=m_cp_w_out, gla_w_in=m_gla_w_in,
               gla_gate_w2=m_gla_gate_w2, gla_gate_b=m_gla_gate_b, gla_head_g=m_gla_head_g, gla_w_out=m_gla_w_out,
               final_norm_g=m_final_norm_g.reshape(1, -1))
    var = dict(meta_tokens=v_meta_tokens, mix_norm_g=v_mix_norm_g, ffn_norm_g=v_ffn_norm_g, ffn_w1=v_ffn_w1, ffn_w2=v_ffn_w2,
               cp_w_in=v_cp_w_in, cp_conv_w=v_cp_conv_w, cp_conv_b=v_cp_conv_b, cp_ln_g=v_cp_ln_g, cp_ln_b=v_cp_ln_b,
               cp_pool_w=v_cp_pool_w, cp_pool_scale=v_cp_pool_scale, cp_w_out=v_cp_w_out, gla_w_in=v_gla_w_in,
               gla_gate_w2=v_gla_gate_w2, gla_gate_b=v_gla_gate_b, gla_head_g=v_gla_head_g, gla_w_out=v_gla_w_out,
               final_norm_g=v_final_norm_g.reshape(1, -1))
    d = x.shape[-1]
    replicated = dict(mix_g=w["mix_norm_g"], ffn_g=w["ffn_norm_g"], conv_b=w["cp_conv_b"], ln_g=w["cp_ln_g"],
                      ln_b=w["cp_ln_b"], pool_w=w["cp_pool_w"][0].astype(BF16), pool_scale=w["cp_pool_scale"],
                      final_g=w["final_norm_g"])
    exchanges = _Exchanges(w, d)
    loss_blk, grad_x, _ = _local_step(x[0], loss_target[0], replicated, exchanges)
    loss = lax.psum(loss_blk[0, 0], ("x", "y", "c"))
    out = exchanges.finish(w, mom, var)

    def leaf(n, k):
        a = out[n][k]
        return a.reshape(-1) if n == "final_norm_g" else a

    return (loss, grad_x[None], *[leaf(n, 0) for n in _NAMES], *[leaf(n, 1) for n in _NAMES],
            *[leaf(n, 2) for n in _NAMES], *[leaf(n, 3) for n in _NAMES])
```

```python
import functools

import jax
import jax.numpy as jnp
from jax import lax
from jax.experimental import pallas as pl
from jax.experimental.pallas import tpu as pltpu

F32, BF16 = jnp.float32, jnp.bfloat16
N_DEV = 8
CHUNK = 64
N_META = 16
PAD_ROWS = CHUNK - N_META
HALO = 32
EPS = 1e-5
CONV_WIDTH = 31
POOL_WINDOWS = (2, 4, 8, 16)
HEADS = 4
GATE_RANK = 16
GATE_NORM = 16.0
GATE_PAD = 128
ADAM_LR, ADAM_B1, ADAM_B2, ADAM_EPS, ADAM_WD, ADAM_STEP = 0.001, 0.9, 0.999, 1e-08, 0.01, 10
V7X_VMEM_LIMIT = 56 * 2 ** 20
LANE = 128
HI = lax.Precision.HIGHEST


def _cparams(*sem):
    return pltpu.CompilerParams(dimension_semantics=sem, vmem_limit_bytes=V7X_VMEM_LIMIT)


def _row_tile(t, cap):
    best = CHUNK
    for r in range(CHUNK, min(t, cap) + 1, CHUNK):
        if t % r == 0:
            best = r
    return best


def _resident(shape):
    return pl.BlockSpec(shape, lambda *_: (0,) * len(shape), pipeline_mode=pl.Buffered(1))


def _dot(a, b):
    return jnp.dot(a, b, preferred_element_type=F32)


def _dot_nt(a, b):
    return lax.dot_general(a, b, (((1,), (1,)), ((), ())), preferred_element_type=F32)


def _dot_tn(a, b):
    return lax.dot_general(a, b, (((0,), (0,)), ((), ())), preferred_element_type=F32)


def _rowsum(a):
    return jnp.sum(a, axis=0, keepdims=True)


def _sigmoid(a):
    return 1.0 / (1.0 + jnp.exp(-a))


def _row_ids(tile, rt):
    return tile * rt + lax.broadcasted_iota(jnp.int32, (rt, 1), 0)


def _sds(shape, dtype):
    return jax.ShapeDtypeStruct(shape, dtype)


def _linear_fwd(x, w, *, gain=None, res=None, w_t=False, out_dtype=F32, name):
    t, k = x.shape
    n = w.shape[0] if w_t else w.shape[1]
    rt = _row_tile(t, 320)

    def body(*refs):
        refs = list(refs)
        x_ref, w_ref = refs[:2]
        pos = 2
        g_ref = r_ref = u_ref = None
        if gain is not None:
            g_ref = refs[pos]
            pos += 1
        if res is not None:
            r_ref = refs[pos]
            pos += 1
        y_ref = refs[pos]
        if gain is not None:
            u_ref = refs[pos + 1]
            xv = x_ref[...]
            u = (xv * lax.rsqrt(jnp.mean(xv * xv, axis=-1, keepdims=True) + EPS) * g_ref[...]).astype(BF16)
            u_ref[...] = u
        else:
            u = x_ref[...]
        y = _dot_nt(u, w_ref[...]) if w_t else _dot(u, w_ref[...])
        if res is not None:
            y = y + r_ref[...]
        y_ref[...] = y.astype(y_ref.dtype)

    rows = lambda i: (i, 0)
    in_specs = [pl.BlockSpec((rt, k), rows), _resident(w.shape)]
    args = [x, w]
    if gain is not None:
        in_specs.append(_resident((1, k)))
        args.append(gain)
    if res is not None:
        in_specs.append(pl.BlockSpec((rt, n), rows))
        args.append(res)
    out_shape = [_sds((t, n), out_dtype)]
    out_specs = [pl.BlockSpec((rt, n), rows)]
    if gain is not None:
        out_shape.append(_sds((t, k), BF16))
        out_specs.append(pl.BlockSpec((rt, k), rows))
    out = pl.pallas_call(body, grid=(t // rt,), in_specs=in_specs, out_specs=out_specs, out_shape=out_shape,
                         compiler_params=_cparams("parallel"), name=name)(*args)
    return out if gain is not None else out[0]


def _linear_bwd_x(dys, ws, *, norm=None, w_t=False, after=None, name):
    t = dys[0].shape[0]
    k = ws[0].shape[1] if w_t else ws[0].shape[0]
    rt = _row_tile(t, 320)
    nd = len(dys)
    dep_specs, deps = _dep_specs(after)

    def body(*refs):
        refs = list(refs)[len(deps):]
        dy_refs, w_refs = refs[:nd], refs[nd:2 * nd]
        dx = None
        for dy_ref, w_ref in zip(dy_refs, w_refs):
            dy = dy_ref[...].astype(BF16)
            part = _dot(dy, w_ref[...]) if w_t else _dot_nt(dy, w_ref[...])
            dx = part if dx is None else dx + part
        if norm is None:
            refs[2 * nd][...] = dx
            return
        h_ref, g_ref, dres_ref, dh_ref, dg_ref = refs[2 * nd:]
        hv = h_ref[...]
        rstd = lax.rsqrt(jnp.mean(hv * hv, axis=-1, keepdims=True) + EPS)
        xh = hv * rstd

        @pl.when(pl.program_id(0) == 0)
        def _():
            dg_ref[...] = jnp.zeros_like(dg_ref)

        dg_ref[...] += _rowsum(dx * xh)
        dxh = dx * g_ref[...]
        dh_ref[...] = dres_ref[...] + rstd * (dxh - xh * jnp.mean(dxh * xh, axis=-1, keepdims=True))

    rows = lambda i: (i, 0)
    in_specs = [pl.BlockSpec((rt, dy.shape[1]), rows) for dy in dys] + [_resident(w.shape) for w in ws]
    args = list(dys) + list(ws)
    out_shape = [_sds((t, k), F32)]
    out_specs = [pl.BlockSpec((rt, k), rows)]
    if norm is not None:
        h, gain, dres = norm
        in_specs += [pl.BlockSpec((rt, k), rows), _resident((1, k)), pl.BlockSpec((rt, k), rows)]
        args += [h, gain, dres]
        out_shape.append(_sds((1, k), F32))
        out_specs.append(pl.BlockSpec((1, k), lambda i: (0, 0)))
    out = pl.pallas_call(body, grid=(t // rt,), in_specs=dep_specs + in_specs, out_specs=out_specs, out_shape=out_shape,
                         compiler_params=_cparams("arbitrary"), name=name)(*deps, *args)
    return out if norm is not None else out[0]


DW_ROWS = 1024


def _linear_bwd_w(x, dy, *, name):
    t, k = x.shape
    n = dy.shape[1]
    cut_k = k > n
    width = k if cut_k else n
    blk = max(c for c in (640, 512, 384, 256, LANE) if width % c == 0)

    def body(x_ref, dy_ref, o_ref, acc):
        for c0 in range(0, t, DW_ROWS):
            rows = slice(c0, min(c0 + DW_ROWS, t))
            part = _dot_tn(x_ref[rows, :].astype(BF16), dy_ref[rows, :].astype(BF16))
            if c0 == 0:
                acc[...] = part
            else:
                acc[...] += part
        o_ref[...] = acc[...].astype(BF16)

    if cut_k:
        in_specs = [pl.BlockSpec((t, blk), lambda j: (0, j)), _resident((t, n))]
        out_specs = pl.BlockSpec((blk, n), lambda j: (j, 0))
        acc_shape = (blk, n)
    else:
        in_specs = [_resident((t, k)), pl.BlockSpec((t, blk), lambda j: (0, j))]
        out_specs = pl.BlockSpec((k, blk), lambda j: (0, j))
        acc_shape = (k, blk)
    return pl.pallas_call(
        body, grid=(width // blk,), in_specs=in_specs, out_specs=out_specs, out_shape=_sds((k, n), BF16),
        scratch_shapes=[pltpu.VMEM(acc_shape, F32)], compiler_params=_cparams("parallel"), name=name)(x, dy)


def _ffn_fwd(h, gain, w1g, w2g, *, name):
    t, d = h.shape
    f8 = w1g.shape[-1]
    rt = _row_tile(t, 832)

    def body(h_ref, g_ref, w1_ref, w2_ref, o_ref, u_ref, r_ref, acc_ref):
        j = pl.program_id(1)

        @pl.when(j == 0)
        def _():
            hv = h_ref[...]
            u_ref[...] = (hv * lax.rsqrt(jnp.mean(hv * hv, axis=-1, keepdims=True) + EPS) * g_ref[...]).astype(BF16)
            acc_ref[...] = jnp.zeros_like(acc_ref)

        a = jnp.maximum(_dot(u_ref[...], w1_ref[...]), 0.0)
        r_ref[...] = a.astype(BF16)
        acc_ref[...] += _dot((a * a).astype(BF16), w2_ref[...])

        @pl.when(j == N_DEV - 1)
        def _():
            o_ref[...] = h_ref[...] + acc_ref[...]

    return pl.pallas_call(
        body, grid=(t // rt, N_DEV),
        in_specs=[pl.BlockSpec((rt, d), lambda i, j: (i, 0)), _resident((1, d)),
                  pl.BlockSpec((None, d, f8), lambda i, j: (j, 0, 0)),
                  pl.BlockSpec((None, f8, d), lambda i, j: (j, 0, 0))],
        out_specs=[pl.BlockSpec((rt, d), lambda i, j: (i, 0)), pl.BlockSpec((rt, d), lambda i, j: (i, 0)),
                   pl.BlockSpec((rt, f8), lambda i, j: (i, j))],
        out_shape=[_sds((t, d), F32), _sds((t, d), BF16), _sds((t, N_DEV * f8), BF16)],
        scratch_shapes=[pltpu.VMEM((rt, d), F32)],
        compiler_params=_cparams("parallel", "arbitrary"), name=name)(h, gain, w1g, w2g)


def _ffn_bwd_x(h, dout, gain, r, w1g, w2g, *, after=None, name):
    t, d = h.shape
    f8 = w1g.shape[-1]
    rt = _row_tile(t, 832)
    last = N_DEV - 1
    dep_specs, deps = _dep_specs(after)

    def body(*refs):
        h_ref, do_ref, g_ref, r_ref, w1_ref, w2_ref, dh_ref, dhh_ref, dob_ref, dg_ref, du_ref = refs[len(deps):]
        i, j = pl.program_id(0), pl.program_id(1)

        @pl.when(j == 0)
        def _():
            dob_ref[...] = do_ref[...].astype(BF16)
            du_ref[...] = jnp.zeros_like(du_ref)

        dhh = (_dot_nt(dob_ref[...], w2_ref[...]) * (2.0 * r_ref[...].astype(F32))).astype(BF16)
        dhh_ref[...] = dhh
        du_ref[...] += _dot_nt(dhh, w1_ref[...])

        @pl.when(j == last)
        def _():
            @pl.when(i == 0)
            def _():
                dg_ref[...] = jnp.zeros_like(dg_ref)

            hv = h_ref[...]
            rstd = lax.rsqrt(jnp.mean(hv * hv, axis=-1, keepdims=True) + EPS)
            xh = hv * rstd
            du = du_ref[...]
            dg_ref[...] += _rowsum(du * xh)
            dxh = du * g_ref[...]
            dh_ref[...] = do_ref[...] + rstd * (dxh - xh * jnp.mean(dxh * xh, axis=-1, keepdims=True))

    rows = lambda i, j: (i, 0)
    return pl.pallas_call(
        body, grid=(t // rt, N_DEV),
        in_specs=dep_specs + [
                  pl.BlockSpec((rt, d), rows), pl.BlockSpec((rt, d), rows), _resident((1, d)),
                  pl.BlockSpec((rt, f8), lambda i, j: (i, j)),
                  pl.BlockSpec((None, d, f8), lambda i, j: (j, 0, 0)),
                  pl.BlockSpec((None, f8, d), lambda i, j: (j, 0, 0))],
        out_specs=[pl.BlockSpec((rt, d), rows), pl.BlockSpec((rt, f8), lambda i, j: (i, j)), pl.BlockSpec((rt, d), rows),
                   pl.BlockSpec((1, d), lambda i, j: (0, 0))],
        out_shape=[_sds((t, d), F32), _sds((t, N_DEV * f8), BF16), _sds((t, d), BF16), _sds((1, d), F32)],
        scratch_shapes=[pltpu.VMEM((rt, d), F32)],
        compiler_params=_cparams("arbitrary", "arbitrary"), name=name)(*deps, h, dout, gain, r, w1g, w2g)


def _ffn_bwd_w(u, dhh, r, dout_b, *, name):
    t, d = u.shape
    f8 = dhh.shape[1] // N_DEV

    def body(u_ref, dhh_ref, r_ref, dob_ref, dw1_ref, dw2_ref, acc1, acc2):
        for c0 in range(0, t, DW_ROWS):
            rows = slice(c0, min(c0 + DW_ROWS, t))
            rr = r_ref[rows, :].astype(F32)
            part1 = _dot_tn(u_ref[rows, :], dhh_ref[rows, :])
            part2 = _dot_tn((rr * rr).astype(BF16), dob_ref[rows, :])
            if c0 == 0:
                acc1[...] = part1
                acc2[...] = part2
            else:
                acc1[...] += part1
                acc2[...] += part2
        dw1_ref[...] = acc1[...].astype(BF16)
        dw2_ref[...] = acc2[...].astype(BF16)

    return pl.pallas_call(
        body, grid=(N_DEV,),
        in_specs=[_resident((t, d)), pl.BlockSpec((t, f8), lambda j: (0, j)), pl.BlockSpec((t, f8), lambda j: (0, j)),
                  _resident((t, d))],
        out_specs=[pl.BlockSpec((None, d, f8), lambda j: (j, 0, 0)), pl.BlockSpec((None, f8, d), lambda j: (j, 0, 0))],
        out_shape=[_sds((N_DEV, d, f8), BF16), _sds((N_DEV, f8, d), BF16)],
        scratch_shapes=[pltpu.VMEM((d, f8), F32), pltpu.VMEM((f8, d), F32)],
        compiler_params=_cparams("parallel"), name=name)(u, dhh, r, dout_b)


def _lane_blocks(width):
    lb = min(LANE, width)
    return [slice(s, s + lb) for s in range(0, width, lb)]


def _conv_rows(src_ref, w_ref, offset, dst_ref, nblk, width, bias_ref=None):
    def blk(rb, carry):
        base = pl.multiple_of(rb * CHUNK, CHUNK)
        for l, ls in enumerate(_lane_blocks(width)):
            acc = jnp.zeros((CHUNK, ls.stop - ls.start), F32)
            if bias_ref is not None:
                acc = acc + bias_ref[:, ls]
            for k in range(CONV_WIDTH):
                acc = acc + w_ref[k:k + 1, ls] * src_ref[l, pl.ds(base + offset(k), CHUNK), :]
            dst_ref[l, pl.ds(base, CHUNK), :] = acc
        return carry

    lax.fori_loop(0, nblk, blk, 0)


def _to_lane_blocks(ref, row0, value):
    for l, ls in enumerate(_lane_blocks(value.shape[1])):
        ref[l, row0:row0 + value.shape[0], :] = value[:, ls]


def _from_lane_blocks(ref):
    return jnp.concatenate([ref[l] for l in range(ref.shape[0])], axis=1)


def _pool_counts(rows, window):
    return jnp.clip(rows - PAD_ROWS + 1, 1, window).astype(F32)


def _trailing_sum(v, window):
    s, sh = v, 1
    while sh < window:
        s = s + pltpu.roll(s, sh, 0)
        sh *= 2
    return s


def _leading_sum(v, window):
    s, sh, n = v, 1, v.shape[0]
    while sh < window:
        s = s + pltpu.roll(s, n - sh, 0)
        sh *= 2
    return s


def _cp_mid_fwd(z, h, w_out, conv_w, conv_b, ln_g, ln_b, pool_w, pool_scale, *, name):
    t, ein = z.shape
    cd = conv_b.shape[1]
    pd = pool_scale.shape[1]
    pg = pd // len(POOL_WINDOWS)
    rt = _row_tile(t, 320)

    def body(z_ref, h_ref, wo_ref, cw_ref, cb_ref, lg_ref, lb_ref, pw_ref, ps_ref, ho_ref, o_ref, gext, pext, conv_s):
        i = pl.program_id(0)

        @pl.when(i == 0)
        def _():
            _to_lane_blocks(gext, 0, jnp.zeros((HALO, cd), F32))
            pext[0:HALO, :] = jnp.zeros((HALO, pd), F32)

        _to_lane_blocks(gext, HALO, z_ref[:, 0:cd] * _sigmoid(z_ref[:, cd:2 * cd]))
        pext[HALO:HALO + rt, :] = z_ref[:, 2 * cd:]
        _conv_rows(gext, cw_ref, lambda k: k + HALO - (CONV_WIDTH - 1), conv_s, rt // CHUNK, cd, cb_ref)
        cv = _from_lane_blocks(conv_s)
        xc = cv - jnp.mean(cv, axis=-1, keepdims=True)
        y = xc * lax.rsqrt(jnp.mean(xc * xc, axis=-1, keepdims=True) + EPS) * lg_ref[...] + lb_ref[...]
        rows = _row_ids(i, rt)
        a = jnp.where(rows >= PAD_ROWS, y * _sigmoid(y), 0.0)
        o_ref[:, 0:cd] = a.astype(BF16)
        for gi, window in enumerate(POOL_WINDOWS):
            ls = slice(gi * pg, (gi + 1) * pg)
            v = pext[:, ls]
            tm = _trailing_sum(v, window)[HALO:] / _pool_counts(rows, window) - v[HALO:]
            p = _dot(tm.astype(BF16), pw_ref[gi]) * ps_ref[:, ls]
            o_ref[:, cd + gi * pg:cd + (gi + 1) * pg] = p.astype(BF16)
        ho_ref[...] = h_ref[...] + _dot(o_ref[...], wo_ref[...])
        gext[:, 0:HALO, :] = gext[:, rt:rt + HALO, :]
        pext[0:HALO, :] = pext[rt:rt + HALO, :]

    nl, lb = len(_lane_blocks(cd)), min(LANE, cd)
    d = h.shape[1]
    rows = lambda i: (i, 0)
    return pl.pallas_call(
        body, grid=(t // rt,),
        in_specs=[pl.BlockSpec((rt, ein), rows), pl.BlockSpec((rt, d), rows), _resident(w_out.shape),
                  _resident(conv_w.shape), _resident((1, cd)),
                  _resident((1, cd)), _resident((1, cd)), _resident(pool_w.shape), _resident((1, pd))],
        out_specs=[pl.BlockSpec((rt, d), rows), pl.BlockSpec((rt, cd + pd), rows)],
        out_shape=[_sds((t, d), F32), _sds((t, cd + pd), BF16)],
        scratch_shapes=[pltpu.VMEM((nl, rt + HALO, lb), F32), pltpu.VMEM((rt + HALO, pd), F32),
                        pltpu.VMEM((nl, rt, lb), F32)],
        compiler_params=_cparams("arbitrary"), name=name)(z, h, w_out, conv_w, conv_b, ln_g, ln_b, pool_w, pool_scale)


def _cp_mid_bwd(z, dh, w_out, conv_w, conv_b, ln_g, ln_b, pool_w, pool_scale, *, after=None, name):
    t, ein = z.shape
    cd = conv_b.shape[1]
    pd = pool_scale.shape[1]
    pg = pd // len(POOL_WINDOWS)
    rt = _row_tile(t, 320)
    ntile = t // rt
    per = rt // CHUNK
    dep_specs, deps = _dep_specs(after)

    def body(*refs):
        (z_ref, zh_ref, dh_ref, wo_ref, cw_ref, cb_ref, lg_ref, lb_ref, pw_ref, ps_ref,
         dz_ref, dcw_ref, dcb_ref, dlg_ref, dlb_ref, dpw_ref, dps_ref, gext, pext, conv_s, dcv, dsp) = refs[len(deps):]
        step = pl.program_id(0)
        tile = ntile - 1 - step
        dcat = _dot_nt(dh_ref[...].astype(BF16), wo_ref[...])

        @pl.when(step == 0)
        def _():
            for ref in (dcw_ref, dcb_ref, dlg_ref, dlb_ref, dpw_ref, dps_ref):
                ref[...] = jnp.zeros_like(ref)
            _to_lane_blocks(dcv, rt, jnp.zeros((HALO, cd), F32))
            dsp[rt:rt + HALO, :] = jnp.zeros((HALO, pd), F32)

        keep = jnp.where(tile > 0, 1.0, 0.0)
        zh = zh_ref[CHUNK - HALO:CHUNK, :]
        _to_lane_blocks(gext, 0, keep * zh[:, 0:cd] * _sigmoid(zh[:, cd:2 * cd]))
        pext[0:HALO, :] = keep * zh[:, 2 * cd:]
        za = z_ref[:, 0:cd]
        sg = _sigmoid(z_ref[:, cd:2 * cd])
        _to_lane_blocks(gext, HALO, za * sg)
        pext[HALO:HALO + rt, :] = z_ref[:, 2 * cd:]
        _conv_rows(gext, cw_ref, lambda k: k + HALO - (CONV_WIDTH - 1), conv_s, per, cd, cb_ref)
        cv = _from_lane_blocks(conv_s)
        xc = cv - jnp.mean(cv, axis=-1, keepdims=True)
        rstd = lax.rsqrt(jnp.mean(xc * xc, axis=-1, keepdims=True) + EPS)
        xh = xc * rstd
        y = xh * lg_ref[...] + lb_ref[...]
        sy = _sigmoid(y)
        rows = _row_ids(tile, rt)
        da = jnp.where(rows >= PAD_ROWS, dcat[:, 0:cd], 0.0)
        dy = da * (sy * (1.0 + y * (1.0 - sy)))
        dlg_ref[...] += _rowsum(dy * xh)
        dlb_ref[...] += _rowsum(dy)
        dxh = dy * lg_ref[...]
        dconv = rstd * (dxh - jnp.mean(dxh, axis=-1, keepdims=True) - xh * jnp.mean(dxh * xh, axis=-1, keepdims=True))
        dcb_ref[...] += _rowsum(dconv)
        _to_lane_blocks(dcv, 0, dconv)
        for l, ls in enumerate(_lane_blocks(cd)):
            def acc_rows(rb, accs, l=l):
                base = pl.multiple_of(rb * CHUNK, CHUNK)
                d_blk = dcv[l, pl.ds(base, CHUNK), :]
                out = []
                for k in range(CONV_WIDTH):
                    prod = d_blk * gext[l, pl.ds(base + k + HALO - (CONV_WIDTH - 1), CHUNK), :]
                    part = prod[0:8]
                    for s in range(8, CHUNK, 8):
                        part = part + prod[s:s + 8]
                    out.append(accs[k] + part)
                return tuple(out)

            zero = jnp.zeros((8, ls.stop - ls.start), F32)
            accs = lax.fori_loop(0, per, acc_rows, (zero,) * CONV_WIDTH)
            for k in range(CONV_WIDTH):
                dcw_ref[k:k + 1, ls] += _rowsum(accs[k])
        _conv_rows(dcv, cw_ref, lambda k: CONV_WIDTH - 1 - k, conv_s, per, cd)
        dglu = _from_lane_blocks(conv_s)
        dz_ref[:, 0:cd] = (dglu * sg).astype(BF16)
        dz_ref[:, cd:2 * cd] = (dglu * za * sg * (1.0 - sg)).astype(BF16)
        dcv[:, rt:rt + HALO, :] = dcv[:, 0:HALO, :]
        for gi, window in enumerate(POOL_WINDOWS):
            ls = slice(gi * pg, (gi + 1) * pg)
            v = pext[:, ls]
            cnt = _pool_counts(rows, window)
            tm = (_trailing_sum(v, window)[HALO:] / cnt - v[HALO:]).astype(BF16)
            dp = dcat[:, cd + gi * pg:cd + (gi + 1) * pg]
            dps_ref[:, ls] += _rowsum(dp * _dot(tm, pw_ref[gi]))
            dpl = (dp * ps_ref[:, ls]).astype(BF16)
            dpw_ref[gi] += _dot_tn(tm, dpl)
            dtm = _dot_nt(dpl, pw_ref[gi])
            dsp[0:rt, ls] = dtm / cnt
            dpin = _leading_sum(dsp[:, ls], window)[0:rt] - dtm
            dz_ref[:, 2 * cd + gi * pg:2 * cd + (gi + 1) * pg] = dpin.astype(BF16)
        dsp[rt:rt + HALO, :] = dsp[0:HALO, :]

    back = lambda i: (ntile - 1 - i, 0)
    halo_idx = lambda i: (jnp.maximum((ntile - 1 - i) * per - 1, 0), 0)
    const2 = lambda i: (0, 0)
    nl, lb = len(_lane_blocks(cd)), min(LANE, cd)
    return pl.pallas_call(
        body, grid=(ntile,),
        in_specs=dep_specs + [
                  pl.BlockSpec((rt, ein), back), pl.BlockSpec((CHUNK, ein), halo_idx), pl.BlockSpec((rt, dh.shape[1]), back),
                  _resident(w_out.shape),
                  _resident(conv_w.shape), _resident((1, cd)), _resident((1, cd)), _resident((1, cd)),
                  _resident(pool_w.shape), _resident((1, pd))],
        out_specs=[pl.BlockSpec((rt, ein), back), pl.BlockSpec(conv_w.shape, const2), pl.BlockSpec((1, cd), const2),
                   pl.BlockSpec((1, cd), const2), pl.BlockSpec((1, cd), const2),
                   pl.BlockSpec(pool_w.shape, lambda i: (0, 0, 0)), pl.BlockSpec((1, pd), const2)],
        out_shape=[_sds((t, ein), BF16), _sds(conv_w.shape, F32), _sds((1, cd), F32), _sds((1, cd), F32),
                   _sds((1, cd), F32), _sds(pool_w.shape, F32), _sds((1, pd), F32)],
        scratch_shapes=[pltpu.VMEM((nl, rt + HALO, lb), F32), pltpu.VMEM((rt + HALO, pd), F32), pltpu.VMEM((nl, rt, lb), F32),
                        pltpu.VMEM((nl, rt + HALO, lb), F32), pltpu.VMEM((rt + HALO, pd), F32)],
        compiler_params=_cparams("arbitrary"), name=name)(*deps, z, z, dh, w_out, conv_w, conv_b, ln_g, ln_b, pool_w,
                                                          pool_scale)


def _log_decay(r, gw_ref, gb_ref, rows):
    gp = _dot(r.astype(BF16), gw_ref[...]) + gb_ref[...]
    log_sig = jnp.minimum(gp, 0.0) - jnp.log(1.0 + jnp.exp(-jnp.abs(gp)))
    return gp, jnp.where(rows >= PAD_ROWS, log_sig / GATE_NORM, 0.0)


def _tri(strict):
    r = lax.broadcasted_iota(jnp.int32, (CHUNK, CHUNK), 0)
    c = lax.broadcasted_iota(jnp.int32, (CHUNK, CHUNK), 1)
    return jnp.where(c < r if strict else c <= r, 1.0, 0.0).astype(F32)


def _gla_mid_fwd(z, h, w_out, gate_w, gate_b, head_g, *, name):
    t = z.shape[0]
    dk = gate_b.shape[1]
    hv = head_g.shape[1]
    hk = dk // HEADS
    dv = hv * HEADS
    r_at = 2 * dk + 2 * dv
    rt = _row_tile(t, 320)
    per = rt // CHUNK
    scale = hk ** -0.5

    def body(z_ref, h_ref, wo_ref, gw_ref, gb_ref, hg_ref, ho_ref, o_ref, st_ref, s_ref, la_ref, dec_ref):
        i = pl.program_id(0)

        @pl.when(i == 0)
        def _():
            s_ref[...] = jnp.zeros_like(s_ref)

        _, la = _log_decay(z_ref[:, r_at:r_at + GATE_PAD], gw_ref, gb_ref, _row_ids(i, rt))
        la_ref[...] = la
        tri = _tri(False)

        def chunk_rows(c):
            return slice(c * CHUNK, (c + 1) * CHUNK)

        def decays(c, carry):
            rows = chunk_rows(c)
            la_c = la_ref[rows, :]
            cum = jnp.dot(tri, la_c, precision=HI, preferred_element_type=F32)
            dec_ref[rows, :] = jnp.exp(_rowsum(la_c) - cum)
            return carry

        def states(c, carry):
            rows = chunk_rows(c)
            etot = jnp.exp(_rowsum(la_ref[rows, :]))
            for hd in range(HEADS):
                ks = slice(hd * hk, (hd + 1) * hk)
                kd = z_ref[rows, dk + hd * hk:dk + (hd + 1) * hk] * dec_ref[rows, ks]
                v = z_ref[rows, 2 * dk + hd * hv:2 * dk + (hd + 1) * hv]
                s_new = s_ref[hd] * etot[:, ks] + _dot_tn(v.astype(BF16), kd.astype(BF16))
                s_ref[hd] = s_new
                st_ref[c, hd] = s_new
            return carry

        def outputs(c, carry):
            rows = chunk_rows(c)
            for hd in range(HEADS):
                q = z_ref[rows, hd * hk:(hd + 1) * hk] * scale
                g = z_ref[rows, 2 * dk + dv + hd * hv:2 * dk + dv + (hd + 1) * hv]
                o = _dot_nt(q.astype(BF16), st_ref[c, hd].astype(BF16))
                on = o * lax.rsqrt(jnp.mean(o * o, axis=-1, keepdims=True) + EPS) * hg_ref[...]
                o_ref[rows, hd * hv:(hd + 1) * hv] = (on * (g * _sigmoid(g))).astype(BF16)
            return carry

        for phase in (decays, states, outputs):
            for c in range(per):
                phase(c, 0)
        ho_ref[...] = h_ref[...] + _dot(o_ref[...], wo_ref[...])

    d = h.shape[1]
    rows = lambda i: (i, 0)
    return pl.pallas_call(
        body, grid=(t // rt,),
        in_specs=[pl.BlockSpec((rt, z.shape[1]), rows), pl.BlockSpec((rt, d), rows), _resident(w_out.shape),
                  _resident(gate_w.shape), _resident((1, dk)), _resident((1, hv))],
        out_specs=[pl.BlockSpec((rt, d), rows), pl.BlockSpec((rt, dv), rows),
                   pl.BlockSpec((per, HEADS, hv, hk), lambda i: (i, 0, 0, 0))],
        out_shape=[_sds((t, d), F32), _sds((t, dv), BF16), _sds((t // CHUNK, HEADS, hv, hk), F32)],
        scratch_shapes=[pltpu.VMEM((HEADS, hv, hk), F32), pltpu.VMEM((rt, dk), F32), pltpu.VMEM((rt, dk), F32)],
        compiler_params=_cparams("arbitrary"), name=name)(z, h, w_out, gate_w, gate_b, head_g)


def _gla_mid_bwd(z, dh, w_out, states, gate_w, gate_b, head_g, *, after=None, name):
    t = z.shape[0]
    dk = gate_b.shape[1]
    hv = head_g.shape[1]
    hk = dk // HEADS
    dv = hv * HEADS
    r_at = 2 * dk + 2 * dv
    rt = _row_tile(t, 320)
    ntile = t // rt
    per = rt // CHUNK
    scale = hk ** -0.5
    dep_specs, deps = _dep_specs(after)

    def body(*refs):
        (z_ref, dh_ref, wo_ref, st_ref, stp_ref, gw_ref, gb_ref, hg_ref,
         dz_ref, dgw_ref, dgb_ref, dhg_ref, ds_ref, la_ref, dla_ref, dec_ref, dos_ref, e_ref, do_ref) = refs[len(deps):]
        step = pl.program_id(0)
        tile = ntile - 1 - step
        do_ref[...] = _dot_nt(dh_ref[...].astype(BF16), wo_ref[...])

        @pl.when(step == 0)
        def _():
            ds_ref[...] = jnp.zeros_like(ds_ref)
            dgw_ref[...] = jnp.zeros_like(dgw_ref)
            dgb_ref[...] = jnp.zeros_like(dgb_ref)
            dhg_ref[...] = jnp.zeros_like(dhg_ref)

        rows_id = _row_ids(tile, rt)
        r = z_ref[:, r_at:r_at + GATE_PAD]
        gp, la = _log_decay(r, gw_ref, gb_ref, rows_id)
        la_ref[...] = la
        tri, tri_strict = _tri(False), _tri(True)
        keep = jnp.where(tile > 0, 1.0, 0.0)

        def chunk_rows(c):
            return slice(c * CHUNK, (c + 1) * CHUNK)

        def recompute(c, dhg):
            rows = chunk_rows(c)
            la_c = la_ref[rows, :]
            cum = jnp.dot(tri, la_c, precision=HI, preferred_element_type=F32)
            dec_ref[rows, :] = jnp.exp(_rowsum(la_c) - cum)
            for hd in range(HEADS):
                q = (z_ref[rows, hd * hk:(hd + 1) * hk] * scale).astype(BF16)
                g = z_ref[rows, 2 * dk + dv + hd * hv:2 * dk + dv + (hd + 1) * hv]
                s_b = st_ref[c, hd].astype(BF16)
                o = _dot_nt(q, s_b)
                rstd = lax.rsqrt(jnp.mean(o * o, axis=-1, keepdims=True) + EPS)
                oh = o * rstd
                sg = _sigmoid(g)
                d_og = do_ref[rows, hd * hv:(hd + 1) * hv]
                dz_ref[rows, 2 * dk + dv + hd * hv:2 * dk + dv + (hd + 1) * hv] = (
                    d_og * oh * hg_ref[...] * (sg * (1.0 + g * (1.0 - sg)))).astype(BF16)
                don = d_og * (g * sg)
                dhg = dhg + _rowsum(don * oh)
                doh = don * hg_ref[...]
                d_o = (rstd * (doh - oh * jnp.mean(doh * oh, axis=-1, keepdims=True))).astype(BF16)
                dos_ref[rows, hd * hv:(hd + 1) * hv] = d_o
                dz_ref[rows, hd * hk:(hd + 1) * hk] = (_dot(d_o, s_b) * scale).astype(BF16)
            return dhg

        def recurrence(cc, carry):
            c = per - 1 - cc
            rows = chunk_rows(c)
            etot = jnp.exp(_rowsum(la_ref[rows, :]))
            for hd in range(HEADS):
                ks = slice(hd * hk, (hd + 1) * hk)
                q = (z_ref[rows, hd * hk:(hd + 1) * hk] * scale).astype(BF16)
                dec = dec_ref[rows, ks]
                kd = z_ref[rows, dk + hd * hk:dk + (hd + 1) * hk] * dec
                v = z_ref[rows, 2 * dk + hd * hv:2 * dk + (hd + 1) * hv].astype(BF16)
                s_prev = st_ref[c - 1, hd] if c > 0 else keep * stp_ref[0, hd]
                ds_t = ds_ref[hd] + _dot_tn(dos_ref[rows, hd * hv:(hd + 1) * hv], q)
                ds_b = ds_t.astype(BF16)
                dkd = _dot(v, ds_b)
                dz_ref[rows, 2 * dk + hd * hv:2 * dk + (hd + 1) * hv] = _dot_nt(kd.astype(BF16), ds_b).astype(BF16)
                dtot = etot[:, ks] * _rowsum(ds_t * s_prev)
                ds_ref[hd] = ds_t * etot[:, ks]
                dz_ref[rows, dk + hd * hk:dk + (hd + 1) * hk] = (dkd * dec).astype(BF16)
                e_ref[rows, ks] = dkd * kd
                dla_ref[rows, ks] = jnp.broadcast_to(dtot, (CHUNK, hk))
            return carry

        def decay_cotangent(c, carry):
            rows = chunk_rows(c)
            dla_ref[rows, :] += jnp.dot(tri_strict, e_ref[rows, :], precision=HI, preferred_element_type=F32)
            return carry

        dhg = jnp.zeros((1, hv), F32)
        for c in range(per):
            dhg = recompute(c, dhg)
        dhg_ref[...] += dhg
        for phase in (recurrence, decay_cotangent):
            for c in range(per):
                phase(c, 0)
        dla = jnp.where(rows_id >= PAD_ROWS, dla_ref[...], 0.0)
        dgp = dla * (1.0 / GATE_NORM) * (1.0 - _sigmoid(gp))
        dgb_ref[...] += _rowsum(dgp)
        dgp_b = dgp.astype(BF16)
        dgw_ref[...] += _dot_tn(r.astype(BF16), dgp_b)
        dz_ref[:, r_at:r_at + GATE_PAD] = _dot_nt(dgp_b, gw_ref[...]).astype(BF16)

    back = lambda i: (ntile - 1 - i, 0)
    const2 = lambda i: (0, 0)
    return pl.pallas_call(
        body, grid=(ntile,),
        in_specs=dep_specs + [
                  pl.BlockSpec((rt, z.shape[1]), back), pl.BlockSpec((rt, dh.shape[1]), back), _resident(w_out.shape),
                  pl.BlockSpec((per, HEADS, hv, hk), lambda i: (ntile - 1 - i, 0, 0, 0)),
                  pl.BlockSpec((1, HEADS, hv, hk), lambda i: (jnp.maximum((ntile - 1 - i) * per - 1, 0), 0, 0, 0)),
                  _resident(gate_w.shape), _resident((1, dk)), _resident((1, hv))],
        out_specs=[pl.BlockSpec((rt, z.shape[1]), back),
                   pl.BlockSpec(gate_w.shape, const2), pl.BlockSpec((1, dk), const2), pl.BlockSpec((1, hv), const2)],
        out_shape=[_sds(z.shape, BF16), _sds(gate_w.shape, F32), _sds((1, dk), F32), _sds((1, hv), F32)],
        scratch_shapes=[pltpu.VMEM((HEADS, hv, hk), F32), pltpu.VMEM((rt, dk), F32), pltpu.VMEM((rt, dk), F32),
                        pltpu.VMEM((rt, dk), F32), pltpu.VMEM((rt, dv), BF16), pltpu.VMEM((rt, dk), F32),
                        pltpu.VMEM((rt, dv), F32)],
        compiler_params=_cparams("arbitrary"), name=name)(*deps, z, dh, w_out, states, states, gate_w, gate_b, head_g)


def _head(h, gain, target, *, name):
    t, d = h.shape
    rt = _row_tile(t, 832)

    def body(h_ref, g_ref, t_ref, dh_ref, loss_ref, dg_ref):
        i = pl.program_id(0)

        @pl.when(i == 0)
        def _():
            loss_ref[...] = jnp.zeros_like(loss_ref)
            dg_ref[...] = jnp.zeros_like(dg_ref)

        hv = h_ref[...]
        rstd = lax.rsqrt(jnp.mean(hv * hv, axis=-1, keepdims=True) + EPS)
        xh = hv * rstd
        err = jnp.where(_row_ids(i, rt) >= CHUNK, xh * g_ref[...] - t_ref[...], 0.0)
        loss_ref[...] += (0.5 / d) * jnp.sum(err * err)
        dy = err * (1.0 / d)
        dg_ref[...] += _rowsum(dy * xh)
        dxh = dy * g_ref[...]
        dh_ref[...] = rstd * (dxh - xh * jnp.mean(dxh * xh, axis=-1, keepdims=True))

    return pl.pallas_call(
        body, grid=(t // rt,),
        in_specs=[pl.BlockSpec((rt, d), lambda i: (i, 0)), _resident((1, d)), pl.BlockSpec((rt, d), lambda i: (i, 0))],
        out_specs=[pl.BlockSpec((rt, d), lambda i: (i, 0)), pl.BlockSpec((8, LANE), lambda i: (0, 0)),
                   pl.BlockSpec((1, d), lambda i: (0, 0))],
        out_shape=[_sds((t, d), F32), _sds((8, LANE), F32), _sds((1, d), F32)],
        compiler_params=_cparams("arbitrary"), name=name)(h, gain, target)


def _adamw_math(w, g, m, v):
    m = ADAM_B1 * m + (1.0 - ADAM_B1) * g
    v = ADAM_B2 * v + (1.0 - ADAM_B2) * (g * g)
    m_hat = m / (1.0 - ADAM_B1 ** ADAM_STEP)
    v_hat = v / (1.0 - ADAM_B2 ** ADAM_STEP)
    return -ADAM_LR * (m_hat / (jnp.sqrt(v_hat) + ADAM_EPS) + ADAM_WD * w), m, v


N_CHIP = N_DEV // 2
BLOCK_ELEMS = 128 * 1024


def _my_slot():
    return 4 * lax.axis_index("x") + 2 * lax.axis_index("y") + lax.axis_index("c")


def _row_block(r, c):
    cap = max(8, BLOCK_ELEMS // (-(-c // LANE) * LANE))
    return max([b for b in range(8, r + 1, 8) if r % b == 0 and b <= cap] or [r])


def _blocks(r, c):
    rb = _row_block(r, c)
    if rb < r or r * c <= BLOCK_ELEMS:
        return rb, c
    return r, max([b for b in (512, 256, LANE) if c % b == 0 and r * b <= BLOCK_ELEMS] or [c])


def _reduce_adam(parts, w, m, v, *, after=None, name):
    nl, r, c = w.shape
    rb, cb = _blocks(r, c)
    dep_specs, deps = _dep_specs(after)

    def body(*refs):
        me = refs[0][0]
        refs = refs[1 + len(deps):]
        p_refs = refs[:2 * nl]
        w_ref, m_ref, v_ref, g_out, d_out, m_out, v_out = refs[2 * nl:]
        layer = pl.program_id(0)
        for li in range(nl):
            @pl.when(layer == li)
            def _(li=li):
                own_ref, land_ref = p_refs[2 * li], p_refs[2 * li + 1]
                mine = own_ref[...].astype(F32)
                g = None
                for dev in range(N_DEV):
                    term = jnp.where(me == dev, mine, land_ref[dev].astype(F32))
                    g = term if g is None else g + term
                g_out[...] = g
                d_out[...], m_out[...], v_out[...] = _adamw_math(w_ref[...], g, m_ref[...], v_ref[...])

    blk = pl.BlockSpec((None, rb, cb), lambda l, i, j, me: (l, i, j))
    p_specs = []
    for li in range(nl):
        p_specs += [
            pl.BlockSpec((None, rb, cb), lambda l, i, j, me, li=li: (me[0], jnp.where(l == li, i, 0), jnp.where(l == li, j, 0))),
            pl.BlockSpec((N_DEV, rb, cb), lambda l, i, j, me, li=li: (0, jnp.where(l == li, i, 0), jnp.where(l == li, j, 0)))]
    flat = [p for pair in parts for p in pair]
    grid_spec = pltpu.PrefetchScalarGridSpec(
        num_scalar_prefetch=1, grid=(nl, r // rb, c // cb), in_specs=dep_specs + p_specs + [blk, blk, blk],
        out_specs=[blk] * 4)
    return pl.pallas_call(
        body, grid_spec=grid_spec, out_shape=[_sds(w.shape, F32)] * 4,
        compiler_params=_cparams("arbitrary", "arbitrary", "arbitrary"), name=name)(
        _my_slot().reshape(1), *deps, *flat, w, m, v)


def _adam_small(own, landed, split, w, m, v, *, name):
    n = len(w)

    def body(*refs):
        own_refs, land_refs, w_refs, m_refs, v_refs = (refs[k * n:(k + 1) * n] for k in range(5))
        outs = refs[5 * n:]
        me = _my_slot()
        for k in range(n):
            mine = own_refs[k][me] if split[k] else own_refs[k][...]
            g = None
            for dev in range(N_DEV):
                term = jnp.where(me == dev, mine, land_refs[k][dev])
                g = term if g is None else g + term
            outs[4 * k][...] = g
            outs[4 * k + 1][...], outs[4 * k + 2][...], outs[4 * k + 3][...] = _adamw_math(
                w_refs[k][...], g, m_refs[k][...], v_refs[k][...])

    out = pl.pallas_call(body, out_shape=[_sds(a.shape, F32) for a in w for _ in range(4)],
                         compiler_params=pltpu.CompilerParams(vmem_limit_bytes=V7X_VMEM_LIMIT), name=name)(
        *own, *landed, *w, *m, *v)
    return [tuple(out[4 * k:4 * k + 4]) for k in range(n)]


_HBM = pl.BlockSpec(memory_space=pltpu.HBM)
_SEM = pl.BlockSpec(memory_space=pltpu.SEMAPHORE)
_DATAFLOW = pltpu.SideEffectType.DATAFLOW_SIDE_EFFECTING


def _plan_to_all(src, land):
    x, y, c = lax.axis_index("x"), lax.axis_index("y"), lax.axis_index("c")
    return [(src, land.at[_my_slot()], (x ^ ((d >> 2) & 1), y ^ ((d >> 1) & 1), c ^ (d & 1))) for d in range(1, N_DEV)]


def _plan_split_to_all(src, land):
    x, y, c = lax.axis_index("x"), lax.axis_index("y"), lax.axis_index("c")
    peers = [(x ^ ((d >> 2) & 1), y ^ ((d >> 1) & 1), c ^ (d & 1)) for d in range(1, N_DEV)]
    return [(src.at[4 * px + 2 * py + pc], land.at[_my_slot()], (px, py, pc)) for px, py, pc in peers]


_PLAN_COPIES = {_plan_to_all: N_DEV - 1, _plan_split_to_all: N_DEV - 1}


def _plans(plan, n):
    return list(plan) if isinstance(plan, (list, tuple)) else [plan] * n


def _exchange_copies(plan, ins, lands, send, recv):
    copies, sem = [], 0
    for p, src, land in zip(_plans(plan, len(lands)), ins, lands):
        for s, dst, dev in p(src, land):
            copies.append(pltpu.make_async_remote_copy(
                src_ref=s, dst_ref=dst, send_sem=send.at[sem], recv_sem=recv.at[sem],
                device_id=dev, device_id_type=pl.DeviceIdType.MESH))
            sem += 1
    return copies


def _place_own(a, dtype, *, after=None, name):
    r, c = a.shape
    rb = _row_block(r, c)
    dep_specs, deps = _dep_specs(after)

    def body(*refs):
        a_ref, o_ref = refs[1 + len(deps):]
        o_ref[...] = a_ref[...].astype(dtype)

    grid_spec = pltpu.PrefetchScalarGridSpec(
        num_scalar_prefetch=1, grid=(r // rb,), in_specs=dep_specs + [pl.BlockSpec((rb, c), lambda i, me: (i, 0))],
        out_specs=pl.BlockSpec((None, rb, c), lambda i, me: (me[0], i, 0)))
    return pl.pallas_call(body, grid_spec=grid_spec, out_shape=_sds((N_DEV, r, c), dtype),
                          compiler_params=_cparams("arbitrary"), name=name)(_my_slot().reshape(1), *deps, a)


def _plan_gather_first(land, _):
    x, y, c = lax.axis_index("x"), lax.axis_index("y"), lax.axis_index("c")
    mine = land.at[_my_slot()]
    return [(mine, mine, (x, y, 1 - c))] + [(mine, mine, (x ^ (d >> 1), y ^ (d & 1), c)) for d in range(1, N_CHIP)]


def _plan_gather_relay(land, _):
    x, y, c = lax.axis_index("x"), lax.axis_index("y"), lax.axis_index("c")
    slots = [land.at[4 * (x ^ (d >> 1)) + 2 * (y ^ (d & 1)) + c] for d in range(1, N_CHIP)]
    return [(s, s, (x, y, 1 - c)) for s in slots]


_PLAN_COPIES[_plan_gather_first] = N_CHIP
_PLAN_COPIES[_plan_gather_relay] = N_CHIP - 1


def _exchange_start(plan, arrs, lands, *, after=None, name):
    bufs = list(lands) if arrs is None else list(arrs) + list(lands)
    n, nb = len(lands), len(bufs)
    nsem = sum(_PLAN_COPIES[p] for p in _plans(plan, n))
    dep_specs, deps = _dep_specs(after)

    def body(*refs):
        ins, land_refs = refs[:n], refs[nb - n:nb]
        send, recv = refs[nb + len(deps)], refs[nb + len(deps) + 1]
        for cp in _exchange_copies(plan, ins, land_refs, send, recv):
            cp.start()
        refs[-1][...] = jnp.zeros_like(refs[-1])

    out = pl.pallas_call(
        body, name=name,
        out_shape=(pltpu.SemaphoreType.DMA((nsem,)), pltpu.SemaphoreType.DMA((nsem,)),
                   *[pltpu.HBM(a.shape, a.dtype) for a in bufs], _sds((8, LANE), F32)),
        in_specs=[_HBM] * nb + dep_specs,
        out_specs=(_SEM, _SEM, *([_HBM] * nb), pl.BlockSpec(memory_space=pltpu.VMEM)),
        input_output_aliases={i: 2 + i for i in range(nb)},
        compiler_params=pltpu.CompilerParams(has_side_effects=_DATAFLOW),
    )(*[pltpu.with_memory_space_constraint(a, pltpu.HBM) for a in bufs], *deps)
    return (plan, n, out[0], out[1], list(out[2:2 + nb])), out[-1]


def _exchange_now(plan, lands, *, name):
    n = len(lands)
    nsem = sum(_PLAN_COPIES[p] for p in _plans(plan, n))

    def body(*refs):
        land_refs, send, recv = refs[n:2 * n], refs[2 * n], refs[2 * n + 1]
        copies = _exchange_copies(plan, land_refs, land_refs, send, recv)
        for cp in copies:
            cp.start()
        for cp in copies:
            cp.wait_send()
            cp.wait_recv()

    hbm = pl.BlockSpec(memory_space=pl.ANY)
    return pl.pallas_call(
        body, in_specs=[hbm] * n, out_specs=[hbm] * n, out_shape=[_sds(a.shape, a.dtype) for a in lands],
        input_output_aliases={i: i for i in range(n)},
        scratch_shapes=[pltpu.SemaphoreType.DMA((nsem,)), pltpu.SemaphoreType.DMA((nsem,))], name=name)(*lands)


def _exchange_wait(state, after, *, name):
    plan, n, send_sem, recv_sem, bufs = state
    nb = len(bufs)
    after = list(after) if isinstance(after, (list, tuple)) else [after]

    def body(*refs):
        ins, land_refs, send, recv = refs[:n], refs[nb - n:nb], refs[nb], refs[nb + 1]
        for cp in _exchange_copies(plan, ins, land_refs, send, recv):
            cp.wait_send()
            cp.wait_recv()

    out = pl.pallas_call(
        body, name=name, out_shape=[pltpu.HBM(a.shape, a.dtype) for a in bufs],
        in_specs=[_HBM] * nb + [_SEM, _SEM] + [pl.BlockSpec(memory_space=pl.ANY)] * len(after), out_specs=[_HBM] * nb,
        input_output_aliases={i: i for i in range(nb)},
        compiler_params=pltpu.CompilerParams(has_side_effects=_DATAFLOW),
    )(*bufs, send_sem, recv_sem, *after)
    return list(out[:n]), list(out[nb - n:])


def _dep_specs(after):
    return ([], []) if after is None else ([pl.BlockSpec(memory_space=pl.ANY)], [after])


def _undo_column_split(g):
    return jnp.transpose(g, (1, 0, 2)).reshape(g.shape[1], N_DEV * g.shape[2])


def _column_split(a):
    r, c = a.shape
    return jnp.transpose(a.reshape(r, N_DEV, c // N_DEV), (1, 0, 2))


class _WholeWeights:
    def __init__(self, groups):
        self.groups = groups
        self.grads = {}

    def fetch(self, group, after):
        return self.groups[group]

    def emit(self, group, grads):
        self.grads.update(grads)
        return None


def _local_step(x, target, replicated, src):
    d = x.shape[1]
    mix_g, ffn_g = replicated["mix_g"], replicated["ffn_g"]
    h0 = jnp.concatenate([jnp.zeros((CHUNK, d), F32), x], axis=0)
    tgt = jnp.concatenate([jnp.zeros((CHUNK, d), F32), target], axis=0)
    cp = src.fetch("cp", [h0, tgt])
    h0 = lax.dynamic_update_slice(h0, cp["meta"], (PAD_ROWS, 0))
    cp_mid = (cp["conv_w"], replicated["conv_b"], replicated["ln_g"], replicated["ln_b"], replicated["pool_w"],
              replicated["pool_scale"])

    z0, u0 = _linear_fwd(h0, cp["cp_w_in_t"], gain=mix_g[0:1], w_t=True, name="cp_in")
    h1, cat = _cp_mid_fwd(z0, h0, cp["cp_w_out"], *cp_mid, name="cp_mid")
    ffn0 = src.fetch("ffn0", h1)
    h2, uf0, rf0 = _ffn_fwd(h1, ffn_g[0:1], ffn0["w1"], ffn0["w2"], name="ffn0")
    gla = src.fetch("gla", h2)
    gla_mid = (gla["gate_w"], gla["gate_b"], gla["head_g"])
    z1, u1 = _linear_fwd(h2, gla["gla_w_in_t"], gain=mix_g[1:2], w_t=True, name="gla_in")
    h3, og, states = _gla_mid_fwd(z1, h2, gla["gla_w_out"], *gla_mid, name="gla_mid")
    ffn1 = src.fetch("ffn1", h3)
    h4, uf1, rf1 = _ffn_fwd(h3, ffn_g[1:2], ffn1["w1"], ffn1["w2"], name="ffn1")
    dh4, loss, d_final_g = _head(h4, replicated["final_g"], tgt, name="head")

    dh3, dhh1, dob1, dffn_g1 = _ffn_bwd_x(h3, dh4, ffn_g[1:2], rf1, ffn1["w1"], ffn1["w2"], name="ffn1_bwd_x")
    dw1_1, dw2_1 = _ffn_bwd_w(uf1, dhh1, rf1, dob1, name="ffn1_bwd_w")
    sent = src.emit("ffn1", dict(w1=dw1_1, w2=dw2_1))
    d_gla_w_out = _linear_bwd_w(og, dh3, name="gla_out_dw")
    dz1, d_gate_w, d_gate_b, d_head_g = _gla_mid_bwd(z1, dh3, gla["gla_w_out"], states, *gla_mid, after=sent,
                                                     name="gla_mid_bwd")
    dh2, dmix_g1 = _linear_bwd_x([dz1], [gla["gla_w_in_t"]], w_t=True, norm=(h2, mix_g[1:2], dh3), name="gla_in_dx")
    d_gla_w_in_t = _linear_bwd_w(dz1, u1, name="gla_in_dw")
    sent = src.emit("gla", dict(gla_w_in_t=d_gla_w_in_t, gla_w_out=d_gla_w_out))
    dh1, dhh0, dob0, dffn_g0 = _ffn_bwd_x(h1, dh2, ffn_g[0:1], rf0, ffn0["w1"], ffn0["w2"], after=sent, name="ffn0_bwd_x")
    dw1_0, dw2_0 = _ffn_bwd_w(uf0, dhh0, rf0, dob0, name="ffn0_bwd_w")
    src.emit("ffn0", dict(w1=dw1_0, w2=dw2_0))
    d_cp_w_out = _linear_bwd_w(cat, dh1, name="cp_out_dw")
    sent = src.emit("cp_out", dict(cp_w_out=d_cp_w_out))
    dz0, d_conv_w, d_conv_b, d_ln_g, d_ln_b, d_pool_w, d_pool_scale = _cp_mid_bwd(
        z0, dh1, cp["cp_w_out"], *cp_mid, after=sent, name="cp_mid_bwd")
    dh0, dmix_g0 = _linear_bwd_x([dz0], [cp["cp_w_in_t"]], w_t=True, norm=(h0, mix_g[0:1], dh1), name="cp_in_dx")
    d_cp_w_in_t = _linear_bwd_w(dz0, u0, name="cp_in_dw")

    small = dict(
        mix_g=jnp.concatenate([dmix_g0, dmix_g1]), ffn_g=jnp.concatenate([dffn_g0, dffn_g1]), conv_b=d_conv_b, ln_g=d_ln_g,
        ln_b=d_ln_b, pool_w=d_pool_w, pool_scale=d_pool_scale, final_g=d_final_g, meta=dh0[PAD_ROWS:CHUNK], conv_w=d_conv_w,
        gate_w=d_gate_w, gate_b=d_gate_b, head_g=d_head_g)
    src.emit("cp", dict(cp_w_in_t=d_cp_w_in_t, small=small))
    return loss, dh0[CHUNK:], small


_REPLICATED = ("mix_norm_g", "ffn_norm_g", "cp_conv_b", "cp_ln_g", "cp_ln_b", "cp_pool_w", "cp_pool_scale", "final_norm_g")
_SMALL_SHARDED = ("meta_tokens", "cp_conv_w", "gla_gate_w2", "gla_gate_b", "gla_head_g")
_NAMES = ("meta_tokens", "mix_norm_g", "ffn_norm_g", "ffn_w1", "ffn_w2", "cp_w_in", "cp_conv_w", "cp_conv_b", "cp_ln_g",
          "cp_ln_b", "cp_pool_w", "cp_pool_scale", "cp_w_out", "gla_w_in", "gla_gate_w2", "gla_gate_b", "gla_head_g",
          "gla_w_out", "final_norm_g")
_SMALL_GRADS = ("mix_g", "ffn_g", "conv_b", "ln_g", "ln_b", "pool_w", "pool_scale", "final_g", "meta", "conv_w", "gate_w",
                "gate_b", "head_g")
_GROUPS = ("cp", "ffn0", "gla", "ffn1")


class _Exchanges:
    def __init__(self, w, d):
        self.d = d
        small = [w[n].reshape(w[n].shape[-2:]) for n in _SMALL_SHARDED]
        self.small_shard_shapes = [w[n].shape for n in _SMALL_SHARDED]
        shards = dict(
            cp=[(w["cp_w_in"][0].T, BF16), (w["cp_w_out"][0], BF16)] + [(a, F32) for a in small],
            ffn0=[(w["ffn_w1"][0], BF16), (w["ffn_w2"][0], BF16)],
            gla=[(w["gla_w_in"][0].T, BF16), (w["gla_w_out"][0], BF16)],
            ffn1=[(w["ffn_w1"][1], BF16), (w["ffn_w2"][1], BF16)])
        self.gathers = {}
        self.sent = {}
        token = None
        for group in _GROUPS:
            lands = [_place_own(a, dtype, after=token, name=f"place_w_{group}_{k}")
                     for k, (a, dtype) in enumerate(shards[group])]
            self.gathers[group], token = _exchange_start(_plan_gather_first, None, lands, after=token,
                                                         name=f"start_w_{group}")
        self.token = token

    def fetch(self, group, after):
        d = self.d
        after = (list(after) if isinstance(after, (list, tuple)) else [after]) + [self.token]
        _, lands = _exchange_wait(self.gathers[group], after, name=f"wait_w_{group}")
        got = _exchange_now(_plan_gather_relay, lands, name=f"relay_w_{group}")
        if group in ("ffn0", "ffn1"):
            return dict(w1=got[0], w2=got[1])
        if group == "gla":
            w_in_t = jnp.pad(got[0].reshape(-1, d), ((0, GATE_PAD - GATE_RANK), (0, 0)))
            return dict(gla_w_in_t=w_in_t, gla_w_out=got[1].reshape(d, d), gate_w=self.gate_w, gate_b=self.gate_b,
                        head_g=self.head_g)
        meta, conv_w, gate_w, self.gate_b, self.head_g = [_undo_column_split(a) for a in got[2:]]
        self.gate_w = jnp.pad(gate_w, ((0, GATE_PAD - GATE_RANK), (0, 0))).astype(BF16)
        return dict(cp_w_in_t=got[0].reshape(-1, d), cp_w_out=got[1].reshape(d, d), meta=meta,
                    conv_w=jnp.pad(conv_w, ((0, 1), (0, 0))))

    def emit(self, group, g):
        d = self.d
        if group in ("ffn0", "ffn1"):
            arrs = [g["w1"], g["w2"]]
        elif group == "gla":
            w_in_t = g["gla_w_in_t"][:3 * d + GATE_RANK]
            arrs = [w_in_t.reshape(N_DEV, -1, d), g["gla_w_out"].reshape(N_DEV, d // N_DEV, d)]
        elif group == "cp_out":
            arrs = [g["cp_w_out"].reshape(N_DEV, d // N_DEV, d)]
        else:
            s = dict(g["small"])
            s.update(pool_w=s["pool_w"][None], conv_w=s["conv_w"][:CONV_WIDTH], gate_w=s["gate_w"][:GATE_RANK])
            own = [s[n] for n in _SMALL_GRADS[:len(_REPLICATED)]]
            own += [_column_split(s[n]).reshape((N_DEV,) + shape)
                    for n, shape in zip(_SMALL_GRADS[len(_REPLICATED):], self.small_shard_shapes)]
            plans = [_plan_to_all] * len(_REPLICATED) + [_plan_split_to_all] * len(_SMALL_SHARDED)
            lands = [lax.empty((N_DEV,) + a.shape, F32) for a in own[:len(_REPLICATED)]]
            lands += [lax.empty(a.shape, F32) for a in own[len(_REPLICATED):]]
            self.small_sent, self.token = _exchange_start(plans, own, lands, after=self.token, name="start_g_small")
            arrs = [g["cp_w_in_t"].reshape(N_DEV, -1, d)]
        self.sent[group], self.token = _exchange_start(_plan_split_to_all, arrs, [lax.empty(a.shape, a.dtype) for a in arrs],
                                                       after=self.token, name=f"start_g_{group}")
        return self.token

    def finish(self, w, mom, var):
        out = {}
        after = self.token

        def landed(group):
            own, got = _exchange_wait(self.sent[group], after, name=f"wait_g_{group}")
            return list(zip(own, got))

        def adam(n, parts, behind=None, transposed=False):
            flip = (lambda a: jnp.transpose(a, (0, 2, 1))) if transposed else (lambda a: a)
            res = _reduce_adam(parts, flip(w[n]), flip(mom[n]), flip(var[n]), after=behind, name=f"adam_{n}")
            out[n] = tuple(flip(a) for a in res)
            return res[0]

        ffn1 = landed("ffn1")
        after = ffn1[0][1]
        gla = landed("gla")
        after = gla[0][1]
        ffn0 = landed("ffn0")
        after = adam("ffn_w1", [ffn0[0], ffn1[0]])
        after = adam("ffn_w2", [ffn0[1], ffn1[1]], after)
        after = adam("gla_w_in", [gla[0]], after, transposed=True)
        after = adam("gla_w_out", [gla[1]], after)
        small_own, small_landed = _exchange_wait(self.small_sent, after, name="wait_g_small")
        names = _REPLICATED + _SMALL_SHARDED
        split = [False] * len(_REPLICATED) + [True] * len(_SMALL_SHARDED)
        small_new = _adam_small(small_own, small_landed, split, [w[n] for n in names], [mom[n] for n in names],
                                [var[n] for n in names], name="adam_small")
        out.update(zip(names, small_new))

        after = small_new[0][0]
        cp_out = landed("cp_out")
        after = adam("cp_w_out", [cp_out[0]])
        cp = landed("cp")
        adam("cp_w_in", [cp[0]], transposed=True)
        return out


def kernel(x, meta_tokens, mix_norm_g, ffn_norm_g, ffn_w1, ffn_w2, cp_w_in, cp_conv_w, cp_conv_b, cp_ln_g, cp_ln_b, cp_pool_w, cp_pool_scale, cp_w_out, gla_w_in, gla_gate_w2, gla_gate_b, gla_head_g, gla_w_out, final_norm_g, loss_target, m_meta_tokens, m_mix_norm_g, m_ffn_norm_g, m_ffn_w1, m_ffn_w2, m_cp_w_in, m_cp_conv_w, m_cp_conv_b, m_cp_ln_g, m_cp_ln_b, m_cp_pool_w, m_cp_pool_scale, m_cp_w_out, m_gla_w_in, m_gla_gate_w2, m_gla_gate_b, m_gla_head_g, m_gla_w_out, m_final_norm_g, v_meta_tokens, v_mix_norm_g, v_ffn_norm_g, v_ffn_w1, v_ffn_w2, v_cp_w_in, v_cp_conv_w, v_cp_conv_b, v_cp_ln_g, v_cp_ln_b, v_cp_pool_w, v_cp_pool_scale, v_cp_w_out, v_gla_w_in, v_gla_gate_w2, v_gla_gate_b, v_gla_head_g, v_gla_w_out, v_final_norm_g):
    w = dict(meta_tokens=meta_tokens, mix_norm_g=mix_norm_g, ffn_norm_g=ffn_norm_g, ffn_w1=ffn_w1, ffn_w2=ffn_w2,
             cp_w_in=cp_w_in, cp_conv_w=cp_conv_w, cp_conv_b=cp_conv_b, cp_ln_g=cp_ln_g, cp_ln_b=cp_ln_b,
             cp_pool_w=cp_pool_w, cp_pool_scale=cp_pool_scale, cp_w_out=cp_w_out, gla_w_in=gla_w_in,
             gla_gate_w2=gla_gate_w2, gla_gate_b=gla_gate_b, gla_head_g=gla_head_g, gla_w_out=gla_w_out,
             final_norm_g=final_norm_g.reshape(1, -1))
    mom = dict(meta_tokens=m_meta_tokens, mix_norm_g=m_mix_norm_g, ffn_norm_g=m_ffn_norm_g, ffn_w1=m_ffn_w1, ffn_w2=m_ffn_w2,
               cp_w_in=m_cp_w_in, cp_conv_w=m_cp_conv_w, cp_conv_b=m_cp_conv_b, cp_ln_g=m_cp_ln_g, cp_ln_b=m_cp_ln_b,
               cp_pool_w=m_cp_pool_w, cp_pool_scale=m_cp_pool_scale, cp_w_out=m_cp_w_out, gla_w_in=m_gla_w_in,
               gla_gate_w2=m_gla_gate_w2, gla_gate_b=m_gla_gate_b, gla_head_g=m_gla_head_g, gla_w_out=m_gla_w_out,
               final_norm_g=m_final_norm_g.reshape(1, -1))
    var = dict(meta_tokens=v_meta_tokens, mix_norm_g=v_mix_norm_g, ffn_norm_g=v_ffn_norm_g, ffn_w1=v_ffn_w1, ffn_w2=v_ffn_w2,
               cp_w_in=v_cp_w_in, cp_conv_w=v_cp_conv_w, cp_conv_b=v_cp_conv_b, cp_ln_g=v_cp_ln_g, cp_ln_b=v_cp_ln_b,
               cp_pool_w=v_cp_pool_w, cp_pool_scale=v_cp_pool_scale, cp_w_out=v_cp_w_out, gla_w_in=v_gla_w_in,
               gla_gate_w2=v_gla_gate_w2, gla_gate_b=v_gla_gate_b, gla_head_g=v_gla_head_g, gla_w_out=v_gla_w_out,
               final_norm_g=v_final_norm_g.reshape(1, -1))
    d = x.shape[-1]
    replicated = dict(mix_g=w["mix_norm_g"], ffn_g=w["ffn_norm_g"], conv_b=w["cp_conv_b"], ln_g=w["cp_ln_g"],
                      ln_b=w["cp_ln_b"], pool_w=w["cp_pool_w"][0].astype(BF16), pool_scale=w["cp_pool_scale"],
                      final_g=w["final_norm_g"])
    exchanges = _Exchanges(w, d)
    loss_blk, grad_x, _ = _local_step(x[0], loss_target[0], replicated, exchanges)
    loss = lax.psum(loss_blk[0, 0], ("x", "y", "c"))
    out = exchanges.finish(w, mom, var)

    def leaf(n, k):
        a = out[n][k]
        return a.reshape(-1) if n == "final_norm_g" else a

    return (loss, grad_x[None], *[leaf(n, 0) for n in _NAMES], *[leaf(n, 1) for n in _NAMES],
            *[leaf(n, 2) for n in _NAMES], *[leaf(n, 3) for n in _NAMES])
```

```python
import functools

import jax
import jax.numpy as jnp
from jax import lax
from jax.experimental import pallas as pl
from jax.experimental.pallas import tpu as pltpu

F32, BF16 = jnp.float32, jnp.bfloat16
N_DEV = 8
CHUNK = 64
N_META = 16
PAD_ROWS = CHUNK - N_META
HALO = 32
EPS = 1e-5
CONV_WIDTH = 31
POOL_WINDOWS = (2, 4, 8, 16)
HEADS = 4
GATE_RANK = 16
GATE_NORM = 16.0
GATE_PAD = 128
ADAM_LR, ADAM_B1, ADAM_B2, ADAM_EPS, ADAM_WD, ADAM_STEP = 0.001, 0.9, 0.999, 1e-08, 0.01, 10
V7X_VMEM_LIMIT = 56 * 2 ** 20
LANE = 128
HI = lax.Precision.HIGHEST


def _cparams(*sem):
    return pltpu.CompilerParams(dimension_semantics=sem, vmem_limit_bytes=V7X_VMEM_LIMIT)


def _row_tile(t, cap):
    best = CHUNK
    for r in range(CHUNK, min(t, cap) + 1, CHUNK):
        if t % r == 0:
            best = r
    return best


def _resident(shape):
    return pl.BlockSpec(shape, lambda *_: (0,) * len(shape), pipeline_mode=pl.Buffered(1))


def _dot(a, b):
    return jnp.dot(a, b, preferred_element_type=F32)


def _dot_nt(a, b):
    return lax.dot_general(a, b, (((1,), (1,)), ((), ())), preferred_element_type=F32)


def _dot_tn(a, b):
    return lax.dot_general(a, b, (((0,), (0,)), ((), ())), preferred_element_type=F32)


def _rowsum(a):
    return jnp.sum(a, axis=0, keepdims=True)


def _sigmoid(a):
    return 1.0 / (1.0 + jnp.exp(-a))


def _row_ids(tile, rt):
    return tile * rt + lax.broadcasted_iota(jnp.int32, (rt, 1), 0)


def _sds(shape, dtype):
    return jax.ShapeDtypeStruct(shape, dtype)


DW_ROWS = 1024


def _linear_bwd_w(x, dy, *, name):
    t, k = x.shape
    n = dy.shape[1]
    cut_k = k > n
    width = k if cut_k else n
    blk = max(c for c in (640, 512, 384, 256, LANE) if width % c == 0)

    def body(x_ref, dy_ref, o_ref, acc):
        for c0 in range(0, t, DW_ROWS):
            rows = slice(c0, min(c0 + DW_ROWS, t))
            part = _dot_tn(x_ref[rows, :].astype(BF16), dy_ref[rows, :].astype(BF16))
            if c0 == 0:
                acc[...] = part
            else:
                acc[...] += part
        o_ref[...] = acc[...].astype(BF16)

    if cut_k:
        in_specs = [pl.BlockSpec((t, blk), lambda j: (0, j)), _resident((t, n))]
        out_specs = pl.BlockSpec((blk, n), lambda j: (j, 0))
        acc_shape = (blk, n)
    else:
        in_specs = [_resident((t, k)), pl.BlockSpec((t, blk), lambda j: (0, j))]
        out_specs = pl.BlockSpec((k, blk), lambda j: (0, j))
        acc_shape = (k, blk)
    return pl.pallas_call(
        body, grid=(width // blk,), in_specs=in_specs, out_specs=out_specs, out_shape=_sds((k, n), BF16),
        scratch_shapes=[pltpu.VMEM(acc_shape, F32)], compiler_params=_cparams("parallel"), name=name)(x, dy)


def _ffn_fwd(h, gain, w1g, w2g, *, name):
    t, d = h.shape
    f8 = w1g.shape[-1]
    rt = _row_tile(t, 832)

    def body(h_ref, g_ref, w1_ref, w2_ref, o_ref, u_ref, r_ref, acc_ref):
        j = pl.program_id(1)

        @pl.when(j == 0)
        def _():
            hv = h_ref[...]
            u_ref[...] = (hv * lax.rsqrt(jnp.mean(hv * hv, axis=-1, keepdims=True) + EPS) * g_ref[...]).astype(BF16)
            acc_ref[...] = jnp.zeros_like(acc_ref)

        a = jnp.maximum(_dot(u_ref[...], w1_ref[...]), 0.0)
        r_ref[...] = a.astype(BF16)
        acc_ref[...] += _dot((a * a).astype(BF16), w2_ref[...])

        @pl.when(j == N_DEV - 1)
        def _():
            o_ref[...] = h_ref[...] + acc_ref[...]

    return pl.pallas_call(
        body, grid=(t // rt, N_DEV),
        in_specs=[pl.BlockSpec((rt, d), lambda i, j: (i, 0)), _resident((1, d)),
                  pl.BlockSpec((None, d, f8), lambda i, j: (j, 0, 0)),
                  pl.BlockSpec((None, f8, d), lambda i, j: (j, 0, 0))],
        out_specs=[pl.BlockSpec((rt, d), lambda i, j: (i, 0)), pl.BlockSpec((rt, d), lambda i, j: (i, 0)),
                   pl.BlockSpec((rt, f8), lambda i, j: (i, j))],
        out_shape=[_sds((t, d), F32), _sds((t, d), BF16), _sds((t, N_DEV * f8), BF16)],
        scratch_shapes=[pltpu.VMEM((rt, d), F32)],
        compiler_params=_cparams("parallel", "arbitrary"), name=name)(h, gain, w1g, w2g)


def _ffn_bwd_x(h, dout, gain, r, w1g, w2g, *, after=None, name):
    t, d = h.shape
    f8 = w1g.shape[-1]
    rt = _row_tile(t, 832)
    last = N_DEV - 1
    dep_specs, deps = _dep_specs(after)

    def body(*refs):
        h_ref, do_ref, g_ref, r_ref, w1_ref, w2_ref, dh_ref, dhh_ref, dob_ref, dg_ref, du_ref = refs[len(deps):]
        i, j = pl.program_id(0), pl.program_id(1)

        @pl.when(j == 0)
        def _():
            dob_ref[...] = do_ref[...].astype(BF16)
            du_ref[...] = jnp.zeros_like(du_ref)

        dhh = (_dot_nt(dob_ref[...], w2_ref[...]) * (2.0 * r_ref[...].astype(F32))).astype(BF16)
        dhh_ref[...] = dhh
        du_ref[...] += _dot_nt(dhh, w1_ref[...])

        @pl.when(j == last)
        def _():
            @pl.when(i == 0)
            def _():
                dg_ref[...] = jnp.zeros_like(dg_ref)

            hv = h_ref[...]
            rstd = lax.rsqrt(jnp.mean(hv * hv, axis=-1, keepdims=True) + EPS)
            xh = hv * rstd
            du = du_ref[...]
            dg_ref[...] += _rowsum(du * xh)
            dxh = du * g_ref[...]
            dh_ref[...] = do_ref[...] + rstd * (dxh - xh * jnp.mean(dxh * xh, axis=-1, keepdims=True))

    rows = lambda i, j: (i, 0)
    return pl.pallas_call(
        body, grid=(t // rt, N_DEV),
        in_specs=dep_specs + [
                  pl.BlockSpec((rt, d), rows), pl.BlockSpec((rt, d), rows), _resident((1, d)),
                  pl.BlockSpec((rt, f8), lambda i, j: (i, j)),
                  pl.BlockSpec((None, d, f8), lambda i, j: (j, 0, 0)),
                  pl.BlockSpec((None, f8, d), lambda i, j: (j, 0, 0))],
        out_specs=[pl.BlockSpec((rt, d), rows), pl.BlockSpec((rt, f8), lambda i, j: (i, j)), pl.BlockSpec((rt, d), rows),
                   pl.BlockSpec((1, d), lambda i, j: (0, 0))],
        out_shape=[_sds((t, d), F32), _sds((t, N_DEV * f8), BF16), _sds((t, d), BF16), _sds((1, d), F32)],
        scratch_shapes=[pltpu.VMEM((rt, d), F32)],
        compiler_params=_cparams("arbitrary", "arbitrary"), name=name)(*deps, h, dout, gain, r, w1g, w2g)


def _ffn_bwd_w(u, dhh, r, dout_b, *, name):
    t, d = u.shape
    f8 = dhh.shape[1] // N_DEV

    def body(u_ref, dhh_ref, r_ref, dob_ref, dw1_ref, dw2_ref, acc1, acc2):
        for c0 in range(0, t, DW_ROWS):
            rows = slice(c0, min(c0 + DW_ROWS, t))
            rr = r_ref[rows, :].astype(F32)
            part1 = _dot_tn(u_ref[rows, :], dhh_ref[rows, :])
            part2 = _dot_tn((rr * rr).astype(BF16), dob_ref[rows, :])
            if c0 == 0:
                acc1[...] = part1
                acc2[...] = part2
            else:
                acc1[...] += part1
                acc2[...] += part2
        dw1_ref[...] = acc1[...].astype(BF16)
        dw2_ref[...] = acc2[...].astype(BF16)

    return pl.pallas_call(
        body, grid=(N_DEV,),
        in_specs=[_resident((t, d)), pl.BlockSpec((t, f8), lambda j: (0, j)), pl.BlockSpec((t, f8), lambda j: (0, j)),
                  _resident((t, d))],
        out_specs=[pl.BlockSpec((None, d, f8), lambda j: (j, 0, 0)), pl.BlockSpec((None, f8, d), lambda j: (j, 0, 0))],
        out_shape=[_sds((N_DEV, d, f8), BF16), _sds((N_DEV, f8, d), BF16)],
        scratch_shapes=[pltpu.VMEM((d, f8), F32), pltpu.VMEM((f8, d), F32)],
        compiler_params=_cparams("parallel"), name=name)(u, dhh, r, dout_b)


def _lane_blocks(width):
    lb = min(LANE, width)
    return [slice(s, s + lb) for s in range(0, width, lb)]


def _conv_rows(src_ref, w_ref, offset, dst_ref, nblk, width, bias_ref=None):
    def blk(rb, carry):
        base = pl.multiple_of(rb * CHUNK, CHUNK)
        for l, ls in enumerate(_lane_blocks(width)):
            acc = jnp.zeros((CHUNK, ls.stop - ls.start), F32)
            if bias_ref is not None:
                acc = acc + bias_ref[:, ls]
            for k in range(CONV_WIDTH):
                acc = acc + w_ref[k:k + 1, ls] * src_ref[l, pl.ds(base + offset(k), CHUNK), :]
            dst_ref[l, pl.ds(base, CHUNK), :] = acc
        return carry

    lax.fori_loop(0, nblk, blk, 0)


def _to_lane_blocks(ref, row0, value):
    for l, ls in enumerate(_lane_blocks(value.shape[1])):
        ref[l, row0:row0 + value.shape[0], :] = value[:, ls]


def _from_lane_blocks(ref):
    return jnp.concatenate([ref[l] for l in range(ref.shape[0])], axis=1)


def _pool_counts(rows, window):
    return jnp.clip(rows - PAD_ROWS + 1, 1, window).astype(F32)


def _trailing_sum(v, window):
    s, sh = v, 1
    while sh < window:
        s = s + pltpu.roll(s, sh, 0)
        sh *= 2
    return s


def _leading_sum(v, window):
    s, sh, n = v, 1, v.shape[0]
    while sh < window:
        s = s + pltpu.roll(s, n - sh, 0)
        sh *= 2
    return s


def _norm_project(h_ref, g_ref, w_t_ref, u_ref, z_ref):
    hv = h_ref[...]
    u = (hv * lax.rsqrt(jnp.mean(hv * hv, axis=-1, keepdims=True) + EPS) * g_ref[...]).astype(BF16)
    u_ref[...] = u
    z_ref[...] = _dot_nt(u, w_t_ref[...])


def _project_back(dz_ref, w_t_ref, h_ref, g_ref, dres_ref, dh_ref, dg_ref, first):
    dx = _dot(dz_ref[...], w_t_ref[...])
    hv = h_ref[...]
    rstd = lax.rsqrt(jnp.mean(hv * hv, axis=-1, keepdims=True) + EPS)
    xh = hv * rstd

    @pl.when(first)
    def _():
        dg_ref[...] = jnp.zeros_like(dg_ref)

    dg_ref[...] += _rowsum(dx * xh)
    dxh = dx * g_ref[...]
    dh_ref[...] = dres_ref[...] + rstd * (dxh - xh * jnp.mean(dxh * xh, axis=-1, keepdims=True))


def _cp_mid_fwd(h, gain, w_in_t, w_out, conv_w, conv_b, ln_g, ln_b, pool_w, pool_scale, *, name):
    t, d = h.shape
    ein = w_in_t.shape[0]
    cd = conv_b.shape[1]
    pd = pool_scale.shape[1]
    pg = pd // len(POOL_WINDOWS)
    rt = _row_tile(t, 320)

    def body(h_ref, g_ref, wi_ref, wo_ref, cw_ref, cb_ref, lg_ref, lb_ref, pw_ref, ps_ref,
             ho_ref, o_ref, z_ref, u_ref, gext, pext, conv_s):
        i = pl.program_id(0)

        @pl.when(i == 0)
        def _():
            _to_lane_blocks(gext, 0, jnp.zeros((HALO, cd), F32))
            pext[0:HALO, :] = jnp.zeros((HALO, pd), F32)

        _norm_project(h_ref, g_ref, wi_ref, u_ref, z_ref)

        _to_lane_blocks(gext, HALO, z_ref[:, 0:cd] * _sigmoid(z_ref[:, cd:2 * cd]))
        pext[HALO:HALO + rt, :] = z_ref[:, 2 * cd:]
        _conv_rows(gext, cw_ref, lambda k: k + HALO - (CONV_WIDTH - 1), conv_s, rt // CHUNK, cd, cb_ref)
        cv = _from_lane_blocks(conv_s)
        xc = cv - jnp.mean(cv, axis=-1, keepdims=True)
        y = xc * lax.rsqrt(jnp.mean(xc * xc, axis=-1, keepdims=True) + EPS) * lg_ref[...] + lb_ref[...]
        rows = _row_ids(i, rt)
        a = jnp.where(rows >= PAD_ROWS, y * _sigmoid(y), 0.0)
        o_ref[:, 0:cd] = a.astype(BF16)
        for gi, window in enumerate(POOL_WINDOWS):
            ls = slice(gi * pg, (gi + 1) * pg)
            v = pext[:, ls]
            tm = _trailing_sum(v, window)[HALO:] / _pool_counts(rows, window) - v[HALO:]
            p = _dot(tm.astype(BF16), pw_ref[gi]) * ps_ref[:, ls]
            o_ref[:, cd + gi * pg:cd + (gi + 1) * pg] = p.astype(BF16)
        ho_ref[...] = h_ref[...] + _dot(o_ref[...], wo_ref[...])
        gext[:, 0:HALO, :] = gext[:, rt:rt + HALO, :]
        pext[0:HALO, :] = pext[rt:rt + HALO, :]

    nl, lb = len(_lane_blocks(cd)), min(LANE, cd)
    rows = lambda i: (i, 0)
    return pl.pallas_call(
        body, grid=(t // rt,),
        in_specs=[pl.BlockSpec((rt, d), rows), _resident((1, d)), _resident(w_in_t.shape), _resident(w_out.shape),
                  _resident(conv_w.shape), _resident((1, cd)),
                  _resident((1, cd)), _resident((1, cd)), _resident(pool_w.shape), _resident((1, pd))],
        out_specs=[pl.BlockSpec((rt, d), rows), pl.BlockSpec((rt, cd + pd), rows), pl.BlockSpec((rt, ein), rows),
                   pl.BlockSpec((rt, d), rows)],
        out_shape=[_sds((t, d), F32), _sds((t, cd + pd), BF16), _sds((t, ein), F32), _sds((t, d), BF16)],
        scratch_shapes=[pltpu.VMEM((nl, rt + HALO, lb), F32), pltpu.VMEM((rt + HALO, pd), F32),
                        pltpu.VMEM((nl, rt, lb), F32)],
        compiler_params=_cparams("arbitrary"), name=name)(h, gain, w_in_t, w_out, conv_w, conv_b, ln_g, ln_b, pool_w,
                                                          pool_scale)


def _cp_mid_bwd(z, h, gain, w_in_t, dh, w_out, conv_w, conv_b, ln_g, ln_b, pool_w, pool_scale, *, after=None, name):
    t, ein = z.shape
    cd = conv_b.shape[1]
    pd = pool_scale.shape[1]
    pg = pd // len(POOL_WINDOWS)
    rt = _row_tile(t, 320)
    ntile = t // rt
    per = rt // CHUNK
    dep_specs, deps = _dep_specs(after)

    def body(*refs):
        (z_ref, zh_ref, h_ref, g_ref, wi_ref, dh_ref, wo_ref, cw_ref, cb_ref, lg_ref, lb_ref, pw_ref, ps_ref,
         dhi_ref, dg_ref, dz_ref, dcw_ref, dcb_ref, dlg_ref, dlb_ref, dpw_ref, dps_ref,
         gext, pext, conv_s, dcv, dsp) = refs[len(deps):]
        step = pl.program_id(0)
        tile = ntile - 1 - step
        dcat = _dot_nt(dh_ref[...].astype(BF16), wo_ref[...])

        @pl.when(step == 0)
        def _():
            for ref in (dcw_ref, dcb_ref, dlg_ref, dlb_ref, dpw_ref, dps_ref):
                ref[...] = jnp.zeros_like(ref)
            _to_lane_blocks(dcv, rt, jnp.zeros((HALO, cd), F32))
            dsp[rt:rt + HALO, :] = jnp.zeros((HALO, pd), F32)

        keep = jnp.where(tile > 0, 1.0, 0.0)
        zh = zh_ref[CHUNK - HALO:CHUNK, :]
        _to_lane_blocks(gext, 0, keep * zh[:, 0:cd] * _sigmoid(zh[:, cd:2 * cd]))
        pext[0:HALO, :] = keep * zh[:, 2 * cd:]
        za = z_ref[:, 0:cd]
        sg = _sigmoid(z_ref[:, cd:2 * cd])
        _to_lane_blocks(gext, HALO, za * sg)
        pext[HALO:HALO + rt, :] = z_ref[:, 2 * cd:]
        _conv_rows(gext, cw_ref, lambda k: k + HALO - (CONV_WIDTH - 1), conv_s, per, cd, cb_ref)
        cv = _from_lane_blocks(conv_s)
        xc = cv - jnp.mean(cv, axis=-1, keepdims=True)
        rstd = lax.rsqrt(jnp.mean(xc * xc, axis=-1, keepdims=True) + EPS)
        xh = xc * rstd
        y = xh * lg_ref[...] + lb_ref[...]
        sy = _sigmoid(y)
        rows = _row_ids(tile, rt)
        da = jnp.where(rows >= PAD_ROWS, dcat[:, 0:cd], 0.0)
        dy = da * (sy * (1.0 + y * (1.0 - sy)))
        dlg_ref[...] += _rowsum(dy * xh)
        dlb_ref[...] += _rowsum(dy)
        dxh = dy * lg_ref[...]
        dconv = rstd * (dxh - jnp.mean(dxh, axis=-1, keepdims=True) - xh * jnp.mean(dxh * xh, axis=-1, keepdims=True))
        dcb_ref[...] += _rowsum(dconv)
        _to_lane_blocks(dcv, 0, dconv)
        for l, ls in enumerate(_lane_blocks(cd)):
            def acc_rows(rb, accs, l=l):
                base = pl.multiple_of(rb * CHUNK, CHUNK)
                d_blk = dcv[l, pl.ds(base, CHUNK), :]
                out = []
                for k in range(CONV_WIDTH):
                    prod = d_blk * gext[l, pl.ds(base + k + HALO - (CONV_WIDTH - 1), CHUNK), :]
                    part = prod[0:8]
                    for s in range(8, CHUNK, 8):
                        part = part + prod[s:s + 8]
                    out.append(accs[k] + part)
                return tuple(out)

            zero = jnp.zeros((8, ls.stop - ls.start), F32)
            accs = lax.fori_loop(0, per, acc_rows, (zero,) * CONV_WIDTH)
            for k in range(CONV_WIDTH):
                dcw_ref[k:k + 1, ls] += _rowsum(accs[k])
        _conv_rows(dcv, cw_ref, lambda k: CONV_WIDTH - 1 - k, conv_s, per, cd)
        dglu = _from_lane_blocks(conv_s)
        dz_ref[:, 0:cd] = (dglu * sg).astype(BF16)
        dz_ref[:, cd:2 * cd] = (dglu * za * sg * (1.0 - sg)).astype(BF16)
        dcv[:, rt:rt + HALO, :] = dcv[:, 0:HALO, :]
        for gi, window in enumerate(POOL_WINDOWS):
            ls = slice(gi * pg, (gi + 1) * pg)
            v = pext[:, ls]
            cnt = _pool_counts(rows, window)
            tm = (_trailing_sum(v, window)[HALO:] / cnt - v[HALO:]).astype(BF16)
            dp = dcat[:, cd + gi * pg:cd + (gi + 1) * pg]
            dps_ref[:, ls] += _rowsum(dp * _dot(tm, pw_ref[gi]))
            dpl = (dp * ps_ref[:, ls]).astype(BF16)
            dpw_ref[gi] += _dot_tn(tm, dpl)
            dtm = _dot_nt(dpl, pw_ref[gi])
            dsp[0:rt, ls] = dtm / cnt
            dpin = _leading_sum(dsp[:, ls], window)[0:rt] - dtm
            dz_ref[:, 2 * cd + gi * pg:2 * cd + (gi + 1) * pg] = dpin.astype(BF16)
        dsp[rt:rt + HALO, :] = dsp[0:HALO, :]
        _project_back(dz_ref, wi_ref, h_ref, g_ref, dh_ref, dhi_ref, dg_ref, step == 0)

    d = h.shape[1]
    back = lambda i: (ntile - 1 - i, 0)
    halo_idx = lambda i: (jnp.maximum((ntile - 1 - i) * per - 1, 0), 0)
    const2 = lambda i: (0, 0)
    nl, lb = len(_lane_blocks(cd)), min(LANE, cd)
    return pl.pallas_call(
        body, grid=(ntile,),
        in_specs=dep_specs + [
                  pl.BlockSpec((rt, ein), back), pl.BlockSpec((CHUNK, ein), halo_idx), pl.BlockSpec((rt, d), back),
                  _resident((1, d)), _resident(w_in_t.shape), pl.BlockSpec((rt, d), back), _resident(w_out.shape),
                  _resident(conv_w.shape), _resident((1, cd)), _resident((1, cd)), _resident((1, cd)),
                  _resident(pool_w.shape), _resident((1, pd))],
        out_specs=[pl.BlockSpec((rt, d), back), pl.BlockSpec((1, d), const2),
                   pl.BlockSpec((rt, ein), back), pl.BlockSpec(conv_w.shape, const2), pl.BlockSpec((1, cd), const2),
                   pl.BlockSpec((1, cd), const2), pl.BlockSpec((1, cd), const2),
                   pl.BlockSpec(pool_w.shape, lambda i: (0, 0, 0)), pl.BlockSpec((1, pd), const2)],
        out_shape=[_sds((t, d), F32), _sds((1, d), F32),
                   _sds((t, ein), BF16), _sds(conv_w.shape, F32), _sds((1, cd), F32), _sds((1, cd), F32),
                   _sds((1, cd), F32), _sds(pool_w.shape, F32), _sds((1, pd), F32)],
        scratch_shapes=[pltpu.VMEM((nl, rt + HALO, lb), F32), pltpu.VMEM((rt + HALO, pd), F32), pltpu.VMEM((nl, rt, lb), F32),
                        pltpu.VMEM((nl, rt + HALO, lb), F32), pltpu.VMEM((rt + HALO, pd), F32)],
        compiler_params=_cparams("arbitrary"), name=name)(*deps, z, z, h, gain, w_in_t, dh, w_out, conv_w, conv_b, ln_g,
                                                          ln_b, pool_w, pool_scale)


def _log_decay(r, gw_ref, gb_ref, rows):
    gp = _dot(r.astype(BF16), gw_ref[...]) + gb_ref[...]
    log_sig = jnp.minimum(gp, 0.0) - jnp.log(1.0 + jnp.exp(-jnp.abs(gp)))
    return gp, jnp.where(rows >= PAD_ROWS, log_sig / GATE_NORM, 0.0)


def _tri(strict):
    r = lax.broadcasted_iota(jnp.int32, (CHUNK, CHUNK), 0)
    c = lax.broadcasted_iota(jnp.int32, (CHUNK, CHUNK), 1)
    return jnp.where(c < r if strict else c <= r, 1.0, 0.0).astype(F32)


def _gla_mid_fwd(h, gain, w_in_t, w_out, gate_w, gate_b, head_g, *, name):
    t = h.shape[0]
    zw = w_in_t.shape[0]
    dk = gate_b.shape[1]
    hv = head_g.shape[1]
    hk = dk // HEADS
    dv = hv * HEADS
    r_at = 2 * dk + 2 * dv
    rt = _row_tile(t, 320)
    per = rt // CHUNK
    scale = hk ** -0.5

    def body(h_ref, g_ref, wi_ref, wo_ref, gw_ref, gb_ref, hg_ref, ho_ref, o_ref, st_ref, z_ref, u_ref,
             s_ref, la_ref, dec_ref):
        i = pl.program_id(0)

        @pl.when(i == 0)
        def _():
            s_ref[...] = jnp.zeros_like(s_ref)

        _norm_project(h_ref, g_ref, wi_ref, u_ref, z_ref)

        _, la = _log_decay(z_ref[:, r_at:r_at + GATE_PAD], gw_ref, gb_ref, _row_ids(i, rt))
        la_ref[...] = la
        tri = _tri(False)

        def chunk_rows(c):
            return slice(c * CHUNK, (c + 1) * CHUNK)

        def decays(c, carry):
            rows = chunk_rows(c)
            la_c = la_ref[rows, :]
            cum = jnp.dot(tri, la_c, precision=HI, preferred_element_type=F32)
            dec_ref[rows, :] = jnp.exp(_rowsum(la_c) - cum)
            return carry

        def states(c, carry):
            rows = chunk_rows(c)
            etot = jnp.exp(_rowsum(la_ref[rows, :]))
            for hd in range(HEADS):
                ks = slice(hd * hk, (hd + 1) * hk)
                kd = z_ref[rows, dk + hd * hk:dk + (hd + 1) * hk] * dec_ref[rows, ks]
                v = z_ref[rows, 2 * dk + hd * hv:2 * dk + (hd + 1) * hv]
                s_new = s_ref[hd] * etot[:, ks] + _dot_tn(v.astype(BF16), kd.astype(BF16))
                s_ref[hd] = s_new
                st_ref[c, hd] = s_new
            return carry

        def outputs(c, carry):
            rows = chunk_rows(c)
            for hd in range(HEADS):
                q = z_ref[rows, hd * hk:(hd + 1) * hk] * scale
                g = z_ref[rows, 2 * dk + dv + hd * hv:2 * dk + dv + (hd + 1) * hv]
                o = _dot_nt(q.astype(BF16), st_ref[c, hd].astype(BF16))
                on = o * lax.rsqrt(jnp.mean(o * o, axis=-1, keepdims=True) + EPS) * hg_ref[...]
                o_ref[rows, hd * hv:(hd + 1) * hv] = (on * (g * _sigmoid(g))).astype(BF16)
            return carry

        for phase in (decays, states, outputs):
            for c in range(per):
                phase(c, 0)
        ho_ref[...] = h_ref[...] + _dot(o_ref[...], wo_ref[...])

    d = h.shape[1]
    rows = lambda i: (i, 0)
    return pl.pallas_call(
        body, grid=(t // rt,),
        in_specs=[pl.BlockSpec((rt, d), rows), _resident((1, d)), _resident(w_in_t.shape), _resident(w_out.shape),
                  _resident(gate_w.shape), _resident((1, dk)), _resident((1, hv))],
        out_specs=[pl.BlockSpec((rt, d), rows), pl.BlockSpec((rt, dv), rows),
                   pl.BlockSpec((per, HEADS, hv, hk), lambda i: (i, 0, 0, 0)), pl.BlockSpec((rt, zw), rows),
                   pl.BlockSpec((rt, d), rows)],
        out_shape=[_sds((t, d), F32), _sds((t, dv), BF16), _sds((t // CHUNK, HEADS, hv, hk), F32), _sds((t, zw), F32),
                   _sds((t, d), BF16)],
        scratch_shapes=[pltpu.VMEM((HEADS, hv, hk), F32), pltpu.VMEM((rt, dk), F32), pltpu.VMEM((rt, dk), F32)],
        compiler_params=_cparams("arbitrary"), name=name)(h, gain, w_in_t, w_out, gate_w, gate_b, head_g)


def _gla_mid_bwd(z, h, gain, w_in_t, dh, w_out, states, gate_w, gate_b, head_g, *, after=None, name):
    t = z.shape[0]
    dk = gate_b.shape[1]
    hv = head_g.shape[1]
    hk = dk // HEADS
    dv = hv * HEADS
    r_at = 2 * dk + 2 * dv
    rt = _row_tile(t, 320)
    ntile = t // rt
    per = rt // CHUNK
    scale = hk ** -0.5
    dep_specs, deps = _dep_specs(after)

    def body(*refs):
        (z_ref, h_ref, g_ref, wi_ref, dh_ref, wo_ref, st_ref, stp_ref, gw_ref, gb_ref, hg_ref,
         dhi_ref, dg_ref, dz_ref, dgw_ref, dgb_ref, dhg_ref,
         ds_ref, la_ref, dla_ref, dec_ref, dos_ref, e_ref, do_ref) = refs[len(deps):]
        step = pl.program_id(0)
        tile = ntile - 1 - step
        do_ref[...] = _dot_nt(dh_ref[...].astype(BF16), wo_ref[...])

        @pl.when(step == 0)
        def _():
            ds_ref[...] = jnp.zeros_like(ds_ref)
            dgw_ref[...] = jnp.zeros_like(dgw_ref)
            dgb_ref[...] = jnp.zeros_like(dgb_ref)
            dhg_ref[...] = jnp.zeros_like(dhg_ref)

        rows_id = _row_ids(tile, rt)
        r = z_ref[:, r_at:r_at + GATE_PAD]
        gp, la = _log_decay(r, gw_ref, gb_ref, rows_id)
        la_ref[...] = la
        tri, tri_strict = _tri(False), _tri(True)
        keep = jnp.where(tile > 0, 1.0, 0.0)

        def chunk_rows(c):
            return slice(c * CHUNK, (c + 1) * CHUNK)

        def recompute(c, dhg):
            rows = chunk_rows(c)
            la_c = la_ref[rows, :]
            cum = jnp.dot(tri, la_c, precision=HI, preferred_element_type=F32)
            dec_ref[rows, :] = jnp.exp(_rowsum(la_c) - cum)
            for hd in range(HEADS):
                q = (z_ref[rows, hd * hk:(hd + 1) * hk] * scale).astype(BF16)
                g = z_ref[rows, 2 * dk + dv + hd * hv:2 * dk + dv + (hd + 1) * hv]
                s_b = st_ref[c, hd].astype(BF16)
                o = _dot_nt(q, s_b)
                rstd = lax.rsqrt(jnp.mean(o * o, axis=-1, keepdims=True) + EPS)
                oh = o * rstd
                sg = _sigmoid(g)
                d_og = do_ref[rows, hd * hv:(hd + 1) * hv]
                dz_ref[rows, 2 * dk + dv + hd * hv:2 * dk + dv + (hd + 1) * hv] = (
                    d_og * oh * hg_ref[...] * (sg * (1.0 + g * (1.0 - sg)))).astype(BF16)
                don = d_og * (g * sg)
                dhg = dhg + _rowsum(don * oh)
                doh = don * hg_ref[...]
                d_o = (rstd * (doh - oh * jnp.mean(doh * oh, axis=-1, keepdims=True))).astype(BF16)
                dos_ref[rows, hd * hv:(hd + 1) * hv] = d_o
                dz_ref[rows, hd * hk:(hd + 1) * hk] = (_dot(d_o, s_b) * scale).astype(BF16)
            return dhg

        def recurrence(cc, carry):
            c = per - 1 - cc
            rows = chunk_rows(c)
            etot = jnp.exp(_rowsum(la_ref[rows, :]))
            for hd in range(HEADS):
                ks = slice(hd * hk, (hd + 1) * hk)
                q = (z_ref[rows, hd * hk:(hd + 1) * hk] * scale).astype(BF16)
                dec = dec_ref[rows, ks]
                kd = z_ref[rows, dk + hd * hk:dk + (hd + 1) * hk] * dec
                v = z_ref[rows, 2 * dk + hd * hv:2 * dk + (hd + 1) * hv].astype(BF16)
                s_prev = st_ref[c - 1, hd] if c > 0 else keep * stp_ref[0, hd]
                ds_t = ds_ref[hd] + _dot_tn(dos_ref[rows, hd * hv:(hd + 1) * hv], q)
                ds_b = ds_t.astype(BF16)
                dkd = _dot(v, ds_b)
                dz_ref[rows, 2 * dk + hd * hv:2 * dk + (hd + 1) * hv] = _dot_nt(kd.astype(BF16), ds_b).astype(BF16)
                dtot = etot[:, ks] * _rowsum(ds_t * s_prev)
                ds_ref[hd] = ds_t * etot[:, ks]
                dz_ref[rows, dk + hd * hk:dk + (hd + 1) * hk] = (dkd * dec).astype(BF16)
                e_ref[rows, ks] = dkd * kd
                dla_ref[rows, ks] = jnp.broadcast_to(dtot, (CHUNK, hk))
            return carry

        def decay_cotangent(c, carry):
            rows = chunk_rows(c)
            dla_ref[rows, :] += jnp.dot(tri_strict, e_ref[rows, :], precision=HI, preferred_element_type=F32)
            return carry

        dhg = jnp.zeros((1, hv), F32)
        for c in range(per):
            dhg = recompute(c, dhg)
        dhg_ref[...] += dhg
        for phase in (recurrence, decay_cotangent):
            for c in range(per):
                phase(c, 0)
        dla = jnp.where(rows_id >= PAD_ROWS, dla_ref[...], 0.0)
        dgp = dla * (1.0 / GATE_NORM) * (1.0 - _sigmoid(gp))
        dgb_ref[...] += _rowsum(dgp)
        dgp_b = dgp.astype(BF16)
        dgw_ref[...] += _dot_tn(r.astype(BF16), dgp_b)
        dz_ref[:, r_at:r_at + GATE_PAD] = _dot_nt(dgp_b, gw_ref[...]).astype(BF16)
        _project_back(dz_ref, wi_ref, h_ref, g_ref, dh_ref, dhi_ref, dg_ref, step == 0)

    d = h.shape[1]
    back = lambda i: (ntile - 1 - i, 0)
    const2 = lambda i: (0, 0)
    return pl.pallas_call(
        body, grid=(ntile,),
        in_specs=dep_specs + [
                  pl.BlockSpec((rt, z.shape[1]), back), pl.BlockSpec((rt, d), back), _resident((1, d)),
                  _resident(w_in_t.shape), pl.BlockSpec((rt, d), back), _resident(w_out.shape),
                  pl.BlockSpec((per, HEADS, hv, hk), lambda i: (ntile - 1 - i, 0, 0, 0)),
                  pl.BlockSpec((1, HEADS, hv, hk), lambda i: (jnp.maximum((ntile - 1 - i) * per - 1, 0), 0, 0, 0)),
                  _resident(gate_w.shape), _resident((1, dk)), _resident((1, hv))],
        out_specs=[pl.BlockSpec((rt, d), back), pl.BlockSpec((1, d), const2), pl.BlockSpec((rt, z.shape[1]), back),
                   pl.BlockSpec(gate_w.shape, const2), pl.BlockSpec((1, dk), const2), pl.BlockSpec((1, hv), const2)],
        out_shape=[_sds((t, d), F32), _sds((1, d), F32), _sds(z.shape, BF16), _sds(gate_w.shape, F32),
                   _sds((1, dk), F32), _sds((1, hv), F32)],
        scratch_shapes=[pltpu.VMEM((HEADS, hv, hk), F32), pltpu.VMEM((rt, dk), F32), pltpu.VMEM((rt, dk), F32),
                        pltpu.VMEM((rt, dk), F32), pltpu.VMEM((rt, dv), BF16), pltpu.VMEM((rt, dk), F32),
                        pltpu.VMEM((rt, dv), F32)],
        compiler_params=_cparams("arbitrary"), name=name)(*deps, z, h, gain, w_in_t, dh, w_out, states, states, gate_w,
                                                          gate_b, head_g)


def _head(h, gain, target, *, name):
    t, d = h.shape
    rt = _row_tile(t, 832)

    def body(h_ref, g_ref, t_ref, dh_ref, loss_ref, dg_ref):
        i = pl.program_id(0)

        @pl.when(i == 0)
        def _():
            loss_ref[...] = jnp.zeros_like(loss_ref)
            dg_ref[...] = jnp.zeros_like(dg_ref)

        hv = h_ref[...]
        rstd = lax.rsqrt(jnp.mean(hv * hv, axis=-1, keepdims=True) + EPS)
        xh = hv * rstd
        err = jnp.where(_row_ids(i, rt) >= CHUNK, xh * g_ref[...] - t_ref[...], 0.0)
        loss_ref[...] += (0.5 / d) * jnp.sum(err * err)
        dy = err * (1.0 / d)
        dg_ref[...] += _rowsum(dy * xh)
        dxh = dy * g_ref[...]
        dh_ref[...] = rstd * (dxh - xh * jnp.mean(dxh * xh, axis=-1, keepdims=True))

    return pl.pallas_call(
        body, grid=(t // rt,),
        in_specs=[pl.BlockSpec((rt, d), lambda i: (i, 0)), _resident((1, d)), pl.BlockSpec((rt, d), lambda i: (i, 0))],
        out_specs=[pl.BlockSpec((rt, d), lambda i: (i, 0)), pl.BlockSpec((8, LANE), lambda i: (0, 0)),
                   pl.BlockSpec((1, d), lambda i: (0, 0))],
        out_shape=[_sds((t, d), F32), _sds((8, LANE), F32), _sds((1, d), F32)],
        compiler_params=_cparams("arbitrary"), name=name)(h, gain, target)


def _adamw_math(w, g, m, v):
    m = ADAM_B1 * m + (1.0 - ADAM_B1) * g
    v = ADAM_B2 * v + (1.0 - ADAM_B2) * (g * g)
    m_hat = m / (1.0 - ADAM_B1 ** ADAM_STEP)
    v_hat = v / (1.0 - ADAM_B2 ** ADAM_STEP)
    return -ADAM_LR * (m_hat / (jnp.sqrt(v_hat) + ADAM_EPS) + ADAM_WD * w), m, v


N_CHIP = N_DEV // 2
BLOCK_ELEMS = 128 * 1024


def _my_slot():
    return 4 * lax.axis_index("x") + 2 * lax.axis_index("y") + lax.axis_index("c")


def _row_block(r, c):
    cap = max(8, BLOCK_ELEMS // (-(-c // LANE) * LANE))
    return max([b for b in range(8, r + 1, 8) if r % b == 0 and b <= cap] or [r])


def _blocks(r, c):
    rb = _row_block(r, c)
    if rb < r or r * c <= BLOCK_ELEMS:
        return rb, c
    return r, max([b for b in (512, 256, LANE) if c % b == 0 and r * b <= BLOCK_ELEMS] or [c])


def _reduce_adam(parts, w, m, v, *, after=None, name):
    nl, r, c = w.shape
    rb, cb = _blocks(r, c)
    dep_specs, deps = _dep_specs(after)

    def body(*refs):
        me = refs[0][0]
        refs = refs[1 + len(deps):]
        p_refs = refs[:2 * nl]
        w_ref, m_ref, v_ref, g_out, d_out, m_out, v_out = refs[2 * nl:]
        layer = pl.program_id(0)
        for li in range(nl):
            @pl.when(layer == li)
            def _(li=li):
                own_ref, land_ref = p_refs[2 * li], p_refs[2 * li + 1]
                mine = own_ref[...].astype(F32)
                g = None
                for dev in range(N_DEV):
                    term = jnp.where(me == dev, mine, land_ref[dev].astype(F32))
                    g = term if g is None else g + term
                g_out[...] = g
                d_out[...], m_out[...], v_out[...] = _adamw_math(w_ref[...], g, m_ref[...], v_ref[...])

    blk = pl.BlockSpec((None, rb, cb), lambda l, i, j, me: (l, i, j))
    p_specs = []
    for li in range(nl):
        p_specs += [
            pl.BlockSpec((None, rb, cb), lambda l, i, j, me, li=li: (me[0], jnp.where(l == li, i, 0), jnp.where(l == li, j, 0))),
            pl.BlockSpec((N_DEV, rb, cb), lambda l, i, j, me, li=li: (0, jnp.where(l == li, i, 0), jnp.where(l == li, j, 0)))]
    flat = [p for pair in parts for p in pair]
    grid_spec = pltpu.PrefetchScalarGridSpec(
        num_scalar_prefetch=1, grid=(nl, r // rb, c // cb), in_specs=dep_specs + p_specs + [blk, blk, blk],
        out_specs=[blk] * 4)
    return pl.pallas_call(
        body, grid_spec=grid_spec, out_shape=[_sds(w.shape, F32)] * 4,
        compiler_params=_cparams("arbitrary", "arbitrary", "arbitrary"), name=name)(
        _my_slot().reshape(1), *deps, *flat, w, m, v)


def _adam_small(own, landed, split, w, m, v, *, name):
    n = len(w)

    def body(*refs):
        own_refs, land_refs, w_refs, m_refs, v_refs = (refs[k * n:(k + 1) * n] for k in range(5))
        outs = refs[5 * n:]
        me = _my_slot()
        for k in range(n):
            mine = own_refs[k][me] if split[k] else own_refs[k][...]
            g = None
            for dev in range(N_DEV):
                term = jnp.where(me == dev, mine, land_refs[k][dev])
                g = term if g is None else g + term
            outs[4 * k][...] = g
            outs[4 * k + 1][...], outs[4 * k + 2][...], outs[4 * k + 3][...] = _adamw_math(
                w_refs[k][...], g, m_refs[k][...], v_refs[k][...])

    out = pl.pallas_call(body, out_shape=[_sds(a.shape, F32) for a in w for _ in range(4)],
                         compiler_params=pltpu.CompilerParams(vmem_limit_bytes=V7X_VMEM_LIMIT), name=name)(
        *own, *landed, *w, *m, *v)
    return [tuple(out[4 * k:4 * k + 4]) for k in range(n)]


_HBM = pl.BlockSpec(memory_space=pltpu.HBM)
_SEM = pl.BlockSpec(memory_space=pltpu.SEMAPHORE)
_DATAFLOW = pltpu.SideEffectType.DATAFLOW_SIDE_EFFECTING


def _plan_to_all(src, land):
    x, y, c = lax.axis_index("x"), lax.axis_index("y"), lax.axis_index("c")
    return [(src, land.at[_my_slot()], (x ^ ((d >> 2) & 1), y ^ ((d >> 1) & 1), c ^ (d & 1))) for d in range(1, N_DEV)]


def _plan_split_to_all(src, land):
    x, y, c = lax.axis_index("x"), lax.axis_index("y"), lax.axis_index("c")
    peers = [(x ^ ((d >> 2) & 1), y ^ ((d >> 1) & 1), c ^ (d & 1)) for d in range(1, N_DEV)]
    return [(src.at[4 * px + 2 * py + pc], land.at[_my_slot()], (px, py, pc)) for px, py, pc in peers]


_PLAN_COPIES = {_plan_to_all: N_DEV - 1, _plan_split_to_all: N_DEV - 1}


def _plans(plan, n):
    return list(plan) if isinstance(plan, (list, tuple)) else [plan] * n


def _exchange_copies(plan, ins, lands, send, recv):
    copies, sem = [], 0
    for p, src, land in zip(_plans(plan, len(lands)), ins, lands):
        for s, dst, dev in p(src, land):
            copies.append(pltpu.make_async_remote_copy(
                src_ref=s, dst_ref=dst, send_sem=send.at[sem], recv_sem=recv.at[sem],
                device_id=dev, device_id_type=pl.DeviceIdType.MESH))
            sem += 1
    return copies


def _place_own(a, dtype, *, after=None, name):
    r, c = a.shape
    rb = _row_block(r, c)
    dep_specs, deps = _dep_specs(after)

    def body(*refs):
        a_ref, o_ref = refs[1 + len(deps):]
        o_ref[...] = a_ref[...].astype(dtype)

    grid_spec = pltpu.PrefetchScalarGridSpec(
        num_scalar_prefetch=1, grid=(r // rb,), in_specs=dep_specs + [pl.BlockSpec((rb, c), lambda i, me: (i, 0))],
        out_specs=pl.BlockSpec((None, rb, c), lambda i, me: (me[0], i, 0)))
    return pl.pallas_call(body, grid_spec=grid_spec, out_shape=_sds((N_DEV, r, c), dtype),
                          compiler_params=_cparams("arbitrary"), name=name)(_my_slot().reshape(1), *deps, a)


def _plan_gather_first(land, _):
    x, y, c = lax.axis_index("x"), lax.axis_index("y"), lax.axis_index("c")
    mine = land.at[_my_slot()]
    return [(mine, mine, (x, y, 1 - c))] + [(mine, mine, (x ^ (d >> 1), y ^ (d & 1), c)) for d in range(1, N_CHIP)]


def _plan_gather_relay(land, _):
    x, y, c = lax.axis_index("x"), lax.axis_index("y"), lax.axis_index("c")
    slots = [land.at[4 * (x ^ (d >> 1)) + 2 * (y ^ (d & 1)) + c] for d in range(1, N_CHIP)]
    return [(s, s, (x, y, 1 - c)) for s in slots]


_PLAN_COPIES[_plan_gather_first] = N_CHIP
_PLAN_COPIES[_plan_gather_relay] = N_CHIP - 1


def _exchange_start(plan, arrs, lands, *, after=None, name):
    bufs = list(lands) if arrs is None else list(arrs) + list(lands)
    n, nb = len(lands), len(bufs)
    nsem = sum(_PLAN_COPIES[p] for p in _plans(plan, n))
    dep_specs, deps = _dep_specs(after)

    def body(*refs):
        ins, land_refs = refs[:n], refs[nb - n:nb]
        send, recv = refs[nb + len(deps)], refs[nb + len(deps) + 1]
        for cp in _exchange_copies(plan, ins, land_refs, send, recv):
            cp.start()
        refs[-1][...] = jnp.zeros_like(refs[-1])

    out = pl.pallas_call(
        body, name=name,
        out_shape=(pltpu.SemaphoreType.DMA((nsem,)), pltpu.SemaphoreType.DMA((nsem,)),
                   *[pltpu.HBM(a.shape, a.dtype) for a in bufs], _sds((8, LANE), F32)),
        in_specs=[_HBM] * nb + dep_specs,
        out_specs=(_SEM, _SEM, *([_HBM] * nb), pl.BlockSpec(memory_space=pltpu.VMEM)),
        input_output_aliases={i: 2 + i for i in range(nb)},
        compiler_params=pltpu.CompilerParams(has_side_effects=_DATAFLOW),
    )(*[pltpu.with_memory_space_constraint(a, pltpu.HBM) for a in bufs], *deps)
    return (plan, n, out[0], out[1], list(out[2:2 + nb])), out[-1]


def _exchange_now(plan, lands, *, name):
    n = len(lands)
    nsem = sum(_PLAN_COPIES[p] for p in _plans(plan, n))

    def body(*refs):
        land_refs, send, recv = refs[n:2 * n], refs[2 * n], refs[2 * n + 1]
        copies = _exchange_copies(plan, land_refs, land_refs, send, recv)
        for cp in copies:
            cp.start()
        for cp in copies:
            cp.wait_send()
            cp.wait_recv()

    hbm = pl.BlockSpec(memory_space=pl.ANY)
    return pl.pallas_call(
        body, in_specs=[hbm] * n, out_specs=[hbm] * n, out_shape=[_sds(a.shape, a.dtype) for a in lands],
        input_output_aliases={i: i for i in range(n)},
        scratch_shapes=[pltpu.SemaphoreType.DMA((nsem,)), pltpu.SemaphoreType.DMA((nsem,))], name=name)(*lands)


def _exchange_wait(state, after, *, name):
    plan, n, send_sem, recv_sem, bufs = state
    nb = len(bufs)
    after = list(after) if isinstance(after, (list, tuple)) else [after]

    def body(*refs):
        ins, land_refs, send, recv = refs[:n], refs[nb - n:nb], refs[nb], refs[nb + 1]
        for cp in _exchange_copies(plan, ins, land_refs, send, recv):
            cp.wait_send()
            cp.wait_recv()

    out = pl.pallas_call(
        body, name=name, out_shape=[pltpu.HBM(a.shape, a.dtype) for a in bufs],
        in_specs=[_HBM] * nb + [_SEM, _SEM] + [pl.BlockSpec(memory_space=pl.ANY)] * len(after), out_specs=[_HBM] * nb,
        input_output_aliases={i: i for i in range(nb)},
        compiler_params=pltpu.CompilerParams(has_side_effects=_DATAFLOW),
    )(*bufs, send_sem, recv_sem, *after)
    return list(out[:n]), list(out[nb - n:])


def _dep_specs(after):
    return ([], []) if after is None else ([pl.BlockSpec(memory_space=pl.ANY)], [after])


def _undo_column_split(g):
    return jnp.transpose(g, (1, 0, 2)).reshape(g.shape[1], N_DEV * g.shape[2])


def _column_split(a):
    r, c = a.shape
    return jnp.transpose(a.reshape(r, N_DEV, c // N_DEV), (1, 0, 2))


class _WholeWeights:
    def __init__(self, groups):
        self.groups = groups
        self.grads = {}

    def fetch(self, group, after):
        return self.groups[group]

    def emit(self, group, grads):
        self.grads.update(grads)
        return None


def _local_step(x, target, replicated, src):
    d = x.shape[1]
    mix_g, ffn_g = replicated["mix_g"], replicated["ffn_g"]
    h0 = jnp.concatenate([jnp.zeros((CHUNK, d), F32), x], axis=0)
    tgt = jnp.concatenate([jnp.zeros((CHUNK, d), F32), target], axis=0)
    cp = src.fetch("cp", [h0, tgt])
    h0 = lax.dynamic_update_slice(h0, cp["meta"], (PAD_ROWS, 0))
    cp_mid = (cp["conv_w"], replicated["conv_b"], replicated["ln_g"], replicated["ln_b"], replicated["pool_w"],
              replicated["pool_scale"])

    h1, cat, z0, u0 = _cp_mid_fwd(h0, mix_g[0:1], cp["cp_w_in_t"], cp["cp_w_out"], *cp_mid, name="cp_mixer")
    ffn0 = src.fetch("ffn0", h1)
    h2, uf0, rf0 = _ffn_fwd(h1, ffn_g[0:1], ffn0["w1"], ffn0["w2"], name="ffn0")
    gla = src.fetch("gla", h2)
    gla_mid = (gla["gate_w"], gla["gate_b"], gla["head_g"])
    h3, og, states, z1, u1 = _gla_mid_fwd(h2, mix_g[1:2], gla["gla_w_in_t"], gla["gla_w_out"], *gla_mid, name="gla_mixer")
    ffn1 = src.fetch("ffn1", h3)
    h4, uf1, rf1 = _ffn_fwd(h3, ffn_g[1:2], ffn1["w1"], ffn1["w2"], name="ffn1")
    dh4, loss, d_final_g = _head(h4, replicated["final_g"], tgt, name="head")

    dh3, dhh1, dob1, dffn_g1 = _ffn_bwd_x(h3, dh4, ffn_g[1:2], rf1, ffn1["w1"], ffn1["w2"], name="ffn1_bwd_x")
    dw1_1, dw2_1 = _ffn_bwd_w(uf1, dhh1, rf1, dob1, name="ffn1_bwd_w")
    sent = src.emit("ffn1", dict(w1=dw1_1, w2=dw2_1))
    d_gla_w_out = _linear_bwd_w(og, dh3, name="gla_out_dw")
    dh2, dmix_g1, dz1, d_gate_w, d_gate_b, d_head_g = _gla_mid_bwd(
        z1, h2, mix_g[1:2], gla["gla_w_in_t"], dh3, gla["gla_w_out"], states, *gla_mid, after=sent, name="gla_mixer_bwd")
    d_gla_w_in_t = _linear_bwd_w(dz1, u1, name="gla_in_dw")
    sent = src.emit("gla", dict(gla_w_in_t=d_gla_w_in_t, gla_w_out=d_gla_w_out))
    dh1, dhh0, dob0, dffn_g0 = _ffn_bwd_x(h1, dh2, ffn_g[0:1], rf0, ffn0["w1"], ffn0["w2"], after=sent, name="ffn0_bwd_x")
    dw1_0, dw2_0 = _ffn_bwd_w(uf0, dhh0, rf0, dob0, name="ffn0_bwd_w")
    src.emit("ffn0", dict(w1=dw1_0, w2=dw2_0))
    d_cp_w_out = _linear_bwd_w(cat, dh1, name="cp_out_dw")
    sent = src.emit("cp_out", dict(cp_w_out=d_cp_w_out))
    dh0, dmix_g0, dz0, d_conv_w, d_conv_b, d_ln_g, d_ln_b, d_pool_w, d_pool_scale = _cp_mid_bwd(
        z0, h0, mix_g[0:1], cp["cp_w_in_t"], dh1, cp["cp_w_out"], *cp_mid, after=sent, name="cp_mixer_bwd")
    d_cp_w_in_t = _linear_bwd_w(dz0, u0, name="cp_in_dw")

    small = dict(
        mix_g=jnp.concatenate([dmix_g0, dmix_g1]), ffn_g=jnp.concatenate([dffn_g0, dffn_g1]), conv_b=d_conv_b, ln_g=d_ln_g,
        ln_b=d_ln_b, pool_w=d_pool_w, pool_scale=d_pool_scale, final_g=d_final_g, meta=dh0[PAD_ROWS:CHUNK], conv_w=d_conv_w,
        gate_w=d_gate_w, gate_b=d_gate_b, head_g=d_head_g)
    src.emit("cp", dict(cp_w_in_t=d_cp_w_in_t, small=small))
    return loss, dh0[CHUNK:], small


_REPLICATED = ("mix_norm_g", "ffn_norm_g", "cp_conv_b", "cp_ln_g", "cp_ln_b", "cp_pool_w", "cp_pool_scale", "final_norm_g")
_SMALL_SHARDED = ("meta_tokens", "cp_conv_w", "gla_gate_w2", "gla_gate_b", "gla_head_g")
_NAMES = ("meta_tokens", "mix_norm_g", "ffn_norm_g", "ffn_w1", "ffn_w2", "cp_w_in", "cp_conv_w", "cp_conv_b", "cp_ln_g",
          "cp_ln_b", "cp_pool_w", "cp_pool_scale", "cp_w_out", "gla_w_in", "gla_gate_w2", "gla_gate_b", "gla_head_g",
          "gla_w_out", "final_norm_g")
_SMALL_GRADS = ("mix_g", "ffn_g", "conv_b", "ln_g", "ln_b", "pool_w", "pool_scale", "final_g", "meta", "conv_w", "gate_w",
                "gate_b", "head_g")
_GROUPS = ("cp", "ffn0", "gla", "ffn1")


class _Exchanges:
    def __init__(self, w, d):
        self.d = d
        small = [w[n].reshape(w[n].shape[-2:]) for n in _SMALL_SHARDED]
        self.small_shard_shapes = [w[n].shape for n in _SMALL_SHARDED]
        shards = dict(
            cp=[(w["cp_w_in"][0].T, BF16), (w["cp_w_out"][0], BF16)] + [(a, F32) for a in small],
            ffn0=[(w["ffn_w1"][0], BF16), (w["ffn_w2"][0], BF16)],
            gla=[(w["gla_w_in"][0].T, BF16), (w["gla_w_out"][0], BF16)],
            ffn1=[(w["ffn_w1"][1], BF16), (w["ffn_w2"][1], BF16)])
        self.gathers = {}
        self.sent = {}
        token = None
        for group in _GROUPS:
            lands = [_place_own(a, dtype, after=token, name=f"place_w_{group}_{k}")
                     for k, (a, dtype) in enumerate(shards[group])]
            self.gathers[group], token = _exchange_start(_plan_gather_first, None, lands, after=token,
                                                         name=f"start_w_{group}")
        self.token = token

    def fetch(self, group, after):
        d = self.d
        after = (list(after) if isinstance(after, (list, tuple)) else [after]) + [self.token]
        _, lands = _exchange_wait(self.gathers[group], after, name=f"wait_w_{group}")
        got = _exchange_now(_plan_gather_relay, lands, name=f"relay_w_{group}")
        if group in ("ffn0", "ffn1"):
            return dict(w1=got[0], w2=got[1])
        if group == "gla":
            w_in_t = jnp.pad(got[0].reshape(-1, d), ((0, GATE_PAD - GATE_RANK), (0, 0)))
            return dict(gla_w_in_t=w_in_t, gla_w_out=got[1].reshape(d, d), gate_w=self.gate_w, gate_b=self.gate_b,
                        head_g=self.head_g)
        meta, conv_w, gate_w, self.gate_b, self.head_g = [_undo_column_split(a) for a in got[2:]]
        self.gate_w = jnp.pad(gate_w, ((0, GATE_PAD - GATE_RANK), (0, 0))).astype(BF16)
        return dict(cp_w_in_t=got[0].reshape(-1, d), cp_w_out=got[1].reshape(d, d), meta=meta,
                    conv_w=jnp.pad(conv_w, ((0, 1), (0, 0))))

    def emit(self, group, g):
        d = self.d
        if group in ("ffn0", "ffn1"):
            arrs = [g["w1"], g["w2"]]
        elif group == "gla":
            w_in_t = g["gla_w_in_t"][:3 * d + GATE_RANK]
            arrs = [w_in_t.reshape(N_DEV, -1, d), g["gla_w_out"].reshape(N_DEV, d // N_DEV, d)]
        elif group == "cp_out":
            arrs = [g["cp_w_out"].reshape(N_DEV, d // N_DEV, d)]
        else:
            s = dict(g["small"])
            s.update(pool_w=s["pool_w"][None], conv_w=s["conv_w"][:CONV_WIDTH], gate_w=s["gate_w"][:GATE_RANK])
            own = [s[n] for n in _SMALL_GRADS[:len(_REPLICATED)]]
            own += [_column_split(s[n]).reshape((N_DEV,) + shape)
                    for n, shape in zip(_SMALL_GRADS[len(_REPLICATED):], self.small_shard_shapes)]
            plans = [_plan_to_all] * len(_REPLICATED) + [_plan_split_to_all] * len(_SMALL_SHARDED)
            lands = [lax.empty((N_DEV,) + a.shape, F32) for a in own[:len(_REPLICATED)]]
            lands += [lax.empty(a.shape, F32) for a in own[len(_REPLICATED):]]
            self.small_sent, self.token = _exchange_start(plans, own, lands, after=self.token, name="start_g_small")
            arrs = [g["cp_w_in_t"].reshape(N_DEV, -1, d)]
        self.sent[group], self.token = _exchange_start(_plan_split_to_all, arrs, [lax.empty(a.shape, a.dtype) for a in arrs],
                                                       after=self.token, name=f"start_g_{group}")
        return self.token

    def finish(self, w, mom, var):
        out = {}
        after = self.token

        def landed(group):
            own, got = _exchange_wait(self.sent[group], after, name=f"wait_g_{group}")
            return list(zip(own, got))

        def adam(n, parts, behind=None, transposed=False):
            flip = (lambda a: jnp.transpose(a, (0, 2, 1))) if transposed else (lambda a: a)
            res = _reduce_adam(parts, flip(w[n]), flip(mom[n]), flip(var[n]), after=behind, name=f"adam_{n}")
            out[n] = tuple(flip(a) for a in res)
            return res[0]

        ffn1 = landed("ffn1")
        after = ffn1[0][1]
        gla = landed("gla")
        after = gla[0][1]
        ffn0 = landed("ffn0")
        after = adam("ffn_w1", [ffn0[0], ffn1[0]])
        after = adam("ffn_w2", [ffn0[1], ffn1[1]], after)
        after = adam("gla_w_in", [gla[0]], after, transposed=True)
        after = adam("gla_w_out", [gla[1]], after)
        small_own, small_landed = _exchange_wait(self.small_sent, after, name="wait_g_small")
        names = _REPLICATED + _SMALL_SHARDED
        split = [False] * len(_REPLICATED) + [True] * len(_SMALL_SHARDED)
        small_new = _adam_small(small_own, small_landed, split, [w[n] for n in names], [mom[n] for n in names],
                                [var[n] for n in names], name="adam_small")
        out.update(zip(names, small_new))

        after = small_new[0][0]
        cp_out = landed("cp_out")
        after = adam("cp_w_out", [cp_out[0]])
        cp = landed("cp")
        adam("cp_w_in", [cp[0]], transposed=True)
        return out


def kernel(x, meta_tokens, mix_norm_g, ffn_norm_g, ffn_w1, ffn_w2, cp_w_in, cp_conv_w, cp_conv_b, cp_ln_g, cp_ln_b, cp_pool_w, cp_pool_scale, cp_w_out, gla_w_in, gla_gate_w2, gla_gate_b, gla_head_g, gla_w_out, final_norm_g, loss_target, m_meta_tokens, m_mix_norm_g, m_ffn_norm_g, m_ffn_w1, m_ffn_w2, m_cp_w_in, m_cp_conv_w, m_cp_conv_b, m_cp_ln_g, m_cp_ln_b, m_cp_pool_w, m_cp_pool_scale, m_cp_w_out, m_gla_w_in, m_gla_gate_w2, m_gla_gate_b, m_gla_head_g, m_gla_w_out, m_final_norm_g, v_meta_tokens, v_mix_norm_g, v_ffn_norm_g, v_ffn_w1, v_ffn_w2, v_cp_w_in, v_cp_conv_w, v_cp_conv_b, v_cp_ln_g, v_cp_ln_b, v_cp_pool_w, v_cp_pool_scale, v_cp_w_out, v_gla_w_in, v_gla_gate_w2, v_gla_gate_b, v_gla_head_g, v_gla_w_out, v_final_norm_g):
    w = dict(meta_tokens=meta_tokens, mix_norm_g=mix_norm_g, ffn_norm_g=ffn_norm_g, ffn_w1=ffn_w1, ffn_w2=ffn_w2,
             cp_w_in=cp_w_in, cp_conv_w=cp_conv_w, cp_conv_b=cp_conv_b, cp_ln_g=cp_ln_g, cp_ln_b=cp_ln_b,
             cp_pool_w=cp_pool_w, cp_pool_scale=cp_pool_scale, cp_w_out=cp_w_out, gla_w_in=gla_w_in,
             gla_gate_w2=gla_gate_w2, gla_gate_b=gla_gate_b, gla_head_g=gla_head_g, gla_w_out=gla_w_out,
             final_norm_g=final_norm_g.reshape(1, -1))
    mom = dict(meta_tokens=m_meta_tokens, mix_norm_g=m_mix_norm_g, ffn_norm_g=m_ffn_norm_g, ffn_w1=m_ffn_w1, ffn_w2=m_ffn_w2,
               cp_w_in=m_cp_w_in, cp_conv_w=m_cp_conv_w, cp_conv_b=m_cp_conv_b, cp_ln_g=m_cp_ln_g, cp_ln_b=m_cp_ln_b,
               cp_pool_w=m_cp_pool_w, cp_pool_scale=m_cp_pool_scale, cp_w_out=m_cp_w_out, gla_w_in=m_gla_w_in,
               gla_gate_w2=m_gla_gate_w2, gla_gate_b=m_gla_gate_b, gla_head_g=m_gla_head_g, gla_w_out=m_gla_w_out,
               final_norm_g=m_final_norm_g.reshape(1, -1))
    var = dict(meta_tokens=v_meta_tokens, mix_norm_g=v_mix_norm_g, ffn_norm_g=v_ffn_norm_g, ffn_w1=v_ffn_w1, ffn_w2=v_ffn_w2,
               cp_w_in=v_cp_w_in, cp_conv_w=v_cp_conv_w, cp_conv_b=v_cp_conv_b, cp_ln_g=v_cp_ln_g, cp_ln_b=v_cp_ln_b,
               cp_pool_w=v_cp_pool_w, cp_pool_scale=v_cp_pool_scale, cp_w_out=v_cp_w_out, gla_w_in=v_gla_w_in,
               gla_gate_w2=v_gla_gate_w2, gla_gate_b=v_gla_gate_b, gla_head_g=v_gla_head_g, gla_w_out=v_gla_w_out,
               final_norm_g=v_final_norm_g.reshape(1, -1))
    d = x.shape[-1]
    replicated = dict(mix_g=w["mix_norm_g"], ffn_g=w["ffn_norm_g"], conv_b=w["cp_conv_b"], ln_g=w["cp_ln_g"],
                      ln_b=w["cp_ln_b"], pool_w=w["cp_pool_w"][0].astype(BF16), pool_scale=w["cp_pool_scale"],
                      final_g=w["final_norm_g"])
    exchanges = _Exchanges(w, d)
    loss_blk, grad_x, _ = _local_step(x[0], loss_target[0], replicated, exchanges)
    loss = lax.psum(loss_blk[0, 0], ("x", "y", "c"))
    out = exchanges.finish(w, mom, var)

    def leaf(n, k):
        a = out[n][k]
        return a.reshape(-1) if n == "final_norm_g" else a

    return (loss, grad_x[None], *[leaf(n, 0) for n in _NAMES], *[leaf(n, 1) for n in _NAMES],
            *[leaf(n, 2) for n in _NAMES], *[leaf(n, 3) for n in _NAMES])
```

```python
import functools

import jax
import jax.numpy as jnp
from jax import lax
from jax.experimental import pallas as pl
from jax.experimental.pallas import tpu as pltpu

F32, BF16 = jnp.float32, jnp.bfloat16
N_DEV = 8
CHUNK = 64
N_META = 16
PAD_ROWS = CHUNK - N_META
HALO = 32
EPS = 1e-5
CONV_WIDTH = 31
POOL_WINDOWS = (2, 4, 8, 16)
HEADS = 4
GATE_RANK = 16
GATE_NORM = 16.0
GATE_PAD = 128
ADAM_LR, ADAM_B1, ADAM_B2, ADAM_EPS, ADAM_WD, ADAM_STEP = 0.001, 0.9, 0.999, 1e-08, 0.01, 10
V7X_VMEM_LIMIT = 56 * 2 ** 20
LANE = 128


def _cparams(*sem):
    return pltpu.CompilerParams(dimension_semantics=sem, vmem_limit_bytes=V7X_VMEM_LIMIT)


def _row_tile(t, cap):
    best = CHUNK
    for r in range(CHUNK, min(t, cap) + 1, CHUNK):
        if t % r == 0:
            best = r
    return best


def _resident(shape):
    return pl.BlockSpec(shape, lambda *_: (0,) * len(shape), pipeline_mode=pl.Buffered(1))


def _dot(a, b):
    return jnp.dot(a, b, preferred_element_type=F32)


def _dot_nt(a, b):
    return lax.dot_general(a, b, (((1,), (1,)), ((), ())), preferred_element_type=F32)


def _dot_tn(a, b):
    return lax.dot_general(a, b, (((0,), (0,)), ((), ())), preferred_element_type=F32)


def _rowsum(a):
    return jnp.sum(a, axis=0, keepdims=True)


def _sigmoid(a):
    return 1.0 / (1.0 + jnp.exp(-a))


def _row_ids(tile, rt):
    return tile * rt + lax.broadcasted_iota(jnp.int32, (rt, 1), 0)


def _sds(shape, dtype):
    return jax.ShapeDtypeStruct(shape, dtype)


DW_ROWS = 1024


def _linear_bwd_w(x, dy, *, name):
    t, k = x.shape
    n = dy.shape[1]
    cut_k = k > n
    width = k if cut_k else n
    blk = max(c for c in (640, 512, 384, 256, LANE) if width % c == 0)

    def body(x_ref, dy_ref, o_ref, acc):
        for c0 in range(0, t, DW_ROWS):
            rows = slice(c0, min(c0 + DW_ROWS, t))
            part = _dot_tn(x_ref[rows, :].astype(BF16), dy_ref[rows, :].astype(BF16))
            if c0 == 0:
                acc[...] = part
            else:
                acc[...] += part
        o_ref[...] = acc[...].astype(BF16)

    if cut_k:
        in_specs = [pl.BlockSpec((t, blk), lambda j: (0, j)), _resident((t, n))]
        out_specs = pl.BlockSpec((blk, n), lambda j: (j, 0))
        acc_shape = (blk, n)
    else:
        in_specs = [_resident((t, k)), pl.BlockSpec((t, blk), lambda j: (0, j))]
        out_specs = pl.BlockSpec((k, blk), lambda j: (0, j))
        acc_shape = (k, blk)
    return pl.pallas_call(
        body, grid=(width // blk,), in_specs=in_specs, out_specs=out_specs, out_shape=_sds((k, n), BF16),
        scratch_shapes=[pltpu.VMEM(acc_shape, F32)], compiler_params=_cparams("parallel"), name=name)(x, dy)


def _ffn_fwd(h, gain, w1g, w2g, *, name):
    t, d = h.shape
    f8 = w1g.shape[-1]
    rt = _row_tile(t, 832)

    def body(h_ref, g_ref, w1_ref, w2_ref, o_ref, u_ref, r_ref, acc_ref):
        j = pl.program_id(1)

        @pl.when(j == 0)
        def _():
            hv = h_ref[...]
            u_ref[...] = (hv * lax.rsqrt(jnp.mean(hv * hv, axis=-1, keepdims=True) + EPS) * g_ref[...]).astype(BF16)
            acc_ref[...] = jnp.zeros_like(acc_ref)

        a = jnp.maximum(_dot(u_ref[...], w1_ref[...]), 0.0)
        r_ref[...] = a.astype(BF16)
        acc_ref[...] += _dot((a * a).astype(BF16), w2_ref[...])

        @pl.when(j == N_DEV - 1)
        def _():
            o_ref[...] = h_ref[...] + acc_ref[...]

    return pl.pallas_call(
        body, grid=(t // rt, N_DEV),
        in_specs=[pl.BlockSpec((rt, d), lambda i, j: (i, 0)), _resident((1, d)),
                  pl.BlockSpec((None, d, f8), lambda i, j: (j, 0, 0)),
                  pl.BlockSpec((None, f8, d), lambda i, j: (j, 0, 0))],
        out_specs=[pl.BlockSpec((rt, d), lambda i, j: (i, 0)), pl.BlockSpec((rt, d), lambda i, j: (i, 0)),
                   pl.BlockSpec((rt, f8), lambda i, j: (i, j))],
        out_shape=[_sds((t, d), F32), _sds((t, d), BF16), _sds((t, N_DEV * f8), BF16)],
        scratch_shapes=[pltpu.VMEM((rt, d), F32)],
        compiler_params=_cparams("parallel", "arbitrary"), name=name)(h, gain, w1g, w2g)


def _ffn_bwd_x(h, dout, gain, r, w1g, w2g, *, after=None, name):
    t, d = h.shape
    f8 = w1g.shape[-1]
    rt = _row_tile(t, 832)
    last = N_DEV - 1
    dep_specs, deps = _dep_specs(after)

    def body(*refs):
        h_ref, do_ref, g_ref, r_ref, w1_ref, w2_ref, dh_ref, dhh_ref, dob_ref, dg_ref, du_ref = refs[len(deps):]
        i, j = pl.program_id(0), pl.program_id(1)

        @pl.when(j == 0)
        def _():
            dob_ref[...] = do_ref[...].astype(BF16)
            du_ref[...] = jnp.zeros_like(du_ref)

        dhh = (_dot_nt(dob_ref[...], w2_ref[...]) * (2.0 * r_ref[...].astype(F32))).astype(BF16)
        dhh_ref[...] = dhh
        du_ref[...] += _dot_nt(dhh, w1_ref[...])

        @pl.when(j == last)
        def _():
            @pl.when(i == 0)
            def _():
                dg_ref[...] = jnp.zeros_like(dg_ref)

            hv = h_ref[...]
            rstd = lax.rsqrt(jnp.mean(hv * hv, axis=-1, keepdims=True) + EPS)
            xh = hv * rstd
            du = du_ref[...]
            dg_ref[...] += _rowsum(du * xh)
            dxh = du * g_ref[...]
            dh_ref[...] = do_ref[...] + rstd * (dxh - xh * jnp.mean(dxh * xh, axis=-1, keepdims=True))

    rows = lambda i, j: (i, 0)
    return pl.pallas_call(
        body, grid=(t // rt, N_DEV),
        in_specs=dep_specs + [
                  pl.BlockSpec((rt, d), rows), pl.BlockSpec((rt, d), rows), _resident((1, d)),
                  pl.BlockSpec((rt, f8), lambda i, j: (i, j)),
                  pl.BlockSpec((None, d, f8), lambda i, j: (j, 0, 0)),
                  pl.BlockSpec((None, f8, d), lambda i, j: (j, 0, 0))],
        out_specs=[pl.BlockSpec((rt, d), rows), pl.BlockSpec((rt, f8), lambda i, j: (i, j)), pl.BlockSpec((rt, d), rows),
                   pl.BlockSpec((1, d), lambda i, j: (0, 0))],
        out_shape=[_sds((t, d), F32), _sds((t, N_DEV * f8), BF16), _sds((t, d), BF16), _sds((1, d), F32)],
        scratch_shapes=[pltpu.VMEM((rt, d), F32)],
        compiler_params=_cparams("arbitrary", "arbitrary"), name=name)(*deps, h, dout, gain, r, w1g, w2g)


def _ffn_bwd_w(u, dhh, r, dout_b, *, name):
    t, d = u.shape
    f8 = dhh.shape[1] // N_DEV

    def body(u_ref, dhh_ref, r_ref, dob_ref, dw1_ref, dw2_ref, acc1, acc2):
        for c0 in range(0, t, DW_ROWS):
            rows = slice(c0, min(c0 + DW_ROWS, t))
            rr = r_ref[rows, :].astype(F32)
            part1 = _dot_tn(u_ref[rows, :], dhh_ref[rows, :])
            part2 = _dot_tn((rr * rr).astype(BF16), dob_ref[rows, :])
            if c0 == 0:
                acc1[...] = part1
                acc2[...] = part2
            else:
                acc1[...] += part1
                acc2[...] += part2
        dw1_ref[...] = acc1[...].astype(BF16)
        dw2_ref[...] = acc2[...].astype(BF16)

    return pl.pallas_call(
        body, grid=(N_DEV,),
        in_specs=[_resident((t, d)), pl.BlockSpec((t, f8), lambda j: (0, j)), pl.BlockSpec((t, f8), lambda j: (0, j)),
                  _resident((t, d))],
        out_specs=[pl.BlockSpec((None, d, f8), lambda j: (j, 0, 0)), pl.BlockSpec((None, f8, d), lambda j: (j, 0, 0))],
        out_shape=[_sds((N_DEV, d, f8), BF16), _sds((N_DEV, f8, d), BF16)],
        scratch_shapes=[pltpu.VMEM((d, f8), F32), pltpu.VMEM((f8, d), F32)],
        compiler_params=_cparams("parallel"), name=name)(u, dhh, r, dout_b)


def _lane_blocks(width):
    lb = min(LANE, width)
    return [slice(s, s + lb) for s in range(0, width, lb)]


def _conv_rows(src_ref, w_ref, offset, dst_ref, nblk, width, bias_ref=None):
    def blk(rb, carry):
        base = pl.multiple_of(rb * CHUNK, CHUNK)
        for l, ls in enumerate(_lane_blocks(width)):
            acc = jnp.zeros((CHUNK, ls.stop - ls.start), F32)
            if bias_ref is not None:
                acc = acc + bias_ref[:, ls]
            for k in range(CONV_WIDTH):
                acc = acc + w_ref[k:k + 1, ls] * src_ref[l, pl.ds(base + offset(k), CHUNK), :]
            dst_ref[l, pl.ds(base, CHUNK), :] = acc
        return carry

    lax.fori_loop(0, nblk, blk, 0)


def _to_lane_blocks(ref, row0, value):
    for l, ls in enumerate(_lane_blocks(value.shape[1])):
        ref[l, row0:row0 + value.shape[0], :] = value[:, ls]


def _from_lane_blocks(ref):
    return jnp.concatenate([ref[l] for l in range(ref.shape[0])], axis=1)


def _pool_counts(rows, window):
    return jnp.clip(rows - PAD_ROWS + 1, 1, window).astype(F32)


def _trailing_sum(v, window):
    s, sh = v, 1
    while sh < window:
        s = s + pltpu.roll(s, sh, 0)
        sh *= 2
    return s


def _leading_sum(v, window):
    s, sh, n = v, 1, v.shape[0]
    while sh < window:
        s = s + pltpu.roll(s, n - sh, 0)
        sh *= 2
    return s


def _norm_project(h_ref, g_ref, w_t_ref, u_ref, z_ref):
    hv = h_ref[...]
    u = (hv * lax.rsqrt(jnp.mean(hv * hv, axis=-1, keepdims=True) + EPS) * g_ref[...]).astype(BF16)
    u_ref[...] = u
    z_ref[...] = _dot_nt(u, w_t_ref[...])


def _project_back(dz_ref, w_t_ref, h_ref, g_ref, dres_ref, dh_ref, dg_ref, first):
    dx = _dot(dz_ref[...], w_t_ref[...])
    hv = h_ref[...]
    rstd = lax.rsqrt(jnp.mean(hv * hv, axis=-1, keepdims=True) + EPS)
    xh = hv * rstd

    @pl.when(first)
    def _():
        dg_ref[...] = jnp.zeros_like(dg_ref)

    dg_ref[...] += _rowsum(dx * xh)
    dxh = dx * g_ref[...]
    dh_ref[...] = dres_ref[...] + rstd * (dxh - xh * jnp.mean(dxh * xh, axis=-1, keepdims=True))


def _cp_mid_fwd(h, gain, w_in_t, w_out, conv_w, conv_b, ln_g, ln_b, pool_w, pool_scale, *, name):
    t, d = h.shape
    ein = w_in_t.shape[0]
    cd = conv_b.shape[1]
    pd = pool_scale.shape[1]
    pg = pd // len(POOL_WINDOWS)
    rt = _row_tile(t, 320)

    def body(h_ref, g_ref, wi_ref, wo_ref, cw_ref, cb_ref, lg_ref, lb_ref, pw_ref, ps_ref,
             ho_ref, o_ref, z_ref, u_ref, cv_ref, gext, pext, conv_s):
        i = pl.program_id(0)

        @pl.when(i == 0)
        def _():
            _to_lane_blocks(gext, 0, jnp.zeros((HALO, cd), F32))
            pext[0:HALO, :] = jnp.zeros((HALO, pd), F32)

        _norm_project(h_ref, g_ref, wi_ref, u_ref, z_ref)

        _to_lane_blocks(gext, HALO, z_ref[:, 0:cd] * _sigmoid(z_ref[:, cd:2 * cd]))
        pext[HALO:HALO + rt, :] = z_ref[:, 2 * cd:]
        _conv_rows(gext, cw_ref, lambda k: k + HALO - (CONV_WIDTH - 1), conv_s, rt // CHUNK, cd, cb_ref)
        cv = _from_lane_blocks(conv_s)
        cv_ref[...] = cv
        xc = cv - jnp.mean(cv, axis=-1, keepdims=True)
        y = xc * lax.rsqrt(jnp.mean(xc * xc, axis=-1, keepdims=True) + EPS) * lg_ref[...] + lb_ref[...]
        rows = _row_ids(i, rt)
        a = jnp.where(rows >= PAD_ROWS, y * _sigmoid(y), 0.0)
        o_ref[:, 0:cd] = a.astype(BF16)
        for gi, window in enumerate(POOL_WINDOWS):
            ls = slice(gi * pg, (gi + 1) * pg)
            v = pext[:, ls]
            tm = _trailing_sum(v, window)[HALO:] / _pool_counts(rows, window) - v[HALO:]
            p = _dot(tm.astype(BF16), pw_ref[gi]) * ps_ref[:, ls]
            o_ref[:, cd + gi * pg:cd + (gi + 1) * pg] = p.astype(BF16)
        ho_ref[...] = h_ref[...] + _dot(o_ref[...], wo_ref[...])
        gext[:, 0:HALO, :] = gext[:, rt:rt + HALO, :]
        pext[0:HALO, :] = pext[rt:rt + HALO, :]

    nl, lb = len(_lane_blocks(cd)), min(LANE, cd)
    rows = lambda i: (i, 0)
    return pl.pallas_call(
        body, grid=(t // rt,),
        in_specs=[pl.BlockSpec((rt, d), rows), _resident((1, d)), _resident(w_in_t.shape), _resident(w_out.shape),
                  _resident(conv_w.shape), _resident((1, cd)),
                  _resident((1, cd)), _resident((1, cd)), _resident(pool_w.shape), _resident((1, pd))],
        out_specs=[pl.BlockSpec((rt, d), rows), pl.BlockSpec((rt, cd + pd), rows), pl.BlockSpec((rt, ein), rows),
                   pl.BlockSpec((rt, d), rows), pl.BlockSpec((rt, cd), rows)],
        out_shape=[_sds((t, d), F32), _sds((t, cd + pd), BF16), _sds((t, ein), F32), _sds((t, d), BF16),
                   _sds((t, cd), F32)],
        scratch_shapes=[pltpu.VMEM((nl, rt + HALO, lb), F32), pltpu.VMEM((rt + HALO, pd), F32),
                        pltpu.VMEM((nl, rt, lb), F32)],
        compiler_params=_cparams("arbitrary"), name=name)(h, gain, w_in_t, w_out, conv_w, conv_b, ln_g, ln_b, pool_w,
                                                          pool_scale)


def _cp_mid_bwd(z, cv, h, gain, w_in_t, dh, w_out, conv_w, conv_b, ln_g, ln_b, pool_w, pool_scale, *, after=None, name):
    t, ein = z.shape
    cd = conv_b.shape[1]
    pd = pool_scale.shape[1]
    pg = pd // len(POOL_WINDOWS)
    rt = _row_tile(t, 320)
    ntile = t // rt
    per = rt // CHUNK
    dep_specs, deps = _dep_specs(after)

    def body(*refs):
        (z_ref, zh_ref, cv_ref, h_ref, g_ref, wi_ref, dh_ref, wo_ref, cw_ref, cb_ref, lg_ref, lb_ref, pw_ref, ps_ref,
         dhi_ref, dg_ref, dz_ref, dcw_ref, dcb_ref, dlg_ref, dlb_ref, dpw_ref, dps_ref,
         gext, pext, conv_s, dcv, dsp) = refs[len(deps):]
        step = pl.program_id(0)
        tile = ntile - 1 - step
        dcat = _dot_nt(dh_ref[...].astype(BF16), wo_ref[...])

        @pl.when(step == 0)
        def _():
            for ref in (dcw_ref, dcb_ref, dlg_ref, dlb_ref, dpw_ref, dps_ref):
                ref[...] = jnp.zeros_like(ref)
            _to_lane_blocks(dcv, rt, jnp.zeros((HALO, cd), F32))
            dsp[rt:rt + HALO, :] = jnp.zeros((HALO, pd), F32)

        keep = jnp.where(tile > 0, 1.0, 0.0)
        zh = zh_ref[CHUNK - HALO:CHUNK, :]
        _to_lane_blocks(gext, 0, keep * zh[:, 0:cd] * _sigmoid(zh[:, cd:2 * cd]))
        pext[0:HALO, :] = keep * zh[:, 2 * cd:]
        za = z_ref[:, 0:cd]
        sg = _sigmoid(z_ref[:, cd:2 * cd])
        _to_lane_blocks(gext, HALO, za * sg)
        pext[HALO:HALO + rt, :] = z_ref[:, 2 * cd:]
        cv = cv_ref[...]
        xc = cv - jnp.mean(cv, axis=-1, keepdims=True)
        rstd = lax.rsqrt(jnp.mean(xc * xc, axis=-1, keepdims=True) + EPS)
        xh = xc * rstd
        y = xh * lg_ref[...] + lb_ref[...]
        sy = _sigmoid(y)
        rows = _row_ids(tile, rt)
        da = jnp.where(rows >= PAD_ROWS, dcat[:, 0:cd], 0.0)
        dy = da * (sy * (1.0 + y * (1.0 - sy)))
        dlg_ref[...] += _rowsum(dy * xh)
        dlb_ref[...] += _rowsum(dy)
        dxh = dy * lg_ref[...]
        dconv = rstd * (dxh - jnp.mean(dxh, axis=-1, keepdims=True) - xh * jnp.mean(dxh * xh, axis=-1, keepdims=True))
        dcb_ref[...] += _rowsum(dconv)
        _to_lane_blocks(dcv, 0, dconv)
        for l, ls in enumerate(_lane_blocks(cd)):
            def acc_rows(rb, accs, l=l):
                base = pl.multiple_of(rb * CHUNK, CHUNK)
                d_blk = dcv[l, pl.ds(base, CHUNK), :]
                out = []
                for k in range(CONV_WIDTH):
                    prod = d_blk * gext[l, pl.ds(base + k + HALO - (CONV_WIDTH - 1), CHUNK), :]
                    part = prod[0:8]
                    for s in range(8, CHUNK, 8):
                        part = part + prod[s:s + 8]
                    out.append(accs[k] + part)
                return tuple(out)

            zero = jnp.zeros((8, ls.stop - ls.start), F32)
            accs = lax.fori_loop(0, per, acc_rows, (zero,) * CONV_WIDTH)
            for k in range(CONV_WIDTH):
                dcw_ref[k:k + 1, ls] += _rowsum(accs[k])
        _conv_rows(dcv, cw_ref, lambda k: CONV_WIDTH - 1 - k, conv_s, per, cd)
        dglu = _from_lane_blocks(conv_s)
        dz_ref[:, 0:cd] = (dglu * sg).astype(BF16)
        dz_ref[:, cd:2 * cd] = (dglu * za * sg * (1.0 - sg)).astype(BF16)
        dcv[:, rt:rt + HALO, :] = dcv[:, 0:HALO, :]
        for gi, window in enumerate(POOL_WINDOWS):
            ls = slice(gi * pg, (gi + 1) * pg)
            v = pext[:, ls]
            cnt = _pool_counts(rows, window)
            tm = (_trailing_sum(v, window)[HALO:] / cnt - v[HALO:]).astype(BF16)
            dp = dcat[:, cd + gi * pg:cd + (gi + 1) * pg]
            dps_ref[:, ls] += _rowsum(dp * _dot(tm, pw_ref[gi]))
            dpl = (dp * ps_ref[:, ls]).astype(BF16)
            dpw_ref[gi] += _dot_tn(tm, dpl)
            dtm = _dot_nt(dpl, pw_ref[gi])
            dsp[0:rt, ls] = dtm / cnt
            dpin = _leading_sum(dsp[:, ls], window)[0:rt] - dtm
            dz_ref[:, 2 * cd + gi * pg:2 * cd + (gi + 1) * pg] = dpin.astype(BF16)
        dsp[rt:rt + HALO, :] = dsp[0:HALO, :]
        _project_back(dz_ref, wi_ref, h_ref, g_ref, dh_ref, dhi_ref, dg_ref, step == 0)

    d = h.shape[1]
    back = lambda i: (ntile - 1 - i, 0)
    halo_idx = lambda i: (jnp.maximum((ntile - 1 - i) * per - 1, 0), 0)
    const2 = lambda i: (0, 0)
    nl, lb = len(_lane_blocks(cd)), min(LANE, cd)
    return pl.pallas_call(
        body, grid=(ntile,),
        in_specs=dep_specs + [
                  pl.BlockSpec((rt, ein), back), pl.BlockSpec((CHUNK, ein), halo_idx), pl.BlockSpec((rt, cd), back),
                  pl.BlockSpec((rt, d), back),
                  _resident((1, d)), _resident(w_in_t.shape), pl.BlockSpec((rt, d), back), _resident(w_out.shape),
                  _resident(conv_w.shape), _resident((1, cd)), _resident((1, cd)), _resident((1, cd)),
                  _resident(pool_w.shape), _resident((1, pd))],
        out_specs=[pl.BlockSpec((rt, d), back), pl.BlockSpec((1, d), const2),
                   pl.BlockSpec((rt, ein), back), pl.BlockSpec(conv_w.shape, const2), pl.BlockSpec((1, cd), const2),
                   pl.BlockSpec((1, cd), const2), pl.BlockSpec((1, cd), const2),
                   pl.BlockSpec(pool_w.shape, lambda i: (0, 0, 0)), pl.BlockSpec((1, pd), const2)],
        out_shape=[_sds((t, d), F32), _sds((1, d), F32),
                   _sds((t, ein), BF16), _sds(conv_w.shape, F32), _sds((1, cd), F32), _sds((1, cd), F32),
                   _sds((1, cd), F32), _sds(pool_w.shape, F32), _sds((1, pd), F32)],
        scratch_shapes=[pltpu.VMEM((nl, rt + HALO, lb), F32), pltpu.VMEM((rt + HALO, pd), F32), pltpu.VMEM((nl, rt, lb), F32),
                        pltpu.VMEM((nl, rt + HALO, lb), F32), pltpu.VMEM((rt + HALO, pd), F32)],
        compiler_params=_cparams("arbitrary"), name=name)(*deps, z, z, cv, h, gain, w_in_t, dh, w_out, conv_w, conv_b, ln_g,
                                                          ln_b, pool_w, pool_scale)


def _log_decay(r, gw_ref, gb_ref, rows):
    gp = _dot(r.astype(BF16), gw_ref[...]) + gb_ref[...]
    log_sig = jnp.minimum(gp, 0.0) - jnp.log(1.0 + jnp.exp(-jnp.abs(gp)))
    return gp, jnp.where(rows >= PAD_ROWS, log_sig / GATE_NORM, 0.0)


def _tri(strict):
    r = lax.broadcasted_iota(jnp.int32, (CHUNK, CHUNK), 0)
    c = lax.broadcasted_iota(jnp.int32, (CHUNK, CHUNK), 1)
    return jnp.where(c < r if strict else c <= r, 1.0, 0.0).astype(BF16)


def _tri_dot(tri, a):
    hi = a.astype(BF16)
    rest = a - hi.astype(F32)
    mid = rest.astype(BF16)
    lo = (rest - mid.astype(F32)).astype(BF16)
    return _dot(tri, hi) + _dot(tri, mid) + _dot(tri, lo)


def _gla_mid_fwd(h, gain, w_in_t, w_out, gate_w, gate_b, head_g, *, name):
    t = h.shape[0]
    zw = w_in_t.shape[0]
    dk = gate_b.shape[1]
    hv = head_g.shape[1]
    hk = dk // HEADS
    dv = hv * HEADS
    r_at = 2 * dk + 2 * dv
    rt = _row_tile(t, 320)
    per = rt // CHUNK
    scale = hk ** -0.5

    def body(h_ref, g_ref, wi_ref, wo_ref, gw_ref, gb_ref, hg_ref, ho_ref, o_ref, st_ref, z_ref, u_ref,
             s_ref, la_ref, dec_ref):
        i = pl.program_id(0)

        @pl.when(i == 0)
        def _():
            s_ref[...] = jnp.zeros_like(s_ref)

        _norm_project(h_ref, g_ref, wi_ref, u_ref, z_ref)

        _, la = _log_decay(z_ref[:, r_at:r_at + GATE_PAD], gw_ref, gb_ref, _row_ids(i, rt))
        la_ref[...] = la
        tri = _tri(False)

        def chunk_rows(c):
            return slice(c * CHUNK, (c + 1) * CHUNK)

        def decays(c, carry):
            rows = chunk_rows(c)
            la_c = la_ref[rows, :]
            cum = _tri_dot(tri, la_c)
            dec_ref[rows, :] = jnp.exp(_rowsum(la_c) - cum)
            return carry

        def states(c, carry):
            rows = chunk_rows(c)
            etot = jnp.exp(_rowsum(la_ref[rows, :]))
            for hd in range(HEADS):
                ks = slice(hd * hk, (hd + 1) * hk)
                kd = z_ref[rows, dk + hd * hk:dk + (hd + 1) * hk] * dec_ref[rows, ks]
                v = z_ref[rows, 2 * dk + hd * hv:2 * dk + (hd + 1) * hv]
                s_new = s_ref[hd] * etot[:, ks] + _dot_tn(v.astype(BF16), kd.astype(BF16))
                s_ref[hd] = s_new
                st_ref[c, hd] = s_new
            return carry

        def outputs(c, carry):
            rows = chunk_rows(c)
            for hd in range(HEADS):
                q = z_ref[rows, hd * hk:(hd + 1) * hk] * scale
                g = z_ref[rows, 2 * dk + dv + hd * hv:2 * dk + dv + (hd + 1) * hv]
                o = _dot_nt(q.astype(BF16), st_ref[c, hd].astype(BF16))
                on = o * lax.rsqrt(jnp.mean(o * o, axis=-1, keepdims=True) + EPS) * hg_ref[...]
                o_ref[rows, hd * hv:(hd + 1) * hv] = (on * (g * _sigmoid(g))).astype(BF16)
            return carry

        for phase in (decays, states, outputs):
            for c in range(per):
                phase(c, 0)
        ho_ref[...] = h_ref[...] + _dot(o_ref[...], wo_ref[...])

    d = h.shape[1]
    rows = lambda i: (i, 0)
    return pl.pallas_call(
        body, grid=(t // rt,),
        in_specs=[pl.BlockSpec((rt, d), rows), _resident((1, d)), _resident(w_in_t.shape), _resident(w_out.shape),
                  _resident(gate_w.shape), _resident((1, dk)), _resident((1, hv))],
        out_specs=[pl.BlockSpec((rt, d), rows), pl.BlockSpec((rt, dv), rows),
                   pl.BlockSpec((per, HEADS, hv, hk), lambda i: (i, 0, 0, 0)), pl.BlockSpec((rt, zw), rows),
                   pl.BlockSpec((rt, d), rows)],
        out_shape=[_sds((t, d), F32), _sds((t, dv), BF16), _sds((t // CHUNK, HEADS, hv, hk), F32), _sds((t, zw), F32),
                   _sds((t, d), BF16)],
        scratch_shapes=[pltpu.VMEM((HEADS, hv, hk), F32), pltpu.VMEM((rt, dk), F32), pltpu.VMEM((rt, dk), F32)],
        compiler_params=_cparams("arbitrary"), name=name)(h, gain, w_in_t, w_out, gate_w, gate_b, head_g)


def _gla_mid_bwd(z, h, gain, w_in_t, dh, w_out, states, gate_w, gate_b, head_g, *, after=None, name):
    t = z.shape[0]
    dk = gate_b.shape[1]
    hv = head_g.shape[1]
    hk = dk // HEADS
    dv = hv * HEADS
    r_at = 2 * dk + 2 * dv
    rt = _row_tile(t, 320)
    ntile = t // rt
    per = rt // CHUNK
    scale = hk ** -0.5
    dep_specs, deps = _dep_specs(after)

    def body(*refs):
        (z_ref, h_ref, g_ref, wi_ref, dh_ref, wo_ref, st_ref, stp_ref, gw_ref, gb_ref, hg_ref,
         dhi_ref, dg_ref, dz_ref, dgw_ref, dgb_ref, dhg_ref,
         ds_ref, la_ref, dla_ref, dec_ref, dos_ref, e_ref, do_ref) = refs[len(deps):]
        step = pl.program_id(0)
        tile = ntile - 1 - step
        do_ref[...] = _dot_nt(dh_ref[...].astype(BF16), wo_ref[...])

        @pl.when(step == 0)
        def _():
            ds_ref[...] = jnp.zeros_like(ds_ref)
            dgw_ref[...] = jnp.zeros_like(dgw_ref)
            dgb_ref[...] = jnp.zeros_like(dgb_ref)
            dhg_ref[...] = jnp.zeros_like(dhg_ref)

        rows_id = _row_ids(tile, rt)
        r = z_ref[:, r_at:r_at + GATE_PAD]
        gp, la = _log_decay(r, gw_ref, gb_ref, rows_id)
        la_ref[...] = la
        tri, tri_strict = _tri(False), _tri(True)
        keep = jnp.where(tile > 0, 1.0, 0.0)

        def chunk_rows(c):
            return slice(c * CHUNK, (c + 1) * CHUNK)

        def recompute(c, dhg):
            rows = chunk_rows(c)
            la_c = la_ref[rows, :]
            cum = _tri_dot(tri, la_c)
            dec_ref[rows, :] = jnp.exp(_rowsum(la_c) - cum)
            for hd in range(HEADS):
                q = (z_ref[rows, hd * hk:(hd + 1) * hk] * scale).astype(BF16)
                g = z_ref[rows, 2 * dk + dv + hd * hv:2 * dk + dv + (hd + 1) * hv]
                s_b = st_ref[c, hd].astype(BF16)
                o = _dot_nt(q, s_b)
                rstd = lax.rsqrt(jnp.mean(o * o, axis=-1, keepdims=True) + EPS)
                oh = o * rstd
                sg = _sigmoid(g)
                d_og = do_ref[rows, hd * hv:(hd + 1) * hv]
                dz_ref[rows, 2 * dk + dv + hd * hv:2 * dk + dv + (hd + 1) * hv] = (
                    d_og * oh * hg_ref[...] * (sg * (1.0 + g * (1.0 - sg)))).astype(BF16)
                don = d_og * (g * sg)
                dhg = dhg + _rowsum(don * oh)
                doh = don * hg_ref[...]
                d_o = (rstd * (doh - oh * jnp.mean(doh * oh, axis=-1, keepdims=True))).astype(BF16)
                dos_ref[rows, hd * hv:(hd + 1) * hv] = d_o
                dz_ref[rows, hd * hk:(hd + 1) * hk] = (_dot(d_o, s_b) * scale).astype(BF16)
            return dhg

        def recurrence(cc, carry):
            c = per - 1 - cc
            rows = chunk_rows(c)
            etot = jnp.exp(_rowsum(la_ref[rows, :]))
            for hd in range(HEADS):
                ks = slice(hd * hk, (hd + 1) * hk)
                q = (z_ref[rows, hd * hk:(hd + 1) * hk] * scale).astype(BF16)
                dec = dec_ref[rows, ks]
                kd = z_ref[rows, dk + hd * hk:dk + (hd + 1) * hk] * dec
                v = z_ref[rows, 2 * dk + hd * hv:2 * dk + (hd + 1) * hv].astype(BF16)
                s_prev = st_ref[c - 1, hd] if c > 0 else keep * stp_ref[0, hd]
                ds_t = ds_ref[hd] + _dot_tn(dos_ref[rows, hd * hv:(hd + 1) * hv], q)
                ds_b = ds_t.astype(BF16)
                dkd = _dot(v, ds_b)
                dz_ref[rows, 2 * dk + hd * hv:2 * dk + (hd + 1) * hv] = _dot_nt(kd.astype(BF16), ds_b).astype(BF16)
                dtot = etot[:, ks] * _rowsum(ds_t * s_prev)
                ds_ref[hd] = ds_t * etot[:, ks]
                dz_ref[rows, dk + hd * hk:dk + (hd + 1) * hk] = (dkd * dec).astype(BF16)
                e_ref[rows, ks] = dkd * kd
                dla_ref[rows, ks] = jnp.broadcast_to(dtot, (CHUNK, hk))
            return carry

        def decay_cotangent(c, carry):
            rows = chunk_rows(c)
            dla_ref[rows, :] += _tri_dot(tri_strict, e_ref[rows, :])
            return carry

        dhg = jnp.zeros((1, hv), F32)
        for c in range(per):
            dhg = recompute(c, dhg)
        dhg_ref[...] += dhg
        for phase in (recurrence, decay_cotangent):
            for c in range(per):
                phase(c, 0)
        dla = jnp.where(rows_id >= PAD_ROWS, dla_ref[...], 0.0)
        dgp = dla * (1.0 / GATE_NORM) * (1.0 - _sigmoid(gp))
        dgb_ref[...] += _rowsum(dgp)
        dgp_b = dgp.astype(BF16)
        dgw_ref[...] += _dot_tn(r.astype(BF16), dgp_b)
        dz_ref[:, r_at:r_at + GATE_PAD] = _dot_nt(dgp_b, gw_ref[...]).astype(BF16)
        _project_back(dz_ref, wi_ref, h_ref, g_ref, dh_ref, dhi_ref, dg_ref, step == 0)

    d = h.shape[1]
    back = lambda i: (ntile - 1 - i, 0)
    const2 = lambda i: (0, 0)
    return pl.pallas_call(
        body, grid=(ntile,),
        in_specs=dep_specs + [
                  pl.BlockSpec((rt, z.shape[1]), back), pl.BlockSpec((rt, d), back), _resident((1, d)),
                  _resident(w_in_t.shape), pl.BlockSpec((rt, d), back), _resident(w_out.shape),
                  pl.BlockSpec((per, HEADS, hv, hk), lambda i: (ntile - 1 - i, 0, 0, 0)),
                  pl.BlockSpec((1, HEADS, hv, hk), lambda i: (jnp.maximum((ntile - 1 - i) * per - 1, 0), 0, 0, 0)),
                  _resident(gate_w.shape), _resident((1, dk)), _resident((1, hv))],
        out_specs=[pl.BlockSpec((rt, d), back), pl.BlockSpec((1, d), const2), pl.BlockSpec((rt, z.shape[1]), back),
                   pl.BlockSpec(gate_w.shape, const2), pl.BlockSpec((1, dk), const2), pl.BlockSpec((1, hv), const2)],
        out_shape=[_sds((t, d), F32), _sds((1, d), F32), _sds(z.shape, BF16), _sds(gate_w.shape, F32),
                   _sds((1, dk), F32), _sds((1, hv), F32)],
        scratch_shapes=[pltpu.VMEM((HEADS, hv, hk), F32), pltpu.VMEM((rt, dk), F32), pltpu.VMEM((rt, dk), F32),
                        pltpu.VMEM((rt, dk), F32), pltpu.VMEM((rt, dv), BF16), pltpu.VMEM((rt, dk), F32),
                        pltpu.VMEM((rt, dv), F32)],
        compiler_params=_cparams("arbitrary"), name=name)(*deps, z, h, gain, w_in_t, dh, w_out, states, states, gate_w,
                                                          gate_b, head_g)


def _head(h, gain, target, *, name):
    t, d = h.shape
    rt = _row_tile(t, 832)

    def body(h_ref, g_ref, t_ref, dh_ref, loss_ref, dg_ref):
        i = pl.program_id(0)

        @pl.when(i == 0)
        def _():
            loss_ref[...] = jnp.zeros_like(loss_ref)
            dg_ref[...] = jnp.zeros_like(dg_ref)

        hv = h_ref[...]
        rstd = lax.rsqrt(jnp.mean(hv * hv, axis=-1, keepdims=True) + EPS)
        xh = hv * rstd
        err = jnp.where(_row_ids(i, rt) >= CHUNK, xh * g_ref[...] - t_ref[...], 0.0)
        loss_ref[...] += (0.5 / d) * jnp.sum(err * err)
        dy = err * (1.0 / d)
        dg_ref[...] += _rowsum(dy * xh)
        dxh = dy * g_ref[...]
        dh_ref[...] = rstd * (dxh - xh * jnp.mean(dxh * xh, axis=-1, keepdims=True))

    return pl.pallas_call(
        body, grid=(t // rt,),
        in_specs=[pl.BlockSpec((rt, d), lambda i: (i, 0)), _resident((1, d)), pl.BlockSpec((rt, d), lambda i: (i, 0))],
        out_specs=[pl.BlockSpec((rt, d), lambda i: (i, 0)), pl.BlockSpec((8, LANE), lambda i: (0, 0)),
                   pl.BlockSpec((1, d), lambda i: (0, 0))],
        out_shape=[_sds((t, d), F32), _sds((8, LANE), F32), _sds((1, d), F32)],
        compiler_params=_cparams("arbitrary"), name=name)(h, gain, target)


def _adamw_math(w, g, m, v):
    m = ADAM_B1 * m + (1.0 - ADAM_B1) * g
    v = ADAM_B2 * v + (1.0 - ADAM_B2) * (g * g)
    m_hat = m / (1.0 - ADAM_B1 ** ADAM_STEP)
    v_hat = v / (1.0 - ADAM_B2 ** ADAM_STEP)
    return -ADAM_LR * (m_hat / (jnp.sqrt(v_hat) + ADAM_EPS) + ADAM_WD * w), m, v


N_CHIP = N_DEV // 2
BLOCK_ELEMS = 128 * 1024


def _my_slot():
    return 4 * lax.axis_index("x") + 2 * lax.axis_index("y") + lax.axis_index("c")


def _row_block(r, c):
    cap = max(8, BLOCK_ELEMS // (-(-c // LANE) * LANE))
    return max([b for b in range(8, r + 1, 8) if r % b == 0 and b <= cap] or [r])


def _blocks(r, c):
    rb = _row_block(r, c)
    if rb < r or r * c <= BLOCK_ELEMS:
        return rb, c
    return r, max([b for b in (512, 256, LANE) if c % b == 0 and r * b <= BLOCK_ELEMS] or [c])


def _reduce_adam(parts, w, m, v, *, after=None, name):
    nl, r, c = w.shape
    rb, cb = _blocks(r, c)
    dep_specs, deps = _dep_specs(after)

    def body(*refs):
        me = refs[0][0]
        refs = refs[1 + len(deps):]
        p_refs = refs[:2 * nl]
        w_ref, m_ref, v_ref, g_out, d_out, m_out, v_out = refs[2 * nl:]
        layer = pl.program_id(0)
        for li in range(nl):
            @pl.when(layer == li)
            def _(li=li):
                own_ref, land_ref = p_refs[2 * li], p_refs[2 * li + 1]
                mine = own_ref[...].astype(F32)
                g = None
                for dev in range(N_DEV):
                    term = jnp.where(me == dev, mine, land_ref[dev].astype(F32))
                    g = term if g is None else g + term
                g_out[...] = g
                d_out[...], m_out[...], v_out[...] = _adamw_math(w_ref[...], g, m_ref[...], v_ref[...])

    blk = pl.BlockSpec((None, rb, cb), lambda l, i, j, me: (l, i, j))
    p_specs = []
    for li in range(nl):
        p_specs += [
            pl.BlockSpec((None, rb, cb), lambda l, i, j, me, li=li: (me[0], jnp.where(l == li, i, 0), jnp.where(l == li, j, 0))),
            pl.BlockSpec((N_DEV, rb, cb), lambda l, i, j, me, li=li: (0, jnp.where(l == li, i, 0), jnp.where(l == li, j, 0)))]
    flat = [p for pair in parts for p in pair]
    grid_spec = pltpu.PrefetchScalarGridSpec(
        num_scalar_prefetch=1, grid=(nl, r // rb, c // cb), in_specs=dep_specs + p_specs + [blk, blk, blk],
        out_specs=[blk] * 4)
    return pl.pallas_call(
        body, grid_spec=grid_spec, out_shape=[_sds(w.shape, F32)] * 4,
        compiler_params=_cparams("arbitrary", "arbitrary", "arbitrary"), name=name)(
        _my_slot().reshape(1), *deps, *flat, w, m, v)


def _adam_small(own, landed, split, w, m, v, *, name):
    n = len(w)

    def body(*refs):
        own_refs, land_refs, w_refs, m_refs, v_refs = (refs[k * n:(k + 1) * n] for k in range(5))
        outs = refs[5 * n:]
        me = _my_slot()
        for k in range(n):
            mine = own_refs[k][me] if split[k] else own_refs[k][...]
            g = None
            for dev in range(N_DEV):
                term = jnp.where(me == dev, mine, land_refs[k][dev])
                g = term if g is None else g + term
            outs[4 * k][...] = g
            outs[4 * k + 1][...], outs[4 * k + 2][...], outs[4 * k + 3][...] = _adamw_math(
                w_refs[k][...], g, m_refs[k][...], v_refs[k][...])

    out = pl.pallas_call(body, out_shape=[_sds(a.shape, F32) for a in w for _ in range(4)],
                         compiler_params=pltpu.CompilerParams(vmem_limit_bytes=V7X_VMEM_LIMIT), name=name)(
        *own, *landed, *w, *m, *v)
    return [tuple(out[4 * k:4 * k + 4]) for k in range(n)]


_HBM = pl.BlockSpec(memory_space=pltpu.HBM)
_SEM = pl.BlockSpec(memory_space=pltpu.SEMAPHORE)
_DATAFLOW = pltpu.SideEffectType.DATAFLOW_SIDE_EFFECTING


def _plan_to_all(src, land):
    x, y, c = lax.axis_index("x"), lax.axis_index("y"), lax.axis_index("c")
    return [(src, land.at[_my_slot()], (x ^ ((d >> 2) & 1), y ^ ((d >> 1) & 1), c ^ (d & 1))) for d in range(1, N_DEV)]


def _plan_split_to_all(src, land):
    x, y, c = lax.axis_index("x"), lax.axis_index("y"), lax.axis_index("c")
    peers = [(x ^ ((d >> 2) & 1), y ^ ((d >> 1) & 1), c ^ (d & 1)) for d in range(1, N_DEV)]
    return [(src.at[4 * px + 2 * py + pc], land.at[_my_slot()], (px, py, pc)) for px, py, pc in peers]


_PLAN_COPIES = {_plan_to_all: N_DEV - 1, _plan_split_to_all: N_DEV - 1}


def _plans(plan, n):
    return list(plan) if isinstance(plan, (list, tuple)) else [plan] * n


def _exchange_copies(plan, ins, lands, send, recv):
    copies, sem = [], 0
    for p, src, land in zip(_plans(plan, len(lands)), ins, lands):
        for s, dst, dev in p(src, land):
            copies.append(pltpu.make_async_remote_copy(
                src_ref=s, dst_ref=dst, send_sem=send.at[sem], recv_sem=recv.at[sem],
                device_id=dev, device_id_type=pl.DeviceIdType.MESH))
            sem += 1
    return copies


def _place_own(a, dtype, *, after=None, name):
    r, c = a.shape
    rb = _row_block(r, c)
    dep_specs, deps = _dep_specs(after)

    def body(*refs):
        a_ref, o_ref = refs[1 + len(deps):]
        o_ref[...] = a_ref[...].astype(dtype)

    grid_spec = pltpu.PrefetchScalarGridSpec(
        num_scalar_prefetch=1, grid=(r // rb,), in_specs=dep_specs + [pl.BlockSpec((rb, c), lambda i, me: (i, 0))],
        out_specs=pl.BlockSpec((None, rb, c), lambda i, me: (me[0], i, 0)))
    return pl.pallas_call(body, grid_spec=grid_spec, out_shape=_sds((N_DEV, r, c), dtype),
                          compiler_params=_cparams("arbitrary"), name=name)(_my_slot().reshape(1), *deps, a)


def _plan_gather_first(land, _):
    x, y, c = lax.axis_index("x"), lax.axis_index("y"), lax.axis_index("c")
    mine = land.at[_my_slot()]
    return [(mine, mine, (x, y, 1 - c))] + [(mine, mine, (x ^ (d >> 1), y ^ (d & 1), c)) for d in range(1, N_CHIP)]


def _plan_gather_relay(land, _):
    x, y, c = lax.axis_index("x"), lax.axis_index("y"), lax.axis_index("c")
    slots = [land.at[4 * (x ^ (d >> 1)) + 2 * (y ^ (d & 1)) + c] for d in range(1, N_CHIP)]
    return [(s, s, (x, y, 1 - c)) for s in slots]


def _plan_gather_direct(land, _):
    return _plan_to_all(land.at[_my_slot()], land)


_PLAN_COPIES[_plan_gather_first] = N_CHIP
_PLAN_COPIES[_plan_gather_relay] = N_CHIP - 1
_PLAN_COPIES[_plan_gather_direct] = N_DEV - 1


def _exchange_start(plan, arrs, lands, *, after=None, name):
    bufs = list(lands) if arrs is None else list(arrs) + list(lands)
    n, nb = len(lands), len(bufs)
    nsem = sum(_PLAN_COPIES[p] for p in _plans(plan, n))
    dep_specs, deps = _dep_specs(after)

    def body(*refs):
        ins, land_refs = refs[:n], refs[nb - n:nb]
        send, recv = refs[nb + len(deps)], refs[nb + len(deps) + 1]
        for cp in _exchange_copies(plan, ins, land_refs, send, recv):
            cp.start()
        refs[-1][...] = jnp.zeros_like(refs[-1])

    out = pl.pallas_call(
        body, name=name,
        out_shape=(pltpu.SemaphoreType.DMA((nsem,)), pltpu.SemaphoreType.DMA((nsem,)),
                   *[pltpu.HBM(a.shape, a.dtype) for a in bufs], _sds((8, LANE), F32)),
        in_specs=[_HBM] * nb + dep_specs,
        out_specs=(_SEM, _SEM, *([_HBM] * nb), pl.BlockSpec(memory_space=pltpu.VMEM)),
        input_output_aliases={i: 2 + i for i in range(nb)},
        compiler_params=pltpu.CompilerParams(has_side_effects=_DATAFLOW),
    )(*[pltpu.with_memory_space_constraint(a, pltpu.HBM) for a in bufs], *deps)
    return (plan, n, out[0], out[1], list(out[2:2 + nb])), out[-1]


def _exchange_now(plan, lands, *, name):
    n = len(lands)
    nsem = sum(_PLAN_COPIES[p] for p in _plans(plan, n))

    def body(*refs):
        land_refs, send, recv = refs[n:2 * n], refs[2 * n], refs[2 * n + 1]
        copies = _exchange_copies(plan, land_refs, land_refs, send, recv)
        for cp in copies:
            cp.start()
        for cp in copies:
            cp.wait_send()
            cp.wait_recv()

    hbm = pl.BlockSpec(memory_space=pl.ANY)
    return pl.pallas_call(
        body, in_specs=[hbm] * n, out_specs=[hbm] * n, out_shape=[_sds(a.shape, a.dtype) for a in lands],
        input_output_aliases={i: i for i in range(n)},
        scratch_shapes=[pltpu.SemaphoreType.DMA((nsem,)), pltpu.SemaphoreType.DMA((nsem,))], name=name)(*lands)


def _exchange_wait(state, after, *, name):
    plan, n, send_sem, recv_sem, bufs = state
    nb = len(bufs)
    after = list(after) if isinstance(after, (list, tuple)) else [after]

    def body(*refs):
        ins, land_refs, send, recv = refs[:n], refs[nb - n:nb], refs[nb], refs[nb + 1]
        for cp in _exchange_copies(plan, ins, land_refs, send, recv):
            cp.wait_send()
            cp.wait_recv()

    out = pl.pallas_call(
        body, name=name, out_shape=[pltpu.HBM(a.shape, a.dtype) for a in bufs],
        in_specs=[_HBM] * nb + [_SEM, _SEM] + [pl.BlockSpec(memory_space=pl.ANY)] * len(after), out_specs=[_HBM] * nb,
        input_output_aliases={i: i for i in range(nb)},
        compiler_params=pltpu.CompilerParams(has_side_effects=_DATAFLOW),
    )(*bufs, send_sem, recv_sem, *after)
    return list(out[:n]), list(out[nb - n:])


def _dep_specs(after):
    return ([], []) if after is None else ([pl.BlockSpec(memory_space=pl.ANY)], [after])


def _undo_column_split(g):
    return jnp.transpose(g, (1, 0, 2)).reshape(g.shape[1], N_DEV * g.shape[2])


def _column_split(a):
    r, c = a.shape
    return jnp.transpose(a.reshape(r, N_DEV, c // N_DEV), (1, 0, 2))


class _WholeWeights:
    def __init__(self, groups):
        self.groups = groups
        self.grads = {}

    def fetch(self, group, after):
        return self.groups[group]

    def emit(self, group, grads):
        self.grads.update(grads)
        return None


def _local_step(x, target, replicated, src):
    d = x.shape[1]
    mix_g, ffn_g = replicated["mix_g"], replicated["ffn_g"]
    h0 = jnp.concatenate([jnp.zeros((CHUNK, d), F32), x], axis=0)
    tgt = jnp.concatenate([jnp.zeros((CHUNK, d), F32), target], axis=0)
    cp = src.fetch("cp", [h0, tgt])
    h0 = lax.dynamic_update_slice(h0, cp["meta"], (PAD_ROWS, 0))
    cp_mid = (cp["conv_w"], replicated["conv_b"], replicated["ln_g"], replicated["ln_b"], replicated["pool_w"],
              replicated["pool_scale"])

    h1, cat, z0, u0, cv0 = _cp_mid_fwd(h0, mix_g[0:1], cp["cp_w_in_t"], cp["cp_w_out"], *cp_mid, name="cp_mixer")
    ffn0 = src.fetch("ffn0", h1)
    h2, uf0, rf0 = _ffn_fwd(h1, ffn_g[0:1], ffn0["w1"], ffn0["w2"], name="ffn0")
    gla = src.fetch("gla", h2)
    gla_mid = (gla["gate_w"], gla["gate_b"], gla["head_g"])
    h3, og, states, z1, u1 = _gla_mid_fwd(h2, mix_g[1:2], gla["gla_w_in_t"], gla["gla_w_out"], *gla_mid, name="gla_mixer")
    ffn1 = src.fetch("ffn1", h3)
    h4, uf1, rf1 = _ffn_fwd(h3, ffn_g[1:2], ffn1["w1"], ffn1["w2"], name="ffn1")
    dh4, loss, d_final_g = _head(h4, replicated["final_g"], tgt, name="head")

    dh3, dhh1, dob1, dffn_g1 = _ffn_bwd_x(h3, dh4, ffn_g[1:2], rf1, ffn1["w1"], ffn1["w2"], name="ffn1_bwd_x")
    dw1_1, dw2_1 = _ffn_bwd_w(uf1, dhh1, rf1, dob1, name="ffn1_bwd_w")
    sent = src.emit("ffn1", dict(w1=dw1_1, w2=dw2_1))
    d_gla_w_out = _linear_bwd_w(og, dh3, name="gla_out_dw")
    dh2, dmix_g1, dz1, d_gate_w, d_gate_b, d_head_g = _gla_mid_bwd(
        z1, h2, mix_g[1:2], gla["gla_w_in_t"], dh3, gla["gla_w_out"], states, *gla_mid, after=sent, name="gla_mixer_bwd")
    d_gla_w_in_t = _linear_bwd_w(dz1, u1, name="gla_in_dw")
    sent = src.emit("gla", dict(gla_w_in_t=d_gla_w_in_t, gla_w_out=d_gla_w_out))
    dh1, dhh0, dob0, dffn_g0 = _ffn_bwd_x(h1, dh2, ffn_g[0:1], rf0, ffn0["w1"], ffn0["w2"], after=sent, name="ffn0_bwd_x")
    dw1_0, dw2_0 = _ffn_bwd_w(uf0, dhh0, rf0, dob0, name="ffn0_bwd_w")
    src.emit("ffn0", dict(w1=dw1_0, w2=dw2_0))
    d_cp_w_out = _linear_bwd_w(cat, dh1, name="cp_out_dw")
    sent = src.emit("cp_out", dict(cp_w_out=d_cp_w_out))
    dh0, dmix_g0, dz0, d_conv_w, d_conv_b, d_ln_g, d_ln_b, d_pool_w, d_pool_scale = _cp_mid_bwd(
        z0, cv0, h0, mix_g[0:1], cp["cp_w_in_t"], dh1, cp["cp_w_out"], *cp_mid, after=sent, name="cp_mixer_bwd")
    d_cp_w_in_t = _linear_bwd_w(dz0, u0, name="cp_in_dw")

    small = dict(
        mix_g=jnp.concatenate([dmix_g0, dmix_g1]), ffn_g=jnp.concatenate([dffn_g0, dffn_g1]), conv_b=d_conv_b, ln_g=d_ln_g,
        ln_b=d_ln_b, pool_w=d_pool_w, pool_scale=d_pool_scale, final_g=d_final_g, meta=dh0[PAD_ROWS:CHUNK], conv_w=d_conv_w,
        gate_w=d_gate_w, gate_b=d_gate_b, head_g=d_head_g)
    src.emit("cp", dict(cp_w_in_t=d_cp_w_in_t, small=small))
    return loss, dh0[CHUNK:], small


_REPLICATED = ("mix_norm_g", "ffn_norm_g", "cp_conv_b", "cp_ln_g", "cp_ln_b", "cp_pool_w", "cp_pool_scale", "final_norm_g")
_SMALL_SHARDED = ("meta_tokens", "cp_conv_w", "gla_gate_w2", "gla_gate_b", "gla_head_g")
_NAMES = ("meta_tokens", "mix_norm_g", "ffn_norm_g", "ffn_w1", "ffn_w2", "cp_w_in", "cp_conv_w", "cp_conv_b", "cp_ln_g",
          "cp_ln_b", "cp_pool_w", "cp_pool_scale", "cp_w_out", "gla_w_in", "gla_gate_w2", "gla_gate_b", "gla_head_g",
          "gla_w_out", "final_norm_g")
_SMALL_GRADS = ("mix_g", "ffn_g", "conv_b", "ln_g", "ln_b", "pool_w", "pool_scale", "final_g", "meta", "conv_w", "gate_w",
                "gate_b", "head_g")
_GROUPS = ("cp", "ffn0", "gla", "ffn1")
_TWO_LEG_GATHERS = ("cp", "ffn0")


class _Exchanges:
    def __init__(self, w, d):
        self.d = d
        small = [w[n].reshape(w[n].shape[-2:]) for n in _SMALL_SHARDED]
        self.small_shard_shapes = [w[n].shape for n in _SMALL_SHARDED]
        shards = dict(
            cp=[(w["cp_w_in"][0].T, BF16), (w["cp_w_out"][0], BF16)] + [(a, F32) for a in small],
            ffn0=[(w["ffn_w1"][0], BF16), (w["ffn_w2"][0], BF16)],
            gla=[(w["gla_w_in"][0].T, BF16), (w["gla_w_out"][0], BF16)],
            ffn1=[(w["ffn_w1"][1], BF16), (w["ffn_w2"][1], BF16)])
        self.gathers = {}
        self.sent = {}
        token = None
        for group in _GROUPS:
            lands = [_place_own(a, dtype, after=token, name=f"place_w_{group}_{k}")
                     for k, (a, dtype) in enumerate(shards[group])]
            plan = _plan_gather_first if group in _TWO_LEG_GATHERS else _plan_gather_direct
            self.gathers[group], token = _exchange_start(plan, None, lands, after=token, name=f"start_w_{group}")
        self.token = token

    def fetch(self, group, after):
        d = self.d
        after = (list(after) if isinstance(after, (list, tuple)) else [after]) + [self.token]
        _, got = _exchange_wait(self.gathers[group], after, name=f"wait_w_{group}")
        if group in _TWO_LEG_GATHERS:
            got = _exchange_now(_plan_gather_relay, got, name=f"relay_w_{group}")
        if group in ("ffn0", "ffn1"):
            return dict(w1=got[0], w2=got[1])
        if group == "gla":
            w_in_t = jnp.pad(got[0].reshape(-1, d), ((0, GATE_PAD - GATE_RANK), (0, 0)))
            return dict(gla_w_in_t=w_in_t, gla_w_out=got[1].reshape(d, d), gate_w=self.gate_w, gate_b=self.gate_b,
                        head_g=self.head_g)
        meta, conv_w, gate_w, self.gate_b, self.head_g = [_undo_column_split(a) for a in got[2:]]
        self.gate_w = jnp.pad(gate_w, ((0, GATE_PAD - GATE_RANK), (0, 0))).astype(BF16)
        return dict(cp_w_in_t=got[0].reshape(-1, d), cp_w_out=got[1].reshape(d, d), meta=meta,
                    conv_w=jnp.pad(conv_w, ((0, 1), (0, 0))))

    def emit(self, group, g):
        d = self.d
        if group in ("ffn0", "ffn1"):
            arrs = [g["w1"], g["w2"]]
        elif group == "gla":
            w_in_t = g["gla_w_in_t"][:3 * d + GATE_RANK]
            arrs = [w_in_t.reshape(N_DEV, -1, d), g["gla_w_out"].reshape(N_DEV, d // N_DEV, d)]
        elif group == "cp_out":
            arrs = [g["cp_w_out"].reshape(N_DEV, d // N_DEV, d)]
        else:
            s = dict(g["small"])
            s.update(pool_w=s["pool_w"][None], conv_w=s["conv_w"][:CONV_WIDTH], gate_w=s["gate_w"][:GATE_RANK])
            own = [s[n] for n in _SMALL_GRADS[:len(_REPLICATED)]]
            own += [_column_split(s[n]).reshape((N_DEV,) + shape)
                    for n, shape in zip(_SMALL_GRADS[len(_REPLICATED):], self.small_shard_shapes)]
            plans = [_plan_to_all] * len(_REPLICATED) + [_plan_split_to_all] * len(_SMALL_SHARDED)
            lands = [lax.empty((N_DEV,) + a.shape, F32) for a in own[:len(_REPLICATED)]]
            lands += [lax.empty(a.shape, F32) for a in own[len(_REPLICATED):]]
            self.small_sent, self.token = _exchange_start(plans, own, lands, after=self.token, name="start_g_small")
            arrs = [g["cp_w_in_t"].reshape(N_DEV, -1, d)]
        self.sent[group], self.token = _exchange_start(_plan_split_to_all, arrs, [lax.empty(a.shape, a.dtype) for a in arrs],
                                                       after=self.token, name=f"start_g_{group}")
        return self.token

    def finish(self, w, mom, var):
        out = {}
        after = self.token

        def landed(group):
            own, got = _exchange_wait(self.sent[group], after, name=f"wait_g_{group}")
            return list(zip(own, got))

        def adam(n, parts, behind=None, transposed=False):
            flip = (lambda a: jnp.transpose(a, (0, 2, 1))) if transposed else (lambda a: a)
            res = _reduce_adam(parts, flip(w[n]), flip(mom[n]), flip(var[n]), after=behind, name=f"adam_{n}")
            out[n] = tuple(flip(a) for a in res)
            return res[0]

        ffn1 = landed("ffn1")
        after = ffn1[0][1]
        gla = landed("gla")
        after = gla[0][1]
        ffn0 = landed("ffn0")
        after = adam("ffn_w1", [ffn0[0], ffn1[0]])
        after = adam("ffn_w2", [ffn0[1], ffn1[1]], after)
        after = adam("gla_w_in", [gla[0]], after, transposed=True)
        after = adam("gla_w_out", [gla[1]], after)
        small_own, small_landed = _exchange_wait(self.small_sent, after, name="wait_g_small")
        names = _REPLICATED + _SMALL_SHARDED
        split = [False] * len(_REPLICATED) + [True] * len(_SMALL_SHARDED)
        small_new = _adam_small(small_own, small_landed, split, [w[n] for n in names], [mom[n] for n in names],
                                [var[n] for n in names], name="adam_small")
        out.update(zip(names, small_new))

        after = small_new[0][0]
        cp_out = landed("cp_out")
        after = adam("cp_w_out", [cp_out[0]])
        cp = landed("cp")
        adam("cp_w_in", [cp[0]], transposed=True)
        return out


def kernel(x, meta_tokens, mix_norm_g, ffn_norm_g, ffn_w1, ffn_w2, cp_w_in, cp_conv_w, cp_conv_b, cp_ln_g, cp_ln_b, cp_pool_w, cp_pool_scale, cp_w_out, gla_w_in, gla_gate_w2, gla_gate_b, gla_head_g, gla_w_out, final_norm_g, loss_target, m_meta_tokens, m_mix_norm_g, m_ffn_norm_g, m_ffn_w1, m_ffn_w2, m_cp_w_in, m_cp_conv_w, m_cp_conv_b, m_cp_ln_g, m_cp_ln_b, m_cp_pool_w, m_cp_pool_scale, m_cp_w_out, m_gla_w_in, m_gla_gate_w2, m_gla_gate_b, m_gla_head_g, m_gla_w_out, m_final_norm_g, v_meta_tokens, v_mix_norm_g, v_ffn_norm_g, v_ffn_w1, v_ffn_w2, v_cp_w_in, v_cp_conv_w, v_cp_conv_b, v_cp_ln_g, v_cp_ln_b, v_cp_pool_w, v_cp_pool_scale, v_cp_w_out, v_gla_w_in, v_gla_gate_w2, v_gla_gate_b, v_gla_head_g, v_gla_w_out, v_final_norm_g):
    w = dict(meta_tokens=meta_tokens, mix_norm_g=mix_norm_g, ffn_norm_g=ffn_norm_g, ffn_w1=ffn_w1, ffn_w2=ffn_w2,
             cp_w_in=cp_w_in, cp_conv_w=cp_conv_w, cp_conv_b=cp_conv_b, cp_ln_g=cp_ln_g, cp_ln_b=cp_ln_b,
             cp_pool_w=cp_pool_w, cp_pool_scale=cp_pool_scale, cp_w_out=cp_w_out, gla_w_in=gla_w_in,
             gla_gate_w2=gla_gate_w2, gla_gate_b=gla_gate_b, gla_head_g=gla_head_g, gla_w_out=gla_w_out,
             final_norm_g=final_norm_g.reshape(1, -1))
    mom = dict(meta_tokens=m_meta_tokens, mix_norm_g=m_mix_norm_g, ffn_norm_g=m_ffn_norm_g, ffn_w1=m_ffn_w1, ffn_w2=m_ffn_w2,
               cp_w_in=m_cp_w_in, cp_conv_w=m_cp_conv_w, cp_conv_b=m_cp_conv_b, cp_ln_g=m_cp_ln_g, cp_ln_b=m_cp_ln_b,
               cp_pool_w=m_cp_pool_w, cp_pool_scale=m_cp_pool_scale, cp_w_out=m_cp_w_out, gla_w_in=m_gla_w_in,
               gla_gate_w2=m_gla_gate_w2, gla_gate_b=m_gla_gate_b, gla_head_g=m_gla_head_g, gla_w_out=m_gla_w_out,
               final_norm_g=m_final_norm_g.reshape(1, -1))
    var = dict(meta_tokens=v_meta_tokens, mix_norm_g=v_mix_norm_g, ffn_norm_g=v_ffn_norm_g, ffn_w1=v_ffn_w1, ffn_w2=v_ffn_w2,
               cp_w_in=v_cp_w_in, cp_conv_w=v_cp_conv_w, cp_conv_b=v_cp_conv_b, cp_ln_g=v_cp_ln_g, cp_ln_b=v_cp_ln_b,
               cp_pool_w=v_cp_pool_w, cp_pool_scale=v_cp_pool_scale, cp_w_out=v_cp_w_out, gla_w_in=v_gla_w_in,
               gla_gate_w2=v_gla_gate_w2, gla_gate_b=v_gla_gate_b, gla_head_g=v_gla_head_g, gla_w_out=v_gla_w_out,
               final_norm_g=v_final_norm_g.reshape(1, -1))
    d = x.shape[-1]
    replicated = dict(mix_g=w["mix_norm_g"], ffn_g=w["ffn_norm_g"], conv_b=w["cp_conv_b"], ln_g=w["cp_ln_g"],
                      ln_b=w["cp_ln_b"], pool_w=w["cp_pool_w"][0].astype(BF16), pool_scale=w["cp_pool_scale"],
                      final_g=w["final_norm_g"])
    exchanges = _Exchanges(w, d)
    loss_blk, grad_x, _ = _local_step(x[0], loss_target[0], replicated, exchanges)
    loss = lax.psum(loss_blk[0, 0], ("x", "y", "c"))
    out = exchanges.finish(w, mom, var)

    def leaf(n, k):
        a = out[n][k]
        return a.reshape(-1) if n == "final_norm_g" else a

    return (loss, grad_x[None], *[leaf(n, 0) for n in _NAMES], *[leaf(n, 1) for n in _NAMES],
            *[leaf(n, 2) for n in _NAMES], *[leaf(n, 3) for n in _NAMES])
```

```python
import functools

import jax
import jax.numpy as jnp
from jax import lax
from jax.experimental import pallas as pl
from jax.experimental.pallas import tpu as pltpu

F32, BF16 = jnp.float32, jnp.bfloat16
N_DEV = 8
CHUNK = 64
N_META = 16
PAD_ROWS = CHUNK - N_META
HALO = 32
EPS = 1e-5
CONV_WIDTH = 31
POOL_WINDOWS = (2, 4, 8, 16)
HEADS = 4
GATE_RANK = 16
GATE_NORM = 16.0
GATE_PAD = 128
ADAM_LR, ADAM_B1, ADAM_B2, ADAM_EPS, ADAM_WD, ADAM_STEP = 0.001, 0.9, 0.999, 1e-08, 0.01, 10
V7X_VMEM_LIMIT = 56 * 2 ** 20
LANE = 128


def _cparams(*sem):
    return pltpu.CompilerParams(dimension_semantics=sem, vmem_limit_bytes=V7X_VMEM_LIMIT)


def _row_tile(t, cap):
    best = CHUNK
    for r in range(CHUNK, min(t, cap) + 1, CHUNK):
        if t % r == 0:
            best = r
    return best


def _resident(shape):
    return pl.BlockSpec(shape, lambda *_: (0,) * len(shape), pipeline_mode=pl.Buffered(1))


def _dot(a, b):
    return jnp.dot(a, b, preferred_element_type=F32)


def _dot_nt(a, b):
    return lax.dot_general(a, b, (((1,), (1,)), ((), ())), preferred_element_type=F32)


def _dot_tn(a, b):
    return lax.dot_general(a, b, (((0,), (0,)), ((), ())), preferred_element_type=F32)


def _rowsum(a):
    return jnp.sum(a, axis=0, keepdims=True)


def _sigmoid(a):
    return 1.0 / (1.0 + jnp.exp(-a))


def _row_ids(tile, rt):
    return tile * rt + lax.broadcasted_iota(jnp.int32, (rt, 1), 0)


def _sds(shape, dtype):
    return jax.ShapeDtypeStruct(shape, dtype)


DW_ROWS = 1024


def _linear_bwd_w(x, dy, *, name):
    t, k = x.shape
    n = dy.shape[1]
    cut_k = k > n
    width = k if cut_k else n
    blk = max(c for c in (640, 512, 384, 256, LANE) if width % c == 0)

    def body(x_ref, dy_ref, o_ref, acc):
        for c0 in range(0, t, DW_ROWS):
            rows = slice(c0, min(c0 + DW_ROWS, t))
            part = _dot_tn(x_ref[rows, :].astype(BF16), dy_ref[rows, :].astype(BF16))
            if c0 == 0:
                acc[...] = part
            else:
                acc[...] += part
        o_ref[...] = acc[...].astype(BF16)

    if cut_k:
        in_specs = [pl.BlockSpec((t, blk), lambda j: (0, j)), _resident((t, n))]
        out_specs = pl.BlockSpec((blk, n), lambda j: (j, 0))
        acc_shape = (blk, n)
    else:
        in_specs = [_resident((t, k)), pl.BlockSpec((t, blk), lambda j: (0, j))]
        out_specs = pl.BlockSpec((k, blk), lambda j: (0, j))
        acc_shape = (k, blk)
    return pl.pallas_call(
        body, grid=(width // blk,), in_specs=in_specs, out_specs=out_specs, out_shape=_sds((k, n), BF16),
        scratch_shapes=[pltpu.VMEM(acc_shape, F32)], compiler_params=_cparams("parallel"), name=name)(x, dy)


def _ffn_fwd(h, gain, w1g, w2g, *, name):
    t, d = h.shape
    f8 = w1g.shape[-1]
    rt = _row_tile(t, 832)

    def body(h_ref, g_ref, w1_ref, w2_ref, o_ref, u_ref, r_ref, acc_ref):
        j = pl.program_id(1)

        @pl.when(j == 0)
        def _():
            hv = h_ref[...]
            u_ref[...] = (hv * lax.rsqrt(jnp.mean(hv * hv, axis=-1, keepdims=True) + EPS) * g_ref[...]).astype(BF16)
            acc_ref[...] = jnp.zeros_like(acc_ref)

        a = jnp.maximum(_dot(u_ref[...], w1_ref[...]), 0.0)
        r_ref[...] = a.astype(BF16)
        acc_ref[...] += _dot((a * a).astype(BF16), w2_ref[...])

        @pl.when(j == N_DEV - 1)
        def _():
            o_ref[...] = h_ref[...] + acc_ref[...]

    return pl.pallas_call(
        body, grid=(t // rt, N_DEV),
        in_specs=[pl.BlockSpec((rt, d), lambda i, j: (i, 0)), _resident((1, d)),
                  pl.BlockSpec((None, d, f8), lambda i, j: (j, 0, 0)),
                  pl.BlockSpec((None, f8, d), lambda i, j: (j, 0, 0))],
        out_specs=[pl.BlockSpec((rt, d), lambda i, j: (i, 0)), pl.BlockSpec((rt, d), lambda i, j: (i, 0)),
                   pl.BlockSpec((rt, f8), lambda i, j: (i, j))],
        out_shape=[_sds((t, d), F32), _sds((t, d), BF16), _sds((t, N_DEV * f8), BF16)],
        scratch_shapes=[pltpu.VMEM((rt, d), F32)],
        compiler_params=_cparams("parallel", "arbitrary"), name=name)(h, gain, w1g, w2g)


def _ffn_bwd_x(h, dout, gain, r, w1g, w2g, *, after=None, name):
    t, d = h.shape
    f8 = w1g.shape[-1]
    rt = _row_tile(t, 832)
    last = N_DEV - 1
    dep_specs, deps = _dep_specs(after)

    def body(*refs):
        h_ref, do_ref, g_ref, r_ref, w1_ref, w2_ref, dh_ref, dhh_ref, dob_ref, dg_ref, du_ref = refs[len(deps):]
        i, j = pl.program_id(0), pl.program_id(1)

        @pl.when(j == 0)
        def _():
            dob_ref[...] = do_ref[...].astype(BF16)
            du_ref[...] = jnp.zeros_like(du_ref)

        dhh = (_dot_nt(dob_ref[...], w2_ref[...]) * (2.0 * r_ref[...].astype(F32))).astype(BF16)
        dhh_ref[...] = dhh
        du_ref[...] += _dot_nt(dhh, w1_ref[...])

        @pl.when(j == last)
        def _():
            @pl.when(i == 0)
            def _():
                dg_ref[...] = jnp.zeros_like(dg_ref)

            hv = h_ref[...]
            rstd = lax.rsqrt(jnp.mean(hv * hv, axis=-1, keepdims=True) + EPS)
            xh = hv * rstd
            du = du_ref[...]
            dg_ref[...] += _rowsum(du * xh)
            dxh = du * g_ref[...]
            dh_ref[...] = do_ref[...] + rstd * (dxh - xh * jnp.mean(dxh * xh, axis=-1, keepdims=True))

    rows = lambda i, j: (i, 0)
    return pl.pallas_call(
        body, grid=(t // rt, N_DEV),
        in_specs=dep_specs + [
                  pl.BlockSpec((rt, d), rows), pl.BlockSpec((rt, d), rows), _resident((1, d)),
                  pl.BlockSpec((rt, f8), lambda i, j: (i, j)),
                  pl.BlockSpec((None, d, f8), lambda i, j: (j, 0, 0)),
                  pl.BlockSpec((None, f8, d), lambda i, j: (j, 0, 0))],
        out_specs=[pl.BlockSpec((rt, d), rows), pl.BlockSpec((rt, f8), lambda i, j: (i, j)), pl.BlockSpec((rt, d), rows),
                   pl.BlockSpec((1, d), lambda i, j: (0, 0))],
        out_shape=[_sds((t, d), F32), _sds((t, N_DEV * f8), BF16), _sds((t, d), BF16), _sds((1, d), F32)],
        scratch_shapes=[pltpu.VMEM((rt, d), F32)],
        compiler_params=_cparams("arbitrary", "arbitrary"), name=name)(*deps, h, dout, gain, r, w1g, w2g)


def _ffn_bwd_w(u, dhh, r, dout_b, *, name):
    t, d = u.shape
    f8 = dhh.shape[1] // N_DEV

    def body(u_ref, dhh_ref, r_ref, dob_ref, dw1_ref, dw2_ref, acc1, acc2):
        for c0 in range(0, t, DW_ROWS):
            rows = slice(c0, min(c0 + DW_ROWS, t))
            rr = r_ref[rows, :].astype(F32)
            part1 = _dot_tn(u_ref[rows, :], dhh_ref[rows, :])
            part2 = _dot_tn((rr * rr).astype(BF16), dob_ref[rows, :])
            if c0 == 0:
                acc1[...] = part1
                acc2[...] = part2
            else:
                acc1[...] += part1
                acc2[...] += part2
        dw1_ref[...] = acc1[...].astype(BF16)
        dw2_ref[...] = acc2[...].astype(BF16)

    return pl.pallas_call(
        body, grid=(N_DEV,),
        in_specs=[_resident((t, d)), pl.BlockSpec((t, f8), lambda j: (0, j)), pl.BlockSpec((t, f8), lambda j: (0, j)),
                  _resident((t, d))],
        out_specs=[pl.BlockSpec((None, d, f8), lambda j: (j, 0, 0)), pl.BlockSpec((None, f8, d), lambda j: (j, 0, 0))],
        out_shape=[_sds((N_DEV, d, f8), BF16), _sds((N_DEV, f8, d), BF16)],
        scratch_shapes=[pltpu.VMEM((d, f8), F32), pltpu.VMEM((f8, d), F32)],
        compiler_params=_cparams("parallel"), name=name)(u, dhh, r, dout_b)


def _lane_blocks(width):
    lb = min(LANE, width)
    return [slice(s, s + lb) for s in range(0, width, lb)]


def _conv_rows(src_ref, w_ref, offset, dst_ref, nblk, width, bias_ref=None):
    def blk(rb, carry):
        base = pl.multiple_of(rb * CHUNK, CHUNK)
        for l, ls in enumerate(_lane_blocks(width)):
            acc = jnp.zeros((CHUNK, ls.stop - ls.start), F32)
            if bias_ref is not None:
                acc = acc + bias_ref[:, ls]
            for k in range(CONV_WIDTH):
                acc = acc + w_ref[k:k + 1, ls] * src_ref[l, pl.ds(base + offset(k), CHUNK), :]
            dst_ref[l, pl.ds(base, CHUNK), :] = acc
        return carry

    lax.fori_loop(0, nblk, blk, 0)


def _to_lane_blocks(ref, row0, value):
    for l, ls in enumerate(_lane_blocks(value.shape[1])):
        ref[l, row0:row0 + value.shape[0], :] = value[:, ls]


def _from_lane_blocks(ref):
    return jnp.concatenate([ref[l] for l in range(ref.shape[0])], axis=1)


def _pool_counts(rows, window):
    return jnp.clip(rows - PAD_ROWS + 1, 1, window).astype(F32)


def _trailing_sum(v, window):
    s, sh = v, 1
    while sh < window:
        s = s + pltpu.roll(s, sh, 0)
        sh *= 2
    return s


def _leading_sum(v, window):
    s, sh, n = v, 1, v.shape[0]
    while sh < window:
        s = s + pltpu.roll(s, n - sh, 0)
        sh *= 2
    return s


def _norm_project(h_ref, g_ref, w_t_ref, u_ref, z_ref):
    hv = h_ref[...]
    u = (hv * lax.rsqrt(jnp.mean(hv * hv, axis=-1, keepdims=True) + EPS) * g_ref[...]).astype(BF16)
    u_ref[...] = u
    z_ref[...] = _dot_nt(u, w_t_ref[...])


def _project_back(dz_ref, w_t_ref, h_ref, g_ref, dres_ref, dh_ref, dg_ref, first):
    dx = _dot(dz_ref[...], w_t_ref[...])
    hv = h_ref[...]
    rstd = lax.rsqrt(jnp.mean(hv * hv, axis=-1, keepdims=True) + EPS)
    xh = hv * rstd

    @pl.when(first)
    def _():
        dg_ref[...] = jnp.zeros_like(dg_ref)

    dg_ref[...] += _rowsum(dx * xh)
    dxh = dx * g_ref[...]
    dh_ref[...] = dres_ref[...] + rstd * (dxh - xh * jnp.mean(dxh * xh, axis=-1, keepdims=True))


def _cp_mid_fwd(h, gain, w_in_t, w_out, conv_w, conv_b, ln_g, ln_b, pool_w, pool_scale, *, name):
    t, d = h.shape
    ein = w_in_t.shape[0]
    cd = conv_b.shape[1]
    pd = pool_scale.shape[1]
    pg = pd // len(POOL_WINDOWS)
    rt = _row_tile(t, 320)

    def body(h_ref, g_ref, wi_ref, wo_ref, cw_ref, cb_ref, lg_ref, lb_ref, pw_ref, ps_ref,
             ho_ref, o_ref, z_ref, u_ref, cv_ref, gext, pext, conv_s):
        i = pl.program_id(0)

        @pl.when(i == 0)
        def _():
            _to_lane_blocks(gext, 0, jnp.zeros((HALO, cd), F32))
            pext[0:HALO, :] = jnp.zeros((HALO, pd), F32)

        _norm_project(h_ref, g_ref, wi_ref, u_ref, z_ref)

        _to_lane_blocks(gext, HALO, z_ref[:, 0:cd] * _sigmoid(z_ref[:, cd:2 * cd]))
        pext[HALO:HALO + rt, :] = z_ref[:, 2 * cd:]
        _conv_rows(gext, cw_ref, lambda k: k + HALO - (CONV_WIDTH - 1), conv_s, rt // CHUNK, cd, cb_ref)
        cv = _from_lane_blocks(conv_s)
        cv_ref[...] = cv
        xc = cv - jnp.mean(cv, axis=-1, keepdims=True)
        y = xc * lax.rsqrt(jnp.mean(xc * xc, axis=-1, keepdims=True) + EPS) * lg_ref[...] + lb_ref[...]
        rows = _row_ids(i, rt)
        a = jnp.where(rows >= PAD_ROWS, y * _sigmoid(y), 0.0)
        o_ref[:, 0:cd] = a.astype(BF16)
        for gi, window in enumerate(POOL_WINDOWS):
            ls = slice(gi * pg, (gi + 1) * pg)
            v = pext[:, ls]
            tm = _trailing_sum(v, window)[HALO:] / _pool_counts(rows, window) - v[HALO:]
            p = _dot(tm.astype(BF16), pw_ref[gi]) * ps_ref[:, ls]
            o_ref[:, cd + gi * pg:cd + (gi + 1) * pg] = p.astype(BF16)
        ho_ref[...] = h_ref[...] + _dot(o_ref[...], wo_ref[...])
        gext[:, 0:HALO, :] = gext[:, rt:rt + HALO, :]
        pext[0:HALO, :] = pext[rt:rt + HALO, :]

    nl, lb = len(_lane_blocks(cd)), min(LANE, cd)
    rows = lambda i: (i, 0)
    return pl.pallas_call(
        body, grid=(t // rt,),
        in_specs=[pl.BlockSpec((rt, d), rows), _resident((1, d)), _resident(w_in_t.shape), _resident(w_out.shape),
                  _resident(conv_w.shape), _resident((1, cd)),
                  _resident((1, cd)), _resident((1, cd)), _resident(pool_w.shape), _resident((1, pd))],
        out_specs=[pl.BlockSpec((rt, d), rows), pl.BlockSpec((rt, cd + pd), rows), pl.BlockSpec((rt, ein), rows),
                   pl.BlockSpec((rt, d), rows), pl.BlockSpec((rt, cd), rows)],
        out_shape=[_sds((t, d), F32), _sds((t, cd + pd), BF16), _sds((t, ein), F32), _sds((t, d), BF16),
                   _sds((t, cd), F32)],
        scratch_shapes=[pltpu.VMEM((nl, rt + HALO, lb), F32), pltpu.VMEM((rt + HALO, pd), F32),
                        pltpu.VMEM((nl, rt, lb), F32)],
        compiler_params=_cparams("arbitrary"), name=name)(h, gain, w_in_t, w_out, conv_w, conv_b, ln_g, ln_b, pool_w,
                                                          pool_scale)


def _cp_mid_bwd(z, cv, h, gain, w_in_t, dh, w_out, conv_w, conv_b, ln_g, ln_b, pool_w, pool_scale, *, after=None, name):
    t, ein = z.shape
    cd = conv_b.shape[1]
    pd = pool_scale.shape[1]
    pg = pd // len(POOL_WINDOWS)
    rt = _row_tile(t, 320)
    ntile = t // rt
    per = rt // CHUNK
    dep_specs, deps = _dep_specs(after)

    def body(*refs):
        (z_ref, zh_ref, cv_ref, h_ref, g_ref, wi_ref, dh_ref, wo_ref, cw_ref, cb_ref, lg_ref, lb_ref, pw_ref, ps_ref,
         dhi_ref, dg_ref, dz_ref, dcw_ref, dcb_ref, dlg_ref, dlb_ref, dpw_ref, dps_ref,
         gext, pext, conv_s, dcv, dsp) = refs[len(deps):]
        step = pl.program_id(0)
        tile = ntile - 1 - step
        dcat = _dot_nt(dh_ref[...].astype(BF16), wo_ref[...])

        @pl.when(step == 0)
        def _():
            for ref in (dcw_ref, dcb_ref, dlg_ref, dlb_ref, dpw_ref, dps_ref):
                ref[...] = jnp.zeros_like(ref)
            _to_lane_blocks(dcv, rt, jnp.zeros((HALO, cd), F32))
            dsp[rt:rt + HALO, :] = jnp.zeros((HALO, pd), F32)

        keep = jnp.where(tile > 0, 1.0, 0.0)
        zh = zh_ref[CHUNK - HALO:CHUNK, :]
        _to_lane_blocks(gext, 0, keep * zh[:, 0:cd] * _sigmoid(zh[:, cd:2 * cd]))
        pext[0:HALO, :] = keep * zh[:, 2 * cd:]
        za = z_ref[:, 0:cd]
        sg = _sigmoid(z_ref[:, cd:2 * cd])
        _to_lane_blocks(gext, HALO, za * sg)
        pext[HALO:HALO + rt, :] = z_ref[:, 2 * cd:]
        cv = cv_ref[...]
        xc = cv - jnp.mean(cv, axis=-1, keepdims=True)
        rstd = lax.rsqrt(jnp.mean(xc * xc, axis=-1, keepdims=True) + EPS)
        xh = xc * rstd
        y = xh * lg_ref[...] + lb_ref[...]
        sy = _sigmoid(y)
        rows = _row_ids(tile, rt)
        da = jnp.where(rows >= PAD_ROWS, dcat[:, 0:cd], 0.0)
        dy = da * (sy * (1.0 + y * (1.0 - sy)))
        dlg_ref[...] += _rowsum(dy * xh)
        dlb_ref[...] += _rowsum(dy)
        dxh = dy * lg_ref[...]
        dconv = rstd * (dxh - jnp.mean(dxh, axis=-1, keepdims=True) - xh * jnp.mean(dxh * xh, axis=-1, keepdims=True))
        dcb_ref[...] += _rowsum(dconv)
        _to_lane_blocks(dcv, 0, dconv)
        for l, ls in enumerate(_lane_blocks(cd)):
            def acc_rows(rb, accs, l=l):
                base = pl.multiple_of(rb * CHUNK, CHUNK)
                d_blk = dcv[l, pl.ds(base, CHUNK), :]
                out = []
                for k in range(CONV_WIDTH):
                    prod = d_blk * gext[l, pl.ds(base + k + HALO - (CONV_WIDTH - 1), CHUNK), :]
                    part = prod[0:8]
                    for s in range(8, CHUNK, 8):
                        part = part + prod[s:s + 8]
                    out.append(accs[k] + part)
                return tuple(out)

            zero = jnp.zeros((8, ls.stop - ls.start), F32)
            accs = lax.fori_loop(0, per, acc_rows, (zero,) * CONV_WIDTH)
            for k in range(CONV_WIDTH):
                dcw_ref[k:k + 1, ls] += _rowsum(accs[k])
        _conv_rows(dcv, cw_ref, lambda k: CONV_WIDTH - 1 - k, conv_s, per, cd)
        dglu = _from_lane_blocks(conv_s)
        dz_ref[:, 0:cd] = (dglu * sg).astype(BF16)
        dz_ref[:, cd:2 * cd] = (dglu * za * sg * (1.0 - sg)).astype(BF16)
        dcv[:, rt:rt + HALO, :] = dcv[:, 0:HALO, :]
        for gi, window in enumerate(POOL_WINDOWS):
            ls = slice(gi * pg, (gi + 1) * pg)
            v = pext[:, ls]
            cnt = _pool_counts(rows, window)
            tm = (_trailing_sum(v, window)[HALO:] / cnt - v[HALO:]).astype(BF16)
            dp = dcat[:, cd + gi * pg:cd + (gi + 1) * pg]
            dps_ref[:, ls] += _rowsum(dp * _dot(tm, pw_ref[gi]))
            dpl = (dp * ps_ref[:, ls]).astype(BF16)
            dpw_ref[gi] += _dot_tn(tm, dpl)
            dtm = _dot_nt(dpl, pw_ref[gi])
            dsp[0:rt, ls] = dtm / cnt
            dpin = _leading_sum(dsp[:, ls], window)[0:rt] - dtm
            dz_ref[:, 2 * cd + gi * pg:2 * cd + (gi + 1) * pg] = dpin.astype(BF16)
        dsp[rt:rt + HALO, :] = dsp[0:HALO, :]
        _project_back(dz_ref, wi_ref, h_ref, g_ref, dh_ref, dhi_ref, dg_ref, step == 0)

    d = h.shape[1]
    back = lambda i: (ntile - 1 - i, 0)
    halo_idx = lambda i: (jnp.maximum((ntile - 1 - i) * per - 1, 0), 0)
    const2 = lambda i: (0, 0)
    nl, lb = len(_lane_blocks(cd)), min(LANE, cd)
    return pl.pallas_call(
        body, grid=(ntile,),
        in_specs=dep_specs + [
                  pl.BlockSpec((rt, ein), back), pl.BlockSpec((CHUNK, ein), halo_idx), pl.BlockSpec((rt, cd), back),
                  pl.BlockSpec((rt, d), back),
                  _resident((1, d)), _resident(w_in_t.shape), pl.BlockSpec((rt, d), back), _resident(w_out.shape),
                  _resident(conv_w.shape), _resident((1, cd)), _resident((1, cd)), _resident((1, cd)),
                  _resident(pool_w.shape), _resident((1, pd))],
        out_specs=[pl.BlockSpec((rt, d), back), pl.BlockSpec((1, d), const2),
                   pl.BlockSpec((rt, ein), back), pl.BlockSpec(conv_w.shape, const2), pl.BlockSpec((1, cd), const2),
                   pl.BlockSpec((1, cd), const2), pl.BlockSpec((1, cd), const2),
                   pl.BlockSpec(pool_w.shape, lambda i: (0, 0, 0)), pl.BlockSpec((1, pd), const2)],
        out_shape=[_sds((t, d), F32), _sds((1, d), F32),
                   _sds((t, ein), BF16), _sds(conv_w.shape, F32), _sds((1, cd), F32), _sds((1, cd), F32),
                   _sds((1, cd), F32), _sds(pool_w.shape, F32), _sds((1, pd), F32)],
        scratch_shapes=[pltpu.VMEM((nl, rt + HALO, lb), F32), pltpu.VMEM((rt + HALO, pd), F32), pltpu.VMEM((nl, rt, lb), F32),
                        pltpu.VMEM((nl, rt + HALO, lb), F32), pltpu.VMEM((rt + HALO, pd), F32)],
        compiler_params=_cparams("arbitrary"), name=name)(*deps, z, z, cv, h, gain, w_in_t, dh, w_out, conv_w, conv_b, ln_g,
                                                          ln_b, pool_w, pool_scale)


def _log_decay(r, gw_ref, gb_ref, rows):
    gp = _dot(r.astype(BF16), gw_ref[...]) + gb_ref[...]
    log_sig = jnp.minimum(gp, 0.0) - jnp.log(1.0 + jnp.exp(-jnp.abs(gp)))
    return gp, jnp.where(rows >= PAD_ROWS, log_sig / GATE_NORM, 0.0)


def _tri(strict):
    r = lax.broadcasted_iota(jnp.int32, (CHUNK, CHUNK), 0)
    c = lax.broadcasted_iota(jnp.int32, (CHUNK, CHUNK), 1)
    return jnp.where(c < r if strict else c <= r, 1.0, 0.0).astype(BF16)


def _tri_dot(tri, a):
    hi = a.astype(BF16)
    rest = a - hi.astype(F32)
    mid = rest.astype(BF16)
    lo = (rest - mid.astype(F32)).astype(BF16)
    return _dot(tri, hi) + _dot(tri, mid) + _dot(tri, lo)


def _gla_mid_fwd(h, gain, w_in_t, w_out, gate_w, gate_b, head_g, *, name):
    t = h.shape[0]
    zw = w_in_t.shape[0]
    dk = gate_b.shape[1]
    hv = head_g.shape[1]
    hk = dk // HEADS
    dv = hv * HEADS
    r_at = 2 * dk + 2 * dv
    rt = _row_tile(t, 320)
    per = rt // CHUNK
    scale = hk ** -0.5

    def body(h_ref, g_ref, wi_ref, wo_ref, gw_ref, gb_ref, hg_ref, ho_ref, o_ref, st_ref, z_ref, u_ref,
             s_ref, la_ref, dec_ref):
        i = pl.program_id(0)

        @pl.when(i == 0)
        def _():
            s_ref[...] = jnp.zeros_like(s_ref)

        _norm_project(h_ref, g_ref, wi_ref, u_ref, z_ref)

        _, la = _log_decay(z_ref[:, r_at:r_at + GATE_PAD], gw_ref, gb_ref, _row_ids(i, rt))
        la_ref[...] = la
        tri = _tri(False)

        def chunk_rows(c):
            return slice(c * CHUNK, (c + 1) * CHUNK)

        def decays(c, carry):
            rows = chunk_rows(c)
            la_c = la_ref[rows, :]
            cum = _tri_dot(tri, la_c)
            dec_ref[rows, :] = jnp.exp(_rowsum(la_c) - cum)
            return carry

        def states(c, carry):
            rows = chunk_rows(c)
            etot = jnp.exp(_rowsum(la_ref[rows, :]))
            for hd in range(HEADS):
                ks = slice(hd * hk, (hd + 1) * hk)
                kd = z_ref[rows, dk + hd * hk:dk + (hd + 1) * hk] * dec_ref[rows, ks]
                v = z_ref[rows, 2 * dk + hd * hv:2 * dk + (hd + 1) * hv]
                s_new = s_ref[hd] * etot[:, ks] + _dot_tn(v.astype(BF16), kd.astype(BF16))
                s_ref[hd] = s_new
                st_ref[c, hd] = s_new
            return carry

        def outputs(c, carry):
            rows = chunk_rows(c)
            for hd in range(HEADS):
                q = z_ref[rows, hd * hk:(hd + 1) * hk] * scale
                g = z_ref[rows, 2 * dk + dv + hd * hv:2 * dk + dv + (hd + 1) * hv]
                o = _dot_nt(q.astype(BF16), st_ref[c, hd].astype(BF16))
                on = o * lax.rsqrt(jnp.mean(o * o, axis=-1, keepdims=True) + EPS) * hg_ref[...]
                o_ref[rows, hd * hv:(hd + 1) * hv] = (on * (g * _sigmoid(g))).astype(BF16)
            return carry

        for phase in (decays, states, outputs):
            for c in range(per):
                phase(c, 0)
        ho_ref[...] = h_ref[...] + _dot(o_ref[...], wo_ref[...])

    d = h.shape[1]
    rows = lambda i: (i, 0)
    return pl.pallas_call(
        body, grid=(t // rt,),
        in_specs=[pl.BlockSpec((rt, d), rows), _resident((1, d)), _resident(w_in_t.shape), _resident(w_out.shape),
                  _resident(gate_w.shape), _resident((1, dk)), _resident((1, hv))],
        out_specs=[pl.BlockSpec((rt, d), rows), pl.BlockSpec((rt, dv), rows),
                   pl.BlockSpec((per, HEADS, hv, hk), lambda i: (i, 0, 0, 0)), pl.BlockSpec((rt, zw), rows),
                   pl.BlockSpec((rt, d), rows)],
        out_shape=[_sds((t, d), F32), _sds((t, dv), BF16), _sds((t // CHUNK, HEADS, hv, hk), F32), _sds((t, zw), F32),
                   _sds((t, d), BF16)],
        scratch_shapes=[pltpu.VMEM((HEADS, hv, hk), F32), pltpu.VMEM((rt, dk), F32), pltpu.VMEM((rt, dk), F32)],
        compiler_params=_cparams("arbitrary"), name=name)(h, gain, w_in_t, w_out, gate_w, gate_b, head_g)


def _gla_mid_bwd(z, h, gain, w_in_t, dh, w_out, states, gate_w, gate_b, head_g, *, after=None, name):
    t = z.shape[0]
    dk = gate_b.shape[1]
    hv = head_g.shape[1]
    hk = dk // HEADS
    dv = hv * HEADS
    r_at = 2 * dk + 2 * dv
    rt = _row_tile(t, 320)
    ntile = t // rt
    per = rt // CHUNK
    scale = hk ** -0.5
    dep_specs, deps = _dep_specs(after)

    def body(*refs):
        (z_ref, h_ref, g_ref, wi_ref, dh_ref, wo_ref, st_ref, stp_ref, gw_ref, gb_ref, hg_ref,
         dhi_ref, dg_ref, dz_ref, dgw_ref, dgb_ref, dhg_ref,
         ds_ref, la_ref, dla_ref, dec_ref, dos_ref, e_ref, do_ref) = refs[len(deps):]
        step = pl.program_id(0)
        tile = ntile - 1 - step
        do_ref[...] = _dot_nt(dh_ref[...].astype(BF16), wo_ref[...])

        @pl.when(step == 0)
        def _():
            ds_ref[...] = jnp.zeros_like(ds_ref)
            dgw_ref[...] = jnp.zeros_like(dgw_ref)
            dgb_ref[...] = jnp.zeros_like(dgb_ref)
            dhg_ref[...] = jnp.zeros_like(dhg_ref)

        rows_id = _row_ids(tile, rt)
        r = z_ref[:, r_at:r_at + GATE_PAD]
        gp, la = _log_decay(r, gw_ref, gb_ref, rows_id)
        la_ref[...] = la
        tri, tri_strict = _tri(False), _tri(True)
        keep = jnp.where(tile > 0, 1.0, 0.0)

        def chunk_rows(c):
            return slice(c * CHUNK, (c + 1) * CHUNK)

        def recompute(c, dhg):
            rows = chunk_rows(c)
            la_c = la_ref[rows, :]
            cum = _tri_dot(tri, la_c)
            dec_ref[rows, :] = jnp.exp(_rowsum(la_c) - cum)
            for hd in range(HEADS):
                q = (z_ref[rows, hd * hk:(hd + 1) * hk] * scale).astype(BF16)
                g = z_ref[rows, 2 * dk + dv + hd * hv:2 * dk + dv + (hd + 1) * hv]
                s_b = st_ref[c, hd].astype(BF16)
                o = _dot_nt(q, s_b)
                rstd = lax.rsqrt(jnp.mean(o * o, axis=-1, keepdims=True) + EPS)
                oh = o * rstd
                sg = _sigmoid(g)
                d_og = do_ref[rows, hd * hv:(hd + 1) * hv]
                dz_ref[rows, 2 * dk + dv + hd * hv:2 * dk + dv + (hd + 1) * hv] = (
                    d_og * oh * hg_ref[...] * (sg * (1.0 + g * (1.0 - sg)))).astype(BF16)
                don = d_og * (g * sg)
                dhg = dhg + _rowsum(don * oh)
                doh = don * hg_ref[...]
                d_o = (rstd * (doh - oh * jnp.mean(doh * oh, axis=-1, keepdims=True))).astype(BF16)
                dos_ref[rows, hd * hv:(hd + 1) * hv] = d_o
                dz_ref[rows, hd * hk:(hd + 1) * hk] = (_dot(d_o, s_b) * scale).astype(BF16)
            return dhg

        def recurrence(cc, carry):
            c = per - 1 - cc
            rows = chunk_rows(c)
            etot = jnp.exp(_rowsum(la_ref[rows, :]))
            for hd in range(HEADS):
                ks = slice(hd * hk, (hd + 1) * hk)
                q = (z_ref[rows, hd * hk:(hd + 1) * hk] * scale).astype(BF16)
                dec = dec_ref[rows, ks]
                kd = z_ref[rows, dk + hd * hk:dk + (hd + 1) * hk] * dec
                v = z_ref[rows, 2 * dk + hd * hv:2 * dk + (hd + 1) * hv].astype(BF16)
                s_prev = st_ref[c - 1, hd] if c > 0 else keep * stp_ref[0, hd]
                ds_t = ds_ref[hd] + _dot_tn(dos_ref[rows, hd * hv:(hd + 1) * hv], q)
                ds_b = ds_t.astype(BF16)
                dkd = _dot(v, ds_b)
                dz_ref[rows, 2 * dk + hd * hv:2 * dk + (hd + 1) * hv] = _dot_nt(kd.astype(BF16), ds_b).astype(BF16)
                dtot = etot[:, ks] * _rowsum(ds_t * s_prev)
                ds_ref[hd] = ds_t * etot[:, ks]
                dz_ref[rows, dk + hd * hk:dk + (hd + 1) * hk] = (dkd * dec).astype(BF16)
                e_ref[rows, ks] = dkd * kd
                dla_ref[rows, ks] = jnp.broadcast_to(dtot, (CHUNK, hk))
            return carry

        def decay_cotangent(c, carry):
            rows = chunk_rows(c)
            dla_ref[rows, :] += _tri_dot(tri_strict, e_ref[rows, :])
            return carry

        dhg = jnp.zeros((1, hv), F32)
        for c in range(per):
            dhg = recompute(c, dhg)
        dhg_ref[...] += dhg
        for phase in (recurrence, decay_cotangent):
            for c in range(per):
                phase(c, 0)
        dla = jnp.where(rows_id >= PAD_ROWS, dla_ref[...], 0.0)
        dgp = dla * (1.0 / GATE_NORM) * (1.0 - _sigmoid(gp))
        dgb_ref[...] += _rowsum(dgp)
        dgp_b = dgp.astype(BF16)
        dgw_ref[...] += _dot_tn(r.astype(BF16), dgp_b)
        dz_ref[:, r_at:r_at + GATE_PAD] = _dot_nt(dgp_b, gw_ref[...]).astype(BF16)
        _project_back(dz_ref, wi_ref, h_ref, g_ref, dh_ref, dhi_ref, dg_ref, step == 0)

    d = h.shape[1]
    back = lambda i: (ntile - 1 - i, 0)
    const2 = lambda i: (0, 0)
    return pl.pallas_call(
        body, grid=(ntile,),
        in_specs=dep_specs + [
                  pl.BlockSpec((rt, z.shape[1]), back), pl.BlockSpec((rt, d), back), _resident((1, d)),
                  _resident(w_in_t.shape), pl.BlockSpec((rt, d), back), _resident(w_out.shape),
                  pl.BlockSpec((per, HEADS, hv, hk), lambda i: (ntile - 1 - i, 0, 0, 0)),
                  pl.BlockSpec((1, HEADS, hv, hk), lambda i: (jnp.maximum((ntile - 1 - i) * per - 1, 0), 0, 0, 0)),
                  _resident(gate_w.shape), _resident((1, dk)), _resident((1, hv))],
        out_specs=[pl.BlockSpec((rt, d), back), pl.BlockSpec((1, d), const2), pl.BlockSpec((rt, z.shape[1]), back),
                   pl.BlockSpec(gate_w.shape, const2), pl.BlockSpec((1, dk), const2), pl.BlockSpec((1, hv), const2)],
        out_shape=[_sds((t, d), F32), _sds((1, d), F32), _sds(z.shape, BF16), _sds(gate_w.shape, F32),
                   _sds((1, dk), F32), _sds((1, hv), F32)],
        scratch_shapes=[pltpu.VMEM((HEADS, hv, hk), F32), pltpu.VMEM((rt, dk), F32), pltpu.VMEM((rt, dk), F32),
                        pltpu.VMEM((rt, dk), F32), pltpu.VMEM((rt, dv), BF16), pltpu.VMEM((rt, dk), F32),
                        pltpu.VMEM((rt, dv), F32)],
        compiler_params=_cparams("arbitrary"), name=name)(*deps, z, h, gain, w_in_t, dh, w_out, states, states, gate_w,
                                                          gate_b, head_g)


def _head(h, gain, target, *, name):
    t, d = h.shape
    rt = _row_tile(t, 832)

    def body(h_ref, g_ref, t_ref, dh_ref, loss_ref, dg_ref):
        i = pl.program_id(0)

        @pl.when(i == 0)
        def _():
            loss_ref[...] = jnp.zeros_like(loss_ref)
            dg_ref[...] = jnp.zeros_like(dg_ref)

        hv = h_ref[...]
        rstd = lax.rsqrt(jnp.mean(hv * hv, axis=-1, keepdims=True) + EPS)
        xh = hv * rstd
        err = jnp.where(_row_ids(i, rt) >= CHUNK, xh * g_ref[...] - t_ref[...], 0.0)
        loss_ref[...] += (0.5 / d) * jnp.sum(err * err)
        dy = err * (1.0 / d)
        dg_ref[...] += _rowsum(dy * xh)
        dxh = dy * g_ref[...]
        dh_ref[...] = rstd * (dxh - xh * jnp.mean(dxh * xh, axis=-1, keepdims=True))

    return pl.pallas_call(
        body, grid=(t // rt,),
        in_specs=[pl.BlockSpec((rt, d), lambda i: (i, 0)), _resident((1, d)), pl.BlockSpec((rt, d), lambda i: (i, 0))],
        out_specs=[pl.BlockSpec((rt, d), lambda i: (i, 0)), pl.BlockSpec((8, LANE), lambda i: (0, 0)),
                   pl.BlockSpec((1, d), lambda i: (0, 0))],
        out_shape=[_sds((t, d), F32), _sds((8, LANE), F32), _sds((1, d), F32)],
        compiler_params=_cparams("arbitrary"), name=name)(h, gain, target)


def _adamw_math(w, g, m, v):
    m = ADAM_B1 * m + (1.0 - ADAM_B1) * g
    v = ADAM_B2 * v + (1.0 - ADAM_B2) * (g * g)
    m_hat = m / (1.0 - ADAM_B1 ** ADAM_STEP)
    v_hat = v / (1.0 - ADAM_B2 ** ADAM_STEP)
    return -ADAM_LR * (m_hat / (jnp.sqrt(v_hat) + ADAM_EPS) + ADAM_WD * w), m, v


N_CHIP = N_DEV // 2
BLOCK_ELEMS = 128 * 1024


def _my_slot():
    return 4 * lax.axis_index("x") + 2 * lax.axis_index("y") + lax.axis_index("c")


def _row_block(r, c):
    cap = max(8, BLOCK_ELEMS // (-(-c // LANE) * LANE))
    return max([b for b in range(8, r + 1, 8) if r % b == 0 and b <= cap] or [r])


def _blocks(r, c):
    rb = _row_block(r, c)
    if rb < r or r * c <= BLOCK_ELEMS:
        return rb, c
    return r, max([b for b in (512, 256, LANE) if c % b == 0 and r * b <= BLOCK_ELEMS] or [c])


def _reduce_adam(parts, w, m, v, *, after=None, name):
    nl, r, c = w.shape
    rb, cb = _blocks(r, c)
    dep_specs, deps = _dep_specs(after)

    def body(*refs):
        me = refs[0][0]
        refs = refs[1 + len(deps):]
        p_refs = refs[:2 * nl]
        w_ref, m_ref, v_ref, g_out, d_out, m_out, v_out = refs[2 * nl:]
        layer = pl.program_id(0)
        for li in range(nl):
            @pl.when(layer == li)
            def _(li=li):
                own_ref, land_ref = p_refs[2 * li], p_refs[2 * li + 1]
                mine = own_ref[...].astype(F32)
                g = None
                for dev in range(N_DEV):
                    term = jnp.where(me == dev, mine, land_ref[dev].astype(F32))
                    g = term if g is None else g + term
                g_out[...] = g
                d_out[...], m_out[...], v_out[...] = _adamw_math(w_ref[...], g, m_ref[...], v_ref[...])

    blk = pl.BlockSpec((None, rb, cb), lambda l, i, j, me: (l, i, j))
    p_specs = []
    for li in range(nl):
        p_specs += [
            pl.BlockSpec((None, rb, cb), lambda l, i, j, me, li=li: (me[0], jnp.where(l == li, i, 0), jnp.where(l == li, j, 0))),
            pl.BlockSpec((N_DEV, rb, cb), lambda l, i, j, me, li=li: (0, jnp.where(l == li, i, 0), jnp.where(l == li, j, 0)))]
    flat = [p for pair in parts for p in pair]
    grid_spec = pltpu.PrefetchScalarGridSpec(
        num_scalar_prefetch=1, grid=(nl, r // rb, c // cb), in_specs=dep_specs + p_specs + [blk, blk, blk],
        out_specs=[blk] * 4)
    return pl.pallas_call(
        body, grid_spec=grid_spec, out_shape=[_sds(w.shape, F32)] * 4,
        compiler_params=_cparams("arbitrary", "arbitrary", "arbitrary"), name=name)(
        _my_slot().reshape(1), *deps, *flat, w, m, v)


def _sum8(own, landed, *, name):
    def body(own_ref, land_ref, o_ref):
        me = _my_slot()
        total = None
        for dev in range(N_DEV):
            term = jnp.where(me == dev, own_ref[...], land_ref[dev])
            total = term if total is None else total + term
        o_ref[...] = total

    return pl.pallas_call(body, out_shape=_sds(own.shape, F32), name=name)(own, landed)


def _adam_small(own, landed, split, w, m, v, *, name):
    n = len(w)

    def body(*refs):
        own_refs, land_refs, w_refs, m_refs, v_refs = (refs[k * n:(k + 1) * n] for k in range(5))
        outs = refs[5 * n:]
        me = _my_slot()
        for k in range(n):
            mine = own_refs[k][me] if split[k] else own_refs[k][...]
            g = None
            for dev in range(N_DEV):
                term = jnp.where(me == dev, mine, land_refs[k][dev])
                g = term if g is None else g + term
            outs[4 * k][...] = g
            outs[4 * k + 1][...], outs[4 * k + 2][...], outs[4 * k + 3][...] = _adamw_math(
                w_refs[k][...], g, m_refs[k][...], v_refs[k][...])

    out = pl.pallas_call(body, out_shape=[_sds(a.shape, F32) for a in w for _ in range(4)],
                         compiler_params=pltpu.CompilerParams(vmem_limit_bytes=V7X_VMEM_LIMIT), name=name)(
        *own, *landed, *w, *m, *v)
    return [tuple(out[4 * k:4 * k + 4]) for k in range(n)]


_HBM = pl.BlockSpec(memory_space=pltpu.HBM)
_SEM = pl.BlockSpec(memory_space=pltpu.SEMAPHORE)
_DATAFLOW = pltpu.SideEffectType.DATAFLOW_SIDE_EFFECTING


def _plan_to_all(src, land):
    x, y, c = lax.axis_index("x"), lax.axis_index("y"), lax.axis_index("c")
    return [(src, land.at[_my_slot()], (x ^ ((d >> 2) & 1), y ^ ((d >> 1) & 1), c ^ (d & 1))) for d in range(1, N_DEV)]


def _plan_split_to_all(src, land):
    x, y, c = lax.axis_index("x"), lax.axis_index("y"), lax.axis_index("c")
    peers = [(x ^ ((d >> 2) & 1), y ^ ((d >> 1) & 1), c ^ (d & 1)) for d in range(1, N_DEV)]
    return [(src.at[4 * px + 2 * py + pc], land.at[_my_slot()], (px, py, pc)) for px, py, pc in peers]


_PLAN_COPIES = {_plan_to_all: N_DEV - 1, _plan_split_to_all: N_DEV - 1}


def _plans(plan, n):
    return list(plan) if isinstance(plan, (list, tuple)) else [plan] * n


def _exchange_copies(plan, ins, lands, send, recv):
    copies, sem = [], 0
    for p, src, land in zip(_plans(plan, len(lands)), ins, lands):
        for s, dst, dev in p(src, land):
            copies.append(pltpu.make_async_remote_copy(
                src_ref=s, dst_ref=dst, send_sem=send.at[sem], recv_sem=recv.at[sem],
                device_id=dev, device_id_type=pl.DeviceIdType.MESH))
            sem += 1
    return copies


def _place_own(a, dtype, *, after=None, name):
    r, c = a.shape
    rb = _row_block(r, c)
    dep_specs, deps = _dep_specs(after)

    def body(*refs):
        a_ref, o_ref = refs[1 + len(deps):]
        o_ref[...] = a_ref[...].astype(dtype)

    grid_spec = pltpu.PrefetchScalarGridSpec(
        num_scalar_prefetch=1, grid=(r // rb,), in_specs=dep_specs + [pl.BlockSpec((rb, c), lambda i, me: (i, 0))],
        out_specs=pl.BlockSpec((None, rb, c), lambda i, me: (me[0], i, 0)))
    return pl.pallas_call(body, grid_spec=grid_spec, out_shape=_sds((N_DEV, r, c), dtype),
                          compiler_params=_cparams("arbitrary"), name=name)(_my_slot().reshape(1), *deps, a)


def _plan_gather_first(land, _):
    x, y, c = lax.axis_index("x"), lax.axis_index("y"), lax.axis_index("c")
    mine = land.at[_my_slot()]
    return [(mine, mine, (x, y, 1 - c))] + [(mine, mine, (x ^ (d >> 1), y ^ (d & 1), c)) for d in range(1, N_CHIP)]


def _plan_gather_relay(land, _):
    x, y, c = lax.axis_index("x"), lax.axis_index("y"), lax.axis_index("c")
    slots = [land.at[4 * (x ^ (d >> 1)) + 2 * (y ^ (d & 1)) + c] for d in range(1, N_CHIP)]
    return [(s, s, (x, y, 1 - c)) for s in slots]


def _plan_gather_direct(land, _):
    return _plan_to_all(land.at[_my_slot()], land)


_PLAN_COPIES[_plan_gather_first] = N_CHIP
_PLAN_COPIES[_plan_gather_relay] = N_CHIP - 1
_PLAN_COPIES[_plan_gather_direct] = N_DEV - 1


def _exchange_start(plan, arrs, lands, *, after=None, name):
    bufs = list(lands) if arrs is None else list(arrs) + list(lands)
    n, nb = len(lands), len(bufs)
    nsem = sum(_PLAN_COPIES[p] for p in _plans(plan, n))
    dep_specs, deps = _dep_specs(after)

    def body(*refs):
        ins, land_refs = refs[:n], refs[nb - n:nb]
        send, recv = refs[nb + len(deps)], refs[nb + len(deps) + 1]
        for cp in _exchange_copies(plan, ins, land_refs, send, recv):
            cp.start()
        refs[-1][...] = jnp.zeros_like(refs[-1])

    out = pl.pallas_call(
        body, name=name,
        out_shape=(pltpu.SemaphoreType.DMA((nsem,)), pltpu.SemaphoreType.DMA((nsem,)),
                   *[pltpu.HBM(a.shape, a.dtype) for a in bufs], _sds((8, LANE), F32)),
        in_specs=[_HBM] * nb + dep_specs,
        out_specs=(_SEM, _SEM, *([_HBM] * nb), pl.BlockSpec(memory_space=pltpu.VMEM)),
        input_output_aliases={i: 2 + i for i in range(nb)},
        compiler_params=pltpu.CompilerParams(has_side_effects=_DATAFLOW),
    )(*[pltpu.with_memory_space_constraint(a, pltpu.HBM) for a in bufs], *deps)
    return (plan, n, out[0], out[1], list(out[2:2 + nb])), out[-1]


def _exchange_now(plan, lands, *, name):
    n = len(lands)
    nsem = sum(_PLAN_COPIES[p] for p in _plans(plan, n))

    def body(*refs):
        land_refs, send, recv = refs[n:2 * n], refs[2 * n], refs[2 * n + 1]
        copies = _exchange_copies(plan, land_refs, land_refs, send, recv)
        for cp in copies:
            cp.start()
        for cp in copies:
            cp.wait_send()
            cp.wait_recv()

    hbm = pl.BlockSpec(memory_space=pl.ANY)
    return pl.pallas_call(
        body, in_specs=[hbm] * n, out_specs=[hbm] * n, out_shape=[_sds(a.shape, a.dtype) for a in lands],
        input_output_aliases={i: i for i in range(n)},
        scratch_shapes=[pltpu.SemaphoreType.DMA((nsem,)), pltpu.SemaphoreType.DMA((nsem,))], name=name)(*lands)


def _exchange_wait(state, after, *, name):
    plan, n, send_sem, recv_sem, bufs = state
    nb = len(bufs)
    after = list(after) if isinstance(after, (list, tuple)) else [after]

    def body(*refs):
        ins, land_refs, send, recv = refs[:n], refs[nb - n:nb], refs[nb], refs[nb + 1]
        for cp in _exchange_copies(plan, ins, land_refs, send, recv):
            cp.wait_send()
            cp.wait_recv()

    out = pl.pallas_call(
        body, name=name, out_shape=[pltpu.HBM(a.shape, a.dtype) for a in bufs],
        in_specs=[_HBM] * nb + [_SEM, _SEM] + [pl.BlockSpec(memory_space=pl.ANY)] * len(after), out_specs=[_HBM] * nb,
        input_output_aliases={i: i for i in range(nb)},
        compiler_params=pltpu.CompilerParams(has_side_effects=_DATAFLOW),
    )(*bufs, send_sem, recv_sem, *after)
    return list(out[:n]), list(out[nb - n:])


def _dep_specs(after):
    return ([], []) if after is None else ([pl.BlockSpec(memory_space=pl.ANY)], [after])


def _undo_column_split(g):
    return jnp.transpose(g, (1, 0, 2)).reshape(g.shape[1], N_DEV * g.shape[2])


def _column_split(a):
    r, c = a.shape
    return jnp.transpose(a.reshape(r, N_DEV, c // N_DEV), (1, 0, 2))


class _WholeWeights:
    def __init__(self, groups):
        self.groups = groups
        self.grads = {}

    def fetch(self, group, after):
        return self.groups[group]

    def emit(self, group, grads):
        self.grads.update(grads)
        return None


def _local_step(x, target, replicated, src):
    d = x.shape[1]
    mix_g, ffn_g = replicated["mix_g"], replicated["ffn_g"]
    h0 = jnp.concatenate([jnp.zeros((CHUNK, d), F32), x], axis=0)
    tgt = jnp.concatenate([jnp.zeros((CHUNK, d), F32), target], axis=0)
    cp = src.fetch("cp", [h0, tgt])
    h0 = lax.dynamic_update_slice(h0, cp["meta"], (PAD_ROWS, 0))
    cp_mid = (cp["conv_w"], replicated["conv_b"], replicated["ln_g"], replicated["ln_b"], replicated["pool_w"],
              replicated["pool_scale"])

    h1, cat, z0, u0, cv0 = _cp_mid_fwd(h0, mix_g[0:1], cp["cp_w_in_t"], cp["cp_w_out"], *cp_mid, name="cp_mixer")
    ffn0 = src.fetch("ffn0", h1)
    h2, uf0, rf0 = _ffn_fwd(h1, ffn_g[0:1], ffn0["w1"], ffn0["w2"], name="ffn0")
    gla = src.fetch("gla", h2)
    gla_mid = (gla["gate_w"], gla["gate_b"], gla["head_g"])
    h3, og, states, z1, u1 = _gla_mid_fwd(h2, mix_g[1:2], gla["gla_w_in_t"], gla["gla_w_out"], *gla_mid, name="gla_mixer")
    ffn1 = src.fetch("ffn1", h3)
    h4, uf1, rf1 = _ffn_fwd(h3, ffn_g[1:2], ffn1["w1"], ffn1["w2"], name="ffn1")
    dh4, loss, d_final_g = _head(h4, replicated["final_g"], tgt, name="head")

    dh3, dhh1, dob1, dffn_g1 = _ffn_bwd_x(h3, dh4, ffn_g[1:2], rf1, ffn1["w1"], ffn1["w2"], name="ffn1_bwd_x")
    dw1_1, dw2_1 = _ffn_bwd_w(uf1, dhh1, rf1, dob1, name="ffn1_bwd_w")
    sent = src.emit("ffn1", dict(w1=dw1_1, w2=dw2_1))
    d_gla_w_out = _linear_bwd_w(og, dh3, name="gla_out_dw")
    dh2, dmix_g1, dz1, d_gate_w, d_gate_b, d_head_g = _gla_mid_bwd(
        z1, h2, mix_g[1:2], gla["gla_w_in_t"], dh3, gla["gla_w_out"], states, *gla_mid, after=sent, name="gla_mixer_bwd")
    d_gla_w_in_t = _linear_bwd_w(dz1, u1, name="gla_in_dw")
    sent = src.emit("gla", dict(gla_w_in_t=d_gla_w_in_t, gla_w_out=d_gla_w_out))
    dh1, dhh0, dob0, dffn_g0 = _ffn_bwd_x(h1, dh2, ffn_g[0:1], rf0, ffn0["w1"], ffn0["w2"], after=sent, name="ffn0_bwd_x")
    dw1_0, dw2_0 = _ffn_bwd_w(uf0, dhh0, rf0, dob0, name="ffn0_bwd_w")
    src.emit("ffn0", dict(w1=dw1_0, w2=dw2_0))
    d_cp_w_out = _linear_bwd_w(cat, dh1, name="cp_out_dw")
    sent = src.emit("cp_out", dict(cp_w_out=d_cp_w_out))
    dh0, dmix_g0, dz0, d_conv_w, d_conv_b, d_ln_g, d_ln_b, d_pool_w, d_pool_scale = _cp_mid_bwd(
        z0, cv0, h0, mix_g[0:1], cp["cp_w_in_t"], dh1, cp["cp_w_out"], *cp_mid, after=sent, name="cp_mixer_bwd")
    d_cp_w_in_t = _linear_bwd_w(dz0, u0, name="cp_in_dw")

    small = dict(
        mix_g=jnp.concatenate([dmix_g0, dmix_g1]), ffn_g=jnp.concatenate([dffn_g0, dffn_g1]), conv_b=d_conv_b, ln_g=d_ln_g,
        ln_b=d_ln_b, pool_w=d_pool_w, pool_scale=d_pool_scale, final_g=d_final_g, meta=dh0[PAD_ROWS:CHUNK], conv_w=d_conv_w,
        gate_w=d_gate_w, gate_b=d_gate_b, head_g=d_head_g)
    src.emit("cp", dict(cp_w_in_t=d_cp_w_in_t, small=small, loss=loss))
    return loss, dh0[CHUNK:], small


_REPLICATED = ("mix_norm_g", "ffn_norm_g", "cp_conv_b", "cp_ln_g", "cp_ln_b", "cp_pool_w", "cp_pool_scale", "final_norm_g")
_SMALL_SHARDED = ("meta_tokens", "cp_conv_w", "gla_gate_w2", "gla_gate_b", "gla_head_g")
_NAMES = ("meta_tokens", "mix_norm_g", "ffn_norm_g", "ffn_w1", "ffn_w2", "cp_w_in", "cp_conv_w", "cp_conv_b", "cp_ln_g",
          "cp_ln_b", "cp_pool_w", "cp_pool_scale", "cp_w_out", "gla_w_in", "gla_gate_w2", "gla_gate_b", "gla_head_g",
          "gla_w_out", "final_norm_g")
_SMALL_GRADS = ("mix_g", "ffn_g", "conv_b", "ln_g", "ln_b", "pool_w", "pool_scale", "final_g", "meta", "conv_w", "gate_w",
                "gate_b", "head_g")
_GROUPS = ("cp", "ffn0", "gla", "ffn1")
_TWO_LEG_GATHERS = ("cp", "ffn0", "ffn1")


class _Exchanges:
    def __init__(self, w, d):
        self.d = d
        small = [w[n].reshape(w[n].shape[-2:]) for n in _SMALL_SHARDED]
        self.small_shard_shapes = [w[n].shape for n in _SMALL_SHARDED]
        shards = dict(
            cp=[(w["cp_w_in"][0].T, BF16), (w["cp_w_out"][0], BF16)] + [(a, F32) for a in small],
            ffn0=[(w["ffn_w1"][0], BF16), (w["ffn_w2"][0], BF16)],
            gla=[(w["gla_w_in"][0].T, BF16), (w["gla_w_out"][0], BF16)],
            ffn1=[(w["ffn_w1"][1], BF16), (w["ffn_w2"][1], BF16)])
        self.gathers = {}
        self.sent = {}
        token = None
        for group in _GROUPS:
            lands = [_place_own(a, dtype, after=token, name=f"place_w_{group}_{k}")
                     for k, (a, dtype) in enumerate(shards[group])]
            plan = _plan_gather_first if group in _TWO_LEG_GATHERS else _plan_gather_direct
            self.gathers[group], token = _exchange_start(plan, None, lands, after=token, name=f"start_w_{group}")
        self.token = token

    def fetch(self, group, after):
        d = self.d
        after = (list(after) if isinstance(after, (list, tuple)) else [after]) + [self.token]
        _, got = _exchange_wait(self.gathers[group], after, name=f"wait_w_{group}")
        if group in _TWO_LEG_GATHERS:
            got = _exchange_now(_plan_gather_relay, got, name=f"relay_w_{group}")
        if group in ("ffn0", "ffn1"):
            return dict(w1=got[0], w2=got[1])
        if group == "gla":
            w_in_t = jnp.pad(got[0].reshape(-1, d), ((0, GATE_PAD - GATE_RANK), (0, 0)))
            return dict(gla_w_in_t=w_in_t, gla_w_out=got[1].reshape(d, d), gate_w=self.gate_w, gate_b=self.gate_b,
                        head_g=self.head_g)
        meta, conv_w, gate_w, self.gate_b, self.head_g = [_undo_column_split(a) for a in got[2:]]
        self.gate_w = jnp.pad(gate_w, ((0, GATE_PAD - GATE_RANK), (0, 0))).astype(BF16)
        return dict(cp_w_in_t=got[0].reshape(-1, d), cp_w_out=got[1].reshape(d, d), meta=meta,
                    conv_w=jnp.pad(conv_w, ((0, 1), (0, 0))))

    def emit(self, group, g):
        d = self.d
        if group in ("ffn0", "ffn1"):
            arrs = [g["w1"], g["w2"]]
        elif group == "gla":
            w_in_t = g["gla_w_in_t"][:3 * d + GATE_RANK]
            arrs = [w_in_t.reshape(N_DEV, -1, d), g["gla_w_out"].reshape(N_DEV, d // N_DEV, d)]
        elif group == "cp_out":
            arrs = [g["cp_w_out"].reshape(N_DEV, d // N_DEV, d)]
        else:
            s = dict(g["small"])
            s.update(pool_w=s["pool_w"][None], conv_w=s["conv_w"][:CONV_WIDTH], gate_w=s["gate_w"][:GATE_RANK])
            own = [s[n] for n in _SMALL_GRADS[:len(_REPLICATED)]]
            own += [_column_split(s[n]).reshape((N_DEV,) + shape)
                    for n, shape in zip(_SMALL_GRADS[len(_REPLICATED):], self.small_shard_shapes)]
            plans = [_plan_to_all] * len(_REPLICATED) + [_plan_split_to_all] * len(_SMALL_SHARDED)
            lands = [lax.empty((N_DEV,) + a.shape, F32) for a in own[:len(_REPLICATED)]]
            lands += [lax.empty(a.shape, F32) for a in own[len(_REPLICATED):]]
            own.append(g["loss"])
            plans.append(_plan_to_all)
            lands.append(lax.empty((N_DEV,) + g["loss"].shape, F32))
            self.small_sent, self.token = _exchange_start(plans, own, lands, after=self.token, name="start_g_small")
            arrs = [g["cp_w_in_t"].reshape(N_DEV, -1, d)]
        self.sent[group], self.token = _exchange_start(_plan_split_to_all, arrs, [lax.empty(a.shape, a.dtype) for a in arrs],
                                                       after=self.token, name=f"start_g_{group}")
        return self.token

    def finish(self, w, mom, var):
        out = {}
        after = self.token

        def landed(group):
            own, got = _exchange_wait(self.sent[group], after, name=f"wait_g_{group}")
            return list(zip(own, got))

        def adam(n, parts, behind=None, transposed=False):
            flip = (lambda a: jnp.transpose(a, (0, 2, 1))) if transposed else (lambda a: a)
            res = _reduce_adam(parts, flip(w[n]), flip(mom[n]), flip(var[n]), after=behind, name=f"adam_{n}")
            out[n] = tuple(flip(a) for a in res)
            return res[0]

        ffn1 = landed("ffn1")
        after = ffn1[0][1]
        gla = landed("gla")
        after = gla[0][1]
        ffn0 = landed("ffn0")
        after = adam("ffn_w1", [ffn0[0], ffn1[0]])
        after = adam("ffn_w2", [ffn0[1], ffn1[1]], after)
        after = adam("gla_w_in", [gla[0]], after, transposed=True)
        after = adam("gla_w_out", [gla[1]], after)
        small_own, small_landed = _exchange_wait(self.small_sent, after, name="wait_g_small")
        names = _REPLICATED + _SMALL_SHARDED
        split = [False] * len(_REPLICATED) + [True] * len(_SMALL_SHARDED)
        small_new = _adam_small(small_own[:-1], small_landed[:-1], split, [w[n] for n in names], [mom[n] for n in names],
                                [var[n] for n in names], name="adam_small")
        out.update(zip(names, small_new))
        out["loss"] = _sum8(small_own[-1], small_landed[-1], name="sum_loss")[0, 0]

        after = small_new[0][0]
        cp_out = landed("cp_out")
        after = adam("cp_w_out", [cp_out[0]])
        cp = landed("cp")
        adam("cp_w_in", [cp[0]], transposed=True)
        return out


def kernel(x, meta_tokens, mix_norm_g, ffn_norm_g, ffn_w1, ffn_w2, cp_w_in, cp_conv_w, cp_conv_b, cp_ln_g, cp_ln_b, cp_pool_w, cp_pool_scale, cp_w_out, gla_w_in, gla_gate_w2, gla_gate_b, gla_head_g, gla_w_out, final_norm_g, loss_target, m_meta_tokens, m_mix_norm_g, m_ffn_norm_g, m_ffn_w1, m_ffn_w2, m_cp_w_in, m_cp_conv_w, m_cp_conv_b, m_cp_ln_g, m_cp_ln_b, m_cp_pool_w, m_cp_pool_scale, m_cp_w_out, m_gla_w_in, m_gla_gate_w2, m_gla_gate_b, m_gla_head_g, m_gla_w_out, m_final_norm_g, v_meta_tokens, v_mix_norm_g, v_ffn_norm_g, v_ffn_w1, v_ffn_w2, v_cp_w_in, v_cp_conv_w, v_cp_conv_b, v_cp_ln_g, v_cp_ln_b, v_cp_pool_w, v_cp_pool_scale, v_cp_w_out, v_gla_w_in, v_gla_gate_w2, v_gla_gate_b, v_gla_head_g, v_gla_w_out, v_final_norm_g):
    w = dict(meta_tokens=meta_tokens, mix_norm_g=mix_norm_g, ffn_norm_g=ffn_norm_g, ffn_w1=ffn_w1, ffn_w2=ffn_w2,
             cp_w_in=cp_w_in, cp_conv_w=cp_conv_w, cp_conv_b=cp_conv_b, cp_ln_g=cp_ln_g, cp_ln_b=cp_ln_b,
             cp_pool_w=cp_pool_w, cp_pool_scale=cp_pool_scale, cp_w_out=cp_w_out, gla_w_in=gla_w_in,
             gla_gate_w2=gla_gate_w2, gla_gate_b=gla_gate_b, gla_head_g=gla_head_g, gla_w_out=gla_w_out,
             final_norm_g=final_norm_g.reshape(1, -1))
    mom = dict(meta_tokens=m_meta_tokens, mix_norm_g=m_mix_norm_g, ffn_norm_g=m_ffn_norm_g, ffn_w1=m_ffn_w1, ffn_w2=m_ffn_w2,
               cp_w_in=m_cp_w_in, cp_conv_w=m_cp_conv_w, cp_conv_b=m_cp_conv_b, cp_ln_g=m_cp_ln_g, cp_ln_b=m_cp_ln_b,
               cp_pool_w=m_cp_pool_w, cp_pool_scale=m_cp_pool_scale, cp_w_out=m_cp_w_out, gla_w_in=m_gla_w_in,
               gla_gate_w2=m_gla_gate_w2, gla_gate_b=m_gla_gate_b, gla_head_g=m_gla_head_g, gla_w_out=m_gla_w_out,
               final_norm_g=m_final_norm_g.reshape(1, -1))
    var = dict(meta_tokens=v_meta_tokens, mix_norm_g=v_mix_norm_g, ffn_norm_g=v_ffn_norm_g, ffn_w1=v_ffn_w1, ffn_w2=v_ffn_w2,
               cp_w_in=v_cp_w_in, cp_conv_w=v_cp_conv_w, cp_conv_b=v_cp_conv_b, cp_ln_g=v_cp_ln_g, cp_ln_b=v_cp_ln_b,
               cp_pool_w=v_cp_pool_w, cp_pool_scale=v_cp_pool_scale, cp_w_out=v_cp_w_out, gla_w_in=v_gla_w_in,
               gla_gate_w2=v_gla_gate_w2, gla_gate_b=v_gla_gate_b, gla_head_g=v_gla_head_g, gla_w_out=v_gla_w_out,
               final_norm_g=v_final_norm_g.reshape(1, -1))
    d = x.shape[-1]
    replicated = dict(mix_g=w["mix_norm_g"], ffn_g=w["ffn_norm_g"], conv_b=w["cp_conv_b"], ln_g=w["cp_ln_g"],
                      ln_b=w["cp_ln_b"], pool_w=w["cp_pool_w"][0].astype(BF16), pool_scale=w["cp_pool_scale"],
                      final_g=w["final_norm_g"])
    exchanges = _Exchanges(w, d)
    _, grad_x, _ = _local_step(x[0], loss_target[0], replicated, exchanges)
    out = exchanges.finish(w, mom, var)
    loss = out.pop("loss")

    def leaf(n, k):
        a = out[n][k]
        return a.reshape(-1) if n == "final_norm_g" else a

    return (loss, grad_x[None], *[leaf(n, 0) for n in _NAMES], *[leaf(n, 1) for n in _NAMES],
            *[leaf(n, 2) for n in _NAMES], *[leaf(n, 3) for n in _NAMES])
```

```python
import functools

import jax
import jax.numpy as jnp
from jax import lax
from jax.experimental import pallas as pl
from jax.experimental.pallas import tpu as pltpu

F32, BF16 = jnp.float32, jnp.bfloat16
N_DEV = 8
CHUNK = 64
N_META = 16
PAD_ROWS = CHUNK - N_META
HALO = 32
EPS = 1e-5
CONV_WIDTH = 31
POOL_WINDOWS = (2, 4, 8, 16)
HEADS = 4
GATE_RANK = 16
GATE_NORM = 16.0
GATE_PAD = 128
ADAM_LR, ADAM_B1, ADAM_B2, ADAM_EPS, ADAM_WD, ADAM_STEP = 0.001, 0.9, 0.999, 1e-08, 0.01, 10
V7X_VMEM_LIMIT = 56 * 2 ** 20
LANE = 128


def _cparams(*sem):
    return pltpu.CompilerParams(dimension_semantics=sem, vmem_limit_bytes=V7X_VMEM_LIMIT)


def _row_tile(t, cap):
    best = CHUNK
    for r in range(CHUNK, min(t, cap) + 1, CHUNK):
        if t % r == 0:
            best = r
    return best


def _resident(shape):
    return pl.BlockSpec(shape, lambda *_: (0,) * len(shape), pipeline_mode=pl.Buffered(1))


def _dot(a, b):
    return jnp.dot(a, b, preferred_element_type=F32)


def _dot_nt(a, b):
    return lax.dot_general(a, b, (((1,), (1,)), ((), ())), preferred_element_type=F32)


def _dot_tn(a, b):
    return lax.dot_general(a, b, (((0,), (0,)), ((), ())), preferred_element_type=F32)


def _rowsum(a):
    return jnp.sum(a, axis=0, keepdims=True)


def _sigmoid(a):
    return 1.0 / (1.0 + jnp.exp(-a))


def _row_ids(tile, rt):
    return tile * rt + lax.broadcasted_iota(jnp.int32, (rt, 1), 0)


def _sds(shape, dtype):
    return jax.ShapeDtypeStruct(shape, dtype)


DW_ROWS = 1024


def _linear_bwd_w(x, dy, *, name):
    t, k = x.shape
    n = dy.shape[1]
    cut_k = k > n
    width = k if cut_k else n
    blk = max(c for c in (640, 512, 384, 256, LANE) if width % c == 0)

    def body(x_ref, dy_ref, o_ref, acc):
        for c0 in range(0, t, DW_ROWS):
            rows = slice(c0, min(c0 + DW_ROWS, t))
            part = _dot_tn(x_ref[rows, :].astype(BF16), dy_ref[rows, :].astype(BF16))
            if c0 == 0:
                acc[...] = part
            else:
                acc[...] += part
        o_ref[...] = acc[...].astype(BF16)

    if cut_k:
        in_specs = [pl.BlockSpec((t, blk), lambda j: (0, j)), _resident((t, n))]
        out_specs = pl.BlockSpec((blk, n), lambda j: (j, 0))
        acc_shape = (blk, n)
    else:
        in_specs = [_resident((t, k)), pl.BlockSpec((t, blk), lambda j: (0, j))]
        out_specs = pl.BlockSpec((k, blk), lambda j: (0, j))
        acc_shape = (k, blk)
    return pl.pallas_call(
        body, grid=(width // blk,), in_specs=in_specs, out_specs=out_specs, out_shape=_sds((k, n), BF16),
        scratch_shapes=[pltpu.VMEM(acc_shape, F32)], compiler_params=_cparams("parallel"), name=name)(x, dy)


FFN_BLOCKS_PER_STEP = 2


def _ffn_fwd(h, gain, w1g, w2g, *, name):
    t, d = h.shape
    f8 = w1g.shape[-1]
    rt = _row_tile(t, 832)
    nstep = N_DEV // FFN_BLOCKS_PER_STEP

    def body(h_ref, g_ref, w1_ref, w2_ref, o_ref, u_ref, r_ref, acc_ref):
        j = pl.program_id(1)

        @pl.when(j == 0)
        def _():
            hv = h_ref[...]
            u_ref[...] = (hv * lax.rsqrt(jnp.mean(hv * hv, axis=-1, keepdims=True) + EPS) * g_ref[...]).astype(BF16)
            acc_ref[...] = jnp.zeros_like(acc_ref)

        part = None
        for b in range(FFN_BLOCKS_PER_STEP):
            a = jnp.maximum(_dot(u_ref[...], w1_ref[b]), 0.0)
            r_ref[:, b * f8:(b + 1) * f8] = a.astype(BF16)
            term = _dot((a * a).astype(BF16), w2_ref[b])
            part = term if part is None else part + term
        acc_ref[...] += part

        @pl.when(j == nstep - 1)
        def _():
            o_ref[...] = h_ref[...] + acc_ref[...]

    nb = FFN_BLOCKS_PER_STEP
    return pl.pallas_call(
        body, grid=(t // rt, nstep),
        in_specs=[pl.BlockSpec((rt, d), lambda i, j: (i, 0)), _resident((1, d)),
                  pl.BlockSpec((nb, d, f8), lambda i, j: (j, 0, 0)),
                  pl.BlockSpec((nb, f8, d), lambda i, j: (j, 0, 0))],
        out_specs=[pl.BlockSpec((rt, d), lambda i, j: (i, 0)), pl.BlockSpec((rt, d), lambda i, j: (i, 0)),
                   pl.BlockSpec((rt, nb * f8), lambda i, j: (i, j))],
        out_shape=[_sds((t, d), F32), _sds((t, d), BF16), _sds((t, N_DEV * f8), BF16)],
        scratch_shapes=[pltpu.VMEM((rt, d), F32)],
        compiler_params=_cparams("parallel", "arbitrary"), name=name)(h, gain, w1g, w2g)


def _ffn_bwd_x(h, dout, gain, r, w1g, w2g, *, after=None, name):
    t, d = h.shape
    f8 = w1g.shape[-1]
    rt = _row_tile(t, 832)
    nb = FFN_BLOCKS_PER_STEP
    last = N_DEV // nb - 1
    dep_specs, deps = _dep_specs(after)

    def body(*refs):
        h_ref, do_ref, g_ref, r_ref, w1_ref, w2_ref, dh_ref, dhh_ref, dob_ref, dg_ref, du_ref = refs[len(deps):]
        i, j = pl.program_id(0), pl.program_id(1)

        @pl.when(j == 0)
        def _():
            dob_ref[...] = do_ref[...].astype(BF16)
            du_ref[...] = jnp.zeros_like(du_ref)

        part = None
        for b in range(nb):
            cols = slice(b * f8, (b + 1) * f8)
            dhh = (_dot_nt(dob_ref[...], w2_ref[b]) * (2.0 * r_ref[:, cols].astype(F32))).astype(BF16)
            dhh_ref[:, cols] = dhh
            term = _dot_nt(dhh, w1_ref[b])
            part = term if part is None else part + term
        du_ref[...] += part

        @pl.when(j == last)
        def _():
            @pl.when(i == 0)
            def _():
                dg_ref[...] = jnp.zeros_like(dg_ref)

            hv = h_ref[...]
            rstd = lax.rsqrt(jnp.mean(hv * hv, axis=-1, keepdims=True) + EPS)
            xh = hv * rstd
            du = du_ref[...]
            dg_ref[...] += _rowsum(du * xh)
            dxh = du * g_ref[...]
            dh_ref[...] = do_ref[...] + rstd * (dxh - xh * jnp.mean(dxh * xh, axis=-1, keepdims=True))

    rows = lambda i, j: (i, 0)
    return pl.pallas_call(
        body, grid=(t // rt, N_DEV // nb),
        in_specs=dep_specs + [
                  pl.BlockSpec((rt, d), rows), pl.BlockSpec((rt, d), rows), _resident((1, d)),
                  pl.BlockSpec((rt, nb * f8), lambda i, j: (i, j)),
                  pl.BlockSpec((nb, d, f8), lambda i, j: (j, 0, 0)),
                  pl.BlockSpec((nb, f8, d), lambda i, j: (j, 0, 0))],
        out_specs=[pl.BlockSpec((rt, d), rows), pl.BlockSpec((rt, nb * f8), lambda i, j: (i, j)),
                   pl.BlockSpec((rt, d), rows), pl.BlockSpec((1, d), lambda i, j: (0, 0))],
        out_shape=[_sds((t, d), F32), _sds((t, N_DEV * f8), BF16), _sds((t, d), BF16), _sds((1, d), F32)],
        scratch_shapes=[pltpu.VMEM((rt, d), F32)],
        compiler_params=_cparams("arbitrary", "arbitrary"), name=name)(*deps, h, dout, gain, r, w1g, w2g)


def _ffn_bwd_w(u, dhh, r, dout_b, *, name):
    t, d = u.shape
    f8 = dhh.shape[1] // N_DEV

    def body(u_ref, dhh_ref, r_ref, dob_ref, dw1_ref, dw2_ref, acc1, acc2):
        for c0 in range(0, t, DW_ROWS):
            rows = slice(c0, min(c0 + DW_ROWS, t))
            rr = r_ref[rows, :].astype(F32)
            part1 = _dot_tn(u_ref[rows, :], dhh_ref[rows, :])
            part2 = _dot_tn((rr * rr).astype(BF16), dob_ref[rows, :])
            if c0 == 0:
                acc1[...] = part1
                acc2[...] = part2
            else:
                acc1[...] += part1
                acc2[...] += part2
        dw1_ref[...] = acc1[...].astype(BF16)
        dw2_ref[...] = acc2[...].astype(BF16)

    return pl.pallas_call(
        body, grid=(N_DEV,),
        in_specs=[_resident((t, d)), pl.BlockSpec((t, f8), lambda j: (0, j)), pl.BlockSpec((t, f8), lambda j: (0, j)),
                  _resident((t, d))],
        out_specs=[pl.BlockSpec((None, d, f8), lambda j: (j, 0, 0)), pl.BlockSpec((None, f8, d), lambda j: (j, 0, 0))],
        out_shape=[_sds((N_DEV, d, f8), BF16), _sds((N_DEV, f8, d), BF16)],
        scratch_shapes=[pltpu.VMEM((d, f8), F32), pltpu.VMEM((f8, d), F32)],
        compiler_params=_cparams("parallel"), name=name)(u, dhh, r, dout_b)


def _lane_blocks(width):
    lb = min(LANE, width)
    return [slice(s, s + lb) for s in range(0, width, lb)]


def _conv_rows(src_ref, w_ref, offset, dst_ref, nblk, width, bias_ref=None):
    def blk(rb, carry):
        base = pl.multiple_of(rb * CHUNK, CHUNK)
        for l, ls in enumerate(_lane_blocks(width)):
            acc = jnp.zeros((CHUNK, ls.stop - ls.start), F32)
            if bias_ref is not None:
                acc = acc + bias_ref[:, ls]
            for k in range(CONV_WIDTH):
                acc = acc + w_ref[k:k + 1, ls] * src_ref[l, pl.ds(base + offset(k), CHUNK), :]
            dst_ref[l, pl.ds(base, CHUNK), :] = acc
        return carry

    lax.fori_loop(0, nblk, blk, 0)


def _to_lane_blocks(ref, row0, value):
    for l, ls in enumerate(_lane_blocks(value.shape[1])):
        ref[l, row0:row0 + value.shape[0], :] = value[:, ls]


def _from_lane_blocks(ref):
    return jnp.concatenate([ref[l] for l in range(ref.shape[0])], axis=1)


def _pool_counts(rows, window):
    return jnp.clip(rows - PAD_ROWS + 1, 1, window).astype(F32)


def _trailing_sum(v, window):
    s, sh = v, 1
    while sh < window:
        s = s + pltpu.roll(s, sh, 0)
        sh *= 2
    return s


def _leading_sum(v, window):
    s, sh, n = v, 1, v.shape[0]
    while sh < window:
        s = s + pltpu.roll(s, n - sh, 0)
        sh *= 2
    return s


def _norm_project(h_ref, g_ref, w_t_ref, u_ref, z_ref):
    hv = h_ref[...]
    u = (hv * lax.rsqrt(jnp.mean(hv * hv, axis=-1, keepdims=True) + EPS) * g_ref[...]).astype(BF16)
    u_ref[...] = u
    z_ref[...] = _dot_nt(u, w_t_ref[...])


def _project_back(dz_ref, w_t_ref, h_ref, g_ref, dres_ref, dh_ref, dg_ref, first):
    dx = _dot(dz_ref[...], w_t_ref[...])
    hv = h_ref[...]
    rstd = lax.rsqrt(jnp.mean(hv * hv, axis=-1, keepdims=True) + EPS)
    xh = hv * rstd

    @pl.when(first)
    def _():
        dg_ref[...] = jnp.zeros_like(dg_ref)

    dg_ref[...] += _rowsum(dx * xh)
    dxh = dx * g_ref[...]
    dh_ref[...] = dres_ref[...] + rstd * (dxh - xh * jnp.mean(dxh * xh, axis=-1, keepdims=True))


def _cp_mid_fwd(h, gain, w_in_t, w_out, conv_w, conv_b, ln_g, ln_b, pool_w, pool_scale, *, name):
    t, d = h.shape
    ein = w_in_t.shape[0]
    cd = conv_b.shape[1]
    pd = pool_scale.shape[1]
    pg = pd // len(POOL_WINDOWS)
    rt = _row_tile(t, 320)

    def body(h_ref, g_ref, wi_ref, wo_ref, cw_ref, cb_ref, lg_ref, lb_ref, pw_ref, ps_ref,
             ho_ref, o_ref, z_ref, u_ref, cv_ref, gext, pext, conv_s):
        i = pl.program_id(0)

        @pl.when(i == 0)
        def _():
            _to_lane_blocks(gext, 0, jnp.zeros((HALO, cd), F32))
            pext[0:HALO, :] = jnp.zeros((HALO, pd), F32)

        _norm_project(h_ref, g_ref, wi_ref, u_ref, z_ref)

        _to_lane_blocks(gext, HALO, z_ref[:, 0:cd] * _sigmoid(z_ref[:, cd:2 * cd]))
        pext[HALO:HALO + rt, :] = z_ref[:, 2 * cd:]
        _conv_rows(gext, cw_ref, lambda k: k + HALO - (CONV_WIDTH - 1), conv_s, rt // CHUNK, cd, cb_ref)
        cv = _from_lane_blocks(conv_s)
        cv_ref[...] = cv
        xc = cv - jnp.mean(cv, axis=-1, keepdims=True)
        y = xc * lax.rsqrt(jnp.mean(xc * xc, axis=-1, keepdims=True) + EPS) * lg_ref[...] + lb_ref[...]
        rows = _row_ids(i, rt)
        a = jnp.where(rows >= PAD_ROWS, y * _sigmoid(y), 0.0)
        o_ref[:, 0:cd] = a.astype(BF16)
        for gi, window in enumerate(POOL_WINDOWS):
            ls = slice(gi * pg, (gi + 1) * pg)
            v = pext[:, ls]
            tm = _trailing_sum(v, window)[HALO:] / _pool_counts(rows, window) - v[HALO:]
            p = _dot(tm.astype(BF16), pw_ref[gi]) * ps_ref[:, ls]
            o_ref[:, cd + gi * pg:cd + (gi + 1) * pg] = p.astype(BF16)
        ho_ref[...] = h_ref[...] + _dot(o_ref[...], wo_ref[...])
        gext[:, 0:HALO, :] = gext[:, rt:rt + HALO, :]
        pext[0:HALO, :] = pext[rt:rt + HALO, :]

    nl, lb = len(_lane_blocks(cd)), min(LANE, cd)
    rows = lambda i: (i, 0)
    return pl.pallas_call(
        body, grid=(t // rt,),
        in_specs=[pl.BlockSpec((rt, d), rows), _resident((1, d)), _resident(w_in_t.shape), _resident(w_out.shape),
                  _resident(conv_w.shape), _resident((1, cd)),
                  _resident((1, cd)), _resident((1, cd)), _resident(pool_w.shape), _resident((1, pd))],
        out_specs=[pl.BlockSpec((rt, d), rows), pl.BlockSpec((rt, cd + pd), rows), pl.BlockSpec((rt, ein), rows),
                   pl.BlockSpec((rt, d), rows), pl.BlockSpec((rt, cd), rows)],
        out_shape=[_sds((t, d), F32), _sds((t, cd + pd), BF16), _sds((t, ein), F32), _sds((t, d), BF16),
                   _sds((t, cd), F32)],
        scratch_shapes=[pltpu.VMEM((nl, rt + HALO, lb), F32), pltpu.VMEM((rt + HALO, pd), F32),
                        pltpu.VMEM((nl, rt, lb), F32)],
        compiler_params=_cparams("arbitrary"), name=name)(h, gain, w_in_t, w_out, conv_w, conv_b, ln_g, ln_b, pool_w,
                                                          pool_scale)


def _cp_mid_bwd(z, cv, h, gain, w_in_t, dh, w_out, conv_w, conv_b, ln_g, ln_b, pool_w, pool_scale, *, after=None, name):
    t, ein = z.shape
    cd = conv_b.shape[1]
    pd = pool_scale.shape[1]
    pg = pd // len(POOL_WINDOWS)
    rt = _row_tile(t, 320)
    ntile = t // rt
    per = rt // CHUNK
    dep_specs, deps = _dep_specs(after)

    def body(*refs):
        (z_ref, zh_ref, cv_ref, h_ref, g_ref, wi_ref, dh_ref, wo_ref, cw_ref, cb_ref, lg_ref, lb_ref, pw_ref, ps_ref,
         dhi_ref, dg_ref, dz_ref, dcw_ref, dcb_ref, dlg_ref, dlb_ref, dpw_ref, dps_ref,
         gext, pext, conv_s, dcv, dsp) = refs[len(deps):]
        step = pl.program_id(0)
        tile = ntile - 1 - step
        dcat = _dot_nt(dh_ref[...].astype(BF16), wo_ref[...])

        @pl.when(step == 0)
        def _():
            for ref in (dcw_ref, dcb_ref, dlg_ref, dlb_ref, dpw_ref, dps_ref):
                ref[...] = jnp.zeros_like(ref)
            _to_lane_blocks(dcv, rt, jnp.zeros((HALO, cd), F32))
            dsp[rt:rt + HALO, :] = jnp.zeros((HALO, pd), F32)

        keep = jnp.where(tile > 0, 1.0, 0.0)
        zh = zh_ref[CHUNK - HALO:CHUNK, :]
        _to_lane_blocks(gext, 0, keep * zh[:, 0:cd] * _sigmoid(zh[:, cd:2 * cd]))
        pext[0:HALO, :] = keep * zh[:, 2 * cd:]
        za = z_ref[:, 0:cd]
        sg = _sigmoid(z_ref[:, cd:2 * cd])
        _to_lane_blocks(gext, HALO, za * sg)
        pext[HALO:HALO + rt, :] = z_ref[:, 2 * cd:]
        cv = cv_ref[...]
        xc = cv - jnp.mean(cv, axis=-1, keepdims=True)
        rstd = lax.rsqrt(jnp.mean(xc * xc, axis=-1, keepdims=True) + EPS)
        xh = xc * rstd
        y = xh * lg_ref[...] + lb_ref[...]
        sy = _sigmoid(y)
        rows = _row_ids(tile, rt)
        da = jnp.where(rows >= PAD_ROWS, dcat[:, 0:cd], 0.0)
        dy = da * (sy * (1.0 + y * (1.0 - sy)))
        dlg_ref[...] += _rowsum(dy * xh)
        dlb_ref[...] += _rowsum(dy)
        dxh = dy * lg_ref[...]
        dconv = rstd * (dxh - jnp.mean(dxh, axis=-1, keepdims=True) - xh * jnp.mean(dxh * xh, axis=-1, keepdims=True))
        dcb_ref[...] += _rowsum(dconv)
        _to_lane_blocks(dcv, 0, dconv)
        for l, ls in enumerate(_lane_blocks(cd)):
            def acc_rows(rb, accs, l=l):
                base = pl.multiple_of(rb * CHUNK, CHUNK)
                d_blk = dcv[l, pl.ds(base, CHUNK), :]
                out = []
                for k in range(CONV_WIDTH):
                    prod = d_blk * gext[l, pl.ds(base + k + HALO - (CONV_WIDTH - 1), CHUNK), :]
                    part = prod[0:8]
                    for s in range(8, CHUNK, 8):
                        part = part + prod[s:s + 8]
                    out.append(accs[k] + part)
                return tuple(out)

            zero = jnp.zeros((8, ls.stop - ls.start), F32)
            accs = lax.fori_loop(0, per, acc_rows, (zero,) * CONV_WIDTH)
            for k in range(CONV_WIDTH):
                dcw_ref[k:k + 1, ls] += _rowsum(accs[k])
        _conv_rows(dcv, cw_ref, lambda k: CONV_WIDTH - 1 - k, conv_s, per, cd)
        dglu = _from_lane_blocks(conv_s)
        dz_ref[:, 0:cd] = (dglu * sg).astype(BF16)
        dz_ref[:, cd:2 * cd] = (dglu * za * sg * (1.0 - sg)).astype(BF16)
        dcv[:, rt:rt + HALO, :] = dcv[:, 0:HALO, :]
        for gi, window in enumerate(POOL_WINDOWS):
            ls = slice(gi * pg, (gi + 1) * pg)
            v = pext[:, ls]
            cnt = _pool_counts(rows, window)
            tm = (_trailing_sum(v, window)[HALO:] / cnt - v[HALO:]).astype(BF16)
            dp = dcat[:, cd + gi * pg:cd + (gi + 1) * pg]
            dps_ref[:, ls] += _rowsum(dp * _dot(tm, pw_ref[gi]))
            dpl = (dp * ps_ref[:, ls]).astype(BF16)
            dpw_ref[gi] += _dot_tn(tm, dpl)
            dtm = _dot_nt(dpl, pw_ref[gi])
            dsp[0:rt, ls] = dtm / cnt
            dpin = _leading_sum(dsp[:, ls], window)[0:rt] - dtm
            dz_ref[:, 2 * cd + gi * pg:2 * cd + (gi + 1) * pg] = dpin.astype(BF16)
        dsp[rt:rt + HALO, :] = dsp[0:HALO, :]
        _project_back(dz_ref, wi_ref, h_ref, g_ref, dh_ref, dhi_ref, dg_ref, step == 0)

    d = h.shape[1]
    back = lambda i: (ntile - 1 - i, 0)
    halo_idx = lambda i: (jnp.maximum((ntile - 1 - i) * per - 1, 0), 0)
    const2 = lambda i: (0, 0)
    nl, lb = len(_lane_blocks(cd)), min(LANE, cd)
    return pl.pallas_call(
        body, grid=(ntile,),
        in_specs=dep_specs + [
                  pl.BlockSpec((rt, ein), back), pl.BlockSpec((CHUNK, ein), halo_idx), pl.BlockSpec((rt, cd), back),
                  pl.BlockSpec((rt, d), back),
                  _resident((1, d)), _resident(w_in_t.shape), pl.BlockSpec((rt, d), back), _resident(w_out.shape),
                  _resident(conv_w.shape), _resident((1, cd)), _resident((1, cd)), _resident((1, cd)),
                  _resident(pool_w.shape), _resident((1, pd))],
        out_specs=[pl.BlockSpec((rt, d), back), pl.BlockSpec((1, d), const2),
                   pl.BlockSpec((rt, ein), back), pl.BlockSpec(conv_w.shape, const2), pl.BlockSpec((1, cd), const2),
                   pl.BlockSpec((1, cd), const2), pl.BlockSpec((1, cd), const2),
                   pl.BlockSpec(pool_w.shape, lambda i: (0, 0, 0)), pl.BlockSpec((1, pd), const2)],
        out_shape=[_sds((t, d), F32), _sds((1, d), F32),
                   _sds((t, ein), BF16), _sds(conv_w.shape, F32), _sds((1, cd), F32), _sds((1, cd), F32),
                   _sds((1, cd), F32), _sds(pool_w.shape, F32), _sds((1, pd), F32)],
        scratch_shapes=[pltpu.VMEM((nl, rt + HALO, lb), F32), pltpu.VMEM((rt + HALO, pd), F32), pltpu.VMEM((nl, rt, lb), F32),
                        pltpu.VMEM((nl, rt + HALO, lb), F32), pltpu.VMEM((rt + HALO, pd), F32)],
        compiler_params=_cparams("arbitrary"), name=name)(*deps, z, z, cv, h, gain, w_in_t, dh, w_out, conv_w, conv_b, ln_g,
                                                          ln_b, pool_w, pool_scale)


def _log_decay(r, gw_ref, gb_ref, rows):
    gp = _dot(r.astype(BF16), gw_ref[...]) + gb_ref[...]
    log_sig = jnp.minimum(gp, 0.0) - jnp.log(1.0 + jnp.exp(-jnp.abs(gp)))
    return gp, jnp.where(rows >= PAD_ROWS, log_sig / GATE_NORM, 0.0)


def _tri(strict):
    r = lax.broadcasted_iota(jnp.int32, (CHUNK, CHUNK), 0)
    c = lax.broadcasted_iota(jnp.int32, (CHUNK, CHUNK), 1)
    return jnp.where(c < r if strict else c <= r, 1.0, 0.0).astype(BF16)


def _tri_dot(tri, a):
    hi = a.astype(BF16)
    rest = a - hi.astype(F32)
    mid = rest.astype(BF16)
    lo = (rest - mid.astype(F32)).astype(BF16)
    return _dot(tri, hi) + _dot(tri, mid) + _dot(tri, lo)


def _gla_mid_fwd(h, gain, w_in_t, w_out, gate_w, gate_b, head_g, *, name):
    t = h.shape[0]
    zw = w_in_t.shape[0]
    dk = gate_b.shape[1]
    hv = head_g.shape[1]
    hk = dk // HEADS
    dv = hv * HEADS
    r_at = 2 * dk + 2 * dv
    rt = _row_tile(t, 320)
    per = rt // CHUNK
    scale = hk ** -0.5

    def body(h_ref, g_ref, wi_ref, wo_ref, gw_ref, gb_ref, hg_ref, ho_ref, o_ref, st_ref, z_ref, u_ref,
             s_ref, la_ref, dec_ref):
        i = pl.program_id(0)

        @pl.when(i == 0)
        def _():
            s_ref[...] = jnp.zeros_like(s_ref)

        _norm_project(h_ref, g_ref, wi_ref, u_ref, z_ref)

        _, la = _log_decay(z_ref[:, r_at:r_at + GATE_PAD], gw_ref, gb_ref, _row_ids(i, rt))
        la_ref[...] = la
        tri = _tri(False)

        def chunk_rows(c):
            return slice(c * CHUNK, (c + 1) * CHUNK)

        def decays(c, carry):
            rows = chunk_rows(c)
            la_c = la_ref[rows, :]
            cum = _tri_dot(tri, la_c)
            dec_ref[rows, :] = jnp.exp(_rowsum(la_c) - cum)
            return carry

        def states(c, carry):
            rows = chunk_rows(c)
            etot = jnp.exp(_rowsum(la_ref[rows, :]))
            for hd in range(HEADS):
                ks = slice(hd * hk, (hd + 1) * hk)
                kd = z_ref[rows, dk + hd * hk:dk + (hd + 1) * hk] * dec_ref[rows, ks]
                v = z_ref[rows, 2 * dk + hd * hv:2 * dk + (hd + 1) * hv]
                s_new = s_ref[hd] * etot[:, ks] + _dot_tn(v.astype(BF16), kd.astype(BF16))
                s_ref[hd] = s_new
                st_ref[c, hd] = s_new
            return carry

        def outputs(c, carry):
            rows = chunk_rows(c)
            for hd in range(HEADS):
                q = z_ref[rows, hd * hk:(hd + 1) * hk] * scale
                g = z_ref[rows, 2 * dk + dv + hd * hv:2 * dk + dv + (hd + 1) * hv]
                o = _dot_nt(q.astype(BF16), st_ref[c, hd].astype(BF16))
                on = o * lax.rsqrt(jnp.mean(o * o, axis=-1, keepdims=True) + EPS) * hg_ref[...]
                o_ref[rows, hd * hv:(hd + 1) * hv] = (on * (g * _sigmoid(g))).astype(BF16)
            return carry

        for phase in (decays, states, outputs):
            for c in range(per):
                phase(c, 0)
        ho_ref[...] = h_ref[...] + _dot(o_ref[...], wo_ref[...])

    d = h.shape[1]
    rows = lambda i: (i, 0)
    return pl.pallas_call(
        body, grid=(t // rt,),
        in_specs=[pl.BlockSpec((rt, d), rows), _resident((1, d)), _resident(w_in_t.shape), _resident(w_out.shape),
                  _resident(gate_w.shape), _resident((1, dk)), _resident((1, hv))],
        out_specs=[pl.BlockSpec((rt, d), rows), pl.BlockSpec((rt, dv), rows),
                   pl.BlockSpec((per, HEADS, hv, hk), lambda i: (i, 0, 0, 0)), pl.BlockSpec((rt, zw), rows),
                   pl.BlockSpec((rt, d), rows)],
        out_shape=[_sds((t, d), F32), _sds((t, dv), BF16), _sds((t // CHUNK, HEADS, hv, hk), F32), _sds((t, zw), F32),
                   _sds((t, d), BF16)],
        scratch_shapes=[pltpu.VMEM((HEADS, hv, hk), F32), pltpu.VMEM((rt, dk), F32), pltpu.VMEM((rt, dk), F32)],
        compiler_params=_cparams("arbitrary"), name=name)(h, gain, w_in_t, w_out, gate_w, gate_b, head_g)


def _gla_mid_bwd(z, h, gain, w_in_t, dh, w_out, states, gate_w, gate_b, head_g, *, after=None, name):
    t = z.shape[0]
    dk = gate_b.shape[1]
    hv = head_g.shape[1]
    hk = dk // HEADS
    dv = hv * HEADS
    r_at = 2 * dk + 2 * dv
    rt = _row_tile(t, 320)
    ntile = t // rt
    per = rt // CHUNK
    scale = hk ** -0.5
    dep_specs, deps = _dep_specs(after)

    def body(*refs):
        (z_ref, h_ref, g_ref, wi_ref, dh_ref, wo_ref, st_ref, stp_ref, gw_ref, gb_ref, hg_ref,
         dhi_ref, dg_ref, dz_ref, dgw_ref, dgb_ref, dhg_ref,
         ds_ref, la_ref, dla_ref, dec_ref, dos_ref, e_ref, do_ref) = refs[len(deps):]
        step = pl.program_id(0)
        tile = ntile - 1 - step
        do_ref[...] = _dot_nt(dh_ref[...].astype(BF16), wo_ref[...])

        @pl.when(step == 0)
        def _():
            ds_ref[...] = jnp.zeros_like(ds_ref)
            dgw_ref[...] = jnp.zeros_like(dgw_ref)
            dgb_ref[...] = jnp.zeros_like(dgb_ref)
            dhg_ref[...] = jnp.zeros_like(dhg_ref)

        rows_id = _row_ids(tile, rt)
        r = z_ref[:, r_at:r_at + GATE_PAD]
        gp, la = _log_decay(r, gw_ref, gb_ref, rows_id)
        la_ref[...] = la
        tri, tri_strict = _tri(False), _tri(True)
        keep = jnp.where(tile > 0, 1.0, 0.0)

        def chunk_rows(c):
            return slice(c * CHUNK, (c + 1) * CHUNK)

        def recompute(c, dhg):
            rows = chunk_rows(c)
            la_c = la_ref[rows, :]
            cum = _tri_dot(tri, la_c)
            dec_ref[rows, :] = jnp.exp(_rowsum(la_c) - cum)
            for hd in range(HEADS):
                q = (z_ref[rows, hd * hk:(hd + 1) * hk] * scale).astype(BF16)
                g = z_ref[rows, 2 * dk + dv + hd * hv:2 * dk + dv + (hd + 1) * hv]
                s_b = st_ref[c, hd].astype(BF16)
                o = _dot_nt(q, s_b)
                rstd = lax.rsqrt(jnp.mean(o * o, axis=-1, keepdims=True) + EPS)
                oh = o * rstd
                sg = _sigmoid(g)
                d_og = do_ref[rows, hd * hv:(hd + 1) * hv]
                dz_ref[rows, 2 * dk + dv + hd * hv:2 * dk + dv + (hd + 1) * hv] = (
                    d_og * oh * hg_ref[...] * (sg * (1.0 + g * (1.0 - sg)))).astype(BF16)
                don = d_og * (g * sg)
                dhg = dhg + _rowsum(don * oh)
                doh = don * hg_ref[...]
                d_o = (rstd * (doh - oh * jnp.mean(doh * oh, axis=-1, keepdims=True))).astype(BF16)
                dos_ref[rows, hd * hv:(hd + 1) * hv] = d_o
                dz_ref[rows, hd * hk:(hd + 1) * hk] = (_dot(d_o, s_b) * scale).astype(BF16)
            return dhg

        def recurrence(cc, carry):
            c = per - 1 - cc
            rows = chunk_rows(c)
            etot = jnp.exp(_rowsum(la_ref[rows, :]))
            for hd in range(HEADS):
                ks = slice(hd * hk, (hd + 1) * hk)
                q = (z_ref[rows, hd * hk:(hd + 1) * hk] * scale).astype(BF16)
                dec = dec_ref[rows, ks]
                kd = z_ref[rows, dk + hd * hk:dk + (hd + 1) * hk] * dec
                v = z_ref[rows, 2 * dk + hd * hv:2 * dk + (hd + 1) * hv].astype(BF16)
                s_prev = st_ref[c - 1, hd] if c > 0 else keep * stp_ref[0, hd]
                ds_t = ds_ref[hd] + _dot_tn(dos_ref[rows, hd * hv:(hd + 1) * hv], q)
                ds_b = ds_t.astype(BF16)
                dkd = _dot(v, ds_b)
                dz_ref[rows, 2 * dk + hd * hv:2 * dk + (hd + 1) * hv] = _dot_nt(kd.astype(BF16), ds_b).astype(BF16)
                dtot = etot[:, ks] * _rowsum(ds_t * s_prev)
                ds_ref[hd] = ds_t * etot[:, ks]
                dz_ref[rows, dk + hd * hk:dk + (hd + 1) * hk] = (dkd * dec).astype(BF16)
                e_ref[rows, ks] = dkd * kd
                dla_ref[rows, ks] = jnp.broadcast_to(dtot, (CHUNK, hk))
            return carry

        def decay_cotangent(c, carry):
            rows = chunk_rows(c)
            dla_ref[rows, :] += _tri_dot(tri_strict, e_ref[rows, :])
            return carry

        dhg = jnp.zeros((1, hv), F32)
        for c in range(per):
            dhg = recompute(c, dhg)
        dhg_ref[...] += dhg
        for phase in (recurrence, decay_cotangent):
            for c in range(per):
                phase(c, 0)
        dla = jnp.where(rows_id >= PAD_ROWS, dla_ref[...], 0.0)
        dgp = dla * (1.0 / GATE_NORM) * (1.0 - _sigmoid(gp))
        dgb_ref[...] += _rowsum(dgp)
        dgp_b = dgp.astype(BF16)
        dgw_ref[...] += _dot_tn(r.astype(BF16), dgp_b)
        dz_ref[:, r_at:r_at + GATE_PAD] = _dot_nt(dgp_b, gw_ref[...]).astype(BF16)
        _project_back(dz_ref, wi_ref, h_ref, g_ref, dh_ref, dhi_ref, dg_ref, step == 0)

    d = h.shape[1]
    back = lambda i: (ntile - 1 - i, 0)
    const2 = lambda i: (0, 0)
    return pl.pallas_call(
        body, grid=(ntile,),
        in_specs=dep_specs + [
                  pl.BlockSpec((rt, z.shape[1]), back), pl.BlockSpec((rt, d), back), _resident((1, d)),
                  _resident(w_in_t.shape), pl.BlockSpec((rt, d), back), _resident(w_out.shape),
                  pl.BlockSpec((per, HEADS, hv, hk), lambda i: (ntile - 1 - i, 0, 0, 0)),
                  pl.BlockSpec((1, HEADS, hv, hk), lambda i: (jnp.maximum((ntile - 1 - i) * per - 1, 0), 0, 0, 0)),
                  _resident(gate_w.shape), _resident((1, dk)), _resident((1, hv))],
        out_specs=[pl.BlockSpec((rt, d), back), pl.BlockSpec((1, d), const2), pl.BlockSpec((rt, z.shape[1]), back),
                   pl.BlockSpec(gate_w.shape, const2), pl.BlockSpec((1, dk), const2), pl.BlockSpec((1, hv), const2)],
        out_shape=[_sds((t, d), F32), _sds((1, d), F32), _sds(z.shape, BF16), _sds(gate_w.shape, F32),
                   _sds((1, dk), F32), _sds((1, hv), F32)],
        scratch_shapes=[pltpu.VMEM((HEADS, hv, hk), F32), pltpu.VMEM((rt, dk), F32), pltpu.VMEM((rt, dk), F32),
                        pltpu.VMEM((rt, dk), F32), pltpu.VMEM((rt, dv), BF16), pltpu.VMEM((rt, dk), F32),
                        pltpu.VMEM((rt, dv), F32)],
        compiler_params=_cparams("arbitrary"), name=name)(*deps, z, h, gain, w_in_t, dh, w_out, states, states, gate_w,
                                                          gate_b, head_g)


def _head(h, gain, target, *, name):
    t, d = h.shape
    rt = _row_tile(t, 832)

    def body(h_ref, g_ref, t_ref, dh_ref, loss_ref, dg_ref):
        i = pl.program_id(0)

        @pl.when(i == 0)
        def _():
            loss_ref[...] = jnp.zeros_like(loss_ref)
            dg_ref[...] = jnp.zeros_like(dg_ref)

        hv = h_ref[...]
        rstd = lax.rsqrt(jnp.mean(hv * hv, axis=-1, keepdims=True) + EPS)
        xh = hv * rstd
        err = jnp.where(_row_ids(i, rt) >= CHUNK, xh * g_ref[...] - t_ref[...], 0.0)
        loss_ref[...] += (0.5 / d) * jnp.sum(err * err)
        dy = err * (1.0 / d)
        dg_ref[...] += _rowsum(dy * xh)
        dxh = dy * g_ref[...]
        dh_ref[...] = rstd * (dxh - xh * jnp.mean(dxh * xh, axis=-1, keepdims=True))

    return pl.pallas_call(
        body, grid=(t // rt,),
        in_specs=[pl.BlockSpec((rt, d), lambda i: (i, 0)), _resident((1, d)), pl.BlockSpec((rt, d), lambda i: (i, 0))],
        out_specs=[pl.BlockSpec((rt, d), lambda i: (i, 0)), pl.BlockSpec((8, LANE), lambda i: (0, 0)),
                   pl.BlockSpec((1, d), lambda i: (0, 0))],
        out_shape=[_sds((t, d), F32), _sds((8, LANE), F32), _sds((1, d), F32)],
        compiler_params=_cparams("arbitrary"), name=name)(h, gain, target)


def _adamw_math(w, g, m, v):
    m = ADAM_B1 * m + (1.0 - ADAM_B1) * g
    v = ADAM_B2 * v + (1.0 - ADAM_B2) * (g * g)
    m_hat = m / (1.0 - ADAM_B1 ** ADAM_STEP)
    v_hat = v / (1.0 - ADAM_B2 ** ADAM_STEP)
    return -ADAM_LR * (m_hat / (jnp.sqrt(v_hat) + ADAM_EPS) + ADAM_WD * w), m, v


N_CHIP = N_DEV // 2
BLOCK_ELEMS = 128 * 1024


def _my_slot():
    return 4 * lax.axis_index("x") + 2 * lax.axis_index("y") + lax.axis_index("c")


def _row_block(r, c):
    cap = max(8, BLOCK_ELEMS // (-(-c // LANE) * LANE))
    return max([b for b in range(8, r + 1, 8) if r % b == 0 and b <= cap] or [r])


def _blocks(r, c):
    rb = _row_block(r, c)
    if rb < r or r * c <= BLOCK_ELEMS:
        return rb, c
    return r, max([b for b in (512, 256, LANE) if c % b == 0 and r * b <= BLOCK_ELEMS] or [c])


def _reduce_adam(parts, w, m, v, *, after=None, name):
    nl, r, c = w.shape
    rb, cb = _blocks(r, c)
    dep_specs, deps = _dep_specs(after)

    def body(*refs):
        me = refs[0][0]
        refs = refs[1 + len(deps):]
        p_refs = refs[:2 * nl]
        w_ref, m_ref, v_ref, g_out, d_out, m_out, v_out = refs[2 * nl:]
        layer = pl.program_id(0)
        for li in range(nl):
            @pl.when(layer == li)
            def _(li=li):
                own_ref, land_ref = p_refs[2 * li], p_refs[2 * li + 1]
                mine = own_ref[...].astype(F32)
                g = None
                for dev in range(N_DEV):
                    term = jnp.where(me == dev, mine, land_ref[dev].astype(F32))
                    g = term if g is None else g + term
                g_out[...] = g
                d_out[...], m_out[...], v_out[...] = _adamw_math(w_ref[...], g, m_ref[...], v_ref[...])

    blk = pl.BlockSpec((None, rb, cb), lambda l, i, j, me: (l, i, j))
    p_specs = []
    for li in range(nl):
        p_specs += [
            pl.BlockSpec((None, rb, cb), lambda l, i, j, me, li=li: (me[0], jnp.where(l == li, i, 0), jnp.where(l == li, j, 0))),
            pl.BlockSpec((N_DEV, rb, cb), lambda l, i, j, me, li=li: (0, jnp.where(l == li, i, 0), jnp.where(l == li, j, 0)))]
    flat = [p for pair in parts for p in pair]
    grid_spec = pltpu.PrefetchScalarGridSpec(
        num_scalar_prefetch=1, grid=(nl, r // rb, c // cb), in_specs=dep_specs + p_specs + [blk, blk, blk],
        out_specs=[blk] * 4)
    return pl.pallas_call(
        body, grid_spec=grid_spec, out_shape=[_sds(w.shape, F32)] * 4,
        compiler_params=_cparams("arbitrary", "arbitrary", "arbitrary"), name=name)(
        _my_slot().reshape(1), *deps, *flat, w, m, v)


def _sum8(own, landed, *, name):
    def body(own_ref, land_ref, o_ref):
        me = _my_slot()
        total = None
        for dev in range(N_DEV):
            term = jnp.where(me == dev, own_ref[...], land_ref[dev])
            total = term if total is None else total + term
        o_ref[...] = total

    return pl.pallas_call(body, out_shape=_sds(own.shape, F32), name=name)(own, landed)


def _adam_small(own, landed, split, w, m, v, *, name):
    n = len(w)

    def body(*refs):
        own_refs, land_refs, w_refs, m_refs, v_refs = (refs[k * n:(k + 1) * n] for k in range(5))
        outs = refs[5 * n:]
        me = _my_slot()
        for k in range(n):
            mine = own_refs[k][me] if split[k] else own_refs[k][...]
            g = None
            for dev in range(N_DEV):
                term = jnp.where(me == dev, mine, land_refs[k][dev])
                g = term if g is None else g + term
            outs[4 * k][...] = g
            outs[4 * k + 1][...], outs[4 * k + 2][...], outs[4 * k + 3][...] = _adamw_math(
                w_refs[k][...], g, m_refs[k][...], v_refs[k][...])

    out = pl.pallas_call(body, out_shape=[_sds(a.shape, F32) for a in w for _ in range(4)],
                         compiler_params=pltpu.CompilerParams(vmem_limit_bytes=V7X_VMEM_LIMIT), name=name)(
        *own, *landed, *w, *m, *v)
    return [tuple(out[4 * k:4 * k + 4]) for k in range(n)]


_HBM = pl.BlockSpec(memory_space=pltpu.HBM)
_SEM = pl.BlockSpec(memory_space=pltpu.SEMAPHORE)
_DATAFLOW = pltpu.SideEffectType.DATAFLOW_SIDE_EFFECTING


def _plan_to_all(src, land):
    x, y, c = lax.axis_index("x"), lax.axis_index("y"), lax.axis_index("c")
    return [(src, land.at[_my_slot()], (x ^ ((d >> 2) & 1), y ^ ((d >> 1) & 1), c ^ (d & 1))) for d in range(1, N_DEV)]


def _plan_split_to_all(src, land):
    x, y, c = lax.axis_index("x"), lax.axis_index("y"), lax.axis_index("c")
    peers = [(x ^ ((d >> 2) & 1), y ^ ((d >> 1) & 1), c ^ (d & 1)) for d in range(1, N_DEV)]
    return [(src.at[4 * px + 2 * py + pc], land.at[_my_slot()], (px, py, pc)) for px, py, pc in peers]


_PLAN_COPIES = {_plan_to_all: N_DEV - 1, _plan_split_to_all: N_DEV - 1}


def _plans(plan, n):
    return list(plan) if isinstance(plan, (list, tuple)) else [plan] * n


def _exchange_copies(plan, ins, lands, send, recv):
    copies, sem = [], 0
    for p, src, land in zip(_plans(plan, len(lands)), ins, lands):
        for s, dst, dev in p(src, land):
            copies.append(pltpu.make_async_remote_copy(
                src_ref=s, dst_ref=dst, send_sem=send.at[sem], recv_sem=recv.at[sem],
                device_id=dev, device_id_type=pl.DeviceIdType.MESH))
            sem += 1
    return copies


def _place_own(a, dtype, *, after=None, name):
    r, c = a.shape
    rb = _row_block(r, c)
    dep_specs, deps = _dep_specs(after)

    def body(*refs):
        a_ref, o_ref = refs[1 + len(deps):]
        o_ref[...] = a_ref[...].astype(dtype)

    grid_spec = pltpu.PrefetchScalarGridSpec(
        num_scalar_prefetch=1, grid=(r // rb,), in_specs=dep_specs + [pl.BlockSpec((rb, c), lambda i, me: (i, 0))],
        out_specs=pl.BlockSpec((None, rb, c), lambda i, me: (me[0], i, 0)))
    return pl.pallas_call(body, grid_spec=grid_spec, out_shape=_sds((N_DEV, r, c), dtype),
                          compiler_params=_cparams("arbitrary"), name=name)(_my_slot().reshape(1), *deps, a)


def _plan_gather_first(land, _):
    x, y, c = lax.axis_index("x"), lax.axis_index("y"), lax.axis_index("c")
    mine = land.at[_my_slot()]
    return [(mine, mine, (x, y, 1 - c))] + [(mine, mine, (x ^ (d >> 1), y ^ (d & 1), c)) for d in range(1, N_CHIP)]


def _plan_gather_relay(land, _):
    x, y, c = lax.axis_index("x"), lax.axis_index("y"), lax.axis_index("c")
    slots = [land.at[4 * (x ^ (d >> 1)) + 2 * (y ^ (d & 1)) + c] for d in range(1, N_CHIP)]
    return [(s, s, (x, y, 1 - c)) for s in slots]


def _plan_gather_direct(land, _):
    return _plan_to_all(land.at[_my_slot()], land)


_PLAN_COPIES[_plan_gather_first] = N_CHIP
_PLAN_COPIES[_plan_gather_relay] = N_CHIP - 1
_PLAN_COPIES[_plan_gather_direct] = N_DEV - 1


def _exchange_start(plan, arrs, lands, *, after=None, name):
    bufs = list(lands) if arrs is None else list(arrs) + list(lands)
    n, nb = len(lands), len(bufs)
    nsem = sum(_PLAN_COPIES[p] for p in _plans(plan, n))
    dep_specs, deps = _dep_specs(after)

    def body(*refs):
        ins, land_refs = refs[:n], refs[nb - n:nb]
        send, recv = refs[nb + len(deps)], refs[nb + len(deps) + 1]
        for cp in _exchange_copies(plan, ins, land_refs, send, recv):
            cp.start()
        refs[-1][...] = jnp.zeros_like(refs[-1])

    out = pl.pallas_call(
        body, name=name,
        out_shape=(pltpu.SemaphoreType.DMA((nsem,)), pltpu.SemaphoreType.DMA((nsem,)),
                   *[pltpu.HBM(a.shape, a.dtype) for a in bufs], _sds((8, LANE), F32)),
        in_specs=[_HBM] * nb + dep_specs,
        out_specs=(_SEM, _SEM, *([_HBM] * nb), pl.BlockSpec(memory_space=pltpu.VMEM)),
        input_output_aliases={i: 2 + i for i in range(nb)},
        compiler_params=pltpu.CompilerParams(has_side_effects=_DATAFLOW),
    )(*[pltpu.with_memory_space_constraint(a, pltpu.HBM) for a in bufs], *deps)
    return (plan, n, out[0], out[1], list(out[2:2 + nb])), out[-1]


def _exchange_now(plan, lands, *, name):
    n = len(lands)
    nsem = sum(_PLAN_COPIES[p] for p in _plans(plan, n))

    def body(*refs):
        land_refs, send, recv = refs[n:2 * n], refs[2 * n], refs[2 * n + 1]
        copies = _exchange_copies(plan, land_refs, land_refs, send, recv)
        for cp in copies:
            cp.start()
        for cp in copies:
            cp.wait_send()
            cp.wait_recv()

    hbm = pl.BlockSpec(memory_space=pl.ANY)
    return pl.pallas_call(
        body, in_specs=[hbm] * n, out_specs=[hbm] * n, out_shape=[_sds(a.shape, a.dtype) for a in lands],
        input_output_aliases={i: i for i in range(n)},
        scratch_shapes=[pltpu.SemaphoreType.DMA((nsem,)), pltpu.SemaphoreType.DMA((nsem,))], name=name)(*lands)


def _exchange_wait(state, after, *, name):
    plan, n, send_sem, recv_sem, bufs = state
    nb = len(bufs)
    after = list(after) if isinstance(after, (list, tuple)) else [after]

    def body(*refs):
        ins, land_refs, send, recv = refs[:n], refs[nb - n:nb], refs[nb], refs[nb + 1]
        for cp in _exchange_copies(plan, ins, land_refs, send, recv):
            cp.wait_send()
            cp.wait_recv()

    out = pl.pallas_call(
        body, name=name, out_shape=[pltpu.HBM(a.shape, a.dtype) for a in bufs],
        in_specs=[_HBM] * nb + [_SEM, _SEM] + [pl.BlockSpec(memory_space=pl.ANY)] * len(after), out_specs=[_HBM] * nb,
        input_output_aliases={i: i for i in range(nb)},
        compiler_params=pltpu.CompilerParams(has_side_effects=_DATAFLOW),
    )(*bufs, send_sem, recv_sem, *after)
    return list(out[:n]), list(out[nb - n:])


def _dep_specs(after):
    return ([], []) if after is None else ([pl.BlockSpec(memory_space=pl.ANY)], [after])


def _undo_column_split(g):
    return jnp.transpose(g, (1, 0, 2)).reshape(g.shape[1], N_DEV * g.shape[2])


def _column_split(a):
    r, c = a.shape
    return jnp.transpose(a.reshape(r, N_DEV, c // N_DEV), (1, 0, 2))


class _WholeWeights:
    def __init__(self, groups):
        self.groups = groups
        self.grads = {}

    def fetch(self, group, after):
        return self.groups[group]

    def emit(self, group, grads):
        self.grads.update(grads)
        return None


def _local_step(x, target, replicated, src):
    d = x.shape[1]
    mix_g, ffn_g = replicated["mix_g"], replicated["ffn_g"]
    h0 = jnp.concatenate([jnp.zeros((CHUNK, d), F32), x], axis=0)
    tgt = jnp.concatenate([jnp.zeros((CHUNK, d), F32), target], axis=0)
    cp = src.fetch("cp", [h0, tgt])
    h0 = lax.dynamic_update_slice(h0, cp["meta"], (PAD_ROWS, 0))
    cp_mid = (cp["conv_w"], replicated["conv_b"], replicated["ln_g"], replicated["ln_b"], replicated["pool_w"],
              replicated["pool_scale"])

    h1, cat, z0, u0, cv0 = _cp_mid_fwd(h0, mix_g[0:1], cp["cp_w_in_t"], cp["cp_w_out"], *cp_mid, name="cp_mixer")
    ffn0 = src.fetch("ffn0", h1)
    h2, uf0, rf0 = _ffn_fwd(h1, ffn_g[0:1], ffn0["w1"], ffn0["w2"], name="ffn0")
    gla = src.fetch("gla", h2)
    gla_mid = (gla["gate_w"], gla["gate_b"], gla["head_g"])
    h3, og, states, z1, u1 = _gla_mid_fwd(h2, mix_g[1:2], gla["gla_w_in_t"], gla["gla_w_out"], *gla_mid, name="gla_mixer")
    ffn1 = src.fetch("ffn1", h3)
    h4, uf1, rf1 = _ffn_fwd(h3, ffn_g[1:2], ffn1["w1"], ffn1["w2"], name="ffn1")
    dh4, loss, d_final_g = _head(h4, replicated["final_g"], tgt, name="head")

    dh3, dhh1, dob1, dffn_g1 = _ffn_bwd_x(h3, dh4, ffn_g[1:2], rf1, ffn1["w1"], ffn1["w2"], name="ffn1_bwd_x")
    dw1_1, dw2_1 = _ffn_bwd_w(uf1, dhh1, rf1, dob1, name="ffn1_bwd_w")
    sent = src.emit("ffn1", dict(w1=dw1_1, w2=dw2_1))
    d_gla_w_out = _linear_bwd_w(og, dh3, name="gla_out_dw")
    dh2, dmix_g1, dz1, d_gate_w, d_gate_b, d_head_g = _gla_mid_bwd(
        z1, h2, mix_g[1:2], gla["gla_w_in_t"], dh3, gla["gla_w_out"], states, *gla_mid, after=sent, name="gla_mixer_bwd")
    d_gla_w_in_t = _linear_bwd_w(dz1, u1, name="gla_in_dw")
    sent = src.emit("gla", dict(gla_w_in_t=d_gla_w_in_t, gla_w_out=d_gla_w_out))
    dh1, dhh0, dob0, dffn_g0 = _ffn_bwd_x(h1, dh2, ffn_g[0:1], rf0, ffn0["w1"], ffn0["w2"], after=sent, name="ffn0_bwd_x")
    dw1_0, dw2_0 = _ffn_bwd_w(uf0, dhh0, rf0, dob0, name="ffn0_bwd_w")
    src.emit("ffn0", dict(w1=dw1_0, w2=dw2_0))
    d_cp_w_out = _linear_bwd_w(cat, dh1, name="cp_out_dw")
    sent = src.emit("cp_out", dict(cp_w_out=d_cp_w_out))
    dh0, dmix_g0, dz0, d_conv_w, d_conv_b, d_ln_g, d_ln_b, d_pool_w, d_pool_scale = _cp_mid_bwd(
        z0, cv0, h0, mix_g[0:1], cp["cp_w_in_t"], dh1, cp["cp_w_out"], *cp_mid, after=sent, name="cp_mixer_bwd")
    d_cp_w_in_t = _linear_bwd_w(dz0, u0, name="cp_in_dw")

    small = dict(
        mix_g=jnp.concatenate([dmix_g0, dmix_g1]), ffn_g=jnp.concatenate([dffn_g0, dffn_g1]), conv_b=d_conv_b, ln_g=d_ln_g,
        ln_b=d_ln_b, pool_w=d_pool_w, pool_scale=d_pool_scale, final_g=d_final_g, meta=dh0[PAD_ROWS:CHUNK], conv_w=d_conv_w,
        gate_w=d_gate_w, gate_b=d_gate_b, head_g=d_head_g)
    src.emit("cp", dict(cp_w_in_t=d_cp_w_in_t, small=small, loss=loss))
    return loss, dh0[CHUNK:], small


_REPLICATED = ("mix_norm_g", "ffn_norm_g", "cp_conv_b", "cp_ln_g", "cp_ln_b", "cp_pool_w", "cp_pool_scale", "final_norm_g")
_SMALL_SHARDED = ("meta_tokens", "cp_conv_w", "gla_gate_w2", "gla_gate_b", "gla_head_g")
_NAMES = ("meta_tokens", "mix_norm_g", "ffn_norm_g", "ffn_w1", "ffn_w2", "cp_w_in", "cp_conv_w", "cp_conv_b", "cp_ln_g",
          "cp_ln_b", "cp_pool_w", "cp_pool_scale", "cp_w_out", "gla_w_in", "gla_gate_w2", "gla_gate_b", "gla_head_g",
          "gla_w_out", "final_norm_g")
_SMALL_GRADS = ("mix_g", "ffn_g", "conv_b", "ln_g", "ln_b", "pool_w", "pool_scale", "final_g", "meta", "conv_w", "gate_w",
                "gate_b", "head_g")
_GROUPS = ("cp", "ffn0", "gla", "ffn1")
_TWO_LEG_GATHERS = ("cp", "ffn0", "ffn1")


class _Exchanges:
    def __init__(self, w, d):
        self.d = d
        small = [w[n].reshape(w[n].shape[-2:]) for n in _SMALL_SHARDED]
        self.small_shard_shapes = [w[n].shape for n in _SMALL_SHARDED]
        shards = dict(
            cp=[(w["cp_w_in"][0].T, BF16), (w["cp_w_out"][0], BF16)] + [(a, F32) for a in small],
            ffn0=[(w["ffn_w1"][0], BF16), (w["ffn_w2"][0], BF16)],
            gla=[(w["gla_w_in"][0].T, BF16), (w["gla_w_out"][0], BF16)],
            ffn1=[(w["ffn_w1"][1], BF16), (w["ffn_w2"][1], BF16)])
        self.gathers = {}
        self.sent = {}
        token = None
        for group in _GROUPS:
            lands = [_place_own(a, dtype, after=token, name=f"place_w_{group}_{k}")
                     for k, (a, dtype) in enumerate(shards[group])]
            plan = _plan_gather_first if group in _TWO_LEG_GATHERS else _plan_gather_direct
            self.gathers[group], token = _exchange_start(plan, None, lands, after=token, name=f"start_w_{group}")
        self.token = token

    def fetch(self, group, after):
        d = self.d
        after = (list(after) if isinstance(after, (list, tuple)) else [after]) + [self.token]
        _, got = _exchange_wait(self.gathers[group], after, name=f"wait_w_{group}")
        if group in _TWO_LEG_GATHERS:
            got = _exchange_now(_plan_gather_relay, got, name=f"relay_w_{group}")
        if group in ("ffn0", "ffn1"):
            return dict(w1=got[0], w2=got[1])
        if group == "gla":
            w_in_t = jnp.pad(got[0].reshape(-1, d), ((0, GATE_PAD - GATE_RANK), (0, 0)))
            return dict(gla_w_in_t=w_in_t, gla_w_out=got[1].reshape(d, d), gate_w=self.gate_w, gate_b=self.gate_b,
                        head_g=self.head_g)
        meta, conv_w, gate_w, self.gate_b, self.head_g = [_undo_column_split(a) for a in got[2:]]
        self.gate_w = jnp.pad(gate_w, ((0, GATE_PAD - GATE_RANK), (0, 0))).astype(BF16)
        return dict(cp_w_in_t=got[0].reshape(-1, d), cp_w_out=got[1].reshape(d, d), meta=meta,
                    conv_w=jnp.pad(conv_w, ((0, 1), (0, 0))))

    def emit(self, group, g):
        d = self.d
        if group in ("ffn0", "ffn1"):
            arrs = [g["w1"], g["w2"]]
        elif group == "gla":
            w_in_t = g["gla_w_in_t"][:3 * d + GATE_RANK]
            arrs = [w_in_t.reshape(N_DEV, -1, d), g["gla_w_out"].reshape(N_DEV, d // N_DEV, d)]
        elif group == "cp_out":
            arrs = [g["cp_w_out"].reshape(N_DEV, d // N_DEV, d)]
        else:
            s = dict(g["small"])
            s.update(pool_w=s["pool_w"][None], conv_w=s["conv_w"][:CONV_WIDTH], gate_w=s["gate_w"][:GATE_RANK])
            own = [s[n] for n in _SMALL_GRADS[:len(_REPLICATED)]]
            own += [_column_split(s[n]).reshape((N_DEV,) + shape)
                    for n, shape in zip(_SMALL_GRADS[len(_REPLICATED):], self.small_shard_shapes)]
            plans = [_plan_to_all] * len(_REPLICATED) + [_plan_split_to_all] * len(_SMALL_SHARDED)
            lands = [lax.empty((N_DEV,) + a.shape, F32) for a in own[:len(_REPLICATED)]]
            lands += [lax.empty(a.shape, F32) for a in own[len(_REPLICATED):]]
            own.append(g["loss"])
            plans.append(_plan_to_all)
            lands.append(lax.empty((N_DEV,) + g["loss"].shape, F32))
            self.small_sent, self.token = _exchange_start(plans, own, lands, after=self.token, name="start_g_small")
            arrs = [g["cp_w_in_t"].reshape(N_DEV, -1, d)]
        self.sent[group], self.token = _exchange_start(_plan_split_to_all, arrs, [lax.empty(a.shape, a.dtype) for a in arrs],
                                                       after=self.token, name=f"start_g_{group}")
        return self.token

    def finish(self, w, mom, var):
        out = {}
        after = self.token

        def landed(group):
            own, got = _exchange_wait(self.sent[group], after, name=f"wait_g_{group}")
            return list(zip(own, got))

        def adam(n, parts, behind=None, transposed=False):
            flip = (lambda a: jnp.transpose(a, (0, 2, 1))) if transposed else (lambda a: a)
            res = _reduce_adam(parts, flip(w[n]), flip(mom[n]), flip(var[n]), after=behind, name=f"adam_{n}")
            out[n] = tuple(flip(a) for a in res)
            return res[0]

        ffn1 = landed("ffn1")
        after = ffn1[0][1]
        gla = landed("gla")
        after = gla[0][1]
        ffn0 = landed("ffn0")
        after = adam("ffn_w1", [ffn0[0], ffn1[0]])
        after = adam("ffn_w2", [ffn0[1], ffn1[1]], after)
        after = adam("gla_w_in", [gla[0]], after, transposed=True)
        after = adam("gla_w_out", [gla[1]], after)
        small_own, small_landed = _exchange_wait(self.small_sent, after, name="wait_g_small")
        names = _REPLICATED + _SMALL_SHARDED
        split = [False] * len(_REPLICATED) + [True] * len(_SMALL_SHARDED)
        small_new = _adam_small(small_own[:-1], small_landed[:-1], split, [w[n] for n in names], [mom[n] for n in names],
                                [var[n] for n in names], name="adam_small")
        out.update(zip(names, small_new))
        out["loss"] = _sum8(small_own[-1], small_landed[-1], name="sum_loss")[0, 0]

        after = small_new[0][0]
        cp_out = landed("cp_out")
        after = adam("cp_w_out", [cp_out[0]])
        cp = landed("cp")
        adam("cp_w_in", [cp[0]], transposed=True)
        return out


def kernel(x, meta_tokens, mix_norm_g, ffn_norm_g, ffn_w1, ffn_w2, cp_w_in, cp_conv_w, cp_conv_b, cp_ln_g, cp_ln_b, cp_pool_w, cp_pool_scale, cp_w_out, gla_w_in, gla_gate_w2, gla_gate_b, gla_head_g, gla_w_out, final_norm_g, loss_target, m_meta_tokens, m_mix_norm_g, m_ffn_norm_g, m_ffn_w1, m_ffn_w2, m_cp_w_in, m_cp_conv_w, m_cp_conv_b, m_cp_ln_g, m_cp_ln_b, m_cp_pool_w, m_cp_pool_scale, m_cp_w_out, m_gla_w_in, m_gla_gate_w2, m_gla_gate_b, m_gla_head_g, m_gla_w_out, m_final_norm_g, v_meta_tokens, v_mix_norm_g, v_ffn_norm_g, v_ffn_w1, v_ffn_w2, v_cp_w_in, v_cp_conv_w, v_cp_conv_b, v_cp_ln_g, v_cp_ln_b, v_cp_pool_w, v_cp_pool_scale, v_cp_w_out, v_gla_w_in, v_gla_gate_w2, v_gla_gate_b, v_gla_head_g, v_gla_w_out, v_final_norm_g):
    w = dict(meta_tokens=meta_tokens, mix_norm_g=mix_norm_g, ffn_norm_g=ffn_norm_g, ffn_w1=ffn_w1, ffn_w2=ffn_w2,
             cp_w_in=cp_w_in, cp_conv_w=cp_conv_w, cp_conv_b=cp_conv_b, cp_ln_g=cp_ln_g, cp_ln_b=cp_ln_b,
             cp_pool_w=cp_pool_w, cp_pool_scale=cp_pool_scale, cp_w_out=cp_w_out, gla_w_in=gla_w_in,
             gla_gate_w2=gla_gate_w2, gla_gate_b=gla_gate_b, gla_head_g=gla_head_g, gla_w_out=gla_w_out,
             final_norm_g=final_norm_g.reshape(1, -1))
    mom = dict(meta_tokens=m_meta_tokens, mix_norm_g=m_mix_norm_g, ffn_norm_g=m_ffn_norm_g, ffn_w1=m_ffn_w1, ffn_w2=m_ffn_w2,
               cp_w_in=m_cp_w_in, cp_conv_w=m_cp_conv_w, cp_conv_b=m_cp_conv_b, cp_ln_g=m_cp_ln_g, cp_ln_b=m_cp_ln_b,
               cp_pool_w=m_cp_pool_w, cp_pool_scale=m_cp_pool_scale, cp_w_out=m_cp_w_out, gla_w_in=m_gla_w_in,
               gla_gate_w2=m_gla_gate_w2, gla_gate_b=m_gla_gate_b, gla_head_g=m_gla_head_g, gla_w_out=m_gla_w_out,
               final_norm_g=m_final_norm_g.reshape(1, -1))
    var = dict(meta_tokens=v_meta_tokens, mix_norm_g=v_mix_norm_g, ffn_norm_g=v_ffn_norm_g, ffn_w1=v_ffn_w1, ffn_w2=v_ffn_w2,
               cp_w_in=v_cp_w_in, cp_conv_w=v_cp_conv_w, cp_conv_b=v_cp_conv_b, cp_ln_g=v_cp_ln_g, cp_ln_b=v_cp_ln_b,
               cp_pool_w=v_cp_pool_w, cp_pool_scale=v_cp_pool_scale, cp_w_out=v_cp_w_out, gla_w_in=v_gla_w_in,
               gla_gate_w2=v_gla_gate_w2, gla_gate_b=v_gla_gate_b, gla_head_g=v_gla_head_g, gla_w_out=v_gla_w_out,
               final_norm_g=v_final_norm_g.reshape(1, -1))
    d = x.shape[-1]
    replicated = dict(mix_g=w["mix_norm_g"], ffn_g=w["ffn_norm_g"], conv_b=w["cp_conv_b"], ln_g=w["cp_ln_g"],
                      ln_b=w["cp_ln_b"], pool_w=w["cp_pool_w"][0].astype(BF16), pool_scale=w["cp_pool_scale"],
                      final_g=w["final_norm_g"])
    exchanges = _Exchanges(w, d)
    _, grad_x, _ = _local_step(x[0], loss_target[0], replicated, exchanges)
    out = exchanges.finish(w, mom, var)
    loss = out.pop("loss")

    def leaf(n, k):
        a = out[n][k]
        return a.reshape(-1) if n == "final_norm_g" else a

    return (loss, grad_x[None], *[leaf(n, 0) for n in _NAMES], *[leaf(n, 1) for n in _NAMES],
            *[leaf(n, 2) for n in _NAMES], *[leaf(n, 3) for n in _NAMES])
```

```python
import functools

import jax
import jax.numpy as jnp
from jax import lax
from jax.experimental import pallas as pl
from jax.experimental.pallas import tpu as pltpu

F32, BF16 = jnp.float32, jnp.bfloat16
N_DEV = 8
CHUNK = 64
N_META = 16
PAD_ROWS = CHUNK - N_META
HALO = 32
EPS = 1e-5
CONV_WIDTH = 31
POOL_WINDOWS = (2, 4, 8, 16)
HEADS = 4
GATE_RANK = 16
GATE_NORM = 16.0
GATE_PAD = 128
ADAM_LR, ADAM_B1, ADAM_B2, ADAM_EPS, ADAM_WD, ADAM_STEP = 0.001, 0.9, 0.999, 1e-08, 0.01, 10
V7X_VMEM_LIMIT = 56 * 2 ** 20
LANE = 128


def _cparams(*sem):
    return pltpu.CompilerParams(dimension_semantics=sem, vmem_limit_bytes=V7X_VMEM_LIMIT)


def _row_tile(t, cap):
    best = CHUNK
    for r in range(CHUNK, min(t, cap) + 1, CHUNK):
        if t % r == 0:
            best = r
    return best


def _resident(shape):
    return pl.BlockSpec(shape, lambda *_: (0,) * len(shape), pipeline_mode=pl.Buffered(1))


def _dot(a, b):
    return jnp.dot(a, b, preferred_element_type=F32)


def _dot_nt(a, b):
    return lax.dot_general(a, b, (((1,), (1,)), ((), ())), preferred_element_type=F32)


def _dot_tn(a, b):
    return lax.dot_general(a, b, (((0,), (0,)), ((), ())), preferred_element_type=F32)


def _rowsum(a):
    return jnp.sum(a, axis=0, keepdims=True)


def _sigmoid(a):
    return 1.0 / (1.0 + jnp.exp(-a))


def _row_ids(tile, rt):
    return tile * rt + lax.broadcasted_iota(jnp.int32, (rt, 1), 0)


def _sds(shape, dtype):
    return jax.ShapeDtypeStruct(shape, dtype)


DW_ROWS = 1024


def _linear_bwd_w(x, dy, *, name):
    t, k = x.shape
    n = dy.shape[1]
    cut_k = k > n
    width = k if cut_k else n
    blk = max(c for c in (640, 512, 384, 256, LANE) if width % c == 0)

    def body(x_ref, dy_ref, o_ref, acc):
        for c0 in range(0, t, DW_ROWS):
            rows = slice(c0, min(c0 + DW_ROWS, t))
            part = _dot_tn(x_ref[rows, :].astype(BF16), dy_ref[rows, :].astype(BF16))
            if c0 == 0:
                acc[...] = part
            else:
                acc[...] += part
        o_ref[...] = acc[...].astype(BF16)

    if cut_k:
        in_specs = [pl.BlockSpec((t, blk), lambda j: (0, j)), _resident((t, n))]
        out_specs = pl.BlockSpec((blk, n), lambda j: (j, 0))
        acc_shape = (blk, n)
    else:
        in_specs = [_resident((t, k)), pl.BlockSpec((t, blk), lambda j: (0, j))]
        out_specs = pl.BlockSpec((k, blk), lambda j: (0, j))
        acc_shape = (k, blk)
    return pl.pallas_call(
        body, grid=(width // blk,), in_specs=in_specs, out_specs=out_specs, out_shape=_sds((k, n), BF16),
        scratch_shapes=[pltpu.VMEM(acc_shape, F32)], compiler_params=_cparams("parallel"), name=name)(x, dy)


FFN_BLOCKS_PER_STEP = 2


def _ffn_fwd(h, gain, w1g, w2g, *, loss_head=None, name):
    t, d = h.shape
    f8 = w1g.shape[-1]
    rt = _row_tile(t, 832)
    nb = FFN_BLOCKS_PER_STEP
    nstep = N_DEV // nb

    def body(*refs):
        if loss_head is None:
            h_ref, g_ref, w1_ref, w2_ref, o_ref, u_ref, r_ref, acc_ref = refs
        else:
            h_ref, g_ref, w1_ref, w2_ref, fg_ref, t_ref, o_ref, u_ref, r_ref, loss_ref, dfg_ref, acc_ref = refs
        i, j = pl.program_id(0), pl.program_id(1)

        @pl.when(j == 0)
        def _():
            hv = h_ref[...]
            u_ref[...] = (hv * lax.rsqrt(jnp.mean(hv * hv, axis=-1, keepdims=True) + EPS) * g_ref[...]).astype(BF16)
            acc_ref[...] = jnp.zeros_like(acc_ref)

        part = None
        for b in range(nb):
            a = jnp.maximum(_dot(u_ref[...], w1_ref[b]), 0.0)
            r_ref[:, b * f8:(b + 1) * f8] = a.astype(BF16)
            term = _dot((a * a).astype(BF16), w2_ref[b])
            part = term if part is None else part + term
        acc_ref[...] += part

        @pl.when(j == nstep - 1)
        def _():
            y = h_ref[...] + acc_ref[...]
            if loss_head is None:
                o_ref[...] = y
                return

            @pl.when(i == 0)
            def _():
                loss_ref[...] = jnp.zeros_like(loss_ref)
                dfg_ref[...] = jnp.zeros_like(dfg_ref)

            rstd = lax.rsqrt(jnp.mean(y * y, axis=-1, keepdims=True) + EPS)
            xh = y * rstd
            err = jnp.where(_row_ids(i, rt) >= CHUNK, xh * fg_ref[...] - t_ref[...], 0.0)
            loss_ref[...] += (0.5 / d) * jnp.sum(err * err)
            dy = err * (1.0 / d)
            dfg_ref[...] += _rowsum(dy * xh)
            dxh = dy * fg_ref[...]
            o_ref[...] = rstd * (dxh - xh * jnp.mean(dxh * xh, axis=-1, keepdims=True))

    rows = lambda i, j: (i, 0)
    in_specs = [pl.BlockSpec((rt, d), rows), _resident((1, d)),
                pl.BlockSpec((nb, d, f8), lambda i, j: (j, 0, 0)), pl.BlockSpec((nb, f8, d), lambda i, j: (j, 0, 0))]
    out_specs = [pl.BlockSpec((rt, d), rows), pl.BlockSpec((rt, d), rows), pl.BlockSpec((rt, nb * f8), lambda i, j: (i, j))]
    out_shape = [_sds((t, d), F32), _sds((t, d), BF16), _sds((t, N_DEV * f8), BF16)]
    args = [h, gain, w1g, w2g]
    if loss_head is not None:
        in_specs += [_resident((1, d)), pl.BlockSpec((rt, d), rows)]
        out_specs += [pl.BlockSpec((8, LANE), lambda i, j: (0, 0)), pl.BlockSpec((1, d), lambda i, j: (0, 0))]
        out_shape += [_sds((8, LANE), F32), _sds((1, d), F32)]
        args += list(loss_head)
    return pl.pallas_call(
        body, grid=(t // rt, nstep), in_specs=in_specs, out_specs=out_specs, out_shape=out_shape,
        scratch_shapes=[pltpu.VMEM((rt, d), F32)],
        compiler_params=_cparams("arbitrary" if loss_head is not None else "parallel", "arbitrary"), name=name)(*args)


def _ffn_bwd_x(h, dout, gain, r, w1g, w2g, *, after=None, name):
    t, d = h.shape
    f8 = w1g.shape[-1]
    rt = _row_tile(t, 832)
    nb = FFN_BLOCKS_PER_STEP
    last = N_DEV // nb - 1
    dep_specs, deps = _dep_specs(after)

    def body(*refs):
        h_ref, do_ref, g_ref, r_ref, w1_ref, w2_ref, dh_ref, dhh_ref, dob_ref, dg_ref, du_ref = refs[len(deps):]
        i, j = pl.program_id(0), pl.program_id(1)

        @pl.when(j == 0)
        def _():
            dob_ref[...] = do_ref[...].astype(BF16)
            du_ref[...] = jnp.zeros_like(du_ref)

        part = None
        for b in range(nb):
            cols = slice(b * f8, (b + 1) * f8)
            dhh = (_dot_nt(dob_ref[...], w2_ref[b]) * (2.0 * r_ref[:, cols].astype(F32))).astype(BF16)
            dhh_ref[:, cols] = dhh
            term = _dot_nt(dhh, w1_ref[b])
            part = term if part is None else part + term
        du_ref[...] += part

        @pl.when(j == last)
        def _():
            @pl.when(i == 0)
            def _():
                dg_ref[...] = jnp.zeros_like(dg_ref)

            hv = h_ref[...]
            rstd = lax.rsqrt(jnp.mean(hv * hv, axis=-1, keepdims=True) + EPS)
            xh = hv * rstd
            du = du_ref[...]
            dg_ref[...] += _rowsum(du * xh)
            dxh = du * g_ref[...]
            dh_ref[...] = do_ref[...] + rstd * (dxh - xh * jnp.mean(dxh * xh, axis=-1, keepdims=True))

    rows = lambda i, j: (i, 0)
    return pl.pallas_call(
        body, grid=(t // rt, N_DEV // nb),
        in_specs=dep_specs + [
                  pl.BlockSpec((rt, d), rows), pl.BlockSpec((rt, d), rows), _resident((1, d)),
                  pl.BlockSpec((rt, nb * f8), lambda i, j: (i, j)),
                  pl.BlockSpec((nb, d, f8), lambda i, j: (j, 0, 0)),
                  pl.BlockSpec((nb, f8, d), lambda i, j: (j, 0, 0))],
        out_specs=[pl.BlockSpec((rt, d), rows), pl.BlockSpec((rt, nb * f8), lambda i, j: (i, j)),
                   pl.BlockSpec((rt, d), rows), pl.BlockSpec((1, d), lambda i, j: (0, 0))],
        out_shape=[_sds((t, d), F32), _sds((t, N_DEV * f8), BF16), _sds((t, d), BF16), _sds((1, d), F32)],
        scratch_shapes=[pltpu.VMEM((rt, d), F32)],
        compiler_params=_cparams("arbitrary", "arbitrary"), name=name)(*deps, h, dout, gain, r, w1g, w2g)


def _ffn_bwd_w(u, dhh, r, dout_b, *, name):
    t, d = u.shape
    f8 = dhh.shape[1] // N_DEV

    def body(u_ref, dhh_ref, r_ref, dob_ref, dw1_ref, dw2_ref, acc1, acc2):
        for c0 in range(0, t, DW_ROWS):
            rows = slice(c0, min(c0 + DW_ROWS, t))
            rr = r_ref[rows, :].astype(F32)
            part1 = _dot_tn(u_ref[rows, :], dhh_ref[rows, :])
            part2 = _dot_tn((rr * rr).astype(BF16), dob_ref[rows, :])
            if c0 == 0:
                acc1[...] = part1
                acc2[...] = part2
            else:
                acc1[...] += part1
                acc2[...] += part2
        dw1_ref[...] = acc1[...].astype(BF16)
        dw2_ref[...] = acc2[...].astype(BF16)

    return pl.pallas_call(
        body, grid=(N_DEV,),
        in_specs=[_resident((t, d)), pl.BlockSpec((t, f8), lambda j: (0, j)), pl.BlockSpec((t, f8), lambda j: (0, j)),
                  _resident((t, d))],
        out_specs=[pl.BlockSpec((None, d, f8), lambda j: (j, 0, 0)), pl.BlockSpec((None, f8, d), lambda j: (j, 0, 0))],
        out_shape=[_sds((N_DEV, d, f8), BF16), _sds((N_DEV, f8, d), BF16)],
        scratch_shapes=[pltpu.VMEM((d, f8), F32), pltpu.VMEM((f8, d), F32)],
        compiler_params=_cparams("parallel"), name=name)(u, dhh, r, dout_b)


def _lane_blocks(width):
    lb = min(LANE, width)
    return [slice(s, s + lb) for s in range(0, width, lb)]


def _conv_rows(src_ref, w_ref, offset, dst_ref, nblk, width, bias_ref=None):
    def blk(rb, carry):
        base = pl.multiple_of(rb * CHUNK, CHUNK)
        for l, ls in enumerate(_lane_blocks(width)):
            acc = jnp.zeros((CHUNK, ls.stop - ls.start), F32)
            if bias_ref is not None:
                acc = acc + bias_ref[:, ls]
            for k in range(CONV_WIDTH):
                acc = acc + w_ref[k:k + 1, ls] * src_ref[l, pl.ds(base + offset(k), CHUNK), :]
            dst_ref[l, pl.ds(base, CHUNK), :] = acc
        return carry

    lax.fori_loop(0, nblk, blk, 0)


def _to_lane_blocks(ref, row0, value):
    for l, ls in enumerate(_lane_blocks(value.shape[1])):
        ref[l, row0:row0 + value.shape[0], :] = value[:, ls]


def _from_lane_blocks(ref):
    return jnp.concatenate([ref[l] for l in range(ref.shape[0])], axis=1)


def _pool_counts(rows, window):
    return jnp.clip(rows - PAD_ROWS + 1, 1, window).astype(F32)


def _trailing_sum(v, window):
    s, sh = v, 1
    while sh < window:
        s = s + pltpu.roll(s, sh, 0)
        sh *= 2
    return s


def _leading_sum(v, window):
    s, sh, n = v, 1, v.shape[0]
    while sh < window:
        s = s + pltpu.roll(s, n - sh, 0)
        sh *= 2
    return s


def _norm_project(h_ref, g_ref, w_t_ref, u_ref, z_ref):
    hv = h_ref[...]
    u = (hv * lax.rsqrt(jnp.mean(hv * hv, axis=-1, keepdims=True) + EPS) * g_ref[...]).astype(BF16)
    u_ref[...] = u
    z_ref[...] = _dot_nt(u, w_t_ref[...])


def _project_back(dz_ref, w_t_ref, h_ref, g_ref, dres_ref, dh_ref, dg_ref, first):
    dx = _dot(dz_ref[...], w_t_ref[...])
    hv = h_ref[...]
    rstd = lax.rsqrt(jnp.mean(hv * hv, axis=-1, keepdims=True) + EPS)
    xh = hv * rstd

    @pl.when(first)
    def _():
        dg_ref[...] = jnp.zeros_like(dg_ref)

    dg_ref[...] += _rowsum(dx * xh)
    dxh = dx * g_ref[...]
    dh_ref[...] = dres_ref[...] + rstd * (dxh - xh * jnp.mean(dxh * xh, axis=-1, keepdims=True))


def _cp_mid_fwd(h, gain, w_in_t, w_out, conv_w, conv_b, ln_g, ln_b, pool_w, pool_scale, *, name):
    t, d = h.shape
    ein = w_in_t.shape[0]
    cd = conv_b.shape[1]
    pd = pool_scale.shape[1]
    pg = pd // len(POOL_WINDOWS)
    rt = _row_tile(t, 320)

    def body(h_ref, g_ref, wi_ref, wo_ref, cw_ref, cb_ref, lg_ref, lb_ref, pw_ref, ps_ref,
             ho_ref, o_ref, z_ref, u_ref, cv_ref, gext, pext, conv_s):
        i = pl.program_id(0)

        @pl.when(i == 0)
        def _():
            _to_lane_blocks(gext, 0, jnp.zeros((HALO, cd), F32))
            pext[0:HALO, :] = jnp.zeros((HALO, pd), F32)

        _norm_project(h_ref, g_ref, wi_ref, u_ref, z_ref)

        _to_lane_blocks(gext, HALO, z_ref[:, 0:cd] * _sigmoid(z_ref[:, cd:2 * cd]))
        pext[HALO:HALO + rt, :] = z_ref[:, 2 * cd:]
        _conv_rows(gext, cw_ref, lambda k: k + HALO - (CONV_WIDTH - 1), conv_s, rt // CHUNK, cd, cb_ref)
        cv = _from_lane_blocks(conv_s)
        cv_ref[...] = cv
        xc = cv - jnp.mean(cv, axis=-1, keepdims=True)
        y = xc * lax.rsqrt(jnp.mean(xc * xc, axis=-1, keepdims=True) + EPS) * lg_ref[...] + lb_ref[...]
        rows = _row_ids(i, rt)
        a = jnp.where(rows >= PAD_ROWS, y * _sigmoid(y), 0.0)
        o_ref[:, 0:cd] = a.astype(BF16)
        for gi, window in enumerate(POOL_WINDOWS):
            ls = slice(gi * pg, (gi + 1) * pg)
            v = pext[:, ls]
            tm = _trailing_sum(v, window)[HALO:] / _pool_counts(rows, window) - v[HALO:]
            p = _dot(tm.astype(BF16), pw_ref[gi]) * ps_ref[:, ls]
            o_ref[:, cd + gi * pg:cd + (gi + 1) * pg] = p.astype(BF16)
        ho_ref[...] = h_ref[...] + _dot(o_ref[...], wo_ref[...])
        gext[:, 0:HALO, :] = gext[:, rt:rt + HALO, :]
        pext[0:HALO, :] = pext[rt:rt + HALO, :]

    nl, lb = len(_lane_blocks(cd)), min(LANE, cd)
    rows = lambda i: (i, 0)
    return pl.pallas_call(
        body, grid=(t // rt,),
        in_specs=[pl.BlockSpec((rt, d), rows), _resident((1, d)), _resident(w_in_t.shape), _resident(w_out.shape),
                  _resident(conv_w.shape), _resident((1, cd)),
                  _resident((1, cd)), _resident((1, cd)), _resident(pool_w.shape), _resident((1, pd))],
        out_specs=[pl.BlockSpec((rt, d), rows), pl.BlockSpec((rt, cd + pd), rows), pl.BlockSpec((rt, ein), rows),
                   pl.BlockSpec((rt, d), rows), pl.BlockSpec((rt, cd), rows)],
        out_shape=[_sds((t, d), F32), _sds((t, cd + pd), BF16), _sds((t, ein), F32), _sds((t, d), BF16),
                   _sds((t, cd), F32)],
        scratch_shapes=[pltpu.VMEM((nl, rt + HALO, lb), F32), pltpu.VMEM((rt + HALO, pd), F32),
                        pltpu.VMEM((nl, rt, lb), F32)],
        compiler_params=_cparams("arbitrary"), name=name)(h, gain, w_in_t, w_out, conv_w, conv_b, ln_g, ln_b, pool_w,
                                                          pool_scale)


def _cp_mid_bwd(z, cv, h, gain, w_in_t, dh, w_out, conv_w, conv_b, ln_g, ln_b, pool_w, pool_scale, *, after=None, name):
    t, ein = z.shape
    cd = conv_b.shape[1]
    pd = pool_scale.shape[1]
    pg = pd // len(POOL_WINDOWS)
    rt = _row_tile(t, 320)
    ntile = t // rt
    per = rt // CHUNK
    dep_specs, deps = _dep_specs(after)

    def body(*refs):
        (z_ref, zh_ref, cv_ref, h_ref, g_ref, wi_ref, dh_ref, wo_ref, cw_ref, cb_ref, lg_ref, lb_ref, pw_ref, ps_ref,
         dhi_ref, dg_ref, dz_ref, dcw_ref, dcb_ref, dlg_ref, dlb_ref, dpw_ref, dps_ref,
         gext, pext, conv_s, dcv, dsp) = refs[len(deps):]
        step = pl.program_id(0)
        tile = ntile - 1 - step
        dcat = _dot_nt(dh_ref[...].astype(BF16), wo_ref[...])

        @pl.when(step == 0)
        def _():
            for ref in (dcw_ref, dcb_ref, dlg_ref, dlb_ref, dpw_ref, dps_ref):
                ref[...] = jnp.zeros_like(ref)
            _to_lane_blocks(dcv, rt, jnp.zeros((HALO, cd), F32))
            dsp[rt:rt + HALO, :] = jnp.zeros((HALO, pd), F32)

        keep = jnp.where(tile > 0, 1.0, 0.0)
        zh = zh_ref[CHUNK - HALO:CHUNK, :]
        _to_lane_blocks(gext, 0, keep * zh[:, 0:cd] * _sigmoid(zh[:, cd:2 * cd]))
        pext[0:HALO, :] = keep * zh[:, 2 * cd:]
        za = z_ref[:, 0:cd]
        sg = _sigmoid(z_ref[:, cd:2 * cd])
        _to_lane_blocks(gext, HALO, za * sg)
        pext[HALO:HALO + rt, :] = z_ref[:, 2 * cd:]
        cv = cv_ref[...]
        xc = cv - jnp.mean(cv, axis=-1, keepdims=True)
        rstd = lax.rsqrt(jnp.mean(xc * xc, axis=-1, keepdims=True) + EPS)
        xh = xc * rstd
        y = xh * lg_ref[...] + lb_ref[...]
        sy = _sigmoid(y)
        rows = _row_ids(tile, rt)
        da = jnp.where(rows >= PAD_ROWS, dcat[:, 0:cd], 0.0)
        dy = da * (sy * (1.0 + y * (1.0 - sy)))
        dlg_ref[...] += _rowsum(dy * xh)
        dlb_ref[...] += _rowsum(dy)
        dxh = dy * lg_ref[...]
        dconv = rstd * (dxh - jnp.mean(dxh, axis=-1, keepdims=True) - xh * jnp.mean(dxh * xh, axis=-1, keepdims=True))
        dcb_ref[...] += _rowsum(dconv)
        _to_lane_blocks(dcv, 0, dconv)
        for l, ls in enumerate(_lane_blocks(cd)):
            def acc_rows(rb, accs, l=l):
                base = pl.multiple_of(rb * CHUNK, CHUNK)
                d_blk = dcv[l, pl.ds(base, CHUNK), :]
                out = []
                for k in range(CONV_WIDTH):
                    prod = d_blk * gext[l, pl.ds(base + k + HALO - (CONV_WIDTH - 1), CHUNK), :]
                    part = prod[0:8]
                    for s in range(8, CHUNK, 8):
                        part = part + prod[s:s + 8]
                    out.append(accs[k] + part)
                return tuple(out)

            zero = jnp.zeros((8, ls.stop - ls.start), F32)
            accs = lax.fori_loop(0, per, acc_rows, (zero,) * CONV_WIDTH)
            for k in range(CONV_WIDTH):
                dcw_ref[k:k + 1, ls] += _rowsum(accs[k])
        _conv_rows(dcv, cw_ref, lambda k: CONV_WIDTH - 1 - k, conv_s, per, cd)
        dglu = _from_lane_blocks(conv_s)
        dz_ref[:, 0:cd] = (dglu * sg).astype(BF16)
        dz_ref[:, cd:2 * cd] = (dglu * za * sg * (1.0 - sg)).astype(BF16)
        dcv[:, rt:rt + HALO, :] = dcv[:, 0:HALO, :]
        for gi, window in enumerate(POOL_WINDOWS):
            ls = slice(gi * pg, (gi + 1) * pg)
            v = pext[:, ls]
            cnt = _pool_counts(rows, window)
            tm = (_trailing_sum(v, window)[HALO:] / cnt - v[HALO:]).astype(BF16)
            dp = dcat[:, cd + gi * pg:cd + (gi + 1) * pg]
            dps_ref[:, ls] += _rowsum(dp * _dot(tm, pw_ref[gi]))
            dpl = (dp * ps_ref[:, ls]).astype(BF16)
            dpw_ref[gi] += _dot_tn(tm, dpl)
            dtm = _dot_nt(dpl, pw_ref[gi])
            dsp[0:rt, ls] = dtm / cnt
            dpin = _leading_sum(dsp[:, ls], window)[0:rt] - dtm
            dz_ref[:, 2 * cd + gi * pg:2 * cd + (gi + 1) * pg] = dpin.astype(BF16)
        dsp[rt:rt + HALO, :] = dsp[0:HALO, :]
        _project_back(dz_ref, wi_ref, h_ref, g_ref, dh_ref, dhi_ref, dg_ref, step == 0)

    d = h.shape[1]
    back = lambda i: (ntile - 1 - i, 0)
    halo_idx = lambda i: (jnp.maximum((ntile - 1 - i) * per - 1, 0), 0)
    const2 = lambda i: (0, 0)
    nl, lb = len(_lane_blocks(cd)), min(LANE, cd)
    return pl.pallas_call(
        body, grid=(ntile,),
        in_specs=dep_specs + [
                  pl.BlockSpec((rt, ein), back), pl.BlockSpec((CHUNK, ein), halo_idx), pl.BlockSpec((rt, cd), back),
                  pl.BlockSpec((rt, d), back),
                  _resident((1, d)), _resident(w_in_t.shape), pl.BlockSpec((rt, d), back), _resident(w_out.shape),
                  _resident(conv_w.shape), _resident((1, cd)), _resident((1, cd)), _resident((1, cd)),
                  _resident(pool_w.shape), _resident((1, pd))],
        out_specs=[pl.BlockSpec((rt, d), back), pl.BlockSpec((1, d), const2),
                   pl.BlockSpec((rt, ein), back), pl.BlockSpec(conv_w.shape, const2), pl.BlockSpec((1, cd), const2),
                   pl.BlockSpec((1, cd), const2), pl.BlockSpec((1, cd), const2),
                   pl.BlockSpec(pool_w.shape, lambda i: (0, 0, 0)), pl.BlockSpec((1, pd), const2)],
        out_shape=[_sds((t, d), F32), _sds((1, d), F32),
                   _sds((t, ein), BF16), _sds(conv_w.shape, F32), _sds((1, cd), F32), _sds((1, cd), F32),
                   _sds((1, cd), F32), _sds(pool_w.shape, F32), _sds((1, pd), F32)],
        scratch_shapes=[pltpu.VMEM((nl, rt + HALO, lb), F32), pltpu.VMEM((rt + HALO, pd), F32), pltpu.VMEM((nl, rt, lb), F32),
                        pltpu.VMEM((nl, rt + HALO, lb), F32), pltpu.VMEM((rt + HALO, pd), F32)],
        compiler_params=_cparams("arbitrary"), name=name)(*deps, z, z, cv, h, gain, w_in_t, dh, w_out, conv_w, conv_b, ln_g,
                                                          ln_b, pool_w, pool_scale)


def _log_decay(r, gw_ref, gb_ref, rows):
    gp = _dot(r.astype(BF16), gw_ref[...]) + gb_ref[...]
    log_sig = jnp.minimum(gp, 0.0) - jnp.log(1.0 + jnp.exp(-jnp.abs(gp)))
    return gp, jnp.where(rows >= PAD_ROWS, log_sig / GATE_NORM, 0.0)


def _tri(strict):
    r = lax.broadcasted_iota(jnp.int32, (CHUNK, CHUNK), 0)
    c = lax.broadcasted_iota(jnp.int32, (CHUNK, CHUNK), 1)
    return jnp.where(c < r if strict else c <= r, 1.0, 0.0).astype(BF16)


def _tri_dot(tri, a):
    hi = a.astype(BF16)
    rest = a - hi.astype(F32)
    mid = rest.astype(BF16)
    lo = (rest - mid.astype(F32)).astype(BF16)
    return _dot(tri, hi) + _dot(tri, mid) + _dot(tri, lo)


def _gla_mid_fwd(h, gain, w_in_t, w_out, gate_w, gate_b, head_g, *, name):
    t = h.shape[0]
    zw = w_in_t.shape[0]
    dk = gate_b.shape[1]
    hv = head_g.shape[1]
    hk = dk // HEADS
    dv = hv * HEADS
    r_at = 2 * dk + 2 * dv
    rt = _row_tile(t, 320)
    per = rt // CHUNK
    scale = hk ** -0.5

    def body(h_ref, g_ref, wi_ref, wo_ref, gw_ref, gb_ref, hg_ref, ho_ref, o_ref, st_ref, z_ref, u_ref,
             s_ref, la_ref, dec_ref):
        i = pl.program_id(0)

        @pl.when(i == 0)
        def _():
            s_ref[...] = jnp.zeros_like(s_ref)

        _norm_project(h_ref, g_ref, wi_ref, u_ref, z_ref)

        _, la = _log_decay(z_ref[:, r_at:r_at + GATE_PAD], gw_ref, gb_ref, _row_ids(i, rt))
        la_ref[...] = la
        tri = _tri(False)

        def chunk_rows(c):
            return slice(c * CHUNK, (c + 1) * CHUNK)

        def decays(c, carry):
            rows = chunk_rows(c)
            la_c = la_ref[rows, :]
            cum = _tri_dot(tri, la_c)
            dec_ref[rows, :] = jnp.exp(_rowsum(la_c) - cum)
            return carry

        def states(c, carry):
            rows = chunk_rows(c)
            etot = jnp.exp(_rowsum(la_ref[rows, :]))
            for hd in range(HEADS):
                ks = slice(hd * hk, (hd + 1) * hk)
                kd = z_ref[rows, dk + hd * hk:dk + (hd + 1) * hk] * dec_ref[rows, ks]
                v = z_ref[rows, 2 * dk + hd * hv:2 * dk + (hd + 1) * hv]
                s_new = s_ref[hd] * etot[:, ks] + _dot_tn(v.astype(BF16), kd.astype(BF16))
                s_ref[hd] = s_new
                st_ref[c, hd] = s_new
            return carry

        def outputs(c, carry):
            rows = chunk_rows(c)
            for hd in range(HEADS):
                q = z_ref[rows, hd * hk:(hd + 1) * hk] * scale
                g = z_ref[rows, 2 * dk + dv + hd * hv:2 * dk + dv + (hd + 1) * hv]
                o = _dot_nt(q.astype(BF16), st_ref[c, hd].astype(BF16))
                on = o * lax.rsqrt(jnp.mean(o * o, axis=-1, keepdims=True) + EPS) * hg_ref[...]
                o_ref[rows, hd * hv:(hd + 1) * hv] = (on * (g * _sigmoid(g))).astype(BF16)
            return carry

        for phase in (decays, states, outputs):
            for c in range(per):
                phase(c, 0)
        ho_ref[...] = h_ref[...] + _dot(o_ref[...], wo_ref[...])

    d = h.shape[1]
    rows = lambda i: (i, 0)
    return pl.pallas_call(
        body, grid=(t // rt,),
        in_specs=[pl.BlockSpec((rt, d), rows), _resident((1, d)), _resident(w_in_t.shape), _resident(w_out.shape),
                  _resident(gate_w.shape), _resident((1, dk)), _resident((1, hv))],
        out_specs=[pl.BlockSpec((rt, d), rows), pl.BlockSpec((rt, dv), rows),
                   pl.BlockSpec((per, HEADS, hv, hk), lambda i: (i, 0, 0, 0)), pl.BlockSpec((rt, zw), rows),
                   pl.BlockSpec((rt, d), rows)],
        out_shape=[_sds((t, d), F32), _sds((t, dv), BF16), _sds((t // CHUNK, HEADS, hv, hk), F32), _sds((t, zw), F32),
                   _sds((t, d), BF16)],
        scratch_shapes=[pltpu.VMEM((HEADS, hv, hk), F32), pltpu.VMEM((rt, dk), F32), pltpu.VMEM((rt, dk), F32)],
        compiler_params=_cparams("arbitrary"), name=name)(h, gain, w_in_t, w_out, gate_w, gate_b, head_g)


def _gla_mid_bwd(z, h, gain, w_in_t, dh, w_out, states, gate_w, gate_b, head_g, *, after=None, name):
    t = z.shape[0]
    dk = gate_b.shape[1]
    hv = head_g.shape[1]
    hk = dk // HEADS
    dv = hv * HEADS
    r_at = 2 * dk + 2 * dv
    rt = _row_tile(t, 320)
    ntile = t // rt
    per = rt // CHUNK
    scale = hk ** -0.5
    dep_specs, deps = _dep_specs(after)

    def body(*refs):
        (z_ref, h_ref, g_ref, wi_ref, dh_ref, wo_ref, st_ref, stp_ref, gw_ref, gb_ref, hg_ref,
         dhi_ref, dg_ref, dz_ref, dgw_ref, dgb_ref, dhg_ref,
         ds_ref, la_ref, dla_ref, dec_ref, dos_ref, e_ref, do_ref) = refs[len(deps):]
        step = pl.program_id(0)
        tile = ntile - 1 - step
        do_ref[...] = _dot_nt(dh_ref[...].astype(BF16), wo_ref[...])

        @pl.when(step == 0)
        def _():
            ds_ref[...] = jnp.zeros_like(ds_ref)
            dgw_ref[...] = jnp.zeros_like(dgw_ref)
            dgb_ref[...] = jnp.zeros_like(dgb_ref)
            dhg_ref[...] = jnp.zeros_like(dhg_ref)

        rows_id = _row_ids(tile, rt)
        r = z_ref[:, r_at:r_at + GATE_PAD]
        gp, la = _log_decay(r, gw_ref, gb_ref, rows_id)
        la_ref[...] = la
        tri, tri_strict = _tri(False), _tri(True)
        keep = jnp.where(tile > 0, 1.0, 0.0)

        def chunk_rows(c):
            return slice(c * CHUNK, (c + 1) * CHUNK)

        def recompute(c, dhg):
            rows = chunk_rows(c)
            la_c = la_ref[rows, :]
            cum = _tri_dot(tri, la_c)
            dec_ref[rows, :] = jnp.exp(_rowsum(la_c) - cum)
            for hd in range(HEADS):
                q = (z_ref[rows, hd * hk:(hd + 1) * hk] * scale).astype(BF16)
                g = z_ref[rows, 2 * dk + dv + hd * hv:2 * dk + dv + (hd + 1) * hv]
                s_b = st_ref[c, hd].astype(BF16)
                o = _dot_nt(q, s_b)
                rstd = lax.rsqrt(jnp.mean(o * o, axis=-1, keepdims=True) + EPS)
                oh = o * rstd
                sg = _sigmoid(g)
                d_og = do_ref[rows, hd * hv:(hd + 1) * hv]
                dz_ref[rows, 2 * dk + dv + hd * hv:2 * dk + dv + (hd + 1) * hv] = (
                    d_og * oh * hg_ref[...] * (sg * (1.0 + g * (1.0 - sg)))).astype(BF16)
                don = d_og * (g * sg)
                dhg = dhg + _rowsum(don * oh)
                doh = don * hg_ref[...]
                d_o = (rstd * (doh - oh * jnp.mean(doh * oh, axis=-1, keepdims=True))).astype(BF16)
                dos_ref[rows, hd * hv:(hd + 1) * hv] = d_o
                dz_ref[rows, hd * hk:(hd + 1) * hk] = (_dot(d_o, s_b) * scale).astype(BF16)
            return dhg

        def recurrence(cc, carry):
            c = per - 1 - cc
            rows = chunk_rows(c)
            etot = jnp.exp(_rowsum(la_ref[rows, :]))
            for hd in range(HEADS):
                ks = slice(hd * hk, (hd + 1) * hk)
                q = (z_ref[rows, hd * hk:(hd + 1) * hk] * scale).astype(BF16)
                dec = dec_ref[rows, ks]
                kd = z_ref[rows, dk + hd * hk:dk + (hd + 1) * hk] * dec
                v = z_ref[rows, 2 * dk + hd * hv:2 * dk + (hd + 1) * hv].astype(BF16)
                s_prev = st_ref[c - 1, hd] if c > 0 else keep * stp_ref[0, hd]
                ds_t = ds_ref[hd] + _dot_tn(dos_ref[rows, hd * hv:(hd + 1) * hv], q)
                ds_b = ds_t.astype(BF16)
                dkd = _dot(v, ds_b)
                dz_ref[rows, 2 * dk + hd * hv:2 * dk + (hd + 1) * hv] = _dot_nt(kd.astype(BF16), ds_b).astype(BF16)
                dtot = etot[:, ks] * _rowsum(ds_t * s_prev)
                ds_ref[hd] = ds_t * etot[:, ks]
                dz_ref[rows, dk + hd * hk:dk + (hd + 1) * hk] = (dkd * dec).astype(BF16)
                e_ref[rows, ks] = dkd * kd
                dla_ref[rows, ks] = jnp.broadcast_to(dtot, (CHUNK, hk))
            return carry

        def decay_cotangent(c, carry):
            rows = chunk_rows(c)
            dla_ref[rows, :] += _tri_dot(tri_strict, e_ref[rows, :])
            return carry

        dhg = jnp.zeros((1, hv), F32)
        for c in range(per):
            dhg = recompute(c, dhg)
        dhg_ref[...] += dhg
        for phase in (recurrence, decay_cotangent):
            for c in range(per):
                phase(c, 0)
        dla = jnp.where(rows_id >= PAD_ROWS, dla_ref[...], 0.0)
        dgp = dla * (1.0 / GATE_NORM) * (1.0 - _sigmoid(gp))
        dgb_ref[...] += _rowsum(dgp)
        dgp_b = dgp.astype(BF16)
        dgw_ref[...] += _dot_tn(r.astype(BF16), dgp_b)
        dz_ref[:, r_at:r_at + GATE_PAD] = _dot_nt(dgp_b, gw_ref[...]).astype(BF16)
        _project_back(dz_ref, wi_ref, h_ref, g_ref, dh_ref, dhi_ref, dg_ref, step == 0)

    d = h.shape[1]
    back = lambda i: (ntile - 1 - i, 0)
    const2 = lambda i: (0, 0)
    return pl.pallas_call(
        body, grid=(ntile,),
        in_specs=dep_specs + [
                  pl.BlockSpec((rt, z.shape[1]), back), pl.BlockSpec((rt, d), back), _resident((1, d)),
                  _resident(w_in_t.shape), pl.BlockSpec((rt, d), back), _resident(w_out.shape),
                  pl.BlockSpec((per, HEADS, hv, hk), lambda i: (ntile - 1 - i, 0, 0, 0)),
                  pl.BlockSpec((1, HEADS, hv, hk), lambda i: (jnp.maximum((ntile - 1 - i) * per - 1, 0), 0, 0, 0)),
                  _resident(gate_w.shape), _resident((1, dk)), _resident((1, hv))],
        out_specs=[pl.BlockSpec((rt, d), back), pl.BlockSpec((1, d), const2), pl.BlockSpec((rt, z.shape[1]), back),
                   pl.BlockSpec(gate_w.shape, const2), pl.BlockSpec((1, dk), const2), pl.BlockSpec((1, hv), const2)],
        out_shape=[_sds((t, d), F32), _sds((1, d), F32), _sds(z.shape, BF16), _sds(gate_w.shape, F32),
                   _sds((1, dk), F32), _sds((1, hv), F32)],
        scratch_shapes=[pltpu.VMEM((HEADS, hv, hk), F32), pltpu.VMEM((rt, dk), F32), pltpu.VMEM((rt, dk), F32),
                        pltpu.VMEM((rt, dk), F32), pltpu.VMEM((rt, dv), BF16), pltpu.VMEM((rt, dk), F32),
                        pltpu.VMEM((rt, dv), F32)],
        compiler_params=_cparams("arbitrary"), name=name)(*deps, z, h, gain, w_in_t, dh, w_out, states, states, gate_w,
                                                          gate_b, head_g)


def _adamw_math(w, g, m, v):
    m = ADAM_B1 * m + (1.0 - ADAM_B1) * g
    v = ADAM_B2 * v + (1.0 - ADAM_B2) * (g * g)
    m_hat = m / (1.0 - ADAM_B1 ** ADAM_STEP)
    v_hat = v / (1.0 - ADAM_B2 ** ADAM_STEP)
    return -ADAM_LR * (m_hat / (jnp.sqrt(v_hat) + ADAM_EPS) + ADAM_WD * w), m, v


N_CHIP = N_DEV // 2
BLOCK_ELEMS = 128 * 1024


def _my_slot():
    return 4 * lax.axis_index("x") + 2 * lax.axis_index("y") + lax.axis_index("c")


def _row_block(r, c):
    cap = max(8, BLOCK_ELEMS // (-(-c // LANE) * LANE))
    return max([b for b in range(8, r + 1, 8) if r % b == 0 and b <= cap] or [r])


def _blocks(r, c):
    rb = _row_block(r, c)
    if rb < r or r * c <= BLOCK_ELEMS:
        return rb, c
    return r, max([b for b in (512, 256, LANE) if c % b == 0 and r * b <= BLOCK_ELEMS] or [c])


def _reduce_adam(parts, w, m, v, *, after=None, name):
    nl, r, c = w.shape
    rb, cb = _blocks(r, c)
    dep_specs, deps = _dep_specs(after)

    def body(*refs):
        me = refs[0][0]
        refs = refs[1 + len(deps):]
        p_refs = refs[:2 * nl]
        w_ref, m_ref, v_ref, g_out, d_out, m_out, v_out = refs[2 * nl:]
        layer = pl.program_id(0)
        for li in range(nl):
            @pl.when(layer == li)
            def _(li=li):
                own_ref, land_ref = p_refs[2 * li], p_refs[2 * li + 1]
                mine = own_ref[...].astype(F32)
                g = None
                for dev in range(N_DEV):
                    term = jnp.where(me == dev, mine, land_ref[dev].astype(F32))
                    g = term if g is None else g + term
                g_out[...] = g
                d_out[...], m_out[...], v_out[...] = _adamw_math(w_ref[...], g, m_ref[...], v_ref[...])

    blk = pl.BlockSpec((None, rb, cb), lambda l, i, j, me: (l, i, j))
    p_specs = []
    for li in range(nl):
        p_specs += [
            pl.BlockSpec((None, rb, cb), lambda l, i, j, me, li=li: (me[0], jnp.where(l == li, i, 0), jnp.where(l == li, j, 0))),
            pl.BlockSpec((N_DEV, rb, cb), lambda l, i, j, me, li=li: (0, jnp.where(l == li, i, 0), jnp.where(l == li, j, 0)))]
    flat = [p for pair in parts for p in pair]
    grid_spec = pltpu.PrefetchScalarGridSpec(
        num_scalar_prefetch=1, grid=(nl, r // rb, c // cb), in_specs=dep_specs + p_specs + [blk, blk, blk],
        out_specs=[blk] * 4)
    return pl.pallas_call(
        body, grid_spec=grid_spec, out_shape=[_sds(w.shape, F32)] * 4,
        compiler_params=_cparams("arbitrary", "arbitrary", "arbitrary"), name=name)(
        _my_slot().reshape(1), *deps, *flat, w, m, v)


def _sum8(own, landed, *, name):
    def body(own_ref, land_ref, o_ref):
        me = _my_slot()
        total = None
        for dev in range(N_DEV):
            term = jnp.where(me == dev, own_ref[...], land_ref[dev])
            total = term if total is None else total + term
        o_ref[...] = total

    return pl.pallas_call(body, out_shape=_sds(own.shape, F32), name=name)(own, landed)


def _adam_small(own, landed, split, w, m, v, *, name):
    n = len(w)

    def body(*refs):
        own_refs, land_refs, w_refs, m_refs, v_refs = (refs[k * n:(k + 1) * n] for k in range(5))
        outs = refs[5 * n:]
        me = _my_slot()
        for k in range(n):
            mine = own_refs[k][me] if split[k] else own_refs[k][...]
            g = None
            for dev in range(N_DEV):
                term = jnp.where(me == dev, mine, land_refs[k][dev])
                g = term if g is None else g + term
            outs[4 * k][...] = g
            outs[4 * k + 1][...], outs[4 * k + 2][...], outs[4 * k + 3][...] = _adamw_math(
                w_refs[k][...], g, m_refs[k][...], v_refs[k][...])

    out = pl.pallas_call(body, out_shape=[_sds(a.shape, F32) for a in w for _ in range(4)],
                         compiler_params=pltpu.CompilerParams(vmem_limit_bytes=V7X_VMEM_LIMIT), name=name)(
        *own, *landed, *w, *m, *v)
    return [tuple(out[4 * k:4 * k + 4]) for k in range(n)]


_HBM = pl.BlockSpec(memory_space=pltpu.HBM)
_SEM = pl.BlockSpec(memory_space=pltpu.SEMAPHORE)
_DATAFLOW = pltpu.SideEffectType.DATAFLOW_SIDE_EFFECTING


def _plan_to_all(src, land):
    x, y, c = lax.axis_index("x"), lax.axis_index("y"), lax.axis_index("c")
    return [(src, land.at[_my_slot()], (x ^ ((d >> 2) & 1), y ^ ((d >> 1) & 1), c ^ (d & 1))) for d in range(1, N_DEV)]


def _plan_split_to_all(src, land):
    x, y, c = lax.axis_index("x"), lax.axis_index("y"), lax.axis_index("c")
    peers = [(x ^ ((d >> 2) & 1), y ^ ((d >> 1) & 1), c ^ (d & 1)) for d in range(1, N_DEV)]
    return [(src.at[4 * px + 2 * py + pc], land.at[_my_slot()], (px, py, pc)) for px, py, pc in peers]


_PLAN_COPIES = {_plan_to_all: N_DEV - 1, _plan_split_to_all: N_DEV - 1}


def _plans(plan, n):
    return list(plan) if isinstance(plan, (list, tuple)) else [plan] * n


def _exchange_copies(plan, ins, lands, send, recv):
    copies, sem = [], 0
    for p, src, land in zip(_plans(plan, len(lands)), ins, lands):
        for s, dst, dev in p(src, land):
            copies.append(pltpu.make_async_remote_copy(
                src_ref=s, dst_ref=dst, send_sem=send.at[sem], recv_sem=recv.at[sem],
                device_id=dev, device_id_type=pl.DeviceIdType.MESH))
            sem += 1
    return copies


def _place_own(a, dtype, *, after=None, name):
    r, c = a.shape
    rb = _row_block(r, c)
    dep_specs, deps = _dep_specs(after)

    def body(*refs):
        a_ref, o_ref = refs[1 + len(deps):]
        o_ref[...] = a_ref[...].astype(dtype)

    grid_spec = pltpu.PrefetchScalarGridSpec(
        num_scalar_prefetch=1, grid=(r // rb,), in_specs=dep_specs + [pl.BlockSpec((rb, c), lambda i, me: (i, 0))],
        out_specs=pl.BlockSpec((None, rb, c), lambda i, me: (me[0], i, 0)))
    return pl.pallas_call(body, grid_spec=grid_spec, out_shape=_sds((N_DEV, r, c), dtype),
                          compiler_params=_cparams("arbitrary"), name=name)(_my_slot().reshape(1), *deps, a)


def _plan_gather_first(land, _):
    x, y, c = lax.axis_index("x"), lax.axis_index("y"), lax.axis_index("c")
    mine = land.at[_my_slot()]
    return [(mine, mine, (x, y, 1 - c))] + [(mine, mine, (x ^ (d >> 1), y ^ (d & 1), c)) for d in range(1, N_CHIP)]


def _plan_gather_relay(land, _):
    x, y, c = lax.axis_index("x"), lax.axis_index("y"), lax.axis_index("c")
    slots = [land.at[4 * (x ^ (d >> 1)) + 2 * (y ^ (d & 1)) + c] for d in range(1, N_CHIP)]
    return [(s, s, (x, y, 1 - c)) for s in slots]


def _plan_gather_direct(land, _):
    return _plan_to_all(land.at[_my_slot()], land)


_PLAN_COPIES[_plan_gather_first] = N_CHIP
_PLAN_COPIES[_plan_gather_relay] = N_CHIP - 1
_PLAN_COPIES[_plan_gather_direct] = N_DEV - 1


def _exchange_start(plan, arrs, lands, *, after=None, name):
    bufs = list(lands) if arrs is None else list(arrs) + list(lands)
    n, nb = len(lands), len(bufs)
    nsem = sum(_PLAN_COPIES[p] for p in _plans(plan, n))
    dep_specs, deps = _dep_specs(after)

    def body(*refs):
        ins, land_refs = refs[:n], refs[nb - n:nb]
        send, recv = refs[nb + len(deps)], refs[nb + len(deps) + 1]
        for cp in _exchange_copies(plan, ins, land_refs, send, recv):
            cp.start()
        refs[-1][...] = jnp.zeros_like(refs[-1])

    out = pl.pallas_call(
        body, name=name,
        out_shape=(pltpu.SemaphoreType.DMA((nsem,)), pltpu.SemaphoreType.DMA((nsem,)),
                   *[pltpu.HBM(a.shape, a.dtype) for a in bufs], _sds((8, LANE), F32)),
        in_specs=[_HBM] * nb + dep_specs,
        out_specs=(_SEM, _SEM, *([_HBM] * nb), pl.BlockSpec(memory_space=pltpu.VMEM)),
        input_output_aliases={i: 2 + i for i in range(nb)},
        compiler_params=pltpu.CompilerParams(has_side_effects=_DATAFLOW),
    )(*[pltpu.with_memory_space_constraint(a, pltpu.HBM) for a in bufs], *deps)
    return (plan, n, out[0], out[1], list(out[2:2 + nb])), out[-1]


def _exchange_now(plan, lands, *, name):
    n = len(lands)
    nsem = sum(_PLAN_COPIES[p] for p in _plans(plan, n))

    def body(*refs):
        land_refs, send, recv = refs[n:2 * n], refs[2 * n], refs[2 * n + 1]
        copies = _exchange_copies(plan, land_refs, land_refs, send, recv)
        for cp in copies:
            cp.start()
        for cp in copies:
            cp.wait_send()
            cp.wait_recv()

    hbm = pl.BlockSpec(memory_space=pl.ANY)
    return pl.pallas_call(
        body, in_specs=[hbm] * n, out_specs=[hbm] * n, out_shape=[_sds(a.shape, a.dtype) for a in lands],
        input_output_aliases={i: i for i in range(n)},
        scratch_shapes=[pltpu.SemaphoreType.DMA((nsem,)), pltpu.SemaphoreType.DMA((nsem,))], name=name)(*lands)


def _exchange_wait(state, after, *, name):
    plan, n, send_sem, recv_sem, bufs = state
    nb = len(bufs)
    after = list(after) if isinstance(after, (list, tuple)) else [after]

    def body(*refs):
        ins, land_refs, send, recv = refs[:n], refs[nb - n:nb], refs[nb], refs[nb + 1]
        for cp in _exchange_copies(plan, ins, land_refs, send, recv):
            cp.wait_send()
            cp.wait_recv()

    out = pl.pallas_call(
        body, name=name, out_shape=[pltpu.HBM(a.shape, a.dtype) for a in bufs],
        in_specs=[_HBM] * nb + [_SEM, _SEM] + [pl.BlockSpec(memory_space=pl.ANY)] * len(after), out_specs=[_HBM] * nb,
        input_output_aliases={i: i for i in range(nb)},
        compiler_params=pltpu.CompilerParams(has_side_effects=_DATAFLOW),
    )(*bufs, send_sem, recv_sem, *after)
    return list(out[:n]), list(out[nb - n:])


def _dep_specs(after):
    return ([], []) if after is None else ([pl.BlockSpec(memory_space=pl.ANY)], [after])


def _undo_column_split(g):
    return jnp.transpose(g, (1, 0, 2)).reshape(g.shape[1], N_DEV * g.shape[2])


def _column_split(a):
    r, c = a.shape
    return jnp.transpose(a.reshape(r, N_DEV, c // N_DEV), (1, 0, 2))


class _WholeWeights:
    def __init__(self, groups):
        self.groups = groups
        self.grads = {}

    def fetch(self, group, after):
        return self.groups[group]

    def emit(self, group, grads):
        self.grads.update(grads)
        return None


def _local_step(x, target, replicated, src):
    d = x.shape[1]
    mix_g, ffn_g = replicated["mix_g"], replicated["ffn_g"]
    h0 = jnp.concatenate([jnp.zeros((CHUNK, d), F32), x], axis=0)
    tgt = jnp.concatenate([jnp.zeros((CHUNK, d), F32), target], axis=0)
    cp = src.fetch("cp", [h0, tgt])
    h0 = lax.dynamic_update_slice(h0, cp["meta"], (PAD_ROWS, 0))
    cp_mid = (cp["conv_w"], replicated["conv_b"], replicated["ln_g"], replicated["ln_b"], replicated["pool_w"],
              replicated["pool_scale"])

    h1, cat, z0, u0, cv0 = _cp_mid_fwd(h0, mix_g[0:1], cp["cp_w_in_t"], cp["cp_w_out"], *cp_mid, name="cp_mixer")
    ffn0 = src.fetch("ffn0", h1)
    h2, uf0, rf0 = _ffn_fwd(h1, ffn_g[0:1], ffn0["w1"], ffn0["w2"], name="ffn0")
    gla = src.fetch("gla", h2)
    gla_mid = (gla["gate_w"], gla["gate_b"], gla["head_g"])
    h3, og, states, z1, u1 = _gla_mid_fwd(h2, mix_g[1:2], gla["gla_w_in_t"], gla["gla_w_out"], *gla_mid, name="gla_mixer")
    ffn1 = src.fetch("ffn1", h3)
    dh4, uf1, rf1, loss, d_final_g = _ffn_fwd(h3, ffn_g[1:2], ffn1["w1"], ffn1["w2"],
                                              loss_head=(replicated["final_g"], tgt), name="ffn1_loss")

    dh3, dhh1, dob1, dffn_g1 = _ffn_bwd_x(h3, dh4, ffn_g[1:2], rf1, ffn1["w1"], ffn1["w2"], name="ffn1_bwd_x")
    dw1_1, dw2_1 = _ffn_bwd_w(uf1, dhh1, rf1, dob1, name="ffn1_bwd_w")
    sent = src.emit("ffn1", dict(w1=dw1_1, w2=dw2_1))
    d_gla_w_out = _linear_bwd_w(og, dh3, name="gla_out_dw")
    dh2, dmix_g1, dz1, d_gate_w, d_gate_b, d_head_g = _gla_mid_bwd(
        z1, h2, mix_g[1:2], gla["gla_w_in_t"], dh3, gla["gla_w_out"], states, *gla_mid, after=sent, name="gla_mixer_bwd")
    d_gla_w_in_t = _linear_bwd_w(dz1, u1, name="gla_in_dw")
    sent = src.emit("gla", dict(gla_w_in_t=d_gla_w_in_t, gla_w_out=d_gla_w_out))
    dh1, dhh0, dob0, dffn_g0 = _ffn_bwd_x(h1, dh2, ffn_g[0:1], rf0, ffn0["w1"], ffn0["w2"], after=sent, name="ffn0_bwd_x")
    dw1_0, dw2_0 = _ffn_bwd_w(uf0, dhh0, rf0, dob0, name="ffn0_bwd_w")
    src.emit("ffn0", dict(w1=dw1_0, w2=dw2_0))
    d_cp_w_out = _linear_bwd_w(cat, dh1, name="cp_out_dw")
    sent = src.emit("cp_out", dict(cp_w_out=d_cp_w_out))
    dh0, dmix_g0, dz0, d_conv_w, d_conv_b, d_ln_g, d_ln_b, d_pool_w, d_pool_scale = _cp_mid_bwd(
        z0, cv0, h0, mix_g[0:1], cp["cp_w_in_t"], dh1, cp["cp_w_out"], *cp_mid, after=sent, name="cp_mixer_bwd")
    d_cp_w_in_t = _linear_bwd_w(dz0, u0, name="cp_in_dw")

    small = dict(
        mix_g=jnp.concatenate([dmix_g0, dmix_g1]), ffn_g=jnp.concatenate([dffn_g0, dffn_g1]), conv_b=d_conv_b, ln_g=d_ln_g,
        ln_b=d_ln_b, pool_w=d_pool_w, pool_scale=d_pool_scale, final_g=d_final_g, meta=dh0[PAD_ROWS:CHUNK], conv_w=d_conv_w,
        gate_w=d_gate_w, gate_b=d_gate_b, head_g=d_head_g)
    src.emit("cp", dict(cp_w_in_t=d_cp_w_in_t, small=small, loss=loss))
    return loss, dh0[CHUNK:], small


_REPLICATED = ("mix_norm_g", "ffn_norm_g", "cp_conv_b", "cp_ln_g", "cp_ln_b", "cp_pool_w", "cp_pool_scale", "final_norm_g")
_SMALL_SHARDED = ("meta_tokens", "cp_conv_w", "gla_gate_w2", "gla_gate_b", "gla_head_g")
_NAMES = ("meta_tokens", "mix_norm_g", "ffn_norm_g", "ffn_w1", "ffn_w2", "cp_w_in", "cp_conv_w", "cp_conv_b", "cp_ln_g",
          "cp_ln_b", "cp_pool_w", "cp_pool_scale", "cp_w_out", "gla_w_in", "gla_gate_w2", "gla_gate_b", "gla_head_g",
          "gla_w_out", "final_norm_g")
_SMALL_GRADS = ("mix_g", "ffn_g", "conv_b", "ln_g", "ln_b", "pool_w", "pool_scale", "final_g", "meta", "conv_w", "gate_w",
                "gate_b", "head_g")
_GROUPS = ("cp", "ffn0", "gla", "ffn1")
_TWO_LEG_GATHERS = ("cp", "ffn0", "ffn1")


class _Exchanges:
    def __init__(self, w, d):
        self.d = d
        small = [w[n].reshape(w[n].shape[-2:]) for n in _SMALL_SHARDED]
        self.small_shard_shapes = [w[n].shape for n in _SMALL_SHARDED]
        shards = dict(
            cp=[(w["cp_w_in"][0].T, BF16), (w["cp_w_out"][0], BF16)] + [(a, F32) for a in small],
            ffn0=[(w["ffn_w1"][0], BF16), (w["ffn_w2"][0], BF16)],
            gla=[(w["gla_w_in"][0].T, BF16), (w["gla_w_out"][0], BF16)],
            ffn1=[(w["ffn_w1"][1], BF16), (w["ffn_w2"][1], BF16)])
        self.gathers = {}
        self.sent = {}
        token = None
        for group in _GROUPS:
            lands = [_place_own(a, dtype, after=token, name=f"place_w_{group}_{k}")
                     for k, (a, dtype) in enumerate(shards[group])]
            plan = _plan_gather_first if group in _TWO_LEG_GATHERS else _plan_gather_direct
            self.gathers[group], token = _exchange_start(plan, None, lands, after=token, name=f"start_w_{group}")
        self.token = token

    def fetch(self, group, after):
        d = self.d
        after = (list(after) if isinstance(after, (list, tuple)) else [after]) + [self.token]
        _, got = _exchange_wait(self.gathers[group], after, name=f"wait_w_{group}")
        if group in _TWO_LEG_GATHERS:
            got = _exchange_now(_plan_gather_relay, got, name=f"relay_w_{group}")
        if group in ("ffn0", "ffn1"):
            return dict(w1=got[0], w2=got[1])
        if group == "gla":
            w_in_t = jnp.pad(got[0].reshape(-1, d), ((0, GATE_PAD - GATE_RANK), (0, 0)))
            return dict(gla_w_in_t=w_in_t, gla_w_out=got[1].reshape(d, d), gate_w=self.gate_w, gate_b=self.gate_b,
                        head_g=self.head_g)
        meta, conv_w, gate_w, self.gate_b, self.head_g = [_undo_column_split(a) for a in got[2:]]
        self.gate_w = jnp.pad(gate_w, ((0, GATE_PAD - GATE_RANK), (0, 0))).astype(BF16)
        return dict(cp_w_in_t=got[0].reshape(-1, d), cp_w_out=got[1].reshape(d, d), meta=meta,
                    conv_w=jnp.pad(conv_w, ((0, 1), (0, 0))))

    def emit(self, group, g):
        d = self.d
        if group in ("ffn0", "ffn1"):
            arrs = [g["w1"], g["w2"]]
        elif group == "gla":
            w_in_t = g["gla_w_in_t"][:3 * d + GATE_RANK]
            arrs = [w_in_t.reshape(N_DEV, -1, d), g["gla_w_out"].reshape(N_DEV, d // N_DEV, d)]
        elif group == "cp_out":
            arrs = [g["cp_w_out"].reshape(N_DEV, d // N_DEV, d)]
        else:
            s = dict(g["small"])
            s.update(pool_w=s["pool_w"][None], conv_w=s["conv_w"][:CONV_WIDTH], gate_w=s["gate_w"][:GATE_RANK])
            own = [s[n] for n in _SMALL_GRADS[:len(_REPLICATED)]]
            own += [_column_split(s[n]).reshape((N_DEV,) + shape)
                    for n, shape in zip(_SMALL_GRADS[len(_REPLICATED):], self.small_shard_shapes)]
            plans = [_plan_to_all] * len(_REPLICATED) + [_plan_split_to_all] * len(_SMALL_SHARDED)
            lands = [lax.empty((N_DEV,) + a.shape, F32) for a in own[:len(_REPLICATED)]]
            lands += [lax.empty(a.shape, F32) for a in own[len(_REPLICATED):]]
            own.append(g["loss"])
            plans.append(_plan_to_all)
            lands.append(lax.empty((N_DEV,) + g["loss"].shape, F32))
            self.small_sent, self.token = _exchange_start(plans, own, lands, after=self.token, name="start_g_small")
            arrs = [g["cp_w_in_t"].reshape(N_DEV, -1, d)]
        self.sent[group], self.token = _exchange_start(_plan_split_to_all, arrs, [lax.empty(a.shape, a.dtype) for a in arrs],
                                                       after=self.token, name=f"start_g_{group}")
        return self.token

    def finish(self, w, mom, var):
        out = {}
        after = self.token

        def landed(group):
            own, got = _exchange_wait(self.sent[group], after, name=f"wait_g_{group}")
            return list(zip(own, got))

        def adam(n, parts, behind=None, transposed=False):
            flip = (lambda a: jnp.transpose(a, (0, 2, 1))) if transposed else (lambda a: a)
            res = _reduce_adam(parts, flip(w[n]), flip(mom[n]), flip(var[n]), after=behind, name=f"adam_{n}")
            out[n] = tuple(flip(a) for a in res)
            return res[0]

        ffn1 = landed("ffn1")
        after = ffn1[0][1]
        gla = landed("gla")
        after = gla[0][1]
        ffn0 = landed("ffn0")
        after = adam("ffn_w1", [ffn0[0], ffn1[0]])
        after = adam("ffn_w2", [ffn0[1], ffn1[1]], after)
        after = adam("gla_w_in", [gla[0]], after, transposed=True)
        after = adam("gla_w_out", [gla[1]], after)
        small_own, small_landed = _exchange_wait(self.small_sent, after, name="wait_g_small")
        names = _REPLICATED + _SMALL_SHARDED
        split = [False] * len(_REPLICATED) + [True] * len(_SMALL_SHARDED)
        small_new = _adam_small(small_own[:-1], small_landed[:-1], split, [w[n] for n in names], [mom[n] for n in names],
                                [var[n] for n in names], name="adam_small")
        out.update(zip(names, small_new))
        out["loss"] = _sum8(small_own[-1], small_landed[-1], name="sum_loss")[0, 0]

        after = small_new[0][0]
        cp_out = landed("cp_out")
        after = adam("cp_w_out", [cp_out[0]])
        cp = landed("cp")
        adam("cp_w_in", [cp[0]], transposed=True)
        return out


def kernel(x, meta_tokens, mix_norm_g, ffn_norm_g, ffn_w1, ffn_w2, cp_w_in, cp_conv_w, cp_conv_b, cp_ln_g, cp_ln_b, cp_pool_w, cp_pool_scale, cp_w_out, gla_w_in, gla_gate_w2, gla_gate_b, gla_head_g, gla_w_out, final_norm_g, loss_target, m_meta_tokens, m_mix_norm_g, m_ffn_norm_g, m_ffn_w1, m_ffn_w2, m_cp_w_in, m_cp_conv_w, m_cp_conv_b, m_cp_ln_g, m_cp_ln_b, m_cp_pool_w, m_cp_pool_scale, m_cp_w_out, m_gla_w_in, m_gla_gate_w2, m_gla_gate_b, m_gla_head_g, m_gla_w_out, m_final_norm_g, v_meta_tokens, v_mix_norm_g, v_ffn_norm_g, v_ffn_w1, v_ffn_w2, v_cp_w_in, v_cp_conv_w, v_cp_conv_b, v_cp_ln_g, v_cp_ln_b, v_cp_pool_w, v_cp_pool_scale, v_cp_w_out, v_gla_w_in, v_gla_gate_w2, v_gla_gate_b, v_gla_head_g, v_gla_w_out, v_final_norm_g):
    w = dict(meta_tokens=meta_tokens, mix_norm_g=mix_norm_g, ffn_norm_g=ffn_norm_g, ffn_w1=ffn_w1, ffn_w2=ffn_w2,
             cp_w_in=cp_w_in, cp_conv_w=cp_conv_w, cp_conv_b=cp_conv_b, cp_ln_g=cp_ln_g, cp_ln_b=cp_ln_b,
             cp_pool_w=cp_pool_w, cp_pool_scale=cp_pool_scale, cp_w_out=cp_w_out, gla_w_in=gla_w_in,
             gla_gate_w2=gla_gate_w2, gla_gate_b=gla_gate_b, gla_head_g=gla_head_g, gla_w_out=gla_w_out,
             final_norm_g=final_norm_g.reshape(1, -1))
    mom = dict(meta_tokens=m_meta_tokens, mix_norm_g=m_mix_norm_g, ffn_norm_g=m_ffn_norm_g, ffn_w1=m_ffn_w1, ffn_w2=m_ffn_w2,
               cp_w_in=m_cp_w_in, cp_conv_w=m_cp_conv_w, cp_conv_b=m_cp_conv_b, cp_ln_g=m_cp_ln_g, cp_ln_b=m_cp_ln_b,
               cp_pool_w=m_cp_pool_w, cp_pool_scale=m_cp_pool_scale, cp_w_out=m_cp_w_out, gla_w_in=m_gla_w_in,
               gla_gate_w2=m_gla_gate_w2, gla_gate_b=m_gla_gate_b, gla_head_g=m_gla_head_g, gla_w_out=m_gla_w_out,
               final_norm_g=m_final_norm_g.reshape(1, -1))
    var = dict(meta_tokens=v_meta_tokens, mix_norm_g=v_mix_norm_g, ffn_norm_g=v_ffn_norm_g, ffn_w1=v_ffn_w1, ffn_w2=v_ffn_w2,
               cp_w_in=v_cp_w_in, cp_conv_w=v_cp_conv_w, cp_conv_b=v_cp_conv_b, cp_ln_g=v_cp_ln_g, cp_ln_b=v_cp_ln_b,
               cp_pool_w=v_cp_pool_w, cp_pool_scale=v_cp_pool_scale, cp_w_out=v_cp_w_out, gla_w_in=v_gla_w_in,
               gla_gate_w2=v_gla_gate_w2, gla_gate_b=v_gla_gate_b, gla_head_g=v_gla_head_g, gla_w_out=v_gla_w_out,
               final_norm_g=v_final_norm_g.reshape(1, -1))
    d = x.shape[-1]
    replicated = dict(mix_g=w["mix_norm_g"], ffn_g=w["ffn_norm_g"], conv_b=w["cp_conv_b"], ln_g=w["cp_ln_g"],
                      ln_b=w["cp_ln_b"], pool_w=w["cp_pool_w"][0].astype(BF16), pool_scale=w["cp_pool_scale"],
                      final_g=w["final_norm_g"])
    exchanges = _Exchanges(w, d)
    _, grad_x, _ = _local_step(x[0], loss_target[0], replicated, exchanges)
    out = exchanges.finish(w, mom, var)
    loss = out.pop("loss")

    def leaf(n, k):
        a = out[n][k]
        return a.reshape(-1) if n == "final_norm_g" else a

    return (loss, grad_x[None], *[leaf(n, 0) for n in _NAMES], *[leaf(n, 1) for n in _NAMES],
            *[leaf(n, 2) for n in _NAMES], *[leaf(n, 3) for n in _NAMES])
```

```python
import functools

import jax
import jax.numpy as jnp
from jax import lax
from jax.experimental import pallas as pl
from jax.experimental.pallas import tpu as pltpu

F32, BF16 = jnp.float32, jnp.bfloat16
N_DEV = 8
CHUNK = 64
N_META = 16
PAD_ROWS = CHUNK - N_META
HALO = 32
EPS = 1e-5
CONV_WIDTH = 31
POOL_WINDOWS = (2, 4, 8, 16)
HEADS = 4
GATE_RANK = 16
GATE_NORM = 16.0
GATE_PAD = 128
ADAM_LR, ADAM_B1, ADAM_B2, ADAM_EPS, ADAM_WD, ADAM_STEP = 0.001, 0.9, 0.999, 1e-08, 0.01, 10
V7X_VMEM_LIMIT = 56 * 2 ** 20
LANE = 128


def _cparams(*sem):
    return pltpu.CompilerParams(dimension_semantics=sem, vmem_limit_bytes=V7X_VMEM_LIMIT)


def _row_tile(t, cap):
    best = CHUNK
    for r in range(CHUNK, min(t, cap) + 1, CHUNK):
        if t % r == 0:
            best = r
    return best


def _resident(shape):
    return pl.BlockSpec(shape, lambda *_: (0,) * len(shape), pipeline_mode=pl.Buffered(1))


def _dot(a, b):
    return jnp.dot(a, b, preferred_element_type=F32)


def _dot_nt(a, b):
    return lax.dot_general(a, b, (((1,), (1,)), ((), ())), preferred_element_type=F32)


def _dot_tn(a, b):
    return lax.dot_general(a, b, (((0,), (0,)), ((), ())), preferred_element_type=F32)


def _rowsum(a):
    return jnp.sum(a, axis=0, keepdims=True)


def _sigmoid(a):
    return 1.0 / (1.0 + jnp.exp(-a))


def _row_ids(tile, rt):
    return tile * rt + lax.broadcasted_iota(jnp.int32, (rt, 1), 0)


def _sds(shape, dtype):
    return jax.ShapeDtypeStruct(shape, dtype)


DW_ROWS = 1024


def _linear_bwd_w(x, dy, *, after=None, name):
    t, k = x.shape
    n = dy.shape[1]
    cut_k = k > n
    width = k if cut_k else n
    blk = max(c for c in (640, 512, 384, 256, LANE) if width % c == 0)
    dep_specs, deps = _dep_specs(after)

    def body(*refs):
        x_ref, dy_ref, o_ref, acc = refs[len(deps):]
        for c0 in range(0, t, DW_ROWS):
            rows = slice(c0, min(c0 + DW_ROWS, t))
            part = _dot_tn(x_ref[rows, :].astype(BF16), dy_ref[rows, :].astype(BF16))
            if c0 == 0:
                acc[...] = part
            else:
                acc[...] += part
        o_ref[...] = acc[...].astype(BF16)

    if cut_k:
        in_specs = [pl.BlockSpec((t, blk), lambda j: (0, j)), _resident((t, n))]
        out_specs = pl.BlockSpec((blk, n), lambda j: (j, 0))
        acc_shape = (blk, n)
    else:
        in_specs = [_resident((t, k)), pl.BlockSpec((t, blk), lambda j: (0, j))]
        out_specs = pl.BlockSpec((k, blk), lambda j: (0, j))
        acc_shape = (k, blk)
    return pl.pallas_call(
        body, grid=(width // blk,), in_specs=dep_specs + in_specs, out_specs=out_specs, out_shape=_sds((k, n), BF16),
        scratch_shapes=[pltpu.VMEM(acc_shape, F32)], compiler_params=_cparams("parallel"), name=name)(*deps, x, dy)


FFN_BLOCKS_PER_STEP = 2


def _ffn_fwd(h, gain, w1g, w2g, *, loss_head=None, name):
    t, d = h.shape
    f8 = w1g.shape[-1]
    rt = _row_tile(t, 832)
    nb = FFN_BLOCKS_PER_STEP
    nstep = N_DEV // nb

    def body(*refs):
        if loss_head is None:
            h_ref, g_ref, w1_ref, w2_ref, o_ref, u_ref, r_ref, acc_ref = refs
        else:
            h_ref, g_ref, w1_ref, w2_ref, fg_ref, t_ref, o_ref, u_ref, r_ref, loss_ref, dfg_ref, acc_ref = refs
        i, j = pl.program_id(0), pl.program_id(1)

        @pl.when(j == 0)
        def _():
            hv = h_ref[...]
            u_ref[...] = (hv * lax.rsqrt(jnp.mean(hv * hv, axis=-1, keepdims=True) + EPS) * g_ref[...]).astype(BF16)
            acc_ref[...] = jnp.zeros_like(acc_ref)

        part = None
        for b in range(nb):
            a = jnp.maximum(_dot(u_ref[...], w1_ref[b]), 0.0)
            r_ref[:, b * f8:(b + 1) * f8] = a.astype(BF16)
            term = _dot((a * a).astype(BF16), w2_ref[b])
            part = term if part is None else part + term
        acc_ref[...] += part

        @pl.when(j == nstep - 1)
        def _():
            y = h_ref[...] + acc_ref[...]
            if loss_head is None:
                o_ref[...] = y
                return

            @pl.when(i == 0)
            def _():
                loss_ref[...] = jnp.zeros_like(loss_ref)
                dfg_ref[...] = jnp.zeros_like(dfg_ref)

            rstd = lax.rsqrt(jnp.mean(y * y, axis=-1, keepdims=True) + EPS)
            xh = y * rstd
            err = jnp.where(_row_ids(i, rt) >= CHUNK, xh * fg_ref[...] - t_ref[...], 0.0)
            loss_ref[...] += (0.5 / d) * jnp.sum(err * err)
            dy = err * (1.0 / d)
            dfg_ref[...] += _rowsum(dy * xh)
            dxh = dy * fg_ref[...]
            o_ref[...] = rstd * (dxh - xh * jnp.mean(dxh * xh, axis=-1, keepdims=True))

    rows = lambda i, j: (i, 0)
    in_specs = [pl.BlockSpec((rt, d), rows), _resident((1, d)),
                pl.BlockSpec((nb, d, f8), lambda i, j: (j, 0, 0)), pl.BlockSpec((nb, f8, d), lambda i, j: (j, 0, 0))]
    out_specs = [pl.BlockSpec((rt, d), rows), pl.BlockSpec((rt, d), rows), pl.BlockSpec((rt, nb * f8), lambda i, j: (i, j))]
    out_shape = [_sds((t, d), F32), _sds((t, d), BF16), _sds((t, N_DEV * f8), BF16)]
    args = [h, gain, w1g, w2g]
    if loss_head is not None:
        in_specs += [_resident((1, d)), pl.BlockSpec((rt, d), rows)]
        out_specs += [pl.BlockSpec((8, LANE), lambda i, j: (0, 0)), pl.BlockSpec((1, d), lambda i, j: (0, 0))]
        out_shape += [_sds((8, LANE), F32), _sds((1, d), F32)]
        args += list(loss_head)
    return pl.pallas_call(
        body, grid=(t // rt, nstep), in_specs=in_specs, out_specs=out_specs, out_shape=out_shape,
        scratch_shapes=[pltpu.VMEM((rt, d), F32)],
        compiler_params=_cparams("arbitrary" if loss_head is not None else "parallel", "arbitrary"), name=name)(*args)


def _ffn_bwd_x(h, dout, gain, r, w1g, w2g, *, after=None, name):
    t, d = h.shape
    f8 = w1g.shape[-1]
    rt = _row_tile(t, 832)
    nb = FFN_BLOCKS_PER_STEP
    last = N_DEV // nb - 1
    dep_specs, deps = _dep_specs(after)

    def body(*refs):
        h_ref, do_ref, g_ref, r_ref, w1_ref, w2_ref, dh_ref, dhh_ref, dob_ref, dg_ref, du_ref = refs[len(deps):]
        i, j = pl.program_id(0), pl.program_id(1)

        @pl.when(j == 0)
        def _():
            dob_ref[...] = do_ref[...].astype(BF16)
            du_ref[...] = jnp.zeros_like(du_ref)

        part = None
        for b in range(nb):
            cols = slice(b * f8, (b + 1) * f8)
            dhh = (_dot_nt(dob_ref[...], w2_ref[b]) * (2.0 * r_ref[:, cols].astype(F32))).astype(BF16)
            dhh_ref[:, cols] = dhh
            term = _dot_nt(dhh, w1_ref[b])
            part = term if part is None else part + term
        du_ref[...] += part

        @pl.when(j == last)
        def _():
            @pl.when(i == 0)
            def _():
                dg_ref[...] = jnp.zeros_like(dg_ref)

            hv = h_ref[...]
            rstd = lax.rsqrt(jnp.mean(hv * hv, axis=-1, keepdims=True) + EPS)
            xh = hv * rstd
            du = du_ref[...]
            dg_ref[...] += _rowsum(du * xh)
            dxh = du * g_ref[...]
            dh_ref[...] = do_ref[...] + rstd * (dxh - xh * jnp.mean(dxh * xh, axis=-1, keepdims=True))

    rows = lambda i, j: (i, 0)
    return pl.pallas_call(
        body, grid=(t // rt, N_DEV // nb),
        in_specs=dep_specs + [
                  pl.BlockSpec((rt, d), rows), pl.BlockSpec((rt, d), rows), _resident((1, d)),
                  pl.BlockSpec((rt, nb * f8), lambda i, j: (i, j)),
                  pl.BlockSpec((nb, d, f8), lambda i, j: (j, 0, 0)),
                  pl.BlockSpec((nb, f8, d), lambda i, j: (j, 0, 0))],
        out_specs=[pl.BlockSpec((rt, d), rows), pl.BlockSpec((rt, nb * f8), lambda i, j: (i, j)),
                   pl.BlockSpec((rt, d), rows), pl.BlockSpec((1, d), lambda i, j: (0, 0))],
        out_shape=[_sds((t, d), F32), _sds((t, N_DEV * f8), BF16), _sds((t, d), BF16), _sds((1, d), F32)],
        scratch_shapes=[pltpu.VMEM((rt, d), F32)],
        compiler_params=_cparams("arbitrary", "arbitrary"), name=name)(*deps, h, dout, gain, r, w1g, w2g)


def _ffn_bwd_w(u, dhh, r, dout_b, *, name):
    t, d = u.shape
    f8 = dhh.shape[1] // N_DEV

    def body(u_ref, dhh_ref, r_ref, dob_ref, dw1_ref, dw2_ref, acc1, acc2):
        for c0 in range(0, t, DW_ROWS):
            rows = slice(c0, min(c0 + DW_ROWS, t))
            rr = r_ref[rows, :].astype(F32)
            part1 = _dot_tn(u_ref[rows, :], dhh_ref[rows, :])
            part2 = _dot_tn((rr * rr).astype(BF16), dob_ref[rows, :])
            if c0 == 0:
                acc1[...] = part1
                acc2[...] = part2
            else:
                acc1[...] += part1
                acc2[...] += part2
        dw1_ref[...] = acc1[...].astype(BF16)
        dw2_ref[...] = acc2[...].astype(BF16)

    return pl.pallas_call(
        body, grid=(N_DEV,),
        in_specs=[_resident((t, d)), pl.BlockSpec((t, f8), lambda j: (0, j)), pl.BlockSpec((t, f8), lambda j: (0, j)),
                  _resident((t, d))],
        out_specs=[pl.BlockSpec((None, d, f8), lambda j: (j, 0, 0)), pl.BlockSpec((None, f8, d), lambda j: (j, 0, 0))],
        out_shape=[_sds((N_DEV, d, f8), BF16), _sds((N_DEV, f8, d), BF16)],
        scratch_shapes=[pltpu.VMEM((d, f8), F32), pltpu.VMEM((f8, d), F32)],
        compiler_params=_cparams("parallel"), name=name)(u, dhh, r, dout_b)


def _lane_blocks(width):
    lb = min(LANE, width)
    return [slice(s, s + lb) for s in range(0, width, lb)]


def _conv_rows(src_ref, w_ref, offset, dst_ref, nblk, width, bias_ref=None):
    def blk(rb, carry):
        base = pl.multiple_of(rb * CHUNK, CHUNK)
        for l, ls in enumerate(_lane_blocks(width)):
            acc = jnp.zeros((CHUNK, ls.stop - ls.start), F32)
            if bias_ref is not None:
                acc = acc + bias_ref[:, ls]
            for k in range(CONV_WIDTH):
                acc = acc + w_ref[k:k + 1, ls] * src_ref[l, pl.ds(base + offset(k), CHUNK), :]
            dst_ref[l, pl.ds(base, CHUNK), :] = acc
        return carry

    lax.fori_loop(0, nblk, blk, 0)


def _to_lane_blocks(ref, row0, value):
    for l, ls in enumerate(_lane_blocks(value.shape[1])):
        ref[l, row0:row0 + value.shape[0], :] = value[:, ls]


def _from_lane_blocks(ref):
    return jnp.concatenate([ref[l] for l in range(ref.shape[0])], axis=1)


def _pool_counts(rows, window):
    return jnp.clip(rows - PAD_ROWS + 1, 1, window).astype(F32)


def _trailing_sum(v, window):
    s, sh = v, 1
    while sh < window:
        s = s + pltpu.roll(s, sh, 0)
        sh *= 2
    return s


def _leading_sum(v, window):
    s, sh, n = v, 1, v.shape[0]
    while sh < window:
        s = s + pltpu.roll(s, n - sh, 0)
        sh *= 2
    return s


def _norm_project(h_ref, g_ref, w_t_ref, u_ref, z_ref):
    hv = h_ref[...]
    u = (hv * lax.rsqrt(jnp.mean(hv * hv, axis=-1, keepdims=True) + EPS) * g_ref[...]).astype(BF16)
    u_ref[...] = u
    z_ref[...] = _dot_nt(u, w_t_ref[...])


def _project_back(dz_ref, w_t_ref, h_ref, g_ref, dres_ref, dh_ref, dg_ref, first):
    dx = _dot(dz_ref[...], w_t_ref[...])
    hv = h_ref[...]
    rstd = lax.rsqrt(jnp.mean(hv * hv, axis=-1, keepdims=True) + EPS)
    xh = hv * rstd

    @pl.when(first)
    def _():
        dg_ref[...] = jnp.zeros_like(dg_ref)

    dg_ref[...] += _rowsum(dx * xh)
    dxh = dx * g_ref[...]
    dh_ref[...] = dres_ref[...] + rstd * (dxh - xh * jnp.mean(dxh * xh, axis=-1, keepdims=True))


def _cp_mid_fwd(h, gain, w_in_t, w_out, conv_w, conv_b, ln_g, ln_b, pool_w, pool_scale, *, name):
    t, d = h.shape
    ein = w_in_t.shape[0]
    cd = conv_b.shape[1]
    pd = pool_scale.shape[1]
    pg = pd // len(POOL_WINDOWS)
    rt = _row_tile(t, 320)

    def body(h_ref, g_ref, wi_ref, wo_ref, cw_ref, cb_ref, lg_ref, lb_ref, pw_ref, ps_ref,
             ho_ref, o_ref, z_ref, u_ref, cv_ref, gext, pext, conv_s):
        i = pl.program_id(0)

        @pl.when(i == 0)
        def _():
            _to_lane_blocks(gext, 0, jnp.zeros((HALO, cd), F32))
            pext[0:HALO, :] = jnp.zeros((HALO, pd), F32)

        _norm_project(h_ref, g_ref, wi_ref, u_ref, z_ref)

        _to_lane_blocks(gext, HALO, z_ref[:, 0:cd] * _sigmoid(z_ref[:, cd:2 * cd]))
        pext[HALO:HALO + rt, :] = z_ref[:, 2 * cd:]
        _conv_rows(gext, cw_ref, lambda k: k + HALO - (CONV_WIDTH - 1), conv_s, rt // CHUNK, cd, cb_ref)
        cv = _from_lane_blocks(conv_s)
        cv_ref[...] = cv
        xc = cv - jnp.mean(cv, axis=-1, keepdims=True)
        y = xc * lax.rsqrt(jnp.mean(xc * xc, axis=-1, keepdims=True) + EPS) * lg_ref[...] + lb_ref[...]
        rows = _row_ids(i, rt)
        a = jnp.where(rows >= PAD_ROWS, y * _sigmoid(y), 0.0)
        o_ref[:, 0:cd] = a.astype(BF16)
        for gi, window in enumerate(POOL_WINDOWS):
            ls = slice(gi * pg, (gi + 1) * pg)
            v = pext[:, ls]
            tm = _trailing_sum(v, window)[HALO:] / _pool_counts(rows, window) - v[HALO:]
            p = _dot(tm.astype(BF16), pw_ref[gi]) * ps_ref[:, ls]
            o_ref[:, cd + gi * pg:cd + (gi + 1) * pg] = p.astype(BF16)
        ho_ref[...] = h_ref[...] + _dot(o_ref[...], wo_ref[...])
        gext[:, 0:HALO, :] = gext[:, rt:rt + HALO, :]
        pext[0:HALO, :] = pext[rt:rt + HALO, :]

    nl, lb = len(_lane_blocks(cd)), min(LANE, cd)
    rows = lambda i: (i, 0)
    return pl.pallas_call(
        body, grid=(t // rt,),
        in_specs=[pl.BlockSpec((rt, d), rows), _resident((1, d)), _resident(w_in_t.shape), _resident(w_out.shape),
                  _resident(conv_w.shape), _resident((1, cd)),
                  _resident((1, cd)), _resident((1, cd)), _resident(pool_w.shape), _resident((1, pd))],
        out_specs=[pl.BlockSpec((rt, d), rows), pl.BlockSpec((rt, cd + pd), rows), pl.BlockSpec((rt, ein), rows),
                   pl.BlockSpec((rt, d), rows), pl.BlockSpec((rt, cd), rows)],
        out_shape=[_sds((t, d), F32), _sds((t, cd + pd), BF16), _sds((t, ein), F32), _sds((t, d), BF16),
                   _sds((t, cd), F32)],
        scratch_shapes=[pltpu.VMEM((nl, rt + HALO, lb), F32), pltpu.VMEM((rt + HALO, pd), F32),
                        pltpu.VMEM((nl, rt, lb), F32)],
        compiler_params=_cparams("arbitrary"), name=name)(h, gain, w_in_t, w_out, conv_w, conv_b, ln_g, ln_b, pool_w,
                                                          pool_scale)


def _cp_mid_bwd(z, cv, h, gain, w_in_t, dh, w_out, conv_w, conv_b, ln_g, ln_b, pool_w, pool_scale, *, after=None, name):
    t, ein = z.shape
    cd = conv_b.shape[1]
    pd = pool_scale.shape[1]
    pg = pd // len(POOL_WINDOWS)
    rt = _row_tile(t, 320)
    ntile = t // rt
    per = rt // CHUNK
    dep_specs, deps = _dep_specs(after)

    def body(*refs):
        (z_ref, zh_ref, cv_ref, h_ref, g_ref, wi_ref, dh_ref, wo_ref, cw_ref, cb_ref, lg_ref, lb_ref, pw_ref, ps_ref,
         dhi_ref, dg_ref, dz_ref, dcw_ref, dcb_ref, dlg_ref, dlb_ref, dpw_ref, dps_ref,
         gext, pext, conv_s, dcv, dsp) = refs[len(deps):]
        step = pl.program_id(0)
        tile = ntile - 1 - step
        dcat = _dot_nt(dh_ref[...].astype(BF16), wo_ref[...])

        @pl.when(step == 0)
        def _():
            for ref in (dcw_ref, dcb_ref, dlg_ref, dlb_ref, dpw_ref, dps_ref):
                ref[...] = jnp.zeros_like(ref)
            _to_lane_blocks(dcv, rt, jnp.zeros((HALO, cd), F32))
            dsp[rt:rt + HALO, :] = jnp.zeros((HALO, pd), F32)

        keep = jnp.where(tile > 0, 1.0, 0.0)
        zh = zh_ref[CHUNK - HALO:CHUNK, :]
        _to_lane_blocks(gext, 0, keep * zh[:, 0:cd] * _sigmoid(zh[:, cd:2 * cd]))
        pext[0:HALO, :] = keep * zh[:, 2 * cd:]
        za = z_ref[:, 0:cd]
        sg = _sigmoid(z_ref[:, cd:2 * cd])
        _to_lane_blocks(gext, HALO, za * sg)
        pext[HALO:HALO + rt, :] = z_ref[:, 2 * cd:]
        cv = cv_ref[...]
        xc = cv - jnp.mean(cv, axis=-1, keepdims=True)
        rstd = lax.rsqrt(jnp.mean(xc * xc, axis=-1, keepdims=True) + EPS)
        xh = xc * rstd
        y = xh * lg_ref[...] + lb_ref[...]
        sy = _sigmoid(y)
        rows = _row_ids(tile, rt)
        da = jnp.where(rows >= PAD_ROWS, dcat[:, 0:cd], 0.0)
        dy = da * (sy * (1.0 + y * (1.0 - sy)))
        dlg_ref[...] += _rowsum(dy * xh)
        dlb_ref[...] += _rowsum(dy)
        dxh = dy * lg_ref[...]
        dconv = rstd * (dxh - jnp.mean(dxh, axis=-1, keepdims=True) - xh * jnp.mean(dxh * xh, axis=-1, keepdims=True))
        dcb_ref[...] += _rowsum(dconv)
        _to_lane_blocks(dcv, 0, dconv)
        for l, ls in enumerate(_lane_blocks(cd)):
            def acc_rows(rb, accs, l=l):
                base = pl.multiple_of(rb * CHUNK, CHUNK)
                d_blk = dcv[l, pl.ds(base, CHUNK), :]
                out = []
                for k in range(CONV_WIDTH):
                    prod = d_blk * gext[l, pl.ds(base + k + HALO - (CONV_WIDTH - 1), CHUNK), :]
                    part = prod[0:8]
                    for s in range(8, CHUNK, 8):
                        part = part + prod[s:s + 8]
                    out.append(accs[k] + part)
                return tuple(out)

            zero = jnp.zeros((8, ls.stop - ls.start), F32)
            accs = lax.fori_loop(0, per, acc_rows, (zero,) * CONV_WIDTH)
            for k in range(CONV_WIDTH):
                dcw_ref[k:k + 1, ls] += _rowsum(accs[k])
        _conv_rows(dcv, cw_ref, lambda k: CONV_WIDTH - 1 - k, conv_s, per, cd)
        dglu = _from_lane_blocks(conv_s)
        dz_ref[:, 0:cd] = (dglu * sg).astype(BF16)
        dz_ref[:, cd:2 * cd] = (dglu * za * sg * (1.0 - sg)).astype(BF16)
        dcv[:, rt:rt + HALO, :] = dcv[:, 0:HALO, :]
        for gi, window in enumerate(POOL_WINDOWS):
            ls = slice(gi * pg, (gi + 1) * pg)
            v = pext[:, ls]
            cnt = _pool_counts(rows, window)
            tm = (_trailing_sum(v, window)[HALO:] / cnt - v[HALO:]).astype(BF16)
            dp = dcat[:, cd + gi * pg:cd + (gi + 1) * pg]
            dps_ref[:, ls] += _rowsum(dp * _dot(tm, pw_ref[gi]))
            dpl = (dp * ps_ref[:, ls]).astype(BF16)
            dpw_ref[gi] += _dot_tn(tm, dpl)
            dtm = _dot_nt(dpl, pw_ref[gi])
            dsp[0:rt, ls] = dtm / cnt
            dpin = _leading_sum(dsp[:, ls], window)[0:rt] - dtm
            dz_ref[:, 2 * cd + gi * pg:2 * cd + (gi + 1) * pg] = dpin.astype(BF16)
        dsp[rt:rt + HALO, :] = dsp[0:HALO, :]
        _project_back(dz_ref, wi_ref, h_ref, g_ref, dh_ref, dhi_ref, dg_ref, step == 0)

    d = h.shape[1]
    back = lambda i: (ntile - 1 - i, 0)
    halo_idx = lambda i: (jnp.maximum((ntile - 1 - i) * per - 1, 0), 0)
    const2 = lambda i: (0, 0)
    nl, lb = len(_lane_blocks(cd)), min(LANE, cd)
    return pl.pallas_call(
        body, grid=(ntile,),
        in_specs=dep_specs + [
                  pl.BlockSpec((rt, ein), back), pl.BlockSpec((CHUNK, ein), halo_idx), pl.BlockSpec((rt, cd), back),
                  pl.BlockSpec((rt, d), back),
                  _resident((1, d)), _resident(w_in_t.shape), pl.BlockSpec((rt, d), back), _resident(w_out.shape),
                  _resident(conv_w.shape), _resident((1, cd)), _resident((1, cd)), _resident((1, cd)),
                  _resident(pool_w.shape), _resident((1, pd))],
        out_specs=[pl.BlockSpec((rt, d), back), pl.BlockSpec((1, d), const2),
                   pl.BlockSpec((rt, ein), back), pl.BlockSpec(conv_w.shape, const2), pl.BlockSpec((1, cd), const2),
                   pl.BlockSpec((1, cd), const2), pl.BlockSpec((1, cd), const2),
                   pl.BlockSpec(pool_w.shape, lambda i: (0, 0, 0)), pl.BlockSpec((1, pd), const2)],
        out_shape=[_sds((t, d), F32), _sds((1, d), F32),
                   _sds((t, ein), BF16), _sds(conv_w.shape, F32), _sds((1, cd), F32), _sds((1, cd), F32),
                   _sds((1, cd), F32), _sds(pool_w.shape, F32), _sds((1, pd), F32)],
        scratch_shapes=[pltpu.VMEM((nl, rt + HALO, lb), F32), pltpu.VMEM((rt + HALO, pd), F32), pltpu.VMEM((nl, rt, lb), F32),
                        pltpu.VMEM((nl, rt + HALO, lb), F32), pltpu.VMEM((rt + HALO, pd), F32)],
        compiler_params=_cparams("arbitrary"), name=name)(*deps, z, z, cv, h, gain, w_in_t, dh, w_out, conv_w, conv_b, ln_g,
                                                          ln_b, pool_w, pool_scale)


def _log_decay(r, gw_ref, gb_ref, rows):
    gp = _dot(r.astype(BF16), gw_ref[...]) + gb_ref[...]
    log_sig = jnp.minimum(gp, 0.0) - jnp.log(1.0 + jnp.exp(-jnp.abs(gp)))
    return gp, jnp.where(rows >= PAD_ROWS, log_sig / GATE_NORM, 0.0)


def _tri(strict):
    r = lax.broadcasted_iota(jnp.int32, (CHUNK, CHUNK), 0)
    c = lax.broadcasted_iota(jnp.int32, (CHUNK, CHUNK), 1)
    return jnp.where(c < r if strict else c <= r, 1.0, 0.0).astype(BF16)


def _tri_dot(tri, a):
    hi = a.astype(BF16)
    rest = a - hi.astype(F32)
    mid = rest.astype(BF16)
    lo = (rest - mid.astype(F32)).astype(BF16)
    return _dot(tri, hi) + _dot(tri, mid) + _dot(tri, lo)


def _gla_mid_fwd(h, gain, w_in_t, w_out, gate_w, gate_b, head_g, *, name):
    t = h.shape[0]
    zw = w_in_t.shape[0]
    dk = gate_b.shape[1]
    hv = head_g.shape[1]
    hk = dk // HEADS
    dv = hv * HEADS
    r_at = 2 * dk + 2 * dv
    rt = _row_tile(t, 320)
    per = rt // CHUNK
    scale = hk ** -0.5

    def body(h_ref, g_ref, wi_ref, wo_ref, gw_ref, gb_ref, hg_ref, ho_ref, o_ref, st_ref, z_ref, u_ref,
             s_ref, la_ref, dec_ref):
        i = pl.program_id(0)

        @pl.when(i == 0)
        def _():
            s_ref[...] = jnp.zeros_like(s_ref)

        _norm_project(h_ref, g_ref, wi_ref, u_ref, z_ref)

        _, la = _log_decay(z_ref[:, r_at:r_at + GATE_PAD], gw_ref, gb_ref, _row_ids(i, rt))
        la_ref[...] = la
        tri = _tri(False)

        def chunk_rows(c):
            return slice(c * CHUNK, (c + 1) * CHUNK)

        def decays(c, carry):
            rows = chunk_rows(c)
            la_c = la_ref[rows, :]
            cum = _tri_dot(tri, la_c)
            dec_ref[rows, :] = jnp.exp(_rowsum(la_c) - cum)
            return carry

        def states(c, carry):
            rows = chunk_rows(c)
            etot = jnp.exp(_rowsum(la_ref[rows, :]))
            for hd in range(HEADS):
                ks = slice(hd * hk, (hd + 1) * hk)
                kd = z_ref[rows, dk + hd * hk:dk + (hd + 1) * hk] * dec_ref[rows, ks]
                v = z_ref[rows, 2 * dk + hd * hv:2 * dk + (hd + 1) * hv]
                s_new = s_ref[hd] * etot[:, ks] + _dot_tn(v.astype(BF16), kd.astype(BF16))
                s_ref[hd] = s_new
                st_ref[c, hd] = s_new
            return carry

        def outputs(c, carry):
            rows = chunk_rows(c)
            for hd in range(HEADS):
                q = z_ref[rows, hd * hk:(hd + 1) * hk] * scale
                g = z_ref[rows, 2 * dk + dv + hd * hv:2 * dk + dv + (hd + 1) * hv]
                o = _dot_nt(q.astype(BF16), st_ref[c, hd].astype(BF16))
                on = o * lax.rsqrt(jnp.mean(o * o, axis=-1, keepdims=True) + EPS) * hg_ref[...]
                o_ref[rows, hd * hv:(hd + 1) * hv] = (on * (g * _sigmoid(g))).astype(BF16)
            return carry

        for phase in (decays, states, outputs):
            for c in range(per):
                phase(c, 0)
        ho_ref[...] = h_ref[...] + _dot(o_ref[...], wo_ref[...])

    d = h.shape[1]
    rows = lambda i: (i, 0)
    return pl.pallas_call(
        body, grid=(t // rt,),
        in_specs=[pl.BlockSpec((rt, d), rows), _resident((1, d)), _resident(w_in_t.shape), _resident(w_out.shape),
                  _resident(gate_w.shape), _resident((1, dk)), _resident((1, hv))],
        out_specs=[pl.BlockSpec((rt, d), rows), pl.BlockSpec((rt, dv), rows),
                   pl.BlockSpec((per, HEADS, hv, hk), lambda i: (i, 0, 0, 0)), pl.BlockSpec((rt, zw), rows),
                   pl.BlockSpec((rt, d), rows)],
        out_shape=[_sds((t, d), F32), _sds((t, dv), BF16), _sds((t // CHUNK, HEADS, hv, hk), F32), _sds((t, zw), F32),
                   _sds((t, d), BF16)],
        scratch_shapes=[pltpu.VMEM((HEADS, hv, hk), F32), pltpu.VMEM((rt, dk), F32), pltpu.VMEM((rt, dk), F32)],
        compiler_params=_cparams("arbitrary"), name=name)(h, gain, w_in_t, w_out, gate_w, gate_b, head_g)


def _gla_mid_bwd(z, h, gain, w_in_t, dh, w_out, states, gate_w, gate_b, head_g, *, after=None, name):
    t = z.shape[0]
    dk = gate_b.shape[1]
    hv = head_g.shape[1]
    hk = dk // HEADS
    dv = hv * HEADS
    r_at = 2 * dk + 2 * dv
    rt = _row_tile(t, 320)
    ntile = t // rt
    per = rt // CHUNK
    scale = hk ** -0.5
    dep_specs, deps = _dep_specs(after)

    def body(*refs):
        (z_ref, h_ref, g_ref, wi_ref, dh_ref, wo_ref, st_ref, stp_ref, gw_ref, gb_ref, hg_ref,
         dhi_ref, dg_ref, dz_ref, dgw_ref, dgb_ref, dhg_ref,
         ds_ref, la_ref, dla_ref, dec_ref, dos_ref, e_ref, do_ref) = refs[len(deps):]
        step = pl.program_id(0)
        tile = ntile - 1 - step
        do_ref[...] = _dot_nt(dh_ref[...].astype(BF16), wo_ref[...])

        @pl.when(step == 0)
        def _():
            ds_ref[...] = jnp.zeros_like(ds_ref)
            dgw_ref[...] = jnp.zeros_like(dgw_ref)
            dgb_ref[...] = jnp.zeros_like(dgb_ref)
            dhg_ref[...] = jnp.zeros_like(dhg_ref)

        rows_id = _row_ids(tile, rt)
        r = z_ref[:, r_at:r_at + GATE_PAD]
        gp, la = _log_decay(r, gw_ref, gb_ref, rows_id)
        la_ref[...] = la
        tri, tri_strict = _tri(False), _tri(True)
        keep = jnp.where(tile > 0, 1.0, 0.0)

        def chunk_rows(c):
            return slice(c * CHUNK, (c + 1) * CHUNK)

        def recompute(c, dhg):
            rows = chunk_rows(c)
            la_c = la_ref[rows, :]
            cum = _tri_dot(tri, la_c)
            dec_ref[rows, :] = jnp.exp(_rowsum(la_c) - cum)
            for hd in range(HEADS):
                q = (z_ref[rows, hd * hk:(hd + 1) * hk] * scale).astype(BF16)
                g = z_ref[rows, 2 * dk + dv + hd * hv:2 * dk + dv + (hd + 1) * hv]
                s_b = st_ref[c, hd].astype(BF16)
                o = _dot_nt(q, s_b)
                rstd = lax.rsqrt(jnp.mean(o * o, axis=-1, keepdims=True) + EPS)
                oh = o * rstd
                sg = _sigmoid(g)
                d_og = do_ref[rows, hd * hv:(hd + 1) * hv]
                dz_ref[rows, 2 * dk + dv + hd * hv:2 * dk + dv + (hd + 1) * hv] = (
                    d_og * oh * hg_ref[...] * (sg * (1.0 + g * (1.0 - sg)))).astype(BF16)
                don = d_og * (g * sg)
                dhg = dhg + _rowsum(don * oh)
                doh = don * hg_ref[...]
                d_o = (rstd * (doh - oh * jnp.mean(doh * oh, axis=-1, keepdims=True))).astype(BF16)
                dos_ref[rows, hd * hv:(hd + 1) * hv] = d_o
                dz_ref[rows, hd * hk:(hd + 1) * hk] = (_dot(d_o, s_b) * scale).astype(BF16)
            return dhg

        def recurrence(cc, carry):
            c = per - 1 - cc
            rows = chunk_rows(c)
            etot = jnp.exp(_rowsum(la_ref[rows, :]))
            for hd in range(HEADS):
                ks = slice(hd * hk, (hd + 1) * hk)
                q = (z_ref[rows, hd * hk:(hd + 1) * hk] * scale).astype(BF16)
                dec = dec_ref[rows, ks]
                kd = z_ref[rows, dk + hd * hk:dk + (hd + 1) * hk] * dec
                v = z_ref[rows, 2 * dk + hd * hv:2 * dk + (hd + 1) * hv].astype(BF16)
                s_prev = st_ref[c - 1, hd] if c > 0 else keep * stp_ref[0, hd]
                ds_t = ds_ref[hd] + _dot_tn(dos_ref[rows, hd * hv:(hd + 1) * hv], q)
                ds_b = ds_t.astype(BF16)
                dkd = _dot(v, ds_b)
                dz_ref[rows, 2 * dk + hd * hv:2 * dk + (hd + 1) * hv] = _dot_nt(kd.astype(BF16), ds_b).astype(BF16)
                dtot = etot[:, ks] * _rowsum(ds_t * s_prev)
                ds_ref[hd] = ds_t * etot[:, ks]
                dz_ref[rows, dk + hd * hk:dk + (hd + 1) * hk] = (dkd * dec).astype(BF16)
                e_ref[rows, ks] = dkd * kd
                dla_ref[rows, ks] = jnp.broadcast_to(dtot, (CHUNK, hk))
            return carry

        def decay_cotangent(c, carry):
            rows = chunk_rows(c)
            dla_ref[rows, :] += _tri_dot(tri_strict, e_ref[rows, :])
            return carry

        dhg = jnp.zeros((1, hv), F32)
        for c in range(per):
            dhg = recompute(c, dhg)
        dhg_ref[...] += dhg
        for phase in (recurrence, decay_cotangent):
            for c in range(per):
                phase(c, 0)
        dla = jnp.where(rows_id >= PAD_ROWS, dla_ref[...], 0.0)
        dgp = dla * (1.0 / GATE_NORM) * (1.0 - _sigmoid(gp))
        dgb_ref[...] += _rowsum(dgp)
        dgp_b = dgp.astype(BF16)
        dgw_ref[...] += _dot_tn(r.astype(BF16), dgp_b)
        dz_ref[:, r_at:r_at + GATE_PAD] = _dot_nt(dgp_b, gw_ref[...]).astype(BF16)
        _project_back(dz_ref, wi_ref, h_ref, g_ref, dh_ref, dhi_ref, dg_ref, step == 0)

    d = h.shape[1]
    back = lambda i: (ntile - 1 - i, 0)
    const2 = lambda i: (0, 0)
    return pl.pallas_call(
        body, grid=(ntile,),
        in_specs=dep_specs + [
                  pl.BlockSpec((rt, z.shape[1]), back), pl.BlockSpec((rt, d), back), _resident((1, d)),
                  _resident(w_in_t.shape), pl.BlockSpec((rt, d), back), _resident(w_out.shape),
                  pl.BlockSpec((per, HEADS, hv, hk), lambda i: (ntile - 1 - i, 0, 0, 0)),
                  pl.BlockSpec((1, HEADS, hv, hk), lambda i: (jnp.maximum((ntile - 1 - i) * per - 1, 0), 0, 0, 0)),
                  _resident(gate_w.shape), _resident((1, dk)), _resident((1, hv))],
        out_specs=[pl.BlockSpec((rt, d), back), pl.BlockSpec((1, d), const2), pl.BlockSpec((rt, z.shape[1]), back),
                   pl.BlockSpec(gate_w.shape, const2), pl.BlockSpec((1, dk), const2), pl.BlockSpec((1, hv), const2)],
        out_shape=[_sds((t, d), F32), _sds((1, d), F32), _sds(z.shape, BF16), _sds(gate_w.shape, F32),
                   _sds((1, dk), F32), _sds((1, hv), F32)],
        scratch_shapes=[pltpu.VMEM((HEADS, hv, hk), F32), pltpu.VMEM((rt, dk), F32), pltpu.VMEM((rt, dk), F32),
                        pltpu.VMEM((rt, dk), F32), pltpu.VMEM((rt, dv), BF16), pltpu.VMEM((rt, dk), F32),
                        pltpu.VMEM((rt, dv), F32)],
        compiler_params=_cparams("arbitrary"), name=name)(*deps, z, h, gain, w_in_t, dh, w_out, states, states, gate_w,
                                                          gate_b, head_g)


def _adamw_math(w, g, m, v):
    m = ADAM_B1 * m + (1.0 - ADAM_B1) * g
    v = ADAM_B2 * v + (1.0 - ADAM_B2) * (g * g)
    m_hat = m / (1.0 - ADAM_B1 ** ADAM_STEP)
    v_hat = v / (1.0 - ADAM_B2 ** ADAM_STEP)
    return -ADAM_LR * (m_hat / (jnp.sqrt(v_hat) + ADAM_EPS) + ADAM_WD * w), m, v


N_CHIP = N_DEV // 2
BLOCK_ELEMS = 128 * 1024


def _my_slot():
    return 4 * lax.axis_index("x") + 2 * lax.axis_index("y") + lax.axis_index("c")


def _row_block(r, c):
    cap = max(8, BLOCK_ELEMS // (-(-c // LANE) * LANE))
    return max([b for b in range(8, r + 1, 8) if r % b == 0 and b <= cap] or [r])


def _blocks(r, c):
    rb = _row_block(r, c)
    if rb < r or r * c <= BLOCK_ELEMS:
        return rb, c
    return r, max([b for b in (512, 256, LANE) if c % b == 0 and r * b <= BLOCK_ELEMS] or [c])


def _reduce_adam(parts, w, m, v, *, after=None, name):
    nl, r, c = w.shape
    rb, cb = _blocks(r, c)
    dep_specs, deps = _dep_specs(after)

    def body(*refs):
        me = refs[0][0]
        refs = refs[1 + len(deps):]
        p_refs = refs[:2 * nl]
        w_ref, m_ref, v_ref, g_out, d_out, m_out, v_out = refs[2 * nl:]
        layer = pl.program_id(0)
        for li in range(nl):
            @pl.when(layer == li)
            def _(li=li):
                own_ref, land_ref = p_refs[2 * li], p_refs[2 * li + 1]
                mine = own_ref[...].astype(F32)
                g = None
                for dev in range(N_DEV):
                    term = jnp.where(me == dev, mine, land_ref[dev].astype(F32))
                    g = term if g is None else g + term
                g_out[...] = g
                d_out[...], m_out[...], v_out[...] = _adamw_math(w_ref[...], g, m_ref[...], v_ref[...])

    blk = pl.BlockSpec((None, rb, cb), lambda l, i, j, me: (l, i, j))
    p_specs = []
    for li in range(nl):
        p_specs += [
            pl.BlockSpec((None, rb, cb), lambda l, i, j, me, li=li: (me[0], jnp.where(l == li, i, 0), jnp.where(l == li, j, 0))),
            pl.BlockSpec((N_DEV, rb, cb), lambda l, i, j, me, li=li: (0, jnp.where(l == li, i, 0), jnp.where(l == li, j, 0)))]
    flat = [p for pair in parts for p in pair]
    grid_spec = pltpu.PrefetchScalarGridSpec(
        num_scalar_prefetch=1, grid=(nl, r // rb, c // cb), in_specs=dep_specs + p_specs + [blk, blk, blk],
        out_specs=[blk] * 4)
    return pl.pallas_call(
        body, grid_spec=grid_spec, out_shape=[_sds(w.shape, F32)] * 4,
        compiler_params=_cparams("arbitrary", "arbitrary", "arbitrary"), name=name)(
        _my_slot().reshape(1), *deps, *flat, w, m, v)


def _sum8(own, landed, *, name):
    def body(own_ref, land_ref, o_ref):
        me = _my_slot()
        total = None
        for dev in range(N_DEV):
            term = jnp.where(me == dev, own_ref[...], land_ref[dev])
            total = term if total is None else total + term
        o_ref[...] = total

    return pl.pallas_call(body, out_shape=_sds(own.shape, F32), name=name)(own, landed)


def _adam_small(own, landed, split, w, m, v, *, name):
    n = len(w)

    def body(*refs):
        own_refs, land_refs, w_refs, m_refs, v_refs = (refs[k * n:(k + 1) * n] for k in range(5))
        outs = refs[5 * n:]
        me = _my_slot()
        for k in range(n):
            mine = own_refs[k][me] if split[k] else own_refs[k][...]
            g = None
            for dev in range(N_DEV):
                term = jnp.where(me == dev, mine, land_refs[k][dev])
                g = term if g is None else g + term
            outs[4 * k][...] = g
            outs[4 * k + 1][...], outs[4 * k + 2][...], outs[4 * k + 3][...] = _adamw_math(
                w_refs[k][...], g, m_refs[k][...], v_refs[k][...])

    out = pl.pallas_call(body, out_shape=[_sds(a.shape, F32) for a in w for _ in range(4)],
                         compiler_params=pltpu.CompilerParams(vmem_limit_bytes=V7X_VMEM_LIMIT), name=name)(
        *own, *landed, *w, *m, *v)
    return [tuple(out[4 * k:4 * k + 4]) for k in range(n)]


_HBM = pl.BlockSpec(memory_space=pltpu.HBM)
_SEM = pl.BlockSpec(memory_space=pltpu.SEMAPHORE)
_DATAFLOW = pltpu.SideEffectType.DATAFLOW_SIDE_EFFECTING


def _plan_to_all(src, land):
    x, y, c = lax.axis_index("x"), lax.axis_index("y"), lax.axis_index("c")
    return [(src, land.at[_my_slot()], (x ^ ((d >> 2) & 1), y ^ ((d >> 1) & 1), c ^ (d & 1))) for d in range(1, N_DEV)]


def _plan_split_to_all(src, land):
    x, y, c = lax.axis_index("x"), lax.axis_index("y"), lax.axis_index("c")
    peers = [(x ^ ((d >> 2) & 1), y ^ ((d >> 1) & 1), c ^ (d & 1)) for d in range(1, N_DEV)]
    return [(src.at[4 * px + 2 * py + pc], land.at[_my_slot()], (px, py, pc)) for px, py, pc in peers]


_PLAN_COPIES = {_plan_to_all: N_DEV - 1, _plan_split_to_all: N_DEV - 1}


def _plans(plan, n):
    return list(plan) if isinstance(plan, (list, tuple)) else [plan] * n


def _exchange_copies(plan, ins, lands, send, recv):
    copies, sem = [], 0
    for p, src, land in zip(_plans(plan, len(lands)), ins, lands):
        for s, dst, dev in p(src, land):
            copies.append(pltpu.make_async_remote_copy(
                src_ref=s, dst_ref=dst, send_sem=send.at[sem], recv_sem=recv.at[sem],
                device_id=dev, device_id_type=pl.DeviceIdType.MESH))
            sem += 1
    return copies


def _place_own(a, dtype, *, after=None, name):
    r, c = a.shape
    rb = _row_block(r, c)
    dep_specs, deps = _dep_specs(after)

    def body(*refs):
        a_ref, o_ref = refs[1 + len(deps):]
        o_ref[...] = a_ref[...].astype(dtype)

    grid_spec = pltpu.PrefetchScalarGridSpec(
        num_scalar_prefetch=1, grid=(r // rb,), in_specs=dep_specs + [pl.BlockSpec((rb, c), lambda i, me: (i, 0))],
        out_specs=pl.BlockSpec((None, rb, c), lambda i, me: (me[0], i, 0)))
    return pl.pallas_call(body, grid_spec=grid_spec, out_shape=_sds((N_DEV, r, c), dtype),
                          compiler_params=_cparams("arbitrary"), name=name)(_my_slot().reshape(1), *deps, a)


def _plan_gather_first(land, _):
    x, y, c = lax.axis_index("x"), lax.axis_index("y"), lax.axis_index("c")
    mine = land.at[_my_slot()]
    return [(mine, mine, (x, y, 1 - c))] + [(mine, mine, (x ^ (d >> 1), y ^ (d & 1), c)) for d in range(1, N_CHIP)]


def _plan_gather_relay(land, _):
    x, y, c = lax.axis_index("x"), lax.axis_index("y"), lax.axis_index("c")
    slots = [land.at[4 * (x ^ (d >> 1)) + 2 * (y ^ (d & 1)) + c] for d in range(1, N_CHIP)]
    return [(s, s, (x, y, 1 - c)) for s in slots]


def _plan_gather_direct(land, _):
    return _plan_to_all(land.at[_my_slot()], land)


_PLAN_COPIES[_plan_gather_first] = N_CHIP
_PLAN_COPIES[_plan_gather_relay] = N_CHIP - 1
_PLAN_COPIES[_plan_gather_direct] = N_DEV - 1


def _exchange_start(plan, arrs, lands, *, after=None, name):
    bufs = list(lands) if arrs is None else list(arrs) + list(lands)
    n, nb = len(lands), len(bufs)
    nsem = sum(_PLAN_COPIES[p] for p in _plans(plan, n))
    dep_specs, deps = _dep_specs(after)

    def body(*refs):
        ins, land_refs = refs[:n], refs[nb - n:nb]
        send, recv = refs[nb + len(deps)], refs[nb + len(deps) + 1]
        for cp in _exchange_copies(plan, ins, land_refs, send, recv):
            cp.start()
        refs[-1][...] = jnp.zeros_like(refs[-1])

    out = pl.pallas_call(
        body, name=name,
        out_shape=(pltpu.SemaphoreType.DMA((nsem,)), pltpu.SemaphoreType.DMA((nsem,)),
                   *[pltpu.HBM(a.shape, a.dtype) for a in bufs], _sds((8, LANE), F32)),
        in_specs=[_HBM] * nb + dep_specs,
        out_specs=(_SEM, _SEM, *([_HBM] * nb), pl.BlockSpec(memory_space=pltpu.VMEM)),
        input_output_aliases={i: 2 + i for i in range(nb)},
        compiler_params=pltpu.CompilerParams(has_side_effects=_DATAFLOW),
    )(*[pltpu.with_memory_space_constraint(a, pltpu.HBM) for a in bufs], *deps)
    return (plan, n, out[0], out[1], list(out[2:2 + nb])), out[-1]


def _exchange_now(plan, lands, *, name):
    n = len(lands)
    nsem = sum(_PLAN_COPIES[p] for p in _plans(plan, n))

    def body(*refs):
        land_refs, send, recv = refs[n:2 * n], refs[2 * n], refs[2 * n + 1]
        copies = _exchange_copies(plan, land_refs, land_refs, send, recv)
        for cp in copies:
            cp.start()
        for cp in copies:
            cp.wait_send()
            cp.wait_recv()

    hbm = pl.BlockSpec(memory_space=pl.ANY)
    return pl.pallas_call(
        body, in_specs=[hbm] * n, out_specs=[hbm] * n, out_shape=[_sds(a.shape, a.dtype) for a in lands],
        input_output_aliases={i: i for i in range(n)},
        scratch_shapes=[pltpu.SemaphoreType.DMA((nsem,)), pltpu.SemaphoreType.DMA((nsem,))], name=name)(*lands)


def _exchange_wait(state, after, *, name):
    plan, n, send_sem, recv_sem, bufs = state
    nb = len(bufs)
    after = list(after) if isinstance(after, (list, tuple)) else [after]

    def body(*refs):
        ins, land_refs, send, recv = refs[:n], refs[nb - n:nb], refs[nb], refs[nb + 1]
        for cp in _exchange_copies(plan, ins, land_refs, send, recv):
            cp.wait_send()
            cp.wait_recv()

    out = pl.pallas_call(
        body, name=name, out_shape=[pltpu.HBM(a.shape, a.dtype) for a in bufs],
        in_specs=[_HBM] * nb + [_SEM, _SEM] + [pl.BlockSpec(memory_space=pl.ANY)] * len(after), out_specs=[_HBM] * nb,
        input_output_aliases={i: i for i in range(nb)},
        compiler_params=pltpu.CompilerParams(has_side_effects=_DATAFLOW),
    )(*bufs, send_sem, recv_sem, *after)
    return list(out[:n]), list(out[nb - n:])


def _dep_specs(after):
    return ([], []) if after is None else ([pl.BlockSpec(memory_space=pl.ANY)], [after])


def _undo_column_split(g):
    return jnp.transpose(g, (1, 0, 2)).reshape(g.shape[1], N_DEV * g.shape[2])


def _column_split(a):
    r, c = a.shape
    return jnp.transpose(a.reshape(r, N_DEV, c // N_DEV), (1, 0, 2))


class _WholeWeights:
    def __init__(self, groups):
        self.groups = groups
        self.grads = {}

    def fetch(self, group, after):
        return self.groups[group]

    def emit(self, group, grads):
        self.grads.update(grads)
        return None


def _local_step(x, target, replicated, src):
    d = x.shape[1]
    mix_g, ffn_g = replicated["mix_g"], replicated["ffn_g"]
    h0 = jnp.concatenate([jnp.zeros((CHUNK, d), F32), x], axis=0)
    tgt = jnp.concatenate([jnp.zeros((CHUNK, d), F32), target], axis=0)
    cp = src.fetch("cp", [h0, tgt])
    h0 = lax.dynamic_update_slice(h0, cp["meta"], (PAD_ROWS, 0))
    cp_mid = (cp["conv_w"], replicated["conv_b"], replicated["ln_g"], replicated["ln_b"], replicated["pool_w"],
              replicated["pool_scale"])

    h1, cat, z0, u0, cv0 = _cp_mid_fwd(h0, mix_g[0:1], cp["cp_w_in_t"], cp["cp_w_out"], *cp_mid, name="cp_mixer")
    ffn0 = src.fetch("ffn0", h1)
    h2, uf0, rf0 = _ffn_fwd(h1, ffn_g[0:1], ffn0["w1"], ffn0["w2"], name="ffn0")
    gla = src.fetch("gla", h2)
    gla_mid = (gla["gate_w"], gla["gate_b"], gla["head_g"])
    h3, og, states, z1, u1 = _gla_mid_fwd(h2, mix_g[1:2], gla["gla_w_in_t"], gla["gla_w_out"], *gla_mid, name="gla_mixer")
    ffn1 = src.fetch("ffn1", h3)
    dh4, uf1, rf1, loss, d_final_g = _ffn_fwd(h3, ffn_g[1:2], ffn1["w1"], ffn1["w2"],
                                              loss_head=(replicated["final_g"], tgt), name="ffn1_loss")

    dh3, dhh1, dob1, dffn_g1 = _ffn_bwd_x(h3, dh4, ffn_g[1:2], rf1, ffn1["w1"], ffn1["w2"], name="ffn1_bwd_x")
    dw1_1, dw2_1 = _ffn_bwd_w(uf1, dhh1, rf1, dob1, name="ffn1_bwd_w")
    sent = src.emit("ffn1", dict(w1=dw1_1, w2=dw2_1))
    d_gla_w_out = _linear_bwd_w(og, dh3, name="gla_out_dw")
    dh2, dmix_g1, dz1, d_gate_w, d_gate_b, d_head_g = _gla_mid_bwd(
        z1, h2, mix_g[1:2], gla["gla_w_in_t"], dh3, gla["gla_w_out"], states, *gla_mid, after=sent, name="gla_mixer_bwd")
    d_gla_w_in_t = _linear_bwd_w(dz1, u1, name="gla_in_dw")
    sent = src.emit("gla", dict(gla_w_in_t=d_gla_w_in_t, gla_w_out=d_gla_w_out))
    dh1, dhh0, dob0, dffn_g0 = _ffn_bwd_x(h1, dh2, ffn_g[0:1], rf0, ffn0["w1"], ffn0["w2"], after=sent, name="ffn0_bwd_x")
    dw1_0, dw2_0 = _ffn_bwd_w(uf0, dhh0, rf0, dob0, name="ffn0_bwd_w")
    sent = src.emit("ffn0", dict(w1=dw1_0, w2=dw2_0))
    d_cp_w_out = _linear_bwd_w(cat, dh1, after=sent, name="cp_out_dw")
    sent = src.emit("cp_out", dict(cp_w_out=d_cp_w_out))
    dh0, dmix_g0, dz0, d_conv_w, d_conv_b, d_ln_g, d_ln_b, d_pool_w, d_pool_scale = _cp_mid_bwd(
        z0, cv0, h0, mix_g[0:1], cp["cp_w_in_t"], dh1, cp["cp_w_out"], *cp_mid, after=sent, name="cp_mixer_bwd")
    d_cp_w_in_t = _linear_bwd_w(dz0, u0, name="cp_in_dw")

    small = dict(
        mix_g=jnp.concatenate([dmix_g0, dmix_g1]), ffn_g=jnp.concatenate([dffn_g0, dffn_g1]), conv_b=d_conv_b, ln_g=d_ln_g,
        ln_b=d_ln_b, pool_w=d_pool_w, pool_scale=d_pool_scale, final_g=d_final_g, meta=dh0[PAD_ROWS:CHUNK], conv_w=d_conv_w,
        gate_w=d_gate_w, gate_b=d_gate_b, head_g=d_head_g)
    src.emit("cp", dict(cp_w_in_t=d_cp_w_in_t, small=small, loss=loss))
    return loss, dh0[CHUNK:], small


_REPLICATED = ("mix_norm_g", "ffn_norm_g", "cp_conv_b", "cp_ln_g", "cp_ln_b", "cp_pool_w", "cp_pool_scale", "final_norm_g")
_SMALL_SHARDED = ("meta_tokens", "cp_conv_w", "gla_gate_w2", "gla_gate_b", "gla_head_g")
_NAMES = ("meta_tokens", "mix_norm_g", "ffn_norm_g", "ffn_w1", "ffn_w2", "cp_w_in", "cp_conv_w", "cp_conv_b", "cp_ln_g",
          "cp_ln_b", "cp_pool_w", "cp_pool_scale", "cp_w_out", "gla_w_in", "gla_gate_w2", "gla_gate_b", "gla_head_g",
          "gla_w_out", "final_norm_g")
_SMALL_GRADS = ("mix_g", "ffn_g", "conv_b", "ln_g", "ln_b", "pool_w", "pool_scale", "final_g", "meta", "conv_w", "gate_w",
                "gate_b", "head_g")
_GROUPS = ("cp", "ffn0", "gla", "ffn1")
_TWO_LEG_GATHERS = ("cp", "ffn0", "ffn1")


class _Exchanges:
    def __init__(self, w, d):
        self.d = d
        small = [w[n].reshape(w[n].shape[-2:]) for n in _SMALL_SHARDED]
        self.small_shard_shapes = [w[n].shape for n in _SMALL_SHARDED]
        shards = dict(
            cp=[(w["cp_w_in"][0].T, BF16), (w["cp_w_out"][0], BF16)] + [(a, F32) for a in small],
            ffn0=[(w["ffn_w1"][0], BF16), (w["ffn_w2"][0], BF16)],
            gla=[(w["gla_w_in"][0].T, BF16), (w["gla_w_out"][0], BF16)],
            ffn1=[(w["ffn_w1"][1], BF16), (w["ffn_w2"][1], BF16)])
        self.gathers = {}
        self.sent = {}
        token = None
        for group in _GROUPS:
            lands = [_place_own(a, dtype, after=token, name=f"place_w_{group}_{k}")
                     for k, (a, dtype) in enumerate(shards[group])]
            plan = _plan_gather_first if group in _TWO_LEG_GATHERS else _plan_gather_direct
            self.gathers[group], token = _exchange_start(plan, None, lands, after=token, name=f"start_w_{group}")
        self.token = token

    def fetch(self, group, after):
        d = self.d
        after = (list(after) if isinstance(after, (list, tuple)) else [after]) + [self.token]
        _, got = _exchange_wait(self.gathers[group], after, name=f"wait_w_{group}")
        if group in _TWO_LEG_GATHERS:
            got = _exchange_now(_plan_gather_relay, got, name=f"relay_w_{group}")
        if group in ("ffn0", "ffn1"):
            return dict(w1=got[0], w2=got[1])
        if group == "gla":
            w_in_t = jnp.pad(got[0].reshape(-1, d), ((0, GATE_PAD - GATE_RANK), (0, 0)))
            return dict(gla_w_in_t=w_in_t, gla_w_out=got[1].reshape(d, d), gate_w=self.gate_w, gate_b=self.gate_b,
                        head_g=self.head_g)
        meta, conv_w, gate_w, self.gate_b, self.head_g = [_undo_column_split(a) for a in got[2:]]
        self.gate_w = jnp.pad(gate_w, ((0, GATE_PAD - GATE_RANK), (0, 0))).astype(BF16)
        return dict(cp_w_in_t=got[0].reshape(-1, d), cp_w_out=got[1].reshape(d, d), meta=meta,
                    conv_w=jnp.pad(conv_w, ((0, 1), (0, 0))))

    def emit(self, group, g):
        d = self.d
        if group in ("ffn0", "ffn1"):
            arrs = [g["w1"], g["w2"]]
        elif group == "gla":
            w_in_t = g["gla_w_in_t"][:3 * d + GATE_RANK]
            arrs = [w_in_t.reshape(N_DEV, -1, d), g["gla_w_out"].reshape(N_DEV, d // N_DEV, d)]
        elif group == "cp_out":
            arrs = [g["cp_w_out"].reshape(N_DEV, d // N_DEV, d)]
        else:
            s = dict(g["small"])
            s.update(pool_w=s["pool_w"][None], conv_w=s["conv_w"][:CONV_WIDTH], gate_w=s["gate_w"][:GATE_RANK])
            own = [s[n] for n in _SMALL_GRADS[:len(_REPLICATED)]]
            own += [_column_split(s[n]).reshape((N_DEV,) + shape)
                    for n, shape in zip(_SMALL_GRADS[len(_REPLICATED):], self.small_shard_shapes)]
            plans = [_plan_to_all] * len(_REPLICATED) + [_plan_split_to_all] * len(_SMALL_SHARDED)
            lands = [lax.empty((N_DEV,) + a.shape, F32) for a in own[:len(_REPLICATED)]]
            lands += [lax.empty(a.shape, F32) for a in own[len(_REPLICATED):]]
            own.append(g["loss"])
            plans.append(_plan_to_all)
            lands.append(lax.empty((N_DEV,) + g["loss"].shape, F32))
            self.small_sent, self.token = _exchange_start(plans, own, lands, after=self.token, name="start_g_small")
            arrs = [g["cp_w_in_t"].reshape(N_DEV, -1, d)]
        self.sent[group], self.token = _exchange_start(_plan_split_to_all, arrs, [lax.empty(a.shape, a.dtype) for a in arrs],
                                                       after=self.token, name=f"start_g_{group}")
        return self.token

    def finish(self, w, mom, var):
        out = {}
        after = self.token

        def landed(group):
            own, got = _exchange_wait(self.sent[group], after, name=f"wait_g_{group}")
            return list(zip(own, got))

        def adam(n, parts, behind=None, transposed=False):
            flip = (lambda a: jnp.transpose(a, (0, 2, 1))) if transposed else (lambda a: a)
            res = _reduce_adam(parts, flip(w[n]), flip(mom[n]), flip(var[n]), after=behind, name=f"adam_{n}")
            out[n] = tuple(flip(a) for a in res)
            return res[0]

        ffn1 = landed("ffn1")
        after = ffn1[0][1]
        gla = landed("gla")
        after = adam("gla_w_in", [gla[0]], transposed=True)
        after = adam("gla_w_out", [gla[1]], after)
        ffn0 = landed("ffn0")
        after = adam("ffn_w1", [ffn0[0], ffn1[0]])
        after = adam("ffn_w2", [ffn0[1], ffn1[1]], after)
        small_own, small_landed = _exchange_wait(self.small_sent, after, name="wait_g_small")
        names = _REPLICATED + _SMALL_SHARDED
        split = [False] * len(_REPLICATED) + [True] * len(_SMALL_SHARDED)
        small_new = _adam_small(small_own[:-1], small_landed[:-1], split, [w[n] for n in names], [mom[n] for n in names],
                                [var[n] for n in names], name="adam_small")
        out.update(zip(names, small_new))
        out["loss"] = _sum8(small_own[-1], small_landed[-1], name="sum_loss")[0, 0]

        after = small_new[0][0]
        cp_out = landed("cp_out")
        after = adam("cp_w_out", [cp_out[0]])
        cp = landed("cp")
        adam("cp_w_in", [cp[0]], transposed=True)
        return out


def kernel(x, meta_tokens, mix_norm_g, ffn_norm_g, ffn_w1, ffn_w2, cp_w_in, cp_conv_w, cp_conv_b, cp_ln_g, cp_ln_b, cp_pool_w, cp_pool_scale, cp_w_out, gla_w_in, gla_gate_w2, gla_gate_b, gla_head_g, gla_w_out, final_norm_g, loss_target, m_meta_tokens, m_mix_norm_g, m_ffn_norm_g, m_ffn_w1, m_ffn_w2, m_cp_w_in, m_cp_conv_w, m_cp_conv_b, m_cp_ln_g, m_cp_ln_b, m_cp_pool_w, m_cp_pool_scale, m_cp_w_out, m_gla_w_in, m_gla_gate_w2, m_gla_gate_b, m_gla_head_g, m_gla_w_out, m_final_norm_g, v_meta_tokens, v_mix_norm_g, v_ffn_norm_g, v_ffn_w1, v_ffn_w2, v_cp_w_in, v_cp_conv_w, v_cp_conv_b, v_cp_ln_g, v_cp_ln_b, v_cp_pool_w, v_cp_pool_scale, v_cp_w_out, v_gla_w_in, v_gla_gate_w2, v_gla_gate_b, v_gla_head_g, v_gla_w_out, v_final_norm_g):
    w = dict(meta_tokens=meta_tokens, mix_norm_g=mix_norm_g, ffn_norm_g=ffn_norm_g, ffn_w1=ffn_w1, ffn_w2=ffn_w2,
             cp_w_in=cp_w_in, cp_conv_w=cp_conv_w, cp_conv_b=cp_conv_b, cp_ln_g=cp_ln_g, cp_ln_b=cp_ln_b,
             cp_pool_w=cp_pool_w, cp_pool_scale=cp_pool_scale, cp_w_out=cp_w_out, gla_w_in=gla_w_in,
             gla_gate_w2=gla_gate_w2, gla_gate_b=gla_gate_b, gla_head_g=gla_head_g, gla_w_out=gla_w_out,
             final_norm_g=final_norm_g.reshape(1, -1))
    mom = dict(meta_tokens=m_meta_tokens, mix_norm_g=m_mix_norm_g, ffn_norm_g=m_ffn_norm_g, ffn_w1=m_ffn_w1, ffn_w2=m_ffn_w2,
               cp_w_in=m_cp_w_in, cp_conv_w=m_cp_conv_w, cp_conv_b=m_cp_conv_b, cp_ln_g=m_cp_ln_g, cp_ln_b=m_cp_ln_b,
               cp_pool_w=m_cp_pool_w, cp_pool_scale=m_cp_pool_scale, cp_w_out=m_cp_w_out, gla_w_in=m_gla_w_in,
               gla_gate_w2=m_gla_gate_w2, gla_gate_b=m_gla_gate_b, gla_head_g=m_gla_head_g, gla_w_out=m_gla_w_out,
               final_norm_g=m_final_norm_g.reshape(1, -1))
    var = dict(meta_tokens=v_meta_tokens, mix_norm_g=v_mix_norm_g, ffn_norm_g=v_ffn_norm_g, ffn_w1=v_ffn_w1, ffn_w2=v_ffn_w2,
               cp_w_in=v_cp_w_in, cp_conv_w=v_cp_conv_w, cp_conv_b=v_cp_conv_b, cp_ln_g=v_cp_ln_g, cp_ln_b=v_cp_ln_b,
               cp_pool_w=v_cp_pool_w, cp_pool_scale=v_cp_pool_scale, cp_w_out=v_cp_w_out, gla_w_in=v_gla_w_in,
               gla_gate_w2=v_gla_gate_w2, gla_gate_b=v_gla_gate_b, gla_head_g=v_gla_head_g, gla_w_out=v_gla_w_out,
               final_norm_g=v_final_norm_g.reshape(1, -1))
    d = x.shape[-1]
    replicated = dict(mix_g=w["mix_norm_g"], ffn_g=w["ffn_norm_g"], conv_b=w["cp_conv_b"], ln_g=w["cp_ln_g"],
                      ln_b=w["cp_ln_b"], pool_w=w["cp_pool_w"][0].astype(BF16), pool_scale=w["cp_pool_scale"],
                      final_g=w["final_norm_g"])
    exchanges = _Exchanges(w, d)
    _, grad_x, _ = _local_step(x[0], loss_target[0], replicated, exchanges)
    out = exchanges.finish(w, mom, var)
    loss = out.pop("loss")

    def leaf(n, k):
        a = out[n][k]
        return a.reshape(-1) if n == "final_norm_g" else a

    return (loss, grad_x[None], *[leaf(n, 0) for n in _NAMES], *[leaf(n, 1) for n in _NAMES],
            *[leaf(n, 2) for n in _NAMES], *[leaf(n, 3) for n in _NAMES])
```

```python
import functools

import jax
import jax.numpy as jnp
from jax import lax
from jax.experimental import pallas as pl
from jax.experimental.pallas import tpu as pltpu

F32, BF16 = jnp.float32, jnp.bfloat16
N_DEV = 8
CHUNK = 64
N_META = 16
PAD_ROWS = CHUNK - N_META
HALO = 32
EPS = 1e-5
CONV_WIDTH = 31
POOL_WINDOWS = (2, 4, 8, 16)
HEADS = 4
GATE_RANK = 16
GATE_NORM = 16.0
GATE_PAD = 128
ADAM_LR, ADAM_B1, ADAM_B2, ADAM_EPS, ADAM_WD, ADAM_STEP = 0.001, 0.9, 0.999, 1e-08, 0.01, 10
V7X_VMEM_LIMIT = 56 * 2 ** 20
LANE = 128


def _cparams(*sem):
    return pltpu.CompilerParams(dimension_semantics=sem, vmem_limit_bytes=V7X_VMEM_LIMIT)


def _row_tile(t, cap):
    best = CHUNK
    for r in range(CHUNK, min(t, cap) + 1, CHUNK):
        if t % r == 0:
            best = r
    return best


def _resident(shape):
    return pl.BlockSpec(shape, lambda *_: (0,) * len(shape), pipeline_mode=pl.Buffered(1))


def _dot(a, b):
    return jnp.dot(a, b, preferred_element_type=F32)


def _dot_nt(a, b):
    return lax.dot_general(a, b, (((1,), (1,)), ((), ())), preferred_element_type=F32)


def _dot_tn(a, b):
    return lax.dot_general(a, b, (((0,), (0,)), ((), ())), preferred_element_type=F32)


def _rowsum(a):
    return jnp.sum(a, axis=0, keepdims=True)


def _sigmoid(a):
    return 1.0 / (1.0 + jnp.exp(-a))


def _row_ids(tile, rt):
    return tile * rt + lax.broadcasted_iota(jnp.int32, (rt, 1), 0)


def _sds(shape, dtype):
    return jax.ShapeDtypeStruct(shape, dtype)


DW_ROWS = 1024


def _linear_bwd_w(x, dy, *, square_x=False, column_blocks=None, after=None, name):
    t, k = x.shape
    n = dy.shape[1]
    cut_k = k > n and column_blocks is None
    width = k if cut_k else n
    blk = n // column_blocks if column_blocks else max(c for c in (640, 512, 384, 256, LANE) if width % c == 0)
    dep_specs, deps = _dep_specs(after)

    def body(*refs):
        x_ref, dy_ref, o_ref, acc = refs[len(deps):]
        for c0 in range(0, t, DW_ROWS):
            rows = slice(c0, min(c0 + DW_ROWS, t))
            xv = x_ref[rows, :]
            if square_x:
                xv = xv.astype(F32)
                xv = xv * xv
            part = _dot_tn(xv.astype(BF16), dy_ref[rows, :].astype(BF16))
            if c0 == 0:
                acc[...] = part
            else:
                acc[...] += part
        o_ref[...] = acc[...].astype(BF16)

    out_shape = _sds((k, n), BF16)
    if cut_k:
        in_specs = [pl.BlockSpec((t, blk), lambda j: (0, j)), _resident((t, n))]
        out_specs = pl.BlockSpec((blk, n), lambda j: (j, 0))
        acc_shape = (blk, n)
    else:
        in_specs = [_resident((t, k)), pl.BlockSpec((t, blk), lambda j: (0, j))]
        out_specs = pl.BlockSpec((k, blk), lambda j: (0, j))
        acc_shape = (k, blk)
        if column_blocks:
            out_specs = pl.BlockSpec((None, k, blk), lambda j: (j, 0, 0))
            out_shape = _sds((column_blocks, k, blk), BF16)
    return pl.pallas_call(
        body, grid=(width // blk,), in_specs=dep_specs + in_specs, out_specs=out_specs, out_shape=out_shape,
        scratch_shapes=[pltpu.VMEM(acc_shape, F32)], compiler_params=_cparams("parallel"), name=name)(*deps, x, dy)


FFN_BLOCKS_PER_STEP = 2


def _ffn_fwd(h, gain, w1g, w2g, *, loss_head=None, name):
    t, d = h.shape
    f8 = w1g.shape[-1]
    rt = _row_tile(t, 832)
    nb = FFN_BLOCKS_PER_STEP
    nstep = N_DEV // nb

    def body(*refs):
        if loss_head is None:
            h_ref, g_ref, w1_ref, w2_ref, o_ref, u_ref, r_ref, acc_ref = refs
        else:
            h_ref, g_ref, w1_ref, w2_ref, fg_ref, t_ref, o_ref, u_ref, r_ref, loss_ref, dfg_ref, acc_ref = refs
        i, j = pl.program_id(0), pl.program_id(1)

        @pl.when(j == 0)
        def _():
            hv = h_ref[...]
            u_ref[...] = (hv * lax.rsqrt(jnp.mean(hv * hv, axis=-1, keepdims=True) + EPS) * g_ref[...]).astype(BF16)
            acc_ref[...] = jnp.zeros_like(acc_ref)

        part = None
        for b in range(nb):
            a = jnp.maximum(_dot(u_ref[...], w1_ref[b]), 0.0)
            r_ref[:, b * f8:(b + 1) * f8] = a.astype(BF16)
            term = _dot((a * a).astype(BF16), w2_ref[b])
            part = term if part is None else part + term
        acc_ref[...] += part

        @pl.when(j == nstep - 1)
        def _():
            y = h_ref[...] + acc_ref[...]
            if loss_head is None:
                o_ref[...] = y
                return

            @pl.when(i == 0)
            def _():
                loss_ref[...] = jnp.zeros_like(loss_ref)
                dfg_ref[...] = jnp.zeros_like(dfg_ref)

            rstd = lax.rsqrt(jnp.mean(y * y, axis=-1, keepdims=True) + EPS)
            xh = y * rstd
            err = jnp.where(_row_ids(i, rt) >= CHUNK, xh * fg_ref[...] - t_ref[...], 0.0)
            loss_ref[...] += (0.5 / d) * jnp.sum(err * err)
            dy = err * (1.0 / d)
            dfg_ref[...] += _rowsum(dy * xh)
            dxh = dy * fg_ref[...]
            o_ref[...] = rstd * (dxh - xh * jnp.mean(dxh * xh, axis=-1, keepdims=True))

    rows = lambda i, j: (i, 0)
    in_specs = [pl.BlockSpec((rt, d), rows), _resident((1, d)),
                pl.BlockSpec((nb, d, f8), lambda i, j: (j, 0, 0)), pl.BlockSpec((nb, f8, d), lambda i, j: (j, 0, 0))]
    out_specs = [pl.BlockSpec((rt, d), rows), pl.BlockSpec((rt, d), rows), pl.BlockSpec((rt, nb * f8), lambda i, j: (i, j))]
    out_shape = [_sds((t, d), F32), _sds((t, d), BF16), _sds((t, N_DEV * f8), BF16)]
    args = [h, gain, w1g, w2g]
    if loss_head is not None:
        in_specs += [_resident((1, d)), pl.BlockSpec((rt, d), rows)]
        out_specs += [pl.BlockSpec((8, LANE), lambda i, j: (0, 0)), pl.BlockSpec((1, d), lambda i, j: (0, 0))]
        out_shape += [_sds((8, LANE), F32), _sds((1, d), F32)]
        args += list(loss_head)
    return pl.pallas_call(
        body, grid=(t // rt, nstep), in_specs=in_specs, out_specs=out_specs, out_shape=out_shape,
        scratch_shapes=[pltpu.VMEM((rt, d), F32)],
        compiler_params=_cparams("arbitrary" if loss_head is not None else "parallel", "arbitrary"), name=name)(*args)


def _ffn_bwd_x(h, dout, gain, r, w1g, w2g, *, after=None, name):
    t, d = h.shape
    f8 = w1g.shape[-1]
    rt = _row_tile(t, 832)
    nb = FFN_BLOCKS_PER_STEP
    last = N_DEV // nb - 1
    dep_specs, deps = _dep_specs(after)

    def body(*refs):
        h_ref, do_ref, g_ref, r_ref, w1_ref, w2_ref, dh_ref, dhh_ref, dob_ref, dg_ref, du_ref = refs[len(deps):]
        i, j = pl.program_id(0), pl.program_id(1)

        @pl.when(j == 0)
        def _():
            dob_ref[...] = do_ref[...].astype(BF16)
            du_ref[...] = jnp.zeros_like(du_ref)

        part = None
        for b in range(nb):
            cols = slice(b * f8, (b + 1) * f8)
            dhh = (_dot_nt(dob_ref[...], w2_ref[b]) * (2.0 * r_ref[:, cols].astype(F32))).astype(BF16)
            dhh_ref[:, cols] = dhh
            term = _dot_nt(dhh, w1_ref[b])
            part = term if part is None else part + term
        du_ref[...] += part

        @pl.when(j == last)
        def _():
            @pl.when(i == 0)
            def _():
                dg_ref[...] = jnp.zeros_like(dg_ref)

            hv = h_ref[...]
            rstd = lax.rsqrt(jnp.mean(hv * hv, axis=-1, keepdims=True) + EPS)
            xh = hv * rstd
            du = du_ref[...]
            dg_ref[...] += _rowsum(du * xh)
            dxh = du * g_ref[...]
            dh_ref[...] = do_ref[...] + rstd * (dxh - xh * jnp.mean(dxh * xh, axis=-1, keepdims=True))

    rows = lambda i, j: (i, 0)
    return pl.pallas_call(
        body, grid=(t // rt, N_DEV // nb),
        in_specs=dep_specs + [
                  pl.BlockSpec((rt, d), rows), pl.BlockSpec((rt, d), rows), _resident((1, d)),
                  pl.BlockSpec((rt, nb * f8), lambda i, j: (i, j)),
                  pl.BlockSpec((nb, d, f8), lambda i, j: (j, 0, 0)),
                  pl.BlockSpec((nb, f8, d), lambda i, j: (j, 0, 0))],
        out_specs=[pl.BlockSpec((rt, d), rows), pl.BlockSpec((rt, nb * f8), lambda i, j: (i, j)),
                   pl.BlockSpec((rt, d), rows), pl.BlockSpec((1, d), lambda i, j: (0, 0))],
        out_shape=[_sds((t, d), F32), _sds((t, N_DEV * f8), BF16), _sds((t, d), BF16), _sds((1, d), F32)],
        scratch_shapes=[pltpu.VMEM((rt, d), F32)],
        compiler_params=_cparams("arbitrary", "arbitrary"), name=name)(*deps, h, dout, gain, r, w1g, w2g)


def _lane_blocks(width):
    lb = min(LANE, width)
    return [slice(s, s + lb) for s in range(0, width, lb)]


def _conv_rows(src_ref, w_ref, offset, dst_ref, nblk, width, bias_ref=None):
    def blk(rb, carry):
        base = pl.multiple_of(rb * CHUNK, CHUNK)
        for l, ls in enumerate(_lane_blocks(width)):
            acc = jnp.zeros((CHUNK, ls.stop - ls.start), F32)
            if bias_ref is not None:
                acc = acc + bias_ref[:, ls]
            for k in range(CONV_WIDTH):
                acc = acc + w_ref[k:k + 1, ls] * src_ref[l, pl.ds(base + offset(k), CHUNK), :]
            dst_ref[l, pl.ds(base, CHUNK), :] = acc
        return carry

    lax.fori_loop(0, nblk, blk, 0)


def _to_lane_blocks(ref, row0, value):
    for l, ls in enumerate(_lane_blocks(value.shape[1])):
        ref[l, row0:row0 + value.shape[0], :] = value[:, ls]


def _from_lane_blocks(ref):
    return jnp.concatenate([ref[l] for l in range(ref.shape[0])], axis=1)


def _pool_counts(rows, window):
    return jnp.clip(rows - PAD_ROWS + 1, 1, window).astype(F32)


def _trailing_sum(v, window):
    s, sh = v, 1
    while sh < window:
        s = s + pltpu.roll(s, sh, 0)
        sh *= 2
    return s


def _leading_sum(v, window):
    s, sh, n = v, 1, v.shape[0]
    while sh < window:
        s = s + pltpu.roll(s, n - sh, 0)
        sh *= 2
    return s


def _norm_project(h_ref, g_ref, w_t_ref, u_ref, z_ref):
    hv = h_ref[...]
    u = (hv * lax.rsqrt(jnp.mean(hv * hv, axis=-1, keepdims=True) + EPS) * g_ref[...]).astype(BF16)
    u_ref[...] = u
    z_ref[...] = _dot_nt(u, w_t_ref[...])


def _project_back(dz_ref, w_t_ref, h_ref, g_ref, dres_ref, dh_ref, dg_ref, first):
    dx = _dot(dz_ref[...], w_t_ref[...])
    hv = h_ref[...]
    rstd = lax.rsqrt(jnp.mean(hv * hv, axis=-1, keepdims=True) + EPS)
    xh = hv * rstd

    @pl.when(first)
    def _():
        dg_ref[...] = jnp.zeros_like(dg_ref)

    dg_ref[...] += _rowsum(dx * xh)
    dxh = dx * g_ref[...]
    dh_ref[...] = dres_ref[...] + rstd * (dxh - xh * jnp.mean(dxh * xh, axis=-1, keepdims=True))


def _cp_mid_fwd(h, gain, w_in_t, w_out, conv_w, conv_b, ln_g, ln_b, pool_w, pool_scale, *, name):
    t, d = h.shape
    ein = w_in_t.shape[0]
    cd = conv_b.shape[1]
    pd = pool_scale.shape[1]
    pg = pd // len(POOL_WINDOWS)
    rt = _row_tile(t, 320)

    def body(h_ref, g_ref, wi_ref, wo_ref, cw_ref, cb_ref, lg_ref, lb_ref, pw_ref, ps_ref,
             ho_ref, o_ref, z_ref, u_ref, cv_ref, gext, pext, conv_s):
        i = pl.program_id(0)

        @pl.when(i == 0)
        def _():
            _to_lane_blocks(gext, 0, jnp.zeros((HALO, cd), F32))
            pext[0:HALO, :] = jnp.zeros((HALO, pd), F32)

        _norm_project(h_ref, g_ref, wi_ref, u_ref, z_ref)

        _to_lane_blocks(gext, HALO, z_ref[:, 0:cd] * _sigmoid(z_ref[:, cd:2 * cd]))
        pext[HALO:HALO + rt, :] = z_ref[:, 2 * cd:]
        _conv_rows(gext, cw_ref, lambda k: k + HALO - (CONV_WIDTH - 1), conv_s, rt // CHUNK, cd, cb_ref)
        cv = _from_lane_blocks(conv_s)
        cv_ref[...] = cv
        xc = cv - jnp.mean(cv, axis=-1, keepdims=True)
        y = xc * lax.rsqrt(jnp.mean(xc * xc, axis=-1, keepdims=True) + EPS) * lg_ref[...] + lb_ref[...]
        rows = _row_ids(i, rt)
        a = jnp.where(rows >= PAD_ROWS, y * _sigmoid(y), 0.0)
        o_ref[:, 0:cd] = a.astype(BF16)
        for gi, window in enumerate(POOL_WINDOWS):
            ls = slice(gi * pg, (gi + 1) * pg)
            v = pext[:, ls]
            tm = _trailing_sum(v, window)[HALO:] / _pool_counts(rows, window) - v[HALO:]
            p = _dot(tm.astype(BF16), pw_ref[gi]) * ps_ref[:, ls]
            o_ref[:, cd + gi * pg:cd + (gi + 1) * pg] = p.astype(BF16)
        ho_ref[...] = h_ref[...] + _dot(o_ref[...], wo_ref[...])
        gext[:, 0:HALO, :] = gext[:, rt:rt + HALO, :]
        pext[0:HALO, :] = pext[rt:rt + HALO, :]

    nl, lb = len(_lane_blocks(cd)), min(LANE, cd)
    rows = lambda i: (i, 0)
    return pl.pallas_call(
        body, grid=(t // rt,),
        in_specs=[pl.BlockSpec((rt, d), rows), _resident((1, d)), _resident(w_in_t.shape), _resident(w_out.shape),
                  _resident(conv_w.shape), _resident((1, cd)),
                  _resident((1, cd)), _resident((1, cd)), _resident(pool_w.shape), _resident((1, pd))],
        out_specs=[pl.BlockSpec((rt, d), rows), pl.BlockSpec((rt, cd + pd), rows), pl.BlockSpec((rt, ein), rows),
                   pl.BlockSpec((rt, d), rows), pl.BlockSpec((rt, cd), rows)],
        out_shape=[_sds((t, d), F32), _sds((t, cd + pd), BF16), _sds((t, ein), F32), _sds((t, d), BF16),
                   _sds((t, cd), F32)],
        scratch_shapes=[pltpu.VMEM((nl, rt + HALO, lb), F32), pltpu.VMEM((rt + HALO, pd), F32),
                        pltpu.VMEM((nl, rt, lb), F32)],
        compiler_params=_cparams("arbitrary"), name=name)(h, gain, w_in_t, w_out, conv_w, conv_b, ln_g, ln_b, pool_w,
                                                          pool_scale)


def _cp_mid_bwd(z, cv, h, gain, w_in_t, dh, w_out, conv_w, conv_b, ln_g, ln_b, pool_w, pool_scale, *, after=None, name):
    t, ein = z.shape
    cd = conv_b.shape[1]
    pd = pool_scale.shape[1]
    pg = pd // len(POOL_WINDOWS)
    rt = _row_tile(t, 320)
    ntile = t // rt
    per = rt // CHUNK
    dep_specs, deps = _dep_specs(after)

    def body(*refs):
        (z_ref, zh_ref, cv_ref, h_ref, g_ref, wi_ref, dh_ref, wo_ref, cw_ref, cb_ref, lg_ref, lb_ref, pw_ref, ps_ref,
         dhi_ref, dg_ref, dz_ref, dcw_ref, dcb_ref, dlg_ref, dlb_ref, dpw_ref, dps_ref,
         gext, pext, conv_s, dcv, dsp) = refs[len(deps):]
        step = pl.program_id(0)
        tile = ntile - 1 - step
        dcat = _dot_nt(dh_ref[...].astype(BF16), wo_ref[...])

        @pl.when(step == 0)
        def _():
            for ref in (dcw_ref, dcb_ref, dlg_ref, dlb_ref, dpw_ref, dps_ref):
                ref[...] = jnp.zeros_like(ref)
            _to_lane_blocks(dcv, rt, jnp.zeros((HALO, cd), F32))
            dsp[rt:rt + HALO, :] = jnp.zeros((HALO, pd), F32)

        keep = jnp.where(tile > 0, 1.0, 0.0)
        zh = zh_ref[CHUNK - HALO:CHUNK, :]
        _to_lane_blocks(gext, 0, keep * zh[:, 0:cd] * _sigmoid(zh[:, cd:2 * cd]))
        pext[0:HALO, :] = keep * zh[:, 2 * cd:]
        za = z_ref[:, 0:cd]
        sg = _sigmoid(z_ref[:, cd:2 * cd])
        _to_lane_blocks(gext, HALO, za * sg)
        pext[HALO:HALO + rt, :] = z_ref[:, 2 * cd:]
        cv = cv_ref[...]
        xc = cv - jnp.mean(cv, axis=-1, keepdims=True)
        rstd = lax.rsqrt(jnp.mean(xc * xc, axis=-1, keepdims=True) + EPS)
        xh = xc * rstd
        y = xh * lg_ref[...] + lb_ref[...]
        sy = _sigmoid(y)
        rows = _row_ids(tile, rt)
        da = jnp.where(rows >= PAD_ROWS, dcat[:, 0:cd], 0.0)
        dy = da * (sy * (1.0 + y * (1.0 - sy)))
        dlg_ref[...] += _rowsum(dy * xh)
        dlb_ref[...] += _rowsum(dy)
        dxh = dy * lg_ref[...]
        dconv = rstd * (dxh - jnp.mean(dxh, axis=-1, keepdims=True) - xh * jnp.mean(dxh * xh, axis=-1, keepdims=True))
        dcb_ref[...] += _rowsum(dconv)
        _to_lane_blocks(dcv, 0, dconv)
        for l, ls in enumerate(_lane_blocks(cd)):
            def acc_rows(rb, accs, l=l):
                base = pl.multiple_of(rb * CHUNK, CHUNK)
                d_blk = dcv[l, pl.ds(base, CHUNK), :]
                out = []
                for k in range(CONV_WIDTH):
                    prod = d_blk * gext[l, pl.ds(base + k + HALO - (CONV_WIDTH - 1), CHUNK), :]
                    part = prod[0:8]
                    for s in range(8, CHUNK, 8):
                        part = part + prod[s:s + 8]
                    out.append(accs[k] + part)
                return tuple(out)

            zero = jnp.zeros((8, ls.stop - ls.start), F32)
            accs = lax.fori_loop(0, per, acc_rows, (zero,) * CONV_WIDTH)
            for k in range(CONV_WIDTH):
                dcw_ref[k:k + 1, ls] += _rowsum(accs[k])
        _conv_rows(dcv, cw_ref, lambda k: CONV_WIDTH - 1 - k, conv_s, per, cd)
        dglu = _from_lane_blocks(conv_s)
        dz_ref[:, 0:cd] = (dglu * sg).astype(BF16)
        dz_ref[:, cd:2 * cd] = (dglu * za * sg * (1.0 - sg)).astype(BF16)
        dcv[:, rt:rt + HALO, :] = dcv[:, 0:HALO, :]
        for gi, window in enumerate(POOL_WINDOWS):
            ls = slice(gi * pg, (gi + 1) * pg)
            v = pext[:, ls]
            cnt = _pool_counts(rows, window)
            tm = (_trailing_sum(v, window)[HALO:] / cnt - v[HALO:]).astype(BF16)
            dp = dcat[:, cd + gi * pg:cd + (gi + 1) * pg]
            dps_ref[:, ls] += _rowsum(dp * _dot(tm, pw_ref[gi]))
            dpl = (dp * ps_ref[:, ls]).astype(BF16)
            dpw_ref[gi] += _dot_tn(tm, dpl)
            dtm = _dot_nt(dpl, pw_ref[gi])
            dsp[0:rt, ls] = dtm / cnt
            dpin = _leading_sum(dsp[:, ls], window)[0:rt] - dtm
            dz_ref[:, 2 * cd + gi * pg:2 * cd + (gi + 1) * pg] = dpin.astype(BF16)
        dsp[rt:rt + HALO, :] = dsp[0:HALO, :]
        _project_back(dz_ref, wi_ref, h_ref, g_ref, dh_ref, dhi_ref, dg_ref, step == 0)

    d = h.shape[1]
    back = lambda i: (ntile - 1 - i, 0)
    halo_idx = lambda i: (jnp.maximum((ntile - 1 - i) * per - 1, 0), 0)
    const2 = lambda i: (0, 0)
    nl, lb = len(_lane_blocks(cd)), min(LANE, cd)
    return pl.pallas_call(
        body, grid=(ntile,),
        in_specs=dep_specs + [
                  pl.BlockSpec((rt, ein), back), pl.BlockSpec((CHUNK, ein), halo_idx), pl.BlockSpec((rt, cd), back),
                  pl.BlockSpec((rt, d), back),
                  _resident((1, d)), _resident(w_in_t.shape), pl.BlockSpec((rt, d), back), _resident(w_out.shape),
                  _resident(conv_w.shape), _resident((1, cd)), _resident((1, cd)), _resident((1, cd)),
                  _resident(pool_w.shape), _resident((1, pd))],
        out_specs=[pl.BlockSpec((rt, d), back), pl.BlockSpec((1, d), const2),
                   pl.BlockSpec((rt, ein), back), pl.BlockSpec(conv_w.shape, const2), pl.BlockSpec((1, cd), const2),
                   pl.BlockSpec((1, cd), const2), pl.BlockSpec((1, cd), const2),
                   pl.BlockSpec(pool_w.shape, lambda i: (0, 0, 0)), pl.BlockSpec((1, pd), const2)],
        out_shape=[_sds((t, d), F32), _sds((1, d), F32),
                   _sds((t, ein), BF16), _sds(conv_w.shape, F32), _sds((1, cd), F32), _sds((1, cd), F32),
                   _sds((1, cd), F32), _sds(pool_w.shape, F32), _sds((1, pd), F32)],
        scratch_shapes=[pltpu.VMEM((nl, rt + HALO, lb), F32), pltpu.VMEM((rt + HALO, pd), F32), pltpu.VMEM((nl, rt, lb), F32),
                        pltpu.VMEM((nl, rt + HALO, lb), F32), pltpu.VMEM((rt + HALO, pd), F32)],
        compiler_params=_cparams("arbitrary"), name=name)(*deps, z, z, cv, h, gain, w_in_t, dh, w_out, conv_w, conv_b, ln_g,
                                                          ln_b, pool_w, pool_scale)


def _log_decay(r, gw_ref, gb_ref, rows):
    gp = _dot(r.astype(BF16), gw_ref[...]) + gb_ref[...]
    log_sig = jnp.minimum(gp, 0.0) - jnp.log(1.0 + jnp.exp(-jnp.abs(gp)))
    return gp, jnp.where(rows >= PAD_ROWS, log_sig / GATE_NORM, 0.0)


def _tri(strict):
    r = lax.broadcasted_iota(jnp.int32, (CHUNK, CHUNK), 0)
    c = lax.broadcasted_iota(jnp.int32, (CHUNK, CHUNK), 1)
    return jnp.where(c < r if strict else c <= r, 1.0, 0.0).astype(BF16)


def _tri_dot(tri, a):
    hi = a.astype(BF16)
    rest = a - hi.astype(F32)
    mid = rest.astype(BF16)
    lo = (rest - mid.astype(F32)).astype(BF16)
    return _dot(tri, hi) + _dot(tri, mid) + _dot(tri, lo)


def _gla_mid_fwd(h, gain, w_in_t, w_out, gate_w, gate_b, head_g, *, name):
    t = h.shape[0]
    zw = w_in_t.shape[0]
    dk = gate_b.shape[1]
    hv = head_g.shape[1]
    hk = dk // HEADS
    dv = hv * HEADS
    r_at = 2 * dk + 2 * dv
    rt = _row_tile(t, 320)
    per = rt // CHUNK
    scale = hk ** -0.5

    def body(h_ref, g_ref, wi_ref, wo_ref, gw_ref, gb_ref, hg_ref, ho_ref, o_ref, st_ref, z_ref, u_ref,
             s_ref, la_ref, dec_ref):
        i = pl.program_id(0)

        @pl.when(i == 0)
        def _():
            s_ref[...] = jnp.zeros_like(s_ref)

        _norm_project(h_ref, g_ref, wi_ref, u_ref, z_ref)

        _, la = _log_decay(z_ref[:, r_at:r_at + GATE_PAD], gw_ref, gb_ref, _row_ids(i, rt))
        la_ref[...] = la
        tri = _tri(False)

        def chunk_rows(c):
            return slice(c * CHUNK, (c + 1) * CHUNK)

        def decays(c, carry):
            rows = chunk_rows(c)
            la_c = la_ref[rows, :]
            cum = _tri_dot(tri, la_c)
            dec_ref[rows, :] = jnp.exp(_rowsum(la_c) - cum)
            return carry

        def states(c, carry):
            rows = chunk_rows(c)
            etot = jnp.exp(_rowsum(la_ref[rows, :]))
            for hd in range(HEADS):
                ks = slice(hd * hk, (hd + 1) * hk)
                kd = z_ref[rows, dk + hd * hk:dk + (hd + 1) * hk] * dec_ref[rows, ks]
                v = z_ref[rows, 2 * dk + hd * hv:2 * dk + (hd + 1) * hv]
                s_new = s_ref[hd] * etot[:, ks] + _dot_tn(v.astype(BF16), kd.astype(BF16))
                s_ref[hd] = s_new
                st_ref[c, hd] = s_new
            return carry

        def outputs(c, carry):
            rows = chunk_rows(c)
            for hd in range(HEADS):
                q = z_ref[rows, hd * hk:(hd + 1) * hk] * scale
                g = z_ref[rows, 2 * dk + dv + hd * hv:2 * dk + dv + (hd + 1) * hv]
                o = _dot_nt(q.astype(BF16), st_ref[c, hd].astype(BF16))
                on = o * lax.rsqrt(jnp.mean(o * o, axis=-1, keepdims=True) + EPS) * hg_ref[...]
                o_ref[rows, hd * hv:(hd + 1) * hv] = (on * (g * _sigmoid(g))).astype(BF16)
            return carry

        for phase in (decays, states, outputs):
            for c in range(per):
                phase(c, 0)
        ho_ref[...] = h_ref[...] + _dot(o_ref[...], wo_ref[...])

    d = h.shape[1]
    rows = lambda i: (i, 0)
    return pl.pallas_call(
        body, grid=(t // rt,),
        in_specs=[pl.BlockSpec((rt, d), rows), _resident((1, d)), _resident(w_in_t.shape), _resident(w_out.shape),
                  _resident(gate_w.shape), _resident((1, dk)), _resident((1, hv))],
        out_specs=[pl.BlockSpec((rt, d), rows), pl.BlockSpec((rt, dv), rows),
                   pl.BlockSpec((per, HEADS, hv, hk), lambda i: (i, 0, 0, 0)), pl.BlockSpec((rt, zw), rows),
                   pl.BlockSpec((rt, d), rows)],
        out_shape=[_sds((t, d), F32), _sds((t, dv), BF16), _sds((t // CHUNK, HEADS, hv, hk), F32), _sds((t, zw), F32),
                   _sds((t, d), BF16)],
        scratch_shapes=[pltpu.VMEM((HEADS, hv, hk), F32), pltpu.VMEM((rt, dk), F32), pltpu.VMEM((rt, dk), F32)],
        compiler_params=_cparams("arbitrary"), name=name)(h, gain, w_in_t, w_out, gate_w, gate_b, head_g)


def _gla_mid_bwd(z, h, gain, w_in_t, dh, w_out, states, gate_w, gate_b, head_g, *, after=None, name):
    t = z.shape[0]
    dk = gate_b.shape[1]
    hv = head_g.shape[1]
    hk = dk // HEADS
    dv = hv * HEADS
    r_at = 2 * dk + 2 * dv
    rt = _row_tile(t, 320)
    ntile = t // rt
    per = rt // CHUNK
    scale = hk ** -0.5
    dep_specs, deps = _dep_specs(after)

    def body(*refs):
        (z_ref, h_ref, g_ref, wi_ref, dh_ref, wo_ref, st_ref, stp_ref, gw_ref, gb_ref, hg_ref,
         dhi_ref, dg_ref, dz_ref, dgw_ref, dgb_ref, dhg_ref,
         ds_ref, la_ref, dla_ref, dec_ref, dos_ref, e_ref, do_ref) = refs[len(deps):]
        step = pl.program_id(0)
        tile = ntile - 1 - step
        do_ref[...] = _dot_nt(dh_ref[...].astype(BF16), wo_ref[...])

        @pl.when(step == 0)
        def _():
            ds_ref[...] = jnp.zeros_like(ds_ref)
            dgw_ref[...] = jnp.zeros_like(dgw_ref)
            dgb_ref[...] = jnp.zeros_like(dgb_ref)
            dhg_ref[...] = jnp.zeros_like(dhg_ref)

        rows_id = _row_ids(tile, rt)
        r = z_ref[:, r_at:r_at + GATE_PAD]
        gp, la = _log_decay(r, gw_ref, gb_ref, rows_id)
        la_ref[...] = la
        tri, tri_strict = _tri(False), _tri(True)
        keep = jnp.where(tile > 0, 1.0, 0.0)

        def chunk_rows(c):
            return slice(c * CHUNK, (c + 1) * CHUNK)

        def recompute(c, dhg):
            rows = chunk_rows(c)
            la_c = la_ref[rows, :]
            cum = _tri_dot(tri, la_c)
            dec_ref[rows, :] = jnp.exp(_rowsum(la_c) - cum)
            for hd in range(HEADS):
                q = (z_ref[rows, hd * hk:(hd + 1) * hk] * scale).astype(BF16)
                g = z_ref[rows, 2 * dk + dv + hd * hv:2 * dk + dv + (hd + 1) * hv]
                s_b = st_ref[c, hd].astype(BF16)
                o = _dot_nt(q, s_b)
                rstd = lax.rsqrt(jnp.mean(o * o, axis=-1, keepdims=True) + EPS)
                oh = o * rstd
                sg = _sigmoid(g)
                d_og = do_ref[rows, hd * hv:(hd + 1) * hv]
                dz_ref[rows, 2 * dk + dv + hd * hv:2 * dk + dv + (hd + 1) * hv] = (
                    d_og * oh * hg_ref[...] * (sg * (1.0 + g * (1.0 - sg)))).astype(BF16)
                don = d_og * (g * sg)
                dhg = dhg + _rowsum(don * oh)
                doh = don * hg_ref[...]
                d_o = (rstd * (doh - oh * jnp.mean(doh * oh, axis=-1, keepdims=True))).astype(BF16)
                dos_ref[rows, hd * hv:(hd + 1) * hv] = d_o
                dz_ref[rows, hd * hk:(hd + 1) * hk] = (_dot(d_o, s_b) * scale).astype(BF16)
            return dhg

        def recurrence(cc, carry):
            c = per - 1 - cc
            rows = chunk_rows(c)
            etot = jnp.exp(_rowsum(la_ref[rows, :]))
            for hd in range(HEADS):
                ks = slice(hd * hk, (hd + 1) * hk)
                q = (z_ref[rows, hd * hk:(hd + 1) * hk] * scale).astype(BF16)
                dec = dec_ref[rows, ks]
                kd = z_ref[rows, dk + hd * hk:dk + (hd + 1) * hk] * dec
                v = z_ref[rows, 2 * dk + hd * hv:2 * dk + (hd + 1) * hv].astype(BF16)
                s_prev = st_ref[c - 1, hd] if c > 0 else keep * stp_ref[0, hd]
                ds_t = ds_ref[hd] + _dot_tn(dos_ref[rows, hd * hv:(hd + 1) * hv], q)
                ds_b = ds_t.astype(BF16)
                dkd = _dot(v, ds_b)
                dz_ref[rows, 2 * dk + hd * hv:2 * dk + (hd + 1) * hv] = _dot_nt(kd.astype(BF16), ds_b).astype(BF16)
                dtot = etot[:, ks] * _rowsum(ds_t * s_prev)
                ds_ref[hd] = ds_t * etot[:, ks]
                dz_ref[rows, dk + hd * hk:dk + (hd + 1) * hk] = (dkd * dec).astype(BF16)
                e_ref[rows, ks] = dkd * kd
                dla_ref[rows, ks] = jnp.broadcast_to(dtot, (CHUNK, hk))
            return carry

        def decay_cotangent(c, carry):
            rows = chunk_rows(c)
            dla_ref[rows, :] += _tri_dot(tri_strict, e_ref[rows, :])
            return carry

        dhg = jnp.zeros((1, hv), F32)
        for c in range(per):
            dhg = recompute(c, dhg)
        dhg_ref[...] += dhg
        for phase in (recurrence, decay_cotangent):
            for c in range(per):
                phase(c, 0)
        dla = jnp.where(rows_id >= PAD_ROWS, dla_ref[...], 0.0)
        dgp = dla * (1.0 / GATE_NORM) * (1.0 - _sigmoid(gp))
        dgb_ref[...] += _rowsum(dgp)
        dgp_b = dgp.astype(BF16)
        dgw_ref[...] += _dot_tn(r.astype(BF16), dgp_b)
        dz_ref[:, r_at:r_at + GATE_PAD] = _dot_nt(dgp_b, gw_ref[...]).astype(BF16)
        _project_back(dz_ref, wi_ref, h_ref, g_ref, dh_ref, dhi_ref, dg_ref, step == 0)

    d = h.shape[1]
    back = lambda i: (ntile - 1 - i, 0)
    const2 = lambda i: (0, 0)
    return pl.pallas_call(
        body, grid=(ntile,),
        in_specs=dep_specs + [
                  pl.BlockSpec((rt, z.shape[1]), back), pl.BlockSpec((rt, d), back), _resident((1, d)),
                  _resident(w_in_t.shape), pl.BlockSpec((rt, d), back), _resident(w_out.shape),
                  pl.BlockSpec((per, HEADS, hv, hk), lambda i: (ntile - 1 - i, 0, 0, 0)),
                  pl.BlockSpec((1, HEADS, hv, hk), lambda i: (jnp.maximum((ntile - 1 - i) * per - 1, 0), 0, 0, 0)),
                  _resident(gate_w.shape), _resident((1, dk)), _resident((1, hv))],
        out_specs=[pl.BlockSpec((rt, d), back), pl.BlockSpec((1, d), const2), pl.BlockSpec((rt, z.shape[1]), back),
                   pl.BlockSpec(gate_w.shape, const2), pl.BlockSpec((1, dk), const2), pl.BlockSpec((1, hv), const2)],
        out_shape=[_sds((t, d), F32), _sds((1, d), F32), _sds(z.shape, BF16), _sds(gate_w.shape, F32),
                   _sds((1, dk), F32), _sds((1, hv), F32)],
        scratch_shapes=[pltpu.VMEM((HEADS, hv, hk), F32), pltpu.VMEM((rt, dk), F32), pltpu.VMEM((rt, dk), F32),
                        pltpu.VMEM((rt, dk), F32), pltpu.VMEM((rt, dv), BF16), pltpu.VMEM((rt, dk), F32),
                        pltpu.VMEM((rt, dv), F32)],
        compiler_params=_cparams("arbitrary"), name=name)(*deps, z, h, gain, w_in_t, dh, w_out, states, states, gate_w,
                                                          gate_b, head_g)


def _adamw_math(w, g, m, v):
    m = ADAM_B1 * m + (1.0 - ADAM_B1) * g
    v = ADAM_B2 * v + (1.0 - ADAM_B2) * (g * g)
    m_hat = m / (1.0 - ADAM_B1 ** ADAM_STEP)
    v_hat = v / (1.0 - ADAM_B2 ** ADAM_STEP)
    return -ADAM_LR * (m_hat / (jnp.sqrt(v_hat) + ADAM_EPS) + ADAM_WD * w), m, v


N_CHIP = N_DEV // 2
BLOCK_ELEMS = 128 * 1024


def _my_slot():
    return 4 * lax.axis_index("x") + 2 * lax.axis_index("y") + lax.axis_index("c")


def _row_block(r, c):
    cap = max(8, BLOCK_ELEMS // (-(-c // LANE) * LANE))
    return max([b for b in range(8, r + 1, 8) if r % b == 0 and b <= cap] or [r])


def _blocks(r, c):
    rb = _row_block(r, c)
    if rb < r or r * c <= BLOCK_ELEMS:
        return rb, c
    return r, max([b for b in (512, 256, LANE) if c % b == 0 and r * b <= BLOCK_ELEMS] or [c])


def _reduce_adam(parts, w, m, v, *, after=None, name):
    nl, r, c = w.shape
    rb, cb = _blocks(r, c)
    dep_specs, deps = _dep_specs(after)

    def body(*refs):
        me = refs[0][0]
        refs = refs[1 + len(deps):]
        p_refs = refs[:2 * nl]
        w_ref, m_ref, v_ref, g_out, d_out, m_out, v_out = refs[2 * nl:]
        layer = pl.program_id(0)
        for li in range(nl):
            @pl.when(layer == li)
            def _(li=li):
                own_ref, land_ref = p_refs[2 * li], p_refs[2 * li + 1]
                mine = own_ref[...].astype(F32)
                g = None
                for dev in range(N_DEV):
                    term = jnp.where(me == dev, mine, land_ref[dev].astype(F32))
                    g = term if g is None else g + term
                g_out[...] = g
                d_out[...], m_out[...], v_out[...] = _adamw_math(w_ref[...], g, m_ref[...], v_ref[...])

    blk = pl.BlockSpec((None, rb, cb), lambda l, i, j, me: (l, i, j))
    p_specs = []
    for li in range(nl):
        p_specs += [
            pl.BlockSpec((None, rb, cb), lambda l, i, j, me, li=li: (me[0], jnp.where(l == li, i, 0), jnp.where(l == li, j, 0))),
            pl.BlockSpec((N_DEV, rb, cb), lambda l, i, j, me, li=li: (0, jnp.where(l == li, i, 0), jnp.where(l == li, j, 0)))]
    flat = [p for pair in parts for p in pair]
    grid_spec = pltpu.PrefetchScalarGridSpec(
        num_scalar_prefetch=1, grid=(nl, r // rb, c // cb), in_specs=dep_specs + p_specs + [blk, blk, blk],
        out_specs=[blk] * 4)
    return pl.pallas_call(
        body, grid_spec=grid_spec, out_shape=[_sds(w.shape, F32)] * 4,
        compiler_params=_cparams("arbitrary", "arbitrary", "arbitrary"), name=name)(
        _my_slot().reshape(1), *deps, *flat, w, m, v)


def _sum8(own, landed, *, name):
    def body(own_ref, land_ref, o_ref):
        me = _my_slot()
        total = None
        for dev in range(N_DEV):
            term = jnp.where(me == dev, own_ref[...], land_ref[dev])
            total = term if total is None else total + term
        o_ref[...] = total

    return pl.pallas_call(body, out_shape=_sds(own.shape, F32), name=name)(own, landed)


def _adam_small(own, landed, split, w, m, v, *, name):
    n = len(w)

    def body(*refs):
        own_refs, land_refs, w_refs, m_refs, v_refs = (refs[k * n:(k + 1) * n] for k in range(5))
        outs = refs[5 * n:]
        me = _my_slot()
        for k in range(n):
            mine = own_refs[k][me] if split[k] else own_refs[k][...]
            g = None
            for dev in range(N_DEV):
                term = jnp.where(me == dev, mine, land_refs[k][dev])
                g = term if g is None else g + term
            outs[4 * k][...] = g
            outs[4 * k + 1][...], outs[4 * k + 2][...], outs[4 * k + 3][...] = _adamw_math(
                w_refs[k][...], g, m_refs[k][...], v_refs[k][...])

    out = pl.pallas_call(body, out_shape=[_sds(a.shape, F32) for a in w for _ in range(4)],
                         compiler_params=pltpu.CompilerParams(vmem_limit_bytes=V7X_VMEM_LIMIT), name=name)(
        *own, *landed, *w, *m, *v)
    return [tuple(out[4 * k:4 * k + 4]) for k in range(n)]


_HBM = pl.BlockSpec(memory_space=pltpu.HBM)
_SEM = pl.BlockSpec(memory_space=pltpu.SEMAPHORE)
_DATAFLOW = pltpu.SideEffectType.DATAFLOW_SIDE_EFFECTING


def _plan_to_all(src, land):
    x, y, c = lax.axis_index("x"), lax.axis_index("y"), lax.axis_index("c")
    return [(src, land.at[_my_slot()], (x ^ ((d >> 2) & 1), y ^ ((d >> 1) & 1), c ^ (d & 1))) for d in range(1, N_DEV)]


def _plan_split_to_all(src, land):
    x, y, c = lax.axis_index("x"), lax.axis_index("y"), lax.axis_index("c")
    peers = [(x ^ ((d >> 2) & 1), y ^ ((d >> 1) & 1), c ^ (d & 1)) for d in range(1, N_DEV)]
    return [(src.at[4 * px + 2 * py + pc], land.at[_my_slot()], (px, py, pc)) for px, py, pc in peers]


_PLAN_COPIES = {_plan_to_all: N_DEV - 1, _plan_split_to_all: N_DEV - 1}


def _plans(plan, n):
    return list(plan) if isinstance(plan, (list, tuple)) else [plan] * n


def _exchange_copies(plan, ins, lands, send, recv):
    copies, sem = [], 0
    for p, src, land in zip(_plans(plan, len(lands)), ins, lands):
        for s, dst, dev in p(src, land):
            copies.append(pltpu.make_async_remote_copy(
                src_ref=s, dst_ref=dst, send_sem=send.at[sem], recv_sem=recv.at[sem],
                device_id=dev, device_id_type=pl.DeviceIdType.MESH))
            sem += 1
    return copies


def _place_own(a, dtype, *, after=None, name):
    r, c = a.shape
    rb = _row_block(r, c)
    dep_specs, deps = _dep_specs(after)

    def body(*refs):
        a_ref, o_ref = refs[1 + len(deps):]
        o_ref[...] = a_ref[...].astype(dtype)

    grid_spec = pltpu.PrefetchScalarGridSpec(
        num_scalar_prefetch=1, grid=(r // rb,), in_specs=dep_specs + [pl.BlockSpec((rb, c), lambda i, me: (i, 0))],
        out_specs=pl.BlockSpec((None, rb, c), lambda i, me: (me[0], i, 0)))
    return pl.pallas_call(body, grid_spec=grid_spec, out_shape=_sds((N_DEV, r, c), dtype),
                          compiler_params=_cparams("arbitrary"), name=name)(_my_slot().reshape(1), *deps, a)


def _plan_gather_first(land, _):
    x, y, c = lax.axis_index("x"), lax.axis_index("y"), lax.axis_index("c")
    mine = land.at[_my_slot()]
    return [(mine, mine, (x, y, 1 - c))] + [(mine, mine, (x ^ (d >> 1), y ^ (d & 1), c)) for d in range(1, N_CHIP)]


def _plan_gather_relay(land, _):
    x, y, c = lax.axis_index("x"), lax.axis_index("y"), lax.axis_index("c")
    slots = [land.at[4 * (x ^ (d >> 1)) + 2 * (y ^ (d & 1)) + c] for d in range(1, N_CHIP)]
    return [(s, s, (x, y, 1 - c)) for s in slots]


def _plan_gather_direct(land, _):
    return _plan_to_all(land.at[_my_slot()], land)


_PLAN_COPIES[_plan_gather_first] = N_CHIP
_PLAN_COPIES[_plan_gather_relay] = N_CHIP - 1
_PLAN_COPIES[_plan_gather_direct] = N_DEV - 1


def _exchange_start(plan, arrs, lands, *, after=None, name):
    bufs = list(lands) if arrs is None else list(arrs) + list(lands)
    n, nb = len(lands), len(bufs)
    nsem = sum(_PLAN_COPIES[p] for p in _plans(plan, n))
    dep_specs, deps = _dep_specs(after)

    def body(*refs):
        ins, land_refs = refs[:n], refs[nb - n:nb]
        send, recv = refs[nb + len(deps)], refs[nb + len(deps) + 1]
        for cp in _exchange_copies(plan, ins, land_refs, send, recv):
            cp.start()
        refs[-1][...] = jnp.zeros_like(refs[-1])

    out = pl.pallas_call(
        body, name=name,
        out_shape=(pltpu.SemaphoreType.DMA((nsem,)), pltpu.SemaphoreType.DMA((nsem,)),
                   *[pltpu.HBM(a.shape, a.dtype) for a in bufs], _sds((8, LANE), F32)),
        in_specs=[_HBM] * nb + dep_specs,
        out_specs=(_SEM, _SEM, *([_HBM] * nb), pl.BlockSpec(memory_space=pltpu.VMEM)),
        input_output_aliases={i: 2 + i for i in range(nb)},
        compiler_params=pltpu.CompilerParams(has_side_effects=_DATAFLOW),
    )(*[pltpu.with_memory_space_constraint(a, pltpu.HBM) for a in bufs], *deps)
    return (plan, n, out[0], out[1], list(out[2:2 + nb])), out[-1]


def _exchange_now(plan, lands, *, name):
    n = len(lands)
    nsem = sum(_PLAN_COPIES[p] for p in _plans(plan, n))

    def body(*refs):
        land_refs, send, recv = refs[n:2 * n], refs[2 * n], refs[2 * n + 1]
        copies = _exchange_copies(plan, land_refs, land_refs, send, recv)
        for cp in copies:
            cp.start()
        for cp in copies:
            cp.wait_send()
            cp.wait_recv()

    hbm = pl.BlockSpec(memory_space=pl.ANY)
    return pl.pallas_call(
        body, in_specs=[hbm] * n, out_specs=[hbm] * n, out_shape=[_sds(a.shape, a.dtype) for a in lands],
        input_output_aliases={i: i for i in range(n)},
        scratch_shapes=[pltpu.SemaphoreType.DMA((nsem,)), pltpu.SemaphoreType.DMA((nsem,))], name=name)(*lands)


def _exchange_wait(state, after, *, name):
    plan, n, send_sem, recv_sem, bufs = state
    nb = len(bufs)
    after = list(after) if isinstance(after, (list, tuple)) else [after]

    def body(*refs):
        ins, land_refs, send, recv = refs[:n], refs[nb - n:nb], refs[nb], refs[nb + 1]
        for cp in _exchange_copies(plan, ins, land_refs, send, recv):
            cp.wait_send()
            cp.wait_recv()

    out = pl.pallas_call(
        body, name=name, out_shape=[pltpu.HBM(a.shape, a.dtype) for a in bufs],
        in_specs=[_HBM] * nb + [_SEM, _SEM] + [pl.BlockSpec(memory_space=pl.ANY)] * len(after), out_specs=[_HBM] * nb,
        input_output_aliases={i: i for i in range(nb)},
        compiler_params=pltpu.CompilerParams(has_side_effects=_DATAFLOW),
    )(*bufs, send_sem, recv_sem, *after)
    return list(out[:n]), list(out[nb - n:])


def _dep_specs(after):
    return ([], []) if after is None else ([pl.BlockSpec(memory_space=pl.ANY)], [after])


def _undo_column_split(g):
    return jnp.transpose(g, (1, 0, 2)).reshape(g.shape[1], N_DEV * g.shape[2])


def _column_split(a):
    r, c = a.shape
    return jnp.transpose(a.reshape(r, N_DEV, c // N_DEV), (1, 0, 2))


class _WholeWeights:
    def __init__(self, groups):
        self.groups = groups
        self.grads = {}

    def fetch(self, group, after):
        return self.groups[group]

    def emit(self, group, grads):
        self.grads.update(grads)
        return None


def _local_step(x, target, replicated, src):
    d = x.shape[1]
    mix_g, ffn_g = replicated["mix_g"], replicated["ffn_g"]
    h0 = jnp.concatenate([jnp.zeros((CHUNK, d), F32), x], axis=0)
    tgt = jnp.concatenate([jnp.zeros((CHUNK, d), F32), target], axis=0)
    cp = src.fetch("cp", [h0, tgt])
    h0 = lax.dynamic_update_slice(h0, cp["meta"], (PAD_ROWS, 0))
    cp_mid = (cp["conv_w"], replicated["conv_b"], replicated["ln_g"], replicated["ln_b"], replicated["pool_w"],
              replicated["pool_scale"])

    h1, cat, z0, u0, cv0 = _cp_mid_fwd(h0, mix_g[0:1], cp["cp_w_in_t"], cp["cp_w_out"], *cp_mid, name="cp_mixer")
    ffn0 = src.fetch("ffn0", h1)
    h2, uf0, rf0 = _ffn_fwd(h1, ffn_g[0:1], ffn0["w1"], ffn0["w2"], name="ffn0")
    gla = src.fetch("gla", h2)
    gla_mid = (gla["gate_w"], gla["gate_b"], gla["head_g"])
    h3, og, states, z1, u1 = _gla_mid_fwd(h2, mix_g[1:2], gla["gla_w_in_t"], gla["gla_w_out"], *gla_mid, name="gla_mixer")
    ffn1 = src.fetch("ffn1", h3)
    dh4, uf1, rf1, loss, d_final_g = _ffn_fwd(h3, ffn_g[1:2], ffn1["w1"], ffn1["w2"],
                                              loss_head=(replicated["final_g"], tgt), name="ffn1_loss")

    dh3, dhh1, dob1, dffn_g1 = _ffn_bwd_x(h3, dh4, ffn_g[1:2], rf1, ffn1["w1"], ffn1["w2"], name="ffn1_bwd_x")
    sent = src.emit("ffn1_w1", dict(w1=_linear_bwd_w(uf1, dhh1, column_blocks=N_DEV, name="ffn1_dw1")))
    sent = src.emit("ffn1_w2", dict(w2=_linear_bwd_w(rf1, dob1, square_x=True, after=sent, name="ffn1_dw2")))
    d_gla_w_out = _linear_bwd_w(og, dh3, name="gla_out_dw")
    dh2, dmix_g1, dz1, d_gate_w, d_gate_b, d_head_g = _gla_mid_bwd(
        z1, h2, mix_g[1:2], gla["gla_w_in_t"], dh3, gla["gla_w_out"], states, *gla_mid, after=sent, name="gla_mixer_bwd")
    d_gla_w_in_t = _linear_bwd_w(dz1, u1, name="gla_in_dw")
    sent = src.emit("gla", dict(gla_w_in_t=d_gla_w_in_t, gla_w_out=d_gla_w_out))
    dh1, dhh0, dob0, dffn_g0 = _ffn_bwd_x(h1, dh2, ffn_g[0:1], rf0, ffn0["w1"], ffn0["w2"], after=sent, name="ffn0_bwd_x")
    sent = src.emit("ffn0_w1", dict(w1=_linear_bwd_w(uf0, dhh0, column_blocks=N_DEV, name="ffn0_dw1")))
    sent = src.emit("ffn0_w2", dict(w2=_linear_bwd_w(rf0, dob0, square_x=True, after=sent, name="ffn0_dw2")))
    d_cp_w_out = _linear_bwd_w(cat, dh1, after=sent, name="cp_out_dw")
    sent = src.emit("cp_out", dict(cp_w_out=d_cp_w_out))
    dh0, dmix_g0, dz0, d_conv_w, d_conv_b, d_ln_g, d_ln_b, d_pool_w, d_pool_scale = _cp_mid_bwd(
        z0, cv0, h0, mix_g[0:1], cp["cp_w_in_t"], dh1, cp["cp_w_out"], *cp_mid, after=sent, name="cp_mixer_bwd")
    d_cp_w_in_t = _linear_bwd_w(dz0, u0, name="cp_in_dw")

    small = dict(
        mix_g=jnp.concatenate([dmix_g0, dmix_g1]), ffn_g=jnp.concatenate([dffn_g0, dffn_g1]), conv_b=d_conv_b, ln_g=d_ln_g,
        ln_b=d_ln_b, pool_w=d_pool_w, pool_scale=d_pool_scale, final_g=d_final_g, meta=dh0[PAD_ROWS:CHUNK], conv_w=d_conv_w,
        gate_w=d_gate_w, gate_b=d_gate_b, head_g=d_head_g)
    src.emit("cp", dict(cp_w_in_t=d_cp_w_in_t, small=small, loss=loss))
    return loss, dh0[CHUNK:], small


_REPLICATED = ("mix_norm_g", "ffn_norm_g", "cp_conv_b", "cp_ln_g", "cp_ln_b", "cp_pool_w", "cp_pool_scale", "final_norm_g")
_SMALL_SHARDED = ("meta_tokens", "cp_conv_w", "gla_gate_w2", "gla_gate_b", "gla_head_g")
_NAMES = ("meta_tokens", "mix_norm_g", "ffn_norm_g", "ffn_w1", "ffn_w2", "cp_w_in", "cp_conv_w", "cp_conv_b", "cp_ln_g",
          "cp_ln_b", "cp_pool_w", "cp_pool_scale", "cp_w_out", "gla_w_in", "gla_gate_w2", "gla_gate_b", "gla_head_g",
          "gla_w_out", "final_norm_g")
_SMALL_GRADS = ("mix_g", "ffn_g", "conv_b", "ln_g", "ln_b", "pool_w", "pool_scale", "final_g", "meta", "conv_w", "gate_w",
                "gate_b", "head_g")
_GROUPS = ("cp", "ffn0", "gla", "ffn1")
_TWO_LEG_GATHERS = ("cp", "ffn0", "ffn1")


class _Exchanges:
    def __init__(self, w, d):
        self.d = d
        small = [w[n].reshape(w[n].shape[-2:]) for n in _SMALL_SHARDED]
        self.small_shard_shapes = [w[n].shape for n in _SMALL_SHARDED]
        shards = dict(
            cp=[(w["cp_w_in"][0].T, BF16), (w["cp_w_out"][0], BF16)] + [(a, F32) for a in small],
            ffn0=[(w["ffn_w1"][0], BF16), (w["ffn_w2"][0], BF16)],
            gla=[(w["gla_w_in"][0].T, BF16), (w["gla_w_out"][0], BF16)],
            ffn1=[(w["ffn_w1"][1], BF16), (w["ffn_w2"][1], BF16)])
        self.gathers = {}
        self.sent = {}
        token = None
        for group in _GROUPS:
            lands = [_place_own(a, dtype, after=token, name=f"place_w_{group}_{k}")
                     for k, (a, dtype) in enumerate(shards[group])]
            plan = _plan_gather_first if group in _TWO_LEG_GATHERS else _plan_gather_direct
            self.gathers[group], token = _exchange_start(plan, None, lands, after=token, name=f"start_w_{group}")
        self.token = token

    def fetch(self, group, after):
        d = self.d
        after = (list(after) if isinstance(after, (list, tuple)) else [after]) + [self.token]
        _, got = _exchange_wait(self.gathers[group], after, name=f"wait_w_{group}")
        if group in _TWO_LEG_GATHERS:
            got = _exchange_now(_plan_gather_relay, got, name=f"relay_w_{group}")
        if group in ("ffn0", "ffn1"):
            return dict(w1=got[0], w2=got[1])
        if group == "gla":
            w_in_t = jnp.pad(got[0].reshape(-1, d), ((0, GATE_PAD - GATE_RANK), (0, 0)))
            return dict(gla_w_in_t=w_in_t, gla_w_out=got[1].reshape(d, d), gate_w=self.gate_w, gate_b=self.gate_b,
                        head_g=self.head_g)
        meta, conv_w, gate_w, self.gate_b, self.head_g = [_undo_column_split(a) for a in got[2:]]
        self.gate_w = jnp.pad(gate_w, ((0, GATE_PAD - GATE_RANK), (0, 0))).astype(BF16)
        return dict(cp_w_in_t=got[0].reshape(-1, d), cp_w_out=got[1].reshape(d, d), meta=meta,
                    conv_w=jnp.pad(conv_w, ((0, 1), (0, 0))))

    def emit(self, group, g):
        d = self.d
        if group in ("ffn0_w1", "ffn1_w1"):
            arrs = [g["w1"]]
        elif group in ("ffn0_w2", "ffn1_w2"):
            arrs = [g["w2"].reshape(N_DEV, -1, d)]
        elif group == "gla":
            w_in_t = g["gla_w_in_t"][:3 * d + GATE_RANK]
            arrs = [w_in_t.reshape(N_DEV, -1, d), g["gla_w_out"].reshape(N_DEV, d // N_DEV, d)]
        elif group == "cp_out":
            arrs = [g["cp_w_out"].reshape(N_DEV, d // N_DEV, d)]
        else:
            s = dict(g["small"])
            s.update(pool_w=s["pool_w"][None], conv_w=s["conv_w"][:CONV_WIDTH], gate_w=s["gate_w"][:GATE_RANK])
            own = [s[n] for n in _SMALL_GRADS[:len(_REPLICATED)]]
            own += [_column_split(s[n]).reshape((N_DEV,) + shape)
                    for n, shape in zip(_SMALL_GRADS[len(_REPLICATED):], self.small_shard_shapes)]
            plans = [_plan_to_all] * len(_REPLICATED) + [_plan_split_to_all] * len(_SMALL_SHARDED)
            lands = [lax.empty((N_DEV,) + a.shape, F32) for a in own[:len(_REPLICATED)]]
            lands += [lax.empty(a.shape, F32) for a in own[len(_REPLICATED):]]
            own.append(g["loss"])
            plans.append(_plan_to_all)
            lands.append(lax.empty((N_DEV,) + g["loss"].shape, F32))
            self.small_sent, self.token = _exchange_start(plans, own, lands, after=self.token, name="start_g_small")
            arrs = [g["cp_w_in_t"].reshape(N_DEV, -1, d)]
        self.sent[group], self.token = _exchange_start(_plan_split_to_all, arrs, [lax.empty(a.shape, a.dtype) for a in arrs],
                                                       after=self.token, name=f"start_g_{group}")
        return self.token

    def finish(self, w, mom, var):
        out = {}
        after = self.token

        def landed(group):
            own, got = _exchange_wait(self.sent[group], after, name=f"wait_g_{group}")
            return list(zip(own, got))

        def adam(n, parts, behind=None, transposed=False):
            flip = (lambda a: jnp.transpose(a, (0, 2, 1))) if transposed else (lambda a: a)
            res = _reduce_adam(parts, flip(w[n]), flip(mom[n]), flip(var[n]), after=behind, name=f"adam_{n}")
            out[n] = tuple(flip(a) for a in res)
            return res[0]

        ffn1_w1 = landed("ffn1_w1")
        after = ffn1_w1[0][1]
        ffn1_w2 = landed("ffn1_w2")
        after = ffn1_w2[0][1]
        gla = landed("gla")
        after = adam("gla_w_in", [gla[0]], transposed=True)
        after = adam("gla_w_out", [gla[1]], after)
        ffn0_w1 = landed("ffn0_w1")
        after = adam("ffn_w1", [ffn0_w1[0], ffn1_w1[0]])
        ffn0_w2 = landed("ffn0_w2")
        after = adam("ffn_w2", [ffn0_w2[0], ffn1_w2[0]])
        small_own, small_landed = _exchange_wait(self.small_sent, after, name="wait_g_small")
        names = _REPLICATED + _SMALL_SHARDED
        split = [False] * len(_REPLICATED) + [True] * len(_SMALL_SHARDED)
        small_new = _adam_small(small_own[:-1], small_landed[:-1], split, [w[n] for n in names], [mom[n] for n in names],
                                [var[n] for n in names], name="adam_small")
        out.update(zip(names, small_new))
        out["loss"] = _sum8(small_own[-1], small_landed[-1], name="sum_loss")[0, 0]

        after = small_new[0][0]
        cp_out = landed("cp_out")
        after = adam("cp_w_out", [cp_out[0]])
        cp = landed("cp")
        adam("cp_w_in", [cp[0]], transposed=True)
        return out


def kernel(x, meta_tokens, mix_norm_g, ffn_norm_g, ffn_w1, ffn_w2, cp_w_in, cp_conv_w, cp_conv_b, cp_ln_g, cp_ln_b, cp_pool_w, cp_pool_scale, cp_w_out, gla_w_in, gla_gate_w2, gla_gate_b, gla_head_g, gla_w_out, final_norm_g, loss_target, m_meta_tokens, m_mix_norm_g, m_ffn_norm_g, m_ffn_w1, m_ffn_w2, m_cp_w_in, m_cp_conv_w, m_cp_conv_b, m_cp_ln_g, m_cp_ln_b, m_cp_pool_w, m_cp_pool_scale, m_cp_w_out, m_gla_w_in, m_gla_gate_w2, m_gla_gate_b, m_gla_head_g, m_gla_w_out, m_final_norm_g, v_meta_tokens, v_mix_norm_g, v_ffn_norm_g, v_ffn_w1, v_ffn_w2, v_cp_w_in, v_cp_conv_w, v_cp_conv_b, v_cp_ln_g, v_cp_ln_b, v_cp_pool_w, v_cp_pool_scale, v_cp_w_out, v_gla_w_in, v_gla_gate_w2, v_gla_gate_b, v_gla_head_g, v_gla_w_out, v_final_norm_g):
    w = dict(meta_tokens=meta_tokens, mix_norm_g=mix_norm_g, ffn_norm_g=ffn_norm_g, ffn_w1=ffn_w1, ffn_w2=ffn_w2,
             cp_w_in=cp_w_in, cp_conv_w=cp_conv_w, cp_conv_b=cp_conv_b, cp_ln_g=cp_ln_g, cp_ln_b=cp_ln_b,
             cp_pool_w=cp_pool_w, cp_pool_scale=cp_pool_scale, cp_w_out=cp_w_out, gla_w_in=gla_w_in,
             gla_gate_w2=gla_gate_w2, gla_gate_b=gla_gate_b, gla_head_g=gla_head_g, gla_w_out=gla_w_out,
             final_norm_g=final_norm_g.reshape(1, -1))
    mom = dict(meta_tokens=m_meta_tokens, mix_norm_g=m_mix_norm_g, ffn_norm_g=m_ffn_norm_g, ffn_w1=m_ffn_w1, ffn_w2=m_ffn_w2,
               cp_w_in=m_cp_w_in, cp_conv_w=m_cp_conv_w, cp_conv_b=m_cp_conv_b, cp_ln_g=m_cp_ln_g, cp_ln_b=m_cp_ln_b,
               cp_pool_w=m_cp_pool_w, cp_pool_scale=m_cp_pool_scale, cp_w_out=m_cp_w_out, gla_w_in=m_gla_w_in,
               gla_gate_w2=m_gla_gate_w2, gla_gate_b=m_gla_gate_b, gla_head_g=m_gla_head_g, gla_w_out=m_gla_w_out,
               final_norm_g=m_final_norm_g.reshape(1, -1))
    var = dict(meta_tokens=v_meta_tokens, mix_norm_g=v_mix_norm_g, ffn_norm_g=v_ffn_norm_g, ffn_w1=v_ffn_w1, ffn_w2=v_ffn_w2,
               cp_w_in=v_cp_w_in, cp_conv_w=v_cp_conv_w, cp_conv_b=v_cp_conv_b, cp_ln_g=v_cp_ln_g, cp_ln_b=v_cp_ln_b,
               cp_pool_w=v_cp_pool_w, cp_pool_scale=v_cp_pool_scale, cp_w_out=v_cp_w_out, gla_w_in=v_gla_w_in,
               gla_gate_w2=v_gla_gate_w2, gla_gate_b=v_gla_gate_b, gla_head_g=v_gla_head_g, gla_w_out=v_gla_w_out,
               final_norm_g=v_final_norm_g.reshape(1, -1))
    d = x.shape[-1]
    replicated = dict(mix_g=w["mix_norm_g"], ffn_g=w["ffn_norm_g"], conv_b=w["cp_conv_b"], ln_g=w["cp_ln_g"],
                      ln_b=w["cp_ln_b"], pool_w=w["cp_pool_w"][0].astype(BF16), pool_scale=w["cp_pool_scale"],
                      final_g=w["final_norm_g"])
    exchanges = _Exchanges(w, d)
    _, grad_x, _ = _local_step(x[0], loss_target[0], replicated, exchanges)
    out = exchanges.finish(w, mom, var)
    loss = out.pop("loss")

    def leaf(n, k):
        a = out[n][k]
        return a.reshape(-1) if n == "final_norm_g" else a

    return (loss, grad_x[None], *[leaf(n, 0) for n in _NAMES], *[leaf(n, 1) for n in _NAMES],
            *[leaf(n, 2) for n in _NAMES], *[leaf(n, 3) for n in _NAMES])
```

```python
import functools

import jax
import jax.numpy as jnp
from jax import lax
from jax.experimental import pallas as pl
from jax.experimental.pallas import tpu as pltpu

F32, BF16 = jnp.float32, jnp.bfloat16
N_DEV = 8
CHUNK = 64
N_META = 16
PAD_ROWS = CHUNK - N_META
HALO = 32
EPS = 1e-5
CONV_WIDTH = 31
POOL_WINDOWS = (2, 4, 8, 16)
HEADS = 4
GATE_RANK = 16
GATE_NORM = 16.0
GATE_PAD = 128
ADAM_LR, ADAM_B1, ADAM_B2, ADAM_EPS, ADAM_WD, ADAM_STEP = 0.001, 0.9, 0.999, 1e-08, 0.01, 10
V7X_VMEM_LIMIT = 56 * 2 ** 20
LANE = 128


def _cparams(*sem):
    return pltpu.CompilerParams(dimension_semantics=sem, vmem_limit_bytes=V7X_VMEM_LIMIT)


def _row_tile(t, cap):
    best = CHUNK
    for r in range(CHUNK, min(t, cap) + 1, CHUNK):
        if t % r == 0:
            best = r
    return best


def _resident(shape):
    return pl.BlockSpec(shape, lambda *_: (0,) * len(shape), pipeline_mode=pl.Buffered(1))


def _dot(a, b):
    return jnp.dot(a, b, preferred_element_type=F32)


def _dot_nt(a, b):
    return lax.dot_general(a, b, (((1,), (1,)), ((), ())), preferred_element_type=F32)


def _dot_tn(a, b):
    return lax.dot_general(a, b, (((0,), (0,)), ((), ())), preferred_element_type=F32)


def _rowsum(a):
    return jnp.sum(a, axis=0, keepdims=True)


def _sigmoid(a):
    return 1.0 / (1.0 + jnp.exp(-a))


def _row_ids(tile, rt):
    return tile * rt + lax.broadcasted_iota(jnp.int32, (rt, 1), 0)


def _sds(shape, dtype):
    return jax.ShapeDtypeStruct(shape, dtype)


DW_ROWS = 1024


def _linear_bwd_w(x, dy, *, square_x=False, column_blocks=None, after=None, name):
    t, k = x.shape
    n = dy.shape[1]
    cut_k = k > n and column_blocks is None
    width = k if cut_k else n
    blk = n // column_blocks if column_blocks else max(c for c in (640, 512, 384, 256, LANE) if width % c == 0)
    dep_specs, deps = _dep_specs(after)

    def body(*refs):
        x_ref, dy_ref, o_ref, acc = refs[len(deps):]
        for c0 in range(0, t, DW_ROWS):
            rows = slice(c0, min(c0 + DW_ROWS, t))
            xv = x_ref[rows, :]
            if square_x:
                xv = xv.astype(F32)
                xv = xv * xv
            part = _dot_tn(xv.astype(BF16), dy_ref[rows, :].astype(BF16))
            if c0 == 0:
                acc[...] = part
            else:
                acc[...] += part
        o_ref[...] = acc[...].astype(BF16)

    out_shape = _sds((k, n), BF16)
    if cut_k:
        in_specs = [pl.BlockSpec((t, blk), lambda j: (0, j)), _resident((t, n))]
        out_specs = pl.BlockSpec((blk, n), lambda j: (j, 0))
        acc_shape = (blk, n)
    else:
        in_specs = [_resident((t, k)), pl.BlockSpec((t, blk), lambda j: (0, j))]
        out_specs = pl.BlockSpec((k, blk), lambda j: (0, j))
        acc_shape = (k, blk)
        if column_blocks:
            out_specs = pl.BlockSpec((None, k, blk), lambda j: (j, 0, 0))
            out_shape = _sds((column_blocks, k, blk), BF16)
    return pl.pallas_call(
        body, grid=(width // blk,), in_specs=dep_specs + in_specs, out_specs=out_specs, out_shape=out_shape,
        scratch_shapes=[pltpu.VMEM(acc_shape, F32)], compiler_params=_cparams("parallel"), name=name)(*deps, x, dy)


FFN_BLOCKS_PER_STEP = 2


def _ffn_fwd(h, gain, w1g, w2g, *, loss_head=None, name):
    t, d = h.shape
    f8 = w1g.shape[-1]
    rt = _row_tile(t, 832)
    nb = FFN_BLOCKS_PER_STEP
    nstep = N_DEV // nb

    def body(*refs):
        if loss_head is None:
            h_ref, g_ref, w1_ref, w2_ref, o_ref, u_ref, r_ref, acc_ref = refs
        else:
            (h_ref, g_ref, w1_ref, w2_ref, fg_ref, tgt_ref, o_ref, u_ref, r_ref, loss_ref, dfg_ref, acc_ref, t_ref,
             t_sem) = refs
        i, j = pl.program_id(0), pl.program_id(1)

        def target_rows(act):
            @pl.when(i == 0)
            def _():
                act(pltpu.make_async_copy(tgt_ref.at[pl.ds(0, rt - CHUNK)], t_ref.at[pl.ds(CHUNK, rt - CHUNK)], t_sem.at[0]))

            if t > rt:
                @pl.when(i > 0)
                def _():
                    act(pltpu.make_async_copy(tgt_ref.at[pl.ds(pl.multiple_of(i * rt - CHUNK, CHUNK), rt)], t_ref,
                                              t_sem.at[0]))

        @pl.when(j == 0)
        def _():
            if loss_head is not None:
                @pl.when(i == 0)
                def _():
                    t_ref[0:CHUNK, :] = jnp.zeros((CHUNK, d), F32)

                target_rows(lambda copy: copy.start())

            hv = h_ref[...]
            u_ref[...] = (hv * lax.rsqrt(jnp.mean(hv * hv, axis=-1, keepdims=True) + EPS) * g_ref[...]).astype(BF16)
            acc_ref[...] = jnp.zeros_like(acc_ref)

        part = None
        for b in range(nb):
            a = jnp.maximum(_dot(u_ref[...], w1_ref[b]), 0.0)
            r_ref[:, b * f8:(b + 1) * f8] = a.astype(BF16)
            term = _dot((a * a).astype(BF16), w2_ref[b])
            part = term if part is None else part + term
        acc_ref[...] += part

        @pl.when(j == nstep - 1)
        def _():
            y = h_ref[...] + acc_ref[...]
            if loss_head is None:
                o_ref[...] = y
                return

            @pl.when(i == 0)
            def _():
                loss_ref[...] = jnp.zeros_like(loss_ref)
                dfg_ref[...] = jnp.zeros_like(dfg_ref)

            target_rows(lambda copy: copy.wait())

            rstd = lax.rsqrt(jnp.mean(y * y, axis=-1, keepdims=True) + EPS)
            xh = y * rstd
            err = jnp.where(_row_ids(i, rt) >= CHUNK, xh * fg_ref[...] - t_ref[...], 0.0)
            loss_ref[...] += (0.5 / d) * jnp.sum(err * err)
            dy = err * (1.0 / d)
            dfg_ref[...] += _rowsum(dy * xh)
            dxh = dy * fg_ref[...]
            o_ref[...] = rstd * (dxh - xh * jnp.mean(dxh * xh, axis=-1, keepdims=True))

    rows = lambda i, j: (i, 0)
    in_specs = [pl.BlockSpec((rt, d), rows), _resident((1, d)),
                pl.BlockSpec((nb, d, f8), lambda i, j: (j, 0, 0)), pl.BlockSpec((nb, f8, d), lambda i, j: (j, 0, 0))]
    out_specs = [pl.BlockSpec((rt, d), rows), pl.BlockSpec((rt, d), rows), pl.BlockSpec((rt, nb * f8), lambda i, j: (i, j))]
    out_shape = [_sds((t, d), F32), _sds((t, d), BF16), _sds((t, N_DEV * f8), BF16)]
    args = [h, gain, w1g, w2g]
    scratch_shapes = [pltpu.VMEM((rt, d), F32)]
    if loss_head is not None:
        in_specs += [_resident((1, d)), pl.BlockSpec(memory_space=pl.ANY)]
        out_specs += [pl.BlockSpec((8, LANE), lambda i, j: (0, 0)), pl.BlockSpec((1, d), lambda i, j: (0, 0))]
        out_shape += [_sds((8, LANE), F32), _sds((1, d), F32)]
        args += list(loss_head)
        scratch_shapes += [pltpu.VMEM((rt, d), F32), pltpu.SemaphoreType.DMA((1,))]
    return pl.pallas_call(
        body, grid=(t // rt, nstep), in_specs=in_specs, out_specs=out_specs, out_shape=out_shape,
        scratch_shapes=scratch_shapes,
        compiler_params=_cparams("arbitrary" if loss_head is not None else "parallel", "arbitrary"), name=name)(*args)


def _ffn_bwd_x(h, dout, gain, r, w1g, w2g, *, after=None, name):
    t, d = h.shape
    f8 = w1g.shape[-1]
    rt = _row_tile(t, 832)
    nb = FFN_BLOCKS_PER_STEP
    last = N_DEV // nb - 1
    dep_specs, deps = _dep_specs(after)

    def body(*refs):
        h_ref, do_ref, g_ref, r_ref, w1_ref, w2_ref, dh_ref, dhh_ref, dob_ref, dg_ref, du_ref = refs[len(deps):]
        i, j = pl.program_id(0), pl.program_id(1)

        @pl.when(j == 0)
        def _():
            dob_ref[...] = do_ref[...].astype(BF16)
            du_ref[...] = jnp.zeros_like(du_ref)

        part = None
        for b in range(nb):
            cols = slice(b * f8, (b + 1) * f8)
            dhh = (_dot_nt(dob_ref[...], w2_ref[b]) * (2.0 * r_ref[:, cols].astype(F32))).astype(BF16)
            dhh_ref[:, cols] = dhh
            term = _dot_nt(dhh, w1_ref[b])
            part = term if part is None else part + term
        du_ref[...] += part

        @pl.when(j == last)
        def _():
            @pl.when(i == 0)
            def _():
                dg_ref[...] = jnp.zeros_like(dg_ref)

            hv = h_ref[...]
            rstd = lax.rsqrt(jnp.mean(hv * hv, axis=-1, keepdims=True) + EPS)
            xh = hv * rstd
            du = du_ref[...]
            dg_ref[...] += _rowsum(du * xh)
            dxh = du * g_ref[...]
            dh_ref[...] = do_ref[...] + rstd * (dxh - xh * jnp.mean(dxh * xh, axis=-1, keepdims=True))

    rows = lambda i, j: (i, 0)
    return pl.pallas_call(
        body, grid=(t // rt, N_DEV // nb),
        in_specs=dep_specs + [
                  pl.BlockSpec((rt, d), rows), pl.BlockSpec((rt, d), rows), _resident((1, d)),
                  pl.BlockSpec((rt, nb * f8), lambda i, j: (i, j)),
                  pl.BlockSpec((nb, d, f8), lambda i, j: (j, 0, 0)),
                  pl.BlockSpec((nb, f8, d), lambda i, j: (j, 0, 0))],
        out_specs=[pl.BlockSpec((rt, d), rows), pl.BlockSpec((rt, nb * f8), lambda i, j: (i, j)),
                   pl.BlockSpec((rt, d), rows), pl.BlockSpec((1, d), lambda i, j: (0, 0))],
        out_shape=[_sds((t, d), F32), _sds((t, N_DEV * f8), BF16), _sds((t, d), BF16), _sds((1, d), F32)],
        scratch_shapes=[pltpu.VMEM((rt, d), F32)],
        compiler_params=_cparams("arbitrary", "arbitrary"), name=name)(*deps, h, dout, gain, r, w1g, w2g)


def _lane_blocks(width):
    lb = min(LANE, width)
    return [slice(s, s + lb) for s in range(0, width, lb)]


def _conv_rows(src_ref, w_ref, offset, dst_ref, nblk, width, bias_ref=None):
    def blk(rb, carry):
        base = pl.multiple_of(rb * CHUNK, CHUNK)
        for l, ls in enumerate(_lane_blocks(width)):
            acc = jnp.zeros((CHUNK, ls.stop - ls.start), F32)
            if bias_ref is not None:
                acc = acc + bias_ref[:, ls]
            for k in range(CONV_WIDTH):
                acc = acc + w_ref[k:k + 1, ls] * src_ref[l, pl.ds(base + offset(k), CHUNK), :]
            dst_ref[l, pl.ds(base, CHUNK), :] = acc
        return carry

    lax.fori_loop(0, nblk, blk, 0)


def _to_lane_blocks(ref, row0, value):
    for l, ls in enumerate(_lane_blocks(value.shape[1])):
        ref[l, row0:row0 + value.shape[0], :] = value[:, ls]


def _from_lane_blocks(ref):
    return jnp.concatenate([ref[l] for l in range(ref.shape[0])], axis=1)


def _pool_counts(rows, window):
    return jnp.clip(rows - PAD_ROWS + 1, 1, window).astype(F32)


def _trailing_sum(v, window):
    s, sh = v, 1
    while sh < window:
        s = s + pltpu.roll(s, sh, 0)
        sh *= 2
    return s


def _leading_sum(v, window):
    s, sh, n = v, 1, v.shape[0]
    while sh < window:
        s = s + pltpu.roll(s, n - sh, 0)
        sh *= 2
    return s


def _norm_project(h_ref, g_ref, w_t_ref, u_ref, z_ref):
    hv = h_ref[...]
    u = (hv * lax.rsqrt(jnp.mean(hv * hv, axis=-1, keepdims=True) + EPS) * g_ref[...]).astype(BF16)
    u_ref[...] = u
    z_ref[...] = _dot_nt(u, w_t_ref[...])


def _project_back(dz_ref, w_t_ref, h_ref, g_ref, dres_ref, dh_ref, dg_ref, first):
    dx = _dot(dz_ref[...], w_t_ref[...])
    hv = h_ref[...]
    rstd = lax.rsqrt(jnp.mean(hv * hv, axis=-1, keepdims=True) + EPS)
    xh = hv * rstd

    @pl.when(first)
    def _():
        dg_ref[...] = jnp.zeros_like(dg_ref)

    dg_ref[...] += _rowsum(dx * xh)
    dxh = dx * g_ref[...]
    dh_ref[...] = dres_ref[...] + rstd * (dxh - xh * jnp.mean(dxh * xh, axis=-1, keepdims=True))


def _cp_mid_fwd(x, meta, gain, w_in_t, w_out, conv_w, conv_b, ln_g, ln_b, pool_w, pool_scale, *, name):
    seq, d = x.shape
    t = seq + CHUNK
    ein = w_in_t.shape[0]
    cd = conv_b.shape[1]
    pd = pool_scale.shape[1]
    pg = pd // len(POOL_WINDOWS)
    rt = _row_tile(t, 320)
    ntile = t // rt

    def body(x_ref, meta_ref, g_ref, wi_ref, wo_ref, cw_ref, cb_ref, lg_ref, lb_ref, pw_ref, ps_ref,
             ho_ref, o_ref, z_ref, u_ref, cv_ref, h0_ref, gext, pext, conv_s, hbuf, hsem):
        i = pl.program_id(0)
        slot = i % 2
        first_rows = pltpu.make_async_copy(x_ref.at[pl.ds(0, rt - CHUNK)], hbuf.at[0, pl.ds(CHUNK, rt - CHUNK)], hsem.at[0])

        def tile_rows(tile, to):
            return pltpu.make_async_copy(x_ref.at[pl.ds(pl.multiple_of(tile * rt - CHUNK, CHUNK), rt)], hbuf.at[to],
                                         hsem.at[to])

        @pl.when(i == 0)
        def _():
            first_rows.start()
            hbuf[0, 0:PAD_ROWS, :] = jnp.zeros((PAD_ROWS, d), F32)
            hbuf[0, PAD_ROWS:CHUNK, :] = meta_ref[...]
            _to_lane_blocks(gext, 0, jnp.zeros((HALO, cd), F32))
            pext[0:HALO, :] = jnp.zeros((HALO, pd), F32)

        @pl.when(i + 1 < ntile)
        def _():
            tile_rows(i + 1, 1 - slot).start()

        @pl.when(i == 0)
        def _():
            first_rows.wait()

        @pl.when(i > 0)
        def _():
            tile_rows(i, slot).wait()

        h_ref = hbuf.at[slot]
        h0_ref[...] = h_ref[...]
        _norm_project(h_ref, g_ref, wi_ref, u_ref, z_ref)

        _to_lane_blocks(gext, HALO, z_ref[:, 0:cd] * _sigmoid(z_ref[:, cd:2 * cd]))
        pext[HALO:HALO + rt, :] = z_ref[:, 2 * cd:]
        _conv_rows(gext, cw_ref, lambda k: k + HALO - (CONV_WIDTH - 1), conv_s, rt // CHUNK, cd, cb_ref)
        cv = _from_lane_blocks(conv_s)
        cv_ref[...] = cv
        xc = cv - jnp.mean(cv, axis=-1, keepdims=True)
        y = xc * lax.rsqrt(jnp.mean(xc * xc, axis=-1, keepdims=True) + EPS) * lg_ref[...] + lb_ref[...]
        rows = _row_ids(i, rt)
        a = jnp.where(rows >= PAD_ROWS, y * _sigmoid(y), 0.0)
        o_ref[:, 0:cd] = a.astype(BF16)
        for gi, window in enumerate(POOL_WINDOWS):
            ls = slice(gi * pg, (gi + 1) * pg)
            v = pext[:, ls]
            tm = _trailing_sum(v, window)[HALO:] / _pool_counts(rows, window) - v[HALO:]
            p = _dot(tm.astype(BF16), pw_ref[gi]) * ps_ref[:, ls]
            o_ref[:, cd + gi * pg:cd + (gi + 1) * pg] = p.astype(BF16)
        ho_ref[...] = h_ref[...] + _dot(o_ref[...], wo_ref[...])
        gext[:, 0:HALO, :] = gext[:, rt:rt + HALO, :]
        pext[0:HALO, :] = pext[rt:rt + HALO, :]

    nl, lb = len(_lane_blocks(cd)), min(LANE, cd)
    rows = lambda i: (i, 0)
    return pl.pallas_call(
        body, grid=(ntile,),
        in_specs=[pl.BlockSpec(memory_space=pl.ANY), _resident(meta.shape), _resident((1, d)), _resident(w_in_t.shape),
                  _resident(w_out.shape), _resident(conv_w.shape), _resident((1, cd)),
                  _resident((1, cd)), _resident((1, cd)), _resident(pool_w.shape), _resident((1, pd))],
        out_specs=[pl.BlockSpec((rt, d), rows), pl.BlockSpec((rt, cd + pd), rows), pl.BlockSpec((rt, ein), rows),
                   pl.BlockSpec((rt, d), rows), pl.BlockSpec((rt, cd), rows), pl.BlockSpec((rt, d), rows)],
        out_shape=[_sds((t, d), F32), _sds((t, cd + pd), BF16), _sds((t, ein), F32), _sds((t, d), BF16),
                   _sds((t, cd), F32), _sds((t, d), F32)],
        scratch_shapes=[pltpu.VMEM((nl, rt + HALO, lb), F32), pltpu.VMEM((rt + HALO, pd), F32),
                        pltpu.VMEM((nl, rt, lb), F32), pltpu.VMEM((2, rt, d), F32), pltpu.SemaphoreType.DMA((2,))],
        compiler_params=_cparams("arbitrary"), name=name)(x, meta, gain, w_in_t, w_out, conv_w, conv_b, ln_g, ln_b, pool_w,
                                                          pool_scale)


def _cp_mid_bwd(z, cv, h, gain, w_in_t, dh, w_out, conv_w, conv_b, ln_g, ln_b, pool_w, pool_scale, *, after=None, name):
    t, ein = z.shape
    cd = conv_b.shape[1]
    pd = pool_scale.shape[1]
    pg = pd // len(POOL_WINDOWS)
    rt = _row_tile(t, 320)
    ntile = t // rt
    per = rt // CHUNK
    dep_specs, deps = _dep_specs(after)

    def body(*refs):
        (z_ref, zh_ref, cv_ref, h_ref, g_ref, wi_ref, dh_ref, wo_ref, cw_ref, cb_ref, lg_ref, lb_ref, pw_ref, ps_ref,
         dx_ref, dfirst_ref, dg_ref, dz_ref, dcw_ref, dcb_ref, dlg_ref, dlb_ref, dpw_ref, dps_ref,
         gext, pext, conv_s, dcv, dsp, dhi, dx_sem) = refs[len(deps):]
        step = pl.program_id(0)
        tile = ntile - 1 - step
        first_rows = pltpu.make_async_copy(dhi.at[pl.ds(CHUNK, rt - CHUNK)], dx_ref.at[pl.ds(0, rt - CHUNK)], dx_sem.at[0])

        def tile_rows(tile):
            return pltpu.make_async_copy(dhi, dx_ref.at[pl.ds(pl.multiple_of(tile * rt - CHUNK, CHUNK), rt)], dx_sem.at[0])

        dcat = _dot_nt(dh_ref[...].astype(BF16), wo_ref[...])

        @pl.when(step == 0)
        def _():
            for ref in (dcw_ref, dcb_ref, dlg_ref, dlb_ref, dpw_ref, dps_ref):
                ref[...] = jnp.zeros_like(ref)
            _to_lane_blocks(dcv, rt, jnp.zeros((HALO, cd), F32))
            dsp[rt:rt + HALO, :] = jnp.zeros((HALO, pd), F32)

        keep = jnp.where(tile > 0, 1.0, 0.0)
        zh = zh_ref[CHUNK - HALO:CHUNK, :]
        _to_lane_blocks(gext, 0, keep * zh[:, 0:cd] * _sigmoid(zh[:, cd:2 * cd]))
        pext[0:HALO, :] = keep * zh[:, 2 * cd:]
        za = z_ref[:, 0:cd]
        sg = _sigmoid(z_ref[:, cd:2 * cd])
        _to_lane_blocks(gext, HALO, za * sg)
        pext[HALO:HALO + rt, :] = z_ref[:, 2 * cd:]
        cv = cv_ref[...]
        xc = cv - jnp.mean(cv, axis=-1, keepdims=True)
        rstd = lax.rsqrt(jnp.mean(xc * xc, axis=-1, keepdims=True) + EPS)
        xh = xc * rstd
        y = xh * lg_ref[...] + lb_ref[...]
        sy = _sigmoid(y)
        rows = _row_ids(tile, rt)
        da = jnp.where(rows >= PAD_ROWS, dcat[:, 0:cd], 0.0)
        dy = da * (sy * (1.0 + y * (1.0 - sy)))
        dlg_ref[...] += _rowsum(dy * xh)
        dlb_ref[...] += _rowsum(dy)
        dxh = dy * lg_ref[...]
        dconv = rstd * (dxh - jnp.mean(dxh, axis=-1, keepdims=True) - xh * jnp.mean(dxh * xh, axis=-1, keepdims=True))
        dcb_ref[...] += _rowsum(dconv)
        _to_lane_blocks(dcv, 0, dconv)
        for l, ls in enumerate(_lane_blocks(cd)):
            def acc_rows(rb, accs, l=l):
                base = pl.multiple_of(rb * CHUNK, CHUNK)
                d_blk = dcv[l, pl.ds(base, CHUNK), :]
                out = []
                for k in range(CONV_WIDTH):
                    prod = d_blk * gext[l, pl.ds(base + k + HALO - (CONV_WIDTH - 1), CHUNK), :]
                    part = prod[0:8]
                    for s in range(8, CHUNK, 8):
                        part = part + prod[s:s + 8]
                    out.append(accs[k] + part)
                return tuple(out)

            zero = jnp.zeros((8, ls.stop - ls.start), F32)
            accs = lax.fori_loop(0, per, acc_rows, (zero,) * CONV_WIDTH)
            for k in range(CONV_WIDTH):
                dcw_ref[k:k + 1, ls] += _rowsum(accs[k])
        _conv_rows(dcv, cw_ref, lambda k: CONV_WIDTH - 1 - k, conv_s, per, cd)
        dglu = _from_lane_blocks(conv_s)
        dz_ref[:, 0:cd] = (dglu * sg).astype(BF16)
        dz_ref[:, cd:2 * cd] = (dglu * za * sg * (1.0 - sg)).astype(BF16)
        dcv[:, rt:rt + HALO, :] = dcv[:, 0:HALO, :]
        for gi, window in enumerate(POOL_WINDOWS):
            ls = slice(gi * pg, (gi + 1) * pg)
            v = pext[:, ls]
            cnt = _pool_counts(rows, window)
            tm = (_trailing_sum(v, window)[HALO:] / cnt - v[HALO:]).astype(BF16)
            dp = dcat[:, cd + gi * pg:cd + (gi + 1) * pg]
            dps_ref[:, ls] += _rowsum(dp * _dot(tm, pw_ref[gi]))
            dpl = (dp * ps_ref[:, ls]).astype(BF16)
            dpw_ref[gi] += _dot_tn(tm, dpl)
            dtm = _dot_nt(dpl, pw_ref[gi])
            dsp[0:rt, ls] = dtm / cnt
            dpin = _leading_sum(dsp[:, ls], window)[0:rt] - dtm
            dz_ref[:, 2 * cd + gi * pg:2 * cd + (gi + 1) * pg] = dpin.astype(BF16)
        dsp[rt:rt + HALO, :] = dsp[0:HALO, :]

        @pl.when(step > 0)
        def _():
            tile_rows(tile + 1).wait()

        _project_back(dz_ref, wi_ref, h_ref, g_ref, dh_ref, dhi, dg_ref, step == 0)

        @pl.when(tile > 0)
        def _():
            tile_rows(tile).start()

        @pl.when(tile == 0)
        def _():
            first_rows.start()
            dfirst_ref[...] = dhi[0:CHUNK, :]
            first_rows.wait()

    d = h.shape[1]
    back = lambda i: (ntile - 1 - i, 0)
    halo_idx = lambda i: (jnp.maximum((ntile - 1 - i) * per - 1, 0), 0)
    const2 = lambda i: (0, 0)
    nl, lb = len(_lane_blocks(cd)), min(LANE, cd)
    return pl.pallas_call(
        body, grid=(ntile,),
        in_specs=dep_specs + [
                  pl.BlockSpec((rt, ein), back), pl.BlockSpec((CHUNK, ein), halo_idx), pl.BlockSpec((rt, cd), back),
                  pl.BlockSpec((rt, d), back),
                  _resident((1, d)), _resident(w_in_t.shape), pl.BlockSpec((rt, d), back), _resident(w_out.shape),
                  _resident(conv_w.shape), _resident((1, cd)), _resident((1, cd)), _resident((1, cd)),
                  _resident(pool_w.shape), _resident((1, pd))],
        out_specs=[pl.BlockSpec(memory_space=pl.ANY), pl.BlockSpec((CHUNK, d), const2), pl.BlockSpec((1, d), const2),
                   pl.BlockSpec((rt, ein), back), pl.BlockSpec(conv_w.shape, const2), pl.BlockSpec((1, cd), const2),
                   pl.BlockSpec((1, cd), const2), pl.BlockSpec((1, cd), const2),
                   pl.BlockSpec(pool_w.shape, lambda i: (0, 0, 0)), pl.BlockSpec((1, pd), const2)],
        out_shape=[_sds((t - CHUNK, d), F32), _sds((CHUNK, d), F32), _sds((1, d), F32),
                   _sds((t, ein), BF16), _sds(conv_w.shape, F32), _sds((1, cd), F32), _sds((1, cd), F32),
                   _sds((1, cd), F32), _sds(pool_w.shape, F32), _sds((1, pd), F32)],
        scratch_shapes=[pltpu.VMEM((nl, rt + HALO, lb), F32), pltpu.VMEM((rt + HALO, pd), F32), pltpu.VMEM((nl, rt, lb), F32),
                        pltpu.VMEM((nl, rt + HALO, lb), F32), pltpu.VMEM((rt + HALO, pd), F32), pltpu.VMEM((rt, d), F32),
                        pltpu.SemaphoreType.DMA((1,))],
        compiler_params=_cparams("arbitrary"), name=name)(*deps, z, z, cv, h, gain, w_in_t, dh, w_out, conv_w, conv_b, ln_g,
                                                          ln_b, pool_w, pool_scale)


def _log_decay(r, gw_ref, gb_ref, rows):
    gp = _dot(r.astype(BF16), gw_ref[...]) + gb_ref[...]
    log_sig = jnp.minimum(gp, 0.0) - jnp.log(1.0 + jnp.exp(-jnp.abs(gp)))
    return gp, jnp.where(rows >= PAD_ROWS, log_sig / GATE_NORM, 0.0)


def _tri(strict):
    r = lax.broadcasted_iota(jnp.int32, (CHUNK, CHUNK), 0)
    c = lax.broadcasted_iota(jnp.int32, (CHUNK, CHUNK), 1)
    return jnp.where(c < r if strict else c <= r, 1.0, 0.0).astype(BF16)


def _tri_dot(tri, a):
    hi = a.astype(BF16)
    rest = a - hi.astype(F32)
    mid = rest.astype(BF16)
    lo = (rest - mid.astype(F32)).astype(BF16)
    return _dot(tri, hi) + _dot(tri, mid) + _dot(tri, lo)


def _gla_mid_fwd(h, gain, w_in_t, w_out, gate_w, gate_b, head_g, *, name):
    t = h.shape[0]
    zw = w_in_t.shape[0]
    dk = gate_b.shape[1]
    hv = head_g.shape[1]
    hk = dk // HEADS
    dv = hv * HEADS
    r_at = 2 * dk + 2 * dv
    rt = _row_tile(t, 320)
    per = rt // CHUNK
    scale = hk ** -0.5

    def body(h_ref, g_ref, wi_ref, wo_ref, gw_ref, gb_ref, hg_ref, ho_ref, o_ref, st_ref, z_ref, u_ref,
             s_ref, la_ref, dec_ref):
        i = pl.program_id(0)

        @pl.when(i == 0)
        def _():
            s_ref[...] = jnp.zeros_like(s_ref)

        _norm_project(h_ref, g_ref, wi_ref, u_ref, z_ref)

        _, la = _log_decay(z_ref[:, r_at:r_at + GATE_PAD], gw_ref, gb_ref, _row_ids(i, rt))
        la_ref[...] = la
        tri = _tri(False)

        def chunk_rows(c):
            return slice(c * CHUNK, (c + 1) * CHUNK)

        def decays(c, carry):
            rows = chunk_rows(c)
            la_c = la_ref[rows, :]
            cum = _tri_dot(tri, la_c)
            dec_ref[rows, :] = jnp.exp(_rowsum(la_c) - cum)
            return carry

        def states(c, carry):
            rows = chunk_rows(c)
            etot = jnp.exp(_rowsum(la_ref[rows, :]))
            for hd in range(HEADS):
                ks = slice(hd * hk, (hd + 1) * hk)
                kd = z_ref[rows, dk + hd * hk:dk + (hd + 1) * hk] * dec_ref[rows, ks]
                v = z_ref[rows, 2 * dk + hd * hv:2 * dk + (hd + 1) * hv]
                s_new = s_ref[hd] * etot[:, ks] + _dot_tn(v.astype(BF16), kd.astype(BF16))
                s_ref[hd] = s_new
                st_ref[c, hd] = s_new
            return carry

        def outputs(c, carry):
            rows = chunk_rows(c)
            for hd in range(HEADS):
                q = z_ref[rows, hd * hk:(hd + 1) * hk] * scale
                g = z_ref[rows, 2 * dk + dv + hd * hv:2 * dk + dv + (hd + 1) * hv]
                o = _dot_nt(q.astype(BF16), st_ref[c, hd].astype(BF16))
                on = o * lax.rsqrt(jnp.mean(o * o, axis=-1, keepdims=True) + EPS) * hg_ref[...]
                o_ref[rows, hd * hv:(hd + 1) * hv] = (on * (g * _sigmoid(g))).astype(BF16)
            return carry

        for phase in (decays, states, outputs):
            for c in range(per):
                phase(c, 0)
        ho_ref[...] = h_ref[...] + _dot(o_ref[...], wo_ref[...])

    d = h.shape[1]
    rows = lambda i: (i, 0)
    return pl.pallas_call(
        body, grid=(t // rt,),
        in_specs=[pl.BlockSpec((rt, d), rows), _resident((1, d)), _resident(w_in_t.shape), _resident(w_out.shape),
                  _resident(gate_w.shape), _resident((1, dk)), _resident((1, hv))],
        out_specs=[pl.BlockSpec((rt, d), rows), pl.BlockSpec((rt, dv), rows),
                   pl.BlockSpec((per, HEADS, hv, hk), lambda i: (i, 0, 0, 0)), pl.BlockSpec((rt, zw), rows),
                   pl.BlockSpec((rt, d), rows)],
        out_shape=[_sds((t, d), F32), _sds((t, dv), BF16), _sds((t // CHUNK, HEADS, hv, hk), F32), _sds((t, zw), F32),
                   _sds((t, d), BF16)],
        scratch_shapes=[pltpu.VMEM((HEADS, hv, hk), F32), pltpu.VMEM((rt, dk), F32), pltpu.VMEM((rt, dk), F32)],
        compiler_params=_cparams("arbitrary"), name=name)(h, gain, w_in_t, w_out, gate_w, gate_b, head_g)


def _gla_mid_bwd(z, h, gain, w_in_t, dh, w_out, states, gate_w, gate_b, head_g, *, after=None, name):
    t = z.shape[0]
    dk = gate_b.shape[1]
    hv = head_g.shape[1]
    hk = dk // HEADS
    dv = hv * HEADS
    r_at = 2 * dk + 2 * dv
    rt = _row_tile(t, 320)
    ntile = t // rt
    per = rt // CHUNK
    scale = hk ** -0.5
    dep_specs, deps = _dep_specs(after)

    def body(*refs):
        (z_ref, h_ref, g_ref, wi_ref, dh_ref, wo_ref, st_ref, stp_ref, gw_ref, gb_ref, hg_ref,
         dhi_ref, dg_ref, dz_ref, dgw_ref, dgb_ref, dhg_ref,
         ds_ref, la_ref, dla_ref, dec_ref, dos_ref, e_ref, do_ref) = refs[len(deps):]
        step = pl.program_id(0)
        tile = ntile - 1 - step
        do_ref[...] = _dot_nt(dh_ref[...].astype(BF16), wo_ref[...])

        @pl.when(step == 0)
        def _():
            ds_ref[...] = jnp.zeros_like(ds_ref)
            dgw_ref[...] = jnp.zeros_like(dgw_ref)
            dgb_ref[...] = jnp.zeros_like(dgb_ref)
            dhg_ref[...] = jnp.zeros_like(dhg_ref)

        rows_id = _row_ids(tile, rt)
        r = z_ref[:, r_at:r_at + GATE_PAD]
        gp, la = _log_decay(r, gw_ref, gb_ref, rows_id)
        la_ref[...] = la
        tri, tri_strict = _tri(False), _tri(True)
        keep = jnp.where(tile > 0, 1.0, 0.0)

        def chunk_rows(c):
            return slice(c * CHUNK, (c + 1) * CHUNK)

        def recompute(c, dhg):
            rows = chunk_rows(c)
            la_c = la_ref[rows, :]
            cum = _tri_dot(tri, la_c)
            dec_ref[rows, :] = jnp.exp(_rowsum(la_c) - cum)
            for hd in range(HEADS):
                q = (z_ref[rows, hd * hk:(hd + 1) * hk] * scale).astype(BF16)
                g = z_ref[rows, 2 * dk + dv + hd * hv:2 * dk + dv + (hd + 1) * hv]
                s_b = st_ref[c, hd].astype(BF16)
                o = _dot_nt(q, s_b)
                rstd = lax.rsqrt(jnp.mean(o * o, axis=-1, keepdims=True) + EPS)
                oh = o * rstd
                sg = _sigmoid(g)
                d_og = do_ref[rows, hd * hv:(hd + 1) * hv]
                dz_ref[rows, 2 * dk + dv + hd * hv:2 * dk + dv + (hd + 1) * hv] = (
                    d_og * oh * hg_ref[...] * (sg * (1.0 + g * (1.0 - sg)))).astype(BF16)
                don = d_og * (g * sg)
                dhg = dhg + _rowsum(don * oh)
                doh = don * hg_ref[...]
                d_o = (rstd * (doh - oh * jnp.mean(doh * oh, axis=-1, keepdims=True))).astype(BF16)
                dos_ref[rows, hd * hv:(hd + 1) * hv] = d_o
                dz_ref[rows, hd * hk:(hd + 1) * hk] = (_dot(d_o, s_b) * scale).astype(BF16)
            return dhg

        def recurrence(cc, carry):
            c = per - 1 - cc
            rows = chunk_rows(c)
            etot = jnp.exp(_rowsum(la_ref[rows, :]))
            for hd in range(HEADS):
                ks = slice(hd * hk, (hd + 1) * hk)
                q = (z_ref[rows, hd * hk:(hd + 1) * hk] * scale).astype(BF16)
                dec = dec_ref[rows, ks]
                kd = z_ref[rows, dk + hd * hk:dk + (hd + 1) * hk] * dec
                v = z_ref[rows, 2 * dk + hd * hv:2 * dk + (hd + 1) * hv].astype(BF16)
                s_prev = st_ref[c - 1, hd] if c > 0 else keep * stp_ref[0, hd]
                ds_t = ds_ref[hd] + _dot_tn(dos_ref[rows, hd * hv:(hd + 1) * hv], q)
                ds_b = ds_t.astype(BF16)
                dkd = _dot(v, ds_b)
                dz_ref[rows, 2 * dk + hd * hv:2 * dk + (hd + 1) * hv] = _dot_nt(kd.astype(BF16), ds_b).astype(BF16)
                dtot = etot[:, ks] * _rowsum(ds_t * s_prev)
                ds_ref[hd] = ds_t * etot[:, ks]
                dz_ref[rows, dk + hd * hk:dk + (hd + 1) * hk] = (dkd * dec).astype(BF16)
                e_ref[rows, ks] = dkd * kd
                dla_ref[rows, ks] = jnp.broadcast_to(dtot, (CHUNK, hk))
            return carry

        def decay_cotangent(c, carry):
            rows = chunk_rows(c)
            dla_ref[rows, :] += _tri_dot(tri_strict, e_ref[rows, :])
            return carry

        dhg = jnp.zeros((1, hv), F32)
        for c in range(per):
            dhg = recompute(c, dhg)
        dhg_ref[...] += dhg
        for phase in (recurrence, decay_cotangent):
            for c in range(per):
                phase(c, 0)
        dla = jnp.where(rows_id >= PAD_ROWS, dla_ref[...], 0.0)
        dgp = dla * (1.0 / GATE_NORM) * (1.0 - _sigmoid(gp))
        dgb_ref[...] += _rowsum(dgp)
        dgp_b = dgp.astype(BF16)
        dgw_ref[...] += _dot_tn(r.astype(BF16), dgp_b)
        dz_ref[:, r_at:r_at + GATE_PAD] = _dot_nt(dgp_b, gw_ref[...]).astype(BF16)
        _project_back(dz_ref, wi_ref, h_ref, g_ref, dh_ref, dhi_ref, dg_ref, step == 0)

    d = h.shape[1]
    back = lambda i: (ntile - 1 - i, 0)
    const2 = lambda i: (0, 0)
    return pl.pallas_call(
        body, grid=(ntile,),
        in_specs=dep_specs + [
                  pl.BlockSpec((rt, z.shape[1]), back), pl.BlockSpec((rt, d), back), _resident((1, d)),
                  _resident(w_in_t.shape), pl.BlockSpec((rt, d), back), _resident(w_out.shape),
                  pl.BlockSpec((per, HEADS, hv, hk), lambda i: (ntile - 1 - i, 0, 0, 0)),
                  pl.BlockSpec((1, HEADS, hv, hk), lambda i: (jnp.maximum((ntile - 1 - i) * per - 1, 0), 0, 0, 0)),
                  _resident(gate_w.shape), _resident((1, dk)), _resident((1, hv))],
        out_specs=[pl.BlockSpec((rt, d), back), pl.BlockSpec((1, d), const2), pl.BlockSpec((rt, z.shape[1]), back),
                   pl.BlockSpec(gate_w.shape, const2), pl.BlockSpec((1, dk), const2), pl.BlockSpec((1, hv), const2)],
        out_shape=[_sds((t, d), F32), _sds((1, d), F32), _sds(z.shape, BF16), _sds(gate_w.shape, F32),
                   _sds((1, dk), F32), _sds((1, hv), F32)],
        scratch_shapes=[pltpu.VMEM((HEADS, hv, hk), F32), pltpu.VMEM((rt, dk), F32), pltpu.VMEM((rt, dk), F32),
                        pltpu.VMEM((rt, dk), F32), pltpu.VMEM((rt, dv), BF16), pltpu.VMEM((rt, dk), F32),
                        pltpu.VMEM((rt, dv), F32)],
        compiler_params=_cparams("arbitrary"), name=name)(*deps, z, h, gain, w_in_t, dh, w_out, states, states, gate_w,
                                                          gate_b, head_g)


def _adamw_math(w, g, m, v):
    m = ADAM_B1 * m + (1.0 - ADAM_B1) * g
    v = ADAM_B2 * v + (1.0 - ADAM_B2) * (g * g)
    m_hat = m / (1.0 - ADAM_B1 ** ADAM_STEP)
    v_hat = v / (1.0 - ADAM_B2 ** ADAM_STEP)
    return -ADAM_LR * (m_hat / (jnp.sqrt(v_hat) + ADAM_EPS) + ADAM_WD * w), m, v


N_CHIP = N_DEV // 2
BLOCK_ELEMS = 128 * 1024


def _my_slot():
    return 4 * lax.axis_index("x") + 2 * lax.axis_index("y") + lax.axis_index("c")


def _row_block(r, c):
    cap = max(8, BLOCK_ELEMS // (-(-c // LANE) * LANE))
    return max([b for b in range(8, r + 1, 8) if r % b == 0 and b <= cap] or [r])


def _blocks(r, c):
    rb = _row_block(r, c)
    if rb < r or r * c <= BLOCK_ELEMS:
        return rb, c
    return r, max([b for b in (512, 256, LANE) if c % b == 0 and r * b <= BLOCK_ELEMS] or [c])


def _reduce_adam(parts, w, m, v, *, after=None, name):
    nl, r, c = w.shape
    rb, cb = _blocks(r, c)
    dep_specs, deps = _dep_specs(after)

    def body(*refs):
        me = refs[0][0]
        refs = refs[1 + len(deps):]
        p_refs = refs[:2 * nl]
        w_ref, m_ref, v_ref, g_out, d_out, m_out, v_out = refs[2 * nl:]
        layer = pl.program_id(0)
        for li in range(nl):
            @pl.when(layer == li)
            def _(li=li):
                own_ref, land_ref = p_refs[2 * li], p_refs[2 * li + 1]
                mine = own_ref[...].astype(F32)
                g = None
                for dev in range(N_DEV):
                    term = jnp.where(me == dev, mine, land_ref[dev].astype(F32))
                    g = term if g is None else g + term
                g_out[...] = g
                d_out[...], m_out[...], v_out[...] = _adamw_math(w_ref[...], g, m_ref[...], v_ref[...])

    blk = pl.BlockSpec((None, rb, cb), lambda l, i, j, me: (l, i, j))
    p_specs = []
    for li in range(nl):
        p_specs += [
            pl.BlockSpec((None, rb, cb), lambda l, i, j, me, li=li: (me[0], jnp.where(l == li, i, 0), jnp.where(l == li, j, 0))),
            pl.BlockSpec((N_DEV, rb, cb), lambda l, i, j, me, li=li: (0, jnp.where(l == li, i, 0), jnp.where(l == li, j, 0)))]
    flat = [p for pair in parts for p in pair]
    grid_spec = pltpu.PrefetchScalarGridSpec(
        num_scalar_prefetch=1, grid=(nl, r // rb, c // cb), in_specs=dep_specs + p_specs + [blk, blk, blk],
        out_specs=[blk] * 4)
    return pl.pallas_call(
        body, grid_spec=grid_spec, out_shape=[_sds(w.shape, F32)] * 4,
        compiler_params=_cparams("arbitrary", "arbitrary", "arbitrary"), name=name)(
        _my_slot().reshape(1), *deps, *flat, w, m, v)


def _sum8(own, landed, *, name):
    def body(own_ref, land_ref, o_ref):
        me = _my_slot()
        total = None
        for dev in range(N_DEV):
            term = jnp.where(me == dev, own_ref[...], land_ref[dev])
            total = term if total is None else total + term
        o_ref[...] = total

    return pl.pallas_call(body, out_shape=_sds(own.shape, F32), name=name)(own, landed)


def _adam_small(own, landed, split, w, m, v, *, name):
    n = len(w)

    def body(*refs):
        own_refs, land_refs, w_refs, m_refs, v_refs = (refs[k * n:(k + 1) * n] for k in range(5))
        outs = refs[5 * n:]
        me = _my_slot()
        for k in range(n):
            mine = own_refs[k][me] if split[k] else own_refs[k][...]
            g = None
            for dev in range(N_DEV):
                term = jnp.where(me == dev, mine, land_refs[k][dev])
                g = term if g is None else g + term
            outs[4 * k][...] = g
            outs[4 * k + 1][...], outs[4 * k + 2][...], outs[4 * k + 3][...] = _adamw_math(
                w_refs[k][...], g, m_refs[k][...], v_refs[k][...])

    out = pl.pallas_call(body, out_shape=[_sds(a.shape, F32) for a in w for _ in range(4)],
                         compiler_params=pltpu.CompilerParams(vmem_limit_bytes=V7X_VMEM_LIMIT), name=name)(
        *own, *landed, *w, *m, *v)
    return [tuple(out[4 * k:4 * k + 4]) for k in range(n)]


_HBM = pl.BlockSpec(memory_space=pltpu.HBM)
_SEM = pl.BlockSpec(memory_space=pltpu.SEMAPHORE)
_DATAFLOW = pltpu.SideEffectType.DATAFLOW_SIDE_EFFECTING


def _plan_to_all(src, land):
    x, y, c = lax.axis_index("x"), lax.axis_index("y"), lax.axis_index("c")
    return [(src, land.at[_my_slot()], (x ^ ((d >> 2) & 1), y ^ ((d >> 1) & 1), c ^ (d & 1))) for d in range(1, N_DEV)]


def _plan_split_to_all(src, land):
    x, y, c = lax.axis_index("x"), lax.axis_index("y"), lax.axis_index("c")
    peers = [(x ^ ((d >> 2) & 1), y ^ ((d >> 1) & 1), c ^ (d & 1)) for d in range(1, N_DEV)]
    return [(src.at[4 * px + 2 * py + pc], land.at[_my_slot()], (px, py, pc)) for px, py, pc in peers]


_PLAN_COPIES = {_plan_to_all: N_DEV - 1, _plan_split_to_all: N_DEV - 1}


def _plans(plan, n):
    return list(plan) if isinstance(plan, (list, tuple)) else [plan] * n


def _exchange_copies(plan, ins, lands, send, recv):
    copies, sem = [], 0
    for p, src, land in zip(_plans(plan, len(lands)), ins, lands):
        for s, dst, dev in p(src, land):
            copies.append(pltpu.make_async_remote_copy(
                src_ref=s, dst_ref=dst, send_sem=send.at[sem], recv_sem=recv.at[sem],
                device_id=dev, device_id_type=pl.DeviceIdType.MESH))
            sem += 1
    return copies


def _place_own(a, dtype, *, after=None, name):
    r, c = a.shape
    rb = _row_block(r, c)
    dep_specs, deps = _dep_specs(after)

    def body(*refs):
        a_ref, o_ref = refs[1 + len(deps):]
        o_ref[...] = a_ref[...].astype(dtype)

    grid_spec = pltpu.PrefetchScalarGridSpec(
        num_scalar_prefetch=1, grid=(r // rb,), in_specs=dep_specs + [pl.BlockSpec((rb, c), lambda i, me: (i, 0))],
        out_specs=pl.BlockSpec((None, rb, c), lambda i, me: (me[0], i, 0)))
    return pl.pallas_call(body, grid_spec=grid_spec, out_shape=_sds((N_DEV, r, c), dtype),
                          compiler_params=_cparams("arbitrary"), name=name)(_my_slot().reshape(1), *deps, a)


def _plan_gather_first(land, _):
    x, y, c = lax.axis_index("x"), lax.axis_index("y"), lax.axis_index("c")
    mine = land.at[_my_slot()]
    return [(mine, mine, (x, y, 1 - c))] + [(mine, mine, (x ^ (d >> 1), y ^ (d & 1), c)) for d in range(1, N_CHIP)]


def _plan_gather_relay(land, _):
    x, y, c = lax.axis_index("x"), lax.axis_index("y"), lax.axis_index("c")
    slots = [land.at[4 * (x ^ (d >> 1)) + 2 * (y ^ (d & 1)) + c] for d in range(1, N_CHIP)]
    return [(s, s, (x, y, 1 - c)) for s in slots]


def _plan_gather_direct(land, _):
    return _plan_to_all(land.at[_my_slot()], land)


_PLAN_COPIES[_plan_gather_first] = N_CHIP
_PLAN_COPIES[_plan_gather_relay] = N_CHIP - 1
_PLAN_COPIES[_plan_gather_direct] = N_DEV - 1


def _exchange_start(plan, arrs, lands, *, after=None, name):
    bufs = list(lands) if arrs is None else list(arrs) + list(lands)
    n, nb = len(lands), len(bufs)
    nsem = sum(_PLAN_COPIES[p] for p in _plans(plan, n))
    dep_specs, deps = _dep_specs(after)

    def body(*refs):
        ins, land_refs = refs[:n], refs[nb - n:nb]
        send, recv = refs[nb + len(deps)], refs[nb + len(deps) + 1]
        for cp in _exchange_copies(plan, ins, land_refs, send, recv):
            cp.start()
        refs[-1][...] = jnp.zeros_like(refs[-1])

    out = pl.pallas_call(
        body, name=name,
        out_shape=(pltpu.SemaphoreType.DMA((nsem,)), pltpu.SemaphoreType.DMA((nsem,)),
                   *[pltpu.HBM(a.shape, a.dtype) for a in bufs], _sds((8, LANE), F32)),
        in_specs=[_HBM] * nb + dep_specs,
        out_specs=(_SEM, _SEM, *([_HBM] * nb), pl.BlockSpec(memory_space=pltpu.VMEM)),
        input_output_aliases={i: 2 + i for i in range(nb)},
        compiler_params=pltpu.CompilerParams(has_side_effects=_DATAFLOW),
    )(*[pltpu.with_memory_space_constraint(a, pltpu.HBM) for a in bufs], *deps)
    return (plan, n, out[0], out[1], list(out[2:2 + nb])), out[-1]


def _exchange_now(plan, lands, *, name):
    n = len(lands)
    nsem = sum(_PLAN_COPIES[p] for p in _plans(plan, n))

    def body(*refs):
        land_refs, send, recv = refs[n:2 * n], refs[2 * n], refs[2 * n + 1]
        copies = _exchange_copies(plan, land_refs, land_refs, send, recv)
        for cp in copies:
            cp.start()
        for cp in copies:
            cp.wait_send()
            cp.wait_recv()

    hbm = pl.BlockSpec(memory_space=pl.ANY)
    return pl.pallas_call(
        body, in_specs=[hbm] * n, out_specs=[hbm] * n, out_shape=[_sds(a.shape, a.dtype) for a in lands],
        input_output_aliases={i: i for i in range(n)},
        scratch_shapes=[pltpu.SemaphoreType.DMA((nsem,)), pltpu.SemaphoreType.DMA((nsem,))], name=name)(*lands)


def _exchange_wait(state, after, *, name):
    plan, n, send_sem, recv_sem, bufs = state
    nb = len(bufs)
    after = list(after) if isinstance(after, (list, tuple)) else [after]

    def body(*refs):
        ins, land_refs, send, recv = refs[:n], refs[nb - n:nb], refs[nb], refs[nb + 1]
        for cp in _exchange_copies(plan, ins, land_refs, send, recv):
            cp.wait_send()
            cp.wait_recv()

    out = pl.pallas_call(
        body, name=name, out_shape=[pltpu.HBM(a.shape, a.dtype) for a in bufs],
        in_specs=[_HBM] * nb + [_SEM, _SEM] + [pl.BlockSpec(memory_space=pl.ANY)] * len(after), out_specs=[_HBM] * nb,
        input_output_aliases={i: i for i in range(nb)},
        compiler_params=pltpu.CompilerParams(has_side_effects=_DATAFLOW),
    )(*bufs, send_sem, recv_sem, *after)
    return list(out[:n]), list(out[nb - n:])


def _dep_specs(after):
    return ([], []) if after is None else ([pl.BlockSpec(memory_space=pl.ANY)], [after])


def _undo_column_split(g):
    return jnp.transpose(g, (1, 0, 2)).reshape(g.shape[1], N_DEV * g.shape[2])


def _column_split(a):
    r, c = a.shape
    return jnp.transpose(a.reshape(r, N_DEV, c // N_DEV), (1, 0, 2))


class _WholeWeights:
    def __init__(self, groups):
        self.groups = groups
        self.grads = {}

    def fetch(self, group, after):
        return self.groups[group]

    def emit(self, group, grads):
        self.grads.update(grads)
        return None


def _local_step(x, target, replicated, src):
    d = x.shape[1]
    mix_g, ffn_g = replicated["mix_g"], replicated["ffn_g"]
    cp = src.fetch("cp", [])
    cp_mid = (cp["conv_w"], replicated["conv_b"], replicated["ln_g"], replicated["ln_b"], replicated["pool_w"],
              replicated["pool_scale"])

    h1, cat, z0, u0, cv0, h0 = _cp_mid_fwd(x, cp["meta"], mix_g[0:1], cp["cp_w_in_t"], cp["cp_w_out"], *cp_mid,
                                           name="cp_mixer")
    ffn0 = src.fetch("ffn0", h1)
    h2, uf0, rf0 = _ffn_fwd(h1, ffn_g[0:1], ffn0["w1"], ffn0["w2"], name="ffn0")
    gla = src.fetch("gla", h2)
    gla_mid = (gla["gate_w"], gla["gate_b"], gla["head_g"])
    h3, og, states, z1, u1 = _gla_mid_fwd(h2, mix_g[1:2], gla["gla_w_in_t"], gla["gla_w_out"], *gla_mid, name="gla_mixer")
    ffn1 = src.fetch("ffn1", h3)
    dh4, uf1, rf1, loss, d_final_g = _ffn_fwd(h3, ffn_g[1:2], ffn1["w1"], ffn1["w2"],
                                              loss_head=(replicated["final_g"], target), name="ffn1_loss")

    dh3, dhh1, dob1, dffn_g1 = _ffn_bwd_x(h3, dh4, ffn_g[1:2], rf1, ffn1["w1"], ffn1["w2"], name="ffn1_bwd_x")
    sent = src.emit("ffn1_w1", dict(w1=_linear_bwd_w(uf1, dhh1, column_blocks=N_DEV, name="ffn1_dw1")))
    sent = src.emit("ffn1_w2", dict(w2=_linear_bwd_w(rf1, dob1, square_x=True, after=sent, name="ffn1_dw2")))
    d_gla_w_out = _linear_bwd_w(og, dh3, name="gla_out_dw")
    dh2, dmix_g1, dz1, d_gate_w, d_gate_b, d_head_g = _gla_mid_bwd(
        z1, h2, mix_g[1:2], gla["gla_w_in_t"], dh3, gla["gla_w_out"], states, *gla_mid, after=sent, name="gla_mixer_bwd")
    d_gla_w_in_t = _linear_bwd_w(dz1, u1, name="gla_in_dw")
    sent = src.emit("gla", dict(gla_w_in_t=d_gla_w_in_t, gla_w_out=d_gla_w_out))
    dh1, dhh0, dob0, dffn_g0 = _ffn_bwd_x(h1, dh2, ffn_g[0:1], rf0, ffn0["w1"], ffn0["w2"], after=sent, name="ffn0_bwd_x")
    sent = src.emit("ffn0_w1", dict(w1=_linear_bwd_w(uf0, dhh0, column_blocks=N_DEV, name="ffn0_dw1")))
    sent = src.emit("ffn0_w2", dict(w2=_linear_bwd_w(rf0, dob0, square_x=True, after=sent, name="ffn0_dw2")))
    d_cp_w_out = _linear_bwd_w(cat, dh1, after=sent, name="cp_out_dw")
    sent = src.emit("cp_out", dict(cp_w_out=d_cp_w_out))
    dx, dh0_first, dmix_g0, dz0, d_conv_w, d_conv_b, d_ln_g, d_ln_b, d_pool_w, d_pool_scale = _cp_mid_bwd(
        z0, cv0, h0, mix_g[0:1], cp["cp_w_in_t"], dh1, cp["cp_w_out"], *cp_mid, after=sent, name="cp_mixer_bwd")
    d_cp_w_in_t = _linear_bwd_w(dz0, u0, name="cp_in_dw")

    small = dict(
        mix_g=jnp.concatenate([dmix_g0, dmix_g1]), ffn_g=jnp.concatenate([dffn_g0, dffn_g1]), conv_b=d_conv_b, ln_g=d_ln_g,
        ln_b=d_ln_b, pool_w=d_pool_w, pool_scale=d_pool_scale, final_g=d_final_g, meta=dh0_first[PAD_ROWS:], conv_w=d_conv_w,
        gate_w=d_gate_w, gate_b=d_gate_b, head_g=d_head_g)
    src.emit("cp", dict(cp_w_in_t=d_cp_w_in_t, small=small, loss=loss))
    return loss, dx, small


_REPLICATED = ("mix_norm_g", "ffn_norm_g", "cp_conv_b", "cp_ln_g", "cp_ln_b", "cp_pool_w", "cp_pool_scale", "final_norm_g")
_SMALL_SHARDED = ("meta_tokens", "cp_conv_w", "gla_gate_w2", "gla_gate_b", "gla_head_g")
_NAMES = ("meta_tokens", "mix_norm_g", "ffn_norm_g", "ffn_w1", "ffn_w2", "cp_w_in", "cp_conv_w", "cp_conv_b", "cp_ln_g",
          "cp_ln_b", "cp_pool_w", "cp_pool_scale", "cp_w_out", "gla_w_in", "gla_gate_w2", "gla_gate_b", "gla_head_g",
          "gla_w_out", "final_norm_g")
_SMALL_GRADS = ("mix_g", "ffn_g", "conv_b", "ln_g", "ln_b", "pool_w", "pool_scale", "final_g", "meta", "conv_w", "gate_w",
                "gate_b", "head_g")
_GROUPS = ("cp", "ffn0", "gla", "ffn1")
_TWO_LEG_GATHERS = ("cp", "ffn0", "ffn1")


class _Exchanges:
    def __init__(self, w, d):
        self.d = d
        small = [w[n].reshape(w[n].shape[-2:]) for n in _SMALL_SHARDED]
        self.small_shard_shapes = [w[n].shape for n in _SMALL_SHARDED]
        shards = dict(
            cp=[(w["cp_w_in"][0].T, BF16), (w["cp_w_out"][0], BF16)] + [(a, F32) for a in small],
            ffn0=[(w["ffn_w1"][0], BF16), (w["ffn_w2"][0], BF16)],
            gla=[(w["gla_w_in"][0].T, BF16), (w["gla_w_out"][0], BF16)],
            ffn1=[(w["ffn_w1"][1], BF16), (w["ffn_w2"][1], BF16)])
        self.gathers = {}
        self.sent = {}
        token = None
        for group in _GROUPS:
            lands = [_place_own(a, dtype, after=token, name=f"place_w_{group}_{k}")
                     for k, (a, dtype) in enumerate(shards[group])]
            plan = _plan_gather_first if group in _TWO_LEG_GATHERS else _plan_gather_direct
            self.gathers[group], token = _exchange_start(plan, None, lands, after=token, name=f"start_w_{group}")
        self.token = token

    def fetch(self, group, after):
        d = self.d
        after = (list(after) if isinstance(after, (list, tuple)) else [after]) + [self.token]
        _, got = _exchange_wait(self.gathers[group], after, name=f"wait_w_{group}")
        if group in _TWO_LEG_GATHERS:
            got = _exchange_now(_plan_gather_relay, got, name=f"relay_w_{group}")
        if group in ("ffn0", "ffn1"):
            return dict(w1=got[0], w2=got[1])
        if group == "gla":
            w_in_t = jnp.pad(got[0].reshape(-1, d), ((0, GATE_PAD - GATE_RANK), (0, 0)))
            return dict(gla_w_in_t=w_in_t, gla_w_out=got[1].reshape(d, d), gate_w=self.gate_w, gate_b=self.gate_b,
                        head_g=self.head_g)
        meta, conv_w, gate_w, self.gate_b, self.head_g = [_undo_column_split(a) for a in got[2:]]
        self.gate_w = jnp.pad(gate_w, ((0, GATE_PAD - GATE_RANK), (0, 0))).astype(BF16)
        return dict(cp_w_in_t=got[0].reshape(-1, d), cp_w_out=got[1].reshape(d, d), meta=meta,
                    conv_w=jnp.pad(conv_w, ((0, 1), (0, 0))))

    def emit(self, group, g):
        d = self.d
        if group in ("ffn0_w1", "ffn1_w1"):
            arrs = [g["w1"]]
        elif group in ("ffn0_w2", "ffn1_w2"):
            arrs = [g["w2"].reshape(N_DEV, -1, d)]
        elif group == "gla":
            w_in_t = g["gla_w_in_t"][:3 * d + GATE_RANK]
            arrs = [w_in_t.reshape(N_DEV, -1, d), g["gla_w_out"].reshape(N_DEV, d // N_DEV, d)]
        elif group == "cp_out":
            arrs = [g["cp_w_out"].reshape(N_DEV, d // N_DEV, d)]
        else:
            s = dict(g["small"])
            s.update(pool_w=s["pool_w"][None], conv_w=s["conv_w"][:CONV_WIDTH], gate_w=s["gate_w"][:GATE_RANK])
            own = [s[n] for n in _SMALL_GRADS[:len(_REPLICATED)]]
            own += [_column_split(s[n]).reshape((N_DEV,) + shape)
                    for n, shape in zip(_SMALL_GRADS[len(_REPLICATED):], self.small_shard_shapes)]
            plans = [_plan_to_all] * len(_REPLICATED) + [_plan_split_to_all] * len(_SMALL_SHARDED)
            lands = [lax.empty((N_DEV,) + a.shape, F32) for a in own[:len(_REPLICATED)]]
            lands += [lax.empty(a.shape, F32) for a in own[len(_REPLICATED):]]
            own.append(g["loss"])
            plans.append(_plan_to_all)
            lands.append(lax.empty((N_DEV,) + g["loss"].shape, F32))
            self.small_sent, self.token = _exchange_start(plans, own, lands, after=self.token, name="start_g_small")
            arrs = [g["cp_w_in_t"].reshape(N_DEV, -1, d)]
        self.sent[group], self.token = _exchange_start(_plan_split_to_all, arrs, [lax.empty(a.shape, a.dtype) for a in arrs],
                                                       after=self.token, name=f"start_g_{group}")
        return self.token

    def finish(self, w, mom, var):
        out = {}
        after = self.token

        def landed(group):
            own, got = _exchange_wait(self.sent[group], after, name=f"wait_g_{group}")
            return list(zip(own, got))

        def adam(n, parts, behind=None, transposed=False):
            flip = (lambda a: jnp.transpose(a, (0, 2, 1))) if transposed else (lambda a: a)
            res = _reduce_adam(parts, flip(w[n]), flip(mom[n]), flip(var[n]), after=behind, name=f"adam_{n}")
            out[n] = tuple(flip(a) for a in res)
            return res[0]

        ffn1_w1 = landed("ffn1_w1")
        after = ffn1_w1[0][1]
        ffn1_w2 = landed("ffn1_w2")
        after = ffn1_w2[0][1]
        gla = landed("gla")
        after = adam("gla_w_in", [gla[0]], transposed=True)
        after = adam("gla_w_out", [gla[1]], after)
        ffn0_w1 = landed("ffn0_w1")
        after = adam("ffn_w1", [ffn0_w1[0], ffn1_w1[0]])
        ffn0_w2 = landed("ffn0_w2")
        after = adam("ffn_w2", [ffn0_w2[0], ffn1_w2[0]])
        small_own, small_landed = _exchange_wait(self.small_sent, after, name="wait_g_small")
        names = _REPLICATED + _SMALL_SHARDED
        split = [False] * len(_REPLICATED) + [True] * len(_SMALL_SHARDED)
        small_new = _adam_small(small_own[:-1], small_landed[:-1], split, [w[n] for n in names], [mom[n] for n in names],
                                [var[n] for n in names], name="adam_small")
        out.update(zip(names, small_new))
        out["loss"] = _sum8(small_own[-1], small_landed[-1], name="sum_loss")[0, 0]

        after = small_new[0][0]
        cp_out = landed("cp_out")
        after = adam("cp_w_out", [cp_out[0]])
        cp = landed("cp")
        adam("cp_w_in", [cp[0]], transposed=True)
        return out


def kernel(x, meta_tokens, mix_norm_g, ffn_norm_g, ffn_w1, ffn_w2, cp_w_in, cp_conv_w, cp_conv_b, cp_ln_g, cp_ln_b, cp_pool_w, cp_pool_scale, cp_w_out, gla_w_in, gla_gate_w2, gla_gate_b, gla_head_g, gla_w_out, final_norm_g, loss_target, m_meta_tokens, m_mix_norm_g, m_ffn_norm_g, m_ffn_w1, m_ffn_w2, m_cp_w_in, m_cp_conv_w, m_cp_conv_b, m_cp_ln_g, m_cp_ln_b, m_cp_pool_w, m_cp_pool_scale, m_cp_w_out, m_gla_w_in, m_gla_gate_w2, m_gla_gate_b, m_gla_head_g, m_gla_w_out, m_final_norm_g, v_meta_tokens, v_mix_norm_g, v_ffn_norm_g, v_ffn_w1, v_ffn_w2, v_cp_w_in, v_cp_conv_w, v_cp_conv_b, v_cp_ln_g, v_cp_ln_b, v_cp_pool_w, v_cp_pool_scale, v_cp_w_out, v_gla_w_in, v_gla_gate_w2, v_gla_gate_b, v_gla_head_g, v_gla_w_out, v_final_norm_g):
    w = dict(meta_tokens=meta_tokens, mix_norm_g=mix_norm_g, ffn_norm_g=ffn_norm_g, ffn_w1=ffn_w1, ffn_w2=ffn_w2,
             cp_w_in=cp_w_in, cp_conv_w=cp_conv_w, cp_conv_b=cp_conv_b, cp_ln_g=cp_ln_g, cp_ln_b=cp_ln_b,
             cp_pool_w=cp_pool_w, cp_pool_scale=cp_pool_scale, cp_w_out=cp_w_out, gla_w_in=gla_w_in,
             gla_gate_w2=gla_gate_w2, gla_gate_b=gla_gate_b, gla_head_g=gla_head_g, gla_w_out=gla_w_out,
             final_norm_g=final_norm_g.reshape(1, -1))
    mom = dict(meta_tokens=m_meta_tokens, mix_norm_g=m_mix_norm_g, ffn_norm_g=m_ffn_norm_g, ffn_w1=m_ffn_w1, ffn_w2=m_ffn_w2,
               cp_w_in=m_cp_w_in, cp_conv_w=m_cp_conv_w, cp_conv_b=m_cp_conv_b, cp_ln_g=m_cp_ln_g, cp_ln_b=m_cp_ln_b,
               cp_pool_w=m_cp_pool_w, cp_pool_scale=m_cp_pool_scale, cp_w_out=m_cp_w_out, gla_w_in=m_gla_w_in,
               gla_gate_w2=m_gla_gate_w2, gla_gate_b=m_gla_gate_b, gla_head_g=m_gla_head_g, gla_w_out=m_gla_w_out,
               final_norm_g=m_final_norm_g.reshape(1, -1))
    var = dict(meta_tokens=v_meta_tokens, mix_norm_g=v_mix_norm_g, ffn_norm_g=v_ffn_norm_g, ffn_w1=v_ffn_w1, ffn_w2=v_ffn_w2,
               cp_w_in=v_cp_w_in, cp_conv_w=v_cp_conv_w, cp_conv_b=v_cp_conv_b, cp_ln_g=v_cp_ln_g, cp_ln_b=v_cp_ln_b,
               cp_pool_w=v_cp_pool_w, cp_pool_scale=v_cp_pool_scale, cp_w_out=v_cp_w_out, gla_w_in=v_gla_w_in,
               gla_gate_w2=v_gla_gate_w2, gla_gate_b=v_gla_gate_b, gla_head_g=v_gla_head_g, gla_w_out=v_gla_w_out,
               final_norm_g=v_final_norm_g.reshape(1, -1))
    d = x.shape[-1]
    replicated = dict(mix_g=w["mix_norm_g"], ffn_g=w["ffn_norm_g"], conv_b=w["cp_conv_b"], ln_g=w["cp_ln_g"],
                      ln_b=w["cp_ln_b"], pool_w=w["cp_pool_w"][0].astype(BF16), pool_scale=w["cp_pool_scale"],
                      final_g=w["final_norm_g"])
    exchanges = _Exchanges(w, d)
    _, grad_x, _ = _local_step(x[0], loss_target[0], replicated, exchanges)
    out = exchanges.finish(w, mom, var)
    loss = out.pop("loss")

    def leaf(n, k):
        a = out[n][k]
        return a.reshape(-1) if n == "final_norm_g" else a

    return (loss, grad_x[None], *[leaf(n, 0) for n in _NAMES], *[leaf(n, 1) for n in _NAMES],
            *[leaf(n, 2) for n in _NAMES], *[leaf(n, 3) for n in _NAMES])
```

```python
import functools

import jax
import jax.numpy as jnp
from jax import lax
from jax.experimental import pallas as pl
from jax.experimental.pallas import tpu as pltpu

F32, BF16 = jnp.float32, jnp.bfloat16
N_DEV = 8
CHUNK = 64
N_META = 16
PAD_ROWS = CHUNK - N_META
HALO = 32
EPS = 1e-5
CONV_WIDTH = 31
POOL_WINDOWS = (2, 4, 8, 16)
HEADS = 4
GATE_RANK = 16
GATE_NORM = 16.0
GATE_PAD = 128
ADAM_LR, ADAM_B1, ADAM_B2, ADAM_EPS, ADAM_WD, ADAM_STEP = 0.001, 0.9, 0.999, 1e-08, 0.01, 10
V7X_VMEM_LIMIT = 56 * 2 ** 20
LANE = 128


def _cparams(*sem):
    return pltpu.CompilerParams(dimension_semantics=sem, vmem_limit_bytes=V7X_VMEM_LIMIT)


def _row_tile(t, cap):
    best = CHUNK
    for r in range(CHUNK, min(t, cap) + 1, CHUNK):
        if t % r == 0:
            best = r
    return best


def _resident(shape):
    return pl.BlockSpec(shape, lambda *_: (0,) * len(shape), pipeline_mode=pl.Buffered(1))


def _dot(a, b):
    return jnp.dot(a, b, preferred_element_type=F32)


def _dot_nt(a, b):
    return lax.dot_general(a, b, (((1,), (1,)), ((), ())), preferred_element_type=F32)


def _dot_tn(a, b):
    return lax.dot_general(a, b, (((0,), (0,)), ((), ())), preferred_element_type=F32)


def _rowsum(a):
    return jnp.sum(a, axis=0, keepdims=True)


def _sigmoid(a):
    return 1.0 / (1.0 + jnp.exp(-a))


def _row_ids(tile, rt):
    return tile * rt + lax.broadcasted_iota(jnp.int32, (rt, 1), 0)


def _sds(shape, dtype):
    return jax.ShapeDtypeStruct(shape, dtype)


DW_ROWS = 1024


def _linear_bwd_w(x, dy, *, square_x=False, column_blocks=None, row_blocks=None, after=None, name):
    t, k = x.shape
    n = dy.shape[1]
    cut_k = k > n and column_blocks is None
    assert cut_k or row_blocks is None
    width = k if cut_k else n
    blk = n // column_blocks if column_blocks else max(c for c in (640, 512, 384, 256, LANE) if width % c == 0)
    dep_specs, deps = _dep_specs(after)

    def body(*refs):
        x_ref, dy_ref, o_ref, acc = refs[len(deps):]
        for c0 in range(0, t, DW_ROWS):
            rows = slice(c0, min(c0 + DW_ROWS, t))
            xv = x_ref[rows, :]
            if square_x:
                xv = xv.astype(F32)
                xv = xv * xv
            part = _dot_tn(xv.astype(BF16), dy_ref[rows, :].astype(BF16))
            if c0 == 0:
                acc[...] = part
            else:
                acc[...] += part
        if row_blocks is None:
            o_ref[...] = acc[...].astype(BF16)
            return
        nb, rpb = row_blocks
        for s in range(width // blk):
            @pl.when(pl.program_id(0) == s)
            def _(s=s):
                for b in range(nb):
                    lo, hi = max(s * blk, b * rpb), min((s + 1) * blk, (b + 1) * rpb)
                    if lo < hi:
                        o_ref[b, lo - b * rpb:hi - b * rpb, :] = acc[lo - s * blk:hi - s * blk, :].astype(BF16)

    out_shape = _sds((k, n), BF16)
    semantics = "parallel"
    if cut_k:
        in_specs = [pl.BlockSpec((t, blk), lambda j: (0, j)), _resident((t, n))]
        out_specs = pl.BlockSpec((blk, n), lambda j: (j, 0))
        acc_shape = (blk, n)
        if row_blocks:
            out_shape = _sds(row_blocks + (n,), BF16)
            out_specs = pl.BlockSpec(out_shape.shape, lambda j: (0, 0, 0))
            semantics = "arbitrary"
    else:
        in_specs = [_resident((t, k)), pl.BlockSpec((t, blk), lambda j: (0, j))]
        out_specs = pl.BlockSpec((k, blk), lambda j: (0, j))
        acc_shape = (k, blk)
        if column_blocks:
            out_specs = pl.BlockSpec((None, k, blk), lambda j: (j, 0, 0))
            out_shape = _sds((column_blocks, k, blk), BF16)
    return pl.pallas_call(
        body, grid=(width // blk,), in_specs=dep_specs + in_specs, out_specs=out_specs, out_shape=out_shape,
        scratch_shapes=[pltpu.VMEM(acc_shape, F32)], compiler_params=_cparams(semantics), name=name)(*deps, x, dy)


FFN_BLOCKS_PER_STEP = 2


def _ffn_fwd(h, gain, w1g, w2g, *, loss_head=None, name):
    t, d = h.shape
    f8 = w1g.shape[-1]
    rt = _row_tile(t, 832)
    nb = FFN_BLOCKS_PER_STEP
    nstep = N_DEV // nb

    def body(*refs):
        if loss_head is None:
            h_ref, g_ref, w1_ref, w2_ref, o_ref, u_ref, r_ref, acc_ref = refs
        else:
            (h_ref, g_ref, w1_ref, w2_ref, fg_ref, tgt_ref, o_ref, u_ref, r_ref, loss_ref, dfg_ref, acc_ref, t_ref,
             t_sem) = refs
        i, j = pl.program_id(0), pl.program_id(1)

        def target_rows(act):
            @pl.when(i == 0)
            def _():
                act(pltpu.make_async_copy(tgt_ref.at[pl.ds(0, rt - CHUNK)], t_ref.at[pl.ds(CHUNK, rt - CHUNK)], t_sem.at[0]))

            if t > rt:
                @pl.when(i > 0)
                def _():
                    act(pltpu.make_async_copy(tgt_ref.at[pl.ds(pl.multiple_of(i * rt - CHUNK, CHUNK), rt)], t_ref,
                                              t_sem.at[0]))

        @pl.when(j == 0)
        def _():
            if loss_head is not None:
                @pl.when(i == 0)
                def _():
                    t_ref[0:CHUNK, :] = jnp.zeros((CHUNK, d), F32)

                target_rows(lambda copy: copy.start())

            hv = h_ref[...]
            u_ref[...] = (hv * lax.rsqrt(jnp.mean(hv * hv, axis=-1, keepdims=True) + EPS) * g_ref[...]).astype(BF16)
            acc_ref[...] = jnp.zeros_like(acc_ref)

        part = None
        for b in range(nb):
            a = jnp.maximum(_dot(u_ref[...], w1_ref[b]), 0.0)
            r_ref[:, b * f8:(b + 1) * f8] = a.astype(BF16)
            term = _dot((a * a).astype(BF16), w2_ref[b])
            part = term if part is None else part + term
        acc_ref[...] += part

        @pl.when(j == nstep - 1)
        def _():
            y = h_ref[...] + acc_ref[...]
            if loss_head is None:
                o_ref[...] = y
                return

            @pl.when(i == 0)
            def _():
                loss_ref[...] = jnp.zeros_like(loss_ref)
                dfg_ref[...] = jnp.zeros_like(dfg_ref)

            target_rows(lambda copy: copy.wait())

            rstd = lax.rsqrt(jnp.mean(y * y, axis=-1, keepdims=True) + EPS)
            xh = y * rstd
            err = jnp.where(_row_ids(i, rt) >= CHUNK, xh * fg_ref[...] - t_ref[...], 0.0)
            loss_ref[...] += (0.5 / d) * jnp.sum(err * err)
            dy = err * (1.0 / d)
            dfg_ref[...] += _rowsum(dy * xh)
            dxh = dy * fg_ref[...]
            o_ref[...] = rstd * (dxh - xh * jnp.mean(dxh * xh, axis=-1, keepdims=True))

    rows = lambda i, j: (i, 0)
    in_specs = [pl.BlockSpec((rt, d), rows), _resident((1, d)),
                pl.BlockSpec((nb, d, f8), lambda i, j: (j, 0, 0)), pl.BlockSpec((nb, f8, d), lambda i, j: (j, 0, 0))]
    out_specs = [pl.BlockSpec((rt, d), rows), pl.BlockSpec((rt, d), rows), pl.BlockSpec((rt, nb * f8), lambda i, j: (i, j))]
    out_shape = [_sds((t, d), F32), _sds((t, d), BF16), _sds((t, N_DEV * f8), BF16)]
    args = [h, gain, w1g, w2g]
    scratch_shapes = [pltpu.VMEM((rt, d), F32)]
    if loss_head is not None:
        in_specs += [_resident((1, d)), pl.BlockSpec(memory_space=pl.ANY)]
        out_specs += [pl.BlockSpec((8, LANE), lambda i, j: (0, 0)), pl.BlockSpec((1, d), lambda i, j: (0, 0))]
        out_shape += [_sds((8, LANE), F32), _sds((1, d), F32)]
        args += list(loss_head)
        scratch_shapes += [pltpu.VMEM((rt, d), F32), pltpu.SemaphoreType.DMA((1,))]
    return pl.pallas_call(
        body, grid=(t // rt, nstep), in_specs=in_specs, out_specs=out_specs, out_shape=out_shape,
        scratch_shapes=scratch_shapes,
        compiler_params=_cparams("arbitrary" if loss_head is not None else "parallel", "arbitrary"), name=name)(*args)


def _ffn_bwd_x(h, dout, gain, r, w1g, w2g, *, after=None, name):
    t, d = h.shape
    f8 = w1g.shape[-1]
    rt = _row_tile(t, 832)
    nb = FFN_BLOCKS_PER_STEP
    last = N_DEV // nb - 1
    dep_specs, deps = _dep_specs(after)

    def body(*refs):
        h_ref, do_ref, g_ref, r_ref, w1_ref, w2_ref, dh_ref, dhh_ref, dob_ref, dg_ref, du_ref = refs[len(deps):]
        i, j = pl.program_id(0), pl.program_id(1)

        @pl.when(j == 0)
        def _():
            dob_ref[...] = do_ref[...].astype(BF16)
            du_ref[...] = jnp.zeros_like(du_ref)

        part = None
        for b in range(nb):
            cols = slice(b * f8, (b + 1) * f8)
            dhh = (_dot_nt(dob_ref[...], w2_ref[b]) * (2.0 * r_ref[:, cols].astype(F32))).astype(BF16)
            dhh_ref[:, cols] = dhh
            term = _dot_nt(dhh, w1_ref[b])
            part = term if part is None else part + term
        du_ref[...] += part

        @pl.when(j == last)
        def _():
            @pl.when(i == 0)
            def _():
                dg_ref[...] = jnp.zeros_like(dg_ref)

            hv = h_ref[...]
            rstd = lax.rsqrt(jnp.mean(hv * hv, axis=-1, keepdims=True) + EPS)
            xh = hv * rstd
            du = du_ref[...]
            dg_ref[...] += _rowsum(du * xh)
            dxh = du * g_ref[...]
            dh_ref[...] = do_ref[...] + rstd * (dxh - xh * jnp.mean(dxh * xh, axis=-1, keepdims=True))

    rows = lambda i, j: (i, 0)
    return pl.pallas_call(
        body, grid=(t // rt, N_DEV // nb),
        in_specs=dep_specs + [
                  pl.BlockSpec((rt, d), rows), pl.BlockSpec((rt, d), rows), _resident((1, d)),
                  pl.BlockSpec((rt, nb * f8), lambda i, j: (i, j)),
                  pl.BlockSpec((nb, d, f8), lambda i, j: (j, 0, 0)),
                  pl.BlockSpec((nb, f8, d), lambda i, j: (j, 0, 0))],
        out_specs=[pl.BlockSpec((rt, d), rows), pl.BlockSpec((rt, nb * f8), lambda i, j: (i, j)),
                   pl.BlockSpec((rt, d), rows), pl.BlockSpec((1, d), lambda i, j: (0, 0))],
        out_shape=[_sds((t, d), F32), _sds((t, N_DEV * f8), BF16), _sds((t, d), BF16), _sds((1, d), F32)],
        scratch_shapes=[pltpu.VMEM((rt, d), F32)],
        compiler_params=_cparams("arbitrary", "arbitrary"), name=name)(*deps, h, dout, gain, r, w1g, w2g)


def _lane_blocks(width):
    lb = min(LANE, width)
    return [slice(s, s + lb) for s in range(0, width, lb)]


def _conv_rows(src_ref, w_ref, offset, dst_ref, nblk, width, bias_ref=None):
    def blk(rb, carry):
        base = pl.multiple_of(rb * CHUNK, CHUNK)
        for l, ls in enumerate(_lane_blocks(width)):
            acc = jnp.zeros((CHUNK, ls.stop - ls.start), F32)
            if bias_ref is not None:
                acc = acc + bias_ref[:, ls]
            for k in range(CONV_WIDTH):
                acc = acc + w_ref[k:k + 1, ls] * src_ref[l, pl.ds(base + offset(k), CHUNK), :]
            dst_ref[l, pl.ds(base, CHUNK), :] = acc
        return carry

    lax.fori_loop(0, nblk, blk, 0)


def _to_lane_blocks(ref, row0, value):
    for l, ls in enumerate(_lane_blocks(value.shape[1])):
        ref[l, row0:row0 + value.shape[0], :] = value[:, ls]


def _from_lane_blocks(ref):
    return jnp.concatenate([ref[l] for l in range(ref.shape[0])], axis=1)


def _pool_counts(rows, window):
    return jnp.clip(rows - PAD_ROWS + 1, 1, window).astype(F32)


def _trailing_sum(v, window):
    s, sh = v, 1
    while sh < window:
        s = s + pltpu.roll(s, sh, 0)
        sh *= 2
    return s


def _leading_sum(v, window):
    s, sh, n = v, 1, v.shape[0]
    while sh < window:
        s = s + pltpu.roll(s, n - sh, 0)
        sh *= 2
    return s


def _norm_project(h_ref, g_ref, w_t_ref, u_ref, z_ref):
    hv = h_ref[...]
    u = (hv * lax.rsqrt(jnp.mean(hv * hv, axis=-1, keepdims=True) + EPS) * g_ref[...]).astype(BF16)
    u_ref[...] = u
    z_ref[...] = _dot_nt(u, w_t_ref[...])


def _project_back(dz_ref, w_t_ref, h_ref, g_ref, dres_ref, dh_ref, dg_ref, first):
    dx = _dot(dz_ref[...], w_t_ref[...])
    hv = h_ref[...]
    rstd = lax.rsqrt(jnp.mean(hv * hv, axis=-1, keepdims=True) + EPS)
    xh = hv * rstd

    @pl.when(first)
    def _():
        dg_ref[...] = jnp.zeros_like(dg_ref)

    dg_ref[...] += _rowsum(dx * xh)
    dxh = dx * g_ref[...]
    dh_ref[...] = dres_ref[...] + rstd * (dxh - xh * jnp.mean(dxh * xh, axis=-1, keepdims=True))


def _cp_mid_fwd(x, meta, gain, w_in_t, w_out, conv_w, conv_b, ln_g, ln_b, pool_w, pool_scale, *, name):
    seq, d = x.shape
    t = seq + CHUNK
    ein = w_in_t.shape[0]
    cd = conv_b.shape[1]
    pd = pool_scale.shape[1]
    pg = pd // len(POOL_WINDOWS)
    rt = _row_tile(t, 320)
    ntile = t // rt

    def body(x_ref, meta_ref, g_ref, wi_ref, wo_ref, cw_ref, cb_ref, lg_ref, lb_ref, pw_ref, ps_ref,
             ho_ref, o_ref, z_ref, u_ref, cv_ref, h0_ref, gext, pext, conv_s, hbuf, hsem):
        i = pl.program_id(0)
        slot = i % 2
        first_rows = pltpu.make_async_copy(x_ref.at[pl.ds(0, rt - CHUNK)], hbuf.at[0, pl.ds(CHUNK, rt - CHUNK)], hsem.at[0])

        def tile_rows(tile, to):
            return pltpu.make_async_copy(x_ref.at[pl.ds(pl.multiple_of(tile * rt - CHUNK, CHUNK), rt)], hbuf.at[to],
                                         hsem.at[to])

        @pl.when(i == 0)
        def _():
            first_rows.start()
            hbuf[0, 0:PAD_ROWS, :] = jnp.zeros((PAD_ROWS, d), F32)
            hbuf[0, PAD_ROWS:CHUNK, :] = meta_ref[...]
            _to_lane_blocks(gext, 0, jnp.zeros((HALO, cd), F32))
            pext[0:HALO, :] = jnp.zeros((HALO, pd), F32)

        @pl.when(i + 1 < ntile)
        def _():
            tile_rows(i + 1, 1 - slot).start()

        @pl.when(i == 0)
        def _():
            first_rows.wait()

        @pl.when(i > 0)
        def _():
            tile_rows(i, slot).wait()

        h_ref = hbuf.at[slot]
        h0_ref[...] = h_ref[...]
        _norm_project(h_ref, g_ref, wi_ref, u_ref, z_ref)

        _to_lane_blocks(gext, HALO, z_ref[:, 0:cd] * _sigmoid(z_ref[:, cd:2 * cd]))
        pext[HALO:HALO + rt, :] = z_ref[:, 2 * cd:]
        _conv_rows(gext, cw_ref, lambda k: k + HALO - (CONV_WIDTH - 1), conv_s, rt // CHUNK, cd, cb_ref)
        cv = _from_lane_blocks(conv_s)
        cv_ref[...] = cv
        xc = cv - jnp.mean(cv, axis=-1, keepdims=True)
        y = xc * lax.rsqrt(jnp.mean(xc * xc, axis=-1, keepdims=True) + EPS) * lg_ref[...] + lb_ref[...]
        rows = _row_ids(i, rt)
        a = jnp.where(rows >= PAD_ROWS, y * _sigmoid(y), 0.0)
        o_ref[:, 0:cd] = a.astype(BF16)
        for gi, window in enumerate(POOL_WINDOWS):
            ls = slice(gi * pg, (gi + 1) * pg)
            v = pext[:, ls]
            tm = _trailing_sum(v, window)[HALO:] / _pool_counts(rows, window) - v[HALO:]
            p = _dot(tm.astype(BF16), pw_ref[gi]) * ps_ref[:, ls]
            o_ref[:, cd + gi * pg:cd + (gi + 1) * pg] = p.astype(BF16)
        ho_ref[...] = h_ref[...] + _dot(o_ref[...], wo_ref[...])
        gext[:, 0:HALO, :] = gext[:, rt:rt + HALO, :]
        pext[0:HALO, :] = pext[rt:rt + HALO, :]

    nl, lb = len(_lane_blocks(cd)), min(LANE, cd)
    rows = lambda i: (i, 0)
    return pl.pallas_call(
        body, grid=(ntile,),
        in_specs=[pl.BlockSpec(memory_space=pl.ANY), _resident(meta.shape), _resident((1, d)), _resident(w_in_t.shape),
                  _resident(w_out.shape), _resident(conv_w.shape), _resident((1, cd)),
                  _resident((1, cd)), _resident((1, cd)), _resident(pool_w.shape), _resident((1, pd))],
        out_specs=[pl.BlockSpec((rt, d), rows), pl.BlockSpec((rt, cd + pd), rows), pl.BlockSpec((rt, ein), rows),
                   pl.BlockSpec((rt, d), rows), pl.BlockSpec((rt, cd), rows), pl.BlockSpec((rt, d), rows)],
        out_shape=[_sds((t, d), F32), _sds((t, cd + pd), BF16), _sds((t, ein), F32), _sds((t, d), BF16),
                   _sds((t, cd), F32), _sds((t, d), F32)],
        scratch_shapes=[pltpu.VMEM((nl, rt + HALO, lb), F32), pltpu.VMEM((rt + HALO, pd), F32),
                        pltpu.VMEM((nl, rt, lb), F32), pltpu.VMEM((2, rt, d), F32), pltpu.SemaphoreType.DMA((2,))],
        compiler_params=_cparams("arbitrary"), name=name)(x, meta, gain, w_in_t, w_out, conv_w, conv_b, ln_g, ln_b, pool_w,
                                                          pool_scale)


def _cp_mid_bwd(z, cv, h, gain, w_in_t, dh, w_out, conv_w, conv_b, ln_g, ln_b, pool_w, pool_scale, *, after=None, name):
    t, ein = z.shape
    cd = conv_b.shape[1]
    pd = pool_scale.shape[1]
    pg = pd // len(POOL_WINDOWS)
    rt = _row_tile(t, 320)
    ntile = t // rt
    per = rt // CHUNK
    dep_specs, deps = _dep_specs(after)

    def body(*refs):
        (z_ref, zh_ref, cv_ref, h_ref, g_ref, wi_ref, dh_ref, wo_ref, cw_ref, cb_ref, lg_ref, lb_ref, pw_ref, ps_ref,
         dx_ref, dfirst_ref, dg_ref, dz_ref, dcw_ref, dcb_ref, dlg_ref, dlb_ref, dpw_ref, dps_ref,
         gext, pext, conv_s, dcv, dsp, dhi, dx_sem) = refs[len(deps):]
        step = pl.program_id(0)
        tile = ntile - 1 - step
        first_rows = pltpu.make_async_copy(dhi.at[pl.ds(CHUNK, rt - CHUNK)], dx_ref.at[pl.ds(0, rt - CHUNK)], dx_sem.at[0])

        def tile_rows(tile):
            return pltpu.make_async_copy(dhi, dx_ref.at[pl.ds(pl.multiple_of(tile * rt - CHUNK, CHUNK), rt)], dx_sem.at[0])

        dcat = _dot_nt(dh_ref[...].astype(BF16), wo_ref[...])

        @pl.when(step == 0)
        def _():
            for ref in (dcw_ref, dcb_ref, dlg_ref, dlb_ref, dpw_ref, dps_ref):
                ref[...] = jnp.zeros_like(ref)
            _to_lane_blocks(dcv, rt, jnp.zeros((HALO, cd), F32))
            dsp[rt:rt + HALO, :] = jnp.zeros((HALO, pd), F32)

        keep = jnp.where(tile > 0, 1.0, 0.0)
        zh = zh_ref[CHUNK - HALO:CHUNK, :]
        _to_lane_blocks(gext, 0, keep * zh[:, 0:cd] * _sigmoid(zh[:, cd:2 * cd]))
        pext[0:HALO, :] = keep * zh[:, 2 * cd:]
        za = z_ref[:, 0:cd]
        sg = _sigmoid(z_ref[:, cd:2 * cd])
        _to_lane_blocks(gext, HALO, za * sg)
        pext[HALO:HALO + rt, :] = z_ref[:, 2 * cd:]
        cv = cv_ref[...]
        xc = cv - jnp.mean(cv, axis=-1, keepdims=True)
        rstd = lax.rsqrt(jnp.mean(xc * xc, axis=-1, keepdims=True) + EPS)
        xh = xc * rstd
        y = xh * lg_ref[...] + lb_ref[...]
        sy = _sigmoid(y)
        rows = _row_ids(tile, rt)
        da = jnp.where(rows >= PAD_ROWS, dcat[:, 0:cd], 0.0)
        dy = da * (sy * (1.0 + y * (1.0 - sy)))
        dlg_ref[...] += _rowsum(dy * xh)
        dlb_ref[...] += _rowsum(dy)
        dxh = dy * lg_ref[...]
        dconv = rstd * (dxh - jnp.mean(dxh, axis=-1, keepdims=True) - xh * jnp.mean(dxh * xh, axis=-1, keepdims=True))
        dcb_ref[...] += _rowsum(dconv)
        _to_lane_blocks(dcv, 0, dconv)
        for l, ls in enumerate(_lane_blocks(cd)):
            def acc_rows(rb, accs, l=l):
                base = pl.multiple_of(rb * CHUNK, CHUNK)
                d_blk = dcv[l, pl.ds(base, CHUNK), :]
                out = []
                for k in range(CONV_WIDTH):
                    prod = d_blk * gext[l, pl.ds(base + k + HALO - (CONV_WIDTH - 1), CHUNK), :]
                    part = prod[0:8]
                    for s in range(8, CHUNK, 8):
                        part = part + prod[s:s + 8]
                    out.append(accs[k] + part)
                return tuple(out)

            zero = jnp.zeros((8, ls.stop - ls.start), F32)
            accs = lax.fori_loop(0, per, acc_rows, (zero,) * CONV_WIDTH)
            for k in range(CONV_WIDTH):
                dcw_ref[k:k + 1, ls] += _rowsum(accs[k])
        _conv_rows(dcv, cw_ref, lambda k: CONV_WIDTH - 1 - k, conv_s, per, cd)
        dglu = _from_lane_blocks(conv_s)
        dz_ref[:, 0:cd] = (dglu * sg).astype(BF16)
        dz_ref[:, cd:2 * cd] = (dglu * za * sg * (1.0 - sg)).astype(BF16)
        dcv[:, rt:rt + HALO, :] = dcv[:, 0:HALO, :]
        for gi, window in enumerate(POOL_WINDOWS):
            ls = slice(gi * pg, (gi + 1) * pg)
            v = pext[:, ls]
            cnt = _pool_counts(rows, window)
            tm = (_trailing_sum(v, window)[HALO:] / cnt - v[HALO:]).astype(BF16)
            dp = dcat[:, cd + gi * pg:cd + (gi + 1) * pg]
            dps_ref[:, ls] += _rowsum(dp * _dot(tm, pw_ref[gi]))
            dpl = (dp * ps_ref[:, ls]).astype(BF16)
            dpw_ref[gi] += _dot_tn(tm, dpl)
            dtm = _dot_nt(dpl, pw_ref[gi])
            dsp[0:rt, ls] = dtm / cnt
            dpin = _leading_sum(dsp[:, ls], window)[0:rt] - dtm
            dz_ref[:, 2 * cd + gi * pg:2 * cd + (gi + 1) * pg] = dpin.astype(BF16)
        dsp[rt:rt + HALO, :] = dsp[0:HALO, :]

        @pl.when(step > 0)
        def _():
            tile_rows(tile + 1).wait()

        _project_back(dz_ref, wi_ref, h_ref, g_ref, dh_ref, dhi, dg_ref, step == 0)

        @pl.when(tile > 0)
        def _():
            tile_rows(tile).start()

        @pl.when(tile == 0)
        def _():
            first_rows.start()
            dfirst_ref[...] = dhi[0:CHUNK, :]
            first_rows.wait()

    d = h.shape[1]
    back = lambda i: (ntile - 1 - i, 0)
    halo_idx = lambda i: (jnp.maximum((ntile - 1 - i) * per - 1, 0), 0)
    const2 = lambda i: (0, 0)
    nl, lb = len(_lane_blocks(cd)), min(LANE, cd)
    return pl.pallas_call(
        body, grid=(ntile,),
        in_specs=dep_specs + [
                  pl.BlockSpec((rt, ein), back), pl.BlockSpec((CHUNK, ein), halo_idx), pl.BlockSpec((rt, cd), back),
                  pl.BlockSpec((rt, d), back),
                  _resident((1, d)), _resident(w_in_t.shape), pl.BlockSpec((rt, d), back), _resident(w_out.shape),
                  _resident(conv_w.shape), _resident((1, cd)), _resident((1, cd)), _resident((1, cd)),
                  _resident(pool_w.shape), _resident((1, pd))],
        out_specs=[pl.BlockSpec(memory_space=pl.ANY), pl.BlockSpec((CHUNK, d), const2), pl.BlockSpec((1, d), const2),
                   pl.BlockSpec((rt, ein), back), pl.BlockSpec(conv_w.shape, const2), pl.BlockSpec((1, cd), const2),
                   pl.BlockSpec((1, cd), const2), pl.BlockSpec((1, cd), const2),
                   pl.BlockSpec(pool_w.shape, lambda i: (0, 0, 0)), pl.BlockSpec((1, pd), const2)],
        out_shape=[_sds((t - CHUNK, d), F32), _sds((CHUNK, d), F32), _sds((1, d), F32),
                   _sds((t, ein), BF16), _sds(conv_w.shape, F32), _sds((1, cd), F32), _sds((1, cd), F32),
                   _sds((1, cd), F32), _sds(pool_w.shape, F32), _sds((1, pd), F32)],
        scratch_shapes=[pltpu.VMEM((nl, rt + HALO, lb), F32), pltpu.VMEM((rt + HALO, pd), F32), pltpu.VMEM((nl, rt, lb), F32),
                        pltpu.VMEM((nl, rt + HALO, lb), F32), pltpu.VMEM((rt + HALO, pd), F32), pltpu.VMEM((rt, d), F32),
                        pltpu.SemaphoreType.DMA((1,))],
        compiler_params=_cparams("arbitrary"), name=name)(*deps, z, z, cv, h, gain, w_in_t, dh, w_out, conv_w, conv_b, ln_g,
                                                          ln_b, pool_w, pool_scale)


def _log_decay(r, gw_ref, gb_ref, rows):
    gp = _dot(r.astype(BF16), gw_ref[...]) + gb_ref[...]
    log_sig = jnp.minimum(gp, 0.0) - jnp.log(1.0 + jnp.exp(-jnp.abs(gp)))
    return gp, jnp.where(rows >= PAD_ROWS, log_sig / GATE_NORM, 0.0)


def _tri(strict):
    r = lax.broadcasted_iota(jnp.int32, (CHUNK, CHUNK), 0)
    c = lax.broadcasted_iota(jnp.int32, (CHUNK, CHUNK), 1)
    return jnp.where(c < r if strict else c <= r, 1.0, 0.0).astype(BF16)


def _tri_dot(tri, a):
    hi = a.astype(BF16)
    rest = a - hi.astype(F32)
    mid = rest.astype(BF16)
    lo = (rest - mid.astype(F32)).astype(BF16)
    return _dot(tri, hi) + _dot(tri, mid) + _dot(tri, lo)


def _gla_mid_fwd(h, gain, w_in_blocks, w_out, gate_w, gate_b, head_g, *, name):
    t = h.shape[0]
    nblk, rpb = w_in_blocks.shape[:2]
    dk = gate_b.shape[1]
    hv = head_g.shape[1]
    hk = dk // HEADS
    dv = hv * HEADS
    r_at = 2 * dk + 2 * dv
    assert nblk * rpb == r_at + GATE_RANK
    zw = r_at + GATE_PAD
    rt = _row_tile(t, 320)
    per = rt // CHUNK
    scale = hk ** -0.5

    def body(h_ref, g_ref, wb_ref, wo_ref, gw_ref, gb_ref, hg_ref, ho_ref, o_ref, st_ref, z_ref, u_ref, wi_ref,
             s_ref, la_ref, dec_ref):
        i = pl.program_id(0)

        @pl.when(i == 0)
        def _():
            s_ref[...] = jnp.zeros_like(s_ref)
            for b in range(nblk):
                wi_ref[b * rpb:(b + 1) * rpb, :] = wb_ref[b]
            wi_ref[nblk * rpb:, :] = jnp.zeros((zw - nblk * rpb, wi_ref.shape[1]), BF16)

        _norm_project(h_ref, g_ref, wi_ref, u_ref, z_ref)

        _, la = _log_decay(z_ref[:, r_at:r_at + GATE_PAD], gw_ref, gb_ref, _row_ids(i, rt))
        la_ref[...] = la
        tri = _tri(False)

        def chunk_rows(c):
            return slice(c * CHUNK, (c + 1) * CHUNK)

        def decays(c, carry):
            rows = chunk_rows(c)
            la_c = la_ref[rows, :]
            cum = _tri_dot(tri, la_c)
            dec_ref[rows, :] = jnp.exp(_rowsum(la_c) - cum)
            return carry

        def states(c, carry):
            rows = chunk_rows(c)
            etot = jnp.exp(_rowsum(la_ref[rows, :]))
            for hd in range(HEADS):
                ks = slice(hd * hk, (hd + 1) * hk)
                kd = z_ref[rows, dk + hd * hk:dk + (hd + 1) * hk] * dec_ref[rows, ks]
                v = z_ref[rows, 2 * dk + hd * hv:2 * dk + (hd + 1) * hv]
                s_new = s_ref[hd] * etot[:, ks] + _dot_tn(v.astype(BF16), kd.astype(BF16))
                s_ref[hd] = s_new
                st_ref[c, hd] = s_new
            return carry

        def outputs(c, carry):
            rows = chunk_rows(c)
            for hd in range(HEADS):
                q = z_ref[rows, hd * hk:(hd + 1) * hk] * scale
                g = z_ref[rows, 2 * dk + dv + hd * hv:2 * dk + dv + (hd + 1) * hv]
                o = _dot_nt(q.astype(BF16), st_ref[c, hd].astype(BF16))
                on = o * lax.rsqrt(jnp.mean(o * o, axis=-1, keepdims=True) + EPS) * hg_ref[...]
                o_ref[rows, hd * hv:(hd + 1) * hv] = (on * (g * _sigmoid(g))).astype(BF16)
            return carry

        for phase in (decays, states, outputs):
            for c in range(per):
                phase(c, 0)
        ho_ref[...] = h_ref[...] + _dot(o_ref[...], wo_ref[...])

    d = h.shape[1]
    rows = lambda i: (i, 0)
    return pl.pallas_call(
        body, grid=(t // rt,),
        in_specs=[pl.BlockSpec((rt, d), rows), _resident((1, d)), _resident(w_in_blocks.shape), _resident(w_out.shape),
                  _resident(gate_w.shape), _resident((1, dk)), _resident((1, hv))],
        out_specs=[pl.BlockSpec((rt, d), rows), pl.BlockSpec((rt, dv), rows),
                   pl.BlockSpec((per, HEADS, hv, hk), lambda i: (i, 0, 0, 0)), pl.BlockSpec((rt, zw), rows),
                   pl.BlockSpec((rt, d), rows), pl.BlockSpec((zw, d), lambda i: (0, 0))],
        out_shape=[_sds((t, d), F32), _sds((t, dv), BF16), _sds((t // CHUNK, HEADS, hv, hk), F32), _sds((t, zw), F32),
                   _sds((t, d), BF16), _sds((zw, d), BF16)],
        scratch_shapes=[pltpu.VMEM((HEADS, hv, hk), F32), pltpu.VMEM((rt, dk), F32), pltpu.VMEM((rt, dk), F32)],
        compiler_params=_cparams("arbitrary"), name=name)(h, gain, w_in_blocks, w_out, gate_w, gate_b, head_g)


def _gla_mid_bwd(z, h, gain, w_in_t, dh, w_out, states, gate_w, gate_b, head_g, *, after=None, name):
    t = z.shape[0]
    dk = gate_b.shape[1]
    hv = head_g.shape[1]
    hk = dk // HEADS
    dv = hv * HEADS
    r_at = 2 * dk + 2 * dv
    rt = _row_tile(t, 320)
    ntile = t // rt
    per = rt // CHUNK
    scale = hk ** -0.5
    dep_specs, deps = _dep_specs(after)

    def body(*refs):
        (z_ref, h_ref, g_ref, wi_ref, dh_ref, wo_ref, st_ref, stp_ref, gw_ref, gb_ref, hg_ref,
         dhi_ref, dg_ref, dz_ref, dgw_ref, dgb_ref, dhg_ref,
         ds_ref, la_ref, dla_ref, dec_ref, dos_ref, e_ref, do_ref) = refs[len(deps):]
        step = pl.program_id(0)
        tile = ntile - 1 - step
        do_ref[...] = _dot_nt(dh_ref[...].astype(BF16), wo_ref[...])

        @pl.when(step == 0)
        def _():
            ds_ref[...] = jnp.zeros_like(ds_ref)
            dgw_ref[...] = jnp.zeros_like(dgw_ref)
            dgb_ref[...] = jnp.zeros_like(dgb_ref)
            dhg_ref[...] = jnp.zeros_like(dhg_ref)

        rows_id = _row_ids(tile, rt)
        r = z_ref[:, r_at:r_at + GATE_PAD]
        gp, la = _log_decay(r, gw_ref, gb_ref, rows_id)
        la_ref[...] = la
        tri, tri_strict = _tri(False), _tri(True)
        keep = jnp.where(tile > 0, 1.0, 0.0)

        def chunk_rows(c):
            return slice(c * CHUNK, (c + 1) * CHUNK)

        def recompute(c, dhg):
            rows = chunk_rows(c)
            la_c = la_ref[rows, :]
            cum = _tri_dot(tri, la_c)
            dec_ref[rows, :] = jnp.exp(_rowsum(la_c) - cum)
            for hd in range(HEADS):
                q = (z_ref[rows, hd * hk:(hd + 1) * hk] * scale).astype(BF16)
                g = z_ref[rows, 2 * dk + dv + hd * hv:2 * dk + dv + (hd + 1) * hv]
                s_b = st_ref[c, hd].astype(BF16)
                o = _dot_nt(q, s_b)
                rstd = lax.rsqrt(jnp.mean(o * o, axis=-1, keepdims=True) + EPS)
                oh = o * rstd
                sg = _sigmoid(g)
                d_og = do_ref[rows, hd * hv:(hd + 1) * hv]
                dz_ref[rows, 2 * dk + dv + hd * hv:2 * dk + dv + (hd + 1) * hv] = (
                    d_og * oh * hg_ref[...] * (sg * (1.0 + g * (1.0 - sg)))).astype(BF16)
                don = d_og * (g * sg)
                dhg = dhg + _rowsum(don * oh)
                doh = don * hg_ref[...]
                d_o = (rstd * (doh - oh * jnp.mean(doh * oh, axis=-1, keepdims=True))).astype(BF16)
                dos_ref[rows, hd * hv:(hd + 1) * hv] = d_o
                dz_ref[rows, hd * hk:(hd + 1) * hk] = (_dot(d_o, s_b) * scale).astype(BF16)
            return dhg

        def recurrence(cc, carry):
            c = per - 1 - cc
            rows = chunk_rows(c)
            etot = jnp.exp(_rowsum(la_ref[rows, :]))
            for hd in range(HEADS):
                ks = slice(hd * hk, (hd + 1) * hk)
                q = (z_ref[rows, hd * hk:(hd + 1) * hk] * scale).astype(BF16)
                dec = dec_ref[rows, ks]
                kd = z_ref[rows, dk + hd * hk:dk + (hd + 1) * hk] * dec
                v = z_ref[rows, 2 * dk + hd * hv:2 * dk + (hd + 1) * hv].astype(BF16)
                s_prev = st_ref[c - 1, hd] if c > 0 else keep * stp_ref[0, hd]
                ds_t = ds_ref[hd] + _dot_tn(dos_ref[rows, hd * hv:(hd + 1) * hv], q)
                ds_b = ds_t.astype(BF16)
                dkd = _dot(v, ds_b)
                dz_ref[rows, 2 * dk + hd * hv:2 * dk + (hd + 1) * hv] = _dot_nt(kd.astype(BF16), ds_b).astype(BF16)
                dtot = etot[:, ks] * _rowsum(ds_t * s_prev)
                ds_ref[hd] = ds_t * etot[:, ks]
                dz_ref[rows, dk + hd * hk:dk + (hd + 1) * hk] = (dkd * dec).astype(BF16)
                e_ref[rows, ks] = dkd * kd
                dla_ref[rows, ks] = jnp.broadcast_to(dtot, (CHUNK, hk))
            return carry

        def decay_cotangent(c, carry):
            rows = chunk_rows(c)
            dla_ref[rows, :] += _tri_dot(tri_strict, e_ref[rows, :])
            return carry

        dhg = jnp.zeros((1, hv), F32)
        for c in range(per):
            dhg = recompute(c, dhg)
        dhg_ref[...] += dhg
        for phase in (recurrence, decay_cotangent):
            for c in range(per):
                phase(c, 0)
        dla = jnp.where(rows_id >= PAD_ROWS, dla_ref[...], 0.0)
        dgp = dla * (1.0 / GATE_NORM) * (1.0 - _sigmoid(gp))
        dgb_ref[...] += _rowsum(dgp)
        dgp_b = dgp.astype(BF16)
        dgw_ref[...] += _dot_tn(r.astype(BF16), dgp_b)
        dz_ref[:, r_at:r_at + GATE_PAD] = _dot_nt(dgp_b, gw_ref[...]).astype(BF16)
        _project_back(dz_ref, wi_ref, h_ref, g_ref, dh_ref, dhi_ref, dg_ref, step == 0)

    d = h.shape[1]
    back = lambda i: (ntile - 1 - i, 0)
    const2 = lambda i: (0, 0)
    return pl.pallas_call(
        body, grid=(ntile,),
        in_specs=dep_specs + [
                  pl.BlockSpec((rt, z.shape[1]), back), pl.BlockSpec((rt, d), back), _resident((1, d)),
                  _resident(w_in_t.shape), pl.BlockSpec((rt, d), back), _resident(w_out.shape),
                  pl.BlockSpec((per, HEADS, hv, hk), lambda i: (ntile - 1 - i, 0, 0, 0)),
                  pl.BlockSpec((1, HEADS, hv, hk), lambda i: (jnp.maximum((ntile - 1 - i) * per - 1, 0), 0, 0, 0)),
                  _resident(gate_w.shape), _resident((1, dk)), _resident((1, hv))],
        out_specs=[pl.BlockSpec((rt, d), back), pl.BlockSpec((1, d), const2), pl.BlockSpec((rt, z.shape[1]), back),
                   pl.BlockSpec(gate_w.shape, const2), pl.BlockSpec((1, dk), const2), pl.BlockSpec((1, hv), const2)],
        out_shape=[_sds((t, d), F32), _sds((1, d), F32), _sds(z.shape, BF16), _sds(gate_w.shape, F32),
                   _sds((1, dk), F32), _sds((1, hv), F32)],
        scratch_shapes=[pltpu.VMEM((HEADS, hv, hk), F32), pltpu.VMEM((rt, dk), F32), pltpu.VMEM((rt, dk), F32),
                        pltpu.VMEM((rt, dk), F32), pltpu.VMEM((rt, dv), BF16), pltpu.VMEM((rt, dk), F32),
                        pltpu.VMEM((rt, dv), F32)],
        compiler_params=_cparams("arbitrary"), name=name)(*deps, z, h, gain, w_in_t, dh, w_out, states, states, gate_w,
                                                          gate_b, head_g)


def _adamw_math(w, g, m, v):
    m = ADAM_B1 * m + (1.0 - ADAM_B1) * g
    v = ADAM_B2 * v + (1.0 - ADAM_B2) * (g * g)
    m_hat = m / (1.0 - ADAM_B1 ** ADAM_STEP)
    v_hat = v / (1.0 - ADAM_B2 ** ADAM_STEP)
    return -ADAM_LR * (m_hat / (jnp.sqrt(v_hat) + ADAM_EPS) + ADAM_WD * w), m, v


N_CHIP = N_DEV // 2
BLOCK_ELEMS = 128 * 1024


def _my_slot():
    return 4 * lax.axis_index("x") + 2 * lax.axis_index("y") + lax.axis_index("c")


def _row_block(r, c):
    cap = max(8, BLOCK_ELEMS // (-(-c // LANE) * LANE))
    return max([b for b in range(8, r + 1, 8) if r % b == 0 and b <= cap] or [r])


def _blocks(r, c):
    rb = _row_block(r, c)
    if rb < r or r * c <= BLOCK_ELEMS:
        return rb, c
    return r, max([b for b in (512, 256, LANE) if c % b == 0 and r * b <= BLOCK_ELEMS] or [c])


def _reduce_adam(parts, w, m, v, *, after=None, name):
    nl, r, c = w.shape
    rb, cb = _blocks(r, c)
    dep_specs, deps = _dep_specs(after)

    def body(*refs):
        me = refs[0][0]
        refs = refs[1 + len(deps):]
        p_refs = refs[:2 * nl]
        w_ref, m_ref, v_ref, g_out, d_out, m_out, v_out = refs[2 * nl:]
        layer = pl.program_id(0)
        for li in range(nl):
            @pl.when(layer == li)
            def _(li=li):
                own_ref, land_ref = p_refs[2 * li], p_refs[2 * li + 1]
                mine = own_ref[...].astype(F32)
                g = None
                for dev in range(N_DEV):
                    term = jnp.where(me == dev, mine, land_ref[dev].astype(F32))
                    g = term if g is None else g + term
                g_out[...] = g
                d_out[...], m_out[...], v_out[...] = _adamw_math(w_ref[...], g, m_ref[...], v_ref[...])

    blk = pl.BlockSpec((None, rb, cb), lambda l, i, j, me: (l, i, j))
    p_specs = []
    for li in range(nl):
        p_specs += [
            pl.BlockSpec((None, rb, cb), lambda l, i, j, me, li=li: (me[0], jnp.where(l == li, i, 0), jnp.where(l == li, j, 0))),
            pl.BlockSpec((N_DEV, rb, cb), lambda l, i, j, me, li=li: (0, jnp.where(l == li, i, 0), jnp.where(l == li, j, 0)))]
    flat = [p for pair in parts for p in pair]
    grid_spec = pltpu.PrefetchScalarGridSpec(
        num_scalar_prefetch=1, grid=(nl, r // rb, c // cb), in_specs=dep_specs + p_specs + [blk, blk, blk],
        out_specs=[blk] * 4)
    return pl.pallas_call(
        body, grid_spec=grid_spec, out_shape=[_sds(w.shape, F32)] * 4,
        compiler_params=_cparams("arbitrary", "arbitrary", "arbitrary"), name=name)(
        _my_slot().reshape(1), *deps, *flat, w, m, v)


def _sum8(own, landed, *, name):
    def body(own_ref, land_ref, o_ref):
        me = _my_slot()
        total = None
        for dev in range(N_DEV):
            term = jnp.where(me == dev, own_ref[...], land_ref[dev])
            total = term if total is None else total + term
        o_ref[...] = total

    return pl.pallas_call(body, out_shape=_sds(own.shape, F32), name=name)(own, landed)


def _adam_small(own, landed, split, w, m, v, *, name):
    n = len(w)

    def body(*refs):
        own_refs, land_refs, w_refs, m_refs, v_refs = (refs[k * n:(k + 1) * n] for k in range(5))
        outs = refs[5 * n:]
        me = _my_slot()
        for k in range(n):
            mine = own_refs[k][me] if split[k] else own_refs[k][...]
            g = None
            for dev in range(N_DEV):
                term = jnp.where(me == dev, mine, land_refs[k][dev])
                g = term if g is None else g + term
            outs[4 * k][...] = g
            outs[4 * k + 1][...], outs[4 * k + 2][...], outs[4 * k + 3][...] = _adamw_math(
                w_refs[k][...], g, m_refs[k][...], v_refs[k][...])

    out = pl.pallas_call(body, out_shape=[_sds(a.shape, F32) for a in w for _ in range(4)],
                         compiler_params=pltpu.CompilerParams(vmem_limit_bytes=V7X_VMEM_LIMIT), name=name)(
        *own, *landed, *w, *m, *v)
    return [tuple(out[4 * k:4 * k + 4]) for k in range(n)]


_HBM = pl.BlockSpec(memory_space=pltpu.HBM)
_SEM = pl.BlockSpec(memory_space=pltpu.SEMAPHORE)
_DATAFLOW = pltpu.SideEffectType.DATAFLOW_SIDE_EFFECTING


def _plan_to_all(src, land):
    x, y, c = lax.axis_index("x"), lax.axis_index("y"), lax.axis_index("c")
    return [(src, land.at[_my_slot()], (x ^ ((d >> 2) & 1), y ^ ((d >> 1) & 1), c ^ (d & 1))) for d in range(1, N_DEV)]


def _plan_split_to_all(src, land):
    x, y, c = lax.axis_index("x"), lax.axis_index("y"), lax.axis_index("c")
    peers = [(x ^ ((d >> 2) & 1), y ^ ((d >> 1) & 1), c ^ (d & 1)) for d in range(1, N_DEV)]
    return [(src.at[4 * px + 2 * py + pc], land.at[_my_slot()], (px, py, pc)) for px, py, pc in peers]


_PLAN_COPIES = {_plan_to_all: N_DEV - 1, _plan_split_to_all: N_DEV - 1}


def _plans(plan, n):
    return list(plan) if isinstance(plan, (list, tuple)) else [plan] * n


def _exchange_copies(plan, ins, lands, send, recv):
    copies, sem = [], 0
    for p, src, land in zip(_plans(plan, len(lands)), ins, lands):
        for s, dst, dev in p(src, land):
            copies.append(pltpu.make_async_remote_copy(
                src_ref=s, dst_ref=dst, send_sem=send.at[sem], recv_sem=recv.at[sem],
                device_id=dev, device_id_type=pl.DeviceIdType.MESH))
            sem += 1
    return copies


def _place_own(a, dtype, *, after=None, name):
    r, c = a.shape
    rb = _row_block(r, c)
    dep_specs, deps = _dep_specs(after)

    def body(*refs):
        a_ref, o_ref = refs[1 + len(deps):]
        o_ref[...] = a_ref[...].astype(dtype)

    grid_spec = pltpu.PrefetchScalarGridSpec(
        num_scalar_prefetch=1, grid=(r // rb,), in_specs=dep_specs + [pl.BlockSpec((rb, c), lambda i, me: (i, 0))],
        out_specs=pl.BlockSpec((None, rb, c), lambda i, me: (me[0], i, 0)))
    return pl.pallas_call(body, grid_spec=grid_spec, out_shape=_sds((N_DEV, r, c), dtype),
                          compiler_params=_cparams("arbitrary"), name=name)(_my_slot().reshape(1), *deps, a)


def _plan_gather_first(land, _):
    x, y, c = lax.axis_index("x"), lax.axis_index("y"), lax.axis_index("c")
    mine = land.at[_my_slot()]
    return [(mine, mine, (x, y, 1 - c))] + [(mine, mine, (x ^ (d >> 1), y ^ (d & 1), c)) for d in range(1, N_CHIP)]


def _plan_gather_relay(land, _):
    x, y, c = lax.axis_index("x"), lax.axis_index("y"), lax.axis_index("c")
    slots = [land.at[4 * (x ^ (d >> 1)) + 2 * (y ^ (d & 1)) + c] for d in range(1, N_CHIP)]
    return [(s, s, (x, y, 1 - c)) for s in slots]


def _plan_gather_direct(land, _):
    return _plan_to_all(land.at[_my_slot()], land)


_PLAN_COPIES[_plan_gather_first] = N_CHIP
_PLAN_COPIES[_plan_gather_relay] = N_CHIP - 1
_PLAN_COPIES[_plan_gather_direct] = N_DEV - 1


def _exchange_start(plan, arrs, lands, *, after=None, name):
    bufs = list(lands) if arrs is None else list(arrs) + list(lands)
    n, nb = len(lands), len(bufs)
    nsem = sum(_PLAN_COPIES[p] for p in _plans(plan, n))
    dep_specs, deps = _dep_specs(after)

    def body(*refs):
        ins, land_refs = refs[:n], refs[nb - n:nb]
        send, recv = refs[nb + len(deps)], refs[nb + len(deps) + 1]
        for cp in _exchange_copies(plan, ins, land_refs, send, recv):
            cp.start()
        refs[-1][...] = jnp.zeros_like(refs[-1])

    out = pl.pallas_call(
        body, name=name,
        out_shape=(pltpu.SemaphoreType.DMA((nsem,)), pltpu.SemaphoreType.DMA((nsem,)),
                   *[pltpu.HBM(a.shape, a.dtype) for a in bufs], _sds((8, LANE), F32)),
        in_specs=[_HBM] * nb + dep_specs,
        out_specs=(_SEM, _SEM, *([_HBM] * nb), pl.BlockSpec(memory_space=pltpu.VMEM)),
        input_output_aliases={i: 2 + i for i in range(nb)},
        compiler_params=pltpu.CompilerParams(has_side_effects=_DATAFLOW),
    )(*[pltpu.with_memory_space_constraint(a, pltpu.HBM) for a in bufs], *deps)
    return (plan, n, out[0], out[1], list(out[2:2 + nb])), out[-1]


def _exchange_now(plan, lands, *, name):
    n = len(lands)
    nsem = sum(_PLAN_COPIES[p] for p in _plans(plan, n))

    def body(*refs):
        land_refs, send, recv = refs[n:2 * n], refs[2 * n], refs[2 * n + 1]
        copies = _exchange_copies(plan, land_refs, land_refs, send, recv)
        for cp in copies:
            cp.start()
        for cp in copies:
            cp.wait_send()
            cp.wait_recv()

    hbm = pl.BlockSpec(memory_space=pl.ANY)
    return pl.pallas_call(
        body, in_specs=[hbm] * n, out_specs=[hbm] * n, out_shape=[_sds(a.shape, a.dtype) for a in lands],
        input_output_aliases={i: i for i in range(n)},
        scratch_shapes=[pltpu.SemaphoreType.DMA((nsem,)), pltpu.SemaphoreType.DMA((nsem,))], name=name)(*lands)


def _exchange_wait(state, after, *, name):
    plan, n, send_sem, recv_sem, bufs = state
    nb = len(bufs)
    after = list(after) if isinstance(after, (list, tuple)) else [after]

    def body(*refs):
        ins, land_refs, send, recv = refs[:n], refs[nb - n:nb], refs[nb], refs[nb + 1]
        for cp in _exchange_copies(plan, ins, land_refs, send, recv):
            cp.wait_send()
            cp.wait_recv()

    out = pl.pallas_call(
        body, name=name, out_shape=[pltpu.HBM(a.shape, a.dtype) for a in bufs],
        in_specs=[_HBM] * nb + [_SEM, _SEM] + [pl.BlockSpec(memory_space=pl.ANY)] * len(after), out_specs=[_HBM] * nb,
        input_output_aliases={i: i for i in range(nb)},
        compiler_params=pltpu.CompilerParams(has_side_effects=_DATAFLOW),
    )(*bufs, send_sem, recv_sem, *after)
    return list(out[:n]), list(out[nb - n:])


def _dep_specs(after):
    return ([], []) if after is None else ([pl.BlockSpec(memory_space=pl.ANY)], [after])


def _undo_column_split(g):
    return jnp.transpose(g, (1, 0, 2)).reshape(g.shape[1], N_DEV * g.shape[2])


def _column_split(a):
    r, c = a.shape
    return jnp.transpose(a.reshape(r, N_DEV, c // N_DEV), (1, 0, 2))


class _WholeWeights:
    def __init__(self, groups):
        self.groups = groups
        self.grads = {}

    def fetch(self, group, after):
        return self.groups[group]

    def emit(self, group, grads):
        self.grads.update(grads)
        return None


def _local_step(x, target, replicated, src):
    d = x.shape[1]
    mix_g, ffn_g = replicated["mix_g"], replicated["ffn_g"]
    cp = src.fetch("cp", [])
    cp_mid = (cp["conv_w"], replicated["conv_b"], replicated["ln_g"], replicated["ln_b"], replicated["pool_w"],
              replicated["pool_scale"])

    h1, cat, z0, u0, cv0, h0 = _cp_mid_fwd(x, cp["meta"], mix_g[0:1], cp["cp_w_in_t"], cp["cp_w_out"], *cp_mid,
                                           name="cp_mixer")
    ffn0 = src.fetch("ffn0", h1)
    h2, uf0, rf0 = _ffn_fwd(h1, ffn_g[0:1], ffn0["w1"], ffn0["w2"], name="ffn0")
    gla = src.fetch("gla", h2)
    gla_mid = (gla["gate_w"], gla["gate_b"], gla["head_g"])
    h3, og, states, z1, u1, gla_w_in_t = _gla_mid_fwd(h2, mix_g[1:2], gla["gla_w_in_t"], gla["gla_w_out"], *gla_mid,
                                                      name="gla_mixer")
    ffn1 = src.fetch("ffn1", h3)
    dh4, uf1, rf1, loss, d_final_g = _ffn_fwd(h3, ffn_g[1:2], ffn1["w1"], ffn1["w2"],
                                              loss_head=(replicated["final_g"], target), name="ffn1_loss")

    dh3, dhh1, dob1, dffn_g1 = _ffn_bwd_x(h3, dh4, ffn_g[1:2], rf1, ffn1["w1"], ffn1["w2"], name="ffn1_bwd_x")
    sent = src.emit("ffn1_w1", dict(w1=_linear_bwd_w(uf1, dhh1, column_blocks=N_DEV, name="ffn1_dw1")))
    sent = src.emit("ffn1_w2", dict(w2=_linear_bwd_w(rf1, dob1, square_x=True, after=sent, name="ffn1_dw2")))
    d_gla_w_out = _linear_bwd_w(og, dh3, name="gla_out_dw")
    dh2, dmix_g1, dz1, d_gate_w, d_gate_b, d_head_g = _gla_mid_bwd(
        z1, h2, mix_g[1:2], gla_w_in_t, dh3, gla["gla_w_out"], states, *gla_mid, after=sent, name="gla_mixer_bwd")
    d_gla_w_in_t = _linear_bwd_w(dz1, u1, row_blocks=gla["gla_w_in_t"].shape[:2], name="gla_in_dw")
    sent = src.emit("gla", dict(gla_w_in_t=d_gla_w_in_t, gla_w_out=d_gla_w_out))
    dh1, dhh0, dob0, dffn_g0 = _ffn_bwd_x(h1, dh2, ffn_g[0:1], rf0, ffn0["w1"], ffn0["w2"], after=sent, name="ffn0_bwd_x")
    sent = src.emit("ffn0_w1", dict(w1=_linear_bwd_w(uf0, dhh0, column_blocks=N_DEV, name="ffn0_dw1")))
    sent = src.emit("ffn0_w2", dict(w2=_linear_bwd_w(rf0, dob0, square_x=True, after=sent, name="ffn0_dw2")))
    d_cp_w_out = _linear_bwd_w(cat, dh1, after=sent, name="cp_out_dw")
    sent = src.emit("cp_out", dict(cp_w_out=d_cp_w_out))
    dx, dh0_first, dmix_g0, dz0, d_conv_w, d_conv_b, d_ln_g, d_ln_b, d_pool_w, d_pool_scale = _cp_mid_bwd(
        z0, cv0, h0, mix_g[0:1], cp["cp_w_in_t"], dh1, cp["cp_w_out"], *cp_mid, after=sent, name="cp_mixer_bwd")
    d_cp_w_in_t = _linear_bwd_w(dz0, u0, name="cp_in_dw")

    small = dict(
        mix_g=jnp.concatenate([dmix_g0, dmix_g1]), ffn_g=jnp.concatenate([dffn_g0, dffn_g1]), conv_b=d_conv_b, ln_g=d_ln_g,
        ln_b=d_ln_b, pool_w=d_pool_w, pool_scale=d_pool_scale, final_g=d_final_g, meta=dh0_first[PAD_ROWS:], conv_w=d_conv_w,
        gate_w=d_gate_w, gate_b=d_gate_b, head_g=d_head_g)
    src.emit("cp", dict(cp_w_in_t=d_cp_w_in_t, small=small, loss=loss))
    return loss, dx, small


_REPLICATED = ("mix_norm_g", "ffn_norm_g", "cp_conv_b", "cp_ln_g", "cp_ln_b", "cp_pool_w", "cp_pool_scale", "final_norm_g")
_SMALL_SHARDED = ("meta_tokens", "cp_conv_w", "gla_gate_w2", "gla_gate_b", "gla_head_g")
_NAMES = ("meta_tokens", "mix_norm_g", "ffn_norm_g", "ffn_w1", "ffn_w2", "cp_w_in", "cp_conv_w", "cp_conv_b", "cp_ln_g",
          "cp_ln_b", "cp_pool_w", "cp_pool_scale", "cp_w_out", "gla_w_in", "gla_gate_w2", "gla_gate_b", "gla_head_g",
          "gla_w_out", "final_norm_g")
_SMALL_GRADS = ("mix_g", "ffn_g", "conv_b", "ln_g", "ln_b", "pool_w", "pool_scale", "final_g", "meta", "conv_w", "gate_w",
                "gate_b", "head_g")
_GROUPS = ("cp", "ffn0", "gla", "ffn1")
_TWO_LEG_GATHERS = ("cp", "ffn0", "ffn1")


class _Exchanges:
    def __init__(self, w, d):
        self.d = d
        small = [w[n].reshape(w[n].shape[-2:]) for n in _SMALL_SHARDED]
        self.small_shard_shapes = [w[n].shape for n in _SMALL_SHARDED]
        shards = dict(
            cp=[(w["cp_w_in"][0].T, BF16), (w["cp_w_out"][0], BF16)] + [(a, F32) for a in small],
            ffn0=[(w["ffn_w1"][0], BF16), (w["ffn_w2"][0], BF16)],
            gla=[(w["gla_w_in"][0].T, BF16), (w["gla_w_out"][0], BF16)],
            ffn1=[(w["ffn_w1"][1], BF16), (w["ffn_w2"][1], BF16)])
        self.gathers = {}
        self.sent = {}
        token = None
        for group in _GROUPS:
            lands = [_place_own(a, dtype, after=token, name=f"place_w_{group}_{k}")
                     for k, (a, dtype) in enumerate(shards[group])]
            plan = _plan_gather_first if group in _TWO_LEG_GATHERS else _plan_gather_direct
            self.gathers[group], token = _exchange_start(plan, None, lands, after=token, name=f"start_w_{group}")
        self.token = token

    def fetch(self, group, after):
        d = self.d
        after = (list(after) if isinstance(after, (list, tuple)) else [after]) + [self.token]
        _, got = _exchange_wait(self.gathers[group], after, name=f"wait_w_{group}")
        if group in _TWO_LEG_GATHERS:
            got = _exchange_now(_plan_gather_relay, got, name=f"relay_w_{group}")
        if group in ("ffn0", "ffn1"):
            return dict(w1=got[0], w2=got[1])
        if group == "gla":
            return dict(gla_w_in_t=got[0], gla_w_out=got[1].reshape(d, d), gate_w=self.gate_w, gate_b=self.gate_b,
                        head_g=self.head_g)
        meta, conv_w, gate_w, self.gate_b, self.head_g = [_undo_column_split(a) for a in got[2:]]
        self.gate_w = jnp.pad(gate_w, ((0, GATE_PAD - GATE_RANK), (0, 0))).astype(BF16)
        return dict(cp_w_in_t=got[0].reshape(-1, d), cp_w_out=got[1].reshape(d, d), meta=meta,
                    conv_w=jnp.pad(conv_w, ((0, 1), (0, 0))))

    def emit(self, group, g):
        d = self.d
        if group in ("ffn0_w1", "ffn1_w1"):
            arrs = [g["w1"]]
        elif group in ("ffn0_w2", "ffn1_w2"):
            arrs = [g["w2"].reshape(N_DEV, -1, d)]
        elif group == "gla":
            arrs = [g["gla_w_in_t"], g["gla_w_out"].reshape(N_DEV, d // N_DEV, d)]
        elif group == "cp_out":
            arrs = [g["cp_w_out"].reshape(N_DEV, d // N_DEV, d)]
        else:
            s = dict(g["small"])
            s.update(pool_w=s["pool_w"][None], conv_w=s["conv_w"][:CONV_WIDTH], gate_w=s["gate_w"][:GATE_RANK])
            own = [s[n] for n in _SMALL_GRADS[:len(_REPLICATED)]]
            own += [_column_split(s[n]).reshape((N_DEV,) + shape)
                    for n, shape in zip(_SMALL_GRADS[len(_REPLICATED):], self.small_shard_shapes)]
            plans = [_plan_to_all] * len(_REPLICATED) + [_plan_split_to_all] * len(_SMALL_SHARDED)
            lands = [lax.empty((N_DEV,) + a.shape, F32) for a in own[:len(_REPLICATED)]]
            lands += [lax.empty(a.shape, F32) for a in own[len(_REPLICATED):]]
            own.append(g["loss"])
            plans.append(_plan_to_all)
            lands.append(lax.empty((N_DEV,) + g["loss"].shape, F32))
            self.small_sent, self.token = _exchange_start(plans, own, lands, after=self.token, name="start_g_small")
            arrs = [g["cp_w_in_t"].reshape(N_DEV, -1, d)]
        self.sent[group], self.token = _exchange_start(_plan_split_to_all, arrs, [lax.empty(a.shape, a.dtype) for a in arrs],
                                                       after=self.token, name=f"start_g_{group}")
        return self.token

    def finish(self, w, mom, var):
        out = {}
        after = self.token

        def landed(group):
            own, got = _exchange_wait(self.sent[group], after, name=f"wait_g_{group}")
            return list(zip(own, got))

        def adam(n, parts, behind=None, transposed=False):
            flip = (lambda a: jnp.transpose(a, (0, 2, 1))) if transposed else (lambda a: a)
            res = _reduce_adam(parts, flip(w[n]), flip(mom[n]), flip(var[n]), after=behind, name=f"adam_{n}")
            out[n] = tuple(flip(a) for a in res)
            return res[0]

        ffn1_w1 = landed("ffn1_w1")
        after = ffn1_w1[0][1]
        ffn1_w2 = landed("ffn1_w2")
        after = ffn1_w2[0][1]
        gla = landed("gla")
        after = adam("gla_w_in", [gla[0]], transposed=True)
        after = adam("gla_w_out", [gla[1]], after)
        ffn0_w1 = landed("ffn0_w1")
        after = adam("ffn_w1", [ffn0_w1[0], ffn1_w1[0]])
        ffn0_w2 = landed("ffn0_w2")
        after = adam("ffn_w2", [ffn0_w2[0], ffn1_w2[0]])
        small_own, small_landed = _exchange_wait(self.small_sent, after, name="wait_g_small")
        names = _REPLICATED + _SMALL_SHARDED
        split = [False] * len(_REPLICATED) + [True] * len(_SMALL_SHARDED)
        small_new = _adam_small(small_own[:-1], small_landed[:-1], split, [w[n] for n in names], [mom[n] for n in names],
                                [var[n] for n in names], name="adam_small")
        out.update(zip(names, small_new))
        out["loss"] = _sum8(small_own[-1], small_landed[-1], name="sum_loss")[0, 0]

        after = small_new[0][0]
        cp_out = landed("cp_out")
        after = adam("cp_w_out", [cp_out[0]])
        cp = landed("cp")
        adam("cp_w_in", [cp[0]], transposed=True)
        return out


def kernel(x, meta_tokens, mix_norm_g, ffn_norm_g, ffn_w1, ffn_w2, cp_w_in, cp_conv_w, cp_conv_b, cp_ln_g, cp_ln_b, cp_pool_w, cp_pool_scale, cp_w_out, gla_w_in, gla_gate_w2, gla_gate_b, gla_head_g, gla_w_out, final_norm_g, loss_target, m_meta_tokens, m_mix_norm_g, m_ffn_norm_g, m_ffn_w1, m_ffn_w2, m_cp_w_in, m_cp_conv_w, m_cp_conv_b, m_cp_ln_g, m_cp_ln_b, m_cp_pool_w, m_cp_pool_scale, m_cp_w_out, m_gla_w_in, m_gla_gate_w2, m_gla_gate_b, m_gla_head_g, m_gla_w_out, m_final_norm_g, v_meta_tokens, v_mix_norm_g, v_ffn_norm_g, v_ffn_w1, v_ffn_w2, v_cp_w_in, v_cp_conv_w, v_cp_conv_b, v_cp_ln_g, v_cp_ln_b, v_cp_pool_w, v_cp_pool_scale, v_cp_w_out, v_gla_w_in, v_gla_gate_w2, v_gla_gate_b, v_gla_head_g, v_gla_w_out, v_final_norm_g):
    w = dict(meta_tokens=meta_tokens, mix_norm_g=mix_norm_g, ffn_norm_g=ffn_norm_g, ffn_w1=ffn_w1, ffn_w2=ffn_w2,
             cp_w_in=cp_w_in, cp_conv_w=cp_conv_w, cp_conv_b=cp_conv_b, cp_ln_g=cp_ln_g, cp_ln_b=cp_ln_b,
             cp_pool_w=cp_pool_w, cp_pool_scale=cp_pool_scale, cp_w_out=cp_w_out, gla_w_in=gla_w_in,
             gla_gate_w2=gla_gate_w2, gla_gate_b=gla_gate_b, gla_head_g=gla_head_g, gla_w_out=gla_w_out,
             final_norm_g=final_norm_g.reshape(1, -1))
    mom = dict(meta_tokens=m_meta_tokens, mix_norm_g=m_mix_norm_g, ffn_norm_g=m_ffn_norm_g, ffn_w1=m_ffn_w1, ffn_w2=m_ffn_w2,
               cp_w_in=m_cp_w_in, cp_conv_w=m_cp_conv_w, cp_conv_b=m_cp_conv_b, cp_ln_g=m_cp_ln_g, cp_ln_b=m_cp_ln_b,
               cp_pool_w=m_cp_pool_w, cp_pool_scale=m_cp_pool_scale, cp_w_out=m_cp_w_out, gla_w_in=m_gla_w_in,
               gla_gate_w2=m_gla_gate_w2, gla_gate_b=m_gla_gate_b, gla_head_g=m_gla_head_g, gla_w_out=m_gla_w_out,
               final_norm_g=m_final_norm_g.reshape(1, -1))
    var = dict(meta_tokens=v_meta_tokens, mix_norm_g=v_mix_norm_g, ffn_norm_g=v_ffn_norm_g, ffn_w1=v_ffn_w1, ffn_w2=v_ffn_w2,
               cp_w_in=v_cp_w_in, cp_conv_w=v_cp_conv_w, cp_conv_b=v_cp_conv_b, cp_ln_g=v_cp_ln_g, cp_ln_b=v_cp_ln_b,
               cp_pool_w=v_cp_pool_w, cp_pool_scale=v_cp_pool_scale, cp_w_out=v_cp_w_out, gla_w_in=v_gla_w_in,
               gla_gate_w2=v_gla_gate_w2, gla_gate_b=v_gla_gate_b, gla_head_g=v_gla_head_g, gla_w_out=v_gla_w_out,
               final_norm_g=v_final_norm_g.reshape(1, -1))
    d = x.shape[-1]
    replicated = dict(mix_g=w["mix_norm_g"], ffn_g=w["ffn_norm_g"], conv_b=w["cp_conv_b"], ln_g=w["cp_ln_g"],
                      ln_b=w["cp_ln_b"], pool_w=w["cp_pool_w"][0].astype(BF16), pool_scale=w["cp_pool_scale"],
                      final_g=w["final_norm_g"])
    exchanges = _Exchanges(w, d)
    _, grad_x, _ = _local_step(x[0], loss_target[0], replicated, exchanges)
    out = exchanges.finish(w, mom, var)
    loss = out.pop("loss")

    def leaf(n, k):
        a = out[n][k]
        return a.reshape(-1) if n == "final_norm_g" else a

    return (loss, grad_x[None], *[leaf(n, 0) for n in _NAMES], *[leaf(n, 1) for n in _NAMES],
            *[leaf(n, 2) for n in _NAMES], *[leaf(n, 3) for n in _NAMES])
```

```python
import functools

import jax
import jax.numpy as jnp
from jax import lax
from jax.experimental import pallas as pl
from jax.experimental.pallas import tpu as pltpu

F32, BF16 = jnp.float32, jnp.bfloat16
N_DEV = 8
CHUNK = 64
N_META = 16
PAD_ROWS = CHUNK - N_META
HALO = 32
EPS = 1e-5
CONV_WIDTH = 31
POOL_WINDOWS = (2, 4, 8, 16)
HEADS = 4
GATE_RANK = 16
GATE_NORM = 16.0
GATE_PAD = 128
ADAM_LR, ADAM_B1, ADAM_B2, ADAM_EPS, ADAM_WD, ADAM_STEP = 0.001, 0.9, 0.999, 1e-08, 0.01, 10
V7X_VMEM_LIMIT = 56 * 2 ** 20
LANE = 128


def _cparams(*sem):
    return pltpu.CompilerParams(dimension_semantics=sem, vmem_limit_bytes=V7X_VMEM_LIMIT)


def _row_tile(t, cap):
    best = CHUNK
    for r in range(CHUNK, min(t, cap) + 1, CHUNK):
        if t % r == 0:
            best = r
    return best


def _resident(shape):
    return pl.BlockSpec(shape, lambda *_: (0,) * len(shape), pipeline_mode=pl.Buffered(1))


def _dot(a, b):
    return jnp.dot(a, b, preferred_element_type=F32)


def _dot_nt(a, b):
    return lax.dot_general(a, b, (((1,), (1,)), ((), ())), preferred_element_type=F32)


def _dot_tn(a, b):
    return lax.dot_general(a, b, (((0,), (0,)), ((), ())), preferred_element_type=F32)


def _rowsum(a):
    return jnp.sum(a, axis=0, keepdims=True)


def _sigmoid(a):
    return 1.0 / (1.0 + jnp.exp(-a))


def _row_ids(tile, rt):
    return tile * rt + lax.broadcasted_iota(jnp.int32, (rt, 1), 0)


def _sds(shape, dtype):
    return jax.ShapeDtypeStruct(shape, dtype)


DW_ROWS = 1024


def _linear_bwd_w(x, dy, *, square_x=False, column_blocks=None, row_blocks=None, after=None, name):
    t, k = x.shape
    n = dy.shape[1]
    cut_k = k > n and column_blocks is None
    assert cut_k or row_blocks is None
    width = k if cut_k else n
    blk = n // column_blocks if column_blocks else max(c for c in (640, 512, 384, 256, LANE) if width % c == 0)
    dep_specs, deps = _dep_specs(after)

    def body(*refs):
        x_ref, dy_ref, o_ref, acc = refs[len(deps):]
        for c0 in range(0, t, DW_ROWS):
            rows = slice(c0, min(c0 + DW_ROWS, t))
            xv = x_ref[rows, :]
            if square_x:
                xv = xv.astype(F32)
                xv = xv * xv
            part = _dot_tn(xv.astype(BF16), dy_ref[rows, :].astype(BF16))
            if c0 == 0:
                acc[...] = part
            else:
                acc[...] += part
        if row_blocks is None:
            o_ref[...] = acc[...].astype(BF16)
            return
        nb, rpb = row_blocks
        for s in range(width // blk):
            @pl.when(pl.program_id(0) == s)
            def _(s=s):
                for b in range(nb):
                    lo, hi = max(s * blk, b * rpb), min((s + 1) * blk, (b + 1) * rpb)
                    if lo < hi:
                        o_ref[b, lo - b * rpb:hi - b * rpb, :] = acc[lo - s * blk:hi - s * blk, :].astype(BF16)

    out_shape = _sds((k, n), BF16)
    semantics = "parallel"
    if cut_k:
        in_specs = [pl.BlockSpec((t, blk), lambda j: (0, j)), _resident((t, n))]
        out_specs = pl.BlockSpec((blk, n), lambda j: (j, 0))
        acc_shape = (blk, n)
        if row_blocks:
            out_shape = _sds(row_blocks + (n,), BF16)
            out_specs = pl.BlockSpec(out_shape.shape, lambda j: (0, 0, 0))
            semantics = "arbitrary"
    else:
        in_specs = [_resident((t, k)), pl.BlockSpec((t, blk), lambda j: (0, j))]
        out_specs = pl.BlockSpec((k, blk), lambda j: (0, j))
        acc_shape = (k, blk)
        if column_blocks:
            out_specs = pl.BlockSpec((None, k, blk), lambda j: (j, 0, 0))
            out_shape = _sds((column_blocks, k, blk), BF16)
    return pl.pallas_call(
        body, grid=(width // blk,), in_specs=dep_specs + in_specs, out_specs=out_specs, out_shape=out_shape,
        scratch_shapes=[pltpu.VMEM(acc_shape, F32)], compiler_params=_cparams(semantics), name=name)(*deps, x, dy)


FFN_BLOCKS_PER_STEP = 2


def _ffn_fwd(h, gain, w1g, w2g, *, loss_head=None, name):
    t, d = h.shape
    f8 = w1g.shape[-1]
    rt = _row_tile(t, 832)
    nb = FFN_BLOCKS_PER_STEP
    nstep = N_DEV // nb

    def body(*refs):
        if loss_head is None:
            h_ref, g_ref, w1_ref, w2_ref, o_ref, u_ref, r_ref, acc_ref = refs
        else:
            (h_ref, g_ref, w1_ref, w2_ref, fg_ref, tgt_ref, o_ref, u_ref, r_ref, loss_ref, dfg_ref, acc_ref, t_ref,
             t_sem) = refs
        i, j = pl.program_id(0), pl.program_id(1)

        def target_rows(act):
            @pl.when(i == 0)
            def _():
                act(pltpu.make_async_copy(tgt_ref.at[pl.ds(0, rt - CHUNK)], t_ref.at[pl.ds(CHUNK, rt - CHUNK)], t_sem.at[0]))

            if t > rt:
                @pl.when(i > 0)
                def _():
                    act(pltpu.make_async_copy(tgt_ref.at[pl.ds(pl.multiple_of(i * rt - CHUNK, CHUNK), rt)], t_ref,
                                              t_sem.at[0]))

        @pl.when(j == 0)
        def _():
            if loss_head is not None:
                @pl.when(i == 0)
                def _():
                    t_ref[0:CHUNK, :] = jnp.zeros((CHUNK, d), F32)

                target_rows(lambda copy: copy.start())

            hv = h_ref[...]
            u_ref[...] = (hv * lax.rsqrt(jnp.mean(hv * hv, axis=-1, keepdims=True) + EPS) * g_ref[...]).astype(BF16)
            acc_ref[...] = jnp.zeros_like(acc_ref)

        part = None
        for b in range(nb):
            a = jnp.maximum(_dot(u_ref[...], w1_ref[b]), 0.0)
            r_ref[:, b * f8:(b + 1) * f8] = a.astype(BF16)
            term = _dot((a * a).astype(BF16), w2_ref[b])
            part = term if part is None else part + term
        acc_ref[...] += part

        @pl.when(j == nstep - 1)
        def _():
            y = h_ref[...] + acc_ref[...]
            if loss_head is None:
                o_ref[...] = y
                return

            @pl.when(i == 0)
            def _():
                loss_ref[...] = jnp.zeros_like(loss_ref)
                dfg_ref[...] = jnp.zeros_like(dfg_ref)

            target_rows(lambda copy: copy.wait())

            rstd = lax.rsqrt(jnp.mean(y * y, axis=-1, keepdims=True) + EPS)
            xh = y * rstd
            err = jnp.where(_row_ids(i, rt) >= CHUNK, xh * fg_ref[...] - t_ref[...], 0.0)
            loss_ref[...] += (0.5 / d) * jnp.sum(err * err)
            dy = err * (1.0 / d)
            dfg_ref[...] += _rowsum(dy * xh)
            dxh = dy * fg_ref[...]
            o_ref[...] = rstd * (dxh - xh * jnp.mean(dxh * xh, axis=-1, keepdims=True))

    rows = lambda i, j: (i, 0)
    in_specs = [pl.BlockSpec((rt, d), rows), _resident((1, d)),
                pl.BlockSpec((nb, d, f8), lambda i, j: (j, 0, 0)), pl.BlockSpec((nb, f8, d), lambda i, j: (j, 0, 0))]
    out_specs = [pl.BlockSpec((rt, d), rows), pl.BlockSpec((rt, d), rows), pl.BlockSpec((rt, nb * f8), lambda i, j: (i, j))]
    out_shape = [_sds((t, d), F32), _sds((t, d), BF16), _sds((t, N_DEV * f8), BF16)]
    args = [h, gain, w1g, w2g]
    scratch_shapes = [pltpu.VMEM((rt, d), F32)]
    if loss_head is not None:
        in_specs += [_resident((1, d)), pl.BlockSpec(memory_space=pl.ANY)]
        out_specs += [pl.BlockSpec((8, LANE), lambda i, j: (0, 0)), pl.BlockSpec((1, d), lambda i, j: (0, 0))]
        out_shape += [_sds((8, LANE), F32), _sds((1, d), F32)]
        args += list(loss_head)
        scratch_shapes += [pltpu.VMEM((rt, d), F32), pltpu.SemaphoreType.DMA((1,))]
    return pl.pallas_call(
        body, grid=(t // rt, nstep), in_specs=in_specs, out_specs=out_specs, out_shape=out_shape,
        scratch_shapes=scratch_shapes,
        compiler_params=_cparams("arbitrary" if loss_head is not None else "parallel", "arbitrary"), name=name)(*args)


def _ffn_bwd_x(h, dout, gain, r, w1g, w2g, *, after=None, name):
    t, d = h.shape
    f8 = w1g.shape[-1]
    rt = _row_tile(t, 832)
    nb = FFN_BLOCKS_PER_STEP
    last = N_DEV // nb - 1
    dep_specs, deps = _dep_specs(after)

    def body(*refs):
        h_ref, do_ref, g_ref, r_ref, w1_ref, w2_ref, dh_ref, dhh_ref, dob_ref, dg_ref, du_ref = refs[len(deps):]
        i, j = pl.program_id(0), pl.program_id(1)

        @pl.when(j == 0)
        def _():
            dob_ref[...] = do_ref[...].astype(BF16)
            du_ref[...] = jnp.zeros_like(du_ref)

        part = None
        for b in range(nb):
            cols = slice(b * f8, (b + 1) * f8)
            dhh = (_dot_nt(dob_ref[...], w2_ref[b]) * (2.0 * r_ref[:, cols].astype(F32))).astype(BF16)
            dhh_ref[:, cols] = dhh
            term = _dot_nt(dhh, w1_ref[b])
            part = term if part is None else part + term
        du_ref[...] += part

        @pl.when(j == last)
        def _():
            @pl.when(i == 0)
            def _():
                dg_ref[...] = jnp.zeros_like(dg_ref)

            hv = h_ref[...]
            rstd = lax.rsqrt(jnp.mean(hv * hv, axis=-1, keepdims=True) + EPS)
            xh = hv * rstd
            du = du_ref[...]
            dg_ref[...] += _rowsum(du * xh)
            dxh = du * g_ref[...]
            dh_ref[...] = do_ref[...] + rstd * (dxh - xh * jnp.mean(dxh * xh, axis=-1, keepdims=True))

    rows = lambda i, j: (i, 0)
    return pl.pallas_call(
        body, grid=(t // rt, N_DEV // nb),
        in_specs=dep_specs + [
                  pl.BlockSpec((rt, d), rows), pl.BlockSpec((rt, d), rows), _resident((1, d)),
                  pl.BlockSpec((rt, nb * f8), lambda i, j: (i, j)),
                  pl.BlockSpec((nb, d, f8), lambda i, j: (j, 0, 0)),
                  pl.BlockSpec((nb, f8, d), lambda i, j: (j, 0, 0))],
        out_specs=[pl.BlockSpec((rt, d), rows), pl.BlockSpec((rt, nb * f8), lambda i, j: (i, j)),
                   pl.BlockSpec((rt, d), rows), pl.BlockSpec((1, d), lambda i, j: (0, 0))],
        out_shape=[_sds((t, d), F32), _sds((t, N_DEV * f8), BF16), _sds((t, d), BF16), _sds((1, d), F32)],
        scratch_shapes=[pltpu.VMEM((rt, d), F32)],
        compiler_params=_cparams("arbitrary", "arbitrary"), name=name)(*deps, h, dout, gain, r, w1g, w2g)


def _lane_blocks(width):
    lb = min(LANE, width)
    return [slice(s, s + lb) for s in range(0, width, lb)]


def _conv_rows(src_ref, w_ref, offset, dst_ref, nblk, width, bias_ref=None):
    def blk(rb, carry):
        base = pl.multiple_of(rb * CHUNK, CHUNK)
        for l, ls in enumerate(_lane_blocks(width)):
            acc = jnp.zeros((CHUNK, ls.stop - ls.start), F32)
            if bias_ref is not None:
                acc = acc + bias_ref[:, ls]
            for k in range(CONV_WIDTH):
                acc = acc + w_ref[k:k + 1, ls] * src_ref[l, pl.ds(base + offset(k), CHUNK), :]
            dst_ref[l, pl.ds(base, CHUNK), :] = acc
        return carry

    lax.fori_loop(0, nblk, blk, 0)


def _to_lane_blocks(ref, row0, value):
    for l, ls in enumerate(_lane_blocks(value.shape[1])):
        ref[l, row0:row0 + value.shape[0], :] = value[:, ls]


def _from_lane_blocks(ref):
    return jnp.concatenate([ref[l] for l in range(ref.shape[0])], axis=1)


def _pool_counts(rows, window):
    return jnp.clip(rows - PAD_ROWS + 1, 1, window).astype(F32)


def _trailing_sum(v, window):
    s, sh = v, 1
    while sh < window:
        s = s + pltpu.roll(s, sh, 0)
        sh *= 2
    return s


def _leading_sum(v, window):
    s, sh, n = v, 1, v.shape[0]
    while sh < window:
        s = s + pltpu.roll(s, n - sh, 0)
        sh *= 2
    return s


def _norm_project(h_ref, g_ref, w_t_ref, u_ref, z_ref):
    hv = h_ref[...]
    u = (hv * lax.rsqrt(jnp.mean(hv * hv, axis=-1, keepdims=True) + EPS) * g_ref[...]).astype(BF16)
    u_ref[...] = u
    z_ref[...] = _dot_nt(u, w_t_ref[...])


def _project_back(dz_ref, w_t_ref, h_ref, g_ref, dres_ref, dh_ref, dg_ref, first):
    dx = _dot(dz_ref[...], w_t_ref[...])
    hv = h_ref[...]
    rstd = lax.rsqrt(jnp.mean(hv * hv, axis=-1, keepdims=True) + EPS)
    xh = hv * rstd

    @pl.when(first)
    def _():
        dg_ref[...] = jnp.zeros_like(dg_ref)

    dg_ref[...] += _rowsum(dx * xh)
    dxh = dx * g_ref[...]
    dh_ref[...] = dres_ref[...] + rstd * (dxh - xh * jnp.mean(dxh * xh, axis=-1, keepdims=True))


def _cp_mid_fwd(x, meta, gain, w_in_t, w_out, conv_w, conv_b, ln_g, ln_b, pool_w, pool_scale, *, name):
    seq, d = x.shape
    t = seq + CHUNK
    ein = w_in_t.shape[0]
    cd = conv_b.shape[1]
    pd = pool_scale.shape[1]
    pg = pd // len(POOL_WINDOWS)
    rt = _row_tile(t, 320)
    ntile = t // rt

    def body(x_ref, meta_ref, g_ref, wi_ref, wo_ref, cw_ref, cb_ref, lg_ref, lb_ref, pw_ref, ps_ref,
             ho_ref, o_ref, z_ref, u_ref, cv_ref, h0_ref, gext, pext, conv_s, hbuf, hsem):
        i = pl.program_id(0)
        slot = i % 2
        first_rows = pltpu.make_async_copy(x_ref.at[pl.ds(0, rt - CHUNK)], hbuf.at[0, pl.ds(CHUNK, rt - CHUNK)], hsem.at[0])

        def tile_rows(tile, to):
            return pltpu.make_async_copy(x_ref.at[pl.ds(pl.multiple_of(tile * rt - CHUNK, CHUNK), rt)], hbuf.at[to],
                                         hsem.at[to])

        @pl.when(i == 0)
        def _():
            first_rows.start()
            hbuf[0, 0:PAD_ROWS, :] = jnp.zeros((PAD_ROWS, d), F32)
            hbuf[0, PAD_ROWS:CHUNK, :] = meta_ref[...]
            _to_lane_blocks(gext, 0, jnp.zeros((HALO, cd), F32))
            pext[0:HALO, :] = jnp.zeros((HALO, pd), F32)

        @pl.when(i + 1 < ntile)
        def _():
            tile_rows(i + 1, 1 - slot).start()

        @pl.when(i == 0)
        def _():
            first_rows.wait()

        @pl.when(i > 0)
        def _():
            tile_rows(i, slot).wait()

        h_ref = hbuf.at[slot]
        h0_ref[...] = h_ref[...]
        _norm_project(h_ref, g_ref, wi_ref, u_ref, z_ref)

        _to_lane_blocks(gext, HALO, z_ref[:, 0:cd] * _sigmoid(z_ref[:, cd:2 * cd]))
        pext[HALO:HALO + rt, :] = z_ref[:, 2 * cd:]
        _conv_rows(gext, cw_ref, lambda k: k + HALO - (CONV_WIDTH - 1), conv_s, rt // CHUNK, cd, cb_ref)
        cv = _from_lane_blocks(conv_s)
        cv_ref[...] = cv
        xc = cv - jnp.mean(cv, axis=-1, keepdims=True)
        y = xc * lax.rsqrt(jnp.mean(xc * xc, axis=-1, keepdims=True) + EPS) * lg_ref[...] + lb_ref[...]
        rows = _row_ids(i, rt)
        a = jnp.where(rows >= PAD_ROWS, y * _sigmoid(y), 0.0)
        o_ref[:, 0:cd] = a.astype(BF16)
        for gi, window in enumerate(POOL_WINDOWS):
            ls = slice(gi * pg, (gi + 1) * pg)
            v = pext[:, ls]
            tm = _trailing_sum(v, window)[HALO:] / _pool_counts(rows, window) - v[HALO:]
            p = _dot(tm.astype(BF16), pw_ref[gi]) * ps_ref[:, ls]
            o_ref[:, cd + gi * pg:cd + (gi + 1) * pg] = p.astype(BF16)
        ho_ref[...] = h_ref[...] + _dot(o_ref[...], wo_ref[...])
        gext[:, 0:HALO, :] = gext[:, rt:rt + HALO, :]
        pext[0:HALO, :] = pext[rt:rt + HALO, :]

    nl, lb = len(_lane_blocks(cd)), min(LANE, cd)
    rows = lambda i: (i, 0)
    return pl.pallas_call(
        body, grid=(ntile,),
        in_specs=[pl.BlockSpec(memory_space=pl.ANY), _resident(meta.shape), _resident((1, d)), _resident(w_in_t.shape),
                  _resident(w_out.shape), _resident(conv_w.shape), _resident((1, cd)),
                  _resident((1, cd)), _resident((1, cd)), _resident(pool_w.shape), _resident((1, pd))],
        out_specs=[pl.BlockSpec((rt, d), rows), pl.BlockSpec((rt, cd + pd), rows), pl.BlockSpec((rt, ein), rows),
                   pl.BlockSpec((rt, d), rows), pl.BlockSpec((rt, cd), rows), pl.BlockSpec((rt, d), rows)],
        out_shape=[_sds((t, d), F32), _sds((t, cd + pd), BF16), _sds((t, ein), F32), _sds((t, d), BF16),
                   _sds((t, cd), F32), _sds((t, d), F32)],
        scratch_shapes=[pltpu.VMEM((nl, rt + HALO, lb), F32), pltpu.VMEM((rt + HALO, pd), F32),
                        pltpu.VMEM((nl, rt, lb), F32), pltpu.VMEM((2, rt, d), F32), pltpu.SemaphoreType.DMA((2,))],
        compiler_params=_cparams("arbitrary"), name=name)(x, meta, gain, w_in_t, w_out, conv_w, conv_b, ln_g, ln_b, pool_w,
                                                          pool_scale)


def _cp_mid_bwd(z, cv, h, gain, w_in_t, dh, w_out, conv_w, conv_b, ln_g, ln_b, pool_w, pool_scale, *, after=None, name):
    t, ein = z.shape
    cd = conv_b.shape[1]
    pd = pool_scale.shape[1]
    pg = pd // len(POOL_WINDOWS)
    rt = _row_tile(t, 320)
    ntile = t // rt
    per = rt // CHUNK
    dep_specs, deps = _dep_specs(after)

    def body(*refs):
        (z_ref, zh_ref, cv_ref, h_ref, g_ref, wi_ref, dh_ref, wo_ref, cw_ref, cb_ref, lg_ref, lb_ref, pw_ref, ps_ref,
         dx_ref, dfirst_ref, dg_ref, dz_ref, dcw_ref, dcb_ref, dlg_ref, dlb_ref, dpw_ref, dps_ref,
         gext, pext, conv_s, dcv, dsp, dhi, dx_sem) = refs[len(deps):]
        step = pl.program_id(0)
        tile = ntile - 1 - step
        first_rows = pltpu.make_async_copy(dhi.at[pl.ds(CHUNK, rt - CHUNK)], dx_ref.at[pl.ds(0, rt - CHUNK)], dx_sem.at[0])

        def tile_rows(tile):
            return pltpu.make_async_copy(dhi, dx_ref.at[pl.ds(pl.multiple_of(tile * rt - CHUNK, CHUNK), rt)], dx_sem.at[0])

        dcat = _dot_nt(dh_ref[...].astype(BF16), wo_ref[...])

        @pl.when(step == 0)
        def _():
            for ref in (dcw_ref, dcb_ref, dlg_ref, dlb_ref, dpw_ref, dps_ref):
                ref[...] = jnp.zeros_like(ref)
            _to_lane_blocks(dcv, rt, jnp.zeros((HALO, cd), F32))
            dsp[rt:rt + HALO, :] = jnp.zeros((HALO, pd), F32)

        keep = jnp.where(tile > 0, 1.0, 0.0)
        zh = zh_ref[CHUNK - HALO:CHUNK, :]
        _to_lane_blocks(gext, 0, keep * zh[:, 0:cd] * _sigmoid(zh[:, cd:2 * cd]))
        pext[0:HALO, :] = keep * zh[:, 2 * cd:]
        za = z_ref[:, 0:cd]
        sg = _sigmoid(z_ref[:, cd:2 * cd])
        _to_lane_blocks(gext, HALO, za * sg)
        pext[HALO:HALO + rt, :] = z_ref[:, 2 * cd:]
        cv = cv_ref[...]
        xc = cv - jnp.mean(cv, axis=-1, keepdims=True)
        rstd = lax.rsqrt(jnp.mean(xc * xc, axis=-1, keepdims=True) + EPS)
        xh = xc * rstd
        y = xh * lg_ref[...] + lb_ref[...]
        sy = _sigmoid(y)
        rows = _row_ids(tile, rt)
        da = jnp.where(rows >= PAD_ROWS, dcat[:, 0:cd], 0.0)
        dy = da * (sy * (1.0 + y * (1.0 - sy)))
        dlg_ref[...] += _rowsum(dy * xh)
        dlb_ref[...] += _rowsum(dy)
        dxh = dy * lg_ref[...]
        dconv = rstd * (dxh - jnp.mean(dxh, axis=-1, keepdims=True) - xh * jnp.mean(dxh * xh, axis=-1, keepdims=True))
        dcb_ref[...] += _rowsum(dconv)
        _to_lane_blocks(dcv, 0, dconv)
        for l, ls in enumerate(_lane_blocks(cd)):
            def acc_rows(rb, accs, l=l):
                base = pl.multiple_of(rb * CHUNK, CHUNK)
                d_blk = dcv[l, pl.ds(base, CHUNK), :]
                out = []
                for k in range(CONV_WIDTH):
                    prod = d_blk * gext[l, pl.ds(base + k + HALO - (CONV_WIDTH - 1), CHUNK), :]
                    part = prod[0:8]
                    for s in range(8, CHUNK, 8):
                        part = part + prod[s:s + 8]
                    out.append(accs[k] + part)
                return tuple(out)

            zero = jnp.zeros((8, ls.stop - ls.start), F32)
            accs = lax.fori_loop(0, per, acc_rows, (zero,) * CONV_WIDTH)
            for k in range(CONV_WIDTH):
                dcw_ref[k:k + 1, ls] += _rowsum(accs[k])
        _conv_rows(dcv, cw_ref, lambda k: CONV_WIDTH - 1 - k, conv_s, per, cd)
        dglu = _from_lane_blocks(conv_s)
        dz_ref[:, 0:cd] = (dglu * sg).astype(BF16)
        dz_ref[:, cd:2 * cd] = (dglu * za * sg * (1.0 - sg)).astype(BF16)
        dcv[:, rt:rt + HALO, :] = dcv[:, 0:HALO, :]
        for gi, window in enumerate(POOL_WINDOWS):
            ls = slice(gi * pg, (gi + 1) * pg)
            v = pext[:, ls]
            cnt = _pool_counts(rows, window)
            tm = (_trailing_sum(v, window)[HALO:] / cnt - v[HALO:]).astype(BF16)
            dp = dcat[:, cd + gi * pg:cd + (gi + 1) * pg]
            dps_ref[:, ls] += _rowsum(dp * _dot(tm, pw_ref[gi]))
            dpl = (dp * ps_ref[:, ls]).astype(BF16)
            dpw_ref[gi] += _dot_tn(tm, dpl)
            dtm = _dot_nt(dpl, pw_ref[gi])
            dsp[0:rt, ls] = dtm / cnt
            dpin = _leading_sum(dsp[:, ls], window)[0:rt] - dtm
            dz_ref[:, 2 * cd + gi * pg:2 * cd + (gi + 1) * pg] = dpin.astype(BF16)
        dsp[rt:rt + HALO, :] = dsp[0:HALO, :]

        @pl.when(step > 0)
        def _():
            tile_rows(tile + 1).wait()

        _project_back(dz_ref, wi_ref, h_ref, g_ref, dh_ref, dhi, dg_ref, step == 0)

        @pl.when(tile > 0)
        def _():
            tile_rows(tile).start()

        @pl.when(tile == 0)
        def _():
            first_rows.start()
            dfirst_ref[...] = dhi[0:CHUNK, :]
            first_rows.wait()

    d = h.shape[1]
    back = lambda i: (ntile - 1 - i, 0)
    halo_idx = lambda i: (jnp.maximum((ntile - 1 - i) * per - 1, 0), 0)
    const2 = lambda i: (0, 0)
    nl, lb = len(_lane_blocks(cd)), min(LANE, cd)
    return pl.pallas_call(
        body, grid=(ntile,),
        in_specs=dep_specs + [
                  pl.BlockSpec((rt, ein), back), pl.BlockSpec((CHUNK, ein), halo_idx), pl.BlockSpec((rt, cd), back),
                  pl.BlockSpec((rt, d), back),
                  _resident((1, d)), _resident(w_in_t.shape), pl.BlockSpec((rt, d), back), _resident(w_out.shape),
                  _resident(conv_w.shape), _resident((1, cd)), _resident((1, cd)), _resident((1, cd)),
                  _resident(pool_w.shape), _resident((1, pd))],
        out_specs=[pl.BlockSpec(memory_space=pl.ANY), pl.BlockSpec((CHUNK, d), const2), pl.BlockSpec((1, d), const2),
                   pl.BlockSpec((rt, ein), back), pl.BlockSpec(conv_w.shape, const2), pl.BlockSpec((1, cd), const2),
                   pl.BlockSpec((1, cd), const2), pl.BlockSpec((1, cd), const2),
                   pl.BlockSpec(pool_w.shape, lambda i: (0, 0, 0)), pl.BlockSpec((1, pd), const2)],
        out_shape=[_sds((t - CHUNK, d), F32), _sds((CHUNK, d), F32), _sds((1, d), F32),
                   _sds((t, ein), BF16), _sds(conv_w.shape, F32), _sds((1, cd), F32), _sds((1, cd), F32),
                   _sds((1, cd), F32), _sds(pool_w.shape, F32), _sds((1, pd), F32)],
        scratch_shapes=[pltpu.VMEM((nl, rt + HALO, lb), F32), pltpu.VMEM((rt + HALO, pd), F32), pltpu.VMEM((nl, rt, lb), F32),
                        pltpu.VMEM((nl, rt + HALO, lb), F32), pltpu.VMEM((rt + HALO, pd), F32), pltpu.VMEM((rt, d), F32),
                        pltpu.SemaphoreType.DMA((1,))],
        compiler_params=_cparams("arbitrary"), name=name)(*deps, z, z, cv, h, gain, w_in_t, dh, w_out, conv_w, conv_b, ln_g,
                                                          ln_b, pool_w, pool_scale)


def _log_decay(r, gw_ref, gb_ref, rows):
    gp = _dot(r.astype(BF16), gw_ref[...]) + gb_ref[...]
    log_sig = jnp.minimum(gp, 0.0) - jnp.log(1.0 + jnp.exp(-jnp.abs(gp)))
    return gp, jnp.where(rows >= PAD_ROWS, log_sig / GATE_NORM, 0.0)


def _tri(strict):
    r = lax.broadcasted_iota(jnp.int32, (CHUNK, CHUNK), 0)
    c = lax.broadcasted_iota(jnp.int32, (CHUNK, CHUNK), 1)
    return jnp.where(c < r if strict else c <= r, 1.0, 0.0).astype(BF16)


def _tri_dot(tri, a):
    hi = a.astype(BF16)
    rest = a - hi.astype(F32)
    mid = rest.astype(BF16)
    lo = (rest - mid.astype(F32)).astype(BF16)
    return _dot(tri, hi) + _dot(tri, mid) + _dot(tri, lo)


def _gla_mid_fwd(h, gain, w_in_blocks, w_out, gate_w, gate_b, head_g, *, name):
    t = h.shape[0]
    nblk, rpb = w_in_blocks.shape[:2]
    dk = gate_b.shape[1]
    hv = head_g.shape[1]
    hk = dk // HEADS
    dv = hv * HEADS
    r_at = 2 * dk + 2 * dv
    assert nblk * rpb == r_at + GATE_RANK
    zw = r_at + GATE_PAD
    rt = _row_tile(t, 320)
    per = rt // CHUNK
    scale = hk ** -0.5

    def body(h_ref, g_ref, wb_ref, wo_ref, gw_ref, gb_ref, hg_ref, ho_ref, o_ref, st_ref, z_ref, u_ref, wi_ref,
             s_ref, la_ref, dec_ref):
        i = pl.program_id(0)

        @pl.when(i == 0)
        def _():
            s_ref[...] = jnp.zeros_like(s_ref)
            for b in range(nblk):
                wi_ref[b * rpb:(b + 1) * rpb, :] = wb_ref[b]
            wi_ref[nblk * rpb:, :] = jnp.zeros((zw - nblk * rpb, wi_ref.shape[1]), BF16)

        _norm_project(h_ref, g_ref, wi_ref, u_ref, z_ref)

        _, la = _log_decay(z_ref[:, r_at:r_at + GATE_PAD], gw_ref, gb_ref, _row_ids(i, rt))
        la_ref[...] = la
        tri = _tri(False)

        def chunk_rows(c):
            return slice(c * CHUNK, (c + 1) * CHUNK)

        def decays(c, carry):
            rows = chunk_rows(c)
            la_c = la_ref[rows, :]
            cum = _tri_dot(tri, la_c)
            dec_ref[rows, :] = jnp.exp(_rowsum(la_c) - cum)
            return carry

        def states(c, carry):
            rows = chunk_rows(c)
            etot = jnp.exp(_rowsum(la_ref[rows, :]))
            for hd in range(HEADS):
                ks = slice(hd * hk, (hd + 1) * hk)
                kd = z_ref[rows, dk + hd * hk:dk + (hd + 1) * hk] * dec_ref[rows, ks]
                v = z_ref[rows, 2 * dk + hd * hv:2 * dk + (hd + 1) * hv]
                s_new = s_ref[hd] * etot[:, ks] + _dot_tn(v.astype(BF16), kd.astype(BF16))
                s_ref[hd] = s_new
                st_ref[c, hd] = s_new
            return carry

        def outputs(c, carry):
            rows = chunk_rows(c)
            for hd in range(HEADS):
                q = z_ref[rows, hd * hk:(hd + 1) * hk] * scale
                g = z_ref[rows, 2 * dk + dv + hd * hv:2 * dk + dv + (hd + 1) * hv]
                o = _dot_nt(q.astype(BF16), st_ref[c, hd].astype(BF16))
                on = o * lax.rsqrt(jnp.mean(o * o, axis=-1, keepdims=True) + EPS) * hg_ref[...]
                o_ref[rows, hd * hv:(hd + 1) * hv] = (on * (g * _sigmoid(g))).astype(BF16)
            return carry

        for phase in (decays, states, outputs):
            for c in range(per):
                phase(c, 0)
        ho_ref[...] = h_ref[...] + _dot(o_ref[...], wo_ref[...])

    d = h.shape[1]
    rows = lambda i: (i, 0)
    return pl.pallas_call(
        body, grid=(t // rt,),
        in_specs=[pl.BlockSpec((rt, d), rows), _resident((1, d)), _resident(w_in_blocks.shape), _resident(w_out.shape),
                  _resident(gate_w.shape), _resident((1, dk)), _resident((1, hv))],
        out_specs=[pl.BlockSpec((rt, d), rows), pl.BlockSpec((rt, dv), rows),
                   pl.BlockSpec((per, HEADS, hv, hk), lambda i: (i, 0, 0, 0)), pl.BlockSpec((rt, zw), rows),
                   pl.BlockSpec((rt, d), rows), pl.BlockSpec((zw, d), lambda i: (0, 0))],
        out_shape=[_sds((t, d), F32), _sds((t, dv), BF16), _sds((t // CHUNK, HEADS, hv, hk), F32), _sds((t, zw), F32),
                   _sds((t, d), BF16), _sds((zw, d), BF16)],
        scratch_shapes=[pltpu.VMEM((HEADS, hv, hk), F32), pltpu.VMEM((rt, dk), F32), pltpu.VMEM((rt, dk), F32)],
        compiler_params=_cparams("arbitrary"), name=name)(h, gain, w_in_blocks, w_out, gate_w, gate_b, head_g)


def _gla_mid_bwd(z, h, gain, w_in_t, dh, w_out, states, gate_w, gate_b, head_g, *, after=None, name):
    t = z.shape[0]
    dk = gate_b.shape[1]
    hv = head_g.shape[1]
    hk = dk // HEADS
    dv = hv * HEADS
    r_at = 2 * dk + 2 * dv
    rt = _row_tile(t, 320)
    ntile = t // rt
    per = rt // CHUNK
    scale = hk ** -0.5
    dep_specs, deps = _dep_specs(after)

    def body(*refs):
        (z_ref, h_ref, g_ref, wi_ref, dh_ref, wo_ref, st_ref, stp_ref, gw_ref, gb_ref, hg_ref,
         dhi_ref, dg_ref, dz_ref, dgw_ref, dgb_ref, dhg_ref,
         ds_ref, la_ref, dla_ref, dec_ref, dos_ref, e_ref, do_ref) = refs[len(deps):]
        step = pl.program_id(0)
        tile = ntile - 1 - step
        do_ref[...] = _dot_nt(dh_ref[...].astype(BF16), wo_ref[...])

        @pl.when(step == 0)
        def _():
            ds_ref[...] = jnp.zeros_like(ds_ref)
            dgw_ref[...] = jnp.zeros_like(dgw_ref)
            dgb_ref[...] = jnp.zeros_like(dgb_ref)
            dhg_ref[...] = jnp.zeros_like(dhg_ref)

        rows_id = _row_ids(tile, rt)
        r = z_ref[:, r_at:r_at + GATE_PAD]
        gp, la = _log_decay(r, gw_ref, gb_ref, rows_id)
        la_ref[...] = la
        tri, tri_strict = _tri(False), _tri(True)
        keep = jnp.where(tile > 0, 1.0, 0.0)

        def chunk_rows(c):
            return slice(c * CHUNK, (c + 1) * CHUNK)

        def recompute(c, dhg):
            rows = chunk_rows(c)
            la_c = la_ref[rows, :]
            cum = _tri_dot(tri, la_c)
            dec_ref[rows, :] = jnp.exp(_rowsum(la_c) - cum)
            for hd in range(HEADS):
                q = (z_ref[rows, hd * hk:(hd + 1) * hk] * scale).astype(BF16)
                g = z_ref[rows, 2 * dk + dv + hd * hv:2 * dk + dv + (hd + 1) * hv]
                s_b = st_ref[c, hd].astype(BF16)
                o = _dot_nt(q, s_b)
                rstd = lax.rsqrt(jnp.mean(o * o, axis=-1, keepdims=True) + EPS)
                oh = o * rstd
                sg = _sigmoid(g)
                d_og = do_ref[rows, hd * hv:(hd + 1) * hv]
                dz_ref[rows, 2 * dk + dv + hd * hv:2 * dk + dv + (hd + 1) * hv] = (
                    d_og * oh * hg_ref[...] * (sg * (1.0 + g * (1.0 - sg)))).astype(BF16)
                don = d_og * (g * sg)
                dhg = dhg + _rowsum(don * oh)
                doh = don * hg_ref[...]
                d_o = (rstd * (doh - oh * jnp.mean(doh * oh, axis=-1, keepdims=True))).astype(BF16)
                dos_ref[rows, hd * hv:(hd + 1) * hv] = d_o
                dz_ref[rows, hd * hk:(hd + 1) * hk] = (_dot(d_o, s_b) * scale).astype(BF16)
            return dhg

        def recurrence(cc, carry):
            c = per - 1 - cc
            rows = chunk_rows(c)
            etot = jnp.exp(_rowsum(la_ref[rows, :]))
            for hd in range(HEADS):
                ks = slice(hd * hk, (hd + 1) * hk)
                q = (z_ref[rows, hd * hk:(hd + 1) * hk] * scale).astype(BF16)
                dec = dec_ref[rows, ks]
                kd = z_ref[rows, dk + hd * hk:dk + (hd + 1) * hk] * dec
                v = z_ref[rows, 2 * dk + hd * hv:2 * dk + (hd + 1) * hv].astype(BF16)
                s_prev = st_ref[c - 1, hd] if c > 0 else keep * stp_ref[0, hd]
                ds_t = ds_ref[hd] + _dot_tn(dos_ref[rows, hd * hv:(hd + 1) * hv], q)
                ds_b = ds_t.astype(BF16)
                dkd = _dot(v, ds_b)
                dz_ref[rows, 2 * dk + hd * hv:2 * dk + (hd + 1) * hv] = _dot_nt(kd.astype(BF16), ds_b).astype(BF16)
                dtot = etot[:, ks] * _rowsum(ds_t * s_prev)
                ds_ref[hd] = ds_t * etot[:, ks]
                dz_ref[rows, dk + hd * hk:dk + (hd + 1) * hk] = (dkd * dec).astype(BF16)
                e_ref[rows, ks] = dkd * kd
                dla_ref[rows, ks] = jnp.broadcast_to(dtot, (CHUNK, hk))
            return carry

        def decay_cotangent(c, carry):
            rows = chunk_rows(c)
            dla_ref[rows, :] += _tri_dot(tri_strict, e_ref[rows, :])
            return carry

        dhg = jnp.zeros((1, hv), F32)
        for c in range(per):
            dhg = recompute(c, dhg)
        dhg_ref[...] += dhg
        for phase in (recurrence, decay_cotangent):
            for c in range(per):
                phase(c, 0)
        dla = jnp.where(rows_id >= PAD_ROWS, dla_ref[...], 0.0)
        dgp = dla * (1.0 / GATE_NORM) * (1.0 - _sigmoid(gp))
        dgb_ref[...] += _rowsum(dgp)
        dgp_b = dgp.astype(BF16)
        dgw_ref[...] += _dot_tn(r.astype(BF16), dgp_b)
        dz_ref[:, r_at:r_at + GATE_PAD] = _dot_nt(dgp_b, gw_ref[...]).astype(BF16)
        _project_back(dz_ref, wi_ref, h_ref, g_ref, dh_ref, dhi_ref, dg_ref, step == 0)

    d = h.shape[1]
    back = lambda i: (ntile - 1 - i, 0)
    const2 = lambda i: (0, 0)
    return pl.pallas_call(
        body, grid=(ntile,),
        in_specs=dep_specs + [
                  pl.BlockSpec((rt, z.shape[1]), back), pl.BlockSpec((rt, d), back), _resident((1, d)),
                  _resident(w_in_t.shape), pl.BlockSpec((rt, d), back), _resident(w_out.shape),
                  pl.BlockSpec((per, HEADS, hv, hk), lambda i: (ntile - 1 - i, 0, 0, 0)),
                  pl.BlockSpec((1, HEADS, hv, hk), lambda i: (jnp.maximum((ntile - 1 - i) * per - 1, 0), 0, 0, 0)),
                  _resident(gate_w.shape), _resident((1, dk)), _resident((1, hv))],
        out_specs=[pl.BlockSpec((rt, d), back), pl.BlockSpec((1, d), const2), pl.BlockSpec((rt, z.shape[1]), back),
                   pl.BlockSpec(gate_w.shape, const2), pl.BlockSpec((1, dk), const2), pl.BlockSpec((1, hv), const2)],
        out_shape=[_sds((t, d), F32), _sds((1, d), F32), _sds(z.shape, BF16), _sds(gate_w.shape, F32),
                   _sds((1, dk), F32), _sds((1, hv), F32)],
        scratch_shapes=[pltpu.VMEM((HEADS, hv, hk), F32), pltpu.VMEM((rt, dk), F32), pltpu.VMEM((rt, dk), F32),
                        pltpu.VMEM((rt, dk), F32), pltpu.VMEM((rt, dv), BF16), pltpu.VMEM((rt, dk), F32),
                        pltpu.VMEM((rt, dv), F32)],
        compiler_params=_cparams("arbitrary"), name=name)(*deps, z, h, gain, w_in_t, dh, w_out, states, states, gate_w,
                                                          gate_b, head_g)


def _adamw_math(w, g, m, v):
    m = ADAM_B1 * m + (1.0 - ADAM_B1) * g
    v = ADAM_B2 * v + (1.0 - ADAM_B2) * (g * g)
    m_hat = m / (1.0 - ADAM_B1 ** ADAM_STEP)
    v_hat = v / (1.0 - ADAM_B2 ** ADAM_STEP)
    return -ADAM_LR * (m_hat / (jnp.sqrt(v_hat) + ADAM_EPS) + ADAM_WD * w), m, v


N_CHIP = N_DEV // 2
BLOCK_ELEMS = 128 * 1024


def _my_slot():
    return 4 * lax.axis_index("x") + 2 * lax.axis_index("y") + lax.axis_index("c")


def _row_block(r, c):
    cap = max(8, BLOCK_ELEMS // (-(-c // LANE) * LANE))
    return max([b for b in range(8, r + 1, 8) if r % b == 0 and b <= cap] or [r])


def _blocks(r, c):
    rb = _row_block(r, c)
    if rb < r or r * c <= BLOCK_ELEMS:
        return rb, c
    return r, max([b for b in (512, 256, LANE) if c % b == 0 and r * b <= BLOCK_ELEMS] or [c])


def _reduce_adam(parts, w, m, v, *, after=None, name):
    nl, r, c = w.shape
    rb, cb = _blocks(r, c)
    dep_specs, deps = _dep_specs(after)

    def body(*refs):
        me = refs[0][0]
        refs = refs[1 + len(deps):]
        p_refs = refs[:2 * nl]
        w_ref, m_ref, v_ref, g_out, d_out, m_out, v_out = refs[2 * nl:]
        layer = pl.program_id(0)
        for li in range(nl):
            @pl.when(layer == li)
            def _(li=li):
                own_ref, land_ref = p_refs[2 * li], p_refs[2 * li + 1]
                mine = own_ref[...].astype(F32)
                g = None
                for dev in range(N_DEV):
                    term = jnp.where(me == dev, mine, land_ref[dev].astype(F32))
                    g = term if g is None else g + term
                g_out[...] = g
                d_out[...], m_out[...], v_out[...] = _adamw_math(w_ref[...], g, m_ref[...], v_ref[...])

    blk = pl.BlockSpec((None, rb, cb), lambda l, i, j, me: (l, i, j))
    p_specs = []
    for li in range(nl):
        p_specs += [
            pl.BlockSpec((None, rb, cb), lambda l, i, j, me, li=li: (me[0], jnp.where(l == li, i, 0), jnp.where(l == li, j, 0))),
            pl.BlockSpec((N_DEV, rb, cb), lambda l, i, j, me, li=li: (0, jnp.where(l == li, i, 0), jnp.where(l == li, j, 0)))]
    flat = [p for pair in parts for p in pair]
    grid_spec = pltpu.PrefetchScalarGridSpec(
        num_scalar_prefetch=1, grid=(nl, r // rb, c // cb), in_specs=dep_specs + p_specs + [blk, blk, blk],
        out_specs=[blk] * 4)
    return pl.pallas_call(
        body, grid_spec=grid_spec, out_shape=[_sds(w.shape, F32)] * 4,
        compiler_params=_cparams("arbitrary", "arbitrary", "arbitrary"), name=name)(
        _my_slot().reshape(1), *deps, *flat, w, m, v)


def _sum8(own, landed, *, name):
    def body(own_ref, land_ref, o_ref):
        me = _my_slot()
        total = None
        for dev in range(N_DEV):
            term = jnp.where(me == dev, own_ref[...], land_ref[dev])
            total = term if total is None else total + term
        o_ref[...] = total

    return pl.pallas_call(body, out_shape=_sds(own.shape, F32), name=name)(own, landed)


def _adam_small(own, landed, split, w, m, v, *, name):
    n = len(w)

    def body(*refs):
        own_refs, land_refs, w_refs, m_refs, v_refs = (refs[k * n:(k + 1) * n] for k in range(5))
        outs = refs[5 * n:]
        me = _my_slot()
        for k in range(n):
            mine = own_refs[k][me] if split[k] else own_refs[k][...]
            g = None
            for dev in range(N_DEV):
                term = jnp.where(me == dev, mine, land_refs[k][dev])
                g = term if g is None else g + term
            outs[4 * k][...] = g
            outs[4 * k + 1][...], outs[4 * k + 2][...], outs[4 * k + 3][...] = _adamw_math(
                w_refs[k][...], g, m_refs[k][...], v_refs[k][...])

    out = pl.pallas_call(body, out_shape=[_sds(a.shape, F32) for a in w for _ in range(4)],
                         compiler_params=pltpu.CompilerParams(vmem_limit_bytes=V7X_VMEM_LIMIT), name=name)(
        *own, *landed, *w, *m, *v)
    return [tuple(out[4 * k:4 * k + 4]) for k in range(n)]


_HBM = pl.BlockSpec(memory_space=pltpu.HBM)
_SEM = pl.BlockSpec(memory_space=pltpu.SEMAPHORE)
_DATAFLOW = pltpu.SideEffectType.DATAFLOW_SIDE_EFFECTING


def _plan_to_all(src, land):
    x, y, c = lax.axis_index("x"), lax.axis_index("y"), lax.axis_index("c")
    return [(src, land.at[_my_slot()], (x ^ ((d >> 2) & 1), y ^ ((d >> 1) & 1), c ^ (d & 1))) for d in range(1, N_DEV)]


def _plan_split_to_all(src, land):
    x, y, c = lax.axis_index("x"), lax.axis_index("y"), lax.axis_index("c")
    peers = [(x ^ ((d >> 2) & 1), y ^ ((d >> 1) & 1), c ^ (d & 1)) for d in range(1, N_DEV)]
    return [(src.at[4 * px + 2 * py + pc], land.at[_my_slot()], (px, py, pc)) for px, py, pc in peers]


_PLAN_COPIES = {_plan_to_all: N_DEV - 1, _plan_split_to_all: N_DEV - 1}


def _plans(plan, n):
    return list(plan) if isinstance(plan, (list, tuple)) else [plan] * n


def _exchange_copies(plan, ins, lands, send, recv):
    copies, sem = [], 0
    for p, src, land in zip(_plans(plan, len(lands)), ins, lands):
        for s, dst, dev in p(src, land):
            copies.append(pltpu.make_async_remote_copy(
                src_ref=s, dst_ref=dst, send_sem=send.at[sem], recv_sem=recv.at[sem],
                device_id=dev, device_id_type=pl.DeviceIdType.MESH))
            sem += 1
    return copies


def _place_own(srcs, *, after=None, name):
    dep_specs, deps = _dep_specs(after)
    n = len(srcs)
    arrays, in_specs, shapes = [], [], []
    for a, dtype in srcs:
        if isinstance(a, tuple):
            a, layer = a
            in_specs.append(pl.BlockSpec((None,) + a.shape[1:], lambda i, layer=layer: (layer, 0, 0)))
            shapes.append(a.shape[1:])
        else:
            in_specs.append(pl.BlockSpec(a.shape, lambda i: (0, 0)))
            shapes.append(a.shape)
        arrays.append(a)
    dtypes = [dtype for _, dtype in srcs]

    def body(*refs):
        refs = refs[len(deps):]
        a_refs, o_refs, cast_refs, sem = refs[:n], refs[n:2 * n], refs[2 * n:3 * n], refs[3 * n]
        me = _my_slot()
        copies = []
        for k in range(n):
            cast_refs[k][...] = a_refs[k][...].astype(dtypes[k])
            copies.append(pltpu.make_async_copy(cast_refs[k], o_refs[k].at[me], sem.at[k]))
            copies[-1].start()
        for cp in copies:
            cp.wait()

    return pl.pallas_call(
        body, grid=(1,), in_specs=dep_specs + in_specs, out_specs=[pl.BlockSpec(memory_space=pl.ANY)] * n,
        out_shape=[_sds((N_DEV,) + shape, dtype) for shape, dtype in zip(shapes, dtypes)],
        scratch_shapes=[pltpu.VMEM(shape, dtype) for shape, dtype in zip(shapes, dtypes)] + [pltpu.SemaphoreType.DMA((n,))],
        compiler_params=pltpu.CompilerParams(vmem_limit_bytes=V7X_VMEM_LIMIT), name=name)(*deps, *arrays)


def _plan_gather_first(land, _):
    x, y, c = lax.axis_index("x"), lax.axis_index("y"), lax.axis_index("c")
    mine = land.at[_my_slot()]
    return [(mine, mine, (x, y, 1 - c))] + [(mine, mine, (x ^ (d >> 1), y ^ (d & 1), c)) for d in range(1, N_CHIP)]


def _plan_gather_relay(land, _):
    x, y, c = lax.axis_index("x"), lax.axis_index("y"), lax.axis_index("c")
    slots = [land.at[4 * (x ^ (d >> 1)) + 2 * (y ^ (d & 1)) + c] for d in range(1, N_CHIP)]
    return [(s, s, (x, y, 1 - c)) for s in slots]


def _plan_gather_direct(land, _):
    return _plan_to_all(land.at[_my_slot()], land)


_PLAN_COPIES[_plan_gather_first] = N_CHIP
_PLAN_COPIES[_plan_gather_relay] = N_CHIP - 1
_PLAN_COPIES[_plan_gather_direct] = N_DEV - 1


def _exchange_start(plan, arrs, lands, *, after=None, name):
    bufs = list(lands) if arrs is None else list(arrs) + list(lands)
    n, nb = len(lands), len(bufs)
    nsem = sum(_PLAN_COPIES[p] for p in _plans(plan, n))
    dep_specs, deps = _dep_specs(after)

    def body(*refs):
        ins, land_refs = refs[:n], refs[nb - n:nb]
        send, recv = refs[nb + len(deps)], refs[nb + len(deps) + 1]
        for cp in _exchange_copies(plan, ins, land_refs, send, recv):
            cp.start()
        refs[-1][...] = jnp.zeros_like(refs[-1])

    out = pl.pallas_call(
        body, name=name,
        out_shape=(pltpu.SemaphoreType.DMA((nsem,)), pltpu.SemaphoreType.DMA((nsem,)),
                   *[pltpu.HBM(a.shape, a.dtype) for a in bufs], _sds((8, LANE), F32)),
        in_specs=[_HBM] * nb + dep_specs,
        out_specs=(_SEM, _SEM, *([_HBM] * nb), pl.BlockSpec(memory_space=pltpu.VMEM)),
        input_output_aliases={i: 2 + i for i in range(nb)},
        compiler_params=pltpu.CompilerParams(has_side_effects=_DATAFLOW),
    )(*[pltpu.with_memory_space_constraint(a, pltpu.HBM) for a in bufs], *deps)
    return (plan, n, out[0], out[1], list(out[2:2 + nb])), out[-1]


def _exchange_now(plan, lands, *, name):
    n = len(lands)
    nsem = sum(_PLAN_COPIES[p] for p in _plans(plan, n))

    def body(*refs):
        land_refs, send, recv = refs[n:2 * n], refs[2 * n], refs[2 * n + 1]
        copies = _exchange_copies(plan, land_refs, land_refs, send, recv)
        for cp in copies:
            cp.start()
        for cp in copies:
            cp.wait_send()
            cp.wait_recv()

    hbm = pl.BlockSpec(memory_space=pl.ANY)
    return pl.pallas_call(
        body, in_specs=[hbm] * n, out_specs=[hbm] * n, out_shape=[_sds(a.shape, a.dtype) for a in lands],
        input_output_aliases={i: i for i in range(n)},
        scratch_shapes=[pltpu.SemaphoreType.DMA((nsem,)), pltpu.SemaphoreType.DMA((nsem,))], name=name)(*lands)


def _exchange_wait(state, after, *, name):
    plan, n, send_sem, recv_sem, bufs = state
    nb = len(bufs)
    after = list(after) if isinstance(after, (list, tuple)) else [after]

    def body(*refs):
        ins, land_refs, send, recv = refs[:n], refs[nb - n:nb], refs[nb], refs[nb + 1]
        for cp in _exchange_copies(plan, ins, land_refs, send, recv):
            cp.wait_send()
            cp.wait_recv()

    out = pl.pallas_call(
        body, name=name, out_shape=[pltpu.HBM(a.shape, a.dtype) for a in bufs],
        in_specs=[_HBM] * nb + [_SEM, _SEM] + [pl.BlockSpec(memory_space=pl.ANY)] * len(after), out_specs=[_HBM] * nb,
        input_output_aliases={i: i for i in range(nb)},
        compiler_params=pltpu.CompilerParams(has_side_effects=_DATAFLOW),
    )(*bufs, send_sem, recv_sem, *after)
    return list(out[:n]), list(out[nb - n:])


def _dep_specs(after):
    return ([], []) if after is None else ([pl.BlockSpec(memory_space=pl.ANY)], [after])


def _undo_column_split(g):
    return jnp.transpose(g, (1, 0, 2)).reshape(g.shape[1], N_DEV * g.shape[2])


def _column_split(a):
    r, c = a.shape
    return jnp.transpose(a.reshape(r, N_DEV, c // N_DEV), (1, 0, 2))


class _WholeWeights:
    def __init__(self, groups):
        self.groups = groups
        self.grads = {}

    def fetch(self, group, after):
        return self.groups[group]

    def emit(self, group, grads):
        self.grads.update(grads)
        return None


def _local_step(x, target, replicated, src):
    d = x.shape[1]
    mix_g, ffn_g = replicated["mix_g"], replicated["ffn_g"]
    cp = src.fetch("cp", [])
    cp_mid = (cp["conv_w"], replicated["conv_b"], replicated["ln_g"], replicated["ln_b"], replicated["pool_w"],
              replicated["pool_scale"])

    h1, cat, z0, u0, cv0, h0 = _cp_mid_fwd(x, cp["meta"], mix_g[0:1], cp["cp_w_in_t"], cp["cp_w_out"], *cp_mid,
                                           name="cp_mixer")
    ffn0 = src.fetch("ffn0", h1)
    h2, uf0, rf0 = _ffn_fwd(h1, ffn_g[0:1], ffn0["w1"], ffn0["w2"], name="ffn0")
    gla = src.fetch("gla", h2)
    gla_mid = (gla["gate_w"], gla["gate_b"], gla["head_g"])
    h3, og, states, z1, u1, gla_w_in_t = _gla_mid_fwd(h2, mix_g[1:2], gla["gla_w_in_t"], gla["gla_w_out"], *gla_mid,
                                                      name="gla_mixer")
    ffn1 = src.fetch("ffn1", h3)
    dh4, uf1, rf1, loss, d_final_g = _ffn_fwd(h3, ffn_g[1:2], ffn1["w1"], ffn1["w2"],
                                              loss_head=(replicated["final_g"], target), name="ffn1_loss")

    dh3, dhh1, dob1, dffn_g1 = _ffn_bwd_x(h3, dh4, ffn_g[1:2], rf1, ffn1["w1"], ffn1["w2"], name="ffn1_bwd_x")
    sent = src.emit("ffn1_w1", dict(w1=_linear_bwd_w(uf1, dhh1, column_blocks=N_DEV, name="ffn1_dw1")))
    sent = src.emit("ffn1_w2", dict(w2=_linear_bwd_w(rf1, dob1, square_x=True, after=sent, name="ffn1_dw2")))
    d_gla_w_out = _linear_bwd_w(og, dh3, name="gla_out_dw")
    dh2, dmix_g1, dz1, d_gate_w, d_gate_b, d_head_g = _gla_mid_bwd(
        z1, h2, mix_g[1:2], gla_w_in_t, dh3, gla["gla_w_out"], states, *gla_mid, after=sent, name="gla_mixer_bwd")
    d_gla_w_in_t = _linear_bwd_w(dz1, u1, row_blocks=gla["gla_w_in_t"].shape[:2], name="gla_in_dw")
    sent = src.emit("gla", dict(gla_w_in_t=d_gla_w_in_t, gla_w_out=d_gla_w_out))
    dh1, dhh0, dob0, dffn_g0 = _ffn_bwd_x(h1, dh2, ffn_g[0:1], rf0, ffn0["w1"], ffn0["w2"], after=sent, name="ffn0_bwd_x")
    sent = src.emit("ffn0_w1", dict(w1=_linear_bwd_w(uf0, dhh0, column_blocks=N_DEV, name="ffn0_dw1")))
    sent = src.emit("ffn0_w2", dict(w2=_linear_bwd_w(rf0, dob0, square_x=True, after=sent, name="ffn0_dw2")))
    d_cp_w_out = _linear_bwd_w(cat, dh1, after=sent, name="cp_out_dw")
    sent = src.emit("cp_out", dict(cp_w_out=d_cp_w_out))
    dx, dh0_first, dmix_g0, dz0, d_conv_w, d_conv_b, d_ln_g, d_ln_b, d_pool_w, d_pool_scale = _cp_mid_bwd(
        z0, cv0, h0, mix_g[0:1], cp["cp_w_in_t"], dh1, cp["cp_w_out"], *cp_mid, after=sent, name="cp_mixer_bwd")
    d_cp_w_in_t = _linear_bwd_w(dz0, u0, name="cp_in_dw")

    small = dict(
        mix_g=jnp.concatenate([dmix_g0, dmix_g1]), ffn_g=jnp.concatenate([dffn_g0, dffn_g1]), conv_b=d_conv_b, ln_g=d_ln_g,
        ln_b=d_ln_b, pool_w=d_pool_w, pool_scale=d_pool_scale, final_g=d_final_g, meta=dh0_first[PAD_ROWS:], conv_w=d_conv_w,
        gate_w=d_gate_w, gate_b=d_gate_b, head_g=d_head_g)
    src.emit("cp", dict(cp_w_in_t=d_cp_w_in_t, small=small, loss=loss))
    return loss, dx, small


_REPLICATED = ("mix_norm_g", "ffn_norm_g", "cp_conv_b", "cp_ln_g", "cp_ln_b", "cp_pool_w", "cp_pool_scale", "final_norm_g")
_SMALL_SHARDED = ("meta_tokens", "cp_conv_w", "gla_gate_w2", "gla_gate_b", "gla_head_g")
_NAMES = ("meta_tokens", "mix_norm_g", "ffn_norm_g", "ffn_w1", "ffn_w2", "cp_w_in", "cp_conv_w", "cp_conv_b", "cp_ln_g",
          "cp_ln_b", "cp_pool_w", "cp_pool_scale", "cp_w_out", "gla_w_in", "gla_gate_w2", "gla_gate_b", "gla_head_g",
          "gla_w_out", "final_norm_g")
_SMALL_GRADS = ("mix_g", "ffn_g", "conv_b", "ln_g", "ln_b", "pool_w", "pool_scale", "final_g", "meta", "conv_w", "gate_w",
                "gate_b", "head_g")
_GROUPS = ("cp", "ffn0", "gla", "ffn1")
_TWO_LEG_GATHERS = ("cp", "ffn0", "ffn1")


class _Exchanges:
    def __init__(self, w, d):
        self.d = d
        small = [w[n].reshape(w[n].shape[-2:]) for n in _SMALL_SHARDED]
        self.small_shard_shapes = [w[n].shape for n in _SMALL_SHARDED]
        shards = dict(
            cp=[(w["cp_w_in"][0].T, BF16), (w["cp_w_out"][0], BF16)] + [(a, F32) for a in small],
            ffn0=[((w["ffn_w1"], 0), BF16), ((w["ffn_w2"], 0), BF16)],
            gla=[(w["gla_w_in"][0].T, BF16), (w["gla_w_out"][0], BF16)],
            ffn1=[((w["ffn_w1"], 1), BF16), ((w["ffn_w2"], 1), BF16)])
        self.gathers = {}
        self.sent = {}
        token = None
        for group in _GROUPS:
            lands = _place_own(shards[group], after=token, name=f"place_w_{group}")
            plan = _plan_gather_first if group in _TWO_LEG_GATHERS else _plan_gather_direct
            self.gathers[group], token = _exchange_start(plan, None, lands, after=token, name=f"start_w_{group}")
        self.token = token

    def fetch(self, group, after):
        d = self.d
        after = (list(after) if isinstance(after, (list, tuple)) else [after]) + [self.token]
        _, got = _exchange_wait(self.gathers[group], after, name=f"wait_w_{group}")
        if group in _TWO_LEG_GATHERS:
            got = _exchange_now(_plan_gather_relay, got, name=f"relay_w_{group}")
        if group in ("ffn0", "ffn1"):
            return dict(w1=got[0], w2=got[1])
        if group == "gla":
            return dict(gla_w_in_t=got[0], gla_w_out=got[1].reshape(d, d), gate_w=self.gate_w, gate_b=self.gate_b,
                        head_g=self.head_g)
        meta, conv_w, gate_w, self.gate_b, self.head_g = [_undo_column_split(a) for a in got[2:]]
        self.gate_w = jnp.pad(gate_w, ((0, GATE_PAD - GATE_RANK), (0, 0))).astype(BF16)
        return dict(cp_w_in_t=got[0].reshape(-1, d), cp_w_out=got[1].reshape(d, d), meta=meta,
                    conv_w=jnp.pad(conv_w, ((0, 1), (0, 0))))

    def emit(self, group, g):
        d = self.d
        if group in ("ffn0_w1", "ffn1_w1"):
            arrs = [g["w1"]]
        elif group in ("ffn0_w2", "ffn1_w2"):
            arrs = [g["w2"].reshape(N_DEV, -1, d)]
        elif group == "gla":
            arrs = [g["gla_w_in_t"], g["gla_w_out"].reshape(N_DEV, d // N_DEV, d)]
        elif group == "cp_out":
            arrs = [g["cp_w_out"].reshape(N_DEV, d // N_DEV, d)]
        else:
            s = dict(g["small"])
            s.update(pool_w=s["pool_w"][None], conv_w=s["conv_w"][:CONV_WIDTH], gate_w=s["gate_w"][:GATE_RANK])
            own = [s[n] for n in _SMALL_GRADS[:len(_REPLICATED)]]
            own += [_column_split(s[n]).reshape((N_DEV,) + shape)
                    for n, shape in zip(_SMALL_GRADS[len(_REPLICATED):], self.small_shard_shapes)]
            plans = [_plan_to_all] * len(_REPLICATED) + [_plan_split_to_all] * len(_SMALL_SHARDED)
            lands = [lax.empty((N_DEV,) + a.shape, F32) for a in own[:len(_REPLICATED)]]
            lands += [lax.empty(a.shape, F32) for a in own[len(_REPLICATED):]]
            own.append(g["loss"])
            plans.append(_plan_to_all)
            lands.append(lax.empty((N_DEV,) + g["loss"].shape, F32))
            self.small_sent, self.token = _exchange_start(plans, own, lands, after=self.token, name="start_g_small")
            arrs = [g["cp_w_in_t"].reshape(N_DEV, -1, d)]
        self.sent[group], self.token = _exchange_start(_plan_split_to_all, arrs, [lax.empty(a.shape, a.dtype) for a in arrs],
                                                       after=self.token, name=f"start_g_{group}")
        return self.token

    def finish(self, w, mom, var):
        out = {}
        after = self.token

        def landed(group):
            own, got = _exchange_wait(self.sent[group], after, name=f"wait_g_{group}")
            return list(zip(own, got))

        def adam(n, parts, behind=None, transposed=False):
            flip = (lambda a: jnp.transpose(a, (0, 2, 1))) if transposed else (lambda a: a)
            res = _reduce_adam(parts, flip(w[n]), flip(mom[n]), flip(var[n]), after=behind, name=f"adam_{n}")
            out[n] = tuple(flip(a) for a in res)
            return res[0]

        ffn1_w1 = landed("ffn1_w1")
        after = ffn1_w1[0][1]
        ffn1_w2 = landed("ffn1_w2")
        after = ffn1_w2[0][1]
        gla = landed("gla")
        after = adam("gla_w_in", [gla[0]], transposed=True)
        after = adam("gla_w_out", [gla[1]], after)
        ffn0_w1 = landed("ffn0_w1")
        after = adam("ffn_w1", [ffn0_w1[0], ffn1_w1[0]])
        ffn0_w2 = landed("ffn0_w2")
        after = adam("ffn_w2", [ffn0_w2[0], ffn1_w2[0]])
        small_own, small_landed = _exchange_wait(self.small_sent, after, name="wait_g_small")
        names = _REPLICATED + _SMALL_SHARDED
        split = [False] * len(_REPLICATED) + [True] * len(_SMALL_SHARDED)
        small_new = _adam_small(small_own[:-1], small_landed[:-1], split, [w[n] for n in names], [mom[n] for n in names],
                                [var[n] for n in names], name="adam_small")
        out.update(zip(names, small_new))
        out["loss"] = _sum8(small_own[-1], small_landed[-1], name="sum_loss")[0, 0]

        after = small_new[0][0]
        cp_out = landed("cp_out")
        after = adam("cp_w_out", [cp_out[0]])
        cp = landed("cp")
        adam("cp_w_in", [cp[0]], transposed=True)
        return out


def kernel(x, meta_tokens, mix_norm_g, ffn_norm_g, ffn_w1, ffn_w2, cp_w_in, cp_conv_w, cp_conv_b, cp_ln_g, cp_ln_b, cp_pool_w, cp_pool_scale, cp_w_out, gla_w_in, gla_gate_w2, gla_gate_b, gla_head_g, gla_w_out, final_norm_g, loss_target, m_meta_tokens, m_mix_norm_g, m_ffn_norm_g, m_ffn_w1, m_ffn_w2, m_cp_w_in, m_cp_conv_w, m_cp_conv_b, m_cp_ln_g, m_cp_ln_b, m_cp_pool_w, m_cp_pool_scale, m_cp_w_out, m_gla_w_in, m_gla_gate_w2, m_gla_gate_b, m_gla_head_g, m_gla_w_out, m_final_norm_g, v_meta_tokens, v_mix_norm_g, v_ffn_norm_g, v_ffn_w1, v_ffn_w2, v_cp_w_in, v_cp_conv_w, v_cp_conv_b, v_cp_ln_g, v_cp_ln_b, v_cp_pool_w, v_cp_pool_scale, v_cp_w_out, v_gla_w_in, v_gla_gate_w2, v_gla_gate_b, v_gla_head_g, v_gla_w_out, v_final_norm_g):
    w = dict(meta_tokens=meta_tokens, mix_norm_g=mix_norm_g, ffn_norm_g=ffn_norm_g, ffn_w1=ffn_w1, ffn_w2=ffn_w2,
             cp_w_in=cp_w_in, cp_conv_w=cp_conv_w, cp_conv_b=cp_conv_b, cp_ln_g=cp_ln_g, cp_ln_b=cp_ln_b,
             cp_pool_w=cp_pool_w, cp_pool_scale=cp_pool_scale, cp_w_out=cp_w_out, gla_w_in=gla_w_in,
             gla_gate_w2=gla_gate_w2, gla_gate_b=gla_gate_b, gla_head_g=gla_head_g, gla_w_out=gla_w_out,
             final_norm_g=final_norm_g.reshape(1, -1))
    mom = dict(meta_tokens=m_meta_tokens, mix_norm_g=m_mix_norm_g, ffn_norm_g=m_ffn_norm_g, ffn_w1=m_ffn_w1, ffn_w2=m_ffn_w2,
               cp_w_in=m_cp_w_in, cp_conv_w=m_cp_conv_w, cp_conv_b=m_cp_conv_b, cp_ln_g=m_cp_ln_g, cp_ln_b=m_cp_ln_b,
               cp_pool_w=m_cp_pool_w, cp_pool_scale=m_cp_pool_scale, cp_w_out=m_cp_w_out, gla_w_in=m_gla_w_in,
               gla_gate_w2=m_gla_gate_w2, gla_gate_b=m_gla_gate_b, gla_head_g=m_gla_head_g, gla_w_out=m_gla_w_out,
               final_norm_g=m_final_norm_g.reshape(1, -1))
    var = dict(meta_tokens=v_meta_tokens, mix_norm_g=v_mix_norm_g, ffn_norm_g=v_ffn_norm_g, ffn_w1=v_ffn_w1, ffn_w2=v_ffn_w2,
               cp_w_in=v_cp_w_in, cp_conv_w=v_cp_conv_w, cp_conv_b=v_cp_conv_b, cp_ln_g=v_cp_ln_g, cp_ln_b=v_cp_ln_b,
               cp_pool_w=v_cp_pool_w, cp_pool_scale=v_cp_pool_scale, cp_w_out=v_cp_w_out, gla_w_in=v_gla_w_in,
               gla_gate_w2=v_gla_gate_w2, gla_gate_b=v_gla_gate_b, gla_head_g=v_gla_head_g, gla_w_out=v_gla_w_out,
               final_norm_g=v_final_norm_g.reshape(1, -1))
    d = x.shape[-1]
    replicated = dict(mix_g=w["mix_norm_g"], ffn_g=w["ffn_norm_g"], conv_b=w["cp_conv_b"], ln_g=w["cp_ln_g"],
                      ln_b=w["cp_ln_b"], pool_w=w["cp_pool_w"][0].astype(BF16), pool_scale=w["cp_pool_scale"],
                      final_g=w["final_norm_g"])
    exchanges = _Exchanges(w, d)
    _, grad_x, _ = _local_step(x[0], loss_target[0], replicated, exchanges)
    out = exchanges.finish(w, mom, var)
    loss = out.pop("loss")

    def leaf(n, k):
        a = out[n][k]
        return a.reshape(-1) if n == "final_norm_g" else a

    return (loss, grad_x[None], *[leaf(n, 0) for n in _NAMES], *[leaf(n, 1) for n in _NAMES],
            *[leaf(n, 2) for n in _NAMES], *[leaf(n, 3) for n in _NAMES])
```

```python
import functools

import jax
import jax.numpy as jnp
from jax import lax
from jax.experimental import pallas as pl
from jax.experimental.pallas import tpu as pltpu

F32, BF16 = jnp.float32, jnp.bfloat16
N_DEV = 8
CHUNK = 64
N_META = 16
PAD_ROWS = CHUNK - N_META
HALO = 32
EPS = 1e-5
CONV_WIDTH = 31
POOL_WINDOWS = (2, 4, 8, 16)
HEADS = 4
GATE_RANK = 16
GATE_NORM = 16.0
GATE_PAD = 128
ADAM_LR, ADAM_B1, ADAM_B2, ADAM_EPS, ADAM_WD, ADAM_STEP = 0.001, 0.9, 0.999, 1e-08, 0.01, 10
V7X_VMEM_LIMIT = 56 * 2 ** 20
LANE = 128


def _cparams(*sem):
    return pltpu.CompilerParams(dimension_semantics=sem, vmem_limit_bytes=V7X_VMEM_LIMIT)


def _row_tile(t, cap):
    best = CHUNK
    for r in range(CHUNK, min(t, cap) + 1, CHUNK):
        if t % r == 0:
            best = r
    return best


def _resident(shape):
    return pl.BlockSpec(shape, lambda *_: (0,) * len(shape), pipeline_mode=pl.Buffered(1))


def _dot(a, b):
    return jnp.dot(a, b, preferred_element_type=F32)


def _dot_nt(a, b):
    return lax.dot_general(a, b, (((1,), (1,)), ((), ())), preferred_element_type=F32)


def _dot_tn(a, b):
    return lax.dot_general(a, b, (((0,), (0,)), ((), ())), preferred_element_type=F32)


def _rowsum(a):
    return jnp.sum(a, axis=0, keepdims=True)


def _sigmoid(a):
    return 1.0 / (1.0 + jnp.exp(-a))


def _row_ids(tile, rt):
    return tile * rt + lax.broadcasted_iota(jnp.int32, (rt, 1), 0)


def _sds(shape, dtype):
    return jax.ShapeDtypeStruct(shape, dtype)


DW_ROWS = 1024


def _linear_bwd_w(x, dy, *, square_x=False, column_blocks=None, row_blocks=None, after=None, name):
    t, k = x.shape
    n = dy.shape[1]
    cut_k = k > n and column_blocks is None
    assert cut_k or row_blocks is None
    width = k if cut_k else n
    blk = n // column_blocks if column_blocks else max(c for c in (640, 512, 384, 256, LANE) if width % c == 0)
    dep_specs, deps = _dep_specs(after)

    def body(*refs):
        x_ref, dy_ref, o_ref, acc = refs[len(deps):]
        for c0 in range(0, t, DW_ROWS):
            rows = slice(c0, min(c0 + DW_ROWS, t))
            xv = x_ref[rows, :]
            if square_x:
                xv = xv.astype(F32)
                xv = xv * xv
            part = _dot_tn(xv.astype(BF16), dy_ref[rows, :].astype(BF16))
            if c0 == 0:
                acc[...] = part
            else:
                acc[...] += part
        if row_blocks is None:
            o_ref[...] = acc[...].astype(BF16)
            return
        nb, rpb = row_blocks
        for s in range(width // blk):
            @pl.when(pl.program_id(0) == s)
            def _(s=s):
                for b in range(nb):
                    lo, hi = max(s * blk, b * rpb), min((s + 1) * blk, (b + 1) * rpb)
                    if lo < hi:
                        o_ref[b, lo - b * rpb:hi - b * rpb, :] = acc[lo - s * blk:hi - s * blk, :].astype(BF16)

    out_shape = _sds((k, n), BF16)
    semantics = "parallel"
    if cut_k:
        in_specs = [pl.BlockSpec((t, blk), lambda j: (0, j)), _resident((t, n))]
        out_specs = pl.BlockSpec((blk, n), lambda j: (j, 0))
        acc_shape = (blk, n)
        if row_blocks:
            out_shape = _sds(row_blocks + (n,), BF16)
            out_specs = pl.BlockSpec(out_shape.shape, lambda j: (0, 0, 0))
            semantics = "arbitrary"
    else:
        in_specs = [_resident((t, k)), pl.BlockSpec((t, blk), lambda j: (0, j))]
        out_specs = pl.BlockSpec((k, blk), lambda j: (0, j))
        acc_shape = (k, blk)
        if column_blocks:
            out_specs = pl.BlockSpec((None, k, blk), lambda j: (j, 0, 0))
            out_shape = _sds((column_blocks, k, blk), BF16)
    return pl.pallas_call(
        body, grid=(width // blk,), in_specs=dep_specs + in_specs, out_specs=out_specs, out_shape=out_shape,
        scratch_shapes=[pltpu.VMEM(acc_shape, F32)], compiler_params=_cparams(semantics), name=name)(*deps, x, dy)


FFN_BLOCKS_PER_STEP = 2


def _ffn_fwd(h, gain, w1g, w2g, *, loss_head=None, name):
    t, d = h.shape
    f8 = w1g.shape[-1]
    rt = _row_tile(t, 832)
    nb = FFN_BLOCKS_PER_STEP
    nstep = N_DEV // nb

    def body(*refs):
        if loss_head is None:
            h_ref, g_ref, w1_ref, w2_ref, o_ref, u_ref, r_ref, acc_ref = refs
        else:
            (h_ref, g_ref, w1_ref, w2_ref, fg_ref, tgt_ref, o_ref, u_ref, r_ref, loss_ref, dfg_ref, acc_ref, t_ref,
             t_sem) = refs
        i, j = pl.program_id(0), pl.program_id(1)

        def target_rows(act):
            @pl.when(i == 0)
            def _():
                act(pltpu.make_async_copy(tgt_ref.at[pl.ds(0, rt - CHUNK)], t_ref.at[pl.ds(CHUNK, rt - CHUNK)], t_sem.at[0]))

            if t > rt:
                @pl.when(i > 0)
                def _():
                    act(pltpu.make_async_copy(tgt_ref.at[pl.ds(pl.multiple_of(i * rt - CHUNK, CHUNK), rt)], t_ref,
                                              t_sem.at[0]))

        @pl.when(j == 0)
        def _():
            if loss_head is not None:
                @pl.when(i == 0)
                def _():
                    t_ref[0:CHUNK, :] = jnp.zeros((CHUNK, d), F32)

                target_rows(lambda copy: copy.start())

            hv = h_ref[...]
            u_ref[...] = (hv * lax.rsqrt(jnp.mean(hv * hv, axis=-1, keepdims=True) + EPS) * g_ref[...]).astype(BF16)
            acc_ref[...] = jnp.zeros_like(acc_ref)

        part = None
        for b in range(nb):
            a = jnp.maximum(_dot(u_ref[...], w1_ref[b]), 0.0)
            r_ref[:, b * f8:(b + 1) * f8] = a.astype(BF16)
            term = _dot((a * a).astype(BF16), w2_ref[b])
            part = term if part is None else part + term
        acc_ref[...] += part

        @pl.when(j == nstep - 1)
        def _():
            y = h_ref[...] + acc_ref[...]
            if loss_head is None:
                o_ref[...] = y
                return

            @pl.when(i == 0)
            def _():
                loss_ref[...] = jnp.zeros_like(loss_ref)
                dfg_ref[...] = jnp.zeros_like(dfg_ref)

            target_rows(lambda copy: copy.wait())

            rstd = lax.rsqrt(jnp.mean(y * y, axis=-1, keepdims=True) + EPS)
            xh = y * rstd
            err = jnp.where(_row_ids(i, rt) >= CHUNK, xh * fg_ref[...] - t_ref[...], 0.0)
            loss_ref[...] += (0.5 / d) * jnp.sum(err * err)
            dy = err * (1.0 / d)
            dfg_ref[...] += _rowsum(dy * xh)
            dxh = dy * fg_ref[...]
            o_ref[...] = rstd * (dxh - xh * jnp.mean(dxh * xh, axis=-1, keepdims=True))

    rows = lambda i, j: (i, 0)
    in_specs = [pl.BlockSpec((rt, d), rows), _resident((1, d)),
                pl.BlockSpec((nb, d, f8), lambda i, j: (j, 0, 0)), pl.BlockSpec((nb, f8, d), lambda i, j: (j, 0, 0))]
    out_specs = [pl.BlockSpec((rt, d), rows), pl.BlockSpec((rt, d), rows), pl.BlockSpec((rt, nb * f8), lambda i, j: (i, j))]
    out_shape = [_sds((t, d), F32), _sds((t, d), BF16), _sds((t, N_DEV * f8), BF16)]
    args = [h, gain, w1g, w2g]
    scratch_shapes = [pltpu.VMEM((rt, d), F32)]
    if loss_head is not None:
        in_specs += [_resident((1, d)), pl.BlockSpec(memory_space=pl.ANY)]
        out_specs += [pl.BlockSpec((8, LANE), lambda i, j: (0, 0)), pl.BlockSpec((1, d), lambda i, j: (0, 0))]
        out_shape += [_sds((8, LANE), F32), _sds((1, d), F32)]
        args += list(loss_head)
        scratch_shapes += [pltpu.VMEM((rt, d), F32), pltpu.SemaphoreType.DMA((1,))]
    return pl.pallas_call(
        body, grid=(t // rt, nstep), in_specs=in_specs, out_specs=out_specs, out_shape=out_shape,
        scratch_shapes=scratch_shapes,
        compiler_params=_cparams("arbitrary" if loss_head is not None else "parallel", "arbitrary"), name=name)(*args)


def _ffn_bwd_x(h, dout, gain, r, w1g, w2g, *, after=None, name):
    t, d = h.shape
    f8 = w1g.shape[-1]
    rt = _row_tile(t, 832)
    nb = FFN_BLOCKS_PER_STEP
    last = N_DEV // nb - 1
    dep_specs, deps = _dep_specs(after)

    def body(*refs):
        h_ref, do_ref, g_ref, r_ref, w1_ref, w2_ref, dh_ref, dhh_ref, dob_ref, dg_ref, du_ref = refs[len(deps):]
        i, j = pl.program_id(0), pl.program_id(1)

        @pl.when(j == 0)
        def _():
            dob_ref[...] = do_ref[...].astype(BF16)
            du_ref[...] = jnp.zeros_like(du_ref)

        part = None
        for b in range(nb):
            cols = slice(b * f8, (b + 1) * f8)
            dhh = (_dot_nt(dob_ref[...], w2_ref[b]) * (2.0 * r_ref[:, cols].astype(F32))).astype(BF16)
            dhh_ref[:, cols] = dhh
            term = _dot_nt(dhh, w1_ref[b])
            part = term if part is None else part + term
        du_ref[...] += part

        @pl.when(j == last)
        def _():
            @pl.when(i == 0)
            def _():
                dg_ref[...] = jnp.zeros_like(dg_ref)

            hv = h_ref[...]
            rstd = lax.rsqrt(jnp.mean(hv * hv, axis=-1, keepdims=True) + EPS)
            xh = hv * rstd
            du = du_ref[...]
            dg_ref[...] += _rowsum(du * xh)
            dxh = du * g_ref[...]
            dh_ref[...] = do_ref[...] + rstd * (dxh - xh * jnp.mean(dxh * xh, axis=-1, keepdims=True))

    rows = lambda i, j: (i, 0)
    return pl.pallas_call(
        body, grid=(t // rt, N_DEV // nb),
        in_specs=dep_specs + [
                  pl.BlockSpec((rt, d), rows), pl.BlockSpec((rt, d), rows), _resident((1, d)),
                  pl.BlockSpec((rt, nb * f8), lambda i, j: (i, j)),
                  pl.BlockSpec((nb, d, f8), lambda i, j: (j, 0, 0)),
                  pl.BlockSpec((nb, f8, d), lambda i, j: (j, 0, 0))],
        out_specs=[pl.BlockSpec((rt, d), rows), pl.BlockSpec((rt, nb * f8), lambda i, j: (i, j)),
                   pl.BlockSpec((rt, d), rows), pl.BlockSpec((1, d), lambda i, j: (0, 0))],
        out_shape=[_sds((t, d), F32), _sds((t, N_DEV * f8), BF16), _sds((t, d), BF16), _sds((1, d), F32)],
        scratch_shapes=[pltpu.VMEM((rt, d), F32)],
        compiler_params=_cparams("arbitrary", "arbitrary"), name=name)(*deps, h, dout, gain, r, w1g, w2g)


def _lane_blocks(width):
    lb = min(LANE, width)
    return [slice(s, s + lb) for s in range(0, width, lb)]


def _conv_rows(src_ref, w_ref, offset, dst_ref, nblk, width, bias_ref=None):
    def blk(rb, carry):
        base = pl.multiple_of(rb * CHUNK, CHUNK)
        for l, ls in enumerate(_lane_blocks(width)):
            acc = jnp.zeros((CHUNK, ls.stop - ls.start), F32)
            if bias_ref is not None:
                acc = acc + bias_ref[:, ls]
            for k in range(CONV_WIDTH):
                acc = acc + w_ref[k:k + 1, ls] * src_ref[l, pl.ds(base + offset(k), CHUNK), :]
            dst_ref[l, pl.ds(base, CHUNK), :] = acc
        return carry

    lax.fori_loop(0, nblk, blk, 0)


def _to_lane_blocks(ref, row0, value):
    for l, ls in enumerate(_lane_blocks(value.shape[1])):
        ref[l, row0:row0 + value.shape[0], :] = value[:, ls]


def _from_lane_blocks(ref):
    return jnp.concatenate([ref[l] for l in range(ref.shape[0])], axis=1)


def _pool_counts(rows, window):
    return jnp.clip(rows - PAD_ROWS + 1, 1, window).astype(F32)


def _trailing_sum(v, window):
    s, sh = v, 1
    while sh < window:
        s = s + pltpu.roll(s, sh, 0)
        sh *= 2
    return s


def _leading_sum(v, window):
    s, sh, n = v, 1, v.shape[0]
    while sh < window:
        s = s + pltpu.roll(s, n - sh, 0)
        sh *= 2
    return s


def _norm_project(h_ref, g_ref, w_t_ref, u_ref, z_ref):
    hv = h_ref[...]
    u = (hv * lax.rsqrt(jnp.mean(hv * hv, axis=-1, keepdims=True) + EPS) * g_ref[...]).astype(BF16)
    u_ref[...] = u
    z_ref[...] = _dot_nt(u, w_t_ref[...])


def _project_back(dz_ref, w_t_ref, h_ref, g_ref, dres_ref, dh_ref, dg_ref, first):
    dx = _dot(dz_ref[...], w_t_ref[...])
    hv = h_ref[...]
    rstd = lax.rsqrt(jnp.mean(hv * hv, axis=-1, keepdims=True) + EPS)
    xh = hv * rstd

    @pl.when(first)
    def _():
        dg_ref[...] = jnp.zeros_like(dg_ref)

    dg_ref[...] += _rowsum(dx * xh)
    dxh = dx * g_ref[...]
    dh_ref[...] = dres_ref[...] + rstd * (dxh - xh * jnp.mean(dxh * xh, axis=-1, keepdims=True))


def _cp_mid_fwd(x, meta, gain, w_in_t, w_out, conv_w, conv_b, ln_g, ln_b, pool_w, pool_scale, *, name):
    seq, d = x.shape
    t = seq + CHUNK
    ein = w_in_t.shape[0]
    cd = conv_b.shape[1]
    pd = pool_scale.shape[1]
    pg = pd // len(POOL_WINDOWS)
    rt = _row_tile(t, 320)
    ntile = t // rt

    def body(x_ref, meta_ref, g_ref, wi_ref, wo_ref, cw_ref, cb_ref, lg_ref, lb_ref, pw_ref, ps_ref,
             ho_ref, o_ref, z_ref, u_ref, cv_ref, h0_ref, gext, pext, conv_s, hbuf, hsem):
        i = pl.program_id(0)
        slot = i % 2
        first_rows = pltpu.make_async_copy(x_ref.at[pl.ds(0, rt - CHUNK)], hbuf.at[0, pl.ds(CHUNK, rt - CHUNK)], hsem.at[0])

        def tile_rows(tile, to):
            return pltpu.make_async_copy(x_ref.at[pl.ds(pl.multiple_of(tile * rt - CHUNK, CHUNK), rt)], hbuf.at[to],
                                         hsem.at[to])

        @pl.when(i == 0)
        def _():
            first_rows.start()
            hbuf[0, 0:PAD_ROWS, :] = jnp.zeros((PAD_ROWS, d), F32)
            hbuf[0, PAD_ROWS:CHUNK, :] = meta_ref[...]
            _to_lane_blocks(gext, 0, jnp.zeros((HALO, cd), F32))
            pext[0:HALO, :] = jnp.zeros((HALO, pd), F32)

        @pl.when(i + 1 < ntile)
        def _():
            tile_rows(i + 1, 1 - slot).start()

        @pl.when(i == 0)
        def _():
            first_rows.wait()

        @pl.when(i > 0)
        def _():
            tile_rows(i, slot).wait()

        h_ref = hbuf.at[slot]
        h0_ref[...] = h_ref[...]
        _norm_project(h_ref, g_ref, wi_ref, u_ref, z_ref)

        _to_lane_blocks(gext, HALO, z_ref[:, 0:cd] * _sigmoid(z_ref[:, cd:2 * cd]))
        pext[HALO:HALO + rt, :] = z_ref[:, 2 * cd:]
        _conv_rows(gext, cw_ref, lambda k: k + HALO - (CONV_WIDTH - 1), conv_s, rt // CHUNK, cd, cb_ref)
        cv = _from_lane_blocks(conv_s)
        cv_ref[...] = cv
        xc = cv - jnp.mean(cv, axis=-1, keepdims=True)
        y = xc * lax.rsqrt(jnp.mean(xc * xc, axis=-1, keepdims=True) + EPS) * lg_ref[...] + lb_ref[...]
        rows = _row_ids(i, rt)
        a = jnp.where(rows >= PAD_ROWS, y * _sigmoid(y), 0.0)
        o_ref[:, 0:cd] = a.astype(BF16)
        for gi, window in enumerate(POOL_WINDOWS):
            ls = slice(gi * pg, (gi + 1) * pg)
            v = pext[:, ls]
            tm = _trailing_sum(v, window)[HALO:] / _pool_counts(rows, window) - v[HALO:]
            p = _dot(tm.astype(BF16), pw_ref[gi]) * ps_ref[:, ls]
            o_ref[:, cd + gi * pg:cd + (gi + 1) * pg] = p.astype(BF16)
        ho_ref[...] = h_ref[...] + _dot(o_ref[...], wo_ref[...])
        gext[:, 0:HALO, :] = gext[:, rt:rt + HALO, :]
        pext[0:HALO, :] = pext[rt:rt + HALO, :]

    nl, lb = len(_lane_blocks(cd)), min(LANE, cd)
    rows = lambda i: (i, 0)
    return pl.pallas_call(
        body, grid=(ntile,),
        in_specs=[pl.BlockSpec(memory_space=pl.ANY), _resident(meta.shape), _resident((1, d)), _resident(w_in_t.shape),
                  _resident(w_out.shape), _resident(conv_w.shape), _resident((1, cd)),
                  _resident((1, cd)), _resident((1, cd)), _resident(pool_w.shape), _resident((1, pd))],
        out_specs=[pl.BlockSpec((rt, d), rows), pl.BlockSpec((rt, cd + pd), rows), pl.BlockSpec((rt, ein), rows),
                   pl.BlockSpec((rt, d), rows), pl.BlockSpec((rt, cd), rows), pl.BlockSpec((rt, d), rows)],
        out_shape=[_sds((t, d), F32), _sds((t, cd + pd), BF16), _sds((t, ein), F32), _sds((t, d), BF16),
                   _sds((t, cd), F32), _sds((t, d), F32)],
        scratch_shapes=[pltpu.VMEM((nl, rt + HALO, lb), F32), pltpu.VMEM((rt + HALO, pd), F32),
                        pltpu.VMEM((nl, rt, lb), F32), pltpu.VMEM((2, rt, d), F32), pltpu.SemaphoreType.DMA((2,))],
        compiler_params=_cparams("arbitrary"), name=name)(x, meta, gain, w_in_t, w_out, conv_w, conv_b, ln_g, ln_b, pool_w,
                                                          pool_scale)


def _cp_mid_bwd(z, cv, h, gain, w_in_t, dh, w_out, conv_w, conv_b, ln_g, ln_b, pool_w, pool_scale, *, after=None, name):
    t, ein = z.shape
    cd = conv_b.shape[1]
    pd = pool_scale.shape[1]
    pg = pd // len(POOL_WINDOWS)
    rt = _row_tile(t, 320)
    ntile = t // rt
    per = rt // CHUNK
    dep_specs, deps = _dep_specs(after)

    def body(*refs):
        (z_ref, zh_ref, cv_ref, h_ref, g_ref, wi_ref, dh_ref, wo_ref, cw_ref, cb_ref, lg_ref, lb_ref, pw_ref, ps_ref,
         dx_ref, dfirst_ref, dg_ref, dz_ref, dcw_ref, dcb_ref, dlg_ref, dlb_ref, dpw_ref, dps_ref,
         gext, pext, conv_s, dcv, dsp, dhi, dx_sem) = refs[len(deps):]
        step = pl.program_id(0)
        tile = ntile - 1 - step
        slot = step % 2
        last_slot = (ntile - 1) % 2
        first_rows = pltpu.make_async_copy(dhi.at[last_slot, pl.ds(CHUNK, rt - CHUNK)], dx_ref.at[pl.ds(0, rt - CHUNK)],
                                           dx_sem.at[last_slot])

        def tile_rows(tile, slot):
            return pltpu.make_async_copy(dhi.at[slot], dx_ref.at[pl.ds(pl.multiple_of(tile * rt - CHUNK, CHUNK), rt)],
                                         dx_sem.at[slot])

        dcat = _dot_nt(dh_ref[...].astype(BF16), wo_ref[...])

        @pl.when(step >= 2)
        def _():
            tile_rows(tile + 2, slot).wait()

        @pl.when(step == 0)
        def _():
            for ref in (dcw_ref, dcb_ref, dlg_ref, dlb_ref, dpw_ref, dps_ref):
                ref[...] = jnp.zeros_like(ref)
            _to_lane_blocks(dcv, rt, jnp.zeros((HALO, cd), F32))
            dsp[rt:rt + HALO, :] = jnp.zeros((HALO, pd), F32)

        keep = jnp.where(tile > 0, 1.0, 0.0)
        zh = zh_ref[CHUNK - HALO:CHUNK, :]
        _to_lane_blocks(gext, 0, keep * zh[:, 0:cd] * _sigmoid(zh[:, cd:2 * cd]))
        pext[0:HALO, :] = keep * zh[:, 2 * cd:]
        za = z_ref[:, 0:cd]
        sg = _sigmoid(z_ref[:, cd:2 * cd])
        _to_lane_blocks(gext, HALO, za * sg)
        pext[HALO:HALO + rt, :] = z_ref[:, 2 * cd:]
        cv = cv_ref[...]
        xc = cv - jnp.mean(cv, axis=-1, keepdims=True)
        rstd = lax.rsqrt(jnp.mean(xc * xc, axis=-1, keepdims=True) + EPS)
        xh = xc * rstd
        y = xh * lg_ref[...] + lb_ref[...]
        sy = _sigmoid(y)
        rows = _row_ids(tile, rt)
        da = jnp.where(rows >= PAD_ROWS, dcat[:, 0:cd], 0.0)
        dy = da * (sy * (1.0 + y * (1.0 - sy)))
        dlg_ref[...] += _rowsum(dy * xh)
        dlb_ref[...] += _rowsum(dy)
        dxh = dy * lg_ref[...]
        dconv = rstd * (dxh - jnp.mean(dxh, axis=-1, keepdims=True) - xh * jnp.mean(dxh * xh, axis=-1, keepdims=True))
        dcb_ref[...] += _rowsum(dconv)
        _to_lane_blocks(dcv, 0, dconv)
        for l, ls in enumerate(_lane_blocks(cd)):
            def acc_rows(rb, accs, l=l):
                base = pl.multiple_of(rb * CHUNK, CHUNK)
                d_blk = dcv[l, pl.ds(base, CHUNK), :]
                out = []
                for k in range(CONV_WIDTH):
                    prod = d_blk * gext[l, pl.ds(base + k + HALO - (CONV_WIDTH - 1), CHUNK), :]
                    part = prod[0:8]
                    for s in range(8, CHUNK, 8):
                        part = part + prod[s:s + 8]
                    out.append(accs[k] + part)
                return tuple(out)

            zero = jnp.zeros((8, ls.stop - ls.start), F32)
            accs = lax.fori_loop(0, per, acc_rows, (zero,) * CONV_WIDTH)
            for k in range(CONV_WIDTH):
                dcw_ref[k:k + 1, ls] += _rowsum(accs[k])
        _conv_rows(dcv, cw_ref, lambda k: CONV_WIDTH - 1 - k, conv_s, per, cd)
        dglu = _from_lane_blocks(conv_s)
        dz_ref[:, 0:cd] = (dglu * sg).astype(BF16)
        dz_ref[:, cd:2 * cd] = (dglu * za * sg * (1.0 - sg)).astype(BF16)
        dcv[:, rt:rt + HALO, :] = dcv[:, 0:HALO, :]
        for gi, window in enumerate(POOL_WINDOWS):
            ls = slice(gi * pg, (gi + 1) * pg)
            v = pext[:, ls]
            cnt = _pool_counts(rows, window)
            tm = (_trailing_sum(v, window)[HALO:] / cnt - v[HALO:]).astype(BF16)
            dp = dcat[:, cd + gi * pg:cd + (gi + 1) * pg]
            dps_ref[:, ls] += _rowsum(dp * _dot(tm, pw_ref[gi]))
            dpl = (dp * ps_ref[:, ls]).astype(BF16)
            dpw_ref[gi] += _dot_tn(tm, dpl)
            dtm = _dot_nt(dpl, pw_ref[gi])
            dsp[0:rt, ls] = dtm / cnt
            dpin = _leading_sum(dsp[:, ls], window)[0:rt] - dtm
            dz_ref[:, 2 * cd + gi * pg:2 * cd + (gi + 1) * pg] = dpin.astype(BF16)
        dsp[rt:rt + HALO, :] = dsp[0:HALO, :]

        _project_back(dz_ref, wi_ref, h_ref, g_ref, dh_ref, dhi.at[slot], dg_ref, step == 0)

        @pl.when(tile > 0)
        def _():
            tile_rows(tile, slot).start()

        @pl.when(tile == 0)
        def _():
            first_rows.start()
            dfirst_ref[...] = dhi[last_slot, 0:CHUNK, :]
            if ntile > 1:
                tile_rows(1, 1 - last_slot).wait()
            first_rows.wait()

    d = h.shape[1]
    back = lambda i: (ntile - 1 - i, 0)
    halo_idx = lambda i: (jnp.maximum((ntile - 1 - i) * per - 1, 0), 0)
    const2 = lambda i: (0, 0)
    nl, lb = len(_lane_blocks(cd)), min(LANE, cd)
    return pl.pallas_call(
        body, grid=(ntile,),
        in_specs=dep_specs + [
                  pl.BlockSpec((rt, ein), back), pl.BlockSpec((CHUNK, ein), halo_idx), pl.BlockSpec((rt, cd), back),
                  pl.BlockSpec((rt, d), back),
                  _resident((1, d)), _resident(w_in_t.shape), pl.BlockSpec((rt, d), back), _resident(w_out.shape),
                  _resident(conv_w.shape), _resident((1, cd)), _resident((1, cd)), _resident((1, cd)),
                  _resident(pool_w.shape), _resident((1, pd))],
        out_specs=[pl.BlockSpec(memory_space=pl.ANY), pl.BlockSpec((CHUNK, d), const2), pl.BlockSpec((1, d), const2),
                   pl.BlockSpec((rt, ein), back), pl.BlockSpec(conv_w.shape, const2), pl.BlockSpec((1, cd), const2),
                   pl.BlockSpec((1, cd), const2), pl.BlockSpec((1, cd), const2),
                   pl.BlockSpec(pool_w.shape, lambda i: (0, 0, 0)), pl.BlockSpec((1, pd), const2)],
        out_shape=[_sds((t - CHUNK, d), F32), _sds((CHUNK, d), F32), _sds((1, d), F32),
                   _sds((t, ein), BF16), _sds(conv_w.shape, F32), _sds((1, cd), F32), _sds((1, cd), F32),
                   _sds((1, cd), F32), _sds(pool_w.shape, F32), _sds((1, pd), F32)],
        scratch_shapes=[pltpu.VMEM((nl, rt + HALO, lb), F32), pltpu.VMEM((rt + HALO, pd), F32), pltpu.VMEM((nl, rt, lb), F32),
                        pltpu.VMEM((nl, rt + HALO, lb), F32), pltpu.VMEM((rt + HALO, pd), F32), pltpu.VMEM((2, rt, d), F32),
                        pltpu.SemaphoreType.DMA((2,))],
        compiler_params=_cparams("arbitrary"), name=name)(*deps, z, z, cv, h, gain, w_in_t, dh, w_out, conv_w, conv_b, ln_g,
                                                          ln_b, pool_w, pool_scale)


def _log_decay(r, gw_ref, gb_ref, rows):
    gp = _dot(r.astype(BF16), gw_ref[...]) + gb_ref[...]
    log_sig = jnp.minimum(gp, 0.0) - jnp.log(1.0 + jnp.exp(-jnp.abs(gp)))
    return gp, jnp.where(rows >= PAD_ROWS, log_sig / GATE_NORM, 0.0)


def _tri(strict):
    r = lax.broadcasted_iota(jnp.int32, (CHUNK, CHUNK), 0)
    c = lax.broadcasted_iota(jnp.int32, (CHUNK, CHUNK), 1)
    return jnp.where(c < r if strict else c <= r, 1.0, 0.0).astype(BF16)


def _tri_dot(tri, a):
    hi = a.astype(BF16)
    rest = a - hi.astype(F32)
    mid = rest.astype(BF16)
    lo = (rest - mid.astype(F32)).astype(BF16)
    return _dot(tri, hi) + _dot(tri, mid) + _dot(tri, lo)


def _gla_mid_fwd(h, gain, w_in_blocks, w_out, gate_w, gate_b, head_g, *, name):
    t = h.shape[0]
    nblk, rpb = w_in_blocks.shape[:2]
    dk = gate_b.shape[1]
    hv = head_g.shape[1]
    hk = dk // HEADS
    dv = hv * HEADS
    r_at = 2 * dk + 2 * dv
    assert nblk * rpb == r_at + GATE_RANK
    zw = r_at + GATE_PAD
    rt = _row_tile(t, 320)
    per = rt // CHUNK
    scale = hk ** -0.5

    def body(h_ref, g_ref, wb_ref, wo_ref, gw_ref, gb_ref, hg_ref, ho_ref, o_ref, st_ref, z_ref, u_ref, wi_ref,
             s_ref, la_ref, dec_ref):
        i = pl.program_id(0)

        @pl.when(i == 0)
        def _():
            s_ref[...] = jnp.zeros_like(s_ref)
            for b in range(nblk):
                wi_ref[b * rpb:(b + 1) * rpb, :] = wb_ref[b]
            wi_ref[nblk * rpb:, :] = jnp.zeros((zw - nblk * rpb, wi_ref.shape[1]), BF16)

        _norm_project(h_ref, g_ref, wi_ref, u_ref, z_ref)

        _, la = _log_decay(z_ref[:, r_at:r_at + GATE_PAD], gw_ref, gb_ref, _row_ids(i, rt))
        la_ref[...] = la
        tri = _tri(False)

        def chunk_rows(c):
            return slice(c * CHUNK, (c + 1) * CHUNK)

        def decays(c, carry):
            rows = chunk_rows(c)
            la_c = la_ref[rows, :]
            cum = _tri_dot(tri, la_c)
            dec_ref[rows, :] = jnp.exp(_rowsum(la_c) - cum)
            return carry

        def states(c, carry):
            rows = chunk_rows(c)
            etot = jnp.exp(_rowsum(la_ref[rows, :]))
            for hd in range(HEADS):
                ks = slice(hd * hk, (hd + 1) * hk)
                kd = z_ref[rows, dk + hd * hk:dk + (hd + 1) * hk] * dec_ref[rows, ks]
                v = z_ref[rows, 2 * dk + hd * hv:2 * dk + (hd + 1) * hv]
                s_new = s_ref[hd] * etot[:, ks] + _dot_tn(v.astype(BF16), kd.astype(BF16))
                s_ref[hd] = s_new
                st_ref[c, hd] = s_new
            return carry

        def outputs(c, carry):
            rows = chunk_rows(c)
            for hd in range(HEADS):
                q = z_ref[rows, hd * hk:(hd + 1) * hk] * scale
                g = z_ref[rows, 2 * dk + dv + hd * hv:2 * dk + dv + (hd + 1) * hv]
                o = _dot_nt(q.astype(BF16), st_ref[c, hd].astype(BF16))
                on = o * lax.rsqrt(jnp.mean(o * o, axis=-1, keepdims=True) + EPS) * hg_ref[...]
                o_ref[rows, hd * hv:(hd + 1) * hv] = (on * (g * _sigmoid(g))).astype(BF16)
            return carry

        for phase in (decays, states, outputs):
            for c in range(per):
                phase(c, 0)
        ho_ref[...] = h_ref[...] + _dot(o_ref[...], wo_ref[...])

    d = h.shape[1]
    rows = lambda i: (i, 0)
    return pl.pallas_call(
        body, grid=(t // rt,),
        in_specs=[pl.BlockSpec((rt, d), rows), _resident((1, d)), _resident(w_in_blocks.shape), _resident(w_out.shape),
                  _resident(gate_w.shape), _resident((1, dk)), _resident((1, hv))],
        out_specs=[pl.BlockSpec((rt, d), rows), pl.BlockSpec((rt, dv), rows),
                   pl.BlockSpec((per, HEADS, hv, hk), lambda i: (i, 0, 0, 0)), pl.BlockSpec((rt, zw), rows),
                   pl.BlockSpec((rt, d), rows), pl.BlockSpec((zw, d), lambda i: (0, 0))],
        out_shape=[_sds((t, d), F32), _sds((t, dv), BF16), _sds((t // CHUNK, HEADS, hv, hk), F32), _sds((t, zw), F32),
                   _sds((t, d), BF16), _sds((zw, d), BF16)],
        scratch_shapes=[pltpu.VMEM((HEADS, hv, hk), F32), pltpu.VMEM((rt, dk), F32), pltpu.VMEM((rt, dk), F32)],
        compiler_params=_cparams("arbitrary"), name=name)(h, gain, w_in_blocks, w_out, gate_w, gate_b, head_g)


def _gla_mid_bwd(z, h, gain, w_in_t, dh, w_out, states, gate_w, gate_b, head_g, *, after=None, name):
    t = z.shape[0]
    dk = gate_b.shape[1]
    hv = head_g.shape[1]
    hk = dk // HEADS
    dv = hv * HEADS
    r_at = 2 * dk + 2 * dv
    rt = _row_tile(t, 320)
    ntile = t // rt
    per = rt // CHUNK
    scale = hk ** -0.5
    dep_specs, deps = _dep_specs(after)

    def body(*refs):
        (z_ref, h_ref, g_ref, wi_ref, dh_ref, wo_ref, st_ref, stp_ref, gw_ref, gb_ref, hg_ref,
         dhi_ref, dg_ref, dz_ref, dgw_ref, dgb_ref, dhg_ref,
         ds_ref, la_ref, dla_ref, dec_ref, dos_ref, e_ref, do_ref) = refs[len(deps):]
        step = pl.program_id(0)
        tile = ntile - 1 - step
        do_ref[...] = _dot_nt(dh_ref[...].astype(BF16), wo_ref[...])

        @pl.when(step == 0)
        def _():
            ds_ref[...] = jnp.zeros_like(ds_ref)
            dgw_ref[...] = jnp.zeros_like(dgw_ref)
            dgb_ref[...] = jnp.zeros_like(dgb_ref)
            dhg_ref[...] = jnp.zeros_like(dhg_ref)

        rows_id = _row_ids(tile, rt)
        r = z_ref[:, r_at:r_at + GATE_PAD]
        gp, la = _log_decay(r, gw_ref, gb_ref, rows_id)
        la_ref[...] = la
        tri, tri_strict = _tri(False), _tri(True)
        keep = jnp.where(tile > 0, 1.0, 0.0)

        def chunk_rows(c):
            return slice(c * CHUNK, (c + 1) * CHUNK)

        def recompute(c, dhg):
            rows = chunk_rows(c)
            la_c = la_ref[rows, :]
            cum = _tri_dot(tri, la_c)
            dec_ref[rows, :] = jnp.exp(_rowsum(la_c) - cum)
            for hd in range(HEADS):
                q = (z_ref[rows, hd * hk:(hd + 1) * hk] * scale).astype(BF16)
                g = z_ref[rows, 2 * dk + dv + hd * hv:2 * dk + dv + (hd + 1) * hv]
                s_b = st_ref[c, hd].astype(BF16)
                o = _dot_nt(q, s_b)
                rstd = lax.rsqrt(jnp.mean(o * o, axis=-1, keepdims=True) + EPS)
                oh = o * rstd
                sg = _sigmoid(g)
                d_og = do_ref[rows, hd * hv:(hd + 1) * hv]
                dz_ref[rows, 2 * dk + dv + hd * hv:2 * dk + dv + (hd + 1) * hv] = (
                    d_og * oh * hg_ref[...] * (sg * (1.0 + g * (1.0 - sg)))).astype(BF16)
                don = d_og * (g * sg)
                dhg = dhg + _rowsum(don * oh)
                doh = don * hg_ref[...]
                d_o = (rstd * (doh - oh * jnp.mean(doh * oh, axis=-1, keepdims=True))).astype(BF16)
                dos_ref[rows, hd * hv:(hd + 1) * hv] = d_o
                dz_ref[rows, hd * hk:(hd + 1) * hk] = (_dot(d_o, s_b) * scale).astype(BF16)
            return dhg

        def recurrence(cc, carry):
            c = per - 1 - cc
            rows = chunk_rows(c)
            etot = jnp.exp(_rowsum(la_ref[rows, :]))
            for hd in range(HEADS):
                ks = slice(hd * hk, (hd + 1) * hk)
                q = (z_ref[rows, hd * hk:(hd + 1) * hk] * scale).astype(BF16)
                dec = dec_ref[rows, ks]
                kd = z_ref[rows, dk + hd * hk:dk + (hd + 1) * hk] * dec
                v = z_ref[rows, 2 * dk + hd * hv:2 * dk + (hd + 1) * hv].astype(BF16)
                s_prev = st_ref[c - 1, hd] if c > 0 else keep * stp_ref[0, hd]
                ds_t = ds_ref[hd] + _dot_tn(dos_ref[rows, hd * hv:(hd + 1) * hv], q)
                ds_b = ds_t.astype(BF16)
                dkd = _dot(v, ds_b)
                dz_ref[rows, 2 * dk + hd * hv:2 * dk + (hd + 1) * hv] = _dot_nt(kd.astype(BF16), ds_b).astype(BF16)
                dtot = etot[:, ks] * _rowsum(ds_t * s_prev)
                ds_ref[hd] = ds_t * etot[:, ks]
                dz_ref[rows, dk + hd * hk:dk + (hd + 1) * hk] = (dkd * dec).astype(BF16)
                e_ref[rows, ks] = dkd * kd
                dla_ref[rows, ks] = jnp.broadcast_to(dtot, (CHUNK, hk))
            return carry

        def decay_cotangent(c, carry):
            rows = chunk_rows(c)
            dla_ref[rows, :] += _tri_dot(tri_strict, e_ref[rows, :])
            return carry

        dhg = jnp.zeros((1, hv), F32)
        for c in range(per):
            dhg = recompute(c, dhg)
        dhg_ref[...] += dhg
        for phase in (recurrence, decay_cotangent):
            for c in range(per):
                phase(c, 0)
        dla = jnp.where(rows_id >= PAD_ROWS, dla_ref[...], 0.0)
        dgp = dla * (1.0 / GATE_NORM) * (1.0 - _sigmoid(gp))
        dgb_ref[...] += _rowsum(dgp)
        dgp_b = dgp.astype(BF16)
        dgw_ref[...] += _dot_tn(r.astype(BF16), dgp_b)
        dz_ref[:, r_at:r_at + GATE_PAD] = _dot_nt(dgp_b, gw_ref[...]).astype(BF16)
        _project_back(dz_ref, wi_ref, h_ref, g_ref, dh_ref, dhi_ref, dg_ref, step == 0)

    d = h.shape[1]
    back = lambda i: (ntile - 1 - i, 0)
    const2 = lambda i: (0, 0)
    return pl.pallas_call(
        body, grid=(ntile,),
        in_specs=dep_specs + [
                  pl.BlockSpec((rt, z.shape[1]), back), pl.BlockSpec((rt, d), back), _resident((1, d)),
                  _resident(w_in_t.shape), pl.BlockSpec((rt, d), back), _resident(w_out.shape),
                  pl.BlockSpec((per, HEADS, hv, hk), lambda i: (ntile - 1 - i, 0, 0, 0)),
                  pl.BlockSpec((1, HEADS, hv, hk), lambda i: (jnp.maximum((ntile - 1 - i) * per - 1, 0), 0, 0, 0)),
                  _resident(gate_w.shape), _resident((1, dk)), _resident((1, hv))],
        out_specs=[pl.BlockSpec((rt, d), back), pl.BlockSpec((1, d), const2), pl.BlockSpec((rt, z.shape[1]), back),
                   pl.BlockSpec(gate_w.shape, const2), pl.BlockSpec((1, dk), const2), pl.BlockSpec((1, hv), const2)],
        out_shape=[_sds((t, d), F32), _sds((1, d), F32), _sds(z.shape, BF16), _sds(gate_w.shape, F32),
                   _sds((1, dk), F32), _sds((1, hv), F32)],
        scratch_shapes=[pltpu.VMEM((HEADS, hv, hk), F32), pltpu.VMEM((rt, dk), F32), pltpu.VMEM((rt, dk), F32),
                        pltpu.VMEM((rt, dk), F32), pltpu.VMEM((rt, dv), BF16), pltpu.VMEM((rt, dk), F32),
                        pltpu.VMEM((rt, dv), F32)],
        compiler_params=_cparams("arbitrary"), name=name)(*deps, z, h, gain, w_in_t, dh, w_out, states, states, gate_w,
                                                          gate_b, head_g)


def _adamw_math(w, g, m, v):
    m = ADAM_B1 * m + (1.0 - ADAM_B1) * g
    v = ADAM_B2 * v + (1.0 - ADAM_B2) * (g * g)
    m_hat = m / (1.0 - ADAM_B1 ** ADAM_STEP)
    v_hat = v / (1.0 - ADAM_B2 ** ADAM_STEP)
    return -ADAM_LR * (m_hat / (jnp.sqrt(v_hat) + ADAM_EPS) + ADAM_WD * w), m, v


N_CHIP = N_DEV // 2
BLOCK_ELEMS = 128 * 1024


def _my_slot():
    return 4 * lax.axis_index("x") + 2 * lax.axis_index("y") + lax.axis_index("c")


def _row_block(r, c):
    cap = max(8, BLOCK_ELEMS // (-(-c // LANE) * LANE))
    return max([b for b in range(8, r + 1, 8) if r % b == 0 and b <= cap] or [r])


def _blocks(r, c):
    rb = _row_block(r, c)
    if rb < r or r * c <= BLOCK_ELEMS:
        return rb, c
    return r, max([b for b in (512, 256, LANE) if c % b == 0 and r * b <= BLOCK_ELEMS] or [c])


def _reduce_adam(parts, w, m, v, *, after=None, name):
    nl, r, c = w.shape
    rb, cb = _blocks(r, c)
    dep_specs, deps = _dep_specs(after)

    def body(*refs):
        me = refs[0][0]
        refs = refs[1 + len(deps):]
        p_refs = refs[:2 * nl]
        w_ref, m_ref, v_ref, g_out, d_out, m_out, v_out = refs[2 * nl:]
        layer = pl.program_id(0)
        for li in range(nl):
            @pl.when(layer == li)
            def _(li=li):
                own_ref, land_ref = p_refs[2 * li], p_refs[2 * li + 1]
                mine = own_ref[...].astype(F32)
                g = None
                for dev in range(N_DEV):
                    term = jnp.where(me == dev, mine, land_ref[dev].astype(F32))
                    g = term if g is None else g + term
                g_out[...] = g
                d_out[...], m_out[...], v_out[...] = _adamw_math(w_ref[...], g, m_ref[...], v_ref[...])

    blk = pl.BlockSpec((None, rb, cb), lambda l, i, j, me: (l, i, j))
    p_specs = []
    for li in range(nl):
        p_specs += [
            pl.BlockSpec((None, rb, cb), lambda l, i, j, me, li=li: (me[0], jnp.where(l == li, i, 0), jnp.where(l == li, j, 0))),
            pl.BlockSpec((N_DEV, rb, cb), lambda l, i, j, me, li=li: (0, jnp.where(l == li, i, 0), jnp.where(l == li, j, 0)))]
    flat = [p for pair in parts for p in pair]
    grid_spec = pltpu.PrefetchScalarGridSpec(
        num_scalar_prefetch=1, grid=(nl, r // rb, c // cb), in_specs=dep_specs + p_specs + [blk, blk, blk],
        out_specs=[blk] * 4)
    return pl.pallas_call(
        body, grid_spec=grid_spec, out_shape=[_sds(w.shape, F32)] * 4,
        compiler_params=_cparams("arbitrary", "arbitrary", "arbitrary"), name=name)(
        _my_slot().reshape(1), *deps, *flat, w, m, v)


def _sum8(own, landed, *, name):
    def body(own_ref, land_ref, o_ref):
        me = _my_slot()
        total = None
        for dev in range(N_DEV):
            term = jnp.where(me == dev, own_ref[...], land_ref[dev])
            total = term if total is None else total + term
        o_ref[...] = total

    return pl.pallas_call(body, out_shape=_sds(own.shape, F32), name=name)(own, landed)


def _adam_small(own, landed, split, w, m, v, *, name):
    n = len(w)

    def body(*refs):
        own_refs, land_refs, w_refs, m_refs, v_refs = (refs[k * n:(k + 1) * n] for k in range(5))
        outs = refs[5 * n:]
        me = _my_slot()
        for k in range(n):
            mine = own_refs[k][me] if split[k] else own_refs[k][...]
            g = None
            for dev in range(N_DEV):
                term = jnp.where(me == dev, mine, land_refs[k][dev])
                g = term if g is None else g + term
            outs[4 * k][...] = g
            outs[4 * k + 1][...], outs[4 * k + 2][...], outs[4 * k + 3][...] = _adamw_math(
                w_refs[k][...], g, m_refs[k][...], v_refs[k][...])

    out = pl.pallas_call(body, out_shape=[_sds(a.shape, F32) for a in w for _ in range(4)],
                         compiler_params=pltpu.CompilerParams(vmem_limit_bytes=V7X_VMEM_LIMIT), name=name)(
        *own, *landed, *w, *m, *v)
    return [tuple(out[4 * k:4 * k + 4]) for k in range(n)]


_HBM = pl.BlockSpec(memory_space=pltpu.HBM)
_SEM = pl.BlockSpec(memory_space=pltpu.SEMAPHORE)
_DATAFLOW = pltpu.SideEffectType.DATAFLOW_SIDE_EFFECTING


def _plan_to_all(src, land):
    x, y, c = lax.axis_index("x"), lax.axis_index("y"), lax.axis_index("c")
    return [(src, land.at[_my_slot()], (x ^ ((d >> 2) & 1), y ^ ((d >> 1) & 1), c ^ (d & 1))) for d in range(1, N_DEV)]


def _plan_split_to_all(src, land):
    x, y, c = lax.axis_index("x"), lax.axis_index("y"), lax.axis_index("c")
    peers = [(x ^ ((d >> 2) & 1), y ^ ((d >> 1) & 1), c ^ (d & 1)) for d in range(1, N_DEV)]
    return [(src.at[4 * px + 2 * py + pc], land.at[_my_slot()], (px, py, pc)) for px, py, pc in peers]


_PLAN_COPIES = {_plan_to_all: N_DEV - 1, _plan_split_to_all: N_DEV - 1}


def _plans(plan, n):
    return list(plan) if isinstance(plan, (list, tuple)) else [plan] * n


def _exchange_copies(plan, ins, lands, send, recv):
    copies, sem = [], 0
    for p, src, land in zip(_plans(plan, len(lands)), ins, lands):
        for s, dst, dev in p(src, land):
            copies.append(pltpu.make_async_remote_copy(
                src_ref=s, dst_ref=dst, send_sem=send.at[sem], recv_sem=recv.at[sem],
                device_id=dev, device_id_type=pl.DeviceIdType.MESH))
            sem += 1
    return copies


def _place_own(srcs, *, after=None, name):
    dep_specs, deps = _dep_specs(after)
    n = len(srcs)
    arrays, in_specs, shapes = [], [], []
    for a, dtype in srcs:
        if isinstance(a, tuple):
            a, layer = a
            in_specs.append(pl.BlockSpec((None,) + a.shape[1:], lambda i, layer=layer: (layer, 0, 0)))
            shapes.append(a.shape[1:])
        else:
            in_specs.append(pl.BlockSpec(a.shape, lambda i: (0, 0)))
            shapes.append(a.shape)
        arrays.append(a)
    dtypes = [dtype for _, dtype in srcs]

    def body(*refs):
        refs = refs[len(deps):]
        a_refs, o_refs, cast_refs, sem = refs[:n], refs[n:2 * n], refs[2 * n:3 * n], refs[3 * n]
        me = _my_slot()
        copies = []
        for k in range(n):
            cast_refs[k][...] = a_refs[k][...].astype(dtypes[k])
            copies.append(pltpu.make_async_copy(cast_refs[k], o_refs[k].at[me], sem.at[k]))
            copies[-1].start()
        for cp in copies:
            cp.wait()

    return pl.pallas_call(
        body, grid=(1,), in_specs=dep_specs + in_specs, out_specs=[pl.BlockSpec(memory_space=pl.ANY)] * n,
        out_shape=[_sds((N_DEV,) + shape, dtype) for shape, dtype in zip(shapes, dtypes)],
        scratch_shapes=[pltpu.VMEM(shape, dtype) for shape, dtype in zip(shapes, dtypes)] + [pltpu.SemaphoreType.DMA((n,))],
        compiler_params=pltpu.CompilerParams(vmem_limit_bytes=V7X_VMEM_LIMIT), name=name)(*deps, *arrays)


def _plan_gather_first(land, _):
    x, y, c = lax.axis_index("x"), lax.axis_index("y"), lax.axis_index("c")
    mine = land.at[_my_slot()]
    return [(mine, mine, (x, y, 1 - c))] + [(mine, mine, (x ^ (d >> 1), y ^ (d & 1), c)) for d in range(1, N_CHIP)]


def _plan_gather_relay(land, _):
    x, y, c = lax.axis_index("x"), lax.axis_index("y"), lax.axis_index("c")
    slots = [land.at[4 * (x ^ (d >> 1)) + 2 * (y ^ (d & 1)) + c] for d in range(1, N_CHIP)]
    return [(s, s, (x, y, 1 - c)) for s in slots]


def _plan_gather_direct(land, _):
    return _plan_to_all(land.at[_my_slot()], land)


_PLAN_COPIES[_plan_gather_first] = N_CHIP
_PLAN_COPIES[_plan_gather_relay] = N_CHIP - 1
_PLAN_COPIES[_plan_gather_direct] = N_DEV - 1


def _exchange_start(plan, arrs, lands, *, after=None, name):
    bufs = list(lands) if arrs is None else list(arrs) + list(lands)
    n, nb = len(lands), len(bufs)
    nsem = sum(_PLAN_COPIES[p] for p in _plans(plan, n))
    dep_specs, deps = _dep_specs(after)

    def body(*refs):
        ins, land_refs = refs[:n], refs[nb - n:nb]
        send, recv = refs[nb + len(deps)], refs[nb + len(deps) + 1]
        for cp in _exchange_copies(plan, ins, land_refs, send, recv):
            cp.start()
        refs[-1][...] = jnp.zeros_like(refs[-1])

    out = pl.pallas_call(
        body, name=name,
        out_shape=(pltpu.SemaphoreType.DMA((nsem,)), pltpu.SemaphoreType.DMA((nsem,)),
                   *[pltpu.HBM(a.shape, a.dtype) for a in bufs], _sds((8, LANE), F32)),
        in_specs=[_HBM] * nb + dep_specs,
        out_specs=(_SEM, _SEM, *([_HBM] * nb), pl.BlockSpec(memory_space=pltpu.VMEM)),
        input_output_aliases={i: 2 + i for i in range(nb)},
        compiler_params=pltpu.CompilerParams(has_side_effects=_DATAFLOW),
    )(*[pltpu.with_memory_space_constraint(a, pltpu.HBM) for a in bufs], *deps)
    return (plan, n, out[0], out[1], list(out[2:2 + nb])), out[-1]


def _exchange_now(plan, lands, *, name):
    n = len(lands)
    nsem = sum(_PLAN_COPIES[p] for p in _plans(plan, n))

    def body(*refs):
        land_refs, send, recv = refs[n:2 * n], refs[2 * n], refs[2 * n + 1]
        copies = _exchange_copies(plan, land_refs, land_refs, send, recv)
        for cp in copies:
            cp.start()
        for cp in copies:
            cp.wait_send()
            cp.wait_recv()

    hbm = pl.BlockSpec(memory_space=pl.ANY)
    return pl.pallas_call(
        body, in_specs=[hbm] * n, out_specs=[hbm] * n, out_shape=[_sds(a.shape, a.dtype) for a in lands],
        input_output_aliases={i: i for i in range(n)},
        scratch_shapes=[pltpu.SemaphoreType.DMA((nsem,)), pltpu.SemaphoreType.DMA((nsem,))], name=name)(*lands)


def _exchange_wait(state, after, *, name):
    plan, n, send_sem, recv_sem, bufs = state
    nb = len(bufs)
    after = list(after) if isinstance(after, (list, tuple)) else [after]

    def body(*refs):
        ins, land_refs, send, recv = refs[:n], refs[nb - n:nb], refs[nb], refs[nb + 1]
        for cp in _exchange_copies(plan, ins, land_refs, send, recv):
            cp.wait_send()
            cp.wait_recv()

    out = pl.pallas_call(
        body, name=name, out_shape=[pltpu.HBM(a.shape, a.dtype) for a in bufs],
        in_specs=[_HBM] * nb + [_SEM, _SEM] + [pl.BlockSpec(memory_space=pl.ANY)] * len(after), out_specs=[_HBM] * nb,
        input_output_aliases={i: i for i in range(nb)},
        compiler_params=pltpu.CompilerParams(has_side_effects=_DATAFLOW),
    )(*bufs, send_sem, recv_sem, *after)
    return list(out[:n]), list(out[nb - n:])


def _dep_specs(after):
    return ([], []) if after is None else ([pl.BlockSpec(memory_space=pl.ANY)], [after])


def _undo_column_split(g):
    return jnp.transpose(g, (1, 0, 2)).reshape(g.shape[1], N_DEV * g.shape[2])


def _column_split(a):
    r, c = a.shape
    return jnp.transpose(a.reshape(r, N_DEV, c // N_DEV), (1, 0, 2))


class _WholeWeights:
    def __init__(self, groups):
        self.groups = groups
        self.grads = {}

    def fetch(self, group, after):
        return self.groups[group]

    def emit(self, group, grads):
        self.grads.update(grads)
        return None


def _local_step(x, target, replicated, src):
    d = x.shape[1]
    mix_g, ffn_g = replicated["mix_g"], replicated["ffn_g"]
    cp = src.fetch("cp", [])
    cp_mid = (cp["conv_w"], replicated["conv_b"], replicated["ln_g"], replicated["ln_b"], replicated["pool_w"],
              replicated["pool_scale"])

    h1, cat, z0, u0, cv0, h0 = _cp_mid_fwd(x, cp["meta"], mix_g[0:1], cp["cp_w_in_t"], cp["cp_w_out"], *cp_mid,
                                           name="cp_mixer")
    ffn0 = src.fetch("ffn0", h1)
    h2, uf0, rf0 = _ffn_fwd(h1, ffn_g[0:1], ffn0["w1"], ffn0["w2"], name="ffn0")
    gla = src.fetch("gla", h2)
    gla_mid = (gla["gate_w"], gla["gate_b"], gla["head_g"])
    h3, og, states, z1, u1, gla_w_in_t = _gla_mid_fwd(h2, mix_g[1:2], gla["gla_w_in_t"], gla["gla_w_out"], *gla_mid,
                                                      name="gla_mixer")
    ffn1 = src.fetch("ffn1", h3)
    dh4, uf1, rf1, loss, d_final_g = _ffn_fwd(h3, ffn_g[1:2], ffn1["w1"], ffn1["w2"],
                                              loss_head=(replicated["final_g"], target), name="ffn1_loss")

    dh3, dhh1, dob1, dffn_g1 = _ffn_bwd_x(h3, dh4, ffn_g[1:2], rf1, ffn1["w1"], ffn1["w2"], name="ffn1_bwd_x")
    sent = src.emit("ffn1_w1", dict(w1=_linear_bwd_w(uf1, dhh1, column_blocks=N_DEV, name="ffn1_dw1")))
    sent = src.emit("ffn1_w2", dict(w2=_linear_bwd_w(rf1, dob1, square_x=True, after=sent, name="ffn1_dw2")))
    d_gla_w_out = _linear_bwd_w(og, dh3, name="gla_out_dw")
    dh2, dmix_g1, dz1, d_gate_w, d_gate_b, d_head_g = _gla_mid_bwd(
        z1, h2, mix_g[1:2], gla_w_in_t, dh3, gla["gla_w_out"], states, *gla_mid, after=sent, name="gla_mixer_bwd")
    d_gla_w_in_t = _linear_bwd_w(dz1, u1, row_blocks=gla["gla_w_in_t"].shape[:2], name="gla_in_dw")
    sent = src.emit("gla", dict(gla_w_in_t=d_gla_w_in_t, gla_w_out=d_gla_w_out))
    dh1, dhh0, dob0, dffn_g0 = _ffn_bwd_x(h1, dh2, ffn_g[0:1], rf0, ffn0["w1"], ffn0["w2"], after=sent, name="ffn0_bwd_x")
    sent = src.emit("ffn0_w1", dict(w1=_linear_bwd_w(uf0, dhh0, column_blocks=N_DEV, name="ffn0_dw1")))
    sent = src.emit("ffn0_w2", dict(w2=_linear_bwd_w(rf0, dob0, square_x=True, after=sent, name="ffn0_dw2")))
    d_cp_w_out = _linear_bwd_w(cat, dh1, after=sent, name="cp_out_dw")
    sent = src.emit("cp_out", dict(cp_w_out=d_cp_w_out))
    dx, dh0_first, dmix_g0, dz0, d_conv_w, d_conv_b, d_ln_g, d_ln_b, d_pool_w, d_pool_scale = _cp_mid_bwd(
        z0, cv0, h0, mix_g[0:1], cp["cp_w_in_t"], dh1, cp["cp_w_out"], *cp_mid, after=sent, name="cp_mixer_bwd")
    d_cp_w_in_t = _linear_bwd_w(dz0, u0, name="cp_in_dw")

    small = dict(
        mix_g=jnp.concatenate([dmix_g0, dmix_g1]), ffn_g=jnp.concatenate([dffn_g0, dffn_g1]), conv_b=d_conv_b, ln_g=d_ln_g,
        ln_b=d_ln_b, pool_w=d_pool_w, pool_scale=d_pool_scale, final_g=d_final_g, meta=dh0_first[PAD_ROWS:], conv_w=d_conv_w,
        gate_w=d_gate_w, gate_b=d_gate_b, head_g=d_head_g)
    src.emit("cp", dict(cp_w_in_t=d_cp_w_in_t, small=small, loss=loss))
    return loss, dx, small


_REPLICATED = ("mix_norm_g", "ffn_norm_g", "cp_conv_b", "cp_ln_g", "cp_ln_b", "cp_pool_w", "cp_pool_scale", "final_norm_g")
_SMALL_SHARDED = ("meta_tokens", "cp_conv_w", "gla_gate_w2", "gla_gate_b", "gla_head_g")
_NAMES = ("meta_tokens", "mix_norm_g", "ffn_norm_g", "ffn_w1", "ffn_w2", "cp_w_in", "cp_conv_w", "cp_conv_b", "cp_ln_g",
          "cp_ln_b", "cp_pool_w", "cp_pool_scale", "cp_w_out", "gla_w_in", "gla_gate_w2", "gla_gate_b", "gla_head_g",
          "gla_w_out", "final_norm_g")
_SMALL_GRADS = ("mix_g", "ffn_g", "conv_b", "ln_g", "ln_b", "pool_w", "pool_scale", "final_g", "meta", "conv_w", "gate_w",
                "gate_b", "head_g")
_GROUPS = ("cp", "ffn0", "gla", "ffn1")
_TWO_LEG_GATHERS = ("cp", "ffn0", "ffn1")


class _Exchanges:
    def __init__(self, w, d):
        self.d = d
        small = [w[n].reshape(w[n].shape[-2:]) for n in _SMALL_SHARDED]
        self.small_shard_shapes = [w[n].shape for n in _SMALL_SHARDED]
        shards = dict(
            cp=[(w["cp_w_in"][0].T, BF16), (w["cp_w_out"][0], BF16)] + [(a, F32) for a in small],
            ffn0=[((w["ffn_w1"], 0), BF16), ((w["ffn_w2"], 0), BF16)],
            gla=[(w["gla_w_in"][0].T, BF16), (w["gla_w_out"][0], BF16)],
            ffn1=[((w["ffn_w1"], 1), BF16), ((w["ffn_w2"], 1), BF16)])
        self.gathers = {}
        self.sent = {}
        token = None
        for group in _GROUPS:
            lands = _place_own(shards[group], after=token, name=f"place_w_{group}")
            plan = _plan_gather_first if group in _TWO_LEG_GATHERS else _plan_gather_direct
            self.gathers[group], token = _exchange_start(plan, None, lands, after=token, name=f"start_w_{group}")
        self.token = token

    def fetch(self, group, after):
        d = self.d
        after = (list(after) if isinstance(after, (list, tuple)) else [after]) + [self.token]
        _, got = _exchange_wait(self.gathers[group], after, name=f"wait_w_{group}")
        if group in _TWO_LEG_GATHERS:
            got = _exchange_now(_plan_gather_relay, got, name=f"relay_w_{group}")
        if group in ("ffn0", "ffn1"):
            return dict(w1=got[0], w2=got[1])
        if group == "gla":
            return dict(gla_w_in_t=got[0], gla_w_out=got[1].reshape(d, d), gate_w=self.gate_w, gate_b=self.gate_b,
                        head_g=self.head_g)
        meta, conv_w, gate_w, self.gate_b, self.head_g = [_undo_column_split(a) for a in got[2:]]
        self.gate_w = jnp.pad(gate_w, ((0, GATE_PAD - GATE_RANK), (0, 0))).astype(BF16)
        return dict(cp_w_in_t=got[0].reshape(-1, d), cp_w_out=got[1].reshape(d, d), meta=meta,
                    conv_w=jnp.pad(conv_w, ((0, 1), (0, 0))))

    def emit(self, group, g):
        d = self.d
        if group in ("ffn0_w1", "ffn1_w1"):
            arrs = [g["w1"]]
        elif group in ("ffn0_w2", "ffn1_w2"):
            arrs = [g["w2"].reshape(N_DEV, -1, d)]
        elif group == "gla":
            arrs = [g["gla_w_in_t"], g["gla_w_out"].reshape(N_DEV, d // N_DEV, d)]
        elif group == "cp_out":
            arrs = [g["cp_w_out"].reshape(N_DEV, d // N_DEV, d)]
        else:
            s = dict(g["small"])
            s.update(pool_w=s["pool_w"][None], conv_w=s["conv_w"][:CONV_WIDTH], gate_w=s["gate_w"][:GATE_RANK])
            own = [s[n] for n in _SMALL_GRADS[:len(_REPLICATED)]]
            own += [_column_split(s[n]).reshape((N_DEV,) + shape)
                    for n, shape in zip(_SMALL_GRADS[len(_REPLICATED):], self.small_shard_shapes)]
            plans = [_plan_to_all] * len(_REPLICATED) + [_plan_split_to_all] * len(_SMALL_SHARDED)
            lands = [lax.empty((N_DEV,) + a.shape, F32) for a in own[:len(_REPLICATED)]]
            lands += [lax.empty(a.shape, F32) for a in own[len(_REPLICATED):]]
            own.append(g["loss"])
            plans.append(_plan_to_all)
            lands.append(lax.empty((N_DEV,) + g["loss"].shape, F32))
            self.small_sent, self.token = _exchange_start(plans, own, lands, after=self.token, name="start_g_small")
            arrs = [g["cp_w_in_t"].reshape(N_DEV, -1, d)]
        self.sent[group], self.token = _exchange_start(_plan_split_to_all, arrs, [lax.empty(a.shape, a.dtype) for a in arrs],
                                                       after=self.token, name=f"start_g_{group}")
        return self.token

    def finish(self, w, mom, var):
        out = {}
        after = self.token

        def landed(group):
            own, got = _exchange_wait(self.sent[group], after, name=f"wait_g_{group}")
            return list(zip(own, got))

        def adam(n, parts, behind=None, transposed=False):
            flip = (lambda a: jnp.transpose(a, (0, 2, 1))) if transposed else (lambda a: a)
            res = _reduce_adam(parts, flip(w[n]), flip(mom[n]), flip(var[n]), after=behind, name=f"adam_{n}")
            out[n] = tuple(flip(a) for a in res)
            return res[0]

        ffn1_w1 = landed("ffn1_w1")
        after = ffn1_w1[0][1]
        ffn1_w2 = landed("ffn1_w2")
        after = ffn1_w2[0][1]
        gla = landed("gla")
        after = adam("gla_w_in", [gla[0]], transposed=True)
        after = adam("gla_w_out", [gla[1]], after)
        ffn0_w1 = landed("ffn0_w1")
        after = adam("ffn_w1", [ffn0_w1[0], ffn1_w1[0]])
        ffn0_w2 = landed("ffn0_w2")
        after = adam("ffn_w2", [ffn0_w2[0], ffn1_w2[0]])
        small_own, small_landed = _exchange_wait(self.small_sent, after, name="wait_g_small")
        names = _REPLICATED + _SMALL_SHARDED
        split = [False] * len(_REPLICATED) + [True] * len(_SMALL_SHARDED)
        small_new = _adam_small(small_own[:-1], small_landed[:-1], split, [w[n] for n in names], [mom[n] for n in names],
                                [var[n] for n in names], name="adam_small")
        out.update(zip(names, small_new))
        out["loss"] = _sum8(small_own[-1], small_landed[-1], name="sum_loss")[0, 0]

        after = small_new[0][0]
        cp_out = landed("cp_out")
        after = adam("cp_w_out", [cp_out[0]])
        cp = landed("cp")
        adam("cp_w_in", [cp[0]], transposed=True)
        return out


def kernel(x, meta_tokens, mix_norm_g, ffn_norm_g, ffn_w1, ffn_w2, cp_w_in, cp_conv_w, cp_conv_b, cp_ln_g, cp_ln_b, cp_pool_w, cp_pool_scale, cp_w_out, gla_w_in, gla_gate_w2, gla_gate_b, gla_head_g, gla_w_out, final_norm_g, loss_target, m_meta_tokens, m_mix_norm_g, m_ffn_norm_g, m_ffn_w1, m_ffn_w2, m_cp_w_in, m_cp_conv_w, m_cp_conv_b, m_cp_ln_g, m_cp_ln_b, m_cp_pool_w, m_cp_pool_scale, m_cp_w_out, m_gla_w_in, m_gla_gate_w2, m_gla_gate_b, m_gla_head_g, m_gla_w_out, m_final_norm_g, v_meta_tokens, v_mix_norm_g, v_ffn_norm_g, v_ffn_w1, v_ffn_w2, v_cp_w_in, v_cp_conv_w, v_cp_conv_b, v_cp_ln_g, v_cp_ln_b, v_cp_pool_w, v_cp_pool_scale, v_cp_w_out, v_gla_w_in, v_gla_gate_w2, v_gla_gate_b, v_gla_head_g, v_gla_w_out, v_final_norm_g):
    w = dict(meta_tokens=meta_tokens, mix_norm_g=mix_norm_g, ffn_norm_g=ffn_norm_g, ffn_w1=ffn_w1, ffn_w2=ffn_w2,
             cp_w_in=cp_w_in, cp_conv_w=cp_conv_w, cp_conv_b=cp_conv_b, cp_ln_g=cp_ln_g, cp_ln_b=cp_ln_b,
             cp_pool_w=cp_pool_w, cp_pool_scale=cp_pool_scale, cp_w_out=cp_w_out, gla_w_in=gla_w_in,
             gla_gate_w2=gla_gate_w2, gla_gate_b=gla_gate_b, gla_head_g=gla_head_g, gla_w_out=gla_w_out,
             final_norm_g=final_norm_g.reshape(1, -1))
    mom = dict(meta_tokens=m_meta_tokens, mix_norm_g=m_mix_norm_g, ffn_norm_g=m_ffn_norm_g, ffn_w1=m_ffn_w1, ffn_w2=m_ffn_w2,
               cp_w_in=m_cp_w_in, cp_conv_w=m_cp_conv_w, cp_conv_b=m_cp_conv_b, cp_ln_g=m_cp_ln_g, cp_ln_b=m_cp_ln_b,
               cp_pool_w=m_cp_pool_w, cp_pool_scale=m_cp_pool_scale, cp_w_out=m_cp_w_out, gla_w_in=m_gla_w_in,
               gla_gate_w2=m_gla_gate_w2, gla_gate_b=m_gla_gate_b, gla_head_g=m_gla_head_g, gla_w_out=m_gla_w_out,
               final_norm_g=m_final_norm_g.reshape(1, -1))
    var = dict(meta_tokens=v_meta_tokens, mix_norm_g=v_mix_norm_g, ffn_norm_g=v_ffn_norm_g, ffn_w1=v_ffn_w1, ffn_w2=v_ffn_w2,
               cp_w_in=v_cp_w_in, cp_conv_w=v_cp_conv_w, cp_conv_b=v_cp_conv_b, cp_ln_g=v_cp_ln_g, cp_ln_b=v_cp_ln_b,
               cp_pool_w=v_cp_pool_w, cp_pool_scale=v_cp_pool_scale, cp_w_out=v_cp_w_out, gla_w_in=v_gla_w_in,
               gla_gate_w2=v_gla_gate_w2, gla_gate_b=v_gla_gate_b, gla_head_g=v_gla_head_g, gla_w_out=v_gla_w_out,
               final_norm_g=v_final_norm_g.reshape(1, -1))
    d = x.shape[-1]
    replicated = dict(mix_g=w["mix_norm_g"], ffn_g=w["ffn_norm_g"], conv_b=w["cp_conv_b"], ln_g=w["cp_ln_g"],
                      ln_b=w["cp_ln_b"], pool_w=w["cp_pool_w"][0].astype(BF16), pool_scale=w["cp_pool_scale"],
                      final_g=w["final_norm_g"])
    exchanges = _Exchanges(w, d)
    _, grad_x, _ = _local_step(x[0], loss_target[0], replicated, exchanges)
    out = exchanges.finish(w, mom, var)
    loss = out.pop("loss")

    def leaf(n, k):
        a = out[n][k]
        return a.reshape(-1) if n == "final_norm_g" else a

    return (loss, grad_x[None], *[leaf(n, 0) for n in _NAMES], *[leaf(n, 1) for n in _NAMES],
            *[leaf(n, 2) for n in _NAMES], *[leaf(n, 3) for n in _NAMES])
```

```python
import functools

import jax
import jax.numpy as jnp
from jax import lax
from jax.experimental import pallas as pl
from jax.experimental.pallas import tpu as pltpu

F32, BF16 = jnp.float32, jnp.bfloat16
N_DEV = 8
CHUNK = 64
N_META = 16
PAD_ROWS = CHUNK - N_META
HALO = 32
EPS = 1e-5
CONV_WIDTH = 31
POOL_WINDOWS = (2, 4, 8, 16)
HEADS = 4
GATE_RANK = 16
GATE_NORM = 16.0
GATE_PAD = 128
ADAM_LR, ADAM_B1, ADAM_B2, ADAM_EPS, ADAM_WD, ADAM_STEP = 0.001, 0.9, 0.999, 1e-08, 0.01, 10
V7X_VMEM_LIMIT = 56 * 2 ** 20
LANE = 128


def _cparams(*sem):
    return pltpu.CompilerParams(dimension_semantics=sem, vmem_limit_bytes=V7X_VMEM_LIMIT)


def _row_tile(t, cap):
    best = CHUNK
    for r in range(CHUNK, min(t, cap) + 1, CHUNK):
        if t % r == 0:
            best = r
    return best


def _resident(shape):
    return pl.BlockSpec(shape, lambda *_: (0,) * len(shape), pipeline_mode=pl.Buffered(1))


def _dot(a, b):
    return jnp.dot(a, b, preferred_element_type=F32)


def _dot_nt(a, b):
    return lax.dot_general(a, b, (((1,), (1,)), ((), ())), preferred_element_type=F32)


def _dot_tn(a, b):
    return lax.dot_general(a, b, (((0,), (0,)), ((), ())), preferred_element_type=F32)


def _rowsum(a):
    return jnp.sum(a, axis=0, keepdims=True)


def _sigmoid(a):
    return 1.0 / (1.0 + jnp.exp(-a))


def _row_ids(tile, rt):
    return tile * rt + lax.broadcasted_iota(jnp.int32, (rt, 1), 0)


def _sds(shape, dtype):
    return jax.ShapeDtypeStruct(shape, dtype)


DW_ROWS = 1024


def _linear_bwd_w(x, dy, *, square_x=False, column_blocks=None, row_blocks=None, after=None, name):
    t, k = x.shape
    n = dy.shape[1]
    cut_k = k > n and column_blocks is None
    assert cut_k or row_blocks is None
    width = k if cut_k else n
    blk = n // column_blocks if column_blocks else max(c for c in (640, 512, 384, 256, LANE) if width % c == 0)
    dep_specs, deps = _dep_specs(after)
    chunks = [(c0, min(c0 + DW_ROWS, t) - c0) for c0 in range(0, t, DW_ROWS)]
    kept = dy if cut_k else x

    def body(*refs):
        x_ref, dy_ref, o_ref, acc, kept_ref, kept_sem = refs[len(deps):]
        first = pl.program_id(0) == 0
        copies = [pltpu.make_async_copy((dy_ref if cut_k else x_ref).at[pl.ds(c0, size)], kept_ref.at[pl.ds(c0, size)],
                                        kept_sem.at[i]) for i, (c0, size) in enumerate(chunks)]

        @pl.when(first)
        def _():
            for copy in copies:
                copy.start()

        if cut_k:
            dy_ref = kept_ref
        else:
            x_ref = kept_ref
        for i, (c0, size) in enumerate(chunks):
            pl.when(first)(copies[i].wait)
            rows = slice(c0, c0 + size)
            xv = x_ref[rows, :]
            if square_x:
                xv = xv.astype(F32)
                xv = xv * xv
            part = _dot_tn(xv.astype(BF16), dy_ref[rows, :].astype(BF16))
            if c0 == 0:
                acc[...] = part
            else:
                acc[...] += part
        if row_blocks is None:
            o_ref[...] = acc[...].astype(BF16)
            return
        nb, rpb = row_blocks
        for s in range(width // blk):
            @pl.when(pl.program_id(0) == s)
            def _(s=s):
                for b in range(nb):
                    lo, hi = max(s * blk, b * rpb), min((s + 1) * blk, (b + 1) * rpb)
                    if lo < hi:
                        o_ref[b, lo - b * rpb:hi - b * rpb, :] = acc[lo - s * blk:hi - s * blk, :].astype(BF16)

    out_shape = _sds((k, n), BF16)
    if cut_k:
        in_specs = [pl.BlockSpec((t, blk), lambda j: (0, j)), pl.BlockSpec(memory_space=pl.ANY)]
        out_specs = pl.BlockSpec((blk, n), lambda j: (j, 0))
        acc_shape = (blk, n)
        if row_blocks:
            out_shape = _sds(row_blocks + (n,), BF16)
            out_specs = pl.BlockSpec(out_shape.shape, lambda j: (0, 0, 0))
    else:
        in_specs = [pl.BlockSpec(memory_space=pl.ANY), pl.BlockSpec((t, blk), lambda j: (0, j))]
        out_specs = pl.BlockSpec((k, blk), lambda j: (0, j))
        acc_shape = (k, blk)
        if column_blocks:
            out_specs = pl.BlockSpec((None, k, blk), lambda j: (j, 0, 0))
            out_shape = _sds((column_blocks, k, blk), BF16)
    return pl.pallas_call(
        body, grid=(width // blk,), in_specs=dep_specs + in_specs, out_specs=out_specs, out_shape=out_shape,
        scratch_shapes=[pltpu.VMEM(acc_shape, F32), pltpu.VMEM(kept.shape, kept.dtype), pltpu.SemaphoreType.DMA((len(chunks),))],
        compiler_params=_cparams("arbitrary"), name=name)(*deps, x, dy)


FFN_BLOCKS_PER_STEP = 2


def _ffn_fwd(h, gain, w1g, w2g, *, loss_head=None, name):
    t, d = h.shape
    f8 = w1g.shape[-1]
    rt = _row_tile(t, 832)
    nb = FFN_BLOCKS_PER_STEP
    nstep = N_DEV // nb

    def body(*refs):
        if loss_head is None:
            h_ref, g_ref, w1_ref, w2_ref, o_ref, u_ref, r_ref, acc_ref = refs
        else:
            (h_ref, g_ref, w1_ref, w2_ref, fg_ref, tgt_ref, o_ref, u_ref, r_ref, loss_ref, dfg_ref, acc_ref, t_ref,
             t_sem) = refs
        i, j = pl.program_id(0), pl.program_id(1)

        def target_rows(act):
            @pl.when(i == 0)
            def _():
                act(pltpu.make_async_copy(tgt_ref.at[pl.ds(0, rt - CHUNK)], t_ref.at[pl.ds(CHUNK, rt - CHUNK)], t_sem.at[0]))

            if t > rt:
                @pl.when(i > 0)
                def _():
                    act(pltpu.make_async_copy(tgt_ref.at[pl.ds(pl.multiple_of(i * rt - CHUNK, CHUNK), rt)], t_ref,
                                              t_sem.at[0]))

        @pl.when(j == 0)
        def _():
            if loss_head is not None:
                @pl.when(i == 0)
                def _():
                    t_ref[0:CHUNK, :] = jnp.zeros((CHUNK, d), F32)

                target_rows(lambda copy: copy.start())

            hv = h_ref[...]
            u_ref[...] = (hv * lax.rsqrt(jnp.mean(hv * hv, axis=-1, keepdims=True) + EPS) * g_ref[...]).astype(BF16)
            acc_ref[...] = jnp.zeros_like(acc_ref)

        part = None
        for b in range(nb):
            a = jnp.maximum(_dot(u_ref[...], w1_ref[b]), 0.0)
            r_ref[:, b * f8:(b + 1) * f8] = a.astype(BF16)
            term = _dot((a * a).astype(BF16), w2_ref[b])
            part = term if part is None else part + term
        acc_ref[...] += part

        @pl.when(j == nstep - 1)
        def _():
            y = h_ref[...] + acc_ref[...]
            if loss_head is None:
                o_ref[...] = y
                return

            @pl.when(i == 0)
            def _():
                loss_ref[...] = jnp.zeros_like(loss_ref)
                dfg_ref[...] = jnp.zeros_like(dfg_ref)

            target_rows(lambda copy: copy.wait())

            rstd = lax.rsqrt(jnp.mean(y * y, axis=-1, keepdims=True) + EPS)
            xh = y * rstd
            err = jnp.where(_row_ids(i, rt) >= CHUNK, xh * fg_ref[...] - t_ref[...], 0.0)
            loss_ref[...] += (0.5 / d) * jnp.sum(err * err)
            dy = err * (1.0 / d)
            dfg_ref[...] += _rowsum(dy * xh)
            dxh = dy * fg_ref[...]
            o_ref[...] = rstd * (dxh - xh * jnp.mean(dxh * xh, axis=-1, keepdims=True))

    rows = lambda i, j: (i, 0)
    in_specs = [pl.BlockSpec((rt, d), rows), _resident((1, d)),
                pl.BlockSpec((nb, d, f8), lambda i, j: (j, 0, 0)), pl.BlockSpec((nb, f8, d), lambda i, j: (j, 0, 0))]
    out_specs = [pl.BlockSpec((rt, d), rows), pl.BlockSpec((rt, d), rows), pl.BlockSpec((rt, nb * f8), lambda i, j: (i, j))]
    out_shape = [_sds((t, d), F32), _sds((t, d), BF16), _sds((t, N_DEV * f8), BF16)]
    args = [h, gain, w1g, w2g]
    scratch_shapes = [pltpu.VMEM((rt, d), F32)]
    if loss_head is not None:
        in_specs += [_resident((1, d)), pl.BlockSpec(memory_space=pl.ANY)]
        out_specs += [pl.BlockSpec((8, LANE), lambda i, j: (0, 0)), pl.BlockSpec((1, d), lambda i, j: (0, 0))]
        out_shape += [_sds((8, LANE), F32), _sds((1, d), F32)]
        args += list(loss_head)
        scratch_shapes += [pltpu.VMEM((rt, d), F32), pltpu.SemaphoreType.DMA((1,))]
    return pl.pallas_call(
        body, grid=(t // rt, nstep), in_specs=in_specs, out_specs=out_specs, out_shape=out_shape,
        scratch_shapes=scratch_shapes,
        compiler_params=_cparams("arbitrary" if loss_head is not None else "parallel", "arbitrary"), name=name)(*args)


def _ffn_bwd_x(h, dout, gain, r, w1g, w2g, *, after=None, name):
    t, d = h.shape
    f8 = w1g.shape[-1]
    rt = _row_tile(t, 832)
    nb = FFN_BLOCKS_PER_STEP
    last = N_DEV // nb - 1
    dep_specs, deps = _dep_specs(after)

    def body(*refs):
        h_ref, do_ref, g_ref, r_ref, w1_ref, w2_ref, dh_ref, dhh_ref, dob_ref, dg_ref, du_ref = refs[len(deps):]
        i, j = pl.program_id(0), pl.program_id(1)

        @pl.when(j == 0)
        def _():
            dob_ref[...] = do_ref[...].astype(BF16)
            du_ref[...] = jnp.zeros_like(du_ref)

        part = None
        for b in range(nb):
            cols = slice(b * f8, (b + 1) * f8)
            dhh = (_dot_nt(dob_ref[...], w2_ref[b]) * (2.0 * r_ref[:, cols].astype(F32))).astype(BF16)
            dhh_ref[:, cols] = dhh
            term = _dot_nt(dhh, w1_ref[b])
            part = term if part is None else part + term
        du_ref[...] += part

        @pl.when(j == last)
        def _():
            @pl.when(i == 0)
            def _():
                dg_ref[...] = jnp.zeros_like(dg_ref)

            hv = h_ref[...]
            rstd = lax.rsqrt(jnp.mean(hv * hv, axis=-1, keepdims=True) + EPS)
            xh = hv * rstd
            du = du_ref[...]
            dg_ref[...] += _rowsum(du * xh)
            dxh = du * g_ref[...]
            dh_ref[...] = do_ref[...] + rstd * (dxh - xh * jnp.mean(dxh * xh, axis=-1, keepdims=True))

    rows = lambda i, j: (i, 0)
    return pl.pallas_call(
        body, grid=(t // rt, N_DEV // nb),
        in_specs=dep_specs + [
                  pl.BlockSpec((rt, d), rows), pl.BlockSpec((rt, d), rows), _resident((1, d)),
                  pl.BlockSpec((rt, nb * f8), lambda i, j: (i, j)),
                  pl.BlockSpec((nb, d, f8), lambda i, j: (j, 0, 0)),
                  pl.BlockSpec((nb, f8, d), lambda i, j: (j, 0, 0))],
        out_specs=[pl.BlockSpec((rt, d), rows), pl.BlockSpec((rt, nb * f8), lambda i, j: (i, j)),
                   pl.BlockSpec((rt, d), rows), pl.BlockSpec((1, d), lambda i, j: (0, 0))],
        out_shape=[_sds((t, d), F32), _sds((t, N_DEV * f8), BF16), _sds((t, d), BF16), _sds((1, d), F32)],
        scratch_shapes=[pltpu.VMEM((rt, d), F32)],
        compiler_params=_cparams("arbitrary", "arbitrary"), name=name)(*deps, h, dout, gain, r, w1g, w2g)


def _lane_blocks(width):
    lb = min(LANE, width)
    return [slice(s, s + lb) for s in range(0, width, lb)]


def _conv_rows(src_ref, w_ref, offset, dst_ref, nblk, width, bias_ref=None):
    def blk(rb, carry):
        base = pl.multiple_of(rb * CHUNK, CHUNK)
        for l, ls in enumerate(_lane_blocks(width)):
            acc = jnp.zeros((CHUNK, ls.stop - ls.start), F32)
            if bias_ref is not None:
                acc = acc + bias_ref[:, ls]
            for k in range(CONV_WIDTH):
                acc = acc + w_ref[k:k + 1, ls] * src_ref[l, pl.ds(base + offset(k), CHUNK), :]
            dst_ref[l, pl.ds(base, CHUNK), :] = acc
        return carry

    lax.fori_loop(0, nblk, blk, 0)


def _to_lane_blocks(ref, row0, value):
    for l, ls in enumerate(_lane_blocks(value.shape[1])):
        ref[l, row0:row0 + value.shape[0], :] = value[:, ls]


def _from_lane_blocks(ref):
    return jnp.concatenate([ref[l] for l in range(ref.shape[0])], axis=1)


def _pool_counts(rows, window):
    return jnp.clip(rows - PAD_ROWS + 1, 1, window).astype(F32)


def _trailing_sum(v, window):
    s, sh = v, 1
    while sh < window:
        s = s + pltpu.roll(s, sh, 0)
        sh *= 2
    return s


def _leading_sum(v, window):
    s, sh, n = v, 1, v.shape[0]
    while sh < window:
        s = s + pltpu.roll(s, n - sh, 0)
        sh *= 2
    return s


def _norm_project(h_ref, g_ref, w_t_ref, u_ref, z_ref):
    hv = h_ref[...]
    u = (hv * lax.rsqrt(jnp.mean(hv * hv, axis=-1, keepdims=True) + EPS) * g_ref[...]).astype(BF16)
    u_ref[...] = u
    z_ref[...] = _dot_nt(u, w_t_ref[...])


def _project_back(dz_ref, w_t_ref, h_ref, g_ref, dres_ref, dh_ref, dg_ref, first):
    dx = _dot(dz_ref[...], w_t_ref[...])
    hv = h_ref[...]
    rstd = lax.rsqrt(jnp.mean(hv * hv, axis=-1, keepdims=True) + EPS)
    xh = hv * rstd

    @pl.when(first)
    def _():
        dg_ref[...] = jnp.zeros_like(dg_ref)

    dg_ref[...] += _rowsum(dx * xh)
    dxh = dx * g_ref[...]
    dh_ref[...] = dres_ref[...] + rstd * (dxh - xh * jnp.mean(dxh * xh, axis=-1, keepdims=True))


def _cp_mid_fwd(x, meta, gain, w_in_t, w_out, conv_w, conv_b, ln_g, ln_b, pool_w, pool_scale, *, name):
    seq, d = x.shape
    t = seq + CHUNK
    ein = w_in_t.shape[0]
    cd = conv_b.shape[1]
    pd = pool_scale.shape[1]
    pg = pd // len(POOL_WINDOWS)
    rt = _row_tile(t, 320)
    ntile = t // rt

    def body(x_ref, meta_ref, g_ref, wi_ref, wo_ref, cw_ref, cb_ref, lg_ref, lb_ref, pw_ref, ps_ref,
             ho_ref, o_ref, z_ref, u_ref, cv_ref, h0_ref, gext, pext, conv_s, hbuf, hsem):
        i = pl.program_id(0)
        slot = i % 2
        first_rows = pltpu.make_async_copy(x_ref.at[pl.ds(0, rt - CHUNK)], hbuf.at[0, pl.ds(CHUNK, rt - CHUNK)], hsem.at[0])

        def tile_rows(tile, to):
            return pltpu.make_async_copy(x_ref.at[pl.ds(pl.multiple_of(tile * rt - CHUNK, CHUNK), rt)], hbuf.at[to],
                                         hsem.at[to])

        @pl.when(i == 0)
        def _():
            first_rows.start()
            hbuf[0, 0:PAD_ROWS, :] = jnp.zeros((PAD_ROWS, d), F32)
            hbuf[0, PAD_ROWS:CHUNK, :] = meta_ref[...]
            _to_lane_blocks(gext, 0, jnp.zeros((HALO, cd), F32))
            pext[0:HALO, :] = jnp.zeros((HALO, pd), F32)

        @pl.when(i + 1 < ntile)
        def _():
            tile_rows(i + 1, 1 - slot).start()

        @pl.when(i == 0)
        def _():
            first_rows.wait()

        @pl.when(i > 0)
        def _():
            tile_rows(i, slot).wait()

        h_ref = hbuf.at[slot]
        h0_ref[...] = h_ref[...]
        _norm_project(h_ref, g_ref, wi_ref, u_ref, z_ref)

        _to_lane_blocks(gext, HALO, z_ref[:, 0:cd] * _sigmoid(z_ref[:, cd:2 * cd]))
        pext[HALO:HALO + rt, :] = z_ref[:, 2 * cd:]
        _conv_rows(gext, cw_ref, lambda k: k + HALO - (CONV_WIDTH - 1), conv_s, rt // CHUNK, cd, cb_ref)
        cv = _from_lane_blocks(conv_s)
        cv_ref[...] = cv
        xc = cv - jnp.mean(cv, axis=-1, keepdims=True)
        y = xc * lax.rsqrt(jnp.mean(xc * xc, axis=-1, keepdims=True) + EPS) * lg_ref[...] + lb_ref[...]
        rows = _row_ids(i, rt)
        a = jnp.where(rows >= PAD_ROWS, y * _sigmoid(y), 0.0)
        o_ref[:, 0:cd] = a.astype(BF16)
        for gi, window in enumerate(POOL_WINDOWS):
            ls = slice(gi * pg, (gi + 1) * pg)
            v = pext[:, ls]
            tm = _trailing_sum(v, window)[HALO:] / _pool_counts(rows, window) - v[HALO:]
            p = _dot(tm.astype(BF16), pw_ref[gi]) * ps_ref[:, ls]
            o_ref[:, cd + gi * pg:cd + (gi + 1) * pg] = p.astype(BF16)
        ho_ref[...] = h_ref[...] + _dot(o_ref[...], wo_ref[...])
        gext[:, 0:HALO, :] = gext[:, rt:rt + HALO, :]
        pext[0:HALO, :] = pext[rt:rt + HALO, :]

    nl, lb = len(_lane_blocks(cd)), min(LANE, cd)
    rows = lambda i: (i, 0)
    return pl.pallas_call(
        body, grid=(ntile,),
        in_specs=[pl.BlockSpec(memory_space=pl.ANY), _resident(meta.shape), _resident((1, d)), _resident(w_in_t.shape),
                  _resident(w_out.shape), _resident(conv_w.shape), _resident((1, cd)),
                  _resident((1, cd)), _resident((1, cd)), _resident(pool_w.shape), _resident((1, pd))],
        out_specs=[pl.BlockSpec((rt, d), rows), pl.BlockSpec((rt, cd + pd), rows), pl.BlockSpec((rt, ein), rows),
                   pl.BlockSpec((rt, d), rows), pl.BlockSpec((rt, cd), rows), pl.BlockSpec((rt, d), rows)],
        out_shape=[_sds((t, d), F32), _sds((t, cd + pd), BF16), _sds((t, ein), F32), _sds((t, d), BF16),
                   _sds((t, cd), F32), _sds((t, d), F32)],
        scratch_shapes=[pltpu.VMEM((nl, rt + HALO, lb), F32), pltpu.VMEM((rt + HALO, pd), F32),
                        pltpu.VMEM((nl, rt, lb), F32), pltpu.VMEM((2, rt, d), F32), pltpu.SemaphoreType.DMA((2,))],
        compiler_params=_cparams("arbitrary"), name=name)(x, meta, gain, w_in_t, w_out, conv_w, conv_b, ln_g, ln_b, pool_w,
                                                          pool_scale)


def _cp_mid_bwd(z, cv, h, gain, w_in_t, dh, w_out, conv_w, conv_b, ln_g, ln_b, pool_w, pool_scale, *, after=None, name):
    t, ein = z.shape
    cd = conv_b.shape[1]
    pd = pool_scale.shape[1]
    pg = pd // len(POOL_WINDOWS)
    rt = _row_tile(t, 320)
    ntile = t // rt
    per = rt // CHUNK
    dep_specs, deps = _dep_specs(after)

    def body(*refs):
        (z_ref, zh_ref, cv_ref, h_ref, g_ref, wi_ref, dh_ref, wo_ref, cw_ref, cb_ref, lg_ref, lb_ref, pw_ref, ps_ref,
         dx_ref, dfirst_ref, dg_ref, dz_ref, dcw_ref, dcb_ref, dlg_ref, dlb_ref, dpw_ref, dps_ref,
         gext, pext, conv_s, dcv, dsp, dhi, dx_sem) = refs[len(deps):]
        step = pl.program_id(0)
        tile = ntile - 1 - step
        slot = step % 2
        last_slot = (ntile - 1) % 2
        first_rows = pltpu.make_async_copy(dhi.at[last_slot, pl.ds(CHUNK, rt - CHUNK)], dx_ref.at[pl.ds(0, rt - CHUNK)],
                                           dx_sem.at[last_slot])

        def tile_rows(tile, slot):
            return pltpu.make_async_copy(dhi.at[slot], dx_ref.at[pl.ds(pl.multiple_of(tile * rt - CHUNK, CHUNK), rt)],
                                         dx_sem.at[slot])

        dcat = _dot_nt(dh_ref[...].astype(BF16), wo_ref[...])

        @pl.when(step >= 2)
        def _():
            tile_rows(tile + 2, slot).wait()

        @pl.when(step == 0)
        def _():
            for ref in (dcw_ref, dcb_ref, dlg_ref, dlb_ref, dpw_ref, dps_ref):
                ref[...] = jnp.zeros_like(ref)
            _to_lane_blocks(dcv, rt, jnp.zeros((HALO, cd), F32))
            dsp[rt:rt + HALO, :] = jnp.zeros((HALO, pd), F32)

        keep = jnp.where(tile > 0, 1.0, 0.0)
        zh = zh_ref[CHUNK - HALO:CHUNK, :]
        _to_lane_blocks(gext, 0, keep * zh[:, 0:cd] * _sigmoid(zh[:, cd:2 * cd]))
        pext[0:HALO, :] = keep * zh[:, 2 * cd:]
        za = z_ref[:, 0:cd]
        sg = _sigmoid(z_ref[:, cd:2 * cd])
        _to_lane_blocks(gext, HALO, za * sg)
        pext[HALO:HALO + rt, :] = z_ref[:, 2 * cd:]
        cv = cv_ref[...]
        xc = cv - jnp.mean(cv, axis=-1, keepdims=True)
        rstd = lax.rsqrt(jnp.mean(xc * xc, axis=-1, keepdims=True) + EPS)
        xh = xc * rstd
        y = xh * lg_ref[...] + lb_ref[...]
        sy = _sigmoid(y)
        rows = _row_ids(tile, rt)
        da = jnp.where(rows >= PAD_ROWS, dcat[:, 0:cd], 0.0)
        dy = da * (sy * (1.0 + y * (1.0 - sy)))
        dlg_ref[...] += _rowsum(dy * xh)
        dlb_ref[...] += _rowsum(dy)
        dxh = dy * lg_ref[...]
        dconv = rstd * (dxh - jnp.mean(dxh, axis=-1, keepdims=True) - xh * jnp.mean(dxh * xh, axis=-1, keepdims=True))
        dcb_ref[...] += _rowsum(dconv)
        _to_lane_blocks(dcv, 0, dconv)
        for l, ls in enumerate(_lane_blocks(cd)):
            def acc_rows(rb, accs, l=l):
                base = pl.multiple_of(rb * CHUNK, CHUNK)
                d_blk = dcv[l, pl.ds(base, CHUNK), :]
                out = []
                for k in range(CONV_WIDTH):
                    prod = d_blk * gext[l, pl.ds(base + k + HALO - (CONV_WIDTH - 1), CHUNK), :]
                    part = prod[0:8]
                    for s in range(8, CHUNK, 8):
                        part = part + prod[s:s + 8]
                    out.append(accs[k] + part)
                return tuple(out)

            zero = jnp.zeros((8, ls.stop - ls.start), F32)
            accs = lax.fori_loop(0, per, acc_rows, (zero,) * CONV_WIDTH)
            for k in range(CONV_WIDTH):
                dcw_ref[k:k + 1, ls] += _rowsum(accs[k])
        _conv_rows(dcv, cw_ref, lambda k: CONV_WIDTH - 1 - k, conv_s, per, cd)
        dglu = _from_lane_blocks(conv_s)
        dz_ref[:, 0:cd] = (dglu * sg).astype(BF16)
        dz_ref[:, cd:2 * cd] = (dglu * za * sg * (1.0 - sg)).astype(BF16)
        dcv[:, rt:rt + HALO, :] = dcv[:, 0:HALO, :]
        for gi, window in enumerate(POOL_WINDOWS):
            ls = slice(gi * pg, (gi + 1) * pg)
            v = pext[:, ls]
            cnt = _pool_counts(rows, window)
            tm = (_trailing_sum(v, window)[HALO:] / cnt - v[HALO:]).astype(BF16)
            dp = dcat[:, cd + gi * pg:cd + (gi + 1) * pg]
            dps_ref[:, ls] += _rowsum(dp * _dot(tm, pw_ref[gi]))
            dpl = (dp * ps_ref[:, ls]).astype(BF16)
            dpw_ref[gi] += _dot_tn(tm, dpl)
            dtm = _dot_nt(dpl, pw_ref[gi])
            dsp[0:rt, ls] = dtm / cnt
            dpin = _leading_sum(dsp[:, ls], window)[0:rt] - dtm
            dz_ref[:, 2 * cd + gi * pg:2 * cd + (gi + 1) * pg] = dpin.astype(BF16)
        dsp[rt:rt + HALO, :] = dsp[0:HALO, :]

        _project_back(dz_ref, wi_ref, h_ref, g_ref, dh_ref, dhi.at[slot], dg_ref, step == 0)

        @pl.when(tile > 0)
        def _():
            tile_rows(tile, slot).start()

        @pl.when(tile == 0)
        def _():
            first_rows.start()
            dfirst_ref[...] = dhi[last_slot, 0:CHUNK, :]
            if ntile > 1:
                tile_rows(1, 1 - last_slot).wait()
            first_rows.wait()

    d = h.shape[1]
    back = lambda i: (ntile - 1 - i, 0)
    halo_idx = lambda i: (jnp.maximum((ntile - 1 - i) * per - 1, 0), 0)
    const2 = lambda i: (0, 0)
    nl, lb = len(_lane_blocks(cd)), min(LANE, cd)
    return pl.pallas_call(
        body, grid=(ntile,),
        in_specs=dep_specs + [
                  pl.BlockSpec((rt, ein), back), pl.BlockSpec((CHUNK, ein), halo_idx), pl.BlockSpec((rt, cd), back),
                  pl.BlockSpec((rt, d), back),
                  _resident((1, d)), _resident(w_in_t.shape), pl.BlockSpec((rt, d), back), _resident(w_out.shape),
                  _resident(conv_w.shape), _resident((1, cd)), _resident((1, cd)), _resident((1, cd)),
                  _resident(pool_w.shape), _resident((1, pd))],
        out_specs=[pl.BlockSpec(memory_space=pl.ANY), pl.BlockSpec((CHUNK, d), const2), pl.BlockSpec((1, d), const2),
                   pl.BlockSpec((rt, ein), back), pl.BlockSpec(conv_w.shape, const2), pl.BlockSpec((1, cd), const2),
                   pl.BlockSpec((1, cd), const2), pl.BlockSpec((1, cd), const2),
                   pl.BlockSpec(pool_w.shape, lambda i: (0, 0, 0)), pl.BlockSpec((1, pd), const2)],
        out_shape=[_sds((t - CHUNK, d), F32), _sds((CHUNK, d), F32), _sds((1, d), F32),
                   _sds((t, ein), BF16), _sds(conv_w.shape, F32), _sds((1, cd), F32), _sds((1, cd), F32),
                   _sds((1, cd), F32), _sds(pool_w.shape, F32), _sds((1, pd), F32)],
        scratch_shapes=[pltpu.VMEM((nl, rt + HALO, lb), F32), pltpu.VMEM((rt + HALO, pd), F32), pltpu.VMEM((nl, rt, lb), F32),
                        pltpu.VMEM((nl, rt + HALO, lb), F32), pltpu.VMEM((rt + HALO, pd), F32), pltpu.VMEM((2, rt, d), F32),
                        pltpu.SemaphoreType.DMA((2,))],
        compiler_params=_cparams("arbitrary"), name=name)(*deps, z, z, cv, h, gain, w_in_t, dh, w_out, conv_w, conv_b, ln_g,
                                                          ln_b, pool_w, pool_scale)


def _log_decay(r, gw_ref, gb_ref, rows):
    gp = _dot(r.astype(BF16), gw_ref[...]) + gb_ref[...]
    log_sig = jnp.minimum(gp, 0.0) - jnp.log(1.0 + jnp.exp(-jnp.abs(gp)))
    return gp, jnp.where(rows >= PAD_ROWS, log_sig / GATE_NORM, 0.0)


def _tri(strict):
    r = lax.broadcasted_iota(jnp.int32, (CHUNK, CHUNK), 0)
    c = lax.broadcasted_iota(jnp.int32, (CHUNK, CHUNK), 1)
    return jnp.where(c < r if strict else c <= r, 1.0, 0.0).astype(BF16)


def _tri_dot(tri, a):
    hi = a.astype(BF16)
    rest = a - hi.astype(F32)
    mid = rest.astype(BF16)
    lo = (rest - mid.astype(F32)).astype(BF16)
    return _dot(tri, hi) + _dot(tri, mid) + _dot(tri, lo)


def _gla_mid_fwd(h, gain, w_in_blocks, w_out, gate_w, gate_b, head_g, *, name):
    t = h.shape[0]
    nblk, rpb = w_in_blocks.shape[:2]
    dk = gate_b.shape[1]
    hv = head_g.shape[1]
    hk = dk // HEADS
    dv = hv * HEADS
    r_at = 2 * dk + 2 * dv
    assert nblk * rpb == r_at + GATE_RANK
    zw = r_at + GATE_PAD
    rt = _row_tile(t, 320)
    per = rt // CHUNK
    scale = hk ** -0.5

    def body(h_ref, g_ref, wb_ref, wo_ref, gw_ref, gb_ref, hg_ref, ho_ref, o_ref, st_ref, z_ref, u_ref, wi_ref,
             s_ref, la_ref, dec_ref):
        i = pl.program_id(0)

        @pl.when(i == 0)
        def _():
            s_ref[...] = jnp.zeros_like(s_ref)
            for b in range(nblk):
                wi_ref[b * rpb:(b + 1) * rpb, :] = wb_ref[b]
            wi_ref[nblk * rpb:, :] = jnp.zeros((zw - nblk * rpb, wi_ref.shape[1]), BF16)

        _norm_project(h_ref, g_ref, wi_ref, u_ref, z_ref)

        _, la = _log_decay(z_ref[:, r_at:r_at + GATE_PAD], gw_ref, gb_ref, _row_ids(i, rt))
        la_ref[...] = la
        tri = _tri(False)

        def chunk_rows(c):
            return slice(c * CHUNK, (c + 1) * CHUNK)

        def decays(c, carry):
            rows = chunk_rows(c)
            la_c = la_ref[rows, :]
            cum = _tri_dot(tri, la_c)
            dec_ref[rows, :] = jnp.exp(_rowsum(la_c) - cum)
            return carry

        def states(c, carry):
            rows = chunk_rows(c)
            etot = jnp.exp(_rowsum(la_ref[rows, :]))
            for hd in range(HEADS):
                ks = slice(hd * hk, (hd + 1) * hk)
                kd = z_ref[rows, dk + hd * hk:dk + (hd + 1) * hk] * dec_ref[rows, ks]
                v = z_ref[rows, 2 * dk + hd * hv:2 * dk + (hd + 1) * hv]
                s_new = s_ref[hd] * etot[:, ks] + _dot_tn(v.astype(BF16), kd.astype(BF16))
                s_ref[hd] = s_new
                st_ref[c, hd] = s_new
            return carry

        def outputs(c, carry):
            rows = chunk_rows(c)
            for hd in range(HEADS):
                q = z_ref[rows, hd * hk:(hd + 1) * hk] * scale
                g = z_ref[rows, 2 * dk + dv + hd * hv:2 * dk + dv + (hd + 1) * hv]
                o = _dot_nt(q.astype(BF16), st_ref[c, hd].astype(BF16))
                on = o * lax.rsqrt(jnp.mean(o * o, axis=-1, keepdims=True) + EPS) * hg_ref[...]
                o_ref[rows, hd * hv:(hd + 1) * hv] = (on * (g * _sigmoid(g))).astype(BF16)
            return carry

        for phase in (decays, states, outputs):
            for c in range(per):
                phase(c, 0)
        ho_ref[...] = h_ref[...] + _dot(o_ref[...], wo_ref[...])

    d = h.shape[1]
    rows = lambda i: (i, 0)
    return pl.pallas_call(
        body, grid=(t // rt,),
        in_specs=[pl.BlockSpec((rt, d), rows), _resident((1, d)), _resident(w_in_blocks.shape), _resident(w_out.shape),
                  _resident(gate_w.shape), _resident((1, dk)), _resident((1, hv))],
        out_specs=[pl.BlockSpec((rt, d), rows), pl.BlockSpec((rt, dv), rows),
                   pl.BlockSpec((per, HEADS, hv, hk), lambda i: (i, 0, 0, 0)), pl.BlockSpec((rt, zw), rows),
                   pl.BlockSpec((rt, d), rows), pl.BlockSpec((zw, d), lambda i: (0, 0))],
        out_shape=[_sds((t, d), F32), _sds((t, dv), BF16), _sds((t // CHUNK, HEADS, hv, hk), F32), _sds((t, zw), F32),
                   _sds((t, d), BF16), _sds((zw, d), BF16)],
        scratch_shapes=[pltpu.VMEM((HEADS, hv, hk), F32), pltpu.VMEM((rt, dk), F32), pltpu.VMEM((rt, dk), F32)],
        compiler_params=_cparams("arbitrary"), name=name)(h, gain, w_in_blocks, w_out, gate_w, gate_b, head_g)


def _gla_mid_bwd(z, h, gain, w_in_t, dh, w_out, states, gate_w, gate_b, head_g, *, after=None, name):
    t = z.shape[0]
    dk = gate_b.shape[1]
    hv = head_g.shape[1]
    hk = dk // HEADS
    dv = hv * HEADS
    r_at = 2 * dk + 2 * dv
    rt = _row_tile(t, 320)
    ntile = t // rt
    per = rt // CHUNK
    scale = hk ** -0.5
    dep_specs, deps = _dep_specs(after)

    def body(*refs):
        (z_ref, h_ref, g_ref, wi_ref, dh_ref, wo_ref, st_ref, stp_ref, gw_ref, gb_ref, hg_ref,
         dhi_ref, dg_ref, dz_ref, dgw_ref, dgb_ref, dhg_ref,
         ds_ref, la_ref, dla_ref, dec_ref, dos_ref, e_ref, do_ref) = refs[len(deps):]
        step = pl.program_id(0)
        tile = ntile - 1 - step
        do_ref[...] = _dot_nt(dh_ref[...].astype(BF16), wo_ref[...])

        @pl.when(step == 0)
        def _():
            ds_ref[...] = jnp.zeros_like(ds_ref)
            dgw_ref[...] = jnp.zeros_like(dgw_ref)
            dgb_ref[...] = jnp.zeros_like(dgb_ref)
            dhg_ref[...] = jnp.zeros_like(dhg_ref)

        rows_id = _row_ids(tile, rt)
        r = z_ref[:, r_at:r_at + GATE_PAD]
        gp, la = _log_decay(r, gw_ref, gb_ref, rows_id)
        la_ref[...] = la
        tri, tri_strict = _tri(False), _tri(True)
        keep = jnp.where(tile > 0, 1.0, 0.0)

        def chunk_rows(c):
            return slice(c * CHUNK, (c + 1) * CHUNK)

        def recompute(c, dhg):
            rows = chunk_rows(c)
            la_c = la_ref[rows, :]
            cum = _tri_dot(tri, la_c)
            dec_ref[rows, :] = jnp.exp(_rowsum(la_c) - cum)
            for hd in range(HEADS):
                q = (z_ref[rows, hd * hk:(hd + 1) * hk] * scale).astype(BF16)
                g = z_ref[rows, 2 * dk + dv + hd * hv:2 * dk + dv + (hd + 1) * hv]
                s_b = st_ref[c, hd].astype(BF16)
                o = _dot_nt(q, s_b)
                rstd = lax.rsqrt(jnp.mean(o * o, axis=-1, keepdims=True) + EPS)
                oh = o * rstd
                sg = _sigmoid(g)
                d_og = do_ref[rows, hd * hv:(hd + 1) * hv]
                dz_ref[rows, 2 * dk + dv + hd * hv:2 * dk + dv + (hd + 1) * hv] = (
                    d_og * oh * hg_ref[...] * (sg * (1.0 + g * (1.0 - sg)))).astype(BF16)
                don = d_og * (g * sg)
                dhg = dhg + _rowsum(don * oh)
                doh = don * hg_ref[...]
                d_o = (rstd * (doh - oh * jnp.mean(doh * oh, axis=-1, keepdims=True))).astype(BF16)
                dos_ref[rows, hd * hv:(hd + 1) * hv] = d_o
                dz_ref[rows, hd * hk:(hd + 1) * hk] = (_dot(d_o, s_b) * scale).astype(BF16)
            return dhg

        def recurrence(cc, carry):
            c = per - 1 - cc
            rows = chunk_rows(c)
            etot = jnp.exp(_rowsum(la_ref[rows, :]))
            for hd in range(HEADS):
                ks = slice(hd * hk, (hd + 1) * hk)
                q = (z_ref[rows, hd * hk:(hd + 1) * hk] * scale).astype(BF16)
                dec = dec_ref[rows, ks]
                kd = z_ref[rows, dk + hd * hk:dk + (hd + 1) * hk] * dec
                v = z_ref[rows, 2 * dk + hd * hv:2 * dk + (hd + 1) * hv].astype(BF16)
                s_prev = st_ref[c - 1, hd] if c > 0 else keep * stp_ref[0, hd]
                ds_t = ds_ref[hd] + _dot_tn(dos_ref[rows, hd * hv:(hd + 1) * hv], q)
                ds_b = ds_t.astype(BF16)
                dkd = _dot(v, ds_b)
                dz_ref[rows, 2 * dk + hd * hv:2 * dk + (hd + 1) * hv] = _dot_nt(kd.astype(BF16), ds_b).astype(BF16)
                dtot = etot[:, ks] * _rowsum(ds_t * s_prev)
                ds_ref[hd] = ds_t * etot[:, ks]
                dz_ref[rows, dk + hd * hk:dk + (hd + 1) * hk] = (dkd * dec).astype(BF16)
                e_ref[rows, ks] = dkd * kd
                dla_ref[rows, ks] = jnp.broadcast_to(dtot, (CHUNK, hk))
            return carry

        def decay_cotangent(c, carry):
            rows = chunk_rows(c)
            dla_ref[rows, :] += _tri_dot(tri_strict, e_ref[rows, :])
            return carry

        dhg = jnp.zeros((1, hv), F32)
        for c in range(per):
            dhg = recompute(c, dhg)
        dhg_ref[...] += dhg
        for phase in (recurrence, decay_cotangent):
            for c in range(per):
                phase(c, 0)
        dla = jnp.where(rows_id >= PAD_ROWS, dla_ref[...], 0.0)
        dgp = dla * (1.0 / GATE_NORM) * (1.0 - _sigmoid(gp))
        dgb_ref[...] += _rowsum(dgp)
        dgp_b = dgp.astype(BF16)
        dgw_ref[...] += _dot_tn(r.astype(BF16), dgp_b)
        dz_ref[:, r_at:r_at + GATE_PAD] = _dot_nt(dgp_b, gw_ref[...]).astype(BF16)
        _project_back(dz_ref, wi_ref, h_ref, g_ref, dh_ref, dhi_ref, dg_ref, step == 0)

    d = h.shape[1]
    back = lambda i: (ntile - 1 - i, 0)
    const2 = lambda i: (0, 0)
    return pl.pallas_call(
        body, grid=(ntile,),
        in_specs=dep_specs + [
                  pl.BlockSpec((rt, z.shape[1]), back), pl.BlockSpec((rt, d), back), _resident((1, d)),
                  _resident(w_in_t.shape), pl.BlockSpec((rt, d), back), _resident(w_out.shape),
                  pl.BlockSpec((per, HEADS, hv, hk), lambda i: (ntile - 1 - i, 0, 0, 0)),
                  pl.BlockSpec((1, HEADS, hv, hk), lambda i: (jnp.maximum((ntile - 1 - i) * per - 1, 0), 0, 0, 0)),
                  _resident(gate_w.shape), _resident((1, dk)), _resident((1, hv))],
        out_specs=[pl.BlockSpec((rt, d), back), pl.BlockSpec((1, d), const2), pl.BlockSpec((rt, z.shape[1]), back),
                   pl.BlockSpec(gate_w.shape, const2), pl.BlockSpec((1, dk), const2), pl.BlockSpec((1, hv), const2)],
        out_shape=[_sds((t, d), F32), _sds((1, d), F32), _sds(z.shape, BF16), _sds(gate_w.shape, F32),
                   _sds((1, dk), F32), _sds((1, hv), F32)],
        scratch_shapes=[pltpu.VMEM((HEADS, hv, hk), F32), pltpu.VMEM((rt, dk), F32), pltpu.VMEM((rt, dk), F32),
                        pltpu.VMEM((rt, dk), F32), pltpu.VMEM((rt, dv), BF16), pltpu.VMEM((rt, dk), F32),
                        pltpu.VMEM((rt, dv), F32)],
        compiler_params=_cparams("arbitrary"), name=name)(*deps, z, h, gain, w_in_t, dh, w_out, states, states, gate_w,
                                                          gate_b, head_g)


def _adamw_math(w, g, m, v):
    m = ADAM_B1 * m + (1.0 - ADAM_B1) * g
    v = ADAM_B2 * v + (1.0 - ADAM_B2) * (g * g)
    m_hat = m / (1.0 - ADAM_B1 ** ADAM_STEP)
    v_hat = v / (1.0 - ADAM_B2 ** ADAM_STEP)
    return -ADAM_LR * (m_hat / (jnp.sqrt(v_hat) + ADAM_EPS) + ADAM_WD * w), m, v


N_CHIP = N_DEV // 2
BLOCK_ELEMS = 128 * 1024


def _my_slot():
    return 4 * lax.axis_index("x") + 2 * lax.axis_index("y") + lax.axis_index("c")


def _row_block(r, c):
    cap = max(8, BLOCK_ELEMS // (-(-c // LANE) * LANE))
    return max([b for b in range(8, r + 1, 8) if r % b == 0 and b <= cap] or [r])


def _blocks(r, c):
    rb = _row_block(r, c)
    if rb < r or r * c <= BLOCK_ELEMS:
        return rb, c
    return r, max([b for b in (512, 256, LANE) if c % b == 0 and r * b <= BLOCK_ELEMS] or [c])


def _reduce_adam(parts, w, m, v, *, after=None, name):
    nl, r, c = w.shape
    rb, cb = _blocks(r, c)
    dep_specs, deps = _dep_specs(after)

    def body(*refs):
        me = refs[0][0]
        refs = refs[1 + len(deps):]
        p_refs = refs[:2 * nl]
        w_ref, m_ref, v_ref, g_out, d_out, m_out, v_out = refs[2 * nl:]
        layer = pl.program_id(0)
        for li in range(nl):
            @pl.when(layer == li)
            def _(li=li):
                own_ref, land_ref = p_refs[2 * li], p_refs[2 * li + 1]
                mine = own_ref[...].astype(F32)
                g = None
                for dev in range(N_DEV):
                    term = jnp.where(me == dev, mine, land_ref[dev].astype(F32))
                    g = term if g is None else g + term
                g_out[...] = g
                d_out[...], m_out[...], v_out[...] = _adamw_math(w_ref[...], g, m_ref[...], v_ref[...])

    blk = pl.BlockSpec((None, rb, cb), lambda l, i, j, me: (l, i, j))
    p_specs = []
    for li in range(nl):
        p_specs += [
            pl.BlockSpec((None, rb, cb), lambda l, i, j, me, li=li: (me[0], jnp.where(l == li, i, 0), jnp.where(l == li, j, 0))),
            pl.BlockSpec((N_DEV, rb, cb), lambda l, i, j, me, li=li: (0, jnp.where(l == li, i, 0), jnp.where(l == li, j, 0)))]
    flat = [p for pair in parts for p in pair]
    grid_spec = pltpu.PrefetchScalarGridSpec(
        num_scalar_prefetch=1, grid=(nl, r // rb, c // cb), in_specs=dep_specs + p_specs + [blk, blk, blk],
        out_specs=[blk] * 4)
    return pl.pallas_call(
        body, grid_spec=grid_spec, out_shape=[_sds(w.shape, F32)] * 4,
        compiler_params=_cparams("arbitrary", "arbitrary", "arbitrary"), name=name)(
        _my_slot().reshape(1), *deps, *flat, w, m, v)


def _sum8(own, landed, *, name):
    def body(own_ref, land_ref, o_ref):
        me = _my_slot()
        total = None
        for dev in range(N_DEV):
            term = jnp.where(me == dev, own_ref[...], land_ref[dev])
            total = term if total is None else total + term
        o_ref[...] = total

    return pl.pallas_call(body, out_shape=_sds(own.shape, F32), name=name)(own, landed)


def _adam_small(own, landed, split, w, m, v, *, name):
    n = len(w)

    def body(*refs):
        own_refs, land_refs, w_refs, m_refs, v_refs = (refs[k * n:(k + 1) * n] for k in range(5))
        outs = refs[5 * n:]
        me = _my_slot()
        for k in range(n):
            mine = own_refs[k][me] if split[k] else own_refs[k][...]
            g = None
            for dev in range(N_DEV):
                term = jnp.where(me == dev, mine, land_refs[k][dev])
                g = term if g is None else g + term
            outs[4 * k][...] = g
            outs[4 * k + 1][...], outs[4 * k + 2][...], outs[4 * k + 3][...] = _adamw_math(
                w_refs[k][...], g, m_refs[k][...], v_refs[k][...])

    out = pl.pallas_call(body, out_shape=[_sds(a.shape, F32) for a in w for _ in range(4)],
                         compiler_params=pltpu.CompilerParams(vmem_limit_bytes=V7X_VMEM_LIMIT), name=name)(
        *own, *landed, *w, *m, *v)
    return [tuple(out[4 * k:4 * k + 4]) for k in range(n)]


_HBM = pl.BlockSpec(memory_space=pltpu.HBM)
_SEM = pl.BlockSpec(memory_space=pltpu.SEMAPHORE)
_DATAFLOW = pltpu.SideEffectType.DATAFLOW_SIDE_EFFECTING


def _plan_to_all(src, land):
    x, y, c = lax.axis_index("x"), lax.axis_index("y"), lax.axis_index("c")
    return [(src, land.at[_my_slot()], (x ^ ((d >> 2) & 1), y ^ ((d >> 1) & 1), c ^ (d & 1))) for d in range(1, N_DEV)]


def _plan_split_to_all(src, land):
    x, y, c = lax.axis_index("x"), lax.axis_index("y"), lax.axis_index("c")
    peers = [(x ^ ((d >> 2) & 1), y ^ ((d >> 1) & 1), c ^ (d & 1)) for d in range(1, N_DEV)]
    return [(src.at[4 * px + 2 * py + pc], land.at[_my_slot()], (px, py, pc)) for px, py, pc in peers]


_PLAN_COPIES = {_plan_to_all: N_DEV - 1, _plan_split_to_all: N_DEV - 1}


def _plans(plan, n):
    return list(plan) if isinstance(plan, (list, tuple)) else [plan] * n


def _exchange_copies(plan, ins, lands, send, recv):
    copies, sem = [], 0
    for p, src, land in zip(_plans(plan, len(lands)), ins, lands):
        for s, dst, dev in p(src, land):
            copies.append(pltpu.make_async_remote_copy(
                src_ref=s, dst_ref=dst, send_sem=send.at[sem], recv_sem=recv.at[sem],
                device_id=dev, device_id_type=pl.DeviceIdType.MESH))
            sem += 1
    return copies


def _place_own(srcs, *, after=None, name):
    dep_specs, deps = _dep_specs(after)
    n = len(srcs)
    arrays, in_specs, shapes = [], [], []
    for a, dtype in srcs:
        if isinstance(a, tuple):
            a, layer = a
            in_specs.append(pl.BlockSpec((None,) + a.shape[1:], lambda i, layer=layer: (layer, 0, 0)))
            shapes.append(a.shape[1:])
        else:
            in_specs.append(pl.BlockSpec(a.shape, lambda i: (0, 0)))
            shapes.append(a.shape)
        arrays.append(a)
    dtypes = [dtype for _, dtype in srcs]

    def body(*refs):
        refs = refs[len(deps):]
        a_refs, o_refs, cast_refs, sem = refs[:n], refs[n:2 * n], refs[2 * n:3 * n], refs[3 * n]
        me = _my_slot()
        copies = []
        for k in range(n):
            cast_refs[k][...] = a_refs[k][...].astype(dtypes[k])
            copies.append(pltpu.make_async_copy(cast_refs[k], o_refs[k].at[me], sem.at[k]))
            copies[-1].start()
        for cp in copies:
            cp.wait()

    return pl.pallas_call(
        body, grid=(1,), in_specs=dep_specs + in_specs, out_specs=[pl.BlockSpec(memory_space=pl.ANY)] * n,
        out_shape=[_sds((N_DEV,) + shape, dtype) for shape, dtype in zip(shapes, dtypes)],
        scratch_shapes=[pltpu.VMEM(shape, dtype) for shape, dtype in zip(shapes, dtypes)] + [pltpu.SemaphoreType.DMA((n,))],
        compiler_params=pltpu.CompilerParams(vmem_limit_bytes=V7X_VMEM_LIMIT), name=name)(*deps, *arrays)


def _plan_gather_first(land, _):
    x, y, c = lax.axis_index("x"), lax.axis_index("y"), lax.axis_index("c")
    mine = land.at[_my_slot()]
    return [(mine, mine, (x, y, 1 - c))] + [(mine, mine, (x ^ (d >> 1), y ^ (d & 1), c)) for d in range(1, N_CHIP)]


def _plan_gather_relay(land, _):
    x, y, c = lax.axis_index("x"), lax.axis_index("y"), lax.axis_index("c")
    slots = [land.at[4 * (x ^ (d >> 1)) + 2 * (y ^ (d & 1)) + c] for d in range(1, N_CHIP)]
    return [(s, s, (x, y, 1 - c)) for s in slots]


def _plan_gather_direct(land, _):
    return _plan_to_all(land.at[_my_slot()], land)


_PLAN_COPIES[_plan_gather_first] = N_CHIP
_PLAN_COPIES[_plan_gather_relay] = N_CHIP - 1
_PLAN_COPIES[_plan_gather_direct] = N_DEV - 1


def _exchange_start(plan, arrs, lands, *, after=None, name):
    bufs = list(lands) if arrs is None else list(arrs) + list(lands)
    n, nb = len(lands), len(bufs)
    nsem = sum(_PLAN_COPIES[p] for p in _plans(plan, n))
    dep_specs, deps = _dep_specs(after)

    def body(*refs):
        ins, land_refs = refs[:n], refs[nb - n:nb]
        send, recv = refs[nb + len(deps)], refs[nb + len(deps) + 1]
        for cp in _exchange_copies(plan, ins, land_refs, send, recv):
            cp.start()
        refs[-1][...] = jnp.zeros_like(refs[-1])

    out = pl.pallas_call(
        body, name=name,
        out_shape=(pltpu.SemaphoreType.DMA((nsem,)), pltpu.SemaphoreType.DMA((nsem,)),
                   *[pltpu.HBM(a.shape, a.dtype) for a in bufs], _sds((8, LANE), F32)),
        in_specs=[_HBM] * nb + dep_specs,
        out_specs=(_SEM, _SEM, *([_HBM] * nb), pl.BlockSpec(memory_space=pltpu.VMEM)),
        input_output_aliases={i: 2 + i for i in range(nb)},
        compiler_params=pltpu.CompilerParams(has_side_effects=_DATAFLOW),
    )(*[pltpu.with_memory_space_constraint(a, pltpu.HBM) for a in bufs], *deps)
    return (plan, n, out[0], out[1], list(out[2:2 + nb])), out[-1]


def _exchange_now(plan, lands, *, name):
    n = len(lands)
    nsem = sum(_PLAN_COPIES[p] for p in _plans(plan, n))

    def body(*refs):
        land_refs, send, recv = refs[n:2 * n], refs[2 * n], refs[2 * n + 1]
        copies = _exchange_copies(plan, land_refs, land_refs, send, recv)
        for cp in copies:
            cp.start()
        for cp in copies:
            cp.wait_send()
            cp.wait_recv()

    hbm = pl.BlockSpec(memory_space=pl.ANY)
    return pl.pallas_call(
        body, in_specs=[hbm] * n, out_specs=[hbm] * n, out_shape=[_sds(a.shape, a.dtype) for a in lands],
        input_output_aliases={i: i for i in range(n)},
        scratch_shapes=[pltpu.SemaphoreType.DMA((nsem,)), pltpu.SemaphoreType.DMA((nsem,))], name=name)(*lands)


def _exchange_wait(state, after, *, name):
    plan, n, send_sem, recv_sem, bufs = state
    nb = len(bufs)
    after = list(after) if isinstance(after, (list, tuple)) else [after]

    def body(*refs):
        ins, land_refs, send, recv = refs[:n], refs[nb - n:nb], refs[nb], refs[nb + 1]
        for cp in _exchange_copies(plan, ins, land_refs, send, recv):
            cp.wait_send()
            cp.wait_recv()

    out = pl.pallas_call(
        body, name=name, out_shape=[pltpu.HBM(a.shape, a.dtype) for a in bufs],
        in_specs=[_HBM] * nb + [_SEM, _SEM] + [pl.BlockSpec(memory_space=pl.ANY)] * len(after), out_specs=[_HBM] * nb,
        input_output_aliases={i: i for i in range(nb)},
        compiler_params=pltpu.CompilerParams(has_side_effects=_DATAFLOW),
    )(*bufs, send_sem, recv_sem, *after)
    return list(out[:n]), list(out[nb - n:])


def _dep_specs(after):
    return ([], []) if after is None else ([pl.BlockSpec(memory_space=pl.ANY)], [after])


def _undo_column_split(g):
    return jnp.transpose(g, (1, 0, 2)).reshape(g.shape[1], N_DEV * g.shape[2])


def _column_split(a):
    r, c = a.shape
    return jnp.transpose(a.reshape(r, N_DEV, c // N_DEV), (1, 0, 2))


class _WholeWeights:
    def __init__(self, groups):
        self.groups = groups
        self.grads = {}

    def fetch(self, group, after):
        return self.groups[group]

    def emit(self, group, grads):
        self.grads.update(grads)
        return None


def _local_step(x, target, replicated, src):
    d = x.shape[1]
    mix_g, ffn_g = replicated["mix_g"], replicated["ffn_g"]
    cp = src.fetch("cp", [])
    cp_mid = (cp["conv_w"], replicated["conv_b"], replicated["ln_g"], replicated["ln_b"], replicated["pool_w"],
              replicated["pool_scale"])

    h1, cat, z0, u0, cv0, h0 = _cp_mid_fwd(x, cp["meta"], mix_g[0:1], cp["cp_w_in_t"], cp["cp_w_out"], *cp_mid,
                                           name="cp_mixer")
    ffn0 = src.fetch("ffn0", h1)
    h2, uf0, rf0 = _ffn_fwd(h1, ffn_g[0:1], ffn0["w1"], ffn0["w2"], name="ffn0")
    gla = src.fetch("gla", h2)
    gla_mid = (gla["gate_w"], gla["gate_b"], gla["head_g"])
    h3, og, states, z1, u1, gla_w_in_t = _gla_mid_fwd(h2, mix_g[1:2], gla["gla_w_in_t"], gla["gla_w_out"], *gla_mid,
                                                      name="gla_mixer")
    ffn1 = src.fetch("ffn1", h3)
    dh4, uf1, rf1, loss, d_final_g = _ffn_fwd(h3, ffn_g[1:2], ffn1["w1"], ffn1["w2"],
                                              loss_head=(replicated["final_g"], target), name="ffn1_loss")

    dh3, dhh1, dob1, dffn_g1 = _ffn_bwd_x(h3, dh4, ffn_g[1:2], rf1, ffn1["w1"], ffn1["w2"], name="ffn1_bwd_x")
    sent = src.emit("ffn1_w1", dict(w1=_linear_bwd_w(uf1, dhh1, column_blocks=N_DEV, name="ffn1_dw1")))
    sent = src.emit("ffn1_w2", dict(w2=_linear_bwd_w(rf1, dob1, square_x=True, after=sent, name="ffn1_dw2")))
    d_gla_w_out = _linear_bwd_w(og, dh3, name="gla_out_dw")
    dh2, dmix_g1, dz1, d_gate_w, d_gate_b, d_head_g = _gla_mid_bwd(
        z1, h2, mix_g[1:2], gla_w_in_t, dh3, gla["gla_w_out"], states, *gla_mid, after=sent, name="gla_mixer_bwd")
    d_gla_w_in_t = _linear_bwd_w(dz1, u1, row_blocks=gla["gla_w_in_t"].shape[:2], name="gla_in_dw")
    sent = src.emit("gla", dict(gla_w_in_t=d_gla_w_in_t, gla_w_out=d_gla_w_out))
    dh1, dhh0, dob0, dffn_g0 = _ffn_bwd_x(h1, dh2, ffn_g[0:1], rf0, ffn0["w1"], ffn0["w2"], after=sent, name="ffn0_bwd_x")
    sent = src.emit("ffn0_w1", dict(w1=_linear_bwd_w(uf0, dhh0, column_blocks=N_DEV, name="ffn0_dw1")))
    sent = src.emit("ffn0_w2", dict(w2=_linear_bwd_w(rf0, dob0, square_x=True, after=sent, name="ffn0_dw2")))
    d_cp_w_out = _linear_bwd_w(cat, dh1, after=sent, name="cp_out_dw")
    sent = src.emit("cp_out", dict(cp_w_out=d_cp_w_out))
    dx, dh0_first, dmix_g0, dz0, d_conv_w, d_conv_b, d_ln_g, d_ln_b, d_pool_w, d_pool_scale = _cp_mid_bwd(
        z0, cv0, h0, mix_g[0:1], cp["cp_w_in_t"], dh1, cp["cp_w_out"], *cp_mid, after=sent, name="cp_mixer_bwd")
    d_cp_w_in_t = _linear_bwd_w(dz0, u0, name="cp_in_dw")

    small = dict(
        mix_g=jnp.concatenate([dmix_g0, dmix_g1]), ffn_g=jnp.concatenate([dffn_g0, dffn_g1]), conv_b=d_conv_b, ln_g=d_ln_g,
        ln_b=d_ln_b, pool_w=d_pool_w, pool_scale=d_pool_scale, final_g=d_final_g, meta=dh0_first[PAD_ROWS:], conv_w=d_conv_w,
        gate_w=d_gate_w, gate_b=d_gate_b, head_g=d_head_g)
    src.emit("cp", dict(cp_w_in_t=d_cp_w_in_t, small=small, loss=loss))
    return loss, dx, small


_REPLICATED = ("mix_norm_g", "ffn_norm_g", "cp_conv_b", "cp_ln_g", "cp_ln_b", "cp_pool_w", "cp_pool_scale", "final_norm_g")
_SMALL_SHARDED = ("meta_tokens", "cp_conv_w", "gla_gate_w2", "gla_gate_b", "gla_head_g")
_NAMES = ("meta_tokens", "mix_norm_g", "ffn_norm_g", "ffn_w1", "ffn_w2", "cp_w_in", "cp_conv_w", "cp_conv_b", "cp_ln_g",
          "cp_ln_b", "cp_pool_w", "cp_pool_scale", "cp_w_out", "gla_w_in", "gla_gate_w2", "gla_gate_b", "gla_head_g",
          "gla_w_out", "final_norm_g")
_SMALL_GRADS = ("mix_g", "ffn_g", "conv_b", "ln_g", "ln_b", "pool_w", "pool_scale", "final_g", "meta", "conv_w", "gate_w",
                "gate_b", "head_g")
_GROUPS = ("cp", "ffn0", "gla", "ffn1")
_TWO_LEG_GATHERS = ("cp", "ffn0", "ffn1")


class _Exchanges:
    def __init__(self, w, d):
        self.d = d
        small = [w[n].reshape(w[n].shape[-2:]) for n in _SMALL_SHARDED]
        self.small_shard_shapes = [w[n].shape for n in _SMALL_SHARDED]
        shards = dict(
            cp=[(w["cp_w_in"][0].T, BF16), (w["cp_w_out"][0], BF16)] + [(a, F32) for a in small],
            ffn0=[((w["ffn_w1"], 0), BF16), ((w["ffn_w2"], 0), BF16)],
            gla=[(w["gla_w_in"][0].T, BF16), (w["gla_w_out"][0], BF16)],
            ffn1=[((w["ffn_w1"], 1), BF16), ((w["ffn_w2"], 1), BF16)])
        self.gathers = {}
        self.sent = {}
        token = None
        for group in _GROUPS:
            lands = _place_own(shards[group], after=token, name=f"place_w_{group}")
            plan = _plan_gather_first if group in _TWO_LEG_GATHERS else _plan_gather_direct
            self.gathers[group], token = _exchange_start(plan, None, lands, after=token, name=f"start_w_{group}")
        self.token = token

    def fetch(self, group, after):
        d = self.d
        after = (list(after) if isinstance(after, (list, tuple)) else [after]) + [self.token]
        _, got = _exchange_wait(self.gathers[group], after, name=f"wait_w_{group}")
        if group in _TWO_LEG_GATHERS:
            got = _exchange_now(_plan_gather_relay, got, name=f"relay_w_{group}")
        if group in ("ffn0", "ffn1"):
            return dict(w1=got[0], w2=got[1])
        if group == "gla":
            return dict(gla_w_in_t=got[0], gla_w_out=got[1].reshape(d, d), gate_w=self.gate_w, gate_b=self.gate_b,
                        head_g=self.head_g)
        meta, conv_w, gate_w, self.gate_b, self.head_g = [_undo_column_split(a) for a in got[2:]]
        self.gate_w = jnp.pad(gate_w, ((0, GATE_PAD - GATE_RANK), (0, 0))).astype(BF16)
        return dict(cp_w_in_t=got[0].reshape(-1, d), cp_w_out=got[1].reshape(d, d), meta=meta,
                    conv_w=jnp.pad(conv_w, ((0, 1), (0, 0))))

    def emit(self, group, g):
        d = self.d
        if group in ("ffn0_w1", "ffn1_w1"):
            arrs = [g["w1"]]
        elif group in ("ffn0_w2", "ffn1_w2"):
            arrs = [g["w2"].reshape(N_DEV, -1, d)]
        elif group == "gla":
            arrs = [g["gla_w_in_t"], g["gla_w_out"].reshape(N_DEV, d // N_DEV, d)]
        elif group == "cp_out":
            arrs = [g["cp_w_out"].reshape(N_DEV, d // N_DEV, d)]
        else:
            s = dict(g["small"])
            s.update(pool_w=s["pool_w"][None], conv_w=s["conv_w"][:CONV_WIDTH], gate_w=s["gate_w"][:GATE_RANK])
            own = [s[n] for n in _SMALL_GRADS[:len(_REPLICATED)]]
            own += [_column_split(s[n]).reshape((N_DEV,) + shape)
                    for n, shape in zip(_SMALL_GRADS[len(_REPLICATED):], self.small_shard_shapes)]
            plans = [_plan_to_all] * len(_REPLICATED) + [_plan_split_to_all] * len(_SMALL_SHARDED)
            lands = [lax.empty((N_DEV,) + a.shape, F32) for a in own[:len(_REPLICATED)]]
            lands += [lax.empty(a.shape, F32) for a in own[len(_REPLICATED):]]
            own.append(g["loss"])
            plans.append(_plan_to_all)
            lands.append(lax.empty((N_DEV,) + g["loss"].shape, F32))
            self.small_sent, self.token = _exchange_start(plans, own, lands, after=self.token, name="start_g_small")
            arrs = [g["cp_w_in_t"].reshape(N_DEV, -1, d)]
        self.sent[group], self.token = _exchange_start(_plan_split_to_all, arrs, [lax.empty(a.shape, a.dtype) for a in arrs],
                                                       after=self.token, name=f"start_g_{group}")
        return self.token

    def finish(self, w, mom, var):
        out = {}
        after = self.token

        def landed(group):
            own, got = _exchange_wait(self.sent[group], after, name=f"wait_g_{group}")
            return list(zip(own, got))

        def adam(n, parts, behind=None, transposed=False):
            flip = (lambda a: jnp.transpose(a, (0, 2, 1))) if transposed else (lambda a: a)
            res = _reduce_adam(parts, flip(w[n]), flip(mom[n]), flip(var[n]), after=behind, name=f"adam_{n}")
            out[n] = tuple(flip(a) for a in res)
            return res[0]

        ffn1_w1 = landed("ffn1_w1")
        after = ffn1_w1[0][1]
        ffn1_w2 = landed("ffn1_w2")
        after = ffn1_w2[0][1]
        gla = landed("gla")
        after = adam("gla_w_in", [gla[0]], transposed=True)
        after = adam("gla_w_out", [gla[1]], after)
        ffn0_w1 = landed("ffn0_w1")
        after = adam("ffn_w1", [ffn0_w1[0], ffn1_w1[0]])
        ffn0_w2 = landed("ffn0_w2")
        after = adam("ffn_w2", [ffn0_w2[0], ffn1_w2[0]])
        small_own, small_landed = _exchange_wait(self.small_sent, after, name="wait_g_small")
        names = _REPLICATED + _SMALL_SHARDED
        split = [False] * len(_REPLICATED) + [True] * len(_SMALL_SHARDED)
        small_new = _adam_small(small_own[:-1], small_landed[:-1], split, [w[n] for n in names], [mom[n] for n in names],
                                [var[n] for n in names], name="adam_small")
        out.update(zip(names, small_new))
        out["loss"] = _sum8(small_own[-1], small_landed[-1], name="sum_loss")[0, 0]

        after = small_new[0][0]
        cp_out = landed("cp_out")
        after = adam("cp_w_out", [cp_out[0]])
        cp = landed("cp")
        adam("cp_w_in", [cp[0]], transposed=True)
        return out


def kernel(x, meta_tokens, mix_norm_g, ffn_norm_g, ffn_w1, ffn_w2, cp_w_in, cp_conv_w, cp_conv_b, cp_ln_g, cp_ln_b, cp_pool_w, cp_pool_scale, cp_w_out, gla_w_in, gla_gate_w2, gla_gate_b, gla_head_g, gla_w_out, final_norm_g, loss_target, m_meta_tokens, m_mix_norm_g, m_ffn_norm_g, m_ffn_w1, m_ffn_w2, m_cp_w_in, m_cp_conv_w, m_cp_conv_b, m_cp_ln_g, m_cp_ln_b, m_cp_pool_w, m_cp_pool_scale, m_cp_w_out, m_gla_w_in, m_gla_gate_w2, m_gla_gate_b, m_gla_head_g, m_gla_w_out, m_final_norm_g, v_meta_tokens, v_mix_norm_g, v_ffn_norm_g, v_ffn_w1, v_ffn_w2, v_cp_w_in, v_cp_conv_w, v_cp_conv_b, v_cp_ln_g, v_cp_ln_b, v_cp_pool_w, v_cp_pool_scale, v_cp_w_out, v_gla_w_in, v_gla_gate_w2, v_gla_gate_b, v_gla_head_g, v_gla_w_out, v_final_norm_g):
    w = dict(meta_tokens=meta_tokens, mix_norm_g=mix_norm_g, ffn_norm_g=ffn_norm_g, ffn_w1=ffn_w1, ffn_w2=ffn_w2,
             cp_w_in=cp_w_in, cp_conv_w=cp_conv_w, cp_conv_b=cp_conv_b, cp_ln_g=cp_ln_g, cp_ln_b=cp_ln_b,
             cp_pool_w=cp_pool_w, cp_pool_scale=cp_pool_scale, cp_w_out=cp_w_out, gla_w_in=gla_w_in,
             gla_gate_w2=gla_gate_w2, gla_gate_b=gla_gate_b, gla_head_g=gla_head_g, gla_w_out=gla_w_out,
             final_norm_g=final_norm_g.reshape(1, -1))
    mom = dict(meta_tokens=m_meta_tokens, mix_norm_g=m_mix_norm_g, ffn_norm_g=m_ffn_norm_g, ffn_w1=m_ffn_w1, ffn_w2=m_ffn_w2,
               cp_w_in=m_cp_w_in, cp_conv_w=m_cp_conv_w, cp_conv_b=m_cp_conv_b, cp_ln_g=m_cp_ln_g, cp_ln_b=m_cp_ln_b,
               cp_pool_w=m_cp_pool_w, cp_pool_scale=m_cp_pool_scale, cp_w_out=m_cp_w_out, gla_w_in=m_gla_w_in,
               gla_gate_w2=m_gla_gate_w2, gla_gate_b=m_gla_gate_b, gla_head_g=m_gla_head_g, gla_w_out=m_gla_w_out,
               final_norm_g=m_final_norm_g.reshape(1, -1))
    var = dict(meta_tokens=v_meta_tokens, mix_norm_g=v_mix_norm_g, ffn_norm_g=v_ffn_norm_g, ffn_w1=v_ffn_w1, ffn_w2=v_ffn_w2,
               cp_w_in=v_cp_w_in, cp_conv_w=v_cp_conv_w, cp_conv_b=v_cp_conv_b, cp_ln_g=v_cp_ln_g, cp_ln_b=v_cp_ln_b,
               cp_pool_w=v_cp_pool_w, cp_pool_scale=v_cp_pool_scale, cp_w_out=v_cp_w_out, gla_w_in=v_gla_w_in,
               gla_gate_w2=v_gla_gate_w2, gla_gate_b=v_gla_gate_b, gla_head_g=v_gla_head_g, gla_w_out=v_gla_w_out,
               final_norm_g=v_final_norm_g.reshape(1, -1))
    d = x.shape[-1]
    replicated = dict(mix_g=w["mix_norm_g"], ffn_g=w["ffn_norm_g"], conv_b=w["cp_conv_b"], ln_g=w["cp_ln_g"],
                      ln_b=w["cp_ln_b"], pool_w=w["cp_pool_w"][0].astype(BF16), pool_scale=w["cp_pool_scale"],
                      final_g=w["final_norm_g"])
    exchanges = _Exchanges(w, d)
    _, grad_x, _ = _local_step(x[0], loss_target[0], replicated, exchanges)
    out = exchanges.finish(w, mom, var)
    loss = out.pop("loss")

    def leaf(n, k):
        a = out[n][k]
        return a.reshape(-1) if n == "final_norm_g" else a

    return (loss, grad_x[None], *[leaf(n, 0) for n in _NAMES], *[leaf(n, 1) for n in _NAMES],
            *[leaf(n, 2) for n in _NAMES], *[leaf(n, 3) for n in _NAMES])
```

```python
import functools

import jax
import jax.numpy as jnp
from jax import lax
from jax.experimental import pallas as pl
from jax.experimental.pallas import tpu as pltpu

F32, BF16 = jnp.float32, jnp.bfloat16
N_DEV = 8
CHUNK = 64
N_META = 16
PAD_ROWS = CHUNK - N_META
HALO = 32
EPS = 1e-5
CONV_WIDTH = 31
POOL_WINDOWS = (2, 4, 8, 16)
HEADS = 4
GATE_RANK = 16
GATE_NORM = 16.0
GATE_PAD = 128
ADAM_LR, ADAM_B1, ADAM_B2, ADAM_EPS, ADAM_WD, ADAM_STEP = 0.001, 0.9, 0.999, 1e-08, 0.01, 10
V7X_VMEM_LIMIT = 56 * 2 ** 20
LANE = 128


def _cparams(*sem):
    return pltpu.CompilerParams(dimension_semantics=sem, vmem_limit_bytes=V7X_VMEM_LIMIT)


def _row_tile(t, cap):
    best = CHUNK
    for r in range(CHUNK, min(t, cap) + 1, CHUNK):
        if t % r == 0:
            best = r
    return best


def _resident(shape):
    return pl.BlockSpec(shape, lambda *_: (0,) * len(shape), pipeline_mode=pl.Buffered(1))


def _dot(a, b):
    return jnp.dot(a, b, preferred_element_type=F32)


def _dot_nt(a, b):
    return lax.dot_general(a, b, (((1,), (1,)), ((), ())), preferred_element_type=F32)


def _dot_tn(a, b):
    return lax.dot_general(a, b, (((0,), (0,)), ((), ())), preferred_element_type=F32)


def _rowsum(a):
    return jnp.sum(a, axis=0, keepdims=True)


def _sigmoid(a):
    return 1.0 / (1.0 + jnp.exp(-a))


def _row_ids(tile, rt):
    return tile * rt + lax.broadcasted_iota(jnp.int32, (rt, 1), 0)


def _sds(shape, dtype):
    return jax.ShapeDtypeStruct(shape, dtype)


DW_ROWS = 1024


def _linear_bwd_w(x, dy, *, square_x=False, column_blocks=None, row_blocks=None, after=None, name):
    t, k = x.shape
    n = dy.shape[1]
    cut_k = k > n and column_blocks is None
    assert cut_k or row_blocks is None
    width = k if cut_k else n
    blk = n // column_blocks if column_blocks else max(c for c in (640, 512, 384, 256, LANE) if width % c == 0)
    dep_specs, deps = _dep_specs(after)
    chunks = [(c0, min(c0 + DW_ROWS, t) - c0) for c0 in range(0, t, DW_ROWS)]
    kept = dy if cut_k else x

    def body(*refs):
        x_ref, dy_ref, o_ref, acc, kept_ref, kept_sem = refs[len(deps):]
        first = pl.program_id(0) == 0
        copies = [pltpu.make_async_copy((dy_ref if cut_k else x_ref).at[pl.ds(c0, size)], kept_ref.at[pl.ds(c0, size)],
                                        kept_sem.at[i]) for i, (c0, size) in enumerate(chunks)]

        if cut_k:
            dy_ref = kept_ref
        else:
            x_ref = kept_ref

        def contract(fetching):
            for i, (c0, size) in enumerate(chunks):
                if fetching:
                    copies[i].wait()
                rows = slice(c0, c0 + size)
                xv = x_ref[rows, :]
                if square_x:
                    xv = xv.astype(F32)
                    xv = xv * xv
                part = _dot_tn(xv.astype(BF16), dy_ref[rows, :].astype(BF16))
                if c0 == 0:
                    acc[...] = part
                else:
                    acc[...] += part

        @pl.when(first)
        def _():
            for copy in copies:
                copy.start()
            contract(True)

        @pl.when(jnp.logical_not(first))
        def _():
            contract(False)

        if row_blocks is None:
            o_ref[...] = acc[...].astype(BF16)
            return
        nb, rpb = row_blocks
        for s in range(width // blk):
            @pl.when(pl.program_id(0) == s)
            def _(s=s):
                for b in range(nb):
                    lo, hi = max(s * blk, b * rpb), min((s + 1) * blk, (b + 1) * rpb)
                    if lo < hi:
                        o_ref[b, lo - b * rpb:hi - b * rpb, :] = acc[lo - s * blk:hi - s * blk, :].astype(BF16)

    out_shape = _sds((k, n), BF16)
    if cut_k:
        in_specs = [pl.BlockSpec((t, blk), lambda j: (0, j)), pl.BlockSpec(memory_space=pl.ANY)]
        out_specs = pl.BlockSpec((blk, n), lambda j: (j, 0))
        acc_shape = (blk, n)
        if row_blocks:
            out_shape = _sds(row_blocks + (n,), BF16)
            out_specs = pl.BlockSpec(out_shape.shape, lambda j: (0, 0, 0))
    else:
        in_specs = [pl.BlockSpec(memory_space=pl.ANY), pl.BlockSpec((t, blk), lambda j: (0, j))]
        out_specs = pl.BlockSpec((k, blk), lambda j: (0, j))
        acc_shape = (k, blk)
        if column_blocks:
            out_specs = pl.BlockSpec((None, k, blk), lambda j: (j, 0, 0))
            out_shape = _sds((column_blocks, k, blk), BF16)
    return pl.pallas_call(
        body, grid=(width // blk,), in_specs=dep_specs + in_specs, out_specs=out_specs, out_shape=out_shape,
        scratch_shapes=[pltpu.VMEM(acc_shape, F32), pltpu.VMEM(kept.shape, kept.dtype), pltpu.SemaphoreType.DMA((len(chunks),))],
        compiler_params=_cparams("arbitrary"), name=name)(*deps, x, dy)


FFN_BLOCKS_PER_STEP = 2


def _ffn_fwd(h, gain, w1g, w2g, *, loss_head=None, name):
    t, d = h.shape
    f8 = w1g.shape[-1]
    rt = _row_tile(t, 832)
    nb = FFN_BLOCKS_PER_STEP
    nstep = N_DEV // nb

    def body(*refs):
        if loss_head is None:
            h_ref, g_ref, w1_ref, w2_ref, o_ref, u_ref, r_ref, acc_ref = refs
        else:
            (h_ref, g_ref, w1_ref, w2_ref, fg_ref, tgt_ref, o_ref, u_ref, r_ref, loss_ref, dfg_ref, acc_ref, t_ref,
             t_sem) = refs
        i, j = pl.program_id(0), pl.program_id(1)

        def target_rows(act):
            @pl.when(i == 0)
            def _():
                act(pltpu.make_async_copy(tgt_ref.at[pl.ds(0, rt - CHUNK)], t_ref.at[pl.ds(CHUNK, rt - CHUNK)], t_sem.at[0]))

            if t > rt:
                @pl.when(i > 0)
                def _():
                    act(pltpu.make_async_copy(tgt_ref.at[pl.ds(pl.multiple_of(i * rt - CHUNK, CHUNK), rt)], t_ref,
                                              t_sem.at[0]))

        @pl.when(j == 0)
        def _():
            if loss_head is not None:
                @pl.when(i == 0)
                def _():
                    t_ref[0:CHUNK, :] = jnp.zeros((CHUNK, d), F32)

                target_rows(lambda copy: copy.start())

            hv = h_ref[...]
            u_ref[...] = (hv * lax.rsqrt(jnp.mean(hv * hv, axis=-1, keepdims=True) + EPS) * g_ref[...]).astype(BF16)
            acc_ref[...] = jnp.zeros_like(acc_ref)

        part = None
        for b in range(nb):
            a = jnp.maximum(_dot(u_ref[...], w1_ref[b]), 0.0)
            r_ref[:, b * f8:(b + 1) * f8] = a.astype(BF16)
            term = _dot((a * a).astype(BF16), w2_ref[b])
            part = term if part is None else part + term
        acc_ref[...] += part

        @pl.when(j == nstep - 1)
        def _():
            y = h_ref[...] + acc_ref[...]
            if loss_head is None:
                o_ref[...] = y
                return

            @pl.when(i == 0)
            def _():
                loss_ref[...] = jnp.zeros_like(loss_ref)
                dfg_ref[...] = jnp.zeros_like(dfg_ref)

            target_rows(lambda copy: copy.wait())

            rstd = lax.rsqrt(jnp.mean(y * y, axis=-1, keepdims=True) + EPS)
            xh = y * rstd
            err = jnp.where(_row_ids(i, rt) >= CHUNK, xh * fg_ref[...] - t_ref[...], 0.0)
            loss_ref[...] += (0.5 / d) * jnp.sum(err * err)
            dy = err * (1.0 / d)
            dfg_ref[...] += _rowsum(dy * xh)
            dxh = dy * fg_ref[...]
            o_ref[...] = rstd * (dxh - xh * jnp.mean(dxh * xh, axis=-1, keepdims=True))

    rows = lambda i, j: (i, 0)
    in_specs = [pl.BlockSpec((rt, d), rows), _resident((1, d)),
                pl.BlockSpec((nb, d, f8), lambda i, j: (j, 0, 0)), pl.BlockSpec((nb, f8, d), lambda i, j: (j, 0, 0))]
    out_specs = [pl.BlockSpec((rt, d), rows), pl.BlockSpec((rt, d), rows), pl.BlockSpec((rt, nb * f8), lambda i, j: (i, j))]
    out_shape = [_sds((t, d), F32), _sds((t, d), BF16), _sds((t, N_DEV * f8), BF16)]
    args = [h, gain, w1g, w2g]
    scratch_shapes = [pltpu.VMEM((rt, d), F32)]
    if loss_head is not None:
        in_specs += [_resident((1, d)), pl.BlockSpec(memory_space=pl.ANY)]
        out_specs += [pl.BlockSpec((8, LANE), lambda i, j: (0, 0)), pl.BlockSpec((1, d), lambda i, j: (0, 0))]
        out_shape += [_sds((8, LANE), F32), _sds((1, d), F32)]
        args += list(loss_head)
        scratch_shapes += [pltpu.VMEM((rt, d), F32), pltpu.SemaphoreType.DMA((1,))]
    return pl.pallas_call(
        body, grid=(t // rt, nstep), in_specs=in_specs, out_specs=out_specs, out_shape=out_shape,
        scratch_shapes=scratch_shapes,
        compiler_params=_cparams("arbitrary" if loss_head is not None else "parallel", "arbitrary"), name=name)(*args)


def _ffn_bwd_x(h, dout, gain, r, w1g, w2g, *, after=None, name):
    t, d = h.shape
    f8 = w1g.shape[-1]
    rt = _row_tile(t, 832)
    nb = FFN_BLOCKS_PER_STEP
    last = N_DEV // nb - 1
    dep_specs, deps = _dep_specs(after)

    def body(*refs):
        h_ref, do_ref, g_ref, r_ref, w1_ref, w2_ref, dh_ref, dhh_ref, dob_ref, dg_ref, du_ref = refs[len(deps):]
        i, j = pl.program_id(0), pl.program_id(1)

        @pl.when(j == 0)
        def _():
            dob_ref[...] = do_ref[...].astype(BF16)
            du_ref[...] = jnp.zeros_like(du_ref)

        part = None
        for b in range(nb):
            cols = slice(b * f8, (b + 1) * f8)
            dhh = (_dot_nt(dob_ref[...], w2_ref[b]) * (2.0 * r_ref[:, cols].astype(F32))).astype(BF16)
            dhh_ref[:, cols] = dhh
            term = _dot_nt(dhh, w1_ref[b])
            part = term if part is None else part + term
        du_ref[...] += part

        @pl.when(j == last)
        def _():
            @pl.when(i == 0)
            def _():
                dg_ref[...] = jnp.zeros_like(dg_ref)

            hv = h_ref[...]
            rstd = lax.rsqrt(jnp.mean(hv * hv, axis=-1, keepdims=True) + EPS)
            xh = hv * rstd
            du = du_ref[...]
            dg_ref[...] += _rowsum(du * xh)
            dxh = du * g_ref[...]
            dh_ref[...] = do_ref[...] + rstd * (dxh - xh * jnp.mean(dxh * xh, axis=-1, keepdims=True))

    rows = lambda i, j: (i, 0)
    return pl.pallas_call(
        body, grid=(t // rt, N_DEV // nb),
        in_specs=dep_specs + [
                  pl.BlockSpec((rt, d), rows), pl.BlockSpec((rt, d), rows), _resident((1, d)),
                  pl.BlockSpec((rt, nb * f8), lambda i, j: (i, j)),
                  pl.BlockSpec((nb, d, f8), lambda i, j: (j, 0, 0)),
                  pl.BlockSpec((nb, f8, d), lambda i, j: (j, 0, 0))],
        out_specs=[pl.BlockSpec((rt, d), rows), pl.BlockSpec((rt, nb * f8), lambda i, j: (i, j)),
                   pl.BlockSpec((rt, d), rows), pl.BlockSpec((1, d), lambda i, j: (0, 0))],
        out_shape=[_sds((t, d), F32), _sds((t, N_DEV * f8), BF16), _sds((t, d), BF16), _sds((1, d), F32)],
        scratch_shapes=[pltpu.VMEM((rt, d), F32)],
        compiler_params=_cparams("arbitrary", "arbitrary"), name=name)(*deps, h, dout, gain, r, w1g, w2g)


def _lane_blocks(width):
    lb = min(LANE, width)
    return [slice(s, s + lb) for s in range(0, width, lb)]


def _conv_rows(src_ref, w_ref, offset, dst_ref, nblk, width, bias_ref=None):
    def blk(rb, carry):
        base = pl.multiple_of(rb * CHUNK, CHUNK)
        for l, ls in enumerate(_lane_blocks(width)):
            acc = jnp.zeros((CHUNK, ls.stop - ls.start), F32)
            if bias_ref is not None:
                acc = acc + bias_ref[:, ls]
            for k in range(CONV_WIDTH):
                acc = acc + w_ref[k:k + 1, ls] * src_ref[l, pl.ds(base + offset(k), CHUNK), :]
            dst_ref[l, pl.ds(base, CHUNK), :] = acc
        return carry

    lax.fori_loop(0, nblk, blk, 0)


def _to_lane_blocks(ref, row0, value):
    for l, ls in enumerate(_lane_blocks(value.shape[1])):
        ref[l, row0:row0 + value.shape[0], :] = value[:, ls]


def _from_lane_blocks(ref):
    return jnp.concatenate([ref[l] for l in range(ref.shape[0])], axis=1)


def _pool_counts(rows, window):
    return jnp.clip(rows - PAD_ROWS + 1, 1, window).astype(F32)


def _trailing_sum(v, window):
    s, sh = v, 1
    while sh < window:
        s = s + pltpu.roll(s, sh, 0)
        sh *= 2
    return s


def _leading_sum(v, window):
    s, sh, n = v, 1, v.shape[0]
    while sh < window:
        s = s + pltpu.roll(s, n - sh, 0)
        sh *= 2
    return s


def _norm_project(h_ref, g_ref, w_t_ref, u_ref, z_ref):
    hv = h_ref[...]
    u = (hv * lax.rsqrt(jnp.mean(hv * hv, axis=-1, keepdims=True) + EPS) * g_ref[...]).astype(BF16)
    u_ref[...] = u
    z_ref[...] = _dot_nt(u, w_t_ref[...])


def _project_back(dz_ref, w_t_ref, h_ref, g_ref, dres_ref, dh_ref, dg_ref, first):
    dx = _dot(dz_ref[...], w_t_ref[...])
    hv = h_ref[...]
    rstd = lax.rsqrt(jnp.mean(hv * hv, axis=-1, keepdims=True) + EPS)
    xh = hv * rstd

    @pl.when(first)
    def _():
        dg_ref[...] = jnp.zeros_like(dg_ref)

    dg_ref[...] += _rowsum(dx * xh)
    dxh = dx * g_ref[...]
    dh_ref[...] = dres_ref[...] + rstd * (dxh - xh * jnp.mean(dxh * xh, axis=-1, keepdims=True))


def _cp_mid_fwd(x, meta, gain, w_in_t, w_out, conv_w, conv_b, ln_g, ln_b, pool_w, pool_scale, *, name):
    seq, d = x.shape
    t = seq + CHUNK
    ein = w_in_t.shape[0]
    cd = conv_b.shape[1]
    pd = pool_scale.shape[1]
    pg = pd // len(POOL_WINDOWS)
    rt = _row_tile(t, 320)
    ntile = t // rt

    def body(x_ref, meta_ref, g_ref, wi_ref, wo_ref, cw_ref, cb_ref, lg_ref, lb_ref, pw_ref, ps_ref,
             ho_ref, o_ref, z_ref, u_ref, cv_ref, h0_ref, gext, pext, conv_s, hbuf, hsem):
        i = pl.program_id(0)
        slot = i % 2
        first_rows = pltpu.make_async_copy(x_ref.at[pl.ds(0, rt - CHUNK)], hbuf.at[0, pl.ds(CHUNK, rt - CHUNK)], hsem.at[0])

        def tile_rows(tile, to):
            return pltpu.make_async_copy(x_ref.at[pl.ds(pl.multiple_of(tile * rt - CHUNK, CHUNK), rt)], hbuf.at[to],
                                         hsem.at[to])

        @pl.when(i == 0)
        def _():
            first_rows.start()
            hbuf[0, 0:PAD_ROWS, :] = jnp.zeros((PAD_ROWS, d), F32)
            hbuf[0, PAD_ROWS:CHUNK, :] = meta_ref[...]
            _to_lane_blocks(gext, 0, jnp.zeros((HALO, cd), F32))
            pext[0:HALO, :] = jnp.zeros((HALO, pd), F32)

        @pl.when(i + 1 < ntile)
        def _():
            tile_rows(i + 1, 1 - slot).start()

        @pl.when(i == 0)
        def _():
            first_rows.wait()

        @pl.when(i > 0)
        def _():
            tile_rows(i, slot).wait()

        h_ref = hbuf.at[slot]
        h0_ref[...] = h_ref[...]
        _norm_project(h_ref, g_ref, wi_ref, u_ref, z_ref)

        _to_lane_blocks(gext, HALO, z_ref[:, 0:cd] * _sigmoid(z_ref[:, cd:2 * cd]))
        pext[HALO:HALO + rt, :] = z_ref[:, 2 * cd:]
        _conv_rows(gext, cw_ref, lambda k: k + HALO - (CONV_WIDTH - 1), conv_s, rt // CHUNK, cd, cb_ref)
        cv = _from_lane_blocks(conv_s)
        cv_ref[...] = cv
        xc = cv - jnp.mean(cv, axis=-1, keepdims=True)
        y = xc * lax.rsqrt(jnp.mean(xc * xc, axis=-1, keepdims=True) + EPS) * lg_ref[...] + lb_ref[...]
        rows = _row_ids(i, rt)
        a = jnp.where(rows >= PAD_ROWS, y * _sigmoid(y), 0.0)
        o_ref[:, 0:cd] = a.astype(BF16)
        for gi, window in enumerate(POOL_WINDOWS):
            ls = slice(gi * pg, (gi + 1) * pg)
            v = pext[:, ls]
            tm = _trailing_sum(v, window)[HALO:] / _pool_counts(rows, window) - v[HALO:]
            p = _dot(tm.astype(BF16), pw_ref[gi]) * ps_ref[:, ls]
            o_ref[:, cd + gi * pg:cd + (gi + 1) * pg] = p.astype(BF16)
        ho_ref[...] = h_ref[...] + _dot(o_ref[...], wo_ref[...])
        gext[:, 0:HALO, :] = gext[:, rt:rt + HALO, :]
        pext[0:HALO, :] = pext[rt:rt + HALO, :]

    nl, lb = len(_lane_blocks(cd)), min(LANE, cd)
    rows = lambda i: (i, 0)
    return pl.pallas_call(
        body, grid=(ntile,),
        in_specs=[pl.BlockSpec(memory_space=pl.ANY), _resident(meta.shape), _resident((1, d)), _resident(w_in_t.shape),
                  _resident(w_out.shape), _resident(conv_w.shape), _resident((1, cd)),
                  _resident((1, cd)), _resident((1, cd)), _resident(pool_w.shape), _resident((1, pd))],
        out_specs=[pl.BlockSpec((rt, d), rows), pl.BlockSpec((rt, cd + pd), rows), pl.BlockSpec((rt, ein), rows),
                   pl.BlockSpec((rt, d), rows), pl.BlockSpec((rt, cd), rows), pl.BlockSpec((rt, d), rows)],
        out_shape=[_sds((t, d), F32), _sds((t, cd + pd), BF16), _sds((t, ein), F32), _sds((t, d), BF16),
                   _sds((t, cd), F32), _sds((t, d), F32)],
        scratch_shapes=[pltpu.VMEM((nl, rt + HALO, lb), F32), pltpu.VMEM((rt + HALO, pd), F32),
                        pltpu.VMEM((nl, rt, lb), F32), pltpu.VMEM((2, rt, d), F32), pltpu.SemaphoreType.DMA((2,))],
        compiler_params=_cparams("arbitrary"), name=name)(x, meta, gain, w_in_t, w_out, conv_w, conv_b, ln_g, ln_b, pool_w,
                                                          pool_scale)


def _cp_mid_bwd(z, cv, h, gain, w_in_t, dh, w_out, conv_w, conv_b, ln_g, ln_b, pool_w, pool_scale, *, after=None, name):
    t, ein = z.shape
    cd = conv_b.shape[1]
    pd = pool_scale.shape[1]
    pg = pd // len(POOL_WINDOWS)
    rt = _row_tile(t, 320)
    ntile = t // rt
    per = rt // CHUNK
    dep_specs, deps = _dep_specs(after)

    def body(*refs):
        (z_ref, zh_ref, cv_ref, h_ref, g_ref, wi_ref, dh_ref, wo_ref, cw_ref, cb_ref, lg_ref, lb_ref, pw_ref, ps_ref,
         dx_ref, dfirst_ref, dg_ref, dz_ref, dcw_ref, dcb_ref, dlg_ref, dlb_ref, dpw_ref, dps_ref,
         gext, pext, conv_s, dcv, dsp, dhi, dx_sem) = refs[len(deps):]
        step = pl.program_id(0)
        tile = ntile - 1 - step
        slot = step % 2
        last_slot = (ntile - 1) % 2
        first_rows = pltpu.make_async_copy(dhi.at[last_slot, pl.ds(CHUNK, rt - CHUNK)], dx_ref.at[pl.ds(0, rt - CHUNK)],
                                           dx_sem.at[last_slot])

        def tile_rows(tile, slot):
            return pltpu.make_async_copy(dhi.at[slot], dx_ref.at[pl.ds(pl.multiple_of(tile * rt - CHUNK, CHUNK), rt)],
                                         dx_sem.at[slot])

        dcat = _dot_nt(dh_ref[...].astype(BF16), wo_ref[...])

        @pl.when(step >= 2)
        def _():
            tile_rows(tile + 2, slot).wait()

        @pl.when(step == 0)
        def _():
            for ref in (dcw_ref, dcb_ref, dlg_ref, dlb_ref, dpw_ref, dps_ref):
                ref[...] = jnp.zeros_like(ref)
            _to_lane_blocks(dcv, rt, jnp.zeros((HALO, cd), F32))
            dsp[rt:rt + HALO, :] = jnp.zeros((HALO, pd), F32)

        keep = jnp.where(tile > 0, 1.0, 0.0)
        zh = zh_ref[CHUNK - HALO:CHUNK, :]
        _to_lane_blocks(gext, 0, keep * zh[:, 0:cd] * _sigmoid(zh[:, cd:2 * cd]))
        pext[0:HALO, :] = keep * zh[:, 2 * cd:]
        za = z_ref[:, 0:cd]
        sg = _sigmoid(z_ref[:, cd:2 * cd])
        _to_lane_blocks(gext, HALO, za * sg)
        pext[HALO:HALO + rt, :] = z_ref[:, 2 * cd:]
        cv = cv_ref[...]
        xc = cv - jnp.mean(cv, axis=-1, keepdims=True)
        rstd = lax.rsqrt(jnp.mean(xc * xc, axis=-1, keepdims=True) + EPS)
        xh = xc * rstd
        y = xh * lg_ref[...] + lb_ref[...]
        sy = _sigmoid(y)
        rows = _row_ids(tile, rt)
        da = jnp.where(rows >= PAD_ROWS, dcat[:, 0:cd], 0.0)
        dy = da * (sy * (1.0 + y * (1.0 - sy)))
        dlg_ref[...] += _rowsum(dy * xh)
        dlb_ref[...] += _rowsum(dy)
        dxh = dy * lg_ref[...]
        dconv = rstd * (dxh - jnp.mean(dxh, axis=-1, keepdims=True) - xh * jnp.mean(dxh * xh, axis=-1, keepdims=True))
        dcb_ref[...] += _rowsum(dconv)
        _to_lane_blocks(dcv, 0, dconv)
        for l, ls in enumerate(_lane_blocks(cd)):
            def acc_rows(rb, accs, l=l):
                base = pl.multiple_of(rb * CHUNK, CHUNK)
                d_blk = dcv[l, pl.ds(base, CHUNK), :]
                out = []
                for k in range(CONV_WIDTH):
                    prod = d_blk * gext[l, pl.ds(base + k + HALO - (CONV_WIDTH - 1), CHUNK), :]
                    part = prod[0:8]
                    for s in range(8, CHUNK, 8):
                        part = part + prod[s:s + 8]
                    out.append(accs[k] + part)
                return tuple(out)

            zero = jnp.zeros((8, ls.stop - ls.start), F32)
            accs = lax.fori_loop(0, per, acc_rows, (zero,) * CONV_WIDTH)
            for k in range(CONV_WIDTH):
                dcw_ref[k:k + 1, ls] += _rowsum(accs[k])
        _conv_rows(dcv, cw_ref, lambda k: CONV_WIDTH - 1 - k, conv_s, per, cd)
        dglu = _from_lane_blocks(conv_s)
        dz_ref[:, 0:cd] = (dglu * sg).astype(BF16)
        dz_ref[:, cd:2 * cd] = (dglu * za * sg * (1.0 - sg)).astype(BF16)
        dcv[:, rt:rt + HALO, :] = dcv[:, 0:HALO, :]
        for gi, window in enumerate(POOL_WINDOWS):
            ls = slice(gi * pg, (gi + 1) * pg)
            v = pext[:, ls]
            cnt = _pool_counts(rows, window)
            tm = (_trailing_sum(v, window)[HALO:] / cnt - v[HALO:]).astype(BF16)
            dp = dcat[:, cd + gi * pg:cd + (gi + 1) * pg]
            dps_ref[:, ls] += _rowsum(dp * _dot(tm, pw_ref[gi]))
            dpl = (dp * ps_ref[:, ls]).astype(BF16)
            dpw_ref[gi] += _dot_tn(tm, dpl)
            dtm = _dot_nt(dpl, pw_ref[gi])
            dsp[0:rt, ls] = dtm / cnt
            dpin = _leading_sum(dsp[:, ls], window)[0:rt] - dtm
            dz_ref[:, 2 * cd + gi * pg:2 * cd + (gi + 1) * pg] = dpin.astype(BF16)
        dsp[rt:rt + HALO, :] = dsp[0:HALO, :]

        _project_back(dz_ref, wi_ref, h_ref, g_ref, dh_ref, dhi.at[slot], dg_ref, step == 0)

        @pl.when(tile > 0)
        def _():
            tile_rows(tile, slot).start()

        @pl.when(tile == 0)
        def _():
            first_rows.start()
            dfirst_ref[...] = dhi[last_slot, 0:CHUNK, :]
            if ntile > 1:
                tile_rows(1, 1 - last_slot).wait()
            first_rows.wait()

    d = h.shape[1]
    back = lambda i: (ntile - 1 - i, 0)
    halo_idx = lambda i: (jnp.maximum((ntile - 1 - i) * per - 1, 0), 0)
    const2 = lambda i: (0, 0)
    nl, lb = len(_lane_blocks(cd)), min(LANE, cd)
    return pl.pallas_call(
        body, grid=(ntile,),
        in_specs=dep_specs + [
                  pl.BlockSpec((rt, ein), back), pl.BlockSpec((CHUNK, ein), halo_idx), pl.BlockSpec((rt, cd), back),
                  pl.BlockSpec((rt, d), back),
                  _resident((1, d)), _resident(w_in_t.shape), pl.BlockSpec((rt, d), back), _resident(w_out.shape),
                  _resident(conv_w.shape), _resident((1, cd)), _resident((1, cd)), _resident((1, cd)),
                  _resident(pool_w.shape), _resident((1, pd))],
        out_specs=[pl.BlockSpec(memory_space=pl.ANY), pl.BlockSpec((CHUNK, d), const2), pl.BlockSpec((1, d), const2),
                   pl.BlockSpec((rt, ein), back), pl.BlockSpec(conv_w.shape, const2), pl.BlockSpec((1, cd), const2),
                   pl.BlockSpec((1, cd), const2), pl.BlockSpec((1, cd), const2),
                   pl.BlockSpec(pool_w.shape, lambda i: (0, 0, 0)), pl.BlockSpec((1, pd), const2)],
        out_shape=[_sds((t - CHUNK, d), F32), _sds((CHUNK, d), F32), _sds((1, d), F32),
                   _sds((t, ein), BF16), _sds(conv_w.shape, F32), _sds((1, cd), F32), _sds((1, cd), F32),
                   _sds((1, cd), F32), _sds(pool_w.shape, F32), _sds((1, pd), F32)],
        scratch_shapes=[pltpu.VMEM((nl, rt + HALO, lb), F32), pltpu.VMEM((rt + HALO, pd), F32), pltpu.VMEM((nl, rt, lb), F32),
                        pltpu.VMEM((nl, rt + HALO, lb), F32), pltpu.VMEM((rt + HALO, pd), F32), pltpu.VMEM((2, rt, d), F32),
                        pltpu.SemaphoreType.DMA((2,))],
        compiler_params=_cparams("arbitrary"), name=name)(*deps, z, z, cv, h, gain, w_in_t, dh, w_out, conv_w, conv_b, ln_g,
                                                          ln_b, pool_w, pool_scale)


def _log_decay(r, gw_ref, gb_ref, rows):
    gp = _dot(r.astype(BF16), gw_ref[...]) + gb_ref[...]
    log_sig = jnp.minimum(gp, 0.0) - jnp.log(1.0 + jnp.exp(-jnp.abs(gp)))
    return gp, jnp.where(rows >= PAD_ROWS, log_sig / GATE_NORM, 0.0)


def _tri(strict):
    r = lax.broadcasted_iota(jnp.int32, (CHUNK, CHUNK), 0)
    c = lax.broadcasted_iota(jnp.int32, (CHUNK, CHUNK), 1)
    return jnp.where(c < r if strict else c <= r, 1.0, 0.0).astype(BF16)


def _tri_dot(tri, a):
    hi = a.astype(BF16)
    rest = a - hi.astype(F32)
    mid = rest.astype(BF16)
    lo = (rest - mid.astype(F32)).astype(BF16)
    return _dot(tri, hi) + _dot(tri, mid) + _dot(tri, lo)


def _gla_mid_fwd(h, gain, w_in_blocks, w_out, gate_w, gate_b, head_g, *, name):
    t = h.shape[0]
    nblk, rpb = w_in_blocks.shape[:2]
    dk = gate_b.shape[1]
    hv = head_g.shape[1]
    hk = dk // HEADS
    dv = hv * HEADS
    r_at = 2 * dk + 2 * dv
    assert nblk * rpb == r_at + GATE_RANK
    zw = r_at + GATE_PAD
    rt = _row_tile(t, 320)
    per = rt // CHUNK
    scale = hk ** -0.5

    def body(h_ref, g_ref, wb_ref, wo_ref, gw_ref, gb_ref, hg_ref, ho_ref, o_ref, st_ref, z_ref, u_ref, wi_ref,
             s_ref, la_ref, dec_ref):
        i = pl.program_id(0)

        @pl.when(i == 0)
        def _():
            s_ref[...] = jnp.zeros_like(s_ref)
            for b in range(nblk):
                wi_ref[b * rpb:(b + 1) * rpb, :] = wb_ref[b]
            wi_ref[nblk * rpb:, :] = jnp.zeros((zw - nblk * rpb, wi_ref.shape[1]), BF16)

        _norm_project(h_ref, g_ref, wi_ref, u_ref, z_ref)

        _, la = _log_decay(z_ref[:, r_at:r_at + GATE_PAD], gw_ref, gb_ref, _row_ids(i, rt))
        la_ref[...] = la
        tri = _tri(False)

        def chunk_rows(c):
            return slice(c * CHUNK, (c + 1) * CHUNK)

        def decays(c, carry):
            rows = chunk_rows(c)
            la_c = la_ref[rows, :]
            cum = _tri_dot(tri, la_c)
            dec_ref[rows, :] = jnp.exp(_rowsum(la_c) - cum)
            return carry

        def states(c, carry):
            rows = chunk_rows(c)
            etot = jnp.exp(_rowsum(la_ref[rows, :]))
            for hd in range(HEADS):
                ks = slice(hd * hk, (hd + 1) * hk)
                kd = z_ref[rows, dk + hd * hk:dk + (hd + 1) * hk] * dec_ref[rows, ks]
                v = z_ref[rows, 2 * dk + hd * hv:2 * dk + (hd + 1) * hv]
                s_new = s_ref[hd] * etot[:, ks] + _dot_tn(v.astype(BF16), kd.astype(BF16))
                s_ref[hd] = s_new
                st_ref[c, hd] = s_new
            return carry

        def outputs(c, carry):
            rows = chunk_rows(c)
            for hd in range(HEADS):
                q = z_ref[rows, hd * hk:(hd + 1) * hk] * scale
                g = z_ref[rows, 2 * dk + dv + hd * hv:2 * dk + dv + (hd + 1) * hv]
                o = _dot_nt(q.astype(BF16), st_ref[c, hd].astype(BF16))
                on = o * lax.rsqrt(jnp.mean(o * o, axis=-1, keepdims=True) + EPS) * hg_ref[...]
                o_ref[rows, hd * hv:(hd + 1) * hv] = (on * (g * _sigmoid(g))).astype(BF16)
            return carry

        for phase in (decays, states, outputs):
            for c in range(per):
                phase(c, 0)
        ho_ref[...] = h_ref[...] + _dot(o_ref[...], wo_ref[...])

    d = h.shape[1]
    rows = lambda i: (i, 0)
    return pl.pallas_call(
        body, grid=(t // rt,),
        in_specs=[pl.BlockSpec((rt, d), rows), _resident((1, d)), _resident(w_in_blocks.shape), _resident(w_out.shape),
                  _resident(gate_w.shape), _resident((1, dk)), _resident((1, hv))],
        out_specs=[pl.BlockSpec((rt, d), rows), pl.BlockSpec((rt, dv), rows),
                   pl.BlockSpec((per, HEADS, hv, hk), lambda i: (i, 0, 0, 0)), pl.BlockSpec((rt, zw), rows),
                   pl.BlockSpec((rt, d), rows), pl.BlockSpec((zw, d), lambda i: (0, 0))],
        out_shape=[_sds((t, d), F32), _sds((t, dv), BF16), _sds((t // CHUNK, HEADS, hv, hk), F32), _sds((t, zw), F32),
                   _sds((t, d), BF16), _sds((zw, d), BF16)],
        scratch_shapes=[pltpu.VMEM((HEADS, hv, hk), F32), pltpu.VMEM((rt, dk), F32), pltpu.VMEM((rt, dk), F32)],
        compiler_params=_cparams("arbitrary"), name=name)(h, gain, w_in_blocks, w_out, gate_w, gate_b, head_g)


def _gla_mid_bwd(z, h, gain, w_in_t, dh, w_out, states, gate_w, gate_b, head_g, *, after=None, name):
    t = z.shape[0]
    dk = gate_b.shape[1]
    hv = head_g.shape[1]
    hk = dk // HEADS
    dv = hv * HEADS
    r_at = 2 * dk + 2 * dv
    rt = _row_tile(t, 320)
    ntile = t // rt
    per = rt // CHUNK
    scale = hk ** -0.5
    dep_specs, deps = _dep_specs(after)

    def body(*refs):
        (z_ref, h_ref, g_ref, wi_ref, dh_ref, wo_ref, st_ref, stp_ref, gw_ref, gb_ref, hg_ref,
         dhi_ref, dg_ref, dz_ref, dgw_ref, dgb_ref, dhg_ref,
         ds_ref, la_ref, dla_ref, dec_ref, dos_ref, e_ref, do_ref) = refs[len(deps):]
        step = pl.program_id(0)
        tile = ntile - 1 - step
        do_ref[...] = _dot_nt(dh_ref[...].astype(BF16), wo_ref[...])

        @pl.when(step == 0)
        def _():
            ds_ref[...] = jnp.zeros_like(ds_ref)
            dgw_ref[...] = jnp.zeros_like(dgw_ref)
            dgb_ref[...] = jnp.zeros_like(dgb_ref)
            dhg_ref[...] = jnp.zeros_like(dhg_ref)

        rows_id = _row_ids(tile, rt)
        r = z_ref[:, r_at:r_at + GATE_PAD]
        gp, la = _log_decay(r, gw_ref, gb_ref, rows_id)
        la_ref[...] = la
        tri, tri_strict = _tri(False), _tri(True)
        keep = jnp.where(tile > 0, 1.0, 0.0)

        def chunk_rows(c):
            return slice(c * CHUNK, (c + 1) * CHUNK)

        def recompute(c, dhg):
            rows = chunk_rows(c)
            la_c = la_ref[rows, :]
            cum = _tri_dot(tri, la_c)
            dec_ref[rows, :] = jnp.exp(_rowsum(la_c) - cum)
            for hd in range(HEADS):
                q = (z_ref[rows, hd * hk:(hd + 1) * hk] * scale).astype(BF16)
                g = z_ref[rows, 2 * dk + dv + hd * hv:2 * dk + dv + (hd + 1) * hv]
                s_b = st_ref[c, hd].astype(BF16)
                o = _dot_nt(q, s_b)
                rstd = lax.rsqrt(jnp.mean(o * o, axis=-1, keepdims=True) + EPS)
                oh = o * rstd
                sg = _sigmoid(g)
                d_og = do_ref[rows, hd * hv:(hd + 1) * hv]
                dz_ref[rows, 2 * dk + dv + hd * hv:2 * dk + dv + (hd + 1) * hv] = (
                    d_og * oh * hg_ref[...] * (sg * (1.0 + g * (1.0 - sg)))).astype(BF16)
                don = d_og * (g * sg)
                dhg = dhg + _rowsum(don * oh)
                doh = don * hg_ref[...]
                d_o = (rstd * (doh - oh * jnp.mean(doh * oh, axis=-1, keepdims=True))).astype(BF16)
                dos_ref[rows, hd * hv:(hd + 1) * hv] = d_o
                dz_ref[rows, hd * hk:(hd + 1) * hk] = (_dot(d_o, s_b) * scale).astype(BF16)
            return dhg

        def recurrence(cc, carry):
            c = per - 1 - cc
            rows = chunk_rows(c)
            etot = jnp.exp(_rowsum(la_ref[rows, :]))
            for hd in range(HEADS):
                ks = slice(hd * hk, (hd + 1) * hk)
                q = (z_ref[rows, hd * hk:(hd + 1) * hk] * scale).astype(BF16)
                dec = dec_ref[rows, ks]
                kd = z_ref[rows, dk + hd * hk:dk + (hd + 1) * hk] * dec
                v = z_ref[rows, 2 * dk + hd * hv:2 * dk + (hd + 1) * hv].astype(BF16)
                s_prev = st_ref[c - 1, hd] if c > 0 else keep * stp_ref[0, hd]
                ds_t = ds_ref[hd] + _dot_tn(dos_ref[rows, hd * hv:(hd + 1) * hv], q)
                ds_b = ds_t.astype(BF16)
                dkd = _dot(v, ds_b)
                dz_ref[rows, 2 * dk + hd * hv:2 * dk + (hd + 1) * hv] = _dot_nt(kd.astype(BF16), ds_b).astype(BF16)
                dtot = etot[:, ks] * _rowsum(ds_t * s_prev)
                ds_ref[hd] = ds_t * etot[:, ks]
                dz_ref[rows, dk + hd * hk:dk + (hd + 1) * hk] = (dkd * dec).astype(BF16)
                e_ref[rows, ks] = dkd * kd
                dla_ref[rows, ks] = jnp.broadcast_to(dtot, (CHUNK, hk))
            return carry

        def decay_cotangent(c, carry):
            rows = chunk_rows(c)
            dla_ref[rows, :] += _tri_dot(tri_strict, e_ref[rows, :])
            return carry

        dhg = jnp.zeros((1, hv), F32)
        for c in range(per):
            dhg = recompute(c, dhg)
        dhg_ref[...] += dhg
        for phase in (recurrence, decay_cotangent):
            for c in range(per):
                phase(c, 0)
        dla = jnp.where(rows_id >= PAD_ROWS, dla_ref[...], 0.0)
        dgp = dla * (1.0 / GATE_NORM) * (1.0 - _sigmoid(gp))
        dgb_ref[...] += _rowsum(dgp)
        dgp_b = dgp.astype(BF16)
        dgw_ref[...] += _dot_tn(r.astype(BF16), dgp_b)
        dz_ref[:, r_at:r_at + GATE_PAD] = _dot_nt(dgp_b, gw_ref[...]).astype(BF16)
        _project_back(dz_ref, wi_ref, h_ref, g_ref, dh_ref, dhi_ref, dg_ref, step == 0)

    d = h.shape[1]
    back = lambda i: (ntile - 1 - i, 0)
    const2 = lambda i: (0, 0)
    return pl.pallas_call(
        body, grid=(ntile,),
        in_specs=dep_specs + [
                  pl.BlockSpec((rt, z.shape[1]), back), pl.BlockSpec((rt, d), back), _resident((1, d)),
                  _resident(w_in_t.shape), pl.BlockSpec((rt, d), back), _resident(w_out.shape),
                  pl.BlockSpec((per, HEADS, hv, hk), lambda i: (ntile - 1 - i, 0, 0, 0)),
                  pl.BlockSpec((1, HEADS, hv, hk), lambda i: (jnp.maximum((ntile - 1 - i) * per - 1, 0), 0, 0, 0)),
                  _resident(gate_w.shape), _resident((1, dk)), _resident((1, hv))],
        out_specs=[pl.BlockSpec((rt, d), back), pl.BlockSpec((1, d), const2), pl.BlockSpec((rt, z.shape[1]), back),
                   pl.BlockSpec(gate_w.shape, const2), pl.BlockSpec((1, dk), const2), pl.BlockSpec((1, hv), const2)],
        out_shape=[_sds((t, d), F32), _sds((1, d), F32), _sds(z.shape, BF16), _sds(gate_w.shape, F32),
                   _sds((1, dk), F32), _sds((1, hv), F32)],
        scratch_shapes=[pltpu.VMEM((HEADS, hv, hk), F32), pltpu.VMEM((rt, dk), F32), pltpu.VMEM((rt, dk), F32),
                        pltpu.VMEM((rt, dk), F32), pltpu.VMEM((rt, dv), BF16), pltpu.VMEM((rt, dk), F32),
                        pltpu.VMEM((rt, dv), F32)],
        compiler_params=_cparams("arbitrary"), name=name)(*deps, z, h, gain, w_in_t, dh, w_out, states, states, gate_w,
                                                          gate_b, head_g)


def _adamw_math(w, g, m, v):
    m = ADAM_B1 * m + (1.0 - ADAM_B1) * g
    v = ADAM_B2 * v + (1.0 - ADAM_B2) * (g * g)
    m_hat = m / (1.0 - ADAM_B1 ** ADAM_STEP)
    v_hat = v / (1.0 - ADAM_B2 ** ADAM_STEP)
    return -ADAM_LR * (m_hat / (jnp.sqrt(v_hat) + ADAM_EPS) + ADAM_WD * w), m, v


N_CHIP = N_DEV // 2
BLOCK_ELEMS = 128 * 1024


def _my_slot():
    return 4 * lax.axis_index("x") + 2 * lax.axis_index("y") + lax.axis_index("c")


def _row_block(r, c):
    cap = max(8, BLOCK_ELEMS // (-(-c // LANE) * LANE))
    return max([b for b in range(8, r + 1, 8) if r % b == 0 and b <= cap] or [r])


def _blocks(r, c):
    rb = _row_block(r, c)
    if rb < r or r * c <= BLOCK_ELEMS:
        return rb, c
    return r, max([b for b in (512, 256, LANE) if c % b == 0 and r * b <= BLOCK_ELEMS] or [c])


def _reduce_adam(parts, w, m, v, *, after=None, name):
    nl, r, c = w.shape
    rb, cb = _blocks(r, c)
    dep_specs, deps = _dep_specs(after)

    def body(*refs):
        me = refs[0][0]
        refs = refs[1 + len(deps):]
        p_refs = refs[:2 * nl]
        w_ref, m_ref, v_ref, g_out, d_out, m_out, v_out = refs[2 * nl:]
        layer = pl.program_id(0)
        for li in range(nl):
            @pl.when(layer == li)
            def _(li=li):
                own_ref, land_ref = p_refs[2 * li], p_refs[2 * li + 1]
                mine = own_ref[...].astype(F32)
                g = None
                for dev in range(N_DEV):
                    term = jnp.where(me == dev, mine, land_ref[dev].astype(F32))
                    g = term if g is None else g + term
                g_out[...] = g
                d_out[...], m_out[...], v_out[...] = _adamw_math(w_ref[...], g, m_ref[...], v_ref[...])

    blk = pl.BlockSpec((None, rb, cb), lambda l, i, j, me: (l, i, j))
    p_specs = []
    for li in range(nl):
        p_specs += [
            pl.BlockSpec((None, rb, cb), lambda l, i, j, me, li=li: (me[0], jnp.where(l == li, i, 0), jnp.where(l == li, j, 0))),
            pl.BlockSpec((N_DEV, rb, cb), lambda l, i, j, me, li=li: (0, jnp.where(l == li, i, 0), jnp.where(l == li, j, 0)))]
    flat = [p for pair in parts for p in pair]
    grid_spec = pltpu.PrefetchScalarGridSpec(
        num_scalar_prefetch=1, grid=(nl, r // rb, c // cb), in_specs=dep_specs + p_specs + [blk, blk, blk],
        out_specs=[blk] * 4)
    return pl.pallas_call(
        body, grid_spec=grid_spec, out_shape=[_sds(w.shape, F32)] * 4,
        compiler_params=_cparams("arbitrary", "arbitrary", "arbitrary"), name=name)(
        _my_slot().reshape(1), *deps, *flat, w, m, v)


def _sum8(own, landed, *, name):
    def body(own_ref, land_ref, o_ref):
        me = _my_slot()
        total = None
        for dev in range(N_DEV):
            term = jnp.where(me == dev, own_ref[...], land_ref[dev])
            total = term if total is None else total + term
        o_ref[...] = total

    return pl.pallas_call(body, out_shape=_sds(own.shape, F32), name=name)(own, landed)


def _adam_small(own, landed, split, w, m, v, *, name):
    n = len(w)

    def body(*refs):
        own_refs, land_refs, w_refs, m_refs, v_refs = (refs[k * n:(k + 1) * n] for k in range(5))
        outs = refs[5 * n:]
        me = _my_slot()
        for k in range(n):
            mine = own_refs[k][me] if split[k] else own_refs[k][...]
            g = None
            for dev in range(N_DEV):
                term = jnp.where(me == dev, mine, land_refs[k][dev])
                g = term if g is None else g + term
            outs[4 * k][...] = g
            outs[4 * k + 1][...], outs[4 * k + 2][...], outs[4 * k + 3][...] = _adamw_math(
                w_refs[k][...], g, m_refs[k][...], v_refs[k][...])

    out = pl.pallas_call(body, out_shape=[_sds(a.shape, F32) for a in w for _ in range(4)],
                         compiler_params=pltpu.CompilerParams(vmem_limit_bytes=V7X_VMEM_LIMIT), name=name)(
        *own, *landed, *w, *m, *v)
    return [tuple(out[4 * k:4 * k + 4]) for k in range(n)]


_HBM = pl.BlockSpec(memory_space=pltpu.HBM)
_SEM = pl.BlockSpec(memory_space=pltpu.SEMAPHORE)
_DATAFLOW = pltpu.SideEffectType.DATAFLOW_SIDE_EFFECTING


def _plan_to_all(src, land):
    x, y, c = lax.axis_index("x"), lax.axis_index("y"), lax.axis_index("c")
    return [(src, land.at[_my_slot()], (x ^ ((d >> 2) & 1), y ^ ((d >> 1) & 1), c ^ (d & 1))) for d in range(1, N_DEV)]


def _plan_split_to_all(src, land):
    x, y, c = lax.axis_index("x"), lax.axis_index("y"), lax.axis_index("c")
    peers = [(x ^ ((d >> 2) & 1), y ^ ((d >> 1) & 1), c ^ (d & 1)) for d in range(1, N_DEV)]
    return [(src.at[4 * px + 2 * py + pc], land.at[_my_slot()], (px, py, pc)) for px, py, pc in peers]


_PLAN_COPIES = {_plan_to_all: N_DEV - 1, _plan_split_to_all: N_DEV - 1}


def _plans(plan, n):
    return list(plan) if isinstance(plan, (list, tuple)) else [plan] * n


def _exchange_copies(plan, ins, lands, send, recv):
    copies, sem = [], 0
    for p, src, land in zip(_plans(plan, len(lands)), ins, lands):
        for s, dst, dev in p(src, land):
            copies.append(pltpu.make_async_remote_copy(
                src_ref=s, dst_ref=dst, send_sem=send.at[sem], recv_sem=recv.at[sem],
                device_id=dev, device_id_type=pl.DeviceIdType.MESH))
            sem += 1
    return copies


def _place_own(srcs, *, after=None, name):
    dep_specs, deps = _dep_specs(after)
    n = len(srcs)
    arrays, in_specs, shapes = [], [], []
    for a, dtype in srcs:
        if isinstance(a, tuple):
            a, layer = a
            in_specs.append(pl.BlockSpec((None,) + a.shape[1:], lambda i, layer=layer: (layer, 0, 0)))
            shapes.append(a.shape[1:])
        else:
            in_specs.append(pl.BlockSpec(a.shape, lambda i: (0, 0)))
            shapes.append(a.shape)
        arrays.append(a)
    dtypes = [dtype for _, dtype in srcs]

    def body(*refs):
        refs = refs[len(deps):]
        a_refs, o_refs, cast_refs, sem = refs[:n], refs[n:2 * n], refs[2 * n:3 * n], refs[3 * n]
        me = _my_slot()
        copies = []
        for k in range(n):
            cast_refs[k][...] = a_refs[k][...].astype(dtypes[k])
            copies.append(pltpu.make_async_copy(cast_refs[k], o_refs[k].at[me], sem.at[k]))
            copies[-1].start()
        for cp in copies:
            cp.wait()

    return pl.pallas_call(
        body, grid=(1,), in_specs=dep_specs + in_specs, out_specs=[pl.BlockSpec(memory_space=pl.ANY)] * n,
        out_shape=[_sds((N_DEV,) + shape, dtype) for shape, dtype in zip(shapes, dtypes)],
        scratch_shapes=[pltpu.VMEM(shape, dtype) for shape, dtype in zip(shapes, dtypes)] + [pltpu.SemaphoreType.DMA((n,))],
        compiler_params=pltpu.CompilerParams(vmem_limit_bytes=V7X_VMEM_LIMIT), name=name)(*deps, *arrays)


def _plan_gather_first(land, _):
    x, y, c = lax.axis_index("x"), lax.axis_index("y"), lax.axis_index("c")
    mine = land.at[_my_slot()]
    return [(mine, mine, (x, y, 1 - c))] + [(mine, mine, (x ^ (d >> 1), y ^ (d & 1), c)) for d in range(1, N_CHIP)]


def _plan_gather_relay(land, _):
    x, y, c = lax.axis_index("x"), lax.axis_index("y"), lax.axis_index("c")
    slots = [land.at[4 * (x ^ (d >> 1)) + 2 * (y ^ (d & 1)) + c] for d in range(1, N_CHIP)]
    return [(s, s, (x, y, 1 - c)) for s in slots]


def _plan_gather_direct(land, _):
    return _plan_to_all(land.at[_my_slot()], land)


_PLAN_COPIES[_plan_gather_first] = N_CHIP
_PLAN_COPIES[_plan_gather_relay] = N_CHIP - 1
_PLAN_COPIES[_plan_gather_direct] = N_DEV - 1


def _exchange_start(plan, arrs, lands, *, after=None, name):
    bufs = list(lands) if arrs is None else list(arrs) + list(lands)
    n, nb = len(lands), len(bufs)
    nsem = sum(_PLAN_COPIES[p] for p in _plans(plan, n))
    dep_specs, deps = _dep_specs(after)

    def body(*refs):
        ins, land_refs = refs[:n], refs[nb - n:nb]
        send, recv = refs[nb + len(deps)], refs[nb + len(deps) + 1]
        for cp in _exchange_copies(plan, ins, land_refs, send, recv):
            cp.start()
        refs[-1][...] = jnp.zeros_like(refs[-1])

    out = pl.pallas_call(
        body, name=name,
        out_shape=(pltpu.SemaphoreType.DMA((nsem,)), pltpu.SemaphoreType.DMA((nsem,)),
                   *[pltpu.HBM(a.shape, a.dtype) for a in bufs], _sds((8, LANE), F32)),
        in_specs=[_HBM] * nb + dep_specs,
        out_specs=(_SEM, _SEM, *([_HBM] * nb), pl.BlockSpec(memory_space=pltpu.VMEM)),
        input_output_aliases={i: 2 + i for i in range(nb)},
        compiler_params=pltpu.CompilerParams(has_side_effects=_DATAFLOW),
    )(*[pltpu.with_memory_space_constraint(a, pltpu.HBM) for a in bufs], *deps)
    return (plan, n, out[0], out[1], list(out[2:2 + nb])), out[-1]


def _exchange_now(plan, lands, *, name):
    n = len(lands)
    nsem = sum(_PLAN_COPIES[p] for p in _plans(plan, n))

    def body(*refs):
        land_refs, send, recv = refs[n:2 * n], refs[2 * n], refs[2 * n + 1]
        copies = _exchange_copies(plan, land_refs, land_refs, send, recv)
        for cp in copies:
            cp.start()
        for cp in copies:
            cp.wait_send()
            cp.wait_recv()

    hbm = pl.BlockSpec(memory_space=pl.ANY)
    return pl.pallas_call(
        body, in_specs=[hbm] * n, out_specs=[hbm] * n, out_shape=[_sds(a.shape, a.dtype) for a in lands],
        input_output_aliases={i: i for i in range(n)},
        scratch_shapes=[pltpu.SemaphoreType.DMA((nsem,)), pltpu.SemaphoreType.DMA((nsem,))], name=name)(*lands)


def _exchange_wait(state, after, *, name):
    plan, n, send_sem, recv_sem, bufs = state
    nb = len(bufs)
    after = list(after) if isinstance(after, (list, tuple)) else [after]

    def body(*refs):
        ins, land_refs, send, recv = refs[:n], refs[nb - n:nb], refs[nb], refs[nb + 1]
        for cp in _exchange_copies(plan, ins, land_refs, send, recv):
            cp.wait_send()
            cp.wait_recv()

    out = pl.pallas_call(
        body, name=name, out_shape=[pltpu.HBM(a.shape, a.dtype) for a in bufs],
        in_specs=[_HBM] * nb + [_SEM, _SEM] + [pl.BlockSpec(memory_space=pl.ANY)] * len(after), out_specs=[_HBM] * nb,
        input_output_aliases={i: i for i in range(nb)},
        compiler_params=pltpu.CompilerParams(has_side_effects=_DATAFLOW),
    )(*bufs, send_sem, recv_sem, *after)
    return list(out[:n]), list(out[nb - n:])


def _dep_specs(after):
    return ([], []) if after is None else ([pl.BlockSpec(memory_space=pl.ANY)], [after])


def _undo_column_split(g):
    return jnp.transpose(g, (1, 0, 2)).reshape(g.shape[1], N_DEV * g.shape[2])


def _column_split(a):
    r, c = a.shape
    return jnp.transpose(a.reshape(r, N_DEV, c // N_DEV), (1, 0, 2))


class _WholeWeights:
    def __init__(self, groups):
        self.groups = groups
        self.grads = {}

    def fetch(self, group, after):
        return self.groups[group]

    def emit(self, group, grads):
        self.grads.update(grads)
        return None


def _local_step(x, target, replicated, src):
    d = x.shape[1]
    mix_g, ffn_g = replicated["mix_g"], replicated["ffn_g"]
    cp = src.fetch("cp", [])
    cp_mid = (cp["conv_w"], replicated["conv_b"], replicated["ln_g"], replicated["ln_b"], replicated["pool_w"],
              replicated["pool_scale"])

    h1, cat, z0, u0, cv0, h0 = _cp_mid_fwd(x, cp["meta"], mix_g[0:1], cp["cp_w_in_t"], cp["cp_w_out"], *cp_mid,
                                           name="cp_mixer")
    ffn0 = src.fetch("ffn0", h1)
    h2, uf0, rf0 = _ffn_fwd(h1, ffn_g[0:1], ffn0["w1"], ffn0["w2"], name="ffn0")
    gla = src.fetch("gla", h2)
    gla_mid = (gla["gate_w"], gla["gate_b"], gla["head_g"])
    h3, og, states, z1, u1, gla_w_in_t = _gla_mid_fwd(h2, mix_g[1:2], gla["gla_w_in_t"], gla["gla_w_out"], *gla_mid,
                                                      name="gla_mixer")
    ffn1 = src.fetch("ffn1", h3)
    dh4, uf1, rf1, loss, d_final_g = _ffn_fwd(h3, ffn_g[1:2], ffn1["w1"], ffn1["w2"],
                                              loss_head=(replicated["final_g"], target), name="ffn1_loss")

    dh3, dhh1, dob1, dffn_g1 = _ffn_bwd_x(h3, dh4, ffn_g[1:2], rf1, ffn1["w1"], ffn1["w2"], name="ffn1_bwd_x")
    sent = src.emit("ffn1_w1", dict(w1=_linear_bwd_w(uf1, dhh1, column_blocks=N_DEV, name="ffn1_dw1")))
    sent = src.emit("ffn1_w2", dict(w2=_linear_bwd_w(rf1, dob1, square_x=True, after=sent, name="ffn1_dw2")))
    d_gla_w_out = _linear_bwd_w(og, dh3, name="gla_out_dw")
    dh2, dmix_g1, dz1, d_gate_w, d_gate_b, d_head_g = _gla_mid_bwd(
        z1, h2, mix_g[1:2], gla_w_in_t, dh3, gla["gla_w_out"], states, *gla_mid, after=sent, name="gla_mixer_bwd")
    d_gla_w_in_t = _linear_bwd_w(dz1, u1, row_blocks=gla["gla_w_in_t"].shape[:2], name="gla_in_dw")
    sent = src.emit("gla", dict(gla_w_in_t=d_gla_w_in_t, gla_w_out=d_gla_w_out))
    dh1, dhh0, dob0, dffn_g0 = _ffn_bwd_x(h1, dh2, ffn_g[0:1], rf0, ffn0["w1"], ffn0["w2"], after=sent, name="ffn0_bwd_x")
    sent = src.emit("ffn0_w1", dict(w1=_linear_bwd_w(uf0, dhh0, column_blocks=N_DEV, name="ffn0_dw1")))
    sent = src.emit("ffn0_w2", dict(w2=_linear_bwd_w(rf0, dob0, square_x=True, after=sent, name="ffn0_dw2")))
    d_cp_w_out = _linear_bwd_w(cat, dh1, after=sent, name="cp_out_dw")
    sent = src.emit("cp_out", dict(cp_w_out=d_cp_w_out))
    dx, dh0_first, dmix_g0, dz0, d_conv_w, d_conv_b, d_ln_g, d_ln_b, d_pool_w, d_pool_scale = _cp_mid_bwd(
        z0, cv0, h0, mix_g[0:1], cp["cp_w_in_t"], dh1, cp["cp_w_out"], *cp_mid, after=sent, name="cp_mixer_bwd")
    d_cp_w_in_t = _linear_bwd_w(dz0, u0, name="cp_in_dw")

    small = dict(
        mix_g=jnp.concatenate([dmix_g0, dmix_g1]), ffn_g=jnp.concatenate([dffn_g0, dffn_g1]), conv_b=d_conv_b, ln_g=d_ln_g,
        ln_b=d_ln_b, pool_w=d_pool_w, pool_scale=d_pool_scale, final_g=d_final_g, meta=dh0_first[PAD_ROWS:], conv_w=d_conv_w,
        gate_w=d_gate_w, gate_b=d_gate_b, head_g=d_head_g)
    src.emit("cp", dict(cp_w_in_t=d_cp_w_in_t, small=small, loss=loss))
    return loss, dx, small


_REPLICATED = ("mix_norm_g", "ffn_norm_g", "cp_conv_b", "cp_ln_g", "cp_ln_b", "cp_pool_w", "cp_pool_scale", "final_norm_g")
_SMALL_SHARDED = ("meta_tokens", "cp_conv_w", "gla_gate_w2", "gla_gate_b", "gla_head_g")
_NAMES = ("meta_tokens", "mix_norm_g", "ffn_norm_g", "ffn_w1", "ffn_w2", "cp_w_in", "cp_conv_w", "cp_conv_b", "cp_ln_g",
          "cp_ln_b", "cp_pool_w", "cp_pool_scale", "cp_w_out", "gla_w_in", "gla_gate_w2", "gla_gate_b", "gla_head_g",
          "gla_w_out", "final_norm_g")
_SMALL_GRADS = ("mix_g", "ffn_g", "conv_b", "ln_g", "ln_b", "pool_w", "pool_scale", "final_g", "meta", "conv_w", "gate_w",
                "gate_b", "head_g")
_GROUPS = ("cp", "ffn0", "gla", "ffn1")
_TWO_LEG_GATHERS = ("cp", "ffn0", "ffn1")


class _Exchanges:
    def __init__(self, w, d):
        self.d = d
        small = [w[n].reshape(w[n].shape[-2:]) for n in _SMALL_SHARDED]
        self.small_shard_shapes = [w[n].shape for n in _SMALL_SHARDED]
        shards = dict(
            cp=[(w["cp_w_in"][0].T, BF16), (w["cp_w_out"][0], BF16)] + [(a, F32) for a in small],
            ffn0=[((w["ffn_w1"], 0), BF16), ((w["ffn_w2"], 0), BF16)],
            gla=[(w["gla_w_in"][0].T, BF16), (w["gla_w_out"][0], BF16)],
            ffn1=[((w["ffn_w1"], 1), BF16), ((w["ffn_w2"], 1), BF16)])
        self.gathers = {}
        self.sent = {}
        token = None
        for group in _GROUPS:
            lands = _place_own(shards[group], after=token, name=f"place_w_{group}")
            plan = _plan_gather_first if group in _TWO_LEG_GATHERS else _plan_gather_direct
            self.gathers[group], token = _exchange_start(plan, None, lands, after=token, name=f"start_w_{group}")
        self.token = token

    def fetch(self, group, after):
        d = self.d
        after = (list(after) if isinstance(after, (list, tuple)) else [after]) + [self.token]
        _, got = _exchange_wait(self.gathers[group], after, name=f"wait_w_{group}")
        if group in _TWO_LEG_GATHERS:
            got = _exchange_now(_plan_gather_relay, got, name=f"relay_w_{group}")
        if group in ("ffn0", "ffn1"):
            return dict(w1=got[0], w2=got[1])
        if group == "gla":
            return dict(gla_w_in_t=got[0], gla_w_out=got[1].reshape(d, d), gate_w=self.gate_w, gate_b=self.gate_b,
                        head_g=self.head_g)
        meta, conv_w, gate_w, self.gate_b, self.head_g = [_undo_column_split(a) for a in got[2:]]
        self.gate_w = jnp.pad(gate_w, ((0, GATE_PAD - GATE_RANK), (0, 0))).astype(BF16)
        return dict(cp_w_in_t=got[0].reshape(-1, d), cp_w_out=got[1].reshape(d, d), meta=meta,
                    conv_w=jnp.pad(conv_w, ((0, 1), (0, 0))))

    def emit(self, group, g):
        d = self.d
        if group in ("ffn0_w1", "ffn1_w1"):
            arrs = [g["w1"]]
        elif group in ("ffn0_w2", "ffn1_w2"):
            arrs = [g["w2"].reshape(N_DEV, -1, d)]
        elif group == "gla":
            arrs = [g["gla_w_in_t"], g["gla_w_out"].reshape(N_DEV, d // N_DEV, d)]
        elif group == "cp_out":
            arrs = [g["cp_w_out"].reshape(N_DEV, d // N_DEV, d)]
        else:
            s = dict(g["small"])
            s.update(pool_w=s["pool_w"][None], conv_w=s["conv_w"][:CONV_WIDTH], gate_w=s["gate_w"][:GATE_RANK])
            own = [s[n] for n in _SMALL_GRADS[:len(_REPLICATED)]]
            own += [_column_split(s[n]).reshape((N_DEV,) + shape)
                    for n, shape in zip(_SMALL_GRADS[len(_REPLICATED):], self.small_shard_shapes)]
            plans = [_plan_to_all] * len(_REPLICATED) + [_plan_split_to_all] * len(_SMALL_SHARDED)
            lands = [lax.empty((N_DEV,) + a.shape, F32) for a in own[:len(_REPLICATED)]]
            lands += [lax.empty(a.shape, F32) for a in own[len(_REPLICATED):]]
            own.append(g["loss"])
            plans.append(_plan_to_all)
            lands.append(lax.empty((N_DEV,) + g["loss"].shape, F32))
            self.small_sent, self.token = _exchange_start(plans, own, lands, after=self.token, name="start_g_small")
            arrs = [g["cp_w_in_t"].reshape(N_DEV, -1, d)]
        self.sent[group], self.token = _exchange_start(_plan_split_to_all, arrs, [lax.empty(a.shape, a.dtype) for a in arrs],
                                                       after=self.token, name=f"start_g_{group}")
        return self.token

    def finish(self, w, mom, var):
        out = {}
        after = self.token

        def landed(group):
            own, got = _exchange_wait(self.sent[group], after, name=f"wait_g_{group}")
            return list(zip(own, got))

        def adam(n, parts, behind=None, transposed=False):
            flip = (lambda a: jnp.transpose(a, (0, 2, 1))) if transposed else (lambda a: a)
            res = _reduce_adam(parts, flip(w[n]), flip(mom[n]), flip(var[n]), after=behind, name=f"adam_{n}")
            out[n] = tuple(flip(a) for a in res)
            return res[0]

        ffn1_w1 = landed("ffn1_w1")
        after = ffn1_w1[0][1]
        ffn1_w2 = landed("ffn1_w2")
        after = ffn1_w2[0][1]
        gla = landed("gla")
        after = adam("gla_w_in", [gla[0]], transposed=True)
        after = adam("gla_w_out", [gla[1]], after)
        ffn0_w1 = landed("ffn0_w1")
        after = adam("ffn_w1", [ffn0_w1[0], ffn1_w1[0]])
        ffn0_w2 = landed("ffn0_w2")
        after = adam("ffn_w2", [ffn0_w2[0], ffn1_w2[0]])
        small_own, small_landed = _exchange_wait(self.small_sent, after, name="wait_g_small")
        names = _REPLICATED + _SMALL_SHARDED
        split = [False] * len(_REPLICATED) + [True] * len(_SMALL_SHARDED)
        small_new = _adam_small(small_own[:-1], small_landed[:-1], split, [w[n] for n in names], [mom[n] for n in names],
                                [var[n] for n in names], name="adam_small")
        out.update(zip(names, small_new))
        out["loss"] = _sum8(small_own[-1], small_landed[-1], name="sum_loss")[0, 0]

        after = small_new[0][0]
        cp_out = landed("cp_out")
        after = adam("cp_w_out", [cp_out[0]])
        cp = landed("cp")
        adam("cp_w_in", [cp[0]], transposed=True)
        return out


def kernel(x, meta_tokens, mix_norm_g, ffn_norm_g, ffn_w1, ffn_w2, cp_w_in, cp_conv_w, cp_conv_b, cp_ln_g, cp_ln_b, cp_pool_w, cp_pool_scale, cp_w_out, gla_w_in, gla_gate_w2, gla_gate_b, gla_head_g, gla_w_out, final_norm_g, loss_target, m_meta_tokens, m_mix_norm_g, m_ffn_norm_g, m_ffn_w1, m_ffn_w2, m_cp_w_in, m_cp_conv_w, m_cp_conv_b, m_cp_ln_g, m_cp_ln_b, m_cp_pool_w, m_cp_pool_scale, m_cp_w_out, m_gla_w_in, m_gla_gate_w2, m_gla_gate_b, m_gla_head_g, m_gla_w_out, m_final_norm_g, v_meta_tokens, v_mix_norm_g, v_ffn_norm_g, v_ffn_w1, v_ffn_w2, v_cp_w_in, v_cp_conv_w, v_cp_conv_b, v_cp_ln_g, v_cp_ln_b, v_cp_pool_w, v_cp_pool_scale, v_cp_w_out, v_gla_w_in, v_gla_gate_w2, v_gla_gate_b, v_gla_head_g, v_gla_w_out, v_final_norm_g):
    w = dict(meta_tokens=meta_tokens, mix_norm_g=mix_norm_g, ffn_norm_g=ffn_norm_g, ffn_w1=ffn_w1, ffn_w2=ffn_w2,
             cp_w_in=cp_w_in, cp_conv_w=cp_conv_w, cp_conv_b=cp_conv_b, cp_ln_g=cp_ln_g, cp_ln_b=cp_ln_b,
             cp_pool_w=cp_pool_w, cp_pool_scale=cp_pool_scale, cp_w_out=cp_w_out, gla_w_in=gla_w_in,
             gla_gate_w2=gla_gate_w2, gla_gate_b=gla_gate_b, gla_head_g=gla_head_g, gla_w_out=gla_w_out,
             final_norm_g=final_norm_g.reshape(1, -1))
    mom = dict(meta_tokens=m_meta_tokens, mix_norm_g=m_mix_norm_g, ffn_norm_g=m_ffn_norm_g, ffn_w1=m_ffn_w1, ffn_w2=m_ffn_w2,
               cp_w_in=m_cp_w_in, cp_conv_w=m_cp_conv_w, cp_conv_b=m_cp_conv_b, cp_ln_g=m_cp_ln_g, cp_ln_b=m_cp_ln_b,
               cp_pool_w=m_cp_pool_w, cp_pool_scale=m_cp_pool_scale, cp_w_out=m_cp_w_out, gla_w_in=m_gla_w_in,
               gla_gate_w2=m_gla_gate_w2, gla_gate_b=m_gla_gate_b, gla_head_g=m_gla_head_g, gla_w_out=m_gla_w_out,
               final_norm_g=m_final_norm_g.reshape(1, -1))
    var = dict(meta_tokens=v_meta_tokens, mix_norm_g=v_mix_norm_g, ffn_norm_g=v_ffn_norm_g, ffn_w1=v_ffn_w1, ffn_w2=v_ffn_w2,
               cp_w_in=v_cp_w_in, cp_conv_w=v_cp_conv_w, cp_conv_b=v_cp_conv_b, cp_ln_g=v_cp_ln_g, cp_ln_b=v_cp_ln_b,
               cp_pool_w=v_cp_pool_w, cp_pool_scale=v_cp_pool_scale, cp_w_out=v_cp_w_out, gla_w_in=v_gla_w_in,
               gla_gate_w2=v_gla_gate_w2, gla_gate_b=v_gla_gate_b, gla_head_g=v_gla_head_g, gla_w_out=v_gla_w_out,
               final_norm_g=v_final_norm_g.reshape(1, -1))
    d = x.shape[-1]
    replicated = dict(mix_g=w["mix_norm_g"], ffn_g=w["ffn_norm_g"], conv_b=w["cp_conv_b"], ln_g=w["cp_ln_g"],
                      ln_b=w["cp_ln_b"], pool_w=w["cp_pool_w"][0].astype(BF16), pool_scale=w["cp_pool_scale"],
                      final_g=w["final_norm_g"])
    exchanges = _Exchanges(w, d)
    _, grad_x, _ = _local_step(x[0], loss_target[0], replicated, exchanges)
    out = exchanges.finish(w, mom, var)
    loss = out.pop("loss")

    def leaf(n, k):
        a = out[n][k]
        return a.reshape(-1) if n == "final_norm_g" else a

    return (loss, grad_x[None], *[leaf(n, 0) for n in _NAMES], *[leaf(n, 1) for n in _NAMES],
            *[leaf(n, 2) for n in _NAMES], *[leaf(n, 3) for n in _NAMES])
```

```python
import functools

import jax
import jax.numpy as jnp
from jax import lax
from jax.experimental import pallas as pl
from jax.experimental.pallas import tpu as pltpu

F32, BF16 = jnp.float32, jnp.bfloat16
N_DEV = 8
CHUNK = 64
N_META = 16
PAD_ROWS = CHUNK - N_META
HALO = 32
EPS = 1e-5
CONV_WIDTH = 31
POOL_WINDOWS = (2, 4, 8, 16)
HEADS = 4
GATE_RANK = 16
GATE_NORM = 16.0
GATE_PAD = 128
ADAM_LR, ADAM_B1, ADAM_B2, ADAM_EPS, ADAM_WD, ADAM_STEP = 0.001, 0.9, 0.999, 1e-08, 0.01, 10
V7X_VMEM_LIMIT = 56 * 2 ** 20
LANE = 128


def _cparams(*sem):
    return pltpu.CompilerParams(dimension_semantics=sem, vmem_limit_bytes=V7X_VMEM_LIMIT)


def _row_tile(t, cap):
    best = CHUNK
    for r in range(CHUNK, min(t, cap) + 1, CHUNK):
        if t % r == 0:
            best = r
    return best


def _resident(shape):
    return pl.BlockSpec(shape, lambda *_: (0,) * len(shape), pipeline_mode=pl.Buffered(1))


def _dot(a, b):
    return jnp.dot(a, b, preferred_element_type=F32)


def _dot_nt(a, b):
    return lax.dot_general(a, b, (((1,), (1,)), ((), ())), preferred_element_type=F32)


def _dot_tn(a, b):
    return lax.dot_general(a, b, (((0,), (0,)), ((), ())), preferred_element_type=F32)


def _rowsum(a):
    return jnp.sum(a, axis=0, keepdims=True)


def _sigmoid(a):
    return 1.0 / (1.0 + jnp.exp(-a))


def _row_ids(tile, rt):
    return tile * rt + lax.broadcasted_iota(jnp.int32, (rt, 1), 0)


def _sds(shape, dtype):
    return jax.ShapeDtypeStruct(shape, dtype)


DW_ROWS = 1024


def _linear_bwd_w(x, dy, *, square_x=False, column_blocks=None, row_blocks=None, after=None, name):
    t, k = x.shape
    n = dy.shape[1]
    cut_k = k > n and column_blocks is None
    assert cut_k or row_blocks is None
    width = k if cut_k else n
    blk = n // column_blocks if column_blocks else max(c for c in (640, 512, 384, 256, LANE) if width % c == 0)
    dep_specs, deps = _dep_specs(after)

    def body(*refs):
        x_ref, dy_ref, o_ref, acc = refs[len(deps):]
        for c0 in range(0, t, DW_ROWS):
            rows = slice(c0, min(c0 + DW_ROWS, t))
            xv = x_ref[rows, :]
            if square_x:
                xv = xv.astype(F32)
                xv = xv * xv
            part = _dot_tn(xv.astype(BF16), dy_ref[rows, :].astype(BF16))
            if c0 == 0:
                acc[...] = part
            else:
                acc[...] += part
        if row_blocks is None:
            o_ref[...] = acc[...].astype(BF16)
            return
        nb, rpb = row_blocks
        for s in range(width // blk):
            @pl.when(pl.program_id(0) == s)
            def _(s=s):
                for b in range(nb):
                    lo, hi = max(s * blk, b * rpb), min((s + 1) * blk, (b + 1) * rpb)
                    if lo < hi:
                        o_ref[b, lo - b * rpb:hi - b * rpb, :] = acc[lo - s * blk:hi - s * blk, :].astype(BF16)

    out_shape = _sds((k, n), BF16)
    semantics = "parallel"
    if cut_k:
        in_specs = [pl.BlockSpec((t, blk), lambda j: (0, j)), _resident((t, n))]
        out_specs = pl.BlockSpec((blk, n), lambda j: (j, 0))
        acc_shape = (blk, n)
        if row_blocks:
            out_shape = _sds(row_blocks + (n,), BF16)
            out_specs = pl.BlockSpec(out_shape.shape, lambda j: (0, 0, 0))
            semantics = "arbitrary"
    else:
        in_specs = [_resident((t, k)), pl.BlockSpec((t, blk), lambda j: (0, j))]
        out_specs = pl.BlockSpec((k, blk), lambda j: (0, j))
        acc_shape = (k, blk)
        if column_blocks:
            out_specs = pl.BlockSpec((None, k, blk), lambda j: (j, 0, 0))
            out_shape = _sds((column_blocks, k, blk), BF16)
    return pl.pallas_call(
        body, grid=(width // blk,), in_specs=dep_specs + in_specs, out_specs=out_specs, out_shape=out_shape,
        scratch_shapes=[pltpu.VMEM(acc_shape, F32)], compiler_params=_cparams(semantics), name=name)(*deps, x, dy)


FFN_BLOCKS_PER_STEP = 2


def _ffn_fwd(h, gain, w1g, w2g, *, loss_head=None, name):
    t, d = h.shape
    f8 = w1g.shape[-1]
    rt = _row_tile(t, 832)
    nb = FFN_BLOCKS_PER_STEP
    nstep = N_DEV // nb

    def body(*refs):
        if loss_head is None:
            h_ref, g_ref, w1_ref, w2_ref, o_ref, u_ref, r_ref, acc_ref = refs
        else:
            (h_ref, g_ref, w1_ref, w2_ref, fg_ref, tgt_ref, o_ref, u_ref, r_ref, loss_ref, dfg_ref, acc_ref, t_ref,
             t_sem) = refs
        i, j = pl.program_id(0), pl.program_id(1)

        def target_rows(act):
            @pl.when(i == 0)
            def _():
                act(pltpu.make_async_copy(tgt_ref.at[pl.ds(0, rt - CHUNK)], t_ref.at[pl.ds(CHUNK, rt - CHUNK)], t_sem.at[0]))

            if t > rt:
                @pl.when(i > 0)
                def _():
                    act(pltpu.make_async_copy(tgt_ref.at[pl.ds(pl.multiple_of(i * rt - CHUNK, CHUNK), rt)], t_ref,
                                              t_sem.at[0]))

        @pl.when(j == 0)
        def _():
            if loss_head is not None:
                @pl.when(i == 0)
                def _():
                    t_ref[0:CHUNK, :] = jnp.zeros((CHUNK, d), F32)

                target_rows(lambda copy: copy.start())

            hv = h_ref[...]
            u_ref[...] = (hv * lax.rsqrt(jnp.mean(hv * hv, axis=-1, keepdims=True) + EPS) * g_ref[...]).astype(BF16)
            acc_ref[...] = jnp.zeros_like(acc_ref)

        part = None
        for b in range(nb):
            a = jnp.maximum(_dot(u_ref[...], w1_ref[b]), 0.0)
            r_ref[:, b * f8:(b + 1) * f8] = a.astype(BF16)
            term = _dot((a * a).astype(BF16), w2_ref[b])
            part = term if part is None else part + term
        acc_ref[...] += part

        @pl.when(j == nstep - 1)
        def _():
            y = h_ref[...] + acc_ref[...]
            if loss_head is None:
                o_ref[...] = y
                return

            @pl.when(i == 0)
            def _():
                loss_ref[...] = jnp.zeros_like(loss_ref)
                dfg_ref[...] = jnp.zeros_like(dfg_ref)

            target_rows(lambda copy: copy.wait())

            rstd = lax.rsqrt(jnp.mean(y * y, axis=-1, keepdims=True) + EPS)
            xh = y * rstd
            err = jnp.where(_row_ids(i, rt) >= CHUNK, xh * fg_ref[...] - t_ref[...], 0.0)
            loss_ref[...] += (0.5 / d) * jnp.sum(err * err)
            dy = err * (1.0 / d)
            dfg_ref[...] += _rowsum(dy * xh)
            dxh = dy * fg_ref[...]
            o_ref[...] = rstd * (dxh - xh * jnp.mean(dxh * xh, axis=-1, keepdims=True))

    rows = lambda i, j: (i, 0)
    in_specs = [pl.BlockSpec((rt, d), rows), _resident((1, d)),
                pl.BlockSpec((nb, d, f8), lambda i, j: (j, 0, 0)), pl.BlockSpec((nb, f8, d), lambda i, j: (j, 0, 0))]
    out_specs = [pl.BlockSpec((rt, d), rows), pl.BlockSpec((rt, d), rows), pl.BlockSpec((rt, nb * f8), lambda i, j: (i, j))]
    out_shape = [_sds((t, d), F32), _sds((t, d), BF16), _sds((t, N_DEV * f8), BF16)]
    args = [h, gain, w1g, w2g]
    scratch_shapes = [pltpu.VMEM((rt, d), F32)]
    if loss_head is not None:
        in_specs += [_resident((1, d)), pl.BlockSpec(memory_space=pl.ANY)]
        out_specs += [pl.BlockSpec((8, LANE), lambda i, j: (0, 0)), pl.BlockSpec((1, d), lambda i, j: (0, 0))]
        out_shape += [_sds((8, LANE), F32), _sds((1, d), F32)]
        args += list(loss_head)
        scratch_shapes += [pltpu.VMEM((rt, d), F32), pltpu.SemaphoreType.DMA((1,))]
    return pl.pallas_call(
        body, grid=(t // rt, nstep), in_specs=in_specs, out_specs=out_specs, out_shape=out_shape,
        scratch_shapes=scratch_shapes,
        compiler_params=_cparams("arbitrary" if loss_head is not None else "parallel", "arbitrary"), name=name)(*args)


def _ffn_bwd_x(h, dout, gain, r, w1g, w2g, *, after=None, name):
    t, d = h.shape
    f8 = w1g.shape[-1]
    rt = _row_tile(t, 832)
    nb = FFN_BLOCKS_PER_STEP
    last = N_DEV // nb - 1
    dep_specs, deps = _dep_specs(after)

    def body(*refs):
        h_ref, do_ref, g_ref, r_ref, w1_ref, w2_ref, dh_ref, dhh_ref, dob_ref, dg_ref, du_ref = refs[len(deps):]
        i, j = pl.program_id(0), pl.program_id(1)

        @pl.when(j == 0)
        def _():
            dob_ref[...] = do_ref[...].astype(BF16)
            du_ref[...] = jnp.zeros_like(du_ref)

        part = None
        for b in range(nb):
            cols = slice(b * f8, (b + 1) * f8)
            dhh = (_dot_nt(dob_ref[...], w2_ref[b]) * (2.0 * r_ref[:, cols].astype(F32))).astype(BF16)
            dhh_ref[:, cols] = dhh
            term = _dot_nt(dhh, w1_ref[b])
            part = term if part is None else part + term
        du_ref[...] += part

        @pl.when(j == last)
        def _():
            @pl.when(i == 0)
            def _():
                dg_ref[...] = jnp.zeros_like(dg_ref)

            hv = h_ref[...]
            rstd = lax.rsqrt(jnp.mean(hv * hv, axis=-1, keepdims=True) + EPS)
            xh = hv * rstd
            du = du_ref[...]
            dg_ref[...] += _rowsum(du * xh)
            dxh = du * g_ref[...]
            dh_ref[...] = do_ref[...] + rstd * (dxh - xh * jnp.mean(dxh * xh, axis=-1, keepdims=True))

    rows = lambda i, j: (i, 0)
    return pl.pallas_call(
        body, grid=(t // rt, N_DEV // nb),
        in_specs=dep_specs + [
                  pl.BlockSpec((rt, d), rows), pl.BlockSpec((rt, d), rows), _resident((1, d)),
                  pl.BlockSpec((rt, nb * f8), lambda i, j: (i, j)),
                  pl.BlockSpec((nb, d, f8), lambda i, j: (j, 0, 0)),
                  pl.BlockSpec((nb, f8, d), lambda i, j: (j, 0, 0))],
        out_specs=[pl.BlockSpec((rt, d), rows), pl.BlockSpec((rt, nb * f8), lambda i, j: (i, j)),
                   pl.BlockSpec((rt, d), rows), pl.BlockSpec((1, d), lambda i, j: (0, 0))],
        out_shape=[_sds((t, d), F32), _sds((t, N_DEV * f8), BF16), _sds((t, d), BF16), _sds((1, d), F32)],
        scratch_shapes=[pltpu.VMEM((rt, d), F32)],
        compiler_params=_cparams("arbitrary", "arbitrary"), name=name)(*deps, h, dout, gain, r, w1g, w2g)


def _lane_blocks(width):
    lb = min(LANE, width)
    return [slice(s, s + lb) for s in range(0, width, lb)]


def _conv_rows(src_ref, w_ref, offset, dst_ref, nblk, width, bias_ref=None):
    def blk(rb, carry):
        base = pl.multiple_of(rb * CHUNK, CHUNK)
        for l, ls in enumerate(_lane_blocks(width)):
            acc = jnp.zeros((CHUNK, ls.stop - ls.start), F32)
            if bias_ref is not None:
                acc = acc + bias_ref[:, ls]
            for k in range(CONV_WIDTH):
                acc = acc + w_ref[k:k + 1, ls] * src_ref[l, pl.ds(base + offset(k), CHUNK), :]
            dst_ref[l, pl.ds(base, CHUNK), :] = acc
        return carry

    lax.fori_loop(0, nblk, blk, 0)


def _to_lane_blocks(ref, row0, value):
    for l, ls in enumerate(_lane_blocks(value.shape[1])):
        ref[l, row0:row0 + value.shape[0], :] = value[:, ls]


def _from_lane_blocks(ref):
    return jnp.concatenate([ref[l] for l in range(ref.shape[0])], axis=1)


def _pool_counts(rows, window):
    return jnp.clip(rows - PAD_ROWS + 1, 1, window).astype(F32)


def _trailing_sum(v, window):
    s, sh = v, 1
    while sh < window:
        s = s + pltpu.roll(s, sh, 0)
        sh *= 2
    return s


def _leading_sum(v, window):
    s, sh, n = v, 1, v.shape[0]
    while sh < window:
        s = s + pltpu.roll(s, n - sh, 0)
        sh *= 2
    return s


def _norm_project(h_ref, g_ref, w_t_ref, u_ref, z_ref):
    hv = h_ref[...]
    u = (hv * lax.rsqrt(jnp.mean(hv * hv, axis=-1, keepdims=True) + EPS) * g_ref[...]).astype(BF16)
    u_ref[...] = u
    z_ref[...] = _dot_nt(u, w_t_ref[...])


def _project_back(dz_ref, w_t_ref, h_ref, g_ref, dres_ref, dh_ref, dg_ref, first):
    dx = _dot(dz_ref[...], w_t_ref[...])
    hv = h_ref[...]
    rstd = lax.rsqrt(jnp.mean(hv * hv, axis=-1, keepdims=True) + EPS)
    xh = hv * rstd

    @pl.when(first)
    def _():
        dg_ref[...] = jnp.zeros_like(dg_ref)

    dg_ref[...] += _rowsum(dx * xh)
    dxh = dx * g_ref[...]
    dh_ref[...] = dres_ref[...] + rstd * (dxh - xh * jnp.mean(dxh * xh, axis=-1, keepdims=True))


def _cp_mid_fwd(x, meta, gain, w_in_t, w_out, conv_w, conv_b, ln_g, ln_b, pool_w, pool_scale, *, name):
    seq, d = x.shape
    t = seq + CHUNK
    ein = w_in_t.shape[0]
    cd = conv_b.shape[1]
    pd = pool_scale.shape[1]
    pg = pd // len(POOL_WINDOWS)
    rt = _row_tile(t, 320)
    ntile = t // rt

    def body(x_ref, meta_ref, g_ref, wi_ref, wo_ref, cw_ref, cb_ref, lg_ref, lb_ref, pw_ref, ps_ref,
             ho_ref, o_ref, z_ref, u_ref, cv_ref, h0_ref, gext, pext, conv_s, hbuf, hsem):
        i = pl.program_id(0)
        slot = i % 2
        first_rows = pltpu.make_async_copy(x_ref.at[pl.ds(0, rt - CHUNK)], hbuf.at[0, pl.ds(CHUNK, rt - CHUNK)], hsem.at[0])

        def tile_rows(tile, to):
            return pltpu.make_async_copy(x_ref.at[pl.ds(pl.multiple_of(tile * rt - CHUNK, CHUNK), rt)], hbuf.at[to],
                                         hsem.at[to])

        @pl.when(i == 0)
        def _():
            first_rows.start()
            hbuf[0, 0:PAD_ROWS, :] = jnp.zeros((PAD_ROWS, d), F32)
            hbuf[0, PAD_ROWS:CHUNK, :] = meta_ref[...]
            _to_lane_blocks(gext, 0, jnp.zeros((HALO, cd), F32))
            pext[0:HALO, :] = jnp.zeros((HALO, pd), F32)

        @pl.when(i + 1 < ntile)
        def _():
            tile_rows(i + 1, 1 - slot).start()

        @pl.when(i == 0)
        def _():
            first_rows.wait()

        @pl.when(i > 0)
        def _():
            tile_rows(i, slot).wait()

        h_ref = hbuf.at[slot]
        h0_ref[...] = h_ref[...]
        _norm_project(h_ref, g_ref, wi_ref, u_ref, z_ref)

        _to_lane_blocks(gext, HALO, z_ref[:, 0:cd] * _sigmoid(z_ref[:, cd:2 * cd]))
        pext[HALO:HALO + rt, :] = z_ref[:, 2 * cd:]
        _conv_rows(gext, cw_ref, lambda k: k + HALO - (CONV_WIDTH - 1), conv_s, rt // CHUNK, cd, cb_ref)
        cv = _from_lane_blocks(conv_s)
        cv_ref[...] = cv
        xc = cv - jnp.mean(cv, axis=-1, keepdims=True)
        y = xc * lax.rsqrt(jnp.mean(xc * xc, axis=-1, keepdims=True) + EPS) * lg_ref[...] + lb_ref[...]
        rows = _row_ids(i, rt)
        a = jnp.where(rows >= PAD_ROWS, y * _sigmoid(y), 0.0)
        o_ref[:, 0:cd] = a.astype(BF16)
        for gi, window in enumerate(POOL_WINDOWS):
            ls = slice(gi * pg, (gi + 1) * pg)
            v = pext[:, ls]
            tm = _trailing_sum(v, window)[HALO:] / _pool_counts(rows, window) - v[HALO:]
            p = _dot(tm.astype(BF16), pw_ref[gi]) * ps_ref[:, ls]
            o_ref[:, cd + gi * pg:cd + (gi + 1) * pg] = p.astype(BF16)
        ho_ref[...] = h_ref[...] + _dot(o_ref[...], wo_ref[...])
        gext[:, 0:HALO, :] = gext[:, rt:rt + HALO, :]
        pext[0:HALO, :] = pext[rt:rt + HALO, :]

    nl, lb = len(_lane_blocks(cd)), min(LANE, cd)
    rows = lambda i: (i, 0)
    return pl.pallas_call(
        body, grid=(ntile,),
        in_specs=[pl.BlockSpec(memory_space=pl.ANY), _resident(meta.shape), _resident((1, d)), _resident(w_in_t.shape),
                  _resident(w_out.shape), _resident(conv_w.shape), _resident((1, cd)),
                  _resident((1, cd)), _resident((1, cd)), _resident(pool_w.shape), _resident((1, pd))],
        out_specs=[pl.BlockSpec((rt, d), rows), pl.BlockSpec((rt, cd + pd), rows), pl.BlockSpec((rt, ein), rows),
                   pl.BlockSpec((rt, d), rows), pl.BlockSpec((rt, cd), rows), pl.BlockSpec((rt, d), rows)],
        out_shape=[_sds((t, d), F32), _sds((t, cd + pd), BF16), _sds((t, ein), F32), _sds((t, d), BF16),
                   _sds((t, cd), F32), _sds((t, d), F32)],
        scratch_shapes=[pltpu.VMEM((nl, rt + HALO, lb), F32), pltpu.VMEM((rt + HALO, pd), F32),
                        pltpu.VMEM((nl, rt, lb), F32), pltpu.VMEM((2, rt, d), F32), pltpu.SemaphoreType.DMA((2,))],
        compiler_params=_cparams("arbitrary"), name=name)(x, meta, gain, w_in_t, w_out, conv_w, conv_b, ln_g, ln_b, pool_w,
                                                          pool_scale)


def _cp_mid_bwd(z, cv, h, gain, w_in_t, dh, w_out, conv_w, conv_b, ln_g, ln_b, pool_w, pool_scale, *, after=None, name):
    t, ein = z.shape
    cd = conv_b.shape[1]
    pd = pool_scale.shape[1]
    pg = pd // len(POOL_WINDOWS)
    rt = _row_tile(t, 320)
    ntile = t // rt
    per = rt // CHUNK
    dep_specs, deps = _dep_specs(after)

    def body(*refs):
        (z_ref, zh_ref, cv_ref, h_ref, g_ref, wi_ref, dh_ref, wo_ref, cw_ref, cb_ref, lg_ref, lb_ref, pw_ref, ps_ref,
         dx_ref, dfirst_ref, dg_ref, dz_ref, dcw_ref, dcb_ref, dlg_ref, dlb_ref, dpw_ref, dps_ref,
         gext, pext, conv_s, dcv, dsp, dhi, dx_sem) = refs[len(deps):]
        step = pl.program_id(0)
        tile = ntile - 1 - step
        slot = step % 2
        last_slot = (ntile - 1) % 2
        first_rows = pltpu.make_async_copy(dhi.at[last_slot, pl.ds(CHUNK, rt - CHUNK)], dx_ref.at[pl.ds(0, rt - CHUNK)],
                                           dx_sem.at[last_slot])

        def tile_rows(tile, slot):
            return pltpu.make_async_copy(dhi.at[slot], dx_ref.at[pl.ds(pl.multiple_of(tile * rt - CHUNK, CHUNK), rt)],
                                         dx_sem.at[slot])

        dcat = _dot_nt(dh_ref[...].astype(BF16), wo_ref[...])

        @pl.when(step >= 2)
        def _():
            tile_rows(tile + 2, slot).wait()

        @pl.when(step == 0)
        def _():
            for ref in (dcw_ref, dcb_ref, dlg_ref, dlb_ref, dpw_ref, dps_ref):
                ref[...] = jnp.zeros_like(ref)
            _to_lane_blocks(dcv, rt, jnp.zeros((HALO, cd), F32))
            dsp[rt:rt + HALO, :] = jnp.zeros((HALO, pd), F32)

        keep = jnp.where(tile > 0, 1.0, 0.0)
        zh = zh_ref[CHUNK - HALO:CHUNK, :]
        _to_lane_blocks(gext, 0, keep * zh[:, 0:cd] * _sigmoid(zh[:, cd:2 * cd]))
        pext[0:HALO, :] = keep * zh[:, 2 * cd:]
        za = z_ref[:, 0:cd]
        sg = _sigmoid(z_ref[:, cd:2 * cd])
        _to_lane_blocks(gext, HALO, za * sg)
        pext[HALO:HALO + rt, :] = z_ref[:, 2 * cd:]
        cv = cv_ref[...]
        xc = cv - jnp.mean(cv, axis=-1, keepdims=True)
        rstd = lax.rsqrt(jnp.mean(xc * xc, axis=-1, keepdims=True) + EPS)
        xh = xc * rstd
        y = xh * lg_ref[...] + lb_ref[...]
        sy = _sigmoid(y)
        rows = _row_ids(tile, rt)
        da = jnp.where(rows >= PAD_ROWS, dcat[:, 0:cd], 0.0)
        dy = da * (sy * (1.0 + y * (1.0 - sy)))
        dlg_ref[...] += _rowsum(dy * xh)
        dlb_ref[...] += _rowsum(dy)
        dxh = dy * lg_ref[...]
        dconv = rstd * (dxh - jnp.mean(dxh, axis=-1, keepdims=True) - xh * jnp.mean(dxh * xh, axis=-1, keepdims=True))
        dcb_ref[...] += _rowsum(dconv)
        _to_lane_blocks(dcv, 0, dconv)
        for l, ls in enumerate(_lane_blocks(cd)):
            def acc_rows(rb, accs, l=l):
                base = pl.multiple_of(rb * CHUNK, CHUNK)
                d_blk = dcv[l, pl.ds(base, CHUNK), :]
                out = []
                for k in range(CONV_WIDTH):
                    prod = d_blk * gext[l, pl.ds(base + k + HALO - (CONV_WIDTH - 1), CHUNK), :]
                    part = prod[0:8]
                    for s in range(8, CHUNK, 8):
                        part = part + prod[s:s + 8]
                    out.append(accs[k] + part)
                return tuple(out)

            zero = jnp.zeros((8, ls.stop - ls.start), F32)
            accs = lax.fori_loop(0, per, acc_rows, (zero,) * CONV_WIDTH)
            for k in range(CONV_WIDTH):
                dcw_ref[k:k + 1, ls] += _rowsum(accs[k])
        _conv_rows(dcv, cw_ref, lambda k: CONV_WIDTH - 1 - k, conv_s, per, cd)
        dglu = _from_lane_blocks(conv_s)
        dz_ref[:, 0:cd] = (dglu * sg).astype(BF16)
        dz_ref[:, cd:2 * cd] = (dglu * za * sg * (1.0 - sg)).astype(BF16)
        dcv[:, rt:rt + HALO, :] = dcv[:, 0:HALO, :]
        for gi, window in enumerate(POOL_WINDOWS):
            ls = slice(gi * pg, (gi + 1) * pg)
            v = pext[:, ls]
            cnt = _pool_counts(rows, window)
            tm = (_trailing_sum(v, window)[HALO:] / cnt - v[HALO:]).astype(BF16)
            dp = dcat[:, cd + gi * pg:cd + (gi + 1) * pg]
            dps_ref[:, ls] += _rowsum(dp * _dot(tm, pw_ref[gi]))
            dpl = (dp * ps_ref[:, ls]).astype(BF16)
            dpw_ref[gi] += _dot_tn(tm, dpl)
            dtm = _dot_nt(dpl, pw_ref[gi])
            dsp[0:rt, ls] = dtm / cnt
            dpin = _leading_sum(dsp[:, ls], window)[0:rt] - dtm
            dz_ref[:, 2 * cd + gi * pg:2 * cd + (gi + 1) * pg] = dpin.astype(BF16)
        dsp[rt:rt + HALO, :] = dsp[0:HALO, :]

        _project_back(dz_ref, wi_ref, h_ref, g_ref, dh_ref, dhi.at[slot], dg_ref, step == 0)

        @pl.when(tile > 0)
        def _():
            tile_rows(tile, slot).start()

        @pl.when(tile == 0)
        def _():
            first_rows.start()
            dfirst_ref[...] = dhi[last_slot, 0:CHUNK, :]
            if ntile > 1:
                tile_rows(1, 1 - last_slot).wait()
            first_rows.wait()

    d = h.shape[1]
    back = lambda i: (ntile - 1 - i, 0)
    halo_idx = lambda i: (jnp.maximum((ntile - 1 - i) * per - 1, 0), 0)
    const2 = lambda i: (0, 0)
    nl, lb = len(_lane_blocks(cd)), min(LANE, cd)
    return pl.pallas_call(
        body, grid=(ntile,),
        in_specs=dep_specs + [
                  pl.BlockSpec((rt, ein), back), pl.BlockSpec((CHUNK, ein), halo_idx), pl.BlockSpec((rt, cd), back),
                  pl.BlockSpec((rt, d), back),
                  _resident((1, d)), _resident(w_in_t.shape), pl.BlockSpec((rt, d), back), _resident(w_out.shape),
                  _resident(conv_w.shape), _resident((1, cd)), _resident((1, cd)), _resident((1, cd)),
                  _resident(pool_w.shape), _resident((1, pd))],
        out_specs=[pl.BlockSpec(memory_space=pl.ANY), pl.BlockSpec((CHUNK, d), const2), pl.BlockSpec((1, d), const2),
                   pl.BlockSpec((rt, ein), back), pl.BlockSpec(conv_w.shape, const2), pl.BlockSpec((1, cd), const2),
                   pl.BlockSpec((1, cd), const2), pl.BlockSpec((1, cd), const2),
                   pl.BlockSpec(pool_w.shape, lambda i: (0, 0, 0)), pl.BlockSpec((1, pd), const2)],
        out_shape=[_sds((t - CHUNK, d), F32), _sds((CHUNK, d), F32), _sds((1, d), F32),
                   _sds((t, ein), BF16), _sds(conv_w.shape, F32), _sds((1, cd), F32), _sds((1, cd), F32),
                   _sds((1, cd), F32), _sds(pool_w.shape, F32), _sds((1, pd), F32)],
        scratch_shapes=[pltpu.VMEM((nl, rt + HALO, lb), F32), pltpu.VMEM((rt + HALO, pd), F32), pltpu.VMEM((nl, rt, lb), F32),
                        pltpu.VMEM((nl, rt + HALO, lb), F32), pltpu.VMEM((rt + HALO, pd), F32), pltpu.VMEM((2, rt, d), F32),
                        pltpu.SemaphoreType.DMA((2,))],
        compiler_params=_cparams("arbitrary"), name=name)(*deps, z, z, cv, h, gain, w_in_t, dh, w_out, conv_w, conv_b, ln_g,
                                                          ln_b, pool_w, pool_scale)


def _log_decay(r, gw_ref, gb_ref, rows):
    gp = _dot(r.astype(BF16), gw_ref[...]) + gb_ref[...]
    log_sig = jnp.minimum(gp, 0.0) - jnp.log(1.0 + jnp.exp(-jnp.abs(gp)))
    return gp, jnp.where(rows >= PAD_ROWS, log_sig / GATE_NORM, 0.0)


def _tri(strict):
    r = lax.broadcasted_iota(jnp.int32, (CHUNK, CHUNK), 0)
    c = lax.broadcasted_iota(jnp.int32, (CHUNK, CHUNK), 1)
    return jnp.where(c < r if strict else c <= r, 1.0, 0.0).astype(BF16)


def _tri_dot(tri, a):
    hi = a.astype(BF16)
    rest = a - hi.astype(F32)
    mid = rest.astype(BF16)
    lo = (rest - mid.astype(F32)).astype(BF16)
    return _dot(tri, hi) + _dot(tri, mid) + _dot(tri, lo)


def _gla_mid_fwd(h, gain, w_in_blocks, w_out, gate_w, gate_b, head_g, *, name):
    t = h.shape[0]
    nblk, rpb = w_in_blocks.shape[:2]
    dk = gate_b.shape[1]
    hv = head_g.shape[1]
    hk = dk // HEADS
    dv = hv * HEADS
    r_at = 2 * dk + 2 * dv
    assert nblk * rpb == r_at + GATE_RANK
    zw = r_at + GATE_PAD
    rt = _row_tile(t, 320)
    per = rt // CHUNK
    scale = hk ** -0.5

    def body(h_ref, g_ref, wb_ref, wo_ref, gw_ref, gb_ref, hg_ref, ho_ref, o_ref, st_ref, z_ref, u_ref, wi_ref,
             s_ref, la_ref, dec_ref):
        i = pl.program_id(0)

        @pl.when(i == 0)
        def _():
            s_ref[...] = jnp.zeros_like(s_ref)
            for b in range(nblk):
                wi_ref[b * rpb:(b + 1) * rpb, :] = wb_ref[b]
            wi_ref[nblk * rpb:, :] = jnp.zeros((zw - nblk * rpb, wi_ref.shape[1]), BF16)

        _norm_project(h_ref, g_ref, wi_ref, u_ref, z_ref)

        _, la = _log_decay(z_ref[:, r_at:r_at + GATE_PAD], gw_ref, gb_ref, _row_ids(i, rt))
        la_ref[...] = la
        tri = _tri(False)

        def chunk_rows(c):
            return slice(c * CHUNK, (c + 1) * CHUNK)

        def decays(c, carry):
            rows = chunk_rows(c)
            la_c = la_ref[rows, :]
            cum = _tri_dot(tri, la_c)
            dec_ref[rows, :] = jnp.exp(_rowsum(la_c) - cum)
            return carry

        def states(c, carry):
            rows = chunk_rows(c)
            etot = jnp.exp(_rowsum(la_ref[rows, :]))
            for hd in range(HEADS):
                ks = slice(hd * hk, (hd + 1) * hk)
                kd = z_ref[rows, dk + hd * hk:dk + (hd + 1) * hk] * dec_ref[rows, ks]
                v = z_ref[rows, 2 * dk + hd * hv:2 * dk + (hd + 1) * hv]
                s_new = s_ref[hd] * etot[:, ks] + _dot_tn(v.astype(BF16), kd.astype(BF16))
                s_ref[hd] = s_new
                st_ref[c, hd] = s_new
            return carry

        def outputs(c, carry):
            rows = chunk_rows(c)
            for hd in range(HEADS):
                q = z_ref[rows, hd * hk:(hd + 1) * hk] * scale
                g = z_ref[rows, 2 * dk + dv + hd * hv:2 * dk + dv + (hd + 1) * hv]
                o = _dot_nt(q.astype(BF16), st_ref[c, hd].astype(BF16))
                on = o * lax.rsqrt(jnp.mean(o * o, axis=-1, keepdims=True) + EPS) * hg_ref[...]
                o_ref[rows, hd * hv:(hd + 1) * hv] = (on * (g * _sigmoid(g))).astype(BF16)
            return carry

        for phase in (decays, states, outputs):
            for c in range(per):
                phase(c, 0)
        ho_ref[...] = h_ref[...] + _dot(o_ref[...], wo_ref[...])

    d = h.shape[1]
    rows = lambda i: (i, 0)
    return pl.pallas_call(
        body, grid=(t // rt,),
        in_specs=[pl.BlockSpec((rt, d), rows), _resident((1, d)), _resident(w_in_blocks.shape), _resident(w_out.shape),
                  _resident(gate_w.shape), _resident((1, dk)), _resident((1, hv))],
        out_specs=[pl.BlockSpec((rt, d), rows), pl.BlockSpec((rt, dv), rows),
                   pl.BlockSpec((per, HEADS, hv, hk), lambda i: (i, 0, 0, 0)), pl.BlockSpec((rt, zw), rows),
                   pl.BlockSpec((rt, d), rows), pl.BlockSpec((zw, d), lambda i: (0, 0))],
        out_shape=[_sds((t, d), F32), _sds((t, dv), BF16), _sds((t // CHUNK, HEADS, hv, hk), F32), _sds((t, zw), F32),
                   _sds((t, d), BF16), _sds((zw, d), BF16)],
        scratch_shapes=[pltpu.VMEM((HEADS, hv, hk), F32), pltpu.VMEM((rt, dk), F32), pltpu.VMEM((rt, dk), F32)],
        compiler_params=_cparams("arbitrary"), name=name)(h, gain, w_in_blocks, w_out, gate_w, gate_b, head_g)


def _gla_mid_bwd(z, h, gain, w_in_t, dh, w_out, states, gate_w, gate_b, head_g, *, after=None, name):
    t = z.shape[0]
    dk = gate_b.shape[1]
    hv = head_g.shape[1]
    hk = dk // HEADS
    dv = hv * HEADS
    r_at = 2 * dk + 2 * dv
    rt = _row_tile(t, 320)
    ntile = t // rt
    per = rt // CHUNK
    scale = hk ** -0.5
    dep_specs, deps = _dep_specs(after)

    def body(*refs):
        (z_ref, h_ref, g_ref, wi_ref, dh_ref, wo_ref, st_ref, stp_ref, gw_ref, gb_ref, hg_ref,
         dhi_ref, dg_ref, dz_ref, dgw_ref, dgb_ref, dhg_ref,
         ds_ref, la_ref, dla_ref, dec_ref, dos_ref, e_ref, do_ref) = refs[len(deps):]
        step = pl.program_id(0)
        tile = ntile - 1 - step
        do_ref[...] = _dot_nt(dh_ref[...].astype(BF16), wo_ref[...])

        @pl.when(step == 0)
        def _():
            ds_ref[...] = jnp.zeros_like(ds_ref)
            dgw_ref[...] = jnp.zeros_like(dgw_ref)
            dgb_ref[...] = jnp.zeros_like(dgb_ref)
            dhg_ref[...] = jnp.zeros_like(dhg_ref)

        rows_id = _row_ids(tile, rt)
        r = z_ref[:, r_at:r_at + GATE_PAD]
        gp, la = _log_decay(r, gw_ref, gb_ref, rows_id)
        la_ref[...] = la
        tri, tri_strict = _tri(False), _tri(True)
        keep = jnp.where(tile > 0, 1.0, 0.0)

        def chunk_rows(c):
            return slice(c * CHUNK, (c + 1) * CHUNK)

        def recompute(c, dhg):
            rows = chunk_rows(c)
            la_c = la_ref[rows, :]
            cum = _tri_dot(tri, la_c)
            dec_ref[rows, :] = jnp.exp(_rowsum(la_c) - cum)
            for hd in range(HEADS):
                q = (z_ref[rows, hd * hk:(hd + 1) * hk] * scale).astype(BF16)
                g = z_ref[rows, 2 * dk + dv + hd * hv:2 * dk + dv + (hd + 1) * hv]
                s_b = st_ref[c, hd].astype(BF16)
                o = _dot_nt(q, s_b)
                rstd = lax.rsqrt(jnp.mean(o * o, axis=-1, keepdims=True) + EPS)
                oh = o * rstd
                sg = _sigmoid(g)
                d_og = do_ref[rows, hd * hv:(hd + 1) * hv]
                dz_ref[rows, 2 * dk + dv + hd * hv:2 * dk + dv + (hd + 1) * hv] = (
                    d_og * oh * hg_ref[...] * (sg * (1.0 + g * (1.0 - sg)))).astype(BF16)
                don = d_og * (g * sg)
                dhg = dhg + _rowsum(don * oh)
                doh = don * hg_ref[...]
                d_o = (rstd * (doh - oh * jnp.mean(doh * oh, axis=-1, keepdims=True))).astype(BF16)
                dos_ref[rows, hd * hv:(hd + 1) * hv] = d_o
                dz_ref[rows, hd * hk:(hd + 1) * hk] = (_dot(d_o, s_b) * scale).astype(BF16)
            return dhg

        def recurrence(cc, carry):
            c = per - 1 - cc
            rows = chunk_rows(c)
            etot = jnp.exp(_rowsum(la_ref[rows, :]))
            for hd in range(HEADS):
                ks = slice(hd * hk, (hd + 1) * hk)
                q = (z_ref[rows, hd * hk:(hd + 1) * hk] * scale).astype(BF16)
                dec = dec_ref[rows, ks]
                kd = z_ref[rows, dk + hd * hk:dk + (hd + 1) * hk] * dec
                v = z_ref[rows, 2 * dk + hd * hv:2 * dk + (hd + 1) * hv].astype(BF16)
                s_prev = st_ref[c - 1, hd] if c > 0 else keep * stp_ref[0, hd]
                ds_t = ds_ref[hd] + _dot_tn(dos_ref[rows, hd * hv:(hd + 1) * hv], q)
                ds_b = ds_t.astype(BF16)
                dkd = _dot(v, ds_b)
                dz_ref[rows, 2 * dk + hd * hv:2 * dk + (hd + 1) * hv] = _dot_nt(kd.astype(BF16), ds_b).astype(BF16)
                dtot = etot[:, ks] * _rowsum(ds_t * s_prev)
                ds_ref[hd] = ds_t * etot[:, ks]
                dz_ref[rows, dk + hd * hk:dk + (hd + 1) * hk] = (dkd * dec).astype(BF16)
                e_ref[rows, ks] = dkd * kd
                dla_ref[rows, ks] = jnp.broadcast_to(dtot, (CHUNK, hk))
            return carry

        def decay_cotangent(c, carry):
            rows = chunk_rows(c)
            dla_ref[rows, :] += _tri_dot(tri_strict, e_ref[rows, :])
            return carry

        dhg = jnp.zeros((1, hv), F32)
        for c in range(per):
            dhg = recompute(c, dhg)
        dhg_ref[...] += dhg
        for phase in (recurrence, decay_cotangent):
            for c in range(per):
                phase(c, 0)
        dla = jnp.where(rows_id >= PAD_ROWS, dla_ref[...], 0.0)
        dgp = dla * (1.0 / GATE_NORM) * (1.0 - _sigmoid(gp))
        dgb_ref[...] += _rowsum(dgp)
        dgp_b = dgp.astype(BF16)
        dgw_ref[...] += _dot_tn(r.astype(BF16), dgp_b)
        dz_ref[:, r_at:r_at + GATE_PAD] = _dot_nt(dgp_b, gw_ref[...]).astype(BF16)
        _project_back(dz_ref, wi_ref, h_ref, g_ref, dh_ref, dhi_ref, dg_ref, step == 0)

    d = h.shape[1]
    back = lambda i: (ntile - 1 - i, 0)
    const2 = lambda i: (0, 0)
    return pl.pallas_call(
        body, grid=(ntile,),
        in_specs=dep_specs + [
                  pl.BlockSpec((rt, z.shape[1]), back), pl.BlockSpec((rt, d), back), _resident((1, d)),
                  _resident(w_in_t.shape), pl.BlockSpec((rt, d), back), _resident(w_out.shape),
                  pl.BlockSpec((per, HEADS, hv, hk), lambda i: (ntile - 1 - i, 0, 0, 0)),
                  pl.BlockSpec((1, HEADS, hv, hk), lambda i: (jnp.maximum((ntile - 1 - i) * per - 1, 0), 0, 0, 0)),
                  _resident(gate_w.shape), _resident((1, dk)), _resident((1, hv))],
        out_specs=[pl.BlockSpec((rt, d), back), pl.BlockSpec((1, d), const2), pl.BlockSpec((rt, z.shape[1]), back),
                   pl.BlockSpec(gate_w.shape, const2), pl.BlockSpec((1, dk), const2), pl.BlockSpec((1, hv), const2)],
        out_shape=[_sds((t, d), F32), _sds((1, d), F32), _sds(z.shape, BF16), _sds(gate_w.shape, F32),
                   _sds((1, dk), F32), _sds((1, hv), F32)],
        scratch_shapes=[pltpu.VMEM((HEADS, hv, hk), F32), pltpu.VMEM((rt, dk), F32), pltpu.VMEM((rt, dk), F32),
                        pltpu.VMEM((rt, dk), F32), pltpu.VMEM((rt, dv), BF16), pltpu.VMEM((rt, dk), F32),
                        pltpu.VMEM((rt, dv), F32)],
        compiler_params=_cparams("arbitrary"), name=name)(*deps, z, h, gain, w_in_t, dh, w_out, states, states, gate_w,
                                                          gate_b, head_g)


def _adamw_math(w, g, m, v):
    m = ADAM_B1 * m + (1.0 - ADAM_B1) * g
    v = ADAM_B2 * v + (1.0 - ADAM_B2) * (g * g)
    m_hat = m / (1.0 - ADAM_B1 ** ADAM_STEP)
    v_hat = v / (1.0 - ADAM_B2 ** ADAM_STEP)
    return -ADAM_LR * (m_hat / (jnp.sqrt(v_hat) + ADAM_EPS) + ADAM_WD * w), m, v


N_CHIP = N_DEV // 2
BLOCK_ELEMS = 128 * 1024


def _my_slot():
    return 4 * lax.axis_index("x") + 2 * lax.axis_index("y") + lax.axis_index("c")


def _row_block(r, c):
    cap = max(8, BLOCK_ELEMS // (-(-c // LANE) * LANE))
    return max([b for b in range(8, r + 1, 8) if r % b == 0 and b <= cap] or [r])


def _blocks(r, c):
    rb = _row_block(r, c)
    if rb < r or r * c <= BLOCK_ELEMS:
        return rb, c
    return r, max([b for b in (512, 256, LANE) if c % b == 0 and r * b <= BLOCK_ELEMS] or [c])


def _reduce_adam(parts, w, m, v, *, by_row=False, after=None, name):
    nl, r, c = (1, w.shape[0], w.shape[2]) if by_row else w.shape
    rb, cb = _blocks(r, c)
    dep_specs, deps = _dep_specs(after)
    whole = (slice(None), 0, slice(None)) if by_row else Ellipsis

    def body(*refs):
        me = refs[0][0]
        refs = refs[1 + len(deps):]
        p_refs = refs[:2 * nl]
        w_ref, m_ref, v_ref, g_out, d_out, m_out, v_out = refs[2 * nl:]
        layer = pl.program_id(0)
        for li in range(nl):
            @pl.when(layer == li)
            def _(li=li):
                own_ref, land_ref = p_refs[2 * li], p_refs[2 * li + 1]
                mine = own_ref[...].astype(F32)
                g = None
                for dev in range(N_DEV):
                    term = jnp.where(me == dev, mine, land_ref[dev].astype(F32))
                    g = term if g is None else g + term
                g_out[whole] = g
                d_out[whole], m_out[whole], v_out[whole] = _adamw_math(w_ref[whole], g, m_ref[whole], v_ref[whole])

    if by_row:
        blk = pl.BlockSpec((rb, 1, cb), lambda l, i, j, me: (i, 0, j))
    else:
        blk = pl.BlockSpec((None, rb, cb), lambda l, i, j, me: (l, i, j))
    p_specs = []
    for li in range(nl):
        p_specs += [
            pl.BlockSpec((None, rb, cb), lambda l, i, j, me, li=li: (me[0], jnp.where(l == li, i, 0), jnp.where(l == li, j, 0))),
            pl.BlockSpec((N_DEV, rb, cb), lambda l, i, j, me, li=li: (0, jnp.where(l == li, i, 0), jnp.where(l == li, j, 0)))]
    flat = [p for pair in parts for p in pair]
    grid_spec = pltpu.PrefetchScalarGridSpec(
        num_scalar_prefetch=1, grid=(nl, r // rb, c // cb), in_specs=dep_specs + p_specs + [blk, blk, blk],
        out_specs=[blk] * 4)
    return pl.pallas_call(
        body, grid_spec=grid_spec, out_shape=[_sds(w.shape, F32)] * 4,
        compiler_params=_cparams("arbitrary", "arbitrary", "arbitrary"), name=name)(
        _my_slot().reshape(1), *deps, *flat, w, m, v)


def _sum8(own, landed, *, name):
    def body(own_ref, land_ref, o_ref):
        me = _my_slot()
        total = None
        for dev in range(N_DEV):
            term = jnp.where(me == dev, own_ref[...], land_ref[dev])
            total = term if total is None else total + term
        o_ref[...] = total

    return pl.pallas_call(body, out_shape=_sds(own.shape, F32), name=name)(own, landed)


def _adam_small(own, landed, split, w, m, v, *, name):
    n = len(w)

    def body(*refs):
        own_refs, land_refs, w_refs, m_refs, v_refs = (refs[k * n:(k + 1) * n] for k in range(5))
        outs = refs[5 * n:]
        me = _my_slot()
        for k in range(n):
            mine = own_refs[k][me] if split[k] else own_refs[k][...]
            g = None
            for dev in range(N_DEV):
                term = jnp.where(me == dev, mine, land_refs[k][dev])
                g = term if g is None else g + term
            outs[4 * k][...] = g
            outs[4 * k + 1][...], outs[4 * k + 2][...], outs[4 * k + 3][...] = _adamw_math(
                w_refs[k][...], g, m_refs[k][...], v_refs[k][...])

    out = pl.pallas_call(body, out_shape=[_sds(a.shape, F32) for a in w for _ in range(4)],
                         compiler_params=pltpu.CompilerParams(vmem_limit_bytes=V7X_VMEM_LIMIT), name=name)(
        *own, *landed, *w, *m, *v)
    return [tuple(out[4 * k:4 * k + 4]) for k in range(n)]


_HBM = pl.BlockSpec(memory_space=pltpu.HBM)
_SEM = pl.BlockSpec(memory_space=pltpu.SEMAPHORE)
_DATAFLOW = pltpu.SideEffectType.DATAFLOW_SIDE_EFFECTING


def _plan_to_all(src, land):
    x, y, c = lax.axis_index("x"), lax.axis_index("y"), lax.axis_index("c")
    return [(src, land.at[_my_slot()], (x ^ ((d >> 2) & 1), y ^ ((d >> 1) & 1), c ^ (d & 1))) for d in range(1, N_DEV)]


def _plan_split_to_all(src, land):
    x, y, c = lax.axis_index("x"), lax.axis_index("y"), lax.axis_index("c")
    peers = [(x ^ ((d >> 2) & 1), y ^ ((d >> 1) & 1), c ^ (d & 1)) for d in range(1, N_DEV)]
    return [(src.at[4 * px + 2 * py + pc], land.at[_my_slot()], (px, py, pc)) for px, py, pc in peers]


_PLAN_COPIES = {_plan_to_all: N_DEV - 1, _plan_split_to_all: N_DEV - 1}


def _plans(plan, n):
    return list(plan) if isinstance(plan, (list, tuple)) else [plan] * n


def _exchange_copies(plan, ins, lands, send, recv):
    copies, sem = [], 0
    for p, src, land in zip(_plans(plan, len(lands)), ins, lands):
        for s, dst, dev in p(src, land):
            copies.append(pltpu.make_async_remote_copy(
                src_ref=s, dst_ref=dst, send_sem=send.at[sem], recv_sem=recv.at[sem],
                device_id=dev, device_id_type=pl.DeviceIdType.MESH))
            sem += 1
    return copies


def _place_own(srcs, *, after=None, name):
    dep_specs, deps = _dep_specs(after)
    n = len(srcs)
    arrays, in_specs, shapes = [], [], []
    for a, dtype in srcs:
        if isinstance(a, tuple):
            a, layer = a
            in_specs.append(pl.BlockSpec((None,) + a.shape[1:], lambda i, layer=layer: (layer, 0, 0)))
            shapes.append(a.shape[1:])
        else:
            in_specs.append(pl.BlockSpec(a.shape, lambda i: (0, 0)))
            shapes.append(a.shape)
        arrays.append(a)
    dtypes = [dtype for _, dtype in srcs]

    def body(*refs):
        refs = refs[len(deps):]
        a_refs, o_refs, cast_refs, sem = refs[:n], refs[n:2 * n], refs[2 * n:3 * n], refs[3 * n]
        me = _my_slot()
        copies = []
        for k in range(n):
            cast_refs[k][...] = a_refs[k][...].astype(dtypes[k])
            copies.append(pltpu.make_async_copy(cast_refs[k], o_refs[k].at[me], sem.at[k]))
            copies[-1].start()
        for cp in copies:
            cp.wait()

    return pl.pallas_call(
        body, grid=(1,), in_specs=dep_specs + in_specs, out_specs=[pl.BlockSpec(memory_space=pl.ANY)] * n,
        out_shape=[_sds((N_DEV,) + shape, dtype) for shape, dtype in zip(shapes, dtypes)],
        scratch_shapes=[pltpu.VMEM(shape, dtype) for shape, dtype in zip(shapes, dtypes)] + [pltpu.SemaphoreType.DMA((n,))],
        compiler_params=pltpu.CompilerParams(vmem_limit_bytes=V7X_VMEM_LIMIT), name=name)(*deps, *arrays)


def _plan_gather_first(land, _):
    x, y, c = lax.axis_index("x"), lax.axis_index("y"), lax.axis_index("c")
    mine = land.at[_my_slot()]
    return [(mine, mine, (x, y, 1 - c))] + [(mine, mine, (x ^ (d >> 1), y ^ (d & 1), c)) for d in range(1, N_CHIP)]


def _plan_gather_relay(land, _):
    x, y, c = lax.axis_index("x"), lax.axis_index("y"), lax.axis_index("c")
    slots = [land.at[4 * (x ^ (d >> 1)) + 2 * (y ^ (d & 1)) + c] for d in range(1, N_CHIP)]
    return [(s, s, (x, y, 1 - c)) for s in slots]


def _plan_gather_direct(land, _):
    return _plan_to_all(land.at[_my_slot()], land)


_PLAN_COPIES[_plan_gather_first] = N_CHIP
_PLAN_COPIES[_plan_gather_relay] = N_CHIP - 1
_PLAN_COPIES[_plan_gather_direct] = N_DEV - 1


def _exchange_start(plan, arrs, lands, *, after=None, name):
    bufs = list(lands) if arrs is None else list(arrs) + list(lands)
    n, nb = len(lands), len(bufs)
    nsem = sum(_PLAN_COPIES[p] for p in _plans(plan, n))
    dep_specs, deps = _dep_specs(after)

    def body(*refs):
        ins, land_refs = refs[:n], refs[nb - n:nb]
        send, recv = refs[nb + len(deps)], refs[nb + len(deps) + 1]
        for cp in _exchange_copies(plan, ins, land_refs, send, recv):
            cp.start()
        refs[-1][...] = jnp.zeros_like(refs[-1])

    out = pl.pallas_call(
        body, name=name,
        out_shape=(pltpu.SemaphoreType.DMA((nsem,)), pltpu.SemaphoreType.DMA((nsem,)),
                   *[pltpu.HBM(a.shape, a.dtype) for a in bufs], _sds((8, LANE), F32)),
        in_specs=[_HBM] * nb + dep_specs,
        out_specs=(_SEM, _SEM, *([_HBM] * nb), pl.BlockSpec(memory_space=pltpu.VMEM)),
        input_output_aliases={i: 2 + i for i in range(nb)},
        compiler_params=pltpu.CompilerParams(has_side_effects=_DATAFLOW),
    )(*[pltpu.with_memory_space_constraint(a, pltpu.HBM) for a in bufs], *deps)
    return (plan, n, out[0], out[1], list(out[2:2 + nb])), out[-1]


def _exchange_now(plan, lands, *, name):
    n = len(lands)
    nsem = sum(_PLAN_COPIES[p] for p in _plans(plan, n))

    def body(*refs):
        land_refs, send, recv = refs[n:2 * n], refs[2 * n], refs[2 * n + 1]
        copies = _exchange_copies(plan, land_refs, land_refs, send, recv)
        for cp in copies:
            cp.start()
        for cp in copies:
            cp.wait_send()
            cp.wait_recv()

    hbm = pl.BlockSpec(memory_space=pl.ANY)
    return pl.pallas_call(
        body, in_specs=[hbm] * n, out_specs=[hbm] * n, out_shape=[_sds(a.shape, a.dtype) for a in lands],
        input_output_aliases={i: i for i in range(n)},
        scratch_shapes=[pltpu.SemaphoreType.DMA((nsem,)), pltpu.SemaphoreType.DMA((nsem,))], name=name)(*lands)


def _exchange_wait(state, after, *, name):
    plan, n, send_sem, recv_sem, bufs = state
    nb = len(bufs)
    after = list(after) if isinstance(after, (list, tuple)) else [after]

    def body(*refs):
        ins, land_refs, send, recv = refs[:n], refs[nb - n:nb], refs[nb], refs[nb + 1]
        for cp in _exchange_copies(plan, ins, land_refs, send, recv):
            cp.wait_send()
            cp.wait_recv()

    out = pl.pallas_call(
        body, name=name, out_shape=[pltpu.HBM(a.shape, a.dtype) for a in bufs],
        in_specs=[_HBM] * nb + [_SEM, _SEM] + [pl.BlockSpec(memory_space=pl.ANY)] * len(after), out_specs=[_HBM] * nb,
        input_output_aliases={i: i for i in range(nb)},
        compiler_params=pltpu.CompilerParams(has_side_effects=_DATAFLOW),
    )(*bufs, send_sem, recv_sem, *after)
    return list(out[:n]), list(out[nb - n:])


def _dep_specs(after):
    return ([], []) if after is None else ([pl.BlockSpec(memory_space=pl.ANY)], [after])


def _undo_column_split(g):
    return jnp.transpose(g, (1, 0, 2)).reshape(g.shape[1], N_DEV * g.shape[2])


def _column_split(a):
    r, c = a.shape
    return jnp.transpose(a.reshape(r, N_DEV, c // N_DEV), (1, 0, 2))


class _WholeWeights:
    def __init__(self, groups):
        self.groups = groups
        self.grads = {}

    def fetch(self, group, after):
        return self.groups[group]

    def emit(self, group, grads):
        self.grads.update(grads)
        return None


def _local_step(x, target, replicated, src):
    d = x.shape[1]
    mix_g, ffn_g = replicated["mix_g"], replicated["ffn_g"]
    cp = src.fetch("cp", [])
    cp_mid = (cp["conv_w"], replicated["conv_b"], replicated["ln_g"], replicated["ln_b"], replicated["pool_w"],
              replicated["pool_scale"])

    h1, cat, z0, u0, cv0, h0 = _cp_mid_fwd(x, cp["meta"], mix_g[0:1], cp["cp_w_in_t"], cp["cp_w_out"], *cp_mid,
                                           name="cp_mixer")
    ffn0 = src.fetch("ffn0", h1)
    h2, uf0, rf0 = _ffn_fwd(h1, ffn_g[0:1], ffn0["w1"], ffn0["w2"], name="ffn0")
    gla = src.fetch("gla", h2)
    gla_mid = (gla["gate_w"], gla["gate_b"], gla["head_g"])
    h3, og, states, z1, u1, gla_w_in_t = _gla_mid_fwd(h2, mix_g[1:2], gla["gla_w_in_t"], gla["gla_w_out"], *gla_mid,
                                                      name="gla_mixer")
    ffn1 = src.fetch("ffn1", h3)
    dh4, uf1, rf1, loss, d_final_g = _ffn_fwd(h3, ffn_g[1:2], ffn1["w1"], ffn1["w2"],
                                              loss_head=(replicated["final_g"], target), name="ffn1_loss")

    dh3, dhh1, dob1, dffn_g1 = _ffn_bwd_x(h3, dh4, ffn_g[1:2], rf1, ffn1["w1"], ffn1["w2"], name="ffn1_bwd_x")
    sent = src.emit("ffn1_w1", dict(w1=_linear_bwd_w(uf1, dhh1, column_blocks=N_DEV, name="ffn1_dw1")))
    sent = src.emit("ffn1_w2", dict(w2=_linear_bwd_w(rf1, dob1, square_x=True, after=sent, name="ffn1_dw2")))
    d_gla_w_out = _linear_bwd_w(og, dh3, name="gla_out_dw")
    dh2, dmix_g1, dz1, d_gate_w, d_gate_b, d_head_g = _gla_mid_bwd(
        z1, h2, mix_g[1:2], gla_w_in_t, dh3, gla["gla_w_out"], states, *gla_mid, after=sent, name="gla_mixer_bwd")
    d_gla_w_in_t = _linear_bwd_w(dz1, u1, row_blocks=gla["gla_w_in_t"].shape[:2], name="gla_in_dw")
    sent = src.emit("gla", dict(gla_w_in_t=d_gla_w_in_t, gla_w_out=d_gla_w_out))
    dh1, dhh0, dob0, dffn_g0 = _ffn_bwd_x(h1, dh2, ffn_g[0:1], rf0, ffn0["w1"], ffn0["w2"], after=sent, name="ffn0_bwd_x")
    sent = src.emit("ffn0_w1", dict(w1=_linear_bwd_w(uf0, dhh0, column_blocks=N_DEV, name="ffn0_dw1")))
    sent = src.emit("ffn0_w2", dict(w2=_linear_bwd_w(rf0, dob0, square_x=True, after=sent, name="ffn0_dw2")))
    d_cp_w_out = _linear_bwd_w(cat, dh1, after=sent, name="cp_out_dw")
    sent = src.emit("cp_out", dict(cp_w_out=d_cp_w_out))
    dx, dh0_first, dmix_g0, dz0, d_conv_w, d_conv_b, d_ln_g, d_ln_b, d_pool_w, d_pool_scale = _cp_mid_bwd(
        z0, cv0, h0, mix_g[0:1], cp["cp_w_in_t"], dh1, cp["cp_w_out"], *cp_mid, after=sent, name="cp_mixer_bwd")
    d_cp_w_in_t = _linear_bwd_w(dz0, u0, name="cp_in_dw")

    small = dict(
        mix_g=jnp.concatenate([dmix_g0, dmix_g1]), ffn_g=jnp.concatenate([dffn_g0, dffn_g1]), conv_b=d_conv_b, ln_g=d_ln_g,
        ln_b=d_ln_b, pool_w=d_pool_w, pool_scale=d_pool_scale, final_g=d_final_g, meta=dh0_first[PAD_ROWS:], conv_w=d_conv_w,
        gate_w=d_gate_w, gate_b=d_gate_b, head_g=d_head_g)
    src.emit("cp", dict(cp_w_in_t=d_cp_w_in_t, small=small, loss=loss))
    return loss, dx, small


_REPLICATED = ("mix_norm_g", "ffn_norm_g", "cp_conv_b", "cp_ln_g", "cp_ln_b", "cp_pool_w", "cp_pool_scale", "final_norm_g")
_SMALL_SHARDED = ("meta_tokens", "cp_conv_w", "gla_gate_w2", "gla_gate_b", "gla_head_g")
_NAMES = ("meta_tokens", "mix_norm_g", "ffn_norm_g", "ffn_w1", "ffn_w2", "cp_w_in", "cp_conv_w", "cp_conv_b", "cp_ln_g",
          "cp_ln_b", "cp_pool_w", "cp_pool_scale", "cp_w_out", "gla_w_in", "gla_gate_w2", "gla_gate_b", "gla_head_g",
          "gla_w_out", "final_norm_g")
_SMALL_GRADS = ("mix_g", "ffn_g", "conv_b", "ln_g", "ln_b", "pool_w", "pool_scale", "final_g", "meta", "conv_w", "gate_w",
                "gate_b", "head_g")
_GROUPS = ("cp", "ffn0", "gla", "ffn1")
_TWO_LEG_GATHERS = ("cp", "ffn0", "ffn1")


class _Exchanges:
    def __init__(self, w, d):
        self.d = d
        small = [w[n].reshape(w[n].shape[-2:]) for n in _SMALL_SHARDED]
        self.small_shard_shapes = [w[n].shape for n in _SMALL_SHARDED]
        shards = dict(
            cp=[(w["cp_w_in"][0].T, BF16), (w["cp_w_out"][0], BF16)] + [(a, F32) for a in small],
            ffn0=[((w["ffn_w1"], 0), BF16), ((w["ffn_w2"], 0), BF16)],
            gla=[(w["gla_w_in"][0].T, BF16), (w["gla_w_out"][0], BF16)],
            ffn1=[((w["ffn_w1"], 1), BF16), ((w["ffn_w2"], 1), BF16)])
        self.gathers = {}
        self.sent = {}
        token = None
        for group in _GROUPS:
            lands = _place_own(shards[group], after=token, name=f"place_w_{group}")
            plan = _plan_gather_first if group in _TWO_LEG_GATHERS else _plan_gather_direct
            self.gathers[group], token = _exchange_start(plan, None, lands, after=token, name=f"start_w_{group}")
        self.token = token

    def fetch(self, group, after):
        d = self.d
        after = (list(after) if isinstance(after, (list, tuple)) else [after]) + [self.token]
        _, got = _exchange_wait(self.gathers[group], after, name=f"wait_w_{group}")
        if group in _TWO_LEG_GATHERS:
            got = _exchange_now(_plan_gather_relay, got, name=f"relay_w_{group}")
        if group in ("ffn0", "ffn1"):
            return dict(w1=got[0], w2=got[1])
        if group == "gla":
            return dict(gla_w_in_t=got[0], gla_w_out=got[1].reshape(d, d), gate_w=self.gate_w, gate_b=self.gate_b,
                        head_g=self.head_g)
        meta, conv_w, gate_w, self.gate_b, self.head_g = [_undo_column_split(a) for a in got[2:]]
        self.gate_w = jnp.pad(gate_w, ((0, GATE_PAD - GATE_RANK), (0, 0))).astype(BF16)
        return dict(cp_w_in_t=got[0].reshape(-1, d), cp_w_out=got[1].reshape(d, d), meta=meta,
                    conv_w=jnp.pad(conv_w, ((0, 1), (0, 0))))

    def emit(self, group, g):
        d = self.d
        if group in ("ffn0_w1", "ffn1_w1"):
            arrs = [g["w1"]]
        elif group in ("ffn0_w2", "ffn1_w2"):
            arrs = [g["w2"].reshape(N_DEV, -1, d)]
        elif group == "gla":
            arrs = [g["gla_w_in_t"], g["gla_w_out"].reshape(N_DEV, d // N_DEV, d)]
        elif group == "cp_out":
            arrs = [g["cp_w_out"].reshape(N_DEV, d // N_DEV, d)]
        else:
            s = dict(g["small"])
            s.update(pool_w=s["pool_w"][None], conv_w=s["conv_w"][:CONV_WIDTH], gate_w=s["gate_w"][:GATE_RANK])
            own = [s[n] for n in _SMALL_GRADS[:len(_REPLICATED)]]
            own += [_column_split(s[n]).reshape((N_DEV,) + shape)
                    for n, shape in zip(_SMALL_GRADS[len(_REPLICATED):], self.small_shard_shapes)]
            plans = [_plan_to_all] * len(_REPLICATED) + [_plan_split_to_all] * len(_SMALL_SHARDED)
            lands = [lax.empty((N_DEV,) + a.shape, F32) for a in own[:len(_REPLICATED)]]
            lands += [lax.empty(a.shape, F32) for a in own[len(_REPLICATED):]]
            own.append(g["loss"])
            plans.append(_plan_to_all)
            lands.append(lax.empty((N_DEV,) + g["loss"].shape, F32))
            self.small_sent, self.token = _exchange_start(plans, own, lands, after=self.token, name="start_g_small")
            arrs = [g["cp_w_in_t"].reshape(N_DEV, -1, d)]
        self.sent[group], self.token = _exchange_start(_plan_split_to_all, arrs, [lax.empty(a.shape, a.dtype) for a in arrs],
                                                       after=self.token, name=f"start_g_{group}")
        return self.token

    def finish(self, w, mom, var):
        out = {}
        after = self.token

        def landed(group):
            own, got = _exchange_wait(self.sent[group], after, name=f"wait_g_{group}")
            return list(zip(own, got))

        def adam(n, parts, behind=None, transposed=False):
            by_row = transposed and w[n].shape[2] % 8 != 0
            if by_row:
                flip, back = (lambda a: jnp.transpose(a, (2, 0, 1))), (lambda a: jnp.transpose(a, (1, 2, 0)))
            else:
                flip = back = (lambda a: jnp.transpose(a, (0, 2, 1))) if transposed else (lambda a: a)
            res = _reduce_adam(parts, flip(w[n]), flip(mom[n]), flip(var[n]), by_row=by_row, after=behind, name=f"adam_{n}")
            out[n] = tuple(back(a) for a in res)
            return res[0]

        ffn1_w1 = landed("ffn1_w1")
        after = ffn1_w1[0][1]
        ffn1_w2 = landed("ffn1_w2")
        after = ffn1_w2[0][1]
        gla = landed("gla")
        after = adam("gla_w_in", [gla[0]], transposed=True)
        after = adam("gla_w_out", [gla[1]], after)
        ffn0_w1 = landed("ffn0_w1")
        after = adam("ffn_w1", [ffn0_w1[0], ffn1_w1[0]])
        ffn0_w2 = landed("ffn0_w2")
        after = adam("ffn_w2", [ffn0_w2[0], ffn1_w2[0]])
        small_own, small_landed = _exchange_wait(self.small_sent, after, name="wait_g_small")
        names = _REPLICATED + _SMALL_SHARDED
        split = [False] * len(_REPLICATED) + [True] * len(_SMALL_SHARDED)
        small_new = _adam_small(small_own[:-1], small_landed[:-1], split, [w[n] for n in names], [mom[n] for n in names],
                                [var[n] for n in names], name="adam_small")
        out.update(zip(names, small_new))
        out["loss"] = _sum8(small_own[-1], small_landed[-1], name="sum_loss")[0, 0]

        after = small_new[0][0]
        cp_out = landed("cp_out")
        after = adam("cp_w_out", [cp_out[0]])
        cp = landed("cp")
        adam("cp_w_in", [cp[0]], transposed=True)
        return out


def kernel(x, meta_tokens, mix_norm_g, ffn_norm_g, ffn_w1, ffn_w2, cp_w_in, cp_conv_w, cp_conv_b, cp_ln_g, cp_ln_b, cp_pool_w, cp_pool_scale, cp_w_out, gla_w_in, gla_gate_w2, gla_gate_b, gla_head_g, gla_w_out, final_norm_g, loss_target, m_meta_tokens, m_mix_norm_g, m_ffn_norm_g, m_ffn_w1, m_ffn_w2, m_cp_w_in, m_cp_conv_w, m_cp_conv_b, m_cp_ln_g, m_cp_ln_b, m_cp_pool_w, m_cp_pool_scale, m_cp_w_out, m_gla_w_in, m_gla_gate_w2, m_gla_gate_b, m_gla_head_g, m_gla_w_out, m_final_norm_g, v_meta_tokens, v_mix_norm_g, v_ffn_norm_g, v_ffn_w1, v_ffn_w2, v_cp_w_in, v_cp_conv_w, v_cp_conv_b, v_cp_ln_g, v_cp_ln_b, v_cp_pool_w, v_cp_pool_scale, v_cp_w_out, v_gla_w_in, v_gla_gate_w2, v_gla_gate_b, v_gla_head_g, v_gla_w_out, v_final_norm_g):
    w = dict(meta_tokens=meta_tokens, mix_norm_g=mix_norm_g, ffn_norm_g=ffn_norm_g, ffn_w1=ffn_w1, ffn_w2=ffn_w2,
             cp_w_in=cp_w_in, cp_conv_w=cp_conv_w, cp_conv_b=cp_conv_b, cp_ln_g=cp_ln_g, cp_ln_b=cp_ln_b,
             cp_pool_w=cp_pool_w, cp_pool_scale=cp_pool_scale, cp_w_out=cp_w_out, gla_w_in=gla_w_in,
             gla_gate_w2=gla_gate_w2, gla_gate_b=gla_gate_b, gla_head_g=gla_head_g, gla_w_out=gla_w_out,
             final_norm_g=final_norm_g.reshape(1, -1))
    mom = dict(meta_tokens=m_meta_tokens, mix_norm_g=m_mix_norm_g, ffn_norm_g=m_ffn_norm_g, ffn_w1=m_ffn_w1, ffn_w2=m_ffn_w2,
               cp_w_in=m_cp_w_in, cp_conv_w=m_cp_conv_w, cp_conv_b=m_cp_conv_b, cp_ln_g=m_cp_ln_g, cp_ln_b=m_cp_ln_b,
               cp_pool_w=m_cp_pool_w, cp_pool_scale=m_cp_pool_scale, cp_w_out=m_cp_w_out, gla_w_in=m_gla_w_in,
               gla_gate_w2=m_gla_gate_w2, gla_gate_b=m_gla_gate_b, gla_head_g=m_gla_head_g, gla_w_out=m_gla_w_out,
               final_norm_g=m_final_norm_g.reshape(1, -1))
    var = dict(meta_tokens=v_meta_tokens, mix_norm_g=v_mix_norm_g, ffn_norm_g=v_ffn_norm_g, ffn_w1=v_ffn_w1, ffn_w2=v_ffn_w2,
               cp_w_in=v_cp_w_in, cp_conv_w=v_cp_conv_w, cp_conv_b=v_cp_conv_b, cp_ln_g=v_cp_ln_g, cp_ln_b=v_cp_ln_b,
               cp_pool_w=v_cp_pool_w, cp_pool_scale=v_cp_pool_scale, cp_w_out=v_cp_w_out, gla_w_in=v_gla_w_in,
               gla_gate_w2=v_gla_gate_w2, gla_gate_b=v_gla_gate_b, gla_head_g=v_gla_head_g, gla_w_out=v_gla_w_out,
               final_norm_g=v_final_norm_g.reshape(1, -1))
    d = x.shape[-1]
    replicated = dict(mix_g=w["mix_norm_g"], ffn_g=w["ffn_norm_g"], conv_b=w["cp_conv_b"], ln_g=w["cp_ln_g"],
                      ln_b=w["cp_ln_b"], pool_w=w["cp_pool_w"][0].astype(BF16), pool_scale=w["cp_pool_scale"],
                      final_g=w["final_norm_g"])
    exchanges = _Exchanges(w, d)
    _, grad_x, _ = _local_step(x[0], loss_target[0], replicated, exchanges)
    out = exchanges.finish(w, mom, var)
    loss = out.pop("loss")

    def leaf(n, k):
        a = out[n][k]
        return a.reshape(-1) if n == "final_norm_g" else a

    return (loss, grad_x[None], *[leaf(n, 0) for n in _NAMES], *[leaf(n, 1) for n in _NAMES],
            *[leaf(n, 2) for n in _NAMES], *[leaf(n, 3) for n in _NAMES])
```

```python
import functools

import jax
import jax.numpy as jnp
from jax import lax
from jax.experimental import pallas as pl
from jax.experimental.pallas import tpu as pltpu

F32, BF16 = jnp.float32, jnp.bfloat16
N_DEV = 8
CHUNK = 64
N_META = 16
PAD_ROWS = CHUNK - N_META
HALO = 32
EPS = 1e-5
CONV_WIDTH = 31
POOL_WINDOWS = (2, 4, 8, 16)
HEADS = 4
GATE_RANK = 16
GATE_NORM = 16.0
GATE_PAD = 128
ADAM_LR, ADAM_B1, ADAM_B2, ADAM_EPS, ADAM_WD, ADAM_STEP = 0.001, 0.9, 0.999, 1e-08, 0.01, 10
V7X_VMEM_LIMIT = 56 * 2 ** 20
LANE = 128


def _cparams(*sem):
    return pltpu.CompilerParams(dimension_semantics=sem, vmem_limit_bytes=V7X_VMEM_LIMIT)


def _row_tile(t, cap):
    best = CHUNK
    for r in range(CHUNK, min(t, cap) + 1, CHUNK):
        if t % r == 0:
            best = r
    return best


def _resident(shape):
    return pl.BlockSpec(shape, lambda *_: (0,) * len(shape), pipeline_mode=pl.Buffered(1))


def _dot(a, b):
    return jnp.dot(a, b, preferred_element_type=F32)


def _dot_nt(a, b):
    return lax.dot_general(a, b, (((1,), (1,)), ((), ())), preferred_element_type=F32)


def _dot_tn(a, b):
    return lax.dot_general(a, b, (((0,), (0,)), ((), ())), preferred_element_type=F32)


def _rowsum(a):
    return jnp.sum(a, axis=0, keepdims=True)


def _sigmoid(a):
    return 1.0 / (1.0 + jnp.exp(-a))


def _row_ids(tile, rt):
    return tile * rt + lax.broadcasted_iota(jnp.int32, (rt, 1), 0)


def _sds(shape, dtype):
    return jax.ShapeDtypeStruct(shape, dtype)


DW_ROWS = 1024


def _linear_bwd_w(x, dy, *, square_x=False, column_blocks=None, row_blocks=None, after=None, name):
    t, k = x.shape
    n = dy.shape[1]
    cut_k = k > n and column_blocks is None
    assert cut_k or row_blocks is None
    width = k if cut_k else n
    blk = n // column_blocks if column_blocks else max(c for c in (640, 512, 384, 256, LANE) if width % c == 0)
    dep_specs, deps = _dep_specs(after)

    def body(*refs):
        x_ref, dy_ref, o_ref, acc = refs[len(deps):]
        for c0 in range(0, t, DW_ROWS):
            rows = slice(c0, min(c0 + DW_ROWS, t))
            xv = x_ref[rows, :]
            if square_x:
                xv = xv.astype(F32)
                xv = xv * xv
            part = _dot_tn(xv.astype(BF16), dy_ref[rows, :].astype(BF16))
            if c0 == 0:
                acc[...] = part
            else:
                acc[...] += part
        if row_blocks is None:
            o_ref[...] = acc[...].astype(BF16)
            return
        nb, rpb = row_blocks
        for s in range(width // blk):
            @pl.when(pl.program_id(0) == s)
            def _(s=s):
                for b in range(nb):
                    lo, hi = max(s * blk, b * rpb), min((s + 1) * blk, (b + 1) * rpb)
                    if lo < hi:
                        o_ref[b, lo - b * rpb:hi - b * rpb, :] = acc[lo - s * blk:hi - s * blk, :].astype(BF16)

    out_shape = _sds((k, n), BF16)
    semantics = "parallel"
    if cut_k:
        in_specs = [pl.BlockSpec((t, blk), lambda j: (0, j)), _resident((t, n))]
        out_specs = pl.BlockSpec((blk, n), lambda j: (j, 0))
        acc_shape = (blk, n)
        if row_blocks:
            out_shape = _sds(row_blocks + (n,), BF16)
            out_specs = pl.BlockSpec(out_shape.shape, lambda j: (0, 0, 0))
            semantics = "arbitrary"
    else:
        in_specs = [_resident((t, k)), pl.BlockSpec((t, blk), lambda j: (0, j))]
        out_specs = pl.BlockSpec((k, blk), lambda j: (0, j))
        acc_shape = (k, blk)
        if column_blocks:
            out_specs = pl.BlockSpec((None, k, blk), lambda j: (j, 0, 0))
            out_shape = _sds((column_blocks, k, blk), BF16)
    return pl.pallas_call(
        body, grid=(width // blk,), in_specs=dep_specs + in_specs, out_specs=out_specs, out_shape=out_shape,
        scratch_shapes=[pltpu.VMEM(acc_shape, F32)], compiler_params=_cparams(semantics), name=name)(*deps, x, dy)


FFN_BLOCKS_PER_STEP = 2


def _ffn_fwd(h, gain, w1g, w2g, *, loss_head=None, name):
    t, d = h.shape
    f8 = w1g.shape[-1]
    rt = _row_tile(t, 832)
    nb = FFN_BLOCKS_PER_STEP
    nstep = N_DEV // nb

    def body(*refs):
        if loss_head is None:
            h_ref, g_ref, w1_ref, w2_ref, o_ref, u_ref, r_ref, acc_ref = refs
        else:
            (h_ref, g_ref, w1_ref, w2_ref, fg_ref, tgt_ref, o_ref, u_ref, r_ref, loss_ref, dfg_ref, acc_ref, t_ref,
             t_sem) = refs
        i, j = pl.program_id(0), pl.program_id(1)

        def target_rows(act):
            @pl.when(i == 0)
            def _():
                act(pltpu.make_async_copy(tgt_ref.at[pl.ds(0, rt - CHUNK)], t_ref.at[pl.ds(CHUNK, rt - CHUNK)], t_sem.at[0]))

            if t > rt:
                @pl.when(i > 0)
                def _():
                    act(pltpu.make_async_copy(tgt_ref.at[pl.ds(pl.multiple_of(i * rt - CHUNK, CHUNK), rt)], t_ref,
                                              t_sem.at[0]))

        @pl.when(j == 0)
        def _():
            if loss_head is not None:
                @pl.when(i == 0)
                def _():
                    t_ref[0:CHUNK, :] = jnp.zeros((CHUNK, d), F32)

                target_rows(lambda copy: copy.start())

            hv = h_ref[...]
            u_ref[...] = (hv * lax.rsqrt(jnp.mean(hv * hv, axis=-1, keepdims=True) + EPS) * g_ref[...]).astype(BF16)
            acc_ref[...] = jnp.zeros_like(acc_ref)

        part = None
        for b in range(nb):
            a = jnp.maximum(_dot(u_ref[...], w1_ref[b]), 0.0)
            r_ref[:, b * f8:(b + 1) * f8] = a.astype(BF16)
            term = _dot((a * a).astype(BF16), w2_ref[b])
            part = term if part is None else part + term
        acc_ref[...] += part

        @pl.when(j == nstep - 1)
        def _():
            y = h_ref[...] + acc_ref[...]
            if loss_head is None:
                o_ref[...] = y
                return

            @pl.when(i == 0)
            def _():
                loss_ref[...] = jnp.zeros_like(loss_ref)
                dfg_ref[...] = jnp.zeros_like(dfg_ref)

            target_rows(lambda copy: copy.wait())

            rstd = lax.rsqrt(jnp.mean(y * y, axis=-1, keepdims=True) + EPS)
            xh = y * rstd
            err = jnp.where(_row_ids(i, rt) >= CHUNK, xh * fg_ref[...] - t_ref[...], 0.0)
            loss_ref[...] += (0.5 / d) * jnp.sum(err * err)
            dy = err * (1.0 / d)
            dfg_ref[...] += _rowsum(dy * xh)
            dxh = dy * fg_ref[...]
            o_ref[...] = rstd * (dxh - xh * jnp.mean(dxh * xh, axis=-1, keepdims=True))

    rows = lambda i, j: (i, 0)
    in_specs = [pl.BlockSpec((rt, d), rows), _resident((1, d)),
                pl.BlockSpec((nb, d, f8), lambda i, j: (j, 0, 0)), pl.BlockSpec((nb, f8, d), lambda i, j: (j, 0, 0))]
    out_specs = [pl.BlockSpec((rt, d), rows), pl.BlockSpec((rt, d), rows), pl.BlockSpec((rt, nb * f8), lambda i, j: (i, j))]
    out_shape = [_sds((t, d), F32), _sds((t, d), BF16), _sds((t, N_DEV * f8), BF16)]
    args = [h, gain, w1g, w2g]
    scratch_shapes = [pltpu.VMEM((rt, d), F32)]
    if loss_head is not None:
        in_specs += [_resident((1, d)), pl.BlockSpec(memory_space=pl.ANY)]
        out_specs += [pl.BlockSpec((8, LANE), lambda i, j: (0, 0)), pl.BlockSpec((1, d), lambda i, j: (0, 0))]
        out_shape += [_sds((8, LANE), F32), _sds((1, d), F32)]
        args += list(loss_head)
        scratch_shapes += [pltpu.VMEM((rt, d), F32), pltpu.SemaphoreType.DMA((1,))]
    return pl.pallas_call(
        body, grid=(t // rt, nstep), in_specs=in_specs, out_specs=out_specs, out_shape=out_shape,
        scratch_shapes=scratch_shapes,
        compiler_params=_cparams("arbitrary" if loss_head is not None else "parallel", "arbitrary"), name=name)(*args)


def _ffn_bwd_x(h, dout, gain, r, w1g, w2g, *, after=None, name):
    t, d = h.shape
    f8 = w1g.shape[-1]
    rt = _row_tile(t, 832)
    nb = FFN_BLOCKS_PER_STEP
    last = N_DEV // nb - 1
    dep_specs, deps = _dep_specs(after)

    def body(*refs):
        h_ref, do_ref, g_ref, r_ref, w1_ref, w2_ref, dh_ref, dhh_ref, dob_ref, dg_ref, du_ref = refs[len(deps):]
        i, j = pl.program_id(0), pl.program_id(1)

        @pl.when(j == 0)
        def _():
            dob_ref[...] = do_ref[...].astype(BF16)
            du_ref[...] = jnp.zeros_like(du_ref)

        part = None
        for b in range(nb):
            cols = slice(b * f8, (b + 1) * f8)
            dhh = (_dot_nt(dob_ref[...], w2_ref[b]) * (2.0 * r_ref[:, cols].astype(F32))).astype(BF16)
            dhh_ref[:, cols] = dhh
            term = _dot_nt(dhh, w1_ref[b])
            part = term if part is None else part + term
        du_ref[...] += part

        @pl.when(j == last)
        def _():
            @pl.when(i == 0)
            def _():
                dg_ref[...] = jnp.zeros_like(dg_ref)

            hv = h_ref[...]
            rstd = lax.rsqrt(jnp.mean(hv * hv, axis=-1, keepdims=True) + EPS)
            xh = hv * rstd
            du = du_ref[...]
            dg_ref[...] += _rowsum(du * xh)
            dxh = du * g_ref[...]
            dh_ref[...] = do_ref[...] + rstd * (dxh - xh * jnp.mean(dxh * xh, axis=-1, keepdims=True))

    rows = lambda i, j: (i, 0)
    return pl.pallas_call(
        body, grid=(t // rt, N_DEV // nb),
        in_specs=dep_specs + [
                  pl.BlockSpec((rt, d), rows), pl.BlockSpec((rt, d), rows), _resident((1, d)),
                  pl.BlockSpec((rt, nb * f8), lambda i, j: (i, j)),
                  pl.BlockSpec((nb, d, f8), lambda i, j: (j, 0, 0)),
                  pl.BlockSpec((nb, f8, d), lambda i, j: (j, 0, 0))],
        out_specs=[pl.BlockSpec((rt, d), rows), pl.BlockSpec((rt, nb * f8), lambda i, j: (i, j)),
                   pl.BlockSpec((rt, d), rows), pl.BlockSpec((1, d), lambda i, j: (0, 0))],
        out_shape=[_sds((t, d), F32), _sds((t, N_DEV * f8), BF16), _sds((t, d), BF16), _sds((1, d), F32)],
        scratch_shapes=[pltpu.VMEM((rt, d), F32)],
        compiler_params=_cparams("arbitrary", "arbitrary"), name=name)(*deps, h, dout, gain, r, w1g, w2g)


def _lane_blocks(width):
    lb = min(LANE, width)
    return [slice(s, s + lb) for s in range(0, width, lb)]


def _conv_rows(src_ref, w_ref, offset, dst_ref, nblk, width, bias_ref=None):
    def blk(rb, carry):
        base = pl.multiple_of(rb * CHUNK, CHUNK)
        for l, ls in enumerate(_lane_blocks(width)):
            acc = jnp.zeros((CHUNK, ls.stop - ls.start), F32)
            if bias_ref is not None:
                acc = acc + bias_ref[:, ls]
            for k in range(CONV_WIDTH):
                acc = acc + w_ref[k:k + 1, ls] * src_ref[l, pl.ds(base + offset(k), CHUNK), :]
            dst_ref[l, pl.ds(base, CHUNK), :] = acc
        return carry

    lax.fori_loop(0, nblk, blk, 0)


def _to_lane_blocks(ref, row0, value):
    for l, ls in enumerate(_lane_blocks(value.shape[1])):
        ref[l, row0:row0 + value.shape[0], :] = value[:, ls]


def _from_lane_blocks(ref):
    return jnp.concatenate([ref[l] for l in range(ref.shape[0])], axis=1)


def _pool_counts(rows, window):
    return jnp.clip(rows - PAD_ROWS + 1, 1, window).astype(F32)


def _trailing_sum(v, window):
    s, sh = v, 1
    while sh < window:
        s = s + pltpu.roll(s, sh, 0)
        sh *= 2
    return s


def _leading_sum(v, window):
    s, sh, n = v, 1, v.shape[0]
    while sh < window:
        s = s + pltpu.roll(s, n - sh, 0)
        sh *= 2
    return s


def _norm_project(h_ref, g_ref, w_t_ref, u_ref, z_ref):
    hv = h_ref[...]
    u = (hv * lax.rsqrt(jnp.mean(hv * hv, axis=-1, keepdims=True) + EPS) * g_ref[...]).astype(BF16)
    u_ref[...] = u
    z_ref[...] = _dot_nt(u, w_t_ref[...])


def _project_back(dz_ref, w_t_ref, h_ref, g_ref, dres_ref, dh_ref, dg_ref, first):
    dx = _dot(dz_ref[...], w_t_ref[...])
    hv = h_ref[...]
    rstd = lax.rsqrt(jnp.mean(hv * hv, axis=-1, keepdims=True) + EPS)
    xh = hv * rstd

    @pl.when(first)
    def _():
        dg_ref[...] = jnp.zeros_like(dg_ref)

    dg_ref[...] += _rowsum(dx * xh)
    dxh = dx * g_ref[...]
    dh_ref[...] = dres_ref[...] + rstd * (dxh - xh * jnp.mean(dxh * xh, axis=-1, keepdims=True))


def _cp_mid_fwd(x, meta, gain, w_in_t, w_out, conv_w, conv_b, ln_g, ln_b, pool_w, pool_scale, *, name):
    seq, d = x.shape
    t = seq + CHUNK
    ein = w_in_t.shape[0]
    cd = conv_b.shape[1]
    pd = pool_scale.shape[1]
    pg = pd // len(POOL_WINDOWS)
    rt = _row_tile(t, 320)
    ntile = t // rt

    def body(x_ref, meta_ref, g_ref, wi_ref, wo_ref, cw_ref, cb_ref, lg_ref, lb_ref, pw_ref, ps_ref,
             ho_ref, o_ref, z_ref, u_ref, cv_ref, h0_ref, gext, pext, conv_s, hbuf, hsem):
        i = pl.program_id(0)
        slot = i % 2
        first_rows = pltpu.make_async_copy(x_ref.at[pl.ds(0, rt - CHUNK)], hbuf.at[0, pl.ds(CHUNK, rt - CHUNK)], hsem.at[0])

        def tile_rows(tile, to):
            return pltpu.make_async_copy(x_ref.at[pl.ds(pl.multiple_of(tile * rt - CHUNK, CHUNK), rt)], hbuf.at[to],
                                         hsem.at[to])

        @pl.when(i == 0)
        def _():
            first_rows.start()
            hbuf[0, 0:PAD_ROWS, :] = jnp.zeros((PAD_ROWS, d), F32)
            hbuf[0, PAD_ROWS:CHUNK, :] = meta_ref[...]
            _to_lane_blocks(gext, 0, jnp.zeros((HALO, cd), F32))
            pext[0:HALO, :] = jnp.zeros((HALO, pd), F32)

        @pl.when(i + 1 < ntile)
        def _():
            tile_rows(i + 1, 1 - slot).start()

        @pl.when(i == 0)
        def _():
            first_rows.wait()

        @pl.when(i > 0)
        def _():
            tile_rows(i, slot).wait()

        h_ref = hbuf.at[slot]
        h0_ref[...] = h_ref[...]
        _norm_project(h_ref, g_ref, wi_ref, u_ref, z_ref)

        _to_lane_blocks(gext, HALO, z_ref[:, 0:cd] * _sigmoid(z_ref[:, cd:2 * cd]))
        pext[HALO:HALO + rt, :] = z_ref[:, 2 * cd:]
        _conv_rows(gext, cw_ref, lambda k: k + HALO - (CONV_WIDTH - 1), conv_s, rt // CHUNK, cd, cb_ref)
        cv = _from_lane_blocks(conv_s)
        cv_ref[...] = cv
        xc = cv - jnp.mean(cv, axis=-1, keepdims=True)
        y = xc * lax.rsqrt(jnp.mean(xc * xc, axis=-1, keepdims=True) + EPS) * lg_ref[...] + lb_ref[...]
        rows = _row_ids(i, rt)
        a = jnp.where(rows >= PAD_ROWS, y * _sigmoid(y), 0.0)
        o_ref[:, 0:cd] = a.astype(BF16)
        for gi, window in enumerate(POOL_WINDOWS):
            ls = slice(gi * pg, (gi + 1) * pg)
            v = pext[:, ls]
            tm = _trailing_sum(v, window)[HALO:] / _pool_counts(rows, window) - v[HALO:]
            p = _dot(tm.astype(BF16), pw_ref[gi]) * ps_ref[:, ls]
            o_ref[:, cd + gi * pg:cd + (gi + 1) * pg] = p.astype(BF16)
        ho_ref[...] = h_ref[...] + _dot(o_ref[...], wo_ref[...])
        gext[:, 0:HALO, :] = gext[:, rt:rt + HALO, :]
        pext[0:HALO, :] = pext[rt:rt + HALO, :]

    nl, lb = len(_lane_blocks(cd)), min(LANE, cd)
    rows = lambda i: (i, 0)
    return pl.pallas_call(
        body, grid=(ntile,),
        in_specs=[pl.BlockSpec(memory_space=pl.ANY), _resident(meta.shape), _resident((1, d)), _resident(w_in_t.shape),
                  _resident(w_out.shape), _resident(conv_w.shape), _resident((1, cd)),
                  _resident((1, cd)), _resident((1, cd)), _resident(pool_w.shape), _resident((1, pd))],
        out_specs=[pl.BlockSpec((rt, d), rows), pl.BlockSpec((rt, cd + pd), rows), pl.BlockSpec((rt, ein), rows),
                   pl.BlockSpec((rt, d), rows), pl.BlockSpec((rt, cd), rows), pl.BlockSpec((rt, d), rows)],
        out_shape=[_sds((t, d), F32), _sds((t, cd + pd), BF16), _sds((t, ein), F32), _sds((t, d), BF16),
                   _sds((t, cd), F32), _sds((t, d), F32)],
        scratch_shapes=[pltpu.VMEM((nl, rt + HALO, lb), F32), pltpu.VMEM((rt + HALO, pd), F32),
                        pltpu.VMEM((nl, rt, lb), F32), pltpu.VMEM((2, rt, d), F32), pltpu.SemaphoreType.DMA((2,))],
        compiler_params=_cparams("arbitrary"), name=name)(x, meta, gain, w_in_t, w_out, conv_w, conv_b, ln_g, ln_b, pool_w,
                                                          pool_scale)


def _cp_mid_bwd(z, cv, h, gain, w_in_t, dh, w_out, conv_w, conv_b, ln_g, ln_b, pool_w, pool_scale, *, after=None, name):
    t, ein = z.shape
    cd = conv_b.shape[1]
    pd = pool_scale.shape[1]
    pg = pd // len(POOL_WINDOWS)
    rt = _row_tile(t, 320)
    ntile = t // rt
    per = rt // CHUNK
    dep_specs, deps = _dep_specs(after)

    def body(*refs):
        (z_ref, zh_ref, cv_ref, h_ref, g_ref, wi_ref, dh_ref, wo_ref, cw_ref, cb_ref, lg_ref, lb_ref, pw_ref, ps_ref,
         dx_ref, dfirst_ref, dg_ref, dz_ref, dcw_ref, dcb_ref, dlg_ref, dlb_ref, dpw_ref, dps_ref,
         gext, pext, conv_s, dcv, dsp, dhi, dx_sem) = refs[len(deps):]
        step = pl.program_id(0)
        tile = ntile - 1 - step
        slot = step % 2
        last_slot = (ntile - 1) % 2
        first_rows = pltpu.make_async_copy(dhi.at[last_slot, pl.ds(CHUNK, rt - CHUNK)], dx_ref.at[pl.ds(0, rt - CHUNK)],
                                           dx_sem.at[last_slot])

        def tile_rows(tile, slot):
            return pltpu.make_async_copy(dhi.at[slot], dx_ref.at[pl.ds(pl.multiple_of(tile * rt - CHUNK, CHUNK), rt)],
                                         dx_sem.at[slot])

        dcat = _dot_nt(dh_ref[...].astype(BF16), wo_ref[...])

        @pl.when(step >= 2)
        def _():
            tile_rows(tile + 2, slot).wait()

        @pl.when(step == 0)
        def _():
            for ref in (dcw_ref, dcb_ref, dlg_ref, dlb_ref, dpw_ref, dps_ref):
                ref[...] = jnp.zeros_like(ref)
            _to_lane_blocks(dcv, rt, jnp.zeros((HALO, cd), F32))
            dsp[rt:rt + HALO, :] = jnp.zeros((HALO, pd), F32)

        keep = jnp.where(tile > 0, 1.0, 0.0)
        zh = zh_ref[CHUNK - HALO:CHUNK, :]
        _to_lane_blocks(gext, 0, keep * zh[:, 0:cd] * _sigmoid(zh[:, cd:2 * cd]))
        pext[0:HALO, :] = keep * zh[:, 2 * cd:]
        za = z_ref[:, 0:cd]
        sg = _sigmoid(z_ref[:, cd:2 * cd])
        _to_lane_blocks(gext, HALO, za * sg)
        pext[HALO:HALO + rt, :] = z_ref[:, 2 * cd:]
        cv = cv_ref[...]
        xc = cv - jnp.mean(cv, axis=-1, keepdims=True)
        rstd = lax.rsqrt(jnp.mean(xc * xc, axis=-1, keepdims=True) + EPS)
        xh = xc * rstd
        y = xh * lg_ref[...] + lb_ref[...]
        sy = _sigmoid(y)
        rows = _row_ids(tile, rt)
        da = jnp.where(rows >= PAD_ROWS, dcat[:, 0:cd], 0.0)
        dy = da * (sy * (1.0 + y * (1.0 - sy)))
        dlg_ref[...] += _rowsum(dy * xh)
        dlb_ref[...] += _rowsum(dy)
        dxh = dy * lg_ref[...]
        dconv = rstd * (dxh - jnp.mean(dxh, axis=-1, keepdims=True) - xh * jnp.mean(dxh * xh, axis=-1, keepdims=True))
        dcb_ref[...] += _rowsum(dconv)
        _to_lane_blocks(dcv, 0, dconv)
        for l, ls in enumerate(_lane_blocks(cd)):
            def acc_rows(rb, accs, l=l):
                base = pl.multiple_of(rb * CHUNK, CHUNK)
                d_blk = dcv[l, pl.ds(base, CHUNK), :]
                out = []
                for k in range(CONV_WIDTH):
                    prod = d_blk * gext[l, pl.ds(base + k + HALO - (CONV_WIDTH - 1), CHUNK), :]
                    part = prod[0:8]
                    for s in range(8, CHUNK, 8):
                        part = part + prod[s:s + 8]
                    out.append(accs[k] + part)
                return tuple(out)

            zero = jnp.zeros((8, ls.stop - ls.start), F32)
            accs = lax.fori_loop(0, per, acc_rows, (zero,) * CONV_WIDTH)
            for k in range(CONV_WIDTH):
                dcw_ref[k:k + 1, ls] += _rowsum(accs[k])
        _conv_rows(dcv, cw_ref, lambda k: CONV_WIDTH - 1 - k, conv_s, per, cd)
        dglu = _from_lane_blocks(conv_s)
        dz_ref[:, 0:cd] = (dglu * sg).astype(BF16)
        dz_ref[:, cd:2 * cd] = (dglu * za * sg * (1.0 - sg)).astype(BF16)
        dcv[:, rt:rt + HALO, :] = dcv[:, 0:HALO, :]
        for gi, window in enumerate(POOL_WINDOWS):
            ls = slice(gi * pg, (gi + 1) * pg)
            v = pext[:, ls]
            cnt = _pool_counts(rows, window)
            tm = (_trailing_sum(v, window)[HALO:] / cnt - v[HALO:]).astype(BF16)
            dp = dcat[:, cd + gi * pg:cd + (gi + 1) * pg]
            dps_ref[:, ls] += _rowsum(dp * _dot(tm, pw_ref[gi]))
            dpl = (dp * ps_ref[:, ls]).astype(BF16)
            dpw_ref[gi] += _dot_tn(tm, dpl)
            dtm = _dot_nt(dpl, pw_ref[gi])
            dsp[0:rt, ls] = dtm / cnt
            dpin = _leading_sum(dsp[:, ls], window)[0:rt] - dtm
            dz_ref[:, 2 * cd + gi * pg:2 * cd + (gi + 1) * pg] = dpin.astype(BF16)
        dsp[rt:rt + HALO, :] = dsp[0:HALO, :]

        _project_back(dz_ref, wi_ref, h_ref, g_ref, dh_ref, dhi.at[slot], dg_ref, step == 0)

        @pl.when(tile > 0)
        def _():
            tile_rows(tile, slot).start()

        @pl.when(tile == 0)
        def _():
            first_rows.start()
            dfirst_ref[...] = dhi[last_slot, 0:CHUNK, :]
            if ntile > 1:
                tile_rows(1, 1 - last_slot).wait()
            first_rows.wait()

    d = h.shape[1]
    back = lambda i: (ntile - 1 - i, 0)
    halo_idx = lambda i: (jnp.maximum((ntile - 1 - i) * per - 1, 0), 0)
    const2 = lambda i: (0, 0)
    nl, lb = len(_lane_blocks(cd)), min(LANE, cd)
    return pl.pallas_call(
        body, grid=(ntile,),
        in_specs=dep_specs + [
                  pl.BlockSpec((rt, ein), back), pl.BlockSpec((CHUNK, ein), halo_idx), pl.BlockSpec((rt, cd), back),
                  pl.BlockSpec((rt, d), back),
                  _resident((1, d)), _resident(w_in_t.shape), pl.BlockSpec((rt, d), back), _resident(w_out.shape),
                  _resident(conv_w.shape), _resident((1, cd)), _resident((1, cd)), _resident((1, cd)),
                  _resident(pool_w.shape), _resident((1, pd))],
        out_specs=[pl.BlockSpec(memory_space=pl.ANY), pl.BlockSpec((CHUNK, d), const2), pl.BlockSpec((1, d), const2),
                   pl.BlockSpec((rt, ein), back), pl.BlockSpec(conv_w.shape, const2), pl.BlockSpec((1, cd), const2),
                   pl.BlockSpec((1, cd), const2), pl.BlockSpec((1, cd), const2),
                   pl.BlockSpec(pool_w.shape, lambda i: (0, 0, 0)), pl.BlockSpec((1, pd), const2)],
        out_shape=[_sds((t - CHUNK, d), F32), _sds((CHUNK, d), F32), _sds((1, d), F32),
                   _sds((t, ein), BF16), _sds(conv_w.shape, F32), _sds((1, cd), F32), _sds((1, cd), F32),
                   _sds((1, cd), F32), _sds(pool_w.shape, F32), _sds((1, pd), F32)],
        scratch_shapes=[pltpu.VMEM((nl, rt + HALO, lb), F32), pltpu.VMEM((rt + HALO, pd), F32), pltpu.VMEM((nl, rt, lb), F32),
                        pltpu.VMEM((nl, rt + HALO, lb), F32), pltpu.VMEM((rt + HALO, pd), F32), pltpu.VMEM((2, rt, d), F32),
                        pltpu.SemaphoreType.DMA((2,))],
        compiler_params=_cparams("arbitrary"), name=name)(*deps, z, z, cv, h, gain, w_in_t, dh, w_out, conv_w, conv_b, ln_g,
                                                          ln_b, pool_w, pool_scale)


def _log_decay(r, gw_ref, gb_ref, rows):
    gp = _dot(r.astype(BF16), gw_ref[...]) + gb_ref[...]
    log_sig = jnp.minimum(gp, 0.0) - jnp.log(1.0 + jnp.exp(-jnp.abs(gp)))
    return gp, jnp.where(rows >= PAD_ROWS, log_sig / GATE_NORM, 0.0)


def _tri(strict):
    r = lax.broadcasted_iota(jnp.int32, (CHUNK, CHUNK), 0)
    c = lax.broadcasted_iota(jnp.int32, (CHUNK, CHUNK), 1)
    return jnp.where(c < r if strict else c <= r, 1.0, 0.0).astype(BF16)


def _tri_dot(tri, a):
    hi = a.astype(BF16)
    rest = a - hi.astype(F32)
    mid = rest.astype(BF16)
    lo = (rest - mid.astype(F32)).astype(BF16)
    return _dot(tri, hi) + _dot(tri, mid) + _dot(tri, lo)


def _gla_mid_fwd(h, gain, w_in_blocks, w_out, gate_w, gate_b, head_g, *, name):
    t = h.shape[0]
    nblk, rpb = w_in_blocks.shape[:2]
    dk = gate_b.shape[1]
    hv = head_g.shape[1]
    hk = dk // HEADS
    dv = hv * HEADS
    r_at = 2 * dk + 2 * dv
    assert nblk * rpb == r_at + GATE_RANK
    zw = r_at + GATE_PAD
    rt = _row_tile(t, 320)
    per = rt // CHUNK
    scale = hk ** -0.5

    def body(h_ref, g_ref, wb_ref, wo_ref, gw_ref, gb_ref, hg_ref, ho_ref, o_ref, st_ref, z_ref, u_ref, wi_ref,
             s_ref, la_ref, dec_ref):
        i = pl.program_id(0)

        @pl.when(i == 0)
        def _():
            s_ref[...] = jnp.zeros_like(s_ref)
            for b in range(nblk):
                wi_ref[b * rpb:(b + 1) * rpb, :] = wb_ref[b]
            wi_ref[nblk * rpb:, :] = jnp.zeros((zw - nblk * rpb, wi_ref.shape[1]), BF16)

        _norm_project(h_ref, g_ref, wi_ref, u_ref, z_ref)

        _, la = _log_decay(z_ref[:, r_at:r_at + GATE_PAD], gw_ref, gb_ref, _row_ids(i, rt))
        la_ref[...] = la
        tri = _tri(False)

        def chunk_rows(c):
            return slice(c * CHUNK, (c + 1) * CHUNK)

        def decays(c, carry):
            rows = chunk_rows(c)
            la_c = la_ref[rows, :]
            cum = _tri_dot(tri, la_c)
            dec_ref[rows, :] = jnp.exp(_rowsum(la_c) - cum)
            return carry

        def states(c, carry):
            rows = chunk_rows(c)
            etot = jnp.exp(_rowsum(la_ref[rows, :]))
            for hd in range(HEADS):
                ks = slice(hd * hk, (hd + 1) * hk)
                kd = z_ref[rows, dk + hd * hk:dk + (hd + 1) * hk] * dec_ref[rows, ks]
                v = z_ref[rows, 2 * dk + hd * hv:2 * dk + (hd + 1) * hv]
                s_new = s_ref[hd] * etot[:, ks] + _dot_tn(v.astype(BF16), kd.astype(BF16))
                s_ref[hd] = s_new
                st_ref[c, hd] = s_new
            return carry

        def outputs(c, carry):
            rows = chunk_rows(c)
            for hd in range(HEADS):
                q = z_ref[rows, hd * hk:(hd + 1) * hk] * scale
                g = z_ref[rows, 2 * dk + dv + hd * hv:2 * dk + dv + (hd + 1) * hv]
                o = _dot_nt(q.astype(BF16), st_ref[c, hd].astype(BF16))
                on = o * lax.rsqrt(jnp.mean(o * o, axis=-1, keepdims=True) + EPS) * hg_ref[...]
                o_ref[rows, hd * hv:(hd + 1) * hv] = (on * (g * _sigmoid(g))).astype(BF16)
            return carry

        for phase in (decays, states, outputs):
            for c in range(per):
                phase(c, 0)
        ho_ref[...] = h_ref[...] + _dot(o_ref[...], wo_ref[...])

    d = h.shape[1]
    rows = lambda i: (i, 0)
    return pl.pallas_call(
        body, grid=(t // rt,),
        in_specs=[pl.BlockSpec((rt, d), rows), _resident((1, d)), _resident(w_in_blocks.shape), _resident(w_out.shape),
                  _resident(gate_w.shape), _resident((1, dk)), _resident((1, hv))],
        out_specs=[pl.BlockSpec((rt, d), rows), pl.BlockSpec((rt, dv), rows),
                   pl.BlockSpec((per, HEADS, hv, hk), lambda i: (i, 0, 0, 0)), pl.BlockSpec((rt, zw), rows),
                   pl.BlockSpec((rt, d), rows), pl.BlockSpec((zw, d), lambda i: (0, 0))],
        out_shape=[_sds((t, d), F32), _sds((t, dv), BF16), _sds((t // CHUNK, HEADS, hv, hk), F32), _sds((t, zw), F32),
                   _sds((t, d), BF16), _sds((zw, d), BF16)],
        scratch_shapes=[pltpu.VMEM((HEADS, hv, hk), F32), pltpu.VMEM((rt, dk), F32), pltpu.VMEM((rt, dk), F32)],
        compiler_params=_cparams("arbitrary"), name=name)(h, gain, w_in_blocks, w_out, gate_w, gate_b, head_g)


def _gla_mid_bwd(z, h, gain, w_in_t, dh, w_out, states, gate_w, gate_b, head_g, *, after=None, name):
    t = z.shape[0]
    dk = gate_b.shape[1]
    hv = head_g.shape[1]
    hk = dk // HEADS
    dv = hv * HEADS
    r_at = 2 * dk + 2 * dv
    rt = _row_tile(t, 320)
    ntile = t // rt
    per = rt // CHUNK
    scale = hk ** -0.5
    dep_specs, deps = _dep_specs(after)

    def body(*refs):
        (z_ref, h_ref, g_ref, wi_ref, dh_ref, wo_ref, st_ref, stp_ref, gw_ref, gb_ref, hg_ref,
         dhi_ref, dg_ref, dz_ref, dgw_ref, dgb_ref, dhg_ref,
         ds_ref, la_ref, dla_ref, dec_ref, dos_ref, e_ref, do_ref) = refs[len(deps):]
        step = pl.program_id(0)
        tile = ntile - 1 - step
        do_ref[...] = _dot_nt(dh_ref[...].astype(BF16), wo_ref[...])

        @pl.when(step == 0)
        def _():
            ds_ref[...] = jnp.zeros_like(ds_ref)
            dgw_ref[...] = jnp.zeros_like(dgw_ref)
            dgb_ref[...] = jnp.zeros_like(dgb_ref)
            dhg_ref[...] = jnp.zeros_like(dhg_ref)

        rows_id = _row_ids(tile, rt)
        r = z_ref[:, r_at:r_at + GATE_PAD]
        gp, la = _log_decay(r, gw_ref, gb_ref, rows_id)
        la_ref[...] = la
        tri, tri_strict = _tri(False), _tri(True)
        keep = jnp.where(tile > 0, 1.0, 0.0)

        def chunk_rows(c):
            return slice(c * CHUNK, (c + 1) * CHUNK)

        def recompute(c, dhg):
            rows = chunk_rows(c)
            la_c = la_ref[rows, :]
            cum = _tri_dot(tri, la_c)
            dec_ref[rows, :] = jnp.exp(_rowsum(la_c) - cum)
            for hd in range(HEADS):
                q = (z_ref[rows, hd * hk:(hd + 1) * hk] * scale).astype(BF16)
                g = z_ref[rows, 2 * dk + dv + hd * hv:2 * dk + dv + (hd + 1) * hv]
                s_b = st_ref[c, hd].astype(BF16)
                o = _dot_nt(q, s_b)
                rstd = lax.rsqrt(jnp.mean(o * o, axis=-1, keepdims=True) + EPS)
                oh = o * rstd
                sg = _sigmoid(g)
                d_og = do_ref[rows, hd * hv:(hd + 1) * hv]
                dz_ref[rows, 2 * dk + dv + hd * hv:2 * dk + dv + (hd + 1) * hv] = (
                    d_og * oh * hg_ref[...] * (sg * (1.0 + g * (1.0 - sg)))).astype(BF16)
                don = d_og * (g * sg)
                dhg = dhg + _rowsum(don * oh)
                doh = don * hg_ref[...]
                d_o = (rstd * (doh - oh * jnp.mean(doh * oh, axis=-1, keepdims=True))).astype(BF16)
                dos_ref[rows, hd * hv:(hd + 1) * hv] = d_o
                dz_ref[rows, hd * hk:(hd + 1) * hk] = (_dot(d_o, s_b) * scale).astype(BF16)
            return dhg

        def recurrence(cc, carry):
            c = per - 1 - cc
            rows = chunk_rows(c)
            etot = jnp.exp(_rowsum(la_ref[rows, :]))
            for hd in range(HEADS):
                ks = slice(hd * hk, (hd + 1) * hk)
                q = (z_ref[rows, hd * hk:(hd + 1) * hk] * scale).astype(BF16)
                dec = dec_ref[rows, ks]
                kd = z_ref[rows, dk + hd * hk:dk + (hd + 1) * hk] * dec
                v = z_ref[rows, 2 * dk + hd * hv:2 * dk + (hd + 1) * hv].astype(BF16)
                s_prev = st_ref[c - 1, hd] if c > 0 else keep * stp_ref[0, hd]
                ds_t = ds_ref[hd] + _dot_tn(dos_ref[rows, hd * hv:(hd + 1) * hv], q)
                ds_b = ds_t.astype(BF16)
                dkd = _dot(v, ds_b)
                dz_ref[rows, 2 * dk + hd * hv:2 * dk + (hd + 1) * hv] = _dot_nt(kd.astype(BF16), ds_b).astype(BF16)
                dtot = etot[:, ks] * _rowsum(ds_t * s_prev)
                ds_ref[hd] = ds_t * etot[:, ks]
                dz_ref[rows, dk + hd * hk:dk + (hd + 1) * hk] = (dkd * dec).astype(BF16)
                e_ref[rows, ks] = dkd * kd
                dla_ref[rows, ks] = jnp.broadcast_to(dtot, (CHUNK, hk))
            return carry

        def decay_cotangent(c, carry):
            rows = chunk_rows(c)
            dla_ref[rows, :] += _tri_dot(tri_strict, e_ref[rows, :])
            return carry

        dhg = jnp.zeros((1, hv), F32)
        for c in range(per):
            dhg = recompute(c, dhg)
        dhg_ref[...] += dhg
        for phase in (recurrence, decay_cotangent):
            for c in range(per):
                phase(c, 0)
        dla = jnp.where(rows_id >= PAD_ROWS, dla_ref[...], 0.0)
        dgp = dla * (1.0 / GATE_NORM) * (1.0 - _sigmoid(gp))
        dgb_ref[...] += _rowsum(dgp)
        dgp_b = dgp.astype(BF16)
        dgw_ref[...] += _dot_tn(r.astype(BF16), dgp_b)
        dz_ref[:, r_at:r_at + GATE_PAD] = _dot_nt(dgp_b, gw_ref[...]).astype(BF16)
        _project_back(dz_ref, wi_ref, h_ref, g_ref, dh_ref, dhi_ref, dg_ref, step == 0)

    d = h.shape[1]
    back = lambda i: (ntile - 1 - i, 0)
    const2 = lambda i: (0, 0)
    return pl.pallas_call(
        body, grid=(ntile,),
        in_specs=dep_specs + [
                  pl.BlockSpec((rt, z.shape[1]), back), pl.BlockSpec((rt, d), back), _resident((1, d)),
                  _resident(w_in_t.shape), pl.BlockSpec((rt, d), back), _resident(w_out.shape),
                  pl.BlockSpec((per, HEADS, hv, hk), lambda i: (ntile - 1 - i, 0, 0, 0)),
                  pl.BlockSpec((1, HEADS, hv, hk), lambda i: (jnp.maximum((ntile - 1 - i) * per - 1, 0), 0, 0, 0)),
                  _resident(gate_w.shape), _resident((1, dk)), _resident((1, hv))],
        out_specs=[pl.BlockSpec((rt, d), back), pl.BlockSpec((1, d), const2), pl.BlockSpec((rt, z.shape[1]), back),
                   pl.BlockSpec(gate_w.shape, const2), pl.BlockSpec((1, dk), const2), pl.BlockSpec((1, hv), const2)],
        out_shape=[_sds((t, d), F32), _sds((1, d), F32), _sds(z.shape, BF16), _sds(gate_w.shape, F32),
                   _sds((1, dk), F32), _sds((1, hv), F32)],
        scratch_shapes=[pltpu.VMEM((HEADS, hv, hk), F32), pltpu.VMEM((rt, dk), F32), pltpu.VMEM((rt, dk), F32),
                        pltpu.VMEM((rt, dk), F32), pltpu.VMEM((rt, dv), BF16), pltpu.VMEM((rt, dk), F32),
                        pltpu.VMEM((rt, dv), F32)],
        compiler_params=_cparams("arbitrary"), name=name)(*deps, z, h, gain, w_in_t, dh, w_out, states, states, gate_w,
                                                          gate_b, head_g)


def _adamw_math(w, g, m, v):
    m = ADAM_B1 * m + (1.0 - ADAM_B1) * g
    v = ADAM_B2 * v + (1.0 - ADAM_B2) * (g * g)
    m_hat = m / (1.0 - ADAM_B1 ** ADAM_STEP)
    v_hat = v / (1.0 - ADAM_B2 ** ADAM_STEP)
    return -ADAM_LR * (m_hat / (jnp.sqrt(v_hat) + ADAM_EPS) + ADAM_WD * w), m, v


N_CHIP = N_DEV // 2
BLOCK_ELEMS = 128 * 1024


def _my_slot():
    return 4 * lax.axis_index("x") + 2 * lax.axis_index("y") + lax.axis_index("c")


def _row_block(r, c):
    cap = max(8, BLOCK_ELEMS // (-(-c // LANE) * LANE))
    return max([b for b in range(8, r + 1, 8) if r % b == 0 and b <= cap] or [r])


def _blocks(r, c):
    rb = _row_block(r, c)
    if rb < r or r * c <= BLOCK_ELEMS:
        return rb, c
    return r, max([b for b in (512, 256, LANE) if c % b == 0 and r * b <= BLOCK_ELEMS] or [c])


def _reduce_adam(parts, w, m, v, *, by_row=False, after=None, name):
    nl, r, c = (1, w.shape[0], w.shape[2]) if by_row else w.shape
    rb, cb = _blocks(r, c)
    dep_specs, deps = _dep_specs(after)
    whole = (slice(None), 0, slice(None)) if by_row else Ellipsis

    def body(*refs):
        me = refs[0][0]
        refs = refs[1 + len(deps):]
        p_refs = refs[:2 * nl]
        w_ref, m_ref, v_ref, g_out, d_out, m_out, v_out = refs[2 * nl:]
        layer = pl.program_id(0)
        for li in range(nl):
            @pl.when(layer == li)
            def _(li=li):
                own_ref, land_ref = p_refs[2 * li], p_refs[2 * li + 1]
                mine = own_ref[...].astype(F32)
                g = None
                for dev in range(N_DEV):
                    term = jnp.where(me == dev, mine, land_ref[dev].astype(F32))
                    g = term if g is None else g + term
                g_out[whole] = g
                d_out[whole], m_out[whole], v_out[whole] = _adamw_math(w_ref[whole], g, m_ref[whole], v_ref[whole])

    if by_row:
        blk = pl.BlockSpec((rb, 1, cb), lambda l, i, j, me: (i, 0, j))
    else:
        blk = pl.BlockSpec((None, rb, cb), lambda l, i, j, me: (l, i, j))
    p_specs = []
    for li in range(nl):
        p_specs += [
            pl.BlockSpec((None, rb, cb), lambda l, i, j, me, li=li: (me[0], jnp.where(l == li, i, 0), jnp.where(l == li, j, 0))),
            pl.BlockSpec((N_DEV, rb, cb), lambda l, i, j, me, li=li: (0, jnp.where(l == li, i, 0), jnp.where(l == li, j, 0)))]
    flat = [p for pair in parts for p in pair]
    grid_spec = pltpu.PrefetchScalarGridSpec(
        num_scalar_prefetch=1, grid=(nl, r // rb, c // cb), in_specs=dep_specs + p_specs + [blk, blk, blk],
        out_specs=[blk] * 4)
    return pl.pallas_call(
        body, grid_spec=grid_spec, out_shape=[_sds(w.shape, F32)] * 4,
        compiler_params=_cparams("arbitrary", "arbitrary", "arbitrary"), name=name)(
        _my_slot().reshape(1), *deps, *flat, w, m, v)


def _sum8(own, landed, *, name):
    def body(own_ref, land_ref, o_ref):
        me = _my_slot()
        total = None
        for dev in range(N_DEV):
            term = jnp.where(me == dev, own_ref[...], land_ref[dev])
            total = term if total is None else total + term
        o_ref[...] = total

    return pl.pallas_call(body, out_shape=_sds(own.shape, F32), name=name)(own, landed)


def _adam_small(own, landed, split, w, m, v, *, name):
    n = len(w)

    def body(*refs):
        own_refs, land_refs, w_refs, m_refs, v_refs = (refs[k * n:(k + 1) * n] for k in range(5))
        outs = refs[5 * n:]
        me = _my_slot()
        for k in range(n):
            mine = own_refs[k][me] if split[k] else own_refs[k][...]
            g = None
            for dev in range(N_DEV):
                term = jnp.where(me == dev, mine, land_refs[k][dev])
                g = term if g is None else g + term
            outs[4 * k][...] = g
            outs[4 * k + 1][...], outs[4 * k + 2][...], outs[4 * k + 3][...] = _adamw_math(
                w_refs[k][...], g, m_refs[k][...], v_refs[k][...])

    out = pl.pallas_call(body, out_shape=[_sds(a.shape, F32) for a in w for _ in range(4)],
                         compiler_params=pltpu.CompilerParams(vmem_limit_bytes=V7X_VMEM_LIMIT), name=name)(
        *own, *landed, *w, *m, *v)
    return [tuple(out[4 * k:4 * k + 4]) for k in range(n)]


_HBM = pl.BlockSpec(memory_space=pltpu.HBM)
_SEM = pl.BlockSpec(memory_space=pltpu.SEMAPHORE)
_DATAFLOW = pltpu.SideEffectType.DATAFLOW_SIDE_EFFECTING


def _plan_to_all(src, land):
    x, y, c = lax.axis_index("x"), lax.axis_index("y"), lax.axis_index("c")
    return [(src, land.at[_my_slot()], (x ^ ((d >> 2) & 1), y ^ ((d >> 1) & 1), c ^ (d & 1))) for d in range(1, N_DEV)]


def _plan_split_to_all(src, land):
    x, y, c = lax.axis_index("x"), lax.axis_index("y"), lax.axis_index("c")
    peers = [(x ^ ((d >> 2) & 1), y ^ ((d >> 1) & 1), c ^ (d & 1)) for d in range(1, N_DEV)]
    return [(src.at[4 * px + 2 * py + pc], land.at[_my_slot()], (px, py, pc)) for px, py, pc in peers]


_PLAN_COPIES = {_plan_to_all: N_DEV - 1, _plan_split_to_all: N_DEV - 1}


def _plans(plan, n):
    return list(plan) if isinstance(plan, (list, tuple)) else [plan] * n


def _exchange_copies(plan, ins, lands, send, recv):
    copies, sem = [], 0
    for p, src, land in zip(_plans(plan, len(lands)), ins, lands):
        for s, dst, dev in p(src, land):
            copies.append(pltpu.make_async_remote_copy(
                src_ref=s, dst_ref=dst, send_sem=send.at[sem], recv_sem=recv.at[sem],
                device_id=dev, device_id_type=pl.DeviceIdType.MESH))
            sem += 1
    return copies


def _place_own(srcs, *, after=None, name):
    dep_specs, deps = _dep_specs(after)
    n = len(srcs)
    arrays, in_specs, shapes = [], [], []
    for a, dtype in srcs:
        if isinstance(a, tuple):
            a, layer = a
            in_specs.append(pl.BlockSpec((None,) + a.shape[1:], lambda i, layer=layer: (layer, 0, 0)))
            shapes.append(a.shape[1:])
        else:
            in_specs.append(pl.BlockSpec(a.shape, lambda i: (0, 0)))
            shapes.append(a.shape)
        arrays.append(a)
    dtypes = [dtype for _, dtype in srcs]

    def body(*refs):
        refs = refs[len(deps):]
        a_refs, o_refs, cast_refs, sem = refs[:n], refs[n:2 * n], refs[2 * n:3 * n], refs[3 * n]
        me = _my_slot()
        copies = []
        for k in range(n):
            cast_refs[k][...] = a_refs[k][...].astype(dtypes[k])
            copies.append(pltpu.make_async_copy(cast_refs[k], o_refs[k].at[me], sem.at[k]))
            copies[-1].start()
        for cp in copies:
            cp.wait()

    return pl.pallas_call(
        body, grid=(1,), in_specs=dep_specs + in_specs, out_specs=[pl.BlockSpec(memory_space=pl.ANY)] * n,
        out_shape=[_sds((N_DEV,) + shape, dtype) for shape, dtype in zip(shapes, dtypes)],
        scratch_shapes=[pltpu.VMEM(shape, dtype) for shape, dtype in zip(shapes, dtypes)] + [pltpu.SemaphoreType.DMA((n,))],
        compiler_params=pltpu.CompilerParams(vmem_limit_bytes=V7X_VMEM_LIMIT), name=name)(*deps, *arrays)


def _plan_gather_first(land, _):
    x, y, c = lax.axis_index("x"), lax.axis_index("y"), lax.axis_index("c")
    mine = land.at[_my_slot()]
    return [(mine, mine, (x, y, 1 - c))] + [(mine, mine, (x ^ (d >> 1), y ^ (d & 1), c)) for d in range(1, N_CHIP)]


def _plan_gather_relay(land, _):
    x, y, c = lax.axis_index("x"), lax.axis_index("y"), lax.axis_index("c")
    slots = [land.at[4 * (x ^ (d >> 1)) + 2 * (y ^ (d & 1)) + c] for d in range(1, N_CHIP)]
    return [(s, s, (x, y, 1 - c)) for s in slots]


def _plan_gather_direct(land, _):
    return _plan_to_all(land.at[_my_slot()], land)


_PLAN_COPIES[_plan_gather_first] = N_CHIP
_PLAN_COPIES[_plan_gather_relay] = N_CHIP - 1
_PLAN_COPIES[_plan_gather_direct] = N_DEV - 1


def _exchange_start(plan, arrs, lands, *, after=None, name):
    bufs = list(lands) if arrs is None else list(arrs) + list(lands)
    n, nb = len(lands), len(bufs)
    nsem = sum(_PLAN_COPIES[p] for p in _plans(plan, n))
    dep_specs, deps = _dep_specs(after)

    def body(*refs):
        ins, land_refs = refs[:n], refs[nb - n:nb]
        send, recv = refs[nb + len(deps)], refs[nb + len(deps) + 1]
        for cp in _exchange_copies(plan, ins, land_refs, send, recv):
            cp.start()
        refs[-1][...] = jnp.zeros_like(refs[-1])

    out = pl.pallas_call(
        body, name=name,
        out_shape=(pltpu.SemaphoreType.DMA((nsem,)), pltpu.SemaphoreType.DMA((nsem,)),
                   *[pltpu.HBM(a.shape, a.dtype) for a in bufs], _sds((8, LANE), F32)),
        in_specs=[_HBM] * nb + dep_specs,
        out_specs=(_SEM, _SEM, *([_HBM] * nb), pl.BlockSpec(memory_space=pltpu.VMEM)),
        input_output_aliases={i: 2 + i for i in range(nb)},
        compiler_params=pltpu.CompilerParams(has_side_effects=_DATAFLOW),
    )(*[pltpu.with_memory_space_constraint(a, pltpu.HBM) for a in bufs], *deps)
    return (plan, n, out[0], out[1], list(out[2:2 + nb])), out[-1]


def _exchange_now(plan, lands, *, name):
    n = len(lands)
    nsem = sum(_PLAN_COPIES[p] for p in _plans(plan, n))

    def body(*refs):
        land_refs, send, recv = refs[n:2 * n], refs[2 * n], refs[2 * n + 1]
        copies = _exchange_copies(plan, land_refs, land_refs, send, recv)
        for cp in copies:
            cp.start()
        for cp in copies:
            cp.wait_send()
            cp.wait_recv()

    hbm = pl.BlockSpec(memory_space=pl.ANY)
    return pl.pallas_call(
        body, in_specs=[hbm] * n, out_specs=[hbm] * n, out_shape=[_sds(a.shape, a.dtype) for a in lands],
        input_output_aliases={i: i for i in range(n)},
        scratch_shapes=[pltpu.SemaphoreType.DMA((nsem,)), pltpu.SemaphoreType.DMA((nsem,))], name=name)(*lands)


def _exchange_wait(state, after, *, name):
    plan, n, send_sem, recv_sem, bufs = state
    nb = len(bufs)
    after = list(after) if isinstance(after, (list, tuple)) else [after]

    def body(*refs):
        ins, land_refs, send, recv = refs[:n], refs[nb - n:nb], refs[nb], refs[nb + 1]
        for cp in _exchange_copies(plan, ins, land_refs, send, recv):
            cp.wait_send()
            cp.wait_recv()

    out = pl.pallas_call(
        body, name=name, out_shape=[pltpu.HBM(a.shape, a.dtype) for a in bufs],
        in_specs=[_HBM] * nb + [_SEM, _SEM] + [pl.BlockSpec(memory_space=pl.ANY)] * len(after), out_specs=[_HBM] * nb,
        input_output_aliases={i: i for i in range(nb)},
        compiler_params=pltpu.CompilerParams(has_side_effects=_DATAFLOW),
    )(*bufs, send_sem, recv_sem, *after)
    return list(out[:n]), list(out[nb - n:])


def _dep_specs(after):
    return ([], []) if after is None else ([pl.BlockSpec(memory_space=pl.ANY)], [after])


def _undo_column_split(g):
    return jnp.transpose(g, (1, 0, 2)).reshape(g.shape[1], N_DEV * g.shape[2])


def _column_split(a):
    r, c = a.shape
    return jnp.transpose(a.reshape(r, N_DEV, c // N_DEV), (1, 0, 2))


class _WholeWeights:
    def __init__(self, groups):
        self.groups = groups
        self.grads = {}

    def fetch(self, group, after):
        return self.groups[group]

    def emit(self, group, grads):
        self.grads.update(grads)
        return None


def _local_step(x, target, replicated, src):
    d = x.shape[1]
    mix_g, ffn_g = replicated["mix_g"], replicated["ffn_g"]
    cp = src.fetch("cp", [])
    cp_mid = (cp["conv_w"], replicated["conv_b"], replicated["ln_g"], replicated["ln_b"], replicated["pool_w"],
              replicated["pool_scale"])

    h1, cat, z0, u0, cv0, h0 = _cp_mid_fwd(x, cp["meta"], mix_g[0:1], cp["cp_w_in_t"], cp["cp_w_out"], *cp_mid,
                                           name="cp_mixer")
    ffn0 = src.fetch("ffn0", h1)
    h2, uf0, rf0 = _ffn_fwd(h1, ffn_g[0:1], ffn0["w1"], ffn0["w2"], name="ffn0")
    gla = src.fetch("gla", h2)
    gla_mid = (gla["gate_w"], gla["gate_b"], gla["head_g"])
    h3, og, states, z1, u1, gla_w_in_t = _gla_mid_fwd(h2, mix_g[1:2], gla["gla_w_in_t"], gla["gla_w_out"], *gla_mid,
                                                      name="gla_mixer")
    ffn1 = src.fetch("ffn1", h3)
    dh4, uf1, rf1, loss, d_final_g = _ffn_fwd(h3, ffn_g[1:2], ffn1["w1"], ffn1["w2"],
                                              loss_head=(replicated["final_g"], target), name="ffn1_loss")

    dh3, dhh1, dob1, dffn_g1 = _ffn_bwd_x(h3, dh4, ffn_g[1:2], rf1, ffn1["w1"], ffn1["w2"], name="ffn1_bwd_x")
    sent = src.emit("ffn1_w1", dict(w1=_linear_bwd_w(uf1, dhh1, column_blocks=N_DEV, name="ffn1_dw1")))
    sent = src.emit("ffn1_w2", dict(w2=_linear_bwd_w(rf1, dob1, square_x=True, after=sent, name="ffn1_dw2")))
    d_gla_w_out = _linear_bwd_w(og, dh3, name="gla_out_dw")
    dh2, dmix_g1, dz1, d_gate_w, d_gate_b, d_head_g = _gla_mid_bwd(
        z1, h2, mix_g[1:2], gla_w_in_t, dh3, gla["gla_w_out"], states, *gla_mid, after=sent, name="gla_mixer_bwd")
    d_gla_w_in_t = _linear_bwd_w(dz1, u1, row_blocks=gla["gla_w_in_t"].shape[:2], name="gla_in_dw")
    sent = src.emit("gla", dict(gla_w_in_t=d_gla_w_in_t, gla_w_out=d_gla_w_out))
    dh1, dhh0, dob0, dffn_g0 = _ffn_bwd_x(h1, dh2, ffn_g[0:1], rf0, ffn0["w1"], ffn0["w2"], after=sent, name="ffn0_bwd_x")
    sent = src.emit("ffn0_w1", dict(w1=_linear_bwd_w(uf0, dhh0, column_blocks=N_DEV, name="ffn0_dw1")))
    sent = src.emit("ffn0_w2", dict(w2=_linear_bwd_w(rf0, dob0, square_x=True, after=sent, name="ffn0_dw2")))
    d_cp_w_out = _linear_bwd_w(cat, dh1, after=sent, name="cp_out_dw")
    sent = src.emit("cp_out", dict(cp_w_out=d_cp_w_out))
    dx, dh0_first, dmix_g0, dz0, d_conv_w, d_conv_b, d_ln_g, d_ln_b, d_pool_w, d_pool_scale = _cp_mid_bwd(
        z0, cv0, h0, mix_g[0:1], cp["cp_w_in_t"], dh1, cp["cp_w_out"], *cp_mid, after=sent, name="cp_mixer_bwd")
    d_cp_w_in_t = _linear_bwd_w(dz0, u0, name="cp_in_dw")

    small = dict(
        mix_g=jnp.concatenate([dmix_g0, dmix_g1]), ffn_g=jnp.concatenate([dffn_g0, dffn_g1]), conv_b=d_conv_b, ln_g=d_ln_g,
        ln_b=d_ln_b, pool_w=d_pool_w, pool_scale=d_pool_scale, final_g=d_final_g, meta=dh0_first[PAD_ROWS:], conv_w=d_conv_w,
        gate_w=d_gate_w, gate_b=d_gate_b, head_g=d_head_g)
    src.emit("cp", dict(cp_w_in_t=d_cp_w_in_t, small=small, loss=loss))
    return loss, dx, small


_REPLICATED = ("mix_norm_g", "ffn_norm_g", "cp_conv_b", "cp_ln_g", "cp_ln_b", "cp_pool_w", "cp_pool_scale", "final_norm_g")
_SMALL_SHARDED = ("meta_tokens", "cp_conv_w", "gla_gate_w2", "gla_gate_b", "gla_head_g")
_NAMES = ("meta_tokens", "mix_norm_g", "ffn_norm_g", "ffn_w1", "ffn_w2", "cp_w_in", "cp_conv_w", "cp_conv_b", "cp_ln_g",
          "cp_ln_b", "cp_pool_w", "cp_pool_scale", "cp_w_out", "gla_w_in", "gla_gate_w2", "gla_gate_b", "gla_head_g",
          "gla_w_out", "final_norm_g")
_SMALL_GRADS = ("mix_g", "ffn_g", "conv_b", "ln_g", "ln_b", "pool_w", "pool_scale", "final_g", "meta", "conv_w", "gate_w",
                "gate_b", "head_g")
_GROUPS = ("cp", "ffn0", "gla", "ffn1")
_TWO_LEG_GATHERS = ("cp", "ffn0", "ffn1")


class _Exchanges:
    def __init__(self, w, d):
        self.d = d
        small = [w[n].reshape(w[n].shape[-2:]) for n in _SMALL_SHARDED]
        self.small_shard_shapes = [w[n].shape for n in _SMALL_SHARDED]
        shards = dict(
            cp=[(w["cp_w_in"][0].T, BF16), (w["cp_w_out"][0], BF16)] + [(a, F32) for a in small],
            ffn0=[((w["ffn_w1"], 0), BF16), ((w["ffn_w2"], 0), BF16)],
            gla=[(w["gla_w_in"][0].T, BF16), (w["gla_w_out"][0], BF16)],
            ffn1=[((w["ffn_w1"], 1), BF16), ((w["ffn_w2"], 1), BF16)])
        self.gathers = {}
        self.sent = {}
        token = None
        for group in _GROUPS:
            lands = _place_own(shards[group], after=token, name=f"place_w_{group}")
            plan = _plan_gather_first if group in _TWO_LEG_GATHERS else _plan_gather_direct
            self.gathers[group], token = _exchange_start(plan, None, lands, after=token, name=f"start_w_{group}")
        self.token = token

    def fetch(self, group, after):
        d = self.d
        after = (list(after) if isinstance(after, (list, tuple)) else [after]) + [self.token]
        _, got = _exchange_wait(self.gathers[group], after, name=f"wait_w_{group}")
        if group in _TWO_LEG_GATHERS:
            got = _exchange_now(_plan_gather_relay, got, name=f"relay_w_{group}")
        if group in ("ffn0", "ffn1"):
            return dict(w1=got[0], w2=got[1])
        if group == "gla":
            return dict(gla_w_in_t=got[0], gla_w_out=got[1].reshape(d, d), gate_w=self.gate_w, gate_b=self.gate_b,
                        head_g=self.head_g)
        meta, conv_w, gate_w, self.gate_b, self.head_g = [_undo_column_split(a) for a in got[2:]]
        self.gate_w = jnp.pad(gate_w, ((0, GATE_PAD - GATE_RANK), (0, 0))).astype(BF16)
        return dict(cp_w_in_t=got[0].reshape(-1, d), cp_w_out=got[1].reshape(d, d), meta=meta,
                    conv_w=jnp.pad(conv_w, ((0, 1), (0, 0))))

    def emit(self, group, g):
        d = self.d
        if group in ("ffn0_w1", "ffn1_w1"):
            arrs = [g["w1"]]
        elif group in ("ffn0_w2", "ffn1_w2"):
            arrs = [g["w2"].reshape(N_DEV, -1, d)]
        elif group == "gla":
            arrs = [g["gla_w_in_t"], g["gla_w_out"].reshape(N_DEV, d // N_DEV, d)]
        elif group == "cp_out":
            arrs = [g["cp_w_out"].reshape(N_DEV, d // N_DEV, d)]
        else:
            s = dict(g["small"])
            s.update(pool_w=s["pool_w"][None], conv_w=s["conv_w"][:CONV_WIDTH], gate_w=s["gate_w"][:GATE_RANK])
            own = [s[n] for n in _SMALL_GRADS[:len(_REPLICATED)]]
            own += [_column_split(s[n]).reshape((N_DEV,) + shape)
                    for n, shape in zip(_SMALL_GRADS[len(_REPLICATED):], self.small_shard_shapes)]
            plans = [_plan_to_all] * len(_REPLICATED) + [_plan_split_to_all] * len(_SMALL_SHARDED)
            lands = [lax.empty((N_DEV,) + a.shape, F32) for a in own[:len(_REPLICATED)]]
            lands += [lax.empty(a.shape, F32) for a in own[len(_REPLICATED):]]
            own.append(g["loss"])
            plans.append(_plan_to_all)
            lands.append(lax.empty((N_DEV,) + g["loss"].shape, F32))
            own.append(g["cp_w_in_t"].reshape(N_DEV, -1, d))
            plans.append(_plan_split_to_all)
            lands.append(lax.empty(own[-1].shape, BF16))
            self.sent[group], self.token = _exchange_start(plans, own, lands, after=self.token, name=f"start_g_{group}")
            return self.token
        self.sent[group], self.token = _exchange_start(_plan_split_to_all, arrs, [lax.empty(a.shape, a.dtype) for a in arrs],
                                                       after=self.token, name=f"start_g_{group}")
        return self.token

    def finish(self, w, mom, var):
        out = {}
        after = self.token

        def landed(group):
            own, got = _exchange_wait(self.sent[group], after, name=f"wait_g_{group}")
            return list(zip(own, got))

        def adam(n, parts, behind=None, transposed=False):
            by_row = transposed and w[n].shape[2] % 8 != 0
            if by_row:
                flip, back = (lambda a: jnp.transpose(a, (2, 0, 1))), (lambda a: jnp.transpose(a, (1, 2, 0)))
            else:
                flip = back = (lambda a: jnp.transpose(a, (0, 2, 1))) if transposed else (lambda a: a)
            res = _reduce_adam(parts, flip(w[n]), flip(mom[n]), flip(var[n]), by_row=by_row, after=behind, name=f"adam_{n}")
            out[n] = tuple(back(a) for a in res)
            return res[0]

        ffn1_w1 = landed("ffn1_w1")
        after = ffn1_w1[0][1]
        ffn1_w2 = landed("ffn1_w2")
        after = ffn1_w2[0][1]
        gla = landed("gla")
        after = adam("gla_w_in", [gla[0]], transposed=True)
        after = adam("gla_w_out", [gla[1]], after)
        ffn0_w1 = landed("ffn0_w1")
        after = adam("ffn_w1", [ffn0_w1[0], ffn1_w1[0]])
        ffn0_w2 = landed("ffn0_w2")
        after = adam("ffn_w2", [ffn0_w2[0], ffn1_w2[0]])
        cp_out = landed("cp_out")
        after = adam("cp_w_out", [cp_out[0]])
        *small, loss, cp_w_in = landed("cp")
        names = _REPLICATED + _SMALL_SHARDED
        split = [False] * len(_REPLICATED) + [True] * len(_SMALL_SHARDED)
        small_new = _adam_small([own for own, _ in small], [got for _, got in small], split, [w[n] for n in names],
                                [mom[n] for n in names], [var[n] for n in names], name="adam_small")
        out.update(zip(names, small_new))
        out["loss"] = _sum8(*loss, name="sum_loss")[0, 0]
        adam("cp_w_in", [cp_w_in], small_new[0][0], transposed=True)
        return out


def kernel(x, meta_tokens, mix_norm_g, ffn_norm_g, ffn_w1, ffn_w2, cp_w_in, cp_conv_w, cp_conv_b, cp_ln_g, cp_ln_b, cp_pool_w, cp_pool_scale, cp_w_out, gla_w_in, gla_gate_w2, gla_gate_b, gla_head_g, gla_w_out, final_norm_g, loss_target, m_meta_tokens, m_mix_norm_g, m_ffn_norm_g, m_ffn_w1, m_ffn_w2, m_cp_w_in, m_cp_conv_w, m_cp_conv_b, m_cp_ln_g, m_cp_ln_b, m_cp_pool_w, m_cp_pool_scale, m_cp_w_out, m_gla_w_in, m_gla_gate_w2, m_gla_gate_b, m_gla_head_g, m_gla_w_out, m_final_norm_g, v_meta_tokens, v_mix_norm_g, v_ffn_norm_g, v_ffn_w1, v_ffn_w2, v_cp_w_in, v_cp_conv_w, v_cp_conv_b, v_cp_ln_g, v_cp_ln_b, v_cp_pool_w, v_cp_pool_scale, v_cp_w_out, v_gla_w_in, v_gla_gate_w2, v_gla_gate_b, v_gla_head_g, v_gla_w_out, v_final_norm_g):
    w = dict(meta_tokens=meta_tokens, mix_norm_g=mix_norm_g, ffn_norm_g=ffn_norm_g, ffn_w1=ffn_w1, ffn_w2=ffn_w2,
             cp_w_in=cp_w_in, cp_conv_w=cp_conv_w, cp_conv_b=cp_conv_b, cp_ln_g=cp_ln_g, cp_ln_b=cp_ln_b,
             cp_pool_w=cp_pool_w, cp_pool_scale=cp_pool_scale, cp_w_out=cp_w_out, gla_w_in=gla_w_in,
             gla_gate_w2=gla_gate_w2, gla_gate_b=gla_gate_b, gla_head_g=gla_head_g, gla_w_out=gla_w_out,
             final_norm_g=final_norm_g.reshape(1, -1))
    mom = dict(meta_tokens=m_meta_tokens, mix_norm_g=m_mix_norm_g, ffn_norm_g=m_ffn_norm_g, ffn_w1=m_ffn_w1, ffn_w2=m_ffn_w2,
               cp_w_in=m_cp_w_in, cp_conv_w=m_cp_conv_w, cp_conv_b=m_cp_conv_b, cp_ln_g=m_cp_ln_g, cp_ln_b=m_cp_ln_b,
               cp_pool_w=m_cp_pool_w, cp_pool_scale=m_cp_pool_scale, cp_w_out=m_cp_w_out, gla_w_in=m_gla_w_in,
               gla_gate_w2=m_gla_gate_w2, gla_gate_b=m_gla_gate_b, gla_head_g=m_gla_head_g, gla_w_out=m_gla_w_out,
               final_norm_g=m_final_norm_g.reshape(1, -1))
    var = dict(meta_tokens=v_meta_tokens, mix_norm_g=v_mix_norm_g, ffn_norm_g=v_ffn_norm_g, ffn_w1=v_ffn_w1, ffn_w2=v_ffn_w2,
               cp_w_in=v_cp_w_in, cp_conv_w=v_cp_conv_w, cp_conv_b=v_cp_conv_b, cp_ln_g=v_cp_ln_g, cp_ln_b=v_cp_ln_b,
               cp_pool_w=v_cp_pool_w, cp_pool_scale=v_cp_pool_scale, cp_w_out=v_cp_w_out, gla_w_in=v_gla_w_in,
               gla_gate_w2=v_gla_gate_w2, gla_gate_b=v_gla_gate_b, gla_head_g=v_gla_head_g, gla_w_out=v_gla_w_out,
               final_norm_g=v_final_norm_g.reshape(1, -1))
    d = x.shape[-1]
    replicated = dict(mix_g=w["mix_norm_g"], ffn_g=w["ffn_norm_g"], conv_b=w["cp_conv_b"], ln_g=w["cp_ln_g"],
                      ln_b=w["cp_ln_b"], pool_w=w["cp_pool_w"][0].astype(BF16), pool_scale=w["cp_pool_scale"],
                      final_g=w["final_norm_g"])
    exchanges = _Exchanges(w, d)
    _, grad_x, _ = _local_step(x[0], loss_target[0], replicated, exchanges)
    out = exchanges.finish(w, mom, var)
    loss = out.pop("loss")

    def leaf(n, k):
        a = out[n][k]
        return a.reshape(-1) if n == "final_norm_g" else a

    return (loss, grad_x[None], *[leaf(n, 0) for n in _NAMES], *[leaf(n, 1) for n in _NAMES],
            *[leaf(n, 2) for n in _NAMES], *[leaf(n, 3) for n in _NAMES])
```

```python
import functools

import jax
import jax.numpy as jnp
from jax import lax
from jax.experimental import pallas as pl
from jax.experimental.pallas import tpu as pltpu

F32, BF16 = jnp.float32, jnp.bfloat16
N_DEV = 8
CHUNK = 64
N_META = 16
PAD_ROWS = CHUNK - N_META
HALO = 32
EPS = 1e-5
CONV_WIDTH = 31
POOL_WINDOWS = (2, 4, 8, 16)
HEADS = 4
GATE_RANK = 16
GATE_NORM = 16.0
GATE_PAD = 128
ADAM_LR, ADAM_B1, ADAM_B2, ADAM_EPS, ADAM_WD, ADAM_STEP = 0.001, 0.9, 0.999, 1e-08, 0.01, 10
V7X_VMEM_LIMIT = 56 * 2 ** 20
LANE = 128


def _cparams(*sem):
    return pltpu.CompilerParams(dimension_semantics=sem, vmem_limit_bytes=V7X_VMEM_LIMIT)


def _row_tile(t, cap):
    best = CHUNK
    for r in range(CHUNK, min(t, cap) + 1, CHUNK):
        if t % r == 0:
            best = r
    return best


def _resident(shape):
    return pl.BlockSpec(shape, lambda *_: (0,) * len(shape), pipeline_mode=pl.Buffered(1))


def _dot(a, b):
    return jnp.dot(a, b, preferred_element_type=F32)


def _dot_nt(a, b):
    return lax.dot_general(a, b, (((1,), (1,)), ((), ())), preferred_element_type=F32)


def _dot_tn(a, b):
    return lax.dot_general(a, b, (((0,), (0,)), ((), ())), preferred_element_type=F32)


def _rowsum(a):
    return jnp.sum(a, axis=0, keepdims=True)


def _sigmoid(a):
    return 1.0 / (1.0 + jnp.exp(-a))


def _row_ids(tile, rt):
    return tile * rt + lax.broadcasted_iota(jnp.int32, (rt, 1), 0)


def _sds(shape, dtype):
    return jax.ShapeDtypeStruct(shape, dtype)


DW_ROWS = 1024


def _linear_bwd_w(x, dy, *, square_x=False, column_blocks=None, row_blocks=None, after=None, name):
    t, k = x.shape
    n = dy.shape[1]
    cut_k = k > n and column_blocks is None
    assert cut_k or row_blocks is None
    width = k if cut_k else n
    blk = n // column_blocks if column_blocks else max(c for c in (640, 512, 384, 256, LANE) if width % c == 0)
    dep_specs, deps = _dep_specs(after)

    def body(*refs):
        x_ref, dy_ref, o_ref, acc = refs[len(deps):]
        for c0 in range(0, t, DW_ROWS):
            rows = slice(c0, min(c0 + DW_ROWS, t))
            xv = x_ref[rows, :]
            if square_x:
                xv = xv.astype(F32)
                xv = xv * xv
            part = _dot_tn(xv.astype(BF16), dy_ref[rows, :].astype(BF16))
            if c0 == 0:
                acc[...] = part
            else:
                acc[...] += part
        if row_blocks is None:
            o_ref[...] = acc[...].astype(BF16)
            return
        nb, rpb = row_blocks
        for s in range(width // blk):
            @pl.when(pl.program_id(0) == s)
            def _(s=s):
                for b in range(nb):
                    lo, hi = max(s * blk, b * rpb), min((s + 1) * blk, (b + 1) * rpb)
                    if lo < hi:
                        o_ref[b, lo - b * rpb:hi - b * rpb, :] = acc[lo - s * blk:hi - s * blk, :].astype(BF16)

    out_shape = _sds((k, n), BF16)
    semantics = "parallel"
    if cut_k:
        in_specs = [pl.BlockSpec((t, blk), lambda j: (0, j)), _resident((t, n))]
        out_specs = pl.BlockSpec((blk, n), lambda j: (j, 0))
        acc_shape = (blk, n)
        if row_blocks:
            out_shape = _sds(row_blocks + (n,), BF16)
            out_specs = pl.BlockSpec(out_shape.shape, lambda j: (0, 0, 0))
            semantics = "arbitrary"
    else:
        in_specs = [_resident((t, k)), pl.BlockSpec((t, blk), lambda j: (0, j))]
        out_specs = pl.BlockSpec((k, blk), lambda j: (0, j))
        acc_shape = (k, blk)
        if column_blocks:
            out_specs = pl.BlockSpec((None, k, blk), lambda j: (j, 0, 0))
            out_shape = _sds((column_blocks, k, blk), BF16)
    return pl.pallas_call(
        body, grid=(width // blk,), in_specs=dep_specs + in_specs, out_specs=out_specs, out_shape=out_shape,
        scratch_shapes=[pltpu.VMEM(acc_shape, F32)], compiler_params=_cparams(semantics), name=name)(*deps, x, dy)


FFN_BLOCKS_PER_STEP = 2


def _ffn_fwd(h, gain, w1g, w2g, *, loss_head=None, name):
    t, d = h.shape
    f8 = w1g.shape[-1]
    rt = _row_tile(t, 832)
    nb = FFN_BLOCKS_PER_STEP
    nstep = N_DEV // nb

    def body(*refs):
        if loss_head is None:
            h_ref, g_ref, w1_ref, w2_ref, o_ref, u_ref, r_ref, acc_ref = refs
        else:
            (h_ref, g_ref, w1_ref, w2_ref, fg_ref, tgt_ref, o_ref, u_ref, r_ref, loss_ref, dfg_ref, acc_ref, t_ref,
             t_sem) = refs
        i, j = pl.program_id(0), pl.program_id(1)

        def target_rows(act):
            @pl.when(i == 0)
            def _():
                act(pltpu.make_async_copy(tgt_ref.at[pl.ds(0, rt - CHUNK)], t_ref.at[pl.ds(CHUNK, rt - CHUNK)], t_sem.at[0]))

            if t > rt:
                @pl.when(i > 0)
                def _():
                    act(pltpu.make_async_copy(tgt_ref.at[pl.ds(pl.multiple_of(i * rt - CHUNK, CHUNK), rt)], t_ref,
                                              t_sem.at[0]))

        @pl.when(j == 0)
        def _():
            if loss_head is not None:
                @pl.when(i == 0)
                def _():
                    t_ref[0:CHUNK, :] = jnp.zeros((CHUNK, d), F32)

                target_rows(lambda copy: copy.start())

            hv = h_ref[...]
            u_ref[...] = (hv * lax.rsqrt(jnp.mean(hv * hv, axis=-1, keepdims=True) + EPS) * g_ref[...]).astype(BF16)
            acc_ref[...] = jnp.zeros_like(acc_ref)

        part = None
        for b in range(nb):
            a = jnp.maximum(_dot(u_ref[...], w1_ref[b]), 0.0)
            r_ref[:, b * f8:(b + 1) * f8] = a.astype(BF16)
            term = _dot((a * a).astype(BF16), w2_ref[b])
            part = term if part is None else part + term
        acc_ref[...] += part

        @pl.when(j == nstep - 1)
        def _():
            y = h_ref[...] + acc_ref[...]
            if loss_head is None:
                o_ref[...] = y
                return

            @pl.when(i == 0)
            def _():
                loss_ref[...] = jnp.zeros_like(loss_ref)
                dfg_ref[...] = jnp.zeros_like(dfg_ref)

            target_rows(lambda copy: copy.wait())

            rstd = lax.rsqrt(jnp.mean(y * y, axis=-1, keepdims=True) + EPS)
            xh = y * rstd
            err = jnp.where(_row_ids(i, rt) >= CHUNK, xh * fg_ref[...] - t_ref[...], 0.0)
            loss_ref[...] += (0.5 / d) * jnp.sum(err * err)
            dy = err * (1.0 / d)
            dfg_ref[...] += _rowsum(dy * xh)
            dxh = dy * fg_ref[...]
            o_ref[...] = rstd * (dxh - xh * jnp.mean(dxh * xh, axis=-1, keepdims=True))

    rows = lambda i, j: (i, 0)
    in_specs = [pl.BlockSpec((rt, d), rows), _resident((1, d)),
                pl.BlockSpec((nb, d, f8), lambda i, j: (j, 0, 0)), pl.BlockSpec((nb, f8, d), lambda i, j: (j, 0, 0))]
    out_specs = [pl.BlockSpec((rt, d), rows), pl.BlockSpec((rt, d), rows), pl.BlockSpec((rt, nb * f8), lambda i, j: (i, j))]
    out_shape = [_sds((t, d), F32), _sds((t, d), BF16), _sds((t, N_DEV * f8), BF16)]
    args = [h, gain, w1g, w2g]
    scratch_shapes = [pltpu.VMEM((rt, d), F32)]
    if loss_head is not None:
        in_specs += [_resident((1, d)), pl.BlockSpec(memory_space=pl.ANY)]
        out_specs += [pl.BlockSpec((8, LANE), lambda i, j: (0, 0)), pl.BlockSpec((1, d), lambda i, j: (0, 0))]
        out_shape += [_sds((8, LANE), F32), _sds((1, d), F32)]
        args += list(loss_head)
        scratch_shapes += [pltpu.VMEM((rt, d), F32), pltpu.SemaphoreType.DMA((1,))]
    return pl.pallas_call(
        body, grid=(t // rt, nstep), in_specs=in_specs, out_specs=out_specs, out_shape=out_shape,
        scratch_shapes=scratch_shapes,
        compiler_params=_cparams("arbitrary" if loss_head is not None else "parallel", "arbitrary"), name=name)(*args)


def _ffn_bwd_x(h, dout, gain, r, w1g, w2g, *, after=None, name):
    t, d = h.shape
    f8 = w1g.shape[-1]
    rt = _row_tile(t, 832)
    nb = FFN_BLOCKS_PER_STEP
    last = N_DEV // nb - 1
    dep_specs, deps = _dep_specs(after)

    def body(*refs):
        h_ref, do_ref, g_ref, r_ref, w1_ref, w2_ref, dh_ref, dhh_ref, dob_ref, dg_ref, du_ref = refs[len(deps):]
        i, j = pl.program_id(0), pl.program_id(1)

        @pl.when(j == 0)
        def _():
            dob_ref[...] = do_ref[...].astype(BF16)
            du_ref[...] = jnp.zeros_like(du_ref)

        part = None
        for b in range(nb):
            cols = slice(b * f8, (b + 1) * f8)
            dhh = (_dot_nt(dob_ref[...], w2_ref[b]) * (2.0 * r_ref[:, cols].astype(F32))).astype(BF16)
            dhh_ref[:, cols] = dhh
            term = _dot_nt(dhh, w1_ref[b])
            part = term if part is None else part + term
        du_ref[...] += part

        @pl.when(j == last)
        def _():
            @pl.when(i == 0)
            def _():
                dg_ref[...] = jnp.zeros_like(dg_ref)

            hv = h_ref[...]
            rstd = lax.rsqrt(jnp.mean(hv * hv, axis=-1, keepdims=True) + EPS)
            xh = hv * rstd
            du = du_ref[...]
            dg_ref[...] += _rowsum(du * xh)
            dxh = du * g_ref[...]
            dh_ref[...] = do_ref[...] + rstd * (dxh - xh * jnp.mean(dxh * xh, axis=-1, keepdims=True))

    rows = lambda i, j: (i, 0)
    return pl.pallas_call(
        body, grid=(t // rt, N_DEV // nb),
        in_specs=dep_specs + [
                  pl.BlockSpec((rt, d), rows), pl.BlockSpec((rt, d), rows), _resident((1, d)),
                  pl.BlockSpec((rt, nb * f8), lambda i, j: (i, j)),
                  pl.BlockSpec((nb, d, f8), lambda i, j: (j, 0, 0)),
                  pl.BlockSpec((nb, f8, d), lambda i, j: (j, 0, 0))],
        out_specs=[pl.BlockSpec((rt, d), rows), pl.BlockSpec((rt, nb * f8), lambda i, j: (i, j)),
                   pl.BlockSpec((rt, d), rows), pl.BlockSpec((1, d), lambda i, j: (0, 0))],
        out_shape=[_sds((t, d), F32), _sds((t, N_DEV * f8), BF16), _sds((t, d), BF16), _sds((1, d), F32)],
        scratch_shapes=[pltpu.VMEM((rt, d), F32)],
        compiler_params=_cparams("arbitrary", "arbitrary"), name=name)(*deps, h, dout, gain, r, w1g, w2g)


def _lane_blocks(width):
    lb = min(LANE, width)
    return [slice(s, s + lb) for s in range(0, width, lb)]


def _conv_rows(src_ref, w_ref, offset, dst_ref, nblk, width, bias_ref=None):
    def blk(rb, carry):
        base = pl.multiple_of(rb * CHUNK, CHUNK)
        for l, ls in enumerate(_lane_blocks(width)):
            acc = jnp.zeros((CHUNK, ls.stop - ls.start), F32)
            if bias_ref is not None:
                acc = acc + bias_ref[:, ls]
            for k in range(CONV_WIDTH):
                acc = acc + w_ref[k:k + 1, ls] * src_ref[l, pl.ds(base + offset(k), CHUNK), :]
            dst_ref[l, pl.ds(base, CHUNK), :] = acc
        return carry

    lax.fori_loop(0, nblk, blk, 0)


def _to_lane_blocks(ref, row0, value):
    for l, ls in enumerate(_lane_blocks(value.shape[1])):
        ref[l, row0:row0 + value.shape[0], :] = value[:, ls]


def _from_lane_blocks(ref):
    return jnp.concatenate([ref[l] for l in range(ref.shape[0])], axis=1)


def _pool_counts(rows, window):
    return jnp.clip(rows - PAD_ROWS + 1, 1, window).astype(F32)


def _trailing_sum(v, window):
    s, sh = v, 1
    while sh < window:
        s = s + pltpu.roll(s, sh, 0)
        sh *= 2
    return s


def _leading_sum(v, window):
    s, sh, n = v, 1, v.shape[0]
    while sh < window:
        s = s + pltpu.roll(s, n - sh, 0)
        sh *= 2
    return s


def _norm_project(h_ref, g_ref, w_t_ref, u_ref, z_ref):
    hv = h_ref[...]
    u = (hv * lax.rsqrt(jnp.mean(hv * hv, axis=-1, keepdims=True) + EPS) * g_ref[...]).astype(BF16)
    u_ref[...] = u
    z_ref[...] = _dot_nt(u, w_t_ref[...])


def _project_back(dz_ref, w_t_ref, h_ref, g_ref, dres_ref, dh_ref, dg_ref, first):
    dx = _dot(dz_ref[...], w_t_ref[...])
    hv = h_ref[...]
    rstd = lax.rsqrt(jnp.mean(hv * hv, axis=-1, keepdims=True) + EPS)
    xh = hv * rstd

    @pl.when(first)
    def _():
        dg_ref[...] = jnp.zeros_like(dg_ref)

    dg_ref[...] += _rowsum(dx * xh)
    dxh = dx * g_ref[...]
    dh_ref[...] = dres_ref[...] + rstd * (dxh - xh * jnp.mean(dxh * xh, axis=-1, keepdims=True))


def _cp_mid_fwd(x, meta, gain, w_in_t, w_out, conv_w, conv_b, ln_g, ln_b, pool_w, pool_scale, *, name):
    seq, d = x.shape
    t = seq + CHUNK
    ein = w_in_t.shape[0]
    cd = conv_b.shape[1]
    pd = pool_scale.shape[1]
    pg = pd // len(POOL_WINDOWS)
    rt = _row_tile(t, 320)
    ntile = t // rt

    def body(x_ref, meta_ref, g_ref, wi_ref, wo_ref, cw_ref, cb_ref, lg_ref, lb_ref, pw_ref, ps_ref,
             ho_ref, o_ref, z_ref, u_ref, cv_ref, h0_ref, gext, pext, conv_s, hbuf, hsem):
        i = pl.program_id(0)
        slot = i % 2
        first_rows = pltpu.make_async_copy(x_ref.at[pl.ds(0, rt - CHUNK)], hbuf.at[0, pl.ds(CHUNK, rt - CHUNK)], hsem.at[0])

        def tile_rows(tile, to):
            return pltpu.make_async_copy(x_ref.at[pl.ds(pl.multiple_of(tile * rt - CHUNK, CHUNK), rt)], hbuf.at[to],
                                         hsem.at[to])

        @pl.when(i == 0)
        def _():
            first_rows.start()
            hbuf[0, 0:PAD_ROWS, :] = jnp.zeros((PAD_ROWS, d), F32)
            hbuf[0, PAD_ROWS:CHUNK, :] = meta_ref[...]
            _to_lane_blocks(gext, 0, jnp.zeros((HALO, cd), F32))
            pext[0:HALO, :] = jnp.zeros((HALO, pd), F32)

        @pl.when(i + 1 < ntile)
        def _():
            tile_rows(i + 1, 1 - slot).start()

        @pl.when(i == 0)
        def _():
            first_rows.wait()

        @pl.when(i > 0)
        def _():
            tile_rows(i, slot).wait()

        h_ref = hbuf.at[slot]
        h0_ref[...] = h_ref[...]
        _norm_project(h_ref, g_ref, wi_ref, u_ref, z_ref)

        _to_lane_blocks(gext, HALO, z_ref[:, 0:cd] * _sigmoid(z_ref[:, cd:2 * cd]))
        pext[HALO:HALO + rt, :] = z_ref[:, 2 * cd:]
        _conv_rows(gext, cw_ref, lambda k: k + HALO - (CONV_WIDTH - 1), conv_s, rt // CHUNK, cd, cb_ref)
        cv = _from_lane_blocks(conv_s)
        cv_ref[...] = cv
        xc = cv - jnp.mean(cv, axis=-1, keepdims=True)
        y = xc * lax.rsqrt(jnp.mean(xc * xc, axis=-1, keepdims=True) + EPS) * lg_ref[...] + lb_ref[...]
        rows = _row_ids(i, rt)
        a = jnp.where(rows >= PAD_ROWS, y * _sigmoid(y), 0.0)
        o_ref[:, 0:cd] = a.astype(BF16)
        for gi, window in enumerate(POOL_WINDOWS):
            ls = slice(gi * pg, (gi + 1) * pg)
            v = pext[:, ls]
            tm = _trailing_sum(v, window)[HALO:] / _pool_counts(rows, window) - v[HALO:]
            p = _dot(tm.astype(BF16), pw_ref[gi]) * ps_ref[:, ls]
            o_ref[:, cd + gi * pg:cd + (gi + 1) * pg] = p.astype(BF16)
        ho_ref[...] = h_ref[...] + _dot(o_ref[...], wo_ref[...])
        gext[:, 0:HALO, :] = gext[:, rt:rt + HALO, :]
        pext[0:HALO, :] = pext[rt:rt + HALO, :]

    nl, lb = len(_lane_blocks(cd)), min(LANE, cd)
    rows = lambda i: (i, 0)
    return pl.pallas_call(
        body, grid=(ntile,),
        in_specs=[pl.BlockSpec(memory_space=pl.ANY), _resident(meta.shape), _resident((1, d)), _resident(w_in_t.shape),
                  _resident(w_out.shape), _resident(conv_w.shape), _resident((1, cd)),
                  _resident((1, cd)), _resident((1, cd)), _resident(pool_w.shape), _resident((1, pd))],
        out_specs=[pl.BlockSpec((rt, d), rows), pl.BlockSpec((rt, cd + pd), rows), pl.BlockSpec((rt, ein), rows),
                   pl.BlockSpec((rt, d), rows), pl.BlockSpec((rt, cd), rows), pl.BlockSpec((rt, d), rows)],
        out_shape=[_sds((t, d), F32), _sds((t, cd + pd), BF16), _sds((t, ein), F32), _sds((t, d), BF16),
                   _sds((t, cd), F32), _sds((t, d), F32)],
        scratch_shapes=[pltpu.VMEM((nl, rt + HALO, lb), F32), pltpu.VMEM((rt + HALO, pd), F32),
                        pltpu.VMEM((nl, rt, lb), F32), pltpu.VMEM((2, rt, d), F32), pltpu.SemaphoreType.DMA((2,))],
        compiler_params=_cparams("arbitrary"), name=name)(x, meta, gain, w_in_t, w_out, conv_w, conv_b, ln_g, ln_b, pool_w,
                                                          pool_scale)


def _cp_mid_bwd(z, cv, h, gain, w_in_t, dh, w_out, conv_w, conv_b, ln_g, ln_b, pool_w, pool_scale, *, after=None, name):
    t, ein = z.shape
    cd = conv_b.shape[1]
    pd = pool_scale.shape[1]
    pg = pd // len(POOL_WINDOWS)
    rt = _row_tile(t, 320)
    ntile = t // rt
    per = rt // CHUNK
    dep_specs, deps = _dep_specs(after)

    def body(*refs):
        (z_ref, zh_ref, cv_ref, h_ref, g_ref, wi_ref, dh_ref, wo_ref, cw_ref, cb_ref, lg_ref, lb_ref, pw_ref, ps_ref,
         dx_ref, dfirst_ref, dg_ref, dz_ref, dcw_ref, dcb_ref, dlg_ref, dlb_ref, dpw_ref, dps_ref,
         gext, pext, conv_s, dcv, dsp, dhi, dx_sem) = refs[len(deps):]
        step = pl.program_id(0)
        tile = ntile - 1 - step
        slot = step % 2
        last_slot = (ntile - 1) % 2
        first_rows = pltpu.make_async_copy(dhi.at[last_slot, pl.ds(CHUNK, rt - CHUNK)], dx_ref.at[pl.ds(0, rt - CHUNK)],
                                           dx_sem.at[last_slot])

        def tile_rows(tile, slot):
            return pltpu.make_async_copy(dhi.at[slot], dx_ref.at[pl.ds(pl.multiple_of(tile * rt - CHUNK, CHUNK), rt)],
                                         dx_sem.at[slot])

        dcat = _dot_nt(dh_ref[...].astype(BF16), wo_ref[...])

        @pl.when(step >= 2)
        def _():
            tile_rows(tile + 2, slot).wait()

        @pl.when(step == 0)
        def _():
            for ref in (dcw_ref, dcb_ref, dlg_ref, dlb_ref, dpw_ref, dps_ref):
                ref[...] = jnp.zeros_like(ref)
            _to_lane_blocks(dcv, rt, jnp.zeros((HALO, cd), F32))
            dsp[rt:rt + HALO, :] = jnp.zeros((HALO, pd), F32)

        keep = jnp.where(tile > 0, 1.0, 0.0)
        zh = zh_ref[CHUNK - HALO:CHUNK, :]
        _to_lane_blocks(gext, 0, keep * zh[:, 0:cd] * _sigmoid(zh[:, cd:2 * cd]))
        pext[0:HALO, :] = keep * zh[:, 2 * cd:]
        za = z_ref[:, 0:cd]
        sg = _sigmoid(z_ref[:, cd:2 * cd])
        _to_lane_blocks(gext, HALO, za * sg)
        pext[HALO:HALO + rt, :] = z_ref[:, 2 * cd:]
        cv = cv_ref[...]
        xc = cv - jnp.mean(cv, axis=-1, keepdims=True)
        rstd = lax.rsqrt(jnp.mean(xc * xc, axis=-1, keepdims=True) + EPS)
        xh = xc * rstd
        y = xh * lg_ref[...] + lb_ref[...]
        sy = _sigmoid(y)
        rows = _row_ids(tile, rt)
        da = jnp.where(rows >= PAD_ROWS, dcat[:, 0:cd], 0.0)
        dy = da * (sy * (1.0 + y * (1.0 - sy)))
        dlg_ref[...] += _rowsum(dy * xh)
        dlb_ref[...] += _rowsum(dy)
        dxh = dy * lg_ref[...]
        dconv = rstd * (dxh - jnp.mean(dxh, axis=-1, keepdims=True) - xh * jnp.mean(dxh * xh, axis=-1, keepdims=True))
        dcb_ref[...] += _rowsum(dconv)
        _to_lane_blocks(dcv, 0, dconv)
        for l, ls in enumerate(_lane_blocks(cd)):
            def acc_rows(rb, accs, l=l):
                base = pl.multiple_of(rb * CHUNK, CHUNK)
                d_blk = dcv[l, pl.ds(base, CHUNK), :]
                out = []
                for k in range(CONV_WIDTH):
                    prod = d_blk * gext[l, pl.ds(base + k + HALO - (CONV_WIDTH - 1), CHUNK), :]
                    part = prod[0:8]
                    for s in range(8, CHUNK, 8):
                        part = part + prod[s:s + 8]
                    out.append(accs[k] + part)
                return tuple(out)

            zero = jnp.zeros((8, ls.stop - ls.start), F32)
            accs = lax.fori_loop(0, per, acc_rows, (zero,) * CONV_WIDTH)
            for k in range(CONV_WIDTH):
                dcw_ref[k:k + 1, ls] += _rowsum(accs[k])
        _conv_rows(dcv, cw_ref, lambda k: CONV_WIDTH - 1 - k, conv_s, per, cd)
        dglu = _from_lane_blocks(conv_s)
        dz_ref[:, 0:cd] = (dglu * sg).astype(BF16)
        dz_ref[:, cd:2 * cd] = (dglu * za * sg * (1.0 - sg)).astype(BF16)
        dcv[:, rt:rt + HALO, :] = dcv[:, 0:HALO, :]
        for gi, window in enumerate(POOL_WINDOWS):
            ls = slice(gi * pg, (gi + 1) * pg)
            v = pext[:, ls]
            cnt = _pool_counts(rows, window)
            tm = (_trailing_sum(v, window)[HALO:] / cnt - v[HALO:]).astype(BF16)
            dp = dcat[:, cd + gi * pg:cd + (gi + 1) * pg]
            dps_ref[:, ls] += _rowsum(dp * _dot(tm, pw_ref[gi]))
            dpl = (dp * ps_ref[:, ls]).astype(BF16)
            dpw_ref[gi] += _dot_tn(tm, dpl)
            dtm = _dot_nt(dpl, pw_ref[gi])
            dsp[0:rt, ls] = dtm / cnt
            dpin = _leading_sum(dsp[:, ls], window)[0:rt] - dtm
            dz_ref[:, 2 * cd + gi * pg:2 * cd + (gi + 1) * pg] = dpin.astype(BF16)
        dsp[rt:rt + HALO, :] = dsp[0:HALO, :]

        _project_back(dz_ref, wi_ref, h_ref, g_ref, dh_ref, dhi.at[slot], dg_ref, step == 0)

        @pl.when(tile > 0)
        def _():
            tile_rows(tile, slot).start()

        @pl.when(tile == 0)
        def _():
            first_rows.start()
            dfirst_ref[...] = dhi[last_slot, 0:CHUNK, :]
            if ntile > 1:
                tile_rows(1, 1 - last_slot).wait()
            first_rows.wait()

    d = h.shape[1]
    back = lambda i: (ntile - 1 - i, 0)
    halo_idx = lambda i: (jnp.maximum((ntile - 1 - i) * per - 1, 0), 0)
    const2 = lambda i: (0, 0)
    nl, lb = len(_lane_blocks(cd)), min(LANE, cd)
    return pl.pallas_call(
        body, grid=(ntile,),
        in_specs=dep_specs + [
                  pl.BlockSpec((rt, ein), back), pl.BlockSpec((CHUNK, ein), halo_idx), pl.BlockSpec((rt, cd), back),
                  pl.BlockSpec((rt, d), back),
                  _resident((1, d)), _resident(w_in_t.shape), pl.BlockSpec((rt, d), back), _resident(w_out.shape),
                  _resident(conv_w.shape), _resident((1, cd)), _resident((1, cd)), _resident((1, cd)),
                  _resident(pool_w.shape), _resident((1, pd))],
        out_specs=[pl.BlockSpec(memory_space=pl.ANY), pl.BlockSpec((CHUNK, d), const2), pl.BlockSpec((1, d), const2),
                   pl.BlockSpec((rt, ein), back), pl.BlockSpec(conv_w.shape, const2), pl.BlockSpec((1, cd), const2),
                   pl.BlockSpec((1, cd), const2), pl.BlockSpec((1, cd), const2),
                   pl.BlockSpec(pool_w.shape, lambda i: (0, 0, 0)), pl.BlockSpec((1, pd), const2)],
        out_shape=[_sds((t - CHUNK, d), F32), _sds((CHUNK, d), F32), _sds((1, d), F32),
                   _sds((t, ein), BF16), _sds(conv_w.shape, F32), _sds((1, cd), F32), _sds((1, cd), F32),
                   _sds((1, cd), F32), _sds(pool_w.shape, F32), _sds((1, pd), F32)],
        scratch_shapes=[pltpu.VMEM((nl, rt + HALO, lb), F32), pltpu.VMEM((rt + HALO, pd), F32), pltpu.VMEM((nl, rt, lb), F32),
                        pltpu.VMEM((nl, rt + HALO, lb), F32), pltpu.VMEM((rt + HALO, pd), F32), pltpu.VMEM((2, rt, d), F32),
                        pltpu.SemaphoreType.DMA((2,))],
        compiler_params=_cparams("arbitrary"), name=name)(*deps, z, z, cv, h, gain, w_in_t, dh, w_out, conv_w, conv_b, ln_g,
                                                          ln_b, pool_w, pool_scale)


def _log_decay(r, gw_ref, gb_ref, rows):
    gp = _dot(r.astype(BF16), gw_ref[...]) + gb_ref[...]
    log_sig = jnp.minimum(gp, 0.0) - jnp.log(1.0 + jnp.exp(-jnp.abs(gp)))
    return gp, jnp.where(rows >= PAD_ROWS, log_sig / GATE_NORM, 0.0)


def _tri(strict):
    r = lax.broadcasted_iota(jnp.int32, (CHUNK, CHUNK), 0)
    c = lax.broadcasted_iota(jnp.int32, (CHUNK, CHUNK), 1)
    return jnp.where(c < r if strict else c <= r, 1.0, 0.0).astype(BF16)


def _tri_dot(tri, a):
    hi = a.astype(BF16)
    rest = a - hi.astype(F32)
    mid = rest.astype(BF16)
    lo = (rest - mid.astype(F32)).astype(BF16)
    return _dot(tri, hi) + _dot(tri, mid) + _dot(tri, lo)


def _gla_mid_fwd(h, gain, w_in_blocks, w_out, gate_w, gate_b, head_g, *, name):
    t = h.shape[0]
    nblk, rpb = w_in_blocks.shape[:2]
    dk = gate_b.shape[1]
    hv = head_g.shape[1]
    hk = dk // HEADS
    dv = hv * HEADS
    r_at = 2 * dk + 2 * dv
    assert nblk * rpb == r_at + GATE_RANK
    zw = r_at + GATE_PAD
    rt = _row_tile(t, 320)
    per = rt // CHUNK
    scale = hk ** -0.5

    def body(h_ref, g_ref, wb_ref, wo_ref, gw_ref, gb_ref, hg_ref, ho_ref, o_ref, st_ref, z_ref, u_ref, wi_ref,
             s_ref, la_ref, dec_ref):
        i = pl.program_id(0)

        @pl.when(i == 0)
        def _():
            s_ref[...] = jnp.zeros_like(s_ref)
            for b in range(nblk):
                wi_ref[b * rpb:(b + 1) * rpb, :] = wb_ref[b]
            wi_ref[nblk * rpb:, :] = jnp.zeros((zw - nblk * rpb, wi_ref.shape[1]), BF16)

        _norm_project(h_ref, g_ref, wi_ref, u_ref, z_ref)

        _, la = _log_decay(z_ref[:, r_at:r_at + GATE_PAD], gw_ref, gb_ref, _row_ids(i, rt))
        la_ref[...] = la
        tri = _tri(False)

        def chunk_rows(c):
            return slice(c * CHUNK, (c + 1) * CHUNK)

        def decays(c, carry):
            rows = chunk_rows(c)
            la_c = la_ref[rows, :]
            cum = _tri_dot(tri, la_c)
            dec_ref[rows, :] = jnp.exp(_rowsum(la_c) - cum)
            return carry

        def states(c, carry):
            rows = chunk_rows(c)
            etot = jnp.exp(_rowsum(la_ref[rows, :]))
            for hd in range(HEADS):
                ks = slice(hd * hk, (hd + 1) * hk)
                kd = z_ref[rows, dk + hd * hk:dk + (hd + 1) * hk] * dec_ref[rows, ks]
                v = z_ref[rows, 2 * dk + hd * hv:2 * dk + (hd + 1) * hv]
                s_new = s_ref[hd] * etot[:, ks] + _dot_tn(v.astype(BF16), kd.astype(BF16))
                s_ref[hd] = s_new
                st_ref[c, hd] = s_new
            return carry

        def outputs(c, carry):
            rows = chunk_rows(c)
            for hd in range(HEADS):
                q = z_ref[rows, hd * hk:(hd + 1) * hk] * scale
                g = z_ref[rows, 2 * dk + dv + hd * hv:2 * dk + dv + (hd + 1) * hv]
                o = _dot_nt(q.astype(BF16), st_ref[c, hd].astype(BF16))
                on = o * lax.rsqrt(jnp.mean(o * o, axis=-1, keepdims=True) + EPS) * hg_ref[...]
                o_ref[rows, hd * hv:(hd + 1) * hv] = (on * (g * _sigmoid(g))).astype(BF16)
            return carry

        for phase in (decays, states, outputs):
            for c in range(per):
                phase(c, 0)
        ho_ref[...] = h_ref[...] + _dot(o_ref[...], wo_ref[...])

    d = h.shape[1]
    rows = lambda i: (i, 0)
    return pl.pallas_call(
        body, grid=(t // rt,),
        in_specs=[pl.BlockSpec((rt, d), rows), _resident((1, d)), _resident(w_in_blocks.shape), _resident(w_out.shape),
                  _resident(gate_w.shape), _resident((1, dk)), _resident((1, hv))],
        out_specs=[pl.BlockSpec((rt, d), rows), pl.BlockSpec((rt, dv), rows),
                   pl.BlockSpec((per, HEADS, hv, hk), lambda i: (i, 0, 0, 0)), pl.BlockSpec((rt, zw), rows),
                   pl.BlockSpec((rt, d), rows), pl.BlockSpec((zw, d), lambda i: (0, 0))],
        out_shape=[_sds((t, d), F32), _sds((t, dv), BF16), _sds((t // CHUNK, HEADS, hv, hk), F32), _sds((t, zw), F32),
                   _sds((t, d), BF16), _sds((zw, d), BF16)],
        scratch_shapes=[pltpu.VMEM((HEADS, hv, hk), F32), pltpu.VMEM((rt, dk), F32), pltpu.VMEM((rt, dk), F32)],
        compiler_params=_cparams("arbitrary"), name=name)(h, gain, w_in_blocks, w_out, gate_w, gate_b, head_g)


def _gla_mid_bwd(z, h, gain, w_in_t, dh, w_out, states, gate_w, gate_b, head_g, *, after=None, name):
    t = z.shape[0]
    dk = gate_b.shape[1]
    hv = head_g.shape[1]
    hk = dk // HEADS
    dv = hv * HEADS
    r_at = 2 * dk + 2 * dv
    rt = _row_tile(t, 320)
    ntile = t // rt
    per = rt // CHUNK
    scale = hk ** -0.5
    dep_specs, deps = _dep_specs(after)

    def body(*refs):
        (z_ref, h_ref, g_ref, wi_ref, dh_ref, wo_ref, st_ref, stp_ref, gw_ref, gb_ref, hg_ref,
         dhi_ref, dg_ref, dz_ref, dgw_ref, dgb_ref, dhg_ref,
         ds_ref, la_ref, dla_ref, dec_ref, dos_ref, e_ref, do_ref) = refs[len(deps):]
        step = pl.program_id(0)
        tile = ntile - 1 - step
        do_ref[...] = _dot_nt(dh_ref[...].astype(BF16), wo_ref[...])

        @pl.when(step == 0)
        def _():
            ds_ref[...] = jnp.zeros_like(ds_ref)
            dgw_ref[...] = jnp.zeros_like(dgw_ref)
            dgb_ref[...] = jnp.zeros_like(dgb_ref)
            dhg_ref[...] = jnp.zeros_like(dhg_ref)

        rows_id = _row_ids(tile, rt)
        r = z_ref[:, r_at:r_at + GATE_PAD]
        gp, la = _log_decay(r, gw_ref, gb_ref, rows_id)
        la_ref[...] = la
        tri, tri_strict = _tri(False), _tri(True)
        keep = jnp.where(tile > 0, 1.0, 0.0)

        def chunk_rows(c):
            return slice(c * CHUNK, (c + 1) * CHUNK)

        def recompute(c, dhg):
            rows = chunk_rows(c)
            la_c = la_ref[rows, :]
            cum = _tri_dot(tri, la_c)
            dec_ref[rows, :] = jnp.exp(_rowsum(la_c) - cum)
            for hd in range(HEADS):
                q = (z_ref[rows, hd * hk:(hd + 1) * hk] * scale).astype(BF16)
                g = z_ref[rows, 2 * dk + dv + hd * hv:2 * dk + dv + (hd + 1) * hv]
                s_b = st_ref[c, hd].astype(BF16)
                o = _dot_nt(q, s_b)
                rstd = lax.rsqrt(jnp.mean(o * o, axis=-1, keepdims=True) + EPS)
                oh = o * rstd
                sg = _sigmoid(g)
                d_og = do_ref[rows, hd * hv:(hd + 1) * hv]
                dz_ref[rows, 2 * dk + dv + hd * hv:2 * dk + dv + (hd + 1) * hv] = (
                    d_og * oh * hg_ref[...] * (sg * (1.0 + g * (1.0 - sg)))).astype(BF16)
                don = d_og * (g * sg)
                dhg = dhg + _rowsum(don * oh)
                doh = don * hg_ref[...]
                d_o = (rstd * (doh - oh * jnp.mean(doh * oh, axis=-1, keepdims=True))).astype(BF16)
                dos_ref[rows, hd * hv:(hd + 1) * hv] = d_o
                dz_ref[rows, hd * hk:(hd + 1) * hk] = (_dot(d_o, s_b) * scale).astype(BF16)
            return dhg

        def recurrence(cc, carry):
            c = per - 1 - cc
            rows = chunk_rows(c)
            etot = jnp.exp(_rowsum(la_ref[rows, :]))
            for hd in range(HEADS):
                ks = slice(hd * hk, (hd + 1) * hk)
                q = (z_ref[rows, hd * hk:(hd + 1) * hk] * scale).astype(BF16)
                dec = dec_ref[rows, ks]
                kd = z_ref[rows, dk + hd * hk:dk + (hd + 1) * hk] * dec
                v = z_ref[rows, 2 * dk + hd * hv:2 * dk + (hd + 1) * hv].astype(BF16)
                s_prev = st_ref[c - 1, hd] if c > 0 else keep * stp_ref[0, hd]
                ds_t = ds_ref[hd] + _dot_tn(dos_ref[rows, hd * hv:(hd + 1) * hv], q)
                ds_b = ds_t.astype(BF16)
                dkd = _dot(v, ds_b)
                dz_ref[rows, 2 * dk + hd * hv:2 * dk + (hd + 1) * hv] = _dot_nt(kd.astype(BF16), ds_b).astype(BF16)
                dtot = etot[:, ks] * _rowsum(ds_t * s_prev)
                ds_ref[hd] = ds_t * etot[:, ks]
                dz_ref[rows, dk + hd * hk:dk + (hd + 1) * hk] = (dkd * dec).astype(BF16)
                e_ref[rows, ks] = dkd * kd
                dla_ref[rows, ks] = jnp.broadcast_to(dtot, (CHUNK, hk))
            return carry

        def decay_cotangent(c, carry):
            rows = chunk_rows(c)
            dla_ref[rows, :] += _tri_dot(tri_strict, e_ref[rows, :])
            return carry

        dhg = jnp.zeros((1, hv), F32)
        for c in range(per):
            dhg = recompute(c, dhg)
        dhg_ref[...] += dhg
        for phase in (recurrence, decay_cotangent):
            for c in range(per):
                phase(c, 0)
        dla = jnp.where(rows_id >= PAD_ROWS, dla_ref[...], 0.0)
        dgp = dla * (1.0 / GATE_NORM) * (1.0 - _sigmoid(gp))
        dgb_ref[...] += _rowsum(dgp)
        dgp_b = dgp.astype(BF16)
        dgw_ref[...] += _dot_tn(r.astype(BF16), dgp_b)
        dz_ref[:, r_at:r_at + GATE_PAD] = _dot_nt(dgp_b, gw_ref[...]).astype(BF16)
        _project_back(dz_ref, wi_ref, h_ref, g_ref, dh_ref, dhi_ref, dg_ref, step == 0)

    d = h.shape[1]
    back = lambda i: (ntile - 1 - i, 0)
    const2 = lambda i: (0, 0)
    return pl.pallas_call(
        body, grid=(ntile,),
        in_specs=dep_specs + [
                  pl.BlockSpec((rt, z.shape[1]), back), pl.BlockSpec((rt, d), back), _resident((1, d)),
                  _resident(w_in_t.shape), pl.BlockSpec((rt, d), back), _resident(w_out.shape),
                  pl.BlockSpec((per, HEADS, hv, hk), lambda i: (ntile - 1 - i, 0, 0, 0)),
                  pl.BlockSpec((1, HEADS, hv, hk), lambda i: (jnp.maximum((ntile - 1 - i) * per - 1, 0), 0, 0, 0)),
                  _resident(gate_w.shape), _resident((1, dk)), _resident((1, hv))],
        out_specs=[pl.BlockSpec((rt, d), back), pl.BlockSpec((1, d), const2), pl.BlockSpec((rt, z.shape[1]), back),
                   pl.BlockSpec(gate_w.shape, const2), pl.BlockSpec((1, dk), const2), pl.BlockSpec((1, hv), const2)],
        out_shape=[_sds((t, d), F32), _sds((1, d), F32), _sds(z.shape, BF16), _sds(gate_w.shape, F32),
                   _sds((1, dk), F32), _sds((1, hv), F32)],
        scratch_shapes=[pltpu.VMEM((HEADS, hv, hk), F32), pltpu.VMEM((rt, dk), F32), pltpu.VMEM((rt, dk), F32),
                        pltpu.VMEM((rt, dk), F32), pltpu.VMEM((rt, dv), BF16), pltpu.VMEM((rt, dk), F32),
                        pltpu.VMEM((rt, dv), F32)],
        compiler_params=_cparams("arbitrary"), name=name)(*deps, z, h, gain, w_in_t, dh, w_out, states, states, gate_w,
                                                          gate_b, head_g)


def _adamw_math(w, g, m, v):
    m = ADAM_B1 * m + (1.0 - ADAM_B1) * g
    v = ADAM_B2 * v + (1.0 - ADAM_B2) * (g * g)
    m_hat = m / (1.0 - ADAM_B1 ** ADAM_STEP)
    v_hat = v / (1.0 - ADAM_B2 ** ADAM_STEP)
    return -ADAM_LR * (m_hat / (jnp.sqrt(v_hat) + ADAM_EPS) + ADAM_WD * w), m, v


N_CHIP = N_DEV // 2
BLOCK_ELEMS = 128 * 1024


def _my_slot():
    return 4 * lax.axis_index("x") + 2 * lax.axis_index("y") + lax.axis_index("c")


def _row_block(r, c):
    cap = max(8, BLOCK_ELEMS // (-(-c // LANE) * LANE))
    return max([b for b in range(8, r + 1, 8) if r % b == 0 and b <= cap] or [r])


def _blocks(r, c):
    rb = _row_block(r, c)
    if rb < r or r * c <= BLOCK_ELEMS:
        return rb, c
    return r, max([b for b in (512, 256, LANE) if c % b == 0 and r * b <= BLOCK_ELEMS] or [c])


def _reduce_adam(parts, w, m, v, *, by_row=False, after=None, name):
    nl, r, c = (1, w.shape[0], w.shape[2]) if by_row else w.shape
    rb, cb = _blocks(r, c)
    dep_specs, deps = _dep_specs(after)
    whole = (slice(None), 0, slice(None)) if by_row else Ellipsis

    def body(*refs):
        me = refs[0][0]
        refs = refs[1 + len(deps):]
        p_refs = refs[:2 * nl]
        w_ref, m_ref, v_ref, g_out, d_out, m_out, v_out = refs[2 * nl:]
        layer = pl.program_id(0)
        for li in range(nl):
            @pl.when(layer == li)
            def _(li=li):
                own_ref, land_ref = p_refs[2 * li], p_refs[2 * li + 1]
                mine = own_ref[...].astype(F32)
                g = None
                for dev in range(N_DEV):
                    term = jnp.where(me == dev, mine, land_ref[dev].astype(F32))
                    g = term if g is None else g + term
                g_out[whole] = g
                d_out[whole], m_out[whole], v_out[whole] = _adamw_math(w_ref[whole], g, m_ref[whole], v_ref[whole])

    if by_row:
        blk = pl.BlockSpec((rb, 1, cb), lambda l, i, j, me: (i, 0, j))
    else:
        blk = pl.BlockSpec((None, rb, cb), lambda l, i, j, me: (l, i, j))
    p_specs = []
    for li in range(nl):
        p_specs += [
            pl.BlockSpec((None, rb, cb), lambda l, i, j, me, li=li: (me[0], jnp.where(l == li, i, 0), jnp.where(l == li, j, 0))),
            pl.BlockSpec((N_DEV, rb, cb), lambda l, i, j, me, li=li: (0, jnp.where(l == li, i, 0), jnp.where(l == li, j, 0)))]
    flat = [p for pair in parts for p in pair]
    grid_spec = pltpu.PrefetchScalarGridSpec(
        num_scalar_prefetch=1, grid=(nl, r // rb, c // cb), in_specs=dep_specs + p_specs + [blk, blk, blk],
        out_specs=[blk] * 4)
    return pl.pallas_call(
        body, grid_spec=grid_spec, out_shape=[_sds(w.shape, F32)] * 4,
        compiler_params=_cparams("arbitrary", "arbitrary", "arbitrary"), name=name)(
        _my_slot().reshape(1), *deps, *flat, w, m, v)


def _sum8(own, landed, *, name):
    def body(own_ref, land_ref, o_ref):
        me = _my_slot()
        total = None
        for dev in range(N_DEV):
            term = jnp.where(me == dev, own_ref[...], land_ref[dev])
            total = term if total is None else total + term
        o_ref[...] = total

    return pl.pallas_call(body, out_shape=_sds(own.shape, F32), name=name)(own, landed)


def _adam_small(own, landed, split, w, m, v, *, name):
    n = len(w)

    def body(*refs):
        own_refs, land_refs, w_refs, m_refs, v_refs = (refs[k * n:(k + 1) * n] for k in range(5))
        outs = refs[5 * n:]
        me = _my_slot()
        for k in range(n):
            mine = own_refs[k][me] if split[k] else own_refs[k][...]
            g = None
            for dev in range(N_DEV):
                term = jnp.where(me == dev, mine, land_refs[k][dev])
                g = term if g is None else g + term
            outs[4 * k][...] = g
            outs[4 * k + 1][...], outs[4 * k + 2][...], outs[4 * k + 3][...] = _adamw_math(
                w_refs[k][...], g, m_refs[k][...], v_refs[k][...])

    out = pl.pallas_call(body, out_shape=[_sds(a.shape, F32) for a in w for _ in range(4)],
                         compiler_params=pltpu.CompilerParams(vmem_limit_bytes=V7X_VMEM_LIMIT), name=name)(
        *own, *landed, *w, *m, *v)
    return [tuple(out[4 * k:4 * k + 4]) for k in range(n)]


_HBM = pl.BlockSpec(memory_space=pltpu.HBM)
_SEM = pl.BlockSpec(memory_space=pltpu.SEMAPHORE)
_DATAFLOW = pltpu.SideEffectType.DATAFLOW_SIDE_EFFECTING


def _plan_to_all(src, land):
    x, y, c = lax.axis_index("x"), lax.axis_index("y"), lax.axis_index("c")
    return [(src, land.at[_my_slot()], (x ^ ((d >> 2) & 1), y ^ ((d >> 1) & 1), c ^ (d & 1))) for d in range(1, N_DEV)]


def _plan_split_to_all(src, land):
    x, y, c = lax.axis_index("x"), lax.axis_index("y"), lax.axis_index("c")
    peers = [(x ^ ((d >> 2) & 1), y ^ ((d >> 1) & 1), c ^ (d & 1)) for d in range(1, N_DEV)]
    return [(src.at[4 * px + 2 * py + pc], land.at[_my_slot()], (px, py, pc)) for px, py, pc in peers]


_PLAN_COPIES = {_plan_to_all: N_DEV - 1, _plan_split_to_all: N_DEV - 1}


def _plans(plan, n):
    return list(plan) if isinstance(plan, (list, tuple)) else [plan] * n


def _exchange_copies(plan, ins, lands, send, recv):
    copies, sem = [], 0
    for p, src, land in zip(_plans(plan, len(lands)), ins, lands):
        for s, dst, dev in p(src, land):
            copies.append(pltpu.make_async_remote_copy(
                src_ref=s, dst_ref=dst, send_sem=send.at[sem], recv_sem=recv.at[sem],
                device_id=dev, device_id_type=pl.DeviceIdType.MESH))
            sem += 1
    return copies


def _place_own(srcs, *, after=None, name):
    dep_specs, deps = _dep_specs(after)
    n = len(srcs)
    arrays, in_specs, shapes = [], [], []
    for a, dtype in srcs:
        if isinstance(a, tuple):
            a, layer = a
            in_specs.append(pl.BlockSpec((None,) + a.shape[1:], lambda i, layer=layer: (layer, 0, 0)))
            shapes.append(a.shape[1:])
        else:
            in_specs.append(pl.BlockSpec(a.shape, lambda i: (0, 0)))
            shapes.append(a.shape)
        arrays.append(a)
    dtypes = [dtype for _, dtype in srcs]

    def body(*refs):
        refs = refs[len(deps):]
        a_refs, o_refs, cast_refs, sem = refs[:n], refs[n:2 * n], refs[2 * n:3 * n], refs[3 * n]
        me = _my_slot()
        copies = []
        for k in range(n):
            cast_refs[k][...] = a_refs[k][...].astype(dtypes[k])
            copies.append(pltpu.make_async_copy(cast_refs[k], o_refs[k].at[me], sem.at[k]))
            copies[-1].start()
        for cp in copies:
            cp.wait()

    return pl.pallas_call(
        body, grid=(1,), in_specs=dep_specs + in_specs, out_specs=[pl.BlockSpec(memory_space=pl.ANY)] * n,
        out_shape=[_sds((N_DEV,) + shape, dtype) for shape, dtype in zip(shapes, dtypes)],
        scratch_shapes=[pltpu.VMEM(shape, dtype) for shape, dtype in zip(shapes, dtypes)] + [pltpu.SemaphoreType.DMA((n,))],
        compiler_params=pltpu.CompilerParams(vmem_limit_bytes=V7X_VMEM_LIMIT), name=name)(*deps, *arrays)


def _plan_gather_first(land, _):
    x, y, c = lax.axis_index("x"), lax.axis_index("y"), lax.axis_index("c")
    mine = land.at[_my_slot()]
    return [(mine, mine, (x, y, 1 - c))] + [(mine, mine, (x ^ (d >> 1), y ^ (d & 1), c)) for d in range(1, N_CHIP)]


def _plan_gather_relay(land, _):
    x, y, c = lax.axis_index("x"), lax.axis_index("y"), lax.axis_index("c")
    slots = [land.at[4 * (x ^ (d >> 1)) + 2 * (y ^ (d & 1)) + c] for d in range(1, N_CHIP)]
    return [(s, s, (x, y, 1 - c)) for s in slots]


def _plan_gather_direct(land, _):
    return _plan_to_all(land.at[_my_slot()], land)


_PLAN_COPIES[_plan_gather_first] = N_CHIP
_PLAN_COPIES[_plan_gather_relay] = N_CHIP - 1
_PLAN_COPIES[_plan_gather_direct] = N_DEV - 1


def _exchange_start(plan, arrs, lands, *, after=None, name):
    bufs = list(lands) if arrs is None else list(arrs) + list(lands)
    n, nb = len(lands), len(bufs)
    nsem = sum(_PLAN_COPIES[p] for p in _plans(plan, n))
    dep_specs, deps = _dep_specs(after)

    def body(*refs):
        ins, land_refs = refs[:n], refs[nb - n:nb]
        send, recv = refs[nb + len(deps)], refs[nb + len(deps) + 1]
        for cp in _exchange_copies(plan, ins, land_refs, send, recv):
            cp.start()
        refs[-1][...] = jnp.zeros_like(refs[-1])

    out = pl.pallas_call(
        body, name=name,
        out_shape=(pltpu.SemaphoreType.DMA((nsem,)), pltpu.SemaphoreType.DMA((nsem,)),
                   *[pltpu.HBM(a.shape, a.dtype) for a in bufs], _sds((8, LANE), F32)),
        in_specs=[_HBM] * nb + dep_specs,
        out_specs=(_SEM, _SEM, *([_HBM] * nb), pl.BlockSpec(memory_space=pltpu.VMEM)),
        input_output_aliases={i: 2 + i for i in range(nb)},
        compiler_params=pltpu.CompilerParams(has_side_effects=_DATAFLOW),
    )(*[pltpu.with_memory_space_constraint(a, pltpu.HBM) for a in bufs], *deps)
    return (plan, n, out[0], out[1], list(out[2:2 + nb])), out[-1]


def _exchange_now(plan, lands, *, name):
    n = len(lands)
    nsem = sum(_PLAN_COPIES[p] for p in _plans(plan, n))

    def body(*refs):
        land_refs, send, recv = refs[n:2 * n], refs[2 * n], refs[2 * n + 1]
        copies = _exchange_copies(plan, land_refs, land_refs, send, recv)
        for cp in copies:
            cp.start()
        for cp in copies:
            cp.wait_send()
            cp.wait_recv()

    hbm = pl.BlockSpec(memory_space=pl.ANY)
    return pl.pallas_call(
        body, in_specs=[hbm] * n, out_specs=[hbm] * n, out_shape=[_sds(a.shape, a.dtype) for a in lands],
        input_output_aliases={i: i for i in range(n)},
        scratch_shapes=[pltpu.SemaphoreType.DMA((nsem,)), pltpu.SemaphoreType.DMA((nsem,))], name=name)(*lands)


def _exchange_wait(state, after, *, name):
    plan, n, send_sem, recv_sem, bufs = state
    nb = len(bufs)
    after = list(after) if isinstance(after, (list, tuple)) else [after]

    def body(*refs):
        ins, land_refs, send, recv = refs[:n], refs[nb - n:nb], refs[nb], refs[nb + 1]
        for cp in _exchange_copies(plan, ins, land_refs, send, recv):
            cp.wait_send()
            cp.wait_recv()

    out = pl.pallas_call(
        body, name=name, out_shape=[pltpu.HBM(a.shape, a.dtype) for a in bufs],
        in_specs=[_HBM] * nb + [_SEM, _SEM] + [pl.BlockSpec(memory_space=pl.ANY)] * len(after), out_specs=[_HBM] * nb,
        input_output_aliases={i: i for i in range(nb)},
        compiler_params=pltpu.CompilerParams(has_side_effects=_DATAFLOW),
    )(*bufs, send_sem, recv_sem, *after)
    return list(out[:n]), list(out[nb - n:])


def _dep_specs(after):
    return ([], []) if after is None else ([pl.BlockSpec(memory_space=pl.ANY)], [after])


def _undo_column_split(g):
    return jnp.transpose(g, (1, 0, 2)).reshape(g.shape[1], N_DEV * g.shape[2])


def _column_split(a):
    r, c = a.shape
    return jnp.transpose(a.reshape(r, N_DEV, c // N_DEV), (1, 0, 2))


class _WholeWeights:
    def __init__(self, groups):
        self.groups = groups
        self.grads = {}

    def fetch(self, group, after):
        return self.groups[group]

    def emit(self, group, grads):
        self.grads.update(grads)
        return None


def _local_step(x, target, replicated, src):
    d = x.shape[1]
    mix_g, ffn_g = replicated["mix_g"], replicated["ffn_g"]
    cp = src.fetch("cp", [])
    cp_mid = (cp["conv_w"], replicated["conv_b"], replicated["ln_g"], replicated["ln_b"], replicated["pool_w"],
              replicated["pool_scale"])

    h1, cat, z0, u0, cv0, h0 = _cp_mid_fwd(x, cp["meta"], mix_g[0:1], cp["cp_w_in_t"], cp["cp_w_out"], *cp_mid,
                                           name="cp_mixer")
    ffn0 = src.fetch("ffn0", h1)
    h2, uf0, rf0 = _ffn_fwd(h1, ffn_g[0:1], ffn0["w1"], ffn0["w2"], name="ffn0")
    gla = src.fetch("gla", h2)
    gla_mid = (gla["gate_w"], gla["gate_b"], gla["head_g"])
    h3, og, states, z1, u1, gla_w_in_t = _gla_mid_fwd(h2, mix_g[1:2], gla["gla_w_in_t"], gla["gla_w_out"], *gla_mid,
                                                      name="gla_mixer")
    ffn1 = src.fetch("ffn1", h3)
    dh4, uf1, rf1, loss, d_final_g = _ffn_fwd(h3, ffn_g[1:2], ffn1["w1"], ffn1["w2"],
                                              loss_head=(replicated["final_g"], target), name="ffn1_loss")

    dh3, dhh1, dob1, dffn_g1 = _ffn_bwd_x(h3, dh4, ffn_g[1:2], rf1, ffn1["w1"], ffn1["w2"], name="ffn1_bwd_x")
    sent = src.emit("ffn1_w1", dict(w1=_linear_bwd_w(uf1, dhh1, column_blocks=N_DEV, name="ffn1_dw1")))
    sent = src.emit("ffn1_w2", dict(w2=_linear_bwd_w(rf1, dob1, square_x=True, after=sent, name="ffn1_dw2")))
    d_gla_w_out = _linear_bwd_w(og, dh3, name="gla_out_dw")
    dh2, dmix_g1, dz1, d_gate_w, d_gate_b, d_head_g = _gla_mid_bwd(
        z1, h2, mix_g[1:2], gla_w_in_t, dh3, gla["gla_w_out"], states, *gla_mid, after=sent, name="gla_mixer_bwd")
    d_gla_w_in_t = _linear_bwd_w(dz1, u1, row_blocks=gla["gla_w_in_t"].shape[:2], name="gla_in_dw")
    sent = src.emit("gla", dict(gla_w_in_t=d_gla_w_in_t, gla_w_out=d_gla_w_out))
    dh1, dhh0, dob0, dffn_g0 = _ffn_bwd_x(h1, dh2, ffn_g[0:1], rf0, ffn0["w1"], ffn0["w2"], after=sent, name="ffn0_bwd_x")
    sent = src.emit("ffn0_w1", dict(w1=_linear_bwd_w(uf0, dhh0, column_blocks=N_DEV, name="ffn0_dw1")))
    sent = src.emit("ffn0_w2", dict(w2=_linear_bwd_w(rf0, dob0, square_x=True, after=sent, name="ffn0_dw2")))
    d_cp_w_out = _linear_bwd_w(cat, dh1, after=sent, name="cp_out_dw")
    sent = src.emit("cp_out", dict(cp_w_out=d_cp_w_out))
    dx, dh0_first, dmix_g0, dz0, d_conv_w, d_conv_b, d_ln_g, d_ln_b, d_pool_w, d_pool_scale = _cp_mid_bwd(
        z0, cv0, h0, mix_g[0:1], cp["cp_w_in_t"], dh1, cp["cp_w_out"], *cp_mid, after=sent, name="cp_mixer_bwd")
    d_cp_w_in_t = _linear_bwd_w(dz0, u0, name="cp_in_dw")

    small = dict(
        mix_g=jnp.concatenate([dmix_g0, dmix_g1]), ffn_g=jnp.concatenate([dffn_g0, dffn_g1]), conv_b=d_conv_b, ln_g=d_ln_g,
        ln_b=d_ln_b, pool_w=d_pool_w, pool_scale=d_pool_scale, final_g=d_final_g, meta=dh0_first[PAD_ROWS:], conv_w=d_conv_w,
        gate_w=d_gate_w, gate_b=d_gate_b, head_g=d_head_g)
    src.emit("cp", dict(cp_w_in_t=d_cp_w_in_t, small=small, loss=loss))
    return loss, dx, small


_REPLICATED = ("mix_norm_g", "ffn_norm_g", "cp_conv_b", "cp_ln_g", "cp_ln_b", "cp_pool_w", "cp_pool_scale", "final_norm_g")
_SMALL_SHARDED = ("meta_tokens", "cp_conv_w", "gla_gate_w2", "gla_gate_b", "gla_head_g")
_NAMES = ("meta_tokens", "mix_norm_g", "ffn_norm_g", "ffn_w1", "ffn_w2", "cp_w_in", "cp_conv_w", "cp_conv_b", "cp_ln_g",
          "cp_ln_b", "cp_pool_w", "cp_pool_scale", "cp_w_out", "gla_w_in", "gla_gate_w2", "gla_gate_b", "gla_head_g",
          "gla_w_out", "final_norm_g")
_SMALL_GRADS = ("mix_g", "ffn_g", "conv_b", "ln_g", "ln_b", "pool_w", "pool_scale", "final_g", "meta", "conv_w", "gate_w",
                "gate_b", "head_g")
_GROUPS = ("cp", "ffn0", "gla", "ffn1")
_TWO_LEG_GATHERS = ("cp", "ffn0", "ffn1")


class _Exchanges:
    def __init__(self, w, d):
        self.d = d
        small = [w[n].reshape(w[n].shape[-2:]) for n in _SMALL_SHARDED]
        self.small_shard_shapes = [w[n].shape for n in _SMALL_SHARDED]
        shards = dict(
            cp=[(w["cp_w_in"][0].T, BF16), (w["cp_w_out"][0], BF16)] + [(a, F32) for a in small],
            ffn0=[((w["ffn_w1"], 0), BF16), ((w["ffn_w2"], 0), BF16)],
            gla=[(w["gla_w_in"][0].T, BF16), (w["gla_w_out"][0], BF16)],
            ffn1=[((w["ffn_w1"], 1), BF16), ((w["ffn_w2"], 1), BF16)])
        self.shards = shards
        self.gathers = {}
        self.sent = {}
        self.token = None
        for group in _GROUPS[:2]:
            self._start_gather(group, self.token)

    def _start_gather(self, group, after):
        lands = _place_own(self.shards[group], after=after, name=f"place_w_{group}")
        plan = _plan_gather_first if group in _TWO_LEG_GATHERS else _plan_gather_direct
        self.gathers[group], self.token = _exchange_start(plan, None, lands, after=self.token, name=f"start_w_{group}")

    def fetch(self, group, after):
        d = self.d
        if group == _GROUPS[1]:
            self._start_gather(_GROUPS[2], after)
            self._start_gather(_GROUPS[3], self.token)
        after = (list(after) if isinstance(after, (list, tuple)) else [after]) + [self.token]
        _, got = _exchange_wait(self.gathers[group], after, name=f"wait_w_{group}")
        if group in _TWO_LEG_GATHERS:
            got = _exchange_now(_plan_gather_relay, got, name=f"relay_w_{group}")
        if group in ("ffn0", "ffn1"):
            return dict(w1=got[0], w2=got[1])
        if group == "gla":
            return dict(gla_w_in_t=got[0], gla_w_out=got[1].reshape(d, d), gate_w=self.gate_w, gate_b=self.gate_b,
                        head_g=self.head_g)
        meta, conv_w, gate_w, self.gate_b, self.head_g = [_undo_column_split(a) for a in got[2:]]
        self.gate_w = jnp.pad(gate_w, ((0, GATE_PAD - GATE_RANK), (0, 0))).astype(BF16)
        return dict(cp_w_in_t=got[0].reshape(-1, d), cp_w_out=got[1].reshape(d, d), meta=meta,
                    conv_w=jnp.pad(conv_w, ((0, 1), (0, 0))))

    def emit(self, group, g):
        d = self.d
        if group in ("ffn0_w1", "ffn1_w1"):
            arrs = [g["w1"]]
        elif group in ("ffn0_w2", "ffn1_w2"):
            arrs = [g["w2"].reshape(N_DEV, -1, d)]
        elif group == "gla":
            arrs = [g["gla_w_in_t"], g["gla_w_out"].reshape(N_DEV, d // N_DEV, d)]
        elif group == "cp_out":
            arrs = [g["cp_w_out"].reshape(N_DEV, d // N_DEV, d)]
        else:
            s = dict(g["small"])
            s.update(pool_w=s["pool_w"][None], conv_w=s["conv_w"][:CONV_WIDTH], gate_w=s["gate_w"][:GATE_RANK])
            own = [s[n] for n in _SMALL_GRADS[:len(_REPLICATED)]]
            own += [_column_split(s[n]).reshape((N_DEV,) + shape)
                    for n, shape in zip(_SMALL_GRADS[len(_REPLICATED):], self.small_shard_shapes)]
            plans = [_plan_to_all] * len(_REPLICATED) + [_plan_split_to_all] * len(_SMALL_SHARDED)
            lands = [lax.empty((N_DEV,) + a.shape, F32) for a in own[:len(_REPLICATED)]]
            lands += [lax.empty(a.shape, F32) for a in own[len(_REPLICATED):]]
            own.append(g["loss"])
            plans.append(_plan_to_all)
            lands.append(lax.empty((N_DEV,) + g["loss"].shape, F32))
            own.append(g["cp_w_in_t"].reshape(N_DEV, -1, d))
            plans.append(_plan_split_to_all)
            lands.append(lax.empty(own[-1].shape, BF16))
            self.sent[group], self.token = _exchange_start(plans, own, lands, after=self.token, name=f"start_g_{group}")
            return self.token
        self.sent[group], self.token = _exchange_start(_plan_split_to_all, arrs, [lax.empty(a.shape, a.dtype) for a in arrs],
                                                       after=self.token, name=f"start_g_{group}")
        return self.token

    def finish(self, w, mom, var):
        out = {}
        after = self.token

        def landed(group):
            own, got = _exchange_wait(self.sent[group], after, name=f"wait_g_{group}")
            return list(zip(own, got))

        def adam(n, parts, behind=None, transposed=False):
            by_row = transposed and w[n].shape[2] % 8 != 0
            if by_row:
                flip, back = (lambda a: jnp.transpose(a, (2, 0, 1))), (lambda a: jnp.transpose(a, (1, 2, 0)))
            else:
                flip = back = (lambda a: jnp.transpose(a, (0, 2, 1))) if transposed else (lambda a: a)
            res = _reduce_adam(parts, flip(w[n]), flip(mom[n]), flip(var[n]), by_row=by_row, after=behind, name=f"adam_{n}")
            out[n] = tuple(back(a) for a in res)
            return res[0]

        ffn1_w1 = landed("ffn1_w1")
        after = ffn1_w1[0][1]
        ffn1_w2 = landed("ffn1_w2")
        after = ffn1_w2[0][1]
        gla = landed("gla")
        after = adam("gla_w_in", [gla[0]], transposed=True)
        after = adam("gla_w_out", [gla[1]], after)
        ffn0_w1 = landed("ffn0_w1")
        after = adam("ffn_w1", [ffn0_w1[0], ffn1_w1[0]])
        ffn0_w2 = landed("ffn0_w2")
        after = adam("ffn_w2", [ffn0_w2[0], ffn1_w2[0]])
        cp_out = landed("cp_out")
        after = adam("cp_w_out", [cp_out[0]])
        *small, loss, cp_w_in = landed("cp")
        names = _REPLICATED + _SMALL_SHARDED
        split = [False] * len(_REPLICATED) + [True] * len(_SMALL_SHARDED)
        small_new = _adam_small([own for own, _ in small], [got for _, got in small], split, [w[n] for n in names],
                                [mom[n] for n in names], [var[n] for n in names], name="adam_small")
        out.update(zip(names, small_new))
        out["loss"] = _sum8(*loss, name="sum_loss")[0, 0]
        adam("cp_w_in", [cp_w_in], small_new[0][0], transposed=True)
        return out


def kernel(x, meta_tokens, mix_norm_g, ffn_norm_g, ffn_w1, ffn_w2, cp_w_in, cp_conv_w, cp_conv_b, cp_ln_g, cp_ln_b, cp_pool_w, cp_pool_scale, cp_w_out, gla_w_in, gla_gate_w2, gla_gate_b, gla_head_g, gla_w_out, final_norm_g, loss_target, m_meta_tokens, m_mix_norm_g, m_ffn_norm_g, m_ffn_w1, m_ffn_w2, m_cp_w_in, m_cp_conv_w, m_cp_conv_b, m_cp_ln_g, m_cp_ln_b, m_cp_pool_w, m_cp_pool_scale, m_cp_w_out, m_gla_w_in, m_gla_gate_w2, m_gla_gate_b, m_gla_head_g, m_gla_w_out, m_final_norm_g, v_meta_tokens, v_mix_norm_g, v_ffn_norm_g, v_ffn_w1, v_ffn_w2, v_cp_w_in, v_cp_conv_w, v_cp_conv_b, v_cp_ln_g, v_cp_ln_b, v_cp_pool_w, v_cp_pool_scale, v_cp_w_out, v_gla_w_in, v_gla_gate_w2, v_gla_gate_b, v_gla_head_g, v_gla_w_out, v_final_norm_g):
    w = dict(meta_tokens=meta_tokens, mix_norm_g=mix_norm_g, ffn_norm_g=ffn_norm_g, ffn_w1=ffn_w1, ffn_w2=ffn_w2,
             cp_w_in=cp_w_in, cp_conv_w=cp_conv_w, cp_conv_b=cp_conv_b, cp_ln_g=cp_ln_g, cp_ln_b=cp_ln_b,
             cp_pool_w=cp_pool_w, cp_pool_scale=cp_pool_scale, cp_w_out=cp_w_out, gla_w_in=gla_w_in,
             gla_gate_w2=gla_gate_w2, gla_gate_b=gla_gate_b, gla_head_g=gla_head_g, gla_w_out=gla_w_out,
             final_norm_g=final_norm_g.reshape(1, -1))
    mom = dict(meta_tokens=m_meta_tokens, mix_norm_g=m_mix_norm_g, ffn_norm_g=m_ffn_norm_g, ffn_w1=m_ffn_w1, ffn_w2=m_ffn_w2,
               cp_w_in=m_cp_w_in, cp_conv_w=m_cp_conv_w, cp_conv_b=m_cp_conv_b, cp_ln_g=m_cp_ln_g, cp_ln_b=m_cp_ln_b,
               cp_pool_w=m_cp_pool_w, cp_pool_scale=m_cp_pool_scale, cp_w_out=m_cp_w_out, gla_w_in=m_gla_w_in,
               gla_gate_w2=m_gla_gate_w2, gla_gate_b=m_gla_gate_b, gla_head_g=m_gla_head_g, gla_w_out=m_gla_w_out,
               final_norm_g=m_final_norm_g.reshape(1, -1))
    var = dict(meta_tokens=v_meta_tokens, mix_norm_g=v_mix_norm_g, ffn_norm_g=v_ffn_norm_g, ffn_w1=v_ffn_w1, ffn_w2=v_ffn_w2,
               cp_w_in=v_cp_w_in, cp_conv_w=v_cp_conv_w, cp_conv_b=v_cp_conv_b, cp_ln_g=v_cp_ln_g, cp_ln_b=v_cp_ln_b,
               cp_pool_w=v_cp_pool_w, cp_pool_scale=v_cp_pool_scale, cp_w_out=v_cp_w_out, gla_w_in=v_gla_w_in,
               gla_gate_w2=v_gla_gate_w2, gla_gate_b=v_gla_gate_b, gla_head_g=v_gla_head_g, gla_w_out=v_gla_w_out,
               final_norm_g=v_final_norm_g.reshape(1, -1))
    d = x.shape[-1]
    replicated = dict(mix_g=w["mix_norm_g"], ffn_g=w["ffn_norm_g"], conv_b=w["cp_conv_b"], ln_g=w["cp_ln_g"],
                      ln_b=w["cp_ln_b"], pool_w=w["cp_pool_w"][0].astype(BF16), pool_scale=w["cp_pool_scale"],
                      final_g=w["final_norm_g"])
    exchanges = _Exchanges(w, d)
    _, grad_x, _ = _local_step(x[0], loss_target[0], replicated, exchanges)
    out = exchanges.finish(w, mom, var)
    loss = out.pop("loss")

    def leaf(n, k):
        a = out[n][k]
        return a.reshape(-1) if n == "final_norm_g" else a

    return (loss, grad_x[None], *[leaf(n, 0) for n in _NAMES], *[leaf(n, 1) for n in _NAMES],
            *[leaf(n, 2) for n in _NAMES], *[leaf(n, 3) for n in _NAMES])
```

```python
import functools

import jax
import jax.numpy as jnp
from jax import lax
from jax.experimental import pallas as pl
from jax.experimental.pallas import tpu as pltpu

F32, BF16 = jnp.float32, jnp.bfloat16
N_DEV = 8
CHUNK = 64
N_META = 16
PAD_ROWS = CHUNK - N_META
HALO = 32
EPS = 1e-5
CONV_WIDTH = 31
POOL_WINDOWS = (2, 4, 8, 16)
HEADS = 4
GATE_RANK = 16
GATE_NORM = 16.0
GATE_PAD = 128
ADAM_LR, ADAM_B1, ADAM_B2, ADAM_EPS, ADAM_WD, ADAM_STEP = 0.001, 0.9, 0.999, 1e-08, 0.01, 10
V7X_VMEM_LIMIT = 56 * 2 ** 20
LANE = 128


def _cparams(*sem):
    return pltpu.CompilerParams(dimension_semantics=sem, vmem_limit_bytes=V7X_VMEM_LIMIT)


def _row_tile(t, cap):
    best = CHUNK
    for r in range(CHUNK, min(t, cap) + 1, CHUNK):
        if t % r == 0:
            best = r
    return best


def _resident(shape):
    return pl.BlockSpec(shape, lambda *_: (0,) * len(shape), pipeline_mode=pl.Buffered(1))


def _dot(a, b):
    return jnp.dot(a, b, preferred_element_type=F32)


def _dot_nt(a, b):
    return lax.dot_general(a, b, (((1,), (1,)), ((), ())), preferred_element_type=F32)


def _dot_tn(a, b):
    return lax.dot_general(a, b, (((0,), (0,)), ((), ())), preferred_element_type=F32)


def _rowsum(a):
    return jnp.sum(a, axis=0, keepdims=True)


def _sigmoid(a):
    return 1.0 / (1.0 + jnp.exp(-a))


def _row_ids(tile, rt):
    return tile * rt + lax.broadcasted_iota(jnp.int32, (rt, 1), 0)


def _sds(shape, dtype):
    return jax.ShapeDtypeStruct(shape, dtype)


DW_ROWS = 1024


def _linear_bwd_w(x, dy, *, square_x=False, column_blocks=None, row_blocks=None, after=None, name):
    t, k = x.shape
    n = dy.shape[1]
    cut_k = k > n and column_blocks is None
    assert cut_k or row_blocks is None
    width = k if cut_k else n
    blk = n // column_blocks if column_blocks else max(c for c in (640, 512, 384, 256, LANE) if width % c == 0)
    dep_specs, deps = _dep_specs(after)

    def body(*refs):
        x_ref, dy_ref, o_ref, acc = refs[len(deps):]
        for c0 in range(0, t, DW_ROWS):
            rows = slice(c0, min(c0 + DW_ROWS, t))
            xv = x_ref[rows, :]
            if square_x:
                xv = xv.astype(F32)
                xv = xv * xv
            part = _dot_tn(xv.astype(BF16), dy_ref[rows, :].astype(BF16))
            if c0 == 0:
                acc[...] = part
            else:
                acc[...] += part
        if row_blocks is None:
            o_ref[...] = acc[...].astype(BF16)
            return
        nb, rpb = row_blocks
        for s in range(width // blk):
            @pl.when(pl.program_id(0) == s)
            def _(s=s):
                for b in range(nb):
                    lo, hi = max(s * blk, b * rpb), min((s + 1) * blk, (b + 1) * rpb)
                    if lo < hi:
                        o_ref[b, lo - b * rpb:hi - b * rpb, :] = acc[lo - s * blk:hi - s * blk, :].astype(BF16)

    out_shape = _sds((k, n), BF16)
    semantics = "parallel"
    if cut_k:
        in_specs = [pl.BlockSpec((t, blk), lambda j: (0, j)), _resident((t, n))]
        out_specs = pl.BlockSpec((blk, n), lambda j: (j, 0))
        acc_shape = (blk, n)
        if row_blocks:
            out_shape = _sds(row_blocks + (n,), BF16)
            out_specs = pl.BlockSpec(out_shape.shape, lambda j: (0, 0, 0))
            semantics = "arbitrary"
    else:
        in_specs = [_resident((t, k)), pl.BlockSpec((t, blk), lambda j: (0, j))]
        out_specs = pl.BlockSpec((k, blk), lambda j: (0, j))
        acc_shape = (k, blk)
        if column_blocks:
            out_specs = pl.BlockSpec((None, k, blk), lambda j: (j, 0, 0))
            out_shape = _sds((column_blocks, k, blk), BF16)
    return pl.pallas_call(
        body, grid=(width // blk,), in_specs=dep_specs + in_specs, out_specs=out_specs, out_shape=out_shape,
        scratch_shapes=[pltpu.VMEM(acc_shape, F32)], compiler_params=_cparams(semantics), name=name)(*deps, x, dy)


FFN_BLOCKS_PER_STEP = 2


def _ffn_fwd(h, gain, w1g, w2g, *, loss_head=None, name):
    t, d = h.shape
    f8 = w1g.shape[-1]
    rt = _row_tile(t, 832)
    nb = FFN_BLOCKS_PER_STEP
    nstep = N_DEV // nb

    def body(*refs):
        if loss_head is None:
            h_ref, g_ref, w1_ref, w2_ref, o_ref, u_ref, r_ref, acc_ref = refs
        else:
            (h_ref, g_ref, w1_ref, w2_ref, fg_ref, tgt_ref, o_ref, u_ref, r_ref, loss_ref, dfg_ref, acc_ref, t_ref,
             t_sem) = refs
        i, j = pl.program_id(0), pl.program_id(1)

        def target_rows(act):
            @pl.when(i == 0)
            def _():
                act(pltpu.make_async_copy(tgt_ref.at[pl.ds(0, rt - CHUNK)], t_ref.at[pl.ds(CHUNK, rt - CHUNK)], t_sem.at[0]))

            if t > rt:
                @pl.when(i > 0)
                def _():
                    act(pltpu.make_async_copy(tgt_ref.at[pl.ds(pl.multiple_of(i * rt - CHUNK, CHUNK), rt)], t_ref,
                                              t_sem.at[0]))

        @pl.when(j == 0)
        def _():
            if loss_head is not None:
                @pl.when(i == 0)
                def _():
                    t_ref[0:CHUNK, :] = jnp.zeros((CHUNK, d), F32)

                target_rows(lambda copy: copy.start())

            hv = h_ref[...]
            u_ref[...] = (hv * lax.rsqrt(jnp.mean(hv * hv, axis=-1, keepdims=True) + EPS) * g_ref[...]).astype(BF16)
            acc_ref[...] = jnp.zeros_like(acc_ref)

        part = None
        for b in range(nb):
            a = jnp.maximum(_dot(u_ref[...], w1_ref[b]), 0.0)
            r_ref[:, b * f8:(b + 1) * f8] = a.astype(BF16)
            term = _dot((a * a).astype(BF16), w2_ref[b])
            part = term if part is None else part + term
        acc_ref[...] += part

        @pl.when(j == nstep - 1)
        def _():
            y = h_ref[...] + acc_ref[...]
            if loss_head is None:
                o_ref[...] = y
                return

            @pl.when(i == 0)
            def _():
                loss_ref[...] = jnp.zeros_like(loss_ref)
                dfg_ref[...] = jnp.zeros_like(dfg_ref)

            target_rows(lambda copy: copy.wait())

            rstd = lax.rsqrt(jnp.mean(y * y, axis=-1, keepdims=True) + EPS)
            xh = y * rstd
            err = jnp.where(_row_ids(i, rt) >= CHUNK, xh * fg_ref[...] - t_ref[...], 0.0)
            loss_ref[...] += (0.5 / d) * jnp.sum(err * err)
            dy = err * (1.0 / d)
            dfg_ref[...] += _rowsum(dy * xh)
            dxh = dy * fg_ref[...]
            o_ref[...] = rstd * (dxh - xh * jnp.mean(dxh * xh, axis=-1, keepdims=True))

    rows = lambda i, j: (i, 0)
    in_specs = [pl.BlockSpec((rt, d), rows), _resident((1, d)),
                pl.BlockSpec((nb, d, f8), lambda i, j: (j, 0, 0)), pl.BlockSpec((nb, f8, d), lambda i, j: (j, 0, 0))]
    out_specs = [pl.BlockSpec((rt, d), rows), pl.BlockSpec((rt, d), rows), pl.BlockSpec((rt, nb * f8), lambda i, j: (i, j))]
    out_shape = [_sds((t, d), F32), _sds((t, d), BF16), _sds((t, N_DEV * f8), BF16)]
    args = [h, gain, w1g, w2g]
    scratch_shapes = [pltpu.VMEM((rt, d), F32)]
    if loss_head is not None:
        in_specs += [_resident((1, d)), pl.BlockSpec(memory_space=pl.ANY)]
        out_specs += [pl.BlockSpec((8, LANE), lambda i, j: (0, 0)), pl.BlockSpec((1, d), lambda i, j: (0, 0))]
        out_shape += [_sds((8, LANE), F32), _sds((1, d), F32)]
        args += list(loss_head)
        scratch_shapes += [pltpu.VMEM((rt, d), F32), pltpu.SemaphoreType.DMA((1,))]
    return pl.pallas_call(
        body, grid=(t // rt, nstep), in_specs=in_specs, out_specs=out_specs, out_shape=out_shape,
        scratch_shapes=scratch_shapes,
        compiler_params=_cparams("arbitrary" if loss_head is not None else "parallel", "arbitrary"), name=name)(*args)


def _ffn_bwd_x(h, dout, gain, r, w1g, w2g, *, after=None, name):
    t, d = h.shape
    f8 = w1g.shape[-1]
    rt = _row_tile(t, 832)
    nb = FFN_BLOCKS_PER_STEP
    last = N_DEV // nb - 1
    dep_specs, deps = _dep_specs(after)

    def body(*refs):
        h_ref, do_ref, g_ref, r_ref, w1_ref, w2_ref, dh_ref, dhh_ref, dob_ref, dg_ref, du_ref = refs[len(deps):]
        i, j = pl.program_id(0), pl.program_id(1)

        @pl.when(j == 0)
        def _():
            dob_ref[...] = do_ref[...].astype(BF16)
            du_ref[...] = jnp.zeros_like(du_ref)

        part = None
        for b in range(nb):
            cols = slice(b * f8, (b + 1) * f8)
            dhh = (_dot_nt(dob_ref[...], w2_ref[b]) * (2.0 * r_ref[:, cols].astype(F32))).astype(BF16)
            dhh_ref[:, cols] = dhh
            term = _dot_nt(dhh, w1_ref[b])
            part = term if part is None else part + term
        du_ref[...] += part

        @pl.when(j == last)
        def _():
            @pl.when(i == 0)
            def _():
                dg_ref[...] = jnp.zeros_like(dg_ref)

            hv = h_ref[...]
            rstd = lax.rsqrt(jnp.mean(hv * hv, axis=-1, keepdims=True) + EPS)
            xh = hv * rstd
            du = du_ref[...]
            dg_ref[...] += _rowsum(du * xh)
            dxh = du * g_ref[...]
            dh_ref[...] = do_ref[...] + rstd * (dxh - xh * jnp.mean(dxh * xh, axis=-1, keepdims=True))

    rows = lambda i, j: (i, 0)
    return pl.pallas_call(
        body, grid=(t // rt, N_DEV // nb),
        in_specs=dep_specs + [
                  pl.BlockSpec((rt, d), rows), pl.BlockSpec((rt, d), rows), _resident((1, d)),
                  pl.BlockSpec((rt, nb * f8), lambda i, j: (i, j)),
                  pl.BlockSpec((nb, d, f8), lambda i, j: (j, 0, 0)),
                  pl.BlockSpec((nb, f8, d), lambda i, j: (j, 0, 0))],
        out_specs=[pl.BlockSpec((rt, d), rows), pl.BlockSpec((rt, nb * f8), lambda i, j: (i, j)),
                   pl.BlockSpec((rt, d), rows), pl.BlockSpec((1, d), lambda i, j: (0, 0))],
        out_shape=[_sds((t, d), F32), _sds((t, N_DEV * f8), BF16), _sds((t, d), BF16), _sds((1, d), F32)],
        scratch_shapes=[pltpu.VMEM((rt, d), F32)],
        compiler_params=_cparams("arbitrary", "arbitrary"), name=name)(*deps, h, dout, gain, r, w1g, w2g)


def _lane_blocks(width):
    lb = min(LANE, width)
    return [slice(s, s + lb) for s in range(0, width, lb)]


def _conv_rows(src_ref, w_ref, offset, dst_ref, nblk, width, bias_ref=None):
    def blk(rb, carry):
        base = pl.multiple_of(rb * CHUNK, CHUNK)
        for l, ls in enumerate(_lane_blocks(width)):
            acc = jnp.zeros((CHUNK, ls.stop - ls.start), F32)
            if bias_ref is not None:
                acc = acc + bias_ref[:, ls]
            for k in range(CONV_WIDTH):
                acc = acc + w_ref[k:k + 1, ls] * src_ref[l, pl.ds(base + offset(k), CHUNK), :]
            dst_ref[l, pl.ds(base, CHUNK), :] = acc
        return carry

    lax.fori_loop(0, nblk, blk, 0)


def _to_lane_blocks(ref, row0, value):
    for l, ls in enumerate(_lane_blocks(value.shape[1])):
        ref[l, row0:row0 + value.shape[0], :] = value[:, ls]


def _from_lane_blocks(ref):
    return jnp.concatenate([ref[l] for l in range(ref.shape[0])], axis=1)


def _pool_counts(rows, window):
    return jnp.clip(rows - PAD_ROWS + 1, 1, window).astype(F32)


def _trailing_sum(v, window):
    s, sh = v, 1
    while sh < window:
        s = s + pltpu.roll(s, sh, 0)
        sh *= 2
    return s


def _leading_sum(v, window):
    s, sh, n = v, 1, v.shape[0]
    while sh < window:
        s = s + pltpu.roll(s, n - sh, 0)
        sh *= 2
    return s


def _norm_project(h_ref, g_ref, w_t_ref, u_ref, z_ref):
    hv = h_ref[...]
    u = (hv * lax.rsqrt(jnp.mean(hv * hv, axis=-1, keepdims=True) + EPS) * g_ref[...]).astype(BF16)
    u_ref[...] = u
    z_ref[...] = _dot_nt(u, w_t_ref[...])


def _project_back(dz_ref, w_t_ref, h_ref, g_ref, dres_ref, dh_ref, dg_ref, first):
    dx = _dot(dz_ref[...], w_t_ref[...])
    hv = h_ref[...]
    rstd = lax.rsqrt(jnp.mean(hv * hv, axis=-1, keepdims=True) + EPS)
    xh = hv * rstd

    @pl.when(first)
    def _():
        dg_ref[...] = jnp.zeros_like(dg_ref)

    dg_ref[...] += _rowsum(dx * xh)
    dxh = dx * g_ref[...]
    dh_ref[...] = dres_ref[...] + rstd * (dxh - xh * jnp.mean(dxh * xh, axis=-1, keepdims=True))


def _cp_mid_fwd(x, meta, gain, w_in_t, w_out, conv_w, conv_b, ln_g, ln_b, pool_w, pool_scale, *, name):
    seq, d = x.shape
    t = seq + CHUNK
    ein = w_in_t.shape[0]
    cd = conv_b.shape[1]
    pd = pool_scale.shape[1]
    pg = pd // len(POOL_WINDOWS)
    rt = _row_tile(t, 320)
    ntile = t // rt

    def body(x_ref, meta_ref, g_ref, wi_ref, wo_ref, cw_ref, cb_ref, lg_ref, lb_ref, pw_ref, ps_ref,
             ho_ref, o_ref, z_ref, u_ref, cv_ref, h0_ref, gext, pext, conv_s, hbuf, hsem):
        i = pl.program_id(0)
        slot = i % 2
        first_rows = pltpu.make_async_copy(x_ref.at[pl.ds(0, rt - CHUNK)], hbuf.at[0, pl.ds(CHUNK, rt - CHUNK)], hsem.at[0])

        def tile_rows(tile, to):
            return pltpu.make_async_copy(x_ref.at[pl.ds(pl.multiple_of(tile * rt - CHUNK, CHUNK), rt)], hbuf.at[to],
                                         hsem.at[to])

        @pl.when(i == 0)
        def _():
            first_rows.start()
            hbuf[0, 0:PAD_ROWS, :] = jnp.zeros((PAD_ROWS, d), F32)
            hbuf[0, PAD_ROWS:CHUNK, :] = meta_ref[...]
            _to_lane_blocks(gext, 0, jnp.zeros((HALO, cd), F32))
            pext[0:HALO, :] = jnp.zeros((HALO, pd), F32)

        @pl.when(i + 1 < ntile)
        def _():
            tile_rows(i + 1, 1 - slot).start()

        @pl.when(i == 0)
        def _():
            first_rows.wait()

        @pl.when(i > 0)
        def _():
            tile_rows(i, slot).wait()

        h_ref = hbuf.at[slot]
        h0_ref[...] = h_ref[...]
        _norm_project(h_ref, g_ref, wi_ref, u_ref, z_ref)

        _to_lane_blocks(gext, HALO, z_ref[:, 0:cd] * _sigmoid(z_ref[:, cd:2 * cd]))
        pext[HALO:HALO + rt, :] = z_ref[:, 2 * cd:]
        _conv_rows(gext, cw_ref, lambda k: k + HALO - (CONV_WIDTH - 1), conv_s, rt // CHUNK, cd, cb_ref)
        cv = _from_lane_blocks(conv_s)
        cv_ref[...] = cv
        xc = cv - jnp.mean(cv, axis=-1, keepdims=True)
        y = xc * lax.rsqrt(jnp.mean(xc * xc, axis=-1, keepdims=True) + EPS) * lg_ref[...] + lb_ref[...]
        rows = _row_ids(i, rt)
        a = jnp.where(rows >= PAD_ROWS, y * _sigmoid(y), 0.0)
        o_ref[:, 0:cd] = a.astype(BF16)
        for gi, window in enumerate(POOL_WINDOWS):
            ls = slice(gi * pg, (gi + 1) * pg)
            v = pext[:, ls]
            tm = _trailing_sum(v, window)[HALO:] / _pool_counts(rows, window) - v[HALO:]
            p = _dot(tm.astype(BF16), pw_ref[gi]) * ps_ref[:, ls]
            o_ref[:, cd + gi * pg:cd + (gi + 1) * pg] = p.astype(BF16)
        ho_ref[...] = h_ref[...] + _dot(o_ref[...], wo_ref[...])
        gext[:, 0:HALO, :] = gext[:, rt:rt + HALO, :]
        pext[0:HALO, :] = pext[rt:rt + HALO, :]

    nl, lb = len(_lane_blocks(cd)), min(LANE, cd)
    rows = lambda i: (i, 0)
    return pl.pallas_call(
        body, grid=(ntile,),
        in_specs=[pl.BlockSpec(memory_space=pl.ANY), _resident(meta.shape), _resident((1, d)), _resident(w_in_t.shape),
                  _resident(w_out.shape), _resident(conv_w.shape), _resident((1, cd)),
                  _resident((1, cd)), _resident((1, cd)), _resident(pool_w.shape), _resident((1, pd))],
        out_specs=[pl.BlockSpec((rt, d), rows), pl.BlockSpec((rt, cd + pd), rows), pl.BlockSpec((rt, ein), rows),
                   pl.BlockSpec((rt, d), rows), pl.BlockSpec((rt, cd), rows), pl.BlockSpec((rt, d), rows)],
        out_shape=[_sds((t, d), F32), _sds((t, cd + pd), BF16), _sds((t, ein), F32), _sds((t, d), BF16),
                   _sds((t, cd), F32), _sds((t, d), F32)],
        scratch_shapes=[pltpu.VMEM((nl, rt + HALO, lb), F32), pltpu.VMEM((rt + HALO, pd), F32),
                        pltpu.VMEM((nl, rt, lb), F32), pltpu.VMEM((2, rt, d), F32), pltpu.SemaphoreType.DMA((2,))],
        compiler_params=_cparams("arbitrary"), name=name)(x, meta, gain, w_in_t, w_out, conv_w, conv_b, ln_g, ln_b, pool_w,
                                                          pool_scale)


def _cp_mid_bwd(z, cv, h, gain, w_in_t, dh, w_out, conv_w, conv_b, ln_g, ln_b, pool_w, pool_scale, *, after=None, name):
    t, ein = z.shape
    cd = conv_b.shape[1]
    pd = pool_scale.shape[1]
    pg = pd // len(POOL_WINDOWS)
    rt = _row_tile(t, 320)
    ntile = t // rt
    per = rt // CHUNK
    dep_specs, deps = _dep_specs(after)

    def body(*refs):
        (z_ref, zh_ref, cv_ref, h_ref, g_ref, wi_ref, dh_ref, wo_ref, cw_ref, cb_ref, lg_ref, lb_ref, pw_ref, ps_ref,
         dx_ref, dfirst_ref, dg_ref, dz_ref, dcw_ref, dcb_ref, dlg_ref, dlb_ref, dpw_ref, dps_ref,
         gext, pext, conv_s, dcv, dsp, dhi, dx_sem) = refs[len(deps):]
        step = pl.program_id(0)
        tile = ntile - 1 - step
        slot = step % 2
        last_slot = (ntile - 1) % 2
        first_rows = pltpu.make_async_copy(dhi.at[last_slot, pl.ds(CHUNK, rt - CHUNK)], dx_ref.at[pl.ds(0, rt - CHUNK)],
                                           dx_sem.at[last_slot])

        def tile_rows(tile, slot):
            return pltpu.make_async_copy(dhi.at[slot], dx_ref.at[pl.ds(pl.multiple_of(tile * rt - CHUNK, CHUNK), rt)],
                                         dx_sem.at[slot])

        dcat = _dot_nt(dh_ref[...].astype(BF16), wo_ref[...])

        @pl.when(step >= 2)
        def _():
            tile_rows(tile + 2, slot).wait()

        @pl.when(step == 0)
        def _():
            for ref in (dcw_ref, dcb_ref, dlg_ref, dlb_ref, dpw_ref, dps_ref):
                ref[...] = jnp.zeros_like(ref)
            _to_lane_blocks(dcv, rt, jnp.zeros((HALO, cd), F32))
            dsp[rt:rt + HALO, :] = jnp.zeros((HALO, pd), F32)

        keep = jnp.where(tile > 0, 1.0, 0.0)
        zh = zh_ref[CHUNK - HALO:CHUNK, :]
        _to_lane_blocks(gext, 0, keep * zh[:, 0:cd] * _sigmoid(zh[:, cd:2 * cd]))
        pext[0:HALO, :] = keep * zh[:, 2 * cd:]
        za = z_ref[:, 0:cd]
        sg = _sigmoid(z_ref[:, cd:2 * cd])
        _to_lane_blocks(gext, HALO, za * sg)
        pext[HALO:HALO + rt, :] = z_ref[:, 2 * cd:]
        cv = cv_ref[...]
        xc = cv - jnp.mean(cv, axis=-1, keepdims=True)
        rstd = lax.rsqrt(jnp.mean(xc * xc, axis=-1, keepdims=True) + EPS)
        xh = xc * rstd
        y = xh * lg_ref[...] + lb_ref[...]
        sy = _sigmoid(y)
        rows = _row_ids(tile, rt)
        da = jnp.where(rows >= PAD_ROWS, dcat[:, 0:cd], 0.0)
        dy = da * (sy * (1.0 + y * (1.0 - sy)))
        dlg_ref[...] += _rowsum(dy * xh)
        dlb_ref[...] += _rowsum(dy)
        dxh = dy * lg_ref[...]
        dconv = rstd * (dxh - jnp.mean(dxh, axis=-1, keepdims=True) - xh * jnp.mean(dxh * xh, axis=-1, keepdims=True))
        dcb_ref[...] += _rowsum(dconv)
        _to_lane_blocks(dcv, 0, dconv)
        for l, ls in enumerate(_lane_blocks(cd)):
            def acc_rows(rb, accs, l=l):
                base = pl.multiple_of(rb * CHUNK, CHUNK)
                d_blk = dcv[l, pl.ds(base, CHUNK), :]
                out = []
                for k in range(CONV_WIDTH):
                    prod = d_blk * gext[l, pl.ds(base + k + HALO - (CONV_WIDTH - 1), CHUNK), :]
                    part = prod[0:8]
                    for s in range(8, CHUNK, 8):
                        part = part + prod[s:s + 8]
                    out.append(accs[k] + part)
                return tuple(out)

            zero = jnp.zeros((8, ls.stop - ls.start), F32)
            accs = lax.fori_loop(0, per, acc_rows, (zero,) * CONV_WIDTH)
            for k in range(CONV_WIDTH):
                dcw_ref[k:k + 1, ls] += _rowsum(accs[k])
        _conv_rows(dcv, cw_ref, lambda k: CONV_WIDTH - 1 - k, conv_s, per, cd)
        dglu = _from_lane_blocks(conv_s)
        dz_ref[:, 0:cd] = (dglu * sg).astype(BF16)
        dz_ref[:, cd:2 * cd] = (dglu * za * sg * (1.0 - sg)).astype(BF16)
        dcv[:, rt:rt + HALO, :] = dcv[:, 0:HALO, :]
        for gi, window in enumerate(POOL_WINDOWS):
            ls = slice(gi * pg, (gi + 1) * pg)
            v = pext[:, ls]
            cnt = _pool_counts(rows, window)
            tm = (_trailing_sum(v, window)[HALO:] / cnt - v[HALO:]).astype(BF16)
            dp = dcat[:, cd + gi * pg:cd + (gi + 1) * pg]
            dps_ref[:, ls] += _rowsum(dp * _dot(tm, pw_ref[gi]))
            dpl = (dp * ps_ref[:, ls]).astype(BF16)
            dpw_ref[gi] += _dot_tn(tm, dpl)
            dtm = _dot_nt(dpl, pw_ref[gi])
            dsp[0:rt, ls] = dtm / cnt
            dpin = _leading_sum(dsp[:, ls], window)[0:rt] - dtm
            dz_ref[:, 2 * cd + gi * pg:2 * cd + (gi + 1) * pg] = dpin.astype(BF16)
        dsp[rt:rt + HALO, :] = dsp[0:HALO, :]

        _project_back(dz_ref, wi_ref, h_ref, g_ref, dh_ref, dhi.at[slot], dg_ref, step == 0)

        @pl.when(tile > 0)
        def _():
            tile_rows(tile, slot).start()

        @pl.when(tile == 0)
        def _():
            first_rows.start()
            dfirst_ref[...] = dhi[last_slot, 0:CHUNK, :]
            if ntile > 1:
                tile_rows(1, 1 - last_slot).wait()
            first_rows.wait()

    d = h.shape[1]
    back = lambda i: (ntile - 1 - i, 0)
    halo_idx = lambda i: (jnp.maximum((ntile - 1 - i) * per - 1, 0), 0)
    const2 = lambda i: (0, 0)
    nl, lb = len(_lane_blocks(cd)), min(LANE, cd)
    return pl.pallas_call(
        body, grid=(ntile,),
        in_specs=dep_specs + [
                  pl.BlockSpec((rt, ein), back), pl.BlockSpec((CHUNK, ein), halo_idx), pl.BlockSpec((rt, cd), back),
                  pl.BlockSpec((rt, d), back),
                  _resident((1, d)), _resident(w_in_t.shape), pl.BlockSpec((rt, d), back), _resident(w_out.shape),
                  _resident(conv_w.shape), _resident((1, cd)), _resident((1, cd)), _resident((1, cd)),
                  _resident(pool_w.shape), _resident((1, pd))],
        out_specs=[pl.BlockSpec(memory_space=pl.ANY), pl.BlockSpec((CHUNK, d), const2), pl.BlockSpec((1, d), const2),
                   pl.BlockSpec((rt, ein), back), pl.BlockSpec(conv_w.shape, const2), pl.BlockSpec((1, cd), const2),
                   pl.BlockSpec((1, cd), const2), pl.BlockSpec((1, cd), const2),
                   pl.BlockSpec(pool_w.shape, lambda i: (0, 0, 0)), pl.BlockSpec((1, pd), const2)],
        out_shape=[_sds((t - CHUNK, d), F32), _sds((CHUNK, d), F32), _sds((1, d), F32),
                   _sds((t, ein), BF16), _sds(conv_w.shape, F32), _sds((1, cd), F32), _sds((1, cd), F32),
                   _sds((1, cd), F32), _sds(pool_w.shape, F32), _sds((1, pd), F32)],
        scratch_shapes=[pltpu.VMEM((nl, rt + HALO, lb), F32), pltpu.VMEM((rt + HALO, pd), F32), pltpu.VMEM((nl, rt, lb), F32),
                        pltpu.VMEM((nl, rt + HALO, lb), F32), pltpu.VMEM((rt + HALO, pd), F32), pltpu.VMEM((2, rt, d), F32),
                        pltpu.SemaphoreType.DMA((2,))],
        compiler_params=_cparams("arbitrary"), name=name)(*deps, z, z, cv, h, gain, w_in_t, dh, w_out, conv_w, conv_b, ln_g,
                                                          ln_b, pool_w, pool_scale)


def _log_decay(r, gw_ref, gb_ref, rows):
    gp = _dot(r.astype(BF16), gw_ref[...]) + gb_ref[...]
    log_sig = jnp.minimum(gp, 0.0) - jnp.log(1.0 + jnp.exp(-jnp.abs(gp)))
    return gp, jnp.where(rows >= PAD_ROWS, log_sig / GATE_NORM, 0.0)


def _tri(strict):
    r = lax.broadcasted_iota(jnp.int32, (CHUNK, CHUNK), 0)
    c = lax.broadcasted_iota(jnp.int32, (CHUNK, CHUNK), 1)
    return jnp.where(c < r if strict else c <= r, 1.0, 0.0).astype(BF16)


def _tri_dot(tri, a):
    hi = a.astype(BF16)
    rest = a - hi.astype(F32)
    mid = rest.astype(BF16)
    lo = (rest - mid.astype(F32)).astype(BF16)
    return _dot(tri, hi) + _dot(tri, mid) + _dot(tri, lo)


def _gla_mid_fwd(h, gain, w_in_blocks, w_out, gate_w, gate_b, head_g, *, name):
    t = h.shape[0]
    nblk, rpb = w_in_blocks.shape[:2]
    dk = gate_b.shape[1]
    hv = head_g.shape[1]
    hk = dk // HEADS
    dv = hv * HEADS
    r_at = 2 * dk + 2 * dv
    assert nblk * rpb == r_at + GATE_RANK
    zw = r_at + GATE_PAD
    rt = _row_tile(t, 320)
    per = rt // CHUNK
    scale = hk ** -0.5

    def body(h_ref, g_ref, wb_ref, wo_ref, gw_ref, gb_ref, hg_ref, ho_ref, o_ref, st_ref, z_ref, u_ref, wi_ref,
             s_ref, la_ref, dec_ref):
        i = pl.program_id(0)

        @pl.when(i == 0)
        def _():
            s_ref[...] = jnp.zeros_like(s_ref)
            for b in range(nblk):
                wi_ref[b * rpb:(b + 1) * rpb, :] = wb_ref[b]
            wi_ref[nblk * rpb:, :] = jnp.zeros((zw - nblk * rpb, wi_ref.shape[1]), BF16)

        _norm_project(h_ref, g_ref, wi_ref, u_ref, z_ref)

        _, la = _log_decay(z_ref[:, r_at:r_at + GATE_PAD], gw_ref, gb_ref, _row_ids(i, rt))
        la_ref[...] = la
        tri = _tri(False)

        def chunk_rows(c):
            return slice(c * CHUNK, (c + 1) * CHUNK)

        def decays(c, carry):
            rows = chunk_rows(c)
            la_c = la_ref[rows, :]
            cum = _tri_dot(tri, la_c)
            dec_ref[rows, :] = jnp.exp(_rowsum(la_c) - cum)
            return carry

        def states(c, carry):
            rows = chunk_rows(c)
            etot = jnp.exp(_rowsum(la_ref[rows, :]))
            for hd in range(HEADS):
                ks = slice(hd * hk, (hd + 1) * hk)
                kd = z_ref[rows, dk + hd * hk:dk + (hd + 1) * hk] * dec_ref[rows, ks]
                v = z_ref[rows, 2 * dk + hd * hv:2 * dk + (hd + 1) * hv]
                s_new = s_ref[hd] * etot[:, ks] + _dot_tn(v.astype(BF16), kd.astype(BF16))
                s_ref[hd] = s_new
                st_ref[c, hd] = s_new
            return carry

        def outputs(c, carry):
            rows = chunk_rows(c)
            for hd in range(HEADS):
                q = z_ref[rows, hd * hk:(hd + 1) * hk] * scale
                g = z_ref[rows, 2 * dk + dv + hd * hv:2 * dk + dv + (hd + 1) * hv]
                o = _dot_nt(q.astype(BF16), st_ref[c, hd].astype(BF16))
                on = o * lax.rsqrt(jnp.mean(o * o, axis=-1, keepdims=True) + EPS) * hg_ref[...]
                o_ref[rows, hd * hv:(hd + 1) * hv] = (on * (g * _sigmoid(g))).astype(BF16)
            return carry

        for phase in (decays, states, outputs):
            for c in range(per):
                phase(c, 0)
        ho_ref[...] = h_ref[...] + _dot(o_ref[...], wo_ref[...])

    d = h.shape[1]
    rows = lambda i: (i, 0)
    return pl.pallas_call(
        body, grid=(t // rt,),
        in_specs=[pl.BlockSpec((rt, d), rows), _resident((1, d)), _resident(w_in_blocks.shape), _resident(w_out.shape),
                  _resident(gate_w.shape), _resident((1, dk)), _resident((1, hv))],
        out_specs=[pl.BlockSpec((rt, d), rows), pl.BlockSpec((rt, dv), rows),
                   pl.BlockSpec((per, HEADS, hv, hk), lambda i: (i, 0, 0, 0)), pl.BlockSpec((rt, zw), rows),
                   pl.BlockSpec((rt, d), rows), pl.BlockSpec((zw, d), lambda i: (0, 0))],
        out_shape=[_sds((t, d), F32), _sds((t, dv), BF16), _sds((t // CHUNK, HEADS, hv, hk), F32), _sds((t, zw), F32),
                   _sds((t, d), BF16), _sds((zw, d), BF16)],
        scratch_shapes=[pltpu.VMEM((HEADS, hv, hk), F32), pltpu.VMEM((rt, dk), F32), pltpu.VMEM((rt, dk), F32)],
        compiler_params=_cparams("arbitrary"), name=name)(h, gain, w_in_blocks, w_out, gate_w, gate_b, head_g)


def _gla_mid_bwd(z, h, gain, w_in_t, dh, w_out, states, gate_w, gate_b, head_g, *, after=None, name):
    t = z.shape[0]
    dk = gate_b.shape[1]
    hv = head_g.shape[1]
    hk = dk // HEADS
    dv = hv * HEADS
    r_at = 2 * dk + 2 * dv
    rt = _row_tile(t, 320)
    ntile = t // rt
    per = rt // CHUNK
    scale = hk ** -0.5
    dep_specs, deps = _dep_specs(after)

    def body(*refs):
        (z_ref, h_ref, g_ref, wi_ref, dh_ref, wo_ref, st_ref, stp_ref, gw_ref, gb_ref, hg_ref,
         dhi_ref, dg_ref, dz_ref, dgw_ref, dgb_ref, dhg_ref,
         ds_ref, la_ref, dla_ref, dec_ref, dos_ref, e_ref, do_ref) = refs[len(deps):]
        step = pl.program_id(0)
        tile = ntile - 1 - step
        do_ref[...] = _dot_nt(dh_ref[...].astype(BF16), wo_ref[...])

        @pl.when(step == 0)
        def _():
            ds_ref[...] = jnp.zeros_like(ds_ref)
            dgw_ref[...] = jnp.zeros_like(dgw_ref)
            dgb_ref[...] = jnp.zeros_like(dgb_ref)
            dhg_ref[...] = jnp.zeros_like(dhg_ref)

        rows_id = _row_ids(tile, rt)
        r = z_ref[:, r_at:r_at + GATE_PAD]
        gp, la = _log_decay(r, gw_ref, gb_ref, rows_id)
        la_ref[...] = la
        tri, tri_strict = _tri(False), _tri(True)
        keep = jnp.where(tile > 0, 1.0, 0.0)

        def chunk_rows(c):
            return slice(c * CHUNK, (c + 1) * CHUNK)

        def recompute(c, dhg):
            rows = chunk_rows(c)
            la_c = la_ref[rows, :]
            cum = _tri_dot(tri, la_c)
            dec_ref[rows, :] = jnp.exp(_rowsum(la_c) - cum)
            for hd in range(HEADS):
                q = (z_ref[rows, hd * hk:(hd + 1) * hk] * scale).astype(BF16)
                g = z_ref[rows, 2 * dk + dv + hd * hv:2 * dk + dv + (hd + 1) * hv]
                s_b = st_ref[c, hd].astype(BF16)
                o = _dot_nt(q, s_b)
                rstd = lax.rsqrt(jnp.mean(o * o, axis=-1, keepdims=True) + EPS)
                oh = o * rstd
                sg = _sigmoid(g)
                d_og = do_ref[rows, hd * hv:(hd + 1) * hv]
                dz_ref[rows, 2 * dk + dv + hd * hv:2 * dk + dv + (hd + 1) * hv] = (
                    d_og * oh * hg_ref[...] * (sg * (1.0 + g * (1.0 - sg)))).astype(BF16)
                don = d_og * (g * sg)
                dhg = dhg + _rowsum(don * oh)
                doh = don * hg_ref[...]
                d_o = (rstd * (doh - oh * jnp.mean(doh * oh, axis=-1, keepdims=True))).astype(BF16)
                dos_ref[rows, hd * hv:(hd + 1) * hv] = d_o
                dz_ref[rows, hd * hk:(hd + 1) * hk] = (_dot(d_o, s_b) * scale).astype(BF16)
            return dhg

        def recurrence(cc, carry):
            c = per - 1 - cc
            rows = chunk_rows(c)
            etot = jnp.exp(_rowsum(la_ref[rows, :]))
            for hd in range(HEADS):
                ks = slice(hd * hk, (hd + 1) * hk)
                q = (z_ref[rows, hd * hk:(hd + 1) * hk] * scale).astype(BF16)
                dec = dec_ref[rows, ks]
                kd = z_ref[rows, dk + hd * hk:dk + (hd + 1) * hk] * dec
                v = z_ref[rows, 2 * dk + hd * hv:2 * dk + (hd + 1) * hv].astype(BF16)
                s_prev = st_ref[c - 1, hd] if c > 0 else keep * stp_ref[0, hd]
                ds_t = ds_ref[hd] + _dot_tn(dos_ref[rows, hd * hv:(hd + 1) * hv], q)
                ds_b = ds_t.astype(BF16)
                dkd = _dot(v, ds_b)
                dz_ref[rows, 2 * dk + hd * hv:2 * dk + (hd + 1) * hv] = _dot_nt(kd.astype(BF16), ds_b).astype(BF16)
                dtot = etot[:, ks] * _rowsum(ds_t * s_prev)
                ds_ref[hd] = ds_t * etot[:, ks]
                dz_ref[rows, dk + hd * hk:dk + (hd + 1) * hk] = (dkd * dec).astype(BF16)
                e_ref[rows, ks] = dkd * kd
                dla_ref[rows, ks] = jnp.broadcast_to(dtot, (CHUNK, hk))
            return carry

        def decay_cotangent(c, carry):
            rows = chunk_rows(c)
            dla_ref[rows, :] += _tri_dot(tri_strict, e_ref[rows, :])
            return carry

        dhg = jnp.zeros((1, hv), F32)
        for c in range(per):
            dhg = recompute(c, dhg)
        dhg_ref[...] += dhg
        for phase in (recurrence, decay_cotangent):
            for c in range(per):
                phase(c, 0)
        dla = jnp.where(rows_id >= PAD_ROWS, dla_ref[...], 0.0)
        dgp = dla * (1.0 / GATE_NORM) * (1.0 - _sigmoid(gp))
        dgb_ref[...] += _rowsum(dgp)
        dgp_b = dgp.astype(BF16)
        dgw_ref[...] += _dot_tn(r.astype(BF16), dgp_b)
        dz_ref[:, r_at:r_at + GATE_PAD] = _dot_nt(dgp_b, gw_ref[...]).astype(BF16)
        _project_back(dz_ref, wi_ref, h_ref, g_ref, dh_ref, dhi_ref, dg_ref, step == 0)

    d = h.shape[1]
    back = lambda i: (ntile - 1 - i, 0)
    const2 = lambda i: (0, 0)
    return pl.pallas_call(
        body, grid=(ntile,),
        in_specs=dep_specs + [
                  pl.BlockSpec((rt, z.shape[1]), back), pl.BlockSpec((rt, d), back), _resident((1, d)),
                  _resident(w_in_t.shape), pl.BlockSpec((rt, d), back), _resident(w_out.shape),
                  pl.BlockSpec((per, HEADS, hv, hk), lambda i: (ntile - 1 - i, 0, 0, 0)),
                  pl.BlockSpec((1, HEADS, hv, hk), lambda i: (jnp.maximum((ntile - 1 - i) * per - 1, 0), 0, 0, 0)),
                  _resident(gate_w.shape), _resident((1, dk)), _resident((1, hv))],
        out_specs=[pl.BlockSpec((rt, d), back), pl.BlockSpec((1, d), const2), pl.BlockSpec((rt, z.shape[1]), back),
                   pl.BlockSpec(gate_w.shape, const2), pl.BlockSpec((1, dk), const2), pl.BlockSpec((1, hv), const2)],
        out_shape=[_sds((t, d), F32), _sds((1, d), F32), _sds(z.shape, BF16), _sds(gate_w.shape, F32),
                   _sds((1, dk), F32), _sds((1, hv), F32)],
        scratch_shapes=[pltpu.VMEM((HEADS, hv, hk), F32), pltpu.VMEM((rt, dk), F32), pltpu.VMEM((rt, dk), F32),
                        pltpu.VMEM((rt, dk), F32), pltpu.VMEM((rt, dv), BF16), pltpu.VMEM((rt, dk), F32),
                        pltpu.VMEM((rt, dv), F32)],
        compiler_params=_cparams("arbitrary"), name=name)(*deps, z, h, gain, w_in_t, dh, w_out, states, states, gate_w,
                                                          gate_b, head_g)


def _adamw_math(w, g, m, v):
    m = ADAM_B1 * m + (1.0 - ADAM_B1) * g
    v = ADAM_B2 * v + (1.0 - ADAM_B2) * (g * g)
    m_hat = m / (1.0 - ADAM_B1 ** ADAM_STEP)
    v_hat = v / (1.0 - ADAM_B2 ** ADAM_STEP)
    return -ADAM_LR * (m_hat / (jnp.sqrt(v_hat) + ADAM_EPS) + ADAM_WD * w), m, v


N_CHIP = N_DEV // 2
BLOCK_ELEMS = 128 * 1024


def _my_slot():
    return 4 * lax.axis_index("x") + 2 * lax.axis_index("y") + lax.axis_index("c")


def _row_block(r, c):
    cap = max(8, BLOCK_ELEMS // (-(-c // LANE) * LANE))
    return max([b for b in range(8, r + 1, 8) if r % b == 0 and b <= cap] or [r])


def _blocks(r, c):
    rb = _row_block(r, c)
    if rb < r or r * c <= BLOCK_ELEMS:
        return rb, c
    return r, max([b for b in (512, 256, LANE) if c % b == 0 and r * b <= BLOCK_ELEMS] or [c])


def _reduce_adam(parts, w, m, v, *, by_row=False, after=None, name):
    nl, r, c = (1, w.shape[0], w.shape[2]) if by_row else w.shape
    rb, cb = _blocks(r, c)
    dep_specs, deps = _dep_specs(after)
    whole = (slice(None), 0, slice(None)) if by_row else Ellipsis

    def body(*refs):
        me = refs[0][0]
        refs = refs[1 + len(deps):]
        p_refs = refs[:2 * nl]
        w_ref, m_ref, v_ref, g_out, d_out, m_out, v_out = refs[2 * nl:]
        layer = pl.program_id(0)
        for li in range(nl):
            @pl.when(layer == li)
            def _(li=li):
                own_ref, land_ref = p_refs[2 * li], p_refs[2 * li + 1]
                mine = own_ref[...].astype(F32)
                g = None
                for dev in range(N_DEV):
                    term = jnp.where(me == dev, mine, land_ref[dev].astype(F32))
                    g = term if g is None else g + term
                g_out[whole] = g
                d_out[whole], m_out[whole], v_out[whole] = _adamw_math(w_ref[whole], g, m_ref[whole], v_ref[whole])

    if by_row:
        blk = pl.BlockSpec((rb, 1, cb), lambda l, i, j, me: (i, 0, j))
    else:
        blk = pl.BlockSpec((None, rb, cb), lambda l, i, j, me: (l, i, j))
    p_specs = []
    for li in range(nl):
        p_specs += [
            pl.BlockSpec((None, rb, cb), lambda l, i, j, me, li=li: (me[0], jnp.where(l == li, i, 0), jnp.where(l == li, j, 0))),
            pl.BlockSpec((N_DEV, rb, cb), lambda l, i, j, me, li=li: (0, jnp.where(l == li, i, 0), jnp.where(l == li, j, 0)))]
    flat = [p for pair in parts for p in pair]
    grid_spec = pltpu.PrefetchScalarGridSpec(
        num_scalar_prefetch=1, grid=(nl, r // rb, c // cb), in_specs=dep_specs + p_specs + [blk, blk, blk],
        out_specs=[blk] * 4)
    return pl.pallas_call(
        body, grid_spec=grid_spec, out_shape=[_sds(w.shape, F32)] * 4,
        compiler_params=_cparams("arbitrary", "arbitrary", "arbitrary"), name=name)(
        _my_slot().reshape(1), *deps, *flat, w, m, v)


def _sum8(own, landed, *, name):
    def body(own_ref, land_ref, o_ref):
        me = _my_slot()
        total = None
        for dev in range(N_DEV):
            term = jnp.where(me == dev, own_ref[...], land_ref[dev])
            total = term if total is None else total + term
        o_ref[...] = total

    return pl.pallas_call(body, out_shape=_sds(own.shape, F32), name=name)(own, landed)


def _adam_small(own, landed, split, w, m, v, *, name):
    n = len(w)

    def body(*refs):
        own_refs, land_refs, w_refs, m_refs, v_refs = (refs[k * n:(k + 1) * n] for k in range(5))
        outs = refs[5 * n:]
        me = _my_slot()
        for k in range(n):
            mine = own_refs[k][me] if split[k] else own_refs[k][...]
            g = None
            for dev in range(N_DEV):
                term = jnp.where(me == dev, mine, land_refs[k][dev])
                g = term if g is None else g + term
            outs[4 * k][...] = g
            outs[4 * k + 1][...], outs[4 * k + 2][...], outs[4 * k + 3][...] = _adamw_math(
                w_refs[k][...], g, m_refs[k][...], v_refs[k][...])

    out = pl.pallas_call(body, out_shape=[_sds(a.shape, F32) for a in w for _ in range(4)],
                         compiler_params=pltpu.CompilerParams(vmem_limit_bytes=V7X_VMEM_LIMIT), name=name)(
        *own, *landed, *w, *m, *v)
    return [tuple(out[4 * k:4 * k + 4]) for k in range(n)]


_HBM = pl.BlockSpec(memory_space=pltpu.HBM)
_SEM = pl.BlockSpec(memory_space=pltpu.SEMAPHORE)
_DATAFLOW = pltpu.SideEffectType.DATAFLOW_SIDE_EFFECTING


def _plan_to_all(src, land):
    x, y, c = lax.axis_index("x"), lax.axis_index("y"), lax.axis_index("c")
    return [(src, land.at[_my_slot()], (x ^ ((d >> 2) & 1), y ^ ((d >> 1) & 1), c ^ (d & 1))) for d in range(1, N_DEV)]


def _plan_split_to_all(src, land):
    x, y, c = lax.axis_index("x"), lax.axis_index("y"), lax.axis_index("c")
    peers = [(x ^ ((d >> 2) & 1), y ^ ((d >> 1) & 1), c ^ (d & 1)) for d in range(1, N_DEV)]
    return [(src.at[4 * px + 2 * py + pc], land.at[_my_slot()], (px, py, pc)) for px, py, pc in peers]


_PLAN_COPIES = {_plan_to_all: N_DEV - 1, _plan_split_to_all: N_DEV - 1}


def _plans(plan, n):
    return list(plan) if isinstance(plan, (list, tuple)) else [plan] * n


def _exchange_copies(plan, ins, lands, send, recv):
    copies, sem = [], 0
    for p, src, land in zip(_plans(plan, len(lands)), ins, lands):
        for s, dst, dev in p(src, land):
            copies.append(pltpu.make_async_remote_copy(
                src_ref=s, dst_ref=dst, send_sem=send.at[sem], recv_sem=recv.at[sem],
                device_id=dev, device_id_type=pl.DeviceIdType.MESH))
            sem += 1
    return copies


def _place_own(srcs, *, after=None, name):
    dep_specs, deps = _dep_specs(after)
    n = len(srcs)
    arrays, in_specs, shapes = [], [], []
    for a, dtype in srcs:
        if isinstance(a, tuple):
            a, layer = a
            in_specs.append(pl.BlockSpec((None,) + a.shape[1:], lambda i, layer=layer: (layer, 0, 0)))
            shapes.append(a.shape[1:])
        else:
            in_specs.append(pl.BlockSpec(a.shape, lambda i: (0, 0)))
            shapes.append(a.shape)
        arrays.append(a)
    dtypes = [dtype for _, dtype in srcs]

    def body(*refs):
        refs = refs[len(deps):]
        a_refs, o_refs, cast_refs, sem = refs[:n], refs[n:2 * n], refs[2 * n:3 * n], refs[3 * n]
        me = _my_slot()
        copies = []
        for k in range(n):
            cast_refs[k][...] = a_refs[k][...].astype(dtypes[k])
            copies.append(pltpu.make_async_copy(cast_refs[k], o_refs[k].at[me], sem.at[k]))
            copies[-1].start()
        for cp in copies:
            cp.wait()

    return pl.pallas_call(
        body, grid=(1,), in_specs=dep_specs + in_specs, out_specs=[pl.BlockSpec(memory_space=pl.ANY)] * n,
        out_shape=[_sds((N_DEV,) + shape, dtype) for shape, dtype in zip(shapes, dtypes)],
        scratch_shapes=[pltpu.VMEM(shape, dtype) for shape, dtype in zip(shapes, dtypes)] + [pltpu.SemaphoreType.DMA((n,))],
        compiler_params=pltpu.CompilerParams(vmem_limit_bytes=V7X_VMEM_LIMIT), name=name)(*deps, *arrays)


def _plan_gather_first(land, _):
    x, y, c = lax.axis_index("x"), lax.axis_index("y"), lax.axis_index("c")
    mine = land.at[_my_slot()]
    return [(mine, mine, (x, y, 1 - c))] + [(mine, mine, (x ^ (d >> 1), y ^ (d & 1), c)) for d in range(1, N_CHIP)]


def _plan_gather_relay(land, _):
    x, y, c = lax.axis_index("x"), lax.axis_index("y"), lax.axis_index("c")
    slots = [land.at[4 * (x ^ (d >> 1)) + 2 * (y ^ (d & 1)) + c] for d in range(1, N_CHIP)]
    return [(s, s, (x, y, 1 - c)) for s in slots]


def _plan_gather_direct(land, _):
    return _plan_to_all(land.at[_my_slot()], land)


_PLAN_COPIES[_plan_gather_first] = N_CHIP
_PLAN_COPIES[_plan_gather_relay] = N_CHIP - 1
_PLAN_COPIES[_plan_gather_direct] = N_DEV - 1


def _exchange_start(plan, arrs, lands, *, after=None, name):
    bufs = list(lands) if arrs is None else list(arrs) + list(lands)
    n, nb = len(lands), len(bufs)
    nsem = sum(_PLAN_COPIES[p] for p in _plans(plan, n))
    dep_specs, deps = _dep_specs(after)

    def body(*refs):
        ins, land_refs = refs[:n], refs[nb - n:nb]
        send, recv = refs[nb + len(deps)], refs[nb + len(deps) + 1]
        for cp in _exchange_copies(plan, ins, land_refs, send, recv):
            cp.start()
        refs[-1][...] = jnp.zeros_like(refs[-1])

    out = pl.pallas_call(
        body, name=name,
        out_shape=(pltpu.SemaphoreType.DMA((nsem,)), pltpu.SemaphoreType.DMA((nsem,)),
                   *[pltpu.HBM(a.shape, a.dtype) for a in bufs], _sds((8, LANE), F32)),
        in_specs=[_HBM] * nb + dep_specs,
        out_specs=(_SEM, _SEM, *([_HBM] * nb), pl.BlockSpec(memory_space=pltpu.VMEM)),
        input_output_aliases={i: 2 + i for i in range(nb)},
        compiler_params=pltpu.CompilerParams(has_side_effects=_DATAFLOW),
    )(*[pltpu.with_memory_space_constraint(a, pltpu.HBM) for a in bufs], *deps)
    return (plan, n, out[0], out[1], list(out[2:2 + nb])), out[-1]


def _exchange_now(plan, lands, *, name):
    n = len(lands)
    nsem = sum(_PLAN_COPIES[p] for p in _plans(plan, n))

    def body(*refs):
        land_refs, send, recv = refs[n:2 * n], refs[2 * n], refs[2 * n + 1]
        copies = _exchange_copies(plan, land_refs, land_refs, send, recv)
        for cp in copies:
            cp.start()
        for cp in copies:
            cp.wait_send()
            cp.wait_recv()

    hbm = pl.BlockSpec(memory_space=pl.ANY)
    return pl.pallas_call(
        body, in_specs=[hbm] * n, out_specs=[hbm] * n, out_shape=[_sds(a.shape, a.dtype) for a in lands],
        input_output_aliases={i: i for i in range(n)},
        scratch_shapes=[pltpu.SemaphoreType.DMA((nsem,)), pltpu.SemaphoreType.DMA((nsem,))], name=name)(*lands)


def _exchange_wait(state, after, *, name):
    plan, n, send_sem, recv_sem, bufs = state
    nb = len(bufs)
    after = list(after) if isinstance(after, (list, tuple)) else [after]

    def body(*refs):
        ins, land_refs, send, recv = refs[:n], refs[nb - n:nb], refs[nb], refs[nb + 1]
        for cp in _exchange_copies(plan, ins, land_refs, send, recv):
            cp.wait_send()
            cp.wait_recv()

    out = pl.pallas_call(
        body, name=name, out_shape=[pltpu.HBM(a.shape, a.dtype) for a in bufs],
        in_specs=[_HBM] * nb + [_SEM, _SEM] + [pl.BlockSpec(memory_space=pl.ANY)] * len(after), out_specs=[_HBM] * nb,
        input_output_aliases={i: i for i in range(nb)},
        compiler_params=pltpu.CompilerParams(has_side_effects=_DATAFLOW),
    )(*bufs, send_sem, recv_sem, *after)
    return list(out[:n]), list(out[nb - n:])


def _dep_specs(after):
    return ([], []) if after is None else ([pl.BlockSpec(memory_space=pl.ANY)], [after])


def _undo_column_split(g):
    return jnp.transpose(g, (1, 0, 2)).reshape(g.shape[1], N_DEV * g.shape[2])


def _column_split(a):
    r, c = a.shape
    return jnp.transpose(a.reshape(r, N_DEV, c // N_DEV), (1, 0, 2))


class _WholeWeights:
    def __init__(self, groups):
        self.groups = groups
        self.grads = {}

    def fetch(self, group, after):
        return self.groups[group]

    def emit(self, group, grads):
        self.grads.update(grads)
        return None


def _local_step(x, target, replicated, src):
    d = x.shape[1]
    mix_g, ffn_g = replicated["mix_g"], replicated["ffn_g"]
    cp = src.fetch("cp", [])
    cp_mid = (cp["conv_w"], replicated["conv_b"], replicated["ln_g"], replicated["ln_b"], replicated["pool_w"],
              replicated["pool_scale"])

    h1, cat, z0, u0, cv0, h0 = _cp_mid_fwd(x, cp["meta"], mix_g[0:1], cp["cp_w_in_t"], cp["cp_w_out"], *cp_mid,
                                           name="cp_mixer")
    ffn0 = src.fetch("ffn0", h1)
    h2, uf0, rf0 = _ffn_fwd(h1, ffn_g[0:1], ffn0["w1"], ffn0["w2"], name="ffn0")
    gla = src.fetch("gla", h2)
    gla_mid = (gla["gate_w"], gla["gate_b"], gla["head_g"])
    h3, og, states, z1, u1, gla_w_in_t = _gla_mid_fwd(h2, mix_g[1:2], gla["gla_w_in_t"], gla["gla_w_out"], *gla_mid,
                                                      name="gla_mixer")
    ffn1 = src.fetch("ffn1", h3)
    dh4, uf1, rf1, loss, d_final_g = _ffn_fwd(h3, ffn_g[1:2], ffn1["w1"], ffn1["w2"],
                                              loss_head=(replicated["final_g"], target), name="ffn1_loss")

    dh3, dhh1, dob1, dffn_g1 = _ffn_bwd_x(h3, dh4, ffn_g[1:2], rf1, ffn1["w1"], ffn1["w2"], name="ffn1_bwd_x")
    sent = src.emit("ffn1_w1", dict(w1=_linear_bwd_w(uf1, dhh1, column_blocks=N_DEV, name="ffn1_dw1")))
    sent = src.emit("ffn1_w2", dict(w2=_linear_bwd_w(rf1, dob1, square_x=True, after=sent, name="ffn1_dw2")))
    d_gla_w_out = _linear_bwd_w(og, dh3, name="gla_out_dw")
    dh2, dmix_g1, dz1, d_gate_w, d_gate_b, d_head_g = _gla_mid_bwd(
        z1, h2, mix_g[1:2], gla_w_in_t, dh3, gla["gla_w_out"], states, *gla_mid, after=sent, name="gla_mixer_bwd")
    d_gla_w_in_t = _linear_bwd_w(dz1, u1, row_blocks=gla["gla_w_in_t"].shape[:2], name="gla_in_dw")
    sent = src.emit("gla", dict(gla_w_in_t=d_gla_w_in_t, gla_w_out=d_gla_w_out))
    dh1, dhh0, dob0, dffn_g0 = _ffn_bwd_x(h1, dh2, ffn_g[0:1], rf0, ffn0["w1"], ffn0["w2"], after=sent, name="ffn0_bwd_x")
    sent = src.emit("ffn0_w1", dict(w1=_linear_bwd_w(uf0, dhh0, column_blocks=N_DEV, name="ffn0_dw1")))
    sent = src.emit("ffn0_w2", dict(w2=_linear_bwd_w(rf0, dob0, square_x=True, after=sent, name="ffn0_dw2")))
    d_cp_w_out = _linear_bwd_w(cat, dh1, after=sent, name="cp_out_dw")
    sent = src.emit("cp_out", dict(cp_w_out=d_cp_w_out))
    dx, dh0_first, dmix_g0, dz0, d_conv_w, d_conv_b, d_ln_g, d_ln_b, d_pool_w, d_pool_scale = _cp_mid_bwd(
        z0, cv0, h0, mix_g[0:1], cp["cp_w_in_t"], dh1, cp["cp_w_out"], *cp_mid, after=sent, name="cp_mixer_bwd")
    d_cp_w_in_t = _linear_bwd_w(dz0, u0, name="cp_in_dw")

    small = dict(
        mix_g=jnp.concatenate([dmix_g0, dmix_g1]), ffn_g=jnp.concatenate([dffn_g0, dffn_g1]), conv_b=d_conv_b, ln_g=d_ln_g,
        ln_b=d_ln_b, pool_w=d_pool_w, pool_scale=d_pool_scale, final_g=d_final_g, meta=dh0_first[PAD_ROWS:], conv_w=d_conv_w,
        gate_w=d_gate_w, gate_b=d_gate_b, head_g=d_head_g)
    src.emit("cp", dict(cp_w_in_t=d_cp_w_in_t, small=small, loss=loss))
    return loss, dx, small


_REPLICATED = ("mix_norm_g", "ffn_norm_g", "cp_conv_b", "cp_ln_g", "cp_ln_b", "cp_pool_w", "cp_pool_scale", "final_norm_g")
_SMALL_SHARDED = ("meta_tokens", "cp_conv_w", "gla_gate_w2", "gla_gate_b", "gla_head_g")
_NAMES = ("meta_tokens", "mix_norm_g", "ffn_norm_g", "ffn_w1", "ffn_w2", "cp_w_in", "cp_conv_w", "cp_conv_b", "cp_ln_g",
          "cp_ln_b", "cp_pool_w", "cp_pool_scale", "cp_w_out", "gla_w_in", "gla_gate_w2", "gla_gate_b", "gla_head_g",
          "gla_w_out", "final_norm_g")
_SMALL_GRADS = ("mix_g", "ffn_g", "conv_b", "ln_g", "ln_b", "pool_w", "pool_scale", "final_g", "meta", "conv_w", "gate_w",
                "gate_b", "head_g")
_GROUPS = ("cp", "ffn0_w1", "ffn0_w2", "gla", "ffn1")
_TWO_LEG_GATHERS = ("cp", "ffn0_w1", "ffn0_w2", "ffn1")


class _Exchanges:
    def __init__(self, w, d):
        self.d = d
        small = [w[n].reshape(w[n].shape[-2:]) for n in _SMALL_SHARDED]
        self.small_shard_shapes = [w[n].shape for n in _SMALL_SHARDED]
        shards = dict(
            cp=[(w["cp_w_in"][0].T, BF16), (w["cp_w_out"][0], BF16)] + [(a, F32) for a in small],
            ffn0_w1=[((w["ffn_w1"], 0), BF16)], ffn0_w2=[((w["ffn_w2"], 0), BF16)],
            gla=[(w["gla_w_in"][0].T, BF16), (w["gla_w_out"][0], BF16)],
            ffn1=[((w["ffn_w1"], 1), BF16), ((w["ffn_w2"], 1), BF16)])
        self.gathers = {}
        self.sent = {}
        token = None
        for group in _GROUPS:
            lands = _place_own(shards[group], after=token, name=f"place_w_{group}")
            plan = _plan_gather_first if group in _TWO_LEG_GATHERS else _plan_gather_direct
            self.gathers[group], token = _exchange_start(plan, None, lands, after=token, name=f"start_w_{group}")
        self.token = token

    def _gathered(self, group, after):
        after = (list(after) if isinstance(after, (list, tuple)) else [after]) + [self.token]
        _, got = _exchange_wait(self.gathers[group], after, name=f"wait_w_{group}")
        if group in _TWO_LEG_GATHERS:
            got = _exchange_now(_plan_gather_relay, got, name=f"relay_w_{group}")
        return got

    def fetch(self, group, after):
        d = self.d
        if group == "ffn0":
            w1, = self._gathered("ffn0_w1", after)
            w2, = self._gathered("ffn0_w2", w1)
            return dict(w1=w1, w2=w2)
        got = self._gathered(group, after)
        if group == "ffn1":
            return dict(w1=got[0], w2=got[1])
        if group == "gla":
            return dict(gla_w_in_t=got[0], gla_w_out=got[1].reshape(d, d), gate_w=self.gate_w, gate_b=self.gate_b,
                        head_g=self.head_g)
        meta, conv_w, gate_w, self.gate_b, self.head_g = [_undo_column_split(a) for a in got[2:]]
        self.gate_w = jnp.pad(gate_w, ((0, GATE_PAD - GATE_RANK), (0, 0))).astype(BF16)
        return dict(cp_w_in_t=got[0].reshape(-1, d), cp_w_out=got[1].reshape(d, d), meta=meta,
                    conv_w=jnp.pad(conv_w, ((0, 1), (0, 0))))

    def emit(self, group, g):
        d = self.d
        if group in ("ffn0_w1", "ffn1_w1"):
            arrs = [g["w1"]]
        elif group in ("ffn0_w2", "ffn1_w2"):
            arrs = [g["w2"].reshape(N_DEV, -1, d)]
        elif group == "gla":
            arrs = [g["gla_w_in_t"], g["gla_w_out"].reshape(N_DEV, d // N_DEV, d)]
        elif group == "cp_out":
            arrs = [g["cp_w_out"].reshape(N_DEV, d // N_DEV, d)]
        else:
            s = dict(g["small"])
            s.update(pool_w=s["pool_w"][None], conv_w=s["conv_w"][:CONV_WIDTH], gate_w=s["gate_w"][:GATE_RANK])
            own = [s[n] for n in _SMALL_GRADS[:len(_REPLICATED)]]
            own += [_column_split(s[n]).reshape((N_DEV,) + shape)
                    for n, shape in zip(_SMALL_GRADS[len(_REPLICATED):], self.small_shard_shapes)]
            plans = [_plan_to_all] * len(_REPLICATED) + [_plan_split_to_all] * len(_SMALL_SHARDED)
            lands = [lax.empty((N_DEV,) + a.shape, F32) for a in own[:len(_REPLICATED)]]
            lands += [lax.empty(a.shape, F32) for a in own[len(_REPLICATED):]]
            own.append(g["loss"])
            plans.append(_plan_to_all)
            lands.append(lax.empty((N_DEV,) + g["loss"].shape, F32))
            own.append(g["cp_w_in_t"].reshape(N_DEV, -1, d))
            plans.append(_plan_split_to_all)
            lands.append(lax.empty(own[-1].shape, BF16))
            self.sent[group], self.token = _exchange_start(plans, own, lands, after=self.token, name=f"start_g_{group}")
            return self.token
        self.sent[group], self.token = _exchange_start(_plan_split_to_all, arrs, [lax.empty(a.shape, a.dtype) for a in arrs],
                                                       after=self.token, name=f"start_g_{group}")
        return self.token

    def finish(self, w, mom, var):
        out = {}
        after = self.token

        def landed(group):
            own, got = _exchange_wait(self.sent[group], after, name=f"wait_g_{group}")
            return list(zip(own, got))

        def adam(n, parts, behind=None, transposed=False):
            by_row = transposed and w[n].shape[2] % 8 != 0
            if by_row:
                flip, back = (lambda a: jnp.transpose(a, (2, 0, 1))), (lambda a: jnp.transpose(a, (1, 2, 0)))
            else:
                flip = back = (lambda a: jnp.transpose(a, (0, 2, 1))) if transposed else (lambda a: a)
            res = _reduce_adam(parts, flip(w[n]), flip(mom[n]), flip(var[n]), by_row=by_row, after=behind, name=f"adam_{n}")
            out[n] = tuple(back(a) for a in res)
            return res[0]

        ffn1_w1 = landed("ffn1_w1")
        after = ffn1_w1[0][1]
        ffn1_w2 = landed("ffn1_w2")
        after = ffn1_w2[0][1]
        gla = landed("gla")
        after = adam("gla_w_in", [gla[0]], transposed=True)
        after = adam("gla_w_out", [gla[1]], after)
        ffn0_w1 = landed("ffn0_w1")
        after = adam("ffn_w1", [ffn0_w1[0], ffn1_w1[0]])
        ffn0_w2 = landed("ffn0_w2")
        after = adam("ffn_w2", [ffn0_w2[0], ffn1_w2[0]])
        cp_out = landed("cp_out")
        after = adam("cp_w_out", [cp_out[0]])
        *small, loss, cp_w_in = landed("cp")
        names = _REPLICATED + _SMALL_SHARDED
        split = [False] * len(_REPLICATED) + [True] * len(_SMALL_SHARDED)
        small_new = _adam_small([own for own, _ in small], [got for _, got in small], split, [w[n] for n in names],
                                [mom[n] for n in names], [var[n] for n in names], name="adam_small")
        out.update(zip(names, small_new))
        out["loss"] = _sum8(*loss, name="sum_loss")[0, 0]
        adam("cp_w_in", [cp_w_in], small_new[0][0], transposed=True)
        return out


def kernel(x, meta_tokens, mix_norm_g, ffn_norm_g, ffn_w1, ffn_w2, cp_w_in, cp_conv_w, cp_conv_b, cp_ln_g, cp_ln_b, cp_pool_w, cp_pool_scale, cp_w_out, gla_w_in, gla_gate_w2, gla_gate_b, gla_head_g, gla_w_out, final_norm_g, loss_target, m_meta_tokens, m_mix_norm_g, m_ffn_norm_g, m_ffn_w1, m_ffn_w2, m_cp_w_in, m_cp_conv_w, m_cp_conv_b, m_cp_ln_g, m_cp_ln_b, m_cp_pool_w, m_cp_pool_scale, m_cp_w_out, m_gla_w_in, m_gla_gate_w2, m_gla_gate_b, m_gla_head_g, m_gla_w_out, m_final_norm_g, v_meta_tokens, v_mix_norm_g, v_ffn_norm_g, v_ffn_w1, v_ffn_w2, v_cp_w_in, v_cp_conv_w, v_cp_conv_b, v_cp_ln_g, v_cp_ln_b, v_cp_pool_w, v_cp_pool_scale, v_cp_w_out, v_gla_w_in, v_gla_gate_w2, v_gla_gate_b, v_gla_head_g, v_gla_w_out, v_final_norm_g):
    w = dict(meta_tokens=meta_tokens, mix_norm_g=mix_norm_g, ffn_norm_g=ffn_norm_g, ffn_w1=ffn_w1, ffn_w2=ffn_w2,
             cp_w_in=cp_w_in, cp_conv_w=cp_conv_w, cp_conv_b=cp_conv_b, cp_ln_g=cp_ln_g, cp_ln_b=cp_ln_b,
             cp_pool_w=cp_pool_w, cp_pool_scale=cp_pool_scale, cp_w_out=cp_w_out, gla_w_in=gla_w_in,
             gla_gate_w2=gla_gate_w2, gla_gate_b=gla_gate_b, gla_head_g=gla_head_g, gla_w_out=gla_w_out,
             final_norm_g=final_norm_g.reshape(1, -1))
    mom = dict(meta_tokens=m_meta_tokens, mix_norm_g=m_mix_norm_g, ffn_norm_g=m_ffn_norm_g, ffn_w1=m_ffn_w1, ffn_w2=m_ffn_w2,
               cp_w_in=m_cp_w_in, cp_conv_w=m_cp_conv_w, cp_conv_b=m_cp_conv_b, cp_ln_g=m_cp_ln_g, cp_ln_b=m_cp_ln_b,
               cp_pool_w=m_cp_pool_w, cp_pool_scale=m_cp_pool_scale, cp_w_out=m_cp_w_out, gla_w_in=m_gla_w_in,
               gla_gate_w2=m_gla_gate_w2, gla_gate_b=m_gla_gate_b, gla_head_g=m_gla_head_g, gla_w_out=m_gla_w_out,
               final_norm_g=m_final_norm_g.reshape(1, -1))
    var = dict(meta_tokens=v_meta_tokens, mix_norm_g=v_mix_norm_g, ffn_norm_g=v_ffn_norm_g, ffn_w1=v_ffn_w1, ffn_w2=v_ffn_w2,
               cp_w_in=v_cp_w_in, cp_conv_w=v_cp_conv_w, cp_conv_b=v_cp_conv_b, cp_ln_g=v_cp_ln_g, cp_ln_b=v_cp_ln_b,
               cp_pool_w=v_cp_pool_w, cp_pool_scale=v_cp_pool_scale, cp_w_out=v_cp_w_out, gla_w_in=v_gla_w_in,
               gla_gate_w2=v_gla_gate_w2, gla_gate_b=v_gla_gate_b, gla_head_g=v_gla_head_g, gla_w_out=v_gla_w_out,
               final_norm_g=v_final_norm_g.reshape(1, -1))
    d = x.shape[-1]
    replicated = dict(mix_g=w["mix_norm_g"], ffn_g=w["ffn_norm_g"], conv_b=w["cp_conv_b"], ln_g=w["cp_ln_g"],
                      ln_b=w["cp_ln_b"], pool_w=w["cp_pool_w"][0].astype(BF16), pool_scale=w["cp_pool_scale"],
                      final_g=w["final_norm_g"])
    exchanges = _Exchanges(w, d)
    _, grad_x, _ = _local_step(x[0], loss_target[0], replicated, exchanges)
    out = exchanges.finish(w, mom, var)
    loss = out.pop("loss")

    def leaf(n, k):
        a = out[n][k]
        return a.reshape(-1) if n == "final_norm_g" else a

    return (loss, grad_x[None], *[leaf(n, 0) for n in _NAMES], *[leaf(n, 1) for n in _NAMES],
            *[leaf(n, 2) for n in _NAMES], *[leaf(n, 3) for n in _NAMES])
```

```python
import functools

import jax
import jax.numpy as jnp
from jax import lax
from jax.experimental import pallas as pl
from jax.experimental.pallas import tpu as pltpu

F32, BF16 = jnp.float32, jnp.bfloat16
N_DEV = 8
CHUNK = 64
N_META = 16
PAD_ROWS = CHUNK - N_META
HALO = 32
EPS = 1e-5
CONV_WIDTH = 31
POOL_WINDOWS = (2, 4, 8, 16)
HEADS = 4
GATE_RANK = 16
GATE_NORM = 16.0
GATE_PAD = 128
ADAM_LR, ADAM_B1, ADAM_B2, ADAM_EPS, ADAM_WD, ADAM_STEP = 0.001, 0.9, 0.999, 1e-08, 0.01, 10
V7X_VMEM_LIMIT = 56 * 2 ** 20
LANE = 128


def _cparams(*sem):
    return pltpu.CompilerParams(dimension_semantics=sem, vmem_limit_bytes=V7X_VMEM_LIMIT)


def _row_tile(t, cap):
    best = CHUNK
    for r in range(CHUNK, min(t, cap) + 1, CHUNK):
        if t % r == 0:
            best = r
    return best


def _resident(shape):
    return pl.BlockSpec(shape, lambda *_: (0,) * len(shape), pipeline_mode=pl.Buffered(1))


def _dot(a, b):
    return jnp.dot(a, b, preferred_element_type=F32)


def _dot_nt(a, b):
    return lax.dot_general(a, b, (((1,), (1,)), ((), ())), preferred_element_type=F32)


def _dot_tn(a, b):
    return lax.dot_general(a, b, (((0,), (0,)), ((), ())), preferred_element_type=F32)


def _rowsum(a):
    return jnp.sum(a, axis=0, keepdims=True)


def _sigmoid(a):
    return 1.0 / (1.0 + jnp.exp(-a))


def _row_ids(tile, rt):
    return tile * rt + lax.broadcasted_iota(jnp.int32, (rt, 1), 0)


def _sds(shape, dtype):
    return jax.ShapeDtypeStruct(shape, dtype)


DW_ROWS = 1024


def _linear_bwd_w(x, dy, *, square_x=False, column_blocks=None, row_blocks=None, after=None, name):
    t, k = x.shape
    n = dy.shape[1]
    cut_k = k > n and column_blocks is None
    assert cut_k or row_blocks is None
    width = k if cut_k else n
    blk = n // column_blocks if column_blocks else max(c for c in (640, 512, 384, 256, LANE) if width % c == 0)
    dep_specs, deps = _dep_specs(after)

    def body(*refs):
        x_ref, dy_ref, o_ref, acc = refs[len(deps):]
        for c0 in range(0, t, DW_ROWS):
            rows = slice(c0, min(c0 + DW_ROWS, t))
            xv = x_ref[rows, :]
            if square_x:
                xv = xv.astype(F32)
                xv = xv * xv
            part = _dot_tn(xv.astype(BF16), dy_ref[rows, :].astype(BF16))
            if c0 == 0:
                acc[...] = part
            else:
                acc[...] += part
        if row_blocks is None:
            o_ref[...] = acc[...].astype(BF16)
            return
        nb, rpb = row_blocks
        for s in range(width // blk):
            @pl.when(pl.program_id(0) == s)
            def _(s=s):
                for b in range(nb):
                    lo, hi = max(s * blk, b * rpb), min((s + 1) * blk, (b + 1) * rpb)
                    if lo < hi:
                        o_ref[b, lo - b * rpb:hi - b * rpb, :] = acc[lo - s * blk:hi - s * blk, :].astype(BF16)

    out_shape = _sds((k, n), BF16)
    semantics = "parallel"
    if cut_k:
        in_specs = [pl.BlockSpec((t, blk), lambda j: (0, j)), _resident((t, n))]
        out_specs = pl.BlockSpec((blk, n), lambda j: (j, 0))
        acc_shape = (blk, n)
        if row_blocks:
            out_shape = _sds(row_blocks + (n,), BF16)
            out_specs = pl.BlockSpec(out_shape.shape, lambda j: (0, 0, 0))
            semantics = "arbitrary"
    else:
        in_specs = [_resident((t, k)), pl.BlockSpec((t, blk), lambda j: (0, j))]
        out_specs = pl.BlockSpec((k, blk), lambda j: (0, j))
        acc_shape = (k, blk)
        if column_blocks:
            out_specs = pl.BlockSpec((None, k, blk), lambda j: (j, 0, 0))
            out_shape = _sds((column_blocks, k, blk), BF16)
    return pl.pallas_call(
        body, grid=(width // blk,), in_specs=dep_specs + in_specs, out_specs=out_specs, out_shape=out_shape,
        scratch_shapes=[pltpu.VMEM(acc_shape, F32)], compiler_params=_cparams(semantics), name=name)(*deps, x, dy)


FFN_BLOCKS_PER_STEP = 2


def _ffn_fwd(h, gain, w1g, w2g, *, loss_head=None, name):
    t, d = h.shape
    f8 = w1g.shape[-1]
    rt = _row_tile(t, 832)
    nb = FFN_BLOCKS_PER_STEP
    nstep = N_DEV // nb

    def body(*refs):
        if loss_head is None:
            h_ref, g_ref, w1_ref, w2_ref, o_ref, u_ref, r_ref, acc_ref = refs
        else:
            (h_ref, g_ref, w1_ref, w2_ref, fg_ref, tgt_ref, o_ref, u_ref, r_ref, loss_ref, dfg_ref, acc_ref, t_ref,
             t_sem) = refs
        i, j = pl.program_id(0), pl.program_id(1)

        def target_rows(act):
            @pl.when(i == 0)
            def _():
                act(pltpu.make_async_copy(tgt_ref.at[pl.ds(0, rt - CHUNK)], t_ref.at[pl.ds(CHUNK, rt - CHUNK)], t_sem.at[0]))

            if t > rt:
                @pl.when(i > 0)
                def _():
                    act(pltpu.make_async_copy(tgt_ref.at[pl.ds(pl.multiple_of(i * rt - CHUNK, CHUNK), rt)], t_ref,
                                              t_sem.at[0]))

        @pl.when(j == 0)
        def _():
            if loss_head is not None:
                @pl.when(i == 0)
                def _():
                    t_ref[0:CHUNK, :] = jnp.zeros((CHUNK, d), F32)

                target_rows(lambda copy: copy.start())

            hv = h_ref[...]
            u_ref[...] = (hv * lax.rsqrt(jnp.mean(hv * hv, axis=-1, keepdims=True) + EPS) * g_ref[...]).astype(BF16)
            acc_ref[...] = jnp.zeros_like(acc_ref)

        part = None
        for b in range(nb):
            a = jnp.maximum(_dot(u_ref[...], w1_ref[b]), 0.0)
            r_ref[:, b * f8:(b + 1) * f8] = a.astype(BF16)
            term = _dot((a * a).astype(BF16), w2_ref[b])
            part = term if part is None else part + term
        acc_ref[...] += part

        @pl.when(j == nstep - 1)
        def _():
            y = h_ref[...] + acc_ref[...]
            if loss_head is None:
                o_ref[...] = y
                return

            @pl.when(i == 0)
            def _():
                loss_ref[...] = jnp.zeros_like(loss_ref)
                dfg_ref[...] = jnp.zeros_like(dfg_ref)

            target_rows(lambda copy: copy.wait())

            rstd = lax.rsqrt(jnp.mean(y * y, axis=-1, keepdims=True) + EPS)
            xh = y * rstd
            err = jnp.where(_row_ids(i, rt) >= CHUNK, xh * fg_ref[...] - t_ref[...], 0.0)
            loss_ref[...] += (0.5 / d) * jnp.sum(err * err)
            dy = err * (1.0 / d)
            dfg_ref[...] += _rowsum(dy * xh)
            dxh = dy * fg_ref[...]
            o_ref[...] = rstd * (dxh - xh * jnp.mean(dxh * xh, axis=-1, keepdims=True))

    rows = lambda i, j: (i, 0)
    in_specs = [pl.BlockSpec((rt, d), rows), _resident((1, d)),
                pl.BlockSpec((nb, d, f8), lambda i, j: (j, 0, 0)), pl.BlockSpec((nb, f8, d), lambda i, j: (j, 0, 0))]
    out_specs = [pl.BlockSpec((rt, d), rows), pl.BlockSpec((rt, d), rows), pl.BlockSpec((rt, nb * f8), lambda i, j: (i, j))]
    out_shape = [_sds((t, d), F32), _sds((t, d), BF16), _sds((t, N_DEV * f8), BF16)]
    args = [h, gain, w1g, w2g]
    scratch_shapes = [pltpu.VMEM((rt, d), F32)]
    if loss_head is not None:
        in_specs += [_resident((1, d)), pl.BlockSpec(memory_space=pl.ANY)]
        out_specs += [pl.BlockSpec((8, LANE), lambda i, j: (0, 0)), pl.BlockSpec((1, d), lambda i, j: (0, 0))]
        out_shape += [_sds((8, LANE), F32), _sds((1, d), F32)]
        args += list(loss_head)
        scratch_shapes += [pltpu.VMEM((rt, d), F32), pltpu.SemaphoreType.DMA((1,))]
    return pl.pallas_call(
        body, grid=(t // rt, nstep), in_specs=in_specs, out_specs=out_specs, out_shape=out_shape,
        scratch_shapes=scratch_shapes,
        compiler_params=_cparams("arbitrary" if loss_head is not None else "parallel", "arbitrary"), name=name)(*args)


def _ffn_bwd_x(h, dout, gain, r, w1g, w2g, *, after=None, name):
    t, d = h.shape
    f8 = w1g.shape[-1]
    rt = _row_tile(t, 832)
    nb = FFN_BLOCKS_PER_STEP
    last = N_DEV // nb - 1
    dep_specs, deps = _dep_specs(after)

    def body(*refs):
        h_ref, do_ref, g_ref, r_ref, w1_ref, w2_ref, dh_ref, dhh_ref, dob_ref, dg_ref, du_ref = refs[len(deps):]
        i, j = pl.program_id(0), pl.program_id(1)

        @pl.when(j == 0)
        def _():
            dob_ref[...] = do_ref[...].astype(BF16)
            du_ref[...] = jnp.zeros_like(du_ref)

        part = None
        for b in range(nb):
            cols = slice(b * f8, (b + 1) * f8)
            dhh = (_dot_nt(dob_ref[...], w2_ref[b]) * (2.0 * r_ref[:, cols].astype(F32))).astype(BF16)
            dhh_ref[:, cols] = dhh
            term = _dot_nt(dhh, w1_ref[b])
            part = term if part is None else part + term
        du_ref[...] += part

        @pl.when(j == last)
        def _():
            @pl.when(i == 0)
            def _():
                dg_ref[...] = jnp.zeros_like(dg_ref)

            hv = h_ref[...]
            rstd = lax.rsqrt(jnp.mean(hv * hv, axis=-1, keepdims=True) + EPS)
            xh = hv * rstd
            du = du_ref[...]
            dg_ref[...] += _rowsum(du * xh)
            dxh = du * g_ref[...]
            dh_ref[...] = do_ref[...] + rstd * (dxh - xh * jnp.mean(dxh * xh, axis=-1, keepdims=True))

    rows = lambda i, j: (i, 0)
    return pl.pallas_call(
        body, grid=(t // rt, N_DEV // nb),
        in_specs=dep_specs + [
                  pl.BlockSpec((rt, d), rows), pl.BlockSpec((rt, d), rows), _resident((1, d)),
                  pl.BlockSpec((rt, nb * f8), lambda i, j: (i, j)),
                  pl.BlockSpec((nb, d, f8), lambda i, j: (j, 0, 0)),
                  pl.BlockSpec((nb, f8, d), lambda i, j: (j, 0, 0))],
        out_specs=[pl.BlockSpec((rt, d), rows), pl.BlockSpec((rt, nb * f8), lambda i, j: (i, j)),
                   pl.BlockSpec((rt, d), rows), pl.BlockSpec((1, d), lambda i, j: (0, 0))],
        out_shape=[_sds((t, d), F32), _sds((t, N_DEV * f8), BF16), _sds((t, d), BF16), _sds((1, d), F32)],
        scratch_shapes=[pltpu.VMEM((rt, d), F32)],
        compiler_params=_cparams("arbitrary", "arbitrary"), name=name)(*deps, h, dout, gain, r, w1g, w2g)


def _lane_blocks(width):
    lb = min(LANE, width)
    return [slice(s, s + lb) for s in range(0, width, lb)]


def _conv_rows(src_ref, w_ref, offset, dst_ref, nblk, width, bias_ref=None):
    def blk(rb, carry):
        base = pl.multiple_of(rb * CHUNK, CHUNK)
        for l, ls in enumerate(_lane_blocks(width)):
            acc = jnp.zeros((CHUNK, ls.stop - ls.start), F32)
            if bias_ref is not None:
                acc = acc + bias_ref[:, ls]
            for k in range(CONV_WIDTH):
                acc = acc + w_ref[k:k + 1, ls] * src_ref[l, pl.ds(base + offset(k), CHUNK), :]
            dst_ref[l, pl.ds(base, CHUNK), :] = acc
        return carry

    lax.fori_loop(0, nblk, blk, 0)


def _to_lane_blocks(ref, row0, value):
    for l, ls in enumerate(_lane_blocks(value.shape[1])):
        ref[l, row0:row0 + value.shape[0], :] = value[:, ls]


def _from_lane_blocks(ref):
    return jnp.concatenate([ref[l] for l in range(ref.shape[0])], axis=1)


def _pool_counts(rows, window):
    return jnp.clip(rows - PAD_ROWS + 1, 1, window).astype(F32)


def _trailing_sum(v, window):
    s, sh = v, 1
    while sh < window:
        s = s + pltpu.roll(s, sh, 0)
        sh *= 2
    return s


def _leading_sum(v, window):
    s, sh, n = v, 1, v.shape[0]
    while sh < window:
        s = s + pltpu.roll(s, n - sh, 0)
        sh *= 2
    return s


def _norm_project(h_ref, g_ref, w_t_ref, u_ref, z_ref):
    hv = h_ref[...]
    u = (hv * lax.rsqrt(jnp.mean(hv * hv, axis=-1, keepdims=True) + EPS) * g_ref[...]).astype(BF16)
    u_ref[...] = u
    z_ref[...] = _dot_nt(u, w_t_ref[...])


def _project_back(dz_ref, w_t_ref, h_ref, g_ref, dres_ref, dh_ref, dg_ref, first):
    dx = _dot(dz_ref[...], w_t_ref[...])
    hv = h_ref[...]
    rstd = lax.rsqrt(jnp.mean(hv * hv, axis=-1, keepdims=True) + EPS)
    xh = hv * rstd

    @pl.when(first)
    def _():
        dg_ref[...] = jnp.zeros_like(dg_ref)

    dg_ref[...] += _rowsum(dx * xh)
    dxh = dx * g_ref[...]
    dh_ref[...] = dres_ref[...] + rstd * (dxh - xh * jnp.mean(dxh * xh, axis=-1, keepdims=True))


def _cp_mid_fwd(x, meta, gain, w_in_t, w_out, conv_w, conv_b, ln_g, ln_b, pool_w, pool_scale, *, name):
    seq, d = x.shape
    t = seq + CHUNK
    ein = w_in_t.shape[0]
    cd = conv_b.shape[1]
    pd = pool_scale.shape[1]
    pg = pd // len(POOL_WINDOWS)
    rt = _row_tile(t, 320)
    ntile = t // rt

    def body(x_ref, meta_ref, g_ref, wi_ref, wo_ref, cw_ref, cb_ref, lg_ref, lb_ref, pw_ref, ps_ref,
             ho_ref, o_ref, z_ref, u_ref, cv_ref, h0_ref, gext, pext, conv_s, hbuf, hsem):
        i = pl.program_id(0)
        slot = i % 2
        first_rows = pltpu.make_async_copy(x_ref.at[pl.ds(0, rt - CHUNK)], hbuf.at[0, pl.ds(CHUNK, rt - CHUNK)], hsem.at[0])

        def tile_rows(tile, to):
            return pltpu.make_async_copy(x_ref.at[pl.ds(pl.multiple_of(tile * rt - CHUNK, CHUNK), rt)], hbuf.at[to],
                                         hsem.at[to])

        @pl.when(i == 0)
        def _():
            first_rows.start()
            hbuf[0, 0:PAD_ROWS, :] = jnp.zeros((PAD_ROWS, d), F32)
            hbuf[0, PAD_ROWS:CHUNK, :] = meta_ref[...]
            _to_lane_blocks(gext, 0, jnp.zeros((HALO, cd), F32))
            pext[0:HALO, :] = jnp.zeros((HALO, pd), F32)

        @pl.when(i + 1 < ntile)
        def _():
            tile_rows(i + 1, 1 - slot).start()

        @pl.when(i == 0)
        def _():
            first_rows.wait()

        @pl.when(i > 0)
        def _():
            tile_rows(i, slot).wait()

        h_ref = hbuf.at[slot]
        h0_ref[...] = h_ref[...]
        _norm_project(h_ref, g_ref, wi_ref, u_ref, z_ref)

        _to_lane_blocks(gext, HALO, z_ref[:, 0:cd] * _sigmoid(z_ref[:, cd:2 * cd]))
        pext[HALO:HALO + rt, :] = z_ref[:, 2 * cd:]
        _conv_rows(gext, cw_ref, lambda k: k + HALO - (CONV_WIDTH - 1), conv_s, rt // CHUNK, cd, cb_ref)
        cv = _from_lane_blocks(conv_s)
        cv_ref[...] = cv
        xc = cv - jnp.mean(cv, axis=-1, keepdims=True)
        y = xc * lax.rsqrt(jnp.mean(xc * xc, axis=-1, keepdims=True) + EPS) * lg_ref[...] + lb_ref[...]
        rows = _row_ids(i, rt)
        a = jnp.where(rows >= PAD_ROWS, y * _sigmoid(y), 0.0)
        o_ref[:, 0:cd] = a.astype(BF16)
        for gi, window in enumerate(POOL_WINDOWS):
            ls = slice(gi * pg, (gi + 1) * pg)
            v = pext[:, ls]
            tm = _trailing_sum(v, window)[HALO:] / _pool_counts(rows, window) - v[HALO:]
            p = _dot(tm.astype(BF16), pw_ref[gi]) * ps_ref[:, ls]
            o_ref[:, cd + gi * pg:cd + (gi + 1) * pg] = p.astype(BF16)
        ho_ref[...] = h_ref[...] + _dot(o_ref[...], wo_ref[...])
        gext[:, 0:HALO, :] = gext[:, rt:rt + HALO, :]
        pext[0:HALO, :] = pext[rt:rt + HALO, :]

    nl, lb = len(_lane_blocks(cd)), min(LANE, cd)
    rows = lambda i: (i, 0)
    return pl.pallas_call(
        body, grid=(ntile,),
        in_specs=[pl.BlockSpec(memory_space=pl.ANY), _resident(meta.shape), _resident((1, d)), _resident(w_in_t.shape),
                  _resident(w_out.shape), _resident(conv_w.shape), _resident((1, cd)),
                  _resident((1, cd)), _resident((1, cd)), _resident(pool_w.shape), _resident((1, pd))],
        out_specs=[pl.BlockSpec((rt, d), rows), pl.BlockSpec((rt, cd + pd), rows), pl.BlockSpec((rt, ein), rows),
                   pl.BlockSpec((rt, d), rows), pl.BlockSpec((rt, cd), rows), pl.BlockSpec((rt, d), rows)],
        out_shape=[_sds((t, d), F32), _sds((t, cd + pd), BF16), _sds((t, ein), F32), _sds((t, d), BF16),
                   _sds((t, cd), F32), _sds((t, d), F32)],
        scratch_shapes=[pltpu.VMEM((nl, rt + HALO, lb), F32), pltpu.VMEM((rt + HALO, pd), F32),
                        pltpu.VMEM((nl, rt, lb), F32), pltpu.VMEM((2, rt, d), F32), pltpu.SemaphoreType.DMA((2,))],
        compiler_params=_cparams("arbitrary"), name=name)(x, meta, gain, w_in_t, w_out, conv_w, conv_b, ln_g, ln_b, pool_w,
                                                          pool_scale)


def _cp_mid_bwd(z, cv, h, gain, w_in_t, dh, w_out, conv_w, conv_b, ln_g, ln_b, pool_w, pool_scale, *, after=None, name):
    t, ein = z.shape
    cd = conv_b.shape[1]
    pd = pool_scale.shape[1]
    pg = pd // len(POOL_WINDOWS)
    rt = _row_tile(t, 320)
    ntile = t // rt
    per = rt // CHUNK
    dep_specs, deps = _dep_specs(after)

    def body(*refs):
        (z_ref, zh_ref, cv_ref, h_ref, g_ref, wi_ref, dh_ref, wo_ref, cw_ref, cb_ref, lg_ref, lb_ref, pw_ref, ps_ref,
         dx_ref, dfirst_ref, dg_ref, dz_ref, dcw_ref, dcb_ref, dlg_ref, dlb_ref, dpw_ref, dps_ref,
         gext, pext, conv_s, dcv, dsp, dhi, dx_sem) = refs[len(deps):]
        step = pl.program_id(0)
        tile = ntile - 1 - step
        slot = step % 2
        last_slot = (ntile - 1) % 2
        first_rows = pltpu.make_async_copy(dhi.at[last_slot, pl.ds(CHUNK, rt - CHUNK)], dx_ref.at[pl.ds(0, rt - CHUNK)],
                                           dx_sem.at[last_slot])

        def tile_rows(tile, slot):
            return pltpu.make_async_copy(dhi.at[slot], dx_ref.at[pl.ds(pl.multiple_of(tile * rt - CHUNK, CHUNK), rt)],
                                         dx_sem.at[slot])

        dcat = _dot_nt(dh_ref[...].astype(BF16), wo_ref[...])

        @pl.when(step >= 2)
        def _():
            tile_rows(tile + 2, slot).wait()

        @pl.when(step == 0)
        def _():
            for ref in (dcw_ref, dcb_ref, dlg_ref, dlb_ref, dpw_ref, dps_ref):
                ref[...] = jnp.zeros_like(ref)
            _to_lane_blocks(dcv, rt, jnp.zeros((HALO, cd), F32))
            dsp[rt:rt + HALO, :] = jnp.zeros((HALO, pd), F32)

        keep = jnp.where(tile > 0, 1.0, 0.0)
        zh = zh_ref[CHUNK - HALO:CHUNK, :]
        _to_lane_blocks(gext, 0, keep * zh[:, 0:cd] * _sigmoid(zh[:, cd:2 * cd]))
        pext[0:HALO, :] = keep * zh[:, 2 * cd:]
        za = z_ref[:, 0:cd]
        sg = _sigmoid(z_ref[:, cd:2 * cd])
        _to_lane_blocks(gext, HALO, za * sg)
        pext[HALO:HALO + rt, :] = z_ref[:, 2 * cd:]
        cv = cv_ref[...]
        xc = cv - jnp.mean(cv, axis=-1, keepdims=True)
        rstd = lax.rsqrt(jnp.mean(xc * xc, axis=-1, keepdims=True) + EPS)
        xh = xc * rstd
        y = xh * lg_ref[...] + lb_ref[...]
        sy = _sigmoid(y)
        rows = _row_ids(tile, rt)
        da = jnp.where(rows >= PAD_ROWS, dcat[:, 0:cd], 0.0)
        dy = da * (sy * (1.0 + y * (1.0 - sy)))
        dlg_ref[...] += _rowsum(dy * xh)
        dlb_ref[...] += _rowsum(dy)
        dxh = dy * lg_ref[...]
        dconv = rstd * (dxh - jnp.mean(dxh, axis=-1, keepdims=True) - xh * jnp.mean(dxh * xh, axis=-1, keepdims=True))
        dcb_ref[...] += _rowsum(dconv)
        _to_lane_blocks(dcv, 0, dconv)
        for l, ls in enumerate(_lane_blocks(cd)):
            def acc_rows(rb, accs, l=l):
                base = pl.multiple_of(rb * CHUNK, CHUNK)
                d_blk = dcv[l, pl.ds(base, CHUNK), :]
                out = []
                for k in range(CONV_WIDTH):
                    prod = d_blk * gext[l, pl.ds(base + k + HALO - (CONV_WIDTH - 1), CHUNK), :]
                    part = prod[0:8]
                    for s in range(8, CHUNK, 8):
                        part = part + prod[s:s + 8]
                    out.append(accs[k] + part)
                return tuple(out)

            zero = jnp.zeros((8, ls.stop - ls.start), F32)
            accs = lax.fori_loop(0, per, acc_rows, (zero,) * CONV_WIDTH)
            for k in range(CONV_WIDTH):
                dcw_ref[k:k + 1, ls] += _rowsum(accs[k])
        _conv_rows(dcv, cw_ref, lambda k: CONV_WIDTH - 1 - k, conv_s, per, cd)
        dglu = _from_lane_blocks(conv_s)
        dz_ref[:, 0:cd] = (dglu * sg).astype(BF16)
        dz_ref[:, cd:2 * cd] = (dglu * za * sg * (1.0 - sg)).astype(BF16)
        dcv[:, rt:rt + HALO, :] = dcv[:, 0:HALO, :]
        for gi, window in enumerate(POOL_WINDOWS):
            ls = slice(gi * pg, (gi + 1) * pg)
            v = pext[:, ls]
            cnt = _pool_counts(rows, window)
            tm = (_trailing_sum(v, window)[HALO:] / cnt - v[HALO:]).astype(BF16)
            dp = dcat[:, cd + gi * pg:cd + (gi + 1) * pg]
            dps_ref[:, ls] += _rowsum(dp * _dot(tm, pw_ref[gi]))
            dpl = (dp * ps_ref[:, ls]).astype(BF16)
            dpw_ref[gi] += _dot_tn(tm, dpl)
            dtm = _dot_nt(dpl, pw_ref[gi])
            dsp[0:rt, ls] = dtm / cnt
            dpin = _leading_sum(dsp[:, ls], window)[0:rt] - dtm
            dz_ref[:, 2 * cd + gi * pg:2 * cd + (gi + 1) * pg] = dpin.astype(BF16)
        dsp[rt:rt + HALO, :] = dsp[0:HALO, :]

        _project_back(dz_ref, wi_ref, h_ref, g_ref, dh_ref, dhi.at[slot], dg_ref, step == 0)

        @pl.when(tile > 0)
        def _():
            tile_rows(tile, slot).start()

        @pl.when(tile == 0)
        def _():
            first_rows.start()
            dfirst_ref[...] = dhi[last_slot, 0:CHUNK, :]
            if ntile > 1:
                tile_rows(1, 1 - last_slot).wait()
            first_rows.wait()

    d = h.shape[1]
    back = lambda i: (ntile - 1 - i, 0)
    halo_idx = lambda i: (jnp.maximum((ntile - 1 - i) * per - 1, 0), 0)
    const2 = lambda i: (0, 0)
    nl, lb = len(_lane_blocks(cd)), min(LANE, cd)
    return pl.pallas_call(
        body, grid=(ntile,),
        in_specs=dep_specs + [
                  pl.BlockSpec((rt, ein), back), pl.BlockSpec((CHUNK, ein), halo_idx), pl.BlockSpec((rt, cd), back),
                  pl.BlockSpec((rt, d), back),
                  _resident((1, d)), _resident(w_in_t.shape), pl.BlockSpec((rt, d), back), _resident(w_out.shape),
                  _resident(conv_w.shape), _resident((1, cd)), _resident((1, cd)), _resident((1, cd)),
                  _resident(pool_w.shape), _resident((1, pd))],
        out_specs=[pl.BlockSpec(memory_space=pl.ANY), pl.BlockSpec((CHUNK, d), const2), pl.BlockSpec((1, d), const2),
                   pl.BlockSpec((rt, ein), back), pl.BlockSpec(conv_w.shape, const2), pl.BlockSpec((1, cd), const2),
                   pl.BlockSpec((1, cd), const2), pl.BlockSpec((1, cd), const2),
                   pl.BlockSpec(pool_w.shape, lambda i: (0, 0, 0)), pl.BlockSpec((1, pd), const2)],
        out_shape=[_sds((t - CHUNK, d), F32), _sds((CHUNK, d), F32), _sds((1, d), F32),
                   _sds((t, ein), BF16), _sds(conv_w.shape, F32), _sds((1, cd), F32), _sds((1, cd), F32),
                   _sds((1, cd), F32), _sds(pool_w.shape, F32), _sds((1, pd), F32)],
        scratch_shapes=[pltpu.VMEM((nl, rt + HALO, lb), F32), pltpu.VMEM((rt + HALO, pd), F32), pltpu.VMEM((nl, rt, lb), F32),
                        pltpu.VMEM((nl, rt + HALO, lb), F32), pltpu.VMEM((rt + HALO, pd), F32), pltpu.VMEM((2, rt, d), F32),
                        pltpu.SemaphoreType.DMA((2,))],
        compiler_params=_cparams("arbitrary"), name=name)(*deps, z, z, cv, h, gain, w_in_t, dh, w_out, conv_w, conv_b, ln_g,
                                                          ln_b, pool_w, pool_scale)


def _log_decay(r, gw_ref, gb_ref, rows):
    gp = _dot(r.astype(BF16), gw_ref[...]) + gb_ref[...]
    log_sig = jnp.minimum(gp, 0.0) - jnp.log(1.0 + jnp.exp(-jnp.abs(gp)))
    return gp, jnp.where(rows >= PAD_ROWS, log_sig / GATE_NORM, 0.0)


def _tri(strict):
    r = lax.broadcasted_iota(jnp.int32, (CHUNK, CHUNK), 0)
    c = lax.broadcasted_iota(jnp.int32, (CHUNK, CHUNK), 1)
    return jnp.where(c < r if strict else c <= r, 1.0, 0.0).astype(BF16)


def _tri_dot(tri, a):
    hi = a.astype(BF16)
    rest = a - hi.astype(F32)
    mid = rest.astype(BF16)
    lo = (rest - mid.astype(F32)).astype(BF16)
    return _dot(tri, hi) + _dot(tri, mid) + _dot(tri, lo)


def _gla_mid_fwd(h, gain, w_in_blocks, w_out, gate_w, gate_b, head_g, *, name):
    t = h.shape[0]
    nblk, rpb = w_in_blocks.shape[:2]
    dk = gate_b.shape[1]
    hv = head_g.shape[1]
    hk = dk // HEADS
    dv = hv * HEADS
    r_at = 2 * dk + 2 * dv
    assert nblk * rpb == r_at + GATE_RANK
    zw = r_at + GATE_PAD
    rt = _row_tile(t, 320)
    per = rt // CHUNK
    scale = hk ** -0.5

    def body(h_ref, g_ref, wb_ref, wo_ref, gw_ref, gb_ref, hg_ref, ho_ref, o_ref, st_ref, z_ref, u_ref, wi_ref,
             s_ref, la_ref, dec_ref):
        i = pl.program_id(0)

        @pl.when(i == 0)
        def _():
            s_ref[...] = jnp.zeros_like(s_ref)
            for b in range(nblk):
                wi_ref[b * rpb:(b + 1) * rpb, :] = wb_ref[b]
            wi_ref[nblk * rpb:, :] = jnp.zeros((zw - nblk * rpb, wi_ref.shape[1]), BF16)

        _norm_project(h_ref, g_ref, wi_ref, u_ref, z_ref)

        _, la = _log_decay(z_ref[:, r_at:r_at + GATE_PAD], gw_ref, gb_ref, _row_ids(i, rt))
        la_ref[...] = la
        tri = _tri(False)

        def chunk_rows(c):
            return slice(c * CHUNK, (c + 1) * CHUNK)

        def decays(c, carry):
            rows = chunk_rows(c)
            la_c = la_ref[rows, :]
            cum = _tri_dot(tri, la_c)
            dec_ref[rows, :] = jnp.exp(_rowsum(la_c) - cum)
            return carry

        def states(c, carry):
            rows = chunk_rows(c)
            etot = jnp.exp(_rowsum(la_ref[rows, :]))
            for hd in range(HEADS):
                ks = slice(hd * hk, (hd + 1) * hk)
                kd = z_ref[rows, dk + hd * hk:dk + (hd + 1) * hk] * dec_ref[rows, ks]
                v = z_ref[rows, 2 * dk + hd * hv:2 * dk + (hd + 1) * hv]
                s_new = s_ref[hd] * etot[:, ks] + _dot_tn(v.astype(BF16), kd.astype(BF16))
                s_ref[hd] = s_new
                st_ref[c, hd] = s_new
            return carry

        def outputs(c, carry):
            rows = chunk_rows(c)
            for hd in range(HEADS):
                q = z_ref[rows, hd * hk:(hd + 1) * hk] * scale
                g = z_ref[rows, 2 * dk + dv + hd * hv:2 * dk + dv + (hd + 1) * hv]
                o = _dot_nt(q.astype(BF16), st_ref[c, hd].astype(BF16))
                on = o * lax.rsqrt(jnp.mean(o * o, axis=-1, keepdims=True) + EPS) * hg_ref[...]
                o_ref[rows, hd * hv:(hd + 1) * hv] = (on * (g * _sigmoid(g))).astype(BF16)
            return carry

        for phase in (decays, states, outputs):
            for c in range(per):
                phase(c, 0)
        ho_ref[...] = h_ref[...] + _dot(o_ref[...], wo_ref[...])

    d = h.shape[1]
    rows = lambda i: (i, 0)
    return pl.pallas_call(
        body, grid=(t // rt,),
        in_specs=[pl.BlockSpec((rt, d), rows), _resident((1, d)), _resident(w_in_blocks.shape), _resident(w_out.shape),
                  _resident(gate_w.shape), _resident((1, dk)), _resident((1, hv))],
        out_specs=[pl.BlockSpec((rt, d), rows), pl.BlockSpec((rt, dv), rows),
                   pl.BlockSpec((per, HEADS, hv, hk), lambda i: (i, 0, 0, 0)), pl.BlockSpec((rt, zw), rows),
                   pl.BlockSpec((rt, d), rows), pl.BlockSpec((zw, d), lambda i: (0, 0))],
        out_shape=[_sds((t, d), F32), _sds((t, dv), BF16), _sds((t // CHUNK, HEADS, hv, hk), F32), _sds((t, zw), F32),
                   _sds((t, d), BF16), _sds((zw, d), BF16)],
        scratch_shapes=[pltpu.VMEM((HEADS, hv, hk), F32), pltpu.VMEM((rt, dk), F32), pltpu.VMEM((rt, dk), F32)],
        compiler_params=_cparams("arbitrary"), name=name)(h, gain, w_in_blocks, w_out, gate_w, gate_b, head_g)


def _gla_mid_bwd(z, h, gain, w_in_t, dh, w_out, states, gate_w, gate_b, head_g, *, after=None, name):
    t = z.shape[0]
    dk = gate_b.shape[1]
    hv = head_g.shape[1]
    hk = dk // HEADS
    dv = hv * HEADS
    r_at = 2 * dk + 2 * dv
    rt = _row_tile(t, 320)
    ntile = t // rt
    per = rt // CHUNK
    scale = hk ** -0.5
    dep_specs, deps = _dep_specs(after)

    def body(*refs):
        (z_ref, h_ref, g_ref, wi_ref, dh_ref, wo_ref, st_ref, stp_ref, gw_ref, gb_ref, hg_ref,
         dhi_ref, dg_ref, dz_ref, dgw_ref, dgb_ref, dhg_ref,
         ds_ref, la_ref, dla_ref, dec_ref, dos_ref, e_ref, do_ref) = refs[len(deps):]
        step = pl.program_id(0)
        tile = ntile - 1 - step
        do_ref[...] = _dot_nt(dh_ref[...].astype(BF16), wo_ref[...])

        @pl.when(step == 0)
        def _():
            ds_ref[...] = jnp.zeros_like(ds_ref)
            dgw_ref[...] = jnp.zeros_like(dgw_ref)
            dgb_ref[...] = jnp.zeros_like(dgb_ref)
            dhg_ref[...] = jnp.zeros_like(dhg_ref)

        rows_id = _row_ids(tile, rt)
        r = z_ref[:, r_at:r_at + GATE_PAD]
        gp, la = _log_decay(r, gw_ref, gb_ref, rows_id)
        la_ref[...] = la
        tri, tri_strict = _tri(False), _tri(True)
        keep = jnp.where(tile > 0, 1.0, 0.0)

        def chunk_rows(c):
            return slice(c * CHUNK, (c + 1) * CHUNK)

        def recompute(c, dhg):
            rows = chunk_rows(c)
            la_c = la_ref[rows, :]
            cum = _tri_dot(tri, la_c)
            dec_ref[rows, :] = jnp.exp(_rowsum(la_c) - cum)
            for hd in range(HEADS):
                q = (z_ref[rows, hd * hk:(hd + 1) * hk] * scale).astype(BF16)
                g = z_ref[rows, 2 * dk + dv + hd * hv:2 * dk + dv + (hd + 1) * hv]
                s_b = st_ref[c, hd].astype(BF16)
                o = _dot_nt(q, s_b)
                rstd = lax.rsqrt(jnp.mean(o * o, axis=-1, keepdims=True) + EPS)
                oh = o * rstd
                sg = _sigmoid(g)
                d_og = do_ref[rows, hd * hv:(hd + 1) * hv]
                dz_ref[rows, 2 * dk + dv + hd * hv:2 * dk + dv + (hd + 1) * hv] = (
                    d_og * oh * hg_ref[...] * (sg * (1.0 + g * (1.0 - sg)))).astype(BF16)
                don = d_og * (g * sg)
                dhg = dhg + _rowsum(don * oh)
                doh = don * hg_ref[...]
                d_o = (rstd * (doh - oh * jnp.mean(doh * oh, axis=-1, keepdims=True))).astype(BF16)
                dos_ref[rows, hd * hv:(hd + 1) * hv] = d_o
                dz_ref[rows, hd * hk:(hd + 1) * hk] = (_dot(d_o, s_b) * scale).astype(BF16)
            return dhg

        def recurrence(cc, carry):
            c = per - 1 - cc
            rows = chunk_rows(c)
            etot = jnp.exp(_rowsum(la_ref[rows, :]))
            for hd in range(HEADS):
                ks = slice(hd * hk, (hd + 1) * hk)
                q = (z_ref[rows, hd * hk:(hd + 1) * hk] * scale).astype(BF16)
                dec = dec_ref[rows, ks]
                kd = z_ref[rows, dk + hd * hk:dk + (hd + 1) * hk] * dec
                v = z_ref[rows, 2 * dk + hd * hv:2 * dk + (hd + 1) * hv].astype(BF16)
                s_prev = st_ref[c - 1, hd] if c > 0 else keep * stp_ref[0, hd]
                ds_t = ds_ref[hd] + _dot_tn(dos_ref[rows, hd * hv:(hd + 1) * hv], q)
                ds_b = ds_t.astype(BF16)
                dkd = _dot(v, ds_b)
                dz_ref[rows, 2 * dk + hd * hv:2 * dk + (hd + 1) * hv] = _dot_nt(kd.astype(BF16), ds_b).astype(BF16)
                dtot = etot[:, ks] * _rowsum(ds_t * s_prev)
                ds_ref[hd] = ds_t * etot[:, ks]
                dz_ref[rows, dk + hd * hk:dk + (hd + 1) * hk] = (dkd * dec).astype(BF16)
                e_ref[rows, ks] = dkd * kd
                dla_ref[rows, ks] = jnp.broadcast_to(dtot, (CHUNK, hk))
            return carry

        def decay_cotangent(c, carry):
            rows = chunk_rows(c)
            dla_ref[rows, :] += _tri_dot(tri_strict, e_ref[rows, :])
            return carry

        dhg = jnp.zeros((1, hv), F32)
        for c in range(per):
            dhg = recompute(c, dhg)
        dhg_ref[...] += dhg
        for phase in (recurrence, decay_cotangent):
            for c in range(per):
                phase(c, 0)
        dla = jnp.where(rows_id >= PAD_ROWS, dla_ref[...], 0.0)
        dgp = dla * (1.0 / GATE_NORM) * (1.0 - _sigmoid(gp))
        dgb_ref[...] += _rowsum(dgp)
        dgp_b = dgp.astype(BF16)
        dgw_ref[...] += _dot_tn(r.astype(BF16), dgp_b)
        dz_ref[:, r_at:r_at + GATE_PAD] = _dot_nt(dgp_b, gw_ref[...]).astype(BF16)
        _project_back(dz_ref, wi_ref, h_ref, g_ref, dh_ref, dhi_ref, dg_ref, step == 0)

    d = h.shape[1]
    back = lambda i: (ntile - 1 - i, 0)
    const2 = lambda i: (0, 0)
    return pl.pallas_call(
        body, grid=(ntile,),
        in_specs=dep_specs + [
                  pl.BlockSpec((rt, z.shape[1]), back), pl.BlockSpec((rt, d), back), _resident((1, d)),
                  _resident(w_in_t.shape), pl.BlockSpec((rt, d), back), _resident(w_out.shape),
                  pl.BlockSpec((per, HEADS, hv, hk), lambda i: (ntile - 1 - i, 0, 0, 0)),
                  pl.BlockSpec((1, HEADS, hv, hk), lambda i: (jnp.maximum((ntile - 1 - i) * per - 1, 0), 0, 0, 0)),
                  _resident(gate_w.shape), _resident((1, dk)), _resident((1, hv))],
        out_specs=[pl.BlockSpec((rt, d), back), pl.BlockSpec((1, d), const2), pl.BlockSpec((rt, z.shape[1]), back),
                   pl.BlockSpec(gate_w.shape, const2), pl.BlockSpec((1, dk), const2), pl.BlockSpec((1, hv), const2)],
        out_shape=[_sds((t, d), F32), _sds((1, d), F32), _sds(z.shape, BF16), _sds(gate_w.shape, F32),
                   _sds((1, dk), F32), _sds((1, hv), F32)],
        scratch_shapes=[pltpu.VMEM((HEADS, hv, hk), F32), pltpu.VMEM((rt, dk), F32), pltpu.VMEM((rt, dk), F32),
                        pltpu.VMEM((rt, dk), F32), pltpu.VMEM((rt, dv), BF16), pltpu.VMEM((rt, dk), F32),
                        pltpu.VMEM((rt, dv), F32)],
        compiler_params=_cparams("arbitrary"), name=name)(*deps, z, h, gain, w_in_t, dh, w_out, states, states, gate_w,
                                                          gate_b, head_g)


def _adamw_math(w, g, m, v):
    m = ADAM_B1 * m + (1.0 - ADAM_B1) * g
    v = ADAM_B2 * v + (1.0 - ADAM_B2) * (g * g)
    m_hat = m / (1.0 - ADAM_B1 ** ADAM_STEP)
    v_hat = v / (1.0 - ADAM_B2 ** ADAM_STEP)
    return -ADAM_LR * (m_hat / (jnp.sqrt(v_hat) + ADAM_EPS) + ADAM_WD * w), m, v


N_CHIP = N_DEV // 2
BLOCK_ELEMS = 128 * 1024


def _my_slot():
    return 4 * lax.axis_index("x") + 2 * lax.axis_index("y") + lax.axis_index("c")


def _row_block(r, c):
    cap = max(8, BLOCK_ELEMS // (-(-c // LANE) * LANE))
    return max([b for b in range(8, r + 1, 8) if r % b == 0 and b <= cap] or [r])


def _blocks(r, c):
    rb = _row_block(r, c)
    if rb < r or r * c <= BLOCK_ELEMS:
        return rb, c
    return r, max([b for b in (512, 256, LANE) if c % b == 0 and r * b <= BLOCK_ELEMS] or [c])


def _reduce_adam(parts, w, m, v, *, by_row=False, after=None, name):
    nl, r, c = (1, w.shape[0], w.shape[2]) if by_row else w.shape
    rb, cb = _blocks(r, c)
    dep_specs, deps = _dep_specs(after)
    whole = (slice(None), 0, slice(None)) if by_row else Ellipsis

    def body(*refs):
        me = refs[0][0]
        refs = refs[1 + len(deps):]
        p_refs = refs[:2 * nl]
        w_ref, m_ref, v_ref, g_out, d_out, m_out, v_out = refs[2 * nl:]
        layer = pl.program_id(0)
        for li in range(nl):
            @pl.when(layer == li)
            def _(li=li):
                own_ref, land_ref = p_refs[2 * li], p_refs[2 * li + 1]
                mine = own_ref[...].astype(F32)
                g = None
                for dev in range(N_DEV):
                    term = jnp.where(me == dev, mine, land_ref[dev].astype(F32))
                    g = term if g is None else g + term
                g_out[whole] = g
                d_out[whole], m_out[whole], v_out[whole] = _adamw_math(w_ref[whole], g, m_ref[whole], v_ref[whole])

    if by_row:
        blk = pl.BlockSpec((rb, 1, cb), lambda l, i, j, me: (i, 0, j))
    else:
        blk = pl.BlockSpec((None, rb, cb), lambda l, i, j, me: (l, i, j))
    p_specs = []
    for li in range(nl):
        p_specs += [
            pl.BlockSpec((None, rb, cb), lambda l, i, j, me, li=li: (me[0], jnp.where(l == li, i, 0), jnp.where(l == li, j, 0))),
            pl.BlockSpec((N_DEV, rb, cb), lambda l, i, j, me, li=li: (0, jnp.where(l == li, i, 0), jnp.where(l == li, j, 0)))]
    flat = [p for pair in parts for p in pair]
    grid_spec = pltpu.PrefetchScalarGridSpec(
        num_scalar_prefetch=1, grid=(nl, r // rb, c // cb), in_specs=dep_specs + p_specs + [blk, blk, blk],
        out_specs=[blk] * 4)
    return pl.pallas_call(
        body, grid_spec=grid_spec, out_shape=[_sds(w.shape, F32)] * 4,
        compiler_params=_cparams("arbitrary", "arbitrary", "arbitrary"), name=name)(
        _my_slot().reshape(1), *deps, *flat, w, m, v)


def _sum8(own, landed, *, name):
    def body(own_ref, land_ref, o_ref):
        me = _my_slot()
        total = None
        for dev in range(N_DEV):
            term = jnp.where(me == dev, own_ref[...], land_ref[dev])
            total = term if total is None else total + term
        o_ref[...] = total

    return pl.pallas_call(body, out_shape=_sds(own.shape, F32), name=name)(own, landed)


def _adam_small(own, landed, split, w, m, v, *, name):
    n = len(w)

    def body(*refs):
        own_refs, land_refs, w_refs, m_refs, v_refs = (refs[k * n:(k + 1) * n] for k in range(5))
        outs = refs[5 * n:]
        me = _my_slot()
        for k in range(n):
            mine = own_refs[k][me] if split[k] else own_refs[k][...]
            g = None
            for dev in range(N_DEV):
                term = jnp.where(me == dev, mine, land_refs[k][dev])
                g = term if g is None else g + term
            outs[4 * k][...] = g
            outs[4 * k + 1][...], outs[4 * k + 2][...], outs[4 * k + 3][...] = _adamw_math(
                w_refs[k][...], g, m_refs[k][...], v_refs[k][...])

    out = pl.pallas_call(body, out_shape=[_sds(a.shape, F32) for a in w for _ in range(4)],
                         compiler_params=pltpu.CompilerParams(vmem_limit_bytes=V7X_VMEM_LIMIT), name=name)(
        *own, *landed, *w, *m, *v)
    return [tuple(out[4 * k:4 * k + 4]) for k in range(n)]


_HBM = pl.BlockSpec(memory_space=pltpu.HBM)
_SEM = pl.BlockSpec(memory_space=pltpu.SEMAPHORE)
_DATAFLOW = pltpu.SideEffectType.DATAFLOW_SIDE_EFFECTING


def _plan_to_all(src, land):
    x, y, c = lax.axis_index("x"), lax.axis_index("y"), lax.axis_index("c")
    return [(src, land.at[_my_slot()], (x ^ ((d >> 2) & 1), y ^ ((d >> 1) & 1), c ^ (d & 1))) for d in range(1, N_DEV)]


def _plan_split_to_all(src, land):
    x, y, c = lax.axis_index("x"), lax.axis_index("y"), lax.axis_index("c")
    peers = [(x ^ ((d >> 2) & 1), y ^ ((d >> 1) & 1), c ^ (d & 1)) for d in range(1, N_DEV)]
    return [(src.at[4 * px + 2 * py + pc], land.at[_my_slot()], (px, py, pc)) for px, py, pc in peers]


_PLAN_COPIES = {_plan_to_all: N_DEV - 1, _plan_split_to_all: N_DEV - 1}


def _plans(plan, n):
    return list(plan) if isinstance(plan, (list, tuple)) else [plan] * n


def _exchange_copies(plan, ins, lands, send, recv):
    copies, sem = [], 0
    for p, src, land in zip(_plans(plan, len(lands)), ins, lands):
        for s, dst, dev in p(src, land):
            copies.append(pltpu.make_async_remote_copy(
                src_ref=s, dst_ref=dst, send_sem=send.at[sem], recv_sem=recv.at[sem],
                device_id=dev, device_id_type=pl.DeviceIdType.MESH))
            sem += 1
    return copies


def _place_own(srcs, *, after=None, name):
    dep_specs, deps = _dep_specs(after)
    n = len(srcs)
    arrays, in_specs, shapes = [], [], []
    for a, dtype in srcs:
        if isinstance(a, tuple):
            a, layer = a
            in_specs.append(pl.BlockSpec((None,) + a.shape[1:], lambda i, layer=layer: (layer, 0, 0)))
            shapes.append(a.shape[1:])
        else:
            in_specs.append(pl.BlockSpec(a.shape, lambda i: (0, 0)))
            shapes.append(a.shape)
        arrays.append(a)
    dtypes = [dtype for _, dtype in srcs]

    def body(*refs):
        refs = refs[len(deps):]
        a_refs, o_refs, cast_refs, sem = refs[:n], refs[n:2 * n], refs[2 * n:3 * n], refs[3 * n]
        me = _my_slot()
        copies = []
        for k in range(n):
            cast_refs[k][...] = a_refs[k][...].astype(dtypes[k])
            copies.append(pltpu.make_async_copy(cast_refs[k], o_refs[k].at[me], sem.at[k]))
            copies[-1].start()
        for cp in copies:
            cp.wait()

    return pl.pallas_call(
        body, grid=(1,), in_specs=dep_specs + in_specs, out_specs=[pl.BlockSpec(memory_space=pl.ANY)] * n,
        out_shape=[_sds((N_DEV,) + shape, dtype) for shape, dtype in zip(shapes, dtypes)],
        scratch_shapes=[pltpu.VMEM(shape, dtype) for shape, dtype in zip(shapes, dtypes)] + [pltpu.SemaphoreType.DMA((n,))],
        compiler_params=pltpu.CompilerParams(vmem_limit_bytes=V7X_VMEM_LIMIT), name=name)(*deps, *arrays)


def _plan_gather_first(land, _):
    x, y, c = lax.axis_index("x"), lax.axis_index("y"), lax.axis_index("c")
    mine = land.at[_my_slot()]
    return [(mine, mine, (x, y, 1 - c))] + [(mine, mine, (x ^ (d >> 1), y ^ (d & 1), c)) for d in range(1, N_CHIP)]


def _plan_gather_relay(land, _):
    x, y, c = lax.axis_index("x"), lax.axis_index("y"), lax.axis_index("c")
    slots = [land.at[4 * (x ^ (d >> 1)) + 2 * (y ^ (d & 1)) + c] for d in range(1, N_CHIP)]
    return [(s, s, (x, y, 1 - c)) for s in slots]


def _plan_gather_direct(land, _):
    return _plan_to_all(land.at[_my_slot()], land)


_PLAN_COPIES[_plan_gather_first] = N_CHIP
_PLAN_COPIES[_plan_gather_relay] = N_CHIP - 1
_PLAN_COPIES[_plan_gather_direct] = N_DEV - 1


def _exchange_start(plan, arrs, lands, *, after=None, name):
    bufs = list(lands) if arrs is None else list(arrs) + list(lands)
    n, nb = len(lands), len(bufs)
    nsem = sum(_PLAN_COPIES[p] for p in _plans(plan, n))
    dep_specs, deps = _dep_specs(after)

    def body(*refs):
        ins, land_refs = refs[:n], refs[nb - n:nb]
        send, recv = refs[nb + len(deps)], refs[nb + len(deps) + 1]
        for cp in _exchange_copies(plan, ins, land_refs, send, recv):
            cp.start()
        refs[-1][...] = jnp.zeros_like(refs[-1])

    out = pl.pallas_call(
        body, name=name,
        out_shape=(pltpu.SemaphoreType.DMA((nsem,)), pltpu.SemaphoreType.DMA((nsem,)),
                   *[pltpu.HBM(a.shape, a.dtype) for a in bufs], _sds((8, LANE), F32)),
        in_specs=[_HBM] * nb + dep_specs,
        out_specs=(_SEM, _SEM, *([_HBM] * nb), pl.BlockSpec(memory_space=pltpu.VMEM)),
        input_output_aliases={i: 2 + i for i in range(nb)},
        compiler_params=pltpu.CompilerParams(has_side_effects=_DATAFLOW),
    )(*[pltpu.with_memory_space_constraint(a, pltpu.HBM) for a in bufs], *deps)
    return (plan, n, out[0], out[1], list(out[2:2 + nb])), out[-1]


def _exchange_now(plan, lands, *, name):
    n = len(lands)
    nsem = sum(_PLAN_COPIES[p] for p in _plans(plan, n))

    def body(*refs):
        land_refs, send, recv = refs[n:2 * n], refs[2 * n], refs[2 * n + 1]
        copies = _exchange_copies(plan, land_refs, land_refs, send, recv)
        for cp in copies:
            cp.start()
        for cp in copies:
            cp.wait_send()
            cp.wait_recv()

    hbm = pl.BlockSpec(memory_space=pl.ANY)
    return pl.pallas_call(
        body, in_specs=[hbm] * n, out_specs=[hbm] * n, out_shape=[_sds(a.shape, a.dtype) for a in lands],
        input_output_aliases={i: i for i in range(n)},
        scratch_shapes=[pltpu.SemaphoreType.DMA((nsem,)), pltpu.SemaphoreType.DMA((nsem,))], name=name)(*lands)


def _exchange_wait(state, after, *, name):
    plan, n, send_sem, recv_sem, bufs = state
    nb = len(bufs)
    after = list(after) if isinstance(after, (list, tuple)) else [after]

    def body(*refs):
        ins, land_refs, send, recv = refs[:n], refs[nb - n:nb], refs[nb], refs[nb + 1]
        for cp in _exchange_copies(plan, ins, land_refs, send, recv):
            cp.wait_send()
            cp.wait_recv()

    out = pl.pallas_call(
        body, name=name, out_shape=[pltpu.HBM(a.shape, a.dtype) for a in bufs],
        in_specs=[_HBM] * nb + [_SEM, _SEM] + [pl.BlockSpec(memory_space=pl.ANY)] * len(after), out_specs=[_HBM] * nb,
        input_output_aliases={i: i for i in range(nb)},
        compiler_params=pltpu.CompilerParams(has_side_effects=_DATAFLOW),
    )(*bufs, send_sem, recv_sem, *after)
    return list(out[:n]), list(out[nb - n:])


def _dep_specs(after):
    return ([], []) if after is None else ([pl.BlockSpec(memory_space=pl.ANY)], [after])


def _undo_column_split(g):
    return jnp.transpose(g, (1, 0, 2)).reshape(g.shape[1], N_DEV * g.shape[2])


def _column_split(a):
    r, c = a.shape
    return jnp.transpose(a.reshape(r, N_DEV, c // N_DEV), (1, 0, 2))


class _WholeWeights:
    def __init__(self, groups):
        self.groups = groups
        self.grads = {}

    def fetch(self, group, after):
        return self.groups[group]

    def emit(self, group, grads):
        self.grads.update(grads)
        return None


def _local_step(x, target, replicated, src):
    d = x.shape[1]
    mix_g, ffn_g = replicated["mix_g"], replicated["ffn_g"]
    cp = src.fetch("cp", [])
    cp_mid = (cp["conv_w"], replicated["conv_b"], replicated["ln_g"], replicated["ln_b"], replicated["pool_w"],
              replicated["pool_scale"])

    h1, cat, z0, u0, cv0, h0 = _cp_mid_fwd(x, cp["meta"], mix_g[0:1], cp["cp_w_in_t"], cp["cp_w_out"], *cp_mid,
                                           name="cp_mixer")
    ffn0 = src.fetch("ffn0", h1)
    h2, uf0, rf0 = _ffn_fwd(h1, ffn_g[0:1], ffn0["w1"], ffn0["w2"], name="ffn0")
    gla = src.fetch("gla", h2)
    gla_mid = (gla["gate_w"], gla["gate_b"], gla["head_g"])
    h3, og, states, z1, u1, gla_w_in_t = _gla_mid_fwd(h2, mix_g[1:2], gla["gla_w_in_t"], gla["gla_w_out"], *gla_mid,
                                                      name="gla_mixer")
    ffn1 = src.fetch("ffn1", h3)
    dh4, uf1, rf1, loss, d_final_g = _ffn_fwd(h3, ffn_g[1:2], ffn1["w1"], ffn1["w2"],
                                              loss_head=(replicated["final_g"], target), name="ffn1_loss")

    dh3, dhh1, dob1, dffn_g1 = _ffn_bwd_x(h3, dh4, ffn_g[1:2], rf1, ffn1["w1"], ffn1["w2"], name="ffn1_bwd_x")
    sent = src.emit("ffn1", dict(w1=_linear_bwd_w(uf1, dhh1, column_blocks=N_DEV, name="ffn1_dw1"),
                                 w2=_linear_bwd_w(rf1, dob1, square_x=True, name="ffn1_dw2")))
    d_gla_w_out = _linear_bwd_w(og, dh3, name="gla_out_dw")
    dh2, dmix_g1, dz1, d_gate_w, d_gate_b, d_head_g = _gla_mid_bwd(
        z1, h2, mix_g[1:2], gla_w_in_t, dh3, gla["gla_w_out"], states, *gla_mid, after=sent, name="gla_mixer_bwd")
    d_gla_w_in_t = _linear_bwd_w(dz1, u1, row_blocks=gla["gla_w_in_t"].shape[:2], name="gla_in_dw")
    sent = src.emit("gla", dict(gla_w_in_t=d_gla_w_in_t, gla_w_out=d_gla_w_out))
    dh1, dhh0, dob0, dffn_g0 = _ffn_bwd_x(h1, dh2, ffn_g[0:1], rf0, ffn0["w1"], ffn0["w2"], after=sent, name="ffn0_bwd_x")
    sent = src.emit("ffn0_w1", dict(w1=_linear_bwd_w(uf0, dhh0, column_blocks=N_DEV, name="ffn0_dw1")))
    sent = src.emit("ffn0_w2", dict(w2=_linear_bwd_w(rf0, dob0, square_x=True, after=sent, name="ffn0_dw2")))
    d_cp_w_out = _linear_bwd_w(cat, dh1, after=sent, name="cp_out_dw")
    sent = src.emit("cp_out", dict(cp_w_out=d_cp_w_out))
    dx, dh0_first, dmix_g0, dz0, d_conv_w, d_conv_b, d_ln_g, d_ln_b, d_pool_w, d_pool_scale = _cp_mid_bwd(
        z0, cv0, h0, mix_g[0:1], cp["cp_w_in_t"], dh1, cp["cp_w_out"], *cp_mid, after=sent, name="cp_mixer_bwd")
    d_cp_w_in_t = _linear_bwd_w(dz0, u0, name="cp_in_dw")

    small = dict(
        mix_g=jnp.concatenate([dmix_g0, dmix_g1]), ffn_g=jnp.concatenate([dffn_g0, dffn_g1]), conv_b=d_conv_b, ln_g=d_ln_g,
        ln_b=d_ln_b, pool_w=d_pool_w, pool_scale=d_pool_scale, final_g=d_final_g, meta=dh0_first[PAD_ROWS:], conv_w=d_conv_w,
        gate_w=d_gate_w, gate_b=d_gate_b, head_g=d_head_g)
    src.emit("cp", dict(cp_w_in_t=d_cp_w_in_t, small=small, loss=loss))
    return loss, dx, small


_REPLICATED = ("mix_norm_g", "ffn_norm_g", "cp_conv_b", "cp_ln_g", "cp_ln_b", "cp_pool_w", "cp_pool_scale", "final_norm_g")
_SMALL_SHARDED = ("meta_tokens", "cp_conv_w", "gla_gate_w2", "gla_gate_b", "gla_head_g")
_NAMES = ("meta_tokens", "mix_norm_g", "ffn_norm_g", "ffn_w1", "ffn_w2", "cp_w_in", "cp_conv_w", "cp_conv_b", "cp_ln_g",
          "cp_ln_b", "cp_pool_w", "cp_pool_scale", "cp_w_out", "gla_w_in", "gla_gate_w2", "gla_gate_b", "gla_head_g",
          "gla_w_out", "final_norm_g")
_SMALL_GRADS = ("mix_g", "ffn_g", "conv_b", "ln_g", "ln_b", "pool_w", "pool_scale", "final_g", "meta", "conv_w", "gate_w",
                "gate_b", "head_g")
_GROUPS = ("cp", "ffn0", "gla", "ffn1")
_TWO_LEG_GATHERS = ("cp", "ffn0", "ffn1")


class _Exchanges:
    def __init__(self, w, d):
        self.d = d
        small = [w[n].reshape(w[n].shape[-2:]) for n in _SMALL_SHARDED]
        self.small_shard_shapes = [w[n].shape for n in _SMALL_SHARDED]
        shards = dict(
            cp=[(w["cp_w_in"][0].T, BF16), (w["cp_w_out"][0], BF16)] + [(a, F32) for a in small],
            ffn0=[((w["ffn_w1"], 0), BF16), ((w["ffn_w2"], 0), BF16)],
            gla=[(w["gla_w_in"][0].T, BF16), (w["gla_w_out"][0], BF16)],
            ffn1=[((w["ffn_w1"], 1), BF16), ((w["ffn_w2"], 1), BF16)])
        self.gathers = {}
        self.sent = {}
        token = None
        for group in _GROUPS:
            lands = _place_own(shards[group], after=token, name=f"place_w_{group}")
            plan = _plan_gather_first if group in _TWO_LEG_GATHERS else _plan_gather_direct
            self.gathers[group], token = _exchange_start(plan, None, lands, after=token, name=f"start_w_{group}")
        self.token = token

    def fetch(self, group, after):
        d = self.d
        after = (list(after) if isinstance(after, (list, tuple)) else [after]) + [self.token]
        _, got = _exchange_wait(self.gathers[group], after, name=f"wait_w_{group}")
        if group in _TWO_LEG_GATHERS:
            got = _exchange_now(_plan_gather_relay, got, name=f"relay_w_{group}")
        if group in ("ffn0", "ffn1"):
            return dict(w1=got[0], w2=got[1])
        if group == "gla":
            return dict(gla_w_in_t=got[0], gla_w_out=got[1].reshape(d, d), gate_w=self.gate_w, gate_b=self.gate_b,
                        head_g=self.head_g)
        meta, conv_w, gate_w, self.gate_b, self.head_g = [_undo_column_split(a) for a in got[2:]]
        self.gate_w = jnp.pad(gate_w, ((0, GATE_PAD - GATE_RANK), (0, 0))).astype(BF16)
        return dict(cp_w_in_t=got[0].reshape(-1, d), cp_w_out=got[1].reshape(d, d), meta=meta,
                    conv_w=jnp.pad(conv_w, ((0, 1), (0, 0))))

    def emit(self, group, g):
        d = self.d
        if group == "ffn1":
            arrs = [g["w1"], g["w2"].reshape(N_DEV, -1, d)]
        elif group == "ffn0_w1":
            arrs = [g["w1"]]
        elif group == "ffn0_w2":
            arrs = [g["w2"].reshape(N_DEV, -1, d)]
        elif group == "gla":
            arrs = [g["gla_w_in_t"], g["gla_w_out"].reshape(N_DEV, d // N_DEV, d)]
        elif group == "cp_out":
            arrs = [g["cp_w_out"].reshape(N_DEV, d // N_DEV, d)]
        else:
            s = dict(g["small"])
            s.update(pool_w=s["pool_w"][None], conv_w=s["conv_w"][:CONV_WIDTH], gate_w=s["gate_w"][:GATE_RANK])
            own = [s[n] for n in _SMALL_GRADS[:len(_REPLICATED)]]
            own += [_column_split(s[n]).reshape((N_DEV,) + shape)
                    for n, shape in zip(_SMALL_GRADS[len(_REPLICATED):], self.small_shard_shapes)]
            plans = [_plan_to_all] * len(_REPLICATED) + [_plan_split_to_all] * len(_SMALL_SHARDED)
            lands = [lax.empty((N_DEV,) + a.shape, F32) for a in own[:len(_REPLICATED)]]
            lands += [lax.empty(a.shape, F32) for a in own[len(_REPLICATED):]]
            own.append(g["loss"])
            plans.append(_plan_to_all)
            lands.append(lax.empty((N_DEV,) + g["loss"].shape, F32))
            own.append(g["cp_w_in_t"].reshape(N_DEV, -1, d))
            plans.append(_plan_split_to_all)
            lands.append(lax.empty(own[-1].shape, BF16))
            self.sent[group], self.token = _exchange_start(plans, own, lands, after=self.token, name=f"start_g_{group}")
            return self.token
        self.sent[group], self.token = _exchange_start(_plan_split_to_all, arrs, [lax.empty(a.shape, a.dtype) for a in arrs],
                                                       after=self.token, name=f"start_g_{group}")
        return self.token

    def finish(self, w, mom, var):
        out = {}
        after = self.token

        def landed(group):
            own, got = _exchange_wait(self.sent[group], after, name=f"wait_g_{group}")
            return list(zip(own, got))

        def adam(n, parts, behind=None, transposed=False):
            by_row = transposed and w[n].shape[2] % 8 != 0
            if by_row:
                flip, back = (lambda a: jnp.transpose(a, (2, 0, 1))), (lambda a: jnp.transpose(a, (1, 2, 0)))
            else:
                flip = back = (lambda a: jnp.transpose(a, (0, 2, 1))) if transposed else (lambda a: a)
            res = _reduce_adam(parts, flip(w[n]), flip(mom[n]), flip(var[n]), by_row=by_row, after=behind, name=f"adam_{n}")
            out[n] = tuple(back(a) for a in res)
            return res[0]

        ffn1 = landed("ffn1")
        after = ffn1[0][1]
        gla = landed("gla")
        after = adam("gla_w_in", [gla[0]], transposed=True)
        after = adam("gla_w_out", [gla[1]], after)
        ffn0_w1 = landed("ffn0_w1")
        after = adam("ffn_w1", [ffn0_w1[0], ffn1[0]])
        ffn0_w2 = landed("ffn0_w2")
        after = adam("ffn_w2", [ffn0_w2[0], ffn1[1]])
        cp_out = landed("cp_out")
        after = adam("cp_w_out", [cp_out[0]])
        *small, loss, cp_w_in = landed("cp")
        names = _REPLICATED + _SMALL_SHARDED
        split = [False] * len(_REPLICATED) + [True] * len(_SMALL_SHARDED)
        small_new = _adam_small([own for own, _ in small], [got for _, got in small], split, [w[n] for n in names],
                                [mom[n] for n in names], [var[n] for n in names], name="adam_small")
        out.update(zip(names, small_new))
        out["loss"] = _sum8(*loss, name="sum_loss")[0, 0]
        adam("cp_w_in", [cp_w_in], small_new[0][0], transposed=True)
        return out


def kernel(x, meta_tokens, mix_norm_g, ffn_norm_g, ffn_w1, ffn_w2, cp_w_in, cp_conv_w, cp_conv_b, cp_ln_g, cp_ln_b, cp_pool_w, cp_pool_scale, cp_w_out, gla_w_in, gla_gate_w2, gla_gate_b, gla_head_g, gla_w_out, final_norm_g, loss_target, m_meta_tokens, m_mix_norm_g, m_ffn_norm_g, m_ffn_w1, m_ffn_w2, m_cp_w_in, m_cp_conv_w, m_cp_conv_b, m_cp_ln_g, m_cp_ln_b, m_cp_pool_w, m_cp_pool_scale, m_cp_w_out, m_gla_w_in, m_gla_gate_w2, m_gla_gate_b, m_gla_head_g, m_gla_w_out, m_final_norm_g, v_meta_tokens, v_mix_norm_g, v_ffn_norm_g, v_ffn_w1, v_ffn_w2, v_cp_w_in, v_cp_conv_w, v_cp_conv_b, v_cp_ln_g, v_cp_ln_b, v_cp_pool_w, v_cp_pool_scale, v_cp_w_out, v_gla_w_in, v_gla_gate_w2, v_gla_gate_b, v_gla_head_g, v_gla_w_out, v_final_norm_g):
    w = dict(meta_tokens=meta_tokens, mix_norm_g=mix_norm_g, ffn_norm_g=ffn_norm_g, ffn_w1=ffn_w1, ffn_w2=ffn_w2,
             cp_w_in=cp_w_in, cp_conv_w=cp_conv_w, cp_conv_b=cp_conv_b, cp_ln_g=cp_ln_g, cp_ln_b=cp_ln_b,
             cp_pool_w=cp_pool_w, cp_pool_scale=cp_pool_scale, cp_w_out=cp_w_out, gla_w_in=gla_w_in,
             gla_gate_w2=gla_gate_w2, gla_gate_b=gla_gate_b, gla_head_g=gla_head_g, gla_w_out=gla_w_out,
             final_norm_g=final_norm_g.reshape(1, -1))
    mom = dict(meta_tokens=m_meta_tokens, mix_norm_g=m_mix_norm_g, ffn_norm_g=m_ffn_norm_g, ffn_w1=m_ffn_w1, ffn_w2=m_ffn_w2,
               cp_w_in=m_cp_w_in, cp_conv_w=m_cp_conv_w, cp_conv_b=m_cp_conv_b, cp_ln_g=m_cp_ln_g, cp_ln_b=m_cp_ln_b,
               cp_pool_w=m_cp_pool_w, cp_pool_scale=m_cp_pool_scale, cp_w_out=m_cp_w_out, gla_w_in=m_gla_w_in,
               gla_gate_w2=m_gla_gate_w2, gla_gate_b=m_gla_gate_b, gla_head_g=m_gla_head_g, gla_w_out=m_gla_w_out,
               final_norm_g=m_final_norm_g.reshape(1, -1))
    var = dict(meta_tokens=v_meta_tokens, mix_norm_g=v_mix_norm_g, ffn_norm_g=v_ffn_norm_g, ffn_w1=v_ffn_w1, ffn_w2=v_ffn_w2,
               cp_w_in=v_cp_w_in, cp_conv_w=v_cp_conv_w, cp_conv_b=v_cp_conv_b, cp_ln_g=v_cp_ln_g, cp_ln_b=v_cp_ln_b,
               cp_pool_w=v_cp_pool_w, cp_pool_scale=v_cp_pool_scale, cp_w_out=v_cp_w_out, gla_w_in=v_gla_w_in,
               gla_gate_w2=v_gla_gate_w2, gla_gate_b=v_gla_gate_b, gla_head_g=v_gla_head_g, gla_w_out=v_gla_w_out,
               final_norm_g=v_final_norm_g.reshape(1, -1))
    d = x.shape[-1]
    replicated = dict(mix_g=w["mix_norm_g"], ffn_g=w["ffn_norm_g"], conv_b=w["cp_conv_b"], ln_g=w["cp_ln_g"],
                      ln_b=w["cp_ln_b"], pool_w=w["cp_pool_w"][0].astype(BF16), pool_scale=w["cp_pool_scale"],
                      final_g=w["final_norm_g"])
    exchanges = _Exchanges(w, d)
    _, grad_x, _ = _local_step(x[0], loss_target[0], replicated, exchanges)
    out = exchanges.finish(w, mom, var)
    loss = out.pop("loss")

    def leaf(n, k):
        a = out[n][k]
        return a.reshape(-1) if n == "final_norm_g" else a

    return (loss, grad_x[None], *[leaf(n, 0) for n in _NAMES], *[leaf(n, 1) for n in _NAMES],
            *[leaf(n, 2) for n in _NAMES], *[leaf(n, 3) for n in _NAMES])
```

```python
import functools

import jax
import jax.numpy as jnp
from jax import lax
from jax.experimental import pallas as pl
from jax.experimental.pallas import tpu as pltpu

F32, BF16 = jnp.float32, jnp.bfloat16
N_DEV = 8
CHUNK = 64
N_META = 16
PAD_ROWS = CHUNK - N_META
HALO = 32
EPS = 1e-5
CONV_WIDTH = 31
POOL_WINDOWS = (2, 4, 8, 16)
HEADS = 4
GATE_RANK = 16
GATE_NORM = 16.0
GATE_PAD = 128
ADAM_LR, ADAM_B1, ADAM_B2, ADAM_EPS, ADAM_WD, ADAM_STEP = 0.001, 0.9, 0.999, 1e-08, 0.01, 10
V7X_VMEM_LIMIT = 56 * 2 ** 20
LANE = 128


def _cparams(*sem):
    return pltpu.CompilerParams(dimension_semantics=sem, vmem_limit_bytes=V7X_VMEM_LIMIT)


def _row_tile(t, cap):
    best = CHUNK
    for r in range(CHUNK, min(t, cap) + 1, CHUNK):
        if t % r == 0:
            best = r
    return best


def _resident(shape):
    return pl.BlockSpec(shape, lambda *_: (0,) * len(shape), pipeline_mode=pl.Buffered(1))


def _dot(a, b):
    return jnp.dot(a, b, preferred_element_type=F32)


def _dot_nt(a, b):
    return lax.dot_general(a, b, (((1,), (1,)), ((), ())), preferred_element_type=F32)


def _dot_tn(a, b):
    return lax.dot_general(a, b, (((0,), (0,)), ((), ())), preferred_element_type=F32)


def _rowsum(a):
    return jnp.sum(a, axis=0, keepdims=True)


def _sigmoid(a):
    return 1.0 / (1.0 + jnp.exp(-a))


def _row_ids(tile, rt):
    return tile * rt + lax.broadcasted_iota(jnp.int32, (rt, 1), 0)


def _sds(shape, dtype):
    return jax.ShapeDtypeStruct(shape, dtype)


DW_ROWS = 1024


def _linear_bwd_w(x, dy, *, square_x=False, column_blocks=None, row_blocks=None, after=None, name):
    t, k = x.shape
    n = dy.shape[1]
    cut_k = k > n and column_blocks is None
    assert cut_k or row_blocks is None
    width = k if cut_k else n
    blk = n // column_blocks if column_blocks else max(c for c in (640, 512, 384, 256, LANE) if width % c == 0)
    dep_specs, deps = _dep_specs(after)

    def body(*refs):
        x_ref, dy_ref, o_ref, acc = refs[len(deps):]
        for c0 in range(0, t, DW_ROWS):
            rows = slice(c0, min(c0 + DW_ROWS, t))
            xv = x_ref[rows, :]
            if square_x:
                xv = xv.astype(F32)
                xv = xv * xv
            part = _dot_tn(xv.astype(BF16), dy_ref[rows, :].astype(BF16))
            if c0 == 0:
                acc[...] = part
            else:
                acc[...] += part
        if row_blocks is None:
            o_ref[...] = acc[...].astype(BF16)
            return
        nb, rpb = row_blocks
        for s in range(width // blk):
            @pl.when(pl.program_id(0) == s)
            def _(s=s):
                for b in range(nb):
                    lo, hi = max(s * blk, b * rpb), min((s + 1) * blk, (b + 1) * rpb)
                    if lo < hi:
                        o_ref[b, lo - b * rpb:hi - b * rpb, :] = acc[lo - s * blk:hi - s * blk, :].astype(BF16)

    out_shape = _sds((k, n), BF16)
    semantics = "parallel"
    if cut_k:
        in_specs = [pl.BlockSpec((t, blk), lambda j: (0, j)), _resident((t, n))]
        out_specs = pl.BlockSpec((blk, n), lambda j: (j, 0))
        acc_shape = (blk, n)
        if row_blocks:
            out_shape = _sds(row_blocks + (n,), BF16)
            out_specs = pl.BlockSpec(out_shape.shape, lambda j: (0, 0, 0))
            semantics = "arbitrary"
    else:
        in_specs = [_resident((t, k)), pl.BlockSpec((t, blk), lambda j: (0, j))]
        out_specs = pl.BlockSpec((k, blk), lambda j: (0, j))
        acc_shape = (k, blk)
        if column_blocks:
            out_specs = pl.BlockSpec((None, k, blk), lambda j: (j, 0, 0))
            out_shape = _sds((column_blocks, k, blk), BF16)
    return pl.pallas_call(
        body, grid=(width // blk,), in_specs=dep_specs + in_specs, out_specs=out_specs, out_shape=out_shape,
        scratch_shapes=[pltpu.VMEM(acc_shape, F32)], compiler_params=_cparams(semantics), name=name)(*deps, x, dy)


FFN_BLOCKS_PER_STEP = 2


def _ffn_fwd(h, gain, w1g, w2g, *, loss_head=None, name):
    t, d = h.shape
    f8 = w1g.shape[-1]
    rt = _row_tile(t, 832)
    nb = FFN_BLOCKS_PER_STEP
    nstep = N_DEV // nb

    def body(*refs):
        if loss_head is None:
            h_ref, g_ref, w1_ref, w2_ref, o_ref, u_ref, r_ref, acc_ref = refs
        else:
            (h_ref, g_ref, w1_ref, w2_ref, fg_ref, tgt_ref, o_ref, u_ref, r_ref, loss_ref, dfg_ref, acc_ref, t_ref,
             t_sem) = refs
        i, j = pl.program_id(0), pl.program_id(1)

        def target_rows(act):
            @pl.when(i == 0)
            def _():
                act(pltpu.make_async_copy(tgt_ref.at[pl.ds(0, rt - CHUNK)], t_ref.at[pl.ds(CHUNK, rt - CHUNK)], t_sem.at[0]))

            if t > rt:
                @pl.when(i > 0)
                def _():
                    act(pltpu.make_async_copy(tgt_ref.at[pl.ds(pl.multiple_of(i * rt - CHUNK, CHUNK), rt)], t_ref,
                                              t_sem.at[0]))

        @pl.when(j == 0)
        def _():
            if loss_head is not None:
                @pl.when(i == 0)
                def _():
                    t_ref[0:CHUNK, :] = jnp.zeros((CHUNK, d), F32)

                target_rows(lambda copy: copy.start())

            hv = h_ref[...]
            u_ref[...] = (hv * lax.rsqrt(jnp.mean(hv * hv, axis=-1, keepdims=True) + EPS) * g_ref[...]).astype(BF16)
            acc_ref[...] = jnp.zeros_like(acc_ref)

        part = None
        for b in range(nb):
            a = jnp.maximum(_dot(u_ref[...], w1_ref[b]), 0.0)
            r_ref[:, b * f8:(b + 1) * f8] = a.astype(BF16)
            term = _dot((a * a).astype(BF16), w2_ref[b])
            part = term if part is None else part + term
        acc_ref[...] += part

        @pl.when(j == nstep - 1)
        def _():
            y = h_ref[...] + acc_ref[...]
            if loss_head is None:
                o_ref[...] = y
                return

            @pl.when(i == 0)
            def _():
                loss_ref[...] = jnp.zeros_like(loss_ref)
                dfg_ref[...] = jnp.zeros_like(dfg_ref)

            target_rows(lambda copy: copy.wait())

            rstd = lax.rsqrt(jnp.mean(y * y, axis=-1, keepdims=True) + EPS)
            xh = y * rstd
            err = jnp.where(_row_ids(i, rt) >= CHUNK, xh * fg_ref[...] - t_ref[...], 0.0)
            loss_ref[...] += (0.5 / d) * jnp.sum(err * err)
            dy = err * (1.0 / d)
            dfg_ref[...] += _rowsum(dy * xh)
            dxh = dy * fg_ref[...]
            o_ref[...] = rstd * (dxh - xh * jnp.mean(dxh * xh, axis=-1, keepdims=True))

    rows = lambda i, j: (i, 0)
    in_specs = [pl.BlockSpec((rt, d), rows), _resident((1, d)),
                pl.BlockSpec((nb, d, f8), lambda i, j: (j, 0, 0)), pl.BlockSpec((nb, f8, d), lambda i, j: (j, 0, 0))]
    out_specs = [pl.BlockSpec((rt, d), rows), pl.BlockSpec((rt, d), rows), pl.BlockSpec((rt, nb * f8), lambda i, j: (i, j))]
    out_shape = [_sds((t, d), F32), _sds((t, d), BF16), _sds((t, N_DEV * f8), BF16)]
    args = [h, gain, w1g, w2g]
    scratch_shapes = [pltpu.VMEM((rt, d), F32)]
    if loss_head is not None:
        in_specs += [_resident((1, d)), pl.BlockSpec(memory_space=pl.ANY)]
        out_specs += [pl.BlockSpec((8, LANE), lambda i, j: (0, 0)), pl.BlockSpec((1, d), lambda i, j: (0, 0))]
        out_shape += [_sds((8, LANE), F32), _sds((1, d), F32)]
        args += list(loss_head)
        scratch_shapes += [pltpu.VMEM((rt, d), F32), pltpu.SemaphoreType.DMA((1,))]
    return pl.pallas_call(
        body, grid=(t // rt, nstep), in_specs=in_specs, out_specs=out_specs, out_shape=out_shape,
        scratch_shapes=scratch_shapes,
        compiler_params=_cparams("arbitrary" if loss_head is not None else "parallel", "arbitrary"), name=name)(*args)


def _ffn_bwd_x(h, dout, gain, r, w1g, w2g, *, after=None, name):
    t, d = h.shape
    f8 = w1g.shape[-1]
    rt = _row_tile(t, 832)
    nb = FFN_BLOCKS_PER_STEP
    last = N_DEV // nb - 1
    dep_specs, deps = _dep_specs(after)

    def body(*refs):
        h_ref, do_ref, g_ref, r_ref, w1_ref, w2_ref, dh_ref, dhh_ref, dob_ref, dg_ref, du_ref = refs[len(deps):]
        i, j = pl.program_id(0), pl.program_id(1)

        @pl.when(j == 0)
        def _():
            dob_ref[...] = do_ref[...].astype(BF16)
            du_ref[...] = jnp.zeros_like(du_ref)

        part = None
        for b in range(nb):
            cols = slice(b * f8, (b + 1) * f8)
            dhh = (_dot_nt(dob_ref[...], w2_ref[b]) * (2.0 * r_ref[:, cols].astype(F32))).astype(BF16)
            dhh_ref[:, cols] = dhh
            term = _dot_nt(dhh, w1_ref[b])
            part = term if part is None else part + term
        du_ref[...] += part

        @pl.when(j == last)
        def _():
            @pl.when(i == 0)
            def _():
                dg_ref[...] = jnp.zeros_like(dg_ref)

            hv = h_ref[...]
            rstd = lax.rsqrt(jnp.mean(hv * hv, axis=-1, keepdims=True) + EPS)
            xh = hv * rstd
            du = du_ref[...]
            dg_ref[...] += _rowsum(du * xh)
            dxh = du * g_ref[...]
            dh_ref[...] = do_ref[...] + rstd * (dxh - xh * jnp.mean(dxh * xh, axis=-1, keepdims=True))

    rows = lambda i, j: (i, 0)
    return pl.pallas_call(
        body, grid=(t // rt, N_DEV // nb),
        in_specs=dep_specs + [
                  pl.BlockSpec((rt, d), rows), pl.BlockSpec((rt, d), rows), _resident((1, d)),
                  pl.BlockSpec((rt, nb * f8), lambda i, j: (i, j)),
                  pl.BlockSpec((nb, d, f8), lambda i, j: (j, 0, 0)),
                  pl.BlockSpec((nb, f8, d), lambda i, j: (j, 0, 0))],
        out_specs=[pl.BlockSpec((rt, d), rows), pl.BlockSpec((rt, nb * f8), lambda i, j: (i, j)),
                   pl.BlockSpec((rt, d), rows), pl.BlockSpec((1, d), lambda i, j: (0, 0))],
        out_shape=[_sds((t, d), F32), _sds((t, N_DEV * f8), BF16), _sds((t, d), BF16), _sds((1, d), F32)],
        scratch_shapes=[pltpu.VMEM((rt, d), F32)],
        compiler_params=_cparams("arbitrary", "arbitrary"), name=name)(*deps, h, dout, gain, r, w1g, w2g)


def _lane_blocks(width):
    lb = min(LANE, width)
    return [slice(s, s + lb) for s in range(0, width, lb)]


def _conv_rows(src_ref, w_ref, offset, dst_ref, nblk, width, bias_ref=None):
    def blk(rb, carry):
        base = pl.multiple_of(rb * CHUNK, CHUNK)
        for l, ls in enumerate(_lane_blocks(width)):
            acc = jnp.zeros((CHUNK, ls.stop - ls.start), F32)
            if bias_ref is not None:
                acc = acc + bias_ref[:, ls]
            for k in range(CONV_WIDTH):
                acc = acc + w_ref[k:k + 1, ls] * src_ref[l, pl.ds(base + offset(k), CHUNK), :]
            dst_ref[l, pl.ds(base, CHUNK), :] = acc
        return carry

    lax.fori_loop(0, nblk, blk, 0)


def _to_lane_blocks(ref, row0, value):
    for l, ls in enumerate(_lane_blocks(value.shape[1])):
        ref[l, row0:row0 + value.shape[0], :] = value[:, ls]


def _from_lane_blocks(ref):
    return jnp.concatenate([ref[l] for l in range(ref.shape[0])], axis=1)


def _pool_counts(rows, window):
    return jnp.clip(rows - PAD_ROWS + 1, 1, window).astype(F32)


def _trailing_sum(v, window):
    s, sh = v, 1
    while sh < window:
        s = s + pltpu.roll(s, sh, 0)
        sh *= 2
    return s


def _leading_sum(v, window):
    s, sh, n = v, 1, v.shape[0]
    while sh < window:
        s = s + pltpu.roll(s, n - sh, 0)
        sh *= 2
    return s


def _norm_project(h_ref, g_ref, w_t_ref, u_ref, z_ref):
    hv = h_ref[...]
    u = (hv * lax.rsqrt(jnp.mean(hv * hv, axis=-1, keepdims=True) + EPS) * g_ref[...]).astype(BF16)
    u_ref[...] = u
    z_ref[...] = _dot_nt(u, w_t_ref[...])


def _project_back(dz_ref, w_t_ref, h_ref, g_ref, dres_ref, dh_ref, dg_ref, first):
    dx = _dot(dz_ref[...], w_t_ref[...])
    hv = h_ref[...]
    rstd = lax.rsqrt(jnp.mean(hv * hv, axis=-1, keepdims=True) + EPS)
    xh = hv * rstd

    @pl.when(first)
    def _():
        dg_ref[...] = jnp.zeros_like(dg_ref)

    dg_ref[...] += _rowsum(dx * xh)
    dxh = dx * g_ref[...]
    dh_ref[...] = dres_ref[...] + rstd * (dxh - xh * jnp.mean(dxh * xh, axis=-1, keepdims=True))


def _cp_mid_fwd(x, meta, gain, w_in_t, w_out, conv_w, conv_b, ln_g, ln_b, pool_w, pool_scale, *, name):
    seq, d = x.shape
    t = seq + CHUNK
    ein = w_in_t.shape[0]
    cd = conv_b.shape[1]
    pd = pool_scale.shape[1]
    pg = pd // len(POOL_WINDOWS)
    rt = _row_tile(t, 320)
    ntile = t // rt

    def body(x_ref, meta_ref, g_ref, wi_ref, wo_ref, cw_ref, cb_ref, lg_ref, lb_ref, pw_ref, ps_ref,
             ho_ref, o_ref, z_ref, u_ref, cv_ref, h0_ref, gext, pext, conv_s, hbuf, hsem):
        i = pl.program_id(0)
        slot = i % 2
        first_rows = pltpu.make_async_copy(x_ref.at[pl.ds(0, rt - CHUNK)], hbuf.at[0, pl.ds(CHUNK, rt - CHUNK)], hsem.at[0])

        def tile_rows(tile, to):
            return pltpu.make_async_copy(x_ref.at[pl.ds(pl.multiple_of(tile * rt - CHUNK, CHUNK), rt)], hbuf.at[to],
                                         hsem.at[to])

        @pl.when(i == 0)
        def _():
            first_rows.start()
            hbuf[0, 0:PAD_ROWS, :] = jnp.zeros((PAD_ROWS, d), F32)
            hbuf[0, PAD_ROWS:CHUNK, :] = meta_ref[...]
            _to_lane_blocks(gext, 0, jnp.zeros((HALO, cd), F32))
            pext[0:HALO, :] = jnp.zeros((HALO, pd), F32)

        @pl.when(i + 1 < ntile)
        def _():
            tile_rows(i + 1, 1 - slot).start()

        @pl.when(i == 0)
        def _():
            first_rows.wait()

        @pl.when(i > 0)
        def _():
            tile_rows(i, slot).wait()

        h_ref = hbuf.at[slot]
        h0_ref[...] = h_ref[...]
        _norm_project(h_ref, g_ref, wi_ref, u_ref, z_ref)

        _to_lane_blocks(gext, HALO, z_ref[:, 0:cd] * _sigmoid(z_ref[:, cd:2 * cd]))
        pext[HALO:HALO + rt, :] = z_ref[:, 2 * cd:]
        _conv_rows(gext, cw_ref, lambda k: k + HALO - (CONV_WIDTH - 1), conv_s, rt // CHUNK, cd, cb_ref)
        cv = _from_lane_blocks(conv_s)
        cv_ref[...] = cv
        xc = cv - jnp.mean(cv, axis=-1, keepdims=True)
        y = xc * lax.rsqrt(jnp.mean(xc * xc, axis=-1, keepdims=True) + EPS) * lg_ref[...] + lb_ref[...]
        rows = _row_ids(i, rt)
        a = jnp.where(rows >= PAD_ROWS, y * _sigmoid(y), 0.0)
        o_ref[:, 0:cd] = a.astype(BF16)
        for gi, window in enumerate(POOL_WINDOWS):
            ls = slice(gi * pg, (gi + 1) * pg)
            v = pext[:, ls]
            tm = _trailing_sum(v, window)[HALO:] / _pool_counts(rows, window) - v[HALO:]
            p = _dot(tm.astype(BF16), pw_ref[gi]) * ps_ref[:, ls]
            o_ref[:, cd + gi * pg:cd + (gi + 1) * pg] = p.astype(BF16)
        ho_ref[...] = h_ref[...] + _dot(o_ref[...], wo_ref[...])
        gext[:, 0:HALO, :] = gext[:, rt:rt + HALO, :]
        pext[0:HALO, :] = pext[rt:rt + HALO, :]

    nl, lb = len(_lane_blocks(cd)), min(LANE, cd)
    rows = lambda i: (i, 0)
    return pl.pallas_call(
        body, grid=(ntile,),
        in_specs=[pl.BlockSpec(memory_space=pl.ANY), _resident(meta.shape), _resident((1, d)), _resident(w_in_t.shape),
                  _resident(w_out.shape), _resident(conv_w.shape), _resident((1, cd)),
                  _resident((1, cd)), _resident((1, cd)), _resident(pool_w.shape), _resident((1, pd))],
        out_specs=[pl.BlockSpec((rt, d), rows), pl.BlockSpec((rt, cd + pd), rows), pl.BlockSpec((rt, ein), rows),
                   pl.BlockSpec((rt, d), rows), pl.BlockSpec((rt, cd), rows), pl.BlockSpec((rt, d), rows)],
        out_shape=[_sds((t, d), F32), _sds((t, cd + pd), BF16), _sds((t, ein), F32), _sds((t, d), BF16),
                   _sds((t, cd), F32), _sds((t, d), F32)],
        scratch_shapes=[pltpu.VMEM((nl, rt + HALO, lb), F32), pltpu.VMEM((rt + HALO, pd), F32),
                        pltpu.VMEM((nl, rt, lb), F32), pltpu.VMEM((2, rt, d), F32), pltpu.SemaphoreType.DMA((2,))],
        compiler_params=_cparams("arbitrary"), name=name)(x, meta, gain, w_in_t, w_out, conv_w, conv_b, ln_g, ln_b, pool_w,
                                                          pool_scale)


def _cp_mid_bwd(z, cv, h, gain, w_in_t, dh, w_out, conv_w, conv_b, ln_g, ln_b, pool_w, pool_scale, *, after=None, name):
    t, ein = z.shape
    cd = conv_b.shape[1]
    pd = pool_scale.shape[1]
    pg = pd // len(POOL_WINDOWS)
    rt = _row_tile(t, 320)
    ntile = t // rt
    per = rt // CHUNK
    dep_specs, deps = _dep_specs(after)

    def body(*refs):
        (z_ref, zh_ref, cv_ref, h_ref, g_ref, wi_ref, dh_ref, wo_ref, cw_ref, cb_ref, lg_ref, lb_ref, pw_ref, ps_ref,
         dx_ref, dfirst_ref, dg_ref, dz_ref, dcw_ref, dcb_ref, dlg_ref, dlb_ref, dpw_ref, dps_ref,
         gext, pext, conv_s, dcv, dsp, dhi, dx_sem) = refs[len(deps):]
        step = pl.program_id(0)
        tile = ntile - 1 - step
        slot = step % 2
        last_slot = (ntile - 1) % 2
        first_rows = pltpu.make_async_copy(dhi.at[last_slot, pl.ds(CHUNK, rt - CHUNK)], dx_ref.at[pl.ds(0, rt - CHUNK)],
                                           dx_sem.at[last_slot])

        def tile_rows(tile, slot):
            return pltpu.make_async_copy(dhi.at[slot], dx_ref.at[pl.ds(pl.multiple_of(tile * rt - CHUNK, CHUNK), rt)],
                                         dx_sem.at[slot])

        dcat = _dot_nt(dh_ref[...].astype(BF16), wo_ref[...])

        @pl.when(step >= 2)
        def _():
            tile_rows(tile + 2, slot).wait()

        @pl.when(step == 0)
        def _():
            for ref in (dcw_ref, dcb_ref, dlg_ref, dlb_ref, dpw_ref, dps_ref):
                ref[...] = jnp.zeros_like(ref)
            _to_lane_blocks(dcv, rt, jnp.zeros((HALO, cd), F32))
            dsp[rt:rt + HALO, :] = jnp.zeros((HALO, pd), F32)

        keep = jnp.where(tile > 0, 1.0, 0.0)
        zh = zh_ref[CHUNK - HALO:CHUNK, :]
        _to_lane_blocks(gext, 0, keep * zh[:, 0:cd] * _sigmoid(zh[:, cd:2 * cd]))
        pext[0:HALO, :] = keep * zh[:, 2 * cd:]
        za = z_ref[:, 0:cd]
        sg = _sigmoid(z_ref[:, cd:2 * cd])
        _to_lane_blocks(gext, HALO, za * sg)
        pext[HALO:HALO + rt, :] = z_ref[:, 2 * cd:]
        cv = cv_ref[...]
        xc = cv - jnp.mean(cv, axis=-1, keepdims=True)
        rstd = lax.rsqrt(jnp.mean(xc * xc, axis=-1, keepdims=True) + EPS)
        xh = xc * rstd
        y = xh * lg_ref[...] + lb_ref[...]
        sy = _sigmoid(y)
        rows = _row_ids(tile, rt)
        da = jnp.where(rows >= PAD_ROWS, dcat[:, 0:cd], 0.0)
        dy = da * (sy * (1.0 + y * (1.0 - sy)))
        dlg_ref[...] += _rowsum(dy * xh)
        dlb_ref[...] += _rowsum(dy)
        dxh = dy * lg_ref[...]
        dconv = rstd * (dxh - jnp.mean(dxh, axis=-1, keepdims=True) - xh * jnp.mean(dxh * xh, axis=-1, keepdims=True))
        dcb_ref[...] += _rowsum(dconv)
        _to_lane_blocks(dcv, 0, dconv)
        for l, ls in enumerate(_lane_blocks(cd)):
            def acc_rows(rb, accs, l=l):
                base = pl.multiple_of(rb * CHUNK, CHUNK)
                d_blk = dcv[l, pl.ds(base, CHUNK), :]
                out = []
                for k in range(CONV_WIDTH):
                    prod = d_blk * gext[l, pl.ds(base + k + HALO - (CONV_WIDTH - 1), CHUNK), :]
                    part = prod[0:8]
                    for s in range(8, CHUNK, 8):
                        part = part + prod[s:s + 8]
                    out.append(accs[k] + part)
                return tuple(out)

            zero = jnp.zeros((8, ls.stop - ls.start), F32)
            accs = lax.fori_loop(0, per, acc_rows, (zero,) * CONV_WIDTH)
            for k in range(CONV_WIDTH):
                dcw_ref[k:k + 1, ls] += _rowsum(accs[k])
        _conv_rows(dcv, cw_ref, lambda k: CONV_WIDTH - 1 - k, conv_s, per, cd)
        dglu = _from_lane_blocks(conv_s)
        dz_ref[:, 0:cd] = (dglu * sg).astype(BF16)
        dz_ref[:, cd:2 * cd] = (dglu * za * sg * (1.0 - sg)).astype(BF16)
        dcv[:, rt:rt + HALO, :] = dcv[:, 0:HALO, :]
        for gi, window in enumerate(POOL_WINDOWS):
            ls = slice(gi * pg, (gi + 1) * pg)
            v = pext[:, ls]
            cnt = _pool_counts(rows, window)
            tm = (_trailing_sum(v, window)[HALO:] / cnt - v[HALO:]).astype(BF16)
            dp = dcat[:, cd + gi * pg:cd + (gi + 1) * pg]
            dps_ref[:, ls] += _rowsum(dp * _dot(tm, pw_ref[gi]))
            dpl = (dp * ps_ref[:, ls]).astype(BF16)
            dpw_ref[gi] += _dot_tn(tm, dpl)
            dtm = _dot_nt(dpl, pw_ref[gi])
            dsp[0:rt, ls] = dtm / cnt
            dpin = _leading_sum(dsp[:, ls], window)[0:rt] - dtm
            dz_ref[:, 2 * cd + gi * pg:2 * cd + (gi + 1) * pg] = dpin.astype(BF16)
        dsp[rt:rt + HALO, :] = dsp[0:HALO, :]

        _project_back(dz_ref, wi_ref, h_ref, g_ref, dh_ref, dhi.at[slot], dg_ref, step == 0)

        @pl.when(tile > 0)
        def _():
            tile_rows(tile, slot).start()

        @pl.when(tile == 0)
        def _():
            first_rows.start()
            dfirst_ref[...] = dhi[last_slot, 0:CHUNK, :]
            if ntile > 1:
                tile_rows(1, 1 - last_slot).wait()
            first_rows.wait()

    d = h.shape[1]
    back = lambda i: (ntile - 1 - i, 0)
    halo_idx = lambda i: (jnp.maximum((ntile - 1 - i) * per - 1, 0), 0)
    const2 = lambda i: (0, 0)
    nl, lb = len(_lane_blocks(cd)), min(LANE, cd)
    return pl.pallas_call(
        body, grid=(ntile,),
        in_specs=dep_specs + [
                  pl.BlockSpec((rt, ein), back), pl.BlockSpec((CHUNK, ein), halo_idx), pl.BlockSpec((rt, cd), back),
                  pl.BlockSpec((rt, d), back),
                  _resident((1, d)), _resident(w_in_t.shape), pl.BlockSpec((rt, d), back), _resident(w_out.shape),
                  _resident(conv_w.shape), _resident((1, cd)), _resident((1, cd)), _resident((1, cd)),
                  _resident(pool_w.shape), _resident((1, pd))],
        out_specs=[pl.BlockSpec(memory_space=pl.ANY), pl.BlockSpec((CHUNK, d), const2), pl.BlockSpec((1, d), const2),
                   pl.BlockSpec((rt, ein), back), pl.BlockSpec(conv_w.shape, const2), pl.BlockSpec((1, cd), const2),
                   pl.BlockSpec((1, cd), const2), pl.BlockSpec((1, cd), const2),
                   pl.BlockSpec(pool_w.shape, lambda i: (0, 0, 0)), pl.BlockSpec((1, pd), const2)],
        out_shape=[_sds((t - CHUNK, d), F32), _sds((CHUNK, d), F32), _sds((1, d), F32),
                   _sds((t, ein), BF16), _sds(conv_w.shape, F32), _sds((1, cd), F32), _sds((1, cd), F32),
                   _sds((1, cd), F32), _sds(pool_w.shape, F32), _sds((1, pd), F32)],
        scratch_shapes=[pltpu.VMEM((nl, rt + HALO, lb), F32), pltpu.VMEM((rt + HALO, pd), F32), pltpu.VMEM((nl, rt, lb), F32),
                        pltpu.VMEM((nl, rt + HALO, lb), F32), pltpu.VMEM((rt + HALO, pd), F32), pltpu.VMEM((2, rt, d), F32),
                        pltpu.SemaphoreType.DMA((2,))],
        compiler_params=_cparams("arbitrary"), name=name)(*deps, z, z, cv, h, gain, w_in_t, dh, w_out, conv_w, conv_b, ln_g,
                                                          ln_b, pool_w, pool_scale)


def _log_decay(r, gw_ref, gb_ref, rows):
    gp = _dot(r.astype(BF16), gw_ref[...]) + gb_ref[...]
    log_sig = jnp.minimum(gp, 0.0) - jnp.log(1.0 + jnp.exp(-jnp.abs(gp)))
    return gp, jnp.where(rows >= PAD_ROWS, log_sig / GATE_NORM, 0.0)


def _tri(strict):
    r = lax.broadcasted_iota(jnp.int32, (CHUNK, CHUNK), 0)
    c = lax.broadcasted_iota(jnp.int32, (CHUNK, CHUNK), 1)
    return jnp.where(c < r if strict else c <= r, 1.0, 0.0).astype(BF16)


def _tri_dot(tri, a):
    hi = a.astype(BF16)
    rest = a - hi.astype(F32)
    mid = rest.astype(BF16)
    lo = (rest - mid.astype(F32)).astype(BF16)
    return _dot(tri, hi) + _dot(tri, mid) + _dot(tri, lo)


def _gla_mid_fwd(h, gain, w_in_blocks, w_out, gate_w, gate_b, head_g, *, name):
    t = h.shape[0]
    nblk, rpb = w_in_blocks.shape[:2]
    dk = gate_b.shape[1]
    hv = head_g.shape[1]
    hk = dk // HEADS
    dv = hv * HEADS
    r_at = 2 * dk + 2 * dv
    assert nblk * rpb == r_at + GATE_RANK
    zw = r_at + GATE_PAD
    rt = _row_tile(t, 320)
    per = rt // CHUNK
    scale = hk ** -0.5

    def body(h_ref, g_ref, wb_ref, wo_ref, gw_ref, gb_ref, hg_ref, ho_ref, o_ref, st_ref, z_ref, u_ref, wi_ref,
             s_ref, la_ref, dec_ref):
        i = pl.program_id(0)

        @pl.when(i == 0)
        def _():
            s_ref[...] = jnp.zeros_like(s_ref)
            for b in range(nblk):
                wi_ref[b * rpb:(b + 1) * rpb, :] = wb_ref[b]
            wi_ref[nblk * rpb:, :] = jnp.zeros((zw - nblk * rpb, wi_ref.shape[1]), BF16)

        _norm_project(h_ref, g_ref, wi_ref, u_ref, z_ref)

        _, la = _log_decay(z_ref[:, r_at:r_at + GATE_PAD], gw_ref, gb_ref, _row_ids(i, rt))
        la_ref[...] = la
        tri = _tri(False)

        def chunk_rows(c):
            return slice(c * CHUNK, (c + 1) * CHUNK)

        def decays(c, carry):
            rows = chunk_rows(c)
            la_c = la_ref[rows, :]
            cum = _tri_dot(tri, la_c)
            dec_ref[rows, :] = jnp.exp(_rowsum(la_c) - cum)
            return carry

        def states(c, carry):
            rows = chunk_rows(c)
            etot = jnp.exp(_rowsum(la_ref[rows, :]))
            for hd in range(HEADS):
                ks = slice(hd * hk, (hd + 1) * hk)
                kd = z_ref[rows, dk + hd * hk:dk + (hd + 1) * hk] * dec_ref[rows, ks]
                v = z_ref[rows, 2 * dk + hd * hv:2 * dk + (hd + 1) * hv]
                s_new = s_ref[hd] * etot[:, ks] + _dot_tn(v.astype(BF16), kd.astype(BF16))
                s_ref[hd] = s_new
                st_ref[c, hd] = s_new
            return carry

        def outputs(c, carry):
            rows = chunk_rows(c)
            for hd in range(HEADS):
                q = z_ref[rows, hd * hk:(hd + 1) * hk] * scale
                g = z_ref[rows, 2 * dk + dv + hd * hv:2 * dk + dv + (hd + 1) * hv]
                o = _dot_nt(q.astype(BF16), st_ref[c, hd].astype(BF16))
                on = o * lax.rsqrt(jnp.mean(o * o, axis=-1, keepdims=True) + EPS) * hg_ref[...]
                o_ref[rows, hd * hv:(hd + 1) * hv] = (on * (g * _sigmoid(g))).astype(BF16)
            return carry

        for phase in (decays, states, outputs):
            for c in range(per):
                phase(c, 0)
        ho_ref[...] = h_ref[...] + _dot(o_ref[...], wo_ref[...])

    d = h.shape[1]
    rows = lambda i: (i, 0)
    return pl.pallas_call(
        body, grid=(t // rt,),
        in_specs=[pl.BlockSpec((rt, d), rows), _resident((1, d)), _resident(w_in_blocks.shape), _resident(w_out.shape),
                  _resident(gate_w.shape), _resident((1, dk)), _resident((1, hv))],
        out_specs=[pl.BlockSpec((rt, d), rows), pl.BlockSpec((rt, dv), rows),
                   pl.BlockSpec((per, HEADS, hv, hk), lambda i: (i, 0, 0, 0)), pl.BlockSpec((rt, zw), rows),
                   pl.BlockSpec((rt, d), rows), pl.BlockSpec((zw, d), lambda i: (0, 0))],
        out_shape=[_sds((t, d), F32), _sds((t, dv), BF16), _sds((t // CHUNK, HEADS, hv, hk), F32), _sds((t, zw), F32),
                   _sds((t, d), BF16), _sds((zw, d), BF16)],
        scratch_shapes=[pltpu.VMEM((HEADS, hv, hk), F32), pltpu.VMEM((rt, dk), F32), pltpu.VMEM((rt, dk), F32)],
        compiler_params=_cparams("arbitrary"), name=name)(h, gain, w_in_blocks, w_out, gate_w, gate_b, head_g)


def _gla_mid_bwd(z, h, gain, w_in_t, dh, w_out, states, gate_w, gate_b, head_g, *, after=None, name):
    t = z.shape[0]
    dk = gate_b.shape[1]
    hv = head_g.shape[1]
    hk = dk // HEADS
    dv = hv * HEADS
    r_at = 2 * dk + 2 * dv
    rt = _row_tile(t, 320)
    ntile = t // rt
    per = rt // CHUNK
    scale = hk ** -0.5
    dep_specs, deps = _dep_specs(after)

    def body(*refs):
        (z_ref, h_ref, g_ref, wi_ref, dh_ref, wo_ref, st_ref, stp_ref, gw_ref, gb_ref, hg_ref,
         dhi_ref, dg_ref, dz_ref, dgw_ref, dgb_ref, dhg_ref,
         ds_ref, la_ref, dla_ref, dec_ref, dos_ref, e_ref, do_ref) = refs[len(deps):]
        step = pl.program_id(0)
        tile = ntile - 1 - step
        do_ref[...] = _dot_nt(dh_ref[...].astype(BF16), wo_ref[...])

        @pl.when(step == 0)
        def _():
            ds_ref[...] = jnp.zeros_like(ds_ref)
            dgw_ref[...] = jnp.zeros_like(dgw_ref)
            dgb_ref[...] = jnp.zeros_like(dgb_ref)
            dhg_ref[...] = jnp.zeros_like(dhg_ref)

        rows_id = _row_ids(tile, rt)
        r = z_ref[:, r_at:r_at + GATE_PAD]
        gp, la = _log_decay(r, gw_ref, gb_ref, rows_id)
        la_ref[...] = la
        tri, tri_strict = _tri(False), _tri(True)
        keep = jnp.where(tile > 0, 1.0, 0.0)

        def chunk_rows(c):
            return slice(c * CHUNK, (c + 1) * CHUNK)

        def recompute(c, dhg):
            rows = chunk_rows(c)
            la_c = la_ref[rows, :]
            cum = _tri_dot(tri, la_c)
            dec_ref[rows, :] = jnp.exp(_rowsum(la_c) - cum)
            for hd in range(HEADS):
                q = (z_ref[rows, hd * hk:(hd + 1) * hk] * scale).astype(BF16)
                g = z_ref[rows, 2 * dk + dv + hd * hv:2 * dk + dv + (hd + 1) * hv]
                s_b = st_ref[c, hd].astype(BF16)
                o = _dot_nt(q, s_b)
                rstd = lax.rsqrt(jnp.mean(o * o, axis=-1, keepdims=True) + EPS)
                oh = o * rstd
                sg = _sigmoid(g)
                d_og = do_ref[rows, hd * hv:(hd + 1) * hv]
                dz_ref[rows, 2 * dk + dv + hd * hv:2 * dk + dv + (hd + 1) * hv] = (
                    d_og * oh * hg_ref[...] * (sg * (1.0 + g * (1.0 - sg)))).astype(BF16)
                don = d_og * (g * sg)
                dhg = dhg + _rowsum(don * oh)
                doh = don * hg_ref[...]
                d_o = (rstd * (doh - oh * jnp.mean(doh * oh, axis=-1, keepdims=True))).astype(BF16)
                dos_ref[rows, hd * hv:(hd + 1) * hv] = d_o
                dz_ref[rows, hd * hk:(hd + 1) * hk] = (_dot(d_o, s_b) * scale).astype(BF16)
            return dhg

        def recurrence(cc, carry):
            c = per - 1 - cc
            rows = chunk_rows(c)
            etot = jnp.exp(_rowsum(la_ref[rows, :]))
            for hd in range(HEADS):
                ks = slice(hd * hk, (hd + 1) * hk)
                q = (z_ref[rows, hd * hk:(hd + 1) * hk] * scale).astype(BF16)
                dec = dec_ref[rows, ks]
                kd = z_ref[rows, dk + hd * hk:dk + (hd + 1) * hk] * dec
                v = z_ref[rows, 2 * dk + hd * hv:2 * dk + (hd + 1) * hv].astype(BF16)
                s_prev = st_ref[c - 1, hd] if c > 0 else keep * stp_ref[0, hd]
                ds_t = ds_ref[hd] + _dot_tn(dos_ref[rows, hd * hv:(hd + 1) * hv], q)
                ds_b = ds_t.astype(BF16)
                dkd = _dot(v, ds_b)
                dz_ref[rows, 2 * dk + hd * hv:2 * dk + (hd + 1) * hv] = _dot_nt(kd.astype(BF16), ds_b).astype(BF16)
                dtot = etot[:, ks] * _rowsum(ds_t * s_prev)
                ds_ref[hd] = ds_t * etot[:, ks]
                dz_ref[rows, dk + hd * hk:dk + (hd + 1) * hk] = (dkd * dec).astype(BF16)
                e_ref[rows, ks] = dkd * kd
                dla_ref[rows, ks] = jnp.broadcast_to(dtot, (CHUNK, hk))
            return carry

        def decay_cotangent(c, carry):
            rows = chunk_rows(c)
            dla_ref[rows, :] += _tri_dot(tri_strict, e_ref[rows, :])
            return carry

        dhg = jnp.zeros((1, hv), F32)
        for c in range(per):
            dhg = recompute(c, dhg)
        dhg_ref[...] += dhg
        for phase in (recurrence, decay_cotangent):
            for c in range(per):
                phase(c, 0)
        dla = jnp.where(rows_id >= PAD_ROWS, dla_ref[...], 0.0)
        dgp = dla * (1.0 / GATE_NORM) * (1.0 - _sigmoid(gp))
        dgb_ref[...] += _rowsum(dgp)
        dgp_b = dgp.astype(BF16)
        dgw_ref[...] += _dot_tn(r.astype(BF16), dgp_b)
        dz_ref[:, r_at:r_at + GATE_PAD] = _dot_nt(dgp_b, gw_ref[...]).astype(BF16)
        _project_back(dz_ref, wi_ref, h_ref, g_ref, dh_ref, dhi_ref, dg_ref, step == 0)

    d = h.shape[1]
    back = lambda i: (ntile - 1 - i, 0)
    const2 = lambda i: (0, 0)
    return pl.pallas_call(
        body, grid=(ntile,),
        in_specs=dep_specs + [
                  pl.BlockSpec((rt, z.shape[1]), back), pl.BlockSpec((rt, d), back), _resident((1, d)),
                  _resident(w_in_t.shape), pl.BlockSpec((rt, d), back), _resident(w_out.shape),
                  pl.BlockSpec((per, HEADS, hv, hk), lambda i: (ntile - 1 - i, 0, 0, 0)),
                  pl.BlockSpec((1, HEADS, hv, hk), lambda i: (jnp.maximum((ntile - 1 - i) * per - 1, 0), 0, 0, 0)),
                  _resident(gate_w.shape), _resident((1, dk)), _resident((1, hv))],
        out_specs=[pl.BlockSpec((rt, d), back), pl.BlockSpec((1, d), const2), pl.BlockSpec((rt, z.shape[1]), back),
                   pl.BlockSpec(gate_w.shape, const2), pl.BlockSpec((1, dk), const2), pl.BlockSpec((1, hv), const2)],
        out_shape=[_sds((t, d), F32), _sds((1, d), F32), _sds(z.shape, BF16), _sds(gate_w.shape, F32),
                   _sds((1, dk), F32), _sds((1, hv), F32)],
        scratch_shapes=[pltpu.VMEM((HEADS, hv, hk), F32), pltpu.VMEM((rt, dk), F32), pltpu.VMEM((rt, dk), F32),
                        pltpu.VMEM((rt, dk), F32), pltpu.VMEM((rt, dv), BF16), pltpu.VMEM((rt, dk), F32),
                        pltpu.VMEM((rt, dv), F32)],
        compiler_params=_cparams("arbitrary"), name=name)(*deps, z, h, gain, w_in_t, dh, w_out, states, states, gate_w,
                                                          gate_b, head_g)


def _adamw_math(w, g, m, v):
    m = ADAM_B1 * m + (1.0 - ADAM_B1) * g
    v = ADAM_B2 * v + (1.0 - ADAM_B2) * (g * g)
    m_hat = m / (1.0 - ADAM_B1 ** ADAM_STEP)
    v_hat = v / (1.0 - ADAM_B2 ** ADAM_STEP)
    return -ADAM_LR * (m_hat / (jnp.sqrt(v_hat) + ADAM_EPS) + ADAM_WD * w), m, v


N_CHIP = N_DEV // 2
BLOCK_ELEMS = 128 * 1024


def _my_slot():
    return 4 * lax.axis_index("x") + 2 * lax.axis_index("y") + lax.axis_index("c")


def _row_block(r, c):
    cap = max(8, BLOCK_ELEMS // (-(-c // LANE) * LANE))
    return max([b for b in range(8, r + 1, 8) if r % b == 0 and b <= cap] or [r])


def _blocks(r, c):
    rb = _row_block(r, c)
    if rb < r or r * c <= BLOCK_ELEMS:
        return rb, c
    return r, max([b for b in (512, 256, LANE) if c % b == 0 and r * b <= BLOCK_ELEMS] or [c])


def _reduce_adam(parts, w, m, v, *, by_row=False, after=None, name):
    nl, r, c = (1, w.shape[0], w.shape[2]) if by_row else w.shape
    rb, cb = _blocks(r, c)
    dep_specs, deps = _dep_specs(after)
    whole = (slice(None), 0, slice(None)) if by_row else Ellipsis

    def body(*refs):
        me = refs[0][0]
        refs = refs[1 + len(deps):]
        p_refs = refs[:2 * nl]
        w_ref, m_ref, v_ref, g_out, d_out, m_out, v_out = refs[2 * nl:]
        layer = pl.program_id(0)
        for li in range(nl):
            @pl.when(layer == li)
            def _(li=li):
                own_ref, land_ref = p_refs[2 * li], p_refs[2 * li + 1]
                mine = own_ref[...].astype(F32)
                g = None
                for dev in range(N_DEV):
                    term = jnp.where(me == dev, mine, land_ref[dev].astype(F32))
                    g = term if g is None else g + term
                g_out[whole] = g
                d_out[whole], m_out[whole], v_out[whole] = _adamw_math(w_ref[whole], g, m_ref[whole], v_ref[whole])

    if by_row:
        blk = pl.BlockSpec((rb, 1, cb), lambda l, i, j, me: (i, 0, j))
    else:
        blk = pl.BlockSpec((None, rb, cb), lambda l, i, j, me: (l, i, j))
    p_specs = []
    for li in range(nl):
        p_specs += [
            pl.BlockSpec((None, rb, cb), lambda l, i, j, me, li=li: (me[0], jnp.where(l == li, i, 0), jnp.where(l == li, j, 0))),
            pl.BlockSpec((N_DEV, rb, cb), lambda l, i, j, me, li=li: (0, jnp.where(l == li, i, 0), jnp.where(l == li, j, 0)))]
    flat = [p for pair in parts for p in pair]
    grid_spec = pltpu.PrefetchScalarGridSpec(
        num_scalar_prefetch=1, grid=(nl, r // rb, c // cb), in_specs=dep_specs + p_specs + [blk, blk, blk],
        out_specs=[blk] * 4)
    return pl.pallas_call(
        body, grid_spec=grid_spec, out_shape=[_sds(w.shape, F32)] * 4,
        compiler_params=_cparams("arbitrary", "arbitrary", "arbitrary"), name=name)(
        _my_slot().reshape(1), *deps, *flat, w, m, v)


def _sum8(own, landed, *, name):
    def body(own_ref, land_ref, o_ref):
        me = _my_slot()
        total = None
        for dev in range(N_DEV):
            term = jnp.where(me == dev, own_ref[...], land_ref[dev])
            total = term if total is None else total + term
        o_ref[...] = total

    return pl.pallas_call(body, out_shape=_sds(own.shape, F32), name=name)(own, landed)


def _adam_small(own, landed, split, w, m, v, *, name):
    n = len(w)

    def body(*refs):
        own_refs, land_refs, w_refs, m_refs, v_refs = (refs[k * n:(k + 1) * n] for k in range(5))
        outs = refs[5 * n:]
        me = _my_slot()
        for k in range(n):
            mine = own_refs[k][me] if split[k] else own_refs[k][...]
            g = None
            for dev in range(N_DEV):
                term = jnp.where(me == dev, mine, land_refs[k][dev])
                g = term if g is None else g + term
            outs[4 * k][...] = g
            outs[4 * k + 1][...], outs[4 * k + 2][...], outs[4 * k + 3][...] = _adamw_math(
                w_refs[k][...], g, m_refs[k][...], v_refs[k][...])

    out = pl.pallas_call(body, out_shape=[_sds(a.shape, F32) for a in w for _ in range(4)],
                         compiler_params=pltpu.CompilerParams(vmem_limit_bytes=V7X_VMEM_LIMIT), name=name)(
        *own, *landed, *w, *m, *v)
    return [tuple(out[4 * k:4 * k + 4]) for k in range(n)]


_HBM = pl.BlockSpec(memory_space=pltpu.HBM)
_SEM = pl.BlockSpec(memory_space=pltpu.SEMAPHORE)
_DATAFLOW = pltpu.SideEffectType.DATAFLOW_SIDE_EFFECTING


def _plan_to_all(src, land):
    x, y, c = lax.axis_index("x"), lax.axis_index("y"), lax.axis_index("c")
    return [(src, land.at[_my_slot()], (x ^ ((d >> 2) & 1), y ^ ((d >> 1) & 1), c ^ (d & 1))) for d in range(1, N_DEV)]


def _plan_split_to_all(src, land):
    x, y, c = lax.axis_index("x"), lax.axis_index("y"), lax.axis_index("c")
    peers = [(x ^ ((d >> 2) & 1), y ^ ((d >> 1) & 1), c ^ (d & 1)) for d in range(1, N_DEV)]
    return [(src.at[4 * px + 2 * py + pc], land.at[_my_slot()], (px, py, pc)) for px, py, pc in peers]


_PLAN_COPIES = {_plan_to_all: N_DEV - 1, _plan_split_to_all: N_DEV - 1}


def _plans(plan, n):
    return list(plan) if isinstance(plan, (list, tuple)) else [plan] * n


def _exchange_copies(plan, ins, lands, send, recv):
    copies, sem = [], 0
    for p, src, land in zip(_plans(plan, len(lands)), ins, lands):
        for s, dst, dev in p(src, land):
            copies.append(pltpu.make_async_remote_copy(
                src_ref=s, dst_ref=dst, send_sem=send.at[sem], recv_sem=recv.at[sem],
                device_id=dev, device_id_type=pl.DeviceIdType.MESH))
            sem += 1
    return copies


def _place_own(srcs, *, after=None, name):
    dep_specs, deps = _dep_specs(after)
    n = len(srcs)
    arrays, in_specs, shapes = [], [], []
    for a, dtype in srcs:
        if isinstance(a, tuple):
            a, layer = a
            in_specs.append(pl.BlockSpec((None,) + a.shape[1:], lambda i, layer=layer: (layer, 0, 0)))
            shapes.append(a.shape[1:])
        else:
            in_specs.append(pl.BlockSpec(a.shape, lambda i: (0, 0)))
            shapes.append(a.shape)
        arrays.append(a)
    dtypes = [dtype for _, dtype in srcs]

    def body(*refs):
        refs = refs[len(deps):]
        a_refs, o_refs, cast_refs, sem = refs[:n], refs[n:2 * n], refs[2 * n:3 * n], refs[3 * n]
        me = _my_slot()
        copies = []
        for k in range(n):
            cast_refs[k][...] = a_refs[k][...].astype(dtypes[k])
            copies.append(pltpu.make_async_copy(cast_refs[k], o_refs[k].at[me], sem.at[k]))
            copies[-1].start()
        for cp in copies:
            cp.wait()

    return pl.pallas_call(
        body, grid=(1,), in_specs=dep_specs + in_specs, out_specs=[pl.BlockSpec(memory_space=pl.ANY)] * n,
        out_shape=[_sds((N_DEV,) + shape, dtype) for shape, dtype in zip(shapes, dtypes)],
        scratch_shapes=[pltpu.VMEM(shape, dtype) for shape, dtype in zip(shapes, dtypes)] + [pltpu.SemaphoreType.DMA((n,))],
        compiler_params=pltpu.CompilerParams(vmem_limit_bytes=V7X_VMEM_LIMIT), name=name)(*deps, *arrays)


def _plan_gather_first(land, _):
    x, y, c = lax.axis_index("x"), lax.axis_index("y"), lax.axis_index("c")
    mine = land.at[_my_slot()]
    return [(mine, mine, (x, y, 1 - c))] + [(mine, mine, (x ^ (d >> 1), y ^ (d & 1), c)) for d in range(1, N_CHIP)]


def _plan_gather_relay(land, _):
    x, y, c = lax.axis_index("x"), lax.axis_index("y"), lax.axis_index("c")
    slots = [land.at[4 * (x ^ (d >> 1)) + 2 * (y ^ (d & 1)) + c] for d in range(1, N_CHIP)]
    return [(s, s, (x, y, 1 - c)) for s in slots]


def _plan_gather_direct(land, _):
    return _plan_to_all(land.at[_my_slot()], land)


_PLAN_COPIES[_plan_gather_first] = N_CHIP
_PLAN_COPIES[_plan_gather_relay] = N_CHIP - 1
_PLAN_COPIES[_plan_gather_direct] = N_DEV - 1


def _exchange_start(plan, arrs, lands, *, after=None, name):
    bufs = list(lands) if arrs is None else list(arrs) + list(lands)
    n, nb = len(lands), len(bufs)
    nsem = sum(_PLAN_COPIES[p] for p in _plans(plan, n))
    dep_specs, deps = _dep_specs(after)

    def body(*refs):
        ins, land_refs = refs[:n], refs[nb - n:nb]
        send, recv = refs[nb + len(deps)], refs[nb + len(deps) + 1]
        for cp in _exchange_copies(plan, ins, land_refs, send, recv):
            cp.start()
        refs[-1][...] = jnp.zeros_like(refs[-1])

    out = pl.pallas_call(
        body, name=name,
        out_shape=(pltpu.SemaphoreType.DMA((nsem,)), pltpu.SemaphoreType.DMA((nsem,)),
                   *[pltpu.HBM(a.shape, a.dtype) for a in bufs], _sds((8, LANE), F32)),
        in_specs=[_HBM] * nb + dep_specs,
        out_specs=(_SEM, _SEM, *([_HBM] * nb), pl.BlockSpec(memory_space=pltpu.VMEM)),
        input_output_aliases={i: 2 + i for i in range(nb)},
        compiler_params=pltpu.CompilerParams(has_side_effects=_DATAFLOW),
    )(*[pltpu.with_memory_space_constraint(a, pltpu.HBM) for a in bufs], *deps)
    return (plan, n, out[0], out[1], list(out[2:2 + nb])), out[-1]


def _exchange_now(plan, lands, *, name):
    n = len(lands)
    nsem = sum(_PLAN_COPIES[p] for p in _plans(plan, n))

    def body(*refs):
        land_refs, send, recv = refs[n:2 * n], refs[2 * n], refs[2 * n + 1]
        copies = _exchange_copies(plan, land_refs, land_refs, send, recv)
        for cp in copies:
            cp.start()
        for cp in copies:
            cp.wait_send()
            cp.wait_recv()

    hbm = pl.BlockSpec(memory_space=pl.ANY)
    return pl.pallas_call(
        body, in_specs=[hbm] * n, out_specs=[hbm] * n, out_shape=[_sds(a.shape, a.dtype) for a in lands],
        input_output_aliases={i: i for i in range(n)},
        scratch_shapes=[pltpu.SemaphoreType.DMA((nsem,)), pltpu.SemaphoreType.DMA((nsem,))], name=name)(*lands)


def _exchange_wait(state, after, *, name):
    plan, n, send_sem, recv_sem, bufs = state
    nb = len(bufs)
    after = list(after) if isinstance(after, (list, tuple)) else [after]

    def body(*refs):
        ins, land_refs, send, recv = refs[:n], refs[nb - n:nb], refs[nb], refs[nb + 1]
        for cp in _exchange_copies(plan, ins, land_refs, send, recv):
            cp.wait_send()
            cp.wait_recv()

    out = pl.pallas_call(
        body, name=name, out_shape=[pltpu.HBM(a.shape, a.dtype) for a in bufs],
        in_specs=[_HBM] * nb + [_SEM, _SEM] + [pl.BlockSpec(memory_space=pl.ANY)] * len(after), out_specs=[_HBM] * nb,
        input_output_aliases={i: i for i in range(nb)},
        compiler_params=pltpu.CompilerParams(has_side_effects=_DATAFLOW),
    )(*bufs, send_sem, recv_sem, *after)
    return list(out[:n]), list(out[nb - n:])


def _dep_specs(after):
    return ([], []) if after is None else ([pl.BlockSpec(memory_space=pl.ANY)], [after])


def _undo_column_split(g):
    return jnp.transpose(g, (1, 0, 2)).reshape(g.shape[1], N_DEV * g.shape[2])


def _column_split(a):
    r, c = a.shape
    return jnp.transpose(a.reshape(r, N_DEV, c // N_DEV), (1, 0, 2))


class _WholeWeights:
    def __init__(self, groups):
        self.groups = groups
        self.grads = {}

    def fetch(self, group, after):
        return self.groups[group]

    def emit(self, group, grads):
        self.grads.update(grads)
        return None


def _local_step(x, target, replicated, src):
    d = x.shape[1]
    mix_g, ffn_g = replicated["mix_g"], replicated["ffn_g"]
    cp = src.fetch("cp", [])
    cp_mid = (cp["conv_w"], replicated["conv_b"], replicated["ln_g"], replicated["ln_b"], replicated["pool_w"],
              replicated["pool_scale"])

    h1, cat, z0, u0, cv0, h0 = _cp_mid_fwd(x, cp["meta"], mix_g[0:1], cp["cp_w_in_t"], cp["cp_w_out"], *cp_mid,
                                           name="cp_mixer")
    ffn0 = src.fetch("ffn0", h1)
    h2, uf0, rf0 = _ffn_fwd(h1, ffn_g[0:1], ffn0["w1"], ffn0["w2"], name="ffn0")
    gla = src.fetch("gla", h2)
    gla_mid = (gla["gate_w"], gla["gate_b"], gla["head_g"])
    h3, og, states, z1, u1, gla_w_in_t = _gla_mid_fwd(h2, mix_g[1:2], gla["gla_w_in_t"], gla["gla_w_out"], *gla_mid,
                                                      name="gla_mixer")
    ffn1 = src.fetch("ffn1", h3)
    dh4, uf1, rf1, loss, d_final_g = _ffn_fwd(h3, ffn_g[1:2], ffn1["w1"], ffn1["w2"],
                                              loss_head=(replicated["final_g"], target), name="ffn1_loss")

    dh3, dhh1, dob1, dffn_g1 = _ffn_bwd_x(h3, dh4, ffn_g[1:2], rf1, ffn1["w1"], ffn1["w2"], name="ffn1_bwd_x")
    sent = src.emit("ffn1", dict(w1=_linear_bwd_w(uf1, dhh1, column_blocks=N_DEV, name="ffn1_dw1"),
                                 w2=_linear_bwd_w(rf1, dob1, square_x=True, name="ffn1_dw2")))
    d_gla_w_out = _linear_bwd_w(og, dh3, name="gla_out_dw")
    dh2, dmix_g1, dz1, d_gate_w, d_gate_b, d_head_g = _gla_mid_bwd(
        z1, h2, mix_g[1:2], gla_w_in_t, dh3, gla["gla_w_out"], states, *gla_mid, after=sent, name="gla_mixer_bwd")
    d_gla_w_in_t = _linear_bwd_w(dz1, u1, row_blocks=gla["gla_w_in_t"].shape[:2], name="gla_in_dw")
    sent = src.emit("gla", dict(gla_w_in_t=d_gla_w_in_t, gla_w_out=d_gla_w_out))
    dh1, dhh0, dob0, dffn_g0 = _ffn_bwd_x(h1, dh2, ffn_g[0:1], rf0, ffn0["w1"], ffn0["w2"], after=sent, name="ffn0_bwd_x")
    sent = src.emit("ffn0_w1", dict(w1=_linear_bwd_w(uf0, dhh0, column_blocks=N_DEV, name="ffn0_dw1")))
    sent = src.emit("ffn0_w2_cp_out", dict(w2=_linear_bwd_w(rf0, dob0, square_x=True, after=sent, name="ffn0_dw2"),
                                           cp_w_out=_linear_bwd_w(cat, dh1, name="cp_out_dw")))
    dx, dh0_first, dmix_g0, dz0, d_conv_w, d_conv_b, d_ln_g, d_ln_b, d_pool_w, d_pool_scale = _cp_mid_bwd(
        z0, cv0, h0, mix_g[0:1], cp["cp_w_in_t"], dh1, cp["cp_w_out"], *cp_mid, after=sent, name="cp_mixer_bwd")
    d_cp_w_in_t = _linear_bwd_w(dz0, u0, name="cp_in_dw")

    small = dict(
        mix_g=jnp.concatenate([dmix_g0, dmix_g1]), ffn_g=jnp.concatenate([dffn_g0, dffn_g1]), conv_b=d_conv_b, ln_g=d_ln_g,
        ln_b=d_ln_b, pool_w=d_pool_w, pool_scale=d_pool_scale, final_g=d_final_g, meta=dh0_first[PAD_ROWS:], conv_w=d_conv_w,
        gate_w=d_gate_w, gate_b=d_gate_b, head_g=d_head_g)
    src.emit("cp", dict(cp_w_in_t=d_cp_w_in_t, small=small, loss=loss))
    return loss, dx, small


_REPLICATED = ("mix_norm_g", "ffn_norm_g", "cp_conv_b", "cp_ln_g", "cp_ln_b", "cp_pool_w", "cp_pool_scale", "final_norm_g")
_SMALL_SHARDED = ("meta_tokens", "cp_conv_w", "gla_gate_w2", "gla_gate_b", "gla_head_g")
_NAMES = ("meta_tokens", "mix_norm_g", "ffn_norm_g", "ffn_w1", "ffn_w2", "cp_w_in", "cp_conv_w", "cp_conv_b", "cp_ln_g",
          "cp_ln_b", "cp_pool_w", "cp_pool_scale", "cp_w_out", "gla_w_in", "gla_gate_w2", "gla_gate_b", "gla_head_g",
          "gla_w_out", "final_norm_g")
_SMALL_GRADS = ("mix_g", "ffn_g", "conv_b", "ln_g", "ln_b", "pool_w", "pool_scale", "final_g", "meta", "conv_w", "gate_w",
                "gate_b", "head_g")
_GROUPS = ("cp", "ffn0", "gla", "ffn1")
_TWO_LEG_GATHERS = ("cp", "ffn0", "ffn1")


class _Exchanges:
    def __init__(self, w, d):
        self.d = d
        small = [w[n].reshape(w[n].shape[-2:]) for n in _SMALL_SHARDED]
        self.small_shard_shapes = [w[n].shape for n in _SMALL_SHARDED]
        shards = dict(
            cp=[(w["cp_w_in"][0].T, BF16), (w["cp_w_out"][0], BF16)] + [(a, F32) for a in small],
            ffn0=[((w["ffn_w1"], 0), BF16), ((w["ffn_w2"], 0), BF16)],
            gla=[(w["gla_w_in"][0].T, BF16), (w["gla_w_out"][0], BF16)],
            ffn1=[((w["ffn_w1"], 1), BF16), ((w["ffn_w2"], 1), BF16)])
        self.gathers = {}
        self.sent = {}
        token = None
        for group in _GROUPS:
            lands = _place_own(shards[group], after=token, name=f"place_w_{group}")
            plan = _plan_gather_first if group in _TWO_LEG_GATHERS else _plan_gather_direct
            self.gathers[group], token = _exchange_start(plan, None, lands, after=token, name=f"start_w_{group}")
        self.token = token

    def fetch(self, group, after):
        d = self.d
        after = (list(after) if isinstance(after, (list, tuple)) else [after]) + [self.token]
        _, got = _exchange_wait(self.gathers[group], after, name=f"wait_w_{group}")
        if group in _TWO_LEG_GATHERS:
            got = _exchange_now(_plan_gather_relay, got, name=f"relay_w_{group}")
        if group in ("ffn0", "ffn1"):
            return dict(w1=got[0], w2=got[1])
        if group == "gla":
            return dict(gla_w_in_t=got[0], gla_w_out=got[1].reshape(d, d), gate_w=self.gate_w, gate_b=self.gate_b,
                        head_g=self.head_g)
        meta, conv_w, gate_w, self.gate_b, self.head_g = [_undo_column_split(a) for a in got[2:]]
        self.gate_w = jnp.pad(gate_w, ((0, GATE_PAD - GATE_RANK), (0, 0))).astype(BF16)
        return dict(cp_w_in_t=got[0].reshape(-1, d), cp_w_out=got[1].reshape(d, d), meta=meta,
                    conv_w=jnp.pad(conv_w, ((0, 1), (0, 0))))

    def emit(self, group, g):
        d = self.d
        if group == "ffn1":
            arrs = [g["w1"], g["w2"].reshape(N_DEV, -1, d)]
        elif group == "ffn0_w1":
            arrs = [g["w1"]]
        elif group == "ffn0_w2_cp_out":
            arrs = [g["w2"].reshape(N_DEV, -1, d), g["cp_w_out"].reshape(N_DEV, d // N_DEV, d)]
        elif group == "gla":
            arrs = [g["gla_w_in_t"], g["gla_w_out"].reshape(N_DEV, d // N_DEV, d)]
        else:
            s = dict(g["small"])
            s.update(pool_w=s["pool_w"][None], conv_w=s["conv_w"][:CONV_WIDTH], gate_w=s["gate_w"][:GATE_RANK])
            own = [s[n] for n in _SMALL_GRADS[:len(_REPLICATED)]]
            own += [_column_split(s[n]).reshape((N_DEV,) + shape)
                    for n, shape in zip(_SMALL_GRADS[len(_REPLICATED):], self.small_shard_shapes)]
            plans = [_plan_to_all] * len(_REPLICATED) + [_plan_split_to_all] * len(_SMALL_SHARDED)
            lands = [lax.empty((N_DEV,) + a.shape, F32) for a in own[:len(_REPLICATED)]]
            lands += [lax.empty(a.shape, F32) for a in own[len(_REPLICATED):]]
            own.append(g["loss"])
            plans.append(_plan_to_all)
            lands.append(lax.empty((N_DEV,) + g["loss"].shape, F32))
            own.append(g["cp_w_in_t"].reshape(N_DEV, -1, d))
            plans.append(_plan_split_to_all)
            lands.append(lax.empty(own[-1].shape, BF16))
            self.sent[group], self.token = _exchange_start(plans, own, lands, after=self.token, name=f"start_g_{group}")
            return self.token
        self.sent[group], self.token = _exchange_start(_plan_split_to_all, arrs, [lax.empty(a.shape, a.dtype) for a in arrs],
                                                       after=self.token, name=f"start_g_{group}")
        return self.token

    def finish(self, w, mom, var):
        out = {}
        after = self.token

        def landed(group):
            own, got = _exchange_wait(self.sent[group], after, name=f"wait_g_{group}")
            return list(zip(own, got))

        def adam(n, parts, behind=None, transposed=False):
            by_row = transposed and w[n].shape[2] % 8 != 0
            if by_row:
                flip, back = (lambda a: jnp.transpose(a, (2, 0, 1))), (lambda a: jnp.transpose(a, (1, 2, 0)))
            else:
                flip = back = (lambda a: jnp.transpose(a, (0, 2, 1))) if transposed else (lambda a: a)
            res = _reduce_adam(parts, flip(w[n]), flip(mom[n]), flip(var[n]), by_row=by_row, after=behind, name=f"adam_{n}")
            out[n] = tuple(back(a) for a in res)
            return res[0]

        ffn1 = landed("ffn1")
        after = ffn1[0][1]
        gla = landed("gla")
        after = adam("gla_w_in", [gla[0]], transposed=True)
        after = adam("gla_w_out", [gla[1]], after)
        ffn0_w1 = landed("ffn0_w1")
        after = adam("ffn_w1", [ffn0_w1[0], ffn1[0]])
        ffn0_w2, cp_w_out = landed("ffn0_w2_cp_out")
        after = adam("ffn_w2", [ffn0_w2, ffn1[1]])
        after = adam("cp_w_out", [cp_w_out], after)
        *small, loss, cp_w_in = landed("cp")
        names = _REPLICATED + _SMALL_SHARDED
        split = [False] * len(_REPLICATED) + [True] * len(_SMALL_SHARDED)
        small_new = _adam_small([own for own, _ in small], [got for _, got in small], split, [w[n] for n in names],
                                [mom[n] for n in names], [var[n] for n in names], name="adam_small")
        out.update(zip(names, small_new))
        out["loss"] = _sum8(*loss, name="sum_loss")[0, 0]
        adam("cp_w_in", [cp_w_in], small_new[0][0], transposed=True)
        return out


def kernel(x, meta_tokens, mix_norm_g, ffn_norm_g, ffn_w1, ffn_w2, cp_w_in, cp_conv_w, cp_conv_b, cp_ln_g, cp_ln_b, cp_pool_w, cp_pool_scale, cp_w_out, gla_w_in, gla_gate_w2, gla_gate_b, gla_head_g, gla_w_out, final_norm_g, loss_target, m_meta_tokens, m_mix_norm_g, m_ffn_norm_g, m_ffn_w1, m_ffn_w2, m_cp_w_in, m_cp_conv_w, m_cp_conv_b, m_cp_ln_g, m_cp_ln_b, m_cp_pool_w, m_cp_pool_scale, m_cp_w_out, m_gla_w_in, m_gla_gate_w2, m_gla_gate_b, m_gla_head_g, m_gla_w_out, m_final_norm_g, v_meta_tokens, v_mix_norm_g, v_ffn_norm_g, v_ffn_w1, v_ffn_w2, v_cp_w_in, v_cp_conv_w, v_cp_conv_b, v_cp_ln_g, v_cp_ln_b, v_cp_pool_w, v_cp_pool_scale, v_cp_w_out, v_gla_w_in, v_gla_gate_w2, v_gla_gate_b, v_gla_head_g, v_gla_w_out, v_final_norm_g):
    w = dict(meta_tokens=meta_tokens, mix_norm_g=mix_norm_g, ffn_norm_g=ffn_norm_g, ffn_w1=ffn_w1, ffn_w2=ffn_w2,
             cp_w_in=cp_w_in, cp_conv_w=cp_conv_w, cp_conv_b=cp_conv_b, cp_ln_g=cp_ln_g, cp_ln_b=cp_ln_b,
             cp_pool_w=cp_pool_w, cp_pool_scale=cp_pool_scale, cp_w_out=cp_w_out, gla_w_in=gla_w_in,
             gla_gate_w2=gla_gate_w2, gla_gate_b=gla_gate_b, gla_head_g=gla_head_g, gla_w_out=gla_w_out,
             final_norm_g=final_norm_g.reshape(1, -1))
    mom = dict(meta_tokens=m_meta_tokens, mix_norm_g=m_mix_norm_g, ffn_norm_g=m_ffn_norm_g, ffn_w1=m_ffn_w1, ffn_w2=m_ffn_w2,
               cp_w_in=m_cp_w_in, cp_conv_w=m_cp_conv_w, cp_conv_b=m_cp_conv_b, cp_ln_g=m_cp_ln_g, cp_ln_b=m_cp_ln_b,
               cp_pool_w=m_cp_pool_w, cp_pool_scale=m_cp_pool_scale, cp_w_out=m_cp_w_out, gla_w_in=m_gla_w_in,
               gla_gate_w2=m_gla_gate_w2, gla_gate_b=m_gla_gate_b, gla_head_g=m_gla_head_g, gla_w_out=m_gla_w_out,
               final_norm_g=m_final_norm_g.reshape(1, -1))
    var = dict(meta_tokens=v_meta_tokens, mix_norm_g=v_mix_norm_g, ffn_norm_g=v_ffn_norm_g, ffn_w1=v_ffn_w1, ffn_w2=v_ffn_w2,
               cp_w_in=v_cp_w_in, cp_conv_w=v_cp_conv_w, cp_conv_b=v_cp_conv_b, cp_ln_g=v_cp_ln_g, cp_ln_b=v_cp_ln_b,
               cp_pool_w=v_cp_pool_w, cp_pool_scale=v_cp_pool_scale, cp_w_out=v_cp_w_out, gla_w_in=v_gla_w_in,
               gla_gate_w2=v_gla_gate_w2, gla_gate_b=v_gla_gate_b, gla_head_g=v_gla_head_g, gla_w_out=v_gla_w_out,
               final_norm_g=v_final_norm_g.reshape(1, -1))
    d = x.shape[-1]
    replicated = dict(mix_g=w["mix_norm_g"], ffn_g=w["ffn_norm_g"], conv_b=w["cp_conv_b"], ln_g=w["cp_ln_g"],
                      ln_b=w["cp_ln_b"], pool_w=w["cp_pool_w"][0].astype(BF16), pool_scale=w["cp_pool_scale"],
                      final_g=w["final_norm_g"])
    exchanges = _Exchanges(w, d)
    _, grad_x, _ = _local_step(x[0], loss_target[0], replicated, exchanges)
    out = exchanges.finish(w, mom, var)
    loss = out.pop("loss")

    def leaf(n, k):
        a = out[n][k]
        return a.reshape(-1) if n == "final_norm_g" else a

    return (loss, grad_x[None], *[leaf(n, 0) for n in _NAMES], *[leaf(n, 1) for n in _NAMES],
            *[leaf(n, 2) for n in _NAMES], *[leaf(n, 3) for n in _NAMES])
```

```python
import functools

import jax
import jax.numpy as jnp
from jax import lax
from jax.experimental import pallas as pl
from jax.experimental.pallas import tpu as pltpu

F32, BF16 = jnp.float32, jnp.bfloat16
N_DEV = 8
CHUNK = 64
N_META = 16
PAD_ROWS = CHUNK - N_META
HALO = 32
EPS = 1e-5
CONV_WIDTH = 31
POOL_WINDOWS = (2, 4, 8, 16)
HEADS = 4
GATE_RANK = 16
GATE_NORM = 16.0
GATE_PAD = 128
ADAM_LR, ADAM_B1, ADAM_B2, ADAM_EPS, ADAM_WD, ADAM_STEP = 0.001, 0.9, 0.999, 1e-08, 0.01, 10
V7X_VMEM_LIMIT = 56 * 2 ** 20
LANE = 128


def _cparams(*sem):
    return pltpu.CompilerParams(dimension_semantics=sem, vmem_limit_bytes=V7X_VMEM_LIMIT)


def _row_tile(t, cap):
    best = CHUNK
    for r in range(CHUNK, min(t, cap) + 1, CHUNK):
        if t % r == 0:
            best = r
    return best


def _resident(shape):
    return pl.BlockSpec(shape, lambda *_: (0,) * len(shape), pipeline_mode=pl.Buffered(1))


def _dot(a, b):
    return jnp.dot(a, b, preferred_element_type=F32)


def _dot_nt(a, b):
    return lax.dot_general(a, b, (((1,), (1,)), ((), ())), preferred_element_type=F32)


def _dot_tn(a, b):
    return lax.dot_general(a, b, (((0,), (0,)), ((), ())), preferred_element_type=F32)


def _rowsum(a):
    return jnp.sum(a, axis=0, keepdims=True)


def _sigmoid(a):
    return 1.0 / (1.0 + jnp.exp(-a))


def _row_ids(tile, rt):
    return tile * rt + lax.broadcasted_iota(jnp.int32, (rt, 1), 0)


def _sds(shape, dtype):
    return jax.ShapeDtypeStruct(shape, dtype)


DW_ROWS = 1024


def _linear_bwd_w(x, dy, *, square_x=False, column_blocks=None, row_blocks=None, after=None, name):
    t, k = x.shape
    n = dy.shape[1]
    cut_k = k > n and column_blocks is None
    assert cut_k or row_blocks is None
    width = k if cut_k else n
    blk = n // column_blocks if column_blocks else max(c for c in (640, 512, 384, 256, LANE) if width % c == 0)
    dep_specs, deps = _dep_specs(after)

    def body(*refs):
        x_ref, dy_ref, o_ref, acc = refs[len(deps):]
        for c0 in range(0, t, DW_ROWS):
            rows = slice(c0, min(c0 + DW_ROWS, t))
            xv = x_ref[rows, :]
            if square_x:
                xv = xv.astype(F32)
                xv = xv * xv
            part = _dot_tn(xv.astype(BF16), dy_ref[rows, :].astype(BF16))
            if c0 == 0:
                acc[...] = part
            else:
                acc[...] += part
        if row_blocks is None:
            o_ref[...] = acc[...].astype(BF16)
            return
        nb, rpb = row_blocks
        for s in range(width // blk):
            @pl.when(pl.program_id(0) == s)
            def _(s=s):
                for b in range(nb):
                    lo, hi = max(s * blk, b * rpb), min((s + 1) * blk, (b + 1) * rpb)
                    if lo < hi:
                        o_ref[b, lo - b * rpb:hi - b * rpb, :] = acc[lo - s * blk:hi - s * blk, :].astype(BF16)

    out_shape = _sds((k, n), BF16)
    semantics = "parallel"
    if cut_k:
        in_specs = [pl.BlockSpec((t, blk), lambda j: (0, j)), _resident((t, n))]
        out_specs = pl.BlockSpec((blk, n), lambda j: (j, 0))
        acc_shape = (blk, n)
        if row_blocks:
            out_shape = _sds(row_blocks + (n,), BF16)
            out_specs = pl.BlockSpec(out_shape.shape, lambda j: (0, 0, 0))
            semantics = "arbitrary"
    else:
        in_specs = [_resident((t, k)), pl.BlockSpec((t, blk), lambda j: (0, j))]
        out_specs = pl.BlockSpec((k, blk), lambda j: (0, j))
        acc_shape = (k, blk)
        if column_blocks:
            out_specs = pl.BlockSpec((None, k, blk), lambda j: (j, 0, 0))
            out_shape = _sds((column_blocks, k, blk), BF16)
    return pl.pallas_call(
        body, grid=(width // blk,), in_specs=dep_specs + in_specs, out_specs=out_specs, out_shape=out_shape,
        scratch_shapes=[pltpu.VMEM(acc_shape, F32)], compiler_params=_cparams(semantics), name=name)(*deps, x, dy)


FFN_BLOCKS_PER_STEP = 2


def _ffn_fwd(h, gain, w1g, w2g, *, loss_head=None, name):
    t, d = h.shape
    f8 = w1g.shape[-1]
    rt = _row_tile(t, 832)
    nb = FFN_BLOCKS_PER_STEP
    nstep = N_DEV // nb

    def body(*refs):
        if loss_head is None:
            h_ref, g_ref, w1_ref, w2_ref, o_ref, u_ref, r_ref, acc_ref = refs
        else:
            (h_ref, g_ref, w1_ref, w2_ref, fg_ref, tgt_ref, o_ref, u_ref, r_ref, loss_ref, dfg_ref, acc_ref, t_ref,
             t_sem) = refs
        i, j = pl.program_id(0), pl.program_id(1)

        def target_rows(act):
            @pl.when(i == 0)
            def _():
                act(pltpu.make_async_copy(tgt_ref.at[pl.ds(0, rt - CHUNK)], t_ref.at[pl.ds(CHUNK, rt - CHUNK)], t_sem.at[0]))

            if t > rt:
                @pl.when(i > 0)
                def _():
                    act(pltpu.make_async_copy(tgt_ref.at[pl.ds(pl.multiple_of(i * rt - CHUNK, CHUNK), rt)], t_ref,
                                              t_sem.at[0]))

        @pl.when(j == 0)
        def _():
            if loss_head is not None:
                @pl.when(i == 0)
                def _():
                    t_ref[0:CHUNK, :] = jnp.zeros((CHUNK, d), F32)

                target_rows(lambda copy: copy.start())

            hv = h_ref[...]
            u_ref[...] = (hv * lax.rsqrt(jnp.mean(hv * hv, axis=-1, keepdims=True) + EPS) * g_ref[...]).astype(BF16)
            acc_ref[...] = jnp.zeros_like(acc_ref)

        part = None
        for b in range(nb):
            a = jnp.maximum(_dot(u_ref[...], w1_ref[b]), 0.0)
            r_ref[:, b * f8:(b + 1) * f8] = a.astype(BF16)
            term = _dot((a * a).astype(BF16), w2_ref[b])
            part = term if part is None else part + term
        acc_ref[...] += part

        @pl.when(j == nstep - 1)
        def _():
            y = h_ref[...] + acc_ref[...]
            if loss_head is None:
                o_ref[...] = y
                return

            @pl.when(i == 0)
            def _():
                loss_ref[...] = jnp.zeros_like(loss_ref)
                dfg_ref[...] = jnp.zeros_like(dfg_ref)

            target_rows(lambda copy: copy.wait())

            rstd = lax.rsqrt(jnp.mean(y * y, axis=-1, keepdims=True) + EPS)
            xh = y * rstd
            err = jnp.where(_row_ids(i, rt) >= CHUNK, xh * fg_ref[...] - t_ref[...], 0.0)
            loss_ref[...] += (0.5 / d) * jnp.sum(err * err)
            dy = err * (1.0 / d)
            dfg_ref[...] += _rowsum(dy * xh)
            dxh = dy * fg_ref[...]
            o_ref[...] = rstd * (dxh - xh * jnp.mean(dxh * xh, axis=-1, keepdims=True))

    rows = lambda i, j: (i, 0)
    in_specs = [pl.BlockSpec((rt, d), rows), _resident((1, d)),
                pl.BlockSpec((nb, d, f8), lambda i, j: (j, 0, 0)), pl.BlockSpec((nb, f8, d), lambda i, j: (j, 0, 0))]
    out_specs = [pl.BlockSpec((rt, d), rows), pl.BlockSpec((rt, d), rows), pl.BlockSpec((rt, nb * f8), lambda i, j: (i, j))]
    out_shape = [_sds((t, d), F32), _sds((t, d), BF16), _sds((t, N_DEV * f8), BF16)]
    args = [h, gain, w1g, w2g]
    scratch_shapes = [pltpu.VMEM((rt, d), F32)]
    if loss_head is not None:
        in_specs += [_resident((1, d)), pl.BlockSpec(memory_space=pl.ANY)]
        out_specs += [pl.BlockSpec((8, LANE), lambda i, j: (0, 0)), pl.BlockSpec((1, d), lambda i, j: (0, 0))]
        out_shape += [_sds((8, LANE), F32), _sds((1, d), F32)]
        args += list(loss_head)
        scratch_shapes += [pltpu.VMEM((rt, d), F32), pltpu.SemaphoreType.DMA((1,))]
    return pl.pallas_call(
        body, grid=(t // rt, nstep), in_specs=in_specs, out_specs=out_specs, out_shape=out_shape,
        scratch_shapes=scratch_shapes,
        compiler_params=_cparams("arbitrary" if loss_head is not None else "parallel", "arbitrary"), name=name)(*args)


def _ffn_bwd_x(h, dout, gain, r, w1g, w2g, *, after=None, name):
    t, d = h.shape
    f8 = w1g.shape[-1]
    rt = _row_tile(t, 832)
    nb = FFN_BLOCKS_PER_STEP
    last = N_DEV // nb - 1
    dep_specs, deps = _dep_specs(after)

    def body(*refs):
        h_ref, do_ref, g_ref, r_ref, w1_ref, w2_ref, dh_ref, dhh_ref, dob_ref, dg_ref, du_ref = refs[len(deps):]
        i, j = pl.program_id(0), pl.program_id(1)

        @pl.when(j == 0)
        def _():
            dob_ref[...] = do_ref[...].astype(BF16)
            du_ref[...] = jnp.zeros_like(du_ref)

        part = None
        for b in range(nb):
            cols = slice(b * f8, (b + 1) * f8)
            dhh = (_dot_nt(dob_ref[...], w2_ref[b]) * (2.0 * r_ref[:, cols].astype(F32))).astype(BF16)
            dhh_ref[:, cols] = dhh
            term = _dot_nt(dhh, w1_ref[b])
            part = term if part is None else part + term
        du_ref[...] += part

        @pl.when(j == last)
        def _():
            @pl.when(i == 0)
            def _():
                dg_ref[...] = jnp.zeros_like(dg_ref)

            hv = h_ref[...]
            rstd = lax.rsqrt(jnp.mean(hv * hv, axis=-1, keepdims=True) + EPS)
            xh = hv * rstd
            du = du_ref[...]
            dg_ref[...] += _rowsum(du * xh)
            dxh = du * g_ref[...]
            dh_ref[...] = do_ref[...] + rstd * (dxh - xh * jnp.mean(dxh * xh, axis=-1, keepdims=True))

    rows = lambda i, j: (i, 0)
    return pl.pallas_call(
        body, grid=(t // rt, N_DEV // nb),
        in_specs=dep_specs + [
                  pl.BlockSpec((rt, d), rows), pl.BlockSpec((rt, d), rows), _resident((1, d)),
                  pl.BlockSpec((rt, nb * f8), lambda i, j: (i, j)),
                  pl.BlockSpec((nb, d, f8), lambda i, j: (j, 0, 0)),
                  pl.BlockSpec((nb, f8, d), lambda i, j: (j, 0, 0))],
        out_specs=[pl.BlockSpec((rt, d), rows), pl.BlockSpec((rt, nb * f8), lambda i, j: (i, j)),
                   pl.BlockSpec((rt, d), rows), pl.BlockSpec((1, d), lambda i, j: (0, 0))],
        out_shape=[_sds((t, d), F32), _sds((t, N_DEV * f8), BF16), _sds((t, d), BF16), _sds((1, d), F32)],
        scratch_shapes=[pltpu.VMEM((rt, d), F32)],
        compiler_params=_cparams("arbitrary", "arbitrary"), name=name)(*deps, h, dout, gain, r, w1g, w2g)


def _lane_blocks(width):
    lb = min(LANE, width)
    return [slice(s, s + lb) for s in range(0, width, lb)]


def _conv_rows(src_ref, w_ref, offset, dst_ref, nblk, width, bias_ref=None):
    def blk(rb, carry):
        base = pl.multiple_of(rb * CHUNK, CHUNK)
        for l, ls in enumerate(_lane_blocks(width)):
            acc = jnp.zeros((CHUNK, ls.stop - ls.start), F32)
            if bias_ref is not None:
                acc = acc + bias_ref[:, ls]
            for k in range(CONV_WIDTH):
                acc = acc + w_ref[k:k + 1, ls] * src_ref[l, pl.ds(base + offset(k), CHUNK), :]
            dst_ref[l, pl.ds(base, CHUNK), :] = acc
        return carry

    lax.fori_loop(0, nblk, blk, 0)


def _to_lane_blocks(ref, row0, value):
    for l, ls in enumerate(_lane_blocks(value.shape[1])):
        ref[l, row0:row0 + value.shape[0], :] = value[:, ls]


def _from_lane_blocks(ref):
    return jnp.concatenate([ref[l] for l in range(ref.shape[0])], axis=1)


def _pool_counts(rows, window):
    return jnp.clip(rows - PAD_ROWS + 1, 1, window).astype(F32)


def _trailing_sum(v, window):
    s, sh = v, 1
    while sh < window:
        s = s + pltpu.roll(s, sh, 0)
        sh *= 2
    return s


def _leading_sum(v, window):
    s, sh, n = v, 1, v.shape[0]
    while sh < window:
        s = s + pltpu.roll(s, n - sh, 0)
        sh *= 2
    return s


def _norm_project(h_ref, g_ref, w_t_ref, u_ref, z_ref):
    hv = h_ref[...]
    u = (hv * lax.rsqrt(jnp.mean(hv * hv, axis=-1, keepdims=True) + EPS) * g_ref[...]).astype(BF16)
    u_ref[...] = u
    z_ref[...] = _dot_nt(u, w_t_ref[...])


def _project_back(dz_ref, w_t_ref, h_ref, g_ref, dres_ref, dh_ref, dg_ref, first):
    dx = _dot(dz_ref[...], w_t_ref[...])
    hv = h_ref[...]
    rstd = lax.rsqrt(jnp.mean(hv * hv, axis=-1, keepdims=True) + EPS)
    xh = hv * rstd

    @pl.when(first)
    def _():
        dg_ref[...] = jnp.zeros_like(dg_ref)

    dg_ref[...] += _rowsum(dx * xh)
    dxh = dx * g_ref[...]
    dh_ref[...] = dres_ref[...] + rstd * (dxh - xh * jnp.mean(dxh * xh, axis=-1, keepdims=True))


def _cp_mid_fwd(x, meta, gain, w_in_t, w_out, conv_w, conv_b, ln_g, ln_b, pool_w, pool_scale, *, name):
    seq, d = x.shape
    t = seq + CHUNK
    ein = w_in_t.shape[0]
    cd = conv_b.shape[1]
    pd = pool_scale.shape[1]
    pg = pd // len(POOL_WINDOWS)
    rt = _row_tile(t, 320)
    ntile = t // rt

    def body(x_ref, meta_ref, g_ref, wi_ref, wo_ref, cw_ref, cb_ref, lg_ref, lb_ref, pw_ref, ps_ref,
             ho_ref, o_ref, z_ref, u_ref, cv_ref, h0_ref, gext, pext, conv_s, hbuf, hsem):
        i = pl.program_id(0)
        slot = i % 2
        first_rows = pltpu.make_async_copy(x_ref.at[pl.ds(0, rt - CHUNK)], hbuf.at[0, pl.ds(CHUNK, rt - CHUNK)], hsem.at[0])

        def tile_rows(tile, to):
            return pltpu.make_async_copy(x_ref.at[pl.ds(pl.multiple_of(tile * rt - CHUNK, CHUNK), rt)], hbuf.at[to],
                                         hsem.at[to])

        @pl.when(i == 0)
        def _():
            first_rows.start()
            hbuf[0, 0:PAD_ROWS, :] = jnp.zeros((PAD_ROWS, d), F32)
            hbuf[0, PAD_ROWS:CHUNK, :] = meta_ref[...]
            _to_lane_blocks(gext, 0, jnp.zeros((HALO, cd), F32))
            pext[0:HALO, :] = jnp.zeros((HALO, pd), F32)

        @pl.when(i + 1 < ntile)
        def _():
            tile_rows(i + 1, 1 - slot).start()

        @pl.when(i == 0)
        def _():
            first_rows.wait()

        @pl.when(i > 0)
        def _():
            tile_rows(i, slot).wait()

        h_ref = hbuf.at[slot]
        h0_ref[...] = h_ref[...]
        _norm_project(h_ref, g_ref, wi_ref, u_ref, z_ref)

        _to_lane_blocks(gext, HALO, z_ref[:, 0:cd] * _sigmoid(z_ref[:, cd:2 * cd]))
        pext[HALO:HALO + rt, :] = z_ref[:, 2 * cd:]
        _conv_rows(gext, cw_ref, lambda k: k + HALO - (CONV_WIDTH - 1), conv_s, rt // CHUNK, cd, cb_ref)
        cv = _from_lane_blocks(conv_s)
        cv_ref[...] = cv
        xc = cv - jnp.mean(cv, axis=-1, keepdims=True)
        y = xc * lax.rsqrt(jnp.mean(xc * xc, axis=-1, keepdims=True) + EPS) * lg_ref[...] + lb_ref[...]
        rows = _row_ids(i, rt)
        a = jnp.where(rows >= PAD_ROWS, y * _sigmoid(y), 0.0)
        o_ref[:, 0:cd] = a.astype(BF16)
        for gi, window in enumerate(POOL_WINDOWS):
            ls = slice(gi * pg, (gi + 1) * pg)
            v = pext[:, ls]
            tm = _trailing_sum(v, window)[HALO:] / _pool_counts(rows, window) - v[HALO:]
            p = _dot(tm.astype(BF16), pw_ref[gi]) * ps_ref[:, ls]
            o_ref[:, cd + gi * pg:cd + (gi + 1) * pg] = p.astype(BF16)
        ho_ref[...] = h_ref[...] + _dot(o_ref[...], wo_ref[...])
        gext[:, 0:HALO, :] = gext[:, rt:rt + HALO, :]
        pext[0:HALO, :] = pext[rt:rt + HALO, :]

    nl, lb = len(_lane_blocks(cd)), min(LANE, cd)
    rows = lambda i: (i, 0)
    return pl.pallas_call(
        body, grid=(ntile,),
        in_specs=[pl.BlockSpec(memory_space=pl.ANY), _resident(meta.shape), _resident((1, d)), _resident(w_in_t.shape),
                  _resident(w_out.shape), _resident(conv_w.shape), _resident((1, cd)),
                  _resident((1, cd)), _resident((1, cd)), _resident(pool_w.shape), _resident((1, pd))],
        out_specs=[pl.BlockSpec((rt, d), rows), pl.BlockSpec((rt, cd + pd), rows), pl.BlockSpec((rt, ein), rows),
                   pl.BlockSpec((rt, d), rows), pl.BlockSpec((rt, cd), rows), pl.BlockSpec((rt, d), rows)],
        out_shape=[_sds((t, d), F32), _sds((t, cd + pd), BF16), _sds((t, ein), F32), _sds((t, d), BF16),
                   _sds((t, cd), F32), _sds((t, d), F32)],
        scratch_shapes=[pltpu.VMEM((nl, rt + HALO, lb), F32), pltpu.VMEM((rt + HALO, pd), F32),
                        pltpu.VMEM((nl, rt, lb), F32), pltpu.VMEM((2, rt, d), F32), pltpu.SemaphoreType.DMA((2,))],
        compiler_params=_cparams("arbitrary"), name=name)(x, meta, gain, w_in_t, w_out, conv_w, conv_b, ln_g, ln_b, pool_w,
                                                          pool_scale)


def _cp_mid_bwd(z, cv, h, gain, w_in_t, dh, w_out, conv_w, conv_b, ln_g, ln_b, pool_w, pool_scale, *, after=None, name):
    t, ein = z.shape
    cd = conv_b.shape[1]
    pd = pool_scale.shape[1]
    pg = pd // len(POOL_WINDOWS)
    rt = _row_tile(t, 320)
    ntile = t // rt
    per = rt // CHUNK
    dep_specs, deps = _dep_specs(after)

    def body(*refs):
        (z_ref, zh_ref, cv_ref, h_ref, g_ref, wi_ref, dh_ref, wo_ref, cw_ref, cb_ref, lg_ref, lb_ref, pw_ref, ps_ref,
         dx_ref, dfirst_ref, dg_ref, dz_ref, dcw_ref, dcb_ref, dlg_ref, dlb_ref, dpw_ref, dps_ref,
         gext, pext, conv_s, dcv, dsp, dhi, dx_sem) = refs[len(deps):]
        step = pl.program_id(0)
        tile = ntile - 1 - step
        slot = step % 2
        last_slot = (ntile - 1) % 2
        first_rows = pltpu.make_async_copy(dhi.at[last_slot, pl.ds(CHUNK, rt - CHUNK)], dx_ref.at[pl.ds(0, rt - CHUNK)],
                                           dx_sem.at[last_slot])

        def tile_rows(tile, slot):
            return pltpu.make_async_copy(dhi.at[slot], dx_ref.at[pl.ds(pl.multiple_of(tile * rt - CHUNK, CHUNK), rt)],
                                         dx_sem.at[slot])

        dcat = _dot_nt(dh_ref[...].astype(BF16), wo_ref[...])

        @pl.when(step >= 2)
        def _():
            tile_rows(tile + 2, slot).wait()

        @pl.when(step == 0)
        def _():
            for ref in (dcw_ref, dcb_ref, dlg_ref, dlb_ref, dpw_ref, dps_ref):
                ref[...] = jnp.zeros_like(ref)
            _to_lane_blocks(dcv, rt, jnp.zeros((HALO, cd), F32))
            dsp[rt:rt + HALO, :] = jnp.zeros((HALO, pd), F32)

        keep = jnp.where(tile > 0, 1.0, 0.0)
        zh = zh_ref[CHUNK - HALO:CHUNK, :]
        _to_lane_blocks(gext, 0, keep * zh[:, 0:cd] * _sigmoid(zh[:, cd:2 * cd]))
        pext[0:HALO, :] = keep * zh[:, 2 * cd:]
        za = z_ref[:, 0:cd]
        sg = _sigmoid(z_ref[:, cd:2 * cd])
        _to_lane_blocks(gext, HALO, za * sg)
        pext[HALO:HALO + rt, :] = z_ref[:, 2 * cd:]
        cv = cv_ref[...]
        xc = cv - jnp.mean(cv, axis=-1, keepdims=True)
        rstd = lax.rsqrt(jnp.mean(xc * xc, axis=-1, keepdims=True) + EPS)
        xh = xc * rstd
        y = xh * lg_ref[...] + lb_ref[...]
        sy = _sigmoid(y)
        rows = _row_ids(tile, rt)
        da = jnp.where(rows >= PAD_ROWS, dcat[:, 0:cd], 0.0)
        dy = da * (sy * (1.0 + y * (1.0 - sy)))
        dlg_ref[...] += _rowsum(dy * xh)
        dlb_ref[...] += _rowsum(dy)
        dxh = dy * lg_ref[...]
        dconv = rstd * (dxh - jnp.mean(dxh, axis=-1, keepdims=True) - xh * jnp.mean(dxh * xh, axis=-1, keepdims=True))
        dcb_ref[...] += _rowsum(dconv)
        _to_lane_blocks(dcv, 0, dconv)
        for l, ls in enumerate(_lane_blocks(cd)):
            def acc_rows(rb, accs, l=l):
                base = pl.multiple_of(rb * CHUNK, CHUNK)
                d_blk = dcv[l, pl.ds(base, CHUNK), :]
                out = []
                for k in range(CONV_WIDTH):
                    prod = d_blk * gext[l, pl.ds(base + k + HALO - (CONV_WIDTH - 1), CHUNK), :]
                    part = prod[0:8]
                    for s in range(8, CHUNK, 8):
                        part = part + prod[s:s + 8]
                    out.append(accs[k] + part)
                return tuple(out)

            zero = jnp.zeros((8, ls.stop - ls.start), F32)
            accs = lax.fori_loop(0, per, acc_rows, (zero,) * CONV_WIDTH)
            for k in range(CONV_WIDTH):
                dcw_ref[k:k + 1, ls] += _rowsum(accs[k])
        _conv_rows(dcv, cw_ref, lambda k: CONV_WIDTH - 1 - k, conv_s, per, cd)
        dglu = _from_lane_blocks(conv_s)
        dz_ref[:, 0:cd] = (dglu * sg).astype(BF16)
        dz_ref[:, cd:2 * cd] = (dglu * za * sg * (1.0 - sg)).astype(BF16)
        dcv[:, rt:rt + HALO, :] = dcv[:, 0:HALO, :]
        for gi, window in enumerate(POOL_WINDOWS):
            ls = slice(gi * pg, (gi + 1) * pg)
            v = pext[:, ls]
            cnt = _pool_counts(rows, window)
            tm = (_trailing_sum(v, window)[HALO:] / cnt - v[HALO:]).astype(BF16)
            dp = dcat[:, cd + gi * pg:cd + (gi + 1) * pg]
            dps_ref[:, ls] += _rowsum(dp * _dot(tm, pw_ref[gi]))
            dpl = (dp * ps_ref[:, ls]).astype(BF16)
            dpw_ref[gi] += _dot_tn(tm, dpl)
            dtm = _dot_nt(dpl, pw_ref[gi])
            dsp[0:rt, ls] = dtm / cnt
            dpin = _leading_sum(dsp[:, ls], window)[0:rt] - dtm
            dz_ref[:, 2 * cd + gi * pg:2 * cd + (gi + 1) * pg] = dpin.astype(BF16)
        dsp[rt:rt + HALO, :] = dsp[0:HALO, :]

        _project_back(dz_ref, wi_ref, h_ref, g_ref, dh_ref, dhi.at[slot], dg_ref, step == 0)

        @pl.when(tile > 0)
        def _():
            tile_rows(tile, slot).start()

        @pl.when(tile == 0)
        def _():
            first_rows.start()
            dfirst_ref[...] = dhi[last_slot, 0:CHUNK, :]
            if ntile > 1:
                tile_rows(1, 1 - last_slot).wait()
            first_rows.wait()

    d = h.shape[1]
    back = lambda i: (ntile - 1 - i, 0)
    halo_idx = lambda i: (jnp.maximum((ntile - 1 - i) * per - 1, 0), 0)
    const2 = lambda i: (0, 0)
    nl, lb = len(_lane_blocks(cd)), min(LANE, cd)
    return pl.pallas_call(
        body, grid=(ntile,),
        in_specs=dep_specs + [
                  pl.BlockSpec((rt, ein), back), pl.BlockSpec((CHUNK, ein), halo_idx), pl.BlockSpec((rt, cd), back),
                  pl.BlockSpec((rt, d), back),
                  _resident((1, d)), _resident(w_in_t.shape), pl.BlockSpec((rt, d), back), _resident(w_out.shape),
                  _resident(conv_w.shape), _resident((1, cd)), _resident((1, cd)), _resident((1, cd)),
                  _resident(pool_w.shape), _resident((1, pd))],
        out_specs=[pl.BlockSpec(memory_space=pl.ANY), pl.BlockSpec((CHUNK, d), const2), pl.BlockSpec((1, d), const2),
                   pl.BlockSpec((rt, ein), back), pl.BlockSpec(conv_w.shape, const2), pl.BlockSpec((1, cd), const2),
                   pl.BlockSpec((1, cd), const2), pl.BlockSpec((1, cd), const2),
                   pl.BlockSpec(pool_w.shape, lambda i: (0, 0, 0)), pl.BlockSpec((1, pd), const2)],
        out_shape=[_sds((t - CHUNK, d), F32), _sds((CHUNK, d), F32), _sds((1, d), F32),
                   _sds((t, ein), BF16), _sds(conv_w.shape, F32), _sds((1, cd), F32), _sds((1, cd), F32),
                   _sds((1, cd), F32), _sds(pool_w.shape, F32), _sds((1, pd), F32)],
        scratch_shapes=[pltpu.VMEM((nl, rt + HALO, lb), F32), pltpu.VMEM((rt + HALO, pd), F32), pltpu.VMEM((nl, rt, lb), F32),
                        pltpu.VMEM((nl, rt + HALO, lb), F32), pltpu.VMEM((rt + HALO, pd), F32), pltpu.VMEM((2, rt, d), F32),
                        pltpu.SemaphoreType.DMA((2,))],
        compiler_params=_cparams("arbitrary"), name=name)(*deps, z, z, cv, h, gain, w_in_t, dh, w_out, conv_w, conv_b, ln_g,
                                                          ln_b, pool_w, pool_scale)


def _log_decay(r, gw_ref, gb_ref, rows):
    gp = _dot(r.astype(BF16), gw_ref[...]) + gb_ref[...]
    log_sig = jnp.minimum(gp, 0.0) - jnp.log(1.0 + jnp.exp(-jnp.abs(gp)))
    return gp, jnp.where(rows >= PAD_ROWS, log_sig / GATE_NORM, 0.0)


def _tri(strict):
    r = lax.broadcasted_iota(jnp.int32, (CHUNK, CHUNK), 0)
    c = lax.broadcasted_iota(jnp.int32, (CHUNK, CHUNK), 1)
    return jnp.where(c < r if strict else c <= r, 1.0, 0.0).astype(BF16)


def _tri_dot(tri, a):
    hi = a.astype(BF16)
    rest = a - hi.astype(F32)
    mid = rest.astype(BF16)
    lo = (rest - mid.astype(F32)).astype(BF16)
    return _dot(tri, hi) + _dot(tri, mid) + _dot(tri, lo)


def _gla_mid_fwd(h, gain, w_in_blocks, w_out, gate_w, gate_b, head_g, *, name):
    t = h.shape[0]
    nblk, rpb = w_in_blocks.shape[:2]
    dk = gate_b.shape[1]
    hv = head_g.shape[1]
    hk = dk // HEADS
    dv = hv * HEADS
    r_at = 2 * dk + 2 * dv
    assert nblk * rpb == r_at + GATE_RANK
    zw = r_at + GATE_PAD
    rt = _row_tile(t, 320)
    per = rt // CHUNK
    scale = hk ** -0.5

    def body(h_ref, g_ref, wb_ref, wo_ref, gw_ref, gb_ref, hg_ref, ho_ref, o_ref, st_ref, z_ref, u_ref, wi_ref,
             s_ref, la_ref, dec_ref):
        i = pl.program_id(0)

        @pl.when(i == 0)
        def _():
            s_ref[...] = jnp.zeros_like(s_ref)
            for b in range(nblk):
                wi_ref[b * rpb:(b + 1) * rpb, :] = wb_ref[b]
            wi_ref[nblk * rpb:, :] = jnp.zeros((zw - nblk * rpb, wi_ref.shape[1]), BF16)

        _norm_project(h_ref, g_ref, wi_ref, u_ref, z_ref)

        _, la = _log_decay(z_ref[:, r_at:r_at + GATE_PAD], gw_ref, gb_ref, _row_ids(i, rt))
        la_ref[...] = la
        tri = _tri(False)

        def chunk_rows(c):
            return slice(c * CHUNK, (c + 1) * CHUNK)

        def decays(c, carry):
            rows = chunk_rows(c)
            la_c = la_ref[rows, :]
            cum = _tri_dot(tri, la_c)
            dec_ref[rows, :] = jnp.exp(_rowsum(la_c) - cum)
            return carry

        def states(c, carry):
            rows = chunk_rows(c)
            etot = jnp.exp(_rowsum(la_ref[rows, :]))
            for hd in range(HEADS):
                ks = slice(hd * hk, (hd + 1) * hk)
                kd = z_ref[rows, dk + hd * hk:dk + (hd + 1) * hk] * dec_ref[rows, ks]
                v = z_ref[rows, 2 * dk + hd * hv:2 * dk + (hd + 1) * hv]
                s_new = s_ref[hd] * etot[:, ks] + _dot_tn(v.astype(BF16), kd.astype(BF16))
                s_ref[hd] = s_new
                st_ref[c, hd] = s_new
            return carry

        def outputs(c, carry):
            rows = chunk_rows(c)
            for hd in range(HEADS):
                q = z_ref[rows, hd * hk:(hd + 1) * hk] * scale
                g = z_ref[rows, 2 * dk + dv + hd * hv:2 * dk + dv + (hd + 1) * hv]
                o = _dot_nt(q.astype(BF16), st_ref[c, hd].astype(BF16))
                on = o * lax.rsqrt(jnp.mean(o * o, axis=-1, keepdims=True) + EPS) * hg_ref[...]
                o_ref[rows, hd * hv:(hd + 1) * hv] = (on * (g * _sigmoid(g))).astype(BF16)
            return carry

        for phase in (decays, states, outputs):
            for c in range(per):
                phase(c, 0)
        ho_ref[...] = h_ref[...] + _dot(o_ref[...], wo_ref[...])

    d = h.shape[1]
    rows = lambda i: (i, 0)
    return pl.pallas_call(
        body, grid=(t // rt,),
        in_specs=[pl.BlockSpec((rt, d), rows), _resident((1, d)), _resident(w_in_blocks.shape), _resident(w_out.shape),
                  _resident(gate_w.shape), _resident((1, dk)), _resident((1, hv))],
        out_specs=[pl.BlockSpec((rt, d), rows), pl.BlockSpec((rt, dv), rows),
                   pl.BlockSpec((per, HEADS, hv, hk), lambda i: (i, 0, 0, 0)), pl.BlockSpec((rt, zw), rows),
                   pl.BlockSpec((rt, d), rows), pl.BlockSpec((zw, d), lambda i: (0, 0))],
        out_shape=[_sds((t, d), F32), _sds((t, dv), BF16), _sds((t // CHUNK, HEADS, hv, hk), F32), _sds((t, zw), F32),
                   _sds((t, d), BF16), _sds((zw, d), BF16)],
        scratch_shapes=[pltpu.VMEM((HEADS, hv, hk), F32), pltpu.VMEM((rt, dk), F32), pltpu.VMEM((rt, dk), F32)],
        compiler_params=_cparams("arbitrary"), name=name)(h, gain, w_in_blocks, w_out, gate_w, gate_b, head_g)


def _gla_mid_bwd(z, h, gain, w_in_t, dh, w_out, states, gate_w, gate_b, head_g, *, after=None, name):
    t = z.shape[0]
    dk = gate_b.shape[1]
    hv = head_g.shape[1]
    hk = dk // HEADS
    dv = hv * HEADS
    r_at = 2 * dk + 2 * dv
    rt = _row_tile(t, 320)
    ntile = t // rt
    per = rt // CHUNK
    scale = hk ** -0.5
    dep_specs, deps = _dep_specs(after)

    def body(*refs):
        (z_ref, h_ref, g_ref, wi_ref, dh_ref, wo_ref, st_ref, stp_ref, gw_ref, gb_ref, hg_ref,
         dhi_ref, dg_ref, dz_ref, dgw_ref, dgb_ref, dhg_ref,
         ds_ref, la_ref, dla_ref, dec_ref, dos_ref, e_ref, do_ref) = refs[len(deps):]
        step = pl.program_id(0)
        tile = ntile - 1 - step
        do_ref[...] = _dot_nt(dh_ref[...].astype(BF16), wo_ref[...])

        @pl.when(step == 0)
        def _():
            ds_ref[...] = jnp.zeros_like(ds_ref)
            dgw_ref[...] = jnp.zeros_like(dgw_ref)
            dgb_ref[...] = jnp.zeros_like(dgb_ref)
            dhg_ref[...] = jnp.zeros_like(dhg_ref)

        rows_id = _row_ids(tile, rt)
        r = z_ref[:, r_at:r_at + GATE_PAD]
        gp, la = _log_decay(r, gw_ref, gb_ref, rows_id)
        la_ref[...] = la
        tri, tri_strict = _tri(False), _tri(True)
        keep = jnp.where(tile > 0, 1.0, 0.0)

        def chunk_rows(c):
            return slice(c * CHUNK, (c + 1) * CHUNK)

        def recompute(c, dhg):
            rows = chunk_rows(c)
            la_c = la_ref[rows, :]
            cum = _tri_dot(tri, la_c)
            dec_ref[rows, :] = jnp.exp(_rowsum(la_c) - cum)
            for hd in range(HEADS):
                q = (z_ref[rows, hd * hk:(hd + 1) * hk] * scale).astype(BF16)
                g = z_ref[rows, 2 * dk + dv + hd * hv:2 * dk + dv + (hd + 1) * hv]
                s_b = st_ref[c, hd].astype(BF16)
                o = _dot_nt(q, s_b)
                rstd = lax.rsqrt(jnp.mean(o * o, axis=-1, keepdims=True) + EPS)
                oh = o * rstd
                sg = _sigmoid(g)
                d_og = do_ref[rows, hd * hv:(hd + 1) * hv]
                dz_ref[rows, 2 * dk + dv + hd * hv:2 * dk + dv + (hd + 1) * hv] = (
                    d_og * oh * hg_ref[...] * (sg * (1.0 + g * (1.0 - sg)))).astype(BF16)
                don = d_og * (g * sg)
                dhg = dhg + _rowsum(don * oh)
                doh = don * hg_ref[...]
                d_o = (rstd * (doh - oh * jnp.mean(doh * oh, axis=-1, keepdims=True))).astype(BF16)
                dos_ref[rows, hd * hv:(hd + 1) * hv] = d_o
                dz_ref[rows, hd * hk:(hd + 1) * hk] = (_dot(d_o, s_b) * scale).astype(BF16)
            return dhg

        def recurrence(cc, carry):
            c = per - 1 - cc
            rows = chunk_rows(c)
            etot = jnp.exp(_rowsum(la_ref[rows, :]))
            for hd in range(HEADS):
                ks = slice(hd * hk, (hd + 1) * hk)
                q = (z_ref[rows, hd * hk:(hd + 1) * hk] * scale).astype(BF16)
                dec = dec_ref[rows, ks]
                kd = z_ref[rows, dk + hd * hk:dk + (hd + 1) * hk] * dec
                v = z_ref[rows, 2 * dk + hd * hv:2 * dk + (hd + 1) * hv].astype(BF16)
                s_prev = st_ref[c - 1, hd] if c > 0 else keep * stp_ref[0, hd]
                ds_t = ds_ref[hd] + _dot_tn(dos_ref[rows, hd * hv:(hd + 1) * hv], q)
                ds_b = ds_t.astype(BF16)
                dkd = _dot(v, ds_b)
                dz_ref[rows, 2 * dk + hd * hv:2 * dk + (hd + 1) * hv] = _dot_nt(kd.astype(BF16), ds_b).astype(BF16)
                dtot = etot[:, ks] * _rowsum(ds_t * s_prev)
                ds_ref[hd] = ds_t * etot[:, ks]
                dz_ref[rows, dk + hd * hk:dk + (hd + 1) * hk] = (dkd * dec).astype(BF16)
                e_ref[rows, ks] = dkd * kd
                dla_ref[rows, ks] = jnp.broadcast_to(dtot, (CHUNK, hk))
            return carry

        def decay_cotangent(c, carry):
            rows = chunk_rows(c)
            dla_ref[rows, :] += _tri_dot(tri_strict, e_ref[rows, :])
            return carry

        dhg = jnp.zeros((1, hv), F32)
        for c in range(per):
            dhg = recompute(c, dhg)
        dhg_ref[...] += dhg
        for phase in (recurrence, decay_cotangent):
            for c in range(per):
                phase(c, 0)
        dla = jnp.where(rows_id >= PAD_ROWS, dla_ref[...], 0.0)
        dgp = dla * (1.0 / GATE_NORM) * (1.0 - _sigmoid(gp))
        dgb_ref[...] += _rowsum(dgp)
        dgp_b = dgp.astype(BF16)
        dgw_ref[...] += _dot_tn(r.astype(BF16), dgp_b)
        dz_ref[:, r_at:r_at + GATE_PAD] = _dot_nt(dgp_b, gw_ref[...]).astype(BF16)
        _project_back(dz_ref, wi_ref, h_ref, g_ref, dh_ref, dhi_ref, dg_ref, step == 0)

    d = h.shape[1]
    back = lambda i: (ntile - 1 - i, 0)
    const2 = lambda i: (0, 0)
    return pl.pallas_call(
        body, grid=(ntile,),
        in_specs=dep_specs + [
                  pl.BlockSpec((rt, z.shape[1]), back), pl.BlockSpec((rt, d), back), _resident((1, d)),
                  _resident(w_in_t.shape), pl.BlockSpec((rt, d), back), _resident(w_out.shape),
                  pl.BlockSpec((per, HEADS, hv, hk), lambda i: (ntile - 1 - i, 0, 0, 0)),
                  pl.BlockSpec((1, HEADS, hv, hk), lambda i: (jnp.maximum((ntile - 1 - i) * per - 1, 0), 0, 0, 0)),
                  _resident(gate_w.shape), _resident((1, dk)), _resident((1, hv))],
        out_specs=[pl.BlockSpec((rt, d), back), pl.BlockSpec((1, d), const2), pl.BlockSpec((rt, z.shape[1]), back),
                   pl.BlockSpec(gate_w.shape, const2), pl.BlockSpec((1, dk), const2), pl.BlockSpec((1, hv), const2)],
        out_shape=[_sds((t, d), F32), _sds((1, d), F32), _sds(z.shape, BF16), _sds(gate_w.shape, F32),
                   _sds((1, dk), F32), _sds((1, hv), F32)],
        scratch_shapes=[pltpu.VMEM((HEADS, hv, hk), F32), pltpu.VMEM((rt, dk), F32), pltpu.VMEM((rt, dk), F32),
                        pltpu.VMEM((rt, dk), F32), pltpu.VMEM((rt, dv), BF16), pltpu.VMEM((rt, dk), F32),
                        pltpu.VMEM((rt, dv), F32)],
        compiler_params=_cparams("arbitrary"), name=name)(*deps, z, h, gain, w_in_t, dh, w_out, states, states, gate_w,
                                                          gate_b, head_g)


def _adamw_math(w, g, m, v):
    m = ADAM_B1 * m + (1.0 - ADAM_B1) * g
    v = ADAM_B2 * v + (1.0 - ADAM_B2) * (g * g)
    m_hat = m / (1.0 - ADAM_B1 ** ADAM_STEP)
    v_hat = v / (1.0 - ADAM_B2 ** ADAM_STEP)
    return -ADAM_LR * (m_hat / (jnp.sqrt(v_hat) + ADAM_EPS) + ADAM_WD * w), m, v


N_CHIP = N_DEV // 2
BLOCK_ELEMS = 256 * 1024


def _my_slot():
    return 4 * lax.axis_index("x") + 2 * lax.axis_index("y") + lax.axis_index("c")


def _row_block(r, c):
    cap = max(8, BLOCK_ELEMS // (-(-c // LANE) * LANE))
    return max([b for b in range(8, r + 1, 8) if r % b == 0 and b <= cap] or [r])


def _blocks(r, c):
    rb = _row_block(r, c)
    if rb < r or r * c <= BLOCK_ELEMS:
        return rb, c
    return r, max([b for b in (512, 256, LANE) if c % b == 0 and r * b <= BLOCK_ELEMS] or [c])


def _reduce_adam(parts, w, m, v, *, by_row=False, after=None, name):
    nl, r, c = (1, w.shape[0], w.shape[2]) if by_row else w.shape
    rb, cb = _blocks(r, c)
    dep_specs, deps = _dep_specs(after)
    whole = (slice(None), 0, slice(None)) if by_row else Ellipsis

    def body(*refs):
        me = refs[0][0]
        refs = refs[1 + len(deps):]
        p_refs = refs[:2 * nl]
        w_ref, m_ref, v_ref, g_out, d_out, m_out, v_out = refs[2 * nl:]
        layer = pl.program_id(0)
        for li in range(nl):
            @pl.when(layer == li)
            def _(li=li):
                own_ref, land_ref = p_refs[2 * li], p_refs[2 * li + 1]
                mine = own_ref[...].astype(F32)
                g = None
                for dev in range(N_DEV):
                    term = jnp.where(me == dev, mine, land_ref[dev].astype(F32))
                    g = term if g is None else g + term
                g_out[whole] = g
                d_out[whole], m_out[whole], v_out[whole] = _adamw_math(w_ref[whole], g, m_ref[whole], v_ref[whole])

    if by_row:
        blk = pl.BlockSpec((rb, 1, cb), lambda l, i, j, me: (i, 0, j))
    else:
        blk = pl.BlockSpec((None, rb, cb), lambda l, i, j, me: (l, i, j))
    p_specs = []
    for li in range(nl):
        p_specs += [
            pl.BlockSpec((None, rb, cb), lambda l, i, j, me, li=li: (me[0], jnp.where(l == li, i, 0), jnp.where(l == li, j, 0))),
            pl.BlockSpec((N_DEV, rb, cb), lambda l, i, j, me, li=li: (0, jnp.where(l == li, i, 0), jnp.where(l == li, j, 0)))]
    flat = [p for pair in parts for p in pair]
    grid_spec = pltpu.PrefetchScalarGridSpec(
        num_scalar_prefetch=1, grid=(nl, r // rb, c // cb), in_specs=dep_specs + p_specs + [blk, blk, blk],
        out_specs=[blk] * 4)
    return pl.pallas_call(
        body, grid_spec=grid_spec, out_shape=[_sds(w.shape, F32)] * 4,
        compiler_params=_cparams("arbitrary", "arbitrary", "arbitrary"), name=name)(
        _my_slot().reshape(1), *deps, *flat, w, m, v)


def _sum8(own, landed, *, name):
    def body(own_ref, land_ref, o_ref):
        me = _my_slot()
        total = None
        for dev in range(N_DEV):
            term = jnp.where(me == dev, own_ref[...], land_ref[dev])
            total = term if total is None else total + term
        o_ref[...] = total

    return pl.pallas_call(body, out_shape=_sds(own.shape, F32), name=name)(own, landed)


def _adam_small(own, landed, split, w, m, v, *, name):
    n = len(w)

    def body(*refs):
        own_refs, land_refs, w_refs, m_refs, v_refs = (refs[k * n:(k + 1) * n] for k in range(5))
        outs = refs[5 * n:]
        me = _my_slot()
        for k in range(n):
            mine = own_refs[k][me] if split[k] else own_refs[k][...]
            g = None
            for dev in range(N_DEV):
                term = jnp.where(me == dev, mine, land_refs[k][dev])
                g = term if g is None else g + term
            outs[4 * k][...] = g
            outs[4 * k + 1][...], outs[4 * k + 2][...], outs[4 * k + 3][...] = _adamw_math(
                w_refs[k][...], g, m_refs[k][...], v_refs[k][...])

    out = pl.pallas_call(body, out_shape=[_sds(a.shape, F32) for a in w for _ in range(4)],
                         compiler_params=pltpu.CompilerParams(vmem_limit_bytes=V7X_VMEM_LIMIT), name=name)(
        *own, *landed, *w, *m, *v)
    return [tuple(out[4 * k:4 * k + 4]) for k in range(n)]


_HBM = pl.BlockSpec(memory_space=pltpu.HBM)
_SEM = pl.BlockSpec(memory_space=pltpu.SEMAPHORE)
_DATAFLOW = pltpu.SideEffectType.DATAFLOW_SIDE_EFFECTING


def _plan_to_all(src, land):
    x, y, c = lax.axis_index("x"), lax.axis_index("y"), lax.axis_index("c")
    return [(src, land.at[_my_slot()], (x ^ ((d >> 2) & 1), y ^ ((d >> 1) & 1), c ^ (d & 1))) for d in range(1, N_DEV)]


def _plan_split_to_all(src, land):
    x, y, c = lax.axis_index("x"), lax.axis_index("y"), lax.axis_index("c")
    peers = [(x ^ ((d >> 2) & 1), y ^ ((d >> 1) & 1), c ^ (d & 1)) for d in range(1, N_DEV)]
    return [(src.at[4 * px + 2 * py + pc], land.at[_my_slot()], (px, py, pc)) for px, py, pc in peers]


_PLAN_COPIES = {_plan_to_all: N_DEV - 1, _plan_split_to_all: N_DEV - 1}


def _plans(plan, n):
    return list(plan) if isinstance(plan, (list, tuple)) else [plan] * n


def _exchange_copies(plan, ins, lands, send, recv):
    copies, sem = [], 0
    for p, src, land in zip(_plans(plan, len(lands)), ins, lands):
        for s, dst, dev in p(src, land):
            copies.append(pltpu.make_async_remote_copy(
                src_ref=s, dst_ref=dst, send_sem=send.at[sem], recv_sem=recv.at[sem],
                device_id=dev, device_id_type=pl.DeviceIdType.MESH))
            sem += 1
    return copies


def _place_own(srcs, *, after=None, name):
    dep_specs, deps = _dep_specs(after)
    n = len(srcs)
    arrays, in_specs, shapes = [], [], []
    for a, dtype in srcs:
        if isinstance(a, tuple):
            a, layer = a
            in_specs.append(pl.BlockSpec((None,) + a.shape[1:], lambda i, layer=layer: (layer, 0, 0)))
            shapes.append(a.shape[1:])
        else:
            in_specs.append(pl.BlockSpec(a.shape, lambda i: (0, 0)))
            shapes.append(a.shape)
        arrays.append(a)
    dtypes = [dtype for _, dtype in srcs]

    def body(*refs):
        refs = refs[len(deps):]
        a_refs, o_refs, cast_refs, sem = refs[:n], refs[n:2 * n], refs[2 * n:3 * n], refs[3 * n]
        me = _my_slot()
        copies = []
        for k in range(n):
            cast_refs[k][...] = a_refs[k][...].astype(dtypes[k])
            copies.append(pltpu.make_async_copy(cast_refs[k], o_refs[k].at[me], sem.at[k]))
            copies[-1].start()
        for cp in copies:
            cp.wait()

    return pl.pallas_call(
        body, grid=(1,), in_specs=dep_specs + in_specs, out_specs=[pl.BlockSpec(memory_space=pl.ANY)] * n,
        out_shape=[_sds((N_DEV,) + shape, dtype) for shape, dtype in zip(shapes, dtypes)],
        scratch_shapes=[pltpu.VMEM(shape, dtype) for shape, dtype in zip(shapes, dtypes)] + [pltpu.SemaphoreType.DMA((n,))],
        compiler_params=pltpu.CompilerParams(vmem_limit_bytes=V7X_VMEM_LIMIT), name=name)(*deps, *arrays)


def _plan_gather_first(land, _):
    x, y, c = lax.axis_index("x"), lax.axis_index("y"), lax.axis_index("c")
    mine = land.at[_my_slot()]
    return [(mine, mine, (x, y, 1 - c))] + [(mine, mine, (x ^ (d >> 1), y ^ (d & 1), c)) for d in range(1, N_CHIP)]


def _plan_gather_relay(land, _):
    x, y, c = lax.axis_index("x"), lax.axis_index("y"), lax.axis_index("c")
    slots = [land.at[4 * (x ^ (d >> 1)) + 2 * (y ^ (d & 1)) + c] for d in range(1, N_CHIP)]
    return [(s, s, (x, y, 1 - c)) for s in slots]


def _plan_gather_direct(land, _):
    return _plan_to_all(land.at[_my_slot()], land)


_PLAN_COPIES[_plan_gather_first] = N_CHIP
_PLAN_COPIES[_plan_gather_relay] = N_CHIP - 1
_PLAN_COPIES[_plan_gather_direct] = N_DEV - 1


def _exchange_start(plan, arrs, lands, *, after=None, name):
    bufs = list(lands) if arrs is None else list(arrs) + list(lands)
    n, nb = len(lands), len(bufs)
    nsem = sum(_PLAN_COPIES[p] for p in _plans(plan, n))
    dep_specs, deps = _dep_specs(after)

    def body(*refs):
        ins, land_refs = refs[:n], refs[nb - n:nb]
        send, recv = refs[nb + len(deps)], refs[nb + len(deps) + 1]
        for cp in _exchange_copies(plan, ins, land_refs, send, recv):
            cp.start()
        refs[-1][...] = jnp.zeros_like(refs[-1])

    out = pl.pallas_call(
        body, name=name,
        out_shape=(pltpu.SemaphoreType.DMA((nsem,)), pltpu.SemaphoreType.DMA((nsem,)),
                   *[pltpu.HBM(a.shape, a.dtype) for a in bufs], _sds((8, LANE), F32)),
        in_specs=[_HBM] * nb + dep_specs,
        out_specs=(_SEM, _SEM, *([_HBM] * nb), pl.BlockSpec(memory_space=pltpu.VMEM)),
        input_output_aliases={i: 2 + i for i in range(nb)},
        compiler_params=pltpu.CompilerParams(has_side_effects=_DATAFLOW),
    )(*[pltpu.with_memory_space_constraint(a, pltpu.HBM) for a in bufs], *deps)
    return (plan, n, out[0], out[1], list(out[2:2 + nb])), out[-1]


def _exchange_now(plan, lands, *, name):
    n = len(lands)
    nsem = sum(_PLAN_COPIES[p] for p in _plans(plan, n))

    def body(*refs):
        land_refs, send, recv = refs[n:2 * n], refs[2 * n], refs[2 * n + 1]
        copies = _exchange_copies(plan, land_refs, land_refs, send, recv)
        for cp in copies:
            cp.start()
        for cp in copies:
            cp.wait_send()
            cp.wait_recv()

    hbm = pl.BlockSpec(memory_space=pl.ANY)
    return pl.pallas_call(
        body, in_specs=[hbm] * n, out_specs=[hbm] * n, out_shape=[_sds(a.shape, a.dtype) for a in lands],
        input_output_aliases={i: i for i in range(n)},
        scratch_shapes=[pltpu.SemaphoreType.DMA((nsem,)), pltpu.SemaphoreType.DMA((nsem,))], name=name)(*lands)


def _exchange_wait(state, after, *, name):
    plan, n, send_sem, recv_sem, bufs = state
    nb = len(bufs)
    after = list(after) if isinstance(after, (list, tuple)) else [after]

    def body(*refs):
        ins, land_refs, send, recv = refs[:n], refs[nb - n:nb], refs[nb], refs[nb + 1]
        for cp in _exchange_copies(plan, ins, land_refs, send, recv):
            cp.wait_send()
            cp.wait_recv()

    out = pl.pallas_call(
        body, name=name, out_shape=[pltpu.HBM(a.shape, a.dtype) for a in bufs],
        in_specs=[_HBM] * nb + [_SEM, _SEM] + [pl.BlockSpec(memory_space=pl.ANY)] * len(after), out_specs=[_HBM] * nb,
        input_output_aliases={i: i for i in range(nb)},
        compiler_params=pltpu.CompilerParams(has_side_effects=_DATAFLOW),
    )(*bufs, send_sem, recv_sem, *after)
    return list(out[:n]), list(out[nb - n:])


def _dep_specs(after):
    return ([], []) if after is None else ([pl.BlockSpec(memory_space=pl.ANY)], [after])


def _undo_column_split(g):
    return jnp.transpose(g, (1, 0, 2)).reshape(g.shape[1], N_DEV * g.shape[2])


def _column_split(a):
    r, c = a.shape
    return jnp.transpose(a.reshape(r, N_DEV, c // N_DEV), (1, 0, 2))


class _WholeWeights:
    def __init__(self, groups):
        self.groups = groups
        self.grads = {}

    def fetch(self, group, after):
        return self.groups[group]

    def emit(self, group, grads):
        self.grads.update(grads)
        return None


def _local_step(x, target, replicated, src):
    d = x.shape[1]
    mix_g, ffn_g = replicated["mix_g"], replicated["ffn_g"]
    cp = src.fetch("cp", [])
    cp_mid = (cp["conv_w"], replicated["conv_b"], replicated["ln_g"], replicated["ln_b"], replicated["pool_w"],
              replicated["pool_scale"])

    h1, cat, z0, u0, cv0, h0 = _cp_mid_fwd(x, cp["meta"], mix_g[0:1], cp["cp_w_in_t"], cp["cp_w_out"], *cp_mid,
                                           name="cp_mixer")
    ffn0 = src.fetch("ffn0", h1)
    h2, uf0, rf0 = _ffn_fwd(h1, ffn_g[0:1], ffn0["w1"], ffn0["w2"], name="ffn0")
    gla = src.fetch("gla", h2)
    gla_mid = (gla["gate_w"], gla["gate_b"], gla["head_g"])
    h3, og, states, z1, u1, gla_w_in_t = _gla_mid_fwd(h2, mix_g[1:2], gla["gla_w_in_t"], gla["gla_w_out"], *gla_mid,
                                                      name="gla_mixer")
    ffn1 = src.fetch("ffn1", h3)
    dh4, uf1, rf1, loss, d_final_g = _ffn_fwd(h3, ffn_g[1:2], ffn1["w1"], ffn1["w2"],
                                              loss_head=(replicated["final_g"], target), name="ffn1_loss")

    dh3, dhh1, dob1, dffn_g1 = _ffn_bwd_x(h3, dh4, ffn_g[1:2], rf1, ffn1["w1"], ffn1["w2"], name="ffn1_bwd_x")
    sent = src.emit("ffn1", dict(w1=_linear_bwd_w(uf1, dhh1, column_blocks=N_DEV, name="ffn1_dw1"),
                                 w2=_linear_bwd_w(rf1, dob1, square_x=True, name="ffn1_dw2")))
    d_gla_w_out = _linear_bwd_w(og, dh3, name="gla_out_dw")
    dh2, dmix_g1, dz1, d_gate_w, d_gate_b, d_head_g = _gla_mid_bwd(
        z1, h2, mix_g[1:2], gla_w_in_t, dh3, gla["gla_w_out"], states, *gla_mid, after=sent, name="gla_mixer_bwd")
    d_gla_w_in_t = _linear_bwd_w(dz1, u1, row_blocks=gla["gla_w_in_t"].shape[:2], name="gla_in_dw")
    sent = src.emit("gla", dict(gla_w_in_t=d_gla_w_in_t, gla_w_out=d_gla_w_out))
    dh1, dhh0, dob0, dffn_g0 = _ffn_bwd_x(h1, dh2, ffn_g[0:1], rf0, ffn0["w1"], ffn0["w2"], after=sent, name="ffn0_bwd_x")
    sent = src.emit("ffn0_w1", dict(w1=_linear_bwd_w(uf0, dhh0, column_blocks=N_DEV, name="ffn0_dw1")))
    sent = src.emit("ffn0_w2", dict(w2=_linear_bwd_w(rf0, dob0, square_x=True, after=sent, name="ffn0_dw2")))
    d_cp_w_out = _linear_bwd_w(cat, dh1, after=sent, name="cp_out_dw")
    sent = src.emit("cp_out", dict(cp_w_out=d_cp_w_out))
    dx, dh0_first, dmix_g0, dz0, d_conv_w, d_conv_b, d_ln_g, d_ln_b, d_pool_w, d_pool_scale = _cp_mid_bwd(
        z0, cv0, h0, mix_g[0:1], cp["cp_w_in_t"], dh1, cp["cp_w_out"], *cp_mid, after=sent, name="cp_mixer_bwd")
    d_cp_w_in_t = _linear_bwd_w(dz0, u0, name="cp_in_dw")

    small = dict(
        mix_g=jnp.concatenate([dmix_g0, dmix_g1]), ffn_g=jnp.concatenate([dffn_g0, dffn_g1]), conv_b=d_conv_b, ln_g=d_ln_g,
        ln_b=d_ln_b, pool_w=d_pool_w, pool_scale=d_pool_scale, final_g=d_final_g, meta=dh0_first[PAD_ROWS:], conv_w=d_conv_w,
        gate_w=d_gate_w, gate_b=d_gate_b, head_g=d_head_g)
    src.emit("cp", dict(cp_w_in_t=d_cp_w_in_t, small=small, loss=loss))
    return loss, dx, small


_REPLICATED = ("mix_norm_g", "ffn_norm_g", "cp_conv_b", "cp_ln_g", "cp_ln_b", "cp_pool_w", "cp_pool_scale", "final_norm_g")
_SMALL_SHARDED = ("meta_tokens", "cp_conv_w", "gla_gate_w2", "gla_gate_b", "gla_head_g")
_NAMES = ("meta_tokens", "mix_norm_g", "ffn_norm_g", "ffn_w1", "ffn_w2", "cp_w_in", "cp_conv_w", "cp_conv_b", "cp_ln_g",
          "cp_ln_b", "cp_pool_w", "cp_pool_scale", "cp_w_out", "gla_w_in", "gla_gate_w2", "gla_gate_b", "gla_head_g",
          "gla_w_out", "final_norm_g")
_SMALL_GRADS = ("mix_g", "ffn_g", "conv_b", "ln_g", "ln_b", "pool_w", "pool_scale", "final_g", "meta", "conv_w", "gate_w",
                "gate_b", "head_g")
_GROUPS = ("cp", "ffn0", "gla", "ffn1")
_TWO_LEG_GATHERS = ("cp", "ffn0", "ffn1")


class _Exchanges:
    def __init__(self, w, d):
        self.d = d
        small = [w[n].reshape(w[n].shape[-2:]) for n in _SMALL_SHARDED]
        self.small_shard_shapes = [w[n].shape for n in _SMALL_SHARDED]
        shards = dict(
            cp=[(w["cp_w_in"][0].T, BF16), (w["cp_w_out"][0], BF16)] + [(a, F32) for a in small],
            ffn0=[((w["ffn_w1"], 0), BF16), ((w["ffn_w2"], 0), BF16)],
            gla=[(w["gla_w_in"][0].T, BF16), (w["gla_w_out"][0], BF16)],
            ffn1=[((w["ffn_w1"], 1), BF16), ((w["ffn_w2"], 1), BF16)])
        self.gathers = {}
        self.sent = {}
        token = None
        for group in _GROUPS:
            lands = _place_own(shards[group], after=token, name=f"place_w_{group}")
            plan = _plan_gather_first if group in _TWO_LEG_GATHERS else _plan_gather_direct
            self.gathers[group], token = _exchange_start(plan, None, lands, after=token, name=f"start_w_{group}")
        self.token = token

    def fetch(self, group, after):
        d = self.d
        after = (list(after) if isinstance(after, (list, tuple)) else [after]) + [self.token]
        _, got = _exchange_wait(self.gathers[group], after, name=f"wait_w_{group}")
        if group in _TWO_LEG_GATHERS:
            got = _exchange_now(_plan_gather_relay, got, name=f"relay_w_{group}")
        if group in ("ffn0", "ffn1"):
            return dict(w1=got[0], w2=got[1])
        if group == "gla":
            return dict(gla_w_in_t=got[0], gla_w_out=got[1].reshape(d, d), gate_w=self.gate_w, gate_b=self.gate_b,
                        head_g=self.head_g)
        meta, conv_w, gate_w, self.gate_b, self.head_g = [_undo_column_split(a) for a in got[2:]]
        self.gate_w = jnp.pad(gate_w, ((0, GATE_PAD - GATE_RANK), (0, 0))).astype(BF16)
        return dict(cp_w_in_t=got[0].reshape(-1, d), cp_w_out=got[1].reshape(d, d), meta=meta,
                    conv_w=jnp.pad(conv_w, ((0, 1), (0, 0))))

    def emit(self, group, g):
        d = self.d
        if group == "ffn1":
            arrs = [g["w1"], g["w2"].reshape(N_DEV, -1, d)]
        elif group == "ffn0_w1":
            arrs = [g["w1"]]
        elif group == "ffn0_w2":
            arrs = [g["w2"].reshape(N_DEV, -1, d)]
        elif group == "gla":
            arrs = [g["gla_w_in_t"], g["gla_w_out"].reshape(N_DEV, d // N_DEV, d)]
        elif group == "cp_out":
            arrs = [g["cp_w_out"].reshape(N_DEV, d // N_DEV, d)]
        else:
            s = dict(g["small"])
            s.update(pool_w=s["pool_w"][None], conv_w=s["conv_w"][:CONV_WIDTH], gate_w=s["gate_w"][:GATE_RANK])
            own = [s[n] for n in _SMALL_GRADS[:len(_REPLICATED)]]
            own += [_column_split(s[n]).reshape((N_DEV,) + shape)
                    for n, shape in zip(_SMALL_GRADS[len(_REPLICATED):], self.small_shard_shapes)]
            plans = [_plan_to_all] * len(_REPLICATED) + [_plan_split_to_all] * len(_SMALL_SHARDED)
            lands = [lax.empty((N_DEV,) + a.shape, F32) for a in own[:len(_REPLICATED)]]
            lands += [lax.empty(a.shape, F32) for a in own[len(_REPLICATED):]]
            own.append(g["loss"])
            plans.append(_plan_to_all)
            lands.append(lax.empty((N_DEV,) + g["loss"].shape, F32))
            own.append(g["cp_w_in_t"].reshape(N_DEV, -1, d))
            plans.append(_plan_split_to_all)
            lands.append(lax.empty(own[-1].shape, BF16))
            self.sent[group], self.token = _exchange_start(plans, own, lands, after=self.token, name=f"start_g_{group}")
            return self.token
        self.sent[group], self.token = _exchange_start(_plan_split_to_all, arrs, [lax.empty(a.shape, a.dtype) for a in arrs],
                                                       after=self.token, name=f"start_g_{group}")
        return self.token

    def finish(self, w, mom, var):
        out = {}
        after = self.token

        def landed(group):
            own, got = _exchange_wait(self.sent[group], after, name=f"wait_g_{group}")
            return list(zip(own, got))

        def adam(n, parts, behind=None, transposed=False):
            by_row = transposed and w[n].shape[2] % 8 != 0
            if by_row:
                flip, back = (lambda a: jnp.transpose(a, (2, 0, 1))), (lambda a: jnp.transpose(a, (1, 2, 0)))
            else:
                flip = back = (lambda a: jnp.transpose(a, (0, 2, 1))) if transposed else (lambda a: a)
            res = _reduce_adam(parts, flip(w[n]), flip(mom[n]), flip(var[n]), by_row=by_row, after=behind, name=f"adam_{n}")
            out[n] = tuple(back(a) for a in res)
            return res[0]

        ffn1 = landed("ffn1")
        after = ffn1[0][1]
        gla = landed("gla")
        after = adam("gla_w_in", [gla[0]], transposed=True)
        after = adam("gla_w_out", [gla[1]], after)
        ffn0_w1 = landed("ffn0_w1")
        after = adam("ffn_w1", [ffn0_w1[0], ffn1[0]])
        ffn0_w2 = landed("ffn0_w2")
        after = adam("ffn_w2", [ffn0_w2[0], ffn1[1]])
        cp_out = landed("cp_out")
        after = adam("cp_w_out", [cp_out[0]])
        *small, loss, cp_w_in = landed("cp")
        names = _REPLICATED + _SMALL_SHARDED
        split = [False] * len(_REPLICATED) + [True] * len(_SMALL_SHARDED)
        small_new = _adam_small([own for own, _ in small], [got for _, got in small], split, [w[n] for n in names],
                                [mom[n] for n in names], [var[n] for n in names], name="adam_small")
        out.update(zip(names, small_new))
        out["loss"] = _sum8(*loss, name="sum_loss")[0, 0]
        adam("cp_w_in", [cp_w_in], small_new[0][0], transposed=True)
        return out


def kernel(x, meta_tokens, mix_norm_g, ffn_norm_g, ffn_w1, ffn_w2, cp_w_in, cp_conv_w, cp_conv_b, cp_ln_g, cp_ln_b, cp_pool_w, cp_pool_scale, cp_w_out, gla_w_in, gla_gate_w2, gla_gate_b, gla_head_g, gla_w_out, final_norm_g, loss_target, m_meta_tokens, m_mix_norm_g, m_ffn_norm_g, m_ffn_w1, m_ffn_w2, m_cp_w_in, m_cp_conv_w, m_cp_conv_b, m_cp_ln_g, m_cp_ln_b, m_cp_pool_w, m_cp_pool_scale, m_cp_w_out, m_gla_w_in, m_gla_gate_w2, m_gla_gate_b, m_gla_head_g, m_gla_w_out, m_final_norm_g, v_meta_tokens, v_mix_norm_g, v_ffn_norm_g, v_ffn_w1, v_ffn_w2, v_cp_w_in, v_cp_conv_w, v_cp_conv_b, v_cp_ln_g, v_cp_ln_b, v_cp_pool_w, v_cp_pool_scale, v_cp_w_out, v_gla_w_in, v_gla_gate_w2, v_gla_gate_b, v_gla_head_g, v_gla_w_out, v_final_norm_g):
    w = dict(meta_tokens=meta_tokens, mix_norm_g=mix_norm_g, ffn_norm_g=ffn_norm_g, ffn_w1=ffn_w1, ffn_w2=ffn_w2,
             cp_w_in=cp_w_in, cp_conv_w=cp_conv_w, cp_conv_b=cp_conv_b, cp_ln_g=cp_ln_g, cp_ln_b=cp_ln_b,
             cp_pool_w=cp_pool_w, cp_pool_scale=cp_pool_scale, cp_w_out=cp_w_out, gla_w_in=gla_w_in,
             gla_gate_w2=gla_gate_w2, gla_gate_b=gla_gate_b, gla_head_g=gla_head_g, gla_w_out=gla_w_out,
             final_norm_g=final_norm_g.reshape(1, -1))
    mom = dict(meta_tokens=m_meta_tokens, mix_norm_g=m_mix_norm_g, ffn_norm_g=m_ffn_norm_g, ffn_w1=m_ffn_w1, ffn_w2=m_ffn_w2,
               cp_w_in=m_cp_w_in, cp_conv_w=m_cp_conv_w, cp_conv_b=m_cp_conv_b, cp_ln_g=m_cp_ln_g, cp_ln_b=m_cp_ln_b,
               cp_pool_w=m_cp_pool_w, cp_pool_scale=m_cp_pool_scale, cp_w_out=m_cp_w_out, gla_w_in=m_gla_w_in,
               gla_gate_w2=m_gla_gate_w2, gla_gate_b=m_gla_gate_b, gla_head_g=m_gla_head_g, gla_w_out=m_gla_w_out,
               final_norm_g=m_final_norm_g.reshape(1, -1))
    var = dict(meta_tokens=v_meta_tokens, mix_norm_g=v_mix_norm_g, ffn_norm_g=v_ffn_norm_g, ffn_w1=v_ffn_w1, ffn_w2=v_ffn_w2,
               cp_w_in=v_cp_w_in, cp_conv_w=v_cp_conv_w, cp_conv_b=v_cp_conv_b, cp_ln_g=v_cp_ln_g, cp_ln_b=v_cp_ln_b,
               cp_pool_w=v_cp_pool_w, cp_pool_scale=v_cp_pool_scale, cp_w_out=v_cp_w_out, gla_w_in=v_gla_w_in,
               gla_gate_w2=v_gla_gate_w2, gla_gate_b=v_gla_gate_b, gla_head_g=v_gla_head_g, gla_w_out=v_gla_w_out,
               final_norm_g=v_final_norm_g.reshape(1, -1))
    d = x.shape[-1]
    replicated = dict(mix_g=w["mix_norm_g"], ffn_g=w["ffn_norm_g"], conv_b=w["cp_conv_b"], ln_g=w["cp_ln_g"],
                      ln_b=w["cp_ln_b"], pool_w=w["cp_pool_w"][0].astype(BF16), pool_scale=w["cp_pool_scale"],
                      final_g=w["final_norm_g"])
    exchanges = _Exchanges(w, d)
    _, grad_x, _ = _local_step(x[0], loss_target[0], replicated, exchanges)
    out = exchanges.finish(w, mom, var)
    loss = out.pop("loss")

    def leaf(n, k):
        a = out[n][k]
        return a.reshape(-1) if n == "final_norm_g" else a

    return (loss, grad_x[None], *[leaf(n, 0) for n in _NAMES], *[leaf(n, 1) for n in _NAMES],
            *[leaf(n, 2) for n in _NAMES], *[leaf(n, 3) for n in _NAMES])
```

```python
import functools

import jax
import jax.numpy as jnp
from jax import lax
from jax.experimental import pallas as pl
from jax.experimental.pallas import tpu as pltpu

F32, BF16 = jnp.float32, jnp.bfloat16
N_DEV = 8
CHUNK = 64
N_META = 16
PAD_ROWS = CHUNK - N_META
HALO = 32
EPS = 1e-5
CONV_WIDTH = 31
POOL_WINDOWS = (2, 4, 8, 16)
HEADS = 4
GATE_RANK = 16
GATE_NORM = 16.0
GATE_PAD = 128
ADAM_LR, ADAM_B1, ADAM_B2, ADAM_EPS, ADAM_WD, ADAM_STEP = 0.001, 0.9, 0.999, 1e-08, 0.01, 10
V7X_VMEM_LIMIT = 56 * 2 ** 20
LANE = 128


def _cparams(*sem):
    return pltpu.CompilerParams(dimension_semantics=sem, vmem_limit_bytes=V7X_VMEM_LIMIT)


def _row_tile(t, cap):
    best = CHUNK
    for r in range(CHUNK, min(t, cap) + 1, CHUNK):
        if t % r == 0:
            best = r
    return best


def _resident(shape):
    return pl.BlockSpec(shape, lambda *_: (0,) * len(shape), pipeline_mode=pl.Buffered(1))


def _dot(a, b):
    return jnp.dot(a, b, preferred_element_type=F32)


def _dot_nt(a, b):
    return lax.dot_general(a, b, (((1,), (1,)), ((), ())), preferred_element_type=F32)


def _dot_tn(a, b):
    return lax.dot_general(a, b, (((0,), (0,)), ((), ())), preferred_element_type=F32)


def _rowsum(a):
    return jnp.sum(a, axis=0, keepdims=True)


def _sigmoid(a):
    return 1.0 / (1.0 + jnp.exp(-a))


def _row_ids(tile, rt):
    return tile * rt + lax.broadcasted_iota(jnp.int32, (rt, 1), 0)


def _sds(shape, dtype):
    return jax.ShapeDtypeStruct(shape, dtype)


DW_ROWS = 1024


def _linear_bwd_w(x, dy, *, square_x=False, column_blocks=None, row_blocks=None, after=None, name):
    t, k = x.shape
    n = dy.shape[1]
    cut_k = k > n and column_blocks is None
    assert cut_k or row_blocks is None
    width = k if cut_k else n
    blk = n // column_blocks if column_blocks else max(c for c in (640, 512, 384, 256, LANE) if width % c == 0)
    dep_specs, deps = _dep_specs(after)

    def body(*refs):
        x_ref, dy_ref, o_ref, acc = refs[len(deps):]
        for c0 in range(0, t, DW_ROWS):
            rows = slice(c0, min(c0 + DW_ROWS, t))
            xv = x_ref[rows, :]
            if square_x:
                xv = xv.astype(F32)
                xv = xv * xv
            part = _dot_tn(xv.astype(BF16), dy_ref[rows, :].astype(BF16))
            if c0 == 0:
                acc[...] = part
            else:
                acc[...] += part
        if row_blocks is None:
            o_ref[...] = acc[...].astype(BF16)
            return
        nb, rpb = row_blocks
        for s in range(width // blk):
            @pl.when(pl.program_id(0) == s)
            def _(s=s):
                for b in range(nb):
                    lo, hi = max(s * blk, b * rpb), min((s + 1) * blk, (b + 1) * rpb)
                    if lo < hi:
                        o_ref[b, lo - b * rpb:hi - b * rpb, :] = acc[lo - s * blk:hi - s * blk, :].astype(BF16)

    out_shape = _sds((k, n), BF16)
    semantics = "parallel"
    if cut_k:
        in_specs = [pl.BlockSpec((t, blk), lambda j: (0, j)), _resident((t, n))]
        out_specs = pl.BlockSpec((blk, n), lambda j: (j, 0))
        acc_shape = (blk, n)
        if row_blocks:
            out_shape = _sds(row_blocks + (n,), BF16)
            out_specs = pl.BlockSpec(out_shape.shape, lambda j: (0, 0, 0))
            semantics = "arbitrary"
    else:
        in_specs = [_resident((t, k)), pl.BlockSpec((t, blk), lambda j: (0, j))]
        out_specs = pl.BlockSpec((k, blk), lambda j: (0, j))
        acc_shape = (k, blk)
        if column_blocks:
            out_specs = pl.BlockSpec((None, k, blk), lambda j: (j, 0, 0))
            out_shape = _sds((column_blocks, k, blk), BF16)
    return pl.pallas_call(
        body, grid=(width // blk,), in_specs=dep_specs + in_specs, out_specs=out_specs, out_shape=out_shape,
        scratch_shapes=[pltpu.VMEM(acc_shape, F32)], compiler_params=_cparams(semantics), name=name)(*deps, x, dy)


FFN_BLOCKS_PER_STEP = 2


def _ffn_fwd(h, gain, w1g, w2g, *, loss_head=None, name):
    t, d = h.shape
    f8 = w1g.shape[-1]
    rt = _row_tile(t, 832)
    nb = FFN_BLOCKS_PER_STEP
    nstep = N_DEV // nb

    def body(*refs):
        if loss_head is None:
            h_ref, g_ref, w1_ref, w2_ref, o_ref, u_ref, r_ref, acc_ref = refs
        else:
            (h_ref, g_ref, w1_ref, w2_ref, fg_ref, tgt_ref, o_ref, u_ref, r_ref, loss_ref, dfg_ref, acc_ref, t_ref,
             t_sem) = refs
        i, j = pl.program_id(0), pl.program_id(1)

        def target_rows(act):
            @pl.when(i == 0)
            def _():
                act(pltpu.make_async_copy(tgt_ref.at[pl.ds(0, rt - CHUNK)], t_ref.at[pl.ds(CHUNK, rt - CHUNK)], t_sem.at[0]))

            if t > rt:
                @pl.when(i > 0)
                def _():
                    act(pltpu.make_async_copy(tgt_ref.at[pl.ds(pl.multiple_of(i * rt - CHUNK, CHUNK), rt)], t_ref,
                                              t_sem.at[0]))

        @pl.when(j == 0)
        def _():
            if loss_head is not None:
                @pl.when(i == 0)
                def _():
                    t_ref[0:CHUNK, :] = jnp.zeros((CHUNK, d), F32)

                target_rows(lambda copy: copy.start())

            hv = h_ref[...]
            u_ref[...] = (hv * lax.rsqrt(jnp.mean(hv * hv, axis=-1, keepdims=True) + EPS) * g_ref[...]).astype(BF16)
            acc_ref[...] = jnp.zeros_like(acc_ref)

        part = None
        for b in range(nb):
            a = jnp.maximum(_dot(u_ref[...], w1_ref[b]), 0.0)
            r_ref[:, b * f8:(b + 1) * f8] = a.astype(BF16)
            term = _dot((a * a).astype(BF16), w2_ref[b])
            part = term if part is None else part + term
        acc_ref[...] += part

        @pl.when(j == nstep - 1)
        def _():
            y = h_ref[...] + acc_ref[...]
            if loss_head is None:
                o_ref[...] = y
                return

            @pl.when(i == 0)
            def _():
                loss_ref[...] = jnp.zeros_like(loss_ref)
                dfg_ref[...] = jnp.zeros_like(dfg_ref)

            target_rows(lambda copy: copy.wait())

            rstd = lax.rsqrt(jnp.mean(y * y, axis=-1, keepdims=True) + EPS)
            xh = y * rstd
            err = jnp.where(_row_ids(i, rt) >= CHUNK, xh * fg_ref[...] - t_ref[...], 0.0)
            loss_ref[...] += (0.5 / d) * jnp.sum(err * err)
            dy = err * (1.0 / d)
            dfg_ref[...] += _rowsum(dy * xh)
            dxh = dy * fg_ref[...]
            o_ref[...] = rstd * (dxh - xh * jnp.mean(dxh * xh, axis=-1, keepdims=True))

    rows = lambda i, j: (i, 0)
    in_specs = [pl.BlockSpec((rt, d), rows), _resident((1, d)),
                pl.BlockSpec((nb, d, f8), lambda i, j: (j, 0, 0)), pl.BlockSpec((nb, f8, d), lambda i, j: (j, 0, 0))]
    out_specs = [pl.BlockSpec((rt, d), rows), pl.BlockSpec((rt, d), rows), pl.BlockSpec((rt, nb * f8), lambda i, j: (i, j))]
    out_shape = [_sds((t, d), F32), _sds((t, d), BF16), _sds((t, N_DEV * f8), BF16)]
    args = [h, gain, w1g, w2g]
    scratch_shapes = [pltpu.VMEM((rt, d), F32)]
    if loss_head is not None:
        in_specs += [_resident((1, d)), pl.BlockSpec(memory_space=pl.ANY)]
        out_specs += [pl.BlockSpec((8, LANE), lambda i, j: (0, 0)), pl.BlockSpec((1, d), lambda i, j: (0, 0))]
        out_shape += [_sds((8, LANE), F32), _sds((1, d), F32)]
        args += list(loss_head)
        scratch_shapes += [pltpu.VMEM((rt, d), F32), pltpu.SemaphoreType.DMA((1,))]
    return pl.pallas_call(
        body, grid=(t // rt, nstep), in_specs=in_specs, out_specs=out_specs, out_shape=out_shape,
        scratch_shapes=scratch_shapes,
        compiler_params=_cparams("arbitrary" if loss_head is not None else "parallel", "arbitrary"), name=name)(*args)


def _ffn_bwd_x(h, dout, gain, r, w1g, w2g, *, after=None, name):
    t, d = h.shape
    f8 = w1g.shape[-1]
    rt = _row_tile(t, 832)
    nb = FFN_BLOCKS_PER_STEP
    last = N_DEV // nb - 1
    dep_specs, deps = _dep_specs(after)

    def body(*refs):
        h_ref, do_ref, g_ref, r_ref, w1_ref, w2_ref, dh_ref, dhh_ref, dob_ref, dg_ref, du_ref = refs[len(deps):]
        i, j = pl.program_id(0), pl.program_id(1)

        @pl.when(j == 0)
        def _():
            dob_ref[...] = do_ref[...].astype(BF16)
            du_ref[...] = jnp.zeros_like(du_ref)

        part = None
        for b in range(nb):
            cols = slice(b * f8, (b + 1) * f8)
            dhh = (_dot_nt(dob_ref[...], w2_ref[b]) * (2.0 * r_ref[:, cols].astype(F32))).astype(BF16)
            dhh_ref[:, cols] = dhh
            term = _dot_nt(dhh, w1_ref[b])
            part = term if part is None else part + term
        du_ref[...] += part

        @pl.when(j == last)
        def _():
            @pl.when(i == 0)
            def _():
                dg_ref[...] = jnp.zeros_like(dg_ref)

            hv = h_ref[...]
            rstd = lax.rsqrt(jnp.mean(hv * hv, axis=-1, keepdims=True) + EPS)
            xh = hv * rstd
            du = du_ref[...]
            dg_ref[...] += _rowsum(du * xh)
            dxh = du * g_ref[...]
            dh_ref[...] = do_ref[...] + rstd * (dxh - xh * jnp.mean(dxh * xh, axis=-1, keepdims=True))

    rows = lambda i, j: (i, 0)
    return pl.pallas_call(
        body, grid=(t // rt, N_DEV // nb),
        in_specs=dep_specs + [
                  pl.BlockSpec((rt, d), rows), pl.BlockSpec((rt, d), rows), _resident((1, d)),
                  pl.BlockSpec((rt, nb * f8), lambda i, j: (i, j)),
                  pl.BlockSpec((nb, d, f8), lambda i, j: (j, 0, 0)),
                  pl.BlockSpec((nb, f8, d), lambda i, j: (j, 0, 0))],
        out_specs=[pl.BlockSpec((rt, d), rows), pl.BlockSpec((rt, nb * f8), lambda i, j: (i, j)),
                   pl.BlockSpec((rt, d), rows), pl.BlockSpec((1, d), lambda i, j: (0, 0))],
        out_shape=[_sds((t, d), F32), _sds((t, N_DEV * f8), BF16), _sds((t, d), BF16), _sds((1, d), F32)],
        scratch_shapes=[pltpu.VMEM((rt, d), F32)],
        compiler_params=_cparams("arbitrary", "arbitrary"), name=name)(*deps, h, dout, gain, r, w1g, w2g)


def _lane_blocks(width):
    lb = min(LANE, width)
    return [slice(s, s + lb) for s in range(0, width, lb)]


def _conv_rows(src_ref, w_ref, offset, dst_ref, nblk, width, bias_ref=None):
    def blk(rb, carry):
        base = pl.multiple_of(rb * CHUNK, CHUNK)
        for l, ls in enumerate(_lane_blocks(width)):
            acc = jnp.zeros((CHUNK, ls.stop - ls.start), F32)
            if bias_ref is not None:
                acc = acc + bias_ref[:, ls]
            for k in range(CONV_WIDTH):
                acc = acc + w_ref[k:k + 1, ls] * src_ref[l, pl.ds(base + offset(k), CHUNK), :]
            dst_ref[l, pl.ds(base, CHUNK), :] = acc
        return carry

    lax.fori_loop(0, nblk, blk, 0)


def _to_lane_blocks(ref, row0, value):
    for l, ls in enumerate(_lane_blocks(value.shape[1])):
        ref[l, row0:row0 + value.shape[0], :] = value[:, ls]


def _from_lane_blocks(ref):
    return jnp.concatenate([ref[l] for l in range(ref.shape[0])], axis=1)


def _pool_counts(rows, window):
    return jnp.clip(rows - PAD_ROWS + 1, 1, window).astype(F32)


def _trailing_sum(v, window):
    s, sh = v, 1
    while sh < window:
        s = s + pltpu.roll(s, sh, 0)
        sh *= 2
    return s


def _leading_sum(v, window):
    s, sh, n = v, 1, v.shape[0]
    while sh < window:
        s = s + pltpu.roll(s, n - sh, 0)
        sh *= 2
    return s


def _norm_project(h_ref, g_ref, w_t_ref, u_ref, z_ref):
    hv = h_ref[...]
    u = (hv * lax.rsqrt(jnp.mean(hv * hv, axis=-1, keepdims=True) + EPS) * g_ref[...]).astype(BF16)
    u_ref[...] = u
    z_ref[...] = _dot_nt(u, w_t_ref[...])


def _project_back(dz_ref, w_t_ref, h_ref, g_ref, dres_ref, dh_ref, dg_ref, first):
    dx = _dot(dz_ref[...], w_t_ref[...])
    hv = h_ref[...]
    rstd = lax.rsqrt(jnp.mean(hv * hv, axis=-1, keepdims=True) + EPS)
    xh = hv * rstd

    @pl.when(first)
    def _():
        dg_ref[...] = jnp.zeros_like(dg_ref)

    dg_ref[...] += _rowsum(dx * xh)
    dxh = dx * g_ref[...]
    dh_ref[...] = dres_ref[...] + rstd * (dxh - xh * jnp.mean(dxh * xh, axis=-1, keepdims=True))


def _cp_mid_fwd(x, meta, gain, w_in_t, w_out, conv_w, conv_b, ln_g, ln_b, pool_w, pool_scale, *, name):
    seq, d = x.shape
    t = seq + CHUNK
    ein = w_in_t.shape[0]
    cd = conv_b.shape[1]
    pd = pool_scale.shape[1]
    pg = pd // len(POOL_WINDOWS)
    rt = _row_tile(t, 320)
    ntile = t // rt

    def body(x_ref, meta_ref, g_ref, wi_ref, wo_ref, cw_ref, cb_ref, lg_ref, lb_ref, pw_ref, ps_ref,
             ho_ref, o_ref, z_ref, u_ref, cv_ref, h0_ref, gext, pext, conv_s, hbuf, hsem):
        i = pl.program_id(0)
        slot = i % 2
        first_rows = pltpu.make_async_copy(x_ref.at[pl.ds(0, rt - CHUNK)], hbuf.at[0, pl.ds(CHUNK, rt - CHUNK)], hsem.at[0])

        def tile_rows(tile, to):
            return pltpu.make_async_copy(x_ref.at[pl.ds(pl.multiple_of(tile * rt - CHUNK, CHUNK), rt)], hbuf.at[to],
                                         hsem.at[to])

        @pl.when(i == 0)
        def _():
            first_rows.start()
            hbuf[0, 0:PAD_ROWS, :] = jnp.zeros((PAD_ROWS, d), F32)
            hbuf[0, PAD_ROWS:CHUNK, :] = meta_ref[...]
            _to_lane_blocks(gext, 0, jnp.zeros((HALO, cd), F32))
            pext[0:HALO, :] = jnp.zeros((HALO, pd), F32)

        @pl.when(i + 1 < ntile)
        def _():
            tile_rows(i + 1, 1 - slot).start()

        @pl.when(i == 0)
        def _():
            first_rows.wait()

        @pl.when(i > 0)
        def _():
            tile_rows(i, slot).wait()

        h_ref = hbuf.at[slot]
        h0_ref[...] = h_ref[...]
        _norm_project(h_ref, g_ref, wi_ref, u_ref, z_ref)

        _to_lane_blocks(gext, HALO, z_ref[:, 0:cd] * _sigmoid(z_ref[:, cd:2 * cd]))
        pext[HALO:HALO + rt, :] = z_ref[:, 2 * cd:]
        _conv_rows(gext, cw_ref, lambda k: k + HALO - (CONV_WIDTH - 1), conv_s, rt // CHUNK, cd, cb_ref)
        cv = _from_lane_blocks(conv_s)
        cv_ref[...] = cv
        xc = cv - jnp.mean(cv, axis=-1, keepdims=True)
        y = xc * lax.rsqrt(jnp.mean(xc * xc, axis=-1, keepdims=True) + EPS) * lg_ref[...] + lb_ref[...]
        rows = _row_ids(i, rt)
        a = jnp.where(rows >= PAD_ROWS, y * _sigmoid(y), 0.0)
        o_ref[:, 0:cd] = a.astype(BF16)
        for gi, window in enumerate(POOL_WINDOWS):
            ls = slice(gi * pg, (gi + 1) * pg)
            v = pext[:, ls]
            tm = _trailing_sum(v, window)[HALO:] / _pool_counts(rows, window) - v[HALO:]
            p = _dot(tm.astype(BF16), pw_ref[gi]) * ps_ref[:, ls]
            o_ref[:, cd + gi * pg:cd + (gi + 1) * pg] = p.astype(BF16)
        ho_ref[...] = h_ref[...] + _dot(o_ref[...], wo_ref[...])
        gext[:, 0:HALO, :] = gext[:, rt:rt + HALO, :]
        pext[0:HALO, :] = pext[rt:rt + HALO, :]

    nl, lb = len(_lane_blocks(cd)), min(LANE, cd)
    rows = lambda i: (i, 0)
    return pl.pallas_call(
        body, grid=(ntile,),
        in_specs=[pl.BlockSpec(memory_space=pl.ANY), _resident(meta.shape), _resident((1, d)), _resident(w_in_t.shape),
                  _resident(w_out.shape), _resident(conv_w.shape), _resident((1, cd)),
                  _resident((1, cd)), _resident((1, cd)), _resident(pool_w.shape), _resident((1, pd))],
        out_specs=[pl.BlockSpec((rt, d), rows), pl.BlockSpec((rt, cd + pd), rows), pl.BlockSpec((rt, ein), rows),
                   pl.BlockSpec((rt, d), rows), pl.BlockSpec((rt, cd), rows), pl.BlockSpec((rt, d), rows)],
        out_shape=[_sds((t, d), F32), _sds((t, cd + pd), BF16), _sds((t, ein), F32), _sds((t, d), BF16),
                   _sds((t, cd), F32), _sds((t, d), F32)],
        scratch_shapes=[pltpu.VMEM((nl, rt + HALO, lb), F32), pltpu.VMEM((rt + HALO, pd), F32),
                        pltpu.VMEM((nl, rt, lb), F32), pltpu.VMEM((2, rt, d), F32), pltpu.SemaphoreType.DMA((2,))],
        compiler_params=_cparams("arbitrary"), name=name)(x, meta, gain, w_in_t, w_out, conv_w, conv_b, ln_g, ln_b, pool_w,
                                                          pool_scale)


def _cp_mid_bwd(z, cv, h, gain, w_in_t, dh, w_out, conv_w, conv_b, ln_g, ln_b, pool_w, pool_scale, *, after=None, name):
    t, ein = z.shape
    cd = conv_b.shape[1]
    pd = pool_scale.shape[1]
    pg = pd // len(POOL_WINDOWS)
    rt = _row_tile(t, 320)
    ntile = t // rt
    per = rt // CHUNK
    dep_specs, deps = _dep_specs(after)

    def body(*refs):
        (z_ref, zh_ref, cv_ref, h_ref, g_ref, wi_ref, dh_ref, wo_ref, cw_ref, cb_ref, lg_ref, lb_ref, pw_ref, ps_ref,
         dx_ref, dfirst_ref, dg_ref, dz_ref, dcw_ref, dcb_ref, dlg_ref, dlb_ref, dpw_ref, dps_ref,
         gext, pext, conv_s, dcv, dsp, dhi, dx_sem) = refs[len(deps):]
        step = pl.program_id(0)
        tile = ntile - 1 - step
        slot = step % 2
        last_slot = (ntile - 1) % 2
        first_rows = pltpu.make_async_copy(dhi.at[last_slot, pl.ds(CHUNK, rt - CHUNK)], dx_ref.at[pl.ds(0, rt - CHUNK)],
                                           dx_sem.at[last_slot])

        def tile_rows(tile, slot):
            return pltpu.make_async_copy(dhi.at[slot], dx_ref.at[pl.ds(pl.multiple_of(tile * rt - CHUNK, CHUNK), rt)],
                                         dx_sem.at[slot])

        dcat = _dot_nt(dh_ref[...].astype(BF16), wo_ref[...])

        @pl.when(step >= 2)
        def _():
            tile_rows(tile + 2, slot).wait()

        @pl.when(step == 0)
        def _():
            for ref in (dcw_ref, dcb_ref, dlg_ref, dlb_ref, dpw_ref, dps_ref):
                ref[...] = jnp.zeros_like(ref)
            _to_lane_blocks(dcv, rt, jnp.zeros((HALO, cd), F32))
            dsp[rt:rt + HALO, :] = jnp.zeros((HALO, pd), F32)

        keep = jnp.where(tile > 0, 1.0, 0.0)
        zh = zh_ref[CHUNK - HALO:CHUNK, :]
        _to_lane_blocks(gext, 0, keep * zh[:, 0:cd] * _sigmoid(zh[:, cd:2 * cd]))
        pext[0:HALO, :] = keep * zh[:, 2 * cd:]
        za = z_ref[:, 0:cd]
        sg = _sigmoid(z_ref[:, cd:2 * cd])
        _to_lane_blocks(gext, HALO, za * sg)
        pext[HALO:HALO + rt, :] = z_ref[:, 2 * cd:]
        cv = cv_ref[...]
        xc = cv - jnp.mean(cv, axis=-1, keepdims=True)
        rstd = lax.rsqrt(jnp.mean(xc * xc, axis=-1, keepdims=True) + EPS)
        xh = xc * rstd
        y = xh * lg_ref[...] + lb_ref[...]
        sy = _sigmoid(y)
        rows = _row_ids(tile, rt)
        da = jnp.where(rows >= PAD_ROWS, dcat[:, 0:cd], 0.0)
        dy = da * (sy * (1.0 + y * (1.0 - sy)))
        dlg_ref[...] += _rowsum(dy * xh)
        dlb_ref[...] += _rowsum(dy)
        dxh = dy * lg_ref[...]
        dconv = rstd * (dxh - jnp.mean(dxh, axis=-1, keepdims=True) - xh * jnp.mean(dxh * xh, axis=-1, keepdims=True))
        dcb_ref[...] += _rowsum(dconv)
        _to_lane_blocks(dcv, 0, dconv)
        for l, ls in enumerate(_lane_blocks(cd)):
            def acc_rows(rb, accs, l=l):
                base = pl.multiple_of(rb * CHUNK, CHUNK)
                d_blk = dcv[l, pl.ds(base, CHUNK), :]
                out = []
                for k in range(CONV_WIDTH):
                    prod = d_blk * gext[l, pl.ds(base + k + HALO - (CONV_WIDTH - 1), CHUNK), :]
                    part = prod[0:8]
                    for s in range(8, CHUNK, 8):
                        part = part + prod[s:s + 8]
                    out.append(accs[k] + part)
                return tuple(out)

            zero = jnp.zeros((8, ls.stop - ls.start), F32)
            accs = lax.fori_loop(0, per, acc_rows, (zero,) * CONV_WIDTH)
            for k in range(CONV_WIDTH):
                dcw_ref[k:k + 1, ls] += _rowsum(accs[k])
        _conv_rows(dcv, cw_ref, lambda k: CONV_WIDTH - 1 - k, conv_s, per, cd)
        dglu = _from_lane_blocks(conv_s)
        dz_ref[:, 0:cd] = (dglu * sg).astype(BF16)
        dz_ref[:, cd:2 * cd] = (dglu * za * sg * (1.0 - sg)).astype(BF16)
        dcv[:, rt:rt + HALO, :] = dcv[:, 0:HALO, :]
        for gi, window in enumerate(POOL_WINDOWS):
            ls = slice(gi * pg, (gi + 1) * pg)
            v = pext[:, ls]
            cnt = _pool_counts(rows, window)
            tm = (_trailing_sum(v, window)[HALO:] / cnt - v[HALO:]).astype(BF16)
            dp = dcat[:, cd + gi * pg:cd + (gi + 1) * pg]
            dps_ref[:, ls] += _rowsum(dp * _dot(tm, pw_ref[gi]))
            dpl = (dp * ps_ref[:, ls]).astype(BF16)
            dpw_ref[gi] += _dot_tn(tm, dpl)
            dtm = _dot_nt(dpl, pw_ref[gi])
            dsp[0:rt, ls] = dtm / cnt
            dpin = _leading_sum(dsp[:, ls], window)[0:rt] - dtm
            dz_ref[:, 2 * cd + gi * pg:2 * cd + (gi + 1) * pg] = dpin.astype(BF16)
        dsp[rt:rt + HALO, :] = dsp[0:HALO, :]

        _project_back(dz_ref, wi_ref, h_ref, g_ref, dh_ref, dhi.at[slot], dg_ref, step == 0)

        @pl.when(tile > 0)
        def _():
            tile_rows(tile, slot).start()

        @pl.when(tile == 0)
        def _():
            first_rows.start()
            dfirst_ref[...] = dhi[last_slot, 0:CHUNK, :]
            if ntile > 1:
                tile_rows(1, 1 - last_slot).wait()
            first_rows.wait()

    d = h.shape[1]
    back = lambda i: (ntile - 1 - i, 0)
    halo_idx = lambda i: (jnp.maximum((ntile - 1 - i) * per - 1, 0), 0)
    const2 = lambda i: (0, 0)
    nl, lb = len(_lane_blocks(cd)), min(LANE, cd)
    return pl.pallas_call(
        body, grid=(ntile,),
        in_specs=dep_specs + [
                  pl.BlockSpec((rt, ein), back), pl.BlockSpec((CHUNK, ein), halo_idx), pl.BlockSpec((rt, cd), back),
                  pl.BlockSpec((rt, d), back),
                  _resident((1, d)), _resident(w_in_t.shape), pl.BlockSpec((rt, d), back), _resident(w_out.shape),
                  _resident(conv_w.shape), _resident((1, cd)), _resident((1, cd)), _resident((1, cd)),
                  _resident(pool_w.shape), _resident((1, pd))],
        out_specs=[pl.BlockSpec(memory_space=pl.ANY), pl.BlockSpec((CHUNK, d), const2), pl.BlockSpec((1, d), const2),
                   pl.BlockSpec((rt, ein), back), pl.BlockSpec(conv_w.shape, const2), pl.BlockSpec((1, cd), const2),
                   pl.BlockSpec((1, cd), const2), pl.BlockSpec((1, cd), const2),
                   pl.BlockSpec(pool_w.shape, lambda i: (0, 0, 0)), pl.BlockSpec((1, pd), const2)],
        out_shape=[_sds((t - CHUNK, d), F32), _sds((CHUNK, d), F32), _sds((1, d), F32),
                   _sds((t, ein), BF16), _sds(conv_w.shape, F32), _sds((1, cd), F32), _sds((1, cd), F32),
                   _sds((1, cd), F32), _sds(pool_w.shape, F32), _sds((1, pd), F32)],
        scratch_shapes=[pltpu.VMEM((nl, rt + HALO, lb), F32), pltpu.VMEM((rt + HALO, pd), F32), pltpu.VMEM((nl, rt, lb), F32),
                        pltpu.VMEM((nl, rt + HALO, lb), F32), pltpu.VMEM((rt + HALO, pd), F32), pltpu.VMEM((2, rt, d), F32),
                        pltpu.SemaphoreType.DMA((2,))],
        compiler_params=_cparams("arbitrary"), name=name)(*deps, z, z, cv, h, gain, w_in_t, dh, w_out, conv_w, conv_b, ln_g,
                                                          ln_b, pool_w, pool_scale)


def _log_decay(r, gw_ref, gb_ref, rows):
    gp = _dot(r.astype(BF16), gw_ref[...]) + gb_ref[...]
    log_sig = jnp.minimum(gp, 0.0) - jnp.log(1.0 + jnp.exp(-jnp.abs(gp)))
    return gp, jnp.where(rows >= PAD_ROWS, log_sig / GATE_NORM, 0.0)


def _tri(strict):
    r = lax.broadcasted_iota(jnp.int32, (CHUNK, CHUNK), 0)
    c = lax.broadcasted_iota(jnp.int32, (CHUNK, CHUNK), 1)
    return jnp.where(c < r if strict else c <= r, 1.0, 0.0).astype(BF16)


def _tri_dot(tri, a):
    hi = a.astype(BF16)
    rest = a - hi.astype(F32)
    mid = rest.astype(BF16)
    lo = (rest - mid.astype(F32)).astype(BF16)
    return _dot(tri, hi) + _dot(tri, mid) + _dot(tri, lo)


def _gla_mid_fwd(h, gain, w_in_blocks, w_out, gate_w, gate_b, head_g, *, name):
    t = h.shape[0]
    nblk, rpb = w_in_blocks.shape[:2]
    dk = gate_b.shape[1]
    hv = head_g.shape[1]
    hk = dk // HEADS
    dv = hv * HEADS
    r_at = 2 * dk + 2 * dv
    assert nblk * rpb == r_at + GATE_RANK
    zw = r_at + GATE_PAD
    rt = _row_tile(t, 320)
    per = rt // CHUNK
    scale = hk ** -0.5

    def body(h_ref, g_ref, wb_ref, wo_ref, gw_ref, gb_ref, hg_ref, ho_ref, o_ref, st_ref, z_ref, u_ref, wi_ref,
             s_ref, la_ref, dec_ref):
        i = pl.program_id(0)

        @pl.when(i == 0)
        def _():
            s_ref[...] = jnp.zeros_like(s_ref)
            for b in range(nblk):
                wi_ref[b * rpb:(b + 1) * rpb, :] = wb_ref[b]
            wi_ref[nblk * rpb:, :] = jnp.zeros((zw - nblk * rpb, wi_ref.shape[1]), BF16)

        _norm_project(h_ref, g_ref, wi_ref, u_ref, z_ref)

        _, la = _log_decay(z_ref[:, r_at:r_at + GATE_PAD], gw_ref, gb_ref, _row_ids(i, rt))
        la_ref[...] = la
        tri = _tri(False)

        def chunk_rows(c):
            return slice(c * CHUNK, (c + 1) * CHUNK)

        def decays(c, carry):
            rows = chunk_rows(c)
            la_c = la_ref[rows, :]
            cum = _tri_dot(tri, la_c)
            dec_ref[rows, :] = jnp.exp(_rowsum(la_c) - cum)
            return carry

        def states(c, carry):
            rows = chunk_rows(c)
            etot = jnp.exp(_rowsum(la_ref[rows, :]))
            for hd in range(HEADS):
                ks = slice(hd * hk, (hd + 1) * hk)
                kd = z_ref[rows, dk + hd * hk:dk + (hd + 1) * hk] * dec_ref[rows, ks]
                v = z_ref[rows, 2 * dk + hd * hv:2 * dk + (hd + 1) * hv]
                s_new = s_ref[hd] * etot[:, ks] + _dot_tn(v.astype(BF16), kd.astype(BF16))
                s_ref[hd] = s_new
                st_ref[c, hd] = s_new
            return carry

        def outputs(c, carry):
            rows = chunk_rows(c)
            for hd in range(HEADS):
                q = z_ref[rows, hd * hk:(hd + 1) * hk] * scale
                g = z_ref[rows, 2 * dk + dv + hd * hv:2 * dk + dv + (hd + 1) * hv]
                o = _dot_nt(q.astype(BF16), st_ref[c, hd].astype(BF16))
                on = o * lax.rsqrt(jnp.mean(o * o, axis=-1, keepdims=True) + EPS) * hg_ref[...]
                o_ref[rows, hd * hv:(hd + 1) * hv] = (on * (g * _sigmoid(g))).astype(BF16)
            return carry

        for phase in (decays, states, outputs):
            for c in range(per):
                phase(c, 0)
        ho_ref[...] = h_ref[...] + _dot(o_ref[...], wo_ref[...])

    d = h.shape[1]
    rows = lambda i: (i, 0)
    return pl.pallas_call(
        body, grid=(t // rt,),
        in_specs=[pl.BlockSpec((rt, d), rows), _resident((1, d)), _resident(w_in_blocks.shape), _resident(w_out.shape),
                  _resident(gate_w.shape), _resident((1, dk)), _resident((1, hv))],
        out_specs=[pl.BlockSpec((rt, d), rows), pl.BlockSpec((rt, dv), rows),
                   pl.BlockSpec((per, HEADS, hv, hk), lambda i: (i, 0, 0, 0)), pl.BlockSpec((rt, zw), rows),
                   pl.BlockSpec((rt, d), rows), pl.BlockSpec((zw, d), lambda i: (0, 0))],
        out_shape=[_sds((t, d), F32), _sds((t, dv), BF16), _sds((t // CHUNK, HEADS, hv, hk), F32), _sds((t, zw), F32),
                   _sds((t, d), BF16), _sds((zw, d), BF16)],
        scratch_shapes=[pltpu.VMEM((HEADS, hv, hk), F32), pltpu.VMEM((rt, dk), F32), pltpu.VMEM((rt, dk), F32)],
        compiler_params=_cparams("arbitrary"), name=name)(h, gain, w_in_blocks, w_out, gate_w, gate_b, head_g)


def _gla_mid_bwd(z, h, gain, w_in_t, dh, w_out, states, gate_w, gate_b, head_g, *, after=None, name):
    t = z.shape[0]
    dk = gate_b.shape[1]
    hv = head_g.shape[1]
    hk = dk // HEADS
    dv = hv * HEADS
    r_at = 2 * dk + 2 * dv
    rt = _row_tile(t, 320)
    ntile = t // rt
    per = rt // CHUNK
    scale = hk ** -0.5
    dep_specs, deps = _dep_specs(after)

    def body(*refs):
        (z_ref, h_ref, g_ref, wi_ref, dh_ref, wo_ref, st_ref, stp_ref, gw_ref, gb_ref, hg_ref,
         dhi_ref, dg_ref, dz_ref, dgw_ref, dgb_ref, dhg_ref,
         ds_ref, la_ref, dla_ref, dec_ref, dos_ref, e_ref, do_ref) = refs[len(deps):]
        step = pl.program_id(0)
        tile = ntile - 1 - step
        do_ref[...] = _dot_nt(dh_ref[...].astype(BF16), wo_ref[...])

        @pl.when(step == 0)
        def _():
            ds_ref[...] = jnp.zeros_like(ds_ref)
            dgw_ref[...] = jnp.zeros_like(dgw_ref)
            dgb_ref[...] = jnp.zeros_like(dgb_ref)
            dhg_ref[...] = jnp.zeros_like(dhg_ref)

        rows_id = _row_ids(tile, rt)
        r = z_ref[:, r_at:r_at + GATE_PAD]
        gp, la = _log_decay(r, gw_ref, gb_ref, rows_id)
        la_ref[...] = la
        tri, tri_strict = _tri(False), _tri(True)
        keep = jnp.where(tile > 0, 1.0, 0.0)

        def chunk_rows(c):
            return slice(c * CHUNK, (c + 1) * CHUNK)

        def recompute(c, dhg):
            rows = chunk_rows(c)
            la_c = la_ref[rows, :]
            cum = _tri_dot(tri, la_c)
            dec_ref[rows, :] = jnp.exp(_rowsum(la_c) - cum)
            for hd in range(HEADS):
                q = (z_ref[rows, hd * hk:(hd + 1) * hk] * scale).astype(BF16)
                g = z_ref[rows, 2 * dk + dv + hd * hv:2 * dk + dv + (hd + 1) * hv]
                s_b = st_ref[c, hd].astype(BF16)
                o = _dot_nt(q, s_b)
                rstd = lax.rsqrt(jnp.mean(o * o, axis=-1, keepdims=True) + EPS)
                oh = o * rstd
                sg = _sigmoid(g)
                d_og = do_ref[rows, hd * hv:(hd + 1) * hv]
                dz_ref[rows, 2 * dk + dv + hd * hv:2 * dk + dv + (hd + 1) * hv] = (
                    d_og * oh * hg_ref[...] * (sg * (1.0 + g * (1.0 - sg)))).astype(BF16)
                don = d_og * (g * sg)
                dhg = dhg + _rowsum(don * oh)
                doh = don * hg_ref[...]
                d_o = (rstd * (doh - oh * jnp.mean(doh * oh, axis=-1, keepdims=True))).astype(BF16)
                dos_ref[rows, hd * hv:(hd + 1) * hv] = d_o
                dz_ref[rows, hd * hk:(hd + 1) * hk] = (_dot(d_o, s_b) * scale).astype(BF16)
            return dhg

        def recurrence(cc, carry):
            c = per - 1 - cc
            rows = chunk_rows(c)
            etot = jnp.exp(_rowsum(la_ref[rows, :]))
            for hd in range(HEADS):
                ks = slice(hd * hk, (hd + 1) * hk)
                q = (z_ref[rows, hd * hk:(hd + 1) * hk] * scale).astype(BF16)
                dec = dec_ref[rows, ks]
                kd = z_ref[rows, dk + hd * hk:dk + (hd + 1) * hk] * dec
                v = z_ref[rows, 2 * dk + hd * hv:2 * dk + (hd + 1) * hv].astype(BF16)
                s_prev = st_ref[c - 1, hd] if c > 0 else keep * stp_ref[0, hd]
                ds_t = ds_ref[hd] + _dot_tn(dos_ref[rows, hd * hv:(hd + 1) * hv], q)
                ds_b = ds_t.astype(BF16)
                dkd = _dot(v, ds_b)
                dz_ref[rows, 2 * dk + hd * hv:2 * dk + (hd + 1) * hv] = _dot_nt(kd.astype(BF16), ds_b).astype(BF16)
                dtot = etot[:, ks] * _rowsum(ds_t * s_prev)
                ds_ref[hd] = ds_t * etot[:, ks]
                dz_ref[rows, dk + hd * hk:dk + (hd + 1) * hk] = (dkd * dec).astype(BF16)
                e_ref[rows, ks] = dkd * kd
                dla_ref[rows, ks] = jnp.broadcast_to(dtot, (CHUNK, hk))
            return carry

        def decay_cotangent(c, carry):
            rows = chunk_rows(c)
            dla_ref[rows, :] += _tri_dot(tri_strict, e_ref[rows, :])
            return carry

        dhg = jnp.zeros((1, hv), F32)
        for c in range(per):
            dhg = recompute(c, dhg)
        dhg_ref[...] += dhg
        for phase in (recurrence, decay_cotangent):
            for c in range(per):
                phase(c, 0)
        dla = jnp.where(rows_id >= PAD_ROWS, dla_ref[...], 0.0)
        dgp = dla * (1.0 / GATE_NORM) * (1.0 - _sigmoid(gp))
        dgb_ref[...] += _rowsum(dgp)
        dgp_b = dgp.astype(BF16)
        dgw_ref[...] += _dot_tn(r.astype(BF16), dgp_b)
        dz_ref[:, r_at:r_at + GATE_PAD] = _dot_nt(dgp_b, gw_ref[...]).astype(BF16)
        _project_back(dz_ref, wi_ref, h_ref, g_ref, dh_ref, dhi_ref, dg_ref, step == 0)

    d = h.shape[1]
    back = lambda i: (ntile - 1 - i, 0)
    const2 = lambda i: (0, 0)
    return pl.pallas_call(
        body, grid=(ntile,),
        in_specs=dep_specs + [
                  pl.BlockSpec((rt, z.shape[1]), back), pl.BlockSpec((rt, d), back), _resident((1, d)),
                  _resident(w_in_t.shape), pl.BlockSpec((rt, d), back), _resident(w_out.shape),
                  pl.BlockSpec((per, HEADS, hv, hk), lambda i: (ntile - 1 - i, 0, 0, 0)),
                  pl.BlockSpec((1, HEADS, hv, hk), lambda i: (jnp.maximum((ntile - 1 - i) * per - 1, 0), 0, 0, 0)),
                  _resident(gate_w.shape), _resident((1, dk)), _resident((1, hv))],
        out_specs=[pl.BlockSpec((rt, d), back), pl.BlockSpec((1, d), const2), pl.BlockSpec((rt, z.shape[1]), back),
                   pl.BlockSpec(gate_w.shape, const2), pl.BlockSpec((1, dk), const2), pl.BlockSpec((1, hv), const2)],
        out_shape=[_sds((t, d), F32), _sds((1, d), F32), _sds(z.shape, BF16), _sds(gate_w.shape, F32),
                   _sds((1, dk), F32), _sds((1, hv), F32)],
        scratch_shapes=[pltpu.VMEM((HEADS, hv, hk), F32), pltpu.VMEM((rt, dk), F32), pltpu.VMEM((rt, dk), F32),
                        pltpu.VMEM((rt, dk), F32), pltpu.VMEM((rt, dv), BF16), pltpu.VMEM((rt, dk), F32),
                        pltpu.VMEM((rt, dv), F32)],
        compiler_params=_cparams("arbitrary"), name=name)(*deps, z, h, gain, w_in_t, dh, w_out, states, states, gate_w,
                                                          gate_b, head_g)


def _adamw_math(w, g, m, v):
    m = ADAM_B1 * m + (1.0 - ADAM_B1) * g
    v = ADAM_B2 * v + (1.0 - ADAM_B2) * (g * g)
    m_hat = m / (1.0 - ADAM_B1 ** ADAM_STEP)
    v_hat = v / (1.0 - ADAM_B2 ** ADAM_STEP)
    return -ADAM_LR * (m_hat / (jnp.sqrt(v_hat) + ADAM_EPS) + ADAM_WD * w), m, v


N_CHIP = N_DEV // 2
BLOCK_ELEMS = 128 * 1024


def _my_slot():
    return 4 * lax.axis_index("x") + 2 * lax.axis_index("y") + lax.axis_index("c")


def _row_block(r, c):
    cap = max(8, BLOCK_ELEMS // (-(-c // LANE) * LANE))
    return max([b for b in range(8, r + 1, 8) if r % b == 0 and b <= cap] or [r])


def _blocks(r, c):
    rb = _row_block(r, c)
    if rb < r or r * c <= BLOCK_ELEMS:
        return rb, c
    return r, max([b for b in (512, 256, LANE) if c % b == 0 and r * b <= BLOCK_ELEMS] or [c])


def _reduce_adam(parts, w, m, v, *, by_row=False, after=None, name):
    nl, r, c = (1, w.shape[0], w.shape[2]) if by_row else w.shape
    rb, cb = _blocks(r, c)
    dep_specs, deps = _dep_specs(after)
    whole = (slice(None), 0, slice(None)) if by_row else Ellipsis

    def body(*refs):
        me = refs[0][0]
        refs = refs[1 + len(deps):]
        p_refs = refs[:2 * nl]
        w_ref, m_ref, v_ref, g_out, d_out, m_out, v_out = refs[2 * nl:]
        layer = pl.program_id(0)
        for li in range(nl):
            @pl.when(layer == li)
            def _(li=li):
                own_ref, land_ref = p_refs[2 * li], p_refs[2 * li + 1]
                mine = own_ref[...].astype(F32)
                g = None
                for dev in range(N_DEV):
                    term = jnp.where(me == dev, mine, land_ref[dev].astype(F32))
                    g = term if g is None else g + term
                g_out[whole] = g
                d_out[whole], m_out[whole], v_out[whole] = _adamw_math(w_ref[whole], g, m_ref[whole], v_ref[whole])

    if by_row:
        blk = pl.BlockSpec((rb, 1, cb), lambda l, i, j, me: (i, 0, j))
    else:
        blk = pl.BlockSpec((None, rb, cb), lambda l, i, j, me: (l, i, j))
    p_specs = []
    for li in range(nl):
        p_specs += [
            pl.BlockSpec((None, rb, cb), lambda l, i, j, me, li=li: (me[0], jnp.where(l == li, i, 0), jnp.where(l == li, j, 0))),
            pl.BlockSpec((N_DEV, rb, cb), lambda l, i, j, me, li=li: (0, jnp.where(l == li, i, 0), jnp.where(l == li, j, 0)))]
    flat = [p for pair in parts for p in pair]
    grid_spec = pltpu.PrefetchScalarGridSpec(
        num_scalar_prefetch=1, grid=(nl, r // rb, c // cb), in_specs=dep_specs + p_specs + [blk, blk, blk],
        out_specs=[blk] * 4)
    return pl.pallas_call(
        body, grid_spec=grid_spec, out_shape=[_sds(w.shape, F32)] * 4,
        compiler_params=_cparams("arbitrary", "arbitrary", "arbitrary"), name=name)(
        _my_slot().reshape(1), *deps, *flat, w, m, v)


def _sum8(own, landed, *, name):
    def body(own_ref, land_ref, o_ref):
        me = _my_slot()
        total = None
        for dev in range(N_DEV):
            term = jnp.where(me == dev, own_ref[...], land_ref[dev])
            total = term if total is None else total + term
        o_ref[...] = total

    return pl.pallas_call(body, out_shape=_sds(own.shape, F32), name=name)(own, landed)


def _adam_small(own, landed, split, w, m, v, *, name):
    n = len(w)

    def body(*refs):
        own_refs, land_refs, w_refs, m_refs, v_refs = (refs[k * n:(k + 1) * n] for k in range(5))
        outs = refs[5 * n:]
        me = _my_slot()
        for k in range(n):
            mine = own_refs[k][me] if split[k] else own_refs[k][...]
            g = None
            for dev in range(N_DEV):
                term = jnp.where(me == dev, mine, land_refs[k][dev])
                g = term if g is None else g + term
            outs[4 * k][...] = g
            outs[4 * k + 1][...], outs[4 * k + 2][...], outs[4 * k + 3][...] = _adamw_math(
                w_refs[k][...], g, m_refs[k][...], v_refs[k][...])

    out = pl.pallas_call(body, out_shape=[_sds(a.shape, F32) for a in w for _ in range(4)],
                         compiler_params=pltpu.CompilerParams(vmem_limit_bytes=V7X_VMEM_LIMIT), name=name)(
        *own, *landed, *w, *m, *v)
    return [tuple(out[4 * k:4 * k + 4]) for k in range(n)]


_HBM = pl.BlockSpec(memory_space=pltpu.HBM)
_SEM = pl.BlockSpec(memory_space=pltpu.SEMAPHORE)
_DATAFLOW = pltpu.SideEffectType.DATAFLOW_SIDE_EFFECTING


def _plan_to_all(src, land):
    x, y, c = lax.axis_index("x"), lax.axis_index("y"), lax.axis_index("c")
    return [(src, land.at[_my_slot()], (x ^ ((d >> 2) & 1), y ^ ((d >> 1) & 1), c ^ (d & 1))) for d in range(1, N_DEV)]


def _plan_split_to_all(src, land):
    x, y, c = lax.axis_index("x"), lax.axis_index("y"), lax.axis_index("c")
    peers = [(x ^ ((d >> 2) & 1), y ^ ((d >> 1) & 1), c ^ (d & 1)) for d in range(1, N_DEV)]
    return [(src.at[4 * px + 2 * py + pc], land.at[_my_slot()], (px, py, pc)) for px, py, pc in peers]


_PLAN_COPIES = {_plan_to_all: N_DEV - 1, _plan_split_to_all: N_DEV - 1}


def _plans(plan, n):
    return list(plan) if isinstance(plan, (list, tuple)) else [plan] * n


def _exchange_copies(plan, ins, lands, send, recv):
    copies, sem = [], 0
    for p, src, land in zip(_plans(plan, len(lands)), ins, lands):
        for s, dst, dev in p(src, land):
            copies.append(pltpu.make_async_remote_copy(
                src_ref=s, dst_ref=dst, send_sem=send.at[sem], recv_sem=recv.at[sem],
                device_id=dev, device_id_type=pl.DeviceIdType.MESH))
            sem += 1
    return copies


def _place_own(srcs, *, after=None, name):
    dep_specs, deps = _dep_specs(after)
    n = len(srcs)
    arrays, in_specs, shapes = [], [], []
    for a, dtype in srcs:
        if isinstance(a, tuple):
            a, layer = a
            in_specs.append(pl.BlockSpec((None,) + a.shape[1:], lambda i, layer=layer: (layer, 0, 0)))
            shapes.append(a.shape[1:])
        else:
            in_specs.append(pl.BlockSpec(a.shape, lambda i: (0, 0)))
            shapes.append(a.shape)
        arrays.append(a)
    dtypes = [dtype for _, dtype in srcs]

    def body(*refs):
        refs = refs[len(deps):]
        a_refs, o_refs, cast_refs, sem = refs[:n], refs[n:2 * n], refs[2 * n:3 * n], refs[3 * n]
        me = _my_slot()
        copies = []
        for k in range(n):
            cast_refs[k][...] = a_refs[k][...].astype(dtypes[k])
            copies.append(pltpu.make_async_copy(cast_refs[k], o_refs[k].at[me], sem.at[k]))
            copies[-1].start()
        for cp in copies:
            cp.wait()

    return pl.pallas_call(
        body, grid=(1,), in_specs=dep_specs + in_specs, out_specs=[pl.BlockSpec(memory_space=pl.ANY)] * n,
        out_shape=[_sds((N_DEV,) + shape, dtype) for shape, dtype in zip(shapes, dtypes)],
        scratch_shapes=[pltpu.VMEM(shape, dtype) for shape, dtype in zip(shapes, dtypes)] + [pltpu.SemaphoreType.DMA((n,))],
        compiler_params=pltpu.CompilerParams(vmem_limit_bytes=V7X_VMEM_LIMIT), name=name)(*deps, *arrays)


def _plan_gather_first(land, _):
    x, y, c = lax.axis_index("x"), lax.axis_index("y"), lax.axis_index("c")
    mine = land.at[_my_slot()]
    return [(mine, mine, (x, y, 1 - c))] + [(mine, mine, (x ^ (d >> 1), y ^ (d & 1), c)) for d in range(1, N_CHIP)]


def _plan_gather_relay(land, _):
    x, y, c = lax.axis_index("x"), lax.axis_index("y"), lax.axis_index("c")
    slots = [land.at[4 * (x ^ (d >> 1)) + 2 * (y ^ (d & 1)) + c] for d in range(1, N_CHIP)]
    return [(s, s, (x, y, 1 - c)) for s in slots]


def _plan_gather_direct(land, _):
    return _plan_to_all(land.at[_my_slot()], land)


_PLAN_COPIES[_plan_gather_first] = N_CHIP
_PLAN_COPIES[_plan_gather_relay] = N_CHIP - 1
_PLAN_COPIES[_plan_gather_direct] = N_DEV - 1


def _exchange_start(plan, arrs, lands, *, after=None, name):
    bufs = list(lands) if arrs is None else list(arrs) + list(lands)
    n, nb = len(lands), len(bufs)
    nsem = sum(_PLAN_COPIES[p] for p in _plans(plan, n))
    dep_specs, deps = _dep_specs(after)

    def body(*refs):
        ins, land_refs = refs[:n], refs[nb - n:nb]
        send, recv = refs[nb + len(deps)], refs[nb + len(deps) + 1]
        for cp in _exchange_copies(plan, ins, land_refs, send, recv):
            cp.start()
        refs[-1][...] = jnp.zeros_like(refs[-1])

    out = pl.pallas_call(
        body, name=name,
        out_shape=(pltpu.SemaphoreType.DMA((nsem,)), pltpu.SemaphoreType.DMA((nsem,)),
                   *[pltpu.HBM(a.shape, a.dtype) for a in bufs], _sds((8, LANE), F32)),
        in_specs=[_HBM] * nb + dep_specs,
        out_specs=(_SEM, _SEM, *([_HBM] * nb), pl.BlockSpec(memory_space=pltpu.VMEM)),
        input_output_aliases={i: 2 + i for i in range(nb)},
        compiler_params=pltpu.CompilerParams(has_side_effects=_DATAFLOW),
    )(*[pltpu.with_memory_space_constraint(a, pltpu.HBM) for a in bufs], *deps)
    return (plan, n, out[0], out[1], list(out[2:2 + nb])), out[-1]


def _exchange_now(plan, lands, *, name):
    n = len(lands)
    nsem = sum(_PLAN_COPIES[p] for p in _plans(plan, n))

    def body(*refs):
        land_refs, send, recv = refs[n:2 * n], refs[2 * n], refs[2 * n + 1]
        copies = _exchange_copies(plan, land_refs, land_refs, send, recv)
        for cp in copies:
            cp.start()
        for cp in copies:
            cp.wait_send()
            cp.wait_recv()

    hbm = pl.BlockSpec(memory_space=pl.ANY)
    return pl.pallas_call(
        body, in_specs=[hbm] * n, out_specs=[hbm] * n, out_shape=[_sds(a.shape, a.dtype) for a in lands],
        input_output_aliases={i: i for i in range(n)},
        scratch_shapes=[pltpu.SemaphoreType.DMA((nsem,)), pltpu.SemaphoreType.DMA((nsem,))], name=name)(*lands)


def _exchange_wait(state, after, *, name):
    plan, n, send_sem, recv_sem, bufs = state
    nb = len(bufs)
    after = list(after) if isinstance(after, (list, tuple)) else [after]

    def body(*refs):
        ins, land_refs, send, recv = refs[:n], refs[nb - n:nb], refs[nb], refs[nb + 1]
        for cp in _exchange_copies(plan, ins, land_refs, send, recv):
            cp.wait_send()
            cp.wait_recv()

    out = pl.pallas_call(
        body, name=name, out_shape=[pltpu.HBM(a.shape, a.dtype) for a in bufs],
        in_specs=[_HBM] * nb + [_SEM, _SEM] + [pl.BlockSpec(memory_space=pl.ANY)] * len(after), out_specs=[_HBM] * nb,
        input_output_aliases={i: i for i in range(nb)},
        compiler_params=pltpu.CompilerParams(has_side_effects=_DATAFLOW),
    )(*bufs, send_sem, recv_sem, *after)
    return list(out[:n]), list(out[nb - n:])


def _dep_specs(after):
    return ([], []) if after is None else ([pl.BlockSpec(memory_space=pl.ANY)], [after])


def _undo_column_split(g):
    return jnp.transpose(g, (1, 0, 2)).reshape(g.shape[1], N_DEV * g.shape[2])


def _column_split(a):
    r, c = a.shape
    return jnp.transpose(a.reshape(r, N_DEV, c // N_DEV), (1, 0, 2))


class _WholeWeights:
    def __init__(self, groups):
        self.groups = groups
        self.grads = {}

    def fetch(self, group, after):
        return self.groups[group]

    def emit(self, group, grads):
        self.grads.update(grads)
        return None


def _local_step(x, target, replicated, src):
    d = x.shape[1]
    mix_g, ffn_g = replicated["mix_g"], replicated["ffn_g"]
    cp = src.fetch("cp", [])
    cp_mid = (cp["conv_w"], replicated["conv_b"], replicated["ln_g"], replicated["ln_b"], replicated["pool_w"],
              replicated["pool_scale"])

    h1, cat, z0, u0, cv0, h0 = _cp_mid_fwd(x, cp["meta"], mix_g[0:1], cp["cp_w_in_t"], cp["cp_w_out"], *cp_mid,
                                           name="cp_mixer")
    ffn0 = src.fetch("ffn0", h1)
    h2, uf0, rf0 = _ffn_fwd(h1, ffn_g[0:1], ffn0["w1"], ffn0["w2"], name="ffn0")
    gla = src.fetch("gla", h2)
    gla_mid = (gla["gate_w"], gla["gate_b"], gla["head_g"])
    h3, og, states, z1, u1, gla_w_in_t = _gla_mid_fwd(h2, mix_g[1:2], gla["gla_w_in_t"], gla["gla_w_out"], *gla_mid,
                                                      name="gla_mixer")
    ffn1 = src.fetch("ffn1", h3)
    dh4, uf1, rf1, loss, d_final_g = _ffn_fwd(h3, ffn_g[1:2], ffn1["w1"], ffn1["w2"],
                                              loss_head=(replicated["final_g"], target), name="ffn1_loss")

    dh3, dhh1, dob1, dffn_g1 = _ffn_bwd_x(h3, dh4, ffn_g[1:2], rf1, ffn1["w1"], ffn1["w2"], name="ffn1_bwd_x")
    sent = src.emit("ffn1", dict(w1=_linear_bwd_w(uf1, dhh1, column_blocks=N_DEV, name="ffn1_dw1"),
                                 w2=_linear_bwd_w(rf1, dob1, square_x=True, name="ffn1_dw2")))
    d_gla_w_out = _linear_bwd_w(og, dh3, name="gla_out_dw")
    dh2, dmix_g1, dz1, d_gate_w, d_gate_b, d_head_g = _gla_mid_bwd(
        z1, h2, mix_g[1:2], gla_w_in_t, dh3, gla["gla_w_out"], states, *gla_mid, after=sent, name="gla_mixer_bwd")
    d_gla_w_in_t = _linear_bwd_w(dz1, u1, row_blocks=gla["gla_w_in_t"].shape[:2], name="gla_in_dw")
    sent = src.emit("gla", dict(gla_w_in_t=d_gla_w_in_t, gla_w_out=d_gla_w_out))
    dh1, dhh0, dob0, dffn_g0 = _ffn_bwd_x(h1, dh2, ffn_g[0:1], rf0, ffn0["w1"], ffn0["w2"], after=sent, name="ffn0_bwd_x")
    sent = src.emit("ffn0_w1", dict(w1=_linear_bwd_w(uf0, dhh0, column_blocks=N_DEV, name="ffn0_dw1")))
    sent = src.emit("ffn0_w2", dict(w2=_linear_bwd_w(rf0, dob0, square_x=True, after=sent, name="ffn0_dw2")))
    d_cp_w_out = _linear_bwd_w(cat, dh1, after=sent, name="cp_out_dw")
    sent = src.emit("cp_out", dict(cp_w_out=d_cp_w_out))
    dx, dh0_first, dmix_g0, dz0, d_conv_w, d_conv_b, d_ln_g, d_ln_b, d_pool_w, d_pool_scale = _cp_mid_bwd(
        z0, cv0, h0, mix_g[0:1], cp["cp_w_in_t"], dh1, cp["cp_w_out"], *cp_mid, after=sent, name="cp_mixer_bwd")
    d_cp_w_in_t = _linear_bwd_w(dz0, u0, name="cp_in_dw")

    small = dict(
        mix_g=jnp.concatenate([dmix_g0, dmix_g1]), ffn_g=jnp.concatenate([dffn_g0, dffn_g1]), conv_b=d_conv_b, ln_g=d_ln_g,
        ln_b=d_ln_b, pool_w=d_pool_w, pool_scale=d_pool_scale, final_g=d_final_g, meta=dh0_first[PAD_ROWS:], conv_w=d_conv_w,
        gate_w=d_gate_w, gate_b=d_gate_b, head_g=d_head_g)
    src.emit("cp", dict(cp_w_in_t=d_cp_w_in_t, small=small, loss=loss))
    return loss, dx, small


_REPLICATED = ("mix_norm_g", "ffn_norm_g", "cp_conv_b", "cp_ln_g", "cp_ln_b", "cp_pool_w", "cp_pool_scale", "final_norm_g")
_SMALL_SHARDED = ("meta_tokens", "cp_conv_w", "gla_gate_w2", "gla_gate_b", "gla_head_g")
_NAMES = ("meta_tokens", "mix_norm_g", "ffn_norm_g", "ffn_w1", "ffn_w2", "cp_w_in", "cp_conv_w", "cp_conv_b", "cp_ln_g",
          "cp_ln_b", "cp_pool_w", "cp_pool_scale", "cp_w_out", "gla_w_in", "gla_gate_w2", "gla_gate_b", "gla_head_g",
          "gla_w_out", "final_norm_g")
_SMALL_GRADS = ("mix_g", "ffn_g", "conv_b", "ln_g", "ln_b", "pool_w", "pool_scale", "final_g", "meta", "conv_w", "gate_w",
                "gate_b", "head_g")
_GROUPS = ("cp", "ffn0", "gla", "ffn1")
_TWO_LEG_GATHERS = ("cp", "ffn0", "ffn1")


class _Exchanges:
    def __init__(self, w, d):
        self.d = d
        small = [w[n].reshape(w[n].shape[-2:]) for n in _SMALL_SHARDED]
        self.small_shard_shapes = [w[n].shape for n in _SMALL_SHARDED]
        shards = dict(
            cp=[(w["cp_w_in"][0].T, BF16), (w["cp_w_out"][0], BF16)] + [(a, F32) for a in small],
            ffn0=[((w["ffn_w1"], 0), BF16), ((w["ffn_w2"], 0), BF16)],
            gla=[(w["gla_w_in"][0].T, BF16), (w["gla_w_out"][0], BF16)],
            ffn1=[((w["ffn_w1"], 1), BF16), ((w["ffn_w2"], 1), BF16)])
        self.gathers = {}
        self.sent = {}
        token = None
        for group in _GROUPS:
            lands = _place_own(shards[group], after=token, name=f"place_w_{group}")
            plan = _plan_gather_first if group in _TWO_LEG_GATHERS else _plan_gather_direct
            self.gathers[group], token = _exchange_start(plan, None, lands, after=token, name=f"start_w_{group}")
        self.token = token

    def fetch(self, group, after):
        d = self.d
        after = (list(after) if isinstance(after, (list, tuple)) else [after]) + [self.token]
        if group == "ffn0":
            after += [self.gate_w, self.gate_b, self.head_g]
        _, got = _exchange_wait(self.gathers[group], after, name=f"wait_w_{group}")
        if group in _TWO_LEG_GATHERS:
            got = _exchange_now(_plan_gather_relay, got, name=f"relay_w_{group}")
        if group in ("ffn0", "ffn1"):
            return dict(w1=got[0], w2=got[1])
        if group == "gla":
            return dict(gla_w_in_t=got[0], gla_w_out=got[1].reshape(d, d), gate_w=self.gate_w, gate_b=self.gate_b,
                        head_g=self.head_g)
        meta, conv_w, gate_w, self.gate_b, self.head_g = [_undo_column_split(a) for a in got[2:]]
        self.gate_w = jnp.pad(gate_w, ((0, GATE_PAD - GATE_RANK), (0, 0))).astype(BF16)
        return dict(cp_w_in_t=got[0].reshape(-1, d), cp_w_out=got[1].reshape(d, d), meta=meta,
                    conv_w=jnp.pad(conv_w, ((0, 1), (0, 0))))

    def emit(self, group, g):
        d = self.d
        if group == "ffn1":
            arrs = [g["w1"], g["w2"].reshape(N_DEV, -1, d)]
        elif group == "ffn0_w1":
            arrs = [g["w1"]]
        elif group == "ffn0_w2":
            arrs = [g["w2"].reshape(N_DEV, -1, d)]
        elif group == "gla":
            arrs = [g["gla_w_in_t"], g["gla_w_out"].reshape(N_DEV, d // N_DEV, d)]
        elif group == "cp_out":
            arrs = [g["cp_w_out"].reshape(N_DEV, d // N_DEV, d)]
        else:
            s = dict(g["small"])
            s.update(pool_w=s["pool_w"][None], conv_w=s["conv_w"][:CONV_WIDTH], gate_w=s["gate_w"][:GATE_RANK])
            own = [s[n] for n in _SMALL_GRADS[:len(_REPLICATED)]]
            own += [_column_split(s[n]).reshape((N_DEV,) + shape)
                    for n, shape in zip(_SMALL_GRADS[len(_REPLICATED):], self.small_shard_shapes)]
            plans = [_plan_to_all] * len(_REPLICATED) + [_plan_split_to_all] * len(_SMALL_SHARDED)
            lands = [lax.empty((N_DEV,) + a.shape, F32) for a in own[:len(_REPLICATED)]]
            lands += [lax.empty(a.shape, F32) for a in own[len(_REPLICATED):]]
            own.append(g["loss"])
            plans.append(_plan_to_all)
            lands.append(lax.empty((N_DEV,) + g["loss"].shape, F32))
            own.append(g["cp_w_in_t"].reshape(N_DEV, -1, d))
            plans.append(_plan_split_to_all)
            lands.append(lax.empty(own[-1].shape, BF16))
            self.sent[group], self.token = _exchange_start(plans, own, lands, after=self.token, name=f"start_g_{group}")
            return self.token
        self.sent[group], self.token = _exchange_start(_plan_split_to_all, arrs, [lax.empty(a.shape, a.dtype) for a in arrs],
                                                       after=self.token, name=f"start_g_{group}")
        return self.token

    def finish(self, w, mom, var):
        out = {}
        after = self.token

        def landed(group):
            own, got = _exchange_wait(self.sent[group], after, name=f"wait_g_{group}")
            return list(zip(own, got))

        def adam(n, parts, behind=None, transposed=False):
            by_row = transposed and w[n].shape[2] % 8 != 0
            if by_row:
                flip, back = (lambda a: jnp.transpose(a, (2, 0, 1))), (lambda a: jnp.transpose(a, (1, 2, 0)))
            else:
                flip = back = (lambda a: jnp.transpose(a, (0, 2, 1))) if transposed else (lambda a: a)
            res = _reduce_adam(parts, flip(w[n]), flip(mom[n]), flip(var[n]), by_row=by_row, after=behind, name=f"adam_{n}")
            out[n] = tuple(back(a) for a in res)
            return res[0]

        ffn1 = landed("ffn1")
        after = ffn1[0][1]
        gla = landed("gla")
        after = adam("gla_w_in", [gla[0]], transposed=True)
        after = adam("gla_w_out", [gla[1]], after)
        ffn0_w1 = landed("ffn0_w1")
        after = adam("ffn_w1", [ffn0_w1[0], ffn1[0]])
        ffn0_w2 = landed("ffn0_w2")
        after = adam("ffn_w2", [ffn0_w2[0], ffn1[1]])
        cp_out = landed("cp_out")
        after = adam("cp_w_out", [cp_out[0]])
        *small, loss, cp_w_in = landed("cp")
        names = _REPLICATED + _SMALL_SHARDED
        split = [False] * len(_REPLICATED) + [True] * len(_SMALL_SHARDED)
        small_new = _adam_small([own for own, _ in small], [got for _, got in small], split, [w[n] for n in names],
                                [mom[n] for n in names], [var[n] for n in names], name="adam_small")
        out.update(zip(names, small_new))
        out["loss"] = _sum8(*loss, name="sum_loss")[0, 0]
        adam("cp_w_in", [cp_w_in], small_new[0][0], transposed=True)
        return out


def kernel(x, meta_tokens, mix_norm_g, ffn_norm_g, ffn_w1, ffn_w2, cp_w_in, cp_conv_w, cp_conv_b, cp_ln_g, cp_ln_b, cp_pool_w, cp_pool_scale, cp_w_out, gla_w_in, gla_gate_w2, gla_gate_b, gla_head_g, gla_w_out, final_norm_g, loss_target, m_meta_tokens, m_mix_norm_g, m_ffn_norm_g, m_ffn_w1, m_ffn_w2, m_cp_w_in, m_cp_conv_w, m_cp_conv_b, m_cp_ln_g, m_cp_ln_b, m_cp_pool_w, m_cp_pool_scale, m_cp_w_out, m_gla_w_in, m_gla_gate_w2, m_gla_gate_b, m_gla_head_g, m_gla_w_out, m_final_norm_g, v_meta_tokens, v_mix_norm_g, v_ffn_norm_g, v_ffn_w1, v_ffn_w2, v_cp_w_in, v_cp_conv_w, v_cp_conv_b, v_cp_ln_g, v_cp_ln_b, v_cp_pool_w, v_cp_pool_scale, v_cp_w_out, v_gla_w_in, v_gla_gate_w2, v_gla_gate_b, v_gla_head_g, v_gla_w_out, v_final_norm_g):
    w = dict(meta_tokens=meta_tokens, mix_norm_g=mix_norm_g, ffn_norm_g=ffn_norm_g, ffn_w1=ffn_w1, ffn_w2=ffn_w2,
             cp_w_in=cp_w_in, cp_conv_w=cp_conv_w, cp_conv_b=cp_conv_b, cp_ln_g=cp_ln_g, cp_ln_b=cp_ln_b,
             cp_pool_w=cp_pool_w, cp_pool_scale=cp_pool_scale, cp_w_out=cp_w_out, gla_w_in=gla_w_in,
             gla_gate_w2=gla_gate_w2, gla_gate_b=gla_gate_b, gla_head_g=gla_head_g, gla_w_out=gla_w_out,
             final_norm_g=final_norm_g.reshape(1, -1))
    mom = dict(meta_tokens=m_meta_tokens, mix_norm_g=m_mix_norm_g, ffn_norm_g=m_ffn_norm_g, ffn_w1=m_ffn_w1, ffn_w2=m_ffn_w2,
               cp_w_in=m_cp_w_in, cp_conv_w=m_cp_conv_w, cp_conv_b=m_cp_conv_b, cp_ln_g=m_cp_ln_g, cp_ln_b=m_cp_ln_b,
               cp_pool_w=m_cp_pool_w, cp_pool_scale=m_cp_pool_scale, cp_w_out=m_cp_w_out, gla_w_in=m_gla_w_in,
               gla_gate_w2=m_gla_gate_w2, gla_gate_b=m_gla_gate_b, gla_head_g=m_gla_head_g, gla_w_out=m_gla_w_out,
               final_norm_g=m_final_norm_g.reshape(1, -1))
    var = dict(meta_tokens=v_meta_tokens, mix_norm_g=v_mix_norm_g, ffn_norm_g=v_ffn_norm_g, ffn_w1=v_ffn_w1, ffn_w2=v_ffn_w2,
               cp_w_in=v_cp_w_in, cp_conv_w=v_cp_conv_w, cp_conv_b=v_cp_conv_b, cp_ln_g=v_cp_ln_g, cp_ln_b=v_cp_ln_b,
               cp_pool_w=v_cp_pool_w, cp_pool_scale=v_cp_pool_scale, cp_w_out=v_cp_w_out, gla_w_in=v_gla_w_in,
               gla_gate_w2=v_gla_gate_w2, gla_gate_b=v_gla_gate_b, gla_head_g=v_gla_head_g, gla_w_out=v_gla_w_out,
               final_norm_g=v_final_norm_g.reshape(1, -1))
    d = x.shape[-1]
    replicated = dict(mix_g=w["mix_norm_g"], ffn_g=w["ffn_norm_g"], conv_b=w["cp_conv_b"], ln_g=w["cp_ln_g"],
                      ln_b=w["cp_ln_b"], pool_w=w["cp_pool_w"][0].astype(BF16), pool_scale=w["cp_pool_scale"],
                      final_g=w["final_norm_g"])
    exchanges = _Exchanges(w, d)
    _, grad_x, _ = _local_step(x[0], loss_target[0], replicated, exchanges)
    out = exchanges.finish(w, mom, var)
    loss = out.pop("loss")

    def leaf(n, k):
        a = out[n][k]
        return a.reshape(-1) if n == "final_norm_g" else a

    return (loss, grad_x[None], *[leaf(n, 0) for n in _NAMES], *[leaf(n, 1) for n in _NAMES],
            *[leaf(n, 2) for n in _NAMES], *[leaf(n, 3) for n in _NAMES])
```

```python
import functools

import jax
import jax.numpy as jnp
from jax import lax
from jax.experimental import pallas as pl
from jax.experimental.pallas import tpu as pltpu

F32, BF16 = jnp.float32, jnp.bfloat16
N_DEV = 8
CHUNK = 64
N_META = 16
PAD_ROWS = CHUNK - N_META
HALO = 32
EPS = 1e-5
CONV_WIDTH = 31
POOL_WINDOWS = (2, 4, 8, 16)
HEADS = 4
GATE_RANK = 16
GATE_NORM = 16.0
GATE_PAD = 128
ADAM_LR, ADAM_B1, ADAM_B2, ADAM_EPS, ADAM_WD, ADAM_STEP = 0.001, 0.9, 0.999, 1e-08, 0.01, 10
V7X_VMEM_LIMIT = 56 * 2 ** 20
LANE = 128


def _cparams(*sem):
    return pltpu.CompilerParams(dimension_semantics=sem, vmem_limit_bytes=V7X_VMEM_LIMIT)


def _row_tile(t, cap):
    best = CHUNK
    for r in range(CHUNK, min(t, cap) + 1, CHUNK):
        if t % r == 0:
            best = r
    return best


def _resident(shape):
    return pl.BlockSpec(shape, lambda *_: (0,) * len(shape), pipeline_mode=pl.Buffered(1))


def _dot(a, b):
    return jnp.dot(a, b, preferred_element_type=F32)


def _dot_nt(a, b):
    return lax.dot_general(a, b, (((1,), (1,)), ((), ())), preferred_element_type=F32)


def _dot_tn(a, b):
    return lax.dot_general(a, b, (((0,), (0,)), ((), ())), preferred_element_type=F32)


def _rowsum(a):
    return jnp.sum(a, axis=0, keepdims=True)


def _sigmoid(a):
    return 1.0 / (1.0 + jnp.exp(-a))


def _row_ids(tile, rt):
    return tile * rt + lax.broadcasted_iota(jnp.int32, (rt, 1), 0)


def _sds(shape, dtype):
    return jax.ShapeDtypeStruct(shape, dtype)


DW_ROWS = 1024


def _linear_bwd_w(x, dy, *, square_x=False, column_blocks=None, row_blocks=None, after=None, name):
    t, k = x.shape
    n = dy.shape[1]
    cut_k = k > n and column_blocks is None
    assert cut_k or row_blocks is None
    width = k if cut_k else n
    blk = n // column_blocks if column_blocks else max(c for c in (640, 512, 384, 256, LANE) if width % c == 0)
    dep_specs, deps = _dep_specs(after)

    def body(*refs):
        x_ref, dy_ref, o_ref, acc = refs[len(deps):]
        for c0 in range(0, t, DW_ROWS):
            rows = slice(c0, min(c0 + DW_ROWS, t))
            xv = x_ref[rows, :]
            if square_x:
                xv = xv.astype(F32)
                xv = xv * xv
            part = _dot_tn(xv.astype(BF16), dy_ref[rows, :].astype(BF16))
            if c0 == 0:
                acc[...] = part
            else:
                acc[...] += part
        if row_blocks is None:
            o_ref[...] = acc[...].astype(BF16)
            return
        nb, rpb = row_blocks
        for s in range(width // blk):
            @pl.when(pl.program_id(0) == s)
            def _(s=s):
                for b in range(nb):
                    lo, hi = max(s * blk, b * rpb), min((s + 1) * blk, (b + 1) * rpb)
                    if lo < hi:
                        o_ref[b, lo - b * rpb:hi - b * rpb, :] = acc[lo - s * blk:hi - s * blk, :].astype(BF16)

    out_shape = _sds((k, n), BF16)
    semantics = "parallel"
    if cut_k:
        in_specs = [pl.BlockSpec((t, blk), lambda j: (0, j)), _resident((t, n))]
        out_specs = pl.BlockSpec((blk, n), lambda j: (j, 0))
        acc_shape = (blk, n)
        if row_blocks:
            out_shape = _sds(row_blocks + (n,), BF16)
            out_specs = pl.BlockSpec(out_shape.shape, lambda j: (0, 0, 0))
            semantics = "arbitrary"
    else:
        in_specs = [_resident((t, k)), pl.BlockSpec((t, blk), lambda j: (0, j))]
        out_specs = pl.BlockSpec((k, blk), lambda j: (0, j))
        acc_shape = (k, blk)
        if column_blocks:
            out_specs = pl.BlockSpec((None, k, blk), lambda j: (j, 0, 0))
            out_shape = _sds((column_blocks, k, blk), BF16)
    return pl.pallas_call(
        body, grid=(width // blk,), in_specs=dep_specs + in_specs, out_specs=out_specs, out_shape=out_shape,
        scratch_shapes=[pltpu.VMEM(acc_shape, F32)], compiler_params=_cparams(semantics), name=name)(*deps, x, dy)


FFN_BLOCKS_PER_STEP = 2


def _ffn_fwd(h, gain, w1g, w2g, *, loss_head=None, name):
    t, d = h.shape
    f8 = w1g.shape[-1]
    rt = _row_tile(t, 832)
    nb = FFN_BLOCKS_PER_STEP
    nstep = N_DEV // nb

    def body(*refs):
        if loss_head is None:
            h_ref, g_ref, w1_ref, w2_ref, o_ref, u_ref, r_ref, acc_ref = refs
        else:
            (h_ref, g_ref, w1_ref, w2_ref, fg_ref, tgt_ref, o_ref, u_ref, r_ref, loss_ref, dfg_ref, acc_ref, t_ref,
             t_sem) = refs
        i, j = pl.program_id(0), pl.program_id(1)

        def target_rows(act):
            @pl.when(i == 0)
            def _():
                act(pltpu.make_async_copy(tgt_ref.at[pl.ds(0, rt - CHUNK)], t_ref.at[pl.ds(CHUNK, rt - CHUNK)], t_sem.at[0]))

            if t > rt:
                @pl.when(i > 0)
                def _():
                    act(pltpu.make_async_copy(tgt_ref.at[pl.ds(pl.multiple_of(i * rt - CHUNK, CHUNK), rt)], t_ref,
                                              t_sem.at[0]))

        @pl.when(j == 0)
        def _():
            if loss_head is not None:
                @pl.when(i == 0)
                def _():
                    t_ref[0:CHUNK, :] = jnp.zeros((CHUNK, d), F32)

                target_rows(lambda copy: copy.start())

            hv = h_ref[...]
            u_ref[...] = (hv * lax.rsqrt(jnp.mean(hv * hv, axis=-1, keepdims=True) + EPS) * g_ref[...]).astype(BF16)
            acc_ref[...] = jnp.zeros_like(acc_ref)

        part = None
        for b in range(nb):
            a = jnp.maximum(_dot(u_ref[...], w1_ref[b]), 0.0)
            r_ref[:, b * f8:(b + 1) * f8] = a.astype(BF16)
            term = _dot((a * a).astype(BF16), w2_ref[b])
            part = term if part is None else part + term
        acc_ref[...] += part

        @pl.when(j == nstep - 1)
        def _():
            y = h_ref[...] + acc_ref[...]
            if loss_head is None:
                o_ref[...] = y
                return

            @pl.when(i == 0)
            def _():
                loss_ref[...] = jnp.zeros_like(loss_ref)
                dfg_ref[...] = jnp.zeros_like(dfg_ref)

            target_rows(lambda copy: copy.wait())

            rstd = lax.rsqrt(jnp.mean(y * y, axis=-1, keepdims=True) + EPS)
            xh = y * rstd
            err = jnp.where(_row_ids(i, rt) >= CHUNK, xh * fg_ref[...] - t_ref[...], 0.0)
            loss_ref[...] += (0.5 / d) * jnp.sum(err * err)
            dy = err * (1.0 / d)
            dfg_ref[...] += _rowsum(dy * xh)
            dxh = dy * fg_ref[...]
            o_ref[...] = rstd * (dxh - xh * jnp.mean(dxh * xh, axis=-1, keepdims=True))

    rows = lambda i, j: (i, 0)
    in_specs = [pl.BlockSpec((rt, d), rows), _resident((1, d)),
                pl.BlockSpec((nb, d, f8), lambda i, j: (j, 0, 0)), pl.BlockSpec((nb, f8, d), lambda i, j: (j, 0, 0))]
    out_specs = [pl.BlockSpec((rt, d), rows), pl.BlockSpec((rt, d), rows), pl.BlockSpec((rt, nb * f8), lambda i, j: (i, j))]
    out_shape = [_sds((t, d), F32), _sds((t, d), BF16), _sds((t, N_DEV * f8), BF16)]
    args = [h, gain, w1g, w2g]
    scratch_shapes = [pltpu.VMEM((rt, d), F32)]
    if loss_head is not None:
        in_specs += [_resident((1, d)), pl.BlockSpec(memory_space=pl.ANY)]
        out_specs += [pl.BlockSpec((8, LANE), lambda i, j: (0, 0)), pl.BlockSpec((1, d), lambda i, j: (0, 0))]
        out_shape += [_sds((8, LANE), F32), _sds((1, d), F32)]
        args += list(loss_head)
        scratch_shapes += [pltpu.VMEM((rt, d), F32), pltpu.SemaphoreType.DMA((1,))]
    return pl.pallas_call(
        body, grid=(t // rt, nstep), in_specs=in_specs, out_specs=out_specs, out_shape=out_shape,
        scratch_shapes=scratch_shapes,
        compiler_params=_cparams("arbitrary" if loss_head is not None else "parallel", "arbitrary"), name=name)(*args)


def _ffn_bwd_x(h, dout, gain, r, w1g, w2g, *, after=None, name):
    t, d = h.shape
    f8 = w1g.shape[-1]
    rt = _row_tile(t, 832)
    nb = FFN_BLOCKS_PER_STEP
    last = N_DEV // nb - 1
    dep_specs, deps = _dep_specs(after)

    def body(*refs):
        h_ref, do_ref, g_ref, r_ref, w1_ref, w2_ref, dh_ref, dhh_ref, dob_ref, dg_ref, dhb_ref, du_ref = refs[len(deps):]
        i, j = pl.program_id(0), pl.program_id(1)

        @pl.when(j == 0)
        def _():
            dob_ref[...] = do_ref[...].astype(BF16)
            du_ref[...] = jnp.zeros_like(du_ref)

        part = None
        for b in range(nb):
            cols = slice(b * f8, (b + 1) * f8)
            dhh = (_dot_nt(dob_ref[...], w2_ref[b]) * (2.0 * r_ref[:, cols].astype(F32))).astype(BF16)
            dhh_ref[:, cols] = dhh
            term = _dot_nt(dhh, w1_ref[b])
            part = term if part is None else part + term
        du_ref[...] += part

        @pl.when(j == last)
        def _():
            @pl.when(i == 0)
            def _():
                dg_ref[...] = jnp.zeros_like(dg_ref)

            hv = h_ref[...]
            rstd = lax.rsqrt(jnp.mean(hv * hv, axis=-1, keepdims=True) + EPS)
            xh = hv * rstd
            du = du_ref[...]
            dg_ref[...] += _rowsum(du * xh)
            dxh = du * g_ref[...]
            dh = do_ref[...] + rstd * (dxh - xh * jnp.mean(dxh * xh, axis=-1, keepdims=True))
            dh_ref[...] = dh
            dhb_ref[...] = dh.astype(BF16)

    rows = lambda i, j: (i, 0)
    return pl.pallas_call(
        body, grid=(t // rt, N_DEV // nb),
        in_specs=dep_specs + [
                  pl.BlockSpec((rt, d), rows), pl.BlockSpec((rt, d), rows), _resident((1, d)),
                  pl.BlockSpec((rt, nb * f8), lambda i, j: (i, j)),
                  pl.BlockSpec((nb, d, f8), lambda i, j: (j, 0, 0)),
                  pl.BlockSpec((nb, f8, d), lambda i, j: (j, 0, 0))],
        out_specs=[pl.BlockSpec((rt, d), rows), pl.BlockSpec((rt, nb * f8), lambda i, j: (i, j)),
                   pl.BlockSpec((rt, d), rows), pl.BlockSpec((1, d), lambda i, j: (0, 0)), pl.BlockSpec((rt, d), rows)],
        out_shape=[_sds((t, d), F32), _sds((t, N_DEV * f8), BF16), _sds((t, d), BF16), _sds((1, d), F32),
                   _sds((t, d), BF16)],
        scratch_shapes=[pltpu.VMEM((rt, d), F32)],
        compiler_params=_cparams("arbitrary", "arbitrary"), name=name)(*deps, h, dout, gain, r, w1g, w2g)


def _lane_blocks(width):
    lb = min(LANE, width)
    return [slice(s, s + lb) for s in range(0, width, lb)]


def _conv_rows(src_ref, w_ref, offset, dst_ref, nblk, width, bias_ref=None):
    def blk(rb, carry):
        base = pl.multiple_of(rb * CHUNK, CHUNK)
        for l, ls in enumerate(_lane_blocks(width)):
            acc = jnp.zeros((CHUNK, ls.stop - ls.start), F32)
            if bias_ref is not None:
                acc = acc + bias_ref[:, ls]
            for k in range(CONV_WIDTH):
                acc = acc + w_ref[k:k + 1, ls] * src_ref[l, pl.ds(base + offset(k), CHUNK), :]
            dst_ref[l, pl.ds(base, CHUNK), :] = acc
        return carry

    lax.fori_loop(0, nblk, blk, 0)


def _to_lane_blocks(ref, row0, value):
    for l, ls in enumerate(_lane_blocks(value.shape[1])):
        ref[l, row0:row0 + value.shape[0], :] = value[:, ls]


def _from_lane_blocks(ref):
    return jnp.concatenate([ref[l] for l in range(ref.shape[0])], axis=1)


def _pool_counts(rows, window):
    return jnp.clip(rows - PAD_ROWS + 1, 1, window).astype(F32)


def _trailing_sum(v, window):
    s, sh = v, 1
    while sh < window:
        s = s + pltpu.roll(s, sh, 0)
        sh *= 2
    return s


def _leading_sum(v, window):
    s, sh, n = v, 1, v.shape[0]
    while sh < window:
        s = s + pltpu.roll(s, n - sh, 0)
        sh *= 2
    return s


def _norm_project(h_ref, g_ref, w_t_ref, u_ref, z_ref):
    hv = h_ref[...]
    u = (hv * lax.rsqrt(jnp.mean(hv * hv, axis=-1, keepdims=True) + EPS) * g_ref[...]).astype(BF16)
    u_ref[...] = u
    z_ref[...] = _dot_nt(u, w_t_ref[...])


def _project_back(dz_ref, w_t_ref, h_ref, g_ref, dres_ref, dh_ref, dg_ref, first):
    dx = _dot(dz_ref[...], w_t_ref[...])
    hv = h_ref[...]
    rstd = lax.rsqrt(jnp.mean(hv * hv, axis=-1, keepdims=True) + EPS)
    xh = hv * rstd

    @pl.when(first)
    def _():
        dg_ref[...] = jnp.zeros_like(dg_ref)

    dg_ref[...] += _rowsum(dx * xh)
    dxh = dx * g_ref[...]
    dh_ref[...] = dres_ref[...] + rstd * (dxh - xh * jnp.mean(dxh * xh, axis=-1, keepdims=True))


def _cp_mid_fwd(x, meta, gain, w_in_t, w_out, conv_w, conv_b, ln_g, ln_b, pool_w, pool_scale, *, name):
    seq, d = x.shape
    t = seq + CHUNK
    ein = w_in_t.shape[0]
    cd = conv_b.shape[1]
    pd = pool_scale.shape[1]
    pg = pd // len(POOL_WINDOWS)
    rt = _row_tile(t, 320)
    ntile = t // rt

    def body(x_ref, meta_ref, g_ref, wi_ref, wo_ref, cw_ref, cb_ref, lg_ref, lb_ref, pw_ref, ps_ref,
             ho_ref, o_ref, z_ref, u_ref, cv_ref, h0_ref, gext, pext, conv_s, hbuf, hsem):
        i = pl.program_id(0)
        slot = i % 2
        first_rows = pltpu.make_async_copy(x_ref.at[pl.ds(0, rt - CHUNK)], hbuf.at[0, pl.ds(CHUNK, rt - CHUNK)], hsem.at[0])

        def tile_rows(tile, to):
            return pltpu.make_async_copy(x_ref.at[pl.ds(pl.multiple_of(tile * rt - CHUNK, CHUNK), rt)], hbuf.at[to],
                                         hsem.at[to])

        @pl.when(i == 0)
        def _():
            first_rows.start()
            hbuf[0, 0:PAD_ROWS, :] = jnp.zeros((PAD_ROWS, d), F32)
            hbuf[0, PAD_ROWS:CHUNK, :] = meta_ref[...]
            _to_lane_blocks(gext, 0, jnp.zeros((HALO, cd), F32))
            pext[0:HALO, :] = jnp.zeros((HALO, pd), F32)

        @pl.when(i + 1 < ntile)
        def _():
            tile_rows(i + 1, 1 - slot).start()

        @pl.when(i == 0)
        def _():
            first_rows.wait()

        @pl.when(i > 0)
        def _():
            tile_rows(i, slot).wait()

        h_ref = hbuf.at[slot]
        h0_ref[...] = h_ref[...]
        _norm_project(h_ref, g_ref, wi_ref, u_ref, z_ref)

        _to_lane_blocks(gext, HALO, z_ref[:, 0:cd] * _sigmoid(z_ref[:, cd:2 * cd]))
        pext[HALO:HALO + rt, :] = z_ref[:, 2 * cd:]
        _conv_rows(gext, cw_ref, lambda k: k + HALO - (CONV_WIDTH - 1), conv_s, rt // CHUNK, cd, cb_ref)
        cv = _from_lane_blocks(conv_s)
        cv_ref[...] = cv
        xc = cv - jnp.mean(cv, axis=-1, keepdims=True)
        y = xc * lax.rsqrt(jnp.mean(xc * xc, axis=-1, keepdims=True) + EPS) * lg_ref[...] + lb_ref[...]
        rows = _row_ids(i, rt)
        a = jnp.where(rows >= PAD_ROWS, y * _sigmoid(y), 0.0)
        o_ref[:, 0:cd] = a.astype(BF16)
        for gi, window in enumerate(POOL_WINDOWS):
            ls = slice(gi * pg, (gi + 1) * pg)
            v = pext[:, ls]
            tm = _trailing_sum(v, window)[HALO:] / _pool_counts(rows, window) - v[HALO:]
            p = _dot(tm.astype(BF16), pw_ref[gi]) * ps_ref[:, ls]
            o_ref[:, cd + gi * pg:cd + (gi + 1) * pg] = p.astype(BF16)
        ho_ref[...] = h_ref[...] + _dot(o_ref[...], wo_ref[...])
        gext[:, 0:HALO, :] = gext[:, rt:rt + HALO, :]
        pext[0:HALO, :] = pext[rt:rt + HALO, :]

    nl, lb = len(_lane_blocks(cd)), min(LANE, cd)
    rows = lambda i: (i, 0)
    return pl.pallas_call(
        body, grid=(ntile,),
        in_specs=[pl.BlockSpec(memory_space=pl.ANY), _resident(meta.shape), _resident((1, d)), _resident(w_in_t.shape),
                  _resident(w_out.shape), _resident(conv_w.shape), _resident((1, cd)),
                  _resident((1, cd)), _resident((1, cd)), _resident(pool_w.shape), _resident((1, pd))],
        out_specs=[pl.BlockSpec((rt, d), rows), pl.BlockSpec((rt, cd + pd), rows), pl.BlockSpec((rt, ein), rows),
                   pl.BlockSpec((rt, d), rows), pl.BlockSpec((rt, cd), rows), pl.BlockSpec((rt, d), rows)],
        out_shape=[_sds((t, d), F32), _sds((t, cd + pd), BF16), _sds((t, ein), F32), _sds((t, d), BF16),
                   _sds((t, cd), F32), _sds((t, d), F32)],
        scratch_shapes=[pltpu.VMEM((nl, rt + HALO, lb), F32), pltpu.VMEM((rt + HALO, pd), F32),
                        pltpu.VMEM((nl, rt, lb), F32), pltpu.VMEM((2, rt, d), F32), pltpu.SemaphoreType.DMA((2,))],
        compiler_params=_cparams("arbitrary"), name=name)(x, meta, gain, w_in_t, w_out, conv_w, conv_b, ln_g, ln_b, pool_w,
                                                          pool_scale)


def _cp_mid_bwd(z, cv, h, gain, w_in_t, dh, w_out, conv_w, conv_b, ln_g, ln_b, pool_w, pool_scale, *, after=None, name):
    t, ein = z.shape
    cd = conv_b.shape[1]
    pd = pool_scale.shape[1]
    pg = pd // len(POOL_WINDOWS)
    rt = _row_tile(t, 320)
    ntile = t // rt
    per = rt // CHUNK
    dep_specs, deps = _dep_specs(after)

    def body(*refs):
        (z_ref, zh_ref, cv_ref, h_ref, g_ref, wi_ref, dh_ref, wo_ref, cw_ref, cb_ref, lg_ref, lb_ref, pw_ref, ps_ref,
         dx_ref, dfirst_ref, dg_ref, dz_ref, dcw_ref, dcb_ref, dlg_ref, dlb_ref, dpw_ref, dps_ref,
         gext, pext, conv_s, dcv, dsp, dhi, dx_sem) = refs[len(deps):]
        step = pl.program_id(0)
        tile = ntile - 1 - step
        slot = step % 2
        last_slot = (ntile - 1) % 2
        first_rows = pltpu.make_async_copy(dhi.at[last_slot, pl.ds(CHUNK, rt - CHUNK)], dx_ref.at[pl.ds(0, rt - CHUNK)],
                                           dx_sem.at[last_slot])

        def tile_rows(tile, slot):
            return pltpu.make_async_copy(dhi.at[slot], dx_ref.at[pl.ds(pl.multiple_of(tile * rt - CHUNK, CHUNK), rt)],
                                         dx_sem.at[slot])

        dcat = _dot_nt(dh_ref[...].astype(BF16), wo_ref[...])

        @pl.when(step >= 2)
        def _():
            tile_rows(tile + 2, slot).wait()

        @pl.when(step == 0)
        def _():
            for ref in (dcw_ref, dcb_ref, dlg_ref, dlb_ref, dpw_ref, dps_ref):
                ref[...] = jnp.zeros_like(ref)
            _to_lane_blocks(dcv, rt, jnp.zeros((HALO, cd), F32))
            dsp[rt:rt + HALO, :] = jnp.zeros((HALO, pd), F32)

        keep = jnp.where(tile > 0, 1.0, 0.0)
        zh = zh_ref[CHUNK - HALO:CHUNK, :]
        _to_lane_blocks(gext, 0, keep * zh[:, 0:cd] * _sigmoid(zh[:, cd:2 * cd]))
        pext[0:HALO, :] = keep * zh[:, 2 * cd:]
        za = z_ref[:, 0:cd]
        sg = _sigmoid(z_ref[:, cd:2 * cd])
        _to_lane_blocks(gext, HALO, za * sg)
        pext[HALO:HALO + rt, :] = z_ref[:, 2 * cd:]
        cv = cv_ref[...]
        xc = cv - jnp.mean(cv, axis=-1, keepdims=True)
        rstd = lax.rsqrt(jnp.mean(xc * xc, axis=-1, keepdims=True) + EPS)
        xh = xc * rstd
        y = xh * lg_ref[...] + lb_ref[...]
        sy = _sigmoid(y)
        rows = _row_ids(tile, rt)
        da = jnp.where(rows >= PAD_ROWS, dcat[:, 0:cd], 0.0)
        dy = da * (sy * (1.0 + y * (1.0 - sy)))
        dlg_ref[...] += _rowsum(dy * xh)
        dlb_ref[...] += _rowsum(dy)
        dxh = dy * lg_ref[...]
        dconv = rstd * (dxh - jnp.mean(dxh, axis=-1, keepdims=True) - xh * jnp.mean(dxh * xh, axis=-1, keepdims=True))
        dcb_ref[...] += _rowsum(dconv)
        _to_lane_blocks(dcv, 0, dconv)
        for l, ls in enumerate(_lane_blocks(cd)):
            def acc_rows(rb, accs, l=l):
                base = pl.multiple_of(rb * CHUNK, CHUNK)
                d_blk = dcv[l, pl.ds(base, CHUNK), :]
                out = []
                for k in range(CONV_WIDTH):
                    prod = d_blk * gext[l, pl.ds(base + k + HALO - (CONV_WIDTH - 1), CHUNK), :]
                    part = prod[0:8]
                    for s in range(8, CHUNK, 8):
                        part = part + prod[s:s + 8]
                    out.append(accs[k] + part)
                return tuple(out)

            zero = jnp.zeros((8, ls.stop - ls.start), F32)
            accs = lax.fori_loop(0, per, acc_rows, (zero,) * CONV_WIDTH)
            for k in range(CONV_WIDTH):
                dcw_ref[k:k + 1, ls] += _rowsum(accs[k])
        _conv_rows(dcv, cw_ref, lambda k: CONV_WIDTH - 1 - k, conv_s, per, cd)
        dglu = _from_lane_blocks(conv_s)
        dz_ref[:, 0:cd] = (dglu * sg).astype(BF16)
        dz_ref[:, cd:2 * cd] = (dglu * za * sg * (1.0 - sg)).astype(BF16)
        dcv[:, rt:rt + HALO, :] = dcv[:, 0:HALO, :]
        for gi, window in enumerate(POOL_WINDOWS):
            ls = slice(gi * pg, (gi + 1) * pg)
            v = pext[:, ls]
            cnt = _pool_counts(rows, window)
            tm = (_trailing_sum(v, window)[HALO:] / cnt - v[HALO:]).astype(BF16)
            dp = dcat[:, cd + gi * pg:cd + (gi + 1) * pg]
            dps_ref[:, ls] += _rowsum(dp * _dot(tm, pw_ref[gi]))
            dpl = (dp * ps_ref[:, ls]).astype(BF16)
            dpw_ref[gi] += _dot_tn(tm, dpl)
            dtm = _dot_nt(dpl, pw_ref[gi])
            dsp[0:rt, ls] = dtm / cnt
            dpin = _leading_sum(dsp[:, ls], window)[0:rt] - dtm
            dz_ref[:, 2 * cd + gi * pg:2 * cd + (gi + 1) * pg] = dpin.astype(BF16)
        dsp[rt:rt + HALO, :] = dsp[0:HALO, :]

        _project_back(dz_ref, wi_ref, h_ref, g_ref, dh_ref, dhi.at[slot], dg_ref, step == 0)

        @pl.when(tile > 0)
        def _():
            tile_rows(tile, slot).start()

        @pl.when(tile == 0)
        def _():
            first_rows.start()
            dfirst_ref[...] = dhi[last_slot, 0:CHUNK, :]
            if ntile > 1:
                tile_rows(1, 1 - last_slot).wait()
            first_rows.wait()

    d = h.shape[1]
    back = lambda i: (ntile - 1 - i, 0)
    halo_idx = lambda i: (jnp.maximum((ntile - 1 - i) * per - 1, 0), 0)
    const2 = lambda i: (0, 0)
    nl, lb = len(_lane_blocks(cd)), min(LANE, cd)
    return pl.pallas_call(
        body, grid=(ntile,),
        in_specs=dep_specs + [
                  pl.BlockSpec((rt, ein), back), pl.BlockSpec((CHUNK, ein), halo_idx), pl.BlockSpec((rt, cd), back),
                  pl.BlockSpec((rt, d), back),
                  _resident((1, d)), _resident(w_in_t.shape), pl.BlockSpec((rt, d), back), _resident(w_out.shape),
                  _resident(conv_w.shape), _resident((1, cd)), _resident((1, cd)), _resident((1, cd)),
                  _resident(pool_w.shape), _resident((1, pd))],
        out_specs=[pl.BlockSpec(memory_space=pl.ANY), pl.BlockSpec((CHUNK, d), const2), pl.BlockSpec((1, d), const2),
                   pl.BlockSpec((rt, ein), back), pl.BlockSpec(conv_w.shape, const2), pl.BlockSpec((1, cd), const2),
                   pl.BlockSpec((1, cd), const2), pl.BlockSpec((1, cd), const2),
                   pl.BlockSpec(pool_w.shape, lambda i: (0, 0, 0)), pl.BlockSpec((1, pd), const2)],
        out_shape=[_sds((t - CHUNK, d), F32), _sds((CHUNK, d), F32), _sds((1, d), F32),
                   _sds((t, ein), BF16), _sds(conv_w.shape, F32), _sds((1, cd), F32), _sds((1, cd), F32),
                   _sds((1, cd), F32), _sds(pool_w.shape, F32), _sds((1, pd), F32)],
        scratch_shapes=[pltpu.VMEM((nl, rt + HALO, lb), F32), pltpu.VMEM((rt + HALO, pd), F32), pltpu.VMEM((nl, rt, lb), F32),
                        pltpu.VMEM((nl, rt + HALO, lb), F32), pltpu.VMEM((rt + HALO, pd), F32), pltpu.VMEM((2, rt, d), F32),
                        pltpu.SemaphoreType.DMA((2,))],
        compiler_params=_cparams("arbitrary"), name=name)(*deps, z, z, cv, h, gain, w_in_t, dh, w_out, conv_w, conv_b, ln_g,
                                                          ln_b, pool_w, pool_scale)


def _log_decay(r, gw_ref, gb_ref, rows):
    gp = _dot(r.astype(BF16), gw_ref[...]) + gb_ref[...]
    log_sig = jnp.minimum(gp, 0.0) - jnp.log(1.0 + jnp.exp(-jnp.abs(gp)))
    return gp, jnp.where(rows >= PAD_ROWS, log_sig / GATE_NORM, 0.0)


def _tri(strict):
    r = lax.broadcasted_iota(jnp.int32, (CHUNK, CHUNK), 0)
    c = lax.broadcasted_iota(jnp.int32, (CHUNK, CHUNK), 1)
    return jnp.where(c < r if strict else c <= r, 1.0, 0.0).astype(BF16)


def _tri_dot(tri, a):
    hi = a.astype(BF16)
    rest = a - hi.astype(F32)
    mid = rest.astype(BF16)
    lo = (rest - mid.astype(F32)).astype(BF16)
    return _dot(tri, hi) + _dot(tri, mid) + _dot(tri, lo)


def _gla_mid_fwd(h, gain, w_in_blocks, w_out, gate_w, gate_b, head_g, *, name):
    t = h.shape[0]
    nblk, rpb = w_in_blocks.shape[:2]
    dk = gate_b.shape[1]
    hv = head_g.shape[1]
    hk = dk // HEADS
    dv = hv * HEADS
    r_at = 2 * dk + 2 * dv
    assert nblk * rpb == r_at + GATE_RANK
    zw = r_at + GATE_PAD
    rt = _row_tile(t, 320)
    per = rt // CHUNK
    scale = hk ** -0.5

    def body(h_ref, g_ref, wb_ref, wo_ref, gw_ref, gb_ref, hg_ref, ho_ref, o_ref, st_ref, z_ref, u_ref, wi_ref,
             s_ref, la_ref, dec_ref):
        i = pl.program_id(0)

        @pl.when(i == 0)
        def _():
            s_ref[...] = jnp.zeros_like(s_ref)
            for b in range(nblk):
                wi_ref[b * rpb:(b + 1) * rpb, :] = wb_ref[b]
            wi_ref[nblk * rpb:, :] = jnp.zeros((zw - nblk * rpb, wi_ref.shape[1]), BF16)

        _norm_project(h_ref, g_ref, wi_ref, u_ref, z_ref)

        _, la = _log_decay(z_ref[:, r_at:r_at + GATE_PAD], gw_ref, gb_ref, _row_ids(i, rt))
        la_ref[...] = la
        tri = _tri(False)

        def chunk_rows(c):
            return slice(c * CHUNK, (c + 1) * CHUNK)

        def decays(c, carry):
            rows = chunk_rows(c)
            la_c = la_ref[rows, :]
            cum = _tri_dot(tri, la_c)
            dec_ref[rows, :] = jnp.exp(_rowsum(la_c) - cum)
            return carry

        def states(c, carry):
            rows = chunk_rows(c)
            etot = jnp.exp(_rowsum(la_ref[rows, :]))
            for hd in range(HEADS):
                ks = slice(hd * hk, (hd + 1) * hk)
                kd = z_ref[rows, dk + hd * hk:dk + (hd + 1) * hk] * dec_ref[rows, ks]
                v = z_ref[rows, 2 * dk + hd * hv:2 * dk + (hd + 1) * hv]
                s_new = s_ref[hd] * etot[:, ks] + _dot_tn(v.astype(BF16), kd.astype(BF16))
                s_ref[hd] = s_new
                st_ref[c, hd] = s_new
            return carry

        def outputs(c, carry):
            rows = chunk_rows(c)
            for hd in range(HEADS):
                q = z_ref[rows, hd * hk:(hd + 1) * hk] * scale
                g = z_ref[rows, 2 * dk + dv + hd * hv:2 * dk + dv + (hd + 1) * hv]
                o = _dot_nt(q.astype(BF16), st_ref[c, hd].astype(BF16))
                on = o * lax.rsqrt(jnp.mean(o * o, axis=-1, keepdims=True) + EPS) * hg_ref[...]
                o_ref[rows, hd * hv:(hd + 1) * hv] = (on * (g * _sigmoid(g))).astype(BF16)
            return carry

        for phase in (decays, states, outputs):
            for c in range(per):
                phase(c, 0)
        ho_ref[...] = h_ref[...] + _dot(o_ref[...], wo_ref[...])

    d = h.shape[1]
    rows = lambda i: (i, 0)
    return pl.pallas_call(
        body, grid=(t // rt,),
        in_specs=[pl.BlockSpec((rt, d), rows), _resident((1, d)), _resident(w_in_blocks.shape), _resident(w_out.shape),
                  _resident(gate_w.shape), _resident((1, dk)), _resident((1, hv))],
        out_specs=[pl.BlockSpec((rt, d), rows), pl.BlockSpec((rt, dv), rows),
                   pl.BlockSpec((per, HEADS, hv, hk), lambda i: (i, 0, 0, 0)), pl.BlockSpec((rt, zw), rows),
                   pl.BlockSpec((rt, d), rows), pl.BlockSpec((zw, d), lambda i: (0, 0))],
        out_shape=[_sds((t, d), F32), _sds((t, dv), BF16), _sds((t // CHUNK, HEADS, hv, hk), F32), _sds((t, zw), F32),
                   _sds((t, d), BF16), _sds((zw, d), BF16)],
        scratch_shapes=[pltpu.VMEM((HEADS, hv, hk), F32), pltpu.VMEM((rt, dk), F32), pltpu.VMEM((rt, dk), F32)],
        compiler_params=_cparams("arbitrary"), name=name)(h, gain, w_in_blocks, w_out, gate_w, gate_b, head_g)


def _gla_mid_bwd(z, h, gain, w_in_t, dh, w_out, states, gate_w, gate_b, head_g, *, after=None, name):
    t = z.shape[0]
    dk = gate_b.shape[1]
    hv = head_g.shape[1]
    hk = dk // HEADS
    dv = hv * HEADS
    r_at = 2 * dk + 2 * dv
    rt = _row_tile(t, 320)
    ntile = t // rt
    per = rt // CHUNK
    scale = hk ** -0.5
    dep_specs, deps = _dep_specs(after)

    def body(*refs):
        (z_ref, h_ref, g_ref, wi_ref, dh_ref, wo_ref, st_ref, stp_ref, gw_ref, gb_ref, hg_ref,
         dhi_ref, dg_ref, dz_ref, dgw_ref, dgb_ref, dhg_ref,
         ds_ref, la_ref, dla_ref, dec_ref, dos_ref, e_ref, do_ref) = refs[len(deps):]
        step = pl.program_id(0)
        tile = ntile - 1 - step
        do_ref[...] = _dot_nt(dh_ref[...].astype(BF16), wo_ref[...])

        @pl.when(step == 0)
        def _():
            ds_ref[...] = jnp.zeros_like(ds_ref)
            dgw_ref[...] = jnp.zeros_like(dgw_ref)
            dgb_ref[...] = jnp.zeros_like(dgb_ref)
            dhg_ref[...] = jnp.zeros_like(dhg_ref)

        rows_id = _row_ids(tile, rt)
        r = z_ref[:, r_at:r_at + GATE_PAD]
        gp, la = _log_decay(r, gw_ref, gb_ref, rows_id)
        la_ref[...] = la
        tri, tri_strict = _tri(False), _tri(True)
        keep = jnp.where(tile > 0, 1.0, 0.0)

        def chunk_rows(c):
            return slice(c * CHUNK, (c + 1) * CHUNK)

        def recompute(c, dhg):
            rows = chunk_rows(c)
            la_c = la_ref[rows, :]
            cum = _tri_dot(tri, la_c)
            dec_ref[rows, :] = jnp.exp(_rowsum(la_c) - cum)
            for hd in range(HEADS):
                q = (z_ref[rows, hd * hk:(hd + 1) * hk] * scale).astype(BF16)
                g = z_ref[rows, 2 * dk + dv + hd * hv:2 * dk + dv + (hd + 1) * hv]
                s_b = st_ref[c, hd].astype(BF16)
                o = _dot_nt(q, s_b)
                rstd = lax.rsqrt(jnp.mean(o * o, axis=-1, keepdims=True) + EPS)
                oh = o * rstd
                sg = _sigmoid(g)
                d_og = do_ref[rows, hd * hv:(hd + 1) * hv]
                dz_ref[rows, 2 * dk + dv + hd * hv:2 * dk + dv + (hd + 1) * hv] = (
                    d_og * oh * hg_ref[...] * (sg * (1.0 + g * (1.0 - sg)))).astype(BF16)
                don = d_og * (g * sg)
                dhg = dhg + _rowsum(don * oh)
                doh = don * hg_ref[...]
                d_o = (rstd * (doh - oh * jnp.mean(doh * oh, axis=-1, keepdims=True))).astype(BF16)
                dos_ref[rows, hd * hv:(hd + 1) * hv] = d_o
                dz_ref[rows, hd * hk:(hd + 1) * hk] = (_dot(d_o, s_b) * scale).astype(BF16)
            return dhg

        def recurrence(cc, carry):
            c = per - 1 - cc
            rows = chunk_rows(c)
            etot = jnp.exp(_rowsum(la_ref[rows, :]))
            for hd in range(HEADS):
                ks = slice(hd * hk, (hd + 1) * hk)
                q = (z_ref[rows, hd * hk:(hd + 1) * hk] * scale).astype(BF16)
                dec = dec_ref[rows, ks]
                kd = z_ref[rows, dk + hd * hk:dk + (hd + 1) * hk] * dec
                v = z_ref[rows, 2 * dk + hd * hv:2 * dk + (hd + 1) * hv].astype(BF16)
                s_prev = st_ref[c - 1, hd] if c > 0 else keep * stp_ref[0, hd]
                ds_t = ds_ref[hd] + _dot_tn(dos_ref[rows, hd * hv:(hd + 1) * hv], q)
                ds_b = ds_t.astype(BF16)
                dkd = _dot(v, ds_b)
                dz_ref[rows, 2 * dk + hd * hv:2 * dk + (hd + 1) * hv] = _dot_nt(kd.astype(BF16), ds_b).astype(BF16)
                dtot = etot[:, ks] * _rowsum(ds_t * s_prev)
                ds_ref[hd] = ds_t * etot[:, ks]
                dz_ref[rows, dk + hd * hk:dk + (hd + 1) * hk] = (dkd * dec).astype(BF16)
                e_ref[rows, ks] = dkd * kd
                dla_ref[rows, ks] = jnp.broadcast_to(dtot, (CHUNK, hk))
            return carry

        def decay_cotangent(c, carry):
            rows = chunk_rows(c)
            dla_ref[rows, :] += _tri_dot(tri_strict, e_ref[rows, :])
            return carry

        dhg = jnp.zeros((1, hv), F32)
        for c in range(per):
            dhg = recompute(c, dhg)
        dhg_ref[...] += dhg
        for phase in (recurrence, decay_cotangent):
            for c in range(per):
                phase(c, 0)
        dla = jnp.where(rows_id >= PAD_ROWS, dla_ref[...], 0.0)
        dgp = dla * (1.0 / GATE_NORM) * (1.0 - _sigmoid(gp))
        dgb_ref[...] += _rowsum(dgp)
        dgp_b = dgp.astype(BF16)
        dgw_ref[...] += _dot_tn(r.astype(BF16), dgp_b)
        dz_ref[:, r_at:r_at + GATE_PAD] = _dot_nt(dgp_b, gw_ref[...]).astype(BF16)
        _project_back(dz_ref, wi_ref, h_ref, g_ref, dh_ref, dhi_ref, dg_ref, step == 0)

    d = h.shape[1]
    back = lambda i: (ntile - 1 - i, 0)
    const2 = lambda i: (0, 0)
    return pl.pallas_call(
        body, grid=(ntile,),
        in_specs=dep_specs + [
                  pl.BlockSpec((rt, z.shape[1]), back), pl.BlockSpec((rt, d), back), _resident((1, d)),
                  _resident(w_in_t.shape), pl.BlockSpec((rt, d), back), _resident(w_out.shape),
                  pl.BlockSpec((per, HEADS, hv, hk), lambda i: (ntile - 1 - i, 0, 0, 0)),
                  pl.BlockSpec((1, HEADS, hv, hk), lambda i: (jnp.maximum((ntile - 1 - i) * per - 1, 0), 0, 0, 0)),
                  _resident(gate_w.shape), _resident((1, dk)), _resident((1, hv))],
        out_specs=[pl.BlockSpec((rt, d), back), pl.BlockSpec((1, d), const2), pl.BlockSpec((rt, z.shape[1]), back),
                   pl.BlockSpec(gate_w.shape, const2), pl.BlockSpec((1, dk), const2), pl.BlockSpec((1, hv), const2)],
        out_shape=[_sds((t, d), F32), _sds((1, d), F32), _sds(z.shape, BF16), _sds(gate_w.shape, F32),
                   _sds((1, dk), F32), _sds((1, hv), F32)],
        scratch_shapes=[pltpu.VMEM((HEADS, hv, hk), F32), pltpu.VMEM((rt, dk), F32), pltpu.VMEM((rt, dk), F32),
                        pltpu.VMEM((rt, dk), F32), pltpu.VMEM((rt, dv), BF16), pltpu.VMEM((rt, dk), F32),
                        pltpu.VMEM((rt, dv), F32)],
        compiler_params=_cparams("arbitrary"), name=name)(*deps, z, h, gain, w_in_t, dh, w_out, states, states, gate_w,
                                                          gate_b, head_g)


def _adamw_math(w, g, m, v):
    m = ADAM_B1 * m + (1.0 - ADAM_B1) * g
    v = ADAM_B2 * v + (1.0 - ADAM_B2) * (g * g)
    m_hat = m / (1.0 - ADAM_B1 ** ADAM_STEP)
    v_hat = v / (1.0 - ADAM_B2 ** ADAM_STEP)
    return -ADAM_LR * (m_hat / (jnp.sqrt(v_hat) + ADAM_EPS) + ADAM_WD * w), m, v


N_CHIP = N_DEV // 2
BLOCK_ELEMS = 128 * 1024


def _my_slot():
    return 4 * lax.axis_index("x") + 2 * lax.axis_index("y") + lax.axis_index("c")


def _row_block(r, c):
    cap = max(8, BLOCK_ELEMS // (-(-c // LANE) * LANE))
    return max([b for b in range(8, r + 1, 8) if r % b == 0 and b <= cap] or [r])


def _blocks(r, c):
    rb = _row_block(r, c)
    if rb < r or r * c <= BLOCK_ELEMS:
        return rb, c
    return r, max([b for b in (512, 256, LANE) if c % b == 0 and r * b <= BLOCK_ELEMS] or [c])


def _reduce_adam(parts, w, m, v, *, by_row=False, after=None, name):
    nl, r, c = (1, w.shape[0], w.shape[2]) if by_row else w.shape
    rb, cb = _blocks(r, c)
    dep_specs, deps = _dep_specs(after)
    whole = (slice(None), 0, slice(None)) if by_row else Ellipsis

    def body(*refs):
        me = refs[0][0]
        refs = refs[1 + len(deps):]
        p_refs = refs[:2 * nl]
        w_ref, m_ref, v_ref, g_out, d_out, m_out, v_out = refs[2 * nl:]
        layer = pl.program_id(0)
        for li in range(nl):
            @pl.when(layer == li)
            def _(li=li):
                own_ref, land_ref = p_refs[2 * li], p_refs[2 * li + 1]
                mine = own_ref[...].astype(F32)
                g = None
                for dev in range(N_DEV):
                    term = jnp.where(me == dev, mine, land_ref[dev].astype(F32))
                    g = term if g is None else g + term
                g_out[whole] = g
                d_out[whole], m_out[whole], v_out[whole] = _adamw_math(w_ref[whole], g, m_ref[whole], v_ref[whole])

    if by_row:
        blk = pl.BlockSpec((rb, 1, cb), lambda l, i, j, me: (i, 0, j))
    else:
        blk = pl.BlockSpec((None, rb, cb), lambda l, i, j, me: (l, i, j))
    p_specs = []
    for li in range(nl):
        p_specs += [
            pl.BlockSpec((None, rb, cb), lambda l, i, j, me, li=li: (me[0], jnp.where(l == li, i, 0), jnp.where(l == li, j, 0))),
            pl.BlockSpec((N_DEV, rb, cb), lambda l, i, j, me, li=li: (0, jnp.where(l == li, i, 0), jnp.where(l == li, j, 0)))]
    flat = [p for pair in parts for p in pair]
    grid_spec = pltpu.PrefetchScalarGridSpec(
        num_scalar_prefetch=1, grid=(nl, r // rb, c // cb), in_specs=dep_specs + p_specs + [blk, blk, blk],
        out_specs=[blk] * 4)
    return pl.pallas_call(
        body, grid_spec=grid_spec, out_shape=[_sds(w.shape, F32)] * 4,
        compiler_params=_cparams("arbitrary", "arbitrary", "arbitrary"), name=name)(
        _my_slot().reshape(1), *deps, *flat, w, m, v)


def _sum8(own, landed, *, name):
    def body(own_ref, land_ref, o_ref):
        me = _my_slot()
        total = None
        for dev in range(N_DEV):
            term = jnp.where(me == dev, own_ref[...], land_ref[dev])
            total = term if total is None else total + term
        o_ref[...] = total

    return pl.pallas_call(body, out_shape=_sds(own.shape, F32), name=name)(own, landed)


def _adam_small(own, landed, split, w, m, v, *, name):
    n = len(w)

    def body(*refs):
        own_refs, land_refs, w_refs, m_refs, v_refs = (refs[k * n:(k + 1) * n] for k in range(5))
        outs = refs[5 * n:]
        me = _my_slot()
        for k in range(n):
            mine = own_refs[k][me] if split[k] else own_refs[k][...]
            g = None
            for dev in range(N_DEV):
                term = jnp.where(me == dev, mine, land_refs[k][dev])
                g = term if g is None else g + term
            outs[4 * k][...] = g
            outs[4 * k + 1][...], outs[4 * k + 2][...], outs[4 * k + 3][...] = _adamw_math(
                w_refs[k][...], g, m_refs[k][...], v_refs[k][...])

    out = pl.pallas_call(body, out_shape=[_sds(a.shape, F32) for a in w for _ in range(4)],
                         compiler_params=pltpu.CompilerParams(vmem_limit_bytes=V7X_VMEM_LIMIT), name=name)(
        *own, *landed, *w, *m, *v)
    return [tuple(out[4 * k:4 * k + 4]) for k in range(n)]


_HBM = pl.BlockSpec(memory_space=pltpu.HBM)
_SEM = pl.BlockSpec(memory_space=pltpu.SEMAPHORE)
_DATAFLOW = pltpu.SideEffectType.DATAFLOW_SIDE_EFFECTING


def _plan_to_all(src, land):
    x, y, c = lax.axis_index("x"), lax.axis_index("y"), lax.axis_index("c")
    return [(src, land.at[_my_slot()], (x ^ ((d >> 2) & 1), y ^ ((d >> 1) & 1), c ^ (d & 1))) for d in range(1, N_DEV)]


def _plan_split_to_all(src, land):
    x, y, c = lax.axis_index("x"), lax.axis_index("y"), lax.axis_index("c")
    peers = [(x ^ ((d >> 2) & 1), y ^ ((d >> 1) & 1), c ^ (d & 1)) for d in range(1, N_DEV)]
    return [(src.at[4 * px + 2 * py + pc], land.at[_my_slot()], (px, py, pc)) for px, py, pc in peers]


_PLAN_COPIES = {_plan_to_all: N_DEV - 1, _plan_split_to_all: N_DEV - 1}


def _plans(plan, n):
    return list(plan) if isinstance(plan, (list, tuple)) else [plan] * n


def _exchange_copies(plan, ins, lands, send, recv):
    copies, sem = [], 0
    for p, src, land in zip(_plans(plan, len(lands)), ins, lands):
        for s, dst, dev in p(src, land):
            copies.append(pltpu.make_async_remote_copy(
                src_ref=s, dst_ref=dst, send_sem=send.at[sem], recv_sem=recv.at[sem],
                device_id=dev, device_id_type=pl.DeviceIdType.MESH))
            sem += 1
    return copies


def _place_own(srcs, *, after=None, name):
    dep_specs, deps = _dep_specs(after)
    n = len(srcs)
    arrays, in_specs, shapes = [], [], []
    for a, dtype in srcs:
        if isinstance(a, tuple):
            a, layer = a
            in_specs.append(pl.BlockSpec((None,) + a.shape[1:], lambda i, layer=layer: (layer, 0, 0)))
            shapes.append(a.shape[1:])
        else:
            in_specs.append(pl.BlockSpec(a.shape, lambda i: (0, 0)))
            shapes.append(a.shape)
        arrays.append(a)
    dtypes = [dtype for _, dtype in srcs]

    def body(*refs):
        refs = refs[len(deps):]
        a_refs, o_refs, cast_refs, sem = refs[:n], refs[n:2 * n], refs[2 * n:3 * n], refs[3 * n]
        me = _my_slot()
        copies = []
        for k in range(n):
            cast_refs[k][...] = a_refs[k][...].astype(dtypes[k])
            copies.append(pltpu.make_async_copy(cast_refs[k], o_refs[k].at[me], sem.at[k]))
            copies[-1].start()
        for cp in copies:
            cp.wait()

    return pl.pallas_call(
        body, grid=(1,), in_specs=dep_specs + in_specs, out_specs=[pl.BlockSpec(memory_space=pl.ANY)] * n,
        out_shape=[_sds((N_DEV,) + shape, dtype) for shape, dtype in zip(shapes, dtypes)],
        scratch_shapes=[pltpu.VMEM(shape, dtype) for shape, dtype in zip(shapes, dtypes)] + [pltpu.SemaphoreType.DMA((n,))],
        compiler_params=pltpu.CompilerParams(vmem_limit_bytes=V7X_VMEM_LIMIT), name=name)(*deps, *arrays)


def _plan_gather_first(land, _):
    x, y, c = lax.axis_index("x"), lax.axis_index("y"), lax.axis_index("c")
    mine = land.at[_my_slot()]
    return [(mine, mine, (x, y, 1 - c))] + [(mine, mine, (x ^ (d >> 1), y ^ (d & 1), c)) for d in range(1, N_CHIP)]


def _plan_gather_relay(land, _):
    x, y, c = lax.axis_index("x"), lax.axis_index("y"), lax.axis_index("c")
    slots = [land.at[4 * (x ^ (d >> 1)) + 2 * (y ^ (d & 1)) + c] for d in range(1, N_CHIP)]
    return [(s, s, (x, y, 1 - c)) for s in slots]


def _plan_gather_direct(land, _):
    return _plan_to_all(land.at[_my_slot()], land)


_PLAN_COPIES[_plan_gather_first] = N_CHIP
_PLAN_COPIES[_plan_gather_relay] = N_CHIP - 1
_PLAN_COPIES[_plan_gather_direct] = N_DEV - 1


def _exchange_start(plan, arrs, lands, *, after=None, name):
    bufs = list(lands) if arrs is None else list(arrs) + list(lands)
    n, nb = len(lands), len(bufs)
    nsem = sum(_PLAN_COPIES[p] for p in _plans(plan, n))
    dep_specs, deps = _dep_specs(after)

    def body(*refs):
        ins, land_refs = refs[:n], refs[nb - n:nb]
        send, recv = refs[nb + len(deps)], refs[nb + len(deps) + 1]
        for cp in _exchange_copies(plan, ins, land_refs, send, recv):
            cp.start()
        refs[-1][...] = jnp.zeros_like(refs[-1])

    out = pl.pallas_call(
        body, name=name,
        out_shape=(pltpu.SemaphoreType.DMA((nsem,)), pltpu.SemaphoreType.DMA((nsem,)),
                   *[pltpu.HBM(a.shape, a.dtype) for a in bufs], _sds((8, LANE), F32)),
        in_specs=[_HBM] * nb + dep_specs,
        out_specs=(_SEM, _SEM, *([_HBM] * nb), pl.BlockSpec(memory_space=pltpu.VMEM)),
        input_output_aliases={i: 2 + i for i in range(nb)},
        compiler_params=pltpu.CompilerParams(has_side_effects=_DATAFLOW),
    )(*[pltpu.with_memory_space_constraint(a, pltpu.HBM) for a in bufs], *deps)
    return (plan, n, out[0], out[1], list(out[2:2 + nb])), out[-1]


def _exchange_now(plan, lands, *, name):
    n = len(lands)
    nsem = sum(_PLAN_COPIES[p] for p in _plans(plan, n))

    def body(*refs):
        land_refs, send, recv = refs[n:2 * n], refs[2 * n], refs[2 * n + 1]
        copies = _exchange_copies(plan, land_refs, land_refs, send, recv)
        for cp in copies:
            cp.start()
        for cp in copies:
            cp.wait_send()
            cp.wait_recv()

    hbm = pl.BlockSpec(memory_space=pl.ANY)
    return pl.pallas_call(
        body, in_specs=[hbm] * n, out_specs=[hbm] * n, out_shape=[_sds(a.shape, a.dtype) for a in lands],
        input_output_aliases={i: i for i in range(n)},
        scratch_shapes=[pltpu.SemaphoreType.DMA((nsem,)), pltpu.SemaphoreType.DMA((nsem,))], name=name)(*lands)


def _exchange_wait(state, after, *, name):
    plan, n, send_sem, recv_sem, bufs = state
    nb = len(bufs)
    after = list(after) if isinstance(after, (list, tuple)) else [after]

    def body(*refs):
        ins, land_refs, send, recv = refs[:n], refs[nb - n:nb], refs[nb], refs[nb + 1]
        for cp in _exchange_copies(plan, ins, land_refs, send, recv):
            cp.wait_send()
            cp.wait_recv()

    out = pl.pallas_call(
        body, name=name, out_shape=[pltpu.HBM(a.shape, a.dtype) for a in bufs],
        in_specs=[_HBM] * nb + [_SEM, _SEM] + [pl.BlockSpec(memory_space=pl.ANY)] * len(after), out_specs=[_HBM] * nb,
        input_output_aliases={i: i for i in range(nb)},
        compiler_params=pltpu.CompilerParams(has_side_effects=_DATAFLOW),
    )(*bufs, send_sem, recv_sem, *after)
    return list(out[:n]), list(out[nb - n:])


def _dep_specs(after):
    return ([], []) if after is None else ([pl.BlockSpec(memory_space=pl.ANY)], [after])


def _undo_column_split(g):
    return jnp.transpose(g, (1, 0, 2)).reshape(g.shape[1], N_DEV * g.shape[2])


def _column_split(a):
    r, c = a.shape
    return jnp.transpose(a.reshape(r, N_DEV, c // N_DEV), (1, 0, 2))


class _WholeWeights:
    def __init__(self, groups):
        self.groups = groups
        self.grads = {}

    def fetch(self, group, after):
        return self.groups[group]

    def emit(self, group, grads):
        self.grads.update(grads)
        return None


def _local_step(x, target, replicated, src):
    d = x.shape[1]
    mix_g, ffn_g = replicated["mix_g"], replicated["ffn_g"]
    cp = src.fetch("cp", [])
    cp_mid = (cp["conv_w"], replicated["conv_b"], replicated["ln_g"], replicated["ln_b"], replicated["pool_w"],
              replicated["pool_scale"])

    h1, cat, z0, u0, cv0, h0 = _cp_mid_fwd(x, cp["meta"], mix_g[0:1], cp["cp_w_in_t"], cp["cp_w_out"], *cp_mid,
                                           name="cp_mixer")
    ffn0 = src.fetch("ffn0", h1)
    h2, uf0, rf0 = _ffn_fwd(h1, ffn_g[0:1], ffn0["w1"], ffn0["w2"], name="ffn0")
    gla = src.fetch("gla", h2)
    gla_mid = (gla["gate_w"], gla["gate_b"], gla["head_g"])
    h3, og, states, z1, u1, gla_w_in_t = _gla_mid_fwd(h2, mix_g[1:2], gla["gla_w_in_t"], gla["gla_w_out"], *gla_mid,
                                                      name="gla_mixer")
    ffn1 = src.fetch("ffn1", h3)
    dh4, uf1, rf1, loss, d_final_g = _ffn_fwd(h3, ffn_g[1:2], ffn1["w1"], ffn1["w2"],
                                              loss_head=(replicated["final_g"], target), name="ffn1_loss")

    dh3, dhh1, dob1, dffn_g1, dh3_b = _ffn_bwd_x(h3, dh4, ffn_g[1:2], rf1, ffn1["w1"], ffn1["w2"], name="ffn1_bwd_x")
    sent = src.emit("ffn1", dict(w1=_linear_bwd_w(uf1, dhh1, column_blocks=N_DEV, name="ffn1_dw1"),
                                 w2=_linear_bwd_w(rf1, dob1, square_x=True, name="ffn1_dw2")))
    d_gla_w_out = _linear_bwd_w(og, dh3_b, name="gla_out_dw")
    dh2, dmix_g1, dz1, d_gate_w, d_gate_b, d_head_g = _gla_mid_bwd(
        z1, h2, mix_g[1:2], gla_w_in_t, dh3, gla["gla_w_out"], states, *gla_mid, after=sent, name="gla_mixer_bwd")
    d_gla_w_in_t = _linear_bwd_w(dz1, u1, row_blocks=gla["gla_w_in_t"].shape[:2], name="gla_in_dw")
    sent = src.emit("gla", dict(gla_w_in_t=d_gla_w_in_t, gla_w_out=d_gla_w_out))
    dh1, dhh0, dob0, dffn_g0, dh1_b = _ffn_bwd_x(h1, dh2, ffn_g[0:1], rf0, ffn0["w1"], ffn0["w2"], after=sent,
                                                 name="ffn0_bwd_x")
    sent = src.emit("ffn0_w1", dict(w1=_linear_bwd_w(uf0, dhh0, column_blocks=N_DEV, name="ffn0_dw1")))
    sent = src.emit("ffn0_w2", dict(w2=_linear_bwd_w(rf0, dob0, square_x=True, after=sent, name="ffn0_dw2")))
    d_cp_w_out = _linear_bwd_w(cat, dh1_b, after=sent, name="cp_out_dw")
    sent = src.emit("cp_out", dict(cp_w_out=d_cp_w_out))
    dx, dh0_first, dmix_g0, dz0, d_conv_w, d_conv_b, d_ln_g, d_ln_b, d_pool_w, d_pool_scale = _cp_mid_bwd(
        z0, cv0, h0, mix_g[0:1], cp["cp_w_in_t"], dh1, cp["cp_w_out"], *cp_mid, after=sent, name="cp_mixer_bwd")
    d_cp_w_in_t = _linear_bwd_w(dz0, u0, name="cp_in_dw")

    small = dict(
        mix_g=jnp.concatenate([dmix_g0, dmix_g1]), ffn_g=jnp.concatenate([dffn_g0, dffn_g1]), conv_b=d_conv_b, ln_g=d_ln_g,
        ln_b=d_ln_b, pool_w=d_pool_w, pool_scale=d_pool_scale, final_g=d_final_g, meta=dh0_first[PAD_ROWS:], conv_w=d_conv_w,
        gate_w=d_gate_w, gate_b=d_gate_b, head_g=d_head_g)
    src.emit("cp", dict(cp_w_in_t=d_cp_w_in_t, small=small, loss=loss))
    return loss, dx, small


_REPLICATED = ("mix_norm_g", "ffn_norm_g", "cp_conv_b", "cp_ln_g", "cp_ln_b", "cp_pool_w", "cp_pool_scale", "final_norm_g")
_SMALL_SHARDED = ("meta_tokens", "cp_conv_w", "gla_gate_w2", "gla_gate_b", "gla_head_g")
_NAMES = ("meta_tokens", "mix_norm_g", "ffn_norm_g", "ffn_w1", "ffn_w2", "cp_w_in", "cp_conv_w", "cp_conv_b", "cp_ln_g",
          "cp_ln_b", "cp_pool_w", "cp_pool_scale", "cp_w_out", "gla_w_in", "gla_gate_w2", "gla_gate_b", "gla_head_g",
          "gla_w_out", "final_norm_g")
_SMALL_GRADS = ("mix_g", "ffn_g", "conv_b", "ln_g", "ln_b", "pool_w", "pool_scale", "final_g", "meta", "conv_w", "gate_w",
                "gate_b", "head_g")
_GROUPS = ("cp", "ffn0", "gla", "ffn1")
_TWO_LEG_GATHERS = ("cp", "ffn0", "ffn1")


class _Exchanges:
    def __init__(self, w, d):
        self.d = d
        small = [w[n].reshape(w[n].shape[-2:]) for n in _SMALL_SHARDED]
        self.small_shard_shapes = [w[n].shape for n in _SMALL_SHARDED]
        shards = dict(
            cp=[(w["cp_w_in"][0].T, BF16), (w["cp_w_out"][0], BF16)] + [(a, F32) for a in small],
            ffn0=[((w["ffn_w1"], 0), BF16), ((w["ffn_w2"], 0), BF16)],
            gla=[(w["gla_w_in"][0].T, BF16), (w["gla_w_out"][0], BF16)],
            ffn1=[((w["ffn_w1"], 1), BF16), ((w["ffn_w2"], 1), BF16)])
        self.gathers = {}
        self.sent = {}
        token = None
        for group in _GROUPS:
            lands = _place_own(shards[group], after=token, name=f"place_w_{group}")
            plan = _plan_gather_first if group in _TWO_LEG_GATHERS else _plan_gather_direct
            self.gathers[group], token = _exchange_start(plan, None, lands, after=token, name=f"start_w_{group}")
        self.token = token

    def fetch(self, group, after):
        d = self.d
        after = (list(after) if isinstance(after, (list, tuple)) else [after]) + [self.token]
        _, got = _exchange_wait(self.gathers[group], after, name=f"wait_w_{group}")
        if group in _TWO_LEG_GATHERS:
            got = _exchange_now(_plan_gather_relay, got, name=f"relay_w_{group}")
        if group in ("ffn0", "ffn1"):
            return dict(w1=got[0], w2=got[1])
        if group == "gla":
            return dict(gla_w_in_t=got[0], gla_w_out=got[1].reshape(d, d), gate_w=self.gate_w, gate_b=self.gate_b,
                        head_g=self.head_g)
        meta, conv_w, gate_w, self.gate_b, self.head_g = [_undo_column_split(a) for a in got[2:]]
        self.gate_w = jnp.pad(gate_w, ((0, GATE_PAD - GATE_RANK), (0, 0))).astype(BF16)
        return dict(cp_w_in_t=got[0].reshape(-1, d), cp_w_out=got[1].reshape(d, d), meta=meta,
                    conv_w=jnp.pad(conv_w, ((0, 1), (0, 0))))

    def emit(self, group, g):
        d = self.d
        if group == "ffn1":
            arrs = [g["w1"], g["w2"].reshape(N_DEV, -1, d)]
        elif group == "ffn0_w1":
            arrs = [g["w1"]]
        elif group == "ffn0_w2":
            arrs = [g["w2"].reshape(N_DEV, -1, d)]
        elif group == "gla":
            arrs = [g["gla_w_in_t"], g["gla_w_out"].reshape(N_DEV, d // N_DEV, d)]
        elif group == "cp_out":
            arrs = [g["cp_w_out"].reshape(N_DEV, d // N_DEV, d)]
        else:
            s = dict(g["small"])
            s.update(pool_w=s["pool_w"][None], conv_w=s["conv_w"][:CONV_WIDTH], gate_w=s["gate_w"][:GATE_RANK])
            own = [s[n] for n in _SMALL_GRADS[:len(_REPLICATED)]]
            own += [_column_split(s[n]).reshape((N_DEV,) + shape)
                    for n, shape in zip(_SMALL_GRADS[len(_REPLICATED):], self.small_shard_shapes)]
            plans = [_plan_to_all] * len(_REPLICATED) + [_plan_split_to_all] * len(_SMALL_SHARDED)
            lands = [lax.empty((N_DEV,) + a.shape, F32) for a in own[:len(_REPLICATED)]]
            lands += [lax.empty(a.shape, F32) for a in own[len(_REPLICATED):]]
            own.append(g["loss"])
            plans.append(_plan_to_all)
            lands.append(lax.empty((N_DEV,) + g["loss"].shape, F32))
            own.append(g["cp_w_in_t"].reshape(N_DEV, -1, d))
            plans.append(_plan_split_to_all)
            lands.append(lax.empty(own[-1].shape, BF16))
            self.sent[group], self.token = _exchange_start(plans, own, lands, after=self.token, name=f"start_g_{group}")
            return self.token
        self.sent[group], self.token = _exchange_start(_plan_split_to_all, arrs, [lax.empty(a.shape, a.dtype) for a in arrs],
                                                       after=self.token, name=f"start_g_{group}")
        return self.token

    def finish(self, w, mom, var):
        out = {}
        after = self.token

        def landed(group):
            own, got = _exchange_wait(self.sent[group], after, name=f"wait_g_{group}")
            return list(zip(own, got))

        def adam(n, parts, behind=None, transposed=False):
            by_row = transposed and w[n].shape[2] % 8 != 0
            if by_row:
                flip, back = (lambda a: jnp.transpose(a, (2, 0, 1))), (lambda a: jnp.transpose(a, (1, 2, 0)))
            else:
                flip = back = (lambda a: jnp.transpose(a, (0, 2, 1))) if transposed else (lambda a: a)
            res = _reduce_adam(parts, flip(w[n]), flip(mom[n]), flip(var[n]), by_row=by_row, after=behind, name=f"adam_{n}")
            out[n] = tuple(back(a) for a in res)
            return res[0]

        ffn1 = landed("ffn1")
        after = ffn1[0][1]
        gla = landed("gla")
        after = adam("gla_w_in", [gla[0]], transposed=True)
        after = adam("gla_w_out", [gla[1]], after)
        ffn0_w1 = landed("ffn0_w1")
        after = adam("ffn_w1", [ffn0_w1[0], ffn1[0]])
        ffn0_w2 = landed("ffn0_w2")
        after = adam("ffn_w2", [ffn0_w2[0], ffn1[1]])
        cp_out = landed("cp_out")
        after = adam("cp_w_out", [cp_out[0]])
        *small, loss, cp_w_in = landed("cp")
        names = _REPLICATED + _SMALL_SHARDED
        split = [False] * len(_REPLICATED) + [True] * len(_SMALL_SHARDED)
        small_new = _adam_small([own for own, _ in small], [got for _, got in small], split, [w[n] for n in names],
                                [mom[n] for n in names], [var[n] for n in names], name="adam_small")
        out.update(zip(names, small_new))
        out["loss"] = _sum8(*loss, name="sum_loss")[0, 0]
        adam("cp_w_in", [cp_w_in], small_new[0][0], transposed=True)
        return out


def kernel(x, meta_tokens, mix_norm_g, ffn_norm_g, ffn_w1, ffn_w2, cp_w_in, cp_conv_w, cp_conv_b, cp_ln_g, cp_ln_b, cp_pool_w, cp_pool_scale, cp_w_out, gla_w_in, gla_gate_w2, gla_gate_b, gla_head_g, gla_w_out, final_norm_g, loss_target, m_meta_tokens, m_mix_norm_g, m_ffn_norm_g, m_ffn_w1, m_ffn_w2, m_cp_w_in, m_cp_conv_w, m_cp_conv_b, m_cp_ln_g, m_cp_ln_b, m_cp_pool_w, m_cp_pool_scale, m_cp_w_out, m_gla_w_in, m_gla_gate_w2, m_gla_gate_b, m_gla_head_g, m_gla_w_out, m_final_norm_g, v_meta_tokens, v_mix_norm_g, v_ffn_norm_g, v_ffn_w1, v_ffn_w2, v_cp_w_in, v_cp_conv_w, v_cp_conv_b, v_cp_ln_g, v_cp_ln_b, v_cp_pool_w, v_cp_pool_scale, v_cp_w_out, v_gla_w_in, v_gla_gate_w2, v_gla_gate_b, v_gla_head_g, v_gla_w_out, v_final_norm_g):
    w = dict(meta_tokens=meta_tokens, mix_norm_g=mix_norm_g, ffn_norm_g=ffn_norm_g, ffn_w1=ffn_w1, ffn_w2=ffn_w2,
             cp_w_in=cp_w_in, cp_conv_w=cp_conv_w, cp_conv_b=cp_conv_b, cp_ln_g=cp_ln_g, cp_ln_b=cp_ln_b,
             cp_pool_w=cp_pool_w, cp_pool_scale=cp_pool_scale, cp_w_out=cp_w_out, gla_w_in=gla_w_in,
             gla_gate_w2=gla_gate_w2, gla_gate_b=gla_gate_b, gla_head_g=gla_head_g, gla_w_out=gla_w_out,
             final_norm_g=final_norm_g.reshape(1, -1))
    mom = dict(meta_tokens=m_meta_tokens, mix_norm_g=m_mix_norm_g, ffn_norm_g=m_ffn_norm_g, ffn_w1=m_ffn_w1, ffn_w2=m_ffn_w2,
               cp_w_in=m_cp_w_in, cp_conv_w=m_cp_conv_w, cp_conv_b=m_cp_conv_b, cp_ln_g=m_cp_ln_g, cp_ln_b=m_cp_ln_b,
               cp_pool_w=m_cp_pool_w, cp_pool_scale=m_cp_pool_scale, cp_w_out=m_cp_w_out, gla_w_in=m_gla_w_in,
               gla_gate_w2=m_gla_gate_w2, gla_gate_b=m_gla_gate_b, gla_head_g=m_gla_head_g, gla_w_out=m_gla_w_out,
               final_norm_g=m_final_norm_g.reshape(1, -1))
    var = dict(meta_tokens=v_meta_tokens, mix_norm_g=v_mix_norm_g, ffn_norm_g=v_ffn_norm_g, ffn_w1=v_ffn_w1, ffn_w2=v_ffn_w2,
               cp_w_in=v_cp_w_in, cp_conv_w=v_cp_conv_w, cp_conv_b=v_cp_conv_b, cp_ln_g=v_cp_ln_g, cp_ln_b=v_cp_ln_b,
               cp_pool_w=v_cp_pool_w, cp_pool_scale=v_cp_pool_scale, cp_w_out=v_cp_w_out, gla_w_in=v_gla_w_in,
               gla_gate_w2=v_gla_gate_w2, gla_gate_b=v_gla_gate_b, gla_head_g=v_gla_head_g, gla_w_out=v_gla_w_out,
               final_norm_g=v_final_norm_g.reshape(1, -1))
    d = x.shape[-1]
    replicated = dict(mix_g=w["mix_norm_g"], ffn_g=w["ffn_norm_g"], conv_b=w["cp_conv_b"], ln_g=w["cp_ln_g"],
                      ln_b=w["cp_ln_b"], pool_w=w["cp_pool_w"][0].astype(BF16), pool_scale=w["cp_pool_scale"],
                      final_g=w["final_norm_g"])
    exchanges = _Exchanges(w, d)
    _, grad_x, _ = _local_step(x[0], loss_target[0], replicated, exchanges)
    out = exchanges.finish(w, mom, var)
    loss = out.pop("loss")

    def leaf(n, k):
        a = out[n][k]
        return a.reshape(-1) if n == "final_norm_g" else a

    return (loss, grad_x[None], *[leaf(n, 0) for n in _NAMES], *[leaf(n, 1) for n in _NAMES],
            *[leaf(n, 2) for n in _NAMES], *[leaf(n, 3) for n in _NAMES])
```

```python
import functools

import jax
import jax.numpy as jnp
from jax import lax
from jax.experimental import pallas as pl
from jax.experimental.pallas import tpu as pltpu

F32, BF16 = jnp.float32, jnp.bfloat16
N_DEV = 8
CHUNK = 64
N_META = 16
PAD_ROWS = CHUNK - N_META
HALO = 32
EPS = 1e-5
CONV_WIDTH = 31
POOL_WINDOWS = (2, 4, 8, 16)
HEADS = 4
GATE_RANK = 16
GATE_NORM = 16.0
GATE_PAD = 128
ADAM_LR, ADAM_B1, ADAM_B2, ADAM_EPS, ADAM_WD, ADAM_STEP = 0.001, 0.9, 0.999, 1e-08, 0.01, 10
V7X_VMEM_LIMIT = 56 * 2 ** 20
LANE = 128


def _cparams(*sem):
    return pltpu.CompilerParams(dimension_semantics=sem, vmem_limit_bytes=V7X_VMEM_LIMIT)


def _row_tile(t, cap):
    best = CHUNK
    for r in range(CHUNK, min(t, cap) + 1, CHUNK):
        if t % r == 0:
            best = r
    return best


def _resident(shape):
    return pl.BlockSpec(shape, lambda *_: (0,) * len(shape), pipeline_mode=pl.Buffered(1))


def _dot(a, b):
    return jnp.dot(a, b, preferred_element_type=F32)


def _dot_nt(a, b):
    return lax.dot_general(a, b, (((1,), (1,)), ((), ())), preferred_element_type=F32)


def _dot_tn(a, b):
    return lax.dot_general(a, b, (((0,), (0,)), ((), ())), preferred_element_type=F32)


def _rowsum(a):
    return jnp.sum(a, axis=0, keepdims=True)


def _sigmoid(a):
    return 1.0 / (1.0 + jnp.exp(-a))


def _row_ids(tile, rt):
    return tile * rt + lax.broadcasted_iota(jnp.int32, (rt, 1), 0)


def _sds(shape, dtype):
    return jax.ShapeDtypeStruct(shape, dtype)


DW_ROWS = 1024


def _linear_bwd_w(x, dy, *, square_x=False, column_blocks=None, row_blocks=None, after=None, name):
    t, k = x.shape
    n = dy.shape[1]
    cut_k = k > n and column_blocks is None
    assert cut_k or row_blocks is None
    width = k if cut_k else n
    blk = n // column_blocks if column_blocks else max(c for c in (640, 512, 384, 256, LANE) if width % c == 0)
    dep_specs, deps = _dep_specs(after)

    def body(*refs):
        x_ref, dy_ref, o_ref, acc = refs[len(deps):]
        for c0 in range(0, t, DW_ROWS):
            rows = slice(c0, min(c0 + DW_ROWS, t))
            xv = x_ref[rows, :]
            if square_x:
                xv = xv.astype(F32)
                xv = xv * xv
            part = _dot_tn(xv.astype(BF16), dy_ref[rows, :].astype(BF16))
            if c0 == 0:
                acc[...] = part
            else:
                acc[...] += part
        if row_blocks is None:
            o_ref[...] = acc[...].astype(BF16)
            return
        nb, rpb = row_blocks
        for s in range(width // blk):
            @pl.when(pl.program_id(0) == s)
            def _(s=s):
                for b in range(nb):
                    lo, hi = max(s * blk, b * rpb), min((s + 1) * blk, (b + 1) * rpb)
                    if lo < hi:
                        o_ref[b, lo - b * rpb:hi - b * rpb, :] = acc[lo - s * blk:hi - s * blk, :].astype(BF16)

    out_shape = _sds((k, n), BF16)
    semantics = "parallel"
    if cut_k:
        in_specs = [pl.BlockSpec((t, blk), lambda j: (0, j)), _resident((t, n))]
        out_specs = pl.BlockSpec((blk, n), lambda j: (j, 0))
        acc_shape = (blk, n)
        if row_blocks:
            out_shape = _sds(row_blocks + (n,), BF16)
            out_specs = pl.BlockSpec(out_shape.shape, lambda j: (0, 0, 0))
            semantics = "arbitrary"
    else:
        in_specs = [_resident((t, k)), pl.BlockSpec((t, blk), lambda j: (0, j))]
        out_specs = pl.BlockSpec((k, blk), lambda j: (0, j))
        acc_shape = (k, blk)
        if column_blocks:
            out_specs = pl.BlockSpec((None, k, blk), lambda j: (j, 0, 0))
            out_shape = _sds((column_blocks, k, blk), BF16)
    return pl.pallas_call(
        body, grid=(width // blk,), in_specs=dep_specs + in_specs, out_specs=out_specs, out_shape=out_shape,
        scratch_shapes=[pltpu.VMEM(acc_shape, F32)], compiler_params=_cparams(semantics), name=name)(*deps, x, dy)


FFN_BLOCKS_PER_STEP = 2


def _ffn_fwd(h, gain, w1g, w2g, *, loss_head=None, name):
    t, d = h.shape
    f8 = w1g.shape[-1]
    rt = _row_tile(t, 832)
    nb = FFN_BLOCKS_PER_STEP
    nstep = N_DEV // nb

    def body(*refs):
        if loss_head is None:
            h_ref, g_ref, w1_ref, w2_ref, o_ref, u_ref, r_ref, acc_ref = refs
        else:
            (h_ref, g_ref, w1_ref, w2_ref, fg_ref, tgt_ref, o_ref, u_ref, r_ref, loss_ref, dfg_ref, acc_ref, t_ref,
             t_sem) = refs
        i, j = pl.program_id(0), pl.program_id(1)

        def target_rows(act):
            @pl.when(i == 0)
            def _():
                act(pltpu.make_async_copy(tgt_ref.at[pl.ds(0, rt - CHUNK)], t_ref.at[pl.ds(CHUNK, rt - CHUNK)], t_sem.at[0]))

            if t > rt:
                @pl.when(i > 0)
                def _():
                    act(pltpu.make_async_copy(tgt_ref.at[pl.ds(pl.multiple_of(i * rt - CHUNK, CHUNK), rt)], t_ref,
                                              t_sem.at[0]))

        @pl.when(j == 0)
        def _():
            if loss_head is not None:
                @pl.when(i == 0)
                def _():
                    t_ref[0:CHUNK, :] = jnp.zeros((CHUNK, d), F32)

                target_rows(lambda copy: copy.start())

            hv = h_ref[...]
            u_ref[...] = (hv * lax.rsqrt(jnp.mean(hv * hv, axis=-1, keepdims=True) + EPS) * g_ref[...]).astype(BF16)
            acc_ref[...] = jnp.zeros_like(acc_ref)

        part = None
        for b in range(nb):
            a = jnp.maximum(_dot(u_ref[...], w1_ref[b]), 0.0)
            r_ref[:, b * f8:(b + 1) * f8] = a.astype(BF16)
            term = _dot((a * a).astype(BF16), w2_ref[b])
            part = term if part is None else part + term
        acc_ref[...] += part

        @pl.when(j == nstep - 1)
        def _():
            y = h_ref[...] + acc_ref[...]
            if loss_head is None:
                o_ref[...] = y
                return

            @pl.when(i == 0)
            def _():
                loss_ref[...] = jnp.zeros_like(loss_ref)
                dfg_ref[...] = jnp.zeros_like(dfg_ref)

            target_rows(lambda copy: copy.wait())

            rstd = lax.rsqrt(jnp.mean(y * y, axis=-1, keepdims=True) + EPS)
            xh = y * rstd
            err = jnp.where(_row_ids(i, rt) >= CHUNK, xh * fg_ref[...] - t_ref[...], 0.0)
            loss_ref[...] += (0.5 / d) * jnp.sum(err * err)
            dy = err * (1.0 / d)
            dfg_ref[...] += _rowsum(dy * xh)
            dxh = dy * fg_ref[...]
            o_ref[...] = rstd * (dxh - xh * jnp.mean(dxh * xh, axis=-1, keepdims=True))

    rows = lambda i, j: (i, 0)
    in_specs = [pl.BlockSpec((rt, d), rows), _resident((1, d)),
                pl.BlockSpec((nb, d, f8), lambda i, j: (j, 0, 0)), pl.BlockSpec((nb, f8, d), lambda i, j: (j, 0, 0))]
    out_specs = [pl.BlockSpec((rt, d), rows), pl.BlockSpec((rt, d), rows), pl.BlockSpec((rt, nb * f8), lambda i, j: (i, j))]
    out_shape = [_sds((t, d), F32), _sds((t, d), BF16), _sds((t, N_DEV * f8), BF16)]
    args = [h, gain, w1g, w2g]
    scratch_shapes = [pltpu.VMEM((rt, d), F32)]
    if loss_head is not None:
        in_specs += [_resident((1, d)), pl.BlockSpec(memory_space=pl.ANY)]
        out_specs += [pl.BlockSpec((8, LANE), lambda i, j: (0, 0)), pl.BlockSpec((1, d), lambda i, j: (0, 0))]
        out_shape += [_sds((8, LANE), F32), _sds((1, d), F32)]
        args += list(loss_head)
        scratch_shapes += [pltpu.VMEM((rt, d), F32), pltpu.SemaphoreType.DMA((1,))]
    return pl.pallas_call(
        body, grid=(t // rt, nstep), in_specs=in_specs, out_specs=out_specs, out_shape=out_shape,
        scratch_shapes=scratch_shapes,
        compiler_params=_cparams("arbitrary" if loss_head is not None else "parallel", "arbitrary"), name=name)(*args)


def _ffn_bwd_x(h, dout, gain, r, w1g, w2g, *, after=None, name):
    t, d = h.shape
    f8 = w1g.shape[-1]
    rt = _row_tile(t, 832)
    nb = FFN_BLOCKS_PER_STEP
    last = N_DEV // nb - 1
    dep_specs, deps = _dep_specs(after)

    def body(*refs):
        h_ref, do_ref, g_ref, r_ref, w1_ref, w2_ref, dh_ref, dhh_ref, dob_ref, dg_ref, du_ref = refs[len(deps):]
        i, j = pl.program_id(0), pl.program_id(1)

        @pl.when(j == 0)
        def _():
            dob_ref[...] = do_ref[...].astype(BF16)
            du_ref[...] = jnp.zeros_like(du_ref)

        part = None
        for b in range(nb):
            cols = slice(b * f8, (b + 1) * f8)
            dhh = (_dot_nt(dob_ref[...], w2_ref[b]) * (2.0 * r_ref[:, cols].astype(F32))).astype(BF16)
            dhh_ref[:, cols] = dhh
            term = _dot_nt(dhh, w1_ref[b])
            part = term if part is None else part + term
        du_ref[...] += part

        @pl.when(j == last)
        def _():
            @pl.when(i == 0)
            def _():
                dg_ref[...] = jnp.zeros_like(dg_ref)

            hv = h_ref[...]
            rstd = lax.rsqrt(jnp.mean(hv * hv, axis=-1, keepdims=True) + EPS)
            xh = hv * rstd
            du = du_ref[...]
            dg_ref[...] += _rowsum(du * xh)
            dxh = du * g_ref[...]
            dh_ref[...] = do_ref[...] + rstd * (dxh - xh * jnp.mean(dxh * xh, axis=-1, keepdims=True))

    rows = lambda i, j: (i, 0)
    return pl.pallas_call(
        body, grid=(t // rt, N_DEV // nb),
        in_specs=dep_specs + [
                  pl.BlockSpec((rt, d), rows), pl.BlockSpec((rt, d), rows), _resident((1, d)),
                  pl.BlockSpec((rt, nb * f8), lambda i, j: (i, j)),
                  pl.BlockSpec((nb, d, f8), lambda i, j: (j, 0, 0)),
                  pl.BlockSpec((nb, f8, d), lambda i, j: (j, 0, 0))],
        out_specs=[pl.BlockSpec((rt, d), rows), pl.BlockSpec((rt, nb * f8), lambda i, j: (i, j)),
                   pl.BlockSpec((rt, d), rows), pl.BlockSpec((1, d), lambda i, j: (0, 0))],
        out_shape=[_sds((t, d), F32), _sds((t, N_DEV * f8), BF16), _sds((t, d), BF16), _sds((1, d), F32)],
        scratch_shapes=[pltpu.VMEM((rt, d), F32)],
        compiler_params=_cparams("arbitrary", "arbitrary"), name=name)(*deps, h, dout, gain, r, w1g, w2g)


def _lane_blocks(width):
    lb = min(LANE, width)
    return [slice(s, s + lb) for s in range(0, width, lb)]


def _conv_rows(src_ref, w_ref, offset, dst_ref, nblk, width, bias_ref=None):
    def blk(rb, carry):
        base = pl.multiple_of(rb * CHUNK, CHUNK)
        for l, ls in enumerate(_lane_blocks(width)):
            acc = jnp.zeros((CHUNK, ls.stop - ls.start), F32)
            if bias_ref is not None:
                acc = acc + bias_ref[:, ls]
            for k in range(CONV_WIDTH):
                acc = acc + w_ref[k:k + 1, ls] * src_ref[l, pl.ds(base + offset(k), CHUNK), :]
            dst_ref[l, pl.ds(base, CHUNK), :] = acc
        return carry

    lax.fori_loop(0, nblk, blk, 0)


def _to_lane_blocks(ref, row0, value):
    for l, ls in enumerate(_lane_blocks(value.shape[1])):
        ref[l, row0:row0 + value.shape[0], :] = value[:, ls]


def _from_lane_blocks(ref):
    return jnp.concatenate([ref[l] for l in range(ref.shape[0])], axis=1)


def _pool_counts(rows, window):
    return jnp.clip(rows - PAD_ROWS + 1, 1, window).astype(F32)


def _trailing_sum(v, window):
    s, sh = v, 1
    while sh < window:
        s = s + pltpu.roll(s, sh, 0)
        sh *= 2
    return s


def _leading_sum(v, window):
    s, sh, n = v, 1, v.shape[0]
    while sh < window:
        s = s + pltpu.roll(s, n - sh, 0)
        sh *= 2
    return s


def _norm_project(h_ref, g_ref, w_t_ref, u_ref, z_ref):
    hv = h_ref[...]
    u = (hv * lax.rsqrt(jnp.mean(hv * hv, axis=-1, keepdims=True) + EPS) * g_ref[...]).astype(BF16)
    u_ref[...] = u
    z_ref[...] = _dot_nt(u, w_t_ref[...])


def _project_back(dz_ref, w_t_ref, h_ref, g_ref, dres_ref, dh_ref, dg_ref, first):
    dx = _dot(dz_ref[...], w_t_ref[...])
    hv = h_ref[...]
    rstd = lax.rsqrt(jnp.mean(hv * hv, axis=-1, keepdims=True) + EPS)
    xh = hv * rstd

    @pl.when(first)
    def _():
        dg_ref[...] = jnp.zeros_like(dg_ref)

    dg_ref[...] += _rowsum(dx * xh)
    dxh = dx * g_ref[...]
    dh_ref[...] = dres_ref[...] + rstd * (dxh - xh * jnp.mean(dxh * xh, axis=-1, keepdims=True))


def _cp_mid_fwd(x, meta, gain, w_in_t, w_out, conv_w, conv_b, ln_g, ln_b, pool_w, pool_scale, *, name):
    seq, d = x.shape
    t = seq + CHUNK
    ein = w_in_t.shape[0]
    cd = conv_b.shape[1]
    pd = pool_scale.shape[1]
    pg = pd // len(POOL_WINDOWS)
    rt = _row_tile(t, 320)
    ntile = t // rt

    def body(x_ref, meta_ref, g_ref, wi_ref, wo_ref, cw_ref, cb_ref, lg_ref, lb_ref, pw_ref, ps_ref,
             ho_ref, o_ref, z_ref, u_ref, cv_ref, h0_ref, gext, pext, conv_s, hbuf, hsem):
        i = pl.program_id(0)
        slot = i % 2
        first_rows = pltpu.make_async_copy(x_ref.at[pl.ds(0, rt - CHUNK)], hbuf.at[0, pl.ds(CHUNK, rt - CHUNK)], hsem.at[0])

        def tile_rows(tile, to):
            return pltpu.make_async_copy(x_ref.at[pl.ds(pl.multiple_of(tile * rt - CHUNK, CHUNK), rt)], hbuf.at[to],
                                         hsem.at[to])

        @pl.when(i == 0)
        def _():
            first_rows.start()
            hbuf[0, 0:PAD_ROWS, :] = jnp.zeros((PAD_ROWS, d), F32)
            hbuf[0, PAD_ROWS:CHUNK, :] = meta_ref[...]
            _to_lane_blocks(gext, 0, jnp.zeros((HALO, cd), F32))
            pext[0:HALO, :] = jnp.zeros((HALO, pd), F32)

        @pl.when(i + 1 < ntile)
        def _():
            tile_rows(i + 1, 1 - slot).start()

        @pl.when(i == 0)
        def _():
            first_rows.wait()

        @pl.when(i > 0)
        def _():
            tile_rows(i, slot).wait()

        h_ref = hbuf.at[slot]
        h0_ref[...] = h_ref[...]
        _norm_project(h_ref, g_ref, wi_ref, u_ref, z_ref)

        _to_lane_blocks(gext, HALO, z_ref[:, 0:cd] * _sigmoid(z_ref[:, cd:2 * cd]))
        pext[HALO:HALO + rt, :] = z_ref[:, 2 * cd:]
        _conv_rows(gext, cw_ref, lambda k: k + HALO - (CONV_WIDTH - 1), conv_s, rt // CHUNK, cd, cb_ref)
        cv = _from_lane_blocks(conv_s)
        cv_ref[...] = cv
        xc = cv - jnp.mean(cv, axis=-1, keepdims=True)
        y = xc * lax.rsqrt(jnp.mean(xc * xc, axis=-1, keepdims=True) + EPS) * lg_ref[...] + lb_ref[...]
        rows = _row_ids(i, rt)
        a = jnp.where(rows >= PAD_ROWS, y * _sigmoid(y), 0.0)
        o_ref[:, 0:cd] = a.astype(BF16)
        for gi, window in enumerate(POOL_WINDOWS):
            ls = slice(gi * pg, (gi + 1) * pg)
            v = pext[:, ls]
            tm = _trailing_sum(v, window)[HALO:] / _pool_counts(rows, window) - v[HALO:]
            p = _dot(tm.astype(BF16), pw_ref[gi]) * ps_ref[:, ls]
            o_ref[:, cd + gi * pg:cd + (gi + 1) * pg] = p.astype(BF16)
        ho_ref[...] = h_ref[...] + _dot(o_ref[...], wo_ref[...])
        gext[:, 0:HALO, :] = gext[:, rt:rt + HALO, :]
        pext[0:HALO, :] = pext[rt:rt + HALO, :]

    nl, lb = len(_lane_blocks(cd)), min(LANE, cd)
    rows = lambda i: (i, 0)
    return pl.pallas_call(
        body, grid=(ntile,),
        in_specs=[pl.BlockSpec(memory_space=pl.ANY), _resident(meta.shape), _resident((1, d)), _resident(w_in_t.shape),
                  _resident(w_out.shape), _resident(conv_w.shape), _resident((1, cd)),
                  _resident((1, cd)), _resident((1, cd)), _resident(pool_w.shape), _resident((1, pd))],
        out_specs=[pl.BlockSpec((rt, d), rows), pl.BlockSpec((rt, cd + pd), rows), pl.BlockSpec((rt, ein), rows),
                   pl.BlockSpec((rt, d), rows), pl.BlockSpec((rt, cd), rows), pl.BlockSpec((rt, d), rows)],
        out_shape=[_sds((t, d), F32), _sds((t, cd + pd), BF16), _sds((t, ein), F32), _sds((t, d), BF16),
                   _sds((t, cd), F32), _sds((t, d), F32)],
        scratch_shapes=[pltpu.VMEM((nl, rt + HALO, lb), F32), pltpu.VMEM((rt + HALO, pd), F32),
                        pltpu.VMEM((nl, rt, lb), F32), pltpu.VMEM((2, rt, d), F32), pltpu.SemaphoreType.DMA((2,))],
        compiler_params=_cparams("arbitrary"), name=name)(x, meta, gain, w_in_t, w_out, conv_w, conv_b, ln_g, ln_b, pool_w,
                                                          pool_scale)


def _cp_mid_bwd(z, cv, h, gain, w_in_t, dh, w_out, conv_w, conv_b, ln_g, ln_b, pool_w, pool_scale, *, after=None, name):
    t, ein = z.shape
    cd = conv_b.shape[1]
    pd = pool_scale.shape[1]
    pg = pd // len(POOL_WINDOWS)
    rt = _row_tile(t, 320)
    ntile = t // rt
    per = rt // CHUNK
    dep_specs, deps = _dep_specs(after)

    def body(*refs):
        (z_ref, zh_ref, cv_ref, h_ref, g_ref, wi_ref, dh_ref, wo_ref, cw_ref, cb_ref, lg_ref, lb_ref, pw_ref, ps_ref,
         dx_ref, dfirst_ref, dg_ref, dz_ref, dcw_ref, dcb_ref, dlg_ref, dlb_ref, dpw_ref, dps_ref,
         gext, pext, conv_s, dcv, dsp, dhi, dx_sem) = refs[len(deps):]
        step = pl.program_id(0)
        tile = ntile - 1 - step
        slot = step % 2
        last_slot = (ntile - 1) % 2
        first_rows = pltpu.make_async_copy(dhi.at[last_slot, pl.ds(CHUNK, rt - CHUNK)], dx_ref.at[pl.ds(0, rt - CHUNK)],
                                           dx_sem.at[last_slot])

        def tile_rows(tile, slot):
            return pltpu.make_async_copy(dhi.at[slot], dx_ref.at[pl.ds(pl.multiple_of(tile * rt - CHUNK, CHUNK), rt)],
                                         dx_sem.at[slot])

        dcat = _dot_nt(dh_ref[...].astype(BF16), wo_ref[...])

        @pl.when(step >= 2)
        def _():
            tile_rows(tile + 2, slot).wait()

        @pl.when(step == 0)
        def _():
            for ref in (dcw_ref, dcb_ref, dlg_ref, dlb_ref, dpw_ref, dps_ref):
                ref[...] = jnp.zeros_like(ref)
            _to_lane_blocks(dcv, rt, jnp.zeros((HALO, cd), F32))
            dsp[rt:rt + HALO, :] = jnp.zeros((HALO, pd), F32)

        keep = jnp.where(tile > 0, 1.0, 0.0)
        zh = zh_ref[CHUNK - HALO:CHUNK, :]
        _to_lane_blocks(gext, 0, keep * zh[:, 0:cd] * _sigmoid(zh[:, cd:2 * cd]))
        pext[0:HALO, :] = keep * zh[:, 2 * cd:]
        za = z_ref[:, 0:cd]
        sg = _sigmoid(z_ref[:, cd:2 * cd])
        _to_lane_blocks(gext, HALO, za * sg)
        pext[HALO:HALO + rt, :] = z_ref[:, 2 * cd:]
        cv = cv_ref[...]
        xc = cv - jnp.mean(cv, axis=-1, keepdims=True)
        rstd = lax.rsqrt(jnp.mean(xc * xc, axis=-1, keepdims=True) + EPS)
        xh = xc * rstd
        y = xh * lg_ref[...] + lb_ref[...]
        sy = _sigmoid(y)
        rows = _row_ids(tile, rt)
        da = jnp.where(rows >= PAD_ROWS, dcat[:, 0:cd], 0.0)
        dy = da * (sy * (1.0 + y * (1.0 - sy)))
        dlg_ref[...] += _rowsum(dy * xh)
        dlb_ref[...] += _rowsum(dy)
        dxh = dy * lg_ref[...]
        dconv = rstd * (dxh - jnp.mean(dxh, axis=-1, keepdims=True) - xh * jnp.mean(dxh * xh, axis=-1, keepdims=True))
        dcb_ref[...] += _rowsum(dconv)
        _to_lane_blocks(dcv, 0, dconv)
        for l, ls in enumerate(_lane_blocks(cd)):
            def acc_rows(rb, accs, l=l):
                base = pl.multiple_of(rb * CHUNK, CHUNK)
                d_blk = dcv[l, pl.ds(base, CHUNK), :]
                out = []
                for k in range(CONV_WIDTH):
                    prod = d_blk * gext[l, pl.ds(base + k + HALO - (CONV_WIDTH - 1), CHUNK), :]
                    part = prod[0:8]
                    for s in range(8, CHUNK, 8):
                        part = part + prod[s:s + 8]
                    out.append(accs[k] + part)
                return tuple(out)

            zero = jnp.zeros((8, ls.stop - ls.start), F32)
            accs = lax.fori_loop(0, per, acc_rows, (zero,) * CONV_WIDTH)
            for k in range(CONV_WIDTH):
                dcw_ref[k:k + 1, ls] += _rowsum(accs[k])
        _conv_rows(dcv, cw_ref, lambda k: CONV_WIDTH - 1 - k, conv_s, per, cd)
        dglu = _from_lane_blocks(conv_s)
        dz_ref[:, 0:cd] = (dglu * sg).astype(BF16)
        dz_ref[:, cd:2 * cd] = (dglu * za * sg * (1.0 - sg)).astype(BF16)
        dcv[:, rt:rt + HALO, :] = dcv[:, 0:HALO, :]
        for gi, window in enumerate(POOL_WINDOWS):
            ls = slice(gi * pg, (gi + 1) * pg)
            v = pext[:, ls]
            cnt = _pool_counts(rows, window)
            tm = (_trailing_sum(v, window)[HALO:] / cnt - v[HALO:]).astype(BF16)
            dp = dcat[:, cd + gi * pg:cd + (gi + 1) * pg]
            dps_ref[:, ls] += _rowsum(dp * _dot(tm, pw_ref[gi]))
            dpl = (dp * ps_ref[:, ls]).astype(BF16)
            dpw_ref[gi] += _dot_tn(tm, dpl)
            dtm = _dot_nt(dpl, pw_ref[gi])
            dsp[0:rt, ls] = dtm / cnt
            dpin = _leading_sum(dsp[:, ls], window)[0:rt] - dtm
            dz_ref[:, 2 * cd + gi * pg:2 * cd + (gi + 1) * pg] = dpin.astype(BF16)
        dsp[rt:rt + HALO, :] = dsp[0:HALO, :]

        _project_back(dz_ref, wi_ref, h_ref, g_ref, dh_ref, dhi.at[slot], dg_ref, step == 0)

        @pl.when(tile > 0)
        def _():
            tile_rows(tile, slot).start()

        @pl.when(tile == 0)
        def _():
            first_rows.start()
            dfirst_ref[...] = dhi[last_slot, 0:CHUNK, :]
            if ntile > 1:
                tile_rows(1, 1 - last_slot).wait()
            first_rows.wait()

    d = h.shape[1]
    back = lambda i: (ntile - 1 - i, 0)
    halo_idx = lambda i: (jnp.maximum((ntile - 1 - i) * per - 1, 0), 0)
    const2 = lambda i: (0, 0)
    nl, lb = len(_lane_blocks(cd)), min(LANE, cd)
    return pl.pallas_call(
        body, grid=(ntile,),
        in_specs=dep_specs + [
                  pl.BlockSpec((rt, ein), back), pl.BlockSpec((CHUNK, ein), halo_idx), pl.BlockSpec((rt, cd), back),
                  pl.BlockSpec((rt, d), back),
                  _resident((1, d)), _resident(w_in_t.shape), pl.BlockSpec((rt, d), back), _resident(w_out.shape),
                  _resident(conv_w.shape), _resident((1, cd)), _resident((1, cd)), _resident((1, cd)),
                  _resident(pool_w.shape), _resident((1, pd))],
        out_specs=[pl.BlockSpec(memory_space=pl.ANY), pl.BlockSpec((CHUNK, d), const2), pl.BlockSpec((1, d), const2),
                   pl.BlockSpec((rt, ein), back), pl.BlockSpec(conv_w.shape, const2), pl.BlockSpec((1, cd), const2),
                   pl.BlockSpec((1, cd), const2), pl.BlockSpec((1, cd), const2),
                   pl.BlockSpec(pool_w.shape, lambda i: (0, 0, 0)), pl.BlockSpec((1, pd), const2)],
        out_shape=[_sds((t - CHUNK, d), F32), _sds((CHUNK, d), F32), _sds((1, d), F32),
                   _sds((t, ein), BF16), _sds(conv_w.shape, F32), _sds((1, cd), F32), _sds((1, cd), F32),
                   _sds((1, cd), F32), _sds(pool_w.shape, F32), _sds((1, pd), F32)],
        scratch_shapes=[pltpu.VMEM((nl, rt + HALO, lb), F32), pltpu.VMEM((rt + HALO, pd), F32), pltpu.VMEM((nl, rt, lb), F32),
                        pltpu.VMEM((nl, rt + HALO, lb), F32), pltpu.VMEM((rt + HALO, pd), F32), pltpu.VMEM((2, rt, d), F32),
                        pltpu.SemaphoreType.DMA((2,))],
        compiler_params=_cparams("arbitrary"), name=name)(*deps, z, z, cv, h, gain, w_in_t, dh, w_out, conv_w, conv_b, ln_g,
                                                          ln_b, pool_w, pool_scale)


def _log_decay(r, gw_ref, gb_ref, rows):
    gp = _dot(r.astype(BF16), gw_ref[...]) + gb_ref[...]
    log_sig = jnp.minimum(gp, 0.0) - jnp.log(1.0 + jnp.exp(-jnp.abs(gp)))
    return gp, jnp.where(rows >= PAD_ROWS, log_sig / GATE_NORM, 0.0)


def _tri(strict):
    r = lax.broadcasted_iota(jnp.int32, (CHUNK, CHUNK), 0)
    c = lax.broadcasted_iota(jnp.int32, (CHUNK, CHUNK), 1)
    return jnp.where(c < r if strict else c <= r, 1.0, 0.0).astype(BF16)


def _tri_dot(tri, a):
    hi = a.astype(BF16)
    rest = a - hi.astype(F32)
    mid = rest.astype(BF16)
    lo = (rest - mid.astype(F32)).astype(BF16)
    return _dot(tri, hi) + _dot(tri, mid) + _dot(tri, lo)


def _gla_mid_fwd(h, gain, w_in_blocks, w_out, gate_w, gate_b, head_g, *, name):
    t = h.shape[0]
    nblk, rpb = w_in_blocks.shape[:2]
    dk = gate_b.shape[1]
    hv = head_g.shape[1]
    hk = dk // HEADS
    dv = hv * HEADS
    r_at = 2 * dk + 2 * dv
    assert nblk * rpb == r_at + GATE_RANK
    zw = r_at + GATE_PAD
    rt = _row_tile(t, 320)
    per = rt // CHUNK
    scale = hk ** -0.5

    def body(h_ref, g_ref, wb_ref, wo_ref, gw_ref, gb_ref, hg_ref, ho_ref, o_ref, st_ref, z_ref, u_ref, wi_ref, oraw_ref,
             s_ref, la_ref, dec_ref):
        i = pl.program_id(0)

        @pl.when(i == 0)
        def _():
            s_ref[...] = jnp.zeros_like(s_ref)
            for b in range(nblk):
                wi_ref[b * rpb:(b + 1) * rpb, :] = wb_ref[b]
            wi_ref[nblk * rpb:, :] = jnp.zeros((zw - nblk * rpb, wi_ref.shape[1]), BF16)

        _norm_project(h_ref, g_ref, wi_ref, u_ref, z_ref)

        _, la = _log_decay(z_ref[:, r_at:r_at + GATE_PAD], gw_ref, gb_ref, _row_ids(i, rt))
        la_ref[...] = la
        tri = _tri(False)

        def chunk_rows(c):
            return slice(c * CHUNK, (c + 1) * CHUNK)

        def decays(c, carry):
            rows = chunk_rows(c)
            la_c = la_ref[rows, :]
            cum = _tri_dot(tri, la_c)
            dec_ref[rows, :] = jnp.exp(_rowsum(la_c) - cum)
            return carry

        def states(c, carry):
            rows = chunk_rows(c)
            etot = jnp.exp(_rowsum(la_ref[rows, :]))
            for hd in range(HEADS):
                ks = slice(hd * hk, (hd + 1) * hk)
                kd = z_ref[rows, dk + hd * hk:dk + (hd + 1) * hk] * dec_ref[rows, ks]
                v = z_ref[rows, 2 * dk + hd * hv:2 * dk + (hd + 1) * hv]
                s_new = s_ref[hd] * etot[:, ks] + _dot_tn(v.astype(BF16), kd.astype(BF16))
                s_ref[hd] = s_new
                st_ref[c, hd] = s_new
            return carry

        def outputs(c, carry):
            rows = chunk_rows(c)
            for hd in range(HEADS):
                q = z_ref[rows, hd * hk:(hd + 1) * hk] * scale
                g = z_ref[rows, 2 * dk + dv + hd * hv:2 * dk + dv + (hd + 1) * hv]
                o = _dot_nt(q.astype(BF16), st_ref[c, hd].astype(BF16))
                oraw_ref[rows, hd * hv:(hd + 1) * hv] = o
                on = o * lax.rsqrt(jnp.mean(o * o, axis=-1, keepdims=True) + EPS) * hg_ref[...]
                o_ref[rows, hd * hv:(hd + 1) * hv] = (on * (g * _sigmoid(g))).astype(BF16)
            return carry

        for phase in (decays, states, outputs):
            for c in range(per):
                phase(c, 0)
        ho_ref[...] = h_ref[...] + _dot(o_ref[...], wo_ref[...])

    d = h.shape[1]
    rows = lambda i: (i, 0)
    return pl.pallas_call(
        body, grid=(t // rt,),
        in_specs=[pl.BlockSpec((rt, d), rows), _resident((1, d)), _resident(w_in_blocks.shape), _resident(w_out.shape),
                  _resident(gate_w.shape), _resident((1, dk)), _resident((1, hv))],
        out_specs=[pl.BlockSpec((rt, d), rows), pl.BlockSpec((rt, dv), rows),
                   pl.BlockSpec((per, HEADS, hv, hk), lambda i: (i, 0, 0, 0)), pl.BlockSpec((rt, zw), rows),
                   pl.BlockSpec((rt, d), rows), pl.BlockSpec((zw, d), lambda i: (0, 0)), pl.BlockSpec((rt, dv), rows)],
        out_shape=[_sds((t, d), F32), _sds((t, dv), BF16), _sds((t // CHUNK, HEADS, hv, hk), F32), _sds((t, zw), F32),
                   _sds((t, d), BF16), _sds((zw, d), BF16), _sds((t, dv), F32)],
        scratch_shapes=[pltpu.VMEM((HEADS, hv, hk), F32), pltpu.VMEM((rt, dk), F32), pltpu.VMEM((rt, dk), F32)],
        compiler_params=_cparams("arbitrary"), name=name)(h, gain, w_in_blocks, w_out, gate_w, gate_b, head_g)


def _gla_mid_bwd(z, o, h, gain, w_in_t, dh, w_out, states, gate_w, gate_b, head_g, *, after=None, name):
    t = z.shape[0]
    dk = gate_b.shape[1]
    hv = head_g.shape[1]
    hk = dk // HEADS
    dv = hv * HEADS
    r_at = 2 * dk + 2 * dv
    rt = _row_tile(t, 320)
    ntile = t // rt
    per = rt // CHUNK
    scale = hk ** -0.5
    dep_specs, deps = _dep_specs(after)

    def body(*refs):
        (z_ref, o_ref, h_ref, g_ref, wi_ref, dh_ref, wo_ref, st_ref, stp_ref, gw_ref, gb_ref, hg_ref,
         dhi_ref, dg_ref, dz_ref, dgw_ref, dgb_ref, dhg_ref,
         ds_ref, la_ref, dla_ref, dec_ref, dos_ref, e_ref, do_ref) = refs[len(deps):]
        step = pl.program_id(0)
        tile = ntile - 1 - step
        do_ref[...] = _dot_nt(dh_ref[...].astype(BF16), wo_ref[...])

        @pl.when(step == 0)
        def _():
            ds_ref[...] = jnp.zeros_like(ds_ref)
            dgw_ref[...] = jnp.zeros_like(dgw_ref)
            dgb_ref[...] = jnp.zeros_like(dgb_ref)
            dhg_ref[...] = jnp.zeros_like(dhg_ref)

        rows_id = _row_ids(tile, rt)
        r = z_ref[:, r_at:r_at + GATE_PAD]
        gp, la = _log_decay(r, gw_ref, gb_ref, rows_id)
        la_ref[...] = la
        tri, tri_strict = _tri(False), _tri(True)
        keep = jnp.where(tile > 0, 1.0, 0.0)

        def chunk_rows(c):
            return slice(c * CHUNK, (c + 1) * CHUNK)

        def recompute(c, dhg):
            rows = chunk_rows(c)
            la_c = la_ref[rows, :]
            cum = _tri_dot(tri, la_c)
            dec_ref[rows, :] = jnp.exp(_rowsum(la_c) - cum)
            for hd in range(HEADS):
                q = (z_ref[rows, hd * hk:(hd + 1) * hk] * scale).astype(BF16)
                g = z_ref[rows, 2 * dk + dv + hd * hv:2 * dk + dv + (hd + 1) * hv]
                s_b = st_ref[c, hd].astype(BF16)
                o = o_ref[rows, hd * hv:(hd + 1) * hv]
                rstd = lax.rsqrt(jnp.mean(o * o, axis=-1, keepdims=True) + EPS)
                oh = o * rstd
                sg = _sigmoid(g)
                d_og = do_ref[rows, hd * hv:(hd + 1) * hv]
                dz_ref[rows, 2 * dk + dv + hd * hv:2 * dk + dv + (hd + 1) * hv] = (
                    d_og * oh * hg_ref[...] * (sg * (1.0 + g * (1.0 - sg)))).astype(BF16)
                don = d_og * (g * sg)
                dhg = dhg + _rowsum(don * oh)
                doh = don * hg_ref[...]
                d_o = (rstd * (doh - oh * jnp.mean(doh * oh, axis=-1, keepdims=True))).astype(BF16)
                dos_ref[rows, hd * hv:(hd + 1) * hv] = d_o
                dz_ref[rows, hd * hk:(hd + 1) * hk] = (_dot(d_o, s_b) * scale).astype(BF16)
            return dhg

        def recurrence(cc, carry):
            c = per - 1 - cc
            rows = chunk_rows(c)
            etot = jnp.exp(_rowsum(la_ref[rows, :]))
            for hd in range(HEADS):
                ks = slice(hd * hk, (hd + 1) * hk)
                q = (z_ref[rows, hd * hk:(hd + 1) * hk] * scale).astype(BF16)
                dec = dec_ref[rows, ks]
                kd = z_ref[rows, dk + hd * hk:dk + (hd + 1) * hk] * dec
                v = z_ref[rows, 2 * dk + hd * hv:2 * dk + (hd + 1) * hv].astype(BF16)
                s_prev = st_ref[c - 1, hd] if c > 0 else keep * stp_ref[0, hd]
                ds_t = ds_ref[hd] + _dot_tn(dos_ref[rows, hd * hv:(hd + 1) * hv], q)
                ds_b = ds_t.astype(BF16)
                dkd = _dot(v, ds_b)
                dz_ref[rows, 2 * dk + hd * hv:2 * dk + (hd + 1) * hv] = _dot_nt(kd.astype(BF16), ds_b).astype(BF16)
                dtot = etot[:, ks] * _rowsum(ds_t * s_prev)
                ds_ref[hd] = ds_t * etot[:, ks]
                dz_ref[rows, dk + hd * hk:dk + (hd + 1) * hk] = (dkd * dec).astype(BF16)
                e_ref[rows, ks] = dkd * kd
                dla_ref[rows, ks] = jnp.broadcast_to(dtot, (CHUNK, hk))
            return carry

        def decay_cotangent(c, carry):
            rows = chunk_rows(c)
            dla_ref[rows, :] += _tri_dot(tri_strict, e_ref[rows, :])
            return carry

        dhg = jnp.zeros((1, hv), F32)
        for c in range(per):
            dhg = recompute(c, dhg)
        dhg_ref[...] += dhg
        for phase in (recurrence, decay_cotangent):
            for c in range(per):
                phase(c, 0)
        dla = jnp.where(rows_id >= PAD_ROWS, dla_ref[...], 0.0)
        dgp = dla * (1.0 / GATE_NORM) * (1.0 - _sigmoid(gp))
        dgb_ref[...] += _rowsum(dgp)
        dgp_b = dgp.astype(BF16)
        dgw_ref[...] += _dot_tn(r.astype(BF16), dgp_b)
        dz_ref[:, r_at:r_at + GATE_PAD] = _dot_nt(dgp_b, gw_ref[...]).astype(BF16)
        _project_back(dz_ref, wi_ref, h_ref, g_ref, dh_ref, dhi_ref, dg_ref, step == 0)

    d = h.shape[1]
    back = lambda i: (ntile - 1 - i, 0)
    const2 = lambda i: (0, 0)
    return pl.pallas_call(
        body, grid=(ntile,),
        in_specs=dep_specs + [
                  pl.BlockSpec((rt, z.shape[1]), back), pl.BlockSpec((rt, dv), back), pl.BlockSpec((rt, d), back),
                  _resident((1, d)),
                  _resident(w_in_t.shape), pl.BlockSpec((rt, d), back), _resident(w_out.shape),
                  pl.BlockSpec((per, HEADS, hv, hk), lambda i: (ntile - 1 - i, 0, 0, 0)),
                  pl.BlockSpec((1, HEADS, hv, hk), lambda i: (jnp.maximum((ntile - 1 - i) * per - 1, 0), 0, 0, 0)),
                  _resident(gate_w.shape), _resident((1, dk)), _resident((1, hv))],
        out_specs=[pl.BlockSpec((rt, d), back), pl.BlockSpec((1, d), const2), pl.BlockSpec((rt, z.shape[1]), back),
                   pl.BlockSpec(gate_w.shape, const2), pl.BlockSpec((1, dk), const2), pl.BlockSpec((1, hv), const2)],
        out_shape=[_sds((t, d), F32), _sds((1, d), F32), _sds(z.shape, BF16), _sds(gate_w.shape, F32),
                   _sds((1, dk), F32), _sds((1, hv), F32)],
        scratch_shapes=[pltpu.VMEM((HEADS, hv, hk), F32), pltpu.VMEM((rt, dk), F32), pltpu.VMEM((rt, dk), F32),
                        pltpu.VMEM((rt, dk), F32), pltpu.VMEM((rt, dv), BF16), pltpu.VMEM((rt, dk), F32),
                        pltpu.VMEM((rt, dv), F32)],
        compiler_params=_cparams("arbitrary"), name=name)(*deps, z, o, h, gain, w_in_t, dh, w_out, states, states, gate_w,
                                                          gate_b, head_g)


def _adamw_math(w, g, m, v):
    m = ADAM_B1 * m + (1.0 - ADAM_B1) * g
    v = ADAM_B2 * v + (1.0 - ADAM_B2) * (g * g)
    m_hat = m / (1.0 - ADAM_B1 ** ADAM_STEP)
    v_hat = v / (1.0 - ADAM_B2 ** ADAM_STEP)
    return -ADAM_LR * (m_hat / (jnp.sqrt(v_hat) + ADAM_EPS) + ADAM_WD * w), m, v


N_CHIP = N_DEV // 2
BLOCK_ELEMS = 128 * 1024


def _my_slot():
    return 4 * lax.axis_index("x") + 2 * lax.axis_index("y") + lax.axis_index("c")


def _row_block(r, c):
    cap = max(8, BLOCK_ELEMS // (-(-c // LANE) * LANE))
    return max([b for b in range(8, r + 1, 8) if r % b == 0 and b <= cap] or [r])


def _blocks(r, c):
    rb = _row_block(r, c)
    if rb < r or r * c <= BLOCK_ELEMS:
        return rb, c
    return r, max([b for b in (512, 256, LANE) if c % b == 0 and r * b <= BLOCK_ELEMS] or [c])


def _reduce_adam(parts, w, m, v, *, by_row=False, after=None, name):
    nl, r, c = (1, w.shape[0], w.shape[2]) if by_row else w.shape
    rb, cb = _blocks(r, c)
    dep_specs, deps = _dep_specs(after)
    whole = (slice(None), 0, slice(None)) if by_row else Ellipsis

    def body(*refs):
        me = refs[0][0]
        refs = refs[1 + len(deps):]
        p_refs = refs[:2 * nl]
        w_ref, m_ref, v_ref, g_out, d_out, m_out, v_out = refs[2 * nl:]
        layer = pl.program_id(0)
        for li in range(nl):
            @pl.when(layer == li)
            def _(li=li):
                own_ref, land_ref = p_refs[2 * li], p_refs[2 * li + 1]
                mine = own_ref[...].astype(F32)
                g = None
                for dev in range(N_DEV):
                    term = jnp.where(me == dev, mine, land_ref[dev].astype(F32))
                    g = term if g is None else g + term
                g_out[whole] = g
                d_out[whole], m_out[whole], v_out[whole] = _adamw_math(w_ref[whole], g, m_ref[whole], v_ref[whole])

    if by_row:
        blk = pl.BlockSpec((rb, 1, cb), lambda l, i, j, me: (i, 0, j))
    else:
        blk = pl.BlockSpec((None, rb, cb), lambda l, i, j, me: (l, i, j))
    p_specs = []
    for li in range(nl):
        p_specs += [
            pl.BlockSpec((None, rb, cb), lambda l, i, j, me, li=li: (me[0], jnp.where(l == li, i, 0), jnp.where(l == li, j, 0))),
            pl.BlockSpec((N_DEV, rb, cb), lambda l, i, j, me, li=li: (0, jnp.where(l == li, i, 0), jnp.where(l == li, j, 0)))]
    flat = [p for pair in parts for p in pair]
    grid_spec = pltpu.PrefetchScalarGridSpec(
        num_scalar_prefetch=1, grid=(nl, r // rb, c // cb), in_specs=dep_specs + p_specs + [blk, blk, blk],
        out_specs=[blk] * 4)
    return pl.pallas_call(
        body, grid_spec=grid_spec, out_shape=[_sds(w.shape, F32)] * 4,
        compiler_params=_cparams("arbitrary", "arbitrary", "arbitrary"), name=name)(
        _my_slot().reshape(1), *deps, *flat, w, m, v)


def _sum8(own, landed, *, name):
    def body(own_ref, land_ref, o_ref):
        me = _my_slot()
        total = None
        for dev in range(N_DEV):
            term = jnp.where(me == dev, own_ref[...], land_ref[dev])
            total = term if total is None else total + term
        o_ref[...] = total

    return pl.pallas_call(body, out_shape=_sds(own.shape, F32), name=name)(own, landed)


def _adam_small(own, landed, split, w, m, v, *, name):
    n = len(w)

    def body(*refs):
        own_refs, land_refs, w_refs, m_refs, v_refs = (refs[k * n:(k + 1) * n] for k in range(5))
        outs = refs[5 * n:]
        me = _my_slot()
        for k in range(n):
            mine = own_refs[k][me] if split[k] else own_refs[k][...]
            g = None
            for dev in range(N_DEV):
                term = jnp.where(me == dev, mine, land_refs[k][dev])
                g = term if g is None else g + term
            outs[4 * k][...] = g
            outs[4 * k + 1][...], outs[4 * k + 2][...], outs[4 * k + 3][...] = _adamw_math(
                w_refs[k][...], g, m_refs[k][...], v_refs[k][...])

    out = pl.pallas_call(body, out_shape=[_sds(a.shape, F32) for a in w for _ in range(4)],
                         compiler_params=pltpu.CompilerParams(vmem_limit_bytes=V7X_VMEM_LIMIT), name=name)(
        *own, *landed, *w, *m, *v)
    return [tuple(out[4 * k:4 * k + 4]) for k in range(n)]


_HBM = pl.BlockSpec(memory_space=pltpu.HBM)
_SEM = pl.BlockSpec(memory_space=pltpu.SEMAPHORE)
_DATAFLOW = pltpu.SideEffectType.DATAFLOW_SIDE_EFFECTING


def _plan_to_all(src, land):
    x, y, c = lax.axis_index("x"), lax.axis_index("y"), lax.axis_index("c")
    return [(src, land.at[_my_slot()], (x ^ ((d >> 2) & 1), y ^ ((d >> 1) & 1), c ^ (d & 1))) for d in range(1, N_DEV)]


def _plan_split_to_all(src, land):
    x, y, c = lax.axis_index("x"), lax.axis_index("y"), lax.axis_index("c")
    peers = [(x ^ ((d >> 2) & 1), y ^ ((d >> 1) & 1), c ^ (d & 1)) for d in range(1, N_DEV)]
    return [(src.at[4 * px + 2 * py + pc], land.at[_my_slot()], (px, py, pc)) for px, py, pc in peers]


_PLAN_COPIES = {_plan_to_all: N_DEV - 1, _plan_split_to_all: N_DEV - 1}


def _plans(plan, n):
    return list(plan) if isinstance(plan, (list, tuple)) else [plan] * n


def _exchange_copies(plan, ins, lands, send, recv):
    copies, sem = [], 0
    for p, src, land in zip(_plans(plan, len(lands)), ins, lands):
        for s, dst, dev in p(src, land):
            copies.append(pltpu.make_async_remote_copy(
                src_ref=s, dst_ref=dst, send_sem=send.at[sem], recv_sem=recv.at[sem],
                device_id=dev, device_id_type=pl.DeviceIdType.MESH))
            sem += 1
    return copies


def _place_own(srcs, *, after=None, name):
    dep_specs, deps = _dep_specs(after)
    n = len(srcs)
    arrays, in_specs, shapes = [], [], []
    for a, dtype in srcs:
        if isinstance(a, tuple):
            a, layer = a
            in_specs.append(pl.BlockSpec((None,) + a.shape[1:], lambda i, layer=layer: (layer, 0, 0)))
            shapes.append(a.shape[1:])
        else:
            in_specs.append(pl.BlockSpec(a.shape, lambda i: (0, 0)))
            shapes.append(a.shape)
        arrays.append(a)
    dtypes = [dtype for _, dtype in srcs]

    def body(*refs):
        refs = refs[len(deps):]
        a_refs, o_refs, cast_refs, sem = refs[:n], refs[n:2 * n], refs[2 * n:3 * n], refs[3 * n]
        me = _my_slot()
        copies = []
        for k in range(n):
            cast_refs[k][...] = a_refs[k][...].astype(dtypes[k])
            copies.append(pltpu.make_async_copy(cast_refs[k], o_refs[k].at[me], sem.at[k]))
            copies[-1].start()
        for cp in copies:
            cp.wait()

    return pl.pallas_call(
        body, grid=(1,), in_specs=dep_specs + in_specs, out_specs=[pl.BlockSpec(memory_space=pl.ANY)] * n,
        out_shape=[_sds((N_DEV,) + shape, dtype) for shape, dtype in zip(shapes, dtypes)],
        scratch_shapes=[pltpu.VMEM(shape, dtype) for shape, dtype in zip(shapes, dtypes)] + [pltpu.SemaphoreType.DMA((n,))],
        compiler_params=pltpu.CompilerParams(vmem_limit_bytes=V7X_VMEM_LIMIT), name=name)(*deps, *arrays)


def _plan_gather_first(land, _):
    x, y, c = lax.axis_index("x"), lax.axis_index("y"), lax.axis_index("c")
    mine = land.at[_my_slot()]
    return [(mine, mine, (x, y, 1 - c))] + [(mine, mine, (x ^ (d >> 1), y ^ (d & 1), c)) for d in range(1, N_CHIP)]


def _plan_gather_relay(land, _):
    x, y, c = lax.axis_index("x"), lax.axis_index("y"), lax.axis_index("c")
    slots = [land.at[4 * (x ^ (d >> 1)) + 2 * (y ^ (d & 1)) + c] for d in range(1, N_CHIP)]
    return [(s, s, (x, y, 1 - c)) for s in slots]


def _plan_gather_direct(land, _):
    return _plan_to_all(land.at[_my_slot()], land)


_PLAN_COPIES[_plan_gather_first] = N_CHIP
_PLAN_COPIES[_plan_gather_relay] = N_CHIP - 1
_PLAN_COPIES[_plan_gather_direct] = N_DEV - 1


def _exchange_start(plan, arrs, lands, *, after=None, name):
    bufs = list(lands) if arrs is None else list(arrs) + list(lands)
    n, nb = len(lands), len(bufs)
    nsem = sum(_PLAN_COPIES[p] for p in _plans(plan, n))
    dep_specs, deps = _dep_specs(after)

    def body(*refs):
        ins, land_refs = refs[:n], refs[nb - n:nb]
        send, recv = refs[nb + len(deps)], refs[nb + len(deps) + 1]
        for cp in _exchange_copies(plan, ins, land_refs, send, recv):
            cp.start()
        refs[-1][...] = jnp.zeros_like(refs[-1])

    out = pl.pallas_call(
        body, name=name,
        out_shape=(pltpu.SemaphoreType.DMA((nsem,)), pltpu.SemaphoreType.DMA((nsem,)),
                   *[pltpu.HBM(a.shape, a.dtype) for a in bufs], _sds((8, LANE), F32)),
        in_specs=[_HBM] * nb + dep_specs,
        out_specs=(_SEM, _SEM, *([_HBM] * nb), pl.BlockSpec(memory_space=pltpu.VMEM)),
        input_output_aliases={i: 2 + i for i in range(nb)},
        compiler_params=pltpu.CompilerParams(has_side_effects=_DATAFLOW),
    )(*[pltpu.with_memory_space_constraint(a, pltpu.HBM) for a in bufs], *deps)
    return (plan, n, out[0], out[1], list(out[2:2 + nb])), out[-1]


def _exchange_now(plan, lands, *, name):
    n = len(lands)
    nsem = sum(_PLAN_COPIES[p] for p in _plans(plan, n))

    def body(*refs):
        land_refs, send, recv = refs[n:2 * n], refs[2 * n], refs[2 * n + 1]
        copies = _exchange_copies(plan, land_refs, land_refs, send, recv)
        for cp in copies:
            cp.start()
        for cp in copies:
            cp.wait_send()
            cp.wait_recv()

    hbm = pl.BlockSpec(memory_space=pl.ANY)
    return pl.pallas_call(
        body, in_specs=[hbm] * n, out_specs=[hbm] * n, out_shape=[_sds(a.shape, a.dtype) for a in lands],
        input_output_aliases={i: i for i in range(n)},
        scratch_shapes=[pltpu.SemaphoreType.DMA((nsem,)), pltpu.SemaphoreType.DMA((nsem,))], name=name)(*lands)


def _exchange_wait(state, after, *, name):
    plan, n, send_sem, recv_sem, bufs = state
    nb = len(bufs)
    after = list(after) if isinstance(after, (list, tuple)) else [after]

    def body(*refs):
        ins, land_refs, send, recv = refs[:n], refs[nb - n:nb], refs[nb], refs[nb + 1]
        for cp in _exchange_copies(plan, ins, land_refs, send, recv):
            cp.wait_send()
            cp.wait_recv()

    out = pl.pallas_call(
        body, name=name, out_shape=[pltpu.HBM(a.shape, a.dtype) for a in bufs],
        in_specs=[_HBM] * nb + [_SEM, _SEM] + [pl.BlockSpec(memory_space=pl.ANY)] * len(after), out_specs=[_HBM] * nb,
        input_output_aliases={i: i for i in range(nb)},
        compiler_params=pltpu.CompilerParams(has_side_effects=_DATAFLOW),
    )(*bufs, send_sem, recv_sem, *after)
    return list(out[:n]), list(out[nb - n:])


def _dep_specs(after):
    return ([], []) if after is None else ([pl.BlockSpec(memory_space=pl.ANY)], [after])


def _undo_column_split(g):
    return jnp.transpose(g, (1, 0, 2)).reshape(g.shape[1], N_DEV * g.shape[2])


def _column_split(a):
    r, c = a.shape
    return jnp.transpose(a.reshape(r, N_DEV, c // N_DEV), (1, 0, 2))


class _WholeWeights:
    def __init__(self, groups):
        self.groups = groups
        self.grads = {}

    def fetch(self, group, after):
        return self.groups[group]

    def emit(self, group, grads):
        self.grads.update(grads)
        return None


def _local_step(x, target, replicated, src):
    d = x.shape[1]
    mix_g, ffn_g = replicated["mix_g"], replicated["ffn_g"]
    cp = src.fetch("cp", [])
    cp_mid = (cp["conv_w"], replicated["conv_b"], replicated["ln_g"], replicated["ln_b"], replicated["pool_w"],
              replicated["pool_scale"])

    h1, cat, z0, u0, cv0, h0 = _cp_mid_fwd(x, cp["meta"], mix_g[0:1], cp["cp_w_in_t"], cp["cp_w_out"], *cp_mid,
                                           name="cp_mixer")
    ffn0 = src.fetch("ffn0", h1)
    h2, uf0, rf0 = _ffn_fwd(h1, ffn_g[0:1], ffn0["w1"], ffn0["w2"], name="ffn0")
    gla = src.fetch("gla", h2)
    gla_mid = (gla["gate_w"], gla["gate_b"], gla["head_g"])
    h3, og, states, z1, u1, gla_w_in_t, o1 = _gla_mid_fwd(h2, mix_g[1:2], gla["gla_w_in_t"], gla["gla_w_out"], *gla_mid,
                                                          name="gla_mixer")
    ffn1 = src.fetch("ffn1", h3)
    dh4, uf1, rf1, loss, d_final_g = _ffn_fwd(h3, ffn_g[1:2], ffn1["w1"], ffn1["w2"],
                                              loss_head=(replicated["final_g"], target), name="ffn1_loss")

    dh3, dhh1, dob1, dffn_g1 = _ffn_bwd_x(h3, dh4, ffn_g[1:2], rf1, ffn1["w1"], ffn1["w2"], name="ffn1_bwd_x")
    sent = src.emit("ffn1", dict(w1=_linear_bwd_w(uf1, dhh1, column_blocks=N_DEV, name="ffn1_dw1"),
                                 w2=_linear_bwd_w(rf1, dob1, square_x=True, name="ffn1_dw2")))
    d_gla_w_out = _linear_bwd_w(og, dh3, name="gla_out_dw")
    dh2, dmix_g1, dz1, d_gate_w, d_gate_b, d_head_g = _gla_mid_bwd(
        z1, o1, h2, mix_g[1:2], gla_w_in_t, dh3, gla["gla_w_out"], states, *gla_mid, after=sent, name="gla_mixer_bwd")
    d_gla_w_in_t = _linear_bwd_w(dz1, u1, row_blocks=gla["gla_w_in_t"].shape[:2], name="gla_in_dw")
    sent = src.emit("gla", dict(gla_w_in_t=d_gla_w_in_t, gla_w_out=d_gla_w_out))
    dh1, dhh0, dob0, dffn_g0 = _ffn_bwd_x(h1, dh2, ffn_g[0:1], rf0, ffn0["w1"], ffn0["w2"], after=sent, name="ffn0_bwd_x")
    sent = src.emit("ffn0_w1", dict(w1=_linear_bwd_w(uf0, dhh0, column_blocks=N_DEV, name="ffn0_dw1")))
    sent = src.emit("ffn0_w2", dict(w2=_linear_bwd_w(rf0, dob0, square_x=True, after=sent, name="ffn0_dw2")))
    d_cp_w_out = _linear_bwd_w(cat, dh1, after=sent, name="cp_out_dw")
    sent = src.emit("cp_out", dict(cp_w_out=d_cp_w_out))
    dx, dh0_first, dmix_g0, dz0, d_conv_w, d_conv_b, d_ln_g, d_ln_b, d_pool_w, d_pool_scale = _cp_mid_bwd(
        z0, cv0, h0, mix_g[0:1], cp["cp_w_in_t"], dh1, cp["cp_w_out"], *cp_mid, after=sent, name="cp_mixer_bwd")
    d_cp_w_in_t = _linear_bwd_w(dz0, u0, name="cp_in_dw")

    small = dict(
        mix_g=jnp.concatenate([dmix_g0, dmix_g1]), ffn_g=jnp.concatenate([dffn_g0, dffn_g1]), conv_b=d_conv_b, ln_g=d_ln_g,
        ln_b=d_ln_b, pool_w=d_pool_w, pool_scale=d_pool_scale, final_g=d_final_g, meta=dh0_first[PAD_ROWS:], conv_w=d_conv_w,
        gate_w=d_gate_w, gate_b=d_gate_b, head_g=d_head_g)
    src.emit("cp", dict(cp_w_in_t=d_cp_w_in_t, small=small, loss=loss))
    return loss, dx, small


_REPLICATED = ("mix_norm_g", "ffn_norm_g", "cp_conv_b", "cp_ln_g", "cp_ln_b", "cp_pool_w", "cp_pool_scale", "final_norm_g")
_SMALL_SHARDED = ("meta_tokens", "cp_conv_w", "gla_gate_w2", "gla_gate_b", "gla_head_g")
_NAMES = ("meta_tokens", "mix_norm_g", "ffn_norm_g", "ffn_w1", "ffn_w2", "cp_w_in", "cp_conv_w", "cp_conv_b", "cp_ln_g",
          "cp_ln_b", "cp_pool_w", "cp_pool_scale", "cp_w_out", "gla_w_in", "gla_gate_w2", "gla_gate_b", "gla_head_g",
          "gla_w_out", "final_norm_g")
_SMALL_GRADS = ("mix_g", "ffn_g", "conv_b", "ln_g", "ln_b", "pool_w", "pool_scale", "final_g", "meta", "conv_w", "gate_w",
                "gate_b", "head_g")
_GROUPS = ("cp", "ffn0", "gla", "ffn1")
_TWO_LEG_GATHERS = ("cp", "ffn0", "ffn1")


class _Exchanges:
    def __init__(self, w, d):
        self.d = d
        small = [w[n].reshape(w[n].shape[-2:]) for n in _SMALL_SHARDED]
        self.small_shard_shapes = [w[n].shape for n in _SMALL_SHARDED]
        shards = dict(
            cp=[(w["cp_w_in"][0].T, BF16), (w["cp_w_out"][0], BF16)] + [(a, F32) for a in small],
            ffn0=[((w["ffn_w1"], 0), BF16), ((w["ffn_w2"], 0), BF16)],
            gla=[(w["gla_w_in"][0].T, BF16), (w["gla_w_out"][0], BF16)],
            ffn1=[((w["ffn_w1"], 1), BF16), ((w["ffn_w2"], 1), BF16)])
        self.gathers = {}
        self.sent = {}
        token = None
        for group in _GROUPS:
            lands = _place_own(shards[group], after=token, name=f"place_w_{group}")
            plan = _plan_gather_first if group in _TWO_LEG_GATHERS else _plan_gather_direct
            self.gathers[group], token = _exchange_start(plan, None, lands, after=token, name=f"start_w_{group}")
        self.token = token

    def fetch(self, group, after):
        d = self.d
        after = (list(after) if isinstance(after, (list, tuple)) else [after]) + [self.token]
        _, got = _exchange_wait(self.gathers[group], after, name=f"wait_w_{group}")
        if group in _TWO_LEG_GATHERS:
            got = _exchange_now(_plan_gather_relay, got, name=f"relay_w_{group}")
        if group in ("ffn0", "ffn1"):
            return dict(w1=got[0], w2=got[1])
        if group == "gla":
            return dict(gla_w_in_t=got[0], gla_w_out=got[1].reshape(d, d), gate_w=self.gate_w, gate_b=self.gate_b,
                        head_g=self.head_g)
        meta, conv_w, gate_w, self.gate_b, self.head_g = [_undo_column_split(a) for a in got[2:]]
        self.gate_w = jnp.pad(gate_w, ((0, GATE_PAD - GATE_RANK), (0, 0))).astype(BF16)
        return dict(cp_w_in_t=got[0].reshape(-1, d), cp_w_out=got[1].reshape(d, d), meta=meta,
                    conv_w=jnp.pad(conv_w, ((0, 1), (0, 0))))

    def emit(self, group, g):
        d = self.d
        if group == "ffn1":
            arrs = [g["w1"], g["w2"].reshape(N_DEV, -1, d)]
        elif group == "ffn0_w1":
            arrs = [g["w1"]]
        elif group == "ffn0_w2":
            arrs = [g["w2"].reshape(N_DEV, -1, d)]
        elif group == "gla":
            arrs = [g["gla_w_in_t"], g["gla_w_out"].reshape(N_DEV, d // N_DEV, d)]
        elif group == "cp_out":
            arrs = [g["cp_w_out"].reshape(N_DEV, d // N_DEV, d)]
        else:
            s = dict(g["small"])
            s.update(pool_w=s["pool_w"][None], conv_w=s["conv_w"][:CONV_WIDTH], gate_w=s["gate_w"][:GATE_RANK])
            own = [s[n] for n in _SMALL_GRADS[:len(_REPLICATED)]]
            own += [_column_split(s[n]).reshape((N_DEV,) + shape)
                    for n, shape in zip(_SMALL_GRADS[len(_REPLICATED):], self.small_shard_shapes)]
            plans = [_plan_to_all] * len(_REPLICATED) + [_plan_split_to_all] * len(_SMALL_SHARDED)
            lands = [lax.empty((N_DEV,) + a.shape, F32) for a in own[:len(_REPLICATED)]]
            lands += [lax.empty(a.shape, F32) for a in own[len(_REPLICATED):]]
            own.append(g["loss"])
            plans.append(_plan_to_all)
            lands.append(lax.empty((N_DEV,) + g["loss"].shape, F32))
            own.append(g["cp_w_in_t"].reshape(N_DEV, -1, d))
            plans.append(_plan_split_to_all)
            lands.append(lax.empty(own[-1].shape, BF16))
            self.sent[group], self.token = _exchange_start(plans, own, lands, after=self.token, name=f"start_g_{group}")
            return self.token
        self.sent[group], self.token = _exchange_start(_plan_split_to_all, arrs, [lax.empty(a.shape, a.dtype) for a in arrs],
                                                       after=self.token, name=f"start_g_{group}")
        return self.token

    def finish(self, w, mom, var):
        out = {}
        after = self.token

        def landed(group):
            own, got = _exchange_wait(self.sent[group], after, name=f"wait_g_{group}")
            return list(zip(own, got))

        def adam(n, parts, behind=None, transposed=False):
            by_row = transposed and w[n].shape[2] % 8 != 0
            if by_row:
                flip, back = (lambda a: jnp.transpose(a, (2, 0, 1))), (lambda a: jnp.transpose(a, (1, 2, 0)))
            else:
                flip = back = (lambda a: jnp.transpose(a, (0, 2, 1))) if transposed else (lambda a: a)
            res = _reduce_adam(parts, flip(w[n]), flip(mom[n]), flip(var[n]), by_row=by_row, after=behind, name=f"adam_{n}")
            out[n] = tuple(back(a) for a in res)
            return res[0]

        ffn1 = landed("ffn1")
        after = ffn1[0][1]
        gla = landed("gla")
        after = adam("gla_w_in", [gla[0]], transposed=True)
        after = adam("gla_w_out", [gla[1]], after)
        ffn0_w1 = landed("ffn0_w1")
        after = adam("ffn_w1", [ffn0_w1[0], ffn1[0]])
        ffn0_w2 = landed("ffn0_w2")
        after = adam("ffn_w2", [ffn0_w2[0], ffn1[1]])
        cp_out = landed("cp_out")
        after = adam("cp_w_out", [cp_out[0]])
        *small, loss, cp_w_in = landed("cp")
        names = _REPLICATED + _SMALL_SHARDED
        split = [False] * len(_REPLICATED) + [True] * len(_SMALL_SHARDED)
        small_new = _adam_small([own for own, _ in small], [got for _, got in small], split, [w[n] for n in names],
                                [mom[n] for n in names], [var[n] for n in names], name="adam_small")
        out.update(zip(names, small_new))
        out["loss"] = _sum8(*loss, name="sum_loss")[0, 0]
        adam("cp_w_in", [cp_w_in], small_new[0][0], transposed=True)
        return out


def kernel(x, meta_tokens, mix_norm_g, ffn_norm_g, ffn_w1, ffn_w2, cp_w_in, cp_conv_w, cp_conv_b, cp_ln_g, cp_ln_b, cp_pool_w, cp_pool_scale, cp_w_out, gla_w_in, gla_gate_w2, gla_gate_b, gla_head_g, gla_w_out, final_norm_g, loss_target, m_meta_tokens, m_mix_norm_g, m_ffn_norm_g, m_ffn_w1, m_ffn_w2, m_cp_w_in, m_cp_conv_w, m_cp_conv_b, m_cp_ln_g, m_cp_ln_b, m_cp_pool_w, m_cp_pool_scale, m_cp_w_out, m_gla_w_in, m_gla_gate_w2, m_gla_gate_b, m_gla_head_g, m_gla_w_out, m_final_norm_g, v_meta_tokens, v_mix_norm_g, v_ffn_norm_g, v_ffn_w1, v_ffn_w2, v_cp_w_in, v_cp_conv_w, v_cp_conv_b, v_cp_ln_g, v_cp_ln_b, v_cp_pool_w, v_cp_pool_scale, v_cp_w_out, v_gla_w_in, v_gla_gate_w2, v_gla_gate_b, v_gla_head_g, v_gla_w_out, v_final_norm_g):
    w = dict(meta_tokens=meta_tokens, mix_norm_g=mix_norm_g, ffn_norm_g=ffn_norm_g, ffn_w1=ffn_w1, ffn_w2=ffn_w2,
             cp_w_in=cp_w_in, cp_conv_w=cp_conv_w, cp_conv_b=cp_conv_b, cp_ln_g=cp_ln_g, cp_ln_b=cp_ln_b,
             cp_pool_w=cp_pool_w, cp_pool_scale=cp_pool_scale, cp_w_out=cp_w_out, gla_w_in=gla_w_in,
             gla_gate_w2=gla_gate_w2, gla_gate_b=gla_gate_b, gla_head_g=gla_head_g, gla_w_out=gla_w_out,
             final_norm_g=final_norm_g.reshape(1, -1))
    mom = dict(meta_tokens=m_meta_tokens, mix_norm_g=m_mix_norm_g, ffn_norm_g=m_ffn_norm_g, ffn_w1=m_ffn_w1, ffn_w2=m_ffn_w2,
               cp_w_in=m_cp_w_in, cp_conv_w=m_cp_conv_w, cp_conv_b=m_cp_conv_b, cp_ln_g=m_cp_ln_g, cp_ln_b=m_cp_ln_b,
               cp_pool_w=m_cp_pool_w, cp_pool_scale=m_cp_pool_scale, cp_w_out=m_cp_w_out, gla_w_in=m_gla_w_in,
               gla_gate_w2=m_gla_gate_w2, gla_gate_b=m_gla_gate_b, gla_head_g=m_gla_head_g, gla_w_out=m_gla_w_out,
               final_norm_g=m_final_norm_g.reshape(1, -1))
    var = dict(meta_tokens=v_meta_tokens, mix_norm_g=v_mix_norm_g, ffn_norm_g=v_ffn_norm_g, ffn_w1=v_ffn_w1, ffn_w2=v_ffn_w2,
               cp_w_in=v_cp_w_in, cp_conv_w=v_cp_conv_w, cp_conv_b=v_cp_conv_b, cp_ln_g=v_cp_ln_g, cp_ln_b=v_cp_ln_b,
               cp_pool_w=v_cp_pool_w, cp_pool_scale=v_cp_pool_scale, cp_w_out=v_cp_w_out, gla_w_in=v_gla_w_in,
               gla_gate_w2=v_gla_gate_w2, gla_gate_b=v_gla_gate_b, gla_head_g=v_gla_head_g, gla_w_out=v_gla_w_out,
               final_norm_g=v_final_norm_g.reshape(1, -1))
    d = x.shape[-1]
    replicated = dict(mix_g=w["mix_norm_g"], ffn_g=w["ffn_norm_g"], conv_b=w["cp_conv_b"], ln_g=w["cp_ln_g"],
                      ln_b=w["cp_ln_b"], pool_w=w["cp_pool_w"][0].astype(BF16), pool_scale=w["cp_pool_scale"],
                      final_g=w["final_norm_g"])
    exchanges = _Exchanges(w, d)
    _, grad_x, _ = _local_step(x[0], loss_target[0], replicated, exchanges)
    out = exchanges.finish(w, mom, var)
    loss = out.pop("loss")

    def leaf(n, k):
        a = out[n][k]
        return a.reshape(-1) if n == "final_norm_g" else a

    return (loss, grad_x[None], *[leaf(n, 0) for n in _NAMES], *[leaf(n, 1) for n in _NAMES],
            *[leaf(n, 2) for n in _NAMES], *[leaf(n, 3) for n in _NAMES])
```

```python
import functools

import jax
import jax.numpy as jnp
from jax import lax
from jax.experimental import pallas as pl
from jax.experimental.pallas import tpu as pltpu

F32, BF16 = jnp.float32, jnp.bfloat16
N_DEV = 8
CHUNK = 64
N_META = 16
PAD_ROWS = CHUNK - N_META
HALO = 32
EPS = 1e-5
CONV_WIDTH = 31
POOL_WINDOWS = (2, 4, 8, 16)
HEADS = 4
GATE_RANK = 16
GATE_NORM = 16.0
GATE_PAD = 128
ADAM_LR, ADAM_B1, ADAM_B2, ADAM_EPS, ADAM_WD, ADAM_STEP = 0.001, 0.9, 0.999, 1e-08, 0.01, 10
V7X_VMEM_LIMIT = 56 * 2 ** 20
LANE = 128


def _cparams(*sem):
    return pltpu.CompilerParams(dimension_semantics=sem, vmem_limit_bytes=V7X_VMEM_LIMIT)


def _row_tile(t, cap):
    best = CHUNK
    for r in range(CHUNK, min(t, cap) + 1, CHUNK):
        if t % r == 0:
            best = r
    return best


def _resident(shape):
    return pl.BlockSpec(shape, lambda *_: (0,) * len(shape), pipeline_mode=pl.Buffered(1))


def _dot(a, b):
    return jnp.dot(a, b, preferred_element_type=F32)


def _dot_nt(a, b):
    return lax.dot_general(a, b, (((1,), (1,)), ((), ())), preferred_element_type=F32)


def _dot_tn(a, b):
    return lax.dot_general(a, b, (((0,), (0,)), ((), ())), preferred_element_type=F32)


def _rowsum(a):
    return jnp.sum(a, axis=0, keepdims=True)


def _sigmoid(a):
    return 1.0 / (1.0 + jnp.exp(-a))


def _row_ids(tile, rt):
    return tile * rt + lax.broadcasted_iota(jnp.int32, (rt, 1), 0)


def _sds(shape, dtype):
    return jax.ShapeDtypeStruct(shape, dtype)


DW_ROWS = 1024


def _linear_bwd_w(x, dy, *, square_x=False, column_blocks=None, row_blocks=None, after=None, name):
    t, k = x.shape
    n = dy.shape[1]
    cut_k = k > n and column_blocks is None
    assert cut_k or row_blocks is None
    width = k if cut_k else n
    blk = n // column_blocks if column_blocks else max(c for c in (640, 512, 384, 256, LANE) if width % c == 0)
    dep_specs, deps = _dep_specs(after)

    def body(*refs):
        x_ref, dy_ref, o_ref, acc = refs[len(deps):]
        for c0 in range(0, t, DW_ROWS):
            rows = slice(c0, min(c0 + DW_ROWS, t))
            xv = x_ref[rows, :]
            if square_x:
                xv = xv.astype(F32)
                xv = xv * xv
            part = _dot_tn(xv.astype(BF16), dy_ref[rows, :].astype(BF16))
            if c0 == 0:
                acc[...] = part
            else:
                acc[...] += part
        if row_blocks is None:
            o_ref[...] = acc[...].astype(BF16)
            return
        nb, rpb = row_blocks
        for s in range(width // blk):
            @pl.when(pl.program_id(0) == s)
            def _(s=s):
                for b in range(nb):
                    lo, hi = max(s * blk, b * rpb), min((s + 1) * blk, (b + 1) * rpb)
                    if lo < hi:
                        o_ref[b, lo - b * rpb:hi - b * rpb, :] = acc[lo - s * blk:hi - s * blk, :].astype(BF16)

    out_shape = _sds((k, n), BF16)
    semantics = "parallel"
    if cut_k:
        in_specs = [pl.BlockSpec((t, blk), lambda j: (0, j)), _resident((t, n))]
        out_specs = pl.BlockSpec((blk, n), lambda j: (j, 0))
        acc_shape = (blk, n)
        if row_blocks:
            out_shape = _sds(row_blocks + (n,), BF16)
            out_specs = pl.BlockSpec(out_shape.shape, lambda j: (0, 0, 0))
            semantics = "arbitrary"
    else:
        in_specs = [_resident((t, k)), pl.BlockSpec((t, blk), lambda j: (0, j))]
        out_specs = pl.BlockSpec((k, blk), lambda j: (0, j))
        acc_shape = (k, blk)
        if column_blocks:
            out_specs = pl.BlockSpec((None, k, blk), lambda j: (j, 0, 0))
            out_shape = _sds((column_blocks, k, blk), BF16)
    return pl.pallas_call(
        body, grid=(width // blk,), in_specs=dep_specs + in_specs, out_specs=out_specs, out_shape=out_shape,
        scratch_shapes=[pltpu.VMEM(acc_shape, F32)], compiler_params=_cparams(semantics), name=name)(*deps, x, dy)


FFN_BLOCKS_PER_STEP = 2


def _ffn_fwd(h, gain, w1g, w2g, *, loss_head=None, name):
    t, d = h.shape
    f8 = w1g.shape[-1]
    rt = _row_tile(t, 832)
    nb = FFN_BLOCKS_PER_STEP
    nstep = N_DEV // nb

    def body(*refs):
        if loss_head is None:
            h_ref, g_ref, w1_ref, w2_ref, o_ref, u_ref, r_ref, acc_ref = refs
        else:
            (h_ref, g_ref, w1_ref, w2_ref, fg_ref, tgt_ref, o_ref, u_ref, r_ref, loss_ref, dfg_ref, acc_ref, t_ref,
             t_sem) = refs
        i, j = pl.program_id(0), pl.program_id(1)

        def target_rows(act):
            @pl.when(i == 0)
            def _():
                act(pltpu.make_async_copy(tgt_ref.at[pl.ds(0, rt - CHUNK)], t_ref.at[pl.ds(CHUNK, rt - CHUNK)], t_sem.at[0]))

            if t > rt:
                @pl.when(i > 0)
                def _():
                    act(pltpu.make_async_copy(tgt_ref.at[pl.ds(pl.multiple_of(i * rt - CHUNK, CHUNK), rt)], t_ref,
                                              t_sem.at[0]))

        @pl.when(j == 0)
        def _():
            if loss_head is not None:
                @pl.when(i == 0)
                def _():
                    t_ref[0:CHUNK, :] = jnp.zeros((CHUNK, d), F32)

                target_rows(lambda copy: copy.start())

            hv = h_ref[...]
            u_ref[...] = (hv * lax.rsqrt(jnp.mean(hv * hv, axis=-1, keepdims=True) + EPS) * g_ref[...]).astype(BF16)
            acc_ref[...] = jnp.zeros_like(acc_ref)

        part = None
        for b in range(nb):
            a = jnp.maximum(_dot(u_ref[...], w1_ref[b]), 0.0)
            r_ref[:, b * f8:(b + 1) * f8] = a.astype(BF16)
            term = _dot((a * a).astype(BF16), w2_ref[b])
            part = term if part is None else part + term
        acc_ref[...] += part

        @pl.when(j == nstep - 1)
        def _():
            y = h_ref[...] + acc_ref[...]
            if loss_head is None:
                o_ref[...] = y
                return

            @pl.when(i == 0)
            def _():
                loss_ref[...] = jnp.zeros_like(loss_ref)
                dfg_ref[...] = jnp.zeros_like(dfg_ref)

            target_rows(lambda copy: copy.wait())

            rstd = lax.rsqrt(jnp.mean(y * y, axis=-1, keepdims=True) + EPS)
            xh = y * rstd
            err = jnp.where(_row_ids(i, rt) >= CHUNK, xh * fg_ref[...] - t_ref[...], 0.0)
            loss_ref[...] += (0.5 / d) * jnp.sum(err * err)
            dy = err * (1.0 / d)
            dfg_ref[...] += _rowsum(dy * xh)
            dxh = dy * fg_ref[...]
            o_ref[...] = rstd * (dxh - xh * jnp.mean(dxh * xh, axis=-1, keepdims=True))

    rows = lambda i, j: (i, 0)
    in_specs = [pl.BlockSpec((rt, d), rows), _resident((1, d)),
                pl.BlockSpec((nb, d, f8), lambda i, j: (j, 0, 0)), pl.BlockSpec((nb, f8, d), lambda i, j: (j, 0, 0))]
    out_specs = [pl.BlockSpec((rt, d), rows), pl.BlockSpec((rt, d), rows), pl.BlockSpec((rt, nb * f8), lambda i, j: (i, j))]
    out_shape = [_sds((t, d), F32), _sds((t, d), BF16), _sds((t, N_DEV * f8), BF16)]
    args = [h, gain, w1g, w2g]
    scratch_shapes = [pltpu.VMEM((rt, d), F32)]
    if loss_head is not None:
        in_specs += [_resident((1, d)), pl.BlockSpec(memory_space=pl.ANY)]
        out_specs += [pl.BlockSpec((8, LANE), lambda i, j: (0, 0)), pl.BlockSpec((1, d), lambda i, j: (0, 0))]
        out_shape += [_sds((8, LANE), F32), _sds((1, d), F32)]
        args += list(loss_head)
        scratch_shapes += [pltpu.VMEM((rt, d), F32), pltpu.SemaphoreType.DMA((1,))]
    return pl.pallas_call(
        body, grid=(t // rt, nstep), in_specs=in_specs, out_specs=out_specs, out_shape=out_shape,
        scratch_shapes=scratch_shapes,
        compiler_params=_cparams("arbitrary" if loss_head is not None else "parallel", "arbitrary"), name=name)(*args)


def _ffn_bwd_x(h, dout, gain, r, w1g, w2g, *, after=None, name):
    t, d = h.shape
    f8 = w1g.shape[-1]
    rt = _row_tile(t, 832)
    nb = FFN_BLOCKS_PER_STEP
    last = N_DEV // nb - 1
    dep_specs, deps = _dep_specs(after)

    def body(*refs):
        h_ref, do_ref, g_ref, r_ref, w1_ref, w2_ref, dh_ref, dhh_ref, dob_ref, dg_ref, du_ref = refs[len(deps):]
        i, j = pl.program_id(0), pl.program_id(1)

        @pl.when(j == 0)
        def _():
            dob_ref[...] = do_ref[...].astype(BF16)
            du_ref[...] = jnp.zeros_like(du_ref)

        part = None
        for b in range(nb):
            cols = slice(b * f8, (b + 1) * f8)
            dhh = (_dot_nt(dob_ref[...], w2_ref[b]) * (2.0 * r_ref[:, cols].astype(F32))).astype(BF16)
            dhh_ref[:, cols] = dhh
            term = _dot_nt(dhh, w1_ref[b])
            part = term if part is None else part + term
        du_ref[...] += part

        @pl.when(j == last)
        def _():
            @pl.when(i == 0)
            def _():
                dg_ref[...] = jnp.zeros_like(dg_ref)

            hv = h_ref[...]
            rstd = lax.rsqrt(jnp.mean(hv * hv, axis=-1, keepdims=True) + EPS)
            xh = hv * rstd
            du = du_ref[...]
            dg_ref[...] += _rowsum(du * xh)
            dxh = du * g_ref[...]
            dh_ref[...] = do_ref[...] + rstd * (dxh - xh * jnp.mean(dxh * xh, axis=-1, keepdims=True))

    rows = lambda i, j: (i, 0)
    return pl.pallas_call(
        body, grid=(t // rt, N_DEV // nb),
        in_specs=dep_specs + [
                  pl.BlockSpec((rt, d), rows), pl.BlockSpec((rt, d), rows), _resident((1, d)),
                  pl.BlockSpec((rt, nb * f8), lambda i, j: (i, j)),
                  pl.BlockSpec((nb, d, f8), lambda i, j: (j, 0, 0)),
                  pl.BlockSpec((nb, f8, d), lambda i, j: (j, 0, 0))],
        out_specs=[pl.BlockSpec((rt, d), rows), pl.BlockSpec((rt, nb * f8), lambda i, j: (i, j)),
                   pl.BlockSpec((rt, d), rows), pl.BlockSpec((1, d), lambda i, j: (0, 0))],
        out_shape=[_sds((t, d), F32), _sds((t, N_DEV * f8), BF16), _sds((t, d), BF16), _sds((1, d), F32)],
        scratch_shapes=[pltpu.VMEM((rt, d), F32)],
        compiler_params=_cparams("arbitrary", "arbitrary"), name=name)(*deps, h, dout, gain, r, w1g, w2g)


def _lane_blocks(width):
    lb = min(LANE, width)
    return [slice(s, s + lb) for s in range(0, width, lb)]


def _conv_rows(src_ref, w_ref, offset, dst_ref, nblk, width, bias_ref=None):
    def blk(rb, carry):
        base = pl.multiple_of(rb * CHUNK, CHUNK)
        for l, ls in enumerate(_lane_blocks(width)):
            acc = jnp.zeros((CHUNK, ls.stop - ls.start), F32)
            if bias_ref is not None:
                acc = acc + bias_ref[:, ls]
            for k in range(CONV_WIDTH):
                acc = acc + w_ref[k:k + 1, ls] * src_ref[l, pl.ds(base + offset(k), CHUNK), :]
            dst_ref[l, pl.ds(base, CHUNK), :] = acc
        return carry

    lax.fori_loop(0, nblk, blk, 0)


def _to_lane_blocks(ref, row0, value):
    for l, ls in enumerate(_lane_blocks(value.shape[1])):
        ref[l, row0:row0 + value.shape[0], :] = value[:, ls]


def _from_lane_blocks(ref):
    return jnp.concatenate([ref[l] for l in range(ref.shape[0])], axis=1)


def _pool_counts(rows, window):
    return jnp.clip(rows - PAD_ROWS + 1, 1, window).astype(F32)


def _trailing_sum(v, window):
    s, sh = v, 1
    while sh < window:
        s = s + pltpu.roll(s, sh, 0)
        sh *= 2
    return s


def _leading_sum(v, window):
    s, sh, n = v, 1, v.shape[0]
    while sh < window:
        s = s + pltpu.roll(s, n - sh, 0)
        sh *= 2
    return s


def _norm_project(h_ref, g_ref, w_t_ref, u_ref, z_ref):
    hv = h_ref[...]
    u = (hv * lax.rsqrt(jnp.mean(hv * hv, axis=-1, keepdims=True) + EPS) * g_ref[...]).astype(BF16)
    u_ref[...] = u
    z_ref[...] = _dot_nt(u, w_t_ref[...])


def _project_back(dz_ref, w_t_ref, h_ref, g_ref, dres_ref, dh_ref, dg_ref, first):
    dx = _dot(dz_ref[...], w_t_ref[...])
    hv = h_ref[...]
    rstd = lax.rsqrt(jnp.mean(hv * hv, axis=-1, keepdims=True) + EPS)
    xh = hv * rstd

    @pl.when(first)
    def _():
        dg_ref[...] = jnp.zeros_like(dg_ref)

    dg_ref[...] += _rowsum(dx * xh)
    dxh = dx * g_ref[...]
    dh_ref[...] = dres_ref[...] + rstd * (dxh - xh * jnp.mean(dxh * xh, axis=-1, keepdims=True))


def _cp_mid_fwd(x, meta, gain, w_in_t, w_out, conv_w, conv_b, ln_g, ln_b, pool_w, pool_scale, *, name):
    seq, d = x.shape
    t = seq + CHUNK
    ein = w_in_t.shape[0]
    cd = conv_b.shape[1]
    pd = pool_scale.shape[1]
    pg = pd // len(POOL_WINDOWS)
    rt = _row_tile(t, 320)
    ntile = t // rt

    def body(x_ref, meta_ref, g_ref, wi_ref, wo_ref, cw_ref, cb_ref, lg_ref, lb_ref, pw_ref, ps_ref,
             ho_ref, o_ref, z_ref, u_ref, cv_ref, h0_ref, gext, pext, conv_s, hbuf, hsem):
        i = pl.program_id(0)
        slot = i % 2
        first_rows = pltpu.make_async_copy(x_ref.at[pl.ds(0, rt - CHUNK)], hbuf.at[0, pl.ds(CHUNK, rt - CHUNK)], hsem.at[0])

        def tile_rows(tile, to):
            return pltpu.make_async_copy(x_ref.at[pl.ds(pl.multiple_of(tile * rt - CHUNK, CHUNK), rt)], hbuf.at[to],
                                         hsem.at[to])

        @pl.when(i == 0)
        def _():
            first_rows.start()
            hbuf[0, 0:PAD_ROWS, :] = jnp.zeros((PAD_ROWS, d), F32)
            hbuf[0, PAD_ROWS:CHUNK, :] = meta_ref[...]
            _to_lane_blocks(gext, 0, jnp.zeros((HALO, cd), F32))
            pext[0:HALO, :] = jnp.zeros((HALO, pd), F32)

        @pl.when(i + 1 < ntile)
        def _():
            tile_rows(i + 1, 1 - slot).start()

        @pl.when(i == 0)
        def _():
            first_rows.wait()

        @pl.when(i > 0)
        def _():
            tile_rows(i, slot).wait()

        h_ref = hbuf.at[slot]
        h0_ref[...] = h_ref[...]
        _norm_project(h_ref, g_ref, wi_ref, u_ref, z_ref)

        _to_lane_blocks(gext, HALO, z_ref[:, 0:cd] * _sigmoid(z_ref[:, cd:2 * cd]))
        pext[HALO:HALO + rt, :] = z_ref[:, 2 * cd:]
        _conv_rows(gext, cw_ref, lambda k: k + HALO - (CONV_WIDTH - 1), conv_s, rt // CHUNK, cd, cb_ref)
        cv = _from_lane_blocks(conv_s)
        cv_ref[...] = cv
        xc = cv - jnp.mean(cv, axis=-1, keepdims=True)
        y = xc * lax.rsqrt(jnp.mean(xc * xc, axis=-1, keepdims=True) + EPS) * lg_ref[...] + lb_ref[...]
        rows = _row_ids(i, rt)
        a = jnp.where(rows >= PAD_ROWS, y * _sigmoid(y), 0.0)
        o_ref[:, 0:cd] = a.astype(BF16)
        for gi, window in enumerate(POOL_WINDOWS):
            ls = slice(gi * pg, (gi + 1) * pg)
            v = pext[:, ls]
            tm = _trailing_sum(v, window)[HALO:] / _pool_counts(rows, window) - v[HALO:]
            p = _dot(tm.astype(BF16), pw_ref[gi]) * ps_ref[:, ls]
            o_ref[:, cd + gi * pg:cd + (gi + 1) * pg] = p.astype(BF16)
        ho_ref[...] = h_ref[...] + _dot(o_ref[...], wo_ref[...])
        gext[:, 0:HALO, :] = gext[:, rt:rt + HALO, :]
        pext[0:HALO, :] = pext[rt:rt + HALO, :]

    nl, lb = len(_lane_blocks(cd)), min(LANE, cd)
    rows = lambda i: (i, 0)
    return pl.pallas_call(
        body, grid=(ntile,),
        in_specs=[pl.BlockSpec(memory_space=pl.ANY), _resident(meta.shape), _resident((1, d)), _resident(w_in_t.shape),
                  _resident(w_out.shape), _resident(conv_w.shape), _resident((1, cd)),
                  _resident((1, cd)), _resident((1, cd)), _resident(pool_w.shape), _resident((1, pd))],
        out_specs=[pl.BlockSpec((rt, d), rows), pl.BlockSpec((rt, cd + pd), rows), pl.BlockSpec((rt, ein), rows),
                   pl.BlockSpec((rt, d), rows), pl.BlockSpec((rt, cd), rows), pl.BlockSpec((rt, d), rows)],
        out_shape=[_sds((t, d), F32), _sds((t, cd + pd), BF16), _sds((t, ein), F32), _sds((t, d), BF16),
                   _sds((t, cd), F32), _sds((t, d), F32)],
        scratch_shapes=[pltpu.VMEM((nl, rt + HALO, lb), F32), pltpu.VMEM((rt + HALO, pd), F32),
                        pltpu.VMEM((nl, rt, lb), F32), pltpu.VMEM((2, rt, d), F32), pltpu.SemaphoreType.DMA((2,))],
        compiler_params=_cparams("arbitrary"), name=name)(x, meta, gain, w_in_t, w_out, conv_w, conv_b, ln_g, ln_b, pool_w,
                                                          pool_scale)


def _cp_mid_bwd(z, cv, h, gain, w_in_t, dh, w_out, conv_w, conv_b, ln_g, ln_b, pool_w, pool_scale, *, after=None, name):
    t, ein = z.shape
    cd = conv_b.shape[1]
    pd = pool_scale.shape[1]
    pg = pd // len(POOL_WINDOWS)
    rt = _row_tile(t, 320)
    ntile = t // rt
    per = rt // CHUNK
    dep_specs, deps = _dep_specs(after)

    def body(*refs):
        (z_ref, zh_ref, cv_ref, h_ref, g_ref, wi_ref, dh_ref, wo_ref, cw_ref, cb_ref, lg_ref, lb_ref, pw_ref, ps_ref,
         dx_ref, dfirst_ref, dg_ref, dz_ref, dcw_ref, dcb_ref, dlg_ref, dlb_ref, dpw_ref, dps_ref,
         gext, pext, conv_s, dcv, dsp, dhi, dx_sem) = refs[len(deps):]
        step = pl.program_id(0)
        tile = ntile - 1 - step
        slot = step % 2
        last_slot = (ntile - 1) % 2
        first_rows = pltpu.make_async_copy(dhi.at[last_slot, pl.ds(CHUNK, rt - CHUNK)], dx_ref.at[pl.ds(0, rt - CHUNK)],
                                           dx_sem.at[last_slot])

        def tile_rows(tile, slot):
            return pltpu.make_async_copy(dhi.at[slot], dx_ref.at[pl.ds(pl.multiple_of(tile * rt - CHUNK, CHUNK), rt)],
                                         dx_sem.at[slot])

        dcat = _dot_nt(dh_ref[...].astype(BF16), wo_ref[...])

        @pl.when(step >= 2)
        def _():
            tile_rows(tile + 2, slot).wait()

        @pl.when(step == 0)
        def _():
            for ref in (dcw_ref, dcb_ref, dlg_ref, dlb_ref, dpw_ref, dps_ref):
                ref[...] = jnp.zeros_like(ref)
            _to_lane_blocks(dcv, rt, jnp.zeros((HALO, cd), F32))
            dsp[rt:rt + HALO, :] = jnp.zeros((HALO, pd), F32)

        keep = jnp.where(tile > 0, 1.0, 0.0)
        zh = zh_ref[CHUNK - HALO:CHUNK, :]
        _to_lane_blocks(gext, 0, keep * zh[:, 0:cd] * _sigmoid(zh[:, cd:2 * cd]))
        pext[0:HALO, :] = keep * zh[:, 2 * cd:]
        za = z_ref[:, 0:cd]
        sg = _sigmoid(z_ref[:, cd:2 * cd])
        _to_lane_blocks(gext, HALO, za * sg)
        pext[HALO:HALO + rt, :] = z_ref[:, 2 * cd:]
        cv = cv_ref[...]
        xc = cv - jnp.mean(cv, axis=-1, keepdims=True)
        rstd = lax.rsqrt(jnp.mean(xc * xc, axis=-1, keepdims=True) + EPS)
        xh = xc * rstd
        y = xh * lg_ref[...] + lb_ref[...]
        sy = _sigmoid(y)
        rows = _row_ids(tile, rt)
        da = jnp.where(rows >= PAD_ROWS, dcat[:, 0:cd], 0.0)
        dy = da * (sy * (1.0 + y * (1.0 - sy)))
        dlg_ref[...] += _rowsum(dy * xh)
        dlb_ref[...] += _rowsum(dy)
        dxh = dy * lg_ref[...]
        dconv = rstd * (dxh - jnp.mean(dxh, axis=-1, keepdims=True) - xh * jnp.mean(dxh * xh, axis=-1, keepdims=True))
        dcb_ref[...] += _rowsum(dconv)
        _to_lane_blocks(dcv, 0, dconv)
        for l, ls in enumerate(_lane_blocks(cd)):
            def acc_rows(rb, accs, l=l):
                base = pl.multiple_of(rb * CHUNK, CHUNK)
                d_blk = dcv[l, pl.ds(base, CHUNK), :]
                out = []
                for k in range(CONV_WIDTH):
                    prod = d_blk * gext[l, pl.ds(base + k + HALO - (CONV_WIDTH - 1), CHUNK), :]
                    part = prod[0:8]
                    for s in range(8, CHUNK, 8):
                        part = part + prod[s:s + 8]
                    out.append(accs[k] + part)
                return tuple(out)

            zero = jnp.zeros((8, ls.stop - ls.start), F32)
            accs = lax.fori_loop(0, per, acc_rows, (zero,) * CONV_WIDTH)
            for k in range(CONV_WIDTH):
                dcw_ref[k:k + 1, ls] += _rowsum(accs[k])
        _conv_rows(dcv, cw_ref, lambda k: CONV_WIDTH - 1 - k, conv_s, per, cd)
        dglu = _from_lane_blocks(conv_s)
        dz_ref[:, 0:cd] = (dglu * sg).astype(BF16)
        dz_ref[:, cd:2 * cd] = (dglu * za * sg * (1.0 - sg)).astype(BF16)
        dcv[:, rt:rt + HALO, :] = dcv[:, 0:HALO, :]
        for gi, window in enumerate(POOL_WINDOWS):
            ls = slice(gi * pg, (gi + 1) * pg)
            v = pext[:, ls]
            cnt = _pool_counts(rows, window)
            tm = (_trailing_sum(v, window)[HALO:] / cnt - v[HALO:]).astype(BF16)
            dp = dcat[:, cd + gi * pg:cd + (gi + 1) * pg]
            dps_ref[:, ls] += _rowsum(dp * _dot(tm, pw_ref[gi]))
            dpl = (dp * ps_ref[:, ls]).astype(BF16)
            dpw_ref[gi] += _dot_tn(tm, dpl)
            dtm = _dot_nt(dpl, pw_ref[gi])
            dsp[0:rt, ls] = dtm / cnt
            dpin = _leading_sum(dsp[:, ls], window)[0:rt] - dtm
            dz_ref[:, 2 * cd + gi * pg:2 * cd + (gi + 1) * pg] = dpin.astype(BF16)
        dsp[rt:rt + HALO, :] = dsp[0:HALO, :]

        _project_back(dz_ref, wi_ref, h_ref, g_ref, dh_ref, dhi.at[slot], dg_ref, step == 0)

        @pl.when(tile > 0)
        def _():
            tile_rows(tile, slot).start()

        @pl.when(tile == 0)
        def _():
            first_rows.start()
            dfirst_ref[...] = dhi[last_slot, 0:CHUNK, :]
            if ntile > 1:
                tile_rows(1, 1 - last_slot).wait()
            first_rows.wait()

    d = h.shape[1]
    back = lambda i: (ntile - 1 - i, 0)
    halo_idx = lambda i: (jnp.maximum((ntile - 1 - i) * per - 1, 0), 0)
    const2 = lambda i: (0, 0)
    nl, lb = len(_lane_blocks(cd)), min(LANE, cd)
    return pl.pallas_call(
        body, grid=(ntile,),
        in_specs=dep_specs + [
                  pl.BlockSpec((rt, ein), back), pl.BlockSpec((CHUNK, ein), halo_idx), pl.BlockSpec((rt, cd), back),
                  pl.BlockSpec((rt, d), back),
                  _resident((1, d)), _resident(w_in_t.shape), pl.BlockSpec((rt, d), back), _resident(w_out.shape),
                  _resident(conv_w.shape), _resident((1, cd)), _resident((1, cd)), _resident((1, cd)),
                  _resident(pool_w.shape), _resident((1, pd))],
        out_specs=[pl.BlockSpec(memory_space=pl.ANY), pl.BlockSpec((CHUNK, d), const2), pl.BlockSpec((1, d), const2),
                   pl.BlockSpec((rt, ein), back), pl.BlockSpec(conv_w.shape, const2), pl.BlockSpec((1, cd), const2),
                   pl.BlockSpec((1, cd), const2), pl.BlockSpec((1, cd), const2),
                   pl.BlockSpec(pool_w.shape, lambda i: (0, 0, 0)), pl.BlockSpec((1, pd), const2)],
        out_shape=[_sds((t - CHUNK, d), F32), _sds((CHUNK, d), F32), _sds((1, d), F32),
                   _sds((t, ein), BF16), _sds(conv_w.shape, F32), _sds((1, cd), F32), _sds((1, cd), F32),
                   _sds((1, cd), F32), _sds(pool_w.shape, F32), _sds((1, pd), F32)],
        scratch_shapes=[pltpu.VMEM((nl, rt + HALO, lb), F32), pltpu.VMEM((rt + HALO, pd), F32), pltpu.VMEM((nl, rt, lb), F32),
                        pltpu.VMEM((nl, rt + HALO, lb), F32), pltpu.VMEM((rt + HALO, pd), F32), pltpu.VMEM((2, rt, d), F32),
                        pltpu.SemaphoreType.DMA((2,))],
        compiler_params=_cparams("arbitrary"), name=name)(*deps, z, z, cv, h, gain, w_in_t, dh, w_out, conv_w, conv_b, ln_g,
                                                          ln_b, pool_w, pool_scale)


def _log_decay(r, gw_ref, gb_ref, rows):
    gp = _dot(r.astype(BF16), gw_ref[...]) + gb_ref[...]
    log_sig = jnp.minimum(gp, 0.0) - jnp.log(1.0 + jnp.exp(-jnp.abs(gp)))
    return gp, jnp.where(rows >= PAD_ROWS, log_sig / GATE_NORM, 0.0)


def _tri(strict):
    r = lax.broadcasted_iota(jnp.int32, (CHUNK, CHUNK), 0)
    c = lax.broadcasted_iota(jnp.int32, (CHUNK, CHUNK), 1)
    return jnp.where(c < r if strict else c <= r, 1.0, 0.0).astype(BF16)


def _tri_dot(tri, a):
    hi = a.astype(BF16)
    rest = a - hi.astype(F32)
    mid = rest.astype(BF16)
    lo = (rest - mid.astype(F32)).astype(BF16)
    return _dot(tri, hi) + _dot(tri, mid) + _dot(tri, lo)


def _gla_mid_fwd(h, gain, w_in_blocks, w_out, gate_w, gate_b, head_g, *, name):
    t = h.shape[0]
    nblk, rpb = w_in_blocks.shape[:2]
    dk = gate_b.shape[1]
    hv = head_g.shape[1]
    hk = dk // HEADS
    dv = hv * HEADS
    r_at = 2 * dk + 2 * dv
    assert nblk * rpb == r_at + GATE_RANK
    zw = r_at + GATE_PAD
    rt = _row_tile(t, 320)
    per = rt // CHUNK
    scale = hk ** -0.5

    def body(h_ref, g_ref, wb_ref, wo_ref, gw_ref, gb_ref, hg_ref, ho_ref, o_ref, st_ref, z_ref, u_ref, wi_ref, oraw_ref,
             dec_ref, s_ref, la_ref):
        i = pl.program_id(0)

        @pl.when(i == 0)
        def _():
            s_ref[...] = jnp.zeros_like(s_ref)
            for b in range(nblk):
                wi_ref[b * rpb:(b + 1) * rpb, :] = wb_ref[b]
            wi_ref[nblk * rpb:, :] = jnp.zeros((zw - nblk * rpb, wi_ref.shape[1]), BF16)

        _norm_project(h_ref, g_ref, wi_ref, u_ref, z_ref)

        _, la = _log_decay(z_ref[:, r_at:r_at + GATE_PAD], gw_ref, gb_ref, _row_ids(i, rt))
        la_ref[...] = la
        tri = _tri(False)

        def chunk_rows(c):
            return slice(c * CHUNK, (c + 1) * CHUNK)

        def decays(c, carry):
            rows = chunk_rows(c)
            la_c = la_ref[rows, :]
            cum = _tri_dot(tri, la_c)
            dec_ref[rows, :] = jnp.exp(_rowsum(la_c) - cum)
            return carry

        def states(c, carry):
            rows = chunk_rows(c)
            etot = jnp.exp(_rowsum(la_ref[rows, :]))
            for hd in range(HEADS):
                ks = slice(hd * hk, (hd + 1) * hk)
                kd = z_ref[rows, dk + hd * hk:dk + (hd + 1) * hk] * dec_ref[rows, ks]
                v = z_ref[rows, 2 * dk + hd * hv:2 * dk + (hd + 1) * hv]
                s_new = s_ref[hd] * etot[:, ks] + _dot_tn(v.astype(BF16), kd.astype(BF16))
                s_ref[hd] = s_new
                st_ref[c, hd] = s_new
            return carry

        def outputs(c, carry):
            rows = chunk_rows(c)
            for hd in range(HEADS):
                q = z_ref[rows, hd * hk:(hd + 1) * hk] * scale
                g = z_ref[rows, 2 * dk + dv + hd * hv:2 * dk + dv + (hd + 1) * hv]
                o = _dot_nt(q.astype(BF16), st_ref[c, hd].astype(BF16))
                oraw_ref[rows, hd * hv:(hd + 1) * hv] = o
                on = o * lax.rsqrt(jnp.mean(o * o, axis=-1, keepdims=True) + EPS) * hg_ref[...]
                o_ref[rows, hd * hv:(hd + 1) * hv] = (on * (g * _sigmoid(g))).astype(BF16)
            return carry

        for phase in (decays, states, outputs):
            for c in range(per):
                phase(c, 0)
        ho_ref[...] = h_ref[...] + _dot(o_ref[...], wo_ref[...])

    d = h.shape[1]
    rows = lambda i: (i, 0)
    return pl.pallas_call(
        body, grid=(t // rt,),
        in_specs=[pl.BlockSpec((rt, d), rows), _resident((1, d)), _resident(w_in_blocks.shape), _resident(w_out.shape),
                  _resident(gate_w.shape), _resident((1, dk)), _resident((1, hv))],
        out_specs=[pl.BlockSpec((rt, d), rows), pl.BlockSpec((rt, dv), rows),
                   pl.BlockSpec((per, HEADS, hv, hk), lambda i: (i, 0, 0, 0)), pl.BlockSpec((rt, zw), rows),
                   pl.BlockSpec((rt, d), rows), pl.BlockSpec((zw, d), lambda i: (0, 0)), pl.BlockSpec((rt, dv), rows),
                   pl.BlockSpec((rt, dk), rows)],
        out_shape=[_sds((t, d), F32), _sds((t, dv), BF16), _sds((t // CHUNK, HEADS, hv, hk), F32), _sds((t, zw), F32),
                   _sds((t, d), BF16), _sds((zw, d), BF16), _sds((t, dv), F32), _sds((t, dk), F32)],
        scratch_shapes=[pltpu.VMEM((HEADS, hv, hk), F32), pltpu.VMEM((rt, dk), F32)],
        compiler_params=_cparams("arbitrary"), name=name)(h, gain, w_in_blocks, w_out, gate_w, gate_b, head_g)


def _gla_mid_bwd(z, o, dec, h, gain, w_in_t, dh, w_out, states, gate_w, gate_b, head_g, *, after=None, name):
    t = z.shape[0]
    dk = gate_b.shape[1]
    hv = head_g.shape[1]
    hk = dk // HEADS
    dv = hv * HEADS
    r_at = 2 * dk + 2 * dv
    rt = _row_tile(t, 320)
    ntile = t // rt
    per = rt // CHUNK
    scale = hk ** -0.5
    dep_specs, deps = _dep_specs(after)

    def body(*refs):
        (z_ref, o_ref, dec_ref, h_ref, g_ref, wi_ref, dh_ref, wo_ref, st_ref, stp_ref, gw_ref, gb_ref, hg_ref,
         dhi_ref, dg_ref, dz_ref, dgw_ref, dgb_ref, dhg_ref,
         ds_ref, la_ref, dla_ref, dos_ref, e_ref, do_ref) = refs[len(deps):]
        step = pl.program_id(0)
        tile = ntile - 1 - step
        do_ref[...] = _dot_nt(dh_ref[...].astype(BF16), wo_ref[...])

        @pl.when(step == 0)
        def _():
            ds_ref[...] = jnp.zeros_like(ds_ref)
            dgw_ref[...] = jnp.zeros_like(dgw_ref)
            dgb_ref[...] = jnp.zeros_like(dgb_ref)
            dhg_ref[...] = jnp.zeros_like(dhg_ref)

        rows_id = _row_ids(tile, rt)
        r = z_ref[:, r_at:r_at + GATE_PAD]
        gp, la = _log_decay(r, gw_ref, gb_ref, rows_id)
        la_ref[...] = la
        tri_strict = _tri(True)
        keep = jnp.where(tile > 0, 1.0, 0.0)

        def chunk_rows(c):
            return slice(c * CHUNK, (c + 1) * CHUNK)

        def recompute(c, dhg):
            rows = chunk_rows(c)
            for hd in range(HEADS):
                q = (z_ref[rows, hd * hk:(hd + 1) * hk] * scale).astype(BF16)
                g = z_ref[rows, 2 * dk + dv + hd * hv:2 * dk + dv + (hd + 1) * hv]
                s_b = st_ref[c, hd].astype(BF16)
                o = o_ref[rows, hd * hv:(hd + 1) * hv]
                rstd = lax.rsqrt(jnp.mean(o * o, axis=-1, keepdims=True) + EPS)
                oh = o * rstd
                sg = _sigmoid(g)
                d_og = do_ref[rows, hd * hv:(hd + 1) * hv]
                dz_ref[rows, 2 * dk + dv + hd * hv:2 * dk + dv + (hd + 1) * hv] = (
                    d_og * oh * hg_ref[...] * (sg * (1.0 + g * (1.0 - sg)))).astype(BF16)
                don = d_og * (g * sg)
                dhg = dhg + _rowsum(don * oh)
                doh = don * hg_ref[...]
                d_o = (rstd * (doh - oh * jnp.mean(doh * oh, axis=-1, keepdims=True))).astype(BF16)
                dos_ref[rows, hd * hv:(hd + 1) * hv] = d_o
                dz_ref[rows, hd * hk:(hd + 1) * hk] = (_dot(d_o, s_b) * scale).astype(BF16)
            return dhg

        def recurrence(cc, carry):
            c = per - 1 - cc
            rows = chunk_rows(c)
            etot = jnp.exp(_rowsum(la_ref[rows, :]))
            for hd in range(HEADS):
                ks = slice(hd * hk, (hd + 1) * hk)
                q = (z_ref[rows, hd * hk:(hd + 1) * hk] * scale).astype(BF16)
                dec = dec_ref[rows, ks]
                kd = z_ref[rows, dk + hd * hk:dk + (hd + 1) * hk] * dec
                v = z_ref[rows, 2 * dk + hd * hv:2 * dk + (hd + 1) * hv].astype(BF16)
                s_prev = st_ref[c - 1, hd] if c > 0 else keep * stp_ref[0, hd]
                ds_t = ds_ref[hd] + _dot_tn(dos_ref[rows, hd * hv:(hd + 1) * hv], q)
                ds_b = ds_t.astype(BF16)
                dkd = _dot(v, ds_b)
                dz_ref[rows, 2 * dk + hd * hv:2 * dk + (hd + 1) * hv] = _dot_nt(kd.astype(BF16), ds_b).astype(BF16)
                dtot = etot[:, ks] * _rowsum(ds_t * s_prev)
                ds_ref[hd] = ds_t * etot[:, ks]
                dz_ref[rows, dk + hd * hk:dk + (hd + 1) * hk] = (dkd * dec).astype(BF16)
                e_ref[rows, ks] = dkd * kd
                dla_ref[rows, ks] = jnp.broadcast_to(dtot, (CHUNK, hk))
            return carry

        def decay_cotangent(c, carry):
            rows = chunk_rows(c)
            dla_ref[rows, :] += _tri_dot(tri_strict, e_ref[rows, :])
            return carry

        dhg = jnp.zeros((1, hv), F32)
        for c in range(per):
            dhg = recompute(c, dhg)
        dhg_ref[...] += dhg
        for phase in (recurrence, decay_cotangent):
            for c in range(per):
                phase(c, 0)
        dla = jnp.where(rows_id >= PAD_ROWS, dla_ref[...], 0.0)
        dgp = dla * (1.0 / GATE_NORM) * (1.0 - _sigmoid(gp))
        dgb_ref[...] += _rowsum(dgp)
        dgp_b = dgp.astype(BF16)
        dgw_ref[...] += _dot_tn(r.astype(BF16), dgp_b)
        dz_ref[:, r_at:r_at + GATE_PAD] = _dot_nt(dgp_b, gw_ref[...]).astype(BF16)
        _project_back(dz_ref, wi_ref, h_ref, g_ref, dh_ref, dhi_ref, dg_ref, step == 0)

    d = h.shape[1]
    back = lambda i: (ntile - 1 - i, 0)
    const2 = lambda i: (0, 0)
    return pl.pallas_call(
        body, grid=(ntile,),
        in_specs=dep_specs + [
                  pl.BlockSpec((rt, z.shape[1]), back), pl.BlockSpec((rt, dv), back), pl.BlockSpec((rt, dk), back),
                  pl.BlockSpec((rt, d), back), _resident((1, d)),
                  _resident(w_in_t.shape), pl.BlockSpec((rt, d), back), _resident(w_out.shape),
                  pl.BlockSpec((per, HEADS, hv, hk), lambda i: (ntile - 1 - i, 0, 0, 0)),
                  pl.BlockSpec((1, HEADS, hv, hk), lambda i: (jnp.maximum((ntile - 1 - i) * per - 1, 0), 0, 0, 0)),
                  _resident(gate_w.shape), _resident((1, dk)), _resident((1, hv))],
        out_specs=[pl.BlockSpec((rt, d), back), pl.BlockSpec((1, d), const2), pl.BlockSpec((rt, z.shape[1]), back),
                   pl.BlockSpec(gate_w.shape, const2), pl.BlockSpec((1, dk), const2), pl.BlockSpec((1, hv), const2)],
        out_shape=[_sds((t, d), F32), _sds((1, d), F32), _sds(z.shape, BF16), _sds(gate_w.shape, F32),
                   _sds((1, dk), F32), _sds((1, hv), F32)],
        scratch_shapes=[pltpu.VMEM((HEADS, hv, hk), F32), pltpu.VMEM((rt, dk), F32), pltpu.VMEM((rt, dk), F32),
                        pltpu.VMEM((rt, dv), BF16), pltpu.VMEM((rt, dk), F32), pltpu.VMEM((rt, dv), F32)],
        compiler_params=_cparams("arbitrary"), name=name)(*deps, z, o, dec, h, gain, w_in_t, dh, w_out, states, states, gate_w,
                                                          gate_b, head_g)


def _adamw_math(w, g, m, v):
    m = ADAM_B1 * m + (1.0 - ADAM_B1) * g
    v = ADAM_B2 * v + (1.0 - ADAM_B2) * (g * g)
    m_hat = m / (1.0 - ADAM_B1 ** ADAM_STEP)
    v_hat = v / (1.0 - ADAM_B2 ** ADAM_STEP)
    return -ADAM_LR * (m_hat / (jnp.sqrt(v_hat) + ADAM_EPS) + ADAM_WD * w), m, v


N_CHIP = N_DEV // 2
BLOCK_ELEMS = 128 * 1024


def _my_slot():
    return 4 * lax.axis_index("x") + 2 * lax.axis_index("y") + lax.axis_index("c")


def _row_block(r, c):
    cap = max(8, BLOCK_ELEMS // (-(-c // LANE) * LANE))
    return max([b for b in range(8, r + 1, 8) if r % b == 0 and b <= cap] or [r])


def _blocks(r, c):
    rb = _row_block(r, c)
    if rb < r or r * c <= BLOCK_ELEMS:
        return rb, c
    return r, max([b for b in (512, 256, LANE) if c % b == 0 and r * b <= BLOCK_ELEMS] or [c])


def _reduce_adam(parts, w, m, v, *, by_row=False, after=None, name):
    nl, r, c = (1, w.shape[0], w.shape[2]) if by_row else w.shape
    rb, cb = _blocks(r, c)
    dep_specs, deps = _dep_specs(after)
    whole = (slice(None), 0, slice(None)) if by_row else Ellipsis

    def body(*refs):
        me = refs[0][0]
        refs = refs[1 + len(deps):]
        p_refs = refs[:2 * nl]
        w_ref, m_ref, v_ref, g_out, d_out, m_out, v_out = refs[2 * nl:]
        layer = pl.program_id(0)
        for li in range(nl):
            @pl.when(layer == li)
            def _(li=li):
                own_ref, land_ref = p_refs[2 * li], p_refs[2 * li + 1]
                mine = own_ref[...].astype(F32)
                g = None
                for dev in range(N_DEV):
                    term = jnp.where(me == dev, mine, land_ref[dev].astype(F32))
                    g = term if g is None else g + term
                g_out[whole] = g
                d_out[whole], m_out[whole], v_out[whole] = _adamw_math(w_ref[whole], g, m_ref[whole], v_ref[whole])

    if by_row:
        blk = pl.BlockSpec((rb, 1, cb), lambda l, i, j, me: (i, 0, j))
    else:
        blk = pl.BlockSpec((None, rb, cb), lambda l, i, j, me: (l, i, j))
    p_specs = []
    for li in range(nl):
        p_specs += [
            pl.BlockSpec((None, rb, cb), lambda l, i, j, me, li=li: (me[0], jnp.where(l == li, i, 0), jnp.where(l == li, j, 0))),
            pl.BlockSpec((N_DEV, rb, cb), lambda l, i, j, me, li=li: (0, jnp.where(l == li, i, 0), jnp.where(l == li, j, 0)))]
    flat = [p for pair in parts for p in pair]
    grid_spec = pltpu.PrefetchScalarGridSpec(
        num_scalar_prefetch=1, grid=(nl, r // rb, c // cb), in_specs=dep_specs + p_specs + [blk, blk, blk],
        out_specs=[blk] * 4)
    return pl.pallas_call(
        body, grid_spec=grid_spec, out_shape=[_sds(w.shape, F32)] * 4,
        compiler_params=_cparams("arbitrary", "arbitrary", "arbitrary"), name=name)(
        _my_slot().reshape(1), *deps, *flat, w, m, v)


def _sum8(own, landed, *, name):
    def body(own_ref, land_ref, o_ref):
        me = _my_slot()
        total = None
        for dev in range(N_DEV):
            term = jnp.where(me == dev, own_ref[...], land_ref[dev])
            total = term if total is None else total + term
        o_ref[...] = total

    return pl.pallas_call(body, out_shape=_sds(own.shape, F32), name=name)(own, landed)


def _adam_small(own, landed, split, w, m, v, *, name):
    n = len(w)

    def body(*refs):
        own_refs, land_refs, w_refs, m_refs, v_refs = (refs[k * n:(k + 1) * n] for k in range(5))
        outs = refs[5 * n:]
        me = _my_slot()
        for k in range(n):
            mine = own_refs[k][me] if split[k] else own_refs[k][...]
            g = None
            for dev in range(N_DEV):
                term = jnp.where(me == dev, mine, land_refs[k][dev])
                g = term if g is None else g + term
            outs[4 * k][...] = g
            outs[4 * k + 1][...], outs[4 * k + 2][...], outs[4 * k + 3][...] = _adamw_math(
                w_refs[k][...], g, m_refs[k][...], v_refs[k][...])

    out = pl.pallas_call(body, out_shape=[_sds(a.shape, F32) for a in w for _ in range(4)],
                         compiler_params=pltpu.CompilerParams(vmem_limit_bytes=V7X_VMEM_LIMIT), name=name)(
        *own, *landed, *w, *m, *v)
    return [tuple(out[4 * k:4 * k + 4]) for k in range(n)]


_HBM = pl.BlockSpec(memory_space=pltpu.HBM)
_SEM = pl.BlockSpec(memory_space=pltpu.SEMAPHORE)
_DATAFLOW = pltpu.SideEffectType.DATAFLOW_SIDE_EFFECTING


def _plan_to_all(src, land):
    x, y, c = lax.axis_index("x"), lax.axis_index("y"), lax.axis_index("c")
    return [(src, land.at[_my_slot()], (x ^ ((d >> 2) & 1), y ^ ((d >> 1) & 1), c ^ (d & 1))) for d in range(1, N_DEV)]


def _plan_split_to_all(src, land):
    x, y, c = lax.axis_index("x"), lax.axis_index("y"), lax.axis_index("c")
    peers = [(x ^ ((d >> 2) & 1), y ^ ((d >> 1) & 1), c ^ (d & 1)) for d in range(1, N_DEV)]
    return [(src.at[4 * px + 2 * py + pc], land.at[_my_slot()], (px, py, pc)) for px, py, pc in peers]


_PLAN_COPIES = {_plan_to_all: N_DEV - 1, _plan_split_to_all: N_DEV - 1}


def _plans(plan, n):
    return list(plan) if isinstance(plan, (list, tuple)) else [plan] * n


def _exchange_copies(plan, ins, lands, send, recv):
    copies, sem = [], 0
    for p, src, land in zip(_plans(plan, len(lands)), ins, lands):
        for s, dst, dev in p(src, land):
            copies.append(pltpu.make_async_remote_copy(
                src_ref=s, dst_ref=dst, send_sem=send.at[sem], recv_sem=recv.at[sem],
                device_id=dev, device_id_type=pl.DeviceIdType.MESH))
            sem += 1
    return copies


def _place_own(srcs, *, after=None, name):
    dep_specs, deps = _dep_specs(after)
    n = len(srcs)
    arrays, in_specs, shapes = [], [], []
    for a, dtype in srcs:
        if isinstance(a, tuple):
            a, layer = a
            in_specs.append(pl.BlockSpec((None,) + a.shape[1:], lambda i, layer=layer: (layer, 0, 0)))
            shapes.append(a.shape[1:])
        else:
            in_specs.append(pl.BlockSpec(a.shape, lambda i: (0, 0)))
            shapes.append(a.shape)
        arrays.append(a)
    dtypes = [dtype for _, dtype in srcs]

    def body(*refs):
        refs = refs[len(deps):]
        a_refs, o_refs, cast_refs, sem = refs[:n], refs[n:2 * n], refs[2 * n:3 * n], refs[3 * n]
        me = _my_slot()
        copies = []
        for k in range(n):
            cast_refs[k][...] = a_refs[k][...].astype(dtypes[k])
            copies.append(pltpu.make_async_copy(cast_refs[k], o_refs[k].at[me], sem.at[k]))
            copies[-1].start()
        for cp in copies:
            cp.wait()

    return pl.pallas_call(
        body, grid=(1,), in_specs=dep_specs + in_specs, out_specs=[pl.BlockSpec(memory_space=pl.ANY)] * n,
        out_shape=[_sds((N_DEV,) + shape, dtype) for shape, dtype in zip(shapes, dtypes)],
        scratch_shapes=[pltpu.VMEM(shape, dtype) for shape, dtype in zip(shapes, dtypes)] + [pltpu.SemaphoreType.DMA((n,))],
        compiler_params=pltpu.CompilerParams(vmem_limit_bytes=V7X_VMEM_LIMIT), name=name)(*deps, *arrays)


def _plan_gather_first(land, _):
    x, y, c = lax.axis_index("x"), lax.axis_index("y"), lax.axis_index("c")
    mine = land.at[_my_slot()]
    return [(mine, mine, (x, y, 1 - c))] + [(mine, mine, (x ^ (d >> 1), y ^ (d & 1), c)) for d in range(1, N_CHIP)]


def _plan_gather_relay(land, _):
    x, y, c = lax.axis_index("x"), lax.axis_index("y"), lax.axis_index("c")
    slots = [land.at[4 * (x ^ (d >> 1)) + 2 * (y ^ (d & 1)) + c] for d in range(1, N_CHIP)]
    return [(s, s, (x, y, 1 - c)) for s in slots]


def _plan_gather_direct(land, _):
    return _plan_to_all(land.at[_my_slot()], land)


_PLAN_COPIES[_plan_gather_first] = N_CHIP
_PLAN_COPIES[_plan_gather_relay] = N_CHIP - 1
_PLAN_COPIES[_plan_gather_direct] = N_DEV - 1


def _exchange_start(plan, arrs, lands, *, after=None, name):
    bufs = list(lands) if arrs is None else list(arrs) + list(lands)
    n, nb = len(lands), len(bufs)
    nsem = sum(_PLAN_COPIES[p] for p in _plans(plan, n))
    dep_specs, deps = _dep_specs(after)

    def body(*refs):
        ins, land_refs = refs[:n], refs[nb - n:nb]
        send, recv = refs[nb + len(deps)], refs[nb + len(deps) + 1]
        for cp in _exchange_copies(plan, ins, land_refs, send, recv):
            cp.start()
        refs[-1][...] = jnp.zeros_like(refs[-1])

    out = pl.pallas_call(
        body, name=name,
        out_shape=(pltpu.SemaphoreType.DMA((nsem,)), pltpu.SemaphoreType.DMA((nsem,)),
                   *[pltpu.HBM(a.shape, a.dtype) for a in bufs], _sds((8, LANE), F32)),
        in_specs=[_HBM] * nb + dep_specs,
        out_specs=(_SEM, _SEM, *([_HBM] * nb), pl.BlockSpec(memory_space=pltpu.VMEM)),
        input_output_aliases={i: 2 + i for i in range(nb)},
        compiler_params=pltpu.CompilerParams(has_side_effects=_DATAFLOW),
    )(*[pltpu.with_memory_space_constraint(a, pltpu.HBM) for a in bufs], *deps)
    return (plan, n, out[0], out[1], list(out[2:2 + nb])), out[-1]


def _exchange_now(plan, lands, *, name):
    n = len(lands)
    nsem = sum(_PLAN_COPIES[p] for p in _plans(plan, n))

    def body(*refs):
        land_refs, send, recv = refs[n:2 * n], refs[2 * n], refs[2 * n + 1]
        copies = _exchange_copies(plan, land_refs, land_refs, send, recv)
        for cp in copies:
            cp.start()
        for cp in copies:
            cp.wait_send()
            cp.wait_recv()

    hbm = pl.BlockSpec(memory_space=pl.ANY)
    return pl.pallas_call(
        body, in_specs=[hbm] * n, out_specs=[hbm] * n, out_shape=[_sds(a.shape, a.dtype) for a in lands],
        input_output_aliases={i: i for i in range(n)},
        scratch_shapes=[pltpu.SemaphoreType.DMA((nsem,)), pltpu.SemaphoreType.DMA((nsem,))], name=name)(*lands)


def _exchange_wait(state, after, *, name):
    plan, n, send_sem, recv_sem, bufs = state
    nb = len(bufs)
    after = list(after) if isinstance(after, (list, tuple)) else [after]

    def body(*refs):
        ins, land_refs, send, recv = refs[:n], refs[nb - n:nb], refs[nb], refs[nb + 1]
        for cp in _exchange_copies(plan, ins, land_refs, send, recv):
            cp.wait_send()
            cp.wait_recv()

    out = pl.pallas_call(
        body, name=name, out_shape=[pltpu.HBM(a.shape, a.dtype) for a in bufs],
        in_specs=[_HBM] * nb + [_SEM, _SEM] + [pl.BlockSpec(memory_space=pl.ANY)] * len(after), out_specs=[_HBM] * nb,
        input_output_aliases={i: i for i in range(nb)},
        compiler_params=pltpu.CompilerParams(has_side_effects=_DATAFLOW),
    )(*bufs, send_sem, recv_sem, *after)
    return list(out[:n]), list(out[nb - n:])


def _dep_specs(after):
    return ([], []) if after is None else ([pl.BlockSpec(memory_space=pl.ANY)], [after])


def _undo_column_split(g):
    return jnp.transpose(g, (1, 0, 2)).reshape(g.shape[1], N_DEV * g.shape[2])


def _column_split(a):
    r, c = a.shape
    return jnp.transpose(a.reshape(r, N_DEV, c // N_DEV), (1, 0, 2))


class _WholeWeights:
    def __init__(self, groups):
        self.groups = groups
        self.grads = {}

    def fetch(self, group, after):
        return self.groups[group]

    def emit(self, group, grads):
        self.grads.update(grads)
        return None


def _local_step(x, target, replicated, src):
    d = x.shape[1]
    mix_g, ffn_g = replicated["mix_g"], replicated["ffn_g"]
    cp = src.fetch("cp", [])
    cp_mid = (cp["conv_w"], replicated["conv_b"], replicated["ln_g"], replicated["ln_b"], replicated["pool_w"],
              replicated["pool_scale"])

    h1, cat, z0, u0, cv0, h0 = _cp_mid_fwd(x, cp["meta"], mix_g[0:1], cp["cp_w_in_t"], cp["cp_w_out"], *cp_mid,
                                           name="cp_mixer")
    ffn0 = src.fetch("ffn0", h1)
    h2, uf0, rf0 = _ffn_fwd(h1, ffn_g[0:1], ffn0["w1"], ffn0["w2"], name="ffn0")
    gla = src.fetch("gla", h2)
    gla_mid = (gla["gate_w"], gla["gate_b"], gla["head_g"])
    h3, og, states, z1, u1, gla_w_in_t, o1, dec1 = _gla_mid_fwd(h2, mix_g[1:2], gla["gla_w_in_t"], gla["gla_w_out"], *gla_mid,
                                                          name="gla_mixer")
    ffn1 = src.fetch("ffn1", h3)
    dh4, uf1, rf1, loss, d_final_g = _ffn_fwd(h3, ffn_g[1:2], ffn1["w1"], ffn1["w2"],
                                              loss_head=(replicated["final_g"], target), name="ffn1_loss")

    dh3, dhh1, dob1, dffn_g1 = _ffn_bwd_x(h3, dh4, ffn_g[1:2], rf1, ffn1["w1"], ffn1["w2"], name="ffn1_bwd_x")
    sent = src.emit("ffn1", dict(w1=_linear_bwd_w(uf1, dhh1, column_blocks=N_DEV, name="ffn1_dw1"),
                                 w2=_linear_bwd_w(rf1, dob1, square_x=True, name="ffn1_dw2")))
    d_gla_w_out = _linear_bwd_w(og, dh3, name="gla_out_dw")
    dh2, dmix_g1, dz1, d_gate_w, d_gate_b, d_head_g = _gla_mid_bwd(
        z1, o1, dec1, h2, mix_g[1:2], gla_w_in_t, dh3, gla["gla_w_out"], states, *gla_mid, after=sent, name="gla_mixer_bwd")
    d_gla_w_in_t = _linear_bwd_w(dz1, u1, row_blocks=gla["gla_w_in_t"].shape[:2], name="gla_in_dw")
    sent = src.emit("gla", dict(gla_w_in_t=d_gla_w_in_t, gla_w_out=d_gla_w_out))
    dh1, dhh0, dob0, dffn_g0 = _ffn_bwd_x(h1, dh2, ffn_g[0:1], rf0, ffn0["w1"], ffn0["w2"], after=sent, name="ffn0_bwd_x")
    sent = src.emit("ffn0_w1", dict(w1=_linear_bwd_w(uf0, dhh0, column_blocks=N_DEV, name="ffn0_dw1")))
    sent = src.emit("ffn0_w2", dict(w2=_linear_bwd_w(rf0, dob0, square_x=True, after=sent, name="ffn0_dw2")))
    d_cp_w_out = _linear_bwd_w(cat, dh1, after=sent, name="cp_out_dw")
    sent = src.emit("cp_out", dict(cp_w_out=d_cp_w_out))
    dx, dh0_first, dmix_g0, dz0, d_conv_w, d_conv_b, d_ln_g, d_ln_b, d_pool_w, d_pool_scale = _cp_mid_bwd(
        z0, cv0, h0, mix_g[0:1], cp["cp_w_in_t"], dh1, cp["cp_w_out"], *cp_mid, after=sent, name="cp_mixer_bwd")
    d_cp_w_in_t = _linear_bwd_w(dz0, u0, name="cp_in_dw")

    small = dict(
        mix_g=jnp.concatenate([dmix_g0, dmix_g1]), ffn_g=jnp.concatenate([dffn_g0, dffn_g1]), conv_b=d_conv_b, ln_g=d_ln_g,
        ln_b=d_ln_b, pool_w=d_pool_w, pool_scale=d_pool_scale, final_g=d_final_g, meta=dh0_first[PAD_ROWS:], conv_w=d_conv_w,
        gate_w=d_gate_w, gate_b=d_gate_b, head_g=d_head_g)
    src.emit("cp", dict(cp_w_in_t=d_cp_w_in_t, small=small, loss=loss))
    return loss, dx, small


_REPLICATED = ("mix_norm_g", "ffn_norm_g", "cp_conv_b", "cp_ln_g", "cp_ln_b", "cp_pool_w", "cp_pool_scale", "final_norm_g")
_SMALL_SHARDED = ("meta_tokens", "cp_conv_w", "gla_gate_w2", "gla_gate_b", "gla_head_g")
_NAMES = ("meta_tokens", "mix_norm_g", "ffn_norm_g", "ffn_w1", "ffn_w2", "cp_w_in", "cp_conv_w", "cp_conv_b", "cp_ln_g",
          "cp_ln_b", "cp_pool_w", "cp_pool_scale", "cp_w_out", "gla_w_in", "gla_gate_w2", "gla_gate_b", "gla_head_g",
          "gla_w_out", "final_norm_g")
_SMALL_GRADS = ("mix_g", "ffn_g", "conv_b", "ln_g", "ln_b", "pool_w", "pool_scale", "final_g", "meta", "conv_w", "gate_w",
                "gate_b", "head_g")
_GROUPS = ("cp", "ffn0", "gla", "ffn1")
_TWO_LEG_GATHERS = ("cp", "ffn0", "ffn1")


class _Exchanges:
    def __init__(self, w, d):
        self.d = d
        small = [w[n].reshape(w[n].shape[-2:]) for n in _SMALL_SHARDED]
        self.small_shard_shapes = [w[n].shape for n in _SMALL_SHARDED]
        shards = dict(
            cp=[(w["cp_w_in"][0].T, BF16), (w["cp_w_out"][0], BF16)] + [(a, F32) for a in small],
            ffn0=[((w["ffn_w1"], 0), BF16), ((w["ffn_w2"], 0), BF16)],
            gla=[(w["gla_w_in"][0].T, BF16), (w["gla_w_out"][0], BF16)],
            ffn1=[((w["ffn_w1"], 1), BF16), ((w["ffn_w2"], 1), BF16)])
        self.gathers = {}
        self.sent = {}
        token = None
        for group in _GROUPS:
            lands = _place_own(shards[group], after=token, name=f"place_w_{group}")
            plan = _plan_gather_first if group in _TWO_LEG_GATHERS else _plan_gather_direct
            self.gathers[group], token = _exchange_start(plan, None, lands, after=token, name=f"start_w_{group}")
        self.token = token

    def fetch(self, group, after):
        d = self.d
        after = (list(after) if isinstance(after, (list, tuple)) else [after]) + [self.token]
        _, got = _exchange_wait(self.gathers[group], after, name=f"wait_w_{group}")
        if group in _TWO_LEG_GATHERS:
            got = _exchange_now(_plan_gather_relay, got, name=f"relay_w_{group}")
        if group in ("ffn0", "ffn1"):
            return dict(w1=got[0], w2=got[1])
        if group == "gla":
            return dict(gla_w_in_t=got[0], gla_w_out=got[1].reshape(d, d), gate_w=self.gate_w, gate_b=self.gate_b,
                        head_g=self.head_g)
        meta, conv_w, gate_w, self.gate_b, self.head_g = [_undo_column_split(a) for a in got[2:]]
        self.gate_w = jnp.pad(gate_w, ((0, GATE_PAD - GATE_RANK), (0, 0))).astype(BF16)
        return dict(cp_w_in_t=got[0].reshape(-1, d), cp_w_out=got[1].reshape(d, d), meta=meta,
                    conv_w=jnp.pad(conv_w, ((0, 1), (0, 0))))

    def emit(self, group, g):
        d = self.d
        if group == "ffn1":
            arrs = [g["w1"], g["w2"].reshape(N_DEV, -1, d)]
        elif group == "ffn0_w1":
            arrs = [g["w1"]]
        elif group == "ffn0_w2":
            arrs = [g["w2"].reshape(N_DEV, -1, d)]
        elif group == "gla":
            arrs = [g["gla_w_in_t"], g["gla_w_out"].reshape(N_DEV, d // N_DEV, d)]
        elif group == "cp_out":
            arrs = [g["cp_w_out"].reshape(N_DEV, d // N_DEV, d)]
        else:
            s = dict(g["small"])
            s.update(pool_w=s["pool_w"][None], conv_w=s["conv_w"][:CONV_WIDTH], gate_w=s["gate_w"][:GATE_RANK])
            own = [s[n] for n in _SMALL_GRADS[:len(_REPLICATED)]]
            own += [_column_split(s[n]).reshape((N_DEV,) + shape)
                    for n, shape in zip(_SMALL_GRADS[len(_REPLICATED):], self.small_shard_shapes)]
            plans = [_plan_to_all] * len(_REPLICATED) + [_plan_split_to_all] * len(_SMALL_SHARDED)
            lands = [lax.empty((N_DEV,) + a.shape, F32) for a in own[:len(_REPLICATED)]]
            lands += [lax.empty(a.shape, F32) for a in own[len(_REPLICATED):]]
            own.append(g["loss"])
            plans.append(_plan_to_all)
            lands.append(lax.empty((N_DEV,) + g["loss"].shape, F32))
            own.append(g["cp_w_in_t"].reshape(N_DEV, -1, d))
            plans.append(_plan_split_to_all)
            lands.append(lax.empty(own[-1].shape, BF16))
            self.sent[group], self.token = _exchange_start(plans, own, lands, after=self.token, name=f"start_g_{group}")
            return self.token
        self.sent[group], self.token = _exchange_start(_plan_split_to_all, arrs, [lax.empty(a.shape, a.dtype) for a in arrs],
                                                       after=self.token, name=f"start_g_{group}")
        return self.token

    def finish(self, w, mom, var):
        out = {}
        after = self.token

        def landed(group):
            own, got = _exchange_wait(self.sent[group], after, name=f"wait_g_{group}")
            return list(zip(own, got))

        def adam(n, parts, behind=None, transposed=False):
            by_row = transposed and w[n].shape[2] % 8 != 0
            if by_row:
                flip, back = (lambda a: jnp.transpose(a, (2, 0, 1))), (lambda a: jnp.transpose(a, (1, 2, 0)))
            else:
                flip = back = (lambda a: jnp.transpose(a, (0, 2, 1))) if transposed else (lambda a: a)
            res = _reduce_adam(parts, flip(w[n]), flip(mom[n]), flip(var[n]), by_row=by_row, after=behind, name=f"adam_{n}")
            out[n] = tuple(back(a) for a in res)
            return res[0]

        ffn1 = landed("ffn1")
        after = ffn1[0][1]
        gla = landed("gla")
        after = adam("gla_w_in", [gla[0]], transposed=True)
        after = adam("gla_w_out", [gla[1]], after)
        ffn0_w1 = landed("ffn0_w1")
        after = adam("ffn_w1", [ffn0_w1[0], ffn1[0]])
        ffn0_w2 = landed("ffn0_w2")
        after = adam("ffn_w2", [ffn0_w2[0], ffn1[1]])
        cp_out = landed("cp_out")
        after = adam("cp_w_out", [cp_out[0]])
        *small, loss, cp_w_in = landed("cp")
        names = _REPLICATED + _SMALL_SHARDED
        split = [False] * len(_REPLICATED) + [True] * len(_SMALL_SHARDED)
        small_new = _adam_small([own for own, _ in small], [got for _, got in small], split, [w[n] for n in names],
                                [mom[n] for n in names], [var[n] for n in names], name="adam_small")
        out.update(zip(names, small_new))
        out["loss"] = _sum8(*loss, name="sum_loss")[0, 0]
        adam("cp_w_in", [cp_w_in], small_new[0][0], transposed=True)
        return out


def kernel(x, meta_tokens, mix_norm_g, ffn_norm_g, ffn_w1, ffn_w2, cp_w_in, cp_conv_w, cp_conv_b, cp_ln_g, cp_ln_b, cp_pool_w, cp_pool_scale, cp_w_out, gla_w_in, gla_gate_w2, gla_gate_b, gla_head_g, gla_w_out, final_norm_g, loss_target, m_meta_tokens, m_mix_norm_g, m_ffn_norm_g, m_ffn_w1, m_ffn_w2, m_cp_w_in, m_cp_conv_w, m_cp_conv_b, m_cp_ln_g, m_cp_ln_b, m_cp_pool_w, m_cp_pool_scale, m_cp_w_out, m_gla_w_in, m_gla_gate_w2, m_gla_gate_b, m_gla_head_g, m_gla_w_out, m_final_norm_g, v_meta_tokens, v_mix_norm_g, v_ffn_norm_g, v_ffn_w1, v_ffn_w2, v_cp_w_in, v_cp_conv_w, v_cp_conv_b, v_cp_ln_g, v_cp_ln_b, v_cp_pool_w, v_cp_pool_scale, v_cp_w_out, v_gla_w_in, v_gla_gate_w2, v_gla_gate_b, v_gla_head_g, v_gla_w_out, v_final_norm_g):
    w = dict(meta_tokens=meta_tokens, mix_norm_g=mix_norm_g, ffn_norm_g=ffn_norm_g, ffn_w1=ffn_w1, ffn_w2=ffn_w2,
             cp_w_in=cp_w_in, cp_conv_w=cp_conv_w, cp_conv_b=cp_conv_b, cp_ln_g=cp_ln_g, cp_ln_b=cp_ln_b,
             cp_pool_w=cp_pool_w, cp_pool_scale=cp_pool_scale, cp_w_out=cp_w_out, gla_w_in=gla_w_in,
             gla_gate_w2=gla_gate_w2, gla_gate_b=gla_gate_b, gla_head_g=gla_head_g, gla_w_out=gla_w_out,
             final_norm_g=final_norm_g.reshape(1, -1))
    mom = dict(meta_tokens=m_meta_tokens, mix_norm_g=m_mix_norm_g, ffn_norm_g=m_ffn_norm_g, ffn_w1=m_ffn_w1, ffn_w2=m_ffn_w2,
               cp_w_in=m_cp_w_in, cp_conv_w=m_cp_conv_w, cp_conv_b=m_cp_conv_b, cp_ln_g=m_cp_ln_g, cp_ln_b=m_cp_ln_b,
               cp_pool_w=m_cp_pool_w, cp_pool_scale=m_cp_pool_scale, cp_w_out=m_cp_w_out, gla_w_in=m_gla_w_in,
               gla_gate_w2=m_gla_gate_w2, gla_gate_b=m_gla_gate_b, gla_head_g=m_gla_head_g, gla_w_out=m_gla_w_out,
               final_norm_g=m_final_norm_g.reshape(1, -1))
    var = dict(meta_tokens=v_meta_tokens, mix_norm_g=v_mix_norm_g, ffn_norm_g=v_ffn_norm_g, ffn_w1=v_ffn_w1, ffn_w2=v_ffn_w2,
               cp_w_in=v_cp_w_in, cp_conv_w=v_cp_conv_w, cp_conv_b=v_cp_conv_b, cp_ln_g=v_cp_ln_g, cp_ln_b=v_cp_ln_b,
               cp_pool_w=v_cp_pool_w, cp_pool_scale=v_cp_pool_scale, cp_w_out=v_cp_w_out, gla_w_in=v_gla_w_in,
               gla_gate_w2=v_gla_gate_w2, gla_gate_b=v_gla_gate_b, gla_head_g=v_gla_head_g, gla_w_out=v_gla_w_out,
               final_norm_g=v_final_norm_g.reshape(1, -1))
    d = x.shape[-1]
    replicated = dict(mix_g=w["mix_norm_g"], ffn_g=w["ffn_norm_g"], conv_b=w["cp_conv_b"], ln_g=w["cp_ln_g"],
                      ln_b=w["cp_ln_b"], pool_w=w["cp_pool_w"][0].astype(BF16), pool_scale=w["cp_pool_scale"],
                      final_g=w["final_norm_g"])
    exchanges = _Exchanges(w, d)
    _, grad_x, _ = _local_step(x[0], loss_target[0], replicated, exchanges)
    out = exchanges.finish(w, mom, var)
    loss = out.pop("loss")

    def leaf(n, k):
        a = out[n][k]
        return a.reshape(-1) if n == "final_norm_g" else a

    return (loss, grad_x[None], *[leaf(n, 0) for n in _NAMES], *[leaf(n, 1) for n in _NAMES],
            *[leaf(n, 2) for n in _NAMES], *[leaf(n, 3) for n in _NAMES])
```
